```python
import jax, jax.numpy as jnp
from jax import lax
import numpy as np

D_MODEL = 1024
BATCH = 8
SEQ = 4096
DEPTH = 1

PLE_DIM = 256
N_ATTN_HEADS = 8
HEAD_DIM = 64
ATTN_WIDTH = N_ATTN_HEADS * HEAD_DIM
SSD_HEADS = 8
SSD_HEAD_DIM = 64
SSD_WIDTH = SSD_HEADS * SSD_HEAD_DIM
SSD_STATE = 128
CONV_WIDTH = 4
CONV_CH = SSD_WIDTH + 2 * SSD_STATE
CHUNK = 128
MIX_WIDTH = ATTN_WIDTH + SSD_WIDTH
IN_PROJ_WIDTH = 3 * ATTN_WIDTH + SSD_WIDTH + CONV_CH + SSD_HEADS
D_FF = 4 * D_MODEL
ROPE_THETA = 10000.0
DILATED_BRANCHES = ((128, 1), (512, 4), (2048, 16))
ATTN_BLOCK = 128
NORM_EPS = 1e-6

kernel_name = 'hybrid_ssd_dilated_attention_layer'


def rms_norm(x, g):
    xf = x.astype(jnp.float32)
    xf = xf * lax.rsqrt(jnp.mean(xf * xf, axis=-1, keepdims=True) + NORM_EPS)
    return xf.astype(x.dtype) * g


def apply_rope(t, positions):
    dh = t.shape[-1]
    half = dh // 2
    inv_freq = ROPE_THETA ** (-jnp.arange(half, dtype=jnp.float32) * 2.0 / dh)
    ang = positions.astype(jnp.float32)[:, :, None] * inv_freq
    cos = jnp.cos(ang)[:, :, None, :]
    sin = jnp.sin(ang)[:, :, None, :]
    tf = t.astype(jnp.float32)
    t1, t2 = tf[..., :half], tf[..., half:]
    return jnp.concatenate([t1 * cos - t2 * sin, t2 * cos + t1 * sin], axis=-1).astype(t.dtype)


def dilated_branch(q, k, v, window, dilation):
    b, s, nh, dh = q.shape
    span = window // dilation
    blk = ATTN_BLOCK
    sub_len = s // dilation
    nb = -(-sub_len // blk)
    sub_pad = nb * blk

    def to_sub(t):
        t = t.reshape(b, sub_len, dilation, nh, dh).transpose(0, 2, 1, 3, 4)
        return jnp.pad(t, ((0, 0), (0, 0), (0, sub_pad - sub_len), (0, 0), (0, 0)))

    def band(t):
        t = jnp.pad(t, ((0, 0), (0, 0), (blk, 0), (0, 0), (0, 0)))
        t = t.reshape(b, dilation, nb + 1, blk, nh, dh)
        return jnp.concatenate([t[:, :, :-1], t[:, :, 1:]], axis=3)

    qb = to_sub(q).reshape(b, dilation, nb, blk, nh, dh)
    kb = band(to_sub(k))
    vb = band(to_sub(v))

    scores = jnp.einsum('brnqhd,brnkhd->brnhqk', qb, kb).astype(jnp.float32)
    qi = jnp.arange(blk)[:, None]
    ki = jnp.arange(2 * blk)[None, :]
    dist = qi + blk - ki
    key_idx = jnp.arange(nb)[:, None, None] * blk - blk + ki
    valid = (dist >= 0) & (dist <= span) & (key_idx >= 0)
    scores = jnp.where(valid[None, None, :, None], scores, -jnp.inf)
    m = jnp.max(scores, axis=-1, keepdims=True)
    e = jnp.exp(scores - m)
    den = jnp.sum(e, axis=-1, keepdims=True)
    out = jnp.einsum('brnhqk,brnkhd->brnqhd', (e / den).astype(v.dtype), vb)
    lse = (m + jnp.log(den))[..., 0]

    out = out.reshape(b, dilation, sub_pad, nh, dh)[:, :, :sub_len]
    out = out.transpose(0, 2, 1, 3, 4).reshape(b, s, nh, dh)
    lse = lse.transpose(0, 1, 2, 4, 3).reshape(b, dilation, sub_pad, nh)[:, :, :sub_len]
    lse = lse.transpose(0, 2, 1, 3).reshape(b, s, nh)
    return out, lse


def causal_conv(u, w, bias):
    y = lax.conv_general_dilated(u, w[:, None, :], window_strides=(1,),
                                 padding=[(CONV_WIDTH - 1, 0)],
                                 dimension_numbers=('NWC', 'WIO', 'NWC'),
                                 feature_group_count=u.shape[-1])
    return y + bias


def segsum_exp(a):
    t = a.shape[-1]
    cs = jnp.cumsum(a, axis=-1)
    diff = cs[..., :, None] - cs[..., None, :]
    mask = jnp.tril(jnp.ones((t, t), dtype=bool))
    return jnp.exp(jnp.where(mask, diff, -jnp.inf))


def ssd_chunked(xdt, adt, bm, cm):
    b, s, nh, hp = xdt.shape
    n = bm.shape[-1]
    c = s // CHUNK
    x_c = xdt.reshape(b, c, CHUNK, nh, hp)
    a_c = adt.reshape(b, c, CHUNK, nh).transpose(0, 3, 1, 2)
    b_c = bm.reshape(b, c, CHUNK, n)
    c_c = cm.reshape(b, c, CHUNK, n)
    a_cs = jnp.cumsum(a_c, axis=-1)

    decay = segsum_exp(a_c)
    cb = jnp.einsum('bcln,bcsn->bcls', c_c, b_c)
    y_diag = jnp.einsum('bcls,bhcls,bcshp->bclhp', cb, decay, x_c)

    decay_states = jnp.exp(a_cs[..., -1:] - a_cs)
    states = jnp.einsum('bcln,bhcl,bclhp->bchpn', b_c, decay_states, x_c)
    chunk_decay = jnp.exp(a_cs[..., -1])

    def step(carry, inp):
        st, dec = inp
        return carry * dec[..., None, None] + st, carry

    init = jnp.zeros_like(states[:, 0])
    _, prev = lax.scan(step, init, (states.transpose(1, 0, 2, 3, 4), chunk_decay.transpose(2, 0, 1)))
    prev = prev.transpose(1, 0, 2, 3, 4)
    y_off = jnp.einsum('bcln,bchpn,bhcl->bclhp', c_c, prev, jnp.exp(a_cs))
    return (y_diag + y_off).reshape(b, s, nh, hp)


def hybrid_mixer(u, positions, w_in, conv_w, conv_b, dt_bias, a_log, d_skip, ssd_norm_g, w_out):
    b, s, _ = u.shape
    proj = u @ w_in
    splits = [ATTN_WIDTH, 2 * ATTN_WIDTH, 3 * ATTN_WIDTH,
              3 * ATTN_WIDTH + SSD_WIDTH, 3 * ATTN_WIDTH + SSD_WIDTH + CONV_CH]
    q, k, v, z, xbc, dt = jnp.split(proj, splits, axis=-1)

    q = apply_rope(q.reshape(b, s, N_ATTN_HEADS, HEAD_DIM), positions) * (HEAD_DIM ** -0.5)
    k = apply_rope(k.reshape(b, s, N_ATTN_HEADS, HEAD_DIM), positions)
    v = v.reshape(b, s, N_ATTN_HEADS, HEAD_DIM)
    outs, lses = [], []
    for window, dilation in DILATED_BRANCHES:
        o, l = dilated_branch(q, k, v, window, dilation)
        outs.append(o)
        lses.append(l)
    alpha = jax.nn.softmax(jnp.stack(lses, axis=0), axis=0)
    attn = jnp.einsum('absh,abshd->bshd', alpha, jnp.stack(outs, axis=0).astype(jnp.float32))
    attn = attn.reshape(b, s, ATTN_WIDTH).astype(u.dtype)

    xbc = jax.nn.silu(causal_conv(xbc, conv_w, conv_b))
    xs, bm, cm = jnp.split(xbc, [SSD_WIDTH, SSD_WIDTH + SSD_STATE], axis=-1)
    xs = xs.reshape(b, s, SSD_HEADS, SSD_HEAD_DIM).astype(jnp.float32)
    dt = jax.nn.softplus(dt.astype(jnp.float32) + dt_bias)
    a = -jnp.exp(a_log.astype(jnp.float32))
    y = ssd_chunked(xs * dt[..., None], dt * a, bm.astype(jnp.float32), cm.astype(jnp.float32))
    y = y + d_skip[:, None] * xs
    y = rms_norm(y.reshape(b, s, SSD_WIDTH) * jax.nn.silu(z.astype(jnp.float32)), ssd_norm_g)
    y = y.astype(u.dtype)

    return jnp.concatenate([attn, y], axis=-1) @ w_out


def _fwd_setup_inputs(seed: int = 0) -> dict:
    key = jax.random.key(seed)
    ks = jax.random.split(key, 24)
    f32 = jnp.float32

    def gain(k, n):
        return 1.0 + 0.01 * jax.random.normal(k, (DEPTH, n), f32)

    x = jax.random.normal(ks[0], (BATCH, SEQ, D_MODEL), f32)
    p = jax.random.normal(ks[1], (DEPTH, BATCH, SEQ, PLE_DIM), f32)
    offset = jax.random.randint(ks[2], (BATCH, 1), 0, 1024, dtype=jnp.int32)
    positions = (jnp.arange(SEQ, dtype=jnp.int32)[None, :] + offset).astype(jnp.int32)

    w_in = jax.random.normal(ks[3], (DEPTH, D_MODEL, IN_PROJ_WIDTH), f32) * D_MODEL ** -0.5
    conv_w = jax.random.normal(ks[4], (DEPTH, CONV_WIDTH, CONV_CH), f32) * CONV_WIDTH ** -0.5
    conv_b = 0.01 * jax.random.normal(ks[5], (DEPTH, CONV_CH), f32)
    dt0 = jnp.exp(jax.random.uniform(ks[6], (DEPTH, SSD_HEADS), f32, np.log(1e-3), np.log(1e-1)))
    dt_bias = dt0 + jnp.log(-jnp.expm1(-dt0))
    a_log = jnp.log(jax.random.uniform(ks[7], (DEPTH, SSD_HEADS), f32, 1.0, 16.0))
    d_skip = 1.0 + 0.01 * jax.random.normal(ks[8], (DEPTH, SSD_HEADS), f32)
    w_out = jax.random.normal(ks[9], (DEPTH, MIX_WIDTH, D_MODEL), f32) * MIX_WIDTH ** -0.5
    w_up = jax.random.normal(ks[10], (DEPTH, D_MODEL, D_FF), f32) * D_MODEL ** -0.5
    w_down = jax.random.normal(ks[11], (DEPTH, D_FF, D_MODEL), f32) * D_FF ** -0.5
    w_ple_gate = jax.random.normal(ks[12], (DEPTH, D_MODEL, D_MODEL), f32) * D_MODEL ** -0.5
    w_ple_proj = jax.random.normal(ks[13], (DEPTH, PLE_DIM, D_MODEL), f32) * PLE_DIM ** -0.5

    return {
        'x': x, 'p': p, 'positions': positions,
        'norm_mix_pre': gain(ks[14], D_MODEL), 'norm_mix_post': gain(ks[15], D_MODEL),
        'w_in': w_in, 'conv_w': conv_w, 'conv_b': conv_b, 'dt_bias': dt_bias,
        'a_log': a_log, 'd_skip': d_skip, 'ssd_norm_g': gain(ks[16], SSD_WIDTH),
        'w_out': w_out,
        'norm_mlp_pre': gain(ks[17], D_MODEL), 'norm_mlp_post': gain(ks[18], D_MODEL),
        'w_up': w_up, 'w_down': w_down,
        'w_ple_gate': w_ple_gate, 'w_ple_proj': w_ple_proj,
        'norm_ple_post': gain(ks[19], D_MODEL),
    }


def _fwd_reference(x, p, positions, norm_mix_pre, norm_mix_post, w_in, conv_w, conv_b, dt_bias,
              a_log, d_skip, ssd_norm_g, w_out, norm_mlp_pre, norm_mlp_post, w_up, w_down,
              w_ple_gate, w_ple_proj, norm_ple_post):
    h = x
    for i in range(DEPTH):
        u = rms_norm(h, norm_mix_pre[i])
        mix = hybrid_mixer(u, positions, w_in[i], conv_w[i], conv_b[i], dt_bias[i], a_log[i],
                           d_skip[i], ssd_norm_g[i], w_out[i])
        h = h + rms_norm(mix, norm_mix_post[i])
        u = rms_norm(h, norm_mlp_pre[i])
        ff = jnp.square(jax.nn.relu(u @ w_up[i])) @ w_down[i]
        h = h + rms_norm(ff, norm_mlp_post[i])
        ple = (p[i] @ w_ple_proj[i]) * jax.nn.sigmoid(h @ w_ple_gate[i])
        h = h + rms_norm(ple, norm_ple_post[i])
    return h


import jax as _jax
import jax.numpy as _jnp

TWIN_FORMAT = 'train_step'
FWD_PARAMS = ['x', 'p', 'positions', 'norm_mix_pre', 'norm_mix_post', 'w_in', 'conv_w', 'conv_b', 'dt_bias', 'a_log', 'd_skip', 'ssd_norm_g', 'w_out', 'norm_mlp_pre', 'norm_mlp_post', 'w_up', 'w_down', 'w_ple_gate', 'w_ple_proj', 'norm_ple_post']
TWIN_WEIGHTS = ['norm_mix_pre', 'norm_mix_post', 'w_in', 'conv_w', 'conv_b', 'dt_bias', 'a_log', 'd_skip', 'ssd_norm_g', 'w_out', 'norm_mlp_pre', 'norm_mlp_post', 'w_up', 'w_down', 'w_ple_gate', 'w_ple_proj', 'norm_ple_post']
TWIN_DIFF_INPUT = 'x'
TWIN_INPUTS = ['x', 'p', 'positions', 'norm_mix_pre', 'norm_mix_post', 'w_in', 'conv_w', 'conv_b', 'dt_bias', 'a_log', 'd_skip', 'ssd_norm_g', 'w_out', 'norm_mlp_pre', 'norm_mlp_post', 'w_up', 'w_down', 'w_ple_gate', 'w_ple_proj', 'norm_ple_post', 'loss_target', 'm_norm_mix_pre', 'm_norm_mix_post', 'm_w_in', 'm_conv_w', 'm_conv_b', 'm_dt_bias', 'm_a_log', 'm_d_skip', 'm_ssd_norm_g', 'm_w_out', 'm_norm_mlp_pre', 'm_norm_mlp_post', 'm_w_up', 'm_w_down', 'm_w_ple_gate', 'm_w_ple_proj', 'm_norm_ple_post', 'v_norm_mix_pre', 'v_norm_mix_post', 'v_w_in', 'v_conv_w', 'v_conv_b', 'v_dt_bias', 'v_a_log', 'v_d_skip', 'v_ssd_norm_g', 'v_w_out', 'v_norm_mlp_pre', 'v_norm_mlp_post', 'v_w_up', 'v_w_down', 'v_w_ple_gate', 'v_w_ple_proj', 'v_norm_ple_post']
TWIN_OUTPUTS = ['loss', 'grad_x', 'grad_norm_mix_pre', 'grad_norm_mix_post', 'grad_w_in', 'grad_conv_w', 'grad_conv_b', 'grad_dt_bias', 'grad_a_log', 'grad_d_skip', 'grad_ssd_norm_g', 'grad_w_out', 'grad_norm_mlp_pre', 'grad_norm_mlp_post', 'grad_w_up', 'grad_w_down', 'grad_w_ple_gate', 'grad_w_ple_proj', 'grad_norm_ple_post', 'delta_norm_mix_pre', 'delta_norm_mix_post', 'delta_w_in', 'delta_conv_w', 'delta_conv_b', 'delta_dt_bias', 'delta_a_log', 'delta_d_skip', 'delta_ssd_norm_g', 'delta_w_out', 'delta_norm_mlp_pre', 'delta_norm_mlp_post', 'delta_w_up', 'delta_w_down', 'delta_w_ple_gate', 'delta_w_ple_proj', 'delta_norm_ple_post', 'new_m_norm_mix_pre', 'new_m_norm_mix_post', 'new_m_w_in', 'new_m_conv_w', 'new_m_conv_b', 'new_m_dt_bias', 'new_m_a_log', 'new_m_d_skip', 'new_m_ssd_norm_g', 'new_m_w_out', 'new_m_norm_mlp_pre', 'new_m_norm_mlp_post', 'new_m_w_up', 'new_m_w_down', 'new_m_w_ple_gate', 'new_m_w_ple_proj', 'new_m_norm_ple_post', 'new_v_norm_mix_pre', 'new_v_norm_mix_post', 'new_v_w_in', 'new_v_conv_w', 'new_v_conv_b', 'new_v_dt_bias', 'new_v_a_log', 'new_v_d_skip', 'new_v_ssd_norm_g', 'new_v_w_out', 'new_v_norm_mlp_pre', 'new_v_norm_mlp_post', 'new_v_w_up', 'new_v_w_down', 'new_v_w_ple_gate', 'new_v_w_ple_proj', 'new_v_norm_ple_post']
TWIN_LEAF_KINDS = {'loss': 'loss', 'grad_x': 'grad_x', 'grad_norm_mix_pre': 'grad_w', 'grad_norm_mix_post': 'grad_w', 'grad_w_in': 'grad_w', 'grad_conv_w': 'grad_w', 'grad_conv_b': 'grad_w', 'grad_dt_bias': 'grad_w', 'grad_a_log': 'grad_w', 'grad_d_skip': 'grad_w', 'grad_ssd_norm_g': 'grad_w', 'grad_w_out': 'grad_w', 'grad_norm_mlp_pre': 'grad_w', 'grad_norm_mlp_post': 'grad_w', 'grad_w_up': 'grad_w', 'grad_w_down': 'grad_w', 'grad_w_ple_gate': 'grad_w', 'grad_w_ple_proj': 'grad_w', 'grad_norm_ple_post': 'grad_w', 'delta_norm_mix_pre': 'delta_w', 'delta_norm_mix_post': 'delta_w', 'delta_w_in': 'delta_w', 'delta_conv_w': 'delta_w', 'delta_conv_b': 'delta_w', 'delta_dt_bias': 'delta_w', 'delta_a_log': 'delta_w', 'delta_d_skip': 'delta_w', 'delta_ssd_norm_g': 'delta_w', 'delta_w_out': 'delta_w', 'delta_norm_mlp_pre': 'delta_w', 'delta_norm_mlp_post': 'delta_w', 'delta_w_up': 'delta_w', 'delta_w_down': 'delta_w', 'delta_w_ple_gate': 'delta_w', 'delta_w_ple_proj': 'delta_w', 'delta_norm_ple_post': 'delta_w', 'new_m_norm_mix_pre': 'new_m', 'new_m_norm_mix_post': 'new_m', 'new_m_w_in': 'new_m', 'new_m_conv_w': 'new_m', 'new_m_conv_b': 'new_m', 'new_m_dt_bias': 'new_m', 'new_m_a_log': 'new_m', 'new_m_d_skip': 'new_m', 'new_m_ssd_norm_g': 'new_m', 'new_m_w_out': 'new_m', 'new_m_norm_mlp_pre': 'new_m', 'new_m_norm_mlp_post': 'new_m', 'new_m_w_up': 'new_m', 'new_m_w_down': 'new_m', 'new_m_w_ple_gate': 'new_m', 'new_m_w_ple_proj': 'new_m', 'new_m_norm_ple_post': 'new_m', 'new_v_norm_mix_pre': 'new_v', 'new_v_norm_mix_post': 'new_v', 'new_v_w_in': 'new_v', 'new_v_conv_w': 'new_v', 'new_v_conv_b': 'new_v', 'new_v_dt_bias': 'new_v', 'new_v_a_log': 'new_v', 'new_v_d_skip': 'new_v', 'new_v_ssd_norm_g': 'new_v', 'new_v_w_out': 'new_v', 'new_v_norm_mlp_pre': 'new_v', 'new_v_norm_mlp_post': 'new_v', 'new_v_w_up': 'new_v', 'new_v_w_down': 'new_v', 'new_v_w_ple_gate': 'new_v', 'new_v_w_ple_proj': 'new_v', 'new_v_norm_ple_post': 'new_v'}


def _forward(args):
    return _fwd_reference(*[args[k] for k in FWD_PARAMS])


def _output_shape():
    out = _jax.eval_shape(lambda: _forward(_fwd_setup_inputs(0)))
    return out.shape, out.dtype

N_MICROBATCH = 1
ADAM_LR = 0.001
ADAM_B1 = 0.9
ADAM_B2 = 0.999
ADAM_EPS = 1e-08
ADAM_WD = 0.01
ADAM_STEP = 10
PER_EXAMPLE_BATCH_AXIS = {'x': 0, 'p': 1, 'positions': 0, 'loss_target': 0}
SHARED_INPUTS = []
_WEIGHT_DTYPES = {'norm_mix_pre': _jnp.float32, 'norm_mix_post': _jnp.float32, 'w_in': _jnp.float32, 'conv_w': _jnp.float32, 'conv_b': _jnp.float32, 'dt_bias': _jnp.float32, 'a_log': _jnp.float32, 'd_skip': _jnp.float32, 'ssd_norm_g': _jnp.float32, 'w_out': _jnp.float32, 'norm_mlp_pre': _jnp.float32, 'norm_mlp_post': _jnp.float32, 'w_up': _jnp.float32, 'w_down': _jnp.float32, 'w_ple_gate': _jnp.float32, 'w_ple_proj': _jnp.float32, 'norm_ple_post': _jnp.float32}
MOMENT_SCALE = {'norm_mix_pre': 8.338615e-01, 'norm_mix_post': 3.214518e+01, 'w_in': 4.904389e-01, 'conv_w': 1.753091e+00, 'conv_b': 6.621198e+00, 'dt_bias': 3.642868e+00, 'a_log': 7.615983e+00, 'd_skip': 6.031021e+00, 'ssd_norm_g': 3.404848e+00, 'w_out': 2.719712e+00, 'norm_mlp_pre': 1.175950e+00, 'norm_mlp_post': 3.300456e+01, 'w_up': 5.894217e-01, 'w_down': 2.630776e+00, 'w_ple_gate': 1.934452e-01, 'w_ple_proj': 3.531725e-01, 'norm_ple_post': 3.285331e+01}


def _to_microbatches(a, axis):
    t = _jnp.moveaxis(a, axis, 0)
    t = t.reshape((N_MICROBATCH, t.shape[0] // N_MICROBATCH) + t.shape[1:])
    return _jnp.moveaxis(t, 1, axis + 1)


def setup_inputs(seed: int = 0) -> dict:
    inp = _fwd_setup_inputs(seed)
    key = _jax.random.fold_in(_jax.random.key(seed), 7919)
    shape, _ = _output_shape()
    out = dict(inp)
    out["loss_target"] = _jax.random.normal(_jax.random.fold_in(key, 0), shape, _jnp.float32)
    for i, name in enumerate(TWIN_WEIGHTS):
        w = inp[name].astype(_jnp.float32)
        if MOMENT_SCALE is None:
            s = _jnp.sqrt(_jnp.mean(_jnp.square(w)) + 1e-30)
        else:
            s = MOMENT_SCALE[name]
        km, kv = _jax.random.split(_jax.random.fold_in(key, i + 1))
        out[name] = w
        out["m_" + name] = s * _jax.random.normal(km, w.shape, _jnp.float32)
        out["v_" + name] = (s * s) * _jax.random.uniform(kv, w.shape, _jnp.float32, 0.5, 1.5)
    if N_MICROBATCH > 1:
        for name, axis in PER_EXAMPLE_BATCH_AXIS.items():
            out[name] = _to_microbatches(out[name], axis)
    return {'x': out['x'], 'p': out['p'], 'positions': out['positions'], 'norm_mix_pre': out['norm_mix_pre'], 'norm_mix_post': out['norm_mix_post'], 'w_in': out['w_in'], 'conv_w': out['conv_w'], 'conv_b': out['conv_b'], 'dt_bias': out['dt_bias'], 'a_log': out['a_log'], 'd_skip': out['d_skip'], 'ssd_norm_g': out['ssd_norm_g'], 'w_out': out['w_out'], 'norm_mlp_pre': out['norm_mlp_pre'], 'norm_mlp_post': out['norm_mlp_post'], 'w_up': out['w_up'], 'w_down': out['w_down'], 'w_ple_gate': out['w_ple_gate'], 'w_ple_proj': out['w_ple_proj'], 'norm_ple_post': out['norm_ple_post'], 'loss_target': out['loss_target'], 'm_norm_mix_pre': out['m_norm_mix_pre'], 'm_norm_mix_post': out['m_norm_mix_post'], 'm_w_in': out['m_w_in'], 'm_conv_w': out['m_conv_w'], 'm_conv_b': out['m_conv_b'], 'm_dt_bias': out['m_dt_bias'], 'm_a_log': out['m_a_log'], 'm_d_skip': out['m_d_skip'], 'm_ssd_norm_g': out['m_ssd_norm_g'], 'm_w_out': out['m_w_out'], 'm_norm_mlp_pre': out['m_norm_mlp_pre'], 'm_norm_mlp_post': out['m_norm_mlp_post'], 'm_w_up': out['m_w_up'], 'm_w_down': out['m_w_down'], 'm_w_ple_gate': out['m_w_ple_gate'], 'm_w_ple_proj': out['m_w_ple_proj'], 'm_norm_ple_post': out['m_norm_ple_post'], 'v_norm_mix_pre': out['v_norm_mix_pre'], 'v_norm_mix_post': out['v_norm_mix_post'], 'v_w_in': out['v_w_in'], 'v_conv_w': out['v_conv_w'], 'v_conv_b': out['v_conv_b'], 'v_dt_bias': out['v_dt_bias'], 'v_a_log': out['v_a_log'], 'v_d_skip': out['v_d_skip'], 'v_ssd_norm_g': out['v_ssd_norm_g'], 'v_w_out': out['v_w_out'], 'v_norm_mlp_pre': out['v_norm_mlp_pre'], 'v_norm_mlp_post': out['v_norm_mlp_post'], 'v_w_up': out['v_w_up'], 'v_w_down': out['v_w_down'], 'v_w_ple_gate': out['v_w_ple_gate'], 'v_w_ple_proj': out['v_w_ple_proj'], 'v_norm_ple_post': out['v_norm_ple_post']}


def _loss(weights, diff, rest, loss_target):
    with _jax.named_scope("forward"):
        args = {**rest, TWIN_DIFF_INPUT: diff, **{k: w.astype(_WEIGHT_DTYPES[k]) for k, w in weights.items()}}
        y = _forward(args)
    with _jax.named_scope("loss_head"):
        err = _jnp.square(y.astype(_jnp.float32) - loss_target)
        return 0.5 * _jnp.sum(_jnp.mean(err, axis=-1)) if err.ndim else 0.5 * err


def _adamw(w, g, m, v):
    m = ADAM_B1 * m + (1.0 - ADAM_B1) * g
    v = ADAM_B2 * v + (1.0 - ADAM_B2) * _jnp.square(g)
    m_hat = m / (1.0 - ADAM_B1 ** ADAM_STEP)
    v_hat = v / (1.0 - ADAM_B2 ** ADAM_STEP)
    delta = -ADAM_LR * (m_hat / (_jnp.sqrt(v_hat) + ADAM_EPS) + ADAM_WD * w)
    return delta, m, v


def reference(x, p, positions, norm_mix_pre, norm_mix_post, w_in, conv_w, conv_b, dt_bias, a_log, d_skip, ssd_norm_g, w_out, norm_mlp_pre, norm_mlp_post, w_up, w_down, w_ple_gate, w_ple_proj, norm_ple_post, loss_target, m_norm_mix_pre, m_norm_mix_post, m_w_in, m_conv_w, m_conv_b, m_dt_bias, m_a_log, m_d_skip, m_ssd_norm_g, m_w_out, m_norm_mlp_pre, m_norm_mlp_post, m_w_up, m_w_down, m_w_ple_gate, m_w_ple_proj, m_norm_ple_post, v_norm_mix_pre, v_norm_mix_post, v_w_in, v_conv_w, v_conv_b, v_dt_bias, v_a_log, v_d_skip, v_ssd_norm_g, v_w_out, v_norm_mlp_pre, v_norm_mlp_post, v_w_up, v_w_down, v_w_ple_gate, v_w_ple_proj, v_norm_ple_post):
    given = dict(x=x, p=p, positions=positions, norm_mix_pre=norm_mix_pre, norm_mix_post=norm_mix_post, w_in=w_in, conv_w=conv_w, conv_b=conv_b, dt_bias=dt_bias, a_log=a_log, d_skip=d_skip, ssd_norm_g=ssd_norm_g, w_out=w_out, norm_mlp_pre=norm_mlp_pre, norm_mlp_post=norm_mlp_post, w_up=w_up, w_down=w_down, w_ple_gate=w_ple_gate, w_ple_proj=w_ple_proj, norm_ple_post=norm_ple_post, loss_target=loss_target, m_norm_mix_pre=m_norm_mix_pre, m_norm_mix_post=m_norm_mix_post, m_w_in=m_w_in, m_conv_w=m_conv_w, m_conv_b=m_conv_b, m_dt_bias=m_dt_bias, m_a_log=m_a_log, m_d_skip=m_d_skip, m_ssd_norm_g=m_ssd_norm_g, m_w_out=m_w_out, m_norm_mlp_pre=m_norm_mlp_pre, m_norm_mlp_post=m_norm_mlp_post, m_w_up=m_w_up, m_w_down=m_w_down, m_w_ple_gate=m_w_ple_gate, m_w_ple_proj=m_w_ple_proj, m_norm_ple_post=m_norm_ple_post, v_norm_mix_pre=v_norm_mix_pre, v_norm_mix_post=v_norm_mix_post, v_w_in=v_w_in, v_conv_w=v_conv_w, v_conv_b=v_conv_b, v_dt_bias=v_dt_bias, v_a_log=v_a_log, v_d_skip=v_d_skip, v_ssd_norm_g=v_ssd_norm_g, v_w_out=v_w_out, v_norm_mlp_pre=v_norm_mlp_pre, v_norm_mlp_post=v_norm_mlp_post, v_w_up=v_w_up, v_w_down=v_w_down, v_w_ple_gate=v_w_ple_gate, v_w_ple_proj=v_w_ple_proj, v_norm_ple_post=v_norm_ple_post)
    weights = {n: given[n] for n in TWIN_WEIGHTS}
    shared = {n: given[n] for n in SHARED_INPUTS}
    per_example = {n: given[n] for n in ['x', 'p', 'positions']}
    grad_fn = _jax.value_and_grad(_loss, argnums=(0, 1))

    def one_microbatch(ex, loss_target):
        ex = dict(ex)
        diff = ex.pop(TWIN_DIFF_INPUT)
        return grad_fn(weights, diff, {**shared, **ex}, loss_target)

    if N_MICROBATCH == 1:
        loss, (grad_w, grad_x) = one_microbatch(per_example, given["loss_target"])
    else:
        def body(carry, xs):
            loss_sum, grad_sum = carry
            l_k, (gw_k, gx_k) = one_microbatch(xs[0], xs[1])
            with _jax.named_scope("update"):
                return (loss_sum + l_k, _jax.tree.map(_jnp.add, grad_sum, gw_k)), gx_k

        init = (_jnp.zeros((), _jnp.float32), _jax.tree.map(_jnp.zeros_like, weights))
        (loss, grad_w), grad_x = _jax.lax.scan(body, init, (per_example, given["loss_target"]))
    with _jax.named_scope("update"):
        delta_w, new_m, new_v = {}, {}, {}
        for n in TWIN_WEIGHTS:
            delta_w[n], new_m[n], new_v[n] = _adamw(weights[n], grad_w[n], given["m_" + n], given["v_" + n])
    return (loss, grad_x, *[grad_w[n] for n in TWIN_WEIGHTS], *[delta_w[n] for n in TWIN_WEIGHTS],
            *[new_m[n] for n in TWIN_WEIGHTS], *[new_v[n] for n in TWIN_WEIGHTS])
```

```python
import functools
import math

import jax
import jax.numpy as jnp
from jax import lax
from jax.experimental import pallas as pl
from jax.experimental.pallas import tpu as pltpu

F32 = jnp.float32
BF16 = jnp.bfloat16
MESH = pl.DeviceIdType.MESH
HIGHEST = lax.Precision.HIGHEST

N_DEV = 8
T = 4096
D = 1024
HEADS = 8
HD = 64
AW = 512
NS = 128
CONV_K = 4
CONV_CH = 768
CHUNK = 128
DFF = 4096
PLE = 256
EPS = 1e-6
ROPE_THETA = 10000.0
DILATIONS = (1, 4, 16)
QBLK = 128
NEG = -1e30
IN_W = 2824
W_IN_SHARD = 353
W_IN_SHARD_PAD = 384
DT_PAD = 128

ADAM_LR, ADAM_B1, ADAM_B2, ADAM_EPS, ADAM_WD, ADAM_STEP = 0.001, 0.9, 0.999, 1e-08, 0.01, 10

VMEM_LIMIT = 56 * 1024 * 1024


def _cparams(sem=None):
    return pltpu.CompilerParams(dimension_semantics=sem, vmem_limit_bytes=VMEM_LIMIT)


def _dot(a, b, ca, cb, precision=None):
    return lax.dot_general(a, b, (((ca,), (cb,)), ((), ())), preferred_element_type=F32, precision=precision)


def _nn(a, b):
    return _dot(a, b, 1, 0)


def _nt(a, b):
    return _dot(a, b, 1, 1)


def _tn(a, b):
    return _dot(a, b, 0, 0)


def _sigmoid(x):
    return 1.0 / (1.0 + jnp.exp(-x))


def _softplus(x):
    return jnp.maximum(x, 0.0) + jnp.log(1.0 + jnp.exp(-jnp.abs(x)))


def _mm(a, b, *, ta=False, tb=False, tm, tn, tk, name,
        a_pre=None, a_rows=(), a_cols=(), b_pre=None, b_rows=(), b_cols=(),
        epi=None, epi_tiles=(), out_dtypes=(F32,)):
    if ta:
        K, M = a.shape
    else:
        M, K = a.shape
    if tb:
        N, K2 = b.shape
    else:
        K2, N = b.shape
    assert K == K2 and M % tm == 0 and N % tn == 0 and K % tk == 0, (name, a.shape, b.shape)
    nk = K // tk
    if ta:
        a_spec = pl.BlockSpec((tk, tm), lambda i, j, k: (k, i))
        a_row_specs = [pl.BlockSpec((tk, 1), lambda i, j, k: (k, 0)) for _ in a_rows]
        a_col_specs = [pl.BlockSpec((1, tm), lambda i, j, k: (0, i)) for _ in a_cols]
    else:
        a_spec = pl.BlockSpec((tm, tk), lambda i, j, k: (i, k))
        a_row_specs = [pl.BlockSpec((tm, 1), lambda i, j, k: (i, 0)) for _ in a_rows]
        a_col_specs = [pl.BlockSpec((1, tk), lambda i, j, k: (0, k)) for _ in a_cols]
    if tb:
        b_spec = pl.BlockSpec((tn, tk), lambda i, j, k: (j, k))
        b_row_specs = [pl.BlockSpec((tn, 1), lambda i, j, k: (j, 0)) for _ in b_rows]
        b_col_specs = [pl.BlockSpec((1, tk), lambda i, j, k: (0, k)) for _ in b_cols]
    else:
        b_spec = pl.BlockSpec((tk, tn), lambda i, j, k: (k, j))
        b_row_specs = [pl.BlockSpec((tk, 1), lambda i, j, k: (k, 0)) for _ in b_rows]
        b_col_specs = [pl.BlockSpec((1, tn), lambda i, j, k: (0, j)) for _ in b_cols]
    o_spec = pl.BlockSpec((tm, tn), lambda i, j, k: (i, j))
    na, nb, ne, no = len(a_rows) + len(a_cols), len(b_rows) + len(b_cols), len(epi_tiles), len(out_dtypes)

    def body(*refs):
        a_ref, b_ref = refs[0], refs[1]
        a_ex = refs[2:2 + na]
        b_ex = refs[2 + na:2 + na + nb]
        e_ex = refs[2 + na + nb:2 + na + nb + ne]
        outs = refs[2 + na + nb + ne:2 + na + nb + ne + no]
        acc = refs[-1]
        k = pl.program_id(2)

        @pl.when(k == 0)
        def _():
            acc[...] = jnp.zeros_like(acc)

        at = a_ref[...]
        if a_pre is not None:
            at = a_pre(at, *[r[...] for r in a_ex])
        bt = b_ref[...]
        if b_pre is not None:
            bt = b_pre(bt, *[r[...] for r in b_ex])
        acc[...] += _dot(at.astype(BF16), bt.astype(BF16), 0 if ta else 1, 1 if tb else 0)

        @pl.when(k == nk - 1)
        def _():
            res = acc[...]
            vals = epi(res, *[r[...] for r in e_ex]) if epi is not None else (res,)
            for o_ref, val in zip(outs, vals):
                o_ref[...] = val.astype(o_ref.dtype)

    outs = pl.pallas_call(
        body, name=name,
        grid=(M // tm, N // tn, nk),
        in_specs=[a_spec, b_spec] + a_row_specs + a_col_specs + b_row_specs + b_col_specs + [o_spec] * ne,
        out_specs=[o_spec] * no,
        out_shape=[jax.ShapeDtypeStruct((M, N), dt) for dt in out_dtypes],
        scratch_shapes=[pltpu.VMEM((tm, tn), F32)],
        compiler_params=_cparams(("parallel", "parallel", "arbitrary")),
    )(a, b, *a_rows, *a_cols, *b_rows, *b_cols, *epi_tiles)
    return outs[0] if no == 1 else outs


def _rowwise(fn, rows, vecs, out_rows, out_sums, *, tm, name):
    specs, arrs = [], []
    R = None
    for r in rows:
        if isinstance(r, tuple):
            arr, width, cb = r
            specs.append(pl.BlockSpec((tm, width), lambda i, cb=cb: (i, cb)))
        else:
            arr = r
            specs.append(pl.BlockSpec((tm, arr.shape[1]), lambda i: (i, 0)))
        R = arr.shape[0] if R is None else R
        assert arr.shape[0] == R, name
        arrs.append(arr)
    assert R % tm == 0, name
    for v in vecs:
        specs.append(pl.BlockSpec(v.shape, lambda i: (0, 0)))
        arrs.append(v)
    nr, nv, no, ns = len(rows), len(vecs), len(out_rows), len(out_sums)
    out_specs = [pl.BlockSpec((tm, w), lambda i: (i, 0)) for w, _ in out_rows]
    out_specs += [pl.BlockSpec(s, lambda i: (0, 0)) for s in out_sums]
    out_shape = [jax.ShapeDtypeStruct((R, w), dt) for w, dt in out_rows]
    out_shape += [jax.ShapeDtypeStruct(s, F32) for s in out_sums]

    def body(*refs):
        ins = [r[...] for r in refs[:nr + nv]]
        o_refs = refs[nr + nv:nr + nv + no]
        s_refs = refs[nr + nv + no:]
        o_vals, s_vals = fn(*ins)
        for ref, val in zip(o_refs, o_vals):
            ref[...] = val.astype(ref.dtype)
        if ns:
            @pl.when(pl.program_id(0) == 0)
            def _():
                for ref in s_refs:
                    ref[...] = jnp.zeros_like(ref)
            for ref, val in zip(s_refs, s_vals):
                ref[...] += val

    outs = pl.pallas_call(
        body, name=name, grid=(R // tm,), in_specs=specs, out_specs=out_specs, out_shape=out_shape,
        compiler_params=_cparams(("arbitrary",) if ns else ("parallel",)),
    )(*arrs)
    return outs


def _colsum(x):
    return jnp.sum(x, axis=0, keepdims=True)


def _rstd(x):
    return lax.rsqrt(jnp.mean(x * x, axis=-1, keepdims=True) + EPS)


def _rms_bwd(xn, r, g, dy):
    dn = dy * g
    return r * (dn - xn * jnp.mean(dn * xn, axis=-1, keepdims=True))


def _partner(t):
    parts = []
    for s in range(t.shape[1] // 128):
        ts = t[:, 128 * s:128 * (s + 1)]
        lane = lax.broadcasted_iota(jnp.int32, ts.shape, 1)
        up = pltpu.roll(ts, 96, 1)
        down = pltpu.roll(ts, 32, 1)
        parts.append(jnp.where((lane % 64) < 32, up, down))
    return jnp.concatenate(parts, axis=1)


def _rope_fwd(qkvz, cosf, sins):
    def fn(q, k, v, c, s):
        qr = (q * c + _partner(q) * s) * (HD ** -0.5)
        kr = k * c + _partner(k) * s
        return (qr, kr, v), ()
    return _rowwise(fn, [(qkvz, AW, 0), (qkvz, AW, 1), (qkvz, AW, 2), cosf, sins], [],
                    [(AW, BF16), (AW, BF16), (AW, BF16)], [], tm=512, name="rope_fwd")


def _rope_bwd(dqs, dks, dvs, dz, cosf, sins):
    def fn(q1, q2, q3, k1, k2, k3, v1, v2, v3, z, c, s):
        dqr = (q1 + q2 + q3) * (HD ** -0.5)
        dkr = k1 + k2 + k3
        dq = dqr * c + _partner(dqr * s)
        dk = dkr * c + _partner(dkr * s)
        return (jnp.concatenate([dq, dk, v1 + v2 + v3, z], axis=1),), ()
    return _rowwise(fn, [*dqs, *dks, *dvs, dz, cosf, sins], [], [(4 * AW, F32)], [], tm=256, name="rope_bwd")[0]


def _band_masks():
    qi = lax.broadcasted_iota(jnp.int32, (QBLK, QBLK), 0)
    kj = lax.broadcasted_iota(jnp.int32, (QBLK, QBLK), 1)
    return kj >= qi, kj <= qi


def _attn_fwd(q, k, v, d):
    L = q.shape[0]
    nb = L // QBLK

    def body(q_ref, kp_ref, kc_ref, vp_ref, vc_ref, o_ref, l_ref):
        n = pl.program_id(1)
        mask_p, mask_c = _band_masks()
        neg_first = jnp.where(n > 0, 0.0, NEG)
        for h in range(HEADS):
            sl = pl.ds(HD * h, HD)
            qh = q_ref[:, sl]
            sp = jnp.where(mask_p, _nt(qh, kp_ref[:, sl]), NEG) + neg_first
            sc = jnp.where(mask_c, _nt(qh, kc_ref[:, sl]), NEG)
            m = jnp.maximum(jnp.max(sp, axis=1, keepdims=True), jnp.max(sc, axis=1, keepdims=True))
            ep = jnp.exp(sp - m)
            ec = jnp.exp(sc - m)
            den = jnp.sum(ep, axis=1, keepdims=True) + jnp.sum(ec, axis=1, keepdims=True)
            inv = 1.0 / den
            o = _nn((ep * inv).astype(BF16), vp_ref[:, sl]) + _nn((ec * inv).astype(BF16), vc_ref[:, sl])
            o_ref[:, sl] = o
            l_ref[:, sl] = jnp.broadcast_to(m + jnp.log(den), (QBLK, HD))

    cur = pl.BlockSpec((QBLK, AW), lambda r, n: (n, r))
    prev = pl.BlockSpec((QBLK, AW), lambda r, n: (jnp.maximum(n - 1, 0), r))
    return pl.pallas_call(
        body, name=f"attn_fwd_d{d}", grid=(d, nb),
        in_specs=[cur, prev, cur, prev, cur], out_specs=[cur, cur],
        out_shape=[jax.ShapeDtypeStruct((L, d * AW), F32)] * 2,
        compiler_params=_cparams(("parallel", "parallel")),
    )(q, k, k, v, v)


def _attn_bwd(q, k, v, do, at, lse, d):
    L = q.shape[0]
    nb = L // QBLK

    def body(q0_ref, q1_ref, kp_ref, kc_ref, vp_ref, vc_ref, do0_ref, do1_ref, at0_ref, at1_ref,
             l0_ref, l1_ref, dq_ref, dk_ref, dv_ref):
        n = pl.program_id(1)
        mask_p, mask_c = _band_masks()
        neg_first = jnp.where(n > 0, 0.0, NEG)
        neg_last = jnp.where(n < nb - 1, 0.0, NEG)
        for h in range(HEADS):
            sl = pl.ds(HD * h, HD)
            one = pl.ds(HD * h, 1)
            q0, q1 = q0_ref[:, sl], q1_ref[:, sl]
            kp, kc, vp, vc = kp_ref[:, sl], kc_ref[:, sl], vp_ref[:, sl], vc_ref[:, sl]
            do0, do1 = do0_ref[:, sl], do1_ref[:, sl]
            dl0 = jnp.sum(do0 * at0_ref[:, sl], axis=1, keepdims=True)
            dl1 = jnp.sum(do1 * at1_ref[:, sl], axis=1, keepdims=True)
            ls0, ls1 = l0_ref[:, one], l1_ref[:, one]
            do0b, do1b = do0.astype(BF16), do1.astype(BF16)
            pp = jnp.exp(jnp.where(mask_p, _nt(q0, kp), NEG) + neg_first - ls0)
            pc = jnp.exp(jnp.where(mask_c, _nt(q0, kc), NEG) - ls0)
            dsp = (pp * (_nt(do0b, vp) - dl0)).astype(BF16)
            dsc = (pc * (_nt(do0b, vc) - dl0)).astype(BF16)
            dq_ref[:, sl] = _nn(dsp, kp) + _nn(dsc, kc)
            px = jnp.exp(jnp.where(mask_p, _nt(q1, kc), NEG) + neg_last - ls1)
            dsx = (px * (_nt(do1b, vc) - dl1)).astype(BF16)
            dv_ref[:, sl] = _tn(pc.astype(BF16), do0b) + _tn(px.astype(BF16), do1b)
            dk_ref[:, sl] = _tn(dsc, q0) + _tn(dsx, q1)

    cur = pl.BlockSpec((QBLK, AW), lambda r, n: (n, r))
    prev = pl.BlockSpec((QBLK, AW), lambda r, n: (jnp.maximum(n - 1, 0), r))
    nxt = pl.BlockSpec((QBLK, AW), lambda r, n: (jnp.minimum(n + 1, nb - 1), r))
    return pl.pallas_call(
        body, name=f"attn_bwd_d{d}", grid=(d, nb),
        in_specs=[cur, nxt, prev, cur, prev, cur, cur, nxt, cur, nxt, cur, nxt], out_specs=[cur, cur, cur],
        out_shape=[jax.ShapeDtypeStruct((L, d * AW), F32)] * 3,
        compiler_params=_cparams(("parallel", "parallel")),
    )(q, q, k, k, v, v, do, do, at, at, lse, lse)


def _attn_merge(outs, lses):
    def fn(o1, o2, o3, l1, l2, l3):
        m = jnp.maximum(jnp.maximum(l1, l2), l3)
        e1, e2, e3 = jnp.exp(l1 - m), jnp.exp(l2 - m), jnp.exp(l3 - m)
        s = e1 + e2 + e3
        inv = 1.0 / s
        return ((e1 * inv) * o1 + (e2 * inv) * o2 + (e3 * inv) * o3, m + jnp.log(s)), ()
    return _rowwise(fn, [*outs, *lses], [], [(AW, F32), (AW, F32)], [], tm=512, name="attn_merge")


CONV_TM = 512
HALO = 8


def _conv_pre(ext, w, b):
    y = b + w[3] * ext
    for kk in range(1, CONV_K):
        y = y + w[3 - kk] * pltpu.roll(ext, kk, 0)
    return y


def _rows_to_block(rows, n, width):
    ri = lax.broadcasted_iota(jnp.int32, (n, width), 0)
    out = jnp.zeros((n, width), F32)
    for j, r in enumerate(rows):
        out = out + jnp.where(ri == j, r, 0.0)
    return out


def _conv_fwd(xbc, w, b):
    nblk = T // CONV_TM

    def body(x_ref, h_ref, w_ref, b_ref, o_ref):
        i = pl.program_id(0)
        halo = jnp.where(i > 0, h_ref[...], 0.0)
        ext = jnp.concatenate([halo, x_ref[...]], axis=0)
        y = _conv_pre(ext, [w_ref[pl.ds(j, 1), :] for j in range(CONV_K)], b_ref[...])[HALO:]
        o_ref[...] = y * _sigmoid(y)

    return pl.pallas_call(
        body, name="conv_fwd", grid=(nblk,),
        in_specs=[pl.BlockSpec((CONV_TM, CONV_CH), lambda i: (i, 0)),
                  pl.BlockSpec((HALO, CONV_CH), lambda i: (jnp.maximum(i * (CONV_TM // HALO) - 1, 0), 0)),
                  pl.BlockSpec((CONV_K, CONV_CH), lambda i: (0, 0)),
                  pl.BlockSpec((1, CONV_CH), lambda i: (0, 0))],
        out_specs=pl.BlockSpec((CONV_TM, CONV_CH), lambda i: (i, 0)),
        out_shape=jax.ShapeDtypeStruct((T, CONV_CH), F32),
        compiler_params=_cparams(("parallel",)),
    )(xbc, xbc, w, b)


def _conv_bwd(xbc, dact, w, b):
    nblk = T // CONV_TM
    per = CONV_TM // HALO

    def body(x_ref, xb_ref, xa_ref, g_ref, ga_ref, w_ref, b_ref, dx_ref, dw_ref):
        i = pl.program_id(0)
        wv = [w_ref[pl.ds(j, 1), :] for j in range(CONV_K)]
        before = jnp.where(i > 0, xb_ref[...], 0.0)
        last = i == nblk - 1
        after = jnp.where(last, 0.0, xa_ref[...])
        g_after = jnp.where(last, 0.0, ga_ref[...])
        ext = jnp.concatenate([before, x_ref[...], after], axis=0)
        y = _conv_pre(ext, wv, b_ref[...])[HALO:]
        sg = _sigmoid(y)
        dy = jnp.concatenate([g_ref[...], g_after], axis=0) * (sg * (1.0 + y * (1.0 - sg)))
        n = CONV_TM + HALO
        dx = wv[3] * dy
        for kk in range(1, CONV_K):
            dx = dx + wv[3 - kk] * pltpu.roll(dy, n - kk, 0)
        dx_ref[...] = dx[:CONV_TM]
        dyc = dy[:CONV_TM]
        rows = [jnp.sum(dyc * (pltpu.roll(ext, 3 - j, 0) if j < 3 else ext)[HALO:HALO + CONV_TM], axis=0, keepdims=True)
                for j in range(CONV_K)]
        rows.append(jnp.sum(dyc, axis=0, keepdims=True))
        part = _rows_to_block(rows, 8, CONV_CH)

        @pl.when(i == 0)
        def _():
            dw_ref[...] = jnp.zeros_like(dw_ref)
        dw_ref[...] += part

    blk = pl.BlockSpec((CONV_TM, CONV_CH), lambda i: (i, 0))
    hb = pl.BlockSpec((HALO, CONV_CH), lambda i: (jnp.maximum(i * per - 1, 0), 0))
    ha = pl.BlockSpec((HALO, CONV_CH), lambda i: (jnp.minimum((i + 1) * per, T // HALO - 1), 0))
    return pl.pallas_call(
        body, name="conv_bwd", grid=(nblk,),
        in_specs=[blk, hb, ha, blk, ha, pl.BlockSpec((CONV_K, CONV_CH), lambda i: (0, 0)),
                  pl.BlockSpec((1, CONV_CH), lambda i: (0, 0))],
        out_specs=[blk, pl.BlockSpec((8, CONV_CH), lambda i: (0, 0))],
        out_shape=[jax.ShapeDtypeStruct((T, CONV_CH), F32), jax.ShapeDtypeStruct((8, CONV_CH), F32)],
        compiler_params=_cparams(("arbitrary",)),
    )(xbc, xbc, xbc, dact, dact, w, b)


def _pick(mat, h):
    lane = lax.broadcasted_iota(jnp.int32, mat.shape, 1)
    return jnp.sum(jnp.where(lane == h, mat, 0.0), axis=1, keepdims=True)


def _ssd_common(dt_ref, bias_ref, alog_ref, b_ref, c_ref):
    li = lax.broadcasted_iota(jnp.int32, (CHUNK, CHUNK), 0)
    si = lax.broadcasted_iota(jnp.int32, (CHUNK, CHUNK), 1)
    tri = li >= si
    dtp = dt_ref[...] + bias_ref[...]
    dt = _softplus(dtp)
    A = -jnp.exp(alog_ref[...])
    a = dt * A
    cs = jnp.dot(tri.astype(F32), a, precision=HIGHEST, preferred_element_type=F32)
    Bm = b_ref[...].astype(BF16)
    Cm = c_ref[...].astype(BF16)
    cb = _nt(Cm, Bm)
    return li, tri, dtp, dt, A, cs, Bm, Cm, cb


def _ssd_head(h, li, tri, dt, cs, cb, xs_ref, state_ref):
    cs_h = _pick(cs, h)
    lam = jnp.exp(jnp.where(tri, cs_h - jnp.broadcast_to(cs_h, (CHUNK, CHUNK)).T, NEG))
    x_h = xs_ref[:, pl.ds(HD * h, HD)]
    dt_h = _pick(dt, h)
    xdt = x_h * dt_h
    g = cb * lam
    prev = state_ref[pl.ds(HD * h, HD), :]
    row = lax.broadcasted_iota(jnp.int32, (CHUNK, 1), 0)
    cl = jnp.sum(jnp.where(row == CHUNK - 1, cs_h, 0.0), axis=0, keepdims=True)
    f = jnp.exp(cl - cs_h)
    return cs_h, lam, x_h, dt_h, xdt, g, prev, cl, f


def _ssd_fwd(act, dtw, bias, alog, dsk):
    nc = T // CHUNK

    def body(xs_ref, b_ref, c_ref, dt_ref, bias_ref, alog_ref, dsk_ref, y_ref, st_ref, state):
        @pl.when(pl.program_id(0) == 0)
        def _():
            state[...] = jnp.zeros_like(state)
        st_ref[...] = state[...]
        li, tri, dtp, dt, A, cs, Bm, Cm, cb = _ssd_common(dt_ref, bias_ref, alog_ref, b_ref, c_ref)
        dskv = dsk_ref[...]
        for h in range(HEADS):
            cs_h, lam, x_h, dt_h, xdt, g, prev, cl, f = _ssd_head(h, li, tri, dt, cs, cb, xs_ref, state)
            y = _nn(g.astype(BF16), xdt.astype(BF16))
            y = y + _nt(Cm, prev.astype(BF16)) * jnp.exp(cs_h) + _pick(dskv, h) * x_h
            y_ref[:, pl.ds(HD * h, HD)] = y
            state[pl.ds(HD * h, HD), :] = prev * jnp.exp(cl) + _tn((xdt * f).astype(BF16), Bm)

    vec = pl.BlockSpec((1, DT_PAD), lambda c: (0, 0))
    return pl.pallas_call(
        body, name="ssd_fwd", grid=(nc,),
        in_specs=[pl.BlockSpec((CHUNK, AW), lambda c: (c, 0)), pl.BlockSpec((CHUNK, NS), lambda c: (c, 4)),
                  pl.BlockSpec((CHUNK, NS), lambda c: (c, 5)), pl.BlockSpec((CHUNK, DT_PAD), lambda c: (c, 0)),
                  vec, vec, vec],
        out_specs=[pl.BlockSpec((CHUNK, AW), lambda c: (c, 0)), pl.BlockSpec((None, AW, NS), lambda c: (c, 0, 0))],
        out_shape=[jax.ShapeDtypeStruct((T, AW), F32), jax.ShapeDtypeStruct((nc, AW, NS), F32)],
        scratch_shapes=[pltpu.VMEM((AW, NS), F32)],
        compiler_params=_cparams(("arbitrary",)),
    )(act, act, act, dtw, bias, alog, dsk)


def _ssd_bwd(act, dtw, bias, alog, dsk, states, dy):
    nc = T // CHUNK

    def body(xs_ref, b_ref, c_ref, dt_ref, bias_ref, alog_ref, dsk_ref, st_ref, dy_ref,
             dxs_ref, db_ref, dc_ref, ddt_ref, par_ref, dstate):
        step = pl.program_id(0)

        @pl.when(step == 0)
        def _():
            dstate[...] = jnp.zeros_like(dstate)
            par_ref[...] = jnp.zeros_like(par_ref)
        li, tri, dtp, dt, A, cs, Bm, Cm, cb = _ssd_common(dt_ref, bias_ref, alog_ref, b_ref, c_ref)
        dskv = dsk_ref[...]
        lane = lax.broadcasted_iota(jnp.int32, (1, DT_PAD), 1)
        row = lax.broadcasted_iota(jnp.int32, (CHUNK, 1), 0)
        dcb = jnp.zeros((CHUNK, CHUNK), F32)
        dB = jnp.zeros((CHUNK, NS), F32)
        dC = jnp.zeros((CHUNK, NS), F32)
        dcs_mat = jnp.zeros((CHUNK, DT_PAD), F32)
        ddt_mat = jnp.zeros((CHUNK, DT_PAD), F32)
        dD = jnp.zeros((1, DT_PAD), F32)
        for h in range(HEADS):
            cs_h, lam, x_h, dt_h, xdt, g, prev, cl, f = _ssd_head(h, li, tri, dt, cs, cb, xs_ref, st_ref)
            sl = pl.ds(HD * h, HD)
            dyh = dy_ref[:, sl]
            dyb = dyh.astype(BF16)
            dnew = dstate[sl, :]
            prevb = prev.astype(BF16)
            E = jnp.exp(cs_h)
            ecl = jnp.exp(cl)
            xdtb = xdt.astype(BF16)
            dD = dD + jnp.where(lane == h, jnp.sum(dyh * x_h, keepdims=True), 0.0)
            dG = _nt(dyb, xdtb)
            dxdt = _tn(g.astype(BF16), dyb)
            dcb = dcb + dG * lam
            Mm = dG * g
            dcs = jnp.sum(Mm, axis=1, keepdims=True) - jnp.sum(Mm.T, axis=1, keepdims=True)
            Yo = _nt(Cm, prevb)
            dYo = (dyh * E).astype(BF16)
            dC = dC + _nn(dYo, prevb)
            dprev = _tn(dYo, Cm) + dnew * ecl
            dcs = dcs + jnp.sum(dyh * Yo, axis=1, keepdims=True) * E
            dcl = jnp.sum(dnew * prev, keepdims=True) * ecl
            dnewb = dnew.astype(BF16)
            W = _nt(Bm, dnewb)
            dB = dB + _nn((xdt * f).astype(BF16), dnewb)
            dxdt = dxdt + W * f
            dF = jnp.sum(W * xdt, axis=1, keepdims=True) * f
            dcs = dcs - dF
            dcl = dcl + jnp.sum(dF, keepdims=True)
            dcs = dcs + jnp.where(row == CHUNK - 1, dcl, 0.0)
            dcs_mat = dcs_mat + jnp.where(lane == h, dcs, 0.0)
            ddt_mat = ddt_mat + jnp.where(lane == h, jnp.sum(dxdt * x_h, axis=1, keepdims=True), 0.0)
            dxs_ref[:, sl] = _pick(dskv, h) * dyh + dxdt * dt_h
            dstate[sl, :] = dprev
        da = jnp.dot((li <= lax.broadcasted_iota(jnp.int32, (CHUNK, CHUNK), 1)).astype(F32), dcs_mat,
                     precision=HIGHEST, preferred_element_type=F32)
        ddtp = jnp.where(lane < HEADS, (ddt_mat + da * A) * _sigmoid(dtp), 0.0)
        ddt_ref[...] = ddtp
        dcbb = dcb.astype(BF16)
        dc_ref[...] = dC + _nn(dcbb, Bm)
        db_ref[...] = dB + _tn(dcbb, Cm)
        dalog = jnp.where(lane < HEADS, jnp.sum(da * dt, axis=0, keepdims=True) * A, 0.0)
        par_ref[...] += _rows_to_block([jnp.sum(ddtp, axis=0, keepdims=True), dalog, dD], 8, DT_PAD)

    vec = pl.BlockSpec((1, DT_PAD), lambda c: (0, 0))
    rev = lambda c: nc - 1 - c
    return pl.pallas_call(
        body, name="ssd_bwd", grid=(nc,),
        in_specs=[pl.BlockSpec((CHUNK, AW), lambda c: (rev(c), 0)), pl.BlockSpec((CHUNK, NS), lambda c: (rev(c), 4)),
                  pl.BlockSpec((CHUNK, NS), lambda c: (rev(c), 5)), pl.BlockSpec((CHUNK, DT_PAD), lambda c: (rev(c), 0)),
                  vec, vec, vec,
                  pl.BlockSpec((None, AW, NS), lambda c: (rev(c), 0, 0)), pl.BlockSpec((CHUNK, AW), lambda c: (rev(c), 0))],
        out_specs=[pl.BlockSpec((CHUNK, AW), lambda c: (rev(c), 0)), pl.BlockSpec((CHUNK, NS), lambda c: (rev(c), 0)),
                   pl.BlockSpec((CHUNK, NS), lambda c: (rev(c), 0)), pl.BlockSpec((CHUNK, DT_PAD), lambda c: (rev(c), 0)),
                   pl.BlockSpec((8, DT_PAD), lambda c: (0, 0))],
        out_shape=[jax.ShapeDtypeStruct((T, AW), F32), jax.ShapeDtypeStruct((T, NS), F32),
                   jax.ShapeDtypeStruct((T, NS), F32), jax.ShapeDtypeStruct((T, DT_PAD), F32),
                   jax.ShapeDtypeStruct((8, DT_PAD), F32)],
        scratch_shapes=[pltpu.VMEM((AW, NS), F32)],
        compiler_params=_cparams(("arbitrary",)),
    )(act, act, act, dtw, bias, alog, dsk, states, dy)


def _place():
    return lax.axis_index("x"), lax.axis_index("y"), lax.axis_index("c")


def _slot(px, py, pc):
    return 4 * px + 2 * py + pc


def _all_gather(arrs, name):
    na = len(arrs)

    def body(*refs):
        ins, outs = refs[:na], refs[na:2 * na]
        send_sems, recv_sems, local_sems = refs[2 * na:]
        x, y, c = _place()
        me, sib = (x, y, c), (x, y, 1 - c)
        chips = [(1 - x, y), (x, 1 - y), (1 - x, 1 - y)]

        def copy(a, kk, block, to, src=None):
            dst = outs[a].at[_slot(*block)]
            return pltpu.make_async_remote_copy(
                src_ref=dst if src is None else src, dst_ref=dst,
                send_sem=send_sems.at[a, kk], recv_sem=recv_sems.at[a, kk], device_id=to, device_id_type=MESH)

        mine = [pltpu.make_async_copy(ins[a], outs[a].at[_slot(*me)], local_sems.at[a]) for a in range(na)]
        for cp in mine:
            cp.start()
        first = []
        for a in range(na):
            first.append(copy(a, 0, me, sib, src=ins[a]))
            first += [copy(a, 1 + j, me, (*chip, c), src=ins[a]) for j, chip in enumerate(chips)]
        for cp in first:
            cp.start()
        passed = []
        for j, chip in enumerate(chips):
            for a in range(na):
                copy(a, 1 + j, (*chip, c), me).wait_recv()
                fw = copy(a, 4 + j, (*chip, c), sib)
                fw.start()
                passed.append(fw)
        for a in range(na):
            copy(a, 0, sib, me).wait_recv()
            for j, chip in enumerate(chips):
                copy(a, 4 + j, (*chip, 1 - c), me).wait_recv()
        for cp in first + passed:
            cp.wait_send()
        for cp in mine:
            cp.wait()

    any_spec = pl.BlockSpec(memory_space=pl.ANY)
    return pl.pallas_call(
        body, name=name,
        in_specs=[any_spec] * na, out_specs=[any_spec] * na,
        out_shape=[jax.ShapeDtypeStruct((N_DEV,) + a.shape, a.dtype) for a in arrs],
        scratch_shapes=[pltpu.SemaphoreType.DMA((na, 7)), pltpu.SemaphoreType.DMA((na, 7)),
                        pltpu.SemaphoreType.DMA((na,))],
    )(*arrs)


def _reduce_scatter(part, name):
    _, r, C = part.shape

    def body(part_ref, out_ref, own, got_sib, got_ici, lsem, s1, r1, s2, r2):
        x, y, c = _place()
        chips = [(x, y), (1 - x, y), (x, 1 - y), (1 - x, 1 - y)]
        loc = [pltpu.make_async_copy(part_ref.at[_slot(*chips[kk], c)], own.at[kk], lsem.at[kk]) for kk in range(4)]
        d2d = [pltpu.make_async_remote_copy(
            src_ref=part_ref.at[_slot(*chips[kk], 1 - c)], dst_ref=got_sib.at[kk],
            send_sem=s1.at[kk], recv_sem=r1.at[kk], device_id=(x, y, 1 - c), device_id_type=MESH) for kk in range(4)]
        for cp in loc + d2d:
            cp.start()
        ici = [pltpu.make_async_remote_copy(
            src_ref=own.at[kk], dst_ref=got_ici.at[kk - 1],
            send_sem=s2.at[kk - 1], recv_sem=r2.at[kk - 1], device_id=(*chips[kk], c), device_id_type=MESH)
            for kk in range(1, 4)]
        for kk in (1, 2, 3, 0):
            loc[kk].wait()
            d2d[kk].wait_recv()
            own[kk] = own[kk] + got_sib[kk]
            if kk:
                ici[kk - 1].start()
        for cp in ici:
            cp.wait_recv()
        out_ref[...] = ((own[0] + got_ici[0]) + got_ici[1]) + got_ici[2]
        for cp in d2d + ici:
            cp.wait_send()

    return pl.pallas_call(
        body, name=name,
        in_specs=[pl.BlockSpec(memory_space=pl.ANY)],
        out_specs=pl.BlockSpec(memory_space=pltpu.VMEM),
        out_shape=jax.ShapeDtypeStruct((r, C), F32),
        scratch_shapes=[pltpu.VMEM((4, r, C), F32), pltpu.VMEM((4, r, C), F32), pltpu.VMEM((3, r, C), F32),
                        pltpu.SemaphoreType.DMA((4,)), pltpu.SemaphoreType.DMA((4,)), pltpu.SemaphoreType.DMA((4,)),
                        pltpu.SemaphoreType.DMA((3,)), pltpu.SemaphoreType.DMA((3,))],
        compiler_params=pltpu.CompilerParams(vmem_limit_bytes=VMEM_LIMIT),
    )(part)


def _all_reduce_small(v, name):
    R, C = v.shape

    def body(v_ref, out_ref, got, send_sems, recv_sems):
        x, y, c = _place()
        mine = _slot(x, y, c)
        copies = []
        for kk in range(1, N_DEV):
            fx, fy, fc = kk >> 2 & 1, kk >> 1 & 1, kk & 1
            peer = (1 - x if fx else x, 1 - y if fy else y, 1 - c if fc else c)
            copies.append(pltpu.make_async_remote_copy(
                src_ref=v_ref, dst_ref=got.at[mine], send_sem=send_sems.at[kk - 1], recv_sem=recv_sems.at[kk - 1],
                device_id=peer, device_id_type=MESH))
        for cp in copies:
            cp.start()
        got[mine] = v_ref[...]
        for cp in copies:
            cp.wait_recv()
        acc = got[0]
        for s in range(1, N_DEV):
            acc = acc + got[s]
        out_ref[...] = acc
        for cp in copies:
            cp.wait_send()

    return pl.pallas_call(
        body, name=name,
        in_specs=[pl.BlockSpec(memory_space=pltpu.VMEM)], out_specs=pl.BlockSpec(memory_space=pltpu.VMEM),
        out_shape=jax.ShapeDtypeStruct((R, C), F32),
        scratch_shapes=[pltpu.VMEM((N_DEV, R, C), F32), pltpu.SemaphoreType.DMA((N_DEV - 1,)),
                        pltpu.SemaphoreType.DMA((N_DEV - 1,))],
    )(v)


def _adamw(w, g, m, v, name):
    R, C = w.shape
    tm = R if R <= 512 else 256

    def fn(w, g, m, v):
        m2 = ADAM_B1 * m + (1.0 - ADAM_B1) * g
        v2 = ADAM_B2 * v + (1.0 - ADAM_B2) * (g * g)
        m_hat = m2 / (1.0 - ADAM_B1 ** ADAM_STEP)
        v_hat = v2 / (1.0 - ADAM_B2 ** ADAM_STEP)
        delta = -ADAM_LR * (m_hat / (jnp.sqrt(v_hat) + ADAM_EPS) + ADAM_WD * w)
        return (delta, m2, v2), ()
    return _rowwise(fn, [w, g, m, v], [], [(C, F32)] * 3, [], tm=tm, name=name)


SMALL = ["norm_mix_pre", "norm_mix_post", "norm_mlp_pre", "norm_mlp_post", "norm_ple_post",
         "conv_b", "ssd_norm_g", "dt_bias", "a_log", "d_skip"]


def _pad_row(v, width=D):
    return jnp.pad(v, ((0, 0), (0, width - v.shape[1])))


def kernel(x, p, positions, norm_mix_pre, norm_mix_post, w_in, conv_w, conv_b, dt_bias, a_log, d_skip, ssd_norm_g, w_out, norm_mlp_pre, norm_mlp_post, w_up, w_down, w_ple_gate, w_ple_proj, norm_ple_post, loss_target, m_norm_mix_pre, m_norm_mix_post, m_w_in, m_conv_w, m_conv_b, m_dt_bias, m_a_log, m_d_skip, m_ssd_norm_g, m_w_out, m_norm_mlp_pre, m_norm_mlp_post, m_w_up, m_w_down, m_w_ple_gate, m_w_ple_proj, m_norm_ple_post, v_norm_mix_pre, v_norm_mix_post, v_w_in, v_conv_w, v_conv_b, v_dt_bias, v_a_log, v_d_skip, v_ssd_norm_g, v_w_out, v_norm_mlp_pre, v_norm_mlp_post, v_w_up, v_w_down, v_w_ple_gate, v_w_ple_proj, v_norm_ple_post):
    args = dict(locals())
    x2, p2, tgt = x[0], p[0, 0], loss_target[0]
    g1, g2, g3, g4, g5 = norm_mix_pre, norm_mix_post, norm_mlp_pre, norm_mlp_post, norm_ple_post

    pack = jnp.concatenate([
        jnp.pad(w_in[0].T, ((0, W_IN_SHARD_PAD - W_IN_SHARD), (0, 0))),
        w_out[0],
        w_up[0].T,
        w_down[0],
        w_ple_gate[0],
        w_ple_proj[0].T.reshape(32, D),
    ], axis=0).astype(BF16)
    conv_pack = jnp.pad(conv_w[0], ((0, 4), (0, 32)))
    gw, gconv = _all_gather([pack, conv_pack], "gather_weights")
    w_inT = gw[:, :W_IN_SHARD].reshape(IN_W, D)
    w_qkvzT = w_inT[:4 * AW]
    w_xbcT = w_inT[4 * AW:4 * AW + CONV_CH]
    w_dtT = jnp.pad(w_inT[4 * AW + CONV_CH:], ((0, DT_PAD - HEADS), (0, 0)))
    w_o = gw[:, 384:512].reshape(D, D)
    w_upT = gw[:, 512:1024].reshape(DFF, D)
    w_dn = gw[:, 1024:1536].reshape(DFF, D)
    w_gate = gw[:, 1536:1664].reshape(D, D)
    w_projT = gw[:, 1664:1696].reshape(D, PLE)
    conv_full = gconv[:, :CONV_K, :96].transpose(1, 0, 2).reshape(CONV_K, CONV_CH)

    inv_freq = ROPE_THETA ** (-jnp.arange(HD // 2, dtype=F32) * 2.0 / HD)
    ang = positions[0].astype(F32)[:, None] * inv_freq
    cos, sin = jnp.cos(ang), jnp.sin(ang)
    cosf = jnp.tile(jnp.concatenate([cos, cos], axis=1), (1, HEADS))
    sins = jnp.tile(jnp.concatenate([-sin, sin], axis=1), (1, HEADS))

    bias_w, alog_w, dsk_w = _pad_row(dt_bias, DT_PAD), _pad_row(a_log, DT_PAD), _pad_row(d_skip, DT_PAD)
    rms_pre = lambda a, r, g: a * r * g

    (r1,) = _rowwise(lambda a: ((_rstd(a),), ()), [x2], [], [(1, F32)], [], tm=512, name="rstd_x")
    qkvz = _mm(x2, w_qkvzT, tb=True, tm=512, tn=1024, tk=1024, a_pre=rms_pre, a_rows=[r1], a_cols=[g1], name="proj_qkvz")
    xbc = _mm(x2, w_xbcT, tb=True, tm=512, tn=768, tk=1024, a_pre=rms_pre, a_rows=[r1], a_cols=[g1], name="proj_xbc")
    dtw = _mm(x2, w_dtT, tb=True, tm=512, tn=128, tk=1024, a_pre=rms_pre, a_rows=[r1], a_cols=[g1], name="proj_dt")

    qr, kr, vb = _rope_fwd(qkvz, cosf, sins)
    outs, lses = [], []
    for d in DILATIONS:
        L = T // d
        o, l = _attn_fwd(qr.reshape(L, d * AW), kr.reshape(L, d * AW), vb.reshape(L, d * AW), d)
        outs.append(o.reshape(T, AW))
        lses.append(l.reshape(T, AW))
    attn, lse = _attn_merge(outs, lses)

    act = _conv_fwd(xbc, conv_full, conv_b)
    y_ssd, states = _ssd_fwd(act, dtw, bias_w, alog_w, dsk_w)

    def gated_fwd(y, z, a, gs):
        gi = y * (z * _sigmoid(z))
        return (jnp.concatenate([a, gi * _rstd(gi) * gs], axis=1),), ()
    (cat,) = _rowwise(gated_fwd, [y_ssd, (qkvz, AW, 3), attn], [ssd_norm_g], [(D, F32)], [], tm=512, name="gated_norm")

    mix = _mm(cat, w_o, tm=512, tn=1024, tk=1024, name="mix_out")

    def post1(xx, mm, ga, gb):
        h = xx + mm * _rstd(mm) * ga
        return (h, _rstd(h)), ()
    h1, r3 = _rowwise(post1, [x2, mix], [g2, g3], [(D, F32), (1, F32)], [], tm=512, name="post_mix")

    a_up = _mm(h1, w_upT, tb=True, tm=512, tn=1024, tk=1024, a_pre=rms_pre, a_rows=[r3], a_cols=[g3], name="mlp_up")
    relu2 = lambda a: jnp.square(jnp.maximum(a, 0.0))
    ff = _mm(a_up, w_dn, tm=512, tn=1024, tk=1024, a_pre=relu2, name="mlp_down")
    (h2,) = _rowwise(lambda hh, f, g: ((hh + f * _rstd(f) * g,), ()), [h1, ff], [g4], [(D, F32)], [], tm=512, name="post_mlp")

    gp = _mm(h2, w_gate, tm=512, tn=1024, tk=1024, name="ple_gate")
    pp = _mm(p2, w_projT, tb=True, tm=512, tn=1024, tk=256, name="ple_proj")

    def final(hh, gpre, ppv, tg, g):
        sg = _sigmoid(gpre)
        ple = ppv * sg
        r = _rstd(ple)
        n = ple * r
        h3 = hh + n * g
        e = h3 - tg
        dh3 = e * (1.0 / D)
        dple = _rms_bwd(n, r, g, dh3)
        return (dh3, dple * sg, dple * ppv * sg * (1.0 - sg)), (_colsum(dh3 * n), _colsum(0.5 * e * e * (1.0 / D)))
    dh3, dpp, dgp, dg5, loss_vec = _rowwise(final, [h2, gp, pp, tgt], [g5], [(D, F32)] * 3, [(1, D), (1, D)],
                                            tm=256, name="loss_ple_bwd")

    gw_projT = _mm(dpp, p2, ta=True, tm=512, tn=256, tk=1024, name="gw_ple_proj")
    gw_gate = _mm(h2, dgp, ta=True, tm=512, tn=1024, tk=1024, name="gw_ple_gate")
    dh2_g = _mm(dgp, w_gate, tb=True, tm=512, tn=1024, tk=1024, name="dx_ple_gate")

    def bwd_mlp_post(d3, dg_, f, g):
        dh2 = d3 + dg_
        r = _rstd(f)
        n = f * r
        return (dh2, _rms_bwd(n, r, g, dh2)), (_colsum(dh2 * n),)
    dh2, dff, dg4 = _rowwise(bwd_mlp_post, [dh3, dh2_g, ff], [g4], [(D, F32)] * 2, [(1, D)], tm=256, name="bwd_post_mlp")

    gw_dn = _mm(a_up, dff, ta=True, tm=1024, tn=1024, tk=512, a_pre=relu2, name="gw_mlp_down")
    da_up = _mm(dff, w_dn, tb=True, tm=512, tn=1024, tk=1024, epi=lambda acc, a: (acc * (2.0 * jnp.maximum(a, 0.0)),),
                epi_tiles=[a_up], name="dx_mlp_down")
    gw_upT = _mm(da_up, h1, ta=True, tm=1024, tn=1024, tk=512, b_pre=rms_pre, b_rows=[r3], b_cols=[g3], name="gw_mlp_up")
    du2 = _mm(da_up, w_upT, tm=512, tn=1024, tk=1024, name="dx_mlp_up")

    def bwd_mix_post(d2, du, hh, rr, mm, ga, gb):
        n3 = hh * rr
        dh1 = d2 + _rms_bwd(n3, rr, gb, du)
        r = _rstd(mm)
        n2 = mm * r
        return (dh1, _rms_bwd(n2, r, ga, dh1)), (_colsum(du * n3), _colsum(dh1 * n2))
    dh1, dmix, dg3, dg2 = _rowwise(bwd_mix_post, [dh2, du2, h1, r3, mix], [g2, g3], [(D, F32)] * 2, [(1, D), (1, D)],
                                   tm=256, name="bwd_post_mix")

    gw_o = _mm(cat, dmix, ta=True, tm=512, tn=1024, tk=1024, name="gw_out")
    dcat = _mm(dmix, w_o, tb=True, tm=512, tn=1024, tk=1024, name="dx_out")

    def gated_bwd(y, z, dyn, gs):
        sg = _sigmoid(z)
        sz = z * sg
        gi = y * sz
        r = _rstd(gi)
        n = gi * r
        dgi = _rms_bwd(n, r, gs, dyn)
        return (dgi * sz, dgi * y * (sg * (1.0 + z * (1.0 - sg)))), (_colsum(dyn * n),)
    dy_ssd, dz, dgs = _rowwise(gated_bwd, [y_ssd, (qkvz, AW, 3), (dcat, AW, 1)], [ssd_norm_g], [(AW, F32)] * 2, [(1, AW)],
                               tm=512, name="bwd_gated_norm")

    dxs, dBm, dCm, ddtw, ssd_par = _ssd_bwd(act, dtw, bias_w, alog_w, dsk_w, states, dy_ssd)
    dact = jnp.concatenate([dxs, dBm, dCm], axis=1)
    dxbc, conv_par = _conv_bwd(xbc, dact, conv_full, conv_b)

    dattn = dcat[:, :AW]
    dqs, dks, dvs = [], [], []
    for d in DILATIONS:
        L = T // d
        rs = lambda t: t.reshape(L, d * AW)
        dq, dk, dv = _attn_bwd(rs(qr), rs(kr), rs(vb), rs(dattn), rs(attn), rs(lse), d)
        dqs.append(dq.reshape(T, AW))
        dks.append(dk.reshape(T, AW))
        dvs.append(dv.reshape(T, AW))
    dqkvz = _rope_bwd(dqs, dks, dvs, dz, cosf, sins)

    du1a = _mm(dqkvz, w_qkvzT, tm=512, tn=1024, tk=1024, name="dx_qkvz")
    du1b = _mm(dxbc, w_xbcT, tm=512, tn=1024, tk=768, name="dx_xbc")
    du1c = _mm(ddtw, w_dtT, tm=512, tn=1024, tk=128, name="dx_dt")
    gw_qkvzT = _mm(dqkvz, x2, ta=True, tm=1024, tn=1024, tk=512, b_pre=rms_pre, b_rows=[r1], b_cols=[g1], name="gw_qkvz")
    gw_xbcT = _mm(dxbc, x2, ta=True, tm=768, tn=1024, tk=512, b_pre=rms_pre, b_rows=[r1], b_cols=[g1], name="gw_xbc")
    gw_dtT = _mm(ddtw, x2, ta=True, tm=128, tn=1024, tk=512, b_pre=rms_pre, b_rows=[r1], b_cols=[g1], name="gw_dt")

    def bwd_in(d1, ua, ub, uc, xx, rr, g):
        n = xx * rr
        du = ua + ub + uc
        return (d1 + _rms_bwd(n, rr, g, du),), (_colsum(du * n),)
    grad_x, dg1 = _rowwise(bwd_in, [dh1, du1a, du1b, du1c, x2, r1], [g1], [(D, F32)], [(1, D)], tm=256, name="bwd_pre_mix")

    gw_inT = jnp.concatenate([gw_qkvzT, gw_xbcT, gw_dtT[:HEADS]], axis=0)
    gw_inT = jnp.pad(gw_inT.reshape(N_DEV, W_IN_SHARD, D), ((0, 0), (0, W_IN_SHARD_PAD - W_IN_SHARD), (0, 0)))
    g_inT = _reduce_scatter(gw_inT, "rs_w_in")
    g_out = _reduce_scatter(gw_o.reshape(N_DEV, 128, D), "rs_w_out")
    g_upT = _reduce_scatter(gw_upT.reshape(N_DEV, 512, D), "rs_w_up")
    g_dn = _reduce_scatter(gw_dn.reshape(N_DEV, 512, D), "rs_w_down")
    g_gate = _reduce_scatter(gw_gate.reshape(N_DEV, 128, D), "rs_w_gate")
    g_projT = _reduce_scatter(gw_projT.reshape(N_DEV, 32, D), "rs_w_proj")

    small = jnp.concatenate([
        dg1, dg2, dg3, dg4, dg5,
        _pad_row(conv_par[4:5]), _pad_row(dgs), _pad_row(ssd_par[0:1]), _pad_row(ssd_par[1:2]), _pad_row(ssd_par[2:3]),
        _pad_row(conv_par[0:4]), loss_vec, jnp.zeros((1, D), F32),
    ], axis=0)
    small = _all_reduce_small(small, "reduce_small")
    loss = jnp.sum(small[14])
    me = lax.axis_index("x") * 4 + lax.axis_index("y") * 2 + lax.axis_index("c")
    g_conv_w = lax.dynamic_slice(small[10:14, :CONV_CH], (0, me * 96), (CONV_K, 96))

    grads = {
        "w_in": g_inT[:W_IN_SHARD].T[None], "w_out": g_out[None], "w_up": g_upT.T[None], "w_down": g_dn[None],
        "w_ple_gate": g_gate[None], "w_ple_proj": g_projT.reshape(128, PLE).T[None], "conv_w": g_conv_w[None],
        "norm_mix_pre": small[0:1], "norm_mix_post": small[1:2], "norm_mlp_pre": small[2:3], "norm_mlp_post": small[3:4],
        "norm_ple_post": small[4:5], "conv_b": small[5:6, :CONV_CH], "ssd_norm_g": small[6:7, :AW],
        "dt_bias": small[7:8, :HEADS], "a_log": small[8:9, :HEADS], "d_skip": small[9:10, :HEADS],
    }
    delta, new_m, new_v = {}, {}, {}
    for nme in ["w_in", "w_out", "w_up", "w_down", "w_ple_gate", "w_ple_proj"]:
        dl, mm_, vv_ = _adamw(args[nme][0], grads[nme][0], args["m_" + nme][0], args["v_" + nme][0], "adamw_" + nme)
        delta[nme], new_m[nme], new_v[nme] = dl[None], mm_[None], vv_[None]

    def pack_small(prefix):
        rows = [_pad_row(args[prefix + nme]) for nme in SMALL]
        rows.append(_pad_row(args[prefix + "conv_w"][0]))
        rows.append(jnp.zeros((2, D), F32))
        return jnp.concatenate(rows, axis=0)
    g_small = jnp.concatenate([small[0:10], _pad_row(g_conv_w), jnp.zeros((2, D), F32)], axis=0)
    dl, mm_, vv_ = _adamw(pack_small(""), g_small, pack_small("m_"), pack_small("v_"), "adamw_small")
    for i, nme in enumerate(SMALL):
        wdt = args[nme].shape[1]
        delta[nme], new_m[nme], new_v[nme] = dl[i:i + 1, :wdt], mm_[i:i + 1, :wdt], vv_[i:i + 1, :wdt]
    delta["conv_w"], new_m["conv_w"], new_v["conv_w"] = dl[None, 10:14, :96], mm_[None, 10:14, :96], vv_[None, 10:14, :96]

    order = ["norm_mix_pre", "norm_mix_post", "w_in", "conv_w", "conv_b", "dt_bias", "a_log", "d_skip", "ssd_norm_g",
             "w_out", "norm_mlp_pre", "norm_mlp_post", "w_up", "w_down", "w_ple_gate", "w_ple_proj", "norm_ple_post"]
    return (loss, grad_x[None], *[grads[n] for n in order], *[delta[n] for n in order],
            *[new_m[n] for n in order], *[new_v[n] for n in order])
```

```python
import functools
import math

import jax
import jax.numpy as jnp
from jax import lax
from jax.experimental import pallas as pl
from jax.experimental.pallas import tpu as pltpu

F32 = jnp.float32
BF16 = jnp.bfloat16
MESH = pl.DeviceIdType.MESH
HIGHEST = lax.Precision.HIGHEST

N_DEV = 8
T = 4096
D = 1024
HEADS = 8
HD = 64
AW = 512
NS = 128
CONV_K = 4
CONV_CH = 768
CHUNK = 128
DFF = 4096
PLE = 256
EPS = 1e-6
ROPE_THETA = 10000.0
DILATIONS = (1, 4, 16)
QBLK = 128
NEG = -1e30
IN_W = 2824
W_IN_SHARD = 353
W_IN_SHARD_PAD = 384
DT_PAD = 128

ADAM_LR, ADAM_B1, ADAM_B2, ADAM_EPS, ADAM_WD, ADAM_STEP = 0.001, 0.9, 0.999, 1e-08, 0.01, 10

VMEM_LIMIT = 56 * 1024 * 1024


def _cparams(sem=None):
    return pltpu.CompilerParams(dimension_semantics=sem, vmem_limit_bytes=VMEM_LIMIT)


def _dot(a, b, ca, cb, precision=None):
    return lax.dot_general(a, b, (((ca,), (cb,)), ((), ())), preferred_element_type=F32, precision=precision)


def _nn(a, b):
    return _dot(a, b, 1, 0)


def _nt(a, b):
    return _dot(a, b, 1, 1)


def _tn(a, b):
    return _dot(a, b, 0, 0)


def _sigmoid(x):
    return 1.0 / (1.0 + jnp.exp(-x))


def _softplus(x):
    return jnp.maximum(x, 0.0) + jnp.log(1.0 + jnp.exp(-jnp.abs(x)))


def _mm(a, b, *, ta=False, tb=False, tm, tn, tk, name,
        a_pre=None, a_rows=(), a_cols=(), b_pre=None, b_rows=(), b_cols=(),
        epi=None, epi_tiles=(), out_dtypes=(F32,)):
    if ta:
        K, M = a.shape
    else:
        M, K = a.shape
    if tb:
        N, K2 = b.shape
    else:
        K2, N = b.shape
    assert K == K2 and M % tm == 0 and N % tn == 0 and K % tk == 0, (name, a.shape, b.shape)
    nk = K // tk
    if ta:
        a_spec = pl.BlockSpec((tk, tm), lambda i, j, k: (k, i))
        a_row_specs = [pl.BlockSpec((tk, 1), lambda i, j, k: (k, 0)) for _ in a_rows]
        a_col_specs = [pl.BlockSpec((1, tm), lambda i, j, k: (0, i)) for _ in a_cols]
    else:
        a_spec = pl.BlockSpec((tm, tk), lambda i, j, k: (i, k))
        a_row_specs = [pl.BlockSpec((tm, 1), lambda i, j, k: (i, 0)) for _ in a_rows]
        a_col_specs = [pl.BlockSpec((1, tk), lambda i, j, k: (0, k)) for _ in a_cols]
    if tb:
        b_spec = pl.BlockSpec((tn, tk), lambda i, j, k: (j, k))
        b_row_specs = [pl.BlockSpec((tn, 1), lambda i, j, k: (j, 0)) for _ in b_rows]
        b_col_specs = [pl.BlockSpec((1, tk), lambda i, j, k: (0, k)) for _ in b_cols]
    else:
        b_spec = pl.BlockSpec((tk, tn), lambda i, j, k: (k, j))
        b_row_specs = [pl.BlockSpec((tk, 1), lambda i, j, k: (k, 0)) for _ in b_rows]
        b_col_specs = [pl.BlockSpec((1, tn), lambda i, j, k: (0, j)) for _ in b_cols]
    o_spec = pl.BlockSpec((tm, tn), lambda i, j, k: (i, j))
    na, nb, ne, no = len(a_rows) + len(a_cols), len(b_rows) + len(b_cols), len(epi_tiles), len(out_dtypes)

    def body(*refs):
        a_ref, b_ref = refs[0], refs[1]
        a_ex = refs[2:2 + na]
        b_ex = refs[2 + na:2 + na + nb]
        e_ex = refs[2 + na + nb:2 + na + nb + ne]
        outs = refs[2 + na + nb + ne:2 + na + nb + ne + no]
        acc = refs[-1]
        k = pl.program_id(2)

        @pl.when(k == 0)
        def _():
            acc[...] = jnp.zeros_like(acc)

        at = a_ref[...]
        if a_pre is not None:
            at = a_pre(at, *[r[...] for r in a_ex])
        bt = b_ref[...]
        if b_pre is not None:
            bt = b_pre(bt, *[r[...] for r in b_ex])
        acc[...] += _dot(at.astype(BF16), bt.astype(BF16), 0 if ta else 1, 1 if tb else 0)

        @pl.when(k == nk - 1)
        def _():
            res = acc[...]
            vals = epi(res, *[r[...] for r in e_ex]) if epi is not None else (res,)
            for o_ref, val in zip(outs, vals):
                o_ref[...] = val.astype(o_ref.dtype)

    outs = pl.pallas_call(
        body, name=name,
        grid=(M // tm, N // tn, nk),
        in_specs=[a_spec, b_spec] + a_row_specs + a_col_specs + b_row_specs + b_col_specs + [o_spec] * ne,
        out_specs=[o_spec] * no,
        out_shape=[jax.ShapeDtypeStruct((M, N), dt) for dt in out_dtypes],
        scratch_shapes=[pltpu.VMEM((tm, tn), F32)],
        compiler_params=_cparams(("parallel", "parallel", "arbitrary")),
    )(a, b, *a_rows, *a_cols, *b_rows, *b_cols, *epi_tiles)
    return outs[0] if no == 1 else outs


def _rowwise(fn, rows, vecs, out_rows, out_sums, *, tm, name):
    specs, arrs = [], []
    R = None
    for r in rows:
        if isinstance(r, tuple):
            arr, width, cb = r
            specs.append(pl.BlockSpec((tm, width), lambda i, cb=cb: (i, cb)))
        else:
            arr = r
            specs.append(pl.BlockSpec((tm, arr.shape[1]), lambda i: (i, 0)))
        R = arr.shape[0] if R is None else R
        assert arr.shape[0] == R, name
        arrs.append(arr)
    assert R % tm == 0, name
    for v in vecs:
        specs.append(pl.BlockSpec(v.shape, lambda i: (0, 0)))
        arrs.append(v)
    nr, nv, no, ns = len(rows), len(vecs), len(out_rows), len(out_sums)
    out_specs = [pl.BlockSpec((tm, w), lambda i: (i, 0)) for w, _ in out_rows]
    out_specs += [pl.BlockSpec(s, lambda i: (0, 0)) for s in out_sums]
    out_shape = [jax.ShapeDtypeStruct((R, w), dt) for w, dt in out_rows]
    out_shape += [jax.ShapeDtypeStruct(s, F32) for s in out_sums]

    def body(*refs):
        ins = [r[...] for r in refs[:nr + nv]]
        o_refs = refs[nr + nv:nr + nv + no]
        s_refs = refs[nr + nv + no:]
        o_vals, s_vals = fn(*ins)
        for ref, val in zip(o_refs, o_vals):
            ref[...] = val.astype(ref.dtype)
        if ns:
            @pl.when(pl.program_id(0) == 0)
            def _():
                for ref in s_refs:
                    ref[...] = jnp.zeros_like(ref)
            for ref, val in zip(s_refs, s_vals):
                ref[...] += val

    outs = pl.pallas_call(
        body, name=name, grid=(R // tm,), in_specs=specs, out_specs=out_specs, out_shape=out_shape,
        compiler_params=_cparams(("arbitrary",) if ns else ("parallel",)),
    )(*arrs)
    return outs


def _colsum(x):
    return jnp.sum(x, axis=0, keepdims=True)


def _rstd(x):
    return lax.rsqrt(jnp.mean(x * x, axis=-1, keepdims=True) + EPS)


def _rms_bwd(xn, r, g, dy):
    dn = dy * g
    return r * (dn - xn * jnp.mean(dn * xn, axis=-1, keepdims=True))


def _partner(t):
    parts = []
    for s in range(t.shape[1] // 128):
        ts = t[:, 128 * s:128 * (s + 1)]
        lane = lax.broadcasted_iota(jnp.int32, ts.shape, 1)
        up = pltpu.roll(ts, 96, 1)
        down = pltpu.roll(ts, 32, 1)
        parts.append(jnp.where((lane % 64) < 32, up, down))
    return jnp.concatenate(parts, axis=1)


def _rope_fwd(qkvz, cosf, sins):
    def fn(q, k, v, c, s):
        qr = (q * c + _partner(q) * s) * (HD ** -0.5)
        kr = k * c + _partner(k) * s
        return (qr, kr, v), ()
    return _rowwise(fn, [(qkvz, AW, 0), (qkvz, AW, 1), (qkvz, AW, 2), cosf, sins], [],
                    [(AW, BF16), (AW, BF16), (AW, BF16)], [], tm=512, name="rope_fwd")


def _rope_bwd(dqs, dks, dvs, dz, cosf, sins):
    def fn(q1, q2, q3, k1, k2, k3, v1, v2, v3, z, c, s):
        dqr = (q1 + q2 + q3) * (HD ** -0.5)
        dkr = k1 + k2 + k3
        dq = dqr * c + _partner(dqr * s)
        dk = dkr * c + _partner(dkr * s)
        return (jnp.concatenate([dq, dk, v1 + v2 + v3, z], axis=1),), ()
    return _rowwise(fn, [*dqs, *dks, *dvs, dz, cosf, sins], [], [(4 * AW, F32)], [], tm=256, name="rope_bwd")[0]


def _band_masks():
    qi = lax.broadcasted_iota(jnp.int32, (QBLK, QBLK), 0)
    kj = lax.broadcasted_iota(jnp.int32, (QBLK, QBLK), 1)
    return kj >= qi, kj <= qi


def _attn_fwd(q, k, v, d):
    L = q.shape[0]
    nb = L // QBLK

    def body(q_ref, kp_ref, kc_ref, vp_ref, vc_ref, o_ref, l_ref):
        n = pl.program_id(1)
        mask_p, mask_c = _band_masks()
        bias = jnp.concatenate([jnp.where(mask_p, 0.0, NEG) + jnp.where(n > 0, 0.0, NEG),
                                jnp.where(mask_c, 0.0, NEG)], axis=1)
        s = []
        for h in range(HEADS):
            sl = pl.ds(HD * h, HD)
            qh = q_ref[:, sl]
            s.append(jnp.concatenate([_nt(qh, kp_ref[:, sl]), _nt(qh, kc_ref[:, sl])], axis=1))
        s = jnp.stack(s) + bias
        m = jnp.max(s, axis=2, keepdims=True)
        e = jnp.exp(s - m)
        den = jnp.sum(e, axis=2, keepdims=True)
        p = (e * (1.0 / den)).astype(BF16)
        lse = m + jnp.log(den)
        for h in range(HEADS):
            sl = pl.ds(HD * h, HD)
            o_ref[:, sl] = _nn(p[h, :, :QBLK], vp_ref[:, sl]) + _nn(p[h, :, QBLK:], vc_ref[:, sl])
            l_ref[:, sl] = jnp.broadcast_to(lse[h], (QBLK, HD))

    cur = pl.BlockSpec((QBLK, AW), lambda r, n: (n, r))
    prev = pl.BlockSpec((QBLK, AW), lambda r, n: (jnp.maximum(n - 1, 0), r))
    return pl.pallas_call(
        body, name=f"attn_fwd_d{d}", grid=(d, nb),
        in_specs=[cur, prev, cur, prev, cur], out_specs=[cur, cur],
        out_shape=[jax.ShapeDtypeStruct((L, d * AW), F32)] * 2,
        compiler_params=_cparams(("parallel", "parallel")),
    )(q, k, k, v, v)


def _attn_bwd(q, k, v, do, at, lse, d):
    L = q.shape[0]
    nb = L // QBLK

    def body(q0_ref, q1_ref, kp_ref, kc_ref, vp_ref, vc_ref, do0_ref, do1_ref, at0_ref, at1_ref,
             l0_ref, l1_ref, dq_ref, dk_ref, dv_ref):
        n = pl.program_id(1)
        mask_p, mask_c = _band_masks()
        prev_bias = jnp.where(mask_p, 0.0, NEG)
        bias = jnp.concatenate([prev_bias + jnp.where(n > 0, 0.0, NEG), jnp.where(mask_c, 0.0, NEG),
                                prev_bias + jnp.where(n < nb - 1, 0.0, NEG)], axis=1)
        s, dp, ls, dl, ops = [], [], [], [], []
        for h in range(HEADS):
            sl = pl.ds(HD * h, HD)
            one = pl.ds(HD * h, 1)
            q0, q1 = q0_ref[:, sl], q1_ref[:, sl]
            kp, kc, vp, vc = kp_ref[:, sl], kc_ref[:, sl], vp_ref[:, sl], vc_ref[:, sl]
            do0, do1 = do0_ref[:, sl], do1_ref[:, sl]
            do0b, do1b = do0.astype(BF16), do1.astype(BF16)
            s.append(jnp.concatenate([_nt(q0, kp), _nt(q0, kc), _nt(q1, kc)], axis=1))
            dp.append(jnp.concatenate([_nt(do0b, vp), _nt(do0b, vc), _nt(do1b, vc)], axis=1))
            dl0 = jnp.sum(do0 * at0_ref[:, sl], axis=1, keepdims=True)
            dl1 = jnp.sum(do1 * at1_ref[:, sl], axis=1, keepdims=True)
            dl.append(jnp.concatenate([jnp.broadcast_to(dl0, (QBLK, 2 * QBLK)), jnp.broadcast_to(dl1, (QBLK, QBLK))], axis=1))
            ls.append(jnp.concatenate([jnp.broadcast_to(l0_ref[:, one], (QBLK, 2 * QBLK)),
                                       jnp.broadcast_to(l1_ref[:, one], (QBLK, QBLK))], axis=1))
            ops.append((q0, q1, kp, kc, do0b, do1b))
        p = jnp.exp(jnp.stack(s) + bias - jnp.stack(ls))
        ds = (p * (jnp.stack(dp) - jnp.stack(dl))).astype(BF16)
        p = p.astype(BF16)
        for h in range(HEADS):
            sl = pl.ds(HD * h, HD)
            q0, q1, kp, kc, do0b, do1b = ops[h]
            dq_ref[:, sl] = _nn(ds[h, :, :QBLK], kp) + _nn(ds[h, :, QBLK:2 * QBLK], kc)
            dv_ref[:, sl] = _tn(p[h, :, QBLK:2 * QBLK], do0b) + _tn(p[h, :, 2 * QBLK:], do1b)
            dk_ref[:, sl] = _tn(ds[h, :, QBLK:2 * QBLK], q0) + _tn(ds[h, :, 2 * QBLK:], q1)

    cur = pl.BlockSpec((QBLK, AW), lambda r, n: (n, r))
    prev = pl.BlockSpec((QBLK, AW), lambda r, n: (jnp.maximum(n - 1, 0), r))
    nxt = pl.BlockSpec((QBLK, AW), lambda r, n: (jnp.minimum(n + 1, nb - 1), r))
    return pl.pallas_call(
        body, name=f"attn_bwd_d{d}", grid=(d, nb),
        in_specs=[cur, nxt, prev, cur, prev, cur, cur, nxt, cur, nxt, cur, nxt], out_specs=[cur, cur, cur],
        out_shape=[jax.ShapeDtypeStruct((L, d * AW), F32)] * 3,
        compiler_params=_cparams(("parallel", "parallel")),
    )(q, q, k, k, v, v, do, do, at, at, lse, lse)


def _attn_merge(outs, lses):
    def fn(o1, o2, o3, l1, l2, l3):
        m = jnp.maximum(jnp.maximum(l1, l2), l3)
        e1, e2, e3 = jnp.exp(l1 - m), jnp.exp(l2 - m), jnp.exp(l3 - m)
        s = e1 + e2 + e3
        inv = 1.0 / s
        return ((e1 * inv) * o1 + (e2 * inv) * o2 + (e3 * inv) * o3, m + jnp.log(s)), ()
    return _rowwise(fn, [*outs, *lses], [], [(AW, F32), (AW, F32)], [], tm=512, name="attn_merge")


CONV_TM = 512
HALO = 8


def _conv_pre(ext, w, b):
    y = b + w[3] * ext
    for kk in range(1, CONV_K):
        y = y + w[3 - kk] * pltpu.roll(ext, kk, 0)
    return y


def _rows_to_block(rows, n, width):
    ri = lax.broadcasted_iota(jnp.int32, (n, width), 0)
    out = jnp.zeros((n, width), F32)
    for j, r in enumerate(rows):
        out = out + jnp.where(ri == j, r, 0.0)
    return out


def _conv_fwd(xbc, w, b):
    nblk = T // CONV_TM

    def body(x_ref, h_ref, w_ref, b_ref, o_ref):
        i = pl.program_id(0)
        halo = jnp.where(i > 0, h_ref[...], 0.0)
        ext = jnp.concatenate([halo, x_ref[...]], axis=0)
        y = _conv_pre(ext, [w_ref[pl.ds(j, 1), :] for j in range(CONV_K)], b_ref[...])[HALO:]
        o_ref[...] = y * _sigmoid(y)

    return pl.pallas_call(
        body, name="conv_fwd", grid=(nblk,),
        in_specs=[pl.BlockSpec((CONV_TM, CONV_CH), lambda i: (i, 0)),
                  pl.BlockSpec((HALO, CONV_CH), lambda i: (jnp.maximum(i * (CONV_TM // HALO) - 1, 0), 0)),
                  pl.BlockSpec((CONV_K, CONV_CH), lambda i: (0, 0)),
                  pl.BlockSpec((1, CONV_CH), lambda i: (0, 0))],
        out_specs=pl.BlockSpec((CONV_TM, CONV_CH), lambda i: (i, 0)),
        out_shape=jax.ShapeDtypeStruct((T, CONV_CH), F32),
        compiler_params=_cparams(("parallel",)),
    )(xbc, xbc, w, b)


def _conv_bwd(xbc, dact, w, b):
    nblk = T // CONV_TM
    per = CONV_TM // HALO

    def body(x_ref, xb_ref, xa_ref, g_ref, ga_ref, w_ref, b_ref, dx_ref, dw_ref):
        i = pl.program_id(0)
        wv = [w_ref[pl.ds(j, 1), :] for j in range(CONV_K)]
        before = jnp.where(i > 0, xb_ref[...], 0.0)
        last = i == nblk - 1
        after = jnp.where(last, 0.0, xa_ref[...])
        g_after = jnp.where(last, 0.0, ga_ref[...])
        ext = jnp.concatenate([before, x_ref[...], after], axis=0)
        y = _conv_pre(ext, wv, b_ref[...])[HALO:]
        sg = _sigmoid(y)
        dy = jnp.concatenate([g_ref[...], g_after], axis=0) * (sg * (1.0 + y * (1.0 - sg)))
        n = CONV_TM + HALO
        dx = wv[3] * dy
        for kk in range(1, CONV_K):
            dx = dx + wv[3 - kk] * pltpu.roll(dy, n - kk, 0)
        dx_ref[...] = dx[:CONV_TM]
        dyc = dy[:CONV_TM]
        rows = [jnp.sum(dyc * (pltpu.roll(ext, 3 - j, 0) if j < 3 else ext)[HALO:HALO + CONV_TM], axis=0, keepdims=True)
                for j in range(CONV_K)]
        rows.append(jnp.sum(dyc, axis=0, keepdims=True))
        part = _rows_to_block(rows, 8, CONV_CH)

        @pl.when(i == 0)
        def _():
            dw_ref[...] = jnp.zeros_like(dw_ref)
        dw_ref[...] += part

    blk = pl.BlockSpec((CONV_TM, CONV_CH), lambda i: (i, 0))
    hb = pl.BlockSpec((HALO, CONV_CH), lambda i: (jnp.maximum(i * per - 1, 0), 0))
    ha = pl.BlockSpec((HALO, CONV_CH), lambda i: (jnp.minimum((i + 1) * per, T // HALO - 1), 0))
    return pl.pallas_call(
        body, name="conv_bwd", grid=(nblk,),
        in_specs=[blk, hb, ha, blk, ha, pl.BlockSpec((CONV_K, CONV_CH), lambda i: (0, 0)),
                  pl.BlockSpec((1, CONV_CH), lambda i: (0, 0))],
        out_specs=[blk, pl.BlockSpec((8, CONV_CH), lambda i: (0, 0))],
        out_shape=[jax.ShapeDtypeStruct((T, CONV_CH), F32), jax.ShapeDtypeStruct((8, CONV_CH), F32)],
        compiler_params=_cparams(("arbitrary",)),
    )(xbc, xbc, xbc, dact, dact, w, b)


def _pick(mat, h):
    lane = lax.broadcasted_iota(jnp.int32, mat.shape, 1)
    return jnp.sum(jnp.where(lane == h, mat, 0.0), axis=1, keepdims=True)


def _ssd_common(dt_ref, bias_ref, alog_ref, b_ref, c_ref):
    li = lax.broadcasted_iota(jnp.int32, (CHUNK, CHUNK), 0)
    si = lax.broadcasted_iota(jnp.int32, (CHUNK, CHUNK), 1)
    tri = li >= si
    dtp = dt_ref[...] + bias_ref[...]
    dt = _softplus(dtp)
    A = -jnp.exp(alog_ref[...])
    a = dt * A
    cs = jnp.dot(tri.astype(F32), a, precision=HIGHEST, preferred_element_type=F32)
    Bm = b_ref[...].astype(BF16)
    Cm = c_ref[...].astype(BF16)
    cb = _nt(Cm, Bm)
    return li, tri, dtp, dt, A, cs, Bm, Cm, cb


def _ssd_head(h, li, tri, dt, cs, cb, xs_ref, state_ref):
    cs_h = _pick(cs, h)
    lam = jnp.exp(jnp.where(tri, cs_h - jnp.broadcast_to(cs_h, (CHUNK, CHUNK)).T, NEG))
    x_h = xs_ref[:, pl.ds(HD * h, HD)]
    dt_h = _pick(dt, h)
    xdt = x_h * dt_h
    g = cb * lam
    prev = state_ref[pl.ds(HD * h, HD), :]
    row = lax.broadcasted_iota(jnp.int32, (CHUNK, 1), 0)
    cl = jnp.sum(jnp.where(row == CHUNK - 1, cs_h, 0.0), axis=0, keepdims=True)
    f = jnp.exp(cl - cs_h)
    return cs_h, lam, x_h, dt_h, xdt, g, prev, cl, f


def _ssd_fwd(act, dtw, bias, alog, dsk):
    nc = T // CHUNK

    def body(xs_ref, b_ref, c_ref, dt_ref, bias_ref, alog_ref, dsk_ref, y_ref, st_ref, state):
        @pl.when(pl.program_id(0) == 0)
        def _():
            state[...] = jnp.zeros_like(state)
        st_ref[...] = state[...]
        li, tri, dtp, dt, A, cs, Bm, Cm, cb = _ssd_common(dt_ref, bias_ref, alog_ref, b_ref, c_ref)
        dskv = dsk_ref[...]
        for h in range(HEADS):
            cs_h, lam, x_h, dt_h, xdt, g, prev, cl, f = _ssd_head(h, li, tri, dt, cs, cb, xs_ref, state)
            y = _nn(g.astype(BF16), xdt.astype(BF16))
            y = y + _nt(Cm, prev.astype(BF16)) * jnp.exp(cs_h) + _pick(dskv, h) * x_h
            y_ref[:, pl.ds(HD * h, HD)] = y
            state[pl.ds(HD * h, HD), :] = prev * jnp.exp(cl) + _tn((xdt * f).astype(BF16), Bm)

    vec = pl.BlockSpec((1, DT_PAD), lambda c: (0, 0))
    return pl.pallas_call(
        body, name="ssd_fwd", grid=(nc,),
        in_specs=[pl.BlockSpec((CHUNK, AW), lambda c: (c, 0)), pl.BlockSpec((CHUNK, NS), lambda c: (c, 4)),
                  pl.BlockSpec((CHUNK, NS), lambda c: (c, 5)), pl.BlockSpec((CHUNK, DT_PAD), lambda c: (c, 0)),
                  vec, vec, vec],
        out_specs=[pl.BlockSpec((CHUNK, AW), lambda c: (c, 0)), pl.BlockSpec((None, AW, NS), lambda c: (c, 0, 0))],
        out_shape=[jax.ShapeDtypeStruct((T, AW), F32), jax.ShapeDtypeStruct((nc, AW, NS), F32)],
        scratch_shapes=[pltpu.VMEM((AW, NS), F32)],
        compiler_params=_cparams(("arbitrary",)),
    )(act, act, act, dtw, bias, alog, dsk)


def _ssd_bwd(act, dtw, bias, alog, dsk, states, dy):
    nc = T // CHUNK

    def body(xs_ref, b_ref, c_ref, dt_ref, bias_ref, alog_ref, dsk_ref, st_ref, dy_ref,
             dxs_ref, db_ref, dc_ref, ddt_ref, par_ref, dstate):
        step = pl.program_id(0)

        @pl.when(step == 0)
        def _():
            dstate[...] = jnp.zeros_like(dstate)
            par_ref[...] = jnp.zeros_like(par_ref)
        li, tri, dtp, dt, A, cs, Bm, Cm, cb = _ssd_common(dt_ref, bias_ref, alog_ref, b_ref, c_ref)
        dskv = dsk_ref[...]
        lane = lax.broadcasted_iota(jnp.int32, (1, DT_PAD), 1)
        row = lax.broadcasted_iota(jnp.int32, (CHUNK, 1), 0)
        dcb = jnp.zeros((CHUNK, CHUNK), F32)
        dB = jnp.zeros((CHUNK, NS), F32)
        dC = jnp.zeros((CHUNK, NS), F32)
        dcs_mat = jnp.zeros((CHUNK, DT_PAD), F32)
        ddt_mat = jnp.zeros((CHUNK, DT_PAD), F32)
        dD = jnp.zeros((1, DT_PAD), F32)
        for h in range(HEADS):
            cs_h, lam, x_h, dt_h, xdt, g, prev, cl, f = _ssd_head(h, li, tri, dt, cs, cb, xs_ref, st_ref)
            sl = pl.ds(HD * h, HD)
            dyh = dy_ref[:, sl]
            dyb = dyh.astype(BF16)
            dnew = dstate[sl, :]
            prevb = prev.astype(BF16)
            E = jnp.exp(cs_h)
            ecl = jnp.exp(cl)
            xdtb = xdt.astype(BF16)
            dD = dD + jnp.where(lane == h, jnp.sum(dyh * x_h, keepdims=True), 0.0)
            dG = _nt(dyb, xdtb)
            dxdt = _tn(g.astype(BF16), dyb)
            dcb = dcb + dG * lam
            Mm = dG * g
            dcs = jnp.sum(Mm, axis=1, keepdims=True) - jnp.sum(Mm.T, axis=1, keepdims=True)
            Yo = _nt(Cm, prevb)
            dYo = (dyh * E).astype(BF16)
            dC = dC + _nn(dYo, prevb)
            dprev = _tn(dYo, Cm) + dnew * ecl
            dcs = dcs + jnp.sum(dyh * Yo, axis=1, keepdims=True) * E
            dcl = jnp.sum(dnew * prev, keepdims=True) * ecl
            dnewb = dnew.astype(BF16)
            W = _nt(Bm, dnewb)
            dB = dB + _nn((xdt * f).astype(BF16), dnewb)
            dxdt = dxdt + W * f
            dF = jnp.sum(W * xdt, axis=1, keepdims=True) * f
            dcs = dcs - dF
            dcl = dcl + jnp.sum(dF, keepdims=True)
            dcs = dcs + jnp.where(row == CHUNK - 1, dcl, 0.0)
            dcs_mat = dcs_mat + jnp.where(lane == h, dcs, 0.0)
            ddt_mat = ddt_mat + jnp.where(lane == h, jnp.sum(dxdt * x_h, axis=1, keepdims=True), 0.0)
            dxs_ref[:, sl] = _pick(dskv, h) * dyh + dxdt * dt_h
            dstate[sl, :] = dprev
        da = jnp.dot((li <= lax.broadcasted_iota(jnp.int32, (CHUNK, CHUNK), 1)).astype(F32), dcs_mat,
                     precision=HIGHEST, preferred_element_type=F32)
        ddtp = jnp.where(lane < HEADS, (ddt_mat + da * A) * _sigmoid(dtp), 0.0)
        ddt_ref[...] = ddtp
        dcbb = dcb.astype(BF16)
        dc_ref[...] = dC + _nn(dcbb, Bm)
        db_ref[...] = dB + _tn(dcbb, Cm)
        dalog = jnp.where(lane < HEADS, jnp.sum(da * dt, axis=0, keepdims=True) * A, 0.0)
        par_ref[...] += _rows_to_block([jnp.sum(ddtp, axis=0, keepdims=True), dalog, dD], 8, DT_PAD)

    vec = pl.BlockSpec((1, DT_PAD), lambda c: (0, 0))
    rev = lambda c: nc - 1 - c
    return pl.pallas_call(
        body, name="ssd_bwd", grid=(nc,),
        in_specs=[pl.BlockSpec((CHUNK, AW), lambda c: (rev(c), 0)), pl.BlockSpec((CHUNK, NS), lambda c: (rev(c), 4)),
                  pl.BlockSpec((CHUNK, NS), lambda c: (rev(c), 5)), pl.BlockSpec((CHUNK, DT_PAD), lambda c: (rev(c), 0)),
                  vec, vec, vec,
                  pl.BlockSpec((None, AW, NS), lambda c: (rev(c), 0, 0)), pl.BlockSpec((CHUNK, AW), lambda c: (rev(c), 0))],
        out_specs=[pl.BlockSpec((CHUNK, AW), lambda c: (rev(c), 0)), pl.BlockSpec((CHUNK, NS), lambda c: (rev(c), 0)),
                   pl.BlockSpec((CHUNK, NS), lambda c: (rev(c), 0)), pl.BlockSpec((CHUNK, DT_PAD), lambda c: (rev(c), 0)),
                   pl.BlockSpec((8, DT_PAD), lambda c: (0, 0))],
        out_shape=[jax.ShapeDtypeStruct((T, AW), F32), jax.ShapeDtypeStruct((T, NS), F32),
                   jax.ShapeDtypeStruct((T, NS), F32), jax.ShapeDtypeStruct((T, DT_PAD), F32),
                   jax.ShapeDtypeStruct((8, DT_PAD), F32)],
        scratch_shapes=[pltpu.VMEM((AW, NS), F32)],
        compiler_params=_cparams(("arbitrary",)),
    )(act, act, act, dtw, bias, alog, dsk, states, dy)


def _place():
    return lax.axis_index("x"), lax.axis_index("y"), lax.axis_index("c")


def _slot(px, py, pc):
    return 4 * px + 2 * py + pc


def _all_gather(arrs, name):
    na = len(arrs)

    def body(*refs):
        ins, outs = refs[:na], refs[na:2 * na]
        send_sems, recv_sems, local_sems = refs[2 * na:]
        x, y, c = _place()
        me, sib = (x, y, c), (x, y, 1 - c)
        chips = [(1 - x, y), (x, 1 - y), (1 - x, 1 - y)]

        def copy(a, kk, block, to, src=None):
            dst = outs[a].at[_slot(*block)]
            return pltpu.make_async_remote_copy(
                src_ref=dst if src is None else src, dst_ref=dst,
                send_sem=send_sems.at[a, kk], recv_sem=recv_sems.at[a, kk], device_id=to, device_id_type=MESH)

        mine = [pltpu.make_async_copy(ins[a], outs[a].at[_slot(*me)], local_sems.at[a]) for a in range(na)]
        for cp in mine:
            cp.start()
        first = []
        for a in range(na):
            first.append(copy(a, 0, me, sib, src=ins[a]))
            first += [copy(a, 1 + j, me, (*chip, c), src=ins[a]) for j, chip in enumerate(chips)]
        for cp in first:
            cp.start()
        passed = []
        for j, chip in enumerate(chips):
            for a in range(na):
                copy(a, 1 + j, (*chip, c), me).wait_recv()
                fw = copy(a, 4 + j, (*chip, c), sib)
                fw.start()
                passed.append(fw)
        for a in range(na):
            copy(a, 0, sib, me).wait_recv()
            for j, chip in enumerate(chips):
                copy(a, 4 + j, (*chip, 1 - c), me).wait_recv()
        for cp in first + passed:
            cp.wait_send()
        for cp in mine:
            cp.wait()

    any_spec = pl.BlockSpec(memory_space=pl.ANY)
    return pl.pallas_call(
        body, name=name,
        in_specs=[any_spec] * na, out_specs=[any_spec] * na,
        out_shape=[jax.ShapeDtypeStruct((N_DEV,) + a.shape, a.dtype) for a in arrs],
        scratch_shapes=[pltpu.SemaphoreType.DMA((na, 7)), pltpu.SemaphoreType.DMA((na, 7)),
                        pltpu.SemaphoreType.DMA((na,))],
    )(*arrs)


def _reduce_scatter(part, name):
    _, r, C = part.shape

    def body(part_ref, out_ref, own, got_sib, chip_sum, got_ici, lsem, s1, r1, s2, r2):
        x, y, c = _place()
        chips = [(x, y), (1 - x, y), (x, 1 - y), (1 - x, 1 - y)]
        loc = [pltpu.make_async_copy(part_ref.at[_slot(*chips[kk], c)], own.at[kk], lsem.at[kk]) for kk in range(4)]
        d2d = [pltpu.make_async_remote_copy(
            src_ref=part_ref.at[_slot(*chips[kk], 1 - c)], dst_ref=got_sib.at[kk],
            send_sem=s1.at[kk], recv_sem=r1.at[kk], device_id=(x, y, 1 - c), device_id_type=MESH) for kk in range(4)]
        for cp in loc + d2d:
            cp.start()
        ici = [pltpu.make_async_remote_copy(
            src_ref=chip_sum.at[kk - 1], dst_ref=got_ici.at[kk - 1],
            send_sem=s2.at[kk - 1], recv_sem=r2.at[kk - 1], device_id=(*chips[kk], c), device_id_type=MESH)
            for kk in range(1, 4)]
        for kk in (1, 2, 3):
            loc[kk].wait()
            d2d[kk].wait_recv()
            chip_sum[kk - 1] = (own[kk].astype(F32) + got_sib[kk].astype(F32)).astype(BF16)
            ici[kk - 1].start()
        loc[0].wait()
        d2d[0].wait_recv()
        acc = own[0].astype(F32) + got_sib[0].astype(F32)
        for cp in ici:
            cp.wait_recv()
        out_ref[...] = ((acc + got_ici[0].astype(F32)) + got_ici[1].astype(F32)) + got_ici[2].astype(F32)
        for cp in d2d + ici:
            cp.wait_send()

    return pl.pallas_call(
        body, name=name,
        in_specs=[pl.BlockSpec(memory_space=pl.ANY)],
        out_specs=pl.BlockSpec(memory_space=pltpu.VMEM),
        out_shape=jax.ShapeDtypeStruct((r, C), F32),
        scratch_shapes=[pltpu.VMEM((4, r, C), BF16), pltpu.VMEM((4, r, C), BF16), pltpu.VMEM((3, r, C), BF16),
                        pltpu.VMEM((3, r, C), BF16),
                        pltpu.SemaphoreType.DMA((4,)), pltpu.SemaphoreType.DMA((4,)), pltpu.SemaphoreType.DMA((4,)),
                        pltpu.SemaphoreType.DMA((3,)), pltpu.SemaphoreType.DMA((3,))],
        compiler_params=pltpu.CompilerParams(vmem_limit_bytes=VMEM_LIMIT),
    )(part)


def _all_reduce_small(v, name):
    R, C = v.shape

    def body(v_ref, out_ref, got, send_sems, recv_sems):
        x, y, c = _place()
        mine = _slot(x, y, c)
        copies = []
        for kk in range(1, N_DEV):
            fx, fy, fc = kk >> 2 & 1, kk >> 1 & 1, kk & 1
            peer = (1 - x if fx else x, 1 - y if fy else y, 1 - c if fc else c)
            copies.append(pltpu.make_async_remote_copy(
                src_ref=v_ref, dst_ref=got.at[mine], send_sem=send_sems.at[kk - 1], recv_sem=recv_sems.at[kk - 1],
                device_id=peer, device_id_type=MESH))
        for cp in copies:
            cp.start()
        got[mine] = v_ref[...]
        for cp in copies:
            cp.wait_recv()
        acc = got[0]
        for s in range(1, N_DEV):
            acc = acc + got[s]
        out_ref[...] = acc
        for cp in copies:
            cp.wait_send()

    return pl.pallas_call(
        body, name=name,
        in_specs=[pl.BlockSpec(memory_space=pltpu.VMEM)], out_specs=pl.BlockSpec(memory_space=pltpu.VMEM),
        out_shape=jax.ShapeDtypeStruct((R, C), F32),
        scratch_shapes=[pltpu.VMEM((N_DEV, R, C), F32), pltpu.SemaphoreType.DMA((N_DEV - 1,)),
                        pltpu.SemaphoreType.DMA((N_DEV - 1,))],
    )(v)


def _adamw(w, g, m, v, name):
    R, C = w.shape
    tm = R if R <= 512 else 256

    def fn(w, g, m, v):
        m2 = ADAM_B1 * m + (1.0 - ADAM_B1) * g
        v2 = ADAM_B2 * v + (1.0 - ADAM_B2) * (g * g)
        m_hat = m2 / (1.0 - ADAM_B1 ** ADAM_STEP)
        v_hat = v2 / (1.0 - ADAM_B2 ** ADAM_STEP)
        delta = -ADAM_LR * (m_hat / (jnp.sqrt(v_hat) + ADAM_EPS) + ADAM_WD * w)
        return (delta, m2, v2), ()
    return _rowwise(fn, [w, g, m, v], [], [(C, F32)] * 3, [], tm=tm, name=name)


SMALL = ["norm_mix_pre", "norm_mix_post", "norm_mlp_pre", "norm_mlp_post", "norm_ple_post",
         "conv_b", "ssd_norm_g", "dt_bias", "a_log", "d_skip"]


def _pad_row(v, width=D):
    return jnp.pad(v, ((0, 0), (0, width - v.shape[1])))


def kernel(x, p, positions, norm_mix_pre, norm_mix_post, w_in, conv_w, conv_b, dt_bias, a_log, d_skip, ssd_norm_g, w_out, norm_mlp_pre, norm_mlp_post, w_up, w_down, w_ple_gate, w_ple_proj, norm_ple_post, loss_target, m_norm_mix_pre, m_norm_mix_post, m_w_in, m_conv_w, m_conv_b, m_dt_bias, m_a_log, m_d_skip, m_ssd_norm_g, m_w_out, m_norm_mlp_pre, m_norm_mlp_post, m_w_up, m_w_down, m_w_ple_gate, m_w_ple_proj, m_norm_ple_post, v_norm_mix_pre, v_norm_mix_post, v_w_in, v_conv_w, v_conv_b, v_dt_bias, v_a_log, v_d_skip, v_ssd_norm_g, v_w_out, v_norm_mlp_pre, v_norm_mlp_post, v_w_up, v_w_down, v_w_ple_gate, v_w_ple_proj, v_norm_ple_post):
    args = dict(locals())
    x2, p2, tgt = x[0], p[0, 0], loss_target[0]
    g1, g2, g3, g4, g5 = norm_mix_pre, norm_mix_post, norm_mlp_pre, norm_mlp_post, norm_ple_post

    pack = jnp.concatenate([
        jnp.pad(w_in[0].T, ((0, W_IN_SHARD_PAD - W_IN_SHARD), (0, 0))),
        w_out[0],
        w_up[0].T,
        w_down[0],
        w_ple_gate[0],
        w_ple_proj[0].T.reshape(32, D),
    ], axis=0).astype(BF16)
    conv_pack = jnp.pad(conv_w[0], ((0, 4), (0, 32)))
    gw, gconv = _all_gather([pack, conv_pack], "gather_weights")
    w_inT = gw[:, :W_IN_SHARD].reshape(IN_W, D)
    w_qkvzT = w_inT[:4 * AW]
    w_xbcT = w_inT[4 * AW:4 * AW + CONV_CH]
    w_dtT = jnp.pad(w_inT[4 * AW + CONV_CH:], ((0, DT_PAD - HEADS), (0, 0)))
    w_o = gw[:, 384:512].reshape(D, D)
    w_upT = gw[:, 512:1024].reshape(DFF, D)
    w_dn = gw[:, 1024:1536].reshape(DFF, D)
    w_gate = gw[:, 1536:1664].reshape(D, D)
    w_projT = gw[:, 1664:1696].reshape(D, PLE)
    conv_full = gconv[:, :CONV_K, :96].transpose(1, 0, 2).reshape(CONV_K, CONV_CH)

    inv_freq = ROPE_THETA ** (-jnp.arange(HD // 2, dtype=F32) * 2.0 / HD)
    ang = positions[0].astype(F32)[:, None] * inv_freq
    cos, sin = jnp.cos(ang), jnp.sin(ang)
    cosf = jnp.tile(jnp.concatenate([cos, cos], axis=1), (1, HEADS))
    sins = jnp.tile(jnp.concatenate([-sin, sin], axis=1), (1, HEADS))

    bias_w, alog_w, dsk_w = _pad_row(dt_bias, DT_PAD), _pad_row(a_log, DT_PAD), _pad_row(d_skip, DT_PAD)
    rms_pre = lambda a, r, g: a * r * g

    (r1,) = _rowwise(lambda a: ((_rstd(a),), ()), [x2], [], [(1, F32)], [], tm=512, name="rstd_x")
    qkvz = _mm(x2, w_qkvzT, tb=True, tm=512, tn=1024, tk=1024, a_pre=rms_pre, a_rows=[r1], a_cols=[g1], name="proj_qkvz")
    xbc = _mm(x2, w_xbcT, tb=True, tm=512, tn=768, tk=1024, a_pre=rms_pre, a_rows=[r1], a_cols=[g1], name="proj_xbc")
    dtw = _mm(x2, w_dtT, tb=True, tm=512, tn=128, tk=1024, a_pre=rms_pre, a_rows=[r1], a_cols=[g1], name="proj_dt")

    qr, kr, vb = _rope_fwd(qkvz, cosf, sins)
    outs, lses = [], []
    for d in DILATIONS:
        L = T // d
        o, l = _attn_fwd(qr.reshape(L, d * AW), kr.reshape(L, d * AW), vb.reshape(L, d * AW), d)
        outs.append(o.reshape(T, AW))
        lses.append(l.reshape(T, AW))
    attn, lse = _attn_merge(outs, lses)

    act = _conv_fwd(xbc, conv_full, conv_b)
    y_ssd, states = _ssd_fwd(act, dtw, bias_w, alog_w, dsk_w)

    def gated_fwd(y, z, a, gs):
        gi = y * (z * _sigmoid(z))
        return (jnp.concatenate([a, gi * _rstd(gi) * gs], axis=1),), ()
    (cat,) = _rowwise(gated_fwd, [y_ssd, (qkvz, AW, 3), attn], [ssd_norm_g], [(D, F32)], [], tm=512, name="gated_norm")

    mix = _mm(cat, w_o, tm=512, tn=1024, tk=1024, name="mix_out")

    def post1(xx, mm, ga, gb):
        h = xx + mm * _rstd(mm) * ga
        return (h, _rstd(h)), ()
    h1, r3 = _rowwise(post1, [x2, mix], [g2, g3], [(D, F32), (1, F32)], [], tm=512, name="post_mix")

    a_up = _mm(h1, w_upT, tb=True, tm=512, tn=1024, tk=1024, a_pre=rms_pre, a_rows=[r3], a_cols=[g3], name="mlp_up")
    relu2 = lambda a: jnp.square(jnp.maximum(a, 0.0))
    ff = _mm(a_up, w_dn, tm=512, tn=1024, tk=1024, a_pre=relu2, name="mlp_down")
    (h2,) = _rowwise(lambda hh, f, g: ((hh + f * _rstd(f) * g,), ()), [h1, ff], [g4], [(D, F32)], [], tm=512, name="post_mlp")

    gp = _mm(h2, w_gate, tm=512, tn=1024, tk=1024, name="ple_gate")
    pp = _mm(p2, w_projT, tb=True, tm=512, tn=1024, tk=256, name="ple_proj")

    def final(hh, gpre, ppv, tg, g):
        sg = _sigmoid(gpre)
        ple = ppv * sg
        r = _rstd(ple)
        n = ple * r
        h3 = hh + n * g
        e = h3 - tg
        dh3 = e * (1.0 / D)
        dple = _rms_bwd(n, r, g, dh3)
        return (dh3, dple * sg, dple * ppv * sg * (1.0 - sg)), (_colsum(dh3 * n), _colsum(0.5 * e * e * (1.0 / D)))
    dh3, dpp, dgp, dg5, loss_vec = _rowwise(final, [h2, gp, pp, tgt], [g5], [(D, F32)] * 3, [(1, D), (1, D)],
                                            tm=256, name="loss_ple_bwd")

    gw_projT = _mm(dpp, p2, ta=True, tm=512, tn=256, tk=1024, out_dtypes=(BF16,), name="gw_ple_proj")
    gw_gate = _mm(h2, dgp, ta=True, tm=512, tn=1024, tk=1024, out_dtypes=(BF16,), name="gw_ple_gate")
    dh2_g = _mm(dgp, w_gate, tb=True, tm=512, tn=1024, tk=1024, name="dx_ple_gate")

    def bwd_mlp_post(d3, dg_, f, g):
        dh2 = d3 + dg_
        r = _rstd(f)
        n = f * r
        return (dh2, _rms_bwd(n, r, g, dh2)), (_colsum(dh2 * n),)
    dh2, dff, dg4 = _rowwise(bwd_mlp_post, [dh3, dh2_g, ff], [g4], [(D, F32)] * 2, [(1, D)], tm=256, name="bwd_post_mlp")

    gw_dn = _mm(a_up, dff, ta=True, tm=1024, tn=1024, tk=512, a_pre=relu2, out_dtypes=(BF16,), name="gw_mlp_down")
    da_up = _mm(dff, w_dn, tb=True, tm=512, tn=1024, tk=1024, epi=lambda acc, a: (acc * (2.0 * jnp.maximum(a, 0.0)),),
                epi_tiles=[a_up], name="dx_mlp_down")
    gw_upT = _mm(da_up, h1, ta=True, tm=1024, tn=1024, tk=512, b_pre=rms_pre, b_rows=[r3], b_cols=[g3],
                 out_dtypes=(BF16,), name="gw_mlp_up")
    du2 = _mm(da_up, w_upT, tm=512, tn=1024, tk=1024, name="dx_mlp_up")

    def bwd_mix_post(d2, du, hh, rr, mm, ga, gb):
        n3 = hh * rr
        dh1 = d2 + _rms_bwd(n3, rr, gb, du)
        r = _rstd(mm)
        n2 = mm * r
        return (dh1, _rms_bwd(n2, r, ga, dh1)), (_colsum(du * n3), _colsum(dh1 * n2))
    dh1, dmix, dg3, dg2 = _rowwise(bwd_mix_post, [dh2, du2, h1, r3, mix], [g2, g3], [(D, F32)] * 2, [(1, D), (1, D)],
                                   tm=256, name="bwd_post_mix")

    gw_o = _mm(cat, dmix, ta=True, tm=512, tn=1024, tk=1024, out_dtypes=(BF16,), name="gw_out")
    dcat = _mm(dmix, w_o, tb=True, tm=512, tn=1024, tk=1024, name="dx_out")

    def gated_bwd(y, z, dyn, gs):
        sg = _sigmoid(z)
        sz = z * sg
        gi = y * sz
        r = _rstd(gi)
        n = gi * r
        dgi = _rms_bwd(n, r, gs, dyn)
        return (dgi * sz, dgi * y * (sg * (1.0 + z * (1.0 - sg)))), (_colsum(dyn * n),)
    dy_ssd, dz, dgs = _rowwise(gated_bwd, [y_ssd, (qkvz, AW, 3), (dcat, AW, 1)], [ssd_norm_g], [(AW, F32)] * 2, [(1, AW)],
                               tm=512, name="bwd_gated_norm")

    dxs, dBm, dCm, ddtw, ssd_par = _ssd_bwd(act, dtw, bias_w, alog_w, dsk_w, states, dy_ssd)
    dact = jnp.concatenate([dxs, dBm, dCm], axis=1)
    dxbc, conv_par = _conv_bwd(xbc, dact, conv_full, conv_b)

    dattn = dcat[:, :AW]
    dqs, dks, dvs = [], [], []
    for d in DILATIONS:
        L = T // d
        rs = lambda t: t.reshape(L, d * AW)
        dq, dk, dv = _attn_bwd(rs(qr), rs(kr), rs(vb), rs(dattn), rs(attn), rs(lse), d)
        dqs.append(dq.reshape(T, AW))
        dks.append(dk.reshape(T, AW))
        dvs.append(dv.reshape(T, AW))
    dqkvz = _rope_bwd(dqs, dks, dvs, dz, cosf, sins)

    du1a = _mm(dqkvz, w_qkvzT, tm=512, tn=1024, tk=1024, name="dx_qkvz")
    du1b = _mm(dxbc, w_xbcT, tm=512, tn=1024, tk=768, name="dx_xbc")
    du1c = _mm(ddtw, w_dtT, tm=512, tn=1024, tk=128, name="dx_dt")
    pre1 = dict(b_pre=rms_pre, b_rows=[r1], b_cols=[g1], out_dtypes=(BF16,))
    gw_qkvzT = _mm(dqkvz, x2, ta=True, tm=1024, tn=1024, tk=512, name="gw_qkvz", **pre1)
    gw_xbcT = _mm(dxbc, x2, ta=True, tm=768, tn=1024, tk=512, name="gw_xbc", **pre1)
    gw_dtT = _mm(ddtw, x2, ta=True, tm=128, tn=1024, tk=512, name="gw_dt", **pre1)

    def bwd_in(d1, ua, ub, uc, xx, rr, g):
        n = xx * rr
        du = ua + ub + uc
        return (d1 + _rms_bwd(n, rr, g, du),), (_colsum(du * n),)
    grad_x, dg1 = _rowwise(bwd_in, [dh1, du1a, du1b, du1c, x2, r1], [g1], [(D, F32)], [(1, D)], tm=256, name="bwd_pre_mix")

    gw_inT = jnp.concatenate([gw_qkvzT, gw_xbcT, gw_dtT], axis=0)[:IN_W]
    gw_inT = jnp.pad(gw_inT.reshape(N_DEV, W_IN_SHARD, D), ((0, 0), (0, W_IN_SHARD_PAD - W_IN_SHARD), (0, 0)))
    g_inT = _reduce_scatter(gw_inT, "rs_w_in")
    g_out = _reduce_scatter(gw_o.reshape(N_DEV, 128, D), "rs_w_out")
    g_upT = _reduce_scatter(gw_upT.reshape(N_DEV, 512, D), "rs_w_up")
    g_dn = _reduce_scatter(gw_dn.reshape(N_DEV, 512, D), "rs_w_down")
    g_gate = _reduce_scatter(gw_gate.reshape(N_DEV, 128, D), "rs_w_gate")
    g_projT = _reduce_scatter(gw_projT.reshape(N_DEV, 32, D), "rs_w_proj")

    small = jnp.concatenate([
        dg1, dg2, dg3, dg4, dg5,
        _pad_row(conv_par[4:5]), _pad_row(dgs), _pad_row(ssd_par[0:1]), _pad_row(ssd_par[1:2]), _pad_row(ssd_par[2:3]),
        _pad_row(conv_par[0:4]), loss_vec, jnp.zeros((1, D), F32),
    ], axis=0)
    small = _all_reduce_small(small, "reduce_small")
    loss = jnp.sum(small[14])
    me = lax.axis_index("x") * 4 + lax.axis_index("y") * 2 + lax.axis_index("c")
    g_conv_w = lax.dynamic_slice(small[10:14, :CONV_CH], (0, me * 96), (CONV_K, 96))

    grads = {
        "w_in": g_inT[:W_IN_SHARD].T[None], "w_out": g_out[None], "w_up": g_upT.T[None], "w_down": g_dn[None],
        "w_ple_gate": g_gate[None], "w_ple_proj": g_projT.reshape(128, PLE).T[None], "conv_w": g_conv_w[None],
        "norm_mix_pre": small[0:1], "norm_mix_post": small[1:2], "norm_mlp_pre": small[2:3], "norm_mlp_post": small[3:4],
        "norm_ple_post": small[4:5], "conv_b": small[5:6, :CONV_CH], "ssd_norm_g": small[6:7, :AW],
        "dt_bias": small[7:8, :HEADS], "a_log": small[8:9, :HEADS], "d_skip": small[9:10, :HEADS],
    }
    delta, new_m, new_v = {}, {}, {}
    for nme in ["w_in", "w_out", "w_up", "w_down", "w_ple_gate", "w_ple_proj"]:
        dl, mm_, vv_ = _adamw(args[nme][0], grads[nme][0], args["m_" + nme][0], args["v_" + nme][0], "adamw_" + nme)
        delta[nme], new_m[nme], new_v[nme] = dl[None], mm_[None], vv_[None]

    def pack_small(prefix):
        rows = [_pad_row(args[prefix + nme]) for nme in SMALL]
        rows.append(_pad_row(args[prefix + "conv_w"][0]))
        rows.append(jnp.zeros((2, D), F32))
        return jnp.concatenate(rows, axis=0)
    g_small = jnp.concatenate([small[0:10], _pad_row(g_conv_w), jnp.zeros((2, D), F32)], axis=0)
    dl, mm_, vv_ = _adamw(pack_small(""), g_small, pack_small("m_"), pack_small("v_"), "adamw_small")
    for i, nme in enumerate(SMALL):
        wdt = args[nme].shape[1]
        delta[nme], new_m[nme], new_v[nme] = dl[i:i + 1, :wdt], mm_[i:i + 1, :wdt], vv_[i:i + 1, :wdt]
    delta["conv_w"], new_m["conv_w"], new_v["conv_w"] = dl[None, 10:14, :96], mm_[None, 10:14, :96], vv_[None, 10:14, :96]

    order = ["norm_mix_pre", "norm_mix_post", "w_in", "conv_w", "conv_b", "dt_bias", "a_log", "d_skip", "ssd_norm_g",
             "w_out", "norm_mlp_pre", "norm_mlp_post", "w_up", "w_down", "w_ple_gate", "w_ple_proj", "norm_ple_post"]
    return (loss, grad_x[None], *[grads[n] for n in order], *[delta[n] for n in order],
            *[new_m[n] for n in order], *[new_v[n] for n in order])
```

```python
import functools
import math

import jax
import jax.numpy as jnp
from jax import lax
from jax.experimental import pallas as pl
from jax.experimental.pallas import tpu as pltpu

F32 = jnp.float32
BF16 = jnp.bfloat16
MESH = pl.DeviceIdType.MESH
HIGHEST = lax.Precision.HIGHEST

N_DEV = 8
T = 4096
D = 1024
HEADS = 8
HD = 64
AW = 512
NS = 128
CONV_K = 4
CONV_CH = 768
CHUNK = 128
DFF = 4096
PLE = 256
EPS = 1e-6
ROPE_THETA = 10000.0
DILATIONS = (1, 4, 16)
QBLK = 128
NEG = -1e30
IN_W = 2824
W_IN_SHARD = 353
W_IN_SHARD_PAD = 384
DT_PAD = 128

ADAM_LR, ADAM_B1, ADAM_B2, ADAM_EPS, ADAM_WD, ADAM_STEP = 0.001, 0.9, 0.999, 1e-08, 0.01, 10

VMEM_LIMIT = 56 * 1024 * 1024


def _cparams(sem=None):
    return pltpu.CompilerParams(dimension_semantics=sem, vmem_limit_bytes=VMEM_LIMIT)


def _dot(a, b, ca, cb, precision=None):
    return lax.dot_general(a, b, (((ca,), (cb,)), ((), ())), preferred_element_type=F32, precision=precision)


def _nn(a, b):
    return _dot(a, b, 1, 0)


def _nt(a, b):
    return _dot(a, b, 1, 1)


def _tn(a, b):
    return _dot(a, b, 0, 0)


def _sigmoid(x):
    return 1.0 / (1.0 + jnp.exp(-x))


def _softplus(x):
    return jnp.maximum(x, 0.0) + jnp.log(1.0 + jnp.exp(-jnp.abs(x)))


def _mm(a, b, *, ta=False, tb=False, tm, tn, tk, name,
        a_pre=None, a_rows=(), a_cols=(), b_pre=None, b_rows=(), b_cols=(),
        epi=None, epi_tiles=(), out_dtypes=(F32,)):
    if ta:
        K, M = a.shape
    else:
        M, K = a.shape
    if tb:
        N, K2 = b.shape
    else:
        K2, N = b.shape
    assert K == K2 and M % tm == 0 and N % tn == 0 and K % tk == 0, (name, a.shape, b.shape)
    nk = K // tk
    if ta:
        a_spec = pl.BlockSpec((tk, tm), lambda i, j, k: (k, i))
        a_row_specs = [pl.BlockSpec((tk, 1), lambda i, j, k: (k, 0)) for _ in a_rows]
        a_col_specs = [pl.BlockSpec((1, tm), lambda i, j, k: (0, i)) for _ in a_cols]
    else:
        a_spec = pl.BlockSpec((tm, tk), lambda i, j, k: (i, k))
        a_row_specs = [pl.BlockSpec((tm, 1), lambda i, j, k: (i, 0)) for _ in a_rows]
        a_col_specs = [pl.BlockSpec((1, tk), lambda i, j, k: (0, k)) for _ in a_cols]
    if tb:
        b_spec = pl.BlockSpec((tn, tk), lambda i, j, k: (j, k))
        b_row_specs = [pl.BlockSpec((tn, 1), lambda i, j, k: (j, 0)) for _ in b_rows]
        b_col_specs = [pl.BlockSpec((1, tk), lambda i, j, k: (0, k)) for _ in b_cols]
    else:
        b_spec = pl.BlockSpec((tk, tn), lambda i, j, k: (k, j))
        b_row_specs = [pl.BlockSpec((tk, 1), lambda i, j, k: (k, 0)) for _ in b_rows]
        b_col_specs = [pl.BlockSpec((1, tn), lambda i, j, k: (0, j)) for _ in b_cols]
    o_spec = pl.BlockSpec((tm, tn), lambda i, j, k: (i, j))
    na, nb, ne, no = len(a_rows) + len(a_cols), len(b_rows) + len(b_cols), len(epi_tiles), len(out_dtypes)

    def body(*refs):
        a_ref, b_ref = refs[0], refs[1]
        a_ex = refs[2:2 + na]
        b_ex = refs[2 + na:2 + na + nb]
        e_ex = refs[2 + na + nb:2 + na + nb + ne]
        outs = refs[2 + na + nb + ne:2 + na + nb + ne + no]
        acc = refs[-1]
        k = pl.program_id(2)

        @pl.when(k == 0)
        def _():
            acc[...] = jnp.zeros_like(acc)

        at = a_ref[...]
        if a_pre is not None:
            at = a_pre(at, *[r[...] for r in a_ex])
        bt = b_ref[...]
        if b_pre is not None:
            bt = b_pre(bt, *[r[...] for r in b_ex])
        acc[...] += _dot(at.astype(BF16), bt.astype(BF16), 0 if ta else 1, 1 if tb else 0)

        @pl.when(k == nk - 1)
        def _():
            res = acc[...]
            vals = epi(res, *[r[...] for r in e_ex]) if epi is not None else (res,)
            for o_ref, val in zip(outs, vals):
                o_ref[...] = val.astype(o_ref.dtype)

    outs = pl.pallas_call(
        body, name=name,
        grid=(M // tm, N // tn, nk),
        in_specs=[a_spec, b_spec] + a_row_specs + a_col_specs + b_row_specs + b_col_specs + [o_spec] * ne,
        out_specs=[o_spec] * no,
        out_shape=[jax.ShapeDtypeStruct((M, N), dt) for dt in out_dtypes],
        scratch_shapes=[pltpu.VMEM((tm, tn), F32)],
        compiler_params=_cparams(("parallel", "parallel", "arbitrary")),
    )(a, b, *a_rows, *a_cols, *b_rows, *b_cols, *epi_tiles)
    return outs[0] if no == 1 else outs


def _rowwise(fn, rows, vecs, out_rows, out_sums, *, tm, name):
    specs, arrs = [], []
    R = None
    for r in rows:
        if isinstance(r, tuple):
            arr, width, cb = r
            specs.append(pl.BlockSpec((tm, width), lambda i, cb=cb: (i, cb)))
        else:
            arr = r
            specs.append(pl.BlockSpec((tm, arr.shape[1]), lambda i: (i, 0)))
        R = arr.shape[0] if R is None else R
        assert arr.shape[0] == R, name
        arrs.append(arr)
    assert R % tm == 0, name
    for v in vecs:
        specs.append(pl.BlockSpec(v.shape, lambda i: (0, 0)))
        arrs.append(v)
    nr, nv, no, ns = len(rows), len(vecs), len(out_rows), len(out_sums)
    out_specs = [pl.BlockSpec((tm, w), lambda i: (i, 0)) for w, _ in out_rows]
    out_specs += [pl.BlockSpec(s, lambda i: (0, 0)) for s in out_sums]
    out_shape = [jax.ShapeDtypeStruct((R, w), dt) for w, dt in out_rows]
    out_shape += [jax.ShapeDtypeStruct(s, F32) for s in out_sums]

    def body(*refs):
        ins = [r[...] for r in refs[:nr + nv]]
        o_refs = refs[nr + nv:nr + nv + no]
        s_refs = refs[nr + nv + no:]
        o_vals, s_vals = fn(*ins)
        for ref, val in zip(o_refs, o_vals):
            ref[...] = val.astype(ref.dtype)
        if ns:
            @pl.when(pl.program_id(0) == 0)
            def _():
                for ref in s_refs:
                    ref[...] = jnp.zeros_like(ref)
            for ref, val in zip(s_refs, s_vals):
                ref[...] += val

    outs = pl.pallas_call(
        body, name=name, grid=(R // tm,), in_specs=specs, out_specs=out_specs, out_shape=out_shape,
        compiler_params=_cparams(("arbitrary",) if ns else ("parallel",)),
    )(*arrs)
    return outs


def _colsum(x):
    return jnp.sum(x, axis=0, keepdims=True)


def _rstd(x):
    return lax.rsqrt(jnp.mean(x * x, axis=-1, keepdims=True) + EPS)


def _rms_bwd(xn, r, g, dy):
    dn = dy * g
    return r * (dn - xn * jnp.mean(dn * xn, axis=-1, keepdims=True))


def _partner(t):
    parts = []
    for s in range(t.shape[1] // 128):
        ts = t[:, 128 * s:128 * (s + 1)]
        lane = lax.broadcasted_iota(jnp.int32, ts.shape, 1)
        up = pltpu.roll(ts, 96, 1)
        down = pltpu.roll(ts, 32, 1)
        parts.append(jnp.where((lane % 64) < 32, up, down))
    return jnp.concatenate(parts, axis=1)


def _rope_fwd(qkvz, cosf, sins):
    def fn(q, k, v, c, s):
        qr = (q * c + _partner(q) * s) * (HD ** -0.5)
        kr = k * c + _partner(k) * s
        return (qr, kr, v), ()
    return _rowwise(fn, [(qkvz, AW, 0), (qkvz, AW, 1), (qkvz, AW, 2), cosf, sins], [],
                    [(AW, BF16), (AW, BF16), (AW, BF16)], [], tm=512, name="rope_fwd")


def _rope_bwd(dqs, dks, dvs, dz, cosf, sins):
    def fn(q1, q2, q3, k1, k2, k3, v1, v2, v3, z, c, s):
        dqr = (q1 + q2 + q3) * (HD ** -0.5)
        dkr = k1 + k2 + k3
        dq = dqr * c + _partner(dqr * s)
        dk = dkr * c + _partner(dkr * s)
        return (jnp.concatenate([dq, dk, v1 + v2 + v3, z], axis=1),), ()
    return _rowwise(fn, [*dqs, *dks, *dvs, dz, cosf, sins], [], [(4 * AW, F32)], [], tm=256, name="rope_bwd")[0]


def _band_masks():
    qi = lax.broadcasted_iota(jnp.int32, (QBLK, QBLK), 0)
    kj = lax.broadcasted_iota(jnp.int32, (QBLK, QBLK), 1)
    return kj >= qi, kj <= qi


def _attn_fwd(q, k, v, d):
    L = q.shape[0]
    nb = L // QBLK

    def body(q_ref, kp_ref, kc_ref, vp_ref, vc_ref, o_ref, l_ref):
        n = pl.program_id(1)
        mask_p, mask_c = _band_masks()
        bias = jnp.concatenate([jnp.where(mask_p, 0.0, NEG) + jnp.where(n > 0, 0.0, NEG),
                                jnp.where(mask_c, 0.0, NEG)], axis=1)
        s = []
        for h in range(HEADS):
            sl = pl.ds(HD * h, HD)
            qh = q_ref[:, sl]
            s.append(jnp.concatenate([_nt(qh, kp_ref[:, sl]), _nt(qh, kc_ref[:, sl])], axis=1))
        s = jnp.stack(s) + bias
        m = jnp.max(s, axis=2, keepdims=True)
        e = jnp.exp(s - m)
        den = jnp.sum(e, axis=2, keepdims=True)
        p = (e * (1.0 / den)).astype(BF16)
        lse = m + jnp.log(den)
        for h in range(HEADS):
            sl = pl.ds(HD * h, HD)
            o_ref[:, sl] = _nn(p[h, :, :QBLK], vp_ref[:, sl]) + _nn(p[h, :, QBLK:], vc_ref[:, sl])
            l_ref[:, sl] = jnp.broadcast_to(lse[h], (QBLK, HD))

    cur = pl.BlockSpec((QBLK, AW), lambda r, n: (n, r))
    prev = pl.BlockSpec((QBLK, AW), lambda r, n: (jnp.maximum(n - 1, 0), r))
    return pl.pallas_call(
        body, name=f"attn_fwd_d{d}", grid=(d, nb),
        in_specs=[cur, prev, cur, prev, cur], out_specs=[cur, cur],
        out_shape=[jax.ShapeDtypeStruct((L, d * AW), F32)] * 2,
        compiler_params=_cparams(("parallel", "parallel")),
    )(q, k, k, v, v)


def _attn_bwd(q, k, v, do, at, lse, d):
    L = q.shape[0]
    nb = L // QBLK

    def body(q0_ref, q1_ref, kp_ref, kc_ref, vp_ref, vc_ref, do0_ref, do1_ref, at0_ref, at1_ref,
             l0_ref, l1_ref, dq_ref, dk_ref, dv_ref):
        n = pl.program_id(1)
        mask_p, mask_c = _band_masks()
        prev_bias = jnp.where(mask_p, 0.0, NEG)
        bias = jnp.concatenate([prev_bias + jnp.where(n > 0, 0.0, NEG), jnp.where(mask_c, 0.0, NEG),
                                prev_bias + jnp.where(n < nb - 1, 0.0, NEG)], axis=1)
        s, dp, ls, dl, ops = [], [], [], [], []
        for h in range(HEADS):
            sl = pl.ds(HD * h, HD)
            one = pl.ds(HD * h, 1)
            q0, q1 = q0_ref[:, sl], q1_ref[:, sl]
            kp, kc, vp, vc = kp_ref[:, sl], kc_ref[:, sl], vp_ref[:, sl], vc_ref[:, sl]
            do0, do1 = do0_ref[:, sl], do1_ref[:, sl]
            do0b, do1b = do0.astype(BF16), do1.astype(BF16)
            s.append(jnp.concatenate([_nt(q0, kp), _nt(q0, kc), _nt(q1, kc)], axis=1))
            dp.append(jnp.concatenate([_nt(do0b, vp), _nt(do0b, vc), _nt(do1b, vc)], axis=1))
            dl0 = jnp.sum(do0 * at0_ref[:, sl], axis=1, keepdims=True)
            dl1 = jnp.sum(do1 * at1_ref[:, sl], axis=1, keepdims=True)
            dl.append(jnp.concatenate([jnp.broadcast_to(dl0, (QBLK, 2 * QBLK)), jnp.broadcast_to(dl1, (QBLK, QBLK))], axis=1))
            ls.append(jnp.concatenate([jnp.broadcast_to(l0_ref[:, one], (QBLK, 2 * QBLK)),
                                       jnp.broadcast_to(l1_ref[:, one], (QBLK, QBLK))], axis=1))
            ops.append((q0, q1, kp, kc, do0b, do1b))
        p = jnp.exp(jnp.stack(s) + bias - jnp.stack(ls))
        ds = (p * (jnp.stack(dp) - jnp.stack(dl))).astype(BF16)
        p = p.astype(BF16)
        for h in range(HEADS):
            sl = pl.ds(HD * h, HD)
            q0, q1, kp, kc, do0b, do1b = ops[h]
            dq_ref[:, sl] = _nn(ds[h, :, :QBLK], kp) + _nn(ds[h, :, QBLK:2 * QBLK], kc)
            dv_ref[:, sl] = _tn(p[h, :, QBLK:2 * QBLK], do0b) + _tn(p[h, :, 2 * QBLK:], do1b)
            dk_ref[:, sl] = _tn(ds[h, :, QBLK:2 * QBLK], q0) + _tn(ds[h, :, 2 * QBLK:], q1)

    cur = pl.BlockSpec((QBLK, AW), lambda r, n: (n, r))
    prev = pl.BlockSpec((QBLK, AW), lambda r, n: (jnp.maximum(n - 1, 0), r))
    nxt = pl.BlockSpec((QBLK, AW), lambda r, n: (jnp.minimum(n + 1, nb - 1), r))
    return pl.pallas_call(
        body, name=f"attn_bwd_d{d}", grid=(d, nb),
        in_specs=[cur, nxt, prev, cur, prev, cur, cur, nxt, cur, nxt, cur, nxt], out_specs=[cur, cur, cur],
        out_shape=[jax.ShapeDtypeStruct((L, d * AW), F32)] * 3,
        compiler_params=_cparams(("parallel", "parallel")),
    )(q, q, k, k, v, v, do, do, at, at, lse, lse)


def _attn_merge(outs, lses):
    def fn(o1, o2, o3, l1, l2, l3):
        m = jnp.maximum(jnp.maximum(l1, l2), l3)
        e1, e2, e3 = jnp.exp(l1 - m), jnp.exp(l2 - m), jnp.exp(l3 - m)
        s = e1 + e2 + e3
        inv = 1.0 / s
        return ((e1 * inv) * o1 + (e2 * inv) * o2 + (e3 * inv) * o3, m + jnp.log(s)), ()
    return _rowwise(fn, [*outs, *lses], [], [(AW, F32), (AW, F32)], [], tm=512, name="attn_merge")


CONV_TM = 512
HALO = 8


def _conv_pre(ext, w, b):
    y = b + w[3] * ext
    for kk in range(1, CONV_K):
        y = y + w[3 - kk] * pltpu.roll(ext, kk, 0)
    return y


def _rows_to_block(rows, n, width):
    ri = lax.broadcasted_iota(jnp.int32, (n, width), 0)
    out = jnp.zeros((n, width), F32)
    for j, r in enumerate(rows):
        out = out + jnp.where(ri == j, r, 0.0)
    return out


def _conv_fwd(xbc, w, b):
    nblk = T // CONV_TM

    def body(x_ref, h_ref, w_ref, b_ref, o_ref):
        i = pl.program_id(0)
        halo = jnp.where(i > 0, h_ref[...], 0.0)
        ext = jnp.concatenate([halo, x_ref[...]], axis=0)
        y = _conv_pre(ext, [w_ref[pl.ds(j, 1), :] for j in range(CONV_K)], b_ref[...])[HALO:]
        o_ref[...] = y * _sigmoid(y)

    return pl.pallas_call(
        body, name="conv_fwd", grid=(nblk,),
        in_specs=[pl.BlockSpec((CONV_TM, CONV_CH), lambda i: (i, 0)),
                  pl.BlockSpec((HALO, CONV_CH), lambda i: (jnp.maximum(i * (CONV_TM // HALO) - 1, 0), 0)),
                  pl.BlockSpec((CONV_K, CONV_CH), lambda i: (0, 0)),
                  pl.BlockSpec((1, CONV_CH), lambda i: (0, 0))],
        out_specs=pl.BlockSpec((CONV_TM, CONV_CH), lambda i: (i, 0)),
        out_shape=jax.ShapeDtypeStruct((T, CONV_CH), F32),
        compiler_params=_cparams(("parallel",)),
    )(xbc, xbc, w, b)


def _conv_bwd(xbc, dact, w, b):
    nblk = T // CONV_TM
    per = CONV_TM // HALO

    def body(x_ref, xb_ref, xa_ref, g_ref, ga_ref, w_ref, b_ref, dx_ref, dw_ref):
        i = pl.program_id(0)
        wv = [w_ref[pl.ds(j, 1), :] for j in range(CONV_K)]
        before = jnp.where(i > 0, xb_ref[...], 0.0)
        last = i == nblk - 1
        after = jnp.where(last, 0.0, xa_ref[...])
        g_after = jnp.where(last, 0.0, ga_ref[...])
        ext = jnp.concatenate([before, x_ref[...], after], axis=0)
        y = _conv_pre(ext, wv, b_ref[...])[HALO:]
        sg = _sigmoid(y)
        dy = jnp.concatenate([g_ref[...], g_after], axis=0) * (sg * (1.0 + y * (1.0 - sg)))
        n = CONV_TM + HALO
        dx = wv[3] * dy
        for kk in range(1, CONV_K):
            dx = dx + wv[3 - kk] * pltpu.roll(dy, n - kk, 0)
        dx_ref[...] = dx[:CONV_TM]
        dyc = dy[:CONV_TM]
        rows = [jnp.sum(dyc * (pltpu.roll(ext, 3 - j, 0) if j < 3 else ext)[HALO:HALO + CONV_TM], axis=0, keepdims=True)
                for j in range(CONV_K)]
        rows.append(jnp.sum(dyc, axis=0, keepdims=True))
        part = _rows_to_block(rows, 8, CONV_CH)

        @pl.when(i == 0)
        def _():
            dw_ref[...] = jnp.zeros_like(dw_ref)
        dw_ref[...] += part

    blk = pl.BlockSpec((CONV_TM, CONV_CH), lambda i: (i, 0))
    hb = pl.BlockSpec((HALO, CONV_CH), lambda i: (jnp.maximum(i * per - 1, 0), 0))
    ha = pl.BlockSpec((HALO, CONV_CH), lambda i: (jnp.minimum((i + 1) * per, T // HALO - 1), 0))
    return pl.pallas_call(
        body, name="conv_bwd", grid=(nblk,),
        in_specs=[blk, hb, ha, blk, ha, pl.BlockSpec((CONV_K, CONV_CH), lambda i: (0, 0)),
                  pl.BlockSpec((1, CONV_CH), lambda i: (0, 0))],
        out_specs=[blk, pl.BlockSpec((8, CONV_CH), lambda i: (0, 0))],
        out_shape=[jax.ShapeDtypeStruct((T, CONV_CH), F32), jax.ShapeDtypeStruct((8, CONV_CH), F32)],
        compiler_params=_cparams(("arbitrary",)),
    )(xbc, xbc, xbc, dact, dact, w, b)


def _pick(mat, h):
    lane = lax.broadcasted_iota(jnp.int32, mat.shape, 1)
    return jnp.sum(jnp.where(lane == h, mat, 0.0), axis=1, keepdims=True)


def _ssd_common(dt_ref, bias_ref, alog_ref, b_ref, c_ref):
    li = lax.broadcasted_iota(jnp.int32, (CHUNK, CHUNK), 0)
    si = lax.broadcasted_iota(jnp.int32, (CHUNK, CHUNK), 1)
    tri = li >= si
    dtp = dt_ref[...] + bias_ref[...]
    dt = _softplus(dtp)
    A = -jnp.exp(alog_ref[...])
    a = dt * A
    cs = jnp.dot(tri.astype(F32), a, precision=HIGHEST, preferred_element_type=F32)
    Bm = b_ref[...].astype(BF16)
    Cm = c_ref[...].astype(BF16)
    cb = _nt(Cm, Bm)
    return li, tri, dtp, dt, A, cs, Bm, Cm, cb


def _ssd_head(h, li, tri, dt, cs, cb, xs_ref, state_ref):
    cs_h = _pick(cs, h)
    lam = jnp.exp(jnp.where(tri, cs_h - jnp.broadcast_to(cs_h, (CHUNK, CHUNK)).T, NEG))
    x_h = xs_ref[:, pl.ds(HD * h, HD)]
    dt_h = _pick(dt, h)
    xdt = x_h * dt_h
    g = cb * lam
    prev = state_ref[pl.ds(HD * h, HD), :]
    row = lax.broadcasted_iota(jnp.int32, (CHUNK, 1), 0)
    cl = jnp.sum(jnp.where(row == CHUNK - 1, cs_h, 0.0), axis=0, keepdims=True)
    f = jnp.exp(cl - cs_h)
    return cs_h, lam, x_h, dt_h, xdt, g, prev, cl, f


def _ssd_fwd(act, dtw, bias, alog, dsk):
    nc = T // CHUNK

    def body(xs_ref, b_ref, c_ref, dt_ref, bias_ref, alog_ref, dsk_ref, y_ref, st_ref, state):
        @pl.when(pl.program_id(0) == 0)
        def _():
            state[...] = jnp.zeros_like(state)
        st_ref[...] = state[...]
        li, tri, dtp, dt, A, cs, Bm, Cm, cb = _ssd_common(dt_ref, bias_ref, alog_ref, b_ref, c_ref)
        dskv = dsk_ref[...]
        for h in range(HEADS):
            cs_h, lam, x_h, dt_h, xdt, g, prev, cl, f = _ssd_head(h, li, tri, dt, cs, cb, xs_ref, state)
            y = _nn(g.astype(BF16), xdt.astype(BF16))
            y = y + _nt(Cm, prev.astype(BF16)) * jnp.exp(cs_h) + _pick(dskv, h) * x_h
            y_ref[:, pl.ds(HD * h, HD)] = y
            state[pl.ds(HD * h, HD), :] = prev * jnp.exp(cl) + _tn((xdt * f).astype(BF16), Bm)

    vec = pl.BlockSpec((1, DT_PAD), lambda c: (0, 0))
    return pl.pallas_call(
        body, name="ssd_fwd", grid=(nc,),
        in_specs=[pl.BlockSpec((CHUNK, AW), lambda c: (c, 0)), pl.BlockSpec((CHUNK, NS), lambda c: (c, 4)),
                  pl.BlockSpec((CHUNK, NS), lambda c: (c, 5)), pl.BlockSpec((CHUNK, DT_PAD), lambda c: (c, 0)),
                  vec, vec, vec],
        out_specs=[pl.BlockSpec((CHUNK, AW), lambda c: (c, 0)), pl.BlockSpec((None, AW, NS), lambda c: (c, 0, 0))],
        out_shape=[jax.ShapeDtypeStruct((T, AW), F32), jax.ShapeDtypeStruct((nc, AW, NS), F32)],
        scratch_shapes=[pltpu.VMEM((AW, NS), F32)],
        compiler_params=_cparams(("arbitrary",)),
    )(act, act, act, dtw, bias, alog, dsk)


def _ssd_bwd(act, dtw, bias, alog, dsk, states, dy):
    nc = T // CHUNK

    def body(xs_ref, b_ref, c_ref, dt_ref, bias_ref, alog_ref, dsk_ref, st_ref, dy_ref,
             dxs_ref, db_ref, dc_ref, ddt_ref, par_ref, dstate):
        step = pl.program_id(0)

        @pl.when(step == 0)
        def _():
            dstate[...] = jnp.zeros_like(dstate)
            par_ref[...] = jnp.zeros_like(par_ref)
        li, tri, dtp, dt, A, cs, Bm, Cm, cb = _ssd_common(dt_ref, bias_ref, alog_ref, b_ref, c_ref)
        dskv = dsk_ref[...]
        lane = lax.broadcasted_iota(jnp.int32, (1, DT_PAD), 1)
        row = lax.broadcasted_iota(jnp.int32, (CHUNK, 1), 0)
        dcb = jnp.zeros((CHUNK, CHUNK), F32)
        dB = jnp.zeros((CHUNK, NS), F32)
        dC = jnp.zeros((CHUNK, NS), F32)
        dcs_mat = jnp.zeros((CHUNK, DT_PAD), F32)
        ddt_mat = jnp.zeros((CHUNK, DT_PAD), F32)
        dD = jnp.zeros((1, DT_PAD), F32)
        for h in range(HEADS):
            cs_h, lam, x_h, dt_h, xdt, g, prev, cl, f = _ssd_head(h, li, tri, dt, cs, cb, xs_ref, st_ref)
            sl = pl.ds(HD * h, HD)
            dyh = dy_ref[:, sl]
            dyb = dyh.astype(BF16)
            dnew = dstate[sl, :]
            prevb = prev.astype(BF16)
            E = jnp.exp(cs_h)
            ecl = jnp.exp(cl)
            xdtb = xdt.astype(BF16)
            dD = dD + jnp.where(lane == h, jnp.sum(dyh * x_h, keepdims=True), 0.0)
            dG = _nt(dyb, xdtb)
            dxdt = _tn(g.astype(BF16), dyb)
            dcb = dcb + dG * lam
            Mm = dG * g
            dcs = jnp.sum(Mm, axis=1, keepdims=True) - jnp.sum(Mm.T, axis=1, keepdims=True)
            Yo = _nt(Cm, prevb)
            dYo = (dyh * E).astype(BF16)
            dC = dC + _nn(dYo, prevb)
            dprev = _tn(dYo, Cm) + dnew * ecl
            dcs = dcs + jnp.sum(dyh * Yo, axis=1, keepdims=True) * E
            dcl = jnp.sum(dnew * prev, keepdims=True) * ecl
            dnewb = dnew.astype(BF16)
            W = _nt(Bm, dnewb)
            dB = dB + _nn((xdt * f).astype(BF16), dnewb)
            dxdt = dxdt + W * f
            dF = jnp.sum(W * xdt, axis=1, keepdims=True) * f
            dcs = dcs - dF
            dcl = dcl + jnp.sum(dF, keepdims=True)
            dcs = dcs + jnp.where(row == CHUNK - 1, dcl, 0.0)
            dcs_mat = dcs_mat + jnp.where(lane == h, dcs, 0.0)
            ddt_mat = ddt_mat + jnp.where(lane == h, jnp.sum(dxdt * x_h, axis=1, keepdims=True), 0.0)
            dxs_ref[:, sl] = _pick(dskv, h) * dyh + dxdt * dt_h
            dstate[sl, :] = dprev
        da = jnp.dot((li <= lax.broadcasted_iota(jnp.int32, (CHUNK, CHUNK), 1)).astype(F32), dcs_mat,
                     precision=HIGHEST, preferred_element_type=F32)
        ddtp = jnp.where(lane < HEADS, (ddt_mat + da * A) * _sigmoid(dtp), 0.0)
        ddt_ref[...] = ddtp
        dcbb = dcb.astype(BF16)
        dc_ref[...] = dC + _nn(dcbb, Bm)
        db_ref[...] = dB + _tn(dcbb, Cm)
        dalog = jnp.where(lane < HEADS, jnp.sum(da * dt, axis=0, keepdims=True) * A, 0.0)
        par_ref[...] += _rows_to_block([jnp.sum(ddtp, axis=0, keepdims=True), dalog, dD], 8, DT_PAD)

    vec = pl.BlockSpec((1, DT_PAD), lambda c: (0, 0))
    rev = lambda c: nc - 1 - c
    return pl.pallas_call(
        body, name="ssd_bwd", grid=(nc,),
        in_specs=[pl.BlockSpec((CHUNK, AW), lambda c: (rev(c), 0)), pl.BlockSpec((CHUNK, NS), lambda c: (rev(c), 4)),
                  pl.BlockSpec((CHUNK, NS), lambda c: (rev(c), 5)), pl.BlockSpec((CHUNK, DT_PAD), lambda c: (rev(c), 0)),
                  vec, vec, vec,
                  pl.BlockSpec((None, AW, NS), lambda c: (rev(c), 0, 0)), pl.BlockSpec((CHUNK, AW), lambda c: (rev(c), 0))],
        out_specs=[pl.BlockSpec((CHUNK, AW), lambda c: (rev(c), 0)), pl.BlockSpec((CHUNK, NS), lambda c: (rev(c), 0)),
                   pl.BlockSpec((CHUNK, NS), lambda c: (rev(c), 0)), pl.BlockSpec((CHUNK, DT_PAD), lambda c: (rev(c), 0)),
                   pl.BlockSpec((8, DT_PAD), lambda c: (0, 0))],
        out_shape=[jax.ShapeDtypeStruct((T, AW), F32), jax.ShapeDtypeStruct((T, NS), F32),
                   jax.ShapeDtypeStruct((T, NS), F32), jax.ShapeDtypeStruct((T, DT_PAD), F32),
                   jax.ShapeDtypeStruct((8, DT_PAD), F32)],
        scratch_shapes=[pltpu.VMEM((AW, NS), F32)],
        compiler_params=_cparams(("arbitrary",)),
    )(act, act, act, dtw, bias, alog, dsk, states, dy)


def _place():
    return lax.axis_index("x"), lax.axis_index("y"), lax.axis_index("c")


def _slot(px, py, pc):
    return 4 * px + 2 * py + pc


def _all_gather(arrs, name):
    na = len(arrs)

    def body(*refs):
        ins, outs = refs[:na], refs[na:2 * na]
        send_sems, recv_sems, local_sems = refs[2 * na:]
        x, y, c = _place()
        me, sib = (x, y, c), (x, y, 1 - c)
        chips = [(1 - x, y), (x, 1 - y), (1 - x, 1 - y)]

        def copy(a, kk, block, to, src=None):
            dst = outs[a].at[_slot(*block)]
            return pltpu.make_async_remote_copy(
                src_ref=dst if src is None else src, dst_ref=dst,
                send_sem=send_sems.at[a, kk], recv_sem=recv_sems.at[a, kk], device_id=to, device_id_type=MESH)

        mine = [pltpu.make_async_copy(ins[a], outs[a].at[_slot(*me)], local_sems.at[a]) for a in range(na)]
        for cp in mine:
            cp.start()
        first = []
        for a in range(na):
            first.append(copy(a, 0, me, sib, src=ins[a]))
            first += [copy(a, 1 + j, me, (*chip, c), src=ins[a]) for j, chip in enumerate(chips)]
        for cp in first:
            cp.start()
        passed = []
        for j, chip in enumerate(chips):
            for a in range(na):
                copy(a, 1 + j, (*chip, c), me).wait_recv()
                fw = copy(a, 4 + j, (*chip, c), sib)
                fw.start()
                passed.append(fw)
        for a in range(na):
            copy(a, 0, sib, me).wait_recv()
            for j, chip in enumerate(chips):
                copy(a, 4 + j, (*chip, 1 - c), me).wait_recv()
        for cp in first + passed:
            cp.wait_send()
        for cp in mine:
            cp.wait()

    any_spec = pl.BlockSpec(memory_space=pl.ANY)
    return pl.pallas_call(
        body, name=name,
        in_specs=[any_spec] * na, out_specs=[any_spec] * na,
        out_shape=[jax.ShapeDtypeStruct((N_DEV,) + a.shape, a.dtype) for a in arrs],
        scratch_shapes=[pltpu.SemaphoreType.DMA((na, 7)), pltpu.SemaphoreType.DMA((na, 7)),
                        pltpu.SemaphoreType.DMA((na,))],
    )(*arrs)


def _reduce_scatter(part, name):
    _, r, C = part.shape

    def body(part_ref, out_ref, own, got_sib, chip_sum, got_ici, lsem, s1, r1, s2, r2):
        x, y, c = _place()
        chips = [(x, y), (1 - x, y), (x, 1 - y), (1 - x, 1 - y)]
        loc = [pltpu.make_async_copy(part_ref.at[_slot(*chips[kk], c)], own.at[kk], lsem.at[kk]) for kk in range(4)]
        d2d = [pltpu.make_async_remote_copy(
            src_ref=part_ref.at[_slot(*chips[kk], 1 - c)], dst_ref=got_sib.at[kk],
            send_sem=s1.at[kk], recv_sem=r1.at[kk], device_id=(x, y, 1 - c), device_id_type=MESH) for kk in range(4)]
        for cp in loc + d2d:
            cp.start()
        ici = [pltpu.make_async_remote_copy(
            src_ref=chip_sum.at[kk - 1], dst_ref=got_ici.at[kk - 1],
            send_sem=s2.at[kk - 1], recv_sem=r2.at[kk - 1], device_id=(*chips[kk], c), device_id_type=MESH)
            for kk in range(1, 4)]
        for kk in (1, 2, 3):
            loc[kk].wait()
            d2d[kk].wait_recv()
            chip_sum[kk - 1] = (own[kk].astype(F32) + got_sib[kk].astype(F32)).astype(BF16)
            ici[kk - 1].start()
        loc[0].wait()
        d2d[0].wait_recv()
        acc = own[0].astype(F32) + got_sib[0].astype(F32)
        for cp in ici:
            cp.wait_recv()
        out_ref[...] = ((acc + got_ici[0].astype(F32)) + got_ici[1].astype(F32)) + got_ici[2].astype(F32)
        for cp in d2d + ici:
            cp.wait_send()

    return pl.pallas_call(
        body, name=name,
        in_specs=[pl.BlockSpec(memory_space=pl.ANY)],
        out_specs=pl.BlockSpec(memory_space=pltpu.VMEM),
        out_shape=jax.ShapeDtypeStruct((r, C), F32),
        scratch_shapes=[pltpu.VMEM((4, r, C), BF16), pltpu.VMEM((4, r, C), BF16), pltpu.VMEM((3, r, C), BF16),
                        pltpu.VMEM((3, r, C), BF16),
                        pltpu.SemaphoreType.DMA((4,)), pltpu.SemaphoreType.DMA((4,)), pltpu.SemaphoreType.DMA((4,)),
                        pltpu.SemaphoreType.DMA((3,)), pltpu.SemaphoreType.DMA((3,))],
        compiler_params=pltpu.CompilerParams(vmem_limit_bytes=VMEM_LIMIT),
    )(part)


def _all_reduce_small(v, name):
    R, C = v.shape

    def body(v_ref, out_ref, got, send_sems, recv_sems):
        x, y, c = _place()
        mine = _slot(x, y, c)
        copies = []
        for kk in range(1, N_DEV):
            fx, fy, fc = kk >> 2 & 1, kk >> 1 & 1, kk & 1
            peer = (1 - x if fx else x, 1 - y if fy else y, 1 - c if fc else c)
            copies.append(pltpu.make_async_remote_copy(
                src_ref=v_ref, dst_ref=got.at[mine], send_sem=send_sems.at[kk - 1], recv_sem=recv_sems.at[kk - 1],
                device_id=peer, device_id_type=MESH))
        for cp in copies:
            cp.start()
        got[mine] = v_ref[...]
        for cp in copies:
            cp.wait_recv()
        acc = got[0]
        for s in range(1, N_DEV):
            acc = acc + got[s]
        out_ref[...] = acc
        for cp in copies:
            cp.wait_send()

    return pl.pallas_call(
        body, name=name,
        in_specs=[pl.BlockSpec(memory_space=pltpu.VMEM)], out_specs=pl.BlockSpec(memory_space=pltpu.VMEM),
        out_shape=jax.ShapeDtypeStruct((R, C), F32),
        scratch_shapes=[pltpu.VMEM((N_DEV, R, C), F32), pltpu.SemaphoreType.DMA((N_DEV - 1,)),
                        pltpu.SemaphoreType.DMA((N_DEV - 1,))],
    )(v)


_HBM = pl.BlockSpec(memory_space=pltpu.HBM)
_SEM = pl.BlockSpec(memory_space=pltpu.SEMAPHORE)
_EFFECT = pltpu.SideEffectType.DATAFLOW_SIDE_EFFECTING


def _peers(x, y, c):
    out = []
    for kk in range(1, N_DEV):
        fx, fy, fc = kk >> 2 & 1, kk >> 1 & 1, kk & 1
        out.append((1 - x if fx else x, 1 - y if fy else y, 1 - c if fc else c))
    return out


def _send_start(src, per_peer, name):
    blk = src.shape[1:] if per_peer else src.shape

    def body(src_ref, land_ref, send_sems, recv_sems, src_thru, land_thru):
        x, y, c = _place()
        mine = _slot(x, y, c)
        for kk, peer in enumerate(_peers(x, y, c)):
            pltpu.make_async_remote_copy(
                src_ref=src_ref.at[_slot(*peer)] if per_peer else src_ref, dst_ref=land_ref.at[mine],
                send_sem=send_sems.at[kk], recv_sem=recv_sems.at[kk], device_id=peer, device_id_type=MESH).start()

    land = lax.empty((N_DEV,) + tuple(blk), src.dtype)
    return pl.pallas_call(
        body, name=name,
        out_shape=(pltpu.SemaphoreType.DMA((N_DEV - 1,)), pltpu.SemaphoreType.DMA((N_DEV - 1,)),
                   pltpu.HBM(src.shape, src.dtype), pltpu.HBM(land.shape, land.dtype)),
        in_specs=(_HBM, _HBM), out_specs=(_SEM, _SEM, _HBM, _HBM), input_output_aliases={0: 2, 1: 3},
        compiler_params=pltpu.CompilerParams(has_side_effects=_EFFECT),
    )(pltpu.with_memory_space_constraint(src, pltpu.HBM), pltpu.with_memory_space_constraint(land, pltpu.HBM))


def _send_wait(handles, after, name):
    send_sems, recv_sems, src_thru, land_thru = handles

    def body(src_ref, land_ref, send_sems, recv_sems, after_ref, src_dead, got_ref):
        me = _place()
        for kk in range(N_DEV - 1):
            cp = pltpu.make_async_remote_copy(
                src_ref=land_ref.at[0], dst_ref=land_ref.at[0], send_sem=send_sems.at[kk], recv_sem=recv_sems.at[kk],
                device_id=me, device_id_type=MESH)
            cp.wait_send()
            cp.wait_recv()

    return pl.pallas_call(
        body, name=name,
        out_shape=(pltpu.HBM(src_thru.shape, src_thru.dtype), pltpu.HBM(land_thru.shape, land_thru.dtype)),
        in_specs=(_HBM, _HBM, _SEM, _SEM, pl.BlockSpec(memory_space=pl.ANY)), out_specs=(_HBM, _HBM),
        input_output_aliases={0: 0, 1: 1},
        compiler_params=pltpu.CompilerParams(has_side_effects=_EFFECT),
    )(src_thru, land_thru, send_sems, recv_sems, after)


def _sum_slots(land, name):
    _, R, C = land.shape
    tm = R if R <= 512 else 512

    def body(x_ref, o_ref):
        acc = x_ref[0].astype(F32)
        for j in range(1, N_DEV):
            acc = acc + x_ref[j].astype(F32)
        o_ref[...] = acc

    return pl.pallas_call(
        body, name=name, grid=(R // tm,),
        in_specs=[pl.BlockSpec((N_DEV, tm, C), lambda i: (0, i, 0))], out_specs=pl.BlockSpec((tm, C), lambda i: (i, 0)),
        out_shape=jax.ShapeDtypeStruct((R, C), F32), compiler_params=_cparams(("parallel",)),
    )(land)


def _adamw(w, g, m, v, name):
    R, C = w.shape
    tm = R if R <= 512 else 256

    def fn(w, g, m, v):
        m2 = ADAM_B1 * m + (1.0 - ADAM_B1) * g
        v2 = ADAM_B2 * v + (1.0 - ADAM_B2) * (g * g)
        m_hat = m2 / (1.0 - ADAM_B1 ** ADAM_STEP)
        v_hat = v2 / (1.0 - ADAM_B2 ** ADAM_STEP)
        delta = -ADAM_LR * (m_hat / (jnp.sqrt(v_hat) + ADAM_EPS) + ADAM_WD * w)
        return (delta, m2, v2), ()
    return _rowwise(fn, [w, g, m, v], [], [(C, F32)] * 3, [], tm=tm, name=name)


SMALL = ["norm_mix_pre", "norm_mix_post", "norm_mlp_pre", "norm_mlp_post", "norm_ple_post",
         "conv_b", "ssd_norm_g", "dt_bias", "a_log", "d_skip"]


def _pad_row(v, width=D):
    return jnp.pad(v, ((0, 0), (0, width - v.shape[1])))


def kernel(x, p, positions, norm_mix_pre, norm_mix_post, w_in, conv_w, conv_b, dt_bias, a_log, d_skip, ssd_norm_g, w_out, norm_mlp_pre, norm_mlp_post, w_up, w_down, w_ple_gate, w_ple_proj, norm_ple_post, loss_target, m_norm_mix_pre, m_norm_mix_post, m_w_in, m_conv_w, m_conv_b, m_dt_bias, m_a_log, m_d_skip, m_ssd_norm_g, m_w_out, m_norm_mlp_pre, m_norm_mlp_post, m_w_up, m_w_down, m_w_ple_gate, m_w_ple_proj, m_norm_ple_post, v_norm_mix_pre, v_norm_mix_post, v_w_in, v_conv_w, v_conv_b, v_dt_bias, v_a_log, v_d_skip, v_ssd_norm_g, v_w_out, v_norm_mlp_pre, v_norm_mlp_post, v_w_up, v_w_down, v_w_ple_gate, v_w_ple_proj, v_norm_ple_post):
    args = dict(locals())
    x2, p2, tgt = x[0], p[0, 0], loss_target[0]
    g1, g2, g3, g4, g5 = norm_mix_pre, norm_mix_post, norm_mlp_pre, norm_mlp_post, norm_ple_post

    me = _slot(*_place())
    pack_in = jnp.pad(w_in[0].T, ((0, W_IN_SHARD_PAD - W_IN_SHARD), (0, 0))).astype(BF16)
    pack_rest = jnp.concatenate([
        w_out[0],
        w_up[0].T,
        w_down[0],
        w_ple_gate[0],
        w_ple_proj[0].T.reshape(32, D),
    ], axis=0).astype(BF16)
    rest_handles = _send_start(pack_rest, False, "gather_rest_start")
    conv_pack = jnp.pad(conv_w[0], ((0, 4), (0, 32)))
    gin, gconv = _all_gather([pack_in, conv_pack], "gather_w_in")
    w_inT = gin[:, :W_IN_SHARD].reshape(IN_W, D)
    w_qkvzT = w_inT[:4 * AW]
    w_xbcT = w_inT[4 * AW:4 * AW + CONV_CH]
    w_dtT = jnp.pad(w_inT[4 * AW + CONV_CH:], ((0, DT_PAD - HEADS), (0, 0)))
    conv_full = gconv[:, :CONV_K, :96].transpose(1, 0, 2).reshape(CONV_K, CONV_CH)

    inv_freq = ROPE_THETA ** (-jnp.arange(HD // 2, dtype=F32) * 2.0 / HD)
    ang = positions[0].astype(F32)[:, None] * inv_freq
    cos, sin = jnp.cos(ang), jnp.sin(ang)
    cosf = jnp.tile(jnp.concatenate([cos, cos], axis=1), (1, HEADS))
    sins = jnp.tile(jnp.concatenate([-sin, sin], axis=1), (1, HEADS))

    bias_w, alog_w, dsk_w = _pad_row(dt_bias, DT_PAD), _pad_row(a_log, DT_PAD), _pad_row(d_skip, DT_PAD)
    rms_pre = lambda a, r, g: a * r * g

    (r1,) = _rowwise(lambda a: ((_rstd(a),), ()), [x2], [], [(1, F32)], [], tm=512, name="rstd_x")
    qkvz = _mm(x2, w_qkvzT, tb=True, tm=512, tn=1024, tk=1024, a_pre=rms_pre, a_rows=[r1], a_cols=[g1], name="proj_qkvz")
    xbc = _mm(x2, w_xbcT, tb=True, tm=512, tn=768, tk=1024, a_pre=rms_pre, a_rows=[r1], a_cols=[g1], name="proj_xbc")
    dtw = _mm(x2, w_dtT, tb=True, tm=512, tn=128, tk=1024, a_pre=rms_pre, a_rows=[r1], a_cols=[g1], name="proj_dt")

    qr, kr, vb = _rope_fwd(qkvz, cosf, sins)
    outs, lses = [], []
    for d in DILATIONS:
        L = T // d
        o, l = _attn_fwd(qr.reshape(L, d * AW), kr.reshape(L, d * AW), vb.reshape(L, d * AW), d)
        outs.append(o.reshape(T, AW))
        lses.append(l.reshape(T, AW))
    attn, lse = _attn_merge(outs, lses)

    act = _conv_fwd(xbc, conv_full, conv_b)
    y_ssd, states = _ssd_fwd(act, dtw, bias_w, alog_w, dsk_w)

    def gated_fwd(y, z, a, gs):
        gi = y * (z * _sigmoid(z))
        return (jnp.concatenate([a, gi * _rstd(gi) * gs], axis=1),), ()
    (cat,) = _rowwise(gated_fwd, [y_ssd, (qkvz, AW, 3), attn], [ssd_norm_g], [(D, F32)], [], tm=512, name="gated_norm")

    pack_back, grest = _send_wait(rest_handles, cat, "gather_rest_wait")
    grest = lax.dynamic_update_slice(grest, pack_back[None], (me, 0, 0))
    w_o = grest[:, 0:128].reshape(D, D)
    w_upT = grest[:, 128:640].reshape(DFF, D)
    w_dn = grest[:, 640:1152].reshape(DFF, D)
    w_gate = grest[:, 1152:1280].reshape(D, D)
    w_projT = grest[:, 1280:1312].reshape(D, PLE)

    mix = _mm(cat, w_o, tm=512, tn=1024, tk=1024, name="mix_out")

    def post1(xx, mm, ga, gb):
        h = xx + mm * _rstd(mm) * ga
        return (h, _rstd(h)), ()
    h1, r3 = _rowwise(post1, [x2, mix], [g2, g3], [(D, F32), (1, F32)], [], tm=512, name="post_mix")

    a_up = _mm(h1, w_upT, tb=True, tm=512, tn=1024, tk=1024, a_pre=rms_pre, a_rows=[r3], a_cols=[g3], name="mlp_up")
    relu2 = lambda a: jnp.square(jnp.maximum(a, 0.0))
    ff = _mm(a_up, w_dn, tm=512, tn=1024, tk=1024, a_pre=relu2, name="mlp_down")
    (h2,) = _rowwise(lambda hh, f, g: ((hh + f * _rstd(f) * g,), ()), [h1, ff], [g4], [(D, F32)], [], tm=512, name="post_mlp")

    gp = _mm(h2, w_gate, tm=512, tn=1024, tk=1024, name="ple_gate")
    pp = _mm(p2, w_projT, tb=True, tm=512, tn=1024, tk=256, name="ple_proj")

    def final(hh, gpre, ppv, tg, g):
        sg = _sigmoid(gpre)
        ple = ppv * sg
        r = _rstd(ple)
        n = ple * r
        h3 = hh + n * g
        e = h3 - tg
        dh3 = e * (1.0 / D)
        dple = _rms_bwd(n, r, g, dh3)
        return (dh3, dple * sg, dple * ppv * sg * (1.0 - sg)), (_colsum(dh3 * n), _colsum(0.5 * e * e * (1.0 / D)))
    dh3, dpp, dgp, dg5, loss_vec = _rowwise(final, [h2, gp, pp, tgt], [g5], [(D, F32)] * 3, [(1, D), (1, D)],
                                            tm=256, name="loss_ple_bwd")

    gw_projT = _mm(dpp, p2, ta=True, tm=512, tn=256, tk=1024, out_dtypes=(BF16,), name="gw_ple_proj")
    gw_gate = _mm(h2, dgp, ta=True, tm=512, tn=1024, tk=1024, out_dtypes=(BF16,), name="gw_ple_gate")
    rs_proj = _send_start(gw_projT.reshape(N_DEV, 32, D), True, "rs_start_w_proj")
    rs_gate = _send_start(gw_gate.reshape(N_DEV, 128, D), True, "rs_start_w_gate")
    dh2_g = _mm(dgp, w_gate, tb=True, tm=512, tn=1024, tk=1024, name="dx_ple_gate")

    def bwd_mlp_post(d3, dg_, f, g):
        dh2 = d3 + dg_
        r = _rstd(f)
        n = f * r
        return (dh2, _rms_bwd(n, r, g, dh2)), (_colsum(dh2 * n),)
    dh2, dff, dg4 = _rowwise(bwd_mlp_post, [dh3, dh2_g, ff], [g4], [(D, F32)] * 2, [(1, D)], tm=256, name="bwd_post_mlp")

    gw_dn = _mm(a_up, dff, ta=True, tm=1024, tn=1024, tk=512, a_pre=relu2, out_dtypes=(BF16,), name="gw_mlp_down")
    rs_dn = _send_start(gw_dn.reshape(N_DEV, 512, D), True, "rs_start_w_down")
    da_up = _mm(dff, w_dn, tb=True, tm=512, tn=1024, tk=1024, epi=lambda acc, a: (acc * (2.0 * jnp.maximum(a, 0.0)),),
                epi_tiles=[a_up], name="dx_mlp_down")
    gw_upT = _mm(da_up, h1, ta=True, tm=1024, tn=1024, tk=512, b_pre=rms_pre, b_rows=[r3], b_cols=[g3],
                 out_dtypes=(BF16,), name="gw_mlp_up")
    rs_up = _send_start(gw_upT.reshape(N_DEV, 512, D), True, "rs_start_w_up")
    du2 = _mm(da_up, w_upT, tm=512, tn=1024, tk=1024, name="dx_mlp_up")

    def bwd_mix_post(d2, du, hh, rr, mm, ga, gb):
        n3 = hh * rr
        dh1 = d2 + _rms_bwd(n3, rr, gb, du)
        r = _rstd(mm)
        n2 = mm * r
        return (dh1, _rms_bwd(n2, r, ga, dh1)), (_colsum(du * n3), _colsum(dh1 * n2))
    dh1, dmix, dg3, dg2 = _rowwise(bwd_mix_post, [dh2, du2, h1, r3, mix], [g2, g3], [(D, F32)] * 2, [(1, D), (1, D)],
                                   tm=256, name="bwd_post_mix")

    gw_o = _mm(cat, dmix, ta=True, tm=512, tn=1024, tk=1024, out_dtypes=(BF16,), name="gw_out")
    rs_o = _send_start(gw_o.reshape(N_DEV, 128, D), True, "rs_start_w_out")
    dcat = _mm(dmix, w_o, tb=True, tm=512, tn=1024, tk=1024, name="dx_out")

    def gated_bwd(y, z, dyn, gs):
        sg = _sigmoid(z)
        sz = z * sg
        gi = y * sz
        r = _rstd(gi)
        n = gi * r
        dgi = _rms_bwd(n, r, gs, dyn)
        return (dgi * sz, dgi * y * (sg * (1.0 + z * (1.0 - sg)))), (_colsum(dyn * n),)
    dy_ssd, dz, dgs = _rowwise(gated_bwd, [y_ssd, (qkvz, AW, 3), (dcat, AW, 1)], [ssd_norm_g], [(AW, F32)] * 2, [(1, AW)],
                               tm=512, name="bwd_gated_norm")

    dxs, dBm, dCm, ddtw, ssd_par = _ssd_bwd(act, dtw, bias_w, alog_w, dsk_w, states, dy_ssd)
    dact = jnp.concatenate([dxs, dBm, dCm], axis=1)
    dxbc, conv_par = _conv_bwd(xbc, dact, conv_full, conv_b)

    dattn = dcat[:, :AW]
    dqs, dks, dvs = [], [], []
    for d in DILATIONS:
        L = T // d
        rs = lambda t: t.reshape(L, d * AW)
        dq, dk, dv = _attn_bwd(rs(qr), rs(kr), rs(vb), rs(dattn), rs(attn), rs(lse), d)
        dqs.append(dq.reshape(T, AW))
        dks.append(dk.reshape(T, AW))
        dvs.append(dv.reshape(T, AW))
    dqkvz = _rope_bwd(dqs, dks, dvs, dz, cosf, sins)

    du1a = _mm(dqkvz, w_qkvzT, tm=512, tn=1024, tk=1024, name="dx_qkvz")
    du1b = _mm(dxbc, w_xbcT, tm=512, tn=1024, tk=768, name="dx_xbc")
    du1c = _mm(ddtw, w_dtT, tm=512, tn=1024, tk=128, name="dx_dt")
    pre1 = dict(b_pre=rms_pre, b_rows=[r1], b_cols=[g1], out_dtypes=(BF16,))
    gw_qkvzT = _mm(dqkvz, x2, ta=True, tm=1024, tn=1024, tk=512, name="gw_qkvz", **pre1)
    gw_xbcT = _mm(dxbc, x2, ta=True, tm=768, tn=1024, tk=512, name="gw_xbc", **pre1)
    gw_dtT = _mm(ddtw, x2, ta=True, tm=128, tn=1024, tk=512, name="gw_dt", **pre1)

    def bwd_in(d1, ua, ub, uc, xx, rr, g):
        n = xx * rr
        du = ua + ub + uc
        return (d1 + _rms_bwd(n, rr, g, du),), (_colsum(du * n),)
    grad_x, dg1 = _rowwise(bwd_in, [dh1, du1a, du1b, du1c, x2, r1], [g1], [(D, F32)], [(1, D)], tm=256, name="bwd_pre_mix")

    gw_inT = jnp.concatenate([gw_qkvzT, gw_xbcT, gw_dtT], axis=0)[:IN_W]
    gw_inT = jnp.pad(gw_inT.reshape(N_DEV, W_IN_SHARD, D), ((0, 0), (0, W_IN_SHARD_PAD - W_IN_SHARD), (0, 0)))
    g_inT = _reduce_scatter(gw_inT, "rs_w_in")

    def scatter_finish(handles, nm):
        part, land = _send_wait(handles, g_inT, "rs_wait_" + nm)
        own = lax.dynamic_slice(part, (me, 0, 0), (1,) + part.shape[1:])
        return _sum_slots(lax.dynamic_update_slice(land, own, (me, 0, 0)), "rs_sum_" + nm)
    g_out = scatter_finish(rs_o, "w_out")
    g_upT = scatter_finish(rs_up, "w_up")
    g_dn = scatter_finish(rs_dn, "w_down")
    g_gate = scatter_finish(rs_gate, "w_gate")
    g_projT = scatter_finish(rs_proj, "w_proj")

    small = jnp.concatenate([
        dg1, dg2, dg3, dg4, dg5,
        _pad_row(conv_par[4:5]), _pad_row(dgs), _pad_row(ssd_par[0:1]), _pad_row(ssd_par[1:2]), _pad_row(ssd_par[2:3]),
        _pad_row(conv_par[0:4]), loss_vec, jnp.zeros((1, D), F32),
    ], axis=0)
    small = _all_reduce_small(small, "reduce_small")
    loss = jnp.sum(small[14])
    me = lax.axis_index("x") * 4 + lax.axis_index("y") * 2 + lax.axis_index("c")
    g_conv_w = lax.dynamic_slice(small[10:14, :CONV_CH], (0, me * 96), (CONV_K, 96))

    grads = {
        "w_in": g_inT[:W_IN_SHARD].T[None], "w_out": g_out[None], "w_up": g_upT.T[None], "w_down": g_dn[None],
        "w_ple_gate": g_gate[None], "w_ple_proj": g_projT.reshape(128, PLE).T[None], "conv_w": g_conv_w[None],
        "norm_mix_pre": small[0:1], "norm_mix_post": small[1:2], "norm_mlp_pre": small[2:3], "norm_mlp_post": small[3:4],
        "norm_ple_post": small[4:5], "conv_b": small[5:6, :CONV_CH], "ssd_norm_g": small[6:7, :AW],
        "dt_bias": small[7:8, :HEADS], "a_log": small[8:9, :HEADS], "d_skip": small[9:10, :HEADS],
    }
    delta, new_m, new_v = {}, {}, {}
    for nme in ["w_in", "w_out", "w_up", "w_down", "w_ple_gate", "w_ple_proj"]:
        dl, mm_, vv_ = _adamw(args[nme][0], grads[nme][0], args["m_" + nme][0], args["v_" + nme][0], "adamw_" + nme)
        delta[nme], new_m[nme], new_v[nme] = dl[None], mm_[None], vv_[None]

    def pack_small(prefix):
        rows = [_pad_row(args[prefix + nme]) for nme in SMALL]
        rows.append(_pad_row(args[prefix + "conv_w"][0]))
        rows.append(jnp.zeros((2, D), F32))
        return jnp.concatenate(rows, axis=0)
    g_small = jnp.concatenate([small[0:10], _pad_row(g_conv_w), jnp.zeros((2, D), F32)], axis=0)
    dl, mm_, vv_ = _adamw(pack_small(""), g_small, pack_small("m_"), pack_small("v_"), "adamw_small")
    for i, nme in enumerate(SMALL):
        wdt = args[nme].shape[1]
        delta[nme], new_m[nme], new_v[nme] = dl[i:i + 1, :wdt], mm_[i:i + 1, :wdt], vv_[i:i + 1, :wdt]
    delta["conv_w"], new_m["conv_w"], new_v["conv_w"] = dl[None, 10:14, :96], mm_[None, 10:14, :96], vv_[None, 10:14, :96]

    order = ["norm_mix_pre", "norm_mix_post", "w_in", "conv_w", "conv_b", "dt_bias", "a_log", "d_skip", "ssd_norm_g",
             "w_out", "norm_mlp_pre", "norm_mlp_post", "w_up", "w_down", "w_ple_gate", "w_ple_proj", "norm_ple_post"]
    return (loss, grad_x[None], *[grads[n] for n in order], *[delta[n] for n in order],
            *[new_m[n] for n in order], *[new_v[n] for n in order])
```

```python
import functools
import math

import jax
import jax.numpy as jnp
from jax import lax
from jax.experimental import pallas as pl
from jax.experimental.pallas import tpu as pltpu

F32 = jnp.float32
BF16 = jnp.bfloat16
MESH = pl.DeviceIdType.MESH
HIGHEST = lax.Precision.HIGHEST

N_DEV = 8
T = 4096
D = 1024
HEADS = 8
HD = 64
AW = 512
NS = 128
CONV_K = 4
CONV_CH = 768
CHUNK = 128
DFF = 4096
PLE = 256
EPS = 1e-6
ROPE_THETA = 10000.0
DILATIONS = (1, 4, 16)
QBLK = 128
NEG = -1e30
IN_W = 2824
W_IN_SHARD = 353
W_IN_SHARD_PAD = 384
DT_PAD = 128

ADAM_LR, ADAM_B1, ADAM_B2, ADAM_EPS, ADAM_WD, ADAM_STEP = 0.001, 0.9, 0.999, 1e-08, 0.01, 10

VMEM_LIMIT = 56 * 1024 * 1024


def _cparams(sem=None):
    return pltpu.CompilerParams(dimension_semantics=sem, vmem_limit_bytes=VMEM_LIMIT)


def _dot(a, b, ca, cb, precision=None):
    return lax.dot_general(a, b, (((ca,), (cb,)), ((), ())), preferred_element_type=F32, precision=precision)


def _nn(a, b):
    return _dot(a, b, 1, 0)


def _nt(a, b):
    return _dot(a, b, 1, 1)


def _tn(a, b):
    return _dot(a, b, 0, 0)


def _sigmoid(x):
    return 1.0 / (1.0 + jnp.exp(-x))


def _softplus(x):
    return jnp.maximum(x, 0.0) + jnp.log(1.0 + jnp.exp(-jnp.abs(x)))


def _mm(a, b, *, ta=False, tb=False, tm, tn, tk, name,
        a_pre=None, a_rows=(), a_cols=(), b_pre=None, b_rows=(), b_cols=(),
        epi=None, epi_tiles=(), out_dtypes=(F32,)):
    if ta:
        K, M = a.shape
    else:
        M, K = a.shape
    if tb:
        N, K2 = b.shape
    else:
        K2, N = b.shape
    assert K == K2 and M % tm == 0 and N % tn == 0 and K % tk == 0, (name, a.shape, b.shape)
    nk = K // tk
    if ta:
        a_spec = pl.BlockSpec((tk, tm), lambda i, j, k: (k, i))
        a_row_specs = [pl.BlockSpec((tk, 1), lambda i, j, k: (k, 0)) for _ in a_rows]
        a_col_specs = [pl.BlockSpec((1, tm), lambda i, j, k: (0, i)) for _ in a_cols]
    else:
        a_spec = pl.BlockSpec((tm, tk), lambda i, j, k: (i, k))
        a_row_specs = [pl.BlockSpec((tm, 1), lambda i, j, k: (i, 0)) for _ in a_rows]
        a_col_specs = [pl.BlockSpec((1, tk), lambda i, j, k: (0, k)) for _ in a_cols]
    if tb:
        b_spec = pl.BlockSpec((tn, tk), lambda i, j, k: (j, k))
        b_row_specs = [pl.BlockSpec((tn, 1), lambda i, j, k: (j, 0)) for _ in b_rows]
        b_col_specs = [pl.BlockSpec((1, tk), lambda i, j, k: (0, k)) for _ in b_cols]
    else:
        b_spec = pl.BlockSpec((tk, tn), lambda i, j, k: (k, j))
        b_row_specs = [pl.BlockSpec((tk, 1), lambda i, j, k: (k, 0)) for _ in b_rows]
        b_col_specs = [pl.BlockSpec((1, tn), lambda i, j, k: (0, j)) for _ in b_cols]
    o_spec = pl.BlockSpec((tm, tn), lambda i, j, k: (i, j))
    na, nb, ne, no = len(a_rows) + len(a_cols), len(b_rows) + len(b_cols), len(epi_tiles), len(out_dtypes)

    def body(*refs):
        a_ref, b_ref = refs[0], refs[1]
        a_ex = refs[2:2 + na]
        b_ex = refs[2 + na:2 + na + nb]
        e_ex = refs[2 + na + nb:2 + na + nb + ne]
        outs = refs[2 + na + nb + ne:2 + na + nb + ne + no]
        acc = refs[-1]
        k = pl.program_id(2)

        @pl.when(k == 0)
        def _():
            acc[...] = jnp.zeros_like(acc)

        at = a_ref[...]
        if a_pre is not None:
            at = a_pre(at, *[r[...] for r in a_ex])
        bt = b_ref[...]
        if b_pre is not None:
            bt = b_pre(bt, *[r[...] for r in b_ex])
        acc[...] += _dot(at.astype(BF16), bt.astype(BF16), 0 if ta else 1, 1 if tb else 0)

        @pl.when(k == nk - 1)
        def _():
            res = acc[...]
            vals = epi(res, *[r[...] for r in e_ex]) if epi is not None else (res,)
            for o_ref, val in zip(outs, vals):
                o_ref[...] = val.astype(o_ref.dtype)

    outs = pl.pallas_call(
        body, name=name,
        grid=(M // tm, N // tn, nk),
        in_specs=[a_spec, b_spec] + a_row_specs + a_col_specs + b_row_specs + b_col_specs + [o_spec] * ne,
        out_specs=[o_spec] * no,
        out_shape=[jax.ShapeDtypeStruct((M, N), dt) for dt in out_dtypes],
        scratch_shapes=[pltpu.VMEM((tm, tn), F32)],
        compiler_params=_cparams(("parallel", "parallel", "arbitrary")),
    )(a, b, *a_rows, *a_cols, *b_rows, *b_cols, *epi_tiles)
    return outs[0] if no == 1 else outs


def _rowwise(fn, rows, vecs, out_rows, out_sums, *, tm, name):
    specs, arrs = [], []
    R = None
    for r in rows:
        if isinstance(r, tuple):
            arr, width, cb = r
            specs.append(pl.BlockSpec((tm, width), lambda i, cb=cb: (i, cb)))
        else:
            arr = r
            specs.append(pl.BlockSpec((tm, arr.shape[1]), lambda i: (i, 0)))
        R = arr.shape[0] if R is None else R
        assert arr.shape[0] == R, name
        arrs.append(arr)
    assert R % tm == 0, name
    for v in vecs:
        specs.append(pl.BlockSpec(v.shape, lambda i: (0, 0)))
        arrs.append(v)
    nr, nv, no, ns = len(rows), len(vecs), len(out_rows), len(out_sums)
    out_specs = [pl.BlockSpec((tm, w), lambda i: (i, 0)) for w, _ in out_rows]
    out_specs += [pl.BlockSpec(s, lambda i: (0, 0)) for s in out_sums]
    out_shape = [jax.ShapeDtypeStruct((R, w), dt) for w, dt in out_rows]
    out_shape += [jax.ShapeDtypeStruct(s, F32) for s in out_sums]

    def body(*refs):
        ins = [r[...] for r in refs[:nr + nv]]
        o_refs = refs[nr + nv:nr + nv + no]
        s_refs = refs[nr + nv + no:]
        o_vals, s_vals = fn(*ins)
        for ref, val in zip(o_refs, o_vals):
            ref[...] = val.astype(ref.dtype)
        if ns:
            @pl.when(pl.program_id(0) == 0)
            def _():
                for ref in s_refs:
                    ref[...] = jnp.zeros_like(ref)
            for ref, val in zip(s_refs, s_vals):
                ref[...] += val

    outs = pl.pallas_call(
        body, name=name, grid=(R // tm,), in_specs=specs, out_specs=out_specs, out_shape=out_shape,
        compiler_params=_cparams(("arbitrary",) if ns else ("parallel",)),
    )(*arrs)
    return outs


def _colsum(x):
    return jnp.sum(x, axis=0, keepdims=True)


def _rstd(x):
    return lax.rsqrt(jnp.mean(x * x, axis=-1, keepdims=True) + EPS)


def _rms_bwd(xn, r, g, dy):
    dn = dy * g
    return r * (dn - xn * jnp.mean(dn * xn, axis=-1, keepdims=True))


def _partner(t):
    parts = []
    for s in range(t.shape[1] // 128):
        ts = t[:, 128 * s:128 * (s + 1)]
        lane = lax.broadcasted_iota(jnp.int32, ts.shape, 1)
        up = pltpu.roll(ts, 96, 1)
        down = pltpu.roll(ts, 32, 1)
        parts.append(jnp.where((lane % 64) < 32, up, down))
    return jnp.concatenate(parts, axis=1)


def _rope_fwd(qkvz, cosf, sins):
    def fn(q, k, v, c, s):
        qr = (q * c + _partner(q) * s) * (HD ** -0.5)
        kr = k * c + _partner(k) * s
        return (qr, kr, v), ()
    return _rowwise(fn, [(qkvz, AW, 0), (qkvz, AW, 1), (qkvz, AW, 2), cosf, sins], [],
                    [(AW, BF16), (AW, BF16), (AW, BF16)], [], tm=512, name="rope_fwd")


def _rope_bwd(dqs, dks, dvs, dz, cosf, sins):
    def fn(q1, q2, q3, k1, k2, k3, v1, v2, v3, z, c, s):
        dqr = (q1 + q2 + q3) * (HD ** -0.5)
        dkr = k1 + k2 + k3
        dq = dqr * c + _partner(dqr * s)
        dk = dkr * c + _partner(dkr * s)
        return (jnp.concatenate([dq, dk, v1 + v2 + v3, z], axis=1),), ()
    return _rowwise(fn, [*dqs, *dks, *dvs, dz, cosf, sins], [], [(4 * AW, F32)], [], tm=256, name="rope_bwd")[0]


def _band_masks():
    qi = lax.broadcasted_iota(jnp.int32, (QBLK, QBLK), 0)
    kj = lax.broadcasted_iota(jnp.int32, (QBLK, QBLK), 1)
    return kj >= qi, kj <= qi


def _attn_fwd(q, k, v, d):
    L = q.shape[0]
    nb = L // QBLK

    def body(q_ref, kp_ref, kc_ref, vp_ref, vc_ref, o_ref, l_ref):
        n = pl.program_id(1)
        mask_p, mask_c = _band_masks()
        bias = jnp.concatenate([jnp.where(mask_p, 0.0, NEG) + jnp.where(n > 0, 0.0, NEG),
                                jnp.where(mask_c, 0.0, NEG)], axis=1)
        s = []
        for h in range(HEADS):
            sl = pl.ds(HD * h, HD)
            qh = q_ref[:, sl]
            s.append(jnp.concatenate([_nt(qh, kp_ref[:, sl]), _nt(qh, kc_ref[:, sl])], axis=1))
        s = jnp.stack(s) + bias
        m = jnp.max(s, axis=2, keepdims=True)
        e = jnp.exp(s - m)
        den = jnp.sum(e, axis=2, keepdims=True)
        p = (e * (1.0 / den)).astype(BF16)
        lse = m + jnp.log(den)
        for h in range(HEADS):
            sl = pl.ds(HD * h, HD)
            o_ref[:, sl] = _nn(p[h, :, :QBLK], vp_ref[:, sl]) + _nn(p[h, :, QBLK:], vc_ref[:, sl])
            l_ref[:, sl] = jnp.broadcast_to(lse[h], (QBLK, HD))

    cur = pl.BlockSpec((QBLK, AW), lambda r, n: (n, r))
    prev = pl.BlockSpec((QBLK, AW), lambda r, n: (jnp.maximum(n - 1, 0), r))
    return pl.pallas_call(
        body, name=f"attn_fwd_d{d}", grid=(d, nb),
        in_specs=[cur, prev, cur, prev, cur], out_specs=[cur, cur],
        out_shape=[jax.ShapeDtypeStruct((L, d * AW), F32)] * 2,
        compiler_params=_cparams(("parallel", "parallel")),
    )(q, k, k, v, v)


def _attn_bwd(q, k, v, do, at, lse, d):
    L = q.shape[0]
    nb = L // QBLK

    def body(q0_ref, q1_ref, kp_ref, kc_ref, vp_ref, vc_ref, do0_ref, do1_ref, at0_ref, at1_ref,
             l0_ref, l1_ref, dq_ref, dk_ref, dv_ref):
        n = pl.program_id(1)
        mask_p, mask_c = _band_masks()
        prev_bias = jnp.where(mask_p, 0.0, NEG)
        bias = jnp.concatenate([prev_bias + jnp.where(n > 0, 0.0, NEG), jnp.where(mask_c, 0.0, NEG),
                                prev_bias + jnp.where(n < nb - 1, 0.0, NEG)], axis=1)
        s, dp, ls, dl, ops = [], [], [], [], []
        for h in range(HEADS):
            sl = pl.ds(HD * h, HD)
            one = pl.ds(HD * h, 1)
            q0, q1 = q0_ref[:, sl], q1_ref[:, sl]
            kp, kc, vp, vc = kp_ref[:, sl], kc_ref[:, sl], vp_ref[:, sl], vc_ref[:, sl]
            do0, do1 = do0_ref[:, sl], do1_ref[:, sl]
            do0b, do1b = do0.astype(BF16), do1.astype(BF16)
            s.append(jnp.concatenate([_nt(q0, kp), _nt(q0, kc), _nt(q1, kc)], axis=1))
            dp.append(jnp.concatenate([_nt(do0b, vp), _nt(do0b, vc), _nt(do1b, vc)], axis=1))
            dl0 = jnp.sum(do0 * at0_ref[:, sl], axis=1, keepdims=True)
            dl1 = jnp.sum(do1 * at1_ref[:, sl], axis=1, keepdims=True)
            dl.append(jnp.concatenate([jnp.broadcast_to(dl0, (QBLK, 2 * QBLK)), jnp.broadcast_to(dl1, (QBLK, QBLK))], axis=1))
            ls.append(jnp.concatenate([jnp.broadcast_to(l0_ref[:, one], (QBLK, 2 * QBLK)),
                                       jnp.broadcast_to(l1_ref[:, one], (QBLK, QBLK))], axis=1))
            ops.append((q0, q1, kp, kc, do0b, do1b))
        p = jnp.exp(jnp.stack(s) + bias - jnp.stack(ls))
        ds = (p * (jnp.stack(dp) - jnp.stack(dl))).astype(BF16)
        p = p.astype(BF16)
        for h in range(HEADS):
            sl = pl.ds(HD * h, HD)
            q0, q1, kp, kc, do0b, do1b = ops[h]
            dq_ref[:, sl] = _nn(ds[h, :, :QBLK], kp) + _nn(ds[h, :, QBLK:2 * QBLK], kc)
            dv_ref[:, sl] = _tn(p[h, :, QBLK:2 * QBLK], do0b) + _tn(p[h, :, 2 * QBLK:], do1b)
            dk_ref[:, sl] = _tn(ds[h, :, QBLK:2 * QBLK], q0) + _tn(ds[h, :, 2 * QBLK:], q1)

    cur = pl.BlockSpec((QBLK, AW), lambda r, n: (n, r))
    prev = pl.BlockSpec((QBLK, AW), lambda r, n: (jnp.maximum(n - 1, 0), r))
    nxt = pl.BlockSpec((QBLK, AW), lambda r, n: (jnp.minimum(n + 1, nb - 1), r))
    return pl.pallas_call(
        body, name=f"attn_bwd_d{d}", grid=(d, nb),
        in_specs=[cur, nxt, prev, cur, prev, cur, cur, nxt, cur, nxt, cur, nxt], out_specs=[cur, cur, cur],
        out_shape=[jax.ShapeDtypeStruct((L, d * AW), F32)] * 3,
        compiler_params=_cparams(("parallel", "parallel")),
    )(q, q, k, k, v, v, do, do, at, at, lse, lse)


def _attn_merge(outs, lses):
    def fn(o1, o2, o3, l1, l2, l3):
        m = jnp.maximum(jnp.maximum(l1, l2), l3)
        e1, e2, e3 = jnp.exp(l1 - m), jnp.exp(l2 - m), jnp.exp(l3 - m)
        s = e1 + e2 + e3
        inv = 1.0 / s
        return ((e1 * inv) * o1 + (e2 * inv) * o2 + (e3 * inv) * o3, m + jnp.log(s)), ()
    return _rowwise(fn, [*outs, *lses], [], [(AW, F32), (AW, F32)], [], tm=512, name="attn_merge")


CONV_TM = 512
HALO = 8


def _conv_pre(ext, w, b):
    y = b + w[3] * ext
    for kk in range(1, CONV_K):
        y = y + w[3 - kk] * pltpu.roll(ext, kk, 0)
    return y


def _rows_to_block(rows, n, width):
    ri = lax.broadcasted_iota(jnp.int32, (n, width), 0)
    out = jnp.zeros((n, width), F32)
    for j, r in enumerate(rows):
        out = out + jnp.where(ri == j, r, 0.0)
    return out


def _conv_fwd(xbc, w, b):
    nblk = T // CONV_TM

    def body(x_ref, h_ref, w_ref, b_ref, o_ref):
        i = pl.program_id(0)
        halo = jnp.where(i > 0, h_ref[...], 0.0)
        ext = jnp.concatenate([halo, x_ref[...]], axis=0)
        y = _conv_pre(ext, [w_ref[pl.ds(j, 1), :] for j in range(CONV_K)], b_ref[...])[HALO:]
        o_ref[...] = y * _sigmoid(y)

    return pl.pallas_call(
        body, name="conv_fwd", grid=(nblk,),
        in_specs=[pl.BlockSpec((CONV_TM, CONV_CH), lambda i: (i, 0)),
                  pl.BlockSpec((HALO, CONV_CH), lambda i: (jnp.maximum(i * (CONV_TM // HALO) - 1, 0), 0)),
                  pl.BlockSpec((CONV_K, CONV_CH), lambda i: (0, 0)),
                  pl.BlockSpec((1, CONV_CH), lambda i: (0, 0))],
        out_specs=pl.BlockSpec((CONV_TM, CONV_CH), lambda i: (i, 0)),
        out_shape=jax.ShapeDtypeStruct((T, CONV_CH), F32),
        compiler_params=_cparams(("parallel",)),
    )(xbc, xbc, w, b)


def _conv_bwd(xbc, dact, ddt, w, b):
    nblk = T // CONV_TM
    per = CONV_TM // HALO

    def body(x_ref, xb_ref, xa_ref, g_ref, ga_ref, ddt_ref, w_ref, b_ref, dx_ref, dw_ref):
        i = pl.program_id(0)
        wv = [w_ref[pl.ds(j, 1), :] for j in range(CONV_K)]
        before = jnp.where(i > 0, xb_ref[...], 0.0)
        last = i == nblk - 1
        after = jnp.where(last, 0.0, xa_ref[...])
        g_after = jnp.where(last, 0.0, ga_ref[...])
        ext = jnp.concatenate([before, x_ref[...], after], axis=0)
        y = _conv_pre(ext, wv, b_ref[...])[HALO:]
        sg = _sigmoid(y)
        dy = jnp.concatenate([g_ref[...], g_after], axis=0) * (sg * (1.0 + y * (1.0 - sg)))
        n = CONV_TM + HALO
        dx = wv[3] * dy
        for kk in range(1, CONV_K):
            dx = dx + wv[3 - kk] * pltpu.roll(dy, n - kk, 0)
        dx_ref[:, pl.ds(0, CONV_CH)] = dx[:CONV_TM]
        dx_ref[:, pl.ds(CONV_CH, DT_PAD)] = ddt_ref[...]
        dyc = dy[:CONV_TM]
        rows = [jnp.sum(dyc * (pltpu.roll(ext, 3 - j, 0) if j < 3 else ext)[HALO:HALO + CONV_TM], axis=0, keepdims=True)
                for j in range(CONV_K)]
        rows.append(jnp.sum(dyc, axis=0, keepdims=True))
        part = _rows_to_block(rows, 8, CONV_CH)

        @pl.when(i == 0)
        def _():
            dw_ref[...] = jnp.zeros_like(dw_ref)
        dw_ref[...] += part

    blk = pl.BlockSpec((CONV_TM, CONV_CH), lambda i: (i, 0))
    hb = pl.BlockSpec((HALO, CONV_CH), lambda i: (jnp.maximum(i * per - 1, 0), 0))
    ha = pl.BlockSpec((HALO, CONV_CH), lambda i: (jnp.minimum((i + 1) * per, T // HALO - 1), 0))
    return pl.pallas_call(
        body, name="conv_bwd", grid=(nblk,),
        in_specs=[blk, hb, ha, blk, ha, pl.BlockSpec((CONV_TM, DT_PAD), lambda i: (i, 0)),
                  pl.BlockSpec((CONV_K, CONV_CH), lambda i: (0, 0)), pl.BlockSpec((1, CONV_CH), lambda i: (0, 0))],
        out_specs=[pl.BlockSpec((CONV_TM, CONV_CH + DT_PAD), lambda i: (i, 0)), pl.BlockSpec((8, CONV_CH), lambda i: (0, 0))],
        out_shape=[jax.ShapeDtypeStruct((T, CONV_CH + DT_PAD), F32), jax.ShapeDtypeStruct((8, CONV_CH), F32)],
        compiler_params=_cparams(("arbitrary",)),
    )(xbc, xbc, xbc, dact, dact, ddt, w, b)


def _pick(mat, h):
    lane = lax.broadcasted_iota(jnp.int32, mat.shape, 1)
    return jnp.sum(jnp.where(lane == h, mat, 0.0), axis=1, keepdims=True)


def _heads(fn):
    return jnp.stack([fn(h) for h in range(HEADS)])


def _ssd_prep(dt_ref, bias_ref, alog_ref, dsk_ref, b_ref, c_ref, xs_ref, state_ref, cst):
    li = lax.broadcasted_iota(jnp.int32, (CHUNK, CHUNK), 0)
    si = lax.broadcasted_iota(jnp.int32, (CHUNK, CHUNK), 1)
    tri = li >= si
    dtp = dt_ref[...] + bias_ref[...]
    dt = _softplus(dtp)
    A = -jnp.exp(alog_ref[...])
    a = dt * A
    cs = jnp.dot(tri.astype(F32), a, precision=HIGHEST, preferred_element_type=F32)
    cst[...] = cs.T
    Bm = b_ref[...].astype(BF16)
    Cm = c_ref[...].astype(BF16)
    cb = _nt(Cm, Bm)
    dskv = dsk_ref[...]
    cs_col = _heads(lambda h: _pick(cs, h))
    cs_row = _heads(lambda h: cst[pl.ds(h, 1), :])
    dt_col = _heads(lambda h: _pick(dt, h))
    dsk_col = _heads(lambda h: _pick(dskv, h))
    lam = jnp.exp(jnp.where(tri, cs_col - cs_row, NEG))
    x = _heads(lambda h: xs_ref[:, pl.ds(HD * h, HD)])
    xdt = x * dt_col
    prev = _heads(lambda h: state_ref[pl.ds(HD * h, HD), :])
    lane = lax.broadcasted_iota(jnp.int32, (1, 1, CHUNK), 2)
    cl = jnp.sum(jnp.where(lane == CHUNK - 1, cs_row, 0.0), axis=2, keepdims=True)
    f = jnp.exp(cl - cs_col)
    return dict(li=li, si=si, dtp=dtp, dt=dt, A=A, Bm=Bm, Cm=Cm, cb=cb, cs_col=cs_col, dt_col=dt_col, dsk_col=dsk_col,
                lam=lam, x=x, xdt=xdt, prev=prev, cl=cl, f=f)


def _ssd_fwd(act, xbcdt, bias, alog, dsk):
    nc = T // CHUNK

    def body(xs_ref, b_ref, c_ref, dt_ref, bias_ref, alog_ref, dsk_ref, y_ref, st_ref, state, cst):
        @pl.when(pl.program_id(0) == 0)
        def _():
            state[...] = jnp.zeros_like(state)
        st_ref[...] = state[...]
        s = _ssd_prep(dt_ref, bias_ref, alog_ref, dsk_ref, b_ref, c_ref, xs_ref, state, cst)
        Bm, Cm, prev = s["Bm"], s["Cm"], s["prev"]
        g = (s["cb"] * s["lam"]).astype(BF16)
        xdtb = s["xdt"].astype(BF16)
        prevb = prev.astype(BF16)
        y = _heads(lambda h: _nn(g[h], xdtb[h])) + _heads(lambda h: _nt(Cm, prevb[h])) * jnp.exp(s["cs_col"])
        y = y + s["dsk_col"] * s["x"]
        xf = (s["xdt"] * s["f"]).astype(BF16)
        new = prev * jnp.exp(s["cl"]) + _heads(lambda h: _tn(xf[h], Bm))
        for h in range(HEADS):
            y_ref[:, pl.ds(HD * h, HD)] = y[h]
            state[pl.ds(HD * h, HD), :] = new[h]

    vec = pl.BlockSpec((1, DT_PAD), lambda c: (0, 0))
    return pl.pallas_call(
        body, name="ssd_fwd", grid=(nc,),
        in_specs=[pl.BlockSpec((CHUNK, AW), lambda c: (c, 0)), pl.BlockSpec((CHUNK, NS), lambda c: (c, 4)),
                  pl.BlockSpec((CHUNK, NS), lambda c: (c, 5)), pl.BlockSpec((CHUNK, DT_PAD), lambda c: (c, 6)),
                  vec, vec, vec],
        out_specs=[pl.BlockSpec((CHUNK, AW), lambda c: (c, 0)), pl.BlockSpec((None, AW, NS), lambda c: (c, 0, 0))],
        out_shape=[jax.ShapeDtypeStruct((T, AW), F32), jax.ShapeDtypeStruct((nc, AW, NS), F32)],
        scratch_shapes=[pltpu.VMEM((AW, NS), F32), pltpu.VMEM((CHUNK, CHUNK), F32)],
        compiler_params=_cparams(("arbitrary",)),
    )(act, act, act, xbcdt, bias, alog, dsk)


def _ssd_bwd(act, xbcdt, bias, alog, dsk, states, dy):
    nc = T // CHUNK

    def body(xs_ref, b_ref, c_ref, dt_ref, bias_ref, alog_ref, dsk_ref, st_ref, dy_ref,
             dact_ref, ddt_ref, par_ref, dstate, cst):
        step = pl.program_id(0)

        @pl.when(step == 0)
        def _():
            dstate[...] = jnp.zeros_like(dstate)
            par_ref[...] = jnp.zeros_like(par_ref)
        s = _ssd_prep(dt_ref, bias_ref, alog_ref, dsk_ref, b_ref, c_ref, xs_ref, st_ref, cst)
        Bm, Cm, prev, lam, x, xdt, f, cl = s["Bm"], s["Cm"], s["prev"], s["lam"], s["x"], s["xdt"], s["f"], s["cl"]
        lane = lax.broadcasted_iota(jnp.int32, (1, DT_PAD), 1)
        row = lax.broadcasted_iota(jnp.int32, (1, CHUNK, 1), 1)
        g = s["cb"] * lam
        gb, xdtb, prevb = g.astype(BF16), xdt.astype(BF16), prev.astype(BF16)
        dy = _heads(lambda h: dy_ref[:, pl.ds(HD * h, HD)])
        dyb = dy.astype(BF16)
        dnew = _heads(lambda h: dstate[pl.ds(HD * h, HD), :])
        dnewb = dnew.astype(BF16)
        E = jnp.exp(s["cs_col"])
        ecl = jnp.exp(cl)
        dG = _heads(lambda h: _nt(dyb[h], xdtb[h]))
        dxdt = _heads(lambda h: _tn(gb[h], dyb[h]))
        Yo = _heads(lambda h: _nt(Cm, prevb[h]))
        W = _heads(lambda h: _nt(Bm, dnewb[h]))
        dcb = jnp.sum(dG * lam, axis=0)
        Mm = dG * g
        col_sums = jnp.sum(Mm, axis=1, keepdims=True)
        dYo = (dy * E).astype(BF16)
        dxdt = dxdt + W * f
        dF = jnp.sum(W * xdt, axis=2, keepdims=True) * f
        dcl = jnp.sum(dnew * prev, axis=(1, 2), keepdims=True) * ecl + jnp.sum(dF, axis=1, keepdims=True)
        dcs = (jnp.sum(Mm, axis=2, keepdims=True) + jnp.sum(dy * Yo, axis=2, keepdims=True) * E - dF
               + jnp.where(row == CHUNK - 1, dcl, 0.0))
        ddt_x = jnp.sum(dxdt * x, axis=2, keepdims=True)
        dD = jnp.sum(dy * x, axis=(1, 2), keepdims=True)
        dx = s["dsk_col"] * dy + dxdt * s["dt_col"]
        xfb = (xdt * f).astype(BF16)
        dprev = _heads(lambda h: _tn(dYo[h], Cm)) + dnew * ecl
        dcbb = dcb.astype(BF16)
        dC = _nn(dcbb, Bm)
        dB = _tn(dcbb, Cm)
        dcs_mat = -_rows_to_block([col_sums[h] for h in range(HEADS)], CHUNK, CHUNK).T
        ddt_mat = jnp.zeros((CHUNK, DT_PAD), F32)
        dD_row = jnp.zeros((1, DT_PAD), F32)
        for h in range(HEADS):
            sl = pl.ds(HD * h, HD)
            dC = dC + _nn(dYo[h], prevb[h])
            dB = dB + _nn(xfb[h], dnewb[h])
            dcs_mat = dcs_mat + jnp.where(lane == h, dcs[h], 0.0)
            ddt_mat = ddt_mat + jnp.where(lane == h, ddt_x[h], 0.0)
            dD_row = dD_row + jnp.where(lane == h, dD[h], 0.0)
            dact_ref[:, sl] = dx[h]
            dstate[sl, :] = dprev[h]
        dact_ref[:, pl.ds(AW, NS)] = dB
        dact_ref[:, pl.ds(AW + NS, NS)] = dC
        da = jnp.dot((s["li"] <= s["si"]).astype(F32), dcs_mat, precision=HIGHEST, preferred_element_type=F32)
        ddtp = jnp.where(lane < HEADS, (ddt_mat + da * s["A"]) * _sigmoid(s["dtp"]), 0.0)
        ddt_ref[...] = ddtp
        dalog = jnp.where(lane < HEADS, jnp.sum(da * s["dt"], axis=0, keepdims=True) * s["A"], 0.0)
        par_ref[...] += _rows_to_block([jnp.sum(ddtp, axis=0, keepdims=True), dalog, dD_row], 8, DT_PAD)

    vec = pl.BlockSpec((1, DT_PAD), lambda c: (0, 0))
    rev = lambda c: nc - 1 - c
    return pl.pallas_call(
        body, name="ssd_bwd", grid=(nc,),
        in_specs=[pl.BlockSpec((CHUNK, AW), lambda c: (rev(c), 0)), pl.BlockSpec((CHUNK, NS), lambda c: (rev(c), 4)),
                  pl.BlockSpec((CHUNK, NS), lambda c: (rev(c), 5)), pl.BlockSpec((CHUNK, DT_PAD), lambda c: (rev(c), 6)),
                  vec, vec, vec,
                  pl.BlockSpec((None, AW, NS), lambda c: (rev(c), 0, 0)), pl.BlockSpec((CHUNK, AW), lambda c: (rev(c), 0))],
        out_specs=[pl.BlockSpec((CHUNK, CONV_CH), lambda c: (rev(c), 0)), pl.BlockSpec((CHUNK, DT_PAD), lambda c: (rev(c), 0)),
                   pl.BlockSpec((8, DT_PAD), lambda c: (0, 0))],
        out_shape=[jax.ShapeDtypeStruct((T, CONV_CH), F32), jax.ShapeDtypeStruct((T, DT_PAD), F32),
                   jax.ShapeDtypeStruct((8, DT_PAD), F32)],
        scratch_shapes=[pltpu.VMEM((AW, NS), F32), pltpu.VMEM((CHUNK, CHUNK), F32)],
        compiler_params=_cparams(("arbitrary",)),
    )(act, act, act, xbcdt, bias, alog, dsk, states, dy)


def _place():
    return lax.axis_index("x"), lax.axis_index("y"), lax.axis_index("c")


def _slot(px, py, pc):
    return 4 * px + 2 * py + pc


def _all_gather(arrs, name):
    na = len(arrs)

    def body(*refs):
        ins, outs = refs[:na], refs[na:2 * na]
        send_sems, recv_sems, local_sems = refs[2 * na:]
        x, y, c = _place()
        me, sib = (x, y, c), (x, y, 1 - c)
        chips = [(1 - x, y), (x, 1 - y), (1 - x, 1 - y)]

        def copy(a, kk, block, to, src=None):
            dst = outs[a].at[_slot(*block)]
            return pltpu.make_async_remote_copy(
                src_ref=dst if src is None else src, dst_ref=dst,
                send_sem=send_sems.at[a, kk], recv_sem=recv_sems.at[a, kk], device_id=to, device_id_type=MESH)

        mine = [pltpu.make_async_copy(ins[a], outs[a].at[_slot(*me)], local_sems.at[a]) for a in range(na)]
        for cp in mine:
            cp.start()
        first = []
        for a in range(na):
            first.append(copy(a, 0, me, sib, src=ins[a]))
            first += [copy(a, 1 + j, me, (*chip, c), src=ins[a]) for j, chip in enumerate(chips)]
        for cp in first:
            cp.start()
        passed = []
        for j, chip in enumerate(chips):
            for a in range(na):
                copy(a, 1 + j, (*chip, c), me).wait_recv()
                fw = copy(a, 4 + j, (*chip, c), sib)
                fw.start()
                passed.append(fw)
        for a in range(na):
            copy(a, 0, sib, me).wait_recv()
            for j, chip in enumerate(chips):
                copy(a, 4 + j, (*chip, 1 - c), me).wait_recv()
        for cp in first + passed:
            cp.wait_send()
        for cp in mine:
            cp.wait()

    any_spec = pl.BlockSpec(memory_space=pl.ANY)
    return pl.pallas_call(
        body, name=name,
        in_specs=[any_spec] * na, out_specs=[any_spec] * na,
        out_shape=[jax.ShapeDtypeStruct((N_DEV,) + a.shape, a.dtype) for a in arrs],
        scratch_shapes=[pltpu.SemaphoreType.DMA((na, 7)), pltpu.SemaphoreType.DMA((na, 7)),
                        pltpu.SemaphoreType.DMA((na,))],
    )(*arrs)


def _reduce_scatter(part, name):
    _, r, C = part.shape

    def body(part_ref, out_ref, own, got_sib, chip_sum, got_ici, lsem, s1, r1, s2, r2):
        x, y, c = _place()
        chips = [(x, y), (1 - x, y), (x, 1 - y), (1 - x, 1 - y)]
        loc = [pltpu.make_async_copy(part_ref.at[_slot(*chips[kk], c)], own.at[kk], lsem.at[kk]) for kk in range(4)]
        d2d = [pltpu.make_async_remote_copy(
            src_ref=part_ref.at[_slot(*chips[kk], 1 - c)], dst_ref=got_sib.at[kk],
            send_sem=s1.at[kk], recv_sem=r1.at[kk], device_id=(x, y, 1 - c), device_id_type=MESH) for kk in range(4)]
        for cp in loc + d2d:
            cp.start()
        ici = [pltpu.make_async_remote_copy(
            src_ref=chip_sum.at[kk - 1], dst_ref=got_ici.at[kk - 1],
            send_sem=s2.at[kk - 1], recv_sem=r2.at[kk - 1], device_id=(*chips[kk], c), device_id_type=MESH)
            for kk in range(1, 4)]
        for kk in (1, 2, 3):
            loc[kk].wait()
            d2d[kk].wait_recv()
            chip_sum[kk - 1] = (own[kk].astype(F32) + got_sib[kk].astype(F32)).astype(BF16)
            ici[kk - 1].start()
        loc[0].wait()
        d2d[0].wait_recv()
        acc = own[0].astype(F32) + got_sib[0].astype(F32)
        for cp in ici:
            cp.wait_recv()
        out_ref[...] = ((acc + got_ici[0].astype(F32)) + got_ici[1].astype(F32)) + got_ici[2].astype(F32)
        for cp in d2d + ici:
            cp.wait_send()

    return pl.pallas_call(
        body, name=name,
        in_specs=[pl.BlockSpec(memory_space=pl.ANY)],
        out_specs=pl.BlockSpec(memory_space=pltpu.VMEM),
        out_shape=jax.ShapeDtypeStruct((r, C), F32),
        scratch_shapes=[pltpu.VMEM((4, r, C), BF16), pltpu.VMEM((4, r, C), BF16), pltpu.VMEM((3, r, C), BF16),
                        pltpu.VMEM((3, r, C), BF16),
                        pltpu.SemaphoreType.DMA((4,)), pltpu.SemaphoreType.DMA((4,)), pltpu.SemaphoreType.DMA((4,)),
                        pltpu.SemaphoreType.DMA((3,)), pltpu.SemaphoreType.DMA((3,))],
        compiler_params=pltpu.CompilerParams(vmem_limit_bytes=VMEM_LIMIT),
    )(part)


def _all_reduce_small(v, name):
    R, C = v.shape

    def body(v_ref, out_ref, got, send_sems, recv_sems):
        x, y, c = _place()
        mine = _slot(x, y, c)
        copies = []
        for kk in range(1, N_DEV):
            fx, fy, fc = kk >> 2 & 1, kk >> 1 & 1, kk & 1
            peer = (1 - x if fx else x, 1 - y if fy else y, 1 - c if fc else c)
            copies.append(pltpu.make_async_remote_copy(
                src_ref=v_ref, dst_ref=got.at[mine], send_sem=send_sems.at[kk - 1], recv_sem=recv_sems.at[kk - 1],
                device_id=peer, device_id_type=MESH))
        for cp in copies:
            cp.start()
        got[mine] = v_ref[...]
        for cp in copies:
            cp.wait_recv()
        acc = got[0]
        for s in range(1, N_DEV):
            acc = acc + got[s]
        out_ref[...] = acc
        for cp in copies:
            cp.wait_send()

    return pl.pallas_call(
        body, name=name,
        in_specs=[pl.BlockSpec(memory_space=pltpu.VMEM)], out_specs=pl.BlockSpec(memory_space=pltpu.VMEM),
        out_shape=jax.ShapeDtypeStruct((R, C), F32),
        scratch_shapes=[pltpu.VMEM((N_DEV, R, C), F32), pltpu.SemaphoreType.DMA((N_DEV - 1,)),
                        pltpu.SemaphoreType.DMA((N_DEV - 1,))],
    )(v)


_HBM = pl.BlockSpec(memory_space=pltpu.HBM)
_SEM = pl.BlockSpec(memory_space=pltpu.SEMAPHORE)
_EFFECT = pltpu.SideEffectType.DATAFLOW_SIDE_EFFECTING


def _peers(x, y, c):
    out = []
    for kk in range(1, N_DEV):
        fx, fy, fc = kk >> 2 & 1, kk >> 1 & 1, kk & 1
        out.append((1 - x if fx else x, 1 - y if fy else y, 1 - c if fc else c))
    return out


def _send_start(src, per_peer, name):
    blk = src.shape[1:] if per_peer else src.shape

    def body(src_ref, land_ref, send_sems, recv_sems, src_thru, land_thru):
        x, y, c = _place()
        mine = _slot(x, y, c)
        for kk, peer in enumerate(_peers(x, y, c)):
            pltpu.make_async_remote_copy(
                src_ref=src_ref.at[_slot(*peer)] if per_peer else src_ref, dst_ref=land_ref.at[mine],
                send_sem=send_sems.at[kk], recv_sem=recv_sems.at[kk], device_id=peer, device_id_type=MESH).start()

    land = lax.empty((N_DEV,) + tuple(blk), src.dtype)
    return pl.pallas_call(
        body, name=name,
        out_shape=(pltpu.SemaphoreType.DMA((N_DEV - 1,)), pltpu.SemaphoreType.DMA((N_DEV - 1,)),
                   pltpu.HBM(src.shape, src.dtype), pltpu.HBM(land.shape, land.dtype)),
        in_specs=(_HBM, _HBM), out_specs=(_SEM, _SEM, _HBM, _HBM), input_output_aliases={0: 2, 1: 3},
        compiler_params=pltpu.CompilerParams(has_side_effects=_EFFECT),
    )(pltpu.with_memory_space_constraint(src, pltpu.HBM), pltpu.with_memory_space_constraint(land, pltpu.HBM))


def _send_wait(handles, after, name):
    send_sems, recv_sems, src_thru, land_thru = handles

    def body(src_ref, land_ref, send_sems, recv_sems, after_ref, src_dead, got_ref):
        me = _place()
        for kk in range(N_DEV - 1):
            cp = pltpu.make_async_remote_copy(
                src_ref=land_ref.at[0], dst_ref=land_ref.at[0], send_sem=send_sems.at[kk], recv_sem=recv_sems.at[kk],
                device_id=me, device_id_type=MESH)
            cp.wait_send()
            cp.wait_recv()

    return pl.pallas_call(
        body, name=name,
        out_shape=(pltpu.HBM(src_thru.shape, src_thru.dtype), pltpu.HBM(land_thru.shape, land_thru.dtype)),
        in_specs=(_HBM, _HBM, _SEM, _SEM, pl.BlockSpec(memory_space=pl.ANY)), out_specs=(_HBM, _HBM),
        input_output_aliases={0: 0, 1: 1},
        compiler_params=pltpu.CompilerParams(has_side_effects=_EFFECT),
    )(src_thru, land_thru, send_sems, recv_sems, after)


def _sum_slots(land, name):
    _, R, C = land.shape
    tm = R if R <= 512 else 512

    def body(x_ref, o_ref):
        acc = x_ref[0].astype(F32)
        for j in range(1, N_DEV):
            acc = acc + x_ref[j].astype(F32)
        o_ref[...] = acc

    return pl.pallas_call(
        body, name=name, grid=(R // tm,),
        in_specs=[pl.BlockSpec((N_DEV, tm, C), lambda i: (0, i, 0))], out_specs=pl.BlockSpec((tm, C), lambda i: (i, 0)),
        out_shape=jax.ShapeDtypeStruct((R, C), F32), compiler_params=_cparams(("parallel",)),
    )(land)


def _adamw(w, g, m, v, name):
    R, C = w.shape
    tm = R if R <= 512 else 256

    def fn(w, g, m, v):
        m2 = ADAM_B1 * m + (1.0 - ADAM_B1) * g
        v2 = ADAM_B2 * v + (1.0 - ADAM_B2) * (g * g)
        m_hat = m2 / (1.0 - ADAM_B1 ** ADAM_STEP)
        v_hat = v2 / (1.0 - ADAM_B2 ** ADAM_STEP)
        delta = -ADAM_LR * (m_hat / (jnp.sqrt(v_hat) + ADAM_EPS) + ADAM_WD * w)
        return (delta, m2, v2), ()
    return _rowwise(fn, [w, g, m, v], [], [(C, F32)] * 3, [], tm=tm, name=name)


SMALL = ["norm_mix_pre", "norm_mix_post", "norm_mlp_pre", "norm_mlp_post", "norm_ple_post",
         "conv_b", "ssd_norm_g", "dt_bias", "a_log", "d_skip"]


def _pad_row(v, width=D):
    return jnp.pad(v, ((0, 0), (0, width - v.shape[1])))


def kernel(x, p, positions, norm_mix_pre, norm_mix_post, w_in, conv_w, conv_b, dt_bias, a_log, d_skip, ssd_norm_g, w_out, norm_mlp_pre, norm_mlp_post, w_up, w_down, w_ple_gate, w_ple_proj, norm_ple_post, loss_target, m_norm_mix_pre, m_norm_mix_post, m_w_in, m_conv_w, m_conv_b, m_dt_bias, m_a_log, m_d_skip, m_ssd_norm_g, m_w_out, m_norm_mlp_pre, m_norm_mlp_post, m_w_up, m_w_down, m_w_ple_gate, m_w_ple_proj, m_norm_ple_post, v_norm_mix_pre, v_norm_mix_post, v_w_in, v_conv_w, v_conv_b, v_dt_bias, v_a_log, v_d_skip, v_ssd_norm_g, v_w_out, v_norm_mlp_pre, v_norm_mlp_post, v_w_up, v_w_down, v_w_ple_gate, v_w_ple_proj, v_norm_ple_post):
    args = dict(locals())
    x2, p2, tgt = x[0], p[0, 0], loss_target[0]
    g1, g2, g3, g4, g5 = norm_mix_pre, norm_mix_post, norm_mlp_pre, norm_mlp_post, norm_ple_post

    me = _slot(*_place())
    pack_in = jnp.pad(w_in[0].T, ((0, W_IN_SHARD_PAD - W_IN_SHARD), (0, 0))).astype(BF16)
    pack_rest = jnp.concatenate([
        w_out[0],
        w_up[0].T,
        w_down[0],
        w_ple_gate[0],
        w_ple_proj[0].T.reshape(32, D),
    ], axis=0).astype(BF16)
    rest_handles = _send_start(pack_rest, False, "gather_rest_start")
    conv_pack = jnp.pad(conv_w[0], ((0, 4), (0, 32)))
    gin, gconv = _all_gather([pack_in, conv_pack], "gather_w_in")
    w_inT = gin[:, :W_IN_SHARD].reshape(IN_W, D)
    w_qkvzT = w_inT[:4 * AW]
    w_xbcdtT = jnp.pad(w_inT[4 * AW:], ((0, DT_PAD - HEADS), (0, 0)))
    conv_full = gconv[:, :CONV_K, :96].transpose(1, 0, 2).reshape(CONV_K, CONV_CH)

    inv_freq = ROPE_THETA ** (-jnp.arange(HD // 2, dtype=F32) * 2.0 / HD)
    ang = positions[0].astype(F32)[:, None] * inv_freq
    cos, sin = jnp.cos(ang), jnp.sin(ang)
    cosf = jnp.tile(jnp.concatenate([cos, cos], axis=1), (1, HEADS))
    sins = jnp.tile(jnp.concatenate([-sin, sin], axis=1), (1, HEADS))

    bias_w, alog_w, dsk_w = _pad_row(dt_bias, DT_PAD), _pad_row(a_log, DT_PAD), _pad_row(d_skip, DT_PAD)
    rms_pre = lambda a, r, g: a * r * g

    (r1,) = _rowwise(lambda a: ((_rstd(a),), ()), [x2], [], [(1, F32)], [], tm=512, name="rstd_x")
    qkvz = _mm(x2, w_qkvzT, tb=True, tm=512, tn=1024, tk=1024, a_pre=rms_pre, a_rows=[r1], a_cols=[g1], name="proj_qkvz")
    xbcdt = _mm(x2, w_xbcdtT, tb=True, tm=512, tn=896, tk=1024, a_pre=rms_pre, a_rows=[r1], a_cols=[g1], name="proj_xbcdt")

    qr, kr, vb = _rope_fwd(qkvz, cosf, sins)
    outs, lses = [], []
    for d in DILATIONS:
        L = T // d
        o, l = _attn_fwd(qr.reshape(L, d * AW), kr.reshape(L, d * AW), vb.reshape(L, d * AW), d)
        outs.append(o.reshape(T, AW))
        lses.append(l.reshape(T, AW))
    attn, lse = _attn_merge(outs, lses)

    act = _conv_fwd(xbcdt, conv_full, conv_b)
    y_ssd, states = _ssd_fwd(act, xbcdt, bias_w, alog_w, dsk_w)

    def gated_fwd(y, z, a, gs):
        gi = y * (z * _sigmoid(z))
        return (jnp.concatenate([a, gi * _rstd(gi) * gs], axis=1),), ()
    (cat,) = _rowwise(gated_fwd, [y_ssd, (qkvz, AW, 3), attn], [ssd_norm_g], [(D, F32)], [], tm=512, name="gated_norm")

    pack_back, grest = _send_wait(rest_handles, cat, "gather_rest_wait")
    grest = lax.dynamic_update_slice(grest, pack_back[None], (me, 0, 0))
    w_o = grest[:, 0:128].reshape(D, D)
    w_upT = grest[:, 128:640].reshape(DFF, D)
    w_dn = grest[:, 640:1152].reshape(DFF, D)
    w_gate = grest[:, 1152:1280].reshape(D, D)
    w_projT = grest[:, 1280:1312].reshape(D, PLE)

    mix = _mm(cat, w_o, tm=512, tn=1024, tk=1024, name="mix_out")

    def post1(xx, mm, ga, gb):
        h = xx + mm * _rstd(mm) * ga
        return (h, _rstd(h)), ()
    h1, r3 = _rowwise(post1, [x2, mix], [g2, g3], [(D, F32), (1, F32)], [], tm=512, name="post_mix")

    a_up = _mm(h1, w_upT, tb=True, tm=512, tn=1024, tk=1024, a_pre=rms_pre, a_rows=[r3], a_cols=[g3], name="mlp_up")
    relu2 = lambda a: jnp.square(jnp.maximum(a, 0.0))
    ff = _mm(a_up, w_dn, tm=512, tn=1024, tk=1024, a_pre=relu2, name="mlp_down")
    (h2,) = _rowwise(lambda hh, f, g: ((hh + f * _rstd(f) * g,), ()), [h1, ff], [g4], [(D, F32)], [], tm=512, name="post_mlp")

    gp = _mm(h2, w_gate, tm=512, tn=1024, tk=1024, name="ple_gate")
    pp = _mm(p2, w_projT, tb=True, tm=512, tn=1024, tk=256, name="ple_proj")

    def final(hh, gpre, ppv, tg, g):
        sg = _sigmoid(gpre)
        ple = ppv * sg
        r = _rstd(ple)
        n = ple * r
        h3 = hh + n * g
        e = h3 - tg
        dh3 = e * (1.0 / D)
        dple = _rms_bwd(n, r, g, dh3)
        return (dh3, dple * sg, dple * ppv * sg * (1.0 - sg)), (_colsum(dh3 * n), _colsum(0.5 * e * e * (1.0 / D)))
    dh3, dpp, dgp, dg5, loss_vec = _rowwise(final, [h2, gp, pp, tgt], [g5], [(D, F32)] * 3, [(1, D), (1, D)],
                                            tm=256, name="loss_ple_bwd")

    gw_projT = _mm(dpp, p2, ta=True, tm=512, tn=256, tk=1024, out_dtypes=(BF16,), name="gw_ple_proj")
    gw_gate = _mm(h2, dgp, ta=True, tm=512, tn=1024, tk=1024, out_dtypes=(BF16,), name="gw_ple_gate")
    rs_proj = _send_start(gw_projT.reshape(N_DEV, 32, D), True, "rs_start_w_proj")
    rs_gate = _send_start(gw_gate.reshape(N_DEV, 128, D), True, "rs_start_w_gate")
    dh2_g = _mm(dgp, w_gate, tb=True, tm=512, tn=1024, tk=1024, name="dx_ple_gate")

    def bwd_mlp_post(d3, dg_, f, g):
        dh2 = d3 + dg_
        r = _rstd(f)
        n = f * r
        return (dh2, _rms_bwd(n, r, g, dh2)), (_colsum(dh2 * n),)
    dh2, dff, dg4 = _rowwise(bwd_mlp_post, [dh3, dh2_g, ff], [g4], [(D, F32)] * 2, [(1, D)], tm=256, name="bwd_post_mlp")

    gw_dn = _mm(a_up, dff, ta=True, tm=1024, tn=1024, tk=512, a_pre=relu2, out_dtypes=(BF16,), name="gw_mlp_down")
    rs_dn = _send_start(gw_dn.reshape(N_DEV, 512, D), True, "rs_start_w_down")
    da_up = _mm(dff, w_dn, tb=True, tm=512, tn=1024, tk=1024, epi=lambda acc, a: (acc * (2.0 * jnp.maximum(a, 0.0)),),
                epi_tiles=[a_up], name="dx_mlp_down")
    gw_upT = _mm(da_up, h1, ta=True, tm=1024, tn=1024, tk=512, b_pre=rms_pre, b_rows=[r3], b_cols=[g3],
                 out_dtypes=(BF16,), name="gw_mlp_up")
    rs_up = _send_start(gw_upT.reshape(N_DEV, 512, D), True, "rs_start_w_up")
    du2 = _mm(da_up, w_upT, tm=512, tn=1024, tk=1024, name="dx_mlp_up")

    def bwd_mix_post(d2, du, hh, rr, mm, ga, gb):
        n3 = hh * rr
        dh1 = d2 + _rms_bwd(n3, rr, gb, du)
        r = _rstd(mm)
        n2 = mm * r
        return (dh1, _rms_bwd(n2, r, ga, dh1)), (_colsum(du * n3), _colsum(dh1 * n2))
    dh1, dmix, dg3, dg2 = _rowwise(bwd_mix_post, [dh2, du2, h1, r3, mix], [g2, g3], [(D, F32)] * 2, [(1, D), (1, D)],
                                   tm=256, name="bwd_post_mix")

    gw_o = _mm(cat, dmix, ta=True, tm=512, tn=1024, tk=1024, out_dtypes=(BF16,), name="gw_out")
    rs_o = _send_start(gw_o.reshape(N_DEV, 128, D), True, "rs_start_w_out")
    dcat = _mm(dmix, w_o, tb=True, tm=512, tn=1024, tk=1024, name="dx_out")

    def gated_bwd(y, z, dyn, gs):
        sg = _sigmoid(z)
        sz = z * sg
        gi = y * sz
        r = _rstd(gi)
        n = gi * r
        dgi = _rms_bwd(n, r, gs, dyn)
        return (dgi * sz, dgi * y * (sg * (1.0 + z * (1.0 - sg)))), (_colsum(dyn * n),)
    dy_ssd, dz, dgs = _rowwise(gated_bwd, [y_ssd, (qkvz, AW, 3), (dcat, AW, 1)], [ssd_norm_g], [(AW, F32)] * 2, [(1, AW)],
                               tm=512, name="bwd_gated_norm")

    dact, ddtw, ssd_par = _ssd_bwd(act, xbcdt, bias_w, alog_w, dsk_w, states, dy_ssd)
    dxbcdt, conv_par = _conv_bwd(xbcdt, dact, ddtw, conv_full, conv_b)

    dattn = dcat[:, :AW]
    dqs, dks, dvs = [], [], []
    for d in DILATIONS:
        L = T // d
        rs = lambda t: t.reshape(L, d * AW)
        dq, dk, dv = _attn_bwd(rs(qr), rs(kr), rs(vb), rs(dattn), rs(attn), rs(lse), d)
        dqs.append(dq.reshape(T, AW))
        dks.append(dk.reshape(T, AW))
        dvs.append(dv.reshape(T, AW))
    dqkvz = _rope_bwd(dqs, dks, dvs, dz, cosf, sins)

    du1a = _mm(dqkvz, w_qkvzT, tm=512, tn=1024, tk=1024, name="dx_qkvz")
    du1b = _mm(dxbcdt, w_xbcdtT, tm=512, tn=1024, tk=896, name="dx_xbcdt")
    pre1 = dict(b_pre=rms_pre, b_rows=[r1], b_cols=[g1], out_dtypes=(BF16,))
    gw_qkvzT = _mm(dqkvz, x2, ta=True, tm=1024, tn=1024, tk=512, name="gw_qkvz", **pre1)
    gw_xbcdtT = _mm(dxbcdt, x2, ta=True, tm=896, tn=1024, tk=512, name="gw_xbcdt", **pre1)

    def bwd_in(d1, ua, ub, xx, rr, g):
        n = xx * rr
        du = ua + ub
        return (d1 + _rms_bwd(n, rr, g, du),), (_colsum(du * n),)
    grad_x, dg1 = _rowwise(bwd_in, [dh1, du1a, du1b, x2, r1], [g1], [(D, F32)], [(1, D)], tm=256, name="bwd_pre_mix")

    gw_inT = jnp.concatenate([gw_qkvzT, gw_xbcdtT], axis=0)[:IN_W]
    gw_inT = jnp.pad(gw_inT.reshape(N_DEV, W_IN_SHARD, D), ((0, 0), (0, W_IN_SHARD_PAD - W_IN_SHARD), (0, 0)))
    g_inT = _reduce_scatter(gw_inT, "rs_w_in")

    def scatter_finish(handles, nm):
        part, land = _send_wait(handles, g_inT, "rs_wait_" + nm)
        own = lax.dynamic_slice(part, (me, 0, 0), (1,) + part.shape[1:])
        return _sum_slots(lax.dynamic_update_slice(land, own, (me, 0, 0)), "rs_sum_" + nm)
    g_out = scatter_finish(rs_o, "w_out")
    g_upT = scatter_finish(rs_up, "w_up")
    g_dn = scatter_finish(rs_dn, "w_down")
    g_gate = scatter_finish(rs_gate, "w_gate")
    g_projT = scatter_finish(rs_proj, "w_proj")

    small = jnp.concatenate([
        dg1, dg2, dg3, dg4, dg5,
        _pad_row(conv_par[4:5]), _pad_row(dgs), _pad_row(ssd_par[0:1]), _pad_row(ssd_par[1:2]), _pad_row(ssd_par[2:3]),
        _pad_row(conv_par[0:4]), loss_vec, jnp.zeros((1, D), F32),
    ], axis=0)
    small = _all_reduce_small(small, "reduce_small")
    loss = jnp.sum(small[14])
    me = lax.axis_index("x") * 4 + lax.axis_index("y") * 2 + lax.axis_index("c")
    g_conv_w = lax.dynamic_slice(small[10:14, :CONV_CH], (0, me * 96), (CONV_K, 96))

    grads = {
        "w_in": g_inT[:W_IN_SHARD].T[None], "w_out": g_out[None], "w_up": g_upT.T[None], "w_down": g_dn[None],
        "w_ple_gate": g_gate[None], "w_ple_proj": g_projT.reshape(128, PLE).T[None], "conv_w": g_conv_w[None],
        "norm_mix_pre": small[0:1], "norm_mix_post": small[1:2], "norm_mlp_pre": small[2:3], "norm_mlp_post": small[3:4],
        "norm_ple_post": small[4:5], "conv_b": small[5:6, :CONV_CH], "ssd_norm_g": small[6:7, :AW],
        "dt_bias": small[7:8, :HEADS], "a_log": small[8:9, :HEADS], "d_skip": small[9:10, :HEADS],
    }
    delta, new_m, new_v = {}, {}, {}
    for nme in ["w_in", "w_out", "w_up", "w_down", "w_ple_gate", "w_ple_proj"]:
        dl, mm_, vv_ = _adamw(args[nme][0], grads[nme][0], args["m_" + nme][0], args["v_" + nme][0], "adamw_" + nme)
        delta[nme], new_m[nme], new_v[nme] = dl[None], mm_[None], vv_[None]

    def pack_small(prefix):
        rows = [_pad_row(args[prefix + nme]) for nme in SMALL]
        rows.append(_pad_row(args[prefix + "conv_w"][0]))
        rows.append(jnp.zeros((2, D), F32))
        return jnp.concatenate(rows, axis=0)
    g_small = jnp.concatenate([small[0:10], _pad_row(g_conv_w), jnp.zeros((2, D), F32)], axis=0)
    dl, mm_, vv_ = _adamw(pack_small(""), g_small, pack_small("m_"), pack_small("v_"), "adamw_small")
    for i, nme in enumerate(SMALL):
        wdt = args[nme].shape[1]
        delta[nme], new_m[nme], new_v[nme] = dl[i:i + 1, :wdt], mm_[i:i + 1, :wdt], vv_[i:i + 1, :wdt]
    delta["conv_w"], new_m["conv_w"], new_v["conv_w"] = dl[None, 10:14, :96], mm_[None, 10:14, :96], vv_[None, 10:14, :96]

    order = ["norm_mix_pre", "norm_mix_post", "w_in", "conv_w", "conv_b", "dt_bias", "a_log", "d_skip", "ssd_norm_g",
             "w_out", "norm_mlp_pre", "norm_mlp_post", "w_up", "w_down", "w_ple_gate", "w_ple_proj", "norm_ple_post"]
    return (loss, grad_x[None], *[grads[n] for n in order], *[delta[n] for n in order],
            *[new_m[n] for n in order], *[new_v[n] for n in order])
```

```python
import functools
import math

import jax
import jax.numpy as jnp
from jax import lax
from jax.experimental import pallas as pl
from jax.experimental.pallas import tpu as pltpu

F32 = jnp.float32
BF16 = jnp.bfloat16
MESH = pl.DeviceIdType.MESH
HIGHEST = lax.Precision.HIGHEST

N_DEV = 8
T = 4096
D = 1024
HEADS = 8
HD = 64
AW = 512
NS = 128
CONV_K = 4
CONV_CH = 768
CHUNK = 128
DFF = 4096
PLE = 256
EPS = 1e-6
ROPE_THETA = 10000.0
DILATIONS = (1, 4, 16)
QBLK = 128
NEG = -1e30
IN_W = 2824
W_IN_SHARD = 353
W_IN_SHARD_PAD = 384
DT_PAD = 128

ADAM_LR, ADAM_B1, ADAM_B2, ADAM_EPS, ADAM_WD, ADAM_STEP = 0.001, 0.9, 0.999, 1e-08, 0.01, 10

VMEM_LIMIT = 56 * 1024 * 1024


def _cparams(sem=None):
    return pltpu.CompilerParams(dimension_semantics=sem, vmem_limit_bytes=VMEM_LIMIT)


def _dot(a, b, ca, cb, precision=None):
    return lax.dot_general(a, b, (((ca,), (cb,)), ((), ())), preferred_element_type=F32, precision=precision)


def _nn(a, b):
    return _dot(a, b, 1, 0)


def _nt(a, b):
    return _dot(a, b, 1, 1)


def _tn(a, b):
    return _dot(a, b, 0, 0)


def _sigmoid(x):
    return 1.0 / (1.0 + jnp.exp(-x))


def _softplus(x):
    return jnp.maximum(x, 0.0) + jnp.log(1.0 + jnp.exp(-jnp.abs(x)))


def _mm(a, b, *, ta=False, tb=False, tm, tn, tk, name,
        a_pre=None, a_rows=(), a_cols=(), b_pre=None, b_rows=(), b_cols=(),
        epi=None, epi_tiles=(), out_dtypes=(F32,)):
    if ta:
        K, M = a.shape
    else:
        M, K = a.shape
    if tb:
        N, K2 = b.shape
    else:
        K2, N = b.shape
    assert K == K2 and M % tm == 0 and N % tn == 0 and K % tk == 0, (name, a.shape, b.shape)
    nk = K // tk
    if ta:
        a_spec = pl.BlockSpec((tk, tm), lambda i, j, k: (k, i))
        a_row_specs = [pl.BlockSpec((tk, 1), lambda i, j, k: (k, 0)) for _ in a_rows]
        a_col_specs = [pl.BlockSpec((1, tm), lambda i, j, k: (0, i)) for _ in a_cols]
    else:
        a_spec = pl.BlockSpec((tm, tk), lambda i, j, k: (i, k))
        a_row_specs = [pl.BlockSpec((tm, 1), lambda i, j, k: (i, 0)) for _ in a_rows]
        a_col_specs = [pl.BlockSpec((1, tk), lambda i, j, k: (0, k)) for _ in a_cols]
    if tb:
        b_spec = pl.BlockSpec((tn, tk), lambda i, j, k: (j, k))
        b_row_specs = [pl.BlockSpec((tn, 1), lambda i, j, k: (j, 0)) for _ in b_rows]
        b_col_specs = [pl.BlockSpec((1, tk), lambda i, j, k: (0, k)) for _ in b_cols]
    else:
        b_spec = pl.BlockSpec((tk, tn), lambda i, j, k: (k, j))
        b_row_specs = [pl.BlockSpec((tk, 1), lambda i, j, k: (k, 0)) for _ in b_rows]
        b_col_specs = [pl.BlockSpec((1, tn), lambda i, j, k: (0, j)) for _ in b_cols]
    o_spec = pl.BlockSpec((tm, tn), lambda i, j, k: (i, j))
    na, nb, ne, no = len(a_rows) + len(a_cols), len(b_rows) + len(b_cols), len(epi_tiles), len(out_dtypes)

    def body(*refs):
        a_ref, b_ref = refs[0], refs[1]
        a_ex = refs[2:2 + na]
        b_ex = refs[2 + na:2 + na + nb]
        e_ex = refs[2 + na + nb:2 + na + nb + ne]
        outs = refs[2 + na + nb + ne:2 + na + nb + ne + no]
        acc = refs[-1]
        k = pl.program_id(2)

        @pl.when(k == 0)
        def _():
            acc[...] = jnp.zeros_like(acc)

        at = a_ref[...]
        if a_pre is not None:
            at = a_pre(at, *[r[...] for r in a_ex])
        bt = b_ref[...]
        if b_pre is not None:
            bt = b_pre(bt, *[r[...] for r in b_ex])
        acc[...] += _dot(at.astype(BF16), bt.astype(BF16), 0 if ta else 1, 1 if tb else 0)

        @pl.when(k == nk - 1)
        def _():
            res = acc[...]
            vals = epi(res, *[r[...] for r in e_ex]) if epi is not None else (res,)
            for o_ref, val in zip(outs, vals):
                o_ref[...] = val.astype(o_ref.dtype)

    outs = pl.pallas_call(
        body, name=name,
        grid=(M // tm, N // tn, nk),
        in_specs=[a_spec, b_spec] + a_row_specs + a_col_specs + b_row_specs + b_col_specs + [o_spec] * ne,
        out_specs=[o_spec] * no,
        out_shape=[jax.ShapeDtypeStruct((M, N), dt) for dt in out_dtypes],
        scratch_shapes=[pltpu.VMEM((tm, tn), F32)],
        compiler_params=_cparams(("parallel", "parallel", "arbitrary")),
    )(a, b, *a_rows, *a_cols, *b_rows, *b_cols, *epi_tiles)
    return outs[0] if no == 1 else outs


MLP_TM = 1024
MLP_TC = 512


def _mlp_fwd(h, r, g, w_upT, w_dn):
    nc = DFF // MLP_TC

    def body(h_ref, r_ref, g_ref, wu_ref, wd_ref, a_ref, ff_ref, u_ref, acc, u_scr):
        c = pl.program_id(1)

        @pl.when(c == 0)
        def _():
            u = (h_ref[...] * r_ref[...] * g_ref[...]).astype(BF16)
            u_scr[...] = u
            u_ref[...] = u
            acc[...] = jnp.zeros_like(acc)
        a = _nt(u_scr[...], wu_ref[...])
        a_ref[...] = a.astype(BF16)
        acc[...] += _nn(jnp.square(jnp.maximum(a, 0.0)).astype(BF16), wd_ref[...])

        @pl.when(c == nc - 1)
        def _():
            ff_ref[...] = acc[...]

    row = pl.BlockSpec((MLP_TM, D), lambda i, c: (i, 0))
    wsp = pl.BlockSpec((MLP_TC, D), lambda i, c: (c, 0))
    return pl.pallas_call(
        body, name="mlp_fwd", grid=(T // MLP_TM, nc),
        in_specs=[row, pl.BlockSpec((MLP_TM, 1), lambda i, c: (i, 0)), pl.BlockSpec((1, D), lambda i, c: (0, 0)), wsp, wsp],
        out_specs=[pl.BlockSpec((MLP_TM, MLP_TC), lambda i, c: (i, c)), row, row],
        out_shape=[jax.ShapeDtypeStruct((T, DFF), BF16), jax.ShapeDtypeStruct((T, D), F32), jax.ShapeDtypeStruct((T, D), BF16)],
        scratch_shapes=[pltpu.VMEM((MLP_TM, D), F32), pltpu.VMEM((MLP_TM, D), BF16)],
        compiler_params=_cparams(("parallel", "arbitrary")),
    )(h, r, g, w_upT, w_dn)


def _mlp_dx(dff, a, w_upT, w_dn):
    nc = DFF // MLP_TC

    def body(d_ref, a_ref, wu_ref, wd_ref, da_ref, du_ref, acc, d_scr):
        c = pl.program_id(1)

        @pl.when(c == 0)
        def _():
            d_scr[...] = d_ref[...].astype(BF16)
            acc[...] = jnp.zeros_like(acc)
        da = (_nt(d_scr[...], wd_ref[...]) * (2.0 * jnp.maximum(a_ref[...].astype(F32), 0.0))).astype(BF16)
        da_ref[...] = da
        acc[...] += _nn(da, wu_ref[...])

        @pl.when(c == nc - 1)
        def _():
            du_ref[...] = acc[...]

    row = pl.BlockSpec((MLP_TM, D), lambda i, c: (i, 0))
    wsp = pl.BlockSpec((MLP_TC, D), lambda i, c: (c, 0))
    chunk = pl.BlockSpec((MLP_TM, MLP_TC), lambda i, c: (i, c))
    return pl.pallas_call(
        body, name="mlp_dx", grid=(T // MLP_TM, nc),
        in_specs=[row, chunk, wsp, wsp], out_specs=[chunk, row],
        out_shape=[jax.ShapeDtypeStruct((T, DFF), BF16), jax.ShapeDtypeStruct((T, D), F32)],
        scratch_shapes=[pltpu.VMEM((MLP_TM, D), F32), pltpu.VMEM((MLP_TM, D), BF16)],
        compiler_params=_cparams(("parallel", "arbitrary")),
    )(dff, a, w_upT, w_dn)


def _rowwise(fn, rows, vecs, out_rows, out_sums, *, tm, name):
    specs, arrs = [], []
    R = None
    for r in rows:
        if isinstance(r, tuple):
            arr, width, cb = r
            specs.append(pl.BlockSpec((tm, width), lambda i, cb=cb: (i, cb)))
        else:
            arr = r
            specs.append(pl.BlockSpec((tm, arr.shape[1]), lambda i: (i, 0)))
        R = arr.shape[0] if R is None else R
        assert arr.shape[0] == R, name
        arrs.append(arr)
    assert R % tm == 0, name
    for v in vecs:
        specs.append(pl.BlockSpec(v.shape, lambda i: (0, 0)))
        arrs.append(v)
    nr, nv, no, ns = len(rows), len(vecs), len(out_rows), len(out_sums)
    out_specs = [pl.BlockSpec((tm, w), lambda i: (i, 0)) for w, _ in out_rows]
    out_specs += [pl.BlockSpec(s, lambda i: (0, 0)) for s in out_sums]
    out_shape = [jax.ShapeDtypeStruct((R, w), dt) for w, dt in out_rows]
    out_shape += [jax.ShapeDtypeStruct(s, F32) for s in out_sums]

    def body(*refs):
        ins = [r[...] for r in refs[:nr + nv]]
        o_refs = refs[nr + nv:nr + nv + no]
        s_refs = refs[nr + nv + no:]
        o_vals, s_vals = fn(*ins)
        for ref, val in zip(o_refs, o_vals):
            ref[...] = val.astype(ref.dtype)
        if ns:
            @pl.when(pl.program_id(0) == 0)
            def _():
                for ref in s_refs:
                    ref[...] = jnp.zeros_like(ref)
            for ref, val in zip(s_refs, s_vals):
                ref[...] += val

    outs = pl.pallas_call(
        body, name=name, grid=(R // tm,), in_specs=specs, out_specs=out_specs, out_shape=out_shape,
        compiler_params=_cparams(("arbitrary",) if ns else ("parallel",)),
    )(*arrs)
    return outs


def _colsum(x):
    return jnp.sum(x, axis=0, keepdims=True)


def _rstd(x):
    return lax.rsqrt(jnp.mean(x * x, axis=-1, keepdims=True) + EPS)


def _rms_bwd(xn, r, g, dy):
    dn = dy * g
    return r * (dn - xn * jnp.mean(dn * xn, axis=-1, keepdims=True))


def _partner(t):
    parts = []
    for s in range(t.shape[1] // 128):
        ts = t[:, 128 * s:128 * (s + 1)]
        lane = lax.broadcasted_iota(jnp.int32, ts.shape, 1)
        up = pltpu.roll(ts, 96, 1)
        down = pltpu.roll(ts, 32, 1)
        parts.append(jnp.where((lane % 64) < 32, up, down))
    return jnp.concatenate(parts, axis=1)


def _rope_fwd(qkvz, cosf, sins):
    def fn(q, k, v, c, s):
        qr = (q * c + _partner(q) * s) * (HD ** -0.5)
        kr = k * c + _partner(k) * s
        return (qr, kr, v), ()
    return _rowwise(fn, [(qkvz, AW, 0), (qkvz, AW, 1), (qkvz, AW, 2), cosf, sins], [],
                    [(AW, BF16), (AW, BF16), (AW, BF16)], [], tm=512, name="rope_fwd")


def _rope_bwd(dqs, dks, dvs, dz, cosf, sins):
    def fn(q1, q2, q3, k1, k2, k3, v1, v2, v3, z, c, s):
        dqr = (q1 + q2 + q3) * (HD ** -0.5)
        dkr = k1 + k2 + k3
        dq = dqr * c + _partner(dqr * s)
        dk = dkr * c + _partner(dkr * s)
        return (jnp.concatenate([dq, dk, v1 + v2 + v3, z], axis=1),), ()
    return _rowwise(fn, [*dqs, *dks, *dvs, dz, cosf, sins], [], [(4 * AW, F32)], [], tm=256, name="rope_bwd")[0]


def _band_masks():
    qi = lax.broadcasted_iota(jnp.int32, (QBLK, QBLK), 0)
    kj = lax.broadcasted_iota(jnp.int32, (QBLK, QBLK), 1)
    return kj >= qi, kj <= qi


def _attn_fwd(q, k, v, d):
    L = q.shape[0]
    nb = L // QBLK

    def body(q_ref, kp_ref, kc_ref, vp_ref, vc_ref, o_ref, l_ref):
        n = pl.program_id(1)
        mask_p, mask_c = _band_masks()
        bias = jnp.concatenate([jnp.where(mask_p, 0.0, NEG) + jnp.where(n > 0, 0.0, NEG),
                                jnp.where(mask_c, 0.0, NEG)], axis=1)
        s = []
        for h in range(HEADS):
            sl = pl.ds(HD * h, HD)
            qh = q_ref[:, sl]
            s.append(jnp.concatenate([_nt(qh, kp_ref[:, sl]), _nt(qh, kc_ref[:, sl])], axis=1))
        s = jnp.stack(s) + bias
        m = jnp.max(s, axis=2, keepdims=True)
        e = jnp.exp(s - m)
        den = jnp.sum(e, axis=2, keepdims=True)
        p = (e * (1.0 / den)).astype(BF16)
        lse = m + jnp.log(den)
        for h in range(HEADS):
            sl = pl.ds(HD * h, HD)
            o_ref[:, sl] = _nn(p[h, :, :QBLK], vp_ref[:, sl]) + _nn(p[h, :, QBLK:], vc_ref[:, sl])
            l_ref[:, sl] = jnp.broadcast_to(lse[h], (QBLK, HD))

    cur = pl.BlockSpec((QBLK, AW), lambda r, n: (n, r))
    prev = pl.BlockSpec((QBLK, AW), lambda r, n: (jnp.maximum(n - 1, 0), r))
    return pl.pallas_call(
        body, name=f"attn_fwd_d{d}", grid=(d, nb),
        in_specs=[cur, prev, cur, prev, cur], out_specs=[cur, cur],
        out_shape=[jax.ShapeDtypeStruct((L, d * AW), F32)] * 2,
        compiler_params=_cparams(("parallel", "parallel")),
    )(q, k, k, v, v)


def _attn_bwd(q, k, v, do, at, lse, d):
    L = q.shape[0]
    nb = L // QBLK

    def body(q0_ref, q1_ref, kp_ref, kc_ref, vp_ref, vc_ref, do0_ref, do1_ref, at0_ref, at1_ref,
             l0_ref, l1_ref, dq_ref, dk_ref, dv_ref):
        n = pl.program_id(1)
        mask_p, mask_c = _band_masks()
        prev_bias = jnp.where(mask_p, 0.0, NEG)
        bias = jnp.concatenate([prev_bias + jnp.where(n > 0, 0.0, NEG), jnp.where(mask_c, 0.0, NEG),
                                prev_bias + jnp.where(n < nb - 1, 0.0, NEG)], axis=1)
        s, dp, ls, dl, ops = [], [], [], [], []
        for h in range(HEADS):
            sl = pl.ds(HD * h, HD)
            one = pl.ds(HD * h, 1)
            q0, q1 = q0_ref[:, sl], q1_ref[:, sl]
            kp, kc, vp, vc = kp_ref[:, sl], kc_ref[:, sl], vp_ref[:, sl], vc_ref[:, sl]
            do0, do1 = do0_ref[:, sl], do1_ref[:, sl]
            do0b, do1b = do0.astype(BF16), do1.astype(BF16)
            s.append(jnp.concatenate([_nt(q0, kp), _nt(q0, kc), _nt(q1, kc)], axis=1))
            dp.append(jnp.concatenate([_nt(do0b, vp), _nt(do0b, vc), _nt(do1b, vc)], axis=1))
            dl0 = jnp.sum(do0 * at0_ref[:, sl], axis=1, keepdims=True)
            dl1 = jnp.sum(do1 * at1_ref[:, sl], axis=1, keepdims=True)
            dl.append(jnp.concatenate([jnp.broadcast_to(dl0, (QBLK, 2 * QBLK)), jnp.broadcast_to(dl1, (QBLK, QBLK))], axis=1))
            ls.append(jnp.concatenate([jnp.broadcast_to(l0_ref[:, one], (QBLK, 2 * QBLK)),
                                       jnp.broadcast_to(l1_ref[:, one], (QBLK, QBLK))], axis=1))
            ops.append((q0, q1, kp, kc, do0b, do1b))
        p = jnp.exp(jnp.stack(s) + bias - jnp.stack(ls))
        ds = (p * (jnp.stack(dp) - jnp.stack(dl))).astype(BF16)
        p = p.astype(BF16)
        for h in range(HEADS):
            sl = pl.ds(HD * h, HD)
            q0, q1, kp, kc, do0b, do1b = ops[h]
            dq_ref[:, sl] = _nn(ds[h, :, :QBLK], kp) + _nn(ds[h, :, QBLK:2 * QBLK], kc)
            dv_ref[:, sl] = _tn(p[h, :, QBLK:2 * QBLK], do0b) + _tn(p[h, :, 2 * QBLK:], do1b)
            dk_ref[:, sl] = _tn(ds[h, :, QBLK:2 * QBLK], q0) + _tn(ds[h, :, 2 * QBLK:], q1)

    cur = pl.BlockSpec((QBLK, AW), lambda r, n: (n, r))
    prev = pl.BlockSpec((QBLK, AW), lambda r, n: (jnp.maximum(n - 1, 0), r))
    nxt = pl.BlockSpec((QBLK, AW), lambda r, n: (jnp.minimum(n + 1, nb - 1), r))
    return pl.pallas_call(
        body, name=f"attn_bwd_d{d}", grid=(d, nb),
        in_specs=[cur, nxt, prev, cur, prev, cur, cur, nxt, cur, nxt, cur, nxt], out_specs=[cur, cur, cur],
        out_shape=[jax.ShapeDtypeStruct((L, d * AW), F32)] * 3,
        compiler_params=_cparams(("parallel", "parallel")),
    )(q, q, k, k, v, v, do, do, at, at, lse, lse)


def _attn_merge(outs, lses):
    def fn(o1, o2, o3, l1, l2, l3):
        m = jnp.maximum(jnp.maximum(l1, l2), l3)
        e1, e2, e3 = jnp.exp(l1 - m), jnp.exp(l2 - m), jnp.exp(l3 - m)
        s = e1 + e2 + e3
        inv = 1.0 / s
        return ((e1 * inv) * o1 + (e2 * inv) * o2 + (e3 * inv) * o3, m + jnp.log(s)), ()
    return _rowwise(fn, [*outs, *lses], [], [(AW, F32), (AW, F32)], [], tm=512, name="attn_merge")


CONV_TM = 512
HALO = 8


def _conv_pre(ext, w, b):
    y = b + w[3] * ext
    for kk in range(1, CONV_K):
        y = y + w[3 - kk] * pltpu.roll(ext, kk, 0)
    return y


def _rows_to_block(rows, n, width):
    ri = lax.broadcasted_iota(jnp.int32, (n, width), 0)
    out = jnp.zeros((n, width), F32)
    for j, r in enumerate(rows):
        out = out + jnp.where(ri == j, r, 0.0)
    return out


def _conv_fwd(xbc, w, b):
    nblk = T // CONV_TM

    def body(x_ref, h_ref, w_ref, b_ref, o_ref):
        i = pl.program_id(0)
        halo = jnp.where(i > 0, h_ref[...], 0.0)
        ext = jnp.concatenate([halo, x_ref[...]], axis=0)
        y = _conv_pre(ext, [w_ref[pl.ds(j, 1), :] for j in range(CONV_K)], b_ref[...])[HALO:]
        o_ref[...] = y * _sigmoid(y)

    return pl.pallas_call(
        body, name="conv_fwd", grid=(nblk,),
        in_specs=[pl.BlockSpec((CONV_TM, CONV_CH), lambda i: (i, 0)),
                  pl.BlockSpec((HALO, CONV_CH), lambda i: (jnp.maximum(i * (CONV_TM // HALO) - 1, 0), 0)),
                  pl.BlockSpec((CONV_K, CONV_CH), lambda i: (0, 0)),
                  pl.BlockSpec((1, CONV_CH), lambda i: (0, 0))],
        out_specs=pl.BlockSpec((CONV_TM, CONV_CH), lambda i: (i, 0)),
        out_shape=jax.ShapeDtypeStruct((T, CONV_CH), F32),
        compiler_params=_cparams(("parallel",)),
    )(xbc, xbc, w, b)


def _conv_bwd(xbc, dact, ddt, w, b):
    nblk = T // CONV_TM
    per = CONV_TM // HALO

    def body(x_ref, xb_ref, xa_ref, g_ref, ga_ref, ddt_ref, w_ref, b_ref, dx_ref, dw_ref):
        i = pl.program_id(0)
        wv = [w_ref[pl.ds(j, 1), :] for j in range(CONV_K)]
        before = jnp.where(i > 0, xb_ref[...], 0.0)
        last = i == nblk - 1
        after = jnp.where(last, 0.0, xa_ref[...])
        g_after = jnp.where(last, 0.0, ga_ref[...])
        ext = jnp.concatenate([before, x_ref[...], after], axis=0)
        y = _conv_pre(ext, wv, b_ref[...])[HALO:]
        sg = _sigmoid(y)
        dy = jnp.concatenate([g_ref[...], g_after], axis=0) * (sg * (1.0 + y * (1.0 - sg)))
        n = CONV_TM + HALO
        dx = wv[3] * dy
        for kk in range(1, CONV_K):
            dx = dx + wv[3 - kk] * pltpu.roll(dy, n - kk, 0)
        dx_ref[:, pl.ds(0, CONV_CH)] = dx[:CONV_TM]
        dx_ref[:, pl.ds(CONV_CH, DT_PAD)] = ddt_ref[...]
        dyc = dy[:CONV_TM]
        rows = [jnp.sum(dyc * (pltpu.roll(ext, 3 - j, 0) if j < 3 else ext)[HALO:HALO + CONV_TM], axis=0, keepdims=True)
                for j in range(CONV_K)]
        rows.append(jnp.sum(dyc, axis=0, keepdims=True))
        part = _rows_to_block(rows, 8, CONV_CH)

        @pl.when(i == 0)
        def _():
            dw_ref[...] = jnp.zeros_like(dw_ref)
        dw_ref[...] += part

    blk = pl.BlockSpec((CONV_TM, CONV_CH), lambda i: (i, 0))
    hb = pl.BlockSpec((HALO, CONV_CH), lambda i: (jnp.maximum(i * per - 1, 0), 0))
    ha = pl.BlockSpec((HALO, CONV_CH), lambda i: (jnp.minimum((i + 1) * per, T // HALO - 1), 0))
    return pl.pallas_call(
        body, name="conv_bwd", grid=(nblk,),
        in_specs=[blk, hb, ha, blk, ha, pl.BlockSpec((CONV_TM, DT_PAD), lambda i: (i, 0)),
                  pl.BlockSpec((CONV_K, CONV_CH), lambda i: (0, 0)), pl.BlockSpec((1, CONV_CH), lambda i: (0, 0))],
        out_specs=[pl.BlockSpec((CONV_TM, CONV_CH + DT_PAD), lambda i: (i, 0)), pl.BlockSpec((8, CONV_CH), lambda i: (0, 0))],
        out_shape=[jax.ShapeDtypeStruct((T, CONV_CH + DT_PAD), F32), jax.ShapeDtypeStruct((8, CONV_CH), F32)],
        compiler_params=_cparams(("arbitrary",)),
    )(xbc, xbc, xbc, dact, dact, ddt, w, b)


def _pick(mat, h):
    lane = lax.broadcasted_iota(jnp.int32, mat.shape, 1)
    return jnp.sum(jnp.where(lane == h, mat, 0.0), axis=1, keepdims=True)


def _heads(fn):
    return jnp.stack([fn(h) for h in range(HEADS)])


def _ssd_prep(dt_ref, bias_ref, alog_ref, dsk_ref, b_ref, c_ref, xs_ref, state_ref, cst):
    li = lax.broadcasted_iota(jnp.int32, (CHUNK, CHUNK), 0)
    si = lax.broadcasted_iota(jnp.int32, (CHUNK, CHUNK), 1)
    tri = li >= si
    dtp = dt_ref[...] + bias_ref[...]
    dt = _softplus(dtp)
    A = -jnp.exp(alog_ref[...])
    a = dt * A
    cs = jnp.dot(tri.astype(F32), a, precision=HIGHEST, preferred_element_type=F32)
    cst[...] = cs.T
    Bm = b_ref[...].astype(BF16)
    Cm = c_ref[...].astype(BF16)
    cb = _nt(Cm, Bm)
    dskv = dsk_ref[...]
    cs_col = _heads(lambda h: _pick(cs, h))
    cs_row = _heads(lambda h: cst[pl.ds(h, 1), :])
    dt_col = _heads(lambda h: _pick(dt, h))
    dsk_col = _heads(lambda h: _pick(dskv, h))
    lam = jnp.exp(jnp.where(tri, cs_col - cs_row, NEG))
    x = _heads(lambda h: xs_ref[:, pl.ds(HD * h, HD)])
    xdt = x * dt_col
    prev = _heads(lambda h: state_ref[pl.ds(HD * h, HD), :])
    lane = lax.broadcasted_iota(jnp.int32, (1, 1, CHUNK), 2)
    cl = jnp.sum(jnp.where(lane == CHUNK - 1, cs_row, 0.0), axis=2, keepdims=True)
    f = jnp.exp(cl - cs_col)
    return dict(li=li, si=si, dtp=dtp, dt=dt, A=A, Bm=Bm, Cm=Cm, cb=cb, cs_col=cs_col, dt_col=dt_col, dsk_col=dsk_col,
                lam=lam, x=x, xdt=xdt, prev=prev, cl=cl, f=f)


def _ssd_fwd(act, xbcdt, bias, alog, dsk):
    nc = T // CHUNK

    def body(xs_ref, b_ref, c_ref, dt_ref, bias_ref, alog_ref, dsk_ref, y_ref, st_ref, state, cst):
        @pl.when(pl.program_id(0) == 0)
        def _():
            state[...] = jnp.zeros_like(state)
        st_ref[...] = state[...]
        s = _ssd_prep(dt_ref, bias_ref, alog_ref, dsk_ref, b_ref, c_ref, xs_ref, state, cst)
        Bm, Cm, prev = s["Bm"], s["Cm"], s["prev"]
        g = (s["cb"] * s["lam"]).astype(BF16)
        xdtb = s["xdt"].astype(BF16)
        prevb = prev.astype(BF16)
        y = _heads(lambda h: _nn(g[h], xdtb[h])) + _heads(lambda h: _nt(Cm, prevb[h])) * jnp.exp(s["cs_col"])
        y = y + s["dsk_col"] * s["x"]
        xf = (s["xdt"] * s["f"]).astype(BF16)
        new = prev * jnp.exp(s["cl"]) + _heads(lambda h: _tn(xf[h], Bm))
        for h in range(HEADS):
            y_ref[:, pl.ds(HD * h, HD)] = y[h]
            state[pl.ds(HD * h, HD), :] = new[h]

    vec = pl.BlockSpec((1, DT_PAD), lambda c: (0, 0))
    return pl.pallas_call(
        body, name="ssd_fwd", grid=(nc,),
        in_specs=[pl.BlockSpec((CHUNK, AW), lambda c: (c, 0)), pl.BlockSpec((CHUNK, NS), lambda c: (c, 4)),
                  pl.BlockSpec((CHUNK, NS), lambda c: (c, 5)), pl.BlockSpec((CHUNK, DT_PAD), lambda c: (c, 6)),
                  vec, vec, vec],
        out_specs=[pl.BlockSpec((CHUNK, AW), lambda c: (c, 0)), pl.BlockSpec((None, AW, NS), lambda c: (c, 0, 0))],
        out_shape=[jax.ShapeDtypeStruct((T, AW), F32), jax.ShapeDtypeStruct((nc, AW, NS), F32)],
        scratch_shapes=[pltpu.VMEM((AW, NS), F32), pltpu.VMEM((CHUNK, CHUNK), F32)],
        compiler_params=_cparams(("arbitrary",)),
    )(act, act, act, xbcdt, bias, alog, dsk)


def _ssd_bwd(act, xbcdt, bias, alog, dsk, states, dy):
    nc = T // CHUNK

    def body(xs_ref, b_ref, c_ref, dt_ref, bias_ref, alog_ref, dsk_ref, st_ref, dy_ref,
             dact_ref, ddt_ref, par_ref, dstate, cst):
        step = pl.program_id(0)

        @pl.when(step == 0)
        def _():
            dstate[...] = jnp.zeros_like(dstate)
            par_ref[...] = jnp.zeros_like(par_ref)
        s = _ssd_prep(dt_ref, bias_ref, alog_ref, dsk_ref, b_ref, c_ref, xs_ref, st_ref, cst)
        Bm, Cm, prev, lam, x, xdt, f, cl = s["Bm"], s["Cm"], s["prev"], s["lam"], s["x"], s["xdt"], s["f"], s["cl"]
        lane = lax.broadcasted_iota(jnp.int32, (1, DT_PAD), 1)
        row = lax.broadcasted_iota(jnp.int32, (1, CHUNK, 1), 1)
        g = s["cb"] * lam
        gb, xdtb, prevb = g.astype(BF16), xdt.astype(BF16), prev.astype(BF16)
        dy = _heads(lambda h: dy_ref[:, pl.ds(HD * h, HD)])
        dyb = dy.astype(BF16)
        dnew = _heads(lambda h: dstate[pl.ds(HD * h, HD), :])
        dnewb = dnew.astype(BF16)
        E = jnp.exp(s["cs_col"])
        ecl = jnp.exp(cl)
        dG = _heads(lambda h: _nt(dyb[h], xdtb[h]))
        dxdt = _heads(lambda h: _tn(gb[h], dyb[h]))
        Yo = _heads(lambda h: _nt(Cm, prevb[h]))
        W = _heads(lambda h: _nt(Bm, dnewb[h]))
        dcb = jnp.sum(dG * lam, axis=0)
        Mm = dG * g
        col_sums = jnp.sum(Mm, axis=1, keepdims=True)
        dYo = (dy * E).astype(BF16)
        dxdt = dxdt + W * f
        dF = jnp.sum(W * xdt, axis=2, keepdims=True) * f
        dcl = jnp.sum(dnew * prev, axis=(1, 2), keepdims=True) * ecl + jnp.sum(dF, axis=1, keepdims=True)
        dcs = (jnp.sum(Mm, axis=2, keepdims=True) + jnp.sum(dy * Yo, axis=2, keepdims=True) * E - dF
               + jnp.where(row == CHUNK - 1, dcl, 0.0))
        ddt_x = jnp.sum(dxdt * x, axis=2, keepdims=True)
        dD = jnp.sum(dy * x, axis=(1, 2), keepdims=True)
        dx = s["dsk_col"] * dy + dxdt * s["dt_col"]
        xfb = (xdt * f).astype(BF16)
        dprev = _heads(lambda h: _tn(dYo[h], Cm)) + dnew * ecl
        dcbb = dcb.astype(BF16)
        dC = _nn(dcbb, Bm)
        dB = _tn(dcbb, Cm)
        dcs_mat = -_rows_to_block([col_sums[h] for h in range(HEADS)], CHUNK, CHUNK).T
        ddt_mat = jnp.zeros((CHUNK, DT_PAD), F32)
        dD_row = jnp.zeros((1, DT_PAD), F32)
        for h in range(HEADS):
            sl = pl.ds(HD * h, HD)
            dC = dC + _nn(dYo[h], prevb[h])
            dB = dB + _nn(xfb[h], dnewb[h])
            dcs_mat = dcs_mat + jnp.where(lane == h, dcs[h], 0.0)
            ddt_mat = ddt_mat + jnp.where(lane == h, ddt_x[h], 0.0)
            dD_row = dD_row + jnp.where(lane == h, dD[h], 0.0)
            dact_ref[:, sl] = dx[h]
            dstate[sl, :] = dprev[h]
        dact_ref[:, pl.ds(AW, NS)] = dB
        dact_ref[:, pl.ds(AW + NS, NS)] = dC
        da = jnp.dot((s["li"] <= s["si"]).astype(F32), dcs_mat, precision=HIGHEST, preferred_element_type=F32)
        ddtp = jnp.where(lane < HEADS, (ddt_mat + da * s["A"]) * _sigmoid(s["dtp"]), 0.0)
        ddt_ref[...] = ddtp
        dalog = jnp.where(lane < HEADS, jnp.sum(da * s["dt"], axis=0, keepdims=True) * s["A"], 0.0)
        par_ref[...] += _rows_to_block([jnp.sum(ddtp, axis=0, keepdims=True), dalog, dD_row], 8, DT_PAD)

    vec = pl.BlockSpec((1, DT_PAD), lambda c: (0, 0))
    rev = lambda c: nc - 1 - c
    return pl.pallas_call(
        body, name="ssd_bwd", grid=(nc,),
        in_specs=[pl.BlockSpec((CHUNK, AW), lambda c: (rev(c), 0)), pl.BlockSpec((CHUNK, NS), lambda c: (rev(c), 4)),
                  pl.BlockSpec((CHUNK, NS), lambda c: (rev(c), 5)), pl.BlockSpec((CHUNK, DT_PAD), lambda c: (rev(c), 6)),
                  vec, vec, vec,
                  pl.BlockSpec((None, AW, NS), lambda c: (rev(c), 0, 0)), pl.BlockSpec((CHUNK, AW), lambda c: (rev(c), 0))],
        out_specs=[pl.BlockSpec((CHUNK, CONV_CH), lambda c: (rev(c), 0)), pl.BlockSpec((CHUNK, DT_PAD), lambda c: (rev(c), 0)),
                   pl.BlockSpec((8, DT_PAD), lambda c: (0, 0))],
        out_shape=[jax.ShapeDtypeStruct((T, CONV_CH), F32), jax.ShapeDtypeStruct((T, DT_PAD), F32),
                   jax.ShapeDtypeStruct((8, DT_PAD), F32)],
        scratch_shapes=[pltpu.VMEM((AW, NS), F32), pltpu.VMEM((CHUNK, CHUNK), F32)],
        compiler_params=_cparams(("arbitrary",)),
    )(act, act, act, xbcdt, bias, alog, dsk, states, dy)


def _place():
    return lax.axis_index("x"), lax.axis_index("y"), lax.axis_index("c")


def _slot(px, py, pc):
    return 4 * px + 2 * py + pc


def _all_gather(arrs, name):
    na = len(arrs)

    def body(*refs):
        ins, outs = refs[:na], refs[na:2 * na]
        send_sems, recv_sems, local_sems = refs[2 * na:]
        x, y, c = _place()
        me, sib = (x, y, c), (x, y, 1 - c)
        chips = [(1 - x, y), (x, 1 - y), (1 - x, 1 - y)]

        def copy(a, kk, block, to, src=None):
            dst = outs[a].at[_slot(*block)]
            return pltpu.make_async_remote_copy(
                src_ref=dst if src is None else src, dst_ref=dst,
                send_sem=send_sems.at[a, kk], recv_sem=recv_sems.at[a, kk], device_id=to, device_id_type=MESH)

        mine = [pltpu.make_async_copy(ins[a], outs[a].at[_slot(*me)], local_sems.at[a]) for a in range(na)]
        for cp in mine:
            cp.start()
        first = []
        for a in range(na):
            first.append(copy(a, 0, me, sib, src=ins[a]))
            first += [copy(a, 1 + j, me, (*chip, c), src=ins[a]) for j, chip in enumerate(chips)]
        for cp in first:
            cp.start()
        passed = []
        for j, chip in enumerate(chips):
            for a in range(na):
                copy(a, 1 + j, (*chip, c), me).wait_recv()
                fw = copy(a, 4 + j, (*chip, c), sib)
                fw.start()
                passed.append(fw)
        for a in range(na):
            copy(a, 0, sib, me).wait_recv()
            for j, chip in enumerate(chips):
                copy(a, 4 + j, (*chip, 1 - c), me).wait_recv()
        for cp in first + passed:
            cp.wait_send()
        for cp in mine:
            cp.wait()

    any_spec = pl.BlockSpec(memory_space=pl.ANY)
    return pl.pallas_call(
        body, name=name,
        in_specs=[any_spec] * na, out_specs=[any_spec] * na,
        out_shape=[jax.ShapeDtypeStruct((N_DEV,) + a.shape, a.dtype) for a in arrs],
        scratch_shapes=[pltpu.SemaphoreType.DMA((na, 7)), pltpu.SemaphoreType.DMA((na, 7)),
                        pltpu.SemaphoreType.DMA((na,))],
    )(*arrs)


def _reduce_scatter(part, name):
    _, r, C = part.shape

    def body(part_ref, out_ref, own, got_sib, chip_sum, got_ici, lsem, s1, r1, s2, r2):
        x, y, c = _place()
        chips = [(x, y), (1 - x, y), (x, 1 - y), (1 - x, 1 - y)]
        loc = [pltpu.make_async_copy(part_ref.at[_slot(*chips[kk], c)], own.at[kk], lsem.at[kk]) for kk in range(4)]
        d2d = [pltpu.make_async_remote_copy(
            src_ref=part_ref.at[_slot(*chips[kk], 1 - c)], dst_ref=got_sib.at[kk],
            send_sem=s1.at[kk], recv_sem=r1.at[kk], device_id=(x, y, 1 - c), device_id_type=MESH) for kk in range(4)]
        for cp in loc + d2d:
            cp.start()
        ici = [pltpu.make_async_remote_copy(
            src_ref=chip_sum.at[kk - 1], dst_ref=got_ici.at[kk - 1],
            send_sem=s2.at[kk - 1], recv_sem=r2.at[kk - 1], device_id=(*chips[kk], c), device_id_type=MESH)
            for kk in range(1, 4)]
        for kk in (1, 2, 3):
            loc[kk].wait()
            d2d[kk].wait_recv()
            chip_sum[kk - 1] = (own[kk].astype(F32) + got_sib[kk].astype(F32)).astype(BF16)
            ici[kk - 1].start()
        loc[0].wait()
        d2d[0].wait_recv()
        acc = own[0].astype(F32) + got_sib[0].astype(F32)
        for cp in ici:
            cp.wait_recv()
        out_ref[...] = ((acc + got_ici[0].astype(F32)) + got_ici[1].astype(F32)) + got_ici[2].astype(F32)
        for cp in d2d + ici:
            cp.wait_send()

    return pl.pallas_call(
        body, name=name,
        in_specs=[pl.BlockSpec(memory_space=pl.ANY)],
        out_specs=pl.BlockSpec(memory_space=pltpu.VMEM),
        out_shape=jax.ShapeDtypeStruct((r, C), F32),
        scratch_shapes=[pltpu.VMEM((4, r, C), BF16), pltpu.VMEM((4, r, C), BF16), pltpu.VMEM((3, r, C), BF16),
                        pltpu.VMEM((3, r, C), BF16),
                        pltpu.SemaphoreType.DMA((4,)), pltpu.SemaphoreType.DMA((4,)), pltpu.SemaphoreType.DMA((4,)),
                        pltpu.SemaphoreType.DMA((3,)), pltpu.SemaphoreType.DMA((3,))],
        compiler_params=pltpu.CompilerParams(vmem_limit_bytes=VMEM_LIMIT),
    )(part)


def _all_reduce_small(v, name):
    R, C = v.shape

    def body(v_ref, out_ref, got, send_sems, recv_sems):
        x, y, c = _place()
        mine = _slot(x, y, c)
        copies = []
        for kk in range(1, N_DEV):
            fx, fy, fc = kk >> 2 & 1, kk >> 1 & 1, kk & 1
            peer = (1 - x if fx else x, 1 - y if fy else y, 1 - c if fc else c)
            copies.append(pltpu.make_async_remote_copy(
                src_ref=v_ref, dst_ref=got.at[mine], send_sem=send_sems.at[kk - 1], recv_sem=recv_sems.at[kk - 1],
                device_id=peer, device_id_type=MESH))
        for cp in copies:
            cp.start()
        got[mine] = v_ref[...]
        for cp in copies:
            cp.wait_recv()
        acc = got[0]
        for s in range(1, N_DEV):
            acc = acc + got[s]
        out_ref[...] = acc
        for cp in copies:
            cp.wait_send()

    return pl.pallas_call(
        body, name=name,
        in_specs=[pl.BlockSpec(memory_space=pltpu.VMEM)], out_specs=pl.BlockSpec(memory_space=pltpu.VMEM),
        out_shape=jax.ShapeDtypeStruct((R, C), F32),
        scratch_shapes=[pltpu.VMEM((N_DEV, R, C), F32), pltpu.SemaphoreType.DMA((N_DEV - 1,)),
                        pltpu.SemaphoreType.DMA((N_DEV - 1,))],
    )(v)


_HBM = pl.BlockSpec(memory_space=pltpu.HBM)
_SEM = pl.BlockSpec(memory_space=pltpu.SEMAPHORE)
_EFFECT = pltpu.SideEffectType.DATAFLOW_SIDE_EFFECTING


def _peers(x, y, c):
    out = []
    for kk in range(1, N_DEV):
        fx, fy, fc = kk >> 2 & 1, kk >> 1 & 1, kk & 1
        out.append((1 - x if fx else x, 1 - y if fy else y, 1 - c if fc else c))
    return out


def _send_start(src, per_peer, name):
    blk = src.shape[1:] if per_peer else src.shape

    def body(src_ref, land_ref, send_sems, recv_sems, src_thru, land_thru):
        x, y, c = _place()
        mine = _slot(x, y, c)
        for kk, peer in enumerate(_peers(x, y, c)):
            pltpu.make_async_remote_copy(
                src_ref=src_ref.at[_slot(*peer)] if per_peer else src_ref, dst_ref=land_ref.at[mine],
                send_sem=send_sems.at[kk], recv_sem=recv_sems.at[kk], device_id=peer, device_id_type=MESH).start()

    land = lax.empty((N_DEV,) + tuple(blk), src.dtype)
    return pl.pallas_call(
        body, name=name,
        out_shape=(pltpu.SemaphoreType.DMA((N_DEV - 1,)), pltpu.SemaphoreType.DMA((N_DEV - 1,)),
                   pltpu.HBM(src.shape, src.dtype), pltpu.HBM(land.shape, land.dtype)),
        in_specs=(_HBM, _HBM), out_specs=(_SEM, _SEM, _HBM, _HBM), input_output_aliases={0: 2, 1: 3},
        compiler_params=pltpu.CompilerParams(has_side_effects=_EFFECT),
    )(pltpu.with_memory_space_constraint(src, pltpu.HBM), pltpu.with_memory_space_constraint(land, pltpu.HBM))


def _send_wait(handles, after, name):
    send_sems, recv_sems, src_thru, land_thru = handles

    def body(src_ref, land_ref, send_sems, recv_sems, after_ref, src_dead, got_ref):
        me = _place()
        for kk in range(N_DEV - 1):
            cp = pltpu.make_async_remote_copy(
                src_ref=land_ref.at[0], dst_ref=land_ref.at[0], send_sem=send_sems.at[kk], recv_sem=recv_sems.at[kk],
                device_id=me, device_id_type=MESH)
            cp.wait_send()
            cp.wait_recv()

    return pl.pallas_call(
        body, name=name,
        out_shape=(pltpu.HBM(src_thru.shape, src_thru.dtype), pltpu.HBM(land_thru.shape, land_thru.dtype)),
        in_specs=(_HBM, _HBM, _SEM, _SEM, pl.BlockSpec(memory_space=pl.ANY)), out_specs=(_HBM, _HBM),
        input_output_aliases={0: 0, 1: 1},
        compiler_params=pltpu.CompilerParams(has_side_effects=_EFFECT),
    )(src_thru, land_thru, send_sems, recv_sems, after)


def _sum_slots(land, name):
    _, R, C = land.shape
    tm = R if R <= 512 else 512

    def body(x_ref, o_ref):
        acc = x_ref[0].astype(F32)
        for j in range(1, N_DEV):
            acc = acc + x_ref[j].astype(F32)
        o_ref[...] = acc

    return pl.pallas_call(
        body, name=name, grid=(R // tm,),
        in_specs=[pl.BlockSpec((N_DEV, tm, C), lambda i: (0, i, 0))], out_specs=pl.BlockSpec((tm, C), lambda i: (i, 0)),
        out_shape=jax.ShapeDtypeStruct((R, C), F32), compiler_params=_cparams(("parallel",)),
    )(land)


def _adamw(w, g, m, v, name):
    R, C = w.shape
    tm = R if R <= 512 else 256

    def fn(w, g, m, v):
        m2 = ADAM_B1 * m + (1.0 - ADAM_B1) * g
        v2 = ADAM_B2 * v + (1.0 - ADAM_B2) * (g * g)
        m_hat = m2 / (1.0 - ADAM_B1 ** ADAM_STEP)
        v_hat = v2 / (1.0 - ADAM_B2 ** ADAM_STEP)
        delta = -ADAM_LR * (m_hat / (jnp.sqrt(v_hat) + ADAM_EPS) + ADAM_WD * w)
        return (delta, m2, v2), ()
    return _rowwise(fn, [w, g, m, v], [], [(C, F32)] * 3, [], tm=tm, name=name)


SMALL = ["norm_mix_pre", "norm_mix_post", "norm_mlp_pre", "norm_mlp_post", "norm_ple_post",
         "conv_b", "ssd_norm_g", "dt_bias", "a_log", "d_skip"]


def _pad_row(v, width=D):
    return jnp.pad(v, ((0, 0), (0, width - v.shape[1])))


def kernel(x, p, positions, norm_mix_pre, norm_mix_post, w_in, conv_w, conv_b, dt_bias, a_log, d_skip, ssd_norm_g, w_out, norm_mlp_pre, norm_mlp_post, w_up, w_down, w_ple_gate, w_ple_proj, norm_ple_post, loss_target, m_norm_mix_pre, m_norm_mix_post, m_w_in, m_conv_w, m_conv_b, m_dt_bias, m_a_log, m_d_skip, m_ssd_norm_g, m_w_out, m_norm_mlp_pre, m_norm_mlp_post, m_w_up, m_w_down, m_w_ple_gate, m_w_ple_proj, m_norm_ple_post, v_norm_mix_pre, v_norm_mix_post, v_w_in, v_conv_w, v_conv_b, v_dt_bias, v_a_log, v_d_skip, v_ssd_norm_g, v_w_out, v_norm_mlp_pre, v_norm_mlp_post, v_w_up, v_w_down, v_w_ple_gate, v_w_ple_proj, v_norm_ple_post):
    args = dict(locals())
    x2, p2, tgt = x[0], p[0, 0], loss_target[0]
    g1, g2, g3, g4, g5 = norm_mix_pre, norm_mix_post, norm_mlp_pre, norm_mlp_post, norm_ple_post

    me = _slot(*_place())
    pack_in = jnp.pad(w_in[0].T, ((0, W_IN_SHARD_PAD - W_IN_SHARD), (0, 0))).astype(BF16)
    pack_rest = jnp.concatenate([
        w_out[0],
        w_up[0].T,
        w_down[0],
        w_ple_gate[0],
        w_ple_proj[0].T.reshape(32, D),
    ], axis=0).astype(BF16)
    rest_handles = _send_start(pack_rest, False, "gather_rest_start")
    conv_pack = jnp.pad(conv_w[0], ((0, 4), (0, 32)))
    gin, gconv = _all_gather([pack_in, conv_pack], "gather_w_in")
    w_inT = gin[:, :W_IN_SHARD].reshape(IN_W, D)
    w_qkvzT = w_inT[:4 * AW]
    w_xbcdtT = jnp.pad(w_inT[4 * AW:], ((0, DT_PAD - HEADS), (0, 0)))
    conv_full = gconv[:, :CONV_K, :96].transpose(1, 0, 2).reshape(CONV_K, CONV_CH)

    inv_freq = ROPE_THETA ** (-jnp.arange(HD // 2, dtype=F32) * 2.0 / HD)
    ang = positions[0].astype(F32)[:, None] * inv_freq
    cos, sin = jnp.cos(ang), jnp.sin(ang)
    cosf = jnp.tile(jnp.concatenate([cos, cos], axis=1), (1, HEADS))
    sins = jnp.tile(jnp.concatenate([-sin, sin], axis=1), (1, HEADS))

    bias_w, alog_w, dsk_w = _pad_row(dt_bias, DT_PAD), _pad_row(a_log, DT_PAD), _pad_row(d_skip, DT_PAD)
    rms_pre = lambda a, r, g: a * r * g

    (r1,) = _rowwise(lambda a: ((_rstd(a),), ()), [x2], [], [(1, F32)], [], tm=512, name="rstd_x")
    qkvz = _mm(x2, w_qkvzT, tb=True, tm=512, tn=1024, tk=1024, a_pre=rms_pre, a_rows=[r1], a_cols=[g1], name="proj_qkvz")
    xbcdt = _mm(x2, w_xbcdtT, tb=True, tm=512, tn=896, tk=1024, a_pre=rms_pre, a_rows=[r1], a_cols=[g1], name="proj_xbcdt")

    qr, kr, vb = _rope_fwd(qkvz, cosf, sins)
    outs, lses = [], []
    for d in DILATIONS:
        L = T // d
        o, l = _attn_fwd(qr.reshape(L, d * AW), kr.reshape(L, d * AW), vb.reshape(L, d * AW), d)
        outs.append(o.reshape(T, AW))
        lses.append(l.reshape(T, AW))
    attn, lse = _attn_merge(outs, lses)

    act = _conv_fwd(xbcdt, conv_full, conv_b)
    y_ssd, states = _ssd_fwd(act, xbcdt, bias_w, alog_w, dsk_w)

    def gated_fwd(y, z, a, gs):
        gi = y * (z * _sigmoid(z))
        return (jnp.concatenate([a, gi * _rstd(gi) * gs], axis=1),), ()
    (cat,) = _rowwise(gated_fwd, [y_ssd, (qkvz, AW, 3), attn], [ssd_norm_g], [(D, F32)], [], tm=512, name="gated_norm")

    pack_back, grest = _send_wait(rest_handles, cat, "gather_rest_wait")
    grest = lax.dynamic_update_slice(grest, pack_back[None], (me, 0, 0))
    w_o = grest[:, 0:128].reshape(D, D)
    w_upT = grest[:, 128:640].reshape(DFF, D)
    w_dn = grest[:, 640:1152].reshape(DFF, D)
    w_gate = grest[:, 1152:1280].reshape(D, D)
    w_projT = grest[:, 1280:1312].reshape(D, PLE)

    mix = _mm(cat, w_o, tm=512, tn=1024, tk=1024, name="mix_out")

    def post1(xx, mm, ga, gb):
        h = xx + mm * _rstd(mm) * ga
        return (h, _rstd(h)), ()
    h1, r3 = _rowwise(post1, [x2, mix], [g2, g3], [(D, F32), (1, F32)], [], tm=512, name="post_mix")

    a_up, ff, u2 = _mlp_fwd(h1, r3, g3, w_upT, w_dn)
    relu2 = lambda a: jnp.square(jnp.maximum(a.astype(F32), 0.0))
    (h2,) = _rowwise(lambda hh, f, g: ((hh + f * _rstd(f) * g,), ()), [h1, ff], [g4], [(D, F32)], [], tm=512, name="post_mlp")

    gp = _mm(h2, w_gate, tm=512, tn=1024, tk=1024, name="ple_gate")
    pp = _mm(p2, w_projT, tb=True, tm=512, tn=1024, tk=256, name="ple_proj")

    def final(hh, gpre, ppv, tg, g):
        sg = _sigmoid(gpre)
        ple = ppv * sg
        r = _rstd(ple)
        n = ple * r
        h3 = hh + n * g
        e = h3 - tg
        dh3 = e * (1.0 / D)
        dple = _rms_bwd(n, r, g, dh3)
        return (dh3, dple * sg, dple * ppv * sg * (1.0 - sg)), (_colsum(dh3 * n), _colsum(0.5 * e * e * (1.0 / D)))
    dh3, dpp, dgp, dg5, loss_vec = _rowwise(final, [h2, gp, pp, tgt], [g5], [(D, F32)] * 3, [(1, D), (1, D)],
                                            tm=256, name="loss_ple_bwd")

    gw_projT = _mm(dpp, p2, ta=True, tm=512, tn=256, tk=1024, out_dtypes=(BF16,), name="gw_ple_proj")
    gw_gate = _mm(h2, dgp, ta=True, tm=512, tn=1024, tk=1024, out_dtypes=(BF16,), name="gw_ple_gate")
    rs_proj = _send_start(gw_projT.reshape(N_DEV, 32, D), True, "rs_start_w_proj")
    rs_gate = _send_start(gw_gate.reshape(N_DEV, 128, D), True, "rs_start_w_gate")
    dh2_g = _mm(dgp, w_gate, tb=True, tm=512, tn=1024, tk=1024, name="dx_ple_gate")

    def bwd_mlp_post(d3, dg_, f, g):
        dh2 = d3 + dg_
        r = _rstd(f)
        n = f * r
        return (dh2, _rms_bwd(n, r, g, dh2)), (_colsum(dh2 * n),)
    dh2, dff, dg4 = _rowwise(bwd_mlp_post, [dh3, dh2_g, ff], [g4], [(D, F32)] * 2, [(1, D)], tm=256, name="bwd_post_mlp")

    gw_dn = _mm(a_up, dff, ta=True, tm=1024, tn=1024, tk=512, a_pre=relu2, out_dtypes=(BF16,), name="gw_mlp_down")
    rs_dn = _send_start(gw_dn.reshape(N_DEV, 512, D), True, "rs_start_w_down")
    da_up, du2 = _mlp_dx(dff, a_up, w_upT, w_dn)
    gw_upT = _mm(da_up, u2, ta=True, tm=1024, tn=1024, tk=512, out_dtypes=(BF16,), name="gw_mlp_up")
    rs_up = _send_start(gw_upT.reshape(N_DEV, 512, D), True, "rs_start_w_up")

    def bwd_mix_post(d2, du, hh, rr, mm, ga, gb):
        n3 = hh * rr
        dh1 = d2 + _rms_bwd(n3, rr, gb, du)
        r = _rstd(mm)
        n2 = mm * r
        return (dh1, _rms_bwd(n2, r, ga, dh1)), (_colsum(du * n3), _colsum(dh1 * n2))
    dh1, dmix, dg3, dg2 = _rowwise(bwd_mix_post, [dh2, du2, h1, r3, mix], [g2, g3], [(D, F32)] * 2, [(1, D), (1, D)],
                                   tm=256, name="bwd_post_mix")

    gw_o = _mm(cat, dmix, ta=True, tm=512, tn=1024, tk=1024, out_dtypes=(BF16,), name="gw_out")
    rs_o = _send_start(gw_o.reshape(N_DEV, 128, D), True, "rs_start_w_out")
    dcat = _mm(dmix, w_o, tb=True, tm=512, tn=1024, tk=1024, name="dx_out")

    def gated_bwd(y, z, dyn, gs):
        sg = _sigmoid(z)
        sz = z * sg
        gi = y * sz
        r = _rstd(gi)
        n = gi * r
        dgi = _rms_bwd(n, r, gs, dyn)
        return (dgi * sz, dgi * y * (sg * (1.0 + z * (1.0 - sg)))), (_colsum(dyn * n),)
    dy_ssd, dz, dgs = _rowwise(gated_bwd, [y_ssd, (qkvz, AW, 3), (dcat, AW, 1)], [ssd_norm_g], [(AW, F32)] * 2, [(1, AW)],
                               tm=512, name="bwd_gated_norm")

    dact, ddtw, ssd_par = _ssd_bwd(act, xbcdt, bias_w, alog_w, dsk_w, states, dy_ssd)
    dxbcdt, conv_par = _conv_bwd(xbcdt, dact, ddtw, conv_full, conv_b)

    dattn = dcat[:, :AW]
    dqs, dks, dvs = [], [], []
    for d in DILATIONS:
        L = T // d
        rs = lambda t: t.reshape(L, d * AW)
        dq, dk, dv = _attn_bwd(rs(qr), rs(kr), rs(vb), rs(dattn), rs(attn), rs(lse), d)
        dqs.append(dq.reshape(T, AW))
        dks.append(dk.reshape(T, AW))
        dvs.append(dv.reshape(T, AW))
    dqkvz = _rope_bwd(dqs, dks, dvs, dz, cosf, sins)

    du1a = _mm(dqkvz, w_qkvzT, tm=512, tn=1024, tk=1024, name="dx_qkvz")
    du1b = _mm(dxbcdt, w_xbcdtT, tm=512, tn=1024, tk=896, name="dx_xbcdt")
    pre1 = dict(b_pre=rms_pre, b_rows=[r1], b_cols=[g1], out_dtypes=(BF16,))
    gw_qkvzT = _mm(dqkvz, x2, ta=True, tm=1024, tn=1024, tk=512, name="gw_qkvz", **pre1)
    gw_xbcdtT = _mm(dxbcdt, x2, ta=True, tm=896, tn=1024, tk=512, name="gw_xbcdt", **pre1)

    def bwd_in(d1, ua, ub, xx, rr, g):
        n = xx * rr
        du = ua + ub
        return (d1 + _rms_bwd(n, rr, g, du),), (_colsum(du * n),)
    grad_x, dg1 = _rowwise(bwd_in, [dh1, du1a, du1b, x2, r1], [g1], [(D, F32)], [(1, D)], tm=256, name="bwd_pre_mix")

    gw_inT = jnp.concatenate([gw_qkvzT, gw_xbcdtT], axis=0)[:IN_W]
    gw_inT = jnp.pad(gw_inT.reshape(N_DEV, W_IN_SHARD, D), ((0, 0), (0, W_IN_SHARD_PAD - W_IN_SHARD), (0, 0)))
    g_inT = _reduce_scatter(gw_inT, "rs_w_in")

    def scatter_finish(handles, nm):
        part, land = _send_wait(handles, g_inT, "rs_wait_" + nm)
        own = lax.dynamic_slice(part, (me, 0, 0), (1,) + part.shape[1:])
        return _sum_slots(lax.dynamic_update_slice(land, own, (me, 0, 0)), "rs_sum_" + nm)
    g_out = scatter_finish(rs_o, "w_out")
    g_upT = scatter_finish(rs_up, "w_up")
    g_dn = scatter_finish(rs_dn, "w_down")
    g_gate = scatter_finish(rs_gate, "w_gate")
    g_projT = scatter_finish(rs_proj, "w_proj")

    small = jnp.concatenate([
        dg1, dg2, dg3, dg4, dg5,
        _pad_row(conv_par[4:5]), _pad_row(dgs), _pad_row(ssd_par[0:1]), _pad_row(ssd_par[1:2]), _pad_row(ssd_par[2:3]),
        _pad_row(conv_par[0:4]), loss_vec, jnp.zeros((1, D), F32),
    ], axis=0)
    small = _all_reduce_small(small, "reduce_small")
    loss = jnp.sum(small[14])
    me = lax.axis_index("x") * 4 + lax.axis_index("y") * 2 + lax.axis_index("c")
    g_conv_w = lax.dynamic_slice(small[10:14, :CONV_CH], (0, me * 96), (CONV_K, 96))

    grads = {
        "w_in": g_inT[:W_IN_SHARD].T[None], "w_out": g_out[None], "w_up": g_upT.T[None], "w_down": g_dn[None],
        "w_ple_gate": g_gate[None], "w_ple_proj": g_projT.reshape(128, PLE).T[None], "conv_w": g_conv_w[None],
        "norm_mix_pre": small[0:1], "norm_mix_post": small[1:2], "norm_mlp_pre": small[2:3], "norm_mlp_post": small[3:4],
        "norm_ple_post": small[4:5], "conv_b": small[5:6, :CONV_CH], "ssd_norm_g": small[6:7, :AW],
        "dt_bias": small[7:8, :HEADS], "a_log": small[8:9, :HEADS], "d_skip": small[9:10, :HEADS],
    }
    delta, new_m, new_v = {}, {}, {}
    for nme in ["w_in", "w_out", "w_up", "w_down", "w_ple_gate", "w_ple_proj"]:
        dl, mm_, vv_ = _adamw(args[nme][0], grads[nme][0], args["m_" + nme][0], args["v_" + nme][0], "adamw_" + nme)
        delta[nme], new_m[nme], new_v[nme] = dl[None], mm_[None], vv_[None]

    def pack_small(prefix):
        rows = [_pad_row(args[prefix + nme]) for nme in SMALL]
        rows.append(_pad_row(args[prefix + "conv_w"][0]))
        rows.append(jnp.zeros((2, D), F32))
        return jnp.concatenate(rows, axis=0)
    g_small = jnp.concatenate([small[0:10], _pad_row(g_conv_w), jnp.zeros((2, D), F32)], axis=0)
    dl, mm_, vv_ = _adamw(pack_small(""), g_small, pack_small("m_"), pack_small("v_"), "adamw_small")
    for i, nme in enumerate(SMALL):
        wdt = args[nme].shape[1]
        delta[nme], new_m[nme], new_v[nme] = dl[i:i + 1, :wdt], mm_[i:i + 1, :wdt], vv_[i:i + 1, :wdt]
    delta["conv_w"], new_m["conv_w"], new_v["conv_w"] = dl[None, 10:14, :96], mm_[None, 10:14, :96], vv_[None, 10:14, :96]

    order = ["norm_mix_pre", "norm_mix_post", "w_in", "conv_w", "conv_b", "dt_bias", "a_log", "d_skip", "ssd_norm_g",
             "w_out", "norm_mlp_pre", "norm_mlp_post", "w_up", "w_down", "w_ple_gate", "w_ple_proj", "norm_ple_post"]
    return (loss, grad_x[None], *[grads[n] for n in order], *[delta[n] for n in order],
            *[new_m[n] for n in order], *[new_v[n] for n in order])
```

```python
import functools
import math

import jax
import jax.numpy as jnp
from jax import lax
from jax.experimental import pallas as pl
from jax.experimental.pallas import tpu as pltpu

F32 = jnp.float32
BF16 = jnp.bfloat16
MESH = pl.DeviceIdType.MESH
HIGHEST = lax.Precision.HIGHEST

N_DEV = 8
T = 4096
D = 1024
HEADS = 8
HD = 64
AW = 512
NS = 128
CONV_K = 4
CONV_CH = 768
CHUNK = 128
DFF = 4096
PLE = 256
EPS = 1e-6
ROPE_THETA = 10000.0
DILATIONS = (1, 4, 16)
QBLK = 128
NEG = -1e30
IN_W = 2824
W_IN_SHARD = 353
W_IN_SHARD_PAD = 384
DT_PAD = 128

ADAM_LR, ADAM_B1, ADAM_B2, ADAM_EPS, ADAM_WD, ADAM_STEP = 0.001, 0.9, 0.999, 1e-08, 0.01, 10

VMEM_LIMIT = 56 * 1024 * 1024


_ANY = pl.BlockSpec(memory_space=pl.ANY)


def _cparams(sem=None):
    return pltpu.CompilerParams(dimension_semantics=sem, vmem_limit_bytes=VMEM_LIMIT)


def _dot(a, b, ca, cb, precision=None):
    return lax.dot_general(a, b, (((ca,), (cb,)), ((), ())), preferred_element_type=F32, precision=precision)


def _nn(a, b):
    return _dot(a, b, 1, 0)


def _nt(a, b):
    return _dot(a, b, 1, 1)


def _tn(a, b):
    return _dot(a, b, 0, 0)


def _sigmoid(x):
    return 1.0 / (1.0 + jnp.exp(-x))


def _softplus(x):
    return jnp.maximum(x, 0.0) + jnp.log(1.0 + jnp.exp(-jnp.abs(x)))


def _mm(a, b, *, ta=False, tb=False, tm, tn, tk, name,
        a_pre=None, a_rows=(), a_cols=(), b_pre=None, b_rows=(), b_cols=(),
        epi=None, epi_tiles=(), out_dtypes=(F32,), deps=()):
    if ta:
        K, M = a.shape
    else:
        M, K = a.shape
    if tb:
        N, K2 = b.shape
    else:
        K2, N = b.shape
    assert K == K2 and M % tm == 0 and N % tn == 0 and K % tk == 0, (name, a.shape, b.shape)
    nk = K // tk
    if ta:
        a_spec = pl.BlockSpec((tk, tm), lambda i, j, k: (k, i))
        a_row_specs = [pl.BlockSpec((tk, 1), lambda i, j, k: (k, 0)) for _ in a_rows]
        a_col_specs = [pl.BlockSpec((1, tm), lambda i, j, k: (0, i)) for _ in a_cols]
    else:
        a_spec = pl.BlockSpec((tm, tk), lambda i, j, k: (i, k))
        a_row_specs = [pl.BlockSpec((tm, 1), lambda i, j, k: (i, 0)) for _ in a_rows]
        a_col_specs = [pl.BlockSpec((1, tk), lambda i, j, k: (0, k)) for _ in a_cols]
    if tb:
        b_spec = pl.BlockSpec((tn, tk), lambda i, j, k: (j, k))
        b_row_specs = [pl.BlockSpec((tn, 1), lambda i, j, k: (j, 0)) for _ in b_rows]
        b_col_specs = [pl.BlockSpec((1, tk), lambda i, j, k: (0, k)) for _ in b_cols]
    else:
        b_spec = pl.BlockSpec((tk, tn), lambda i, j, k: (k, j))
        b_row_specs = [pl.BlockSpec((tk, 1), lambda i, j, k: (k, 0)) for _ in b_rows]
        b_col_specs = [pl.BlockSpec((1, tn), lambda i, j, k: (0, j)) for _ in b_cols]
    o_spec = pl.BlockSpec((tm, tn), lambda i, j, k: (i, j))
    na, nb, ne, no = len(a_rows) + len(a_cols), len(b_rows) + len(b_cols), len(epi_tiles), len(out_dtypes)

    def body(*refs):
        a_ref, b_ref = refs[0], refs[1]
        a_ex = refs[2:2 + na]
        b_ex = refs[2 + na:2 + na + nb]
        e_ex = refs[2 + na + nb:2 + na + nb + ne]
        first_out = 2 + na + nb + ne + len(deps)
        outs = refs[first_out:first_out + no]
        acc = refs[-1]
        k = pl.program_id(2)

        @pl.when(k == 0)
        def _():
            acc[...] = jnp.zeros_like(acc)

        at = a_ref[...]
        if a_pre is not None:
            at = a_pre(at, *[r[...] for r in a_ex])
        bt = b_ref[...]
        if b_pre is not None:
            bt = b_pre(bt, *[r[...] for r in b_ex])
        acc[...] += _dot(at.astype(BF16), bt.astype(BF16), 0 if ta else 1, 1 if tb else 0)

        @pl.when(k == nk - 1)
        def _():
            res = acc[...]
            vals = epi(res, *[r[...] for r in e_ex]) if epi is not None else (res,)
            for o_ref, val in zip(outs, vals):
                o_ref[...] = val.astype(o_ref.dtype)

    outs = pl.pallas_call(
        body, name=name,
        grid=(M // tm, N // tn, nk),
        in_specs=([a_spec, b_spec] + a_row_specs + a_col_specs + b_row_specs + b_col_specs + [o_spec] * ne
                  + [_ANY] * len(deps)),
        out_specs=[o_spec] * no,
        out_shape=[jax.ShapeDtypeStruct((M, N), dt) for dt in out_dtypes],
        scratch_shapes=[pltpu.VMEM((tm, tn), F32)],
        compiler_params=_cparams(("parallel", "parallel", "arbitrary")),
    )(a, b, *a_rows, *a_cols, *b_rows, *b_cols, *epi_tiles, *deps)
    return outs[0] if no == 1 else outs


MLP_TM = 1024
MLP_TC = 512


def _mlp_fwd(h, r, g, w_upT, w_dn):
    nc = DFF // MLP_TC

    def body(h_ref, r_ref, g_ref, wu_ref, wd_ref, a_ref, ff_ref, u_ref, acc, u_scr):
        c = pl.program_id(1)

        @pl.when(c == 0)
        def _():
            u = (h_ref[...] * r_ref[...] * g_ref[...]).astype(BF16)
            u_scr[...] = u
            u_ref[...] = u
            acc[...] = jnp.zeros_like(acc)
        a = _nt(u_scr[...], wu_ref[...])
        a_ref[...] = a.astype(BF16)
        acc[...] += _nn(jnp.square(jnp.maximum(a, 0.0)).astype(BF16), wd_ref[...])

        @pl.when(c == nc - 1)
        def _():
            ff_ref[...] = acc[...]

    row = pl.BlockSpec((MLP_TM, D), lambda i, c: (i, 0))
    wsp = pl.BlockSpec((MLP_TC, D), lambda i, c: (c, 0))
    return pl.pallas_call(
        body, name="mlp_fwd", grid=(T // MLP_TM, nc),
        in_specs=[row, pl.BlockSpec((MLP_TM, 1), lambda i, c: (i, 0)), pl.BlockSpec((1, D), lambda i, c: (0, 0)), wsp, wsp],
        out_specs=[pl.BlockSpec((MLP_TM, MLP_TC), lambda i, c: (i, c)), row, row],
        out_shape=[jax.ShapeDtypeStruct((T, DFF), BF16), jax.ShapeDtypeStruct((T, D), F32), jax.ShapeDtypeStruct((T, D), BF16)],
        scratch_shapes=[pltpu.VMEM((MLP_TM, D), F32), pltpu.VMEM((MLP_TM, D), BF16)],
        compiler_params=_cparams(("parallel", "arbitrary")),
    )(h, r, g, w_upT, w_dn)


def _mlp_dx(dff, a, w_upT, w_dn, dep):
    nc = DFF // MLP_TC

    def body(d_ref, a_ref, wu_ref, wd_ref, dep_ref, da_ref, du_ref, acc, d_scr):
        c = pl.program_id(1)

        @pl.when(c == 0)
        def _():
            d_scr[...] = d_ref[...].astype(BF16)
            acc[...] = jnp.zeros_like(acc)
        da = (_nt(d_scr[...], wd_ref[...]) * (2.0 * jnp.maximum(a_ref[...].astype(F32), 0.0))).astype(BF16)
        da_ref[...] = da
        acc[...] += _nn(da, wu_ref[...])

        @pl.when(c == nc - 1)
        def _():
            du_ref[...] = acc[...]

    row = pl.BlockSpec((MLP_TM, D), lambda i, c: (i, 0))
    wsp = pl.BlockSpec((MLP_TC, D), lambda i, c: (c, 0))
    chunk = pl.BlockSpec((MLP_TM, MLP_TC), lambda i, c: (i, c))
    return pl.pallas_call(
        body, name="mlp_dx", grid=(T // MLP_TM, nc),
        in_specs=[row, chunk, wsp, wsp, _ANY], out_specs=[chunk, row],
        out_shape=[jax.ShapeDtypeStruct((T, DFF), BF16), jax.ShapeDtypeStruct((T, D), F32)],
        scratch_shapes=[pltpu.VMEM((MLP_TM, D), F32), pltpu.VMEM((MLP_TM, D), BF16)],
        compiler_params=_cparams(("parallel", "arbitrary")),
    )(dff, a, w_upT, w_dn, dep)


def _rowwise(fn, rows, vecs, out_rows, out_sums, *, tm, name, deps=()):
    specs, arrs = [], []
    R = None
    for r in rows:
        if isinstance(r, tuple):
            arr, width, cb = r
            specs.append(pl.BlockSpec((tm, width), lambda i, cb=cb: (i, cb)))
        else:
            arr = r
            specs.append(pl.BlockSpec((tm, arr.shape[1]), lambda i: (i, 0)))
        R = arr.shape[0] if R is None else R
        assert arr.shape[0] == R, name
        arrs.append(arr)
    assert R % tm == 0, name
    for v in vecs:
        specs.append(pl.BlockSpec(v.shape, lambda i: (0, 0)))
        arrs.append(v)
    nr, nv, no, ns = len(rows), len(vecs), len(out_rows), len(out_sums)
    out_specs = [pl.BlockSpec((tm, w), lambda i: (i, 0)) for w, _ in out_rows]
    out_specs += [pl.BlockSpec(s, lambda i: (0, 0)) for s in out_sums]
    out_shape = [jax.ShapeDtypeStruct((R, w), dt) for w, dt in out_rows]
    out_shape += [jax.ShapeDtypeStruct(s, F32) for s in out_sums]

    nd = len(deps)

    def body(*refs):
        ins = [r[...] for r in refs[:nr + nv]]
        o_refs = refs[nr + nv + nd:nr + nv + nd + no]
        s_refs = refs[nr + nv + nd + no:]
        o_vals, s_vals = fn(*ins)
        for ref, val in zip(o_refs, o_vals):
            ref[...] = val.astype(ref.dtype)
        if ns:
            @pl.when(pl.program_id(0) == 0)
            def _():
                for ref in s_refs:
                    ref[...] = jnp.zeros_like(ref)
            for ref, val in zip(s_refs, s_vals):
                ref[...] += val

    outs = pl.pallas_call(
        body, name=name, grid=(R // tm,), in_specs=specs + [_ANY] * nd, out_specs=out_specs, out_shape=out_shape,
        compiler_params=_cparams(("arbitrary",) if ns else ("parallel",)),
    )(*arrs, *deps)
    return outs


def _colsum(x):
    return jnp.sum(x, axis=0, keepdims=True)


def _rstd(x):
    return lax.rsqrt(jnp.mean(x * x, axis=-1, keepdims=True) + EPS)


def _rms_bwd(xn, r, g, dy):
    dn = dy * g
    return r * (dn - xn * jnp.mean(dn * xn, axis=-1, keepdims=True))


def _partner(t):
    parts = []
    for s in range(t.shape[1] // 128):
        ts = t[:, 128 * s:128 * (s + 1)]
        lane = lax.broadcasted_iota(jnp.int32, ts.shape, 1)
        up = pltpu.roll(ts, 96, 1)
        down = pltpu.roll(ts, 32, 1)
        parts.append(jnp.where((lane % 64) < 32, up, down))
    return jnp.concatenate(parts, axis=1)


def _rope_fwd(qkvz, cosf, sins):
    def fn(q, k, v, c, s):
        qr = (q * c + _partner(q) * s) * (HD ** -0.5)
        kr = k * c + _partner(k) * s
        return (qr, kr, v), ()
    return _rowwise(fn, [(qkvz, AW, 0), (qkvz, AW, 1), (qkvz, AW, 2), cosf, sins], [],
                    [(AW, BF16), (AW, BF16), (AW, BF16)], [], tm=512, name="rope_fwd")


def _rope_bwd(dqs, dks, dvs, dz, cosf, sins):
    def fn(q1, q2, q3, k1, k2, k3, v1, v2, v3, z, c, s):
        dqr = (q1 + q2 + q3) * (HD ** -0.5)
        dkr = k1 + k2 + k3
        dq = dqr * c + _partner(dqr * s)
        dk = dkr * c + _partner(dkr * s)
        return (jnp.concatenate([dq, dk, v1 + v2 + v3, z], axis=1),), ()
    return _rowwise(fn, [*dqs, *dks, *dvs, dz, cosf, sins], [], [(4 * AW, F32)], [], tm=256, name="rope_bwd")[0]


def _band_masks():
    qi = lax.broadcasted_iota(jnp.int32, (QBLK, QBLK), 0)
    kj = lax.broadcasted_iota(jnp.int32, (QBLK, QBLK), 1)
    return kj >= qi, kj <= qi


def _attn_fwd(q, k, v, d):
    L = q.shape[0]
    nb = L // QBLK

    def body(q_ref, kp_ref, kc_ref, vp_ref, vc_ref, o_ref, l_ref):
        n = pl.program_id(1)
        mask_p, mask_c = _band_masks()
        bias = jnp.concatenate([jnp.where(mask_p, 0.0, NEG) + jnp.where(n > 0, 0.0, NEG),
                                jnp.where(mask_c, 0.0, NEG)], axis=1)
        s = []
        for h in range(HEADS):
            sl = pl.ds(HD * h, HD)
            qh = q_ref[:, sl]
            s.append(jnp.concatenate([_nt(qh, kp_ref[:, sl]), _nt(qh, kc_ref[:, sl])], axis=1))
        s = jnp.stack(s) + bias
        m = jnp.max(s, axis=2, keepdims=True)
        e = jnp.exp(s - m)
        den = jnp.sum(e, axis=2, keepdims=True)
        p = (e * (1.0 / den)).astype(BF16)
        lse = m + jnp.log(den)
        for h in range(HEADS):
            sl = pl.ds(HD * h, HD)
            o_ref[:, sl] = _nn(p[h, :, :QBLK], vp_ref[:, sl]) + _nn(p[h, :, QBLK:], vc_ref[:, sl])
            l_ref[:, sl] = jnp.broadcast_to(lse[h], (QBLK, HD))

    cur = pl.BlockSpec((QBLK, AW), lambda r, n: (n, r))
    prev = pl.BlockSpec((QBLK, AW), lambda r, n: (jnp.maximum(n - 1, 0), r))
    return pl.pallas_call(
        body, name=f"attn_fwd_d{d}", grid=(d, nb),
        in_specs=[cur, prev, cur, prev, cur], out_specs=[cur, cur],
        out_shape=[jax.ShapeDtypeStruct((L, d * AW), F32)] * 2,
        compiler_params=_cparams(("parallel", "parallel")),
    )(q, k, k, v, v)


def _attn_bwd(q, k, v, do, at, lse, d):
    L = q.shape[0]
    nb = L // QBLK

    def body(q0_ref, q1_ref, kp_ref, kc_ref, vp_ref, vc_ref, do0_ref, do1_ref, at0_ref, at1_ref,
             l0_ref, l1_ref, dq_ref, dk_ref, dv_ref):
        n = pl.program_id(1)
        mask_p, mask_c = _band_masks()
        prev_bias = jnp.where(mask_p, 0.0, NEG)
        bias = jnp.concatenate([prev_bias + jnp.where(n > 0, 0.0, NEG), jnp.where(mask_c, 0.0, NEG),
                                prev_bias + jnp.where(n < nb - 1, 0.0, NEG)], axis=1)
        s, dp, ls, dl, ops = [], [], [], [], []
        for h in range(HEADS):
            sl = pl.ds(HD * h, HD)
            one = pl.ds(HD * h, 1)
            q0, q1 = q0_ref[:, sl], q1_ref[:, sl]
            kp, kc, vp, vc = kp_ref[:, sl], kc_ref[:, sl], vp_ref[:, sl], vc_ref[:, sl]
            do0, do1 = do0_ref[:, sl], do1_ref[:, sl]
            do0b, do1b = do0.astype(BF16), do1.astype(BF16)
            s.append(jnp.concatenate([_nt(q0, kp), _nt(q0, kc), _nt(q1, kc)], axis=1))
            dp.append(jnp.concatenate([_nt(do0b, vp), _nt(do0b, vc), _nt(do1b, vc)], axis=1))
            dl0 = jnp.sum(do0 * at0_ref[:, sl], axis=1, keepdims=True)
            dl1 = jnp.sum(do1 * at1_ref[:, sl], axis=1, keepdims=True)
            dl.append(jnp.concatenate([jnp.broadcast_to(dl0, (QBLK, 2 * QBLK)), jnp.broadcast_to(dl1, (QBLK, QBLK))], axis=1))
            ls.append(jnp.concatenate([jnp.broadcast_to(l0_ref[:, one], (QBLK, 2 * QBLK)),
                                       jnp.broadcast_to(l1_ref[:, one], (QBLK, QBLK))], axis=1))
            ops.append((q0, q1, kp, kc, do0b, do1b))
        p = jnp.exp(jnp.stack(s) + bias - jnp.stack(ls))
        ds = (p * (jnp.stack(dp) - jnp.stack(dl))).astype(BF16)
        p = p.astype(BF16)
        for h in range(HEADS):
            sl = pl.ds(HD * h, HD)
            q0, q1, kp, kc, do0b, do1b = ops[h]
            dq_ref[:, sl] = _nn(ds[h, :, :QBLK], kp) + _nn(ds[h, :, QBLK:2 * QBLK], kc)
            dv_ref[:, sl] = _tn(p[h, :, QBLK:2 * QBLK], do0b) + _tn(p[h, :, 2 * QBLK:], do1b)
            dk_ref[:, sl] = _tn(ds[h, :, QBLK:2 * QBLK], q0) + _tn(ds[h, :, 2 * QBLK:], q1)

    cur = pl.BlockSpec((QBLK, AW), lambda r, n: (n, r))
    prev = pl.BlockSpec((QBLK, AW), lambda r, n: (jnp.maximum(n - 1, 0), r))
    nxt = pl.BlockSpec((QBLK, AW), lambda r, n: (jnp.minimum(n + 1, nb - 1), r))
    return pl.pallas_call(
        body, name=f"attn_bwd_d{d}", grid=(d, nb),
        in_specs=[cur, nxt, prev, cur, prev, cur, cur, nxt, cur, nxt, cur, nxt], out_specs=[cur, cur, cur],
        out_shape=[jax.ShapeDtypeStruct((L, d * AW), F32)] * 3,
        compiler_params=_cparams(("parallel", "parallel")),
    )(q, q, k, k, v, v, do, do, at, at, lse, lse)


def _attn_merge(outs, lses):
    def fn(o1, o2, o3, l1, l2, l3):
        m = jnp.maximum(jnp.maximum(l1, l2), l3)
        e1, e2, e3 = jnp.exp(l1 - m), jnp.exp(l2 - m), jnp.exp(l3 - m)
        s = e1 + e2 + e3
        inv = 1.0 / s
        return ((e1 * inv) * o1 + (e2 * inv) * o2 + (e3 * inv) * o3, m + jnp.log(s)), ()
    return _rowwise(fn, [*outs, *lses], [], [(AW, F32), (AW, F32)], [], tm=512, name="attn_merge")


CONV_TM = 512
HALO = 8


def _conv_pre(ext, w, b):
    y = b + w[3] * ext
    for kk in range(1, CONV_K):
        y = y + w[3 - kk] * pltpu.roll(ext, kk, 0)
    return y


def _rows_to_block(rows, n, width):
    ri = lax.broadcasted_iota(jnp.int32, (n, width), 0)
    out = jnp.zeros((n, width), F32)
    for j, r in enumerate(rows):
        out = out + jnp.where(ri == j, r, 0.0)
    return out


def _conv_fwd(xbc, w, b):
    nblk = T // CONV_TM

    def body(x_ref, h_ref, w_ref, b_ref, o_ref):
        i = pl.program_id(0)
        halo = jnp.where(i > 0, h_ref[...], 0.0)
        ext = jnp.concatenate([halo, x_ref[...]], axis=0)
        y = _conv_pre(ext, [w_ref[pl.ds(j, 1), :] for j in range(CONV_K)], b_ref[...])[HALO:]
        o_ref[...] = y * _sigmoid(y)

    return pl.pallas_call(
        body, name="conv_fwd", grid=(nblk,),
        in_specs=[pl.BlockSpec((CONV_TM, CONV_CH), lambda i: (i, 0)),
                  pl.BlockSpec((HALO, CONV_CH), lambda i: (jnp.maximum(i * (CONV_TM // HALO) - 1, 0), 0)),
                  pl.BlockSpec((CONV_K, CONV_CH), lambda i: (0, 0)),
                  pl.BlockSpec((1, CONV_CH), lambda i: (0, 0))],
        out_specs=pl.BlockSpec((CONV_TM, CONV_CH), lambda i: (i, 0)),
        out_shape=jax.ShapeDtypeStruct((T, CONV_CH), F32),
        compiler_params=_cparams(("parallel",)),
    )(xbc, xbc, w, b)


def _conv_bwd(xbc, dact, ddt, w, b):
    nblk = T // CONV_TM
    per = CONV_TM // HALO

    def body(x_ref, xb_ref, xa_ref, g_ref, ga_ref, ddt_ref, w_ref, b_ref, dx_ref, dw_ref):
        i = pl.program_id(0)
        wv = [w_ref[pl.ds(j, 1), :] for j in range(CONV_K)]
        before = jnp.where(i > 0, xb_ref[...], 0.0)
        last = i == nblk - 1
        after = jnp.where(last, 0.0, xa_ref[...])
        g_after = jnp.where(last, 0.0, ga_ref[...])
        ext = jnp.concatenate([before, x_ref[...], after], axis=0)
        y = _conv_pre(ext, wv, b_ref[...])[HALO:]
        sg = _sigmoid(y)
        dy = jnp.concatenate([g_ref[...], g_after], axis=0) * (sg * (1.0 + y * (1.0 - sg)))
        n = CONV_TM + HALO
        dx = wv[3] * dy
        for kk in range(1, CONV_K):
            dx = dx + wv[3 - kk] * pltpu.roll(dy, n - kk, 0)
        dx_ref[:, pl.ds(0, CONV_CH)] = dx[:CONV_TM]
        dx_ref[:, pl.ds(CONV_CH, DT_PAD)] = ddt_ref[...]
        dyc = dy[:CONV_TM]
        rows = [jnp.sum(dyc * (pltpu.roll(ext, 3 - j, 0) if j < 3 else ext)[HALO:HALO + CONV_TM], axis=0, keepdims=True)
                for j in range(CONV_K)]
        rows.append(jnp.sum(dyc, axis=0, keepdims=True))
        part = _rows_to_block(rows, 8, CONV_CH)

        @pl.when(i == 0)
        def _():
            dw_ref[...] = jnp.zeros_like(dw_ref)
        dw_ref[...] += part

    blk = pl.BlockSpec((CONV_TM, CONV_CH), lambda i: (i, 0))
    hb = pl.BlockSpec((HALO, CONV_CH), lambda i: (jnp.maximum(i * per - 1, 0), 0))
    ha = pl.BlockSpec((HALO, CONV_CH), lambda i: (jnp.minimum((i + 1) * per, T // HALO - 1), 0))
    return pl.pallas_call(
        body, name="conv_bwd", grid=(nblk,),
        in_specs=[blk, hb, ha, blk, ha, pl.BlockSpec((CONV_TM, DT_PAD), lambda i: (i, 0)),
                  pl.BlockSpec((CONV_K, CONV_CH), lambda i: (0, 0)), pl.BlockSpec((1, CONV_CH), lambda i: (0, 0))],
        out_specs=[pl.BlockSpec((CONV_TM, CONV_CH + DT_PAD), lambda i: (i, 0)), pl.BlockSpec((8, CONV_CH), lambda i: (0, 0))],
        out_shape=[jax.ShapeDtypeStruct((T, CONV_CH + DT_PAD), F32), jax.ShapeDtypeStruct((8, CONV_CH), F32)],
        compiler_params=_cparams(("arbitrary",)),
    )(xbc, xbc, xbc, dact, dact, ddt, w, b)


def _pick(mat, h):
    lane = lax.broadcasted_iota(jnp.int32, mat.shape, 1)
    return jnp.sum(jnp.where(lane == h, mat, 0.0), axis=1, keepdims=True)


def _heads(fn):
    return jnp.stack([fn(h) for h in range(HEADS)])


def _ssd_prep(dt_ref, bias_ref, alog_ref, dsk_ref, b_ref, c_ref, xs_ref, state_ref, cst):
    li = lax.broadcasted_iota(jnp.int32, (CHUNK, CHUNK), 0)
    si = lax.broadcasted_iota(jnp.int32, (CHUNK, CHUNK), 1)
    tri = li >= si
    dtp = dt_ref[...] + bias_ref[...]
    dt = _softplus(dtp)
    A = -jnp.exp(alog_ref[...])
    a = dt * A
    cs = jnp.dot(tri.astype(F32), a, precision=HIGHEST, preferred_element_type=F32)
    cst[...] = cs.T
    Bm = b_ref[...].astype(BF16)
    Cm = c_ref[...].astype(BF16)
    cb = _nt(Cm, Bm)
    dskv = dsk_ref[...]
    cs_col = _heads(lambda h: _pick(cs, h))
    cs_row = _heads(lambda h: cst[pl.ds(h, 1), :])
    dt_col = _heads(lambda h: _pick(dt, h))
    dsk_col = _heads(lambda h: _pick(dskv, h))
    lam = jnp.exp(jnp.where(tri, cs_col - cs_row, NEG))
    x = _heads(lambda h: xs_ref[:, pl.ds(HD * h, HD)])
    xdt = x * dt_col
    prev = _heads(lambda h: state_ref[pl.ds(HD * h, HD), :])
    lane = lax.broadcasted_iota(jnp.int32, (1, 1, CHUNK), 2)
    cl = jnp.sum(jnp.where(lane == CHUNK - 1, cs_row, 0.0), axis=2, keepdims=True)
    f = jnp.exp(cl - cs_col)
    return dict(li=li, si=si, dtp=dtp, dt=dt, A=A, Bm=Bm, Cm=Cm, cb=cb, cs_col=cs_col, dt_col=dt_col, dsk_col=dsk_col,
                lam=lam, x=x, xdt=xdt, prev=prev, cl=cl, f=f)


def _ssd_fwd(act, xbcdt, bias, alog, dsk):
    nc = T // CHUNK

    def body(xs_ref, b_ref, c_ref, dt_ref, bias_ref, alog_ref, dsk_ref, y_ref, st_ref, state, cst):
        @pl.when(pl.program_id(0) == 0)
        def _():
            state[...] = jnp.zeros_like(state)
        st_ref[...] = state[...]
        s = _ssd_prep(dt_ref, bias_ref, alog_ref, dsk_ref, b_ref, c_ref, xs_ref, state, cst)
        Bm, Cm, prev = s["Bm"], s["Cm"], s["prev"]
        g = (s["cb"] * s["lam"]).astype(BF16)
        xdtb = s["xdt"].astype(BF16)
        prevb = prev.astype(BF16)
        y = _heads(lambda h: _nn(g[h], xdtb[h])) + _heads(lambda h: _nt(Cm, prevb[h])) * jnp.exp(s["cs_col"])
        y = y + s["dsk_col"] * s["x"]
        xf = (s["xdt"] * s["f"]).astype(BF16)
        new = prev * jnp.exp(s["cl"]) + _heads(lambda h: _tn(xf[h], Bm))
        for h in range(HEADS):
            y_ref[:, pl.ds(HD * h, HD)] = y[h]
            state[pl.ds(HD * h, HD), :] = new[h]

    vec = pl.BlockSpec((1, DT_PAD), lambda c: (0, 0))
    return pl.pallas_call(
        body, name="ssd_fwd", grid=(nc,),
        in_specs=[pl.BlockSpec((CHUNK, AW), lambda c: (c, 0)), pl.BlockSpec((CHUNK, NS), lambda c: (c, 4)),
                  pl.BlockSpec((CHUNK, NS), lambda c: (c, 5)), pl.BlockSpec((CHUNK, DT_PAD), lambda c: (c, 6)),
                  vec, vec, vec],
        out_specs=[pl.BlockSpec((CHUNK, AW), lambda c: (c, 0)), pl.BlockSpec((None, AW, NS), lambda c: (c, 0, 0))],
        out_shape=[jax.ShapeDtypeStruct((T, AW), F32), jax.ShapeDtypeStruct((nc, AW, NS), F32)],
        scratch_shapes=[pltpu.VMEM((AW, NS), F32), pltpu.VMEM((CHUNK, CHUNK), F32)],
        compiler_params=_cparams(("arbitrary",)),
    )(act, act, act, xbcdt, bias, alog, dsk)


def _ssd_bwd(act, xbcdt, bias, alog, dsk, states, dy):
    nc = T // CHUNK

    def body(xs_ref, b_ref, c_ref, dt_ref, bias_ref, alog_ref, dsk_ref, st_ref, dy_ref,
             dact_ref, ddt_ref, par_ref, dstate, cst):
        step = pl.program_id(0)

        @pl.when(step == 0)
        def _():
            dstate[...] = jnp.zeros_like(dstate)
            par_ref[...] = jnp.zeros_like(par_ref)
        s = _ssd_prep(dt_ref, bias_ref, alog_ref, dsk_ref, b_ref, c_ref, xs_ref, st_ref, cst)
        Bm, Cm, prev, lam, x, xdt, f, cl = s["Bm"], s["Cm"], s["prev"], s["lam"], s["x"], s["xdt"], s["f"], s["cl"]
        lane = lax.broadcasted_iota(jnp.int32, (1, DT_PAD), 1)
        row = lax.broadcasted_iota(jnp.int32, (1, CHUNK, 1), 1)
        g = s["cb"] * lam
        gb, xdtb, prevb = g.astype(BF16), xdt.astype(BF16), prev.astype(BF16)
        dy = _heads(lambda h: dy_ref[:, pl.ds(HD * h, HD)])
        dyb = dy.astype(BF16)
        dnew = _heads(lambda h: dstate[pl.ds(HD * h, HD), :])
        dnewb = dnew.astype(BF16)
        E = jnp.exp(s["cs_col"])
        ecl = jnp.exp(cl)
        dG = _heads(lambda h: _nt(dyb[h], xdtb[h]))
        dxdt = _heads(lambda h: _tn(gb[h], dyb[h]))
        Yo = _heads(lambda h: _nt(Cm, prevb[h]))
        W = _heads(lambda h: _nt(Bm, dnewb[h]))
        dcb = jnp.sum(dG * lam, axis=0)
        Mm = dG * g
        col_sums = jnp.sum(Mm, axis=1, keepdims=True)
        dYo = (dy * E).astype(BF16)
        dxdt = dxdt + W * f
        dF = jnp.sum(W * xdt, axis=2, keepdims=True) * f
        dcl = jnp.sum(dnew * prev, axis=(1, 2), keepdims=True) * ecl + jnp.sum(dF, axis=1, keepdims=True)
        dcs = (jnp.sum(Mm, axis=2, keepdims=True) + jnp.sum(dy * Yo, axis=2, keepdims=True) * E - dF
               + jnp.where(row == CHUNK - 1, dcl, 0.0))
        ddt_x = jnp.sum(dxdt * x, axis=2, keepdims=True)
        dD = jnp.sum(dy * x, axis=(1, 2), keepdims=True)
        dx = s["dsk_col"] * dy + dxdt * s["dt_col"]
        xfb = (xdt * f).astype(BF16)
        dprev = _heads(lambda h: _tn(dYo[h], Cm)) + dnew * ecl
        dcbb = dcb.astype(BF16)
        dC = _nn(dcbb, Bm)
        dB = _tn(dcbb, Cm)
        dcs_mat = -_rows_to_block([col_sums[h] for h in range(HEADS)], CHUNK, CHUNK).T
        ddt_mat = jnp.zeros((CHUNK, DT_PAD), F32)
        dD_row = jnp.zeros((1, DT_PAD), F32)
        for h in range(HEADS):
            sl = pl.ds(HD * h, HD)
            dC = dC + _nn(dYo[h], prevb[h])
            dB = dB + _nn(xfb[h], dnewb[h])
            dcs_mat = dcs_mat + jnp.where(lane == h, dcs[h], 0.0)
            ddt_mat = ddt_mat + jnp.where(lane == h, ddt_x[h], 0.0)
            dD_row = dD_row + jnp.where(lane == h, dD[h], 0.0)
            dact_ref[:, sl] = dx[h]
            dstate[sl, :] = dprev[h]
        dact_ref[:, pl.ds(AW, NS)] = dB
        dact_ref[:, pl.ds(AW + NS, NS)] = dC
        da = jnp.dot((s["li"] <= s["si"]).astype(F32), dcs_mat, precision=HIGHEST, preferred_element_type=F32)
        ddtp = jnp.where(lane < HEADS, (ddt_mat + da * s["A"]) * _sigmoid(s["dtp"]), 0.0)
        ddt_ref[...] = ddtp
        dalog = jnp.where(lane < HEADS, jnp.sum(da * s["dt"], axis=0, keepdims=True) * s["A"], 0.0)
        par_ref[...] += _rows_to_block([jnp.sum(ddtp, axis=0, keepdims=True), dalog, dD_row], 8, DT_PAD)

    vec = pl.BlockSpec((1, DT_PAD), lambda c: (0, 0))
    rev = lambda c: nc - 1 - c
    return pl.pallas_call(
        body, name="ssd_bwd", grid=(nc,),
        in_specs=[pl.BlockSpec((CHUNK, AW), lambda c: (rev(c), 0)), pl.BlockSpec((CHUNK, NS), lambda c: (rev(c), 4)),
                  pl.BlockSpec((CHUNK, NS), lambda c: (rev(c), 5)), pl.BlockSpec((CHUNK, DT_PAD), lambda c: (rev(c), 6)),
                  vec, vec, vec,
                  pl.BlockSpec((None, AW, NS), lambda c: (rev(c), 0, 0)), pl.BlockSpec((CHUNK, AW), lambda c: (rev(c), 0))],
        out_specs=[pl.BlockSpec((CHUNK, CONV_CH), lambda c: (rev(c), 0)), pl.BlockSpec((CHUNK, DT_PAD), lambda c: (rev(c), 0)),
                   pl.BlockSpec((8, DT_PAD), lambda c: (0, 0))],
        out_shape=[jax.ShapeDtypeStruct((T, CONV_CH), F32), jax.ShapeDtypeStruct((T, DT_PAD), F32),
                   jax.ShapeDtypeStruct((8, DT_PAD), F32)],
        scratch_shapes=[pltpu.VMEM((AW, NS), F32), pltpu.VMEM((CHUNK, CHUNK), F32)],
        compiler_params=_cparams(("arbitrary",)),
    )(act, act, act, xbcdt, bias, alog, dsk, states, dy)


def _place():
    return lax.axis_index("x"), lax.axis_index("y"), lax.axis_index("c")


def _slot(px, py, pc):
    return 4 * px + 2 * py + pc


def _all_gather(arrs, name):
    na = len(arrs)

    def body(*refs):
        ins, outs = refs[:na], refs[na:2 * na]
        send_sems, recv_sems, local_sems = refs[2 * na:]
        x, y, c = _place()
        me, sib = (x, y, c), (x, y, 1 - c)
        chips = [(1 - x, y), (x, 1 - y), (1 - x, 1 - y)]

        def copy(a, kk, block, to, src=None):
            dst = outs[a].at[_slot(*block)]
            return pltpu.make_async_remote_copy(
                src_ref=dst if src is None else src, dst_ref=dst,
                send_sem=send_sems.at[a, kk], recv_sem=recv_sems.at[a, kk], device_id=to, device_id_type=MESH)

        mine = [pltpu.make_async_copy(ins[a], outs[a].at[_slot(*me)], local_sems.at[a]) for a in range(na)]
        for cp in mine:
            cp.start()
        first = []
        for a in range(na):
            first.append(copy(a, 0, me, sib, src=ins[a]))
            first += [copy(a, 1 + j, me, (*chip, c), src=ins[a]) for j, chip in enumerate(chips)]
        for cp in first:
            cp.start()
        passed = []
        for j, chip in enumerate(chips):
            for a in range(na):
                copy(a, 1 + j, (*chip, c), me).wait_recv()
                fw = copy(a, 4 + j, (*chip, c), sib)
                fw.start()
                passed.append(fw)
        for a in range(na):
            copy(a, 0, sib, me).wait_recv()
            for j, chip in enumerate(chips):
                copy(a, 4 + j, (*chip, 1 - c), me).wait_recv()
        for cp in first + passed:
            cp.wait_send()
        for cp in mine:
            cp.wait()

    any_spec = pl.BlockSpec(memory_space=pl.ANY)
    return pl.pallas_call(
        body, name=name,
        in_specs=[any_spec] * na, out_specs=[any_spec] * na,
        out_shape=[jax.ShapeDtypeStruct((N_DEV,) + a.shape, a.dtype) for a in arrs],
        scratch_shapes=[pltpu.SemaphoreType.DMA((na, 7)), pltpu.SemaphoreType.DMA((na, 7)),
                        pltpu.SemaphoreType.DMA((na,))],
    )(*arrs)


def _reduce_scatter(part, name):
    _, r, C = part.shape

    def body(part_ref, out_ref, own, got_sib, chip_sum, got_ici, lsem, s1, r1, s2, r2):
        x, y, c = _place()
        chips = [(x, y), (1 - x, y), (x, 1 - y), (1 - x, 1 - y)]
        loc = [pltpu.make_async_copy(part_ref.at[_slot(*chips[kk], c)], own.at[kk], lsem.at[kk]) for kk in range(4)]
        d2d = [pltpu.make_async_remote_copy(
            src_ref=part_ref.at[_slot(*chips[kk], 1 - c)], dst_ref=got_sib.at[kk],
            send_sem=s1.at[kk], recv_sem=r1.at[kk], device_id=(x, y, 1 - c), device_id_type=MESH) for kk in range(4)]
        for cp in loc + d2d:
            cp.start()
        ici = [pltpu.make_async_remote_copy(
            src_ref=chip_sum.at[kk - 1], dst_ref=got_ici.at[kk - 1],
            send_sem=s2.at[kk - 1], recv_sem=r2.at[kk - 1], device_id=(*chips[kk], c), device_id_type=MESH)
            for kk in range(1, 4)]
        for kk in (1, 2, 3):
            loc[kk].wait()
            d2d[kk].wait_recv()
            chip_sum[kk - 1] = (own[kk].astype(F32) + got_sib[kk].astype(F32)).astype(BF16)
            ici[kk - 1].start()
        loc[0].wait()
        d2d[0].wait_recv()
        acc = own[0].astype(F32) + got_sib[0].astype(F32)
        for cp in ici:
            cp.wait_recv()
        out_ref[...] = ((acc + got_ici[0].astype(F32)) + got_ici[1].astype(F32)) + got_ici[2].astype(F32)
        for cp in d2d + ici:
            cp.wait_send()

    return pl.pallas_call(
        body, name=name,
        in_specs=[pl.BlockSpec(memory_space=pl.ANY)],
        out_specs=pl.BlockSpec(memory_space=pltpu.VMEM),
        out_shape=jax.ShapeDtypeStruct((r, C), F32),
        scratch_shapes=[pltpu.VMEM((4, r, C), BF16), pltpu.VMEM((4, r, C), BF16), pltpu.VMEM((3, r, C), BF16),
                        pltpu.VMEM((3, r, C), BF16),
                        pltpu.SemaphoreType.DMA((4,)), pltpu.SemaphoreType.DMA((4,)), pltpu.SemaphoreType.DMA((4,)),
                        pltpu.SemaphoreType.DMA((3,)), pltpu.SemaphoreType.DMA((3,))],
        compiler_params=pltpu.CompilerParams(vmem_limit_bytes=VMEM_LIMIT),
    )(part)


def _all_reduce_small(v, name):
    R, C = v.shape

    def body(v_ref, out_ref, got, send_sems, recv_sems):
        x, y, c = _place()
        mine = _slot(x, y, c)
        copies = []
        for kk in range(1, N_DEV):
            fx, fy, fc = kk >> 2 & 1, kk >> 1 & 1, kk & 1
            peer = (1 - x if fx else x, 1 - y if fy else y, 1 - c if fc else c)
            copies.append(pltpu.make_async_remote_copy(
                src_ref=v_ref, dst_ref=got.at[mine], send_sem=send_sems.at[kk - 1], recv_sem=recv_sems.at[kk - 1],
                device_id=peer, device_id_type=MESH))
        for cp in copies:
            cp.start()
        got[mine] = v_ref[...]
        for cp in copies:
            cp.wait_recv()
        acc = got[0]
        for s in range(1, N_DEV):
            acc = acc + got[s]
        out_ref[...] = acc
        for cp in copies:
            cp.wait_send()

    return pl.pallas_call(
        body, name=name,
        in_specs=[pl.BlockSpec(memory_space=pltpu.VMEM)], out_specs=pl.BlockSpec(memory_space=pltpu.VMEM),
        out_shape=jax.ShapeDtypeStruct((R, C), F32),
        scratch_shapes=[pltpu.VMEM((N_DEV, R, C), F32), pltpu.SemaphoreType.DMA((N_DEV - 1,)),
                        pltpu.SemaphoreType.DMA((N_DEV - 1,))],
    )(v)


_HBM = pl.BlockSpec(memory_space=pltpu.HBM)
_SEM = pl.BlockSpec(memory_space=pltpu.SEMAPHORE)
_EFFECT = pltpu.SideEffectType.DATAFLOW_SIDE_EFFECTING


def _peers(x, y, c):
    out = []
    for kk in range(1, N_DEV):
        fx, fy, fc = kk >> 2 & 1, kk >> 1 & 1, kk & 1
        out.append((1 - x if fx else x, 1 - y if fy else y, 1 - c if fc else c))
    return out


def _send_start(src, per_peer, name):
    blk = src.shape[1:] if per_peer else src.shape

    def body(src_ref, land_ref, send_sems, recv_sems, src_thru, land_thru, token):
        x, y, c = _place()
        mine = _slot(x, y, c)
        for kk, peer in enumerate(_peers(x, y, c)):
            pltpu.make_async_remote_copy(
                src_ref=src_ref.at[_slot(*peer)] if per_peer else src_ref, dst_ref=land_ref.at[mine],
                send_sem=send_sems.at[kk], recv_sem=recv_sems.at[kk], device_id=peer, device_id_type=MESH).start()
        token[...] = jnp.zeros_like(token)

    land = lax.empty((N_DEV,) + tuple(blk), src.dtype)
    *handles, token = pl.pallas_call(
        body, name=name,
        out_shape=(pltpu.SemaphoreType.DMA((N_DEV - 1,)), pltpu.SemaphoreType.DMA((N_DEV - 1,)),
                   pltpu.HBM(src.shape, src.dtype), pltpu.HBM(land.shape, land.dtype),
                   jax.ShapeDtypeStruct((8, 128), F32)),
        in_specs=(_HBM, _HBM), out_specs=(_SEM, _SEM, _HBM, _HBM, pl.BlockSpec(memory_space=pltpu.VMEM)),
        input_output_aliases={0: 2, 1: 3},
        compiler_params=pltpu.CompilerParams(has_side_effects=_EFFECT),
    )(pltpu.with_memory_space_constraint(src, pltpu.HBM), pltpu.with_memory_space_constraint(land, pltpu.HBM))
    return handles, token


def _send_wait(handles, after, name):
    send_sems, recv_sems, src_thru, land_thru = handles

    def body(src_ref, land_ref, send_sems, recv_sems, after_ref, src_dead, got_ref):
        me = _place()
        for kk in range(N_DEV - 1):
            cp = pltpu.make_async_remote_copy(
                src_ref=land_ref.at[0], dst_ref=land_ref.at[0], send_sem=send_sems.at[kk], recv_sem=recv_sems.at[kk],
                device_id=me, device_id_type=MESH)
            cp.wait_send()
            cp.wait_recv()

    return pl.pallas_call(
        body, name=name,
        out_shape=(pltpu.HBM(src_thru.shape, src_thru.dtype), pltpu.HBM(land_thru.shape, land_thru.dtype)),
        in_specs=(_HBM, _HBM, _SEM, _SEM, pl.BlockSpec(memory_space=pl.ANY)), out_specs=(_HBM, _HBM),
        input_output_aliases={0: 0, 1: 1},
        compiler_params=pltpu.CompilerParams(has_side_effects=_EFFECT),
    )(src_thru, land_thru, send_sems, recv_sems, after)


def _sum_slots(land, name):
    _, R, C = land.shape
    tm = R if R <= 512 else 512

    def body(x_ref, o_ref):
        acc = x_ref[0].astype(F32)
        for j in range(1, N_DEV):
            acc = acc + x_ref[j].astype(F32)
        o_ref[...] = acc

    return pl.pallas_call(
        body, name=name, grid=(R // tm,),
        in_specs=[pl.BlockSpec((N_DEV, tm, C), lambda i: (0, i, 0))], out_specs=pl.BlockSpec((tm, C), lambda i: (i, 0)),
        out_shape=jax.ShapeDtypeStruct((R, C), F32), compiler_params=_cparams(("parallel",)),
    )(land)


def _adamw(w, g, m, v, name):
    R, C = w.shape
    tm = R if R <= 512 else 256

    def fn(w, g, m, v):
        m2 = ADAM_B1 * m + (1.0 - ADAM_B1) * g
        v2 = ADAM_B2 * v + (1.0 - ADAM_B2) * (g * g)
        m_hat = m2 / (1.0 - ADAM_B1 ** ADAM_STEP)
        v_hat = v2 / (1.0 - ADAM_B2 ** ADAM_STEP)
        delta = -ADAM_LR * (m_hat / (jnp.sqrt(v_hat) + ADAM_EPS) + ADAM_WD * w)
        return (delta, m2, v2), ()
    return _rowwise(fn, [w, g, m, v], [], [(C, F32)] * 3, [], tm=tm, name=name)


SMALL = ["norm_mix_pre", "norm_mix_post", "norm_mlp_pre", "norm_mlp_post", "norm_ple_post",
         "conv_b", "ssd_norm_g", "dt_bias", "a_log", "d_skip"]


def _pad_row(v, width=D):
    return jnp.pad(v, ((0, 0), (0, width - v.shape[1])))


def kernel(x, p, positions, norm_mix_pre, norm_mix_post, w_in, conv_w, conv_b, dt_bias, a_log, d_skip, ssd_norm_g, w_out, norm_mlp_pre, norm_mlp_post, w_up, w_down, w_ple_gate, w_ple_proj, norm_ple_post, loss_target, m_norm_mix_pre, m_norm_mix_post, m_w_in, m_conv_w, m_conv_b, m_dt_bias, m_a_log, m_d_skip, m_ssd_norm_g, m_w_out, m_norm_mlp_pre, m_norm_mlp_post, m_w_up, m_w_down, m_w_ple_gate, m_w_ple_proj, m_norm_ple_post, v_norm_mix_pre, v_norm_mix_post, v_w_in, v_conv_w, v_conv_b, v_dt_bias, v_a_log, v_d_skip, v_ssd_norm_g, v_w_out, v_norm_mlp_pre, v_norm_mlp_post, v_w_up, v_w_down, v_w_ple_gate, v_w_ple_proj, v_norm_ple_post):
    args = dict(locals())
    x2, p2, tgt = x[0], p[0, 0], loss_target[0]
    g1, g2, g3, g4, g5 = norm_mix_pre, norm_mix_post, norm_mlp_pre, norm_mlp_post, norm_ple_post

    me = _slot(*_place())
    pack_in = jnp.pad(w_in[0].T, ((0, W_IN_SHARD_PAD - W_IN_SHARD), (0, 0))).astype(BF16)
    pack_rest = jnp.concatenate([
        w_out[0],
        w_up[0].T,
        w_down[0],
        w_ple_gate[0],
        w_ple_proj[0].T.reshape(32, D),
    ], axis=0).astype(BF16)
    rest_handles, tok_rest = _send_start(pack_rest, False, "gather_rest_start")
    conv_pack = jnp.pad(conv_w[0], ((0, 4), (0, 32)))
    gin, gconv = _all_gather([pack_in, conv_pack], "gather_w_in")
    w_inT = gin[:, :W_IN_SHARD].reshape(IN_W, D)
    w_qkvzT = w_inT[:4 * AW]
    w_xbcdtT = jnp.pad(w_inT[4 * AW:], ((0, DT_PAD - HEADS), (0, 0)))
    conv_full = gconv[:, :CONV_K, :96].transpose(1, 0, 2).reshape(CONV_K, CONV_CH)

    inv_freq = ROPE_THETA ** (-jnp.arange(HD // 2, dtype=F32) * 2.0 / HD)
    ang = positions[0].astype(F32)[:, None] * inv_freq
    cos, sin = jnp.cos(ang), jnp.sin(ang)
    cosf = jnp.tile(jnp.concatenate([cos, cos], axis=1), (1, HEADS))
    sins = jnp.tile(jnp.concatenate([-sin, sin], axis=1), (1, HEADS))

    bias_w, alog_w, dsk_w = _pad_row(dt_bias, DT_PAD), _pad_row(a_log, DT_PAD), _pad_row(d_skip, DT_PAD)
    rms_pre = lambda a, r, g: a * r * g

    (r1,) = _rowwise(lambda a: ((_rstd(a),), ()), [x2], [], [(1, F32)], [], tm=512, name="rstd_x", deps=[tok_rest])
    qkvz = _mm(x2, w_qkvzT, tb=True, tm=512, tn=1024, tk=1024, a_pre=rms_pre, a_rows=[r1], a_cols=[g1], name="proj_qkvz")
    xbcdt = _mm(x2, w_xbcdtT, tb=True, tm=512, tn=896, tk=1024, a_pre=rms_pre, a_rows=[r1], a_cols=[g1], name="proj_xbcdt")

    qr, kr, vb = _rope_fwd(qkvz, cosf, sins)
    outs, lses = [], []
    for d in DILATIONS:
        L = T // d
        o, l = _attn_fwd(qr.reshape(L, d * AW), kr.reshape(L, d * AW), vb.reshape(L, d * AW), d)
        outs.append(o.reshape(T, AW))
        lses.append(l.reshape(T, AW))
    attn, lse = _attn_merge(outs, lses)

    act = _conv_fwd(xbcdt, conv_full, conv_b)
    y_ssd, states = _ssd_fwd(act, xbcdt, bias_w, alog_w, dsk_w)

    def gated_fwd(y, z, a, gs):
        gi = y * (z * _sigmoid(z))
        return (jnp.concatenate([a, gi * _rstd(gi) * gs], axis=1),), ()
    (cat,) = _rowwise(gated_fwd, [y_ssd, (qkvz, AW, 3), attn], [ssd_norm_g], [(D, F32)], [], tm=512, name="gated_norm")

    pack_back, grest = _send_wait(rest_handles, cat, "gather_rest_wait")
    grest = lax.dynamic_update_slice(grest, pack_back[None], (me, 0, 0))
    w_o = grest[:, 0:128].reshape(D, D)
    w_upT = grest[:, 128:640].reshape(DFF, D)
    w_dn = grest[:, 640:1152].reshape(DFF, D)
    w_gate = grest[:, 1152:1280].reshape(D, D)
    w_projT = grest[:, 1280:1312].reshape(D, PLE)

    mix = _mm(cat, w_o, tm=512, tn=1024, tk=1024, name="mix_out")

    def post1(xx, mm, ga, gb):
        h = xx + mm * _rstd(mm) * ga
        return (h, _rstd(h)), ()
    h1, r3 = _rowwise(post1, [x2, mix], [g2, g3], [(D, F32), (1, F32)], [], tm=512, name="post_mix")

    a_up, ff, u2 = _mlp_fwd(h1, r3, g3, w_upT, w_dn)
    relu2 = lambda a: jnp.square(jnp.maximum(a.astype(F32), 0.0))
    (h2,) = _rowwise(lambda hh, f, g: ((hh + f * _rstd(f) * g,), ()), [h1, ff], [g4], [(D, F32)], [], tm=512, name="post_mlp")

    gp = _mm(h2, w_gate, tm=512, tn=1024, tk=1024, name="ple_gate")
    pp = _mm(p2, w_projT, tb=True, tm=512, tn=1024, tk=256, name="ple_proj")

    def final(hh, gpre, ppv, tg, g):
        sg = _sigmoid(gpre)
        ple = ppv * sg
        r = _rstd(ple)
        n = ple * r
        h3 = hh + n * g
        e = h3 - tg
        dh3 = e * (1.0 / D)
        dple = _rms_bwd(n, r, g, dh3)
        return (dh3, dple * sg, dple * ppv * sg * (1.0 - sg)), (_colsum(dh3 * n), _colsum(0.5 * e * e * (1.0 / D)))
    dh3, dpp, dgp, dg5, loss_vec = _rowwise(final, [h2, gp, pp, tgt], [g5], [(D, F32)] * 3, [(1, D), (1, D)],
                                            tm=256, name="loss_ple_bwd")

    gw_projT = _mm(dpp, p2, ta=True, tm=512, tn=256, tk=1024, out_dtypes=(BF16,), name="gw_ple_proj")
    gw_gate = _mm(h2, dgp, ta=True, tm=512, tn=1024, tk=1024, out_dtypes=(BF16,), name="gw_ple_gate")
    rs_proj, tok_proj = _send_start(gw_projT.reshape(N_DEV, 32, D), True, "rs_start_w_proj")
    rs_gate, tok_gate = _send_start(gw_gate.reshape(N_DEV, 128, D), True, "rs_start_w_gate")
    dh2_g = _mm(dgp, w_gate, tb=True, tm=512, tn=1024, tk=1024, name="dx_ple_gate", deps=[tok_proj, tok_gate])

    def bwd_mlp_post(d3, dg_, f, g):
        dh2 = d3 + dg_
        r = _rstd(f)
        n = f * r
        return (dh2, _rms_bwd(n, r, g, dh2)), (_colsum(dh2 * n),)
    dh2, dff, dg4 = _rowwise(bwd_mlp_post, [dh3, dh2_g, ff], [g4], [(D, F32)] * 2, [(1, D)], tm=256, name="bwd_post_mlp")

    gw_dn = _mm(a_up, dff, ta=True, tm=1024, tn=1024, tk=512, a_pre=relu2, out_dtypes=(BF16,), name="gw_mlp_down")
    rs_dn, tok_dn = _send_start(gw_dn.reshape(N_DEV, 512, D), True, "rs_start_w_down")
    da_up, du2 = _mlp_dx(dff, a_up, w_upT, w_dn, tok_dn)
    gw_upT = _mm(da_up, u2, ta=True, tm=1024, tn=1024, tk=512, out_dtypes=(BF16,), name="gw_mlp_up")
    rs_up, tok_up = _send_start(gw_upT.reshape(N_DEV, 512, D), True, "rs_start_w_up")

    def bwd_mix_post(d2, du, hh, rr, mm, ga, gb):
        n3 = hh * rr
        dh1 = d2 + _rms_bwd(n3, rr, gb, du)
        r = _rstd(mm)
        n2 = mm * r
        return (dh1, _rms_bwd(n2, r, ga, dh1)), (_colsum(du * n3), _colsum(dh1 * n2))
    dh1, dmix, dg3, dg2 = _rowwise(bwd_mix_post, [dh2, du2, h1, r3, mix], [g2, g3], [(D, F32)] * 2, [(1, D), (1, D)],
                                   tm=256, name="bwd_post_mix", deps=[tok_up])

    gw_o = _mm(cat, dmix, ta=True, tm=512, tn=1024, tk=1024, out_dtypes=(BF16,), name="gw_out")
    rs_o, tok_o = _send_start(gw_o.reshape(N_DEV, 128, D), True, "rs_start_w_out")
    dcat = _mm(dmix, w_o, tb=True, tm=512, tn=1024, tk=1024, name="dx_out", deps=[tok_o])

    def gated_bwd(y, z, dyn, gs):
        sg = _sigmoid(z)
        sz = z * sg
        gi = y * sz
        r = _rstd(gi)
        n = gi * r
        dgi = _rms_bwd(n, r, gs, dyn)
        return (dgi * sz, dgi * y * (sg * (1.0 + z * (1.0 - sg)))), (_colsum(dyn * n),)
    dy_ssd, dz, dgs = _rowwise(gated_bwd, [y_ssd, (qkvz, AW, 3), (dcat, AW, 1)], [ssd_norm_g], [(AW, F32)] * 2, [(1, AW)],
                               tm=512, name="bwd_gated_norm")

    dact, ddtw, ssd_par = _ssd_bwd(act, xbcdt, bias_w, alog_w, dsk_w, states, dy_ssd)
    dxbcdt, conv_par = _conv_bwd(xbcdt, dact, ddtw, conv_full, conv_b)

    dattn = dcat[:, :AW]
    dqs, dks, dvs = [], [], []
    for d in DILATIONS:
        L = T // d
        rs = lambda t: t.reshape(L, d * AW)
        dq, dk, dv = _attn_bwd(rs(qr), rs(kr), rs(vb), rs(dattn), rs(attn), rs(lse), d)
        dqs.append(dq.reshape(T, AW))
        dks.append(dk.reshape(T, AW))
        dvs.append(dv.reshape(T, AW))
    dqkvz = _rope_bwd(dqs, dks, dvs, dz, cosf, sins)

    du1a = _mm(dqkvz, w_qkvzT, tm=512, tn=1024, tk=1024, name="dx_qkvz")
    du1b = _mm(dxbcdt, w_xbcdtT, tm=512, tn=1024, tk=896, name="dx_xbcdt")
    pre1 = dict(b_pre=rms_pre, b_rows=[r1], b_cols=[g1], out_dtypes=(BF16,))
    gw_qkvzT = _mm(dqkvz, x2, ta=True, tm=1024, tn=1024, tk=512, name="gw_qkvz", **pre1)
    gw_xbcdtT = _mm(dxbcdt, x2, ta=True, tm=896, tn=1024, tk=512, name="gw_xbcdt", **pre1)

    def bwd_in(d1, ua, ub, xx, rr, g):
        n = xx * rr
        du = ua + ub
        return (d1 + _rms_bwd(n, rr, g, du),), (_colsum(du * n),)
    grad_x, dg1 = _rowwise(bwd_in, [dh1, du1a, du1b, x2, r1], [g1], [(D, F32)], [(1, D)], tm=256, name="bwd_pre_mix")

    gw_inT = jnp.concatenate([gw_qkvzT, gw_xbcdtT], axis=0)[:IN_W]
    gw_inT = jnp.pad(gw_inT.reshape(N_DEV, W_IN_SHARD, D), ((0, 0), (0, W_IN_SHARD_PAD - W_IN_SHARD), (0, 0)))
    g_inT = _reduce_scatter(gw_inT, "rs_w_in")

    def scatter_finish(handles, nm):
        part, land = _send_wait(handles, g_inT, "rs_wait_" + nm)
        own = lax.dynamic_slice(part, (me, 0, 0), (1,) + part.shape[1:])
        return _sum_slots(lax.dynamic_update_slice(land, own, (me, 0, 0)), "rs_sum_" + nm)
    g_out = scatter_finish(rs_o, "w_out")
    g_upT = scatter_finish(rs_up, "w_up")
    g_dn = scatter_finish(rs_dn, "w_down")
    g_gate = scatter_finish(rs_gate, "w_gate")
    g_projT = scatter_finish(rs_proj, "w_proj")

    small = jnp.concatenate([
        dg1, dg2, dg3, dg4, dg5,
        _pad_row(conv_par[4:5]), _pad_row(dgs), _pad_row(ssd_par[0:1]), _pad_row(ssd_par[1:2]), _pad_row(ssd_par[2:3]),
        _pad_row(conv_par[0:4]), loss_vec, jnp.zeros((1, D), F32),
    ], axis=0)
    small = _all_reduce_small(small, "reduce_small")
    loss = jnp.sum(small[14])
    me = lax.axis_index("x") * 4 + lax.axis_index("y") * 2 + lax.axis_index("c")
    g_conv_w = lax.dynamic_slice(small[10:14, :CONV_CH], (0, me * 96), (CONV_K, 96))

    grads = {
        "w_in": g_inT[:W_IN_SHARD].T[None], "w_out": g_out[None], "w_up": g_upT.T[None], "w_down": g_dn[None],
        "w_ple_gate": g_gate[None], "w_ple_proj": g_projT.reshape(128, PLE).T[None], "conv_w": g_conv_w[None],
        "norm_mix_pre": small[0:1], "norm_mix_post": small[1:2], "norm_mlp_pre": small[2:3], "norm_mlp_post": small[3:4],
        "norm_ple_post": small[4:5], "conv_b": small[5:6, :CONV_CH], "ssd_norm_g": small[6:7, :AW],
        "dt_bias": small[7:8, :HEADS], "a_log": small[8:9, :HEADS], "d_skip": small[9:10, :HEADS],
    }
    delta, new_m, new_v = {}, {}, {}
    for nme in ["w_in", "w_out", "w_up", "w_down", "w_ple_gate", "w_ple_proj"]:
        dl, mm_, vv_ = _adamw(args[nme][0], grads[nme][0], args["m_" + nme][0], args["v_" + nme][0], "adamw_" + nme)
        delta[nme], new_m[nme], new_v[nme] = dl[None], mm_[None], vv_[None]

    def pack_small(prefix):
        rows = [_pad_row(args[prefix + nme]) for nme in SMALL]
        rows.append(_pad_row(args[prefix + "conv_w"][0]))
        rows.append(jnp.zeros((2, D), F32))
        return jnp.concatenate(rows, axis=0)
    g_small = jnp.concatenate([small[0:10], _pad_row(g_conv_w), jnp.zeros((2, D), F32)], axis=0)
    dl, mm_, vv_ = _adamw(pack_small(""), g_small, pack_small("m_"), pack_small("v_"), "adamw_small")
    for i, nme in enumerate(SMALL):
        wdt = args[nme].shape[1]
        delta[nme], new_m[nme], new_v[nme] = dl[i:i + 1, :wdt], mm_[i:i + 1, :wdt], vv_[i:i + 1, :wdt]
    delta["conv_w"], new_m["conv_w"], new_v["conv_w"] = dl[None, 10:14, :96], mm_[None, 10:14, :96], vv_[None, 10:14, :96]

    order = ["norm_mix_pre", "norm_mix_post", "w_in", "conv_w", "conv_b", "dt_bias", "a_log", "d_skip", "ssd_norm_g",
             "w_out", "norm_mlp_pre", "norm_mlp_post", "w_up", "w_down", "w_ple_gate", "w_ple_proj", "norm_ple_post"]
    return (loss, grad_x[None], *[grads[n] for n in order], *[delta[n] for n in order],
            *[new_m[n] for n in order], *[new_v[n] for n in order])
```

```python
import functools
import math

import jax
import jax.numpy as jnp
from jax import lax
from jax.experimental import pallas as pl
from jax.experimental.pallas import tpu as pltpu

F32 = jnp.float32
BF16 = jnp.bfloat16
MESH = pl.DeviceIdType.MESH
HIGHEST = lax.Precision.HIGHEST

N_DEV = 8
T = 4096
D = 1024
HEADS = 8
HD = 64
AW = 512
NS = 128
CONV_K = 4
CONV_CH = 768
CHUNK = 128
DFF = 4096
PLE = 256
EPS = 1e-6
ROPE_THETA = 10000.0
DILATIONS = (1, 4, 16)
QBLK = 128
NEG = -1e30
IN_W = 2824
W_IN_SHARD = 353
W_IN_SHARD_PAD = 384
DT_PAD = 128

ADAM_LR, ADAM_B1, ADAM_B2, ADAM_EPS, ADAM_WD, ADAM_STEP = 0.001, 0.9, 0.999, 1e-08, 0.01, 10

VMEM_LIMIT = 56 * 1024 * 1024


_ANY = pl.BlockSpec(memory_space=pl.ANY)


def _cparams(sem=None):
    return pltpu.CompilerParams(dimension_semantics=sem, vmem_limit_bytes=VMEM_LIMIT)


def _dot(a, b, ca, cb, precision=None):
    return lax.dot_general(a, b, (((ca,), (cb,)), ((), ())), preferred_element_type=F32, precision=precision)


def _nn(a, b):
    return _dot(a, b, 1, 0)


def _nt(a, b):
    return _dot(a, b, 1, 1)


def _tn(a, b):
    return _dot(a, b, 0, 0)


def _sigmoid(x):
    return 1.0 / (1.0 + jnp.exp(-x))


def _softplus(x):
    return jnp.maximum(x, 0.0) + jnp.log(1.0 + jnp.exp(-jnp.abs(x)))


def _mm(a, b, *, ta=False, tb=False, tm, tn, tk, name,
        a_pre=None, a_rows=(), a_cols=(), b_pre=None, b_rows=(), b_cols=(),
        epi=None, epi_tiles=(), out_dtypes=(F32,), deps=()):
    if ta:
        K, M = a.shape
    else:
        M, K = a.shape
    if tb:
        N, K2 = b.shape
    else:
        K2, N = b.shape
    assert K == K2 and M % tm == 0 and N % tn == 0 and K % tk == 0, (name, a.shape, b.shape)
    nk = K // tk
    if ta:
        a_spec = pl.BlockSpec((tk, tm), lambda i, j, k: (k, i))
        a_row_specs = [pl.BlockSpec((tk, 1), lambda i, j, k: (k, 0)) for _ in a_rows]
        a_col_specs = [pl.BlockSpec((1, tm), lambda i, j, k: (0, i)) for _ in a_cols]
    else:
        a_spec = pl.BlockSpec((tm, tk), lambda i, j, k: (i, k))
        a_row_specs = [pl.BlockSpec((tm, 1), lambda i, j, k: (i, 0)) for _ in a_rows]
        a_col_specs = [pl.BlockSpec((1, tk), lambda i, j, k: (0, k)) for _ in a_cols]
    if tb:
        b_spec = pl.BlockSpec((tn, tk), lambda i, j, k: (j, k))
        b_row_specs = [pl.BlockSpec((tn, 1), lambda i, j, k: (j, 0)) for _ in b_rows]
        b_col_specs = [pl.BlockSpec((1, tk), lambda i, j, k: (0, k)) for _ in b_cols]
    else:
        b_spec = pl.BlockSpec((tk, tn), lambda i, j, k: (k, j))
        b_row_specs = [pl.BlockSpec((tk, 1), lambda i, j, k: (k, 0)) for _ in b_rows]
        b_col_specs = [pl.BlockSpec((1, tn), lambda i, j, k: (0, j)) for _ in b_cols]
    o_spec = pl.BlockSpec((tm, tn), lambda i, j, k: (i, j))
    na, nb, ne, no = len(a_rows) + len(a_cols), len(b_rows) + len(b_cols), len(epi_tiles), len(out_dtypes)

    def body(*refs):
        a_ref, b_ref = refs[0], refs[1]
        a_ex = refs[2:2 + na]
        b_ex = refs[2 + na:2 + na + nb]
        e_ex = refs[2 + na + nb:2 + na + nb + ne]
        first_out = 2 + na + nb + ne + len(deps)
        outs = refs[first_out:first_out + no]
        acc = refs[-1]
        k = pl.program_id(2)

        @pl.when(k == 0)
        def _():
            acc[...] = jnp.zeros_like(acc)

        at = a_ref[...]
        if a_pre is not None:
            at = a_pre(at, *[r[...] for r in a_ex])
        bt = b_ref[...]
        if b_pre is not None:
            bt = b_pre(bt, *[r[...] for r in b_ex])
        acc[...] += _dot(at.astype(BF16), bt.astype(BF16), 0 if ta else 1, 1 if tb else 0)

        @pl.when(k == nk - 1)
        def _():
            res = acc[...]
            vals = epi(res, *[r[...] for r in e_ex]) if epi is not None else (res,)
            for o_ref, val in zip(outs, vals):
                o_ref[...] = val.astype(o_ref.dtype)

    outs = pl.pallas_call(
        body, name=name,
        grid=(M // tm, N // tn, nk),
        in_specs=([a_spec, b_spec] + a_row_specs + a_col_specs + b_row_specs + b_col_specs + [o_spec] * ne
                  + [_ANY] * len(deps)),
        out_specs=[o_spec] * no,
        out_shape=[jax.ShapeDtypeStruct((M, N), dt) for dt in out_dtypes],
        scratch_shapes=[pltpu.VMEM((tm, tn), F32)],
        compiler_params=_cparams(("parallel", "parallel", "arbitrary")),
    )(a, b, *a_rows, *a_cols, *b_rows, *b_cols, *epi_tiles, *deps)
    return outs[0] if no == 1 else outs


MLP_TM = 1024
MLP_TC = 512


def _mlp_fwd(h, r, g, w_upT, w_dn):
    nc = DFF // MLP_TC

    def body(h_ref, r_ref, g_ref, wu_ref, wd_ref, a_ref, ff_ref, u_ref, acc, u_scr):
        c = pl.program_id(1)

        @pl.when(c == 0)
        def _():
            u = (h_ref[...] * r_ref[...] * g_ref[...]).astype(BF16)
            u_scr[...] = u
            u_ref[...] = u
            acc[...] = jnp.zeros_like(acc)
        a = _nt(u_scr[...], wu_ref[...])
        a_ref[...] = a.astype(BF16)
        acc[...] += _nn(jnp.square(jnp.maximum(a, 0.0)).astype(BF16), wd_ref[...])

        @pl.when(c == nc - 1)
        def _():
            ff_ref[...] = acc[...]

    row = pl.BlockSpec((MLP_TM, D), lambda i, c: (i, 0))
    wsp = pl.BlockSpec((MLP_TC, D), lambda i, c: (c, 0))
    return pl.pallas_call(
        body, name="mlp_fwd", grid=(T // MLP_TM, nc),
        in_specs=[row, pl.BlockSpec((MLP_TM, 1), lambda i, c: (i, 0)), pl.BlockSpec((1, D), lambda i, c: (0, 0)), wsp, wsp],
        out_specs=[pl.BlockSpec((MLP_TM, MLP_TC), lambda i, c: (i, c)), row, row],
        out_shape=[jax.ShapeDtypeStruct((T, DFF), BF16), jax.ShapeDtypeStruct((T, D), F32), jax.ShapeDtypeStruct((T, D), BF16)],
        scratch_shapes=[pltpu.VMEM((MLP_TM, D), F32), pltpu.VMEM((MLP_TM, D), BF16)],
        compiler_params=_cparams(("parallel", "arbitrary")),
    )(h, r, g, w_upT, w_dn)


def _mlp_dx(dff, a, w_upT, w_dn, dep):
    nc = DFF // MLP_TC

    def body(d_ref, a_ref, wu_ref, wd_ref, dep_ref, da_ref, du_ref, acc, d_scr):
        c = pl.program_id(1)

        @pl.when(c == 0)
        def _():
            d_scr[...] = d_ref[...].astype(BF16)
            acc[...] = jnp.zeros_like(acc)
        da = (_nt(d_scr[...], wd_ref[...]) * (2.0 * jnp.maximum(a_ref[...].astype(F32), 0.0))).astype(BF16)
        da_ref[...] = da
        acc[...] += _nn(da, wu_ref[...])

        @pl.when(c == nc - 1)
        def _():
            du_ref[...] = acc[...]

    row = pl.BlockSpec((MLP_TM, D), lambda i, c: (i, 0))
    wsp = pl.BlockSpec((MLP_TC, D), lambda i, c: (c, 0))
    chunk = pl.BlockSpec((MLP_TM, MLP_TC), lambda i, c: (i, c))
    return pl.pallas_call(
        body, name="mlp_dx", grid=(T // MLP_TM, nc),
        in_specs=[row, chunk, wsp, wsp, _ANY], out_specs=[chunk, row],
        out_shape=[jax.ShapeDtypeStruct((T, DFF), BF16), jax.ShapeDtypeStruct((T, D), F32)],
        scratch_shapes=[pltpu.VMEM((MLP_TM, D), F32), pltpu.VMEM((MLP_TM, D), BF16)],
        compiler_params=_cparams(("parallel", "arbitrary")),
    )(dff, a, w_upT, w_dn, dep)


def _rowwise(fn, rows, vecs, out_rows, out_sums, *, tm, name, deps=()):
    specs, arrs = [], []
    R = None
    for r in rows:
        if isinstance(r, tuple):
            arr, width, cb = r
            specs.append(pl.BlockSpec((tm, width), lambda i, cb=cb: (i, cb)))
        else:
            arr = r
            specs.append(pl.BlockSpec((tm, arr.shape[1]), lambda i: (i, 0)))
        R = arr.shape[0] if R is None else R
        assert arr.shape[0] == R, name
        arrs.append(arr)
    assert R % tm == 0, name
    for v in vecs:
        specs.append(pl.BlockSpec(v.shape, lambda i: (0, 0)))
        arrs.append(v)
    nr, nv, no, ns = len(rows), len(vecs), len(out_rows), len(out_sums)
    out_specs = [pl.BlockSpec((tm, w), lambda i: (i, 0)) for w, _ in out_rows]
    out_specs += [pl.BlockSpec(s, lambda i: (0, 0)) for s in out_sums]
    out_shape = [jax.ShapeDtypeStruct((R, w), dt) for w, dt in out_rows]
    out_shape += [jax.ShapeDtypeStruct(s, F32) for s in out_sums]

    nd = len(deps)

    def body(*refs):
        ins = [r[...] for r in refs[:nr + nv]]
        o_refs = refs[nr + nv + nd:nr + nv + nd + no]
        s_refs = refs[nr + nv + nd + no:]
        o_vals, s_vals = fn(*ins)
        for ref, val in zip(o_refs, o_vals):
            ref[...] = val.astype(ref.dtype)
        if ns:
            @pl.when(pl.program_id(0) == 0)
            def _():
                for ref in s_refs:
                    ref[...] = jnp.zeros_like(ref)
            for ref, val in zip(s_refs, s_vals):
                ref[...] += val

    outs = pl.pallas_call(
        body, name=name, grid=(R // tm,), in_specs=specs + [_ANY] * nd, out_specs=out_specs, out_shape=out_shape,
        compiler_params=_cparams(("arbitrary",) if ns else ("parallel",)),
    )(*arrs, *deps)
    return outs


def _colsum(x):
    return jnp.sum(x, axis=0, keepdims=True)


def _rstd(x):
    return lax.rsqrt(jnp.mean(x * x, axis=-1, keepdims=True) + EPS)


def _rms_bwd(xn, r, g, dy):
    dn = dy * g
    return r * (dn - xn * jnp.mean(dn * xn, axis=-1, keepdims=True))


def _partner(t):
    parts = []
    for s in range(t.shape[1] // 128):
        ts = t[:, 128 * s:128 * (s + 1)]
        lane = lax.broadcasted_iota(jnp.int32, ts.shape, 1)
        up = pltpu.roll(ts, 96, 1)
        down = pltpu.roll(ts, 32, 1)
        parts.append(jnp.where((lane % 64) < 32, up, down))
    return jnp.concatenate(parts, axis=1)


def _rope_fwd(qkvz, cosf, sins):
    def fn(q, k, v, c, s):
        qr = (q * c + _partner(q) * s) * (HD ** -0.5)
        kr = k * c + _partner(k) * s
        return (qr, kr, v), ()
    return _rowwise(fn, [(qkvz, AW, 0), (qkvz, AW, 1), (qkvz, AW, 2), cosf, sins], [],
                    [(AW, BF16), (AW, BF16), (AW, BF16)], [], tm=512, name="rope_fwd")


def _rope_bwd(dqs, dks, dvs, dz, cosf, sins):
    def fn(q1, q2, q3, k1, k2, k3, v1, v2, v3, z, c, s):
        dqr = (q1 + q2 + q3) * (HD ** -0.5)
        dkr = k1 + k2 + k3
        dq = dqr * c + _partner(dqr * s)
        dk = dkr * c + _partner(dkr * s)
        return (jnp.concatenate([dq, dk, v1 + v2 + v3, z], axis=1),), ()
    return _rowwise(fn, [*dqs, *dks, *dvs, dz, cosf, sins], [], [(4 * AW, F32)], [], tm=256, name="rope_bwd")[0]


def _band_masks():
    qi = lax.broadcasted_iota(jnp.int32, (QBLK, QBLK), 0)
    kj = lax.broadcasted_iota(jnp.int32, (QBLK, QBLK), 1)
    return kj >= qi, kj <= qi


def _attn_fwd(q, k, v, d):
    L = q.shape[0]
    nb = L // QBLK

    def body(q_ref, kp_ref, kc_ref, vp_ref, vc_ref, o_ref, l_ref):
        n = pl.program_id(1)
        mask_p, mask_c = _band_masks()
        bias = jnp.concatenate([jnp.where(mask_p, 0.0, NEG) + jnp.where(n > 0, 0.0, NEG),
                                jnp.where(mask_c, 0.0, NEG)], axis=1)
        s = []
        for h in range(HEADS):
            sl = pl.ds(HD * h, HD)
            qh = q_ref[:, sl]
            s.append(jnp.concatenate([_nt(qh, kp_ref[:, sl]), _nt(qh, kc_ref[:, sl])], axis=1))
        s = jnp.stack(s) + bias
        m = jnp.max(s, axis=2, keepdims=True)
        e = jnp.exp(s - m)
        den = jnp.sum(e, axis=2, keepdims=True)
        p = (e * (1.0 / den)).astype(BF16)
        lse = m + jnp.log(den)
        for h in range(HEADS):
            sl = pl.ds(HD * h, HD)
            o_ref[:, sl] = _nn(p[h, :, :QBLK], vp_ref[:, sl]) + _nn(p[h, :, QBLK:], vc_ref[:, sl])
            l_ref[:, sl] = jnp.broadcast_to(lse[h], (QBLK, HD))

    cur = pl.BlockSpec((QBLK, AW), lambda r, n: (n, r))
    prev = pl.BlockSpec((QBLK, AW), lambda r, n: (jnp.maximum(n - 1, 0), r))
    return pl.pallas_call(
        body, name=f"attn_fwd_d{d}", grid=(d, nb),
        in_specs=[cur, prev, cur, prev, cur], out_specs=[cur, cur],
        out_shape=[jax.ShapeDtypeStruct((L, d * AW), F32)] * 2,
        compiler_params=_cparams(("parallel", "parallel")),
    )(q, k, k, v, v)


def _attn_bwd(q, k, v, do, at, lse, d):
    L = q.shape[0]
    nb = L // QBLK

    def body(q0_ref, q1_ref, kp_ref, kc_ref, vp_ref, vc_ref, do0_ref, do1_ref, at0_ref, at1_ref,
             l0_ref, l1_ref, dq_ref, dk_ref, dv_ref):
        n = pl.program_id(1)
        mask_p, mask_c = _band_masks()
        prev_bias = jnp.where(mask_p, 0.0, NEG)
        bias = jnp.concatenate([prev_bias + jnp.where(n > 0, 0.0, NEG), jnp.where(mask_c, 0.0, NEG),
                                prev_bias + jnp.where(n < nb - 1, 0.0, NEG)], axis=1)
        s, dp, ls, dl, ops = [], [], [], [], []
        for h in range(HEADS):
            sl = pl.ds(HD * h, HD)
            one = pl.ds(HD * h, 1)
            q0, q1 = q0_ref[:, sl], q1_ref[:, sl]
            kp, kc, vp, vc = kp_ref[:, sl], kc_ref[:, sl], vp_ref[:, sl], vc_ref[:, sl]
            do0, do1 = do0_ref[:, sl], do1_ref[:, sl]
            do0b, do1b = do0.astype(BF16), do1.astype(BF16)
            s.append(jnp.concatenate([_nt(q0, kp), _nt(q0, kc), _nt(q1, kc)], axis=1))
            dp.append(jnp.concatenate([_nt(do0b, vp), _nt(do0b, vc), _nt(do1b, vc)], axis=1))
            dl0 = jnp.sum(do0 * at0_ref[:, sl], axis=1, keepdims=True)
            dl1 = jnp.sum(do1 * at1_ref[:, sl], axis=1, keepdims=True)
            dl.append(jnp.concatenate([jnp.broadcast_to(dl0, (QBLK, 2 * QBLK)), jnp.broadcast_to(dl1, (QBLK, QBLK))], axis=1))
            ls.append(jnp.concatenate([jnp.broadcast_to(l0_ref[:, one], (QBLK, 2 * QBLK)),
                                       jnp.broadcast_to(l1_ref[:, one], (QBLK, QBLK))], axis=1))
            ops.append((q0, q1, kp, kc, do0b, do1b))
        p = jnp.exp(jnp.stack(s) + bias - jnp.stack(ls))
        ds = (p * (jnp.stack(dp) - jnp.stack(dl))).astype(BF16)
        p = p.astype(BF16)
        for h in range(HEADS):
            sl = pl.ds(HD * h, HD)
            q0, q1, kp, kc, do0b, do1b = ops[h]
            dq_ref[:, sl] = _nn(ds[h, :, :QBLK], kp) + _nn(ds[h, :, QBLK:2 * QBLK], kc)
            dv_ref[:, sl] = _tn(p[h, :, QBLK:2 * QBLK], do0b) + _tn(p[h, :, 2 * QBLK:], do1b)
            dk_ref[:, sl] = _tn(ds[h, :, QBLK:2 * QBLK], q0) + _tn(ds[h, :, 2 * QBLK:], q1)

    cur = pl.BlockSpec((QBLK, AW), lambda r, n: (n, r))
    prev = pl.BlockSpec((QBLK, AW), lambda r, n: (jnp.maximum(n - 1, 0), r))
    nxt = pl.BlockSpec((QBLK, AW), lambda r, n: (jnp.minimum(n + 1, nb - 1), r))
    return pl.pallas_call(
        body, name=f"attn_bwd_d{d}", grid=(d, nb),
        in_specs=[cur, nxt, prev, cur, prev, cur, cur, nxt, cur, nxt, cur, nxt], out_specs=[cur, cur, cur],
        out_shape=[jax.ShapeDtypeStruct((L, d * AW), F32)] * 3,
        compiler_params=_cparams(("parallel", "parallel")),
    )(q, q, k, k, v, v, do, do, at, at, lse, lse)


def _attn_merge(outs, lses):
    def fn(o1, o2, o3, l1, l2, l3):
        m = jnp.maximum(jnp.maximum(l1, l2), l3)
        e1, e2, e3 = jnp.exp(l1 - m), jnp.exp(l2 - m), jnp.exp(l3 - m)
        s = e1 + e2 + e3
        inv = 1.0 / s
        return ((e1 * inv) * o1 + (e2 * inv) * o2 + (e3 * inv) * o3, m + jnp.log(s)), ()
    return _rowwise(fn, [*outs, *lses], [], [(AW, F32), (AW, F32)], [], tm=512, name="attn_merge")


CONV_TM = 512
HALO = 8


def _conv_pre(ext, w, b):
    y = b + w[3] * ext
    for kk in range(1, CONV_K):
        y = y + w[3 - kk] * pltpu.roll(ext, kk, 0)
    return y


def _rows_to_block(rows, n, width):
    ri = lax.broadcasted_iota(jnp.int32, (n, width), 0)
    out = jnp.zeros((n, width), F32)
    for j, r in enumerate(rows):
        out = out + jnp.where(ri == j, r, 0.0)
    return out


def _conv_fwd(xbc, w, b):
    nblk = T // CONV_TM

    def body(x_ref, h_ref, w_ref, b_ref, o_ref):
        i = pl.program_id(0)
        halo = jnp.where(i > 0, h_ref[...], 0.0)
        ext = jnp.concatenate([halo, x_ref[...]], axis=0)
        y = _conv_pre(ext, [w_ref[pl.ds(j, 1), :] for j in range(CONV_K)], b_ref[...])[HALO:]
        o_ref[...] = y * _sigmoid(y)

    return pl.pallas_call(
        body, name="conv_fwd", grid=(nblk,),
        in_specs=[pl.BlockSpec((CONV_TM, CONV_CH), lambda i: (i, 0)),
                  pl.BlockSpec((HALO, CONV_CH), lambda i: (jnp.maximum(i * (CONV_TM // HALO) - 1, 0), 0)),
                  pl.BlockSpec((CONV_K, CONV_CH), lambda i: (0, 0)),
                  pl.BlockSpec((1, CONV_CH), lambda i: (0, 0))],
        out_specs=pl.BlockSpec((CONV_TM, CONV_CH), lambda i: (i, 0)),
        out_shape=jax.ShapeDtypeStruct((T, CONV_CH), F32),
        compiler_params=_cparams(("parallel",)),
    )(xbc, xbc, w, b)


def _conv_bwd(xbc, dact, ddt, w, b):
    nblk = T // CONV_TM
    per = CONV_TM // HALO

    def body(x_ref, xb_ref, xa_ref, g_ref, ga_ref, ddt_ref, w_ref, b_ref, dx_ref, dw_ref):
        i = pl.program_id(0)
        wv = [w_ref[pl.ds(j, 1), :] for j in range(CONV_K)]
        before = jnp.where(i > 0, xb_ref[...], 0.0)
        last = i == nblk - 1
        after = jnp.where(last, 0.0, xa_ref[...])
        g_after = jnp.where(last, 0.0, ga_ref[...])
        ext = jnp.concatenate([before, x_ref[...], after], axis=0)
        y = _conv_pre(ext, wv, b_ref[...])[HALO:]
        sg = _sigmoid(y)
        dy = jnp.concatenate([g_ref[...], g_after], axis=0) * (sg * (1.0 + y * (1.0 - sg)))
        n = CONV_TM + HALO
        dx = wv[3] * dy
        for kk in range(1, CONV_K):
            dx = dx + wv[3 - kk] * pltpu.roll(dy, n - kk, 0)
        dx_ref[:, pl.ds(0, CONV_CH)] = dx[:CONV_TM]
        dx_ref[:, pl.ds(CONV_CH, DT_PAD)] = ddt_ref[...]
        dyc = dy[:CONV_TM]
        rows = [jnp.sum(dyc * (pltpu.roll(ext, 3 - j, 0) if j < 3 else ext)[HALO:HALO + CONV_TM], axis=0, keepdims=True)
                for j in range(CONV_K)]
        rows.append(jnp.sum(dyc, axis=0, keepdims=True))
        part = _rows_to_block(rows, 8, CONV_CH)

        @pl.when(i == 0)
        def _():
            dw_ref[...] = jnp.zeros_like(dw_ref)
        dw_ref[...] += part

    blk = pl.BlockSpec((CONV_TM, CONV_CH), lambda i: (i, 0))
    hb = pl.BlockSpec((HALO, CONV_CH), lambda i: (jnp.maximum(i * per - 1, 0), 0))
    ha = pl.BlockSpec((HALO, CONV_CH), lambda i: (jnp.minimum((i + 1) * per, T // HALO - 1), 0))
    return pl.pallas_call(
        body, name="conv_bwd", grid=(nblk,),
        in_specs=[blk, hb, ha, blk, ha, pl.BlockSpec((CONV_TM, DT_PAD), lambda i: (i, 0)),
                  pl.BlockSpec((CONV_K, CONV_CH), lambda i: (0, 0)), pl.BlockSpec((1, CONV_CH), lambda i: (0, 0))],
        out_specs=[pl.BlockSpec((CONV_TM, CONV_CH + DT_PAD), lambda i: (i, 0)), pl.BlockSpec((8, CONV_CH), lambda i: (0, 0))],
        out_shape=[jax.ShapeDtypeStruct((T, CONV_CH + DT_PAD), F32), jax.ShapeDtypeStruct((8, CONV_CH), F32)],
        compiler_params=_cparams(("arbitrary",)),
    )(xbc, xbc, xbc, dact, dact, ddt, w, b)


def _pick(mat, h):
    lane = lax.broadcasted_iota(jnp.int32, mat.shape, 1)
    return jnp.sum(jnp.where(lane == h, mat, 0.0), axis=1, keepdims=True)


def _heads(fn):
    return jnp.stack([fn(h) for h in range(HEADS)])


def _ssd_prep(dt_ref, bias_ref, alog_ref, dsk_ref, b_ref, c_ref, xs_ref, state_ref, cst):
    li = lax.broadcasted_iota(jnp.int32, (CHUNK, CHUNK), 0)
    si = lax.broadcasted_iota(jnp.int32, (CHUNK, CHUNK), 1)
    tri = li >= si
    dtp = dt_ref[...] + bias_ref[...]
    dt = _softplus(dtp)
    A = -jnp.exp(alog_ref[...])
    a = dt * A
    cs = jnp.dot(tri.astype(F32), a, precision=HIGHEST, preferred_element_type=F32)
    cst[...] = cs.T
    Bm = b_ref[...].astype(BF16)
    Cm = c_ref[...].astype(BF16)
    cb = _nt(Cm, Bm)
    dskv = dsk_ref[...]
    cs_col = _heads(lambda h: _pick(cs, h))
    cs_row = _heads(lambda h: cst[pl.ds(h, 1), :])
    dt_col = _heads(lambda h: _pick(dt, h))
    dsk_col = _heads(lambda h: _pick(dskv, h))
    lam = jnp.exp(jnp.where(tri, cs_col - cs_row, NEG))
    x = _heads(lambda h: xs_ref[:, pl.ds(HD * h, HD)])
    xdt = x * dt_col
    prev = _heads(lambda h: state_ref[pl.ds(HD * h, HD), :])
    lane = lax.broadcasted_iota(jnp.int32, (1, 1, CHUNK), 2)
    cl = jnp.sum(jnp.where(lane == CHUNK - 1, cs_row, 0.0), axis=2, keepdims=True)
    f = jnp.exp(cl - cs_col)
    return dict(li=li, si=si, dtp=dtp, dt=dt, A=A, Bm=Bm, Cm=Cm, cb=cb, cs_col=cs_col, dt_col=dt_col, dsk_col=dsk_col,
                lam=lam, x=x, xdt=xdt, prev=prev, cl=cl, f=f)


def _ssd_fwd(act, xbcdt, bias, alog, dsk):
    nc = T // CHUNK

    def body(xs_ref, b_ref, c_ref, dt_ref, bias_ref, alog_ref, dsk_ref, y_ref, st_ref, state, cst):
        @pl.when(pl.program_id(0) == 0)
        def _():
            state[...] = jnp.zeros_like(state)
        st_ref[...] = state[...]
        s = _ssd_prep(dt_ref, bias_ref, alog_ref, dsk_ref, b_ref, c_ref, xs_ref, state, cst)
        Bm, Cm, prev = s["Bm"], s["Cm"], s["prev"]
        g = (s["cb"] * s["lam"]).astype(BF16)
        xdtb = s["xdt"].astype(BF16)
        prevb = prev.astype(BF16)
        y = _heads(lambda h: _nn(g[h], xdtb[h])) + _heads(lambda h: _nt(Cm, prevb[h])) * jnp.exp(s["cs_col"])
        y = y + s["dsk_col"] * s["x"]
        xf = (s["xdt"] * s["f"]).astype(BF16)
        new = prev * jnp.exp(s["cl"]) + _heads(lambda h: _tn(xf[h], Bm))
        for h in range(HEADS):
            y_ref[:, pl.ds(HD * h, HD)] = y[h]
            state[pl.ds(HD * h, HD), :] = new[h]

    vec = pl.BlockSpec((1, DT_PAD), lambda c: (0, 0))
    return pl.pallas_call(
        body, name="ssd_fwd", grid=(nc,),
        in_specs=[pl.BlockSpec((CHUNK, AW), lambda c: (c, 0)), pl.BlockSpec((CHUNK, NS), lambda c: (c, 4)),
                  pl.BlockSpec((CHUNK, NS), lambda c: (c, 5)), pl.BlockSpec((CHUNK, DT_PAD), lambda c: (c, 6)),
                  vec, vec, vec],
        out_specs=[pl.BlockSpec((CHUNK, AW), lambda c: (c, 0)), pl.BlockSpec((None, AW, NS), lambda c: (c, 0, 0))],
        out_shape=[jax.ShapeDtypeStruct((T, AW), F32), jax.ShapeDtypeStruct((nc, AW, NS), F32)],
        scratch_shapes=[pltpu.VMEM((AW, NS), F32), pltpu.VMEM((CHUNK, CHUNK), F32)],
        compiler_params=_cparams(("arbitrary",)),
    )(act, act, act, xbcdt, bias, alog, dsk)


def _ssd_bwd(act, xbcdt, bias, alog, dsk, states, dy):
    nc = T // CHUNK

    def body(xs_ref, b_ref, c_ref, dt_ref, bias_ref, alog_ref, dsk_ref, st_ref, dy_ref,
             dact_ref, ddt_ref, par_ref, dstate, cst):
        step = pl.program_id(0)

        @pl.when(step == 0)
        def _():
            dstate[...] = jnp.zeros_like(dstate)
            par_ref[...] = jnp.zeros_like(par_ref)
        s = _ssd_prep(dt_ref, bias_ref, alog_ref, dsk_ref, b_ref, c_ref, xs_ref, st_ref, cst)
        Bm, Cm, prev, lam, x, xdt, f, cl = s["Bm"], s["Cm"], s["prev"], s["lam"], s["x"], s["xdt"], s["f"], s["cl"]
        lane = lax.broadcasted_iota(jnp.int32, (1, DT_PAD), 1)
        row = lax.broadcasted_iota(jnp.int32, (1, CHUNK, 1), 1)
        g = s["cb"] * lam
        gb, xdtb, prevb = g.astype(BF16), xdt.astype(BF16), prev.astype(BF16)
        dy = _heads(lambda h: dy_ref[:, pl.ds(HD * h, HD)])
        dyb = dy.astype(BF16)
        dnew = _heads(lambda h: dstate[pl.ds(HD * h, HD), :])
        dnewb = dnew.astype(BF16)
        E = jnp.exp(s["cs_col"])
        ecl = jnp.exp(cl)
        dG = _heads(lambda h: _nt(dyb[h], xdtb[h]))
        dxdt = _heads(lambda h: _tn(gb[h], dyb[h]))
        Yo = _heads(lambda h: _nt(Cm, prevb[h]))
        W = _heads(lambda h: _nt(Bm, dnewb[h]))
        dcb = jnp.sum(dG * lam, axis=0)
        Mm = dG * g
        col_sums = jnp.sum(Mm, axis=1, keepdims=True)
        dYo = (dy * E).astype(BF16)
        dxdt = dxdt + W * f
        dF = jnp.sum(W * xdt, axis=2, keepdims=True) * f
        dcl = jnp.sum(dnew * prev, axis=(1, 2), keepdims=True) * ecl + jnp.sum(dF, axis=1, keepdims=True)
        dcs = (jnp.sum(Mm, axis=2, keepdims=True) + jnp.sum(dy * Yo, axis=2, keepdims=True) * E - dF
               + jnp.where(row == CHUNK - 1, dcl, 0.0))
        ddt_x = jnp.sum(dxdt * x, axis=2, keepdims=True)
        dD = jnp.sum(dy * x, axis=(1, 2), keepdims=True)
        dx = s["dsk_col"] * dy + dxdt * s["dt_col"]
        xfb = (xdt * f).astype(BF16)
        dprev = _heads(lambda h: _tn(dYo[h], Cm)) + dnew * ecl
        dcbb = dcb.astype(BF16)
        dC = _nn(dcbb, Bm)
        dB = _tn(dcbb, Cm)
        dcs_mat = -_rows_to_block([col_sums[h] for h in range(HEADS)], CHUNK, CHUNK).T
        ddt_mat = jnp.zeros((CHUNK, DT_PAD), F32)
        dD_row = jnp.zeros((1, DT_PAD), F32)
        for h in range(HEADS):
            sl = pl.ds(HD * h, HD)
            dC = dC + _nn(dYo[h], prevb[h])
            dB = dB + _nn(xfb[h], dnewb[h])
            dcs_mat = dcs_mat + jnp.where(lane == h, dcs[h], 0.0)
            ddt_mat = ddt_mat + jnp.where(lane == h, ddt_x[h], 0.0)
            dD_row = dD_row + jnp.where(lane == h, dD[h], 0.0)
            dact_ref[:, sl] = dx[h]
            dstate[sl, :] = dprev[h]
        dact_ref[:, pl.ds(AW, NS)] = dB
        dact_ref[:, pl.ds(AW + NS, NS)] = dC
        da = jnp.dot((s["li"] <= s["si"]).astype(F32), dcs_mat, precision=HIGHEST, preferred_element_type=F32)
        ddtp = jnp.where(lane < HEADS, (ddt_mat + da * s["A"]) * _sigmoid(s["dtp"]), 0.0)
        ddt_ref[...] = ddtp
        dalog = jnp.where(lane < HEADS, jnp.sum(da * s["dt"], axis=0, keepdims=True) * s["A"], 0.0)
        par_ref[...] += _rows_to_block([jnp.sum(ddtp, axis=0, keepdims=True), dalog, dD_row], 8, DT_PAD)

    vec = pl.BlockSpec((1, DT_PAD), lambda c: (0, 0))
    rev = lambda c: nc - 1 - c
    return pl.pallas_call(
        body, name="ssd_bwd", grid=(nc,),
        in_specs=[pl.BlockSpec((CHUNK, AW), lambda c: (rev(c), 0)), pl.BlockSpec((CHUNK, NS), lambda c: (rev(c), 4)),
                  pl.BlockSpec((CHUNK, NS), lambda c: (rev(c), 5)), pl.BlockSpec((CHUNK, DT_PAD), lambda c: (rev(c), 6)),
                  vec, vec, vec,
                  pl.BlockSpec((None, AW, NS), lambda c: (rev(c), 0, 0)), pl.BlockSpec((CHUNK, AW), lambda c: (rev(c), 0))],
        out_specs=[pl.BlockSpec((CHUNK, CONV_CH), lambda c: (rev(c), 0)), pl.BlockSpec((CHUNK, DT_PAD), lambda c: (rev(c), 0)),
                   pl.BlockSpec((8, DT_PAD), lambda c: (0, 0))],
        out_shape=[jax.ShapeDtypeStruct((T, CONV_CH), F32), jax.ShapeDtypeStruct((T, DT_PAD), F32),
                   jax.ShapeDtypeStruct((8, DT_PAD), F32)],
        scratch_shapes=[pltpu.VMEM((AW, NS), F32), pltpu.VMEM((CHUNK, CHUNK), F32)],
        compiler_params=_cparams(("arbitrary",)),
    )(act, act, act, xbcdt, bias, alog, dsk, states, dy)


def _place():
    return lax.axis_index("x"), lax.axis_index("y"), lax.axis_index("c")


def _slot(px, py, pc):
    return 4 * px + 2 * py + pc


def _all_gather(arrs, name):
    na = len(arrs)

    def body(*refs):
        ins, outs = refs[:na], refs[na:2 * na]
        send_sems, recv_sems, local_sems = refs[2 * na:]
        x, y, c = _place()
        me, sib = (x, y, c), (x, y, 1 - c)
        chips = [(1 - x, y), (x, 1 - y), (1 - x, 1 - y)]

        def copy(a, kk, block, to, src=None):
            dst = outs[a].at[_slot(*block)]
            return pltpu.make_async_remote_copy(
                src_ref=dst if src is None else src, dst_ref=dst,
                send_sem=send_sems.at[a, kk], recv_sem=recv_sems.at[a, kk], device_id=to, device_id_type=MESH)

        mine = [pltpu.make_async_copy(ins[a], outs[a].at[_slot(*me)], local_sems.at[a]) for a in range(na)]
        for cp in mine:
            cp.start()
        first = []
        for a in range(na):
            first.append(copy(a, 0, me, sib, src=ins[a]))
            first += [copy(a, 1 + j, me, (*chip, c), src=ins[a]) for j, chip in enumerate(chips)]
        for cp in first:
            cp.start()
        passed = []
        for j, chip in enumerate(chips):
            for a in range(na):
                copy(a, 1 + j, (*chip, c), me).wait_recv()
                fw = copy(a, 4 + j, (*chip, c), sib)
                fw.start()
                passed.append(fw)
        for a in range(na):
            copy(a, 0, sib, me).wait_recv()
            for j, chip in enumerate(chips):
                copy(a, 4 + j, (*chip, 1 - c), me).wait_recv()
        for cp in first + passed:
            cp.wait_send()
        for cp in mine:
            cp.wait()

    any_spec = pl.BlockSpec(memory_space=pl.ANY)
    return pl.pallas_call(
        body, name=name,
        in_specs=[any_spec] * na, out_specs=[any_spec] * na,
        out_shape=[jax.ShapeDtypeStruct((N_DEV,) + a.shape, a.dtype) for a in arrs],
        scratch_shapes=[pltpu.SemaphoreType.DMA((na, 7)), pltpu.SemaphoreType.DMA((na, 7)),
                        pltpu.SemaphoreType.DMA((na,))],
    )(*arrs)


def _reduce_scatter(part, name):
    _, r, C = part.shape

    def body(part_ref, out_ref, own, got_sib, chip_sum, got_ici, lsem, s1, r1, s2, r2):
        x, y, c = _place()
        chips = [(x, y), (1 - x, y), (x, 1 - y), (1 - x, 1 - y)]
        loc = [pltpu.make_async_copy(part_ref.at[_slot(*chips[kk], c)], own.at[kk], lsem.at[kk]) for kk in range(4)]
        d2d = [pltpu.make_async_remote_copy(
            src_ref=part_ref.at[_slot(*chips[kk], 1 - c)], dst_ref=got_sib.at[kk],
            send_sem=s1.at[kk], recv_sem=r1.at[kk], device_id=(x, y, 1 - c), device_id_type=MESH) for kk in range(4)]
        for cp in loc + d2d:
            cp.start()
        ici = [pltpu.make_async_remote_copy(
            src_ref=chip_sum.at[kk - 1], dst_ref=got_ici.at[kk - 1],
            send_sem=s2.at[kk - 1], recv_sem=r2.at[kk - 1], device_id=(*chips[kk], c), device_id_type=MESH)
            for kk in range(1, 4)]
        for kk in (1, 2, 3):
            loc[kk].wait()
            d2d[kk].wait_recv()
            chip_sum[kk - 1] = (own[kk].astype(F32) + got_sib[kk].astype(F32)).astype(BF16)
            ici[kk - 1].start()
        loc[0].wait()
        d2d[0].wait_recv()
        acc = own[0].astype(F32) + got_sib[0].astype(F32)
        for cp in ici:
            cp.wait_recv()
        out_ref[...] = ((acc + got_ici[0].astype(F32)) + got_ici[1].astype(F32)) + got_ici[2].astype(F32)
        for cp in d2d + ici:
            cp.wait_send()

    return pl.pallas_call(
        body, name=name,
        in_specs=[pl.BlockSpec(memory_space=pl.ANY)],
        out_specs=pl.BlockSpec(memory_space=pltpu.VMEM),
        out_shape=jax.ShapeDtypeStruct((r, C), F32),
        scratch_shapes=[pltpu.VMEM((4, r, C), BF16), pltpu.VMEM((4, r, C), BF16), pltpu.VMEM((3, r, C), BF16),
                        pltpu.VMEM((3, r, C), BF16),
                        pltpu.SemaphoreType.DMA((4,)), pltpu.SemaphoreType.DMA((4,)), pltpu.SemaphoreType.DMA((4,)),
                        pltpu.SemaphoreType.DMA((3,)), pltpu.SemaphoreType.DMA((3,))],
        compiler_params=pltpu.CompilerParams(vmem_limit_bytes=VMEM_LIMIT),
    )(part)


def _all_reduce_small(v, name):
    R, C = v.shape

    def body(v_ref, out_ref, got, send_sems, recv_sems):
        x, y, c = _place()
        mine = _slot(x, y, c)
        copies = []
        for kk in range(1, N_DEV):
            fx, fy, fc = kk >> 2 & 1, kk >> 1 & 1, kk & 1
            peer = (1 - x if fx else x, 1 - y if fy else y, 1 - c if fc else c)
            copies.append(pltpu.make_async_remote_copy(
                src_ref=v_ref, dst_ref=got.at[mine], send_sem=send_sems.at[kk - 1], recv_sem=recv_sems.at[kk - 1],
                device_id=peer, device_id_type=MESH))
        for cp in copies:
            cp.start()
        got[mine] = v_ref[...]
        for cp in copies:
            cp.wait_recv()
        acc = got[0]
        for s in range(1, N_DEV):
            acc = acc + got[s]
        out_ref[...] = acc
        for cp in copies:
            cp.wait_send()

    return pl.pallas_call(
        body, name=name,
        in_specs=[pl.BlockSpec(memory_space=pltpu.VMEM)], out_specs=pl.BlockSpec(memory_space=pltpu.VMEM),
        out_shape=jax.ShapeDtypeStruct((R, C), F32),
        scratch_shapes=[pltpu.VMEM((N_DEV, R, C), F32), pltpu.SemaphoreType.DMA((N_DEV - 1,)),
                        pltpu.SemaphoreType.DMA((N_DEV - 1,))],
    )(v)


_HBM = pl.BlockSpec(memory_space=pltpu.HBM)
_SEM = pl.BlockSpec(memory_space=pltpu.SEMAPHORE)
_EFFECT = pltpu.SideEffectType.DATAFLOW_SIDE_EFFECTING


def _peers(x, y, c):
    out = []
    for kk in range(1, N_DEV):
        fx, fy, fc = kk >> 2 & 1, kk >> 1 & 1, kk & 1
        out.append((1 - x if fx else x, 1 - y if fy else y, 1 - c if fc else c))
    return out


def _send_start(src, per_peer, name, dep):
    blk = src.shape[1:] if per_peer else src.shape

    def body(src_ref, land_ref, dep_ref, send_sems, recv_sems, src_thru, land_thru, token):
        x, y, c = _place()
        mine = _slot(x, y, c)
        for kk, peer in enumerate(_peers(x, y, c)):
            pltpu.make_async_remote_copy(
                src_ref=src_ref.at[_slot(*peer)] if per_peer else src_ref, dst_ref=land_ref.at[mine],
                send_sem=send_sems.at[kk], recv_sem=recv_sems.at[kk], device_id=peer, device_id_type=MESH).start()
        token[...] = jnp.zeros_like(token)

    land = lax.empty((N_DEV,) + tuple(blk), src.dtype)
    *handles, token = pl.pallas_call(
        body, name=name,
        out_shape=(pltpu.SemaphoreType.DMA((N_DEV - 1,)), pltpu.SemaphoreType.DMA((N_DEV - 1,)),
                   pltpu.HBM(src.shape, src.dtype), pltpu.HBM(land.shape, land.dtype),
                   jax.ShapeDtypeStruct((8, 128), F32)),
        in_specs=(_HBM, _HBM, _ANY), out_specs=(_SEM, _SEM, _HBM, _HBM, pl.BlockSpec(memory_space=pltpu.VMEM)),
        input_output_aliases={0: 2, 1: 3},
        compiler_params=pltpu.CompilerParams(has_side_effects=_EFFECT),
    )(pltpu.with_memory_space_constraint(src, pltpu.HBM), pltpu.with_memory_space_constraint(land, pltpu.HBM), dep)
    return handles, token


def _send_wait(handles, after, name):
    send_sems, recv_sems, src_thru, land_thru = handles

    def body(src_ref, land_ref, send_sems, recv_sems, after_ref, src_dead, got_ref):
        me = _place()
        for kk in range(N_DEV - 1):
            cp = pltpu.make_async_remote_copy(
                src_ref=land_ref.at[0], dst_ref=land_ref.at[0], send_sem=send_sems.at[kk], recv_sem=recv_sems.at[kk],
                device_id=me, device_id_type=MESH)
            cp.wait_send()
            cp.wait_recv()

    return pl.pallas_call(
        body, name=name,
        out_shape=(pltpu.HBM(src_thru.shape, src_thru.dtype), pltpu.HBM(land_thru.shape, land_thru.dtype)),
        in_specs=(_HBM, _HBM, _SEM, _SEM, pl.BlockSpec(memory_space=pl.ANY)), out_specs=(_HBM, _HBM),
        input_output_aliases={0: 0, 1: 1},
        compiler_params=pltpu.CompilerParams(has_side_effects=_EFFECT),
    )(src_thru, land_thru, send_sems, recv_sems, after)


def _sum_slots(land, name):
    _, R, C = land.shape
    tm = R if R <= 512 else 512

    def body(x_ref, o_ref):
        acc = x_ref[0].astype(F32)
        for j in range(1, N_DEV):
            acc = acc + x_ref[j].astype(F32)
        o_ref[...] = acc

    return pl.pallas_call(
        body, name=name, grid=(R // tm,),
        in_specs=[pl.BlockSpec((N_DEV, tm, C), lambda i: (0, i, 0))], out_specs=pl.BlockSpec((tm, C), lambda i: (i, 0)),
        out_shape=jax.ShapeDtypeStruct((R, C), F32), compiler_params=_cparams(("parallel",)),
    )(land)


def _adamw(w, g, m, v, name):
    R, C = w.shape
    tm = R if R <= 512 else 256

    def fn(w, g, m, v):
        m2 = ADAM_B1 * m + (1.0 - ADAM_B1) * g
        v2 = ADAM_B2 * v + (1.0 - ADAM_B2) * (g * g)
        m_hat = m2 / (1.0 - ADAM_B1 ** ADAM_STEP)
        v_hat = v2 / (1.0 - ADAM_B2 ** ADAM_STEP)
        delta = -ADAM_LR * (m_hat / (jnp.sqrt(v_hat) + ADAM_EPS) + ADAM_WD * w)
        return (delta, m2, v2), ()
    return _rowwise(fn, [w, g, m, v], [], [(C, F32)] * 3, [], tm=tm, name=name)


SMALL = ["norm_mix_pre", "norm_mix_post", "norm_mlp_pre", "norm_mlp_post", "norm_ple_post",
         "conv_b", "ssd_norm_g", "dt_bias", "a_log", "d_skip"]


def _pad_row(v, width=D):
    return jnp.pad(v, ((0, 0), (0, width - v.shape[1])))


def kernel(x, p, positions, norm_mix_pre, norm_mix_post, w_in, conv_w, conv_b, dt_bias, a_log, d_skip, ssd_norm_g, w_out, norm_mlp_pre, norm_mlp_post, w_up, w_down, w_ple_gate, w_ple_proj, norm_ple_post, loss_target, m_norm_mix_pre, m_norm_mix_post, m_w_in, m_conv_w, m_conv_b, m_dt_bias, m_a_log, m_d_skip, m_ssd_norm_g, m_w_out, m_norm_mlp_pre, m_norm_mlp_post, m_w_up, m_w_down, m_w_ple_gate, m_w_ple_proj, m_norm_ple_post, v_norm_mix_pre, v_norm_mix_post, v_w_in, v_conv_w, v_conv_b, v_dt_bias, v_a_log, v_d_skip, v_ssd_norm_g, v_w_out, v_norm_mlp_pre, v_norm_mlp_post, v_w_up, v_w_down, v_w_ple_gate, v_w_ple_proj, v_norm_ple_post):
    args = dict(locals())
    x2, p2, tgt = x[0], p[0, 0], loss_target[0]
    g1, g2, g3, g4, g5 = norm_mix_pre, norm_mix_post, norm_mlp_pre, norm_mlp_post, norm_ple_post

    me = _slot(*_place())
    pack_in = jnp.pad(w_in[0].T, ((0, W_IN_SHARD_PAD - W_IN_SHARD), (0, 0))).astype(BF16)
    pack_rest = jnp.concatenate([
        w_out[0],
        w_up[0].T,
        w_down[0],
        w_ple_gate[0],
        w_ple_proj[0].T.reshape(32, D),
    ], axis=0).astype(BF16)
    conv_pack = jnp.pad(conv_w[0], ((0, 4), (0, 32)))
    gin, gconv = _all_gather([pack_in, conv_pack], "gather_w_in")
    rest_handles, tok_rest = _send_start(pack_rest, False, "gather_rest_start", gconv)
    w_inT = gin[:, :W_IN_SHARD].reshape(IN_W, D)
    w_qkvzT = w_inT[:4 * AW]
    w_xbcdtT = jnp.pad(w_inT[4 * AW:], ((0, DT_PAD - HEADS), (0, 0)))
    conv_full = gconv[:, :CONV_K, :96].transpose(1, 0, 2).reshape(CONV_K, CONV_CH)

    inv_freq = ROPE_THETA ** (-jnp.arange(HD // 2, dtype=F32) * 2.0 / HD)
    ang = positions[0].astype(F32)[:, None] * inv_freq
    cos, sin = jnp.cos(ang), jnp.sin(ang)
    cosf = jnp.tile(jnp.concatenate([cos, cos], axis=1), (1, HEADS))
    sins = jnp.tile(jnp.concatenate([-sin, sin], axis=1), (1, HEADS))

    bias_w, alog_w, dsk_w = _pad_row(dt_bias, DT_PAD), _pad_row(a_log, DT_PAD), _pad_row(d_skip, DT_PAD)
    rms_pre = lambda a, r, g: a * r * g

    (r1,) = _rowwise(lambda a: ((_rstd(a),), ()), [x2], [], [(1, F32)], [], tm=512, name="rstd_x")
    qkvz = _mm(x2, w_qkvzT, tb=True, tm=512, tn=1024, tk=1024, a_pre=rms_pre, a_rows=[r1], a_cols=[g1], name="proj_qkvz",
               deps=[tok_rest])
    xbcdt = _mm(x2, w_xbcdtT, tb=True, tm=512, tn=896, tk=1024, a_pre=rms_pre, a_rows=[r1], a_cols=[g1], name="proj_xbcdt")

    qr, kr, vb = _rope_fwd(qkvz, cosf, sins)
    outs, lses = [], []
    for d in DILATIONS:
        L = T // d
        o, l = _attn_fwd(qr.reshape(L, d * AW), kr.reshape(L, d * AW), vb.reshape(L, d * AW), d)
        outs.append(o.reshape(T, AW))
        lses.append(l.reshape(T, AW))
    attn, lse = _attn_merge(outs, lses)

    act = _conv_fwd(xbcdt, conv_full, conv_b)
    y_ssd, states = _ssd_fwd(act, xbcdt, bias_w, alog_w, dsk_w)

    def gated_fwd(y, z, a, gs):
        gi = y * (z * _sigmoid(z))
        return (jnp.concatenate([a, gi * _rstd(gi) * gs], axis=1),), ()
    (cat,) = _rowwise(gated_fwd, [y_ssd, (qkvz, AW, 3), attn], [ssd_norm_g], [(D, F32)], [], tm=512, name="gated_norm")

    pack_back, grest = _send_wait(rest_handles, cat, "gather_rest_wait")
    grest = lax.dynamic_update_slice(grest, pack_back[None], (me, 0, 0))
    w_o = grest[:, 0:128].reshape(D, D)
    w_upT = grest[:, 128:640].reshape(DFF, D)
    w_dn = grest[:, 640:1152].reshape(DFF, D)
    w_gate = grest[:, 1152:1280].reshape(D, D)
    w_projT = grest[:, 1280:1312].reshape(D, PLE)

    mix = _mm(cat, w_o, tm=512, tn=1024, tk=1024, name="mix_out")

    def post1(xx, mm, ga, gb):
        h = xx + mm * _rstd(mm) * ga
        return (h, _rstd(h)), ()
    h1, r3 = _rowwise(post1, [x2, mix], [g2, g3], [(D, F32), (1, F32)], [], tm=512, name="post_mix")

    a_up, ff, u2 = _mlp_fwd(h1, r3, g3, w_upT, w_dn)
    relu2 = lambda a: jnp.square(jnp.maximum(a.astype(F32), 0.0))
    (h2,) = _rowwise(lambda hh, f, g: ((hh + f * _rstd(f) * g,), ()), [h1, ff], [g4], [(D, F32)], [], tm=512, name="post_mlp")

    gp = _mm(h2, w_gate, tm=512, tn=1024, tk=1024, name="ple_gate")
    pp = _mm(p2, w_projT, tb=True, tm=512, tn=1024, tk=256, name="ple_proj")

    def final(hh, gpre, ppv, tg, g):
        sg = _sigmoid(gpre)
        ple = ppv * sg
        r = _rstd(ple)
        n = ple * r
        h3 = hh + n * g
        e = h3 - tg
        dh3 = e * (1.0 / D)
        dple = _rms_bwd(n, r, g, dh3)
        return (dh3, dple * sg, dple * ppv * sg * (1.0 - sg)), (_colsum(dh3 * n), _colsum(0.5 * e * e * (1.0 / D)))
    dh3, dpp, dgp, dg5, loss_vec = _rowwise(final, [h2, gp, pp, tgt], [g5], [(D, F32)] * 3, [(1, D), (1, D)],
                                            tm=256, name="loss_ple_bwd")

    gw_projT = _mm(dpp, p2, ta=True, tm=512, tn=256, tk=1024, out_dtypes=(BF16,), name="gw_ple_proj")
    gw_gate = _mm(h2, dgp, ta=True, tm=512, tn=1024, tk=1024, out_dtypes=(BF16,), name="gw_ple_gate")
    rs_proj, tok_proj = _send_start(gw_projT.reshape(N_DEV, 32, D), True, "rs_start_w_proj", g1)
    rs_gate, tok_gate = _send_start(gw_gate.reshape(N_DEV, 128, D), True, "rs_start_w_gate", g1)
    dh2_g = _mm(dgp, w_gate, tb=True, tm=512, tn=1024, tk=1024, name="dx_ple_gate", deps=[tok_proj, tok_gate])

    def bwd_mlp_post(d3, dg_, f, g):
        dh2 = d3 + dg_
        r = _rstd(f)
        n = f * r
        return (dh2, _rms_bwd(n, r, g, dh2)), (_colsum(dh2 * n),)
    dh2, dff, dg4 = _rowwise(bwd_mlp_post, [dh3, dh2_g, ff], [g4], [(D, F32)] * 2, [(1, D)], tm=256, name="bwd_post_mlp")

    gw_dn = _mm(a_up, dff, ta=True, tm=1024, tn=1024, tk=512, a_pre=relu2, out_dtypes=(BF16,), name="gw_mlp_down")
    rs_dn, tok_dn = _send_start(gw_dn.reshape(N_DEV, 512, D), True, "rs_start_w_down", g1)
    da_up, du2 = _mlp_dx(dff, a_up, w_upT, w_dn, tok_dn)
    gw_upT = _mm(da_up, u2, ta=True, tm=1024, tn=1024, tk=512, out_dtypes=(BF16,), name="gw_mlp_up")
    rs_up, tok_up = _send_start(gw_upT.reshape(N_DEV, 512, D), True, "rs_start_w_up", g1)

    def bwd_mix_post(d2, du, hh, rr, mm, ga, gb):
        n3 = hh * rr
        dh1 = d2 + _rms_bwd(n3, rr, gb, du)
        r = _rstd(mm)
        n2 = mm * r
        return (dh1, _rms_bwd(n2, r, ga, dh1)), (_colsum(du * n3), _colsum(dh1 * n2))
    dh1, dmix, dg3, dg2 = _rowwise(bwd_mix_post, [dh2, du2, h1, r3, mix], [g2, g3], [(D, F32)] * 2, [(1, D), (1, D)],
                                   tm=256, name="bwd_post_mix", deps=[tok_up])

    gw_o = _mm(cat, dmix, ta=True, tm=512, tn=1024, tk=1024, out_dtypes=(BF16,), name="gw_out")
    rs_o, tok_o = _send_start(gw_o.reshape(N_DEV, 128, D), True, "rs_start_w_out", g1)
    dcat = _mm(dmix, w_o, tb=True, tm=512, tn=1024, tk=1024, name="dx_out", deps=[tok_o])

    def gated_bwd(y, z, dyn, gs):
        sg = _sigmoid(z)
        sz = z * sg
        gi = y * sz
        r = _rstd(gi)
        n = gi * r
        dgi = _rms_bwd(n, r, gs, dyn)
        return (dgi * sz, dgi * y * (sg * (1.0 + z * (1.0 - sg)))), (_colsum(dyn * n),)
    dy_ssd, dz, dgs = _rowwise(gated_bwd, [y_ssd, (qkvz, AW, 3), (dcat, AW, 1)], [ssd_norm_g], [(AW, F32)] * 2, [(1, AW)],
                               tm=512, name="bwd_gated_norm")

    dact, ddtw, ssd_par = _ssd_bwd(act, xbcdt, bias_w, alog_w, dsk_w, states, dy_ssd)
    dxbcdt, conv_par = _conv_bwd(xbcdt, dact, ddtw, conv_full, conv_b)

    dattn = dcat[:, :AW]
    dqs, dks, dvs = [], [], []
    for d in DILATIONS:
        L = T // d
        rs = lambda t: t.reshape(L, d * AW)
        dq, dk, dv = _attn_bwd(rs(qr), rs(kr), rs(vb), rs(dattn), rs(attn), rs(lse), d)
        dqs.append(dq.reshape(T, AW))
        dks.append(dk.reshape(T, AW))
        dvs.append(dv.reshape(T, AW))
    dqkvz = _rope_bwd(dqs, dks, dvs, dz, cosf, sins)

    du1a = _mm(dqkvz, w_qkvzT, tm=512, tn=1024, tk=1024, name="dx_qkvz")
    du1b = _mm(dxbcdt, w_xbcdtT, tm=512, tn=1024, tk=896, name="dx_xbcdt")
    pre1 = dict(b_pre=rms_pre, b_rows=[r1], b_cols=[g1], out_dtypes=(BF16,))
    gw_qkvzT = _mm(dqkvz, x2, ta=True, tm=1024, tn=1024, tk=512, name="gw_qkvz", **pre1)
    gw_xbcdtT = _mm(dxbcdt, x2, ta=True, tm=896, tn=1024, tk=512, name="gw_xbcdt", **pre1)

    def bwd_in(d1, ua, ub, xx, rr, g):
        n = xx * rr
        du = ua + ub
        return (d1 + _rms_bwd(n, rr, g, du),), (_colsum(du * n),)
    grad_x, dg1 = _rowwise(bwd_in, [dh1, du1a, du1b, x2, r1], [g1], [(D, F32)], [(1, D)], tm=256, name="bwd_pre_mix")

    gw_inT = jnp.concatenate([gw_qkvzT, gw_xbcdtT], axis=0)[:IN_W]
    gw_inT = jnp.pad(gw_inT.reshape(N_DEV, W_IN_SHARD, D), ((0, 0), (0, W_IN_SHARD_PAD - W_IN_SHARD), (0, 0)))
    g_inT = _reduce_scatter(gw_inT, "rs_w_in")

    def scatter_finish(handles, nm):
        part, land = _send_wait(handles, g_inT, "rs_wait_" + nm)
        own = lax.dynamic_slice(part, (me, 0, 0), (1,) + part.shape[1:])
        return _sum_slots(lax.dynamic_update_slice(land, own, (me, 0, 0)), "rs_sum_" + nm)
    g_out = scatter_finish(rs_o, "w_out")
    g_upT = scatter_finish(rs_up, "w_up")
    g_dn = scatter_finish(rs_dn, "w_down")
    g_gate = scatter_finish(rs_gate, "w_gate")
    g_projT = scatter_finish(rs_proj, "w_proj")

    small = jnp.concatenate([
        dg1, dg2, dg3, dg4, dg5,
        _pad_row(conv_par[4:5]), _pad_row(dgs), _pad_row(ssd_par[0:1]), _pad_row(ssd_par[1:2]), _pad_row(ssd_par[2:3]),
        _pad_row(conv_par[0:4]), loss_vec, jnp.zeros((1, D), F32),
    ], axis=0)
    small = _all_reduce_small(small, "reduce_small")
    loss = jnp.sum(small[14])
    me = lax.axis_index("x") * 4 + lax.axis_index("y") * 2 + lax.axis_index("c")
    g_conv_w = lax.dynamic_slice(small[10:14, :CONV_CH], (0, me * 96), (CONV_K, 96))

    grads = {
        "w_in": g_inT[:W_IN_SHARD].T[None], "w_out": g_out[None], "w_up": g_upT.T[None], "w_down": g_dn[None],
        "w_ple_gate": g_gate[None], "w_ple_proj": g_projT.reshape(128, PLE).T[None], "conv_w": g_conv_w[None],
        "norm_mix_pre": small[0:1], "norm_mix_post": small[1:2], "norm_mlp_pre": small[2:3], "norm_mlp_post": small[3:4],
        "norm_ple_post": small[4:5], "conv_b": small[5:6, :CONV_CH], "ssd_norm_g": small[6:7, :AW],
        "dt_bias": small[7:8, :HEADS], "a_log": small[8:9, :HEADS], "d_skip": small[9:10, :HEADS],
    }
    delta, new_m, new_v = {}, {}, {}
    for nme in ["w_in", "w_out", "w_up", "w_down", "w_ple_gate", "w_ple_proj"]:
        dl, mm_, vv_ = _adamw(args[nme][0], grads[nme][0], args["m_" + nme][0], args["v_" + nme][0], "adamw_" + nme)
        delta[nme], new_m[nme], new_v[nme] = dl[None], mm_[None], vv_[None]

    def pack_small(prefix):
        rows = [_pad_row(args[prefix + nme]) for nme in SMALL]
        rows.append(_pad_row(args[prefix + "conv_w"][0]))
        rows.append(jnp.zeros((2, D), F32))
        return jnp.concatenate(rows, axis=0)
    g_small = jnp.concatenate([small[0:10], _pad_row(g_conv_w), jnp.zeros((2, D), F32)], axis=0)
    dl, mm_, vv_ = _adamw(pack_small(""), g_small, pack_small("m_"), pack_small("v_"), "adamw_small")
    for i, nme in enumerate(SMALL):
        wdt = args[nme].shape[1]
        delta[nme], new_m[nme], new_v[nme] = dl[i:i + 1, :wdt], mm_[i:i + 1, :wdt], vv_[i:i + 1, :wdt]
    delta["conv_w"], new_m["conv_w"], new_v["conv_w"] = dl[None, 10:14, :96], mm_[None, 10:14, :96], vv_[None, 10:14, :96]

    order = ["norm_mix_pre", "norm_mix_post", "w_in", "conv_w", "conv_b", "dt_bias", "a_log", "d_skip", "ssd_norm_g",
             "w_out", "norm_mlp_pre", "norm_mlp_post", "w_up", "w_down", "w_ple_gate", "w_ple_proj", "norm_ple_post"]
    return (loss, grad_x[None], *[grads[n] for n in order], *[delta[n] for n in order],
            *[new_m[n] for n in order], *[new_v[n] for n in order])
```

```python
import functools
import math

import jax
import jax.numpy as jnp
from jax import lax
from jax.experimental import pallas as pl
from jax.experimental.pallas import tpu as pltpu

F32 = jnp.float32
BF16 = jnp.bfloat16
MESH = pl.DeviceIdType.MESH
HIGHEST = lax.Precision.HIGHEST

N_DEV = 8
T = 4096
D = 1024
HEADS = 8
HD = 64
AW = 512
NS = 128
CONV_K = 4
CONV_CH = 768
CHUNK = 128
DFF = 4096
PLE = 256
EPS = 1e-6
ROPE_THETA = 10000.0
DILATIONS = (1, 4, 16)
QBLK = 128
NEG = -1e30
IN_W = 2824
W_IN_SHARD = 353
W_IN_SHARD_PAD = 384
DT_PAD = 128

ADAM_LR, ADAM_B1, ADAM_B2, ADAM_EPS, ADAM_WD, ADAM_STEP = 0.001, 0.9, 0.999, 1e-08, 0.01, 10

VMEM_LIMIT = 56 * 1024 * 1024


_ANY = pl.BlockSpec(memory_space=pl.ANY)


def _cparams(sem=None):
    return pltpu.CompilerParams(dimension_semantics=sem, vmem_limit_bytes=VMEM_LIMIT)


def _dot(a, b, ca, cb, precision=None):
    return lax.dot_general(a, b, (((ca,), (cb,)), ((), ())), preferred_element_type=F32, precision=precision)


def _nn(a, b):
    return _dot(a, b, 1, 0)


def _nt(a, b):
    return _dot(a, b, 1, 1)


def _tn(a, b):
    return _dot(a, b, 0, 0)


def _sigmoid(x):
    return 1.0 / (1.0 + jnp.exp(-x))


def _softplus(x):
    return jnp.maximum(x, 0.0) + jnp.log(1.0 + jnp.exp(-jnp.abs(x)))


def _mm(a, b, *, ta=False, tb=False, tm, tn, tk, name,
        a_pre=None, a_rows=(), a_cols=(), b_pre=None, b_rows=(), b_cols=(),
        epi=None, epi_tiles=(), out_dtypes=(F32,), deps=()):
    if ta:
        K, M = a.shape
    else:
        M, K = a.shape
    if tb:
        N, K2 = b.shape
    else:
        K2, N = b.shape
    assert K == K2 and M % tm == 0 and N % tn == 0 and K % tk == 0, (name, a.shape, b.shape)
    nk = K // tk
    if ta:
        a_spec = pl.BlockSpec((tk, tm), lambda i, j, k: (k, i))
        a_row_specs = [pl.BlockSpec((tk, 1), lambda i, j, k: (k, 0)) for _ in a_rows]
        a_col_specs = [pl.BlockSpec((1, tm), lambda i, j, k: (0, i)) for _ in a_cols]
    else:
        a_spec = pl.BlockSpec((tm, tk), lambda i, j, k: (i, k))
        a_row_specs = [pl.BlockSpec((tm, 1), lambda i, j, k: (i, 0)) for _ in a_rows]
        a_col_specs = [pl.BlockSpec((1, tk), lambda i, j, k: (0, k)) for _ in a_cols]
    if tb:
        b_spec = pl.BlockSpec((tn, tk), lambda i, j, k: (j, k))
        b_row_specs = [pl.BlockSpec((tn, 1), lambda i, j, k: (j, 0)) for _ in b_rows]
        b_col_specs = [pl.BlockSpec((1, tk), lambda i, j, k: (0, k)) for _ in b_cols]
    else:
        b_spec = pl.BlockSpec((tk, tn), lambda i, j, k: (k, j))
        b_row_specs = [pl.BlockSpec((tk, 1), lambda i, j, k: (k, 0)) for _ in b_rows]
        b_col_specs = [pl.BlockSpec((1, tn), lambda i, j, k: (0, j)) for _ in b_cols]
    o_spec = pl.BlockSpec((tm, tn), lambda i, j, k: (i, j))
    na, nb, ne, no = len(a_rows) + len(a_cols), len(b_rows) + len(b_cols), len(epi_tiles), len(out_dtypes)

    def body(*refs):
        a_ref, b_ref = refs[0], refs[1]
        a_ex = refs[2:2 + na]
        b_ex = refs[2 + na:2 + na + nb]
        e_ex = refs[2 + na + nb:2 + na + nb + ne]
        first_out = 2 + na + nb + ne + len(deps)
        outs = refs[first_out:first_out + no]
        acc = refs[-1]
        k = pl.program_id(2)

        @pl.when(k == 0)
        def _():
            acc[...] = jnp.zeros_like(acc)

        at = a_ref[...]
        if a_pre is not None:
            at = a_pre(at, *[r[...] for r in a_ex])
        bt = b_ref[...]
        if b_pre is not None:
            bt = b_pre(bt, *[r[...] for r in b_ex])
        acc[...] += _dot(at.astype(BF16), bt.astype(BF16), 0 if ta else 1, 1 if tb else 0)

        @pl.when(k == nk - 1)
        def _():
            res = acc[...]
            vals = epi(res, *[r[...] for r in e_ex]) if epi is not None else (res,)
            for o_ref, val in zip(outs, vals):
                o_ref[...] = val.astype(o_ref.dtype)

    outs = pl.pallas_call(
        body, name=name,
        grid=(M // tm, N // tn, nk),
        in_specs=([a_spec, b_spec] + a_row_specs + a_col_specs + b_row_specs + b_col_specs + [o_spec] * ne
                  + [_ANY] * len(deps)),
        out_specs=[o_spec] * no,
        out_shape=[jax.ShapeDtypeStruct((M, N), dt) for dt in out_dtypes],
        scratch_shapes=[pltpu.VMEM((tm, tn), F32)],
        compiler_params=_cparams(("parallel", "parallel", "arbitrary")),
    )(a, b, *a_rows, *a_cols, *b_rows, *b_cols, *epi_tiles, *deps)
    return outs[0] if no == 1 else outs


MLP_TM = 1024
MLP_TC = 512


def _mlp_fwd(h, r, g, w_upT, w_dn):
    nc = DFF // MLP_TC

    def body(h_ref, r_ref, g_ref, wu_ref, wd_ref, a_ref, ff_ref, u_ref, acc, u_scr):
        c = pl.program_id(1)

        @pl.when(c == 0)
        def _():
            u = (h_ref[...] * r_ref[...] * g_ref[...]).astype(BF16)
            u_scr[...] = u
            u_ref[...] = u
            acc[...] = jnp.zeros_like(acc)
        a = _nt(u_scr[...], wu_ref[...])
        a_ref[...] = a.astype(BF16)
        acc[...] += _nn(jnp.square(jnp.maximum(a, 0.0)).astype(BF16), wd_ref[...])

        @pl.when(c == nc - 1)
        def _():
            ff_ref[...] = acc[...]

    row = pl.BlockSpec((MLP_TM, D), lambda i, c: (i, 0))
    wsp = pl.BlockSpec((MLP_TC, D), lambda i, c: (c, 0))
    return pl.pallas_call(
        body, name="mlp_fwd", grid=(T // MLP_TM, nc),
        in_specs=[row, pl.BlockSpec((MLP_TM, 1), lambda i, c: (i, 0)), pl.BlockSpec((1, D), lambda i, c: (0, 0)), wsp, wsp],
        out_specs=[pl.BlockSpec((MLP_TM, MLP_TC), lambda i, c: (i, c)), row, row],
        out_shape=[jax.ShapeDtypeStruct((T, DFF), BF16), jax.ShapeDtypeStruct((T, D), F32), jax.ShapeDtypeStruct((T, D), BF16)],
        scratch_shapes=[pltpu.VMEM((MLP_TM, D), F32), pltpu.VMEM((MLP_TM, D), BF16)],
        compiler_params=_cparams(("parallel", "arbitrary")),
    )(h, r, g, w_upT, w_dn)


def _mlp_dx(dff, a, w_upT, w_dn, dep):
    nc = DFF // MLP_TC

    def body(d_ref, a_ref, wu_ref, wd_ref, dep_ref, da_ref, du_ref, acc, d_scr):
        c = pl.program_id(1)

        @pl.when(c == 0)
        def _():
            d_scr[...] = d_ref[...].astype(BF16)
            acc[...] = jnp.zeros_like(acc)
        da = (_nt(d_scr[...], wd_ref[...]) * (2.0 * jnp.maximum(a_ref[...].astype(F32), 0.0))).astype(BF16)
        da_ref[...] = da
        acc[...] += _nn(da, wu_ref[...])

        @pl.when(c == nc - 1)
        def _():
            du_ref[...] = acc[...]

    row = pl.BlockSpec((MLP_TM, D), lambda i, c: (i, 0))
    wsp = pl.BlockSpec((MLP_TC, D), lambda i, c: (c, 0))
    chunk = pl.BlockSpec((MLP_TM, MLP_TC), lambda i, c: (i, c))
    return pl.pallas_call(
        body, name="mlp_dx", grid=(T // MLP_TM, nc),
        in_specs=[row, chunk, wsp, wsp, _ANY], out_specs=[chunk, row],
        out_shape=[jax.ShapeDtypeStruct((T, DFF), BF16), jax.ShapeDtypeStruct((T, D), F32)],
        scratch_shapes=[pltpu.VMEM((MLP_TM, D), F32), pltpu.VMEM((MLP_TM, D), BF16)],
        compiler_params=_cparams(("parallel", "arbitrary")),
    )(dff, a, w_upT, w_dn, dep)


def _rowwise(fn, rows, vecs, out_rows, out_sums, *, tm, name, deps=()):
    specs, arrs = [], []
    R = None
    for r in rows:
        if isinstance(r, tuple):
            arr, width, cb = r
            specs.append(pl.BlockSpec((tm, width), lambda i, cb=cb: (i, cb)))
        else:
            arr = r
            specs.append(pl.BlockSpec((tm, arr.shape[1]), lambda i: (i, 0)))
        R = arr.shape[0] if R is None else R
        assert arr.shape[0] == R, name
        arrs.append(arr)
    assert R % tm == 0, name
    for v in vecs:
        specs.append(pl.BlockSpec(v.shape, lambda i: (0, 0)))
        arrs.append(v)
    nr, nv, no, ns = len(rows), len(vecs), len(out_rows), len(out_sums)
    out_specs = [pl.BlockSpec((tm, w), lambda i: (i, 0)) for w, _ in out_rows]
    out_specs += [pl.BlockSpec(s, lambda i: (0, 0)) for s in out_sums]
    out_shape = [jax.ShapeDtypeStruct((R, w), dt) for w, dt in out_rows]
    out_shape += [jax.ShapeDtypeStruct(s, F32) for s in out_sums]

    nd = len(deps)

    def body(*refs):
        ins = [r[...] for r in refs[:nr + nv]]
        o_refs = refs[nr + nv + nd:nr + nv + nd + no]
        s_refs = refs[nr + nv + nd + no:]
        o_vals, s_vals = fn(*ins)
        for ref, val in zip(o_refs, o_vals):
            ref[...] = val.astype(ref.dtype)
        if ns:
            @pl.when(pl.program_id(0) == 0)
            def _():
                for ref in s_refs:
                    ref[...] = jnp.zeros_like(ref)
            for ref, val in zip(s_refs, s_vals):
                ref[...] += val

    outs = pl.pallas_call(
        body, name=name, grid=(R // tm,), in_specs=specs + [_ANY] * nd, out_specs=out_specs, out_shape=out_shape,
        compiler_params=_cparams(("arbitrary",) if ns else ("parallel",)),
    )(*arrs, *deps)
    return outs


def _colsum(x):
    return jnp.sum(x, axis=0, keepdims=True)


def _rstd(x):
    return lax.rsqrt(jnp.mean(x * x, axis=-1, keepdims=True) + EPS)


def _rms_bwd(xn, r, g, dy):
    dn = dy * g
    return r * (dn - xn * jnp.mean(dn * xn, axis=-1, keepdims=True))


def _partner(t):
    lane = lax.broadcasted_iota(jnp.int32, t.shape, 1)
    up = pltpu.roll(t, 96, 1)
    down = pltpu.roll(t, 32, 1)
    return jnp.where((lane % 64) < 32, up, down)


SLABS = AW // 128


def _rows(r, n, d):
    return pl.ds(r, n, stride=d) if d > 1 else pl.ds(0, n)


def _undilate(src_ref, dst, d, tm):
    for r in range(d):
        for j in range(SLABS):
            dst[j][_rows(r, tm // d, d), :] = src_ref[:, pl.ds(r * AW + j * 128, 128)].astype(dst[j].dtype)


def _dilate(dst_ref, src, d, tm):
    for r in range(d):
        for j in range(SLABS):
            dst_ref[:, pl.ds(r * AW + j * 128, 128)] = src[j][_rows(r, tm // d, d), :].astype(dst_ref.dtype)


def _slab_scratch(n, tm):
    return [pltpu.VMEM((tm, 128), F32)] * (SLABS * n)


def _slab_groups(flat):
    return [flat[SLABS * i:SLABS * (i + 1)] for i in range(len(flat) // SLABS)]


def _slab_specs(tm, first):
    return [pl.BlockSpec((tm, 128), lambda i, j=j: (i, first + j)) for j in range(SLABS)]


def _dil_spec(tm, d):
    return pl.BlockSpec((tm // d, d * AW), lambda i: (i, 0))


ROPE_TM = 512


def _rope_fwd(qkvz, cos128, sin128):
    tm = ROPE_TM

    def body(*refs):
        q_refs, k_refs, v_refs = refs[0:4], refs[4:8], refs[8:12]
        c_ref, s_ref = refs[12], refs[13]
        outs = refs[14:]
        for di, d in enumerate(DILATIONS):
            oq, ok, ov = outs[3 * di:3 * di + 3]
            for r in range(d):
                rows = _rows(r, tm // d, d)
                c, s = c_ref[rows, :], s_ref[rows, :]
                for j in range(SLABS):
                    cols = pl.ds(r * AW + j * 128, 128)
                    q, k = q_refs[j][rows, :], k_refs[j][rows, :]
                    oq[:, cols] = ((q * c + _partner(q) * s) * (HD ** -0.5)).astype(BF16)
                    ok[:, cols] = (k * c + _partner(k) * s).astype(BF16)
                    ov[:, cols] = v_refs[j][rows, :].astype(BF16)

    tab = pl.BlockSpec((tm, 128), lambda i: (i, 0))
    out_specs, out_shape = [], []
    for d in DILATIONS:
        out_specs += [_dil_spec(tm, d)] * 3
        out_shape += [jax.ShapeDtypeStruct((T // d, d * AW), BF16)] * 3
    return pl.pallas_call(
        body, name="rope_fwd", grid=(T // tm,),
        in_specs=_slab_specs(tm, 0) + _slab_specs(tm, 4) + _slab_specs(tm, 8) + [tab, tab],
        out_specs=out_specs, out_shape=out_shape, compiler_params=_cparams(("parallel",)),
    )(*([qkvz] * 12), cos128, sin128)


def _rope_bwd(grads, dz, cos128, sin128):
    tm = 256

    def body(*refs):
        g_refs = refs[0:9]
        dz_ref, c_ref, s_ref, o_ref = refs[9], refs[10], refs[11], refs[12]
        scr = _slab_groups(refs[13:])
        for di, d in enumerate(DILATIONS[1:]):
            for t in range(3):
                _undilate(g_refs[3 * (di + 1) + t], scr[3 * di + t], d, tm)
        c, s = c_ref[...], s_ref[...]
        for j in range(SLABS):
            cols = pl.ds(j * 128, 128)
            tot = [g_refs[t][:, cols] + scr[t][j][...] + scr[3 + t][j][...] for t in range(3)]
            dqr = tot[0] * (HD ** -0.5)
            o_ref[:, pl.ds(j * 128, 128)] = dqr * c + _partner(dqr * s)
            o_ref[:, pl.ds(AW + j * 128, 128)] = tot[1] * c + _partner(tot[1] * s)
            o_ref[:, pl.ds(2 * AW + j * 128, 128)] = tot[2]
        o_ref[:, pl.ds(3 * AW, AW)] = dz_ref[...]

    tab = pl.BlockSpec((tm, 128), lambda i: (i, 0))
    in_specs, args = [], []
    for d, g in zip(DILATIONS, grads):
        in_specs += [_dil_spec(tm, d)] * 3
        args += list(g)
    return pl.pallas_call(
        body, name="rope_bwd", grid=(T // tm,),
        in_specs=in_specs + [pl.BlockSpec((tm, AW), lambda i: (i, 0)), tab, tab],
        out_specs=pl.BlockSpec((tm, 4 * AW), lambda i: (i, 0)),
        out_shape=jax.ShapeDtypeStruct((T, 4 * AW), F32),
        scratch_shapes=_slab_scratch(6, tm),
        compiler_params=_cparams(("parallel",)),
    )(*args, dz, cos128, sin128)


def _dilate_cols(x, first):
    tm = ROPE_TM

    def body(x0, x1, x2, x3, o4, o16):
        xs = (x0, x1, x2, x3)
        for o_ref, d in ((o4, 4), (o16, 16)):
            for r in range(d):
                for j in range(SLABS):
                    o_ref[:, pl.ds(r * AW + j * 128, 128)] = xs[j][_rows(r, tm // d, d), :]

    return pl.pallas_call(
        body, name="dilate_cols", grid=(T // tm,),
        in_specs=_slab_specs(tm, first), out_specs=[_dil_spec(tm, 4), _dil_spec(tm, 16)],
        out_shape=[jax.ShapeDtypeStruct((T // 4, 4 * AW), F32), jax.ShapeDtypeStruct((T // 16, 16 * AW), F32)],
        compiler_params=_cparams(("parallel",)),
    )(x, x, x, x)


def _band_masks():
    qi = lax.broadcasted_iota(jnp.int32, (QBLK, QBLK), 0)
    kj = lax.broadcasted_iota(jnp.int32, (QBLK, QBLK), 1)
    return kj >= qi, kj <= qi


def _attn_fwd(q, k, v, d):
    L = q.shape[0]
    nb = L // QBLK

    def body(q_ref, kp_ref, kc_ref, vp_ref, vc_ref, o_ref, l_ref):
        n = pl.program_id(1)
        mask_p, mask_c = _band_masks()
        bias = jnp.concatenate([jnp.where(mask_p, 0.0, NEG) + jnp.where(n > 0, 0.0, NEG),
                                jnp.where(mask_c, 0.0, NEG)], axis=1)
        s = []
        for h in range(HEADS):
            sl = pl.ds(HD * h, HD)
            qh = q_ref[:, sl]
            s.append(jnp.concatenate([_nt(qh, kp_ref[:, sl]), _nt(qh, kc_ref[:, sl])], axis=1))
        s = jnp.stack(s) + bias
        m = jnp.max(s, axis=2, keepdims=True)
        e = jnp.exp(s - m)
        den = jnp.sum(e, axis=2, keepdims=True)
        p = (e * (1.0 / den)).astype(BF16)
        lse = m + jnp.log(den)
        for h in range(HEADS):
            sl = pl.ds(HD * h, HD)
            o_ref[:, sl] = _nn(p[h, :, :QBLK], vp_ref[:, sl]) + _nn(p[h, :, QBLK:], vc_ref[:, sl])
            l_ref[:, sl] = jnp.broadcast_to(lse[h], (QBLK, HD))

    cur = pl.BlockSpec((QBLK, AW), lambda r, n: (n, r))
    prev = pl.BlockSpec((QBLK, AW), lambda r, n: (jnp.maximum(n - 1, 0), r))
    return pl.pallas_call(
        body, name=f"attn_fwd_d{d}", grid=(d, nb),
        in_specs=[cur, prev, cur, prev, cur], out_specs=[cur, cur],
        out_shape=[jax.ShapeDtypeStruct((L, d * AW), F32)] * 2,
        compiler_params=_cparams(("parallel", "parallel")),
    )(q, k, k, v, v)


def _attn_bwd(q, k, v, do, at, lse, d):
    L = q.shape[0]
    nb = L // QBLK

    def body(q0_ref, q1_ref, kp_ref, kc_ref, vp_ref, vc_ref, do0_ref, do1_ref, at0_ref, at1_ref,
             l0_ref, l1_ref, dq_ref, dk_ref, dv_ref):
        n = pl.program_id(1)
        mask_p, mask_c = _band_masks()
        prev_bias = jnp.where(mask_p, 0.0, NEG)
        bias = jnp.concatenate([prev_bias + jnp.where(n > 0, 0.0, NEG), jnp.where(mask_c, 0.0, NEG),
                                prev_bias + jnp.where(n < nb - 1, 0.0, NEG)], axis=1)
        s, dp, ls, dl, ops = [], [], [], [], []
        for h in range(HEADS):
            sl = pl.ds(HD * h, HD)
            one = pl.ds(HD * h, 1)
            q0, q1 = q0_ref[:, sl], q1_ref[:, sl]
            kp, kc, vp, vc = kp_ref[:, sl], kc_ref[:, sl], vp_ref[:, sl], vc_ref[:, sl]
            do0, do1 = do0_ref[:, sl], do1_ref[:, sl]
            do0b, do1b = do0.astype(BF16), do1.astype(BF16)
            s.append(jnp.concatenate([_nt(q0, kp), _nt(q0, kc), _nt(q1, kc)], axis=1))
            dp.append(jnp.concatenate([_nt(do0b, vp), _nt(do0b, vc), _nt(do1b, vc)], axis=1))
            dl0 = jnp.sum(do0 * at0_ref[:, sl], axis=1, keepdims=True)
            dl1 = jnp.sum(do1 * at1_ref[:, sl], axis=1, keepdims=True)
            dl.append(jnp.concatenate([jnp.broadcast_to(dl0, (QBLK, 2 * QBLK)), jnp.broadcast_to(dl1, (QBLK, QBLK))], axis=1))
            ls.append(jnp.concatenate([jnp.broadcast_to(l0_ref[:, one], (QBLK, 2 * QBLK)),
                                       jnp.broadcast_to(l1_ref[:, one], (QBLK, QBLK))], axis=1))
            ops.append((q0, q1, kp, kc, do0b, do1b))
        p = jnp.exp(jnp.stack(s) + bias - jnp.stack(ls))
        ds = (p * (jnp.stack(dp) - jnp.stack(dl))).astype(BF16)
        p = p.astype(BF16)
        for h in range(HEADS):
            sl = pl.ds(HD * h, HD)
            q0, q1, kp, kc, do0b, do1b = ops[h]
            dq_ref[:, sl] = _nn(ds[h, :, :QBLK], kp) + _nn(ds[h, :, QBLK:2 * QBLK], kc)
            dv_ref[:, sl] = _tn(p[h, :, QBLK:2 * QBLK], do0b) + _tn(p[h, :, 2 * QBLK:], do1b)
            dk_ref[:, sl] = _tn(ds[h, :, QBLK:2 * QBLK], q0) + _tn(ds[h, :, 2 * QBLK:], q1)

    cur = pl.BlockSpec((QBLK, AW), lambda r, n: (n, r))
    prev = pl.BlockSpec((QBLK, AW), lambda r, n: (jnp.maximum(n - 1, 0), r))
    nxt = pl.BlockSpec((QBLK, AW), lambda r, n: (jnp.minimum(n + 1, nb - 1), r))
    return pl.pallas_call(
        body, name=f"attn_bwd_d{d}", grid=(d, nb),
        in_specs=[cur, nxt, prev, cur, prev, cur, cur, nxt, cur, nxt, cur, nxt], out_specs=[cur, cur, cur],
        out_shape=[jax.ShapeDtypeStruct((L, d * AW), F32)] * 3,
        compiler_params=_cparams(("parallel", "parallel")),
    )(q, q, k, k, v, v, do, do, at, at, lse, lse)


def _attn_merge(outs, lses):
    tm = ROPE_TM

    def body(o1, o4, o16, l1, l4, l16, at_ref, ls_ref, at4, ls4, at16, ls16, *flat):
        so4, so16, sl4, sl16, sa, sl = _slab_groups(flat)
        _undilate(o4, so4, 4, tm)
        _undilate(o16, so16, 16, tm)
        _undilate(l4, sl4, 4, tm)
        _undilate(l16, sl16, 16, tm)
        for j in range(SLABS):
            cols = pl.ds(j * 128, 128)
            a, b, c = l1[:, cols], sl4[j][...], sl16[j][...]
            m = jnp.maximum(jnp.maximum(a, b), c)
            e1, e2, e3 = jnp.exp(a - m), jnp.exp(b - m), jnp.exp(c - m)
            s = e1 + e2 + e3
            inv = 1.0 / s
            attn = (e1 * inv) * o1[:, cols] + (e2 * inv) * so4[j][...] + (e3 * inv) * so16[j][...]
            lse = m + jnp.log(s)
            at_ref[:, cols] = attn
            ls_ref[:, cols] = lse
            sa[j][...] = attn
            sl[j][...] = lse
        _dilate(at4, sa, 4, tm)
        _dilate(at16, sa, 16, tm)
        _dilate(ls4, sl, 4, tm)
        _dilate(ls16, sl, 16, tm)

    specs = [_dil_spec(tm, d) for d in DILATIONS]
    tok = specs[0]
    return pl.pallas_call(
        body, name="attn_merge", grid=(T // tm,),
        in_specs=specs + specs, out_specs=[tok, tok, specs[1], specs[1], specs[2], specs[2]],
        out_shape=[jax.ShapeDtypeStruct((T, AW), F32)] * 2 + [jax.ShapeDtypeStruct((T // 4, 4 * AW), F32)] * 2
        + [jax.ShapeDtypeStruct((T // 16, 16 * AW), F32)] * 2,
        scratch_shapes=_slab_scratch(6, tm),
        compiler_params=_cparams(("parallel",)),
    )(*outs, *lses)


CONV_TM = 512
HALO = 8


def _conv_pre(ext, w, b):
    y = b + w[3] * ext
    for kk in range(1, CONV_K):
        y = y + w[3 - kk] * pltpu.roll(ext, kk, 0)
    return y


def _rows_to_block(rows, n, width):
    ri = lax.broadcasted_iota(jnp.int32, (n, width), 0)
    out = jnp.zeros((n, width), F32)
    for j, r in enumerate(rows):
        out = out + jnp.where(ri == j, r, 0.0)
    return out


def _conv_fwd(xbc, w, b):
    nblk = T // CONV_TM

    def body(x_ref, h_ref, w_ref, b_ref, o_ref):
        i = pl.program_id(0)
        halo = jnp.where(i > 0, h_ref[...], 0.0)
        ext = jnp.concatenate([halo, x_ref[...]], axis=0)
        y = _conv_pre(ext, [w_ref[pl.ds(j, 1), :] for j in range(CONV_K)], b_ref[...])[HALO:]
        o_ref[...] = y * _sigmoid(y)

    return pl.pallas_call(
        body, name="conv_fwd", grid=(nblk,),
        in_specs=[pl.BlockSpec((CONV_TM, CONV_CH), lambda i: (i, 0)),
                  pl.BlockSpec((HALO, CONV_CH), lambda i: (jnp.maximum(i * (CONV_TM // HALO) - 1, 0), 0)),
                  pl.BlockSpec((CONV_K, CONV_CH), lambda i: (0, 0)),
                  pl.BlockSpec((1, CONV_CH), lambda i: (0, 0))],
        out_specs=pl.BlockSpec((CONV_TM, CONV_CH), lambda i: (i, 0)),
        out_shape=jax.ShapeDtypeStruct((T, CONV_CH), F32),
        compiler_params=_cparams(("parallel",)),
    )(xbc, xbc, w, b)


def _conv_bwd(xbc, dact, ddt, w, b):
    nblk = T // CONV_TM
    per = CONV_TM // HALO

    def body(x_ref, xb_ref, xa_ref, g_ref, ga_ref, ddt_ref, w_ref, b_ref, dx_ref, dw_ref):
        i = pl.program_id(0)
        wv = [w_ref[pl.ds(j, 1), :] for j in range(CONV_K)]
        before = jnp.where(i > 0, xb_ref[...], 0.0)
        last = i == nblk - 1
        after = jnp.where(last, 0.0, xa_ref[...])
        g_after = jnp.where(last, 0.0, ga_ref[...])
        ext = jnp.concatenate([before, x_ref[...], after], axis=0)
        y = _conv_pre(ext, wv, b_ref[...])[HALO:]
        sg = _sigmoid(y)
        dy = jnp.concatenate([g_ref[...], g_after], axis=0) * (sg * (1.0 + y * (1.0 - sg)))
        n = CONV_TM + HALO
        dx = wv[3] * dy
        for kk in range(1, CONV_K):
            dx = dx + wv[3 - kk] * pltpu.roll(dy, n - kk, 0)
        dx_ref[:, pl.ds(0, CONV_CH)] = dx[:CONV_TM]
        dx_ref[:, pl.ds(CONV_CH, DT_PAD)] = ddt_ref[...]
        dyc = dy[:CONV_TM]
        rows = [jnp.sum(dyc * (pltpu.roll(ext, 3 - j, 0) if j < 3 else ext)[HALO:HALO + CONV_TM], axis=0, keepdims=True)
                for j in range(CONV_K)]
        rows.append(jnp.sum(dyc, axis=0, keepdims=True))
        part = _rows_to_block(rows, 8, CONV_CH)

        @pl.when(i == 0)
        def _():
            dw_ref[...] = jnp.zeros_like(dw_ref)
        dw_ref[...] += part

    blk = pl.BlockSpec((CONV_TM, CONV_CH), lambda i: (i, 0))
    hb = pl.BlockSpec((HALO, CONV_CH), lambda i: (jnp.maximum(i * per - 1, 0), 0))
    ha = pl.BlockSpec((HALO, CONV_CH), lambda i: (jnp.minimum((i + 1) * per, T // HALO - 1), 0))
    return pl.pallas_call(
        body, name="conv_bwd", grid=(nblk,),
        in_specs=[blk, hb, ha, blk, ha, pl.BlockSpec((CONV_TM, DT_PAD), lambda i: (i, 0)),
                  pl.BlockSpec((CONV_K, CONV_CH), lambda i: (0, 0)), pl.BlockSpec((1, CONV_CH), lambda i: (0, 0))],
        out_specs=[pl.BlockSpec((CONV_TM, CONV_CH + DT_PAD), lambda i: (i, 0)), pl.BlockSpec((8, CONV_CH), lambda i: (0, 0))],
        out_shape=[jax.ShapeDtypeStruct((T, CONV_CH + DT_PAD), F32), jax.ShapeDtypeStruct((8, CONV_CH), F32)],
        compiler_params=_cparams(("arbitrary",)),
    )(xbc, xbc, xbc, dact, dact, ddt, w, b)


def _pick(mat, h):
    lane = lax.broadcasted_iota(jnp.int32, mat.shape, 1)
    return jnp.sum(jnp.where(lane == h, mat, 0.0), axis=1, keepdims=True)


def _heads(fn):
    return jnp.stack([fn(h) for h in range(HEADS)])


def _ssd_prep(dt_ref, bias_ref, alog_ref, dsk_ref, b_ref, c_ref, xs_ref, state_ref, cst):
    li = lax.broadcasted_iota(jnp.int32, (CHUNK, CHUNK), 0)
    si = lax.broadcasted_iota(jnp.int32, (CHUNK, CHUNK), 1)
    tri = li >= si
    dtp = dt_ref[...] + bias_ref[...]
    dt = _softplus(dtp)
    A = -jnp.exp(alog_ref[...])
    a = dt * A
    cs = jnp.dot(tri.astype(F32), a, precision=HIGHEST, preferred_element_type=F32)
    cst[...] = cs.T
    Bm = b_ref[...].astype(BF16)
    Cm = c_ref[...].astype(BF16)
    cb = _nt(Cm, Bm)
    dskv = dsk_ref[...]
    cs_col = _heads(lambda h: _pick(cs, h))
    cs_row = _heads(lambda h: cst[pl.ds(h, 1), :])
    dt_col = _heads(lambda h: _pick(dt, h))
    dsk_col = _heads(lambda h: _pick(dskv, h))
    lam = jnp.exp(jnp.where(tri, cs_col - cs_row, NEG))
    x = _heads(lambda h: xs_ref[:, pl.ds(HD * h, HD)])
    xdt = x * dt_col
    prev = _heads(lambda h: state_ref[pl.ds(HD * h, HD), :])
    lane = lax.broadcasted_iota(jnp.int32, (1, 1, CHUNK), 2)
    cl = jnp.sum(jnp.where(lane == CHUNK - 1, cs_row, 0.0), axis=2, keepdims=True)
    f = jnp.exp(cl - cs_col)
    return dict(li=li, si=si, dtp=dtp, dt=dt, A=A, Bm=Bm, Cm=Cm, cb=cb, cs_col=cs_col, dt_col=dt_col, dsk_col=dsk_col,
                lam=lam, x=x, xdt=xdt, prev=prev, cl=cl, f=f)


def _ssd_fwd(act, xbcdt, bias, alog, dsk):
    nc = T // CHUNK

    def body(xs_ref, b_ref, c_ref, dt_ref, bias_ref, alog_ref, dsk_ref, y_ref, st_ref, state, cst):
        @pl.when(pl.program_id(0) == 0)
        def _():
            state[...] = jnp.zeros_like(state)
        st_ref[...] = state[...]
        s = _ssd_prep(dt_ref, bias_ref, alog_ref, dsk_ref, b_ref, c_ref, xs_ref, state, cst)
        Bm, Cm, prev = s["Bm"], s["Cm"], s["prev"]
        g = (s["cb"] * s["lam"]).astype(BF16)
        xdtb = s["xdt"].astype(BF16)
        prevb = prev.astype(BF16)
        y = _heads(lambda h: _nn(g[h], xdtb[h])) + _heads(lambda h: _nt(Cm, prevb[h])) * jnp.exp(s["cs_col"])
        y = y + s["dsk_col"] * s["x"]
        xf = (s["xdt"] * s["f"]).astype(BF16)
        new = prev * jnp.exp(s["cl"]) + _heads(lambda h: _tn(xf[h], Bm))
        for h in range(HEADS):
            y_ref[:, pl.ds(HD * h, HD)] = y[h]
            state[pl.ds(HD * h, HD), :] = new[h]

    vec = pl.BlockSpec((1, DT_PAD), lambda c: (0, 0))
    return pl.pallas_call(
        body, name="ssd_fwd", grid=(nc,),
        in_specs=[pl.BlockSpec((CHUNK, AW), lambda c: (c, 0)), pl.BlockSpec((CHUNK, NS), lambda c: (c, 4)),
                  pl.BlockSpec((CHUNK, NS), lambda c: (c, 5)), pl.BlockSpec((CHUNK, DT_PAD), lambda c: (c, 6)),
                  vec, vec, vec],
        out_specs=[pl.BlockSpec((CHUNK, AW), lambda c: (c, 0)), pl.BlockSpec((None, AW, NS), lambda c: (c, 0, 0))],
        out_shape=[jax.ShapeDtypeStruct((T, AW), F32), jax.ShapeDtypeStruct((nc, AW, NS), F32)],
        scratch_shapes=[pltpu.VMEM((AW, NS), F32), pltpu.VMEM((CHUNK, CHUNK), F32)],
        compiler_params=_cparams(("arbitrary",)),
    )(act, act, act, xbcdt, bias, alog, dsk)


def _ssd_bwd(act, xbcdt, bias, alog, dsk, states, dy):
    nc = T // CHUNK

    def body(xs_ref, b_ref, c_ref, dt_ref, bias_ref, alog_ref, dsk_ref, st_ref, dy_ref,
             dact_ref, ddt_ref, par_ref, dstate, cst):
        step = pl.program_id(0)

        @pl.when(step == 0)
        def _():
            dstate[...] = jnp.zeros_like(dstate)
            par_ref[...] = jnp.zeros_like(par_ref)
        s = _ssd_prep(dt_ref, bias_ref, alog_ref, dsk_ref, b_ref, c_ref, xs_ref, st_ref, cst)
        Bm, Cm, prev, lam, x, xdt, f, cl = s["Bm"], s["Cm"], s["prev"], s["lam"], s["x"], s["xdt"], s["f"], s["cl"]
        lane = lax.broadcasted_iota(jnp.int32, (1, DT_PAD), 1)
        row = lax.broadcasted_iota(jnp.int32, (1, CHUNK, 1), 1)
        g = s["cb"] * lam
        gb, xdtb, prevb = g.astype(BF16), xdt.astype(BF16), prev.astype(BF16)
        dy = _heads(lambda h: dy_ref[:, pl.ds(HD * h, HD)])
        dyb = dy.astype(BF16)
        dnew = _heads(lambda h: dstate[pl.ds(HD * h, HD), :])
        dnewb = dnew.astype(BF16)
        E = jnp.exp(s["cs_col"])
        ecl = jnp.exp(cl)
        dG = _heads(lambda h: _nt(dyb[h], xdtb[h]))
        dxdt = _heads(lambda h: _tn(gb[h], dyb[h]))
        Yo = _heads(lambda h: _nt(Cm, prevb[h]))
        W = _heads(lambda h: _nt(Bm, dnewb[h]))
        dcb = jnp.sum(dG * lam, axis=0)
        Mm = dG * g
        col_sums = jnp.sum(Mm, axis=1, keepdims=True)
        dYo = (dy * E).astype(BF16)
        dxdt = dxdt + W * f
        dF = jnp.sum(W * xdt, axis=2, keepdims=True) * f
        dcl = jnp.sum(dnew * prev, axis=(1, 2), keepdims=True) * ecl + jnp.sum(dF, axis=1, keepdims=True)
        dcs = (jnp.sum(Mm, axis=2, keepdims=True) + jnp.sum(dy * Yo, axis=2, keepdims=True) * E - dF
               + jnp.where(row == CHUNK - 1, dcl, 0.0))
        ddt_x = jnp.sum(dxdt * x, axis=2, keepdims=True)
        dD = jnp.sum(dy * x, axis=(1, 2), keepdims=True)
        dx = s["dsk_col"] * dy + dxdt * s["dt_col"]
        xfb = (xdt * f).astype(BF16)
        dprev = _heads(lambda h: _tn(dYo[h], Cm)) + dnew * ecl
        dcbb = dcb.astype(BF16)
        dC = _nn(dcbb, Bm)
        dB = _tn(dcbb, Cm)
        dcs_mat = -_rows_to_block([col_sums[h] for h in range(HEADS)], CHUNK, CHUNK).T
        ddt_mat = jnp.zeros((CHUNK, DT_PAD), F32)
        dD_row = jnp.zeros((1, DT_PAD), F32)
        for h in range(HEADS):
            sl = pl.ds(HD * h, HD)
            dC = dC + _nn(dYo[h], prevb[h])
            dB = dB + _nn(xfb[h], dnewb[h])
            dcs_mat = dcs_mat + jnp.where(lane == h, dcs[h], 0.0)
            ddt_mat = ddt_mat + jnp.where(lane == h, ddt_x[h], 0.0)
            dD_row = dD_row + jnp.where(lane == h, dD[h], 0.0)
            dact_ref[:, sl] = dx[h]
            dstate[sl, :] = dprev[h]
        dact_ref[:, pl.ds(AW, NS)] = dB
        dact_ref[:, pl.ds(AW + NS, NS)] = dC
        da = jnp.dot((s["li"] <= s["si"]).astype(F32), dcs_mat, precision=HIGHEST, preferred_element_type=F32)
        ddtp = jnp.where(lane < HEADS, (ddt_mat + da * s["A"]) * _sigmoid(s["dtp"]), 0.0)
        ddt_ref[...] = ddtp
        dalog = jnp.where(lane < HEADS, jnp.sum(da * s["dt"], axis=0, keepdims=True) * s["A"], 0.0)
        par_ref[...] += _rows_to_block([jnp.sum(ddtp, axis=0, keepdims=True), dalog, dD_row], 8, DT_PAD)

    vec = pl.BlockSpec((1, DT_PAD), lambda c: (0, 0))
    rev = lambda c: nc - 1 - c
    return pl.pallas_call(
        body, name="ssd_bwd", grid=(nc,),
        in_specs=[pl.BlockSpec((CHUNK, AW), lambda c: (rev(c), 0)), pl.BlockSpec((CHUNK, NS), lambda c: (rev(c), 4)),
                  pl.BlockSpec((CHUNK, NS), lambda c: (rev(c), 5)), pl.BlockSpec((CHUNK, DT_PAD), lambda c: (rev(c), 6)),
                  vec, vec, vec,
                  pl.BlockSpec((None, AW, NS), lambda c: (rev(c), 0, 0)), pl.BlockSpec((CHUNK, AW), lambda c: (rev(c), 0))],
        out_specs=[pl.BlockSpec((CHUNK, CONV_CH), lambda c: (rev(c), 0)), pl.BlockSpec((CHUNK, DT_PAD), lambda c: (rev(c), 0)),
                   pl.BlockSpec((8, DT_PAD), lambda c: (0, 0))],
        out_shape=[jax.ShapeDtypeStruct((T, CONV_CH), F32), jax.ShapeDtypeStruct((T, DT_PAD), F32),
                   jax.ShapeDtypeStruct((8, DT_PAD), F32)],
        scratch_shapes=[pltpu.VMEM((AW, NS), F32), pltpu.VMEM((CHUNK, CHUNK), F32)],
        compiler_params=_cparams(("arbitrary",)),
    )(act, act, act, xbcdt, bias, alog, dsk, states, dy)


def _place():
    return lax.axis_index("x"), lax.axis_index("y"), lax.axis_index("c")


def _slot(px, py, pc):
    return 4 * px + 2 * py + pc


def _all_gather(arrs, name):
    na = len(arrs)

    def body(*refs):
        ins, outs = refs[:na], refs[na:2 * na]
        send_sems, recv_sems, local_sems = refs[2 * na:]
        x, y, c = _place()
        me, sib = (x, y, c), (x, y, 1 - c)
        chips = [(1 - x, y), (x, 1 - y), (1 - x, 1 - y)]

        def copy(a, kk, block, to, src=None):
            dst = outs[a].at[_slot(*block)]
            return pltpu.make_async_remote_copy(
                src_ref=dst if src is None else src, dst_ref=dst,
                send_sem=send_sems.at[a, kk], recv_sem=recv_sems.at[a, kk], device_id=to, device_id_type=MESH)

        mine = [pltpu.make_async_copy(ins[a], outs[a].at[_slot(*me)], local_sems.at[a]) for a in range(na)]
        for cp in mine:
            cp.start()
        first = []
        for a in range(na):
            first.append(copy(a, 0, me, sib, src=ins[a]))
            first += [copy(a, 1 + j, me, (*chip, c), src=ins[a]) for j, chip in enumerate(chips)]
        for cp in first:
            cp.start()
        passed = []
        for j, chip in enumerate(chips):
            for a in range(na):
                copy(a, 1 + j, (*chip, c), me).wait_recv()
                fw = copy(a, 4 + j, (*chip, c), sib)
                fw.start()
                passed.append(fw)
        for a in range(na):
            copy(a, 0, sib, me).wait_recv()
            for j, chip in enumerate(chips):
                copy(a, 4 + j, (*chip, 1 - c), me).wait_recv()
        for cp in first + passed:
            cp.wait_send()
        for cp in mine:
            cp.wait()

    any_spec = pl.BlockSpec(memory_space=pl.ANY)
    return pl.pallas_call(
        body, name=name,
        in_specs=[any_spec] * na, out_specs=[any_spec] * na,
        out_shape=[jax.ShapeDtypeStruct((N_DEV,) + a.shape, a.dtype) for a in arrs],
        scratch_shapes=[pltpu.SemaphoreType.DMA((na, 7)), pltpu.SemaphoreType.DMA((na, 7)),
                        pltpu.SemaphoreType.DMA((na,))],
    )(*arrs)


def _reduce_scatter(part, name):
    _, r, C = part.shape

    def body(part_ref, out_ref, own, got_sib, chip_sum, got_ici, lsem, s1, r1, s2, r2):
        x, y, c = _place()
        chips = [(x, y), (1 - x, y), (x, 1 - y), (1 - x, 1 - y)]
        loc = [pltpu.make_async_copy(part_ref.at[_slot(*chips[kk], c)], own.at[kk], lsem.at[kk]) for kk in range(4)]
        d2d = [pltpu.make_async_remote_copy(
            src_ref=part_ref.at[_slot(*chips[kk], 1 - c)], dst_ref=got_sib.at[kk],
            send_sem=s1.at[kk], recv_sem=r1.at[kk], device_id=(x, y, 1 - c), device_id_type=MESH) for kk in range(4)]
        for cp in loc + d2d:
            cp.start()
        ici = [pltpu.make_async_remote_copy(
            src_ref=chip_sum.at[kk - 1], dst_ref=got_ici.at[kk - 1],
            send_sem=s2.at[kk - 1], recv_sem=r2.at[kk - 1], device_id=(*chips[kk], c), device_id_type=MESH)
            for kk in range(1, 4)]
        for kk in (1, 2, 3):
            loc[kk].wait()
            d2d[kk].wait_recv()
            chip_sum[kk - 1] = (own[kk].astype(F32) + got_sib[kk].astype(F32)).astype(BF16)
            ici[kk - 1].start()
        loc[0].wait()
        d2d[0].wait_recv()
        acc = own[0].astype(F32) + got_sib[0].astype(F32)
        for cp in ici:
            cp.wait_recv()
        out_ref[...] = ((acc + got_ici[0].astype(F32)) + got_ici[1].astype(F32)) + got_ici[2].astype(F32)
        for cp in d2d + ici:
            cp.wait_send()

    return pl.pallas_call(
        body, name=name,
        in_specs=[pl.BlockSpec(memory_space=pl.ANY)],
        out_specs=pl.BlockSpec(memory_space=pltpu.VMEM),
        out_shape=jax.ShapeDtypeStruct((r, C), F32),
        scratch_shapes=[pltpu.VMEM((4, r, C), BF16), pltpu.VMEM((4, r, C), BF16), pltpu.VMEM((3, r, C), BF16),
                        pltpu.VMEM((3, r, C), BF16),
                        pltpu.SemaphoreType.DMA((4,)), pltpu.SemaphoreType.DMA((4,)), pltpu.SemaphoreType.DMA((4,)),
                        pltpu.SemaphoreType.DMA((3,)), pltpu.SemaphoreType.DMA((3,))],
        compiler_params=pltpu.CompilerParams(vmem_limit_bytes=VMEM_LIMIT),
    )(part)


def _all_reduce_small(v, name):
    R, C = v.shape

    def body(v_ref, out_ref, got, send_sems, recv_sems):
        x, y, c = _place()
        mine = _slot(x, y, c)
        copies = []
        for kk in range(1, N_DEV):
            fx, fy, fc = kk >> 2 & 1, kk >> 1 & 1, kk & 1
            peer = (1 - x if fx else x, 1 - y if fy else y, 1 - c if fc else c)
            copies.append(pltpu.make_async_remote_copy(
                src_ref=v_ref, dst_ref=got.at[mine], send_sem=send_sems.at[kk - 1], recv_sem=recv_sems.at[kk - 1],
                device_id=peer, device_id_type=MESH))
        for cp in copies:
            cp.start()
        got[mine] = v_ref[...]
        for cp in copies:
            cp.wait_recv()
        acc = got[0]
        for s in range(1, N_DEV):
            acc = acc + got[s]
        out_ref[...] = acc
        for cp in copies:
            cp.wait_send()

    return pl.pallas_call(
        body, name=name,
        in_specs=[pl.BlockSpec(memory_space=pltpu.VMEM)], out_specs=pl.BlockSpec(memory_space=pltpu.VMEM),
        out_shape=jax.ShapeDtypeStruct((R, C), F32),
        scratch_shapes=[pltpu.VMEM((N_DEV, R, C), F32), pltpu.SemaphoreType.DMA((N_DEV - 1,)),
                        pltpu.SemaphoreType.DMA((N_DEV - 1,))],
    )(v)


_HBM = pl.BlockSpec(memory_space=pltpu.HBM)
_SEM = pl.BlockSpec(memory_space=pltpu.SEMAPHORE)
_EFFECT = pltpu.SideEffectType.DATAFLOW_SIDE_EFFECTING


def _peers(x, y, c):
    out = []
    for kk in range(1, N_DEV):
        fx, fy, fc = kk >> 2 & 1, kk >> 1 & 1, kk & 1
        out.append((1 - x if fx else x, 1 - y if fy else y, 1 - c if fc else c))
    return out


def _send_start(src, per_peer, name, dep):
    blk = src.shape[1:] if per_peer else src.shape

    def body(src_ref, land_ref, dep_ref, send_sems, recv_sems, src_thru, land_thru, token):
        x, y, c = _place()
        mine = _slot(x, y, c)
        for kk, peer in enumerate(_peers(x, y, c)):
            pltpu.make_async_remote_copy(
                src_ref=src_ref.at[_slot(*peer)] if per_peer else src_ref, dst_ref=land_ref.at[mine],
                send_sem=send_sems.at[kk], recv_sem=recv_sems.at[kk], device_id=peer, device_id_type=MESH).start()
        token[...] = jnp.zeros_like(token)

    land = lax.empty((N_DEV,) + tuple(blk), src.dtype)
    *handles, token = pl.pallas_call(
        body, name=name,
        out_shape=(pltpu.SemaphoreType.DMA((N_DEV - 1,)), pltpu.SemaphoreType.DMA((N_DEV - 1,)),
                   pltpu.HBM(src.shape, src.dtype), pltpu.HBM(land.shape, land.dtype),
                   jax.ShapeDtypeStruct((8, 128), F32)),
        in_specs=(_HBM, _HBM, _ANY), out_specs=(_SEM, _SEM, _HBM, _HBM, pl.BlockSpec(memory_space=pltpu.VMEM)),
        input_output_aliases={0: 2, 1: 3},
        compiler_params=pltpu.CompilerParams(has_side_effects=_EFFECT),
    )(pltpu.with_memory_space_constraint(src, pltpu.HBM), pltpu.with_memory_space_constraint(land, pltpu.HBM), dep)
    return handles, token


def _send_wait(handles, after, name):
    send_sems, recv_sems, src_thru, land_thru = handles

    def body(src_ref, land_ref, send_sems, recv_sems, after_ref, src_dead, got_ref):
        me = _place()
        for kk in range(N_DEV - 1):
            cp = pltpu.make_async_remote_copy(
                src_ref=land_ref.at[0], dst_ref=land_ref.at[0], send_sem=send_sems.at[kk], recv_sem=recv_sems.at[kk],
                device_id=me, device_id_type=MESH)
            cp.wait_send()
            cp.wait_recv()

    return pl.pallas_call(
        body, name=name,
        out_shape=(pltpu.HBM(src_thru.shape, src_thru.dtype), pltpu.HBM(land_thru.shape, land_thru.dtype)),
        in_specs=(_HBM, _HBM, _SEM, _SEM, pl.BlockSpec(memory_space=pl.ANY)), out_specs=(_HBM, _HBM),
        input_output_aliases={0: 0, 1: 1},
        compiler_params=pltpu.CompilerParams(has_side_effects=_EFFECT),
    )(src_thru, land_thru, send_sems, recv_sems, after)


def _sum_slots(land, name):
    _, R, C = land.shape
    tm = R if R <= 512 else 512

    def body(x_ref, o_ref):
        acc = x_ref[0].astype(F32)
        for j in range(1, N_DEV):
            acc = acc + x_ref[j].astype(F32)
        o_ref[...] = acc

    return pl.pallas_call(
        body, name=name, grid=(R // tm,),
        in_specs=[pl.BlockSpec((N_DEV, tm, C), lambda i: (0, i, 0))], out_specs=pl.BlockSpec((tm, C), lambda i: (i, 0)),
        out_shape=jax.ShapeDtypeStruct((R, C), F32), compiler_params=_cparams(("parallel",)),
    )(land)


def _adamw(w, g, m, v, name):
    R, C = w.shape
    tm = R if R <= 512 else 256

    def fn(w, g, m, v):
        m2 = ADAM_B1 * m + (1.0 - ADAM_B1) * g
        v2 = ADAM_B2 * v + (1.0 - ADAM_B2) * (g * g)
        m_hat = m2 / (1.0 - ADAM_B1 ** ADAM_STEP)
        v_hat = v2 / (1.0 - ADAM_B2 ** ADAM_STEP)
        delta = -ADAM_LR * (m_hat / (jnp.sqrt(v_hat) + ADAM_EPS) + ADAM_WD * w)
        return (delta, m2, v2), ()
    return _rowwise(fn, [w, g, m, v], [], [(C, F32)] * 3, [], tm=tm, name=name)


SMALL = ["norm_mix_pre", "norm_mix_post", "norm_mlp_pre", "norm_mlp_post", "norm_ple_post",
         "conv_b", "ssd_norm_g", "dt_bias", "a_log", "d_skip"]


def _pad_row(v, width=D):
    return jnp.pad(v, ((0, 0), (0, width - v.shape[1])))


def kernel(x, p, positions, norm_mix_pre, norm_mix_post, w_in, conv_w, conv_b, dt_bias, a_log, d_skip, ssd_norm_g, w_out, norm_mlp_pre, norm_mlp_post, w_up, w_down, w_ple_gate, w_ple_proj, norm_ple_post, loss_target, m_norm_mix_pre, m_norm_mix_post, m_w_in, m_conv_w, m_conv_b, m_dt_bias, m_a_log, m_d_skip, m_ssd_norm_g, m_w_out, m_norm_mlp_pre, m_norm_mlp_post, m_w_up, m_w_down, m_w_ple_gate, m_w_ple_proj, m_norm_ple_post, v_norm_mix_pre, v_norm_mix_post, v_w_in, v_conv_w, v_conv_b, v_dt_bias, v_a_log, v_d_skip, v_ssd_norm_g, v_w_out, v_norm_mlp_pre, v_norm_mlp_post, v_w_up, v_w_down, v_w_ple_gate, v_w_ple_proj, v_norm_ple_post):
    args = dict(locals())
    x2, p2, tgt = x[0], p[0, 0], loss_target[0]
    g1, g2, g3, g4, g5 = norm_mix_pre, norm_mix_post, norm_mlp_pre, norm_mlp_post, norm_ple_post

    me = _slot(*_place())
    pack_in = jnp.pad(w_in[0].T, ((0, W_IN_SHARD_PAD - W_IN_SHARD), (0, 0))).astype(BF16)
    pack_rest = jnp.concatenate([
        w_out[0],
        w_up[0].T,
        w_down[0],
        w_ple_gate[0],
        w_ple_proj[0].T.reshape(32, D),
    ], axis=0).astype(BF16)
    conv_pack = jnp.pad(conv_w[0], ((0, 4), (0, 32)))
    gin, gconv = _all_gather([pack_in, conv_pack], "gather_w_in")
    rest_handles, tok_rest = _send_start(pack_rest, False, "gather_rest_start", gconv)
    w_inT = gin[:, :W_IN_SHARD].reshape(IN_W, D)
    w_qkvzT = w_inT[:4 * AW]
    w_xbcdtT = jnp.pad(w_inT[4 * AW:], ((0, DT_PAD - HEADS), (0, 0)))
    conv_full = gconv[:, :CONV_K, :96].transpose(1, 0, 2).reshape(CONV_K, CONV_CH)

    inv_freq = ROPE_THETA ** (-jnp.arange(HD // 2, dtype=F32) * 2.0 / HD)
    ang = positions[0].astype(F32)[:, None] * inv_freq
    cos, sin = jnp.cos(ang), jnp.sin(ang)
    cos128 = jnp.concatenate([cos, cos, cos, cos], axis=1)
    sin128 = jnp.concatenate([-sin, sin, -sin, sin], axis=1)

    bias_w, alog_w, dsk_w = _pad_row(dt_bias, DT_PAD), _pad_row(a_log, DT_PAD), _pad_row(d_skip, DT_PAD)
    rms_pre = lambda a, r, g: a * r * g

    (r1,) = _rowwise(lambda a: ((_rstd(a),), ()), [x2], [], [(1, F32)], [], tm=512, name="rstd_x")
    qkvz = _mm(x2, w_qkvzT, tb=True, tm=512, tn=1024, tk=1024, a_pre=rms_pre, a_rows=[r1], a_cols=[g1], name="proj_qkvz",
               deps=[tok_rest])
    xbcdt = _mm(x2, w_xbcdtT, tb=True, tm=512, tn=896, tk=1024, a_pre=rms_pre, a_rows=[r1], a_cols=[g1], name="proj_xbcdt")

    qkv = _rope_fwd(qkvz, cos128, sin128)
    qkv = [qkv[3 * i:3 * i + 3] for i in range(len(DILATIONS))]
    outs, lses = [], []
    for d, (qd, kd, vd) in zip(DILATIONS, qkv):
        o, l = _attn_fwd(qd, kd, vd, d)
        outs.append(o)
        lses.append(l)
    attn, lse, attn4, lse4, attn16, lse16 = _attn_merge(outs, lses)

    act = _conv_fwd(xbcdt, conv_full, conv_b)
    y_ssd, states = _ssd_fwd(act, xbcdt, bias_w, alog_w, dsk_w)

    def gated_fwd(y, z, a, gs):
        gi = y * (z * _sigmoid(z))
        return (jnp.concatenate([a, gi * _rstd(gi) * gs], axis=1),), ()
    (cat,) = _rowwise(gated_fwd, [y_ssd, (qkvz, AW, 3), attn], [ssd_norm_g], [(D, F32)], [], tm=512, name="gated_norm")

    pack_back, grest = _send_wait(rest_handles, cat, "gather_rest_wait")
    grest = lax.dynamic_update_slice(grest, pack_back[None], (me, 0, 0))
    w_o = grest[:, 0:128].reshape(D, D)
    w_upT = grest[:, 128:640].reshape(DFF, D)
    w_dn = grest[:, 640:1152].reshape(DFF, D)
    w_gate = grest[:, 1152:1280].reshape(D, D)
    w_projT = grest[:, 1280:1312].reshape(D, PLE)

    mix = _mm(cat, w_o, tm=512, tn=1024, tk=1024, name="mix_out")

    def post1(xx, mm, ga, gb):
        h = xx + mm * _rstd(mm) * ga
        return (h, _rstd(h)), ()
    h1, r3 = _rowwise(post1, [x2, mix], [g2, g3], [(D, F32), (1, F32)], [], tm=512, name="post_mix")

    a_up, ff, u2 = _mlp_fwd(h1, r3, g3, w_upT, w_dn)
    relu2 = lambda a: jnp.square(jnp.maximum(a.astype(F32), 0.0))
    (h2,) = _rowwise(lambda hh, f, g: ((hh + f * _rstd(f) * g,), ()), [h1, ff], [g4], [(D, F32)], [], tm=512, name="post_mlp")

    gp = _mm(h2, w_gate, tm=512, tn=1024, tk=1024, name="ple_gate")
    pp = _mm(p2, w_projT, tb=True, tm=512, tn=1024, tk=256, name="ple_proj")

    def final(hh, gpre, ppv, tg, g):
        sg = _sigmoid(gpre)
        ple = ppv * sg
        r = _rstd(ple)
        n = ple * r
        h3 = hh + n * g
        e = h3 - tg
        dh3 = e * (1.0 / D)
        dple = _rms_bwd(n, r, g, dh3)
        return (dh3, dple * sg, dple * ppv * sg * (1.0 - sg)), (_colsum(dh3 * n), _colsum(0.5 * e * e * (1.0 / D)))
    dh3, dpp, dgp, dg5, loss_vec = _rowwise(final, [h2, gp, pp, tgt], [g5], [(D, F32)] * 3, [(1, D), (1, D)],
                                            tm=256, name="loss_ple_bwd")

    gw_projT = _mm(dpp, p2, ta=True, tm=512, tn=256, tk=1024, out_dtypes=(BF16,), name="gw_ple_proj")
    gw_gate = _mm(h2, dgp, ta=True, tm=512, tn=1024, tk=1024, out_dtypes=(BF16,), name="gw_ple_gate")
    rs_proj, tok_proj = _send_start(gw_projT.reshape(N_DEV, 32, D), True, "rs_start_w_proj", g1)
    rs_gate, tok_gate = _send_start(gw_gate.reshape(N_DEV, 128, D), True, "rs_start_w_gate", g1)
    dh2_g = _mm(dgp, w_gate, tb=True, tm=512, tn=1024, tk=1024, name="dx_ple_gate", deps=[tok_proj, tok_gate])

    def bwd_mlp_post(d3, dg_, f, g):
        dh2 = d3 + dg_
        r = _rstd(f)
        n = f * r
        return (dh2, _rms_bwd(n, r, g, dh2)), (_colsum(dh2 * n),)
    dh2, dff, dg4 = _rowwise(bwd_mlp_post, [dh3, dh2_g, ff], [g4], [(D, F32)] * 2, [(1, D)], tm=256, name="bwd_post_mlp")

    gw_dn = _mm(a_up, dff, ta=True, tm=1024, tn=1024, tk=512, a_pre=relu2, out_dtypes=(BF16,), name="gw_mlp_down")
    rs_dn, tok_dn = _send_start(gw_dn.reshape(N_DEV, 512, D), True, "rs_start_w_down", g1)
    da_up, du2 = _mlp_dx(dff, a_up, w_upT, w_dn, tok_dn)
    gw_upT = _mm(da_up, u2, ta=True, tm=1024, tn=1024, tk=512, out_dtypes=(BF16,), name="gw_mlp_up")
    rs_up, tok_up = _send_start(gw_upT.reshape(N_DEV, 512, D), True, "rs_start_w_up", g1)

    def bwd_mix_post(d2, du, hh, rr, mm, ga, gb):
        n3 = hh * rr
        dh1 = d2 + _rms_bwd(n3, rr, gb, du)
        r = _rstd(mm)
        n2 = mm * r
        return (dh1, _rms_bwd(n2, r, ga, dh1)), (_colsum(du * n3), _colsum(dh1 * n2))
    dh1, dmix, dg3, dg2 = _rowwise(bwd_mix_post, [dh2, du2, h1, r3, mix], [g2, g3], [(D, F32)] * 2, [(1, D), (1, D)],
                                   tm=256, name="bwd_post_mix", deps=[tok_up])

    gw_o = _mm(cat, dmix, ta=True, tm=512, tn=1024, tk=1024, out_dtypes=(BF16,), name="gw_out")
    rs_o, tok_o = _send_start(gw_o.reshape(N_DEV, 128, D), True, "rs_start_w_out", g1)
    dcat = _mm(dmix, w_o, tb=True, tm=512, tn=1024, tk=1024, name="dx_out", deps=[tok_o])

    def gated_bwd(y, z, dyn, gs):
        sg = _sigmoid(z)
        sz = z * sg
        gi = y * sz
        r = _rstd(gi)
        n = gi * r
        dgi = _rms_bwd(n, r, gs, dyn)
        return (dgi * sz, dgi * y * (sg * (1.0 + z * (1.0 - sg)))), (_colsum(dyn * n),)
    dy_ssd, dz, dgs = _rowwise(gated_bwd, [y_ssd, (qkvz, AW, 3), (dcat, AW, 1)], [ssd_norm_g], [(AW, F32)] * 2, [(1, AW)],
                               tm=512, name="bwd_gated_norm")

    dact, ddtw, ssd_par = _ssd_bwd(act, xbcdt, bias_w, alog_w, dsk_w, states, dy_ssd)
    dxbcdt, conv_par = _conv_bwd(xbcdt, dact, ddtw, conv_full, conv_b)

    dattn4, dattn16 = _dilate_cols(dcat, 0)
    qkv_grads = [_attn_bwd(*qkv[0], dcat, attn, lse, 1),
                 _attn_bwd(*qkv[1], dattn4, attn4, lse4, 4),
                 _attn_bwd(*qkv[2], dattn16, attn16, lse16, 16)]
    dqkvz = _rope_bwd(qkv_grads, dz, cos128, sin128)

    du1a = _mm(dqkvz, w_qkvzT, tm=512, tn=1024, tk=1024, name="dx_qkvz")
    du1b = _mm(dxbcdt, w_xbcdtT, tm=512, tn=1024, tk=896, name="dx_xbcdt")
    pre1 = dict(b_pre=rms_pre, b_rows=[r1], b_cols=[g1], out_dtypes=(BF16,))
    gw_qkvzT = _mm(dqkvz, x2, ta=True, tm=1024, tn=1024, tk=512, name="gw_qkvz", **pre1)
    gw_xbcdtT = _mm(dxbcdt, x2, ta=True, tm=896, tn=1024, tk=512, name="gw_xbcdt", **pre1)

    def bwd_in(d1, ua, ub, xx, rr, g):
        n = xx * rr
        du = ua + ub
        return (d1 + _rms_bwd(n, rr, g, du),), (_colsum(du * n),)
    grad_x, dg1 = _rowwise(bwd_in, [dh1, du1a, du1b, x2, r1], [g1], [(D, F32)], [(1, D)], tm=256, name="bwd_pre_mix")

    gw_inT = jnp.concatenate([gw_qkvzT, gw_xbcdtT], axis=0)[:IN_W]
    gw_inT = jnp.pad(gw_inT.reshape(N_DEV, W_IN_SHARD, D), ((0, 0), (0, W_IN_SHARD_PAD - W_IN_SHARD), (0, 0)))
    g_inT = _reduce_scatter(gw_inT, "rs_w_in")

    def scatter_finish(handles, nm):
        part, land = _send_wait(handles, g_inT, "rs_wait_" + nm)
        own = lax.dynamic_slice(part, (me, 0, 0), (1,) + part.shape[1:])
        return _sum_slots(lax.dynamic_update_slice(land, own, (me, 0, 0)), "rs_sum_" + nm)
    g_out = scatter_finish(rs_o, "w_out")
    g_upT = scatter_finish(rs_up, "w_up")
    g_dn = scatter_finish(rs_dn, "w_down")
    g_gate = scatter_finish(rs_gate, "w_gate")
    g_projT = scatter_finish(rs_proj, "w_proj")

    small = jnp.concatenate([
        dg1, dg2, dg3, dg4, dg5,
        _pad_row(conv_par[4:5]), _pad_row(dgs), _pad_row(ssd_par[0:1]), _pad_row(ssd_par[1:2]), _pad_row(ssd_par[2:3]),
        _pad_row(conv_par[0:4]), loss_vec, jnp.zeros((1, D), F32),
    ], axis=0)
    small = _all_reduce_small(small, "reduce_small")
    loss = jnp.sum(small[14])
    me = lax.axis_index("x") * 4 + lax.axis_index("y") * 2 + lax.axis_index("c")
    g_conv_w = lax.dynamic_slice(small[10:14, :CONV_CH], (0, me * 96), (CONV_K, 96))

    grads = {
        "w_in": g_inT[:W_IN_SHARD].T[None], "w_out": g_out[None], "w_up": g_upT.T[None], "w_down": g_dn[None],
        "w_ple_gate": g_gate[None], "w_ple_proj": g_projT.reshape(128, PLE).T[None], "conv_w": g_conv_w[None],
        "norm_mix_pre": small[0:1], "norm_mix_post": small[1:2], "norm_mlp_pre": small[2:3], "norm_mlp_post": small[3:4],
        "norm_ple_post": small[4:5], "conv_b": small[5:6, :CONV_CH], "ssd_norm_g": small[6:7, :AW],
        "dt_bias": small[7:8, :HEADS], "a_log": small[8:9, :HEADS], "d_skip": small[9:10, :HEADS],
    }
    delta, new_m, new_v = {}, {}, {}
    for nme in ["w_in", "w_out", "w_up", "w_down", "w_ple_gate", "w_ple_proj"]:
        dl, mm_, vv_ = _adamw(args[nme][0], grads[nme][0], args["m_" + nme][0], args["v_" + nme][0], "adamw_" + nme)
        delta[nme], new_m[nme], new_v[nme] = dl[None], mm_[None], vv_[None]

    def pack_small(prefix):
        rows = [_pad_row(args[prefix + nme]) for nme in SMALL]
        rows.append(_pad_row(args[prefix + "conv_w"][0]))
        rows.append(jnp.zeros((2, D), F32))
        return jnp.concatenate(rows, axis=0)
    g_small = jnp.concatenate([small[0:10], _pad_row(g_conv_w), jnp.zeros((2, D), F32)], axis=0)
    dl, mm_, vv_ = _adamw(pack_small(""), g_small, pack_small("m_"), pack_small("v_"), "adamw_small")
    for i, nme in enumerate(SMALL):
        wdt = args[nme].shape[1]
        delta[nme], new_m[nme], new_v[nme] = dl[i:i + 1, :wdt], mm_[i:i + 1, :wdt], vv_[i:i + 1, :wdt]
    delta["conv_w"], new_m["conv_w"], new_v["conv_w"] = dl[None, 10:14, :96], mm_[None, 10:14, :96], vv_[None, 10:14, :96]

    order = ["norm_mix_pre", "norm_mix_post", "w_in", "conv_w", "conv_b", "dt_bias", "a_log", "d_skip", "ssd_norm_g",
             "w_out", "norm_mlp_pre", "norm_mlp_post", "w_up", "w_down", "w_ple_gate", "w_ple_proj", "norm_ple_post"]
    return (loss, grad_x[None], *[grads[n] for n in order], *[delta[n] for n in order],
            *[new_m[n] for n in order], *[new_v[n] for n in order])
```

```python
import functools
import math

import jax
import jax.numpy as jnp
from jax import lax
from jax.experimental import pallas as pl
from jax.experimental.pallas import tpu as pltpu

F32 = jnp.float32
BF16 = jnp.bfloat16
MESH = pl.DeviceIdType.MESH
HIGHEST = lax.Precision.HIGHEST

N_DEV = 8
T = 4096
D = 1024
HEADS = 8
HD = 64
AW = 512
NS = 128
CONV_K = 4
CONV_CH = 768
CHUNK = 128
DFF = 4096
PLE = 256
EPS = 1e-6
ROPE_THETA = 10000.0
DILATIONS = (1, 4, 16)
QBLK = 128
NEG = -1e30
IN_W = 2824
W_IN_SHARD = 353
W_IN_SHARD_PAD = 384
DT_PAD = 128

ADAM_LR, ADAM_B1, ADAM_B2, ADAM_EPS, ADAM_WD, ADAM_STEP = 0.001, 0.9, 0.999, 1e-08, 0.01, 10

VMEM_LIMIT = 56 * 1024 * 1024


_ANY = pl.BlockSpec(memory_space=pl.ANY)


def _cparams(sem=None):
    return pltpu.CompilerParams(dimension_semantics=sem, vmem_limit_bytes=VMEM_LIMIT)


def _dot(a, b, ca, cb, precision=None):
    return lax.dot_general(a, b, (((ca,), (cb,)), ((), ())), preferred_element_type=F32, precision=precision)


def _nn(a, b):
    return _dot(a, b, 1, 0)


def _nt(a, b):
    return _dot(a, b, 1, 1)


def _tn(a, b):
    return _dot(a, b, 0, 0)


def _sigmoid(x):
    return 1.0 / (1.0 + jnp.exp(-x))


def _softplus(x):
    return jnp.maximum(x, 0.0) + jnp.log(1.0 + jnp.exp(-jnp.abs(x)))


def _mm(a, b, *, ta=False, tb=False, tm, tn, tk, name,
        a_pre=None, a_rows=(), a_cols=(), b_pre=None, b_rows=(), b_cols=(),
        epi=None, epi_tiles=(), out_dtypes=(F32,), deps=()):
    if ta:
        K, M = a.shape
    else:
        M, K = a.shape
    if tb:
        N, K2 = b.shape
    else:
        K2, N = b.shape
    assert K == K2 and M % tm == 0 and N % tn == 0 and K % tk == 0, (name, a.shape, b.shape)
    nk = K // tk
    if ta:
        a_spec = pl.BlockSpec((tk, tm), lambda i, j, k: (k, i))
        a_row_specs = [pl.BlockSpec((tk, 1), lambda i, j, k: (k, 0)) for _ in a_rows]
        a_col_specs = [pl.BlockSpec((1, tm), lambda i, j, k: (0, i)) for _ in a_cols]
    else:
        a_spec = pl.BlockSpec((tm, tk), lambda i, j, k: (i, k))
        a_row_specs = [pl.BlockSpec((tm, 1), lambda i, j, k: (i, 0)) for _ in a_rows]
        a_col_specs = [pl.BlockSpec((1, tk), lambda i, j, k: (0, k)) for _ in a_cols]
    if tb:
        b_spec = pl.BlockSpec((tn, tk), lambda i, j, k: (j, k))
        b_row_specs = [pl.BlockSpec((tn, 1), lambda i, j, k: (j, 0)) for _ in b_rows]
        b_col_specs = [pl.BlockSpec((1, tk), lambda i, j, k: (0, k)) for _ in b_cols]
    else:
        b_spec = pl.BlockSpec((tk, tn), lambda i, j, k: (k, j))
        b_row_specs = [pl.BlockSpec((tk, 1), lambda i, j, k: (k, 0)) for _ in b_rows]
        b_col_specs = [pl.BlockSpec((1, tn), lambda i, j, k: (0, j)) for _ in b_cols]
    o_spec = pl.BlockSpec((tm, tn), lambda i, j, k: (i, j))
    na, nb, ne, no = len(a_rows) + len(a_cols), len(b_rows) + len(b_cols), len(epi_tiles), len(out_dtypes)

    def body(*refs):
        a_ref, b_ref = refs[0], refs[1]
        a_ex = refs[2:2 + na]
        b_ex = refs[2 + na:2 + na + nb]
        e_ex = refs[2 + na + nb:2 + na + nb + ne]
        first_out = 2 + na + nb + ne + len(deps)
        outs = refs[first_out:first_out + no]
        acc = refs[-1]
        k = pl.program_id(2)

        @pl.when(k == 0)
        def _():
            acc[...] = jnp.zeros_like(acc)

        at = a_ref[...]
        if a_pre is not None:
            at = a_pre(at, *[r[...] for r in a_ex])
        bt = b_ref[...]
        if b_pre is not None:
            bt = b_pre(bt, *[r[...] for r in b_ex])
        acc[...] += _dot(at.astype(BF16), bt.astype(BF16), 0 if ta else 1, 1 if tb else 0)

        @pl.when(k == nk - 1)
        def _():
            res = acc[...]
            vals = epi(res, *[r[...] for r in e_ex]) if epi is not None else (res,)
            for o_ref, val in zip(outs, vals):
                o_ref[...] = val.astype(o_ref.dtype)

    outs = pl.pallas_call(
        body, name=name,
        grid=(M // tm, N // tn, nk),
        in_specs=([a_spec, b_spec] + a_row_specs + a_col_specs + b_row_specs + b_col_specs + [o_spec] * ne
                  + [_ANY] * len(deps)),
        out_specs=[o_spec] * no,
        out_shape=[jax.ShapeDtypeStruct((M, N), dt) for dt in out_dtypes],
        scratch_shapes=[pltpu.VMEM((tm, tn), F32)],
        compiler_params=_cparams(("parallel", "parallel", "arbitrary")),
    )(a, b, *a_rows, *a_cols, *b_rows, *b_cols, *epi_tiles, *deps)
    return outs[0] if no == 1 else outs


MLP_TM = 1024
MLP_TC = 512


def _mlp_fwd(h, r, g, w_upT, w_dn, g_post):
    nc = DFF // MLP_TC

    def body(h_ref, r_ref, g_ref, wu_ref, wd_ref, gp_ref, a_ref, ff_ref, u_ref, ho_ref, hob_ref, acc, u_scr):
        c = pl.program_id(1)

        @pl.when(c == 0)
        def _():
            u = (h_ref[...] * r_ref[...] * g_ref[...]).astype(BF16)
            u_scr[...] = u
            u_ref[...] = u
            acc[...] = jnp.zeros_like(acc)
        a = _nt(u_scr[...], wu_ref[...])
        a_ref[...] = a.astype(BF16)
        acc[...] += _nn(jnp.square(jnp.maximum(a, 0.0)).astype(BF16), wd_ref[...])

        @pl.when(c == nc - 1)
        def _():
            f = acc[...]
            ff_ref[...] = f
            ho = h_ref[...] + f * _rstd(f) * gp_ref[...]
            ho_ref[...] = ho
            hob_ref[...] = ho.astype(BF16)

    row = pl.BlockSpec((MLP_TM, D), lambda i, c: (i, 0))
    wsp = pl.BlockSpec((MLP_TC, D), lambda i, c: (c, 0))
    vec = pl.BlockSpec((1, D), lambda i, c: (0, 0))
    return pl.pallas_call(
        body, name="mlp_fwd", grid=(T // MLP_TM, nc),
        in_specs=[row, pl.BlockSpec((MLP_TM, 1), lambda i, c: (i, 0)), vec, wsp, wsp, vec],
        out_specs=[pl.BlockSpec((MLP_TM, MLP_TC), lambda i, c: (i, c)), row, row, row, row],
        out_shape=[jax.ShapeDtypeStruct((T, DFF), BF16), jax.ShapeDtypeStruct((T, D), F32), jax.ShapeDtypeStruct((T, D), BF16),
                   jax.ShapeDtypeStruct((T, D), F32), jax.ShapeDtypeStruct((T, D), BF16)],
        scratch_shapes=[pltpu.VMEM((MLP_TM, D), F32), pltpu.VMEM((MLP_TM, D), BF16)],
        compiler_params=_cparams(("parallel", "arbitrary")),
    )(h, r, g, w_upT, w_dn, g_post)


def _mlp_dx(dff, a, w_upT, w_dn, dep):
    nc = DFF // MLP_TC

    def body(d_ref, a_ref, wu_ref, wd_ref, dep_ref, da_ref, du_ref, acc, d_scr):
        c = pl.program_id(1)

        @pl.when(c == 0)
        def _():
            d_scr[...] = d_ref[...].astype(BF16)
            acc[...] = jnp.zeros_like(acc)
        da = (_nt(d_scr[...], wd_ref[...]) * (2.0 * jnp.maximum(a_ref[...].astype(F32), 0.0))).astype(BF16)
        da_ref[...] = da
        acc[...] += _nn(da, wu_ref[...])

        @pl.when(c == nc - 1)
        def _():
            du_ref[...] = acc[...]

    row = pl.BlockSpec((MLP_TM, D), lambda i, c: (i, 0))
    wsp = pl.BlockSpec((MLP_TC, D), lambda i, c: (c, 0))
    chunk = pl.BlockSpec((MLP_TM, MLP_TC), lambda i, c: (i, c))
    return pl.pallas_call(
        body, name="mlp_dx", grid=(T // MLP_TM, nc),
        in_specs=[row, chunk, wsp, wsp, _ANY], out_specs=[chunk, row],
        out_shape=[jax.ShapeDtypeStruct((T, DFF), BF16), jax.ShapeDtypeStruct((T, D), F32)],
        scratch_shapes=[pltpu.VMEM((MLP_TM, D), F32), pltpu.VMEM((MLP_TM, D), BF16)],
        compiler_params=_cparams(("parallel", "arbitrary")),
    )(dff, a, w_upT, w_dn, dep)


def _rowwise(fn, rows, vecs, out_rows, out_sums, *, tm, name, deps=()):
    specs, arrs = [], []
    R = None
    for r in rows:
        if isinstance(r, tuple):
            arr, width, cb = r
            specs.append(pl.BlockSpec((tm, width), lambda i, cb=cb: (i, cb)))
        else:
            arr = r
            specs.append(pl.BlockSpec((tm, arr.shape[1]), lambda i: (i, 0)))
        R = arr.shape[0] if R is None else R
        assert arr.shape[0] == R, name
        arrs.append(arr)
    assert R % tm == 0, name
    for v in vecs:
        specs.append(pl.BlockSpec(v.shape, lambda i: (0, 0)))
        arrs.append(v)
    nr, nv, no, ns = len(rows), len(vecs), len(out_rows), len(out_sums)
    out_specs = [pl.BlockSpec((tm, w), lambda i: (i, 0)) for w, _ in out_rows]
    out_specs += [pl.BlockSpec(s, lambda i: (0, 0)) for s in out_sums]
    out_shape = [jax.ShapeDtypeStruct((R, w), dt) for w, dt in out_rows]
    out_shape += [jax.ShapeDtypeStruct(s, F32) for s in out_sums]

    nd = len(deps)

    def body(*refs):
        ins = [r[...] for r in refs[:nr + nv]]
        o_refs = refs[nr + nv + nd:nr + nv + nd + no]
        s_refs = refs[nr + nv + nd + no:]
        o_vals, s_vals = fn(*ins)
        for ref, val in zip(o_refs, o_vals):
            ref[...] = val.astype(ref.dtype)
        if ns:
            @pl.when(pl.program_id(0) == 0)
            def _():
                for ref in s_refs:
                    ref[...] = jnp.zeros_like(ref)
            for ref, val in zip(s_refs, s_vals):
                ref[...] += val

    outs = pl.pallas_call(
        body, name=name, grid=(R // tm,), in_specs=specs + [_ANY] * nd, out_specs=out_specs, out_shape=out_shape,
        compiler_params=_cparams(("arbitrary",) if ns else ("parallel",)),
    )(*arrs, *deps)
    return outs


def _colsum(x):
    return jnp.sum(x, axis=0, keepdims=True)


def _rstd(x):
    return lax.rsqrt(jnp.mean(x * x, axis=-1, keepdims=True) + EPS)


def _rms_bwd(xn, r, g, dy):
    dn = dy * g
    return r * (dn - xn * jnp.mean(dn * xn, axis=-1, keepdims=True))


def _partner(t):
    lane = lax.broadcasted_iota(jnp.int32, t.shape, 1)
    up = pltpu.roll(t, 96, 1)
    down = pltpu.roll(t, 32, 1)
    return jnp.where((lane % 64) < 32, up, down)


SLABS = AW // 128


def _rows(r, n, d):
    return pl.ds(r, n, stride=d) if d > 1 else pl.ds(0, n)


def _undilate(src_ref, dst, d, tm):
    for r in range(d):
        for j in range(SLABS):
            dst[j][_rows(r, tm // d, d), :] = src_ref[:, pl.ds(r * AW + j * 128, 128)].astype(dst[j].dtype)


def _dilate(dst_ref, src, d, tm):
    for r in range(d):
        for j in range(SLABS):
            dst_ref[:, pl.ds(r * AW + j * 128, 128)] = src[j][_rows(r, tm // d, d), :].astype(dst_ref.dtype)


def _slab_scratch(n, tm):
    return [pltpu.VMEM((tm, 128), F32)] * (SLABS * n)


def _slab_groups(flat):
    return [flat[SLABS * i:SLABS * (i + 1)] for i in range(len(flat) // SLABS)]


def _slab_specs(tm, first):
    return [pl.BlockSpec((tm, 128), lambda i, j=j: (i, first + j)) for j in range(SLABS)]


def _dil_spec(tm, d):
    return pl.BlockSpec((tm // d, d * AW), lambda i: (i, 0))


ROPE_TM = 512


def _rope_fwd(qkvz, cos128, sin128):
    tm = ROPE_TM

    def body(*refs):
        q_refs, k_refs, v_refs = refs[0:4], refs[4:8], refs[8:12]
        c_ref, s_ref = refs[12], refs[13]
        outs = refs[14:]
        for di, d in enumerate(DILATIONS):
            oq, ok, ov = outs[3 * di:3 * di + 3]
            for r in range(d):
                rows = _rows(r, tm // d, d)
                c, s = c_ref[rows, :], s_ref[rows, :]
                for j in range(SLABS):
                    cols = pl.ds(r * AW + j * 128, 128)
                    q, k = q_refs[j][rows, :], k_refs[j][rows, :]
                    oq[:, cols] = ((q * c + _partner(q) * s) * (HD ** -0.5)).astype(BF16)
                    ok[:, cols] = (k * c + _partner(k) * s).astype(BF16)
                    ov[:, cols] = v_refs[j][rows, :].astype(BF16)

    tab = pl.BlockSpec((tm, 128), lambda i: (i, 0))
    out_specs, out_shape = [], []
    for d in DILATIONS:
        out_specs += [_dil_spec(tm, d)] * 3
        out_shape += [jax.ShapeDtypeStruct((T // d, d * AW), BF16)] * 3
    return pl.pallas_call(
        body, name="rope_fwd", grid=(T // tm,),
        in_specs=_slab_specs(tm, 0) + _slab_specs(tm, 4) + _slab_specs(tm, 8) + [tab, tab],
        out_specs=out_specs, out_shape=out_shape, compiler_params=_cparams(("parallel",)),
    )(*([qkvz] * 12), cos128, sin128)


def _rope_bwd(grads, dz, cos128, sin128):
    tm = 256

    def body(*refs):
        g_refs = refs[0:9]
        dz_ref, c_ref, s_ref, o_ref = refs[9], refs[10], refs[11], refs[12]
        scr = _slab_groups(refs[13:])
        for di, d in enumerate(DILATIONS[1:]):
            for t in range(3):
                _undilate(g_refs[3 * (di + 1) + t], scr[3 * di + t], d, tm)
        c, s = c_ref[...], s_ref[...]
        for j in range(SLABS):
            cols = pl.ds(j * 128, 128)
            tot = [g_refs[t][:, cols] + scr[t][j][...] + scr[3 + t][j][...] for t in range(3)]
            dqr = tot[0] * (HD ** -0.5)
            o_ref[:, pl.ds(j * 128, 128)] = (dqr * c + _partner(dqr * s)).astype(BF16)
            o_ref[:, pl.ds(AW + j * 128, 128)] = (tot[1] * c + _partner(tot[1] * s)).astype(BF16)
            o_ref[:, pl.ds(2 * AW + j * 128, 128)] = tot[2].astype(BF16)
        o_ref[:, pl.ds(3 * AW, AW)] = dz_ref[...].astype(BF16)

    tab = pl.BlockSpec((tm, 128), lambda i: (i, 0))
    in_specs, args = [], []
    for d, g in zip(DILATIONS, grads):
        in_specs += [_dil_spec(tm, d)] * 3
        args += list(g)
    return pl.pallas_call(
        body, name="rope_bwd", grid=(T // tm,),
        in_specs=in_specs + [pl.BlockSpec((tm, AW), lambda i: (i, 0)), tab, tab],
        out_specs=pl.BlockSpec((tm, 4 * AW), lambda i: (i, 0)),
        out_shape=jax.ShapeDtypeStruct((T, 4 * AW), BF16),
        scratch_shapes=_slab_scratch(6, tm),
        compiler_params=_cparams(("parallel",)),
    )(*args, dz, cos128, sin128)


def _dilate_cols(x, first):
    tm = ROPE_TM

    def body(x0, x1, x2, x3, o4, o16):
        xs = (x0, x1, x2, x3)
        for o_ref, d in ((o4, 4), (o16, 16)):
            for r in range(d):
                for j in range(SLABS):
                    o_ref[:, pl.ds(r * AW + j * 128, 128)] = xs[j][_rows(r, tm // d, d), :]

    return pl.pallas_call(
        body, name="dilate_cols", grid=(T // tm,),
        in_specs=_slab_specs(tm, first), out_specs=[_dil_spec(tm, 4), _dil_spec(tm, 16)],
        out_shape=[jax.ShapeDtypeStruct((T // 4, 4 * AW), F32), jax.ShapeDtypeStruct((T // 16, 16 * AW), F32)],
        compiler_params=_cparams(("parallel",)),
    )(x, x, x, x)


def _band_masks():
    qi = lax.broadcasted_iota(jnp.int32, (QBLK, QBLK), 0)
    kj = lax.broadcasted_iota(jnp.int32, (QBLK, QBLK), 1)
    return kj >= qi, kj <= qi


def _attn_fwd(q, k, v, d):
    L = q.shape[0]
    nb = L // QBLK

    def body(q_ref, kp_ref, kc_ref, vp_ref, vc_ref, o_ref, l_ref):
        n = pl.program_id(1)
        mask_p, mask_c = _band_masks()
        bias = jnp.concatenate([jnp.where(mask_p, 0.0, NEG) + jnp.where(n > 0, 0.0, NEG),
                                jnp.where(mask_c, 0.0, NEG)], axis=1)
        s = []
        for h in range(HEADS):
            sl = pl.ds(HD * h, HD)
            qh = q_ref[:, sl]
            s.append(jnp.concatenate([_nt(qh, kp_ref[:, sl]), _nt(qh, kc_ref[:, sl])], axis=1))
        s = jnp.stack(s) + bias
        m = jnp.max(s, axis=2, keepdims=True)
        e = jnp.exp(s - m)
        den = jnp.sum(e, axis=2, keepdims=True)
        p = (e * (1.0 / den)).astype(BF16)
        lse = m + jnp.log(den)
        for h in range(HEADS):
            sl = pl.ds(HD * h, HD)
            o_ref[:, sl] = _nn(p[h, :, :QBLK], vp_ref[:, sl]) + _nn(p[h, :, QBLK:], vc_ref[:, sl])
            l_ref[:, sl] = jnp.broadcast_to(lse[h], (QBLK, HD))

    cur = pl.BlockSpec((QBLK, AW), lambda r, n: (n, r))
    prev = pl.BlockSpec((QBLK, AW), lambda r, n: (jnp.maximum(n - 1, 0), r))
    return pl.pallas_call(
        body, name=f"attn_fwd_d{d}", grid=(d, nb),
        in_specs=[cur, prev, cur, prev, cur], out_specs=[cur, cur],
        out_shape=[jax.ShapeDtypeStruct((L, d * AW), F32)] * 2,
        compiler_params=_cparams(("parallel", "parallel")),
    )(q, k, k, v, v)


def _attn_bwd(q, k, v, do, at, lse, d):
    L = q.shape[0]
    nb = L // QBLK

    def body(q0_ref, q1_ref, kp_ref, kc_ref, vp_ref, vc_ref, do0_ref, do1_ref, at0_ref, at1_ref,
             l0_ref, l1_ref, dq_ref, dk_ref, dv_ref):
        n = pl.program_id(1)
        mask_p, mask_c = _band_masks()
        prev_bias = jnp.where(mask_p, 0.0, NEG)
        bias = jnp.concatenate([prev_bias + jnp.where(n > 0, 0.0, NEG), jnp.where(mask_c, 0.0, NEG),
                                prev_bias + jnp.where(n < nb - 1, 0.0, NEG)], axis=1)
        s, dp, ls, dl, ops = [], [], [], [], []
        for h in range(HEADS):
            sl = pl.ds(HD * h, HD)
            one = pl.ds(HD * h, 1)
            q0, q1 = q0_ref[:, sl], q1_ref[:, sl]
            kp, kc, vp, vc = kp_ref[:, sl], kc_ref[:, sl], vp_ref[:, sl], vc_ref[:, sl]
            do0, do1 = do0_ref[:, sl], do1_ref[:, sl]
            do0b, do1b = do0.astype(BF16), do1.astype(BF16)
            s.append(jnp.concatenate([_nt(q0, kp), _nt(q0, kc), _nt(q1, kc)], axis=1))
            dp.append(jnp.concatenate([_nt(do0b, vp), _nt(do0b, vc), _nt(do1b, vc)], axis=1))
            dl0 = jnp.sum(do0 * at0_ref[:, sl], axis=1, keepdims=True)
            dl1 = jnp.sum(do1 * at1_ref[:, sl], axis=1, keepdims=True)
            dl.append(jnp.concatenate([jnp.broadcast_to(dl0, (QBLK, 2 * QBLK)), jnp.broadcast_to(dl1, (QBLK, QBLK))], axis=1))
            ls.append(jnp.concatenate([jnp.broadcast_to(l0_ref[:, one], (QBLK, 2 * QBLK)),
                                       jnp.broadcast_to(l1_ref[:, one], (QBLK, QBLK))], axis=1))
            ops.append((q0, q1, kp, kc, do0b, do1b))
        p = jnp.exp(jnp.stack(s) + bias - jnp.stack(ls))
        ds = (p * (jnp.stack(dp) - jnp.stack(dl))).astype(BF16)
        p = p.astype(BF16)
        for h in range(HEADS):
            sl = pl.ds(HD * h, HD)
            q0, q1, kp, kc, do0b, do1b = ops[h]
            dq_ref[:, sl] = _nn(ds[h, :, :QBLK], kp) + _nn(ds[h, :, QBLK:2 * QBLK], kc)
            dv_ref[:, sl] = _tn(p[h, :, QBLK:2 * QBLK], do0b) + _tn(p[h, :, 2 * QBLK:], do1b)
            dk_ref[:, sl] = _tn(ds[h, :, QBLK:2 * QBLK], q0) + _tn(ds[h, :, 2 * QBLK:], q1)

    cur = pl.BlockSpec((QBLK, AW), lambda r, n: (n, r))
    prev = pl.BlockSpec((QBLK, AW), lambda r, n: (jnp.maximum(n - 1, 0), r))
    nxt = pl.BlockSpec((QBLK, AW), lambda r, n: (jnp.minimum(n + 1, nb - 1), r))
    return pl.pallas_call(
        body, name=f"attn_bwd_d{d}", grid=(d, nb),
        in_specs=[cur, nxt, prev, cur, prev, cur, cur, nxt, cur, nxt, cur, nxt], out_specs=[cur, cur, cur],
        out_shape=[jax.ShapeDtypeStruct((L, d * AW), F32)] * 3,
        compiler_params=_cparams(("parallel", "parallel")),
    )(q, q, k, k, v, v, do, do, at, at, lse, lse)


def _attn_merge(outs, lses):
    tm = ROPE_TM

    def body(o1, o4, o16, l1, l4, l16, at_ref, ls_ref, at4, ls4, at16, ls16, *flat):
        so4, so16, sl4, sl16, sa, sl = _slab_groups(flat)
        _undilate(o4, so4, 4, tm)
        _undilate(o16, so16, 16, tm)
        _undilate(l4, sl4, 4, tm)
        _undilate(l16, sl16, 16, tm)
        for j in range(SLABS):
            cols = pl.ds(j * 128, 128)
            a, b, c = l1[:, cols], sl4[j][...], sl16[j][...]
            m = jnp.maximum(jnp.maximum(a, b), c)
            e1, e2, e3 = jnp.exp(a - m), jnp.exp(b - m), jnp.exp(c - m)
            s = e1 + e2 + e3
            inv = 1.0 / s
            attn = (e1 * inv) * o1[:, cols] + (e2 * inv) * so4[j][...] + (e3 * inv) * so16[j][...]
            lse = m + jnp.log(s)
            at_ref[:, cols] = attn
            ls_ref[:, cols] = lse
            sa[j][...] = attn
            sl[j][...] = lse
        _dilate(at4, sa, 4, tm)
        _dilate(at16, sa, 16, tm)
        _dilate(ls4, sl, 4, tm)
        _dilate(ls16, sl, 16, tm)

    specs = [_dil_spec(tm, d) for d in DILATIONS]
    tok = specs[0]
    return pl.pallas_call(
        body, name="attn_merge", grid=(T // tm,),
        in_specs=specs + specs, out_specs=[tok, tok, specs[1], specs[1], specs[2], specs[2]],
        out_shape=[jax.ShapeDtypeStruct((T, AW), F32)] * 2 + [jax.ShapeDtypeStruct((T // 4, 4 * AW), F32)] * 2
        + [jax.ShapeDtypeStruct((T // 16, 16 * AW), F32)] * 2,
        scratch_shapes=_slab_scratch(6, tm),
        compiler_params=_cparams(("parallel",)),
    )(*outs, *lses)


CONV_TM = 512
HALO = 8


def _conv_pre(ext, w, b):
    y = b + w[3] * ext
    for kk in range(1, CONV_K):
        y = y + w[3 - kk] * pltpu.roll(ext, kk, 0)
    return y


def _rows_to_block(rows, n, width):
    ri = lax.broadcasted_iota(jnp.int32, (n, width), 0)
    out = jnp.zeros((n, width), F32)
    for j, r in enumerate(rows):
        out = out + jnp.where(ri == j, r, 0.0)
    return out


def _conv_fwd(xbc, w, b):
    nblk = T // CONV_TM

    def body(x_ref, h_ref, w_ref, b_ref, o_ref):
        i = pl.program_id(0)
        halo = jnp.where(i > 0, h_ref[...], 0.0)
        ext = jnp.concatenate([halo, x_ref[...]], axis=0)
        y = _conv_pre(ext, [w_ref[pl.ds(j, 1), :] for j in range(CONV_K)], b_ref[...])[HALO:]
        o_ref[...] = y * _sigmoid(y)

    return pl.pallas_call(
        body, name="conv_fwd", grid=(nblk,),
        in_specs=[pl.BlockSpec((CONV_TM, CONV_CH), lambda i: (i, 0)),
                  pl.BlockSpec((HALO, CONV_CH), lambda i: (jnp.maximum(i * (CONV_TM // HALO) - 1, 0), 0)),
                  pl.BlockSpec((CONV_K, CONV_CH), lambda i: (0, 0)),
                  pl.BlockSpec((1, CONV_CH), lambda i: (0, 0))],
        out_specs=pl.BlockSpec((CONV_TM, CONV_CH), lambda i: (i, 0)),
        out_shape=jax.ShapeDtypeStruct((T, CONV_CH), F32),
        compiler_params=_cparams(("parallel",)),
    )(xbc, xbc, w, b)


def _conv_bwd(xbc, dact, ddt, w, b):
    nblk = T // CONV_TM
    per = CONV_TM // HALO

    def body(x_ref, xb_ref, xa_ref, g_ref, ga_ref, ddt_ref, w_ref, b_ref, dx_ref, dw_ref):
        i = pl.program_id(0)
        wv = [w_ref[pl.ds(j, 1), :] for j in range(CONV_K)]
        before = jnp.where(i > 0, xb_ref[...], 0.0)
        last = i == nblk - 1
        after = jnp.where(last, 0.0, xa_ref[...])
        g_after = jnp.where(last, 0.0, ga_ref[...])
        ext = jnp.concatenate([before, x_ref[...], after], axis=0)
        y = _conv_pre(ext, wv, b_ref[...])[HALO:]
        sg = _sigmoid(y)
        dy = jnp.concatenate([g_ref[...], g_after], axis=0) * (sg * (1.0 + y * (1.0 - sg)))
        n = CONV_TM + HALO
        dx = wv[3] * dy
        for kk in range(1, CONV_K):
            dx = dx + wv[3 - kk] * pltpu.roll(dy, n - kk, 0)
        dx_ref[:, pl.ds(0, CONV_CH)] = dx[:CONV_TM].astype(BF16)
        dx_ref[:, pl.ds(CONV_CH, DT_PAD)] = ddt_ref[...].astype(BF16)
        dyc = dy[:CONV_TM]
        rows = [jnp.sum(dyc * (pltpu.roll(ext, 3 - j, 0) if j < 3 else ext)[HALO:HALO + CONV_TM], axis=0, keepdims=True)
                for j in range(CONV_K)]
        rows.append(jnp.sum(dyc, axis=0, keepdims=True))
        part = _rows_to_block(rows, 8, CONV_CH)

        @pl.when(i == 0)
        def _():
            dw_ref[...] = jnp.zeros_like(dw_ref)
        dw_ref[...] += part

    blk = pl.BlockSpec((CONV_TM, CONV_CH), lambda i: (i, 0))
    hb = pl.BlockSpec((HALO, CONV_CH), lambda i: (jnp.maximum(i * per - 1, 0), 0))
    ha = pl.BlockSpec((HALO, CONV_CH), lambda i: (jnp.minimum((i + 1) * per, T // HALO - 1), 0))
    return pl.pallas_call(
        body, name="conv_bwd", grid=(nblk,),
        in_specs=[blk, hb, ha, blk, ha, pl.BlockSpec((CONV_TM, DT_PAD), lambda i: (i, 0)),
                  pl.BlockSpec((CONV_K, CONV_CH), lambda i: (0, 0)), pl.BlockSpec((1, CONV_CH), lambda i: (0, 0))],
        out_specs=[pl.BlockSpec((CONV_TM, CONV_CH + DT_PAD), lambda i: (i, 0)), pl.BlockSpec((8, CONV_CH), lambda i: (0, 0))],
        out_shape=[jax.ShapeDtypeStruct((T, CONV_CH + DT_PAD), BF16), jax.ShapeDtypeStruct((8, CONV_CH), F32)],
        compiler_params=_cparams(("arbitrary",)),
    )(xbc, xbc, xbc, dact, dact, ddt, w, b)


def _pick(mat, h):
    lane = lax.broadcasted_iota(jnp.int32, mat.shape, 1)
    return jnp.sum(jnp.where(lane == h, mat, 0.0), axis=1, keepdims=True)


def _heads(fn):
    return jnp.stack([fn(h) for h in range(HEADS)])


def _ssd_prep(dt_ref, bias_ref, alog_ref, dsk_ref, b_ref, c_ref, xs_ref, state_ref, cst):
    li = lax.broadcasted_iota(jnp.int32, (CHUNK, CHUNK), 0)
    si = lax.broadcasted_iota(jnp.int32, (CHUNK, CHUNK), 1)
    tri = li >= si
    dtp = dt_ref[...] + bias_ref[...]
    dt = _softplus(dtp)
    A = -jnp.exp(alog_ref[...])
    a = dt * A
    cs = jnp.dot(tri.astype(F32), a, precision=HIGHEST, preferred_element_type=F32)
    cst[...] = cs.T
    Bm = b_ref[...].astype(BF16)
    Cm = c_ref[...].astype(BF16)
    cb = _nt(Cm, Bm)
    dskv = dsk_ref[...]
    cs_col = _heads(lambda h: _pick(cs, h))
    cs_row = _heads(lambda h: cst[pl.ds(h, 1), :])
    dt_col = _heads(lambda h: _pick(dt, h))
    dsk_col = _heads(lambda h: _pick(dskv, h))
    lam = jnp.exp(jnp.where(tri, cs_col - cs_row, NEG))
    x = _heads(lambda h: xs_ref[:, pl.ds(HD * h, HD)])
    xdt = x * dt_col
    prev = _heads(lambda h: state_ref[pl.ds(HD * h, HD), :])
    lane = lax.broadcasted_iota(jnp.int32, (1, 1, CHUNK), 2)
    cl = jnp.sum(jnp.where(lane == CHUNK - 1, cs_row, 0.0), axis=2, keepdims=True)
    f = jnp.exp(cl - cs_col)
    return dict(li=li, si=si, dtp=dtp, dt=dt, A=A, Bm=Bm, Cm=Cm, cb=cb, cs_col=cs_col, dt_col=dt_col, dsk_col=dsk_col,
                lam=lam, x=x, xdt=xdt, prev=prev, cl=cl, f=f)


def _ssd_fwd(act, xbcdt, bias, alog, dsk):
    nc = T // CHUNK

    def body(xs_ref, b_ref, c_ref, dt_ref, bias_ref, alog_ref, dsk_ref, y_ref, st_ref, state, cst):
        @pl.when(pl.program_id(0) == 0)
        def _():
            state[...] = jnp.zeros_like(state)
        st_ref[...] = state[...]
        s = _ssd_prep(dt_ref, bias_ref, alog_ref, dsk_ref, b_ref, c_ref, xs_ref, state, cst)
        Bm, Cm, prev = s["Bm"], s["Cm"], s["prev"]
        g = (s["cb"] * s["lam"]).astype(BF16)
        xdtb = s["xdt"].astype(BF16)
        prevb = prev.astype(BF16)
        y = _heads(lambda h: _nn(g[h], xdtb[h])) + _heads(lambda h: _nt(Cm, prevb[h])) * jnp.exp(s["cs_col"])
        y = y + s["dsk_col"] * s["x"]
        xf = (s["xdt"] * s["f"]).astype(BF16)
        new = prev * jnp.exp(s["cl"]) + _heads(lambda h: _tn(xf[h], Bm))
        for h in range(HEADS):
            y_ref[:, pl.ds(HD * h, HD)] = y[h]
            state[pl.ds(HD * h, HD), :] = new[h]

    vec = pl.BlockSpec((1, DT_PAD), lambda c: (0, 0))
    return pl.pallas_call(
        body, name="ssd_fwd", grid=(nc,),
        in_specs=[pl.BlockSpec((CHUNK, AW), lambda c: (c, 0)), pl.BlockSpec((CHUNK, NS), lambda c: (c, 4)),
                  pl.BlockSpec((CHUNK, NS), lambda c: (c, 5)), pl.BlockSpec((CHUNK, DT_PAD), lambda c: (c, 6)),
                  vec, vec, vec],
        out_specs=[pl.BlockSpec((CHUNK, AW), lambda c: (c, 0)), pl.BlockSpec((None, AW, NS), lambda c: (c, 0, 0))],
        out_shape=[jax.ShapeDtypeStruct((T, AW), F32), jax.ShapeDtypeStruct((nc, AW, NS), F32)],
        scratch_shapes=[pltpu.VMEM((AW, NS), F32), pltpu.VMEM((CHUNK, CHUNK), F32)],
        compiler_params=_cparams(("arbitrary",)),
    )(act, act, act, xbcdt, bias, alog, dsk)


def _ssd_bwd(act, xbcdt, bias, alog, dsk, states, dy):
    nc = T // CHUNK

    def body(xs_ref, b_ref, c_ref, dt_ref, bias_ref, alog_ref, dsk_ref, st_ref, dy_ref,
             dact_ref, ddt_ref, par_ref, dstate, cst):
        step = pl.program_id(0)

        @pl.when(step == 0)
        def _():
            dstate[...] = jnp.zeros_like(dstate)
            par_ref[...] = jnp.zeros_like(par_ref)
        s = _ssd_prep(dt_ref, bias_ref, alog_ref, dsk_ref, b_ref, c_ref, xs_ref, st_ref, cst)
        Bm, Cm, prev, lam, x, xdt, f, cl = s["Bm"], s["Cm"], s["prev"], s["lam"], s["x"], s["xdt"], s["f"], s["cl"]
        lane = lax.broadcasted_iota(jnp.int32, (1, DT_PAD), 1)
        row = lax.broadcasted_iota(jnp.int32, (1, CHUNK, 1), 1)
        g = s["cb"] * lam
        gb, xdtb, prevb = g.astype(BF16), xdt.astype(BF16), prev.astype(BF16)
        dy = _heads(lambda h: dy_ref[:, pl.ds(HD * h, HD)])
        dyb = dy.astype(BF16)
        dnew = _heads(lambda h: dstate[pl.ds(HD * h, HD), :])
        dnewb = dnew.astype(BF16)
        E = jnp.exp(s["cs_col"])
        ecl = jnp.exp(cl)
        dG = _heads(lambda h: _nt(dyb[h], xdtb[h]))
        dxdt = _heads(lambda h: _tn(gb[h], dyb[h]))
        Yo = _heads(lambda h: _nt(Cm, prevb[h]))
        W = _heads(lambda h: _nt(Bm, dnewb[h]))
        dcb = jnp.sum(dG * lam, axis=0)
        Mm = dG * g
        col_sums = jnp.sum(Mm, axis=1, keepdims=True)
        dYo = (dy * E).astype(BF16)
        dxdt = dxdt + W * f
        dF = jnp.sum(W * xdt, axis=2, keepdims=True) * f
        dcl = jnp.sum(dnew * prev, axis=(1, 2), keepdims=True) * ecl + jnp.sum(dF, axis=1, keepdims=True)
        dcs = (jnp.sum(Mm, axis=2, keepdims=True) + jnp.sum(dy * Yo, axis=2, keepdims=True) * E - dF
               + jnp.where(row == CHUNK - 1, dcl, 0.0))
        ddt_x = jnp.sum(dxdt * x, axis=2, keepdims=True)
        dD = jnp.sum(dy * x, axis=(1, 2), keepdims=True)
        dx = s["dsk_col"] * dy + dxdt * s["dt_col"]
        xfb = (xdt * f).astype(BF16)
        dprev = _heads(lambda h: _tn(dYo[h], Cm)) + dnew * ecl
        dcbb = dcb.astype(BF16)
        dC = _nn(dcbb, Bm)
        dB = _tn(dcbb, Cm)
        dcs_mat = -_rows_to_block([col_sums[h] for h in range(HEADS)], CHUNK, CHUNK).T
        ddt_mat = jnp.zeros((CHUNK, DT_PAD), F32)
        dD_row = jnp.zeros((1, DT_PAD), F32)
        for h in range(HEADS):
            sl = pl.ds(HD * h, HD)
            dC = dC + _nn(dYo[h], prevb[h])
            dB = dB + _nn(xfb[h], dnewb[h])
            dcs_mat = dcs_mat + jnp.where(lane == h, dcs[h], 0.0)
            ddt_mat = ddt_mat + jnp.where(lane == h, ddt_x[h], 0.0)
            dD_row = dD_row + jnp.where(lane == h, dD[h], 0.0)
            dact_ref[:, sl] = dx[h]
            dstate[sl, :] = dprev[h]
        dact_ref[:, pl.ds(AW, NS)] = dB
        dact_ref[:, pl.ds(AW + NS, NS)] = dC
        da = jnp.dot((s["li"] <= s["si"]).astype(F32), dcs_mat, precision=HIGHEST, preferred_element_type=F32)
        ddtp = jnp.where(lane < HEADS, (ddt_mat + da * s["A"]) * _sigmoid(s["dtp"]), 0.0)
        ddt_ref[...] = ddtp
        dalog = jnp.where(lane < HEADS, jnp.sum(da * s["dt"], axis=0, keepdims=True) * s["A"], 0.0)
        par_ref[...] += _rows_to_block([jnp.sum(ddtp, axis=0, keepdims=True), dalog, dD_row], 8, DT_PAD)

    vec = pl.BlockSpec((1, DT_PAD), lambda c: (0, 0))
    rev = lambda c: nc - 1 - c
    return pl.pallas_call(
        body, name="ssd_bwd", grid=(nc,),
        in_specs=[pl.BlockSpec((CHUNK, AW), lambda c: (rev(c), 0)), pl.BlockSpec((CHUNK, NS), lambda c: (rev(c), 4)),
                  pl.BlockSpec((CHUNK, NS), lambda c: (rev(c), 5)), pl.BlockSpec((CHUNK, DT_PAD), lambda c: (rev(c), 6)),
                  vec, vec, vec,
                  pl.BlockSpec((None, AW, NS), lambda c: (rev(c), 0, 0)), pl.BlockSpec((CHUNK, AW), lambda c: (rev(c), 0))],
        out_specs=[pl.BlockSpec((CHUNK, CONV_CH), lambda c: (rev(c), 0)), pl.BlockSpec((CHUNK, DT_PAD), lambda c: (rev(c), 0)),
                   pl.BlockSpec((8, DT_PAD), lambda c: (0, 0))],
        out_shape=[jax.ShapeDtypeStruct((T, CONV_CH), F32), jax.ShapeDtypeStruct((T, DT_PAD), F32),
                   jax.ShapeDtypeStruct((8, DT_PAD), F32)],
        scratch_shapes=[pltpu.VMEM((AW, NS), F32), pltpu.VMEM((CHUNK, CHUNK), F32)],
        compiler_params=_cparams(("arbitrary",)),
    )(act, act, act, xbcdt, bias, alog, dsk, states, dy)


def _place():
    return lax.axis_index("x"), lax.axis_index("y"), lax.axis_index("c")


def _slot(px, py, pc):
    return 4 * px + 2 * py + pc


def _all_gather(arrs, name):
    na = len(arrs)

    def body(*refs):
        ins, outs = refs[:na], refs[na:2 * na]
        send_sems, recv_sems, local_sems = refs[2 * na:]
        x, y, c = _place()
        me, sib = (x, y, c), (x, y, 1 - c)
        chips = [(1 - x, y), (x, 1 - y), (1 - x, 1 - y)]

        def copy(a, kk, block, to, src=None):
            dst = outs[a].at[_slot(*block)]
            return pltpu.make_async_remote_copy(
                src_ref=dst if src is None else src, dst_ref=dst,
                send_sem=send_sems.at[a, kk], recv_sem=recv_sems.at[a, kk], device_id=to, device_id_type=MESH)

        mine = [pltpu.make_async_copy(ins[a], outs[a].at[_slot(*me)], local_sems.at[a]) for a in range(na)]
        for cp in mine:
            cp.start()
        first = []
        for a in range(na):
            first.append(copy(a, 0, me, sib, src=ins[a]))
            first += [copy(a, 1 + j, me, (*chip, c), src=ins[a]) for j, chip in enumerate(chips)]
        for cp in first:
            cp.start()
        passed = []
        for j, chip in enumerate(chips):
            for a in range(na):
                copy(a, 1 + j, (*chip, c), me).wait_recv()
                fw = copy(a, 4 + j, (*chip, c), sib)
                fw.start()
                passed.append(fw)
        for a in range(na):
            copy(a, 0, sib, me).wait_recv()
            for j, chip in enumerate(chips):
                copy(a, 4 + j, (*chip, 1 - c), me).wait_recv()
        for cp in first + passed:
            cp.wait_send()
        for cp in mine:
            cp.wait()

    any_spec = pl.BlockSpec(memory_space=pl.ANY)
    return pl.pallas_call(
        body, name=name,
        in_specs=[any_spec] * na, out_specs=[any_spec] * na,
        out_shape=[jax.ShapeDtypeStruct((N_DEV,) + a.shape, a.dtype) for a in arrs],
        scratch_shapes=[pltpu.SemaphoreType.DMA((na, 7)), pltpu.SemaphoreType.DMA((na, 7)),
                        pltpu.SemaphoreType.DMA((na,))],
    )(*arrs)


def _reduce_scatter(part, name):
    _, r, C = part.shape

    def body(part_ref, out_ref, own, got_sib, chip_sum, got_ici, lsem, s1, r1, s2, r2):
        x, y, c = _place()
        chips = [(x, y), (1 - x, y), (x, 1 - y), (1 - x, 1 - y)]
        loc = [pltpu.make_async_copy(part_ref.at[_slot(*chips[kk], c)], own.at[kk], lsem.at[kk]) for kk in range(4)]
        d2d = [pltpu.make_async_remote_copy(
            src_ref=part_ref.at[_slot(*chips[kk], 1 - c)], dst_ref=got_sib.at[kk],
            send_sem=s1.at[kk], recv_sem=r1.at[kk], device_id=(x, y, 1 - c), device_id_type=MESH) for kk in range(4)]
        for cp in loc + d2d:
            cp.start()
        ici = [pltpu.make_async_remote_copy(
            src_ref=chip_sum.at[kk - 1], dst_ref=got_ici.at[kk - 1],
            send_sem=s2.at[kk - 1], recv_sem=r2.at[kk - 1], device_id=(*chips[kk], c), device_id_type=MESH)
            for kk in range(1, 4)]
        for kk in (1, 2, 3):
            loc[kk].wait()
            d2d[kk].wait_recv()
            chip_sum[kk - 1] = (own[kk].astype(F32) + got_sib[kk].astype(F32)).astype(BF16)
            ici[kk - 1].start()
        loc[0].wait()
        d2d[0].wait_recv()
        acc = own[0].astype(F32) + got_sib[0].astype(F32)
        for cp in ici:
            cp.wait_recv()
        out_ref[...] = ((acc + got_ici[0].astype(F32)) + got_ici[1].astype(F32)) + got_ici[2].astype(F32)
        for cp in d2d + ici:
            cp.wait_send()

    return pl.pallas_call(
        body, name=name,
        in_specs=[pl.BlockSpec(memory_space=pl.ANY)],
        out_specs=pl.BlockSpec(memory_space=pltpu.VMEM),
        out_shape=jax.ShapeDtypeStruct((r, C), F32),
        scratch_shapes=[pltpu.VMEM((4, r, C), BF16), pltpu.VMEM((4, r, C), BF16), pltpu.VMEM((3, r, C), BF16),
                        pltpu.VMEM((3, r, C), BF16),
                        pltpu.SemaphoreType.DMA((4,)), pltpu.SemaphoreType.DMA((4,)), pltpu.SemaphoreType.DMA((4,)),
                        pltpu.SemaphoreType.DMA((3,)), pltpu.SemaphoreType.DMA((3,))],
        compiler_params=pltpu.CompilerParams(vmem_limit_bytes=VMEM_LIMIT),
    )(part)


def _all_reduce_small(v, name):
    R, C = v.shape

    def body(v_ref, out_ref, got, send_sems, recv_sems):
        x, y, c = _place()
        mine = _slot(x, y, c)
        copies = []
        for kk in range(1, N_DEV):
            fx, fy, fc = kk >> 2 & 1, kk >> 1 & 1, kk & 1
            peer = (1 - x if fx else x, 1 - y if fy else y, 1 - c if fc else c)
            copies.append(pltpu.make_async_remote_copy(
                src_ref=v_ref, dst_ref=got.at[mine], send_sem=send_sems.at[kk - 1], recv_sem=recv_sems.at[kk - 1],
                device_id=peer, device_id_type=MESH))
        for cp in copies:
            cp.start()
        got[mine] = v_ref[...]
        for cp in copies:
            cp.wait_recv()
        acc = got[0]
        for s in range(1, N_DEV):
            acc = acc + got[s]
        out_ref[...] = acc
        for cp in copies:
            cp.wait_send()

    return pl.pallas_call(
        body, name=name,
        in_specs=[pl.BlockSpec(memory_space=pltpu.VMEM)], out_specs=pl.BlockSpec(memory_space=pltpu.VMEM),
        out_shape=jax.ShapeDtypeStruct((R, C), F32),
        scratch_shapes=[pltpu.VMEM((N_DEV, R, C), F32), pltpu.SemaphoreType.DMA((N_DEV - 1,)),
                        pltpu.SemaphoreType.DMA((N_DEV - 1,))],
    )(v)


_HBM = pl.BlockSpec(memory_space=pltpu.HBM)
_SEM = pl.BlockSpec(memory_space=pltpu.SEMAPHORE)
_EFFECT = pltpu.SideEffectType.DATAFLOW_SIDE_EFFECTING


def _peers(x, y, c):
    out = []
    for kk in range(1, N_DEV):
        fx, fy, fc = kk >> 2 & 1, kk >> 1 & 1, kk & 1
        out.append((1 - x if fx else x, 1 - y if fy else y, 1 - c if fc else c))
    return out


def _send_start(src, per_peer, name, dep):
    blk = src.shape[1:] if per_peer else src.shape

    def body(src_ref, land_ref, dep_ref, send_sems, recv_sems, src_thru, land_thru, token):
        x, y, c = _place()
        mine = _slot(x, y, c)
        for kk, peer in enumerate(_peers(x, y, c)):
            pltpu.make_async_remote_copy(
                src_ref=src_ref.at[_slot(*peer)] if per_peer else src_ref, dst_ref=land_ref.at[mine],
                send_sem=send_sems.at[kk], recv_sem=recv_sems.at[kk], device_id=peer, device_id_type=MESH).start()
        token[...] = jnp.zeros_like(token)

    land = lax.empty((N_DEV,) + tuple(blk), src.dtype)
    *handles, token = pl.pallas_call(
        body, name=name,
        out_shape=(pltpu.SemaphoreType.DMA((N_DEV - 1,)), pltpu.SemaphoreType.DMA((N_DEV - 1,)),
                   pltpu.HBM(src.shape, src.dtype), pltpu.HBM(land.shape, land.dtype),
                   jax.ShapeDtypeStruct((8, 128), F32)),
        in_specs=(_HBM, _HBM, _ANY), out_specs=(_SEM, _SEM, _HBM, _HBM, pl.BlockSpec(memory_space=pltpu.VMEM)),
        input_output_aliases={0: 2, 1: 3},
        compiler_params=pltpu.CompilerParams(has_side_effects=_EFFECT),
    )(pltpu.with_memory_space_constraint(src, pltpu.HBM), pltpu.with_memory_space_constraint(land, pltpu.HBM), dep)
    return handles, token


def _send_wait(handles, after, name):
    send_sems, recv_sems, src_thru, land_thru = handles

    def body(src_ref, land_ref, send_sems, recv_sems, after_ref, src_dead, got_ref):
        me = _place()
        for kk in range(N_DEV - 1):
            cp = pltpu.make_async_remote_copy(
                src_ref=land_ref.at[0], dst_ref=land_ref.at[0], send_sem=send_sems.at[kk], recv_sem=recv_sems.at[kk],
                device_id=me, device_id_type=MESH)
            cp.wait_send()
            cp.wait_recv()

    return pl.pallas_call(
        body, name=name,
        out_shape=(pltpu.HBM(src_thru.shape, src_thru.dtype), pltpu.HBM(land_thru.shape, land_thru.dtype)),
        in_specs=(_HBM, _HBM, _SEM, _SEM, pl.BlockSpec(memory_space=pl.ANY)), out_specs=(_HBM, _HBM),
        input_output_aliases={0: 0, 1: 1},
        compiler_params=pltpu.CompilerParams(has_side_effects=_EFFECT),
    )(src_thru, land_thru, send_sems, recv_sems, after)


def _sum_slots(land, name):
    _, R, C = land.shape
    tm = R if R <= 512 else 512

    def body(x_ref, o_ref):
        acc = x_ref[0].astype(F32)
        for j in range(1, N_DEV):
            acc = acc + x_ref[j].astype(F32)
        o_ref[...] = acc

    return pl.pallas_call(
        body, name=name, grid=(R // tm,),
        in_specs=[pl.BlockSpec((N_DEV, tm, C), lambda i: (0, i, 0))], out_specs=pl.BlockSpec((tm, C), lambda i: (i, 0)),
        out_shape=jax.ShapeDtypeStruct((R, C), F32), compiler_params=_cparams(("parallel",)),
    )(land)


def _adamw(w, g, m, v, name):
    R, C = w.shape
    tm = R if R <= 512 else 256

    def fn(w, g, m, v):
        m2 = ADAM_B1 * m + (1.0 - ADAM_B1) * g
        v2 = ADAM_B2 * v + (1.0 - ADAM_B2) * (g * g)
        m_hat = m2 / (1.0 - ADAM_B1 ** ADAM_STEP)
        v_hat = v2 / (1.0 - ADAM_B2 ** ADAM_STEP)
        delta = -ADAM_LR * (m_hat / (jnp.sqrt(v_hat) + ADAM_EPS) + ADAM_WD * w)
        return (delta, m2, v2), ()
    return _rowwise(fn, [w, g, m, v], [], [(C, F32)] * 3, [], tm=tm, name=name)


SMALL = ["norm_mix_pre", "norm_mix_post", "norm_mlp_pre", "norm_mlp_post", "norm_ple_post",
         "conv_b", "ssd_norm_g", "dt_bias", "a_log", "d_skip"]


def _pad_row(v, width=D):
    return jnp.pad(v, ((0, 0), (0, width - v.shape[1])))


def kernel(x, p, positions, norm_mix_pre, norm_mix_post, w_in, conv_w, conv_b, dt_bias, a_log, d_skip, ssd_norm_g, w_out, norm_mlp_pre, norm_mlp_post, w_up, w_down, w_ple_gate, w_ple_proj, norm_ple_post, loss_target, m_norm_mix_pre, m_norm_mix_post, m_w_in, m_conv_w, m_conv_b, m_dt_bias, m_a_log, m_d_skip, m_ssd_norm_g, m_w_out, m_norm_mlp_pre, m_norm_mlp_post, m_w_up, m_w_down, m_w_ple_gate, m_w_ple_proj, m_norm_ple_post, v_norm_mix_pre, v_norm_mix_post, v_w_in, v_conv_w, v_conv_b, v_dt_bias, v_a_log, v_d_skip, v_ssd_norm_g, v_w_out, v_norm_mlp_pre, v_norm_mlp_post, v_w_up, v_w_down, v_w_ple_gate, v_w_ple_proj, v_norm_ple_post):
    args = dict(locals())
    x2, p2, tgt = x[0], p[0, 0], loss_target[0]
    g1, g2, g3, g4, g5 = norm_mix_pre, norm_mix_post, norm_mlp_pre, norm_mlp_post, norm_ple_post

    me = _slot(*_place())
    pack_in = jnp.pad(w_in[0].T, ((0, W_IN_SHARD_PAD - W_IN_SHARD), (0, 0))).astype(BF16)
    pack_rest = jnp.concatenate([
        w_out[0],
        w_up[0].T,
        w_down[0],
        w_ple_gate[0],
        w_ple_proj[0].T.reshape(32, D),
    ], axis=0).astype(BF16)
    conv_pack = jnp.pad(conv_w[0], ((0, 4), (0, 32)))
    gin, gconv = _all_gather([pack_in, conv_pack], "gather_w_in")
    rest_handles, tok_rest = _send_start(pack_rest, False, "gather_rest_start", gconv)
    w_inT = gin[:, :W_IN_SHARD].reshape(IN_W, D)
    w_qkvzT = w_inT[:4 * AW]
    w_xbcdtT = jnp.pad(w_inT[4 * AW:], ((0, DT_PAD - HEADS), (0, 0)))
    conv_full = gconv[:, :CONV_K, :96].transpose(1, 0, 2).reshape(CONV_K, CONV_CH)

    inv_freq = ROPE_THETA ** (-jnp.arange(HD // 2, dtype=F32) * 2.0 / HD)
    ang = positions[0].astype(F32)[:, None] * inv_freq
    cos, sin = jnp.cos(ang), jnp.sin(ang)
    cos128 = jnp.concatenate([cos, cos, cos, cos], axis=1)
    sin128 = jnp.concatenate([-sin, sin, -sin, sin], axis=1)

    bias_w, alog_w, dsk_w = _pad_row(dt_bias, DT_PAD), _pad_row(a_log, DT_PAD), _pad_row(d_skip, DT_PAD)
    rms_pre = lambda a, r, g: a * r * g

    (u1,) = _rowwise(lambda a, g: ((a * _rstd(a) * g,), ()), [x2], [g1], [(D, BF16)], [], tm=512, name="norm_x")
    p2b = p2.astype(BF16)
    qkvz = _mm(u1, w_qkvzT, tb=True, tm=512, tn=1024, tk=1024, name="proj_qkvz", deps=[tok_rest])
    xbcdt = _mm(u1, w_xbcdtT, tb=True, tm=512, tn=896, tk=1024, name="proj_xbcdt")

    qkv = _rope_fwd(qkvz, cos128, sin128)
    qkv = [qkv[3 * i:3 * i + 3] for i in range(len(DILATIONS))]
    outs, lses = [], []
    for d, (qd, kd, vd) in zip(DILATIONS, qkv):
        o, l = _attn_fwd(qd, kd, vd, d)
        outs.append(o)
        lses.append(l)
    attn, lse, attn4, lse4, attn16, lse16 = _attn_merge(outs, lses)

    act = _conv_fwd(xbcdt, conv_full, conv_b)
    y_ssd, states = _ssd_fwd(act, xbcdt, bias_w, alog_w, dsk_w)

    def gated_fwd(y, z, a, gs):
        gi = y * (z * _sigmoid(z))
        return (jnp.concatenate([a, gi * _rstd(gi) * gs], axis=1),), ()
    (cat,) = _rowwise(gated_fwd, [y_ssd, (qkvz, AW, 3), attn], [ssd_norm_g], [(D, BF16)], [], tm=512, name="gated_norm")

    pack_back, grest = _send_wait(rest_handles, cat, "gather_rest_wait")
    grest = lax.dynamic_update_slice(grest, pack_back[None], (me, 0, 0))
    w_o = grest[:, 0:128].reshape(D, D)
    w_upT = grest[:, 128:640].reshape(DFF, D)
    w_dn = grest[:, 640:1152].reshape(DFF, D)
    w_gate = grest[:, 1152:1280].reshape(D, D)
    w_projT = grest[:, 1280:1312].reshape(D, PLE)

    mix = _mm(cat, w_o, tm=512, tn=1024, tk=1024, name="mix_out")

    def post1(xx, mm, ga, gb):
        h = xx + mm * _rstd(mm) * ga
        return (h, _rstd(h)), ()
    h1, r3 = _rowwise(post1, [x2, mix], [g2, g3], [(D, F32), (1, F32)], [], tm=512, name="post_mix")

    a_up, ff, u2, h2, h2b = _mlp_fwd(h1, r3, g3, w_upT, w_dn, g4)
    relu2 = lambda a: jnp.square(jnp.maximum(a.astype(F32), 0.0))

    gp = _mm(h2b, w_gate, tm=512, tn=1024, tk=1024, name="ple_gate")
    pp = _mm(p2b, w_projT, tb=True, tm=512, tn=1024, tk=256, name="ple_proj")

    def final(hh, gpre, ppv, tg, g):
        sg = _sigmoid(gpre)
        ple = ppv * sg
        r = _rstd(ple)
        n = ple * r
        h3 = hh + n * g
        e = h3 - tg
        dh3 = e * (1.0 / D)
        dple = _rms_bwd(n, r, g, dh3)
        return (dh3, dple * sg, dple * ppv * sg * (1.0 - sg)), (_colsum(dh3 * n), _colsum(0.5 * e * e * (1.0 / D)))
    dh3, dpp, dgp, dg5, loss_vec = _rowwise(final, [h2, gp, pp, tgt], [g5], [(D, F32), (D, BF16), (D, BF16)],
                                            [(1, D), (1, D)], tm=256, name="loss_ple_bwd")

    gw_projT = _mm(dpp, p2b, ta=True, tm=512, tn=256, tk=1024, out_dtypes=(BF16,), name="gw_ple_proj")
    gw_gate = _mm(h2b, dgp, ta=True, tm=512, tn=1024, tk=1024, out_dtypes=(BF16,), name="gw_ple_gate")
    rs_proj, tok_proj = _send_start(gw_projT.reshape(N_DEV, 32, D), True, "rs_start_w_proj", g1)
    rs_gate, tok_gate = _send_start(gw_gate.reshape(N_DEV, 128, D), True, "rs_start_w_gate", g1)
    dh2_g = _mm(dgp, w_gate, tb=True, tm=512, tn=1024, tk=1024, name="dx_ple_gate", deps=[tok_proj, tok_gate])

    def bwd_mlp_post(d3, dg_, f, g):
        dh2 = d3 + dg_
        r = _rstd(f)
        n = f * r
        return (dh2, _rms_bwd(n, r, g, dh2)), (_colsum(dh2 * n),)
    dh2, dff, dg4 = _rowwise(bwd_mlp_post, [dh3, dh2_g, ff], [g4], [(D, F32), (D, BF16)], [(1, D)], tm=256,
                             name="bwd_post_mlp")

    gw_dn = _mm(a_up, dff, ta=True, tm=1024, tn=1024, tk=512, a_pre=relu2, out_dtypes=(BF16,), name="gw_mlp_down")
    rs_dn, tok_dn = _send_start(gw_dn.reshape(N_DEV, 512, D), True, "rs_start_w_down", g1)
    da_up, du2 = _mlp_dx(dff, a_up, w_upT, w_dn, tok_dn)
    gw_upT = _mm(da_up, u2, ta=True, tm=1024, tn=1024, tk=512, out_dtypes=(BF16,), name="gw_mlp_up")
    rs_up, tok_up = _send_start(gw_upT.reshape(N_DEV, 512, D), True, "rs_start_w_up", g1)

    def bwd_mix_post(d2, du, hh, rr, mm, ga, gb):
        n3 = hh * rr
        dh1 = d2 + _rms_bwd(n3, rr, gb, du)
        r = _rstd(mm)
        n2 = mm * r
        return (dh1, _rms_bwd(n2, r, ga, dh1)), (_colsum(du * n3), _colsum(dh1 * n2))
    dh1, dmix, dg3, dg2 = _rowwise(bwd_mix_post, [dh2, du2, h1, r3, mix], [g2, g3], [(D, F32), (D, BF16)],
                                   [(1, D), (1, D)], tm=256, name="bwd_post_mix", deps=[tok_up])

    gw_o = _mm(cat, dmix, ta=True, tm=512, tn=1024, tk=1024, out_dtypes=(BF16,), name="gw_out")
    rs_o, tok_o = _send_start(gw_o.reshape(N_DEV, 128, D), True, "rs_start_w_out", g1)
    dcat = _mm(dmix, w_o, tb=True, tm=512, tn=1024, tk=1024, name="dx_out", deps=[tok_o])

    def gated_bwd(y, z, dyn, gs):
        sg = _sigmoid(z)
        sz = z * sg
        gi = y * sz
        r = _rstd(gi)
        n = gi * r
        dgi = _rms_bwd(n, r, gs, dyn)
        return (dgi * sz, dgi * y * (sg * (1.0 + z * (1.0 - sg)))), (_colsum(dyn * n),)
    dy_ssd, dz, dgs = _rowwise(gated_bwd, [y_ssd, (qkvz, AW, 3), (dcat, AW, 1)], [ssd_norm_g], [(AW, F32)] * 2, [(1, AW)],
                               tm=512, name="bwd_gated_norm")

    dact, ddtw, ssd_par = _ssd_bwd(act, xbcdt, bias_w, alog_w, dsk_w, states, dy_ssd)
    dxbcdt, conv_par = _conv_bwd(xbcdt, dact, ddtw, conv_full, conv_b)

    dattn4, dattn16 = _dilate_cols(dcat, 0)
    qkv_grads = [_attn_bwd(*qkv[0], dcat, attn, lse, 1),
                 _attn_bwd(*qkv[1], dattn4, attn4, lse4, 4),
                 _attn_bwd(*qkv[2], dattn16, attn16, lse16, 16)]
    dqkvz = _rope_bwd(qkv_grads, dz, cos128, sin128)

    du1a = _mm(dqkvz, w_qkvzT, tm=512, tn=1024, tk=1024, name="dx_qkvz")
    du1b = _mm(dxbcdt, w_xbcdtT, tm=512, tn=1024, tk=896, name="dx_xbcdt")
    gw_qkvzT = _mm(dqkvz, u1, ta=True, tm=1024, tn=1024, tk=512, out_dtypes=(BF16,), name="gw_qkvz")
    gw_xbcdtT = _mm(dxbcdt, u1, ta=True, tm=896, tn=1024, tk=512, out_dtypes=(BF16,), name="gw_xbcdt")

    def bwd_in(d1, ua, ub, xx, g):
        rr = _rstd(xx)
        n = xx * rr
        du = ua + ub
        return (d1 + _rms_bwd(n, rr, g, du),), (_colsum(du * n),)
    grad_x, dg1 = _rowwise(bwd_in, [dh1, du1a, du1b, x2], [g1], [(D, F32)], [(1, D)], tm=256, name="bwd_pre_mix")

    gw_inT = jnp.concatenate([gw_qkvzT, gw_xbcdtT], axis=0)[:IN_W]
    gw_inT = jnp.pad(gw_inT.reshape(N_DEV, W_IN_SHARD, D), ((0, 0), (0, W_IN_SHARD_PAD - W_IN_SHARD), (0, 0)))
    g_inT = _reduce_scatter(gw_inT, "rs_w_in")

    def scatter_finish(handles, nm):
        part, land = _send_wait(handles, g_inT, "rs_wait_" + nm)
        own = lax.dynamic_slice(part, (me, 0, 0), (1,) + part.shape[1:])
        return _sum_slots(lax.dynamic_update_slice(land, own, (me, 0, 0)), "rs_sum_" + nm)
    g_out = scatter_finish(rs_o, "w_out")
    g_upT = scatter_finish(rs_up, "w_up")
    g_dn = scatter_finish(rs_dn, "w_down")
    g_gate = scatter_finish(rs_gate, "w_gate")
    g_projT = scatter_finish(rs_proj, "w_proj")

    small = jnp.concatenate([
        dg1, dg2, dg3, dg4, dg5,
        _pad_row(conv_par[4:5]), _pad_row(dgs), _pad_row(ssd_par[0:1]), _pad_row(ssd_par[1:2]), _pad_row(ssd_par[2:3]),
        _pad_row(conv_par[0:4]), loss_vec, jnp.zeros((1, D), F32),
    ], axis=0)
    small = _all_reduce_small(small, "reduce_small")
    loss = jnp.sum(small[14])
    me = lax.axis_index("x") * 4 + lax.axis_index("y") * 2 + lax.axis_index("c")
    g_conv_w = lax.dynamic_slice(small[10:14, :CONV_CH], (0, me * 96), (CONV_K, 96))

    grads = {
        "w_in": g_inT[:W_IN_SHARD].T[None], "w_out": g_out[None], "w_up": g_upT.T[None], "w_down": g_dn[None],
        "w_ple_gate": g_gate[None], "w_ple_proj": g_projT.reshape(128, PLE).T[None], "conv_w": g_conv_w[None],
        "norm_mix_pre": small[0:1], "norm_mix_post": small[1:2], "norm_mlp_pre": small[2:3], "norm_mlp_post": small[3:4],
        "norm_ple_post": small[4:5], "conv_b": small[5:6, :CONV_CH], "ssd_norm_g": small[6:7, :AW],
        "dt_bias": small[7:8, :HEADS], "a_log": small[8:9, :HEADS], "d_skip": small[9:10, :HEADS],
    }
    delta, new_m, new_v = {}, {}, {}
    for nme in ["w_in", "w_out", "w_up", "w_down", "w_ple_gate", "w_ple_proj"]:
        dl, mm_, vv_ = _adamw(args[nme][0], grads[nme][0], args["m_" + nme][0], args["v_" + nme][0], "adamw_" + nme)
        delta[nme], new_m[nme], new_v[nme] = dl[None], mm_[None], vv_[None]

    def pack_small(prefix):
        rows = [_pad_row(args[prefix + nme]) for nme in SMALL]
        rows.append(_pad_row(args[prefix + "conv_w"][0]))
        rows.append(jnp.zeros((2, D), F32))
        return jnp.concatenate(rows, axis=0)
    g_small = jnp.concatenate([small[0:10], _pad_row(g_conv_w), jnp.zeros((2, D), F32)], axis=0)
    dl, mm_, vv_ = _adamw(pack_small(""), g_small, pack_small("m_"), pack_small("v_"), "adamw_small")
    for i, nme in enumerate(SMALL):
        wdt = args[nme].shape[1]
        delta[nme], new_m[nme], new_v[nme] = dl[i:i + 1, :wdt], mm_[i:i + 1, :wdt], vv_[i:i + 1, :wdt]
    delta["conv_w"], new_m["conv_w"], new_v["conv_w"] = dl[None, 10:14, :96], mm_[None, 10:14, :96], vv_[None, 10:14, :96]

    order = ["norm_mix_pre", "norm_mix_post", "w_in", "conv_w", "conv_b", "dt_bias", "a_log", "d_skip", "ssd_norm_g",
             "w_out", "norm_mlp_pre", "norm_mlp_post", "w_up", "w_down", "w_ple_gate", "w_ple_proj", "norm_ple_post"]
    return (loss, grad_x[None], *[grads[n] for n in order], *[delta[n] for n in order],
            *[new_m[n] for n in order], *[new_v[n] for n in order])
```

```python
import functools
import math

import jax
import jax.numpy as jnp
from jax import lax
from jax.experimental import pallas as pl
from jax.experimental.pallas import tpu as pltpu

F32 = jnp.float32
BF16 = jnp.bfloat16
MESH = pl.DeviceIdType.MESH
HIGHEST = lax.Precision.HIGHEST

N_DEV = 8
T = 4096
D = 1024
HEADS = 8
HD = 64
AW = 512
NS = 128
CONV_K = 4
CONV_CH = 768
CHUNK = 128
DFF = 4096
PLE = 256
EPS = 1e-6
ROPE_THETA = 10000.0
DILATIONS = (1, 4, 16)
QBLK = 128
NEG = -1e30
IN_W = 2824
W_IN_SHARD = 353
W_IN_SHARD_PAD = 384
DT_PAD = 128

ADAM_LR, ADAM_B1, ADAM_B2, ADAM_EPS, ADAM_WD, ADAM_STEP = 0.001, 0.9, 0.999, 1e-08, 0.01, 10

VMEM_LIMIT = 56 * 1024 * 1024


_ANY = pl.BlockSpec(memory_space=pl.ANY)


def _cparams(sem=None):
    return pltpu.CompilerParams(dimension_semantics=sem, vmem_limit_bytes=VMEM_LIMIT)


def _dot(a, b, ca, cb, precision=None):
    return lax.dot_general(a, b, (((ca,), (cb,)), ((), ())), preferred_element_type=F32, precision=precision)


def _nn(a, b):
    return _dot(a, b, 1, 0)


def _nt(a, b):
    return _dot(a, b, 1, 1)


def _tn(a, b):
    return _dot(a, b, 0, 0)


def _sigmoid(x):
    return 1.0 / (1.0 + jnp.exp(-x))


def _softplus(x):
    return jnp.maximum(x, 0.0) + jnp.log(1.0 + jnp.exp(-jnp.abs(x)))


def _mm(a, b, *, ta=False, tb=False, tm, tn, tk, name,
        a_pre=None, a_rows=(), a_cols=(), b_pre=None, b_rows=(), b_cols=(),
        epi=None, epi_tiles=(), out_dtypes=(F32,), deps=()):
    if ta:
        K, M = a.shape
    else:
        M, K = a.shape
    if tb:
        N, K2 = b.shape
    else:
        K2, N = b.shape
    assert K == K2 and M % tm == 0 and N % tn == 0 and K % tk == 0, (name, a.shape, b.shape)
    nk = K // tk
    if ta:
        a_spec = pl.BlockSpec((tk, tm), lambda i, j, k: (k, i))
        a_row_specs = [pl.BlockSpec((tk, 1), lambda i, j, k: (k, 0)) for _ in a_rows]
        a_col_specs = [pl.BlockSpec((1, tm), lambda i, j, k: (0, i)) for _ in a_cols]
    else:
        a_spec = pl.BlockSpec((tm, tk), lambda i, j, k: (i, k))
        a_row_specs = [pl.BlockSpec((tm, 1), lambda i, j, k: (i, 0)) for _ in a_rows]
        a_col_specs = [pl.BlockSpec((1, tk), lambda i, j, k: (0, k)) for _ in a_cols]
    if tb:
        b_spec = pl.BlockSpec((tn, tk), lambda i, j, k: (j, k))
        b_row_specs = [pl.BlockSpec((tn, 1), lambda i, j, k: (j, 0)) for _ in b_rows]
        b_col_specs = [pl.BlockSpec((1, tk), lambda i, j, k: (0, k)) for _ in b_cols]
    else:
        b_spec = pl.BlockSpec((tk, tn), lambda i, j, k: (k, j))
        b_row_specs = [pl.BlockSpec((tk, 1), lambda i, j, k: (k, 0)) for _ in b_rows]
        b_col_specs = [pl.BlockSpec((1, tn), lambda i, j, k: (0, j)) for _ in b_cols]
    o_spec = pl.BlockSpec((tm, tn), lambda i, j, k: (i, j))
    na, nb, ne, no = len(a_rows) + len(a_cols), len(b_rows) + len(b_cols), len(epi_tiles), len(out_dtypes)

    def body(*refs):
        a_ref, b_ref = refs[0], refs[1]
        a_ex = refs[2:2 + na]
        b_ex = refs[2 + na:2 + na + nb]
        e_ex = refs[2 + na + nb:2 + na + nb + ne]
        first_out = 2 + na + nb + ne + len(deps)
        outs = refs[first_out:first_out + no]
        acc = refs[-1]
        k = pl.program_id(2)

        @pl.when(k == 0)
        def _():
            acc[...] = jnp.zeros_like(acc)

        at = a_ref[...]
        if a_pre is not None:
            at = a_pre(at, *[r[...] for r in a_ex])
        bt = b_ref[...]
        if b_pre is not None:
            bt = b_pre(bt, *[r[...] for r in b_ex])
        acc[...] += _dot(at.astype(BF16), bt.astype(BF16), 0 if ta else 1, 1 if tb else 0)

        @pl.when(k == nk - 1)
        def _():
            res = acc[...]
            vals = epi(res, *[r[...] for r in e_ex]) if epi is not None else (res,)
            for o_ref, val in zip(outs, vals):
                o_ref[...] = val.astype(o_ref.dtype)

    outs = pl.pallas_call(
        body, name=name,
        grid=(M // tm, N // tn, nk),
        in_specs=([a_spec, b_spec] + a_row_specs + a_col_specs + b_row_specs + b_col_specs + [o_spec] * ne
                  + [_ANY] * len(deps)),
        out_specs=[o_spec] * no,
        out_shape=[jax.ShapeDtypeStruct((M, N), dt) for dt in out_dtypes],
        scratch_shapes=[pltpu.VMEM((tm, tn), F32)],
        compiler_params=_cparams(("parallel", "parallel", "arbitrary")),
    )(a, b, *a_rows, *a_cols, *b_rows, *b_cols, *epi_tiles, *deps)
    return outs[0] if no == 1 else outs


MLP_TM = 1024
MLP_TC = 512


def _mlp_fwd(h, r, g, w_upT, w_dn, g_post):
    nc = DFF // MLP_TC

    def body(h_ref, r_ref, g_ref, wu_ref, wd_ref, gp_ref, a_ref, ff_ref, u_ref, ho_ref, hob_ref, acc, u_scr):
        c = pl.program_id(1)

        @pl.when(c == 0)
        def _():
            u = (h_ref[...] * r_ref[...] * g_ref[...]).astype(BF16)
            u_scr[...] = u
            u_ref[...] = u
            acc[...] = jnp.zeros_like(acc)
        a = _nt(u_scr[...], wu_ref[...])
        a_ref[...] = a.astype(BF16)
        acc[...] += _nn(jnp.square(jnp.maximum(a, 0.0)).astype(BF16), wd_ref[...])

        @pl.when(c == nc - 1)
        def _():
            f = acc[...]
            ff_ref[...] = f
            ho = h_ref[...] + f * _rstd(f) * gp_ref[...]
            ho_ref[...] = ho
            hob_ref[...] = ho.astype(BF16)

    row = pl.BlockSpec((MLP_TM, D), lambda i, c: (i, 0))
    wsp = pl.BlockSpec((MLP_TC, D), lambda i, c: (c, 0))
    vec = pl.BlockSpec((1, D), lambda i, c: (0, 0))
    return pl.pallas_call(
        body, name="mlp_fwd", grid=(T // MLP_TM, nc),
        in_specs=[row, pl.BlockSpec((MLP_TM, 1), lambda i, c: (i, 0)), vec, wsp, wsp, vec],
        out_specs=[pl.BlockSpec((MLP_TM, MLP_TC), lambda i, c: (i, c)), row, row, row, row],
        out_shape=[jax.ShapeDtypeStruct((T, DFF), BF16), jax.ShapeDtypeStruct((T, D), F32), jax.ShapeDtypeStruct((T, D), BF16),
                   jax.ShapeDtypeStruct((T, D), F32), jax.ShapeDtypeStruct((T, D), BF16)],
        scratch_shapes=[pltpu.VMEM((MLP_TM, D), F32), pltpu.VMEM((MLP_TM, D), BF16)],
        compiler_params=_cparams(("parallel", "arbitrary")),
    )(h, r, g, w_upT, w_dn, g_post)


def _mlp_dx(dff, a, w_upT, w_dn, dep):
    nc = DFF // MLP_TC

    def body(d_ref, a_ref, wu_ref, wd_ref, dep_ref, da_ref, du_ref, acc, d_scr):
        c = pl.program_id(1)

        @pl.when(c == 0)
        def _():
            d_scr[...] = d_ref[...].astype(BF16)
            acc[...] = jnp.zeros_like(acc)
        da = (_nt(d_scr[...], wd_ref[...]) * (2.0 * jnp.maximum(a_ref[...].astype(F32), 0.0))).astype(BF16)
        da_ref[...] = da
        acc[...] += _nn(da, wu_ref[...])

        @pl.when(c == nc - 1)
        def _():
            du_ref[...] = acc[...]

    row = pl.BlockSpec((MLP_TM, D), lambda i, c: (i, 0))
    wsp = pl.BlockSpec((MLP_TC, D), lambda i, c: (c, 0))
    chunk = pl.BlockSpec((MLP_TM, MLP_TC), lambda i, c: (i, c))
    return pl.pallas_call(
        body, name="mlp_dx", grid=(T // MLP_TM, nc),
        in_specs=[row, chunk, wsp, wsp, _ANY], out_specs=[chunk, row],
        out_shape=[jax.ShapeDtypeStruct((T, DFF), BF16), jax.ShapeDtypeStruct((T, D), F32)],
        scratch_shapes=[pltpu.VMEM((MLP_TM, D), F32), pltpu.VMEM((MLP_TM, D), BF16)],
        compiler_params=_cparams(("parallel", "arbitrary")),
    )(dff, a, w_upT, w_dn, dep)


def _rowwise(fn, rows, vecs, out_rows, out_sums, *, tm, name, deps=()):
    specs, arrs = [], []
    R = None
    for r in rows:
        if isinstance(r, tuple):
            arr, width, cb = r
            specs.append(pl.BlockSpec((tm, width), lambda i, cb=cb: (i, cb)))
        else:
            arr = r
            specs.append(pl.BlockSpec((tm, arr.shape[1]), lambda i: (i, 0)))
        R = arr.shape[0] if R is None else R
        assert arr.shape[0] == R, name
        arrs.append(arr)
    assert R % tm == 0, name
    for v in vecs:
        specs.append(pl.BlockSpec(v.shape, lambda i: (0, 0)))
        arrs.append(v)
    nr, nv, no, ns = len(rows), len(vecs), len(out_rows), len(out_sums)
    out_specs = [pl.BlockSpec((tm, w), lambda i: (i, 0)) for w, _ in out_rows]
    out_specs += [pl.BlockSpec(s, lambda i: (0, 0)) for s in out_sums]
    out_shape = [jax.ShapeDtypeStruct((R, w), dt) for w, dt in out_rows]
    out_shape += [jax.ShapeDtypeStruct(s, F32) for s in out_sums]

    nd = len(deps)

    def body(*refs):
        ins = [r[...] for r in refs[:nr + nv]]
        o_refs = refs[nr + nv + nd:nr + nv + nd + no]
        s_refs = refs[nr + nv + nd + no:]
        o_vals, s_vals = fn(*ins)
        for ref, val in zip(o_refs, o_vals):
            ref[...] = val.astype(ref.dtype)
        if ns:
            @pl.when(pl.program_id(0) == 0)
            def _():
                for ref in s_refs:
                    ref[...] = jnp.zeros_like(ref)
            for ref, val in zip(s_refs, s_vals):
                ref[...] += val

    outs = pl.pallas_call(
        body, name=name, grid=(R // tm,), in_specs=specs + [_ANY] * nd, out_specs=out_specs, out_shape=out_shape,
        compiler_params=_cparams(("arbitrary",) if ns else ("parallel",)),
    )(*arrs, *deps)
    return outs


def _colsum(x):
    return jnp.sum(x, axis=0, keepdims=True)


def _rstd(x):
    return lax.rsqrt(jnp.mean(x * x, axis=-1, keepdims=True) + EPS)


def _rms_bwd(xn, r, g, dy):
    dn = dy * g
    return r * (dn - xn * jnp.mean(dn * xn, axis=-1, keepdims=True))


def _partner(t):
    lane = lax.broadcasted_iota(jnp.int32, t.shape, 1)
    up = pltpu.roll(t, 96, 1)
    down = pltpu.roll(t, 32, 1)
    return jnp.where((lane % 64) < 32, up, down)


SLABS = AW // 128


def _rows(r, n, d):
    return pl.ds(r, n, stride=d) if d > 1 else pl.ds(0, n)


def _undilate(src_ref, dst, d, tm):
    for r in range(d):
        for j in range(SLABS):
            dst[j][_rows(r, tm // d, d), :] = src_ref[:, pl.ds(r * AW + j * 128, 128)].astype(dst[j].dtype)


def _dilate(dst_ref, src, d, tm):
    for r in range(d):
        for j in range(SLABS):
            dst_ref[:, pl.ds(r * AW + j * 128, 128)] = src[j][_rows(r, tm // d, d), :].astype(dst_ref.dtype)


def _slab_scratch(n, tm):
    return [pltpu.VMEM((tm, 128), F32)] * (SLABS * n)


def _slab_groups(flat):
    return [flat[SLABS * i:SLABS * (i + 1)] for i in range(len(flat) // SLABS)]


def _slab_specs(tm, first):
    return [pl.BlockSpec((tm, 128), lambda i, j=j: (i, first + j)) for j in range(SLABS)]


def _dil_spec(tm, d):
    return pl.BlockSpec((tm // d, d * AW), lambda i: (i, 0))


ROPE_TM = 512


def _rope_fwd(qkvz, cos128, sin128):
    tm = ROPE_TM

    def body(*refs):
        q_refs, k_refs, v_refs = refs[0:4], refs[4:8], refs[8:12]
        c_ref, s_ref = refs[12], refs[13]
        outs = refs[14:]
        for di, d in enumerate(DILATIONS):
            oq, ok, ov = outs[3 * di:3 * di + 3]
            for r in range(d):
                rows = _rows(r, tm // d, d)
                c, s = c_ref[rows, :], s_ref[rows, :]
                for j in range(SLABS):
                    cols = pl.ds(r * AW + j * 128, 128)
                    q, k = q_refs[j][rows, :], k_refs[j][rows, :]
                    oq[:, cols] = ((q * c + _partner(q) * s) * (HD ** -0.5)).astype(BF16)
                    ok[:, cols] = (k * c + _partner(k) * s).astype(BF16)
                    ov[:, cols] = v_refs[j][rows, :].astype(BF16)

    tab = pl.BlockSpec((tm, 128), lambda i: (i, 0))
    out_specs, out_shape = [], []
    for d in DILATIONS:
        out_specs += [_dil_spec(tm, d)] * 3
        out_shape += [jax.ShapeDtypeStruct((T // d, d * AW), BF16)] * 3
    return pl.pallas_call(
        body, name="rope_fwd", grid=(T // tm,),
        in_specs=_slab_specs(tm, 0) + _slab_specs(tm, 4) + _slab_specs(tm, 8) + [tab, tab],
        out_specs=out_specs, out_shape=out_shape, compiler_params=_cparams(("parallel",)),
    )(*([qkvz] * 12), cos128, sin128)


def _rope_bwd(grads, dz, cos128, sin128):
    tm = 256

    def body(*refs):
        g_refs = refs[0:9]
        dz_ref, c_ref, s_ref, o_ref = refs[9], refs[10], refs[11], refs[12]
        scr = _slab_groups(refs[13:])
        for di, d in enumerate(DILATIONS[1:]):
            for t in range(3):
                _undilate(g_refs[3 * (di + 1) + t], scr[3 * di + t], d, tm)
        c, s = c_ref[...], s_ref[...]
        for j in range(SLABS):
            cols = pl.ds(j * 128, 128)
            tot = [g_refs[t][:, cols] + scr[t][j][...] + scr[3 + t][j][...] for t in range(3)]
            dqr = tot[0] * (HD ** -0.5)
            o_ref[:, pl.ds(j * 128, 128)] = (dqr * c + _partner(dqr * s)).astype(BF16)
            o_ref[:, pl.ds(AW + j * 128, 128)] = (tot[1] * c + _partner(tot[1] * s)).astype(BF16)
            o_ref[:, pl.ds(2 * AW + j * 128, 128)] = tot[2].astype(BF16)
        o_ref[:, pl.ds(3 * AW, AW)] = dz_ref[...].astype(BF16)

    tab = pl.BlockSpec((tm, 128), lambda i: (i, 0))
    in_specs, args = [], []
    for d, g in zip(DILATIONS, grads):
        in_specs += [_dil_spec(tm, d)] * 3
        args += list(g)
    return pl.pallas_call(
        body, name="rope_bwd", grid=(T // tm,),
        in_specs=in_specs + [pl.BlockSpec((tm, AW), lambda i: (i, 0)), tab, tab],
        out_specs=pl.BlockSpec((tm, 4 * AW), lambda i: (i, 0)),
        out_shape=jax.ShapeDtypeStruct((T, 4 * AW), BF16),
        scratch_shapes=_slab_scratch(6, tm),
        compiler_params=_cparams(("parallel",)),
    )(*args, dz, cos128, sin128)


def _dilate_cols(x, first):
    tm = ROPE_TM

    def body(x0, x1, x2, x3, o4, o16):
        xs = (x0, x1, x2, x3)
        for o_ref, d in ((o4, 4), (o16, 16)):
            for r in range(d):
                for j in range(SLABS):
                    o_ref[:, pl.ds(r * AW + j * 128, 128)] = xs[j][_rows(r, tm // d, d), :]

    return pl.pallas_call(
        body, name="dilate_cols", grid=(T // tm,),
        in_specs=_slab_specs(tm, first), out_specs=[_dil_spec(tm, 4), _dil_spec(tm, 16)],
        out_shape=[jax.ShapeDtypeStruct((T // 4, 4 * AW), F32), jax.ShapeDtypeStruct((T // 16, 16 * AW), F32)],
        compiler_params=_cparams(("parallel",)),
    )(x, x, x, x)


def _band_masks():
    qi = lax.broadcasted_iota(jnp.int32, (QBLK, QBLK), 0)
    kj = lax.broadcasted_iota(jnp.int32, (QBLK, QBLK), 1)
    return kj >= qi, kj <= qi


def _attn_fwd(q, k, v, d):
    L = q.shape[0]
    nb = L // QBLK

    def body(q_ref, kp_ref, kc_ref, vp_ref, vc_ref, o_ref, l_ref):
        n = pl.program_id(1)
        mask_p, mask_c = _band_masks()
        bias = jnp.concatenate([jnp.where(mask_p, 0.0, NEG) + jnp.where(n > 0, 0.0, NEG),
                                jnp.where(mask_c, 0.0, NEG)], axis=1)
        s = []
        for h in range(HEADS):
            sl = pl.ds(HD * h, HD)
            qh = q_ref[:, sl]
            s.append(jnp.concatenate([_nt(qh, kp_ref[:, sl]), _nt(qh, kc_ref[:, sl])], axis=1))
        s = jnp.stack(s) + bias
        m = jnp.max(s, axis=2, keepdims=True)
        e = jnp.exp(s - m)
        den = jnp.sum(e, axis=2, keepdims=True)
        p = e.astype(BF16)
        inv = 1.0 / den
        lse = m + jnp.log(den)
        for h in range(HEADS):
            sl = pl.ds(HD * h, HD)
            o_ref[:, sl] = (_nn(p[h, :, :QBLK], vp_ref[:, sl]) + _nn(p[h, :, QBLK:], vc_ref[:, sl])) * inv[h]
            l_ref[:, sl] = jnp.broadcast_to(lse[h], (QBLK, HD))

    cur = pl.BlockSpec((QBLK, AW), lambda r, n: (n, r))
    prev = pl.BlockSpec((QBLK, AW), lambda r, n: (jnp.maximum(n - 1, 0), r))
    return pl.pallas_call(
        body, name=f"attn_fwd_d{d}", grid=(d, nb),
        in_specs=[cur, prev, cur, prev, cur], out_specs=[cur, cur],
        out_shape=[jax.ShapeDtypeStruct((L, d * AW), F32)] * 2,
        compiler_params=_cparams(("parallel", "parallel")),
    )(q, k, k, v, v)


def _attn_bwd(q, k, v, do, at, lse, d):
    L = q.shape[0]
    nb = L // QBLK

    def body(q0_ref, q1_ref, kp_ref, kc_ref, vp_ref, vc_ref, do0_ref, do1_ref, at0_ref, at1_ref,
             l0_ref, l1_ref, dq_ref, dk_ref, dv_ref):
        n = pl.program_id(1)
        mask_p, mask_c = _band_masks()
        prev_bias = jnp.where(mask_p, 0.0, NEG)
        bias = jnp.concatenate([prev_bias + jnp.where(n > 0, 0.0, NEG), jnp.where(mask_c, 0.0, NEG),
                                prev_bias + jnp.where(n < nb - 1, 0.0, NEG)], axis=1)
        s, dp, ls, dl, ops = [], [], [], [], []
        for h in range(HEADS):
            sl = pl.ds(HD * h, HD)
            one = pl.ds(HD * h, 1)
            q0, q1 = q0_ref[:, sl], q1_ref[:, sl]
            kp, kc, vp, vc = kp_ref[:, sl], kc_ref[:, sl], vp_ref[:, sl], vc_ref[:, sl]
            do0, do1 = do0_ref[:, sl], do1_ref[:, sl]
            do0b, do1b = do0.astype(BF16), do1.astype(BF16)
            s.append(jnp.concatenate([_nt(q0, kp), _nt(q0, kc), _nt(q1, kc)], axis=1))
            dp.append(jnp.concatenate([_nt(do0b, vp), _nt(do0b, vc), _nt(do1b, vc)], axis=1))
            dl0 = jnp.sum(do0 * at0_ref[:, sl], axis=1, keepdims=True)
            dl1 = jnp.sum(do1 * at1_ref[:, sl], axis=1, keepdims=True)
            dl.append(jnp.concatenate([jnp.broadcast_to(dl0, (QBLK, 2 * QBLK)), jnp.broadcast_to(dl1, (QBLK, QBLK))], axis=1))
            ls.append(jnp.concatenate([jnp.broadcast_to(l0_ref[:, one], (QBLK, 2 * QBLK)),
                                       jnp.broadcast_to(l1_ref[:, one], (QBLK, QBLK))], axis=1))
            ops.append((q0, q1, kp, kc, do0b, do1b))
        p = jnp.exp(jnp.stack(s) + bias - jnp.stack(ls))
        ds = (p * (jnp.stack(dp) - jnp.stack(dl))).astype(BF16)
        p = p.astype(BF16)
        for h in range(HEADS):
            sl = pl.ds(HD * h, HD)
            q0, q1, kp, kc, do0b, do1b = ops[h]
            dq_ref[:, sl] = _nn(ds[h, :, :QBLK], kp) + _nn(ds[h, :, QBLK:2 * QBLK], kc)
            dv_ref[:, sl] = _tn(p[h, :, QBLK:2 * QBLK], do0b) + _tn(p[h, :, 2 * QBLK:], do1b)
            dk_ref[:, sl] = _tn(ds[h, :, QBLK:2 * QBLK], q0) + _tn(ds[h, :, 2 * QBLK:], q1)

    cur = pl.BlockSpec((QBLK, AW), lambda r, n: (n, r))
    prev = pl.BlockSpec((QBLK, AW), lambda r, n: (jnp.maximum(n - 1, 0), r))
    nxt = pl.BlockSpec((QBLK, AW), lambda r, n: (jnp.minimum(n + 1, nb - 1), r))
    return pl.pallas_call(
        body, name=f"attn_bwd_d{d}", grid=(d, nb),
        in_specs=[cur, nxt, prev, cur, prev, cur, cur, nxt, cur, nxt, cur, nxt], out_specs=[cur, cur, cur],
        out_shape=[jax.ShapeDtypeStruct((L, d * AW), F32)] * 3,
        compiler_params=_cparams(("parallel", "parallel")),
    )(q, q, k, k, v, v, do, do, at, at, lse, lse)


def _attn_merge(outs, lses):
    tm = ROPE_TM

    def body(o1, o4, o16, l1, l4, l16, at_ref, ls_ref, at4, ls4, at16, ls16, *flat):
        so4, so16, sl4, sl16, sa, sl = _slab_groups(flat)
        _undilate(o4, so4, 4, tm)
        _undilate(o16, so16, 16, tm)
        _undilate(l4, sl4, 4, tm)
        _undilate(l16, sl16, 16, tm)
        for j in range(SLABS):
            cols = pl.ds(j * 128, 128)
            a, b, c = l1[:, cols], sl4[j][...], sl16[j][...]
            m = jnp.maximum(jnp.maximum(a, b), c)
            e1, e2, e3 = jnp.exp(a - m), jnp.exp(b - m), jnp.exp(c - m)
            s = e1 + e2 + e3
            inv = 1.0 / s
            attn = (e1 * inv) * o1[:, cols] + (e2 * inv) * so4[j][...] + (e3 * inv) * so16[j][...]
            lse = m + jnp.log(s)
            at_ref[:, cols] = attn
            ls_ref[:, cols] = lse
            sa[j][...] = attn
            sl[j][...] = lse
        _dilate(at4, sa, 4, tm)
        _dilate(at16, sa, 16, tm)
        _dilate(ls4, sl, 4, tm)
        _dilate(ls16, sl, 16, tm)

    specs = [_dil_spec(tm, d) for d in DILATIONS]
    tok = specs[0]
    return pl.pallas_call(
        body, name="attn_merge", grid=(T // tm,),
        in_specs=specs + specs, out_specs=[tok, tok, specs[1], specs[1], specs[2], specs[2]],
        out_shape=[jax.ShapeDtypeStruct((T, AW), F32)] * 2 + [jax.ShapeDtypeStruct((T // 4, 4 * AW), F32)] * 2
        + [jax.ShapeDtypeStruct((T // 16, 16 * AW), F32)] * 2,
        scratch_shapes=_slab_scratch(6, tm),
        compiler_params=_cparams(("parallel",)),
    )(*outs, *lses)


CONV_TM = 512
HALO = 8


def _conv_pre(ext, w, b):
    y = b + w[3] * ext
    for kk in range(1, CONV_K):
        y = y + w[3 - kk] * pltpu.roll(ext, kk, 0)
    return y


def _rows_to_block(rows, n, width):
    ri = lax.broadcasted_iota(jnp.int32, (n, width), 0)
    out = jnp.zeros((n, width), F32)
    for j, r in enumerate(rows):
        out = out + jnp.where(ri == j, r, 0.0)
    return out


def _conv_fwd(xbc, w, b):
    nblk = T // CONV_TM

    def body(x_ref, h_ref, w_ref, b_ref, o_ref):
        i = pl.program_id(0)
        halo = jnp.where(i > 0, h_ref[...], 0.0)
        ext = jnp.concatenate([halo, x_ref[...]], axis=0)
        y = _conv_pre(ext, [w_ref[pl.ds(j, 1), :] for j in range(CONV_K)], b_ref[...])[HALO:]
        o_ref[...] = y * _sigmoid(y)

    return pl.pallas_call(
        body, name="conv_fwd", grid=(nblk,),
        in_specs=[pl.BlockSpec((CONV_TM, CONV_CH), lambda i: (i, 0)),
                  pl.BlockSpec((HALO, CONV_CH), lambda i: (jnp.maximum(i * (CONV_TM // HALO) - 1, 0), 0)),
                  pl.BlockSpec((CONV_K, CONV_CH), lambda i: (0, 0)),
                  pl.BlockSpec((1, CONV_CH), lambda i: (0, 0))],
        out_specs=pl.BlockSpec((CONV_TM, CONV_CH), lambda i: (i, 0)),
        out_shape=jax.ShapeDtypeStruct((T, CONV_CH), F32),
        compiler_params=_cparams(("parallel",)),
    )(xbc, xbc, w, b)


def _conv_bwd(xbc, dact, ddt, w, b):
    nblk = T // CONV_TM
    per = CONV_TM // HALO

    def body(x_ref, xb_ref, xa_ref, g_ref, ga_ref, ddt_ref, w_ref, b_ref, dx_ref, dw_ref):
        i = pl.program_id(0)
        wv = [w_ref[pl.ds(j, 1), :] for j in range(CONV_K)]
        before = jnp.where(i > 0, xb_ref[...], 0.0)
        last = i == nblk - 1
        after = jnp.where(last, 0.0, xa_ref[...])
        g_after = jnp.where(last, 0.0, ga_ref[...])
        ext = jnp.concatenate([before, x_ref[...], after], axis=0)
        y = _conv_pre(ext, wv, b_ref[...])[HALO:]
        sg = _sigmoid(y)
        dy = jnp.concatenate([g_ref[...], g_after], axis=0) * (sg * (1.0 + y * (1.0 - sg)))
        n = CONV_TM + HALO
        dx = wv[3] * dy
        for kk in range(1, CONV_K):
            dx = dx + wv[3 - kk] * pltpu.roll(dy, n - kk, 0)
        dx_ref[:, pl.ds(0, CONV_CH)] = dx[:CONV_TM].astype(BF16)
        dx_ref[:, pl.ds(CONV_CH, DT_PAD)] = ddt_ref[...].astype(BF16)
        dyc = dy[:CONV_TM]
        rows = [jnp.sum(dyc * (pltpu.roll(ext, 3 - j, 0) if j < 3 else ext)[HALO:HALO + CONV_TM], axis=0, keepdims=True)
                for j in range(CONV_K)]
        rows.append(jnp.sum(dyc, axis=0, keepdims=True))
        part = _rows_to_block(rows, 8, CONV_CH)

        @pl.when(i == 0)
        def _():
            dw_ref[...] = jnp.zeros_like(dw_ref)
        dw_ref[...] += part

    blk = pl.BlockSpec((CONV_TM, CONV_CH), lambda i: (i, 0))
    hb = pl.BlockSpec((HALO, CONV_CH), lambda i: (jnp.maximum(i * per - 1, 0), 0))
    ha = pl.BlockSpec((HALO, CONV_CH), lambda i: (jnp.minimum((i + 1) * per, T // HALO - 1), 0))
    return pl.pallas_call(
        body, name="conv_bwd", grid=(nblk,),
        in_specs=[blk, hb, ha, blk, ha, pl.BlockSpec((CONV_TM, DT_PAD), lambda i: (i, 0)),
                  pl.BlockSpec((CONV_K, CONV_CH), lambda i: (0, 0)), pl.BlockSpec((1, CONV_CH), lambda i: (0, 0))],
        out_specs=[pl.BlockSpec((CONV_TM, CONV_CH + DT_PAD), lambda i: (i, 0)), pl.BlockSpec((8, CONV_CH), lambda i: (0, 0))],
        out_shape=[jax.ShapeDtypeStruct((T, CONV_CH + DT_PAD), BF16), jax.ShapeDtypeStruct((8, CONV_CH), F32)],
        compiler_params=_cparams(("arbitrary",)),
    )(xbc, xbc, xbc, dact, dact, ddt, w, b)


def _pick(mat, h):
    lane = lax.broadcasted_iota(jnp.int32, mat.shape, 1)
    return jnp.sum(jnp.where(lane == h, mat, 0.0), axis=1, keepdims=True)


def _heads(fn):
    return jnp.stack([fn(h) for h in range(HEADS)])


def _ssd_prep(dt_ref, bias_ref, alog_ref, dsk_ref, b_ref, c_ref, xs_ref, state_ref, cst):
    li = lax.broadcasted_iota(jnp.int32, (CHUNK, CHUNK), 0)
    si = lax.broadcasted_iota(jnp.int32, (CHUNK, CHUNK), 1)
    tri = li >= si
    dtp = dt_ref[...] + bias_ref[...]
    dt = _softplus(dtp)
    A = -jnp.exp(alog_ref[...])
    a = dt * A
    cs = jnp.dot(tri.astype(F32), a, precision=HIGHEST, preferred_element_type=F32)
    cst[...] = cs.T
    Bm = b_ref[...].astype(BF16)
    Cm = c_ref[...].astype(BF16)
    cb = _nt(Cm, Bm)
    dskv = dsk_ref[...]
    cs_col = _heads(lambda h: _pick(cs, h))
    cs_row = _heads(lambda h: cst[pl.ds(h, 1), :])
    dt_col = _heads(lambda h: _pick(dt, h))
    dsk_col = _heads(lambda h: _pick(dskv, h))
    lam = jnp.exp(jnp.where(tri, cs_col - cs_row, NEG))
    x = _heads(lambda h: xs_ref[:, pl.ds(HD * h, HD)])
    xdt = x * dt_col
    prev = _heads(lambda h: state_ref[pl.ds(HD * h, HD), :])
    lane = lax.broadcasted_iota(jnp.int32, (1, 1, CHUNK), 2)
    cl = jnp.sum(jnp.where(lane == CHUNK - 1, cs_row, 0.0), axis=2, keepdims=True)
    f = jnp.exp(cl - cs_col)
    return dict(li=li, si=si, dtp=dtp, dt=dt, A=A, Bm=Bm, Cm=Cm, cb=cb, cs_col=cs_col, dt_col=dt_col, dsk_col=dsk_col,
                lam=lam, x=x, xdt=xdt, prev=prev, cl=cl, f=f)


def _ssd_fwd(act, xbcdt, bias, alog, dsk):
    nc = T // CHUNK

    def body(xs_ref, b_ref, c_ref, dt_ref, bias_ref, alog_ref, dsk_ref, y_ref, st_ref, state, cst):
        @pl.when(pl.program_id(0) == 0)
        def _():
            state[...] = jnp.zeros_like(state)
        st_ref[...] = state[...]
        s = _ssd_prep(dt_ref, bias_ref, alog_ref, dsk_ref, b_ref, c_ref, xs_ref, state, cst)
        Bm, Cm, prev = s["Bm"], s["Cm"], s["prev"]
        g = (s["cb"] * s["lam"]).astype(BF16)
        xdtb = s["xdt"].astype(BF16)
        prevb = prev.astype(BF16)
        y = _heads(lambda h: _nn(g[h], xdtb[h])) + _heads(lambda h: _nt(Cm, prevb[h])) * jnp.exp(s["cs_col"])
        y = y + s["dsk_col"] * s["x"]
        xf = (s["xdt"] * s["f"]).astype(BF16)
        new = prev * jnp.exp(s["cl"]) + _heads(lambda h: _tn(xf[h], Bm))
        for h in range(HEADS):
            y_ref[:, pl.ds(HD * h, HD)] = y[h]
            state[pl.ds(HD * h, HD), :] = new[h]

    vec = pl.BlockSpec((1, DT_PAD), lambda c: (0, 0))
    return pl.pallas_call(
        body, name="ssd_fwd", grid=(nc,),
        in_specs=[pl.BlockSpec((CHUNK, AW), lambda c: (c, 0)), pl.BlockSpec((CHUNK, NS), lambda c: (c, 4)),
                  pl.BlockSpec((CHUNK, NS), lambda c: (c, 5)), pl.BlockSpec((CHUNK, DT_PAD), lambda c: (c, 6)),
                  vec, vec, vec],
        out_specs=[pl.BlockSpec((CHUNK, AW), lambda c: (c, 0)), pl.BlockSpec((None, AW, NS), lambda c: (c, 0, 0))],
        out_shape=[jax.ShapeDtypeStruct((T, AW), F32), jax.ShapeDtypeStruct((nc, AW, NS), F32)],
        scratch_shapes=[pltpu.VMEM((AW, NS), F32), pltpu.VMEM((CHUNK, CHUNK), F32)],
        compiler_params=_cparams(("arbitrary",)),
    )(act, act, act, xbcdt, bias, alog, dsk)


def _ssd_bwd(act, xbcdt, bias, alog, dsk, states, dy):
    nc = T // CHUNK

    def body(xs_ref, b_ref, c_ref, dt_ref, bias_ref, alog_ref, dsk_ref, st_ref, dy_ref,
             dact_ref, ddt_ref, par_ref, dstate, cst):
        step = pl.program_id(0)

        @pl.when(step == 0)
        def _():
            dstate[...] = jnp.zeros_like(dstate)
            par_ref[...] = jnp.zeros_like(par_ref)
        s = _ssd_prep(dt_ref, bias_ref, alog_ref, dsk_ref, b_ref, c_ref, xs_ref, st_ref, cst)
        Bm, Cm, prev, lam, x, xdt, f, cl = s["Bm"], s["Cm"], s["prev"], s["lam"], s["x"], s["xdt"], s["f"], s["cl"]
        lane = lax.broadcasted_iota(jnp.int32, (1, DT_PAD), 1)
        row = lax.broadcasted_iota(jnp.int32, (1, CHUNK, 1), 1)
        g = s["cb"] * lam
        gb, xdtb, prevb = g.astype(BF16), xdt.astype(BF16), prev.astype(BF16)
        dy = _heads(lambda h: dy_ref[:, pl.ds(HD * h, HD)])
        dyb = dy.astype(BF16)
        dnew = _heads(lambda h: dstate[pl.ds(HD * h, HD), :])
        dnewb = dnew.astype(BF16)
        E = jnp.exp(s["cs_col"])
        ecl = jnp.exp(cl)
        dG = _heads(lambda h: _nt(dyb[h], xdtb[h]))
        dxdt = _heads(lambda h: _tn(gb[h], dyb[h]))
        Yo = _heads(lambda h: _nt(Cm, prevb[h]))
        W = _heads(lambda h: _nt(Bm, dnewb[h]))
        dcb = jnp.sum(dG * lam, axis=0)
        Mm = dG * g
        col_sums = jnp.sum(Mm, axis=1, keepdims=True)
        dYo = (dy * E).astype(BF16)
        dxdt = dxdt + W * f
        dF = jnp.sum(W * xdt, axis=2, keepdims=True) * f
        dcl = jnp.sum(dnew * prev, axis=(1, 2), keepdims=True) * ecl + jnp.sum(dF, axis=1, keepdims=True)
        dcs = (jnp.sum(Mm, axis=2, keepdims=True) + jnp.sum(dy * Yo, axis=2, keepdims=True) * E - dF
               + jnp.where(row == CHUNK - 1, dcl, 0.0))
        ddt_x = jnp.sum(dxdt * x, axis=2, keepdims=True)
        dD = jnp.sum(dy * x, axis=(1, 2), keepdims=True)
        dx = s["dsk_col"] * dy + dxdt * s["dt_col"]
        xfb = (xdt * f).astype(BF16)
        dprev = _heads(lambda h: _tn(dYo[h], Cm)) + dnew * ecl
        dcbb = dcb.astype(BF16)
        dC = _nn(dcbb, Bm)
        dB = _tn(dcbb, Cm)
        dcs_mat = -_rows_to_block([col_sums[h] for h in range(HEADS)], CHUNK, CHUNK).T
        ddt_mat = jnp.zeros((CHUNK, DT_PAD), F32)
        dD_row = jnp.zeros((1, DT_PAD), F32)
        for h in range(HEADS):
            sl = pl.ds(HD * h, HD)
            dC = dC + _nn(dYo[h], prevb[h])
            dB = dB + _nn(xfb[h], dnewb[h])
            dcs_mat = dcs_mat + jnp.where(lane == h, dcs[h], 0.0)
            ddt_mat = ddt_mat + jnp.where(lane == h, ddt_x[h], 0.0)
            dD_row = dD_row + jnp.where(lane == h, dD[h], 0.0)
            dact_ref[:, sl] = dx[h]
            dstate[sl, :] = dprev[h]
        dact_ref[:, pl.ds(AW, NS)] = dB
        dact_ref[:, pl.ds(AW + NS, NS)] = dC
        da = jnp.dot((s["li"] <= s["si"]).astype(F32), dcs_mat, precision=HIGHEST, preferred_element_type=F32)
        ddtp = jnp.where(lane < HEADS, (ddt_mat + da * s["A"]) * _sigmoid(s["dtp"]), 0.0)
        ddt_ref[...] = ddtp
        dalog = jnp.where(lane < HEADS, jnp.sum(da * s["dt"], axis=0, keepdims=True) * s["A"], 0.0)
        par_ref[...] += _rows_to_block([jnp.sum(ddtp, axis=0, keepdims=True), dalog, dD_row], 8, DT_PAD)

    vec = pl.BlockSpec((1, DT_PAD), lambda c: (0, 0))
    rev = lambda c: nc - 1 - c
    return pl.pallas_call(
        body, name="ssd_bwd", grid=(nc,),
        in_specs=[pl.BlockSpec((CHUNK, AW), lambda c: (rev(c), 0)), pl.BlockSpec((CHUNK, NS), lambda c: (rev(c), 4)),
                  pl.BlockSpec((CHUNK, NS), lambda c: (rev(c), 5)), pl.BlockSpec((CHUNK, DT_PAD), lambda c: (rev(c), 6)),
                  vec, vec, vec,
                  pl.BlockSpec((None, AW, NS), lambda c: (rev(c), 0, 0)), pl.BlockSpec((CHUNK, AW), lambda c: (rev(c), 0))],
        out_specs=[pl.BlockSpec((CHUNK, CONV_CH), lambda c: (rev(c), 0)), pl.BlockSpec((CHUNK, DT_PAD), lambda c: (rev(c), 0)),
                   pl.BlockSpec((8, DT_PAD), lambda c: (0, 0))],
        out_shape=[jax.ShapeDtypeStruct((T, CONV_CH), F32), jax.ShapeDtypeStruct((T, DT_PAD), F32),
                   jax.ShapeDtypeStruct((8, DT_PAD), F32)],
        scratch_shapes=[pltpu.VMEM((AW, NS), F32), pltpu.VMEM((CHUNK, CHUNK), F32)],
        compiler_params=_cparams(("arbitrary",)),
    )(act, act, act, xbcdt, bias, alog, dsk, states, dy)


def _place():
    return lax.axis_index("x"), lax.axis_index("y"), lax.axis_index("c")


def _slot(px, py, pc):
    return 4 * px + 2 * py + pc


def _all_gather(arrs, name):
    na = len(arrs)

    def body(*refs):
        ins, outs = refs[:na], refs[na:2 * na]
        send_sems, recv_sems, local_sems = refs[2 * na:]
        x, y, c = _place()
        me, sib = (x, y, c), (x, y, 1 - c)
        chips = [(1 - x, y), (x, 1 - y), (1 - x, 1 - y)]

        def copy(a, kk, block, to, src=None):
            dst = outs[a].at[_slot(*block)]
            return pltpu.make_async_remote_copy(
                src_ref=dst if src is None else src, dst_ref=dst,
                send_sem=send_sems.at[a, kk], recv_sem=recv_sems.at[a, kk], device_id=to, device_id_type=MESH)

        mine = [pltpu.make_async_copy(ins[a], outs[a].at[_slot(*me)], local_sems.at[a]) for a in range(na)]
        for cp in mine:
            cp.start()
        first = []
        for a in range(na):
            first.append(copy(a, 0, me, sib, src=ins[a]))
            first += [copy(a, 1 + j, me, (*chip, c), src=ins[a]) for j, chip in enumerate(chips)]
        for cp in first:
            cp.start()
        passed = []
        for j, chip in enumerate(chips):
            for a in range(na):
                copy(a, 1 + j, (*chip, c), me).wait_recv()
                fw = copy(a, 4 + j, (*chip, c), sib)
                fw.start()
                passed.append(fw)
        for a in range(na):
            copy(a, 0, sib, me).wait_recv()
            for j, chip in enumerate(chips):
                copy(a, 4 + j, (*chip, 1 - c), me).wait_recv()
        for cp in first + passed:
            cp.wait_send()
        for cp in mine:
            cp.wait()

    any_spec = pl.BlockSpec(memory_space=pl.ANY)
    return pl.pallas_call(
        body, name=name,
        in_specs=[any_spec] * na, out_specs=[any_spec] * na,
        out_shape=[jax.ShapeDtypeStruct((N_DEV,) + a.shape, a.dtype) for a in arrs],
        scratch_shapes=[pltpu.SemaphoreType.DMA((na, 7)), pltpu.SemaphoreType.DMA((na, 7)),
                        pltpu.SemaphoreType.DMA((na,))],
    )(*arrs)


def _reduce_scatter(part, name):
    _, r, C = part.shape

    def body(part_ref, out_ref, own, got_sib, chip_sum, got_ici, lsem, s1, r1, s2, r2):
        x, y, c = _place()
        chips = [(x, y), (1 - x, y), (x, 1 - y), (1 - x, 1 - y)]
        loc = [pltpu.make_async_copy(part_ref.at[_slot(*chips[kk], c)], own.at[kk], lsem.at[kk]) for kk in range(4)]
        d2d = [pltpu.make_async_remote_copy(
            src_ref=part_ref.at[_slot(*chips[kk], 1 - c)], dst_ref=got_sib.at[kk],
            send_sem=s1.at[kk], recv_sem=r1.at[kk], device_id=(x, y, 1 - c), device_id_type=MESH) for kk in range(4)]
        for cp in loc + d2d:
            cp.start()
        ici = [pltpu.make_async_remote_copy(
            src_ref=chip_sum.at[kk - 1], dst_ref=got_ici.at[kk - 1],
            send_sem=s2.at[kk - 1], recv_sem=r2.at[kk - 1], device_id=(*chips[kk], c), device_id_type=MESH)
            for kk in range(1, 4)]
        for kk in (1, 2, 3):
            loc[kk].wait()
            d2d[kk].wait_recv()
            chip_sum[kk - 1] = (own[kk].astype(F32) + got_sib[kk].astype(F32)).astype(BF16)
            ici[kk - 1].start()
        loc[0].wait()
        d2d[0].wait_recv()
        acc = own[0].astype(F32) + got_sib[0].astype(F32)
        for cp in ici:
            cp.wait_recv()
        out_ref[...] = ((acc + got_ici[0].astype(F32)) + got_ici[1].astype(F32)) + got_ici[2].astype(F32)
        for cp in d2d + ici:
            cp.wait_send()

    return pl.pallas_call(
        body, name=name,
        in_specs=[pl.BlockSpec(memory_space=pl.ANY)],
        out_specs=pl.BlockSpec(memory_space=pltpu.VMEM),
        out_shape=jax.ShapeDtypeStruct((r, C), F32),
        scratch_shapes=[pltpu.VMEM((4, r, C), BF16), pltpu.VMEM((4, r, C), BF16), pltpu.VMEM((3, r, C), BF16),
                        pltpu.VMEM((3, r, C), BF16),
                        pltpu.SemaphoreType.DMA((4,)), pltpu.SemaphoreType.DMA((4,)), pltpu.SemaphoreType.DMA((4,)),
                        pltpu.SemaphoreType.DMA((3,)), pltpu.SemaphoreType.DMA((3,))],
        compiler_params=pltpu.CompilerParams(vmem_limit_bytes=VMEM_LIMIT),
    )(part)


def _all_reduce_small(v, name):
    R, C = v.shape

    def body(v_ref, out_ref, got, send_sems, recv_sems):
        x, y, c = _place()
        mine = _slot(x, y, c)
        copies = []
        for kk in range(1, N_DEV):
            fx, fy, fc = kk >> 2 & 1, kk >> 1 & 1, kk & 1
            peer = (1 - x if fx else x, 1 - y if fy else y, 1 - c if fc else c)
            copies.append(pltpu.make_async_remote_copy(
                src_ref=v_ref, dst_ref=got.at[mine], send_sem=send_sems.at[kk - 1], recv_sem=recv_sems.at[kk - 1],
                device_id=peer, device_id_type=MESH))
        for cp in copies:
            cp.start()
        got[mine] = v_ref[...]
        for cp in copies:
            cp.wait_recv()
        acc = got[0]
        for s in range(1, N_DEV):
            acc = acc + got[s]
        out_ref[...] = acc
        for cp in copies:
            cp.wait_send()

    return pl.pallas_call(
        body, name=name,
        in_specs=[pl.BlockSpec(memory_space=pltpu.VMEM)], out_specs=pl.BlockSpec(memory_space=pltpu.VMEM),
        out_shape=jax.ShapeDtypeStruct((R, C), F32),
        scratch_shapes=[pltpu.VMEM((N_DEV, R, C), F32), pltpu.SemaphoreType.DMA((N_DEV - 1,)),
                        pltpu.SemaphoreType.DMA((N_DEV - 1,))],
    )(v)


_HBM = pl.BlockSpec(memory_space=pltpu.HBM)
_SEM = pl.BlockSpec(memory_space=pltpu.SEMAPHORE)
_EFFECT = pltpu.SideEffectType.DATAFLOW_SIDE_EFFECTING


def _peers(x, y, c):
    out = []
    for kk in range(1, N_DEV):
        fx, fy, fc = kk >> 2 & 1, kk >> 1 & 1, kk & 1
        out.append((1 - x if fx else x, 1 - y if fy else y, 1 - c if fc else c))
    return out


def _send_start(src, per_peer, name, dep):
    blk = src.shape[1:] if per_peer else src.shape

    def body(src_ref, land_ref, dep_ref, send_sems, recv_sems, src_thru, land_thru, token):
        x, y, c = _place()
        mine = _slot(x, y, c)
        for kk, peer in enumerate(_peers(x, y, c)):
            pltpu.make_async_remote_copy(
                src_ref=src_ref.at[_slot(*peer)] if per_peer else src_ref, dst_ref=land_ref.at[mine],
                send_sem=send_sems.at[kk], recv_sem=recv_sems.at[kk], device_id=peer, device_id_type=MESH).start()
        token[...] = jnp.zeros_like(token)

    land = lax.empty((N_DEV,) + tuple(blk), src.dtype)
    *handles, token = pl.pallas_call(
        body, name=name,
        out_shape=(pltpu.SemaphoreType.DMA((N_DEV - 1,)), pltpu.SemaphoreType.DMA((N_DEV - 1,)),
                   pltpu.HBM(src.shape, src.dtype), pltpu.HBM(land.shape, land.dtype),
                   jax.ShapeDtypeStruct((8, 128), F32)),
        in_specs=(_HBM, _HBM, _ANY), out_specs=(_SEM, _SEM, _HBM, _HBM, pl.BlockSpec(memory_space=pltpu.VMEM)),
        input_output_aliases={0: 2, 1: 3},
        compiler_params=pltpu.CompilerParams(has_side_effects=_EFFECT),
    )(pltpu.with_memory_space_constraint(src, pltpu.HBM), pltpu.with_memory_space_constraint(land, pltpu.HBM), dep)
    return handles, token


def _send_wait(handles, after, name):
    send_sems, recv_sems, src_thru, land_thru = handles

    def body(src_ref, land_ref, send_sems, recv_sems, after_ref, src_dead, got_ref):
        me = _place()
        for kk in range(N_DEV - 1):
            cp = pltpu.make_async_remote_copy(
                src_ref=land_ref.at[0], dst_ref=land_ref.at[0], send_sem=send_sems.at[kk], recv_sem=recv_sems.at[kk],
                device_id=me, device_id_type=MESH)
            cp.wait_send()
            cp.wait_recv()

    return pl.pallas_call(
        body, name=name,
        out_shape=(pltpu.HBM(src_thru.shape, src_thru.dtype), pltpu.HBM(land_thru.shape, land_thru.dtype)),
        in_specs=(_HBM, _HBM, _SEM, _SEM, pl.BlockSpec(memory_space=pl.ANY)), out_specs=(_HBM, _HBM),
        input_output_aliases={0: 0, 1: 1},
        compiler_params=pltpu.CompilerParams(has_side_effects=_EFFECT),
    )(src_thru, land_thru, send_sems, recv_sems, after)


def _sum_slots(land, name):
    _, R, C = land.shape
    tm = R if R <= 512 else 512

    def body(x_ref, o_ref):
        acc = x_ref[0].astype(F32)
        for j in range(1, N_DEV):
            acc = acc + x_ref[j].astype(F32)
        o_ref[...] = acc

    return pl.pallas_call(
        body, name=name, grid=(R // tm,),
        in_specs=[pl.BlockSpec((N_DEV, tm, C), lambda i: (0, i, 0))], out_specs=pl.BlockSpec((tm, C), lambda i: (i, 0)),
        out_shape=jax.ShapeDtypeStruct((R, C), F32), compiler_params=_cparams(("parallel",)),
    )(land)


def _adamw(w, g, m, v, name):
    R, C = w.shape
    tm = R if R <= 512 else 256

    def fn(w, g, m, v):
        m2 = ADAM_B1 * m + (1.0 - ADAM_B1) * g
        v2 = ADAM_B2 * v + (1.0 - ADAM_B2) * (g * g)
        m_hat = m2 / (1.0 - ADAM_B1 ** ADAM_STEP)
        v_hat = v2 / (1.0 - ADAM_B2 ** ADAM_STEP)
        delta = -ADAM_LR * (m_hat / (jnp.sqrt(v_hat) + ADAM_EPS) + ADAM_WD * w)
        return (delta, m2, v2), ()
    return _rowwise(fn, [w, g, m, v], [], [(C, F32)] * 3, [], tm=tm, name=name)


SMALL = ["norm_mix_pre", "norm_mix_post", "norm_mlp_pre", "norm_mlp_post", "norm_ple_post",
         "conv_b", "ssd_norm_g", "dt_bias", "a_log", "d_skip"]


def _pad_row(v, width=D):
    return jnp.pad(v, ((0, 0), (0, width - v.shape[1])))


def kernel(x, p, positions, norm_mix_pre, norm_mix_post, w_in, conv_w, conv_b, dt_bias, a_log, d_skip, ssd_norm_g, w_out, norm_mlp_pre, norm_mlp_post, w_up, w_down, w_ple_gate, w_ple_proj, norm_ple_post, loss_target, m_norm_mix_pre, m_norm_mix_post, m_w_in, m_conv_w, m_conv_b, m_dt_bias, m_a_log, m_d_skip, m_ssd_norm_g, m_w_out, m_norm_mlp_pre, m_norm_mlp_post, m_w_up, m_w_down, m_w_ple_gate, m_w_ple_proj, m_norm_ple_post, v_norm_mix_pre, v_norm_mix_post, v_w_in, v_conv_w, v_conv_b, v_dt_bias, v_a_log, v_d_skip, v_ssd_norm_g, v_w_out, v_norm_mlp_pre, v_norm_mlp_post, v_w_up, v_w_down, v_w_ple_gate, v_w_ple_proj, v_norm_ple_post):
    args = dict(locals())
    x2, p2, tgt = x[0], p[0, 0], loss_target[0]
    g1, g2, g3, g4, g5 = norm_mix_pre, norm_mix_post, norm_mlp_pre, norm_mlp_post, norm_ple_post

    me = _slot(*_place())
    pack_in = jnp.pad(w_in[0].T, ((0, W_IN_SHARD_PAD - W_IN_SHARD), (0, 0))).astype(BF16)
    pack_rest = jnp.concatenate([
        w_out[0],
        w_up[0].T,
        w_down[0],
        w_ple_gate[0],
        w_ple_proj[0].T.reshape(32, D),
    ], axis=0).astype(BF16)
    conv_pack = jnp.pad(conv_w[0], ((0, 4), (0, 32)))
    gin, gconv = _all_gather([pack_in, conv_pack], "gather_w_in")
    rest_handles, tok_rest = _send_start(pack_rest, False, "gather_rest_start", gconv)
    w_inT = gin[:, :W_IN_SHARD].reshape(IN_W, D)
    w_qkvzT = w_inT[:4 * AW]
    w_xbcdtT = jnp.pad(w_inT[4 * AW:], ((0, DT_PAD - HEADS), (0, 0)))
    conv_full = gconv[:, :CONV_K, :96].transpose(1, 0, 2).reshape(CONV_K, CONV_CH)

    inv_freq = ROPE_THETA ** (-jnp.arange(HD // 2, dtype=F32) * 2.0 / HD)
    ang = positions[0].astype(F32)[:, None] * inv_freq
    cos, sin = jnp.cos(ang), jnp.sin(ang)
    cos128 = jnp.concatenate([cos, cos, cos, cos], axis=1)
    sin128 = jnp.concatenate([-sin, sin, -sin, sin], axis=1)

    bias_w, alog_w, dsk_w = _pad_row(dt_bias, DT_PAD), _pad_row(a_log, DT_PAD), _pad_row(d_skip, DT_PAD)
    rms_pre = lambda a, r, g: a * r * g

    (u1,) = _rowwise(lambda a, g: ((a * _rstd(a) * g,), ()), [x2], [g1], [(D, BF16)], [], tm=512, name="norm_x")
    p2b = p2.astype(BF16)
    qkvz = _mm(u1, w_qkvzT, tb=True, tm=512, tn=1024, tk=1024, name="proj_qkvz", deps=[tok_rest])
    xbcdt = _mm(u1, w_xbcdtT, tb=True, tm=512, tn=896, tk=1024, name="proj_xbcdt")

    qkv = _rope_fwd(qkvz, cos128, sin128)
    qkv = [qkv[3 * i:3 * i + 3] for i in range(len(DILATIONS))]
    outs, lses = [], []
    for d, (qd, kd, vd) in zip(DILATIONS, qkv):
        o, l = _attn_fwd(qd, kd, vd, d)
        outs.append(o)
        lses.append(l)
    attn, lse, attn4, lse4, attn16, lse16 = _attn_merge(outs, lses)

    act = _conv_fwd(xbcdt, conv_full, conv_b)
    y_ssd, states = _ssd_fwd(act, xbcdt, bias_w, alog_w, dsk_w)

    def gated_fwd(y, z, a, gs):
        gi = y * (z * _sigmoid(z))
        return (jnp.concatenate([a, gi * _rstd(gi) * gs], axis=1),), ()
    (cat,) = _rowwise(gated_fwd, [y_ssd, (qkvz, AW, 3), attn], [ssd_norm_g], [(D, BF16)], [], tm=512, name="gated_norm")

    pack_back, grest = _send_wait(rest_handles, cat, "gather_rest_wait")
    grest = lax.dynamic_update_slice(grest, pack_back[None], (me, 0, 0))
    w_o = grest[:, 0:128].reshape(D, D)
    w_upT = grest[:, 128:640].reshape(DFF, D)
    w_dn = grest[:, 640:1152].reshape(DFF, D)
    w_gate = grest[:, 1152:1280].reshape(D, D)
    w_projT = grest[:, 1280:1312].reshape(D, PLE)

    mix = _mm(cat, w_o, tm=512, tn=1024, tk=1024, name="mix_out")

    def post1(xx, mm, ga, gb):
        h = xx + mm * _rstd(mm) * ga
        return (h, _rstd(h)), ()
    h1, r3 = _rowwise(post1, [x2, mix], [g2, g3], [(D, F32), (1, F32)], [], tm=512, name="post_mix")

    a_up, ff, u2, h2, h2b = _mlp_fwd(h1, r3, g3, w_upT, w_dn, g4)
    relu2 = lambda a: jnp.square(jnp.maximum(a.astype(F32), 0.0))

    gp = _mm(h2b, w_gate, tm=512, tn=1024, tk=1024, name="ple_gate")
    pp = _mm(p2b, w_projT, tb=True, tm=512, tn=1024, tk=256, name="ple_proj")

    def final(hh, gpre, ppv, tg, g):
        sg = _sigmoid(gpre)
        ple = ppv * sg
        r = _rstd(ple)
        n = ple * r
        h3 = hh + n * g
        e = h3 - tg
        dh3 = e * (1.0 / D)
        dple = _rms_bwd(n, r, g, dh3)
        return (dh3, dple * sg, dple * ppv * sg * (1.0 - sg)), (_colsum(dh3 * n), _colsum(0.5 * e * e * (1.0 / D)))
    dh3, dpp, dgp, dg5, loss_vec = _rowwise(final, [h2, gp, pp, tgt], [g5], [(D, F32), (D, BF16), (D, BF16)],
                                            [(1, D), (1, D)], tm=256, name="loss_ple_bwd")

    gw_projT = _mm(dpp, p2b, ta=True, tm=512, tn=256, tk=1024, out_dtypes=(BF16,), name="gw_ple_proj")
    gw_gate = _mm(h2b, dgp, ta=True, tm=512, tn=1024, tk=1024, out_dtypes=(BF16,), name="gw_ple_gate")
    rs_proj, tok_proj = _send_start(gw_projT.reshape(N_DEV, 32, D), True, "rs_start_w_proj", g1)
    rs_gate, tok_gate = _send_start(gw_gate.reshape(N_DEV, 128, D), True, "rs_start_w_gate", g1)
    dh2_g = _mm(dgp, w_gate, tb=True, tm=512, tn=1024, tk=1024, name="dx_ple_gate", deps=[tok_proj, tok_gate])

    def bwd_mlp_post(d3, dg_, f, g):
        dh2 = d3 + dg_
        r = _rstd(f)
        n = f * r
        return (dh2, _rms_bwd(n, r, g, dh2)), (_colsum(dh2 * n),)
    dh2, dff, dg4 = _rowwise(bwd_mlp_post, [dh3, dh2_g, ff], [g4], [(D, F32), (D, BF16)], [(1, D)], tm=256,
                             name="bwd_post_mlp")

    gw_dn = _mm(a_up, dff, ta=True, tm=1024, tn=1024, tk=1024, a_pre=relu2, out_dtypes=(BF16,), name="gw_mlp_down")
    rs_dn, tok_dn = _send_start(gw_dn.reshape(N_DEV, 512, D), True, "rs_start_w_down", g1)
    da_up, du2 = _mlp_dx(dff, a_up, w_upT, w_dn, tok_dn)
    gw_upT = _mm(da_up, u2, ta=True, tm=1024, tn=1024, tk=1024, out_dtypes=(BF16,), name="gw_mlp_up")
    rs_up, tok_up = _send_start(gw_upT.reshape(N_DEV, 512, D), True, "rs_start_w_up", g1)

    def bwd_mix_post(d2, du, hh, rr, mm, ga, gb):
        n3 = hh * rr
        dh1 = d2 + _rms_bwd(n3, rr, gb, du)
        r = _rstd(mm)
        n2 = mm * r
        return (dh1, _rms_bwd(n2, r, ga, dh1)), (_colsum(du * n3), _colsum(dh1 * n2))
    dh1, dmix, dg3, dg2 = _rowwise(bwd_mix_post, [dh2, du2, h1, r3, mix], [g2, g3], [(D, F32), (D, BF16)],
                                   [(1, D), (1, D)], tm=256, name="bwd_post_mix", deps=[tok_up])

    gw_o = _mm(cat, dmix, ta=True, tm=512, tn=1024, tk=1024, out_dtypes=(BF16,), name="gw_out")
    rs_o, tok_o = _send_start(gw_o.reshape(N_DEV, 128, D), True, "rs_start_w_out", g1)
    dcat = _mm(dmix, w_o, tb=True, tm=512, tn=1024, tk=1024, name="dx_out", deps=[tok_o])

    def gated_bwd(y, z, dyn, gs):
        sg = _sigmoid(z)
        sz = z * sg
        gi = y * sz
        r = _rstd(gi)
        n = gi * r
        dgi = _rms_bwd(n, r, gs, dyn)
        return (dgi * sz, dgi * y * (sg * (1.0 + z * (1.0 - sg)))), (_colsum(dyn * n),)
    dy_ssd, dz, dgs = _rowwise(gated_bwd, [y_ssd, (qkvz, AW, 3), (dcat, AW, 1)], [ssd_norm_g], [(AW, F32)] * 2, [(1, AW)],
                               tm=512, name="bwd_gated_norm")

    dact, ddtw, ssd_par = _ssd_bwd(act, xbcdt, bias_w, alog_w, dsk_w, states, dy_ssd)
    dxbcdt, conv_par = _conv_bwd(xbcdt, dact, ddtw, conv_full, conv_b)

    dattn4, dattn16 = _dilate_cols(dcat, 0)
    qkv_grads = [_attn_bwd(*qkv[0], dcat, attn, lse, 1),
                 _attn_bwd(*qkv[1], dattn4, attn4, lse4, 4),
                 _attn_bwd(*qkv[2], dattn16, attn16, lse16, 16)]
    dqkvz = _rope_bwd(qkv_grads, dz, cos128, sin128)

    du1a = _mm(dqkvz, w_qkvzT, tm=512, tn=1024, tk=1024, name="dx_qkvz")
    du1b = _mm(dxbcdt, w_xbcdtT, tm=512, tn=1024, tk=896, name="dx_xbcdt")
    gw_qkvzT = _mm(dqkvz, u1, ta=True, tm=1024, tn=1024, tk=1024, out_dtypes=(BF16,), name="gw_qkvz")
    gw_xbcdtT = _mm(dxbcdt, u1, ta=True, tm=896, tn=1024, tk=1024, out_dtypes=(BF16,), name="gw_xbcdt")

    def bwd_in(d1, ua, ub, xx, g):
        rr = _rstd(xx)
        n = xx * rr
        du = ua + ub
        return (d1 + _rms_bwd(n, rr, g, du),), (_colsum(du * n),)
    grad_x, dg1 = _rowwise(bwd_in, [dh1, du1a, du1b, x2], [g1], [(D, F32)], [(1, D)], tm=256, name="bwd_pre_mix")

    gw_inT = jnp.concatenate([gw_qkvzT, gw_xbcdtT], axis=0)[:IN_W]
    gw_inT = jnp.pad(gw_inT.reshape(N_DEV, W_IN_SHARD, D), ((0, 0), (0, W_IN_SHARD_PAD - W_IN_SHARD), (0, 0)))
    rs_in, tok_in = _send_start(gw_inT, True, "rs_start_w_in", g1)

    def scatter_finish(handles, nm, after):
        part, land = _send_wait(handles, after, "rs_wait_" + nm)
        own = lax.dynamic_slice(part, (me, 0, 0), (1,) + part.shape[1:])
        return _sum_slots(lax.dynamic_update_slice(land, own, (me, 0, 0)), "rs_sum_" + nm)
    g_out = scatter_finish(rs_o, "w_out", tok_in)
    g_upT = scatter_finish(rs_up, "w_up", tok_in)
    g_dn = scatter_finish(rs_dn, "w_down", tok_in)
    g_gate = scatter_finish(rs_gate, "w_gate", tok_in)
    g_projT = scatter_finish(rs_proj, "w_proj", tok_in)

    small = jnp.concatenate([
        dg1, dg2, dg3, dg4, dg5,
        _pad_row(conv_par[4:5]), _pad_row(dgs), _pad_row(ssd_par[0:1]), _pad_row(ssd_par[1:2]), _pad_row(ssd_par[2:3]),
        _pad_row(conv_par[0:4]), loss_vec, jnp.zeros((1, D), F32),
    ], axis=0)
    small = _all_reduce_small(small, "reduce_small")
    loss = jnp.sum(small[14])
    me = lax.axis_index("x") * 4 + lax.axis_index("y") * 2 + lax.axis_index("c")
    g_conv_w = lax.dynamic_slice(small[10:14, :CONV_CH], (0, me * 96), (CONV_K, 96))

    grads = {
        "w_out": g_out[None], "w_up": g_upT.T[None], "w_down": g_dn[None],
        "w_ple_gate": g_gate[None], "w_ple_proj": g_projT.reshape(128, PLE).T[None], "conv_w": g_conv_w[None],
        "norm_mix_pre": small[0:1], "norm_mix_post": small[1:2], "norm_mlp_pre": small[2:3], "norm_mlp_post": small[3:4],
        "norm_ple_post": small[4:5], "conv_b": small[5:6, :CONV_CH], "ssd_norm_g": small[6:7, :AW],
        "dt_bias": small[7:8, :HEADS], "a_log": small[8:9, :HEADS], "d_skip": small[9:10, :HEADS],
    }
    delta, new_m, new_v = {}, {}, {}
    for nme in ["w_out", "w_up", "w_down", "w_ple_gate", "w_ple_proj", "w_in"]:
        if nme == "w_in":
            g_inT = scatter_finish(rs_in, "w_in", delta["w_down"])
            grads["w_in"] = g_inT[:W_IN_SHARD].T[None]
        dl, mm_, vv_ = _adamw(args[nme][0], grads[nme][0], args["m_" + nme][0], args["v_" + nme][0], "adamw_" + nme)
        delta[nme], new_m[nme], new_v[nme] = dl[None], mm_[None], vv_[None]

    def pack_small(prefix):
        rows = [_pad_row(args[prefix + nme]) for nme in SMALL]
        rows.append(_pad_row(args[prefix + "conv_w"][0]))
        rows.append(jnp.zeros((2, D), F32))
        return jnp.concatenate(rows, axis=0)
    g_small = jnp.concatenate([small[0:10], _pad_row(g_conv_w), jnp.zeros((2, D), F32)], axis=0)
    dl, mm_, vv_ = _adamw(pack_small(""), g_small, pack_small("m_"), pack_small("v_"), "adamw_small")
    for i, nme in enumerate(SMALL):
        wdt = args[nme].shape[1]
        delta[nme], new_m[nme], new_v[nme] = dl[i:i + 1, :wdt], mm_[i:i + 1, :wdt], vv_[i:i + 1, :wdt]
    delta["conv_w"], new_m["conv_w"], new_v["conv_w"] = dl[None, 10:14, :96], mm_[None, 10:14, :96], vv_[None, 10:14, :96]

    order = ["norm_mix_pre", "norm_mix_post", "w_in", "conv_w", "conv_b", "dt_bias", "a_log", "d_skip", "ssd_norm_g",
             "w_out", "norm_mlp_pre", "norm_mlp_post", "w_up", "w_down", "w_ple_gate", "w_ple_proj", "norm_ple_post"]
    return (loss, grad_x[None], *[grads[n] for n in order], *[delta[n] for n in order],
            *[new_m[n] for n in order], *[new_v[n] for n in order])
```

```python
import functools
import math

import jax
import jax.numpy as jnp
from jax import lax
from jax.experimental import pallas as pl
from jax.experimental.pallas import tpu as pltpu

F32 = jnp.float32
BF16 = jnp.bfloat16
MESH = pl.DeviceIdType.MESH
HIGHEST = lax.Precision.HIGHEST

N_DEV = 8
T = 4096
D = 1024
HEADS = 8
HD = 64
AW = 512
NS = 128
CONV_K = 4
CONV_CH = 768
CHUNK = 128
DFF = 4096
PLE = 256
EPS = 1e-6
ROPE_THETA = 10000.0
DILATIONS = (1, 4, 16)
QBLK = 128
NEG = -1e30
IN_W = 2824
W_IN_SHARD = 353
W_IN_SHARD_PAD = 384
DT_PAD = 128

ADAM_LR, ADAM_B1, ADAM_B2, ADAM_EPS, ADAM_WD, ADAM_STEP = 0.001, 0.9, 0.999, 1e-08, 0.01, 10

VMEM_LIMIT = 56 * 1024 * 1024


_ANY = pl.BlockSpec(memory_space=pl.ANY)


def _cparams(sem=None):
    return pltpu.CompilerParams(dimension_semantics=sem, vmem_limit_bytes=VMEM_LIMIT)


def _dot(a, b, ca, cb, precision=None):
    return lax.dot_general(a, b, (((ca,), (cb,)), ((), ())), preferred_element_type=F32, precision=precision)


def _nn(a, b):
    return _dot(a, b, 1, 0)


def _nt(a, b):
    return _dot(a, b, 1, 1)


def _tn(a, b):
    return _dot(a, b, 0, 0)


def _sigmoid(x):
    return 1.0 / (1.0 + jnp.exp(-x))


def _softplus(x):
    return jnp.maximum(x, 0.0) + jnp.log(1.0 + jnp.exp(-jnp.abs(x)))


def _mm(a, b, *, ta=False, tb=False, tm, tn, tk, name,
        a_pre=None, a_rows=(), a_cols=(), b_pre=None, b_rows=(), b_cols=(),
        epi=None, epi_tiles=(), out_dtypes=(F32,), deps=()):
    if ta:
        K, M = a.shape
    else:
        M, K = a.shape
    if tb:
        N, K2 = b.shape
    else:
        K2, N = b.shape
    assert K == K2 and M % tm == 0 and N % tn == 0 and K % tk == 0, (name, a.shape, b.shape)
    nk = K // tk
    if ta:
        a_spec = pl.BlockSpec((tk, tm), lambda i, j, k: (k, i))
        a_row_specs = [pl.BlockSpec((tk, 1), lambda i, j, k: (k, 0)) for _ in a_rows]
        a_col_specs = [pl.BlockSpec((1, tm), lambda i, j, k: (0, i)) for _ in a_cols]
    else:
        a_spec = pl.BlockSpec((tm, tk), lambda i, j, k: (i, k))
        a_row_specs = [pl.BlockSpec((tm, 1), lambda i, j, k: (i, 0)) for _ in a_rows]
        a_col_specs = [pl.BlockSpec((1, tk), lambda i, j, k: (0, k)) for _ in a_cols]
    if tb:
        b_spec = pl.BlockSpec((tn, tk), lambda i, j, k: (j, k))
        b_row_specs = [pl.BlockSpec((tn, 1), lambda i, j, k: (j, 0)) for _ in b_rows]
        b_col_specs = [pl.BlockSpec((1, tk), lambda i, j, k: (0, k)) for _ in b_cols]
    else:
        b_spec = pl.BlockSpec((tk, tn), lambda i, j, k: (k, j))
        b_row_specs = [pl.BlockSpec((tk, 1), lambda i, j, k: (k, 0)) for _ in b_rows]
        b_col_specs = [pl.BlockSpec((1, tn), lambda i, j, k: (0, j)) for _ in b_cols]
    o_spec = pl.BlockSpec((tm, tn), lambda i, j, k: (i, j))
    na, nb, ne, no = len(a_rows) + len(a_cols), len(b_rows) + len(b_cols), len(epi_tiles), len(out_dtypes)

    def body(*refs):
        a_ref, b_ref = refs[0], refs[1]
        a_ex = refs[2:2 + na]
        b_ex = refs[2 + na:2 + na + nb]
        e_ex = refs[2 + na + nb:2 + na + nb + ne]
        first_out = 2 + na + nb + ne + len(deps)
        outs = refs[first_out:first_out + no]
        acc = refs[-1]
        k = pl.program_id(2)

        @pl.when(k == 0)
        def _():
            acc[...] = jnp.zeros_like(acc)

        at = a_ref[...]
        if a_pre is not None:
            at = a_pre(at, *[r[...] for r in a_ex])
        bt = b_ref[...]
        if b_pre is not None:
            bt = b_pre(bt, *[r[...] for r in b_ex])
        acc[...] += _dot(at.astype(BF16), bt.astype(BF16), 0 if ta else 1, 1 if tb else 0)

        @pl.when(k == nk - 1)
        def _():
            res = acc[...]
            vals = epi(res, *[r[...] for r in e_ex]) if epi is not None else (res,)
            for o_ref, val in zip(outs, vals):
                o_ref[...] = val.astype(o_ref.dtype)

    outs = pl.pallas_call(
        body, name=name,
        grid=(M // tm, N // tn, nk),
        in_specs=([a_spec, b_spec] + a_row_specs + a_col_specs + b_row_specs + b_col_specs + [o_spec] * ne
                  + [_ANY] * len(deps)),
        out_specs=[o_spec] * no,
        out_shape=[jax.ShapeDtypeStruct((M, N), dt) for dt in out_dtypes],
        scratch_shapes=[pltpu.VMEM((tm, tn), F32)],
        compiler_params=_cparams(("parallel", "parallel", "arbitrary")),
    )(a, b, *a_rows, *a_cols, *b_rows, *b_cols, *epi_tiles, *deps)
    return outs[0] if no == 1 else outs


MLP_TM = 1024
MLP_TC = 512


def _mlp_fwd(h, r, g, w_upT, w_dn, g_post):
    nc = DFF // MLP_TC

    def body(h_ref, r_ref, g_ref, wu_ref, wd_ref, gp_ref, a_ref, ff_ref, u_ref, ho_ref, hob_ref, acc, u_scr):
        c = pl.program_id(1)

        @pl.when(c == 0)
        def _():
            u = (h_ref[...] * r_ref[...] * g_ref[...]).astype(BF16)
            u_scr[...] = u
            u_ref[...] = u
            acc[...] = jnp.zeros_like(acc)
        a = _nt(u_scr[...], wu_ref[...])
        a_ref[...] = a.astype(BF16)
        acc[...] += _nn(jnp.square(jnp.maximum(a, 0.0)).astype(BF16), wd_ref[...])

        @pl.when(c == nc - 1)
        def _():
            f = acc[...]
            ff_ref[...] = f
            ho = h_ref[...] + f * _rstd(f) * gp_ref[...]
            ho_ref[...] = ho
            hob_ref[...] = ho.astype(BF16)

    row = pl.BlockSpec((MLP_TM, D), lambda i, c: (i, 0))
    wsp = pl.BlockSpec((MLP_TC, D), lambda i, c: (c, 0))
    vec = pl.BlockSpec((1, D), lambda i, c: (0, 0))
    return pl.pallas_call(
        body, name="mlp_fwd", grid=(T // MLP_TM, nc),
        in_specs=[row, pl.BlockSpec((MLP_TM, 1), lambda i, c: (i, 0)), vec, wsp, wsp, vec],
        out_specs=[pl.BlockSpec((MLP_TM, MLP_TC), lambda i, c: (i, c)), row, row, row, row],
        out_shape=[jax.ShapeDtypeStruct((T, DFF), BF16), jax.ShapeDtypeStruct((T, D), F32), jax.ShapeDtypeStruct((T, D), BF16),
                   jax.ShapeDtypeStruct((T, D), F32), jax.ShapeDtypeStruct((T, D), BF16)],
        scratch_shapes=[pltpu.VMEM((MLP_TM, D), F32), pltpu.VMEM((MLP_TM, D), BF16)],
        compiler_params=_cparams(("parallel", "arbitrary")),
    )(h, r, g, w_upT, w_dn, g_post)


def _mlp_dx(dff, a, w_upT, w_dn, dep):
    nc = DFF // MLP_TC

    def body(d_ref, a_ref, wu_ref, wd_ref, dep_ref, da_ref, du_ref, acc, d_scr):
        c = pl.program_id(1)

        @pl.when(c == 0)
        def _():
            d_scr[...] = d_ref[...].astype(BF16)
            acc[...] = jnp.zeros_like(acc)
        da = (_nt(d_scr[...], wd_ref[...]) * (2.0 * jnp.maximum(a_ref[...].astype(F32), 0.0))).astype(BF16)
        da_ref[...] = da
        acc[...] += _nn(da, wu_ref[...])

        @pl.when(c == nc - 1)
        def _():
            du_ref[...] = acc[...]

    row = pl.BlockSpec((MLP_TM, D), lambda i, c: (i, 0))
    wsp = pl.BlockSpec((MLP_TC, D), lambda i, c: (c, 0))
    chunk = pl.BlockSpec((MLP_TM, MLP_TC), lambda i, c: (i, c))
    return pl.pallas_call(
        body, name="mlp_dx", grid=(T // MLP_TM, nc),
        in_specs=[row, chunk, wsp, wsp, _ANY], out_specs=[chunk, row],
        out_shape=[jax.ShapeDtypeStruct((T, DFF), BF16), jax.ShapeDtypeStruct((T, D), F32)],
        scratch_shapes=[pltpu.VMEM((MLP_TM, D), F32), pltpu.VMEM((MLP_TM, D), BF16)],
        compiler_params=_cparams(("parallel", "arbitrary")),
    )(dff, a, w_upT, w_dn, dep)


def _rowwise(fn, rows, vecs, out_rows, out_sums, *, tm, name, deps=()):
    specs, arrs = [], []
    R = None
    for r in rows:
        if isinstance(r, tuple):
            arr, width, cb = r
            specs.append(pl.BlockSpec((tm, width), lambda i, cb=cb: (i, cb)))
        else:
            arr = r
            specs.append(pl.BlockSpec((tm, arr.shape[1]), lambda i: (i, 0)))
        R = arr.shape[0] if R is None else R
        assert arr.shape[0] == R, name
        arrs.append(arr)
    assert R % tm == 0, name
    for v in vecs:
        specs.append(pl.BlockSpec(v.shape, lambda i: (0, 0)))
        arrs.append(v)
    nr, nv, no, ns = len(rows), len(vecs), len(out_rows), len(out_sums)
    out_specs = [pl.BlockSpec((tm, w), lambda i: (i, 0)) for w, _ in out_rows]
    out_specs += [pl.BlockSpec(s, lambda i: (0, 0)) for s in out_sums]
    out_shape = [jax.ShapeDtypeStruct((R, w), dt) for w, dt in out_rows]
    out_shape += [jax.ShapeDtypeStruct(s, F32) for s in out_sums]

    nd = len(deps)

    def body(*refs):
        ins = [r[...] for r in refs[:nr + nv]]
        o_refs = refs[nr + nv + nd:nr + nv + nd + no]
        s_refs = refs[nr + nv + nd + no:]
        o_vals, s_vals = fn(*ins)
        for ref, val in zip(o_refs, o_vals):
            ref[...] = val.astype(ref.dtype)
        if ns:
            @pl.when(pl.program_id(0) == 0)
            def _():
                for ref in s_refs:
                    ref[...] = jnp.zeros_like(ref)
            for ref, val in zip(s_refs, s_vals):
                ref[...] += val

    outs = pl.pallas_call(
        body, name=name, grid=(R // tm,), in_specs=specs + [_ANY] * nd, out_specs=out_specs, out_shape=out_shape,
        compiler_params=_cparams(("arbitrary",) if ns else ("parallel",)),
    )(*arrs, *deps)
    return outs


def _colsum(x):
    return jnp.sum(x, axis=0, keepdims=True)


def _rstd(x):
    return lax.rsqrt(jnp.mean(x * x, axis=-1, keepdims=True) + EPS)


def _rms_bwd(xn, r, g, dy):
    dn = dy * g
    return r * (dn - xn * jnp.mean(dn * xn, axis=-1, keepdims=True))


def _partner(t):
    lane = lax.broadcasted_iota(jnp.int32, t.shape, 1)
    up = pltpu.roll(t, 96, 1)
    down = pltpu.roll(t, 32, 1)
    return jnp.where((lane % 64) < 32, up, down)


SLABS = AW // 128


def _rows(r, n, d):
    return pl.ds(r, n, stride=d) if d > 1 else pl.ds(0, n)


def _undilate(src_ref, dst, d, tm):
    for r in range(d):
        for j in range(SLABS):
            dst[j][_rows(r, tm // d, d), :] = src_ref[:, pl.ds(r * AW + j * 128, 128)].astype(dst[j].dtype)


def _dilate(dst_ref, src, d, tm):
    for r in range(d):
        for j in range(SLABS):
            dst_ref[:, pl.ds(r * AW + j * 128, 128)] = src[j][_rows(r, tm // d, d), :].astype(dst_ref.dtype)


def _slab_scratch(n, tm):
    return [pltpu.VMEM((tm, 128), F32)] * (SLABS * n)


def _slab_groups(flat):
    return [flat[SLABS * i:SLABS * (i + 1)] for i in range(len(flat) // SLABS)]


def _slab_specs(tm, first):
    return [pl.BlockSpec((tm, 128), lambda i, j=j: (i, first + j)) for j in range(SLABS)]


def _dil_spec(tm, d):
    return pl.BlockSpec((tm // d, d * AW), lambda i: (i, 0))


ROPE_TM = 512


def _rope_fwd(qkvz, cos128, sin128):
    tm = ROPE_TM

    def body(*refs):
        q_refs, k_refs, v_refs = refs[0:4], refs[4:8], refs[8:12]
        c_ref, s_ref = refs[12], refs[13]
        outs = refs[14:]
        for di, d in enumerate(DILATIONS):
            oq, ok, ov = outs[3 * di:3 * di + 3]
            for r in range(d):
                rows = _rows(r, tm // d, d)
                c, s = c_ref[rows, :], s_ref[rows, :]
                for j in range(SLABS):
                    cols = pl.ds(r * AW + j * 128, 128)
                    q, k = q_refs[j][rows, :], k_refs[j][rows, :]
                    oq[:, cols] = ((q * c + _partner(q) * s) * (HD ** -0.5)).astype(BF16)
                    ok[:, cols] = (k * c + _partner(k) * s).astype(BF16)
                    ov[:, cols] = v_refs[j][rows, :].astype(BF16)

    tab = pl.BlockSpec((tm, 128), lambda i: (i, 0))
    out_specs, out_shape = [], []
    for d in DILATIONS:
        out_specs += [_dil_spec(tm, d)] * 3
        out_shape += [jax.ShapeDtypeStruct((T // d, d * AW), BF16)] * 3
    return pl.pallas_call(
        body, name="rope_fwd", grid=(T // tm,),
        in_specs=_slab_specs(tm, 0) + _slab_specs(tm, 4) + _slab_specs(tm, 8) + [tab, tab],
        out_specs=out_specs, out_shape=out_shape, compiler_params=_cparams(("parallel",)),
    )(*([qkvz] * 12), cos128, sin128)


def _rope_bwd(grads, dz, cos128, sin128):
    tm = 256

    def body(*refs):
        g_refs = refs[0:9]
        dz_ref, c_ref, s_ref, o_ref = refs[9], refs[10], refs[11], refs[12]
        scr = _slab_groups(refs[13:])
        for di, d in enumerate(DILATIONS[1:]):
            for t in range(3):
                _undilate(g_refs[3 * (di + 1) + t], scr[3 * di + t], d, tm)
        c, s = c_ref[...], s_ref[...]
        for j in range(SLABS):
            cols = pl.ds(j * 128, 128)
            tot = [g_refs[t][:, cols] + scr[t][j][...] + scr[3 + t][j][...] for t in range(3)]
            dqr = tot[0] * (HD ** -0.5)
            o_ref[:, pl.ds(j * 128, 128)] = (dqr * c + _partner(dqr * s)).astype(BF16)
            o_ref[:, pl.ds(AW + j * 128, 128)] = (tot[1] * c + _partner(tot[1] * s)).astype(BF16)
            o_ref[:, pl.ds(2 * AW + j * 128, 128)] = tot[2].astype(BF16)
        o_ref[:, pl.ds(3 * AW, AW)] = dz_ref[...].astype(BF16)

    tab = pl.BlockSpec((tm, 128), lambda i: (i, 0))
    in_specs, args = [], []
    for d, g in zip(DILATIONS, grads):
        in_specs += [_dil_spec(tm, d)] * 3
        args += list(g)
    return pl.pallas_call(
        body, name="rope_bwd", grid=(T // tm,),
        in_specs=in_specs + [pl.BlockSpec((tm, AW), lambda i: (i, 0)), tab, tab],
        out_specs=pl.BlockSpec((tm, 4 * AW), lambda i: (i, 0)),
        out_shape=jax.ShapeDtypeStruct((T, 4 * AW), BF16),
        scratch_shapes=_slab_scratch(6, tm),
        compiler_params=_cparams(("parallel",)),
    )(*args, dz, cos128, sin128)


def _dilate_cols(x, first):
    tm = ROPE_TM

    def body(x0, x1, x2, x3, o4, o16):
        xs = (x0, x1, x2, x3)
        for o_ref, d in ((o4, 4), (o16, 16)):
            for r in range(d):
                for j in range(SLABS):
                    o_ref[:, pl.ds(r * AW + j * 128, 128)] = xs[j][_rows(r, tm // d, d), :]

    return pl.pallas_call(
        body, name="dilate_cols", grid=(T // tm,),
        in_specs=_slab_specs(tm, first), out_specs=[_dil_spec(tm, 4), _dil_spec(tm, 16)],
        out_shape=[jax.ShapeDtypeStruct((T // 4, 4 * AW), F32), jax.ShapeDtypeStruct((T // 16, 16 * AW), F32)],
        compiler_params=_cparams(("parallel",)),
    )(x, x, x, x)


def _band_masks():
    qi = lax.broadcasted_iota(jnp.int32, (QBLK, QBLK), 0)
    kj = lax.broadcasted_iota(jnp.int32, (QBLK, QBLK), 1)
    return kj >= qi, kj <= qi


def _attn_fwd(q, k, v, d):
    L = q.shape[0]
    nb = L // QBLK

    def body(q_ref, kp_ref, kc_ref, vp_ref, vc_ref, o_ref, l_ref):
        n = pl.program_id(1)
        mask_p, mask_c = _band_masks()
        bias = jnp.concatenate([jnp.where(mask_p, 0.0, NEG) + jnp.where(n > 0, 0.0, NEG),
                                jnp.where(mask_c, 0.0, NEG)], axis=1)
        s = []
        for h in range(HEADS):
            sl = pl.ds(HD * h, HD)
            qh = q_ref[:, sl]
            s.append(jnp.concatenate([_nt(qh, kp_ref[:, sl]), _nt(qh, kc_ref[:, sl])], axis=1))
        s = jnp.stack(s) + bias
        m = jnp.max(s, axis=2, keepdims=True)
        e = jnp.exp(s - m)
        den = jnp.sum(e, axis=2, keepdims=True)
        p = e.astype(BF16)
        inv = 1.0 / den
        lse = m + jnp.log(den)
        for h in range(HEADS):
            sl = pl.ds(HD * h, HD)
            o_ref[:, sl] = (_nn(p[h, :, :QBLK], vp_ref[:, sl]) + _nn(p[h, :, QBLK:], vc_ref[:, sl])) * inv[h]
            l_ref[:, sl] = jnp.broadcast_to(lse[h], (QBLK, HD))

    cur = pl.BlockSpec((QBLK, AW), lambda r, n: (n, r))
    prev = pl.BlockSpec((QBLK, AW), lambda r, n: (jnp.maximum(n - 1, 0), r))
    return pl.pallas_call(
        body, name=f"attn_fwd_d{d}", grid=(d, nb),
        in_specs=[cur, prev, cur, prev, cur], out_specs=[cur, cur],
        out_shape=[jax.ShapeDtypeStruct((L, d * AW), F32)] * 2,
        compiler_params=_cparams(("parallel", "parallel")),
    )(q, k, k, v, v)


def _attn_bwd(q, k, v, do, at, lse, d):
    L = q.shape[0]
    nb = L // QBLK

    def body(q0_ref, q1_ref, kp_ref, kc_ref, vp_ref, vc_ref, do0_ref, do1_ref, at0_ref, at1_ref,
             l0_ref, l1_ref, dq_ref, dk_ref, dv_ref):
        n = pl.program_id(1)
        mask_p, mask_c = _band_masks()
        prev_bias = jnp.where(mask_p, 0.0, NEG)
        bias = jnp.concatenate([prev_bias + jnp.where(n > 0, 0.0, NEG), jnp.where(mask_c, 0.0, NEG),
                                prev_bias + jnp.where(n < nb - 1, 0.0, NEG)], axis=1)
        s, dp, ls, dl, ops = [], [], [], [], []
        for h in range(HEADS):
            sl = pl.ds(HD * h, HD)
            one = pl.ds(HD * h, 1)
            q0, q1 = q0_ref[:, sl], q1_ref[:, sl]
            kp, kc, vp, vc = kp_ref[:, sl], kc_ref[:, sl], vp_ref[:, sl], vc_ref[:, sl]
            do0, do1 = do0_ref[:, sl], do1_ref[:, sl]
            do0b, do1b = do0.astype(BF16), do1.astype(BF16)
            s.append(jnp.concatenate([_nt(q0, kp), _nt(q0, kc), _nt(q1, kc)], axis=1))
            dp.append(jnp.concatenate([_nt(do0b, vp), _nt(do0b, vc), _nt(do1b, vc)], axis=1))
            dl0 = jnp.sum(do0 * at0_ref[:, sl], axis=1, keepdims=True)
            dl1 = jnp.sum(do1 * at1_ref[:, sl], axis=1, keepdims=True)
            dl.append(jnp.concatenate([jnp.broadcast_to(dl0, (QBLK, 2 * QBLK)), jnp.broadcast_to(dl1, (QBLK, QBLK))], axis=1))
            ls.append(jnp.concatenate([jnp.broadcast_to(l0_ref[:, one], (QBLK, 2 * QBLK)),
                                       jnp.broadcast_to(l1_ref[:, one], (QBLK, QBLK))], axis=1))
            ops.append((q0, q1, kp, kc, do0b, do1b))
        p = jnp.exp(jnp.stack(s) + bias - jnp.stack(ls))
        ds = (p * (jnp.stack(dp) - jnp.stack(dl))).astype(BF16)
        p = p.astype(BF16)
        for h in range(HEADS):
            sl = pl.ds(HD * h, HD)
            q0, q1, kp, kc, do0b, do1b = ops[h]
            dq_ref[:, sl] = _nn(ds[h, :, :QBLK], kp) + _nn(ds[h, :, QBLK:2 * QBLK], kc)
            dv_ref[:, sl] = _tn(p[h, :, QBLK:2 * QBLK], do0b) + _tn(p[h, :, 2 * QBLK:], do1b)
            dk_ref[:, sl] = _tn(ds[h, :, QBLK:2 * QBLK], q0) + _tn(ds[h, :, 2 * QBLK:], q1)

    cur = pl.BlockSpec((QBLK, AW), lambda r, n: (n, r))
    prev = pl.BlockSpec((QBLK, AW), lambda r, n: (jnp.maximum(n - 1, 0), r))
    nxt = pl.BlockSpec((QBLK, AW), lambda r, n: (jnp.minimum(n + 1, nb - 1), r))
    return pl.pallas_call(
        body, name=f"attn_bwd_d{d}", grid=(d, nb),
        in_specs=[cur, nxt, prev, cur, prev, cur, cur, nxt, cur, nxt, cur, nxt], out_specs=[cur, cur, cur],
        out_shape=[jax.ShapeDtypeStruct((L, d * AW), F32)] * 3,
        compiler_params=_cparams(("parallel", "parallel")),
    )(q, q, k, k, v, v, do, do, at, at, lse, lse)


def _attn_merge(outs, lses):
    tm = ROPE_TM

    def body(o1, o4, o16, l1, l4, l16, at_ref, ls_ref, at4, ls4, at16, ls16, *flat):
        so4, so16, sl4, sl16, sa, sl = _slab_groups(flat)
        _undilate(o4, so4, 4, tm)
        _undilate(o16, so16, 16, tm)
        _undilate(l4, sl4, 4, tm)
        _undilate(l16, sl16, 16, tm)
        for j in range(SLABS):
            cols = pl.ds(j * 128, 128)
            a, b, c = l1[:, cols], sl4[j][...], sl16[j][...]
            m = jnp.maximum(jnp.maximum(a, b), c)
            e1, e2, e3 = jnp.exp(a - m), jnp.exp(b - m), jnp.exp(c - m)
            s = e1 + e2 + e3
            inv = 1.0 / s
            attn = (e1 * inv) * o1[:, cols] + (e2 * inv) * so4[j][...] + (e3 * inv) * so16[j][...]
            lse = m + jnp.log(s)
            at_ref[:, cols] = attn
            ls_ref[:, cols] = lse
            sa[j][...] = attn
            sl[j][...] = lse
        _dilate(at4, sa, 4, tm)
        _dilate(at16, sa, 16, tm)
        _dilate(ls4, sl, 4, tm)
        _dilate(ls16, sl, 16, tm)

    specs = [_dil_spec(tm, d) for d in DILATIONS]
    tok = specs[0]
    return pl.pallas_call(
        body, name="attn_merge", grid=(T // tm,),
        in_specs=specs + specs, out_specs=[tok, tok, specs[1], specs[1], specs[2], specs[2]],
        out_shape=[jax.ShapeDtypeStruct((T, AW), F32)] * 2 + [jax.ShapeDtypeStruct((T // 4, 4 * AW), F32)] * 2
        + [jax.ShapeDtypeStruct((T // 16, 16 * AW), F32)] * 2,
        scratch_shapes=_slab_scratch(6, tm),
        compiler_params=_cparams(("parallel",)),
    )(*outs, *lses)


CONV_TM = 512
HALO = 8


def _conv_pre(ext, w, b):
    y = b + w[3] * ext
    for kk in range(1, CONV_K):
        y = y + w[3 - kk] * pltpu.roll(ext, kk, 0)
    return y


def _rows_to_block(rows, n, width):
    ri = lax.broadcasted_iota(jnp.int32, (n, width), 0)
    out = jnp.zeros((n, width), F32)
    for j, r in enumerate(rows):
        out = out + jnp.where(ri == j, r, 0.0)
    return out


def _conv_fwd(xbc, w, b):
    nblk = T // CONV_TM

    def body(x_ref, h_ref, w_ref, b_ref, o_ref):
        i = pl.program_id(0)
        halo = jnp.where(i > 0, h_ref[...], 0.0)
        ext = jnp.concatenate([halo, x_ref[...]], axis=0)
        y = _conv_pre(ext, [w_ref[pl.ds(j, 1), :] for j in range(CONV_K)], b_ref[...])[HALO:]
        o_ref[...] = y * _sigmoid(y)

    return pl.pallas_call(
        body, name="conv_fwd", grid=(nblk,),
        in_specs=[pl.BlockSpec((CONV_TM, CONV_CH), lambda i: (i, 0)),
                  pl.BlockSpec((HALO, CONV_CH), lambda i: (jnp.maximum(i * (CONV_TM // HALO) - 1, 0), 0)),
                  pl.BlockSpec((CONV_K, CONV_CH), lambda i: (0, 0)),
                  pl.BlockSpec((1, CONV_CH), lambda i: (0, 0))],
        out_specs=pl.BlockSpec((CONV_TM, CONV_CH), lambda i: (i, 0)),
        out_shape=jax.ShapeDtypeStruct((T, CONV_CH), F32),
        compiler_params=_cparams(("parallel",)),
    )(xbc, xbc, w, b)


def _conv_bwd(xbc, dact, ddt, w, b):
    nblk = T // CONV_TM
    per = CONV_TM // HALO

    def body(x_ref, xb_ref, xa_ref, g_ref, ga_ref, ddt_ref, w_ref, b_ref, dx_ref, dw_ref):
        i = pl.program_id(0)
        wv = [w_ref[pl.ds(j, 1), :] for j in range(CONV_K)]
        before = jnp.where(i > 0, xb_ref[...], 0.0)
        last = i == nblk - 1
        after = jnp.where(last, 0.0, xa_ref[...])
        g_after = jnp.where(last, 0.0, ga_ref[...])
        ext = jnp.concatenate([before, x_ref[...], after], axis=0)
        y = _conv_pre(ext, wv, b_ref[...])[HALO:]
        sg = _sigmoid(y)
        dy = jnp.concatenate([g_ref[...], g_after], axis=0) * (sg * (1.0 + y * (1.0 - sg)))
        n = CONV_TM + HALO
        dx = wv[3] * dy
        for kk in range(1, CONV_K):
            dx = dx + wv[3 - kk] * pltpu.roll(dy, n - kk, 0)
        dx_ref[:, pl.ds(0, CONV_CH)] = dx[:CONV_TM].astype(BF16)
        dx_ref[:, pl.ds(CONV_CH, DT_PAD)] = ddt_ref[...].astype(BF16)
        dyc = dy[:CONV_TM]
        rows = [jnp.sum(dyc * (pltpu.roll(ext, 3 - j, 0) if j < 3 else ext)[HALO:HALO + CONV_TM], axis=0, keepdims=True)
                for j in range(CONV_K)]
        rows.append(jnp.sum(dyc, axis=0, keepdims=True))
        part = _rows_to_block(rows, 8, CONV_CH)

        @pl.when(i == 0)
        def _():
            dw_ref[...] = jnp.zeros_like(dw_ref)
        dw_ref[...] += part

    blk = pl.BlockSpec((CONV_TM, CONV_CH), lambda i: (i, 0))
    hb = pl.BlockSpec((HALO, CONV_CH), lambda i: (jnp.maximum(i * per - 1, 0), 0))
    ha = pl.BlockSpec((HALO, CONV_CH), lambda i: (jnp.minimum((i + 1) * per, T // HALO - 1), 0))
    return pl.pallas_call(
        body, name="conv_bwd", grid=(nblk,),
        in_specs=[blk, hb, ha, blk, ha, pl.BlockSpec((CONV_TM, DT_PAD), lambda i: (i, 0)),
                  pl.BlockSpec((CONV_K, CONV_CH), lambda i: (0, 0)), pl.BlockSpec((1, CONV_CH), lambda i: (0, 0))],
        out_specs=[pl.BlockSpec((CONV_TM, CONV_CH + DT_PAD), lambda i: (i, 0)), pl.BlockSpec((8, CONV_CH), lambda i: (0, 0))],
        out_shape=[jax.ShapeDtypeStruct((T, CONV_CH + DT_PAD), BF16), jax.ShapeDtypeStruct((8, CONV_CH), F32)],
        compiler_params=_cparams(("arbitrary",)),
    )(xbc, xbc, xbc, dact, dact, ddt, w, b)


def _pick(mat, h):
    lane = lax.broadcasted_iota(jnp.int32, mat.shape, 1)
    return jnp.sum(jnp.where(lane == h, mat, 0.0), axis=1, keepdims=True)


def _heads(fn):
    return jnp.stack([fn(h) for h in range(HEADS)])


def _ssd_prep(dt_ref, bias_ref, alog_ref, dsk_ref, b_ref, c_ref, xs_ref, state_ref, cst):
    li = lax.broadcasted_iota(jnp.int32, (CHUNK, CHUNK), 0)
    si = lax.broadcasted_iota(jnp.int32, (CHUNK, CHUNK), 1)
    tri = li >= si
    dtp = dt_ref[...] + bias_ref[...]
    dt = _softplus(dtp)
    A = -jnp.exp(alog_ref[...])
    a = dt * A
    cs = jnp.dot(tri.astype(F32), a, precision=HIGHEST, preferred_element_type=F32)
    cst[...] = cs.T
    Bm = b_ref[...].astype(BF16)
    Cm = c_ref[...].astype(BF16)
    cb = _nt(Cm, Bm)
    dskv = dsk_ref[...]
    cs_col = _heads(lambda h: _pick(cs, h))
    cs_row = _heads(lambda h: cst[pl.ds(h, 1), :])
    dt_col = _heads(lambda h: _pick(dt, h))
    dsk_col = _heads(lambda h: _pick(dskv, h))
    lam = jnp.exp(jnp.where(tri, cs_col - cs_row, NEG))
    x = _heads(lambda h: xs_ref[:, pl.ds(HD * h, HD)])
    xdt = x * dt_col
    prev = _heads(lambda h: state_ref[pl.ds(HD * h, HD), :])
    lane = lax.broadcasted_iota(jnp.int32, (1, 1, CHUNK), 2)
    cl = jnp.sum(jnp.where(lane == CHUNK - 1, cs_row, 0.0), axis=2, keepdims=True)
    f = jnp.exp(cl - cs_col)
    return dict(li=li, si=si, dtp=dtp, dt=dt, A=A, Bm=Bm, Cm=Cm, cb=cb, cs_col=cs_col, dt_col=dt_col, dsk_col=dsk_col,
                lam=lam, x=x, xdt=xdt, prev=prev, cl=cl, f=f)


def _ssd_fwd(act, xbcdt, bias, alog, dsk):
    nc = T // CHUNK

    def body(xs_ref, b_ref, c_ref, dt_ref, bias_ref, alog_ref, dsk_ref, y_ref, st_ref, state, cst):
        @pl.when(pl.program_id(0) == 0)
        def _():
            state[...] = jnp.zeros_like(state)
        st_ref[...] = state[...]
        s = _ssd_prep(dt_ref, bias_ref, alog_ref, dsk_ref, b_ref, c_ref, xs_ref, state, cst)
        Bm, Cm, prev = s["Bm"], s["Cm"], s["prev"]
        g = (s["cb"] * s["lam"]).astype(BF16)
        xdtb = s["xdt"].astype(BF16)
        prevb = prev.astype(BF16)
        y = _heads(lambda h: _nn(g[h], xdtb[h])) + _heads(lambda h: _nt(Cm, prevb[h])) * jnp.exp(s["cs_col"])
        y = y + s["dsk_col"] * s["x"]
        xf = (s["xdt"] * s["f"]).astype(BF16)
        new = prev * jnp.exp(s["cl"]) + _heads(lambda h: _tn(xf[h], Bm))
        for h in range(HEADS):
            y_ref[:, pl.ds(HD * h, HD)] = y[h]
            state[pl.ds(HD * h, HD), :] = new[h]

    vec = pl.BlockSpec((1, DT_PAD), lambda c: (0, 0))
    return pl.pallas_call(
        body, name="ssd_fwd", grid=(nc,),
        in_specs=[pl.BlockSpec((CHUNK, AW), lambda c: (c, 0)), pl.BlockSpec((CHUNK, NS), lambda c: (c, 4)),
                  pl.BlockSpec((CHUNK, NS), lambda c: (c, 5)), pl.BlockSpec((CHUNK, DT_PAD), lambda c: (c, 6)),
                  vec, vec, vec],
        out_specs=[pl.BlockSpec((CHUNK, AW), lambda c: (c, 0)), pl.BlockSpec((None, AW, NS), lambda c: (c, 0, 0))],
        out_shape=[jax.ShapeDtypeStruct((T, AW), F32), jax.ShapeDtypeStruct((nc, AW, NS), F32)],
        scratch_shapes=[pltpu.VMEM((AW, NS), F32), pltpu.VMEM((CHUNK, CHUNK), F32)],
        compiler_params=_cparams(("arbitrary",)),
    )(act, act, act, xbcdt, bias, alog, dsk)


def _ssd_bwd(act, xbcdt, bias, alog, dsk, states, dy):
    nc = T // CHUNK

    def body(xs_ref, b_ref, c_ref, dt_ref, bias_ref, alog_ref, dsk_ref, st_ref, dy_ref,
             dact_ref, ddt_ref, par_ref, dstate, cst):
        step = pl.program_id(0)

        @pl.when(step == 0)
        def _():
            dstate[...] = jnp.zeros_like(dstate)
            par_ref[...] = jnp.zeros_like(par_ref)
        s = _ssd_prep(dt_ref, bias_ref, alog_ref, dsk_ref, b_ref, c_ref, xs_ref, st_ref, cst)
        Bm, Cm, prev, lam, x, xdt, f, cl = s["Bm"], s["Cm"], s["prev"], s["lam"], s["x"], s["xdt"], s["f"], s["cl"]
        lane = lax.broadcasted_iota(jnp.int32, (1, DT_PAD), 1)
        row = lax.broadcasted_iota(jnp.int32, (1, CHUNK, 1), 1)
        g = s["cb"] * lam
        gb, xdtb, prevb = g.astype(BF16), xdt.astype(BF16), prev.astype(BF16)
        dy = _heads(lambda h: dy_ref[:, pl.ds(HD * h, HD)])
        dyb = dy.astype(BF16)
        dnew = _heads(lambda h: dstate[pl.ds(HD * h, HD), :])
        dnewb = dnew.astype(BF16)
        E = jnp.exp(s["cs_col"])
        ecl = jnp.exp(cl)
        dG = _heads(lambda h: _nt(dyb[h], xdtb[h]))
        dxdt = _heads(lambda h: _tn(gb[h], dyb[h]))
        Yo = _heads(lambda h: _nt(Cm, prevb[h]))
        W = _heads(lambda h: _nt(Bm, dnewb[h]))
        dcb = jnp.sum(dG * lam, axis=0)
        Mm = dG * g
        col_sums = jnp.sum(Mm, axis=1, keepdims=True)
        dYo = (dy * E).astype(BF16)
        dxdt = dxdt + W * f
        dF = jnp.sum(W * xdt, axis=2, keepdims=True) * f
        dcl = jnp.sum(dnew * prev, axis=(1, 2), keepdims=True) * ecl + jnp.sum(dF, axis=1, keepdims=True)
        dcs = (jnp.sum(Mm, axis=2, keepdims=True) + jnp.sum(dy * Yo, axis=2, keepdims=True) * E - dF
               + jnp.where(row == CHUNK - 1, dcl, 0.0))
        ddt_x = jnp.sum(dxdt * x, axis=2, keepdims=True)
        dD = jnp.sum(dy * x, axis=(1, 2), keepdims=True)
        dx = s["dsk_col"] * dy + dxdt * s["dt_col"]
        xfb = (xdt * f).astype(BF16)
        dprev = _heads(lambda h: _tn(dYo[h], Cm)) + dnew * ecl
        dcbb = dcb.astype(BF16)
        dC = _nn(dcbb, Bm)
        dB = _tn(dcbb, Cm)
        dcs_mat = -_rows_to_block([col_sums[h] for h in range(HEADS)], CHUNK, CHUNK).T
        ddt_mat = jnp.zeros((CHUNK, DT_PAD), F32)
        dD_row = jnp.zeros((1, DT_PAD), F32)
        for h in range(HEADS):
            sl = pl.ds(HD * h, HD)
            dC = dC + _nn(dYo[h], prevb[h])
            dB = dB + _nn(xfb[h], dnewb[h])
            dcs_mat = dcs_mat + jnp.where(lane == h, dcs[h], 0.0)
            ddt_mat = ddt_mat + jnp.where(lane == h, ddt_x[h], 0.0)
            dD_row = dD_row + jnp.where(lane == h, dD[h], 0.0)
            dact_ref[:, sl] = dx[h]
            dstate[sl, :] = dprev[h]
        dact_ref[:, pl.ds(AW, NS)] = dB
        dact_ref[:, pl.ds(AW + NS, NS)] = dC
        da = jnp.dot((s["li"] <= s["si"]).astype(F32), dcs_mat, precision=HIGHEST, preferred_element_type=F32)
        ddtp = jnp.where(lane < HEADS, (ddt_mat + da * s["A"]) * _sigmoid(s["dtp"]), 0.0)
        ddt_ref[...] = ddtp
        dalog = jnp.where(lane < HEADS, jnp.sum(da * s["dt"], axis=0, keepdims=True) * s["A"], 0.0)
        par_ref[...] += _rows_to_block([jnp.sum(ddtp, axis=0, keepdims=True), dalog, dD_row], 8, DT_PAD)

    vec = pl.BlockSpec((1, DT_PAD), lambda c: (0, 0))
    rev = lambda c: nc - 1 - c
    return pl.pallas_call(
        body, name="ssd_bwd", grid=(nc,),
        in_specs=[pl.BlockSpec((CHUNK, AW), lambda c: (rev(c), 0)), pl.BlockSpec((CHUNK, NS), lambda c: (rev(c), 4)),
                  pl.BlockSpec((CHUNK, NS), lambda c: (rev(c), 5)), pl.BlockSpec((CHUNK, DT_PAD), lambda c: (rev(c), 6)),
                  vec, vec, vec,
                  pl.BlockSpec((None, AW, NS), lambda c: (rev(c), 0, 0)), pl.BlockSpec((CHUNK, AW), lambda c: (rev(c), 0))],
        out_specs=[pl.BlockSpec((CHUNK, CONV_CH), lambda c: (rev(c), 0)), pl.BlockSpec((CHUNK, DT_PAD), lambda c: (rev(c), 0)),
                   pl.BlockSpec((8, DT_PAD), lambda c: (0, 0))],
        out_shape=[jax.ShapeDtypeStruct((T, CONV_CH), F32), jax.ShapeDtypeStruct((T, DT_PAD), F32),
                   jax.ShapeDtypeStruct((8, DT_PAD), F32)],
        scratch_shapes=[pltpu.VMEM((AW, NS), F32), pltpu.VMEM((CHUNK, CHUNK), F32)],
        compiler_params=_cparams(("arbitrary",)),
    )(act, act, act, xbcdt, bias, alog, dsk, states, dy)


def _place():
    return lax.axis_index("x"), lax.axis_index("y"), lax.axis_index("c")


def _slot(px, py, pc):
    return 4 * px + 2 * py + pc


def _all_gather(arrs, name):
    na = len(arrs)

    def body(*refs):
        ins, outs = refs[:na], refs[na:2 * na]
        send_sems, recv_sems, local_sems = refs[2 * na:]
        x, y, c = _place()
        me, sib = (x, y, c), (x, y, 1 - c)
        chips = [(1 - x, y), (x, 1 - y), (1 - x, 1 - y)]

        def copy(a, kk, block, to, src=None):
            dst = outs[a].at[_slot(*block)]
            return pltpu.make_async_remote_copy(
                src_ref=dst if src is None else src, dst_ref=dst,
                send_sem=send_sems.at[a, kk], recv_sem=recv_sems.at[a, kk], device_id=to, device_id_type=MESH)

        mine = [pltpu.make_async_copy(ins[a], outs[a].at[_slot(*me)], local_sems.at[a]) for a in range(na)]
        for cp in mine:
            cp.start()
        first = []
        for a in range(na):
            first.append(copy(a, 0, me, sib, src=ins[a]))
            first += [copy(a, 1 + j, me, (*chip, c), src=ins[a]) for j, chip in enumerate(chips)]
        for cp in first:
            cp.start()
        passed = []
        for j, chip in enumerate(chips):
            for a in range(na):
                copy(a, 1 + j, (*chip, c), me).wait_recv()
                fw = copy(a, 4 + j, (*chip, c), sib)
                fw.start()
                passed.append(fw)
        for a in range(na):
            copy(a, 0, sib, me).wait_recv()
            for j, chip in enumerate(chips):
                copy(a, 4 + j, (*chip, 1 - c), me).wait_recv()
        for cp in first + passed:
            cp.wait_send()
        for cp in mine:
            cp.wait()

    any_spec = pl.BlockSpec(memory_space=pl.ANY)
    return pl.pallas_call(
        body, name=name,
        in_specs=[any_spec] * na, out_specs=[any_spec] * na,
        out_shape=[jax.ShapeDtypeStruct((N_DEV,) + a.shape, a.dtype) for a in arrs],
        scratch_shapes=[pltpu.SemaphoreType.DMA((na, 7)), pltpu.SemaphoreType.DMA((na, 7)),
                        pltpu.SemaphoreType.DMA((na,))],
    )(*arrs)


def _reduce_scatter(part, name):
    _, r, C = part.shape

    def body(part_ref, out_ref, own, got_sib, chip_sum, got_ici, lsem, s1, r1, s2, r2):
        x, y, c = _place()
        chips = [(x, y), (1 - x, y), (x, 1 - y), (1 - x, 1 - y)]
        loc = [pltpu.make_async_copy(part_ref.at[_slot(*chips[kk], c)], own.at[kk], lsem.at[kk]) for kk in range(4)]
        d2d = [pltpu.make_async_remote_copy(
            src_ref=part_ref.at[_slot(*chips[kk], 1 - c)], dst_ref=got_sib.at[kk],
            send_sem=s1.at[kk], recv_sem=r1.at[kk], device_id=(x, y, 1 - c), device_id_type=MESH) for kk in range(4)]
        for cp in loc + d2d:
            cp.start()
        ici = [pltpu.make_async_remote_copy(
            src_ref=chip_sum.at[kk - 1], dst_ref=got_ici.at[kk - 1],
            send_sem=s2.at[kk - 1], recv_sem=r2.at[kk - 1], device_id=(*chips[kk], c), device_id_type=MESH)
            for kk in range(1, 4)]
        for kk in (1, 2, 3):
            loc[kk].wait()
            d2d[kk].wait_recv()
            chip_sum[kk - 1] = (own[kk].astype(F32) + got_sib[kk].astype(F32)).astype(BF16)
            ici[kk - 1].start()
        loc[0].wait()
        d2d[0].wait_recv()
        acc = own[0].astype(F32) + got_sib[0].astype(F32)
        for cp in ici:
            cp.wait_recv()
        out_ref[...] = ((acc + got_ici[0].astype(F32)) + got_ici[1].astype(F32)) + got_ici[2].astype(F32)
        for cp in d2d + ici:
            cp.wait_send()

    return pl.pallas_call(
        body, name=name,
        in_specs=[pl.BlockSpec(memory_space=pl.ANY)],
        out_specs=pl.BlockSpec(memory_space=pltpu.VMEM),
        out_shape=jax.ShapeDtypeStruct((r, C), F32),
        scratch_shapes=[pltpu.VMEM((4, r, C), BF16), pltpu.VMEM((4, r, C), BF16), pltpu.VMEM((3, r, C), BF16),
                        pltpu.VMEM((3, r, C), BF16),
                        pltpu.SemaphoreType.DMA((4,)), pltpu.SemaphoreType.DMA((4,)), pltpu.SemaphoreType.DMA((4,)),
                        pltpu.SemaphoreType.DMA((3,)), pltpu.SemaphoreType.DMA((3,))],
        compiler_params=pltpu.CompilerParams(vmem_limit_bytes=VMEM_LIMIT),
    )(part)


SLAB_ROWS = 24


def _all_reduce_small(parts, name):
    R, C = SLAB_ROWS, D
    n = len(parts)

    def body(*refs):
        in_refs = refs[:n]
        out_ref, slab, got, send_sems, recv_sems = refs[n:]
        slab[...] = jnp.zeros_like(slab)
        for ref, (arr, row) in zip(in_refs, parts):
            slab[pl.ds(row, arr.shape[0]), pl.ds(0, arr.shape[1])] = ref[...]
        x, y, c = _place()
        mine = _slot(x, y, c)
        copies = [pltpu.make_async_remote_copy(
            src_ref=slab, dst_ref=got.at[mine], send_sem=send_sems.at[kk], recv_sem=recv_sems.at[kk],
            device_id=peer, device_id_type=MESH) for kk, peer in enumerate(_peers(x, y, c))]
        for cp in copies:
            cp.start()
        got[mine] = slab[...]
        for cp in copies:
            cp.wait_recv()
        acc = got[0]
        for s in range(1, N_DEV):
            acc = acc + got[s]
        out_ref[...] = acc
        for cp in copies:
            cp.wait_send()

    vm = pl.BlockSpec(memory_space=pltpu.VMEM)
    return pl.pallas_call(
        body, name=name, in_specs=[vm] * n, out_specs=vm,
        out_shape=jax.ShapeDtypeStruct((R, C), F32),
        scratch_shapes=[pltpu.VMEM((R, C), F32), pltpu.VMEM((N_DEV, R, C), F32), pltpu.SemaphoreType.DMA((N_DEV - 1,)),
                        pltpu.SemaphoreType.DMA((N_DEV - 1,))],
    )(*[a for a, _ in parts])


_HBM = pl.BlockSpec(memory_space=pltpu.HBM)
_SEM = pl.BlockSpec(memory_space=pltpu.SEMAPHORE)
_EFFECT = pltpu.SideEffectType.DATAFLOW_SIDE_EFFECTING


def _peers(x, y, c):
    out = []
    for kk in range(1, N_DEV):
        fx, fy, fc = kk >> 2 & 1, kk >> 1 & 1, kk & 1
        out.append((1 - x if fx else x, 1 - y if fy else y, 1 - c if fc else c))
    return out


def _send_start(src, per_peer, name, dep):
    (handles, token) = _send_start_many([src], per_peer, name, dep)
    return handles, token


def _send_start_many(srcs, per_peer, name, dep):
    n = len(srcs)

    def body(*refs):
        src_refs, land_refs = refs[:n], refs[n:2 * n]
        send_sems, recv_sems = refs[2 * n + 1], refs[2 * n + 2]
        token = refs[-1]
        x, y, c = _place()
        mine = _slot(x, y, c)
        for a in range(n):
            for kk, peer in enumerate(_peers(x, y, c)):
                pltpu.make_async_remote_copy(
                    src_ref=src_refs[a].at[_slot(*peer)] if per_peer else src_refs[a], dst_ref=land_refs[a].at[mine],
                    send_sem=send_sems.at[a * (N_DEV - 1) + kk], recv_sem=recv_sems.at[a * (N_DEV - 1) + kk],
                    device_id=peer, device_id_type=MESH).start()
        token[...] = jnp.zeros_like(token)

    lands = [lax.empty((N_DEV,) + tuple(s.shape[1:] if per_peer else s.shape), s.dtype) for s in srcs]
    hbm = lambda t: pltpu.with_memory_space_constraint(t, pltpu.HBM)
    outs = pl.pallas_call(
        body, name=name,
        out_shape=(pltpu.SemaphoreType.DMA((n * (N_DEV - 1),)), pltpu.SemaphoreType.DMA((n * (N_DEV - 1),)),
                   *[pltpu.HBM(s.shape, s.dtype) for s in srcs], *[pltpu.HBM(l.shape, l.dtype) for l in lands],
                   jax.ShapeDtypeStruct((8, 128), F32)),
        in_specs=(*[_HBM] * (2 * n), _ANY),
        out_specs=(_SEM, _SEM, *[_HBM] * (2 * n), pl.BlockSpec(memory_space=pltpu.VMEM)),
        input_output_aliases={i: 2 + i for i in range(2 * n)},
        compiler_params=pltpu.CompilerParams(has_side_effects=_EFFECT),
    )(*[hbm(s) for s in srcs], *[hbm(l) for l in lands], dep)
    return (outs[0], outs[1], list(outs[2:2 + n]), list(outs[2 + n:2 + 2 * n])), outs[-1]


def _send_wait(handles, after, name):
    srcs, lands = _send_wait_many(handles, after, name)
    return srcs[0], lands[0]


def _send_wait_many(handles, after, name):
    send_sems, recv_sems, src_thrus, land_thrus = handles
    n = len(src_thrus)

    def body(*refs):
        land_refs = refs[n:2 * n]
        send_sems, recv_sems = refs[2 * n], refs[2 * n + 1]
        me = _place()
        for a in range(n):
            for kk in range(N_DEV - 1):
                cp = pltpu.make_async_remote_copy(
                    src_ref=land_refs[a].at[0], dst_ref=land_refs[a].at[0],
                    send_sem=send_sems.at[a * (N_DEV - 1) + kk], recv_sem=recv_sems.at[a * (N_DEV - 1) + kk],
                    device_id=me, device_id_type=MESH)
                cp.wait_send()
                cp.wait_recv()

    both = list(src_thrus) + list(land_thrus)
    outs = pl.pallas_call(
        body, name=name,
        out_shape=tuple(pltpu.HBM(t.shape, t.dtype) for t in both),
        in_specs=(*[_HBM] * (2 * n), _SEM, _SEM, _ANY), out_specs=tuple([_HBM] * (2 * n)),
        input_output_aliases={i: i for i in range(2 * n)},
        compiler_params=pltpu.CompilerParams(has_side_effects=_EFFECT),
    )(*both, send_sems, recv_sems, after)
    return list(outs[:n]), list(outs[n:])


def _sum_slots(land, name):
    _, R, C = land.shape
    tm = R if R <= 512 else 512

    def body(x_ref, o_ref):
        acc = x_ref[0].astype(F32)
        for j in range(1, N_DEV):
            acc = acc + x_ref[j].astype(F32)
        o_ref[...] = acc

    return pl.pallas_call(
        body, name=name, grid=(R // tm,),
        in_specs=[pl.BlockSpec((N_DEV, tm, C), lambda i: (0, i, 0))], out_specs=pl.BlockSpec((tm, C), lambda i: (i, 0)),
        out_shape=jax.ShapeDtypeStruct((R, C), F32), compiler_params=_cparams(("parallel",)),
    )(land)


def _adam_math(w, g, m, v):
    m2 = ADAM_B1 * m + (1.0 - ADAM_B1) * g
    v2 = ADAM_B2 * v + (1.0 - ADAM_B2) * (g * g)
    m_hat = m2 / (1.0 - ADAM_B1 ** ADAM_STEP)
    v_hat = v2 / (1.0 - ADAM_B2 ** ADAM_STEP)
    delta = -ADAM_LR * (m_hat / (jnp.sqrt(v_hat) + ADAM_EPS) + ADAM_WD * w)
    return delta, m2, v2


def _adamw(w, g, m, v, name):
    R, C = w.shape
    tm = R if R <= 512 else 256
    return _rowwise(lambda w, g, m, v: (_adam_math(w, g, m, v), ()), [w, g, m, v], [], [(C, F32)] * 3, [], tm=tm, name=name)


def _adamw_small(slab, slab_rows, g_conv_w, ws, ms, vs):
    n = len(ws)

    def body(*refs):
        slab_ref, gc_ref = refs[0], refs[1]
        w_refs, m_refs, v_refs = refs[2:2 + n], refs[2 + n:2 + 2 * n], refs[2 + 2 * n:2 + 3 * n]
        outs = refs[2 + 3 * n:]
        loss_ref = outs[0]
        g_out, d_out, m_out, v_out = (outs[1 + i * n:1 + (i + 1) * n] for i in range(4))
        loss_ref[...] = jnp.sum(slab_ref[pl.ds(6, 1), :], axis=1, keepdims=True)
        for i in range(n):
            g = gc_ref[...] if i == n - 1 else slab_ref[pl.ds(slab_rows[i], 1), pl.ds(0, ws[i].shape[1])]
            d, m2, v2 = _adam_math(w_refs[i][...], g, m_refs[i][...], v_refs[i][...])
            g_out[i][...] = g
            d_out[i][...] = d
            m_out[i][...] = m2
            v_out[i][...] = v2

    vm = pl.BlockSpec(memory_space=pltpu.VMEM)
    shapes = [jax.ShapeDtypeStruct(w.shape, F32) for w in ws]
    outs = pl.pallas_call(
        body, name="adamw_small", in_specs=[vm] * (2 + 3 * n), out_specs=[vm] * (1 + 4 * n),
        out_shape=[jax.ShapeDtypeStruct((1, 1), F32)] + shapes * 4,
    )(slab, g_conv_w, *ws, *ms, *vs)
    return outs[0], outs[1:1 + n], outs[1 + n:1 + 2 * n], outs[1 + 2 * n:1 + 3 * n], outs[1 + 3 * n:]


SMALL = ["norm_mix_pre", "norm_mix_post", "norm_mlp_pre", "norm_mlp_post", "norm_ple_post",
         "conv_b", "ssd_norm_g", "dt_bias", "a_log", "d_skip"]


def _pad_row(v, width=D):
    return jnp.pad(v, ((0, 0), (0, width - v.shape[1])))


def kernel(x, p, positions, norm_mix_pre, norm_mix_post, w_in, conv_w, conv_b, dt_bias, a_log, d_skip, ssd_norm_g, w_out, norm_mlp_pre, norm_mlp_post, w_up, w_down, w_ple_gate, w_ple_proj, norm_ple_post, loss_target, m_norm_mix_pre, m_norm_mix_post, m_w_in, m_conv_w, m_conv_b, m_dt_bias, m_a_log, m_d_skip, m_ssd_norm_g, m_w_out, m_norm_mlp_pre, m_norm_mlp_post, m_w_up, m_w_down, m_w_ple_gate, m_w_ple_proj, m_norm_ple_post, v_norm_mix_pre, v_norm_mix_post, v_w_in, v_conv_w, v_conv_b, v_dt_bias, v_a_log, v_d_skip, v_ssd_norm_g, v_w_out, v_norm_mlp_pre, v_norm_mlp_post, v_w_up, v_w_down, v_w_ple_gate, v_w_ple_proj, v_norm_ple_post):
    args = dict(locals())
    x2, p2, tgt = x[0], p[0, 0], loss_target[0]
    g1, g2, g3, g4, g5 = norm_mix_pre, norm_mix_post, norm_mlp_pre, norm_mlp_post, norm_ple_post

    me = _slot(*_place())
    pack_in = jnp.pad(w_in[0].T, ((0, W_IN_SHARD_PAD - W_IN_SHARD), (0, 0))).astype(BF16)
    rest = [w_out[0].astype(BF16), w_up[0].T.astype(BF16), w_down[0].astype(BF16), w_ple_gate[0].astype(BF16),
            w_ple_proj[0].T.reshape(32, D).astype(BF16)]
    conv_pack = jnp.pad(conv_w[0], ((0, 4), (0, 32)))
    gin, gconv = _all_gather([pack_in, conv_pack], "gather_w_in")
    rest_handles, tok_rest = _send_start_many(rest, False, "gather_rest_start", gconv)
    w_inT = gin[:, :W_IN_SHARD].reshape(IN_W, D)
    w_qkvzT = w_inT[:4 * AW]
    w_xbcdtT = jnp.pad(w_inT[4 * AW:], ((0, DT_PAD - HEADS), (0, 0)))
    conv_full = gconv[:, :CONV_K, :96].transpose(1, 0, 2).reshape(CONV_K, CONV_CH)

    inv_freq = ROPE_THETA ** (-jnp.arange(HD // 2, dtype=F32) * 2.0 / HD)
    ang = positions[0].astype(F32)[:, None] * inv_freq
    cos, sin = jnp.cos(ang), jnp.sin(ang)
    cos128 = jnp.concatenate([cos, cos, cos, cos], axis=1)
    sin128 = jnp.concatenate([-sin, sin, -sin, sin], axis=1)

    bias_w, alog_w, dsk_w = _pad_row(dt_bias, DT_PAD), _pad_row(a_log, DT_PAD), _pad_row(d_skip, DT_PAD)
    rms_pre = lambda a, r, g: a * r * g

    (u1,) = _rowwise(lambda a, g: ((a * _rstd(a) * g,), ()), [x2], [g1], [(D, BF16)], [], tm=512, name="norm_x")
    p2b = p2.astype(BF16)
    qkvz = _mm(u1, w_qkvzT, tb=True, tm=512, tn=1024, tk=1024, name="proj_qkvz", deps=[tok_rest])
    xbcdt = _mm(u1, w_xbcdtT, tb=True, tm=512, tn=896, tk=1024, name="proj_xbcdt")

    qkv = _rope_fwd(qkvz, cos128, sin128)
    qkv = [qkv[3 * i:3 * i + 3] for i in range(len(DILATIONS))]
    outs, lses = [], []
    for d, (qd, kd, vd) in zip(DILATIONS, qkv):
        o, l = _attn_fwd(qd, kd, vd, d)
        outs.append(o)
        lses.append(l)
    attn, lse, attn4, lse4, attn16, lse16 = _attn_merge(outs, lses)

    act = _conv_fwd(xbcdt, conv_full, conv_b)
    y_ssd, states = _ssd_fwd(act, xbcdt, bias_w, alog_w, dsk_w)

    def gated_fwd(y, z, a, gs):
        gi = y * (z * _sigmoid(z))
        return (jnp.concatenate([a, gi * _rstd(gi) * gs], axis=1),), ()
    (cat,) = _rowwise(gated_fwd, [y_ssd, (qkvz, AW, 3), attn], [ssd_norm_g], [(D, BF16)], [], tm=512, name="gated_norm")

    rest_back, landed = _send_wait_many(rest_handles, cat, "gather_rest_wait")
    landed = [lax.dynamic_update_slice(l, b[None], (me, 0, 0)) for l, b in zip(landed, rest_back)]
    w_o, w_upT, w_dn, w_gate = landed[0].reshape(D, D), landed[1].reshape(DFF, D), landed[2].reshape(DFF, D), landed[3].reshape(D, D)
    w_projT = landed[4].reshape(D, PLE)

    mix = _mm(cat, w_o, tm=512, tn=1024, tk=1024, name="mix_out")

    def post1(xx, mm, ga, gb):
        h = xx + mm * _rstd(mm) * ga
        return (h, _rstd(h)), ()
    h1, r3 = _rowwise(post1, [x2, mix], [g2, g3], [(D, F32), (1, F32)], [], tm=512, name="post_mix")

    a_up, ff, u2, h2, h2b = _mlp_fwd(h1, r3, g3, w_upT, w_dn, g4)
    relu2 = lambda a: jnp.square(jnp.maximum(a.astype(F32), 0.0))

    gp = _mm(h2b, w_gate, tm=512, tn=1024, tk=1024, name="ple_gate")
    pp = _mm(p2b, w_projT, tb=True, tm=512, tn=1024, tk=256, name="ple_proj")

    def final(hh, gpre, ppv, tg, g):
        sg = _sigmoid(gpre)
        ple = ppv * sg
        r = _rstd(ple)
        n = ple * r
        h3 = hh + n * g
        e = h3 - tg
        dh3 = e * (1.0 / D)
        dple = _rms_bwd(n, r, g, dh3)
        return (dh3, dple * sg, dple * ppv * sg * (1.0 - sg)), (_colsum(dh3 * n), _colsum(0.5 * e * e * (1.0 / D)))
    dh3, dpp, dgp, dg5, loss_vec = _rowwise(final, [h2, gp, pp, tgt], [g5], [(D, F32), (D, BF16), (D, BF16)],
                                            [(1, D), (1, D)], tm=256, name="loss_ple_bwd")

    gw_projT = _mm(dpp, p2b, ta=True, tm=512, tn=256, tk=1024, out_dtypes=(BF16,), name="gw_ple_proj")
    gw_gate = _mm(h2b, dgp, ta=True, tm=512, tn=1024, tk=1024, out_dtypes=(BF16,), name="gw_ple_gate")
    rs_proj, tok_proj = _send_start(gw_projT.reshape(N_DEV, 32, D), True, "rs_start_w_proj", g1)
    rs_gate, tok_gate = _send_start(gw_gate.reshape(N_DEV, 128, D), True, "rs_start_w_gate", g1)
    dh2_g = _mm(dgp, w_gate, tb=True, tm=512, tn=1024, tk=1024, name="dx_ple_gate", deps=[tok_proj, tok_gate])

    def bwd_mlp_post(d3, dg_, f, g):
        dh2 = d3 + dg_
        r = _rstd(f)
        n = f * r
        return (dh2, _rms_bwd(n, r, g, dh2)), (_colsum(dh2 * n),)
    dh2, dff, dg4 = _rowwise(bwd_mlp_post, [dh3, dh2_g, ff], [g4], [(D, F32), (D, BF16)], [(1, D)], tm=256,
                             name="bwd_post_mlp")

    gw_dn = _mm(a_up, dff, ta=True, tm=1024, tn=1024, tk=1024, a_pre=relu2, out_dtypes=(BF16,), name="gw_mlp_down")
    rs_dn, tok_dn = _send_start(gw_dn.reshape(N_DEV, 512, D), True, "rs_start_w_down", g1)
    da_up, du2 = _mlp_dx(dff, a_up, w_upT, w_dn, tok_dn)
    gw_upT = _mm(da_up, u2, ta=True, tm=1024, tn=1024, tk=1024, out_dtypes=(BF16,), name="gw_mlp_up")
    rs_up, tok_up = _send_start(gw_upT.reshape(N_DEV, 512, D), True, "rs_start_w_up", g1)

    def bwd_mix_post(d2, du, hh, rr, mm, ga, gb):
        n3 = hh * rr
        dh1 = d2 + _rms_bwd(n3, rr, gb, du)
        r = _rstd(mm)
        n2 = mm * r
        return (dh1, _rms_bwd(n2, r, ga, dh1)), (_colsum(du * n3), _colsum(dh1 * n2))
    dh1, dmix, dg3, dg2 = _rowwise(bwd_mix_post, [dh2, du2, h1, r3, mix], [g2, g3], [(D, F32), (D, BF16)],
                                   [(1, D), (1, D)], tm=256, name="bwd_post_mix", deps=[tok_up])

    gw_o = _mm(cat, dmix, ta=True, tm=512, tn=1024, tk=1024, out_dtypes=(BF16,), name="gw_out")
    rs_o, tok_o = _send_start(gw_o.reshape(N_DEV, 128, D), True, "rs_start_w_out", g1)
    dcat = _mm(dmix, w_o, tb=True, tm=512, tn=1024, tk=1024, name="dx_out", deps=[tok_o])

    def gated_bwd(y, z, dyn, gs):
        sg = _sigmoid(z)
        sz = z * sg
        gi = y * sz
        r = _rstd(gi)
        n = gi * r
        dgi = _rms_bwd(n, r, gs, dyn)
        return (dgi * sz, dgi * y * (sg * (1.0 + z * (1.0 - sg)))), (_colsum(dyn * n),)
    dy_ssd, dz, dgs = _rowwise(gated_bwd, [y_ssd, (qkvz, AW, 3), (dcat, AW, 1)], [ssd_norm_g], [(AW, F32)] * 2, [(1, AW)],
                               tm=512, name="bwd_gated_norm")

    dact, ddtw, ssd_par = _ssd_bwd(act, xbcdt, bias_w, alog_w, dsk_w, states, dy_ssd)
    dxbcdt, conv_par = _conv_bwd(xbcdt, dact, ddtw, conv_full, conv_b)

    dattn4, dattn16 = _dilate_cols(dcat, 0)
    qkv_grads = [_attn_bwd(*qkv[0], dcat, attn, lse, 1),
                 _attn_bwd(*qkv[1], dattn4, attn4, lse4, 4),
                 _attn_bwd(*qkv[2], dattn16, attn16, lse16, 16)]
    dqkvz = _rope_bwd(qkv_grads, dz, cos128, sin128)

    gw_qkvzT = _mm(dqkvz, u1, ta=True, tm=1024, tn=1024, tk=1024, out_dtypes=(BF16,), name="gw_qkvz")
    gw_xbcdtT = _mm(dxbcdt, u1, ta=True, tm=896, tn=1024, tk=1024, out_dtypes=(BF16,), name="gw_xbcdt")
    gw_inT = jnp.concatenate([gw_qkvzT, gw_xbcdtT], axis=0)[:IN_W]
    gw_inT = jnp.pad(gw_inT.reshape(N_DEV, W_IN_SHARD, D), ((0, 0), (0, W_IN_SHARD_PAD - W_IN_SHARD), (0, 0)))
    rs_in, tok_in = _send_start(gw_inT, True, "rs_start_w_in", g1)

    du1a = _mm(dqkvz, w_qkvzT, tm=512, tn=1024, tk=1024, name="dx_qkvz", deps=[tok_in])
    du1b = _mm(dxbcdt, w_xbcdtT, tm=512, tn=1024, tk=896, name="dx_xbcdt")

    def bwd_in(d1, ua, ub, xx, g):
        rr = _rstd(xx)
        n = xx * rr
        du = ua + ub
        return (d1 + _rms_bwd(n, rr, g, du),), (_colsum(du * n),)
    grad_x, dg1 = _rowwise(bwd_in, [dh1, du1a, du1b, x2], [g1], [(D, F32)], [(1, D)], tm=256, name="bwd_pre_mix")

    slab = _all_reduce_small([(dg1, 0), (dg2, 1), (dg3, 2), (dg4, 3), (dg5, 4), (dgs, 5), (loss_vec, 6),
                              (conv_par, 8), (ssd_par, 16)], "reduce_small")
    g_conv_w = lax.dynamic_slice(slab[8:12, :CONV_CH], (0, me * 96), (CONV_K, 96))

    def scatter_finish(handles, nm, after):
        part, land = _send_wait(handles, after, "rs_wait_" + nm)
        own = lax.dynamic_slice(part, (me, 0, 0), (1,) + part.shape[1:])
        return _sum_slots(lax.dynamic_update_slice(land, own, (me, 0, 0)), "rs_sum_" + nm)
    g_out = scatter_finish(rs_o, "w_out", slab)
    g_upT = scatter_finish(rs_up, "w_up", slab)
    g_dn = scatter_finish(rs_dn, "w_down", slab)
    g_gate = scatter_finish(rs_gate, "w_gate", slab)
    g_projT = scatter_finish(rs_proj, "w_proj", slab)

    small_names = SMALL + ["conv_w"]
    small_rows = [0, 1, 2, 3, 4, 12, 5, 16, 17, 18, None]
    pick = lambda prefix: [args[prefix + nme] for nme in SMALL] + [args[prefix + "conv_w"][0]]
    loss11, g_s, d_s, m_s, v_s = _adamw_small(slab, small_rows, g_conv_w, pick(""), pick("m_"), pick("v_"))
    loss = loss11[0, 0]
    grads = {
        "w_out": g_out[None], "w_up": g_upT.T[None], "w_down": g_dn[None],
        "w_ple_gate": g_gate[None], "w_ple_proj": g_projT.reshape(128, PLE).T[None],
    }
    delta, new_m, new_v = {}, {}, {}
    for i, nme in enumerate(small_names):
        lead = (lambda t: t[None]) if nme == "conv_w" else (lambda t: t)
        grads[nme], delta[nme], new_m[nme], new_v[nme] = lead(g_s[i]), lead(d_s[i]), lead(m_s[i]), lead(v_s[i])
    for nme in ["w_out", "w_up", "w_down", "w_ple_gate", "w_ple_proj", "w_in"]:
        if nme == "w_in":
            g_inT = scatter_finish(rs_in, "w_in", delta["w_down"])
            grads["w_in"] = g_inT[:W_IN_SHARD].T[None]
        dl, mm_, vv_ = _adamw(args[nme][0], grads[nme][0], args["m_" + nme][0], args["v_" + nme][0], "adamw_" + nme)
        delta[nme], new_m[nme], new_v[nme] = dl[None], mm_[None], vv_[None]

    order = ["norm_mix_pre", "norm_mix_post", "w_in", "conv_w", "conv_b", "dt_bias", "a_log", "d_skip", "ssd_norm_g",
             "w_out", "norm_mlp_pre", "norm_mlp_post", "w_up", "w_down", "w_ple_gate", "w_ple_proj", "norm_ple_post"]
    return (loss, grad_x[None], *[grads[n] for n in order], *[delta[n] for n in order],
            *[new_m[n] for n in order], *[new_v[n] for n in order])
```

```python
import functools
import math

import jax
import jax.numpy as jnp
from jax import lax
from jax.experimental import pallas as pl
from jax.experimental.pallas import tpu as pltpu

F32 = jnp.float32
BF16 = jnp.bfloat16
MESH = pl.DeviceIdType.MESH
HIGHEST = lax.Precision.HIGHEST

N_DEV = 8
T = 4096
D = 1024
HEADS = 8
HD = 64
AW = 512
NS = 128
CONV_K = 4
CONV_CH = 768
CHUNK = 128
DFF = 4096
PLE = 256
EPS = 1e-6
ROPE_THETA = 10000.0
DILATIONS = (1, 4, 16)
QBLK = 128
NEG = -1e30
IN_W = 2824
W_IN_SHARD = 353
W_IN_SHARD_PAD = 384
DT_PAD = 128

ADAM_LR, ADAM_B1, ADAM_B2, ADAM_EPS, ADAM_WD, ADAM_STEP = 0.001, 0.9, 0.999, 1e-08, 0.01, 10

VMEM_LIMIT = 56 * 1024 * 1024


_ANY = pl.BlockSpec(memory_space=pl.ANY)


def _cparams(sem=None):
    return pltpu.CompilerParams(dimension_semantics=sem, vmem_limit_bytes=VMEM_LIMIT)


def _dot(a, b, ca, cb, precision=None):
    return lax.dot_general(a, b, (((ca,), (cb,)), ((), ())), preferred_element_type=F32, precision=precision)


def _nn(a, b):
    return _dot(a, b, 1, 0)


def _nt(a, b):
    return _dot(a, b, 1, 1)


def _tn(a, b):
    return _dot(a, b, 0, 0)


def _sigmoid(x):
    return 1.0 / (1.0 + jnp.exp(-x))


def _softplus(x):
    return jnp.maximum(x, 0.0) + jnp.log(1.0 + jnp.exp(-jnp.abs(x)))


def _mm(a, b, *, ta=False, tb=False, tm, tn, tk, name,
        a_pre=None, a_rows=(), a_cols=(), b_pre=None, b_rows=(), b_cols=(),
        epi=None, epi_tiles=(), out_dtypes=(F32,), deps=()):
    if ta:
        K, M = a.shape
    else:
        M, K = a.shape
    if tb:
        N, K2 = b.shape
    else:
        K2, N = b.shape
    assert K == K2 and M % tm == 0 and N % tn == 0 and K % tk == 0, (name, a.shape, b.shape)
    nk = K // tk
    if ta:
        a_spec = pl.BlockSpec((tk, tm), lambda i, j, k: (k, i))
        a_row_specs = [pl.BlockSpec((tk, 1), lambda i, j, k: (k, 0)) for _ in a_rows]
        a_col_specs = [pl.BlockSpec((1, tm), lambda i, j, k: (0, i)) for _ in a_cols]
    else:
        a_spec = pl.BlockSpec((tm, tk), lambda i, j, k: (i, k))
        a_row_specs = [pl.BlockSpec((tm, 1), lambda i, j, k: (i, 0)) for _ in a_rows]
        a_col_specs = [pl.BlockSpec((1, tk), lambda i, j, k: (0, k)) for _ in a_cols]
    if tb:
        b_spec = pl.BlockSpec((tn, tk), lambda i, j, k: (j, k))
        b_row_specs = [pl.BlockSpec((tn, 1), lambda i, j, k: (j, 0)) for _ in b_rows]
        b_col_specs = [pl.BlockSpec((1, tk), lambda i, j, k: (0, k)) for _ in b_cols]
    else:
        b_spec = pl.BlockSpec((tk, tn), lambda i, j, k: (k, j))
        b_row_specs = [pl.BlockSpec((tk, 1), lambda i, j, k: (k, 0)) for _ in b_rows]
        b_col_specs = [pl.BlockSpec((1, tn), lambda i, j, k: (0, j)) for _ in b_cols]
    o_spec = pl.BlockSpec((tm, tn), lambda i, j, k: (i, j))
    na, nb, ne, no = len(a_rows) + len(a_cols), len(b_rows) + len(b_cols), len(epi_tiles), len(out_dtypes)

    def body(*refs):
        a_ref, b_ref = refs[0], refs[1]
        a_ex = refs[2:2 + na]
        b_ex = refs[2 + na:2 + na + nb]
        e_ex = refs[2 + na + nb:2 + na + nb + ne]
        first_out = 2 + na + nb + ne + len(deps)
        outs = refs[first_out:first_out + no]

        def finish(res):
            vals = epi(res, *[r[...] for r in e_ex]) if epi is not None else (res,)
            for o_ref, val in zip(outs, vals):
                o_ref[...] = val.astype(o_ref.dtype)

        at = a_ref[...]
        if a_pre is not None:
            at = a_pre(at, *[r[...] for r in a_ex])
        bt = b_ref[...]
        if b_pre is not None:
            bt = b_pre(bt, *[r[...] for r in b_ex])
        prod = _dot(at.astype(BF16), bt.astype(BF16), 0 if ta else 1, 1 if tb else 0)
        if nk == 1:
            finish(prod)
            return
        acc = refs[-1]
        k = pl.program_id(2)

        @pl.when(k == 0)
        def _():
            acc[...] = prod

        @pl.when(k > 0)
        def _():
            acc[...] += prod

        @pl.when(k == nk - 1)
        def _():
            finish(acc[...])

    outs = pl.pallas_call(
        body, name=name,
        grid=(M // tm, N // tn, nk),
        in_specs=([a_spec, b_spec] + a_row_specs + a_col_specs + b_row_specs + b_col_specs + [o_spec] * ne
                  + [_ANY] * len(deps)),
        out_specs=[o_spec] * no,
        out_shape=[jax.ShapeDtypeStruct((M, N), dt) for dt in out_dtypes],
        scratch_shapes=[pltpu.VMEM((tm, tn), F32)] if nk > 1 else [],
        compiler_params=_cparams(("parallel", "parallel", "arbitrary")),
    )(a, b, *a_rows, *a_cols, *b_rows, *b_cols, *epi_tiles, *deps)
    return outs[0] if no == 1 else outs


MLP_TM = 1024
MLP_TC = 512


def _mlp_fwd(h, r, g, w_upT, w_dn, g_post):
    nc = DFF // MLP_TC

    def body(h_ref, r_ref, g_ref, wu_ref, wd_ref, gp_ref, a_ref, ff_ref, u_ref, ho_ref, hob_ref, acc, u_scr):
        c = pl.program_id(1)

        @pl.when(c == 0)
        def _():
            u = (h_ref[...] * r_ref[...] * g_ref[...]).astype(BF16)
            u_scr[...] = u
            u_ref[...] = u
            acc[...] = jnp.zeros_like(acc)
        a = _nt(u_scr[...], wu_ref[...])
        a_ref[...] = a.astype(BF16)
        acc[...] += _nn(jnp.square(jnp.maximum(a, 0.0)).astype(BF16), wd_ref[...])

        @pl.when(c == nc - 1)
        def _():
            f = acc[...]
            ff_ref[...] = f
            ho = h_ref[...] + f * _rstd(f) * gp_ref[...]
            ho_ref[...] = ho
            hob_ref[...] = ho.astype(BF16)

    row = pl.BlockSpec((MLP_TM, D), lambda i, c: (i, 0))
    wsp = pl.BlockSpec((MLP_TC, D), lambda i, c: (c, 0))
    vec = pl.BlockSpec((1, D), lambda i, c: (0, 0))
    return pl.pallas_call(
        body, name="mlp_fwd", grid=(T // MLP_TM, nc),
        in_specs=[row, pl.BlockSpec((MLP_TM, 1), lambda i, c: (i, 0)), vec, wsp, wsp, vec],
        out_specs=[pl.BlockSpec((MLP_TM, MLP_TC), lambda i, c: (i, c)), row, row, row, row],
        out_shape=[jax.ShapeDtypeStruct((T, DFF), BF16), jax.ShapeDtypeStruct((T, D), F32), jax.ShapeDtypeStruct((T, D), BF16),
                   jax.ShapeDtypeStruct((T, D), F32), jax.ShapeDtypeStruct((T, D), BF16)],
        scratch_shapes=[pltpu.VMEM((MLP_TM, D), F32), pltpu.VMEM((MLP_TM, D), BF16)],
        compiler_params=_cparams(("parallel", "arbitrary")),
    )(h, r, g, w_upT, w_dn, g_post)


def _mlp_dx(dff, a, w_upT, w_dn, dep):
    nc = DFF // MLP_TC

    def body(d_ref, a_ref, wu_ref, wd_ref, dep_ref, da_ref, du_ref, acc, d_scr):
        c = pl.program_id(1)

        @pl.when(c == 0)
        def _():
            d_scr[...] = d_ref[...].astype(BF16)
            acc[...] = jnp.zeros_like(acc)
        da = (_nt(d_scr[...], wd_ref[...]) * (2.0 * jnp.maximum(a_ref[...].astype(F32), 0.0))).astype(BF16)
        da_ref[...] = da
        acc[...] += _nn(da, wu_ref[...])

        @pl.when(c == nc - 1)
        def _():
            du_ref[...] = acc[...]

    row = pl.BlockSpec((MLP_TM, D), lambda i, c: (i, 0))
    wsp = pl.BlockSpec((MLP_TC, D), lambda i, c: (c, 0))
    chunk = pl.BlockSpec((MLP_TM, MLP_TC), lambda i, c: (i, c))
    return pl.pallas_call(
        body, name="mlp_dx", grid=(T // MLP_TM, nc),
        in_specs=[row, chunk, wsp, wsp, _ANY], out_specs=[chunk, row],
        out_shape=[jax.ShapeDtypeStruct((T, DFF), BF16), jax.ShapeDtypeStruct((T, D), F32)],
        scratch_shapes=[pltpu.VMEM((MLP_TM, D), F32), pltpu.VMEM((MLP_TM, D), BF16)],
        compiler_params=_cparams(("parallel", "arbitrary")),
    )(dff, a, w_upT, w_dn, dep)


def _rowwise(fn, rows, vecs, out_rows, out_sums, *, tm, name, deps=()):
    specs, arrs = [], []
    R = None
    for r in rows:
        if isinstance(r, tuple):
            arr, width, cb = r
            specs.append(pl.BlockSpec((tm, width), lambda i, cb=cb: (i, cb)))
        else:
            arr = r
            specs.append(pl.BlockSpec((tm, arr.shape[1]), lambda i: (i, 0)))
        R = arr.shape[0] if R is None else R
        assert arr.shape[0] == R, name
        arrs.append(arr)
    assert R % tm == 0, name
    for v in vecs:
        specs.append(pl.BlockSpec(v.shape, lambda i: (0, 0)))
        arrs.append(v)
    nr, nv, no, ns = len(rows), len(vecs), len(out_rows), len(out_sums)
    out_specs = [pl.BlockSpec((tm, w), lambda i: (i, 0)) for w, _ in out_rows]
    out_specs += [pl.BlockSpec(s, lambda i: (0, 0)) for s in out_sums]
    out_shape = [jax.ShapeDtypeStruct((R, w), dt) for w, dt in out_rows]
    out_shape += [jax.ShapeDtypeStruct(s, F32) for s in out_sums]

    nd = len(deps)

    def body(*refs):
        ins = [r[...] for r in refs[:nr + nv]]
        o_refs = refs[nr + nv + nd:nr + nv + nd + no]
        s_refs = refs[nr + nv + nd + no:]
        o_vals, s_vals = fn(*ins)
        for ref, val in zip(o_refs, o_vals):
            ref[...] = val.astype(ref.dtype)
        if ns:
            @pl.when(pl.program_id(0) == 0)
            def _():
                for ref in s_refs:
                    ref[...] = jnp.zeros_like(ref)
            for ref, val in zip(s_refs, s_vals):
                ref[...] += val

    outs = pl.pallas_call(
        body, name=name, grid=(R // tm,), in_specs=specs + [_ANY] * nd, out_specs=out_specs, out_shape=out_shape,
        compiler_params=_cparams(("arbitrary",) if ns else ("parallel",)),
    )(*arrs, *deps)
    return outs


def _colsum(x):
    return jnp.sum(x, axis=0, keepdims=True)


def _rstd(x):
    return lax.rsqrt(jnp.mean(x * x, axis=-1, keepdims=True) + EPS)


def _rms_bwd(xn, r, g, dy):
    dn = dy * g
    return r * (dn - xn * jnp.mean(dn * xn, axis=-1, keepdims=True))


def _partner(t):
    lane = lax.broadcasted_iota(jnp.int32, t.shape, 1)
    up = pltpu.roll(t, 96, 1)
    down = pltpu.roll(t, 32, 1)
    return jnp.where((lane % 64) < 32, up, down)


SLABS = AW // 128


def _rows(r, n, d):
    return pl.ds(r, n, stride=d) if d > 1 else pl.ds(0, n)


def _undilate(src_ref, dst, d, tm):
    for r in range(d):
        for j in range(SLABS):
            dst[j][_rows(r, tm // d, d), :] = src_ref[:, pl.ds(r * AW + j * 128, 128)].astype(dst[j].dtype)


def _dilate(dst_ref, src, d, tm):
    for r in range(d):
        for j in range(SLABS):
            dst_ref[:, pl.ds(r * AW + j * 128, 128)] = src[j][_rows(r, tm // d, d), :].astype(dst_ref.dtype)


def _slab_scratch(n, tm):
    return [pltpu.VMEM((tm, 128), F32)] * (SLABS * n)


def _slab_groups(flat):
    return [flat[SLABS * i:SLABS * (i + 1)] for i in range(len(flat) // SLABS)]


def _slab_specs(tm, first):
    return [pl.BlockSpec((tm, 128), lambda i, j=j: (i, first + j)) for j in range(SLABS)]


def _dil_spec(tm, d):
    return pl.BlockSpec((tm // d, d * AW), lambda i: (i, 0))


ROPE_TM = 512


def _rope_fwd(qkvz, cos128, sin128):
    tm = ROPE_TM

    def body(*refs):
        q_refs, k_refs, v_refs = refs[0:4], refs[4:8], refs[8:12]
        c_ref, s_ref = refs[12], refs[13]
        outs = refs[14:]
        for di, d in enumerate(DILATIONS):
            oq, ok, ov = outs[3 * di:3 * di + 3]
            for r in range(d):
                rows = _rows(r, tm // d, d)
                c, s = c_ref[rows, :], s_ref[rows, :]
                for j in range(SLABS):
                    cols = pl.ds(r * AW + j * 128, 128)
                    q, k = q_refs[j][rows, :], k_refs[j][rows, :]
                    oq[:, cols] = ((q * c + _partner(q) * s) * (HD ** -0.5)).astype(BF16)
                    ok[:, cols] = (k * c + _partner(k) * s).astype(BF16)
                    ov[:, cols] = v_refs[j][rows, :].astype(BF16)

    tab = pl.BlockSpec((tm, 128), lambda i: (i, 0))
    out_specs, out_shape = [], []
    for d in DILATIONS:
        out_specs += [_dil_spec(tm, d)] * 3
        out_shape += [jax.ShapeDtypeStruct((T // d, d * AW), BF16)] * 3
    return pl.pallas_call(
        body, name="rope_fwd", grid=(T // tm,),
        in_specs=_slab_specs(tm, 0) + _slab_specs(tm, 4) + _slab_specs(tm, 8) + [tab, tab],
        out_specs=out_specs, out_shape=out_shape, compiler_params=_cparams(("parallel",)),
    )(*([qkvz] * 12), cos128, sin128)


def _rope_bwd(grads, dz, cos128, sin128):
    tm = 256

    def body(*refs):
        g_refs = refs[0:9]
        dz_ref, c_ref, s_ref, o_ref = refs[9], refs[10], refs[11], refs[12]
        scr = _slab_groups(refs[13:])
        for di, d in enumerate(DILATIONS[1:]):
            for t in range(3):
                _undilate(g_refs[3 * (di + 1) + t], scr[3 * di + t], d, tm)
        c, s = c_ref[...], s_ref[...]
        for j in range(SLABS):
            cols = pl.ds(j * 128, 128)
            tot = [g_refs[t][:, cols] + scr[t][j][...] + scr[3 + t][j][...] for t in range(3)]
            dqr = tot[0] * (HD ** -0.5)
            o_ref[:, pl.ds(j * 128, 128)] = (dqr * c + _partner(dqr * s)).astype(BF16)
            o_ref[:, pl.ds(AW + j * 128, 128)] = (tot[1] * c + _partner(tot[1] * s)).astype(BF16)
            o_ref[:, pl.ds(2 * AW + j * 128, 128)] = tot[2].astype(BF16)
        o_ref[:, pl.ds(3 * AW, AW)] = dz_ref[...].astype(BF16)

    tab = pl.BlockSpec((tm, 128), lambda i: (i, 0))
    in_specs, args = [], []
    for d, g in zip(DILATIONS, grads):
        in_specs += [_dil_spec(tm, d)] * 3
        args += list(g)
    return pl.pallas_call(
        body, name="rope_bwd", grid=(T // tm,),
        in_specs=in_specs + [pl.BlockSpec((tm, AW), lambda i: (i, 0)), tab, tab],
        out_specs=pl.BlockSpec((tm, 4 * AW), lambda i: (i, 0)),
        out_shape=jax.ShapeDtypeStruct((T, 4 * AW), BF16),
        scratch_shapes=_slab_scratch(6, tm),
        compiler_params=_cparams(("parallel",)),
    )(*args, dz, cos128, sin128)


def _dilate_cols(x, first):
    tm = ROPE_TM

    def body(x0, x1, x2, x3, o4, o16):
        xs = (x0, x1, x2, x3)
        for o_ref, d in ((o4, 4), (o16, 16)):
            for r in range(d):
                for j in range(SLABS):
                    o_ref[:, pl.ds(r * AW + j * 128, 128)] = xs[j][_rows(r, tm // d, d), :]

    return pl.pallas_call(
        body, name="dilate_cols", grid=(T // tm,),
        in_specs=_slab_specs(tm, first), out_specs=[_dil_spec(tm, 4), _dil_spec(tm, 16)],
        out_shape=[jax.ShapeDtypeStruct((T // 4, 4 * AW), F32), jax.ShapeDtypeStruct((T // 16, 16 * AW), F32)],
        compiler_params=_cparams(("parallel",)),
    )(x, x, x, x)


def _band_masks():
    qi = lax.broadcasted_iota(jnp.int32, (QBLK, QBLK), 0)
    kj = lax.broadcasted_iota(jnp.int32, (QBLK, QBLK), 1)
    return kj >= qi, kj <= qi


def _attn_fwd(q, k, v, d):
    L = q.shape[0]
    nb = L // QBLK

    def body(q_ref, kp_ref, kc_ref, vp_ref, vc_ref, o_ref, l_ref):
        n = pl.program_id(1)
        mask_p, mask_c = _band_masks()
        bias = jnp.concatenate([jnp.where(mask_p, 0.0, NEG) + jnp.where(n > 0, 0.0, NEG),
                                jnp.where(mask_c, 0.0, NEG)], axis=1)
        s = []
        for h in range(HEADS):
            sl = pl.ds(HD * h, HD)
            qh = q_ref[:, sl]
            s.append(jnp.concatenate([_nt(qh, kp_ref[:, sl]), _nt(qh, kc_ref[:, sl])], axis=1))
        s = jnp.stack(s) + bias
        m = jnp.max(s, axis=2, keepdims=True)
        e = jnp.exp(s - m)
        den = jnp.sum(e, axis=2, keepdims=True)
        p = e.astype(BF16)
        inv = 1.0 / den
        lse = m + jnp.log(den)
        for h in range(HEADS):
            sl = pl.ds(HD * h, HD)
            o_ref[:, sl] = (_nn(p[h, :, :QBLK], vp_ref[:, sl]) + _nn(p[h, :, QBLK:], vc_ref[:, sl])) * inv[h]
            l_ref[:, sl] = jnp.broadcast_to(lse[h], (QBLK, HD))

    cur = pl.BlockSpec((QBLK, AW), lambda r, n: (n, r))
    prev = pl.BlockSpec((QBLK, AW), lambda r, n: (jnp.maximum(n - 1, 0), r))
    return pl.pallas_call(
        body, name=f"attn_fwd_d{d}", grid=(d, nb),
        in_specs=[cur, prev, cur, prev, cur], out_specs=[cur, cur],
        out_shape=[jax.ShapeDtypeStruct((L, d * AW), F32)] * 2,
        compiler_params=_cparams(("parallel", "parallel")),
    )(q, k, k, v, v)


def _attn_bwd(q, k, v, do, at, lse, d):
    L = q.shape[0]
    nb = L // QBLK

    def body(q0_ref, q1_ref, kp_ref, kc_ref, vp_ref, vc_ref, do0_ref, do1_ref, at0_ref, at1_ref,
             l0_ref, l1_ref, dq_ref, dk_ref, dv_ref):
        n = pl.program_id(1)
        mask_p, mask_c = _band_masks()
        prev_bias = jnp.where(mask_p, 0.0, NEG)
        bias = jnp.concatenate([prev_bias + jnp.where(n > 0, 0.0, NEG), jnp.where(mask_c, 0.0, NEG),
                                prev_bias + jnp.where(n < nb - 1, 0.0, NEG)], axis=1)
        s, dp, ls, dl, ops = [], [], [], [], []
        for h in range(HEADS):
            sl = pl.ds(HD * h, HD)
            one = pl.ds(HD * h, 1)
            q0, q1 = q0_ref[:, sl], q1_ref[:, sl]
            kp, kc, vp, vc = kp_ref[:, sl], kc_ref[:, sl], vp_ref[:, sl], vc_ref[:, sl]
            do0, do1 = do0_ref[:, sl], do1_ref[:, sl]
            do0b, do1b = do0.astype(BF16), do1.astype(BF16)
            s.append(jnp.concatenate([_nt(q0, kp), _nt(q0, kc), _nt(q1, kc)], axis=1))
            dp.append(jnp.concatenate([_nt(do0b, vp), _nt(do0b, vc), _nt(do1b, vc)], axis=1))
            dl0 = jnp.sum(do0 * at0_ref[:, sl], axis=1, keepdims=True)
            dl1 = jnp.sum(do1 * at1_ref[:, sl], axis=1, keepdims=True)
            dl.append(jnp.concatenate([jnp.broadcast_to(dl0, (QBLK, 2 * QBLK)), jnp.broadcast_to(dl1, (QBLK, QBLK))], axis=1))
            ls.append(jnp.concatenate([jnp.broadcast_to(l0_ref[:, one], (QBLK, 2 * QBLK)),
                                       jnp.broadcast_to(l1_ref[:, one], (QBLK, QBLK))], axis=1))
            ops.append((q0, q1, kp, kc, do0b, do1b))
        p = jnp.exp(jnp.stack(s) + bias - jnp.stack(ls))
        ds = (p * (jnp.stack(dp) - jnp.stack(dl))).astype(BF16)
        p = p.astype(BF16)
        for h in range(HEADS):
            sl = pl.ds(HD * h, HD)
            q0, q1, kp, kc, do0b, do1b = ops[h]
            dq_ref[:, sl] = _nn(ds[h, :, :QBLK], kp) + _nn(ds[h, :, QBLK:2 * QBLK], kc)
            dv_ref[:, sl] = _tn(p[h, :, QBLK:2 * QBLK], do0b) + _tn(p[h, :, 2 * QBLK:], do1b)
            dk_ref[:, sl] = _tn(ds[h, :, QBLK:2 * QBLK], q0) + _tn(ds[h, :, 2 * QBLK:], q1)

    cur = pl.BlockSpec((QBLK, AW), lambda r, n: (n, r))
    prev = pl.BlockSpec((QBLK, AW), lambda r, n: (jnp.maximum(n - 1, 0), r))
    nxt = pl.BlockSpec((QBLK, AW), lambda r, n: (jnp.minimum(n + 1, nb - 1), r))
    return pl.pallas_call(
        body, name=f"attn_bwd_d{d}", grid=(d, nb),
        in_specs=[cur, nxt, prev, cur, prev, cur, cur, nxt, cur, nxt, cur, nxt], out_specs=[cur, cur, cur],
        out_shape=[jax.ShapeDtypeStruct((L, d * AW), F32)] * 3,
        compiler_params=_cparams(("parallel", "parallel")),
    )(q, q, k, k, v, v, do, do, at, at, lse, lse)


def _attn_merge(outs, lses):
    tm = ROPE_TM

    def body(o1, o4, o16, l1, l4, l16, at_ref, ls_ref, at4, ls4, at16, ls16, *flat):
        so4, so16, sl4, sl16, sa, sl = _slab_groups(flat)
        _undilate(o4, so4, 4, tm)
        _undilate(o16, so16, 16, tm)
        _undilate(l4, sl4, 4, tm)
        _undilate(l16, sl16, 16, tm)
        for j in range(SLABS):
            cols = pl.ds(j * 128, 128)
            a, b, c = l1[:, cols], sl4[j][...], sl16[j][...]
            m = jnp.maximum(jnp.maximum(a, b), c)
            e1, e2, e3 = jnp.exp(a - m), jnp.exp(b - m), jnp.exp(c - m)
            s = e1 + e2 + e3
            inv = 1.0 / s
            attn = (e1 * inv) * o1[:, cols] + (e2 * inv) * so4[j][...] + (e3 * inv) * so16[j][...]
            lse = m + jnp.log(s)
            at_ref[:, cols] = attn
            ls_ref[:, cols] = lse
            sa[j][...] = attn
            sl[j][...] = lse
        _dilate(at4, sa, 4, tm)
        _dilate(at16, sa, 16, tm)
        _dilate(ls4, sl, 4, tm)
        _dilate(ls16, sl, 16, tm)

    specs = [_dil_spec(tm, d) for d in DILATIONS]
    tok = specs[0]
    return pl.pallas_call(
        body, name="attn_merge", grid=(T // tm,),
        in_specs=specs + specs, out_specs=[tok, tok, specs[1], specs[1], specs[2], specs[2]],
        out_shape=[jax.ShapeDtypeStruct((T, AW), F32)] * 2 + [jax.ShapeDtypeStruct((T // 4, 4 * AW), F32)] * 2
        + [jax.ShapeDtypeStruct((T // 16, 16 * AW), F32)] * 2,
        scratch_shapes=_slab_scratch(6, tm),
        compiler_params=_cparams(("parallel",)),
    )(*outs, *lses)


CONV_TM = 512
HALO = 8


def _conv_pre(ext, w, b):
    y = b + w[3] * ext
    for kk in range(1, CONV_K):
        y = y + w[3 - kk] * pltpu.roll(ext, kk, 0)
    return y


def _rows_to_block(rows, n, width):
    ri = lax.broadcasted_iota(jnp.int32, (n, width), 0)
    out = jnp.zeros((n, width), F32)
    for j, r in enumerate(rows):
        out = out + jnp.where(ri == j, r, 0.0)
    return out


def _conv_fwd(xbc, w, b):
    nblk = T // CONV_TM

    def body(x_ref, h_ref, w_ref, b_ref, o_ref):
        i = pl.program_id(0)
        halo = jnp.where(i > 0, h_ref[...], 0.0)
        ext = jnp.concatenate([halo, x_ref[...]], axis=0)
        y = _conv_pre(ext, [w_ref[pl.ds(j, 1), :] for j in range(CONV_K)], b_ref[...])[HALO:]
        o_ref[...] = y * _sigmoid(y)

    return pl.pallas_call(
        body, name="conv_fwd", grid=(nblk,),
        in_specs=[pl.BlockSpec((CONV_TM, CONV_CH), lambda i: (i, 0)),
                  pl.BlockSpec((HALO, CONV_CH), lambda i: (jnp.maximum(i * (CONV_TM // HALO) - 1, 0), 0)),
                  pl.BlockSpec((CONV_K, CONV_CH), lambda i: (0, 0)),
                  pl.BlockSpec((1, CONV_CH), lambda i: (0, 0))],
        out_specs=pl.BlockSpec((CONV_TM, CONV_CH), lambda i: (i, 0)),
        out_shape=jax.ShapeDtypeStruct((T, CONV_CH), F32),
        compiler_params=_cparams(("parallel",)),
    )(xbc, xbc, w, b)


def _conv_bwd(xbc, dact, ddt, w, b):
    nblk = T // CONV_TM
    per = CONV_TM // HALO

    def body(x_ref, xb_ref, xa_ref, g_ref, ga_ref, ddt_ref, w_ref, b_ref, dx_ref, dw_ref):
        i = pl.program_id(0)
        wv = [w_ref[pl.ds(j, 1), :] for j in range(CONV_K)]
        before = jnp.where(i > 0, xb_ref[...], 0.0)
        last = i == nblk - 1
        after = jnp.where(last, 0.0, xa_ref[...])
        g_after = jnp.where(last, 0.0, ga_ref[...])
        ext = jnp.concatenate([before, x_ref[...], after], axis=0)
        y = _conv_pre(ext, wv, b_ref[...])[HALO:]
        sg = _sigmoid(y)
        dy = jnp.concatenate([g_ref[...], g_after], axis=0) * (sg * (1.0 + y * (1.0 - sg)))
        n = CONV_TM + HALO
        dx = wv[3] * dy
        for kk in range(1, CONV_K):
            dx = dx + wv[3 - kk] * pltpu.roll(dy, n - kk, 0)
        dx_ref[:, pl.ds(0, CONV_CH)] = dx[:CONV_TM].astype(BF16)
        dx_ref[:, pl.ds(CONV_CH, DT_PAD)] = ddt_ref[...].astype(BF16)
        dyc = dy[:CONV_TM]
        rows = [jnp.sum(dyc * (pltpu.roll(ext, 3 - j, 0) if j < 3 else ext)[HALO:HALO + CONV_TM], axis=0, keepdims=True)
                for j in range(CONV_K)]
        rows.append(jnp.sum(dyc, axis=0, keepdims=True))
        part = _rows_to_block(rows, 8, CONV_CH)

        @pl.when(i == 0)
        def _():
            dw_ref[...] = jnp.zeros_like(dw_ref)
        dw_ref[...] += part

    blk = pl.BlockSpec((CONV_TM, CONV_CH), lambda i: (i, 0))
    hb = pl.BlockSpec((HALO, CONV_CH), lambda i: (jnp.maximum(i * per - 1, 0), 0))
    ha = pl.BlockSpec((HALO, CONV_CH), lambda i: (jnp.minimum((i + 1) * per, T // HALO - 1), 0))
    return pl.pallas_call(
        body, name="conv_bwd", grid=(nblk,),
        in_specs=[blk, hb, ha, blk, ha, pl.BlockSpec((CONV_TM, DT_PAD), lambda i: (i, 0)),
                  pl.BlockSpec((CONV_K, CONV_CH), lambda i: (0, 0)), pl.BlockSpec((1, CONV_CH), lambda i: (0, 0))],
        out_specs=[pl.BlockSpec((CONV_TM, CONV_CH + DT_PAD), lambda i: (i, 0)), pl.BlockSpec((8, CONV_CH), lambda i: (0, 0))],
        out_shape=[jax.ShapeDtypeStruct((T, CONV_CH + DT_PAD), BF16), jax.ShapeDtypeStruct((8, CONV_CH), F32)],
        compiler_params=_cparams(("arbitrary",)),
    )(xbc, xbc, xbc, dact, dact, ddt, w, b)


def _pick(mat, h):
    lane = lax.broadcasted_iota(jnp.int32, mat.shape, 1)
    return jnp.sum(jnp.where(lane == h, mat, 0.0), axis=1, keepdims=True)


def _heads(fn):
    return jnp.stack([fn(h) for h in range(HEADS)])


def _ssd_prep(dt_ref, bias_ref, alog_ref, dsk_ref, b_ref, c_ref, xs_ref, state_ref, cst):
    li = lax.broadcasted_iota(jnp.int32, (CHUNK, CHUNK), 0)
    si = lax.broadcasted_iota(jnp.int32, (CHUNK, CHUNK), 1)
    tri = li >= si
    dtp = dt_ref[...] + bias_ref[...]
    dt = _softplus(dtp)
    A = -jnp.exp(alog_ref[...])
    a = dt * A
    cs = jnp.dot(tri.astype(F32), a, precision=HIGHEST, preferred_element_type=F32)
    cst[...] = cs.T
    Bm = b_ref[...].astype(BF16)
    Cm = c_ref[...].astype(BF16)
    cb = _nt(Cm, Bm)
    dskv = dsk_ref[...]
    cs_col = _heads(lambda h: _pick(cs, h))
    cs_row = _heads(lambda h: cst[pl.ds(h, 1), :])
    dt_col = _heads(lambda h: _pick(dt, h))
    dsk_col = _heads(lambda h: _pick(dskv, h))
    lam = jnp.exp(jnp.where(tri, cs_col - cs_row, NEG))
    x = _heads(lambda h: xs_ref[:, pl.ds(HD * h, HD)])
    xdt = x * dt_col
    prev = _heads(lambda h: state_ref[pl.ds(HD * h, HD), :])
    lane = lax.broadcasted_iota(jnp.int32, (1, 1, CHUNK), 2)
    cl = jnp.sum(jnp.where(lane == CHUNK - 1, cs_row, 0.0), axis=2, keepdims=True)
    f = jnp.exp(cl - cs_col)
    return dict(li=li, si=si, dtp=dtp, dt=dt, A=A, Bm=Bm, Cm=Cm, cb=cb, cs_col=cs_col, dt_col=dt_col, dsk_col=dsk_col,
                lam=lam, x=x, xdt=xdt, prev=prev, cl=cl, f=f)


def _ssd_fwd(act, xbcdt, bias, alog, dsk):
    nc = T // CHUNK

    def body(xs_ref, b_ref, c_ref, dt_ref, bias_ref, alog_ref, dsk_ref, y_ref, st_ref, state, cst):
        @pl.when(pl.program_id(0) == 0)
        def _():
            state[...] = jnp.zeros_like(state)
        st_ref[...] = state[...]
        s = _ssd_prep(dt_ref, bias_ref, alog_ref, dsk_ref, b_ref, c_ref, xs_ref, state, cst)
        Bm, Cm, prev = s["Bm"], s["Cm"], s["prev"]
        g = (s["cb"] * s["lam"]).astype(BF16)
        xdtb = s["xdt"].astype(BF16)
        prevb = prev.astype(BF16)
        y = _heads(lambda h: _nn(g[h], xdtb[h])) + _heads(lambda h: _nt(Cm, prevb[h])) * jnp.exp(s["cs_col"])
        y = y + s["dsk_col"] * s["x"]
        xf = (s["xdt"] * s["f"]).astype(BF16)
        new = prev * jnp.exp(s["cl"]) + _heads(lambda h: _tn(xf[h], Bm))
        for h in range(HEADS):
            y_ref[:, pl.ds(HD * h, HD)] = y[h]
            state[pl.ds(HD * h, HD), :] = new[h]

    vec = pl.BlockSpec((1, DT_PAD), lambda c: (0, 0))
    return pl.pallas_call(
        body, name="ssd_fwd", grid=(nc,),
        in_specs=[pl.BlockSpec((CHUNK, AW), lambda c: (c, 0)), pl.BlockSpec((CHUNK, NS), lambda c: (c, 4)),
                  pl.BlockSpec((CHUNK, NS), lambda c: (c, 5)), pl.BlockSpec((CHUNK, DT_PAD), lambda c: (c, 6)),
                  vec, vec, vec],
        out_specs=[pl.BlockSpec((CHUNK, AW), lambda c: (c, 0)), pl.BlockSpec((None, AW, NS), lambda c: (c, 0, 0))],
        out_shape=[jax.ShapeDtypeStruct((T, AW), F32), jax.ShapeDtypeStruct((nc, AW, NS), F32)],
        scratch_shapes=[pltpu.VMEM((AW, NS), F32), pltpu.VMEM((CHUNK, CHUNK), F32)],
        compiler_params=_cparams(("arbitrary",)),
    )(act, act, act, xbcdt, bias, alog, dsk)


def _ssd_bwd(act, xbcdt, bias, alog, dsk, states, dy):
    nc = T // CHUNK

    def body(xs_ref, b_ref, c_ref, dt_ref, bias_ref, alog_ref, dsk_ref, st_ref, dy_ref,
             dact_ref, ddt_ref, par_ref, dstate, cst):
        step = pl.program_id(0)

        @pl.when(step == 0)
        def _():
            dstate[...] = jnp.zeros_like(dstate)
            par_ref[...] = jnp.zeros_like(par_ref)
        s = _ssd_prep(dt_ref, bias_ref, alog_ref, dsk_ref, b_ref, c_ref, xs_ref, st_ref, cst)
        Bm, Cm, prev, lam, x, xdt, f, cl = s["Bm"], s["Cm"], s["prev"], s["lam"], s["x"], s["xdt"], s["f"], s["cl"]
        lane = lax.broadcasted_iota(jnp.int32, (1, DT_PAD), 1)
        row = lax.broadcasted_iota(jnp.int32, (1, CHUNK, 1), 1)
        g = s["cb"] * lam
        gb, xdtb, prevb = g.astype(BF16), xdt.astype(BF16), prev.astype(BF16)
        dy = _heads(lambda h: dy_ref[:, pl.ds(HD * h, HD)])
        dyb = dy.astype(BF16)
        dnew = _heads(lambda h: dstate[pl.ds(HD * h, HD), :])
        dnewb = dnew.astype(BF16)
        E = jnp.exp(s["cs_col"])
        ecl = jnp.exp(cl)
        dG = _heads(lambda h: _nt(dyb[h], xdtb[h]))
        dxdt = _heads(lambda h: _tn(gb[h], dyb[h]))
        Yo = _heads(lambda h: _nt(Cm, prevb[h]))
        W = _heads(lambda h: _nt(Bm, dnewb[h]))
        dcb = jnp.sum(dG * lam, axis=0)
        Mm = dG * g
        col_sums = jnp.sum(Mm, axis=1, keepdims=True)
        dYo = (dy * E).astype(BF16)
        dxdt = dxdt + W * f
        dF = jnp.sum(W * xdt, axis=2, keepdims=True) * f
        dcl = jnp.sum(dnew * prev, axis=(1, 2), keepdims=True) * ecl + jnp.sum(dF, axis=1, keepdims=True)
        dcs = (jnp.sum(Mm, axis=2, keepdims=True) + jnp.sum(dy * Yo, axis=2, keepdims=True) * E - dF
               + jnp.where(row == CHUNK - 1, dcl, 0.0))
        ddt_x = jnp.sum(dxdt * x, axis=2, keepdims=True)
        dD = jnp.sum(dy * x, axis=(1, 2), keepdims=True)
        dx = s["dsk_col"] * dy + dxdt * s["dt_col"]
        xfb = (xdt * f).astype(BF16)
        dprev = _heads(lambda h: _tn(dYo[h], Cm)) + dnew * ecl
        dcbb = dcb.astype(BF16)
        dC = _nn(dcbb, Bm)
        dB = _tn(dcbb, Cm)
        dcs_mat = -_rows_to_block([col_sums[h] for h in range(HEADS)], CHUNK, CHUNK).T
        ddt_mat = jnp.zeros((CHUNK, DT_PAD), F32)
        dD_row = jnp.zeros((1, DT_PAD), F32)
        for h in range(HEADS):
            sl = pl.ds(HD * h, HD)
            dC = dC + _nn(dYo[h], prevb[h])
            dB = dB + _nn(xfb[h], dnewb[h])
            dcs_mat = dcs_mat + jnp.where(lane == h, dcs[h], 0.0)
            ddt_mat = ddt_mat + jnp.where(lane == h, ddt_x[h], 0.0)
            dD_row = dD_row + jnp.where(lane == h, dD[h], 0.0)
            dact_ref[:, sl] = dx[h]
            dstate[sl, :] = dprev[h]
        dact_ref[:, pl.ds(AW, NS)] = dB
        dact_ref[:, pl.ds(AW + NS, NS)] = dC
        da = jnp.dot((s["li"] <= s["si"]).astype(F32), dcs_mat, precision=HIGHEST, preferred_element_type=F32)
        ddtp = jnp.where(lane < HEADS, (ddt_mat + da * s["A"]) * _sigmoid(s["dtp"]), 0.0)
        ddt_ref[...] = ddtp
        dalog = jnp.where(lane < HEADS, jnp.sum(da * s["dt"], axis=0, keepdims=True) * s["A"], 0.0)
        par_ref[...] += _rows_to_block([jnp.sum(ddtp, axis=0, keepdims=True), dalog, dD_row], 8, DT_PAD)

    vec = pl.BlockSpec((1, DT_PAD), lambda c: (0, 0))
    rev = lambda c: nc - 1 - c
    return pl.pallas_call(
        body, name="ssd_bwd", grid=(nc,),
        in_specs=[pl.BlockSpec((CHUNK, AW), lambda c: (rev(c), 0)), pl.BlockSpec((CHUNK, NS), lambda c: (rev(c), 4)),
                  pl.BlockSpec((CHUNK, NS), lambda c: (rev(c), 5)), pl.BlockSpec((CHUNK, DT_PAD), lambda c: (rev(c), 6)),
                  vec, vec, vec,
                  pl.BlockSpec((None, AW, NS), lambda c: (rev(c), 0, 0)), pl.BlockSpec((CHUNK, AW), lambda c: (rev(c), 0))],
        out_specs=[pl.BlockSpec((CHUNK, CONV_CH), lambda c: (rev(c), 0)), pl.BlockSpec((CHUNK, DT_PAD), lambda c: (rev(c), 0)),
                   pl.BlockSpec((8, DT_PAD), lambda c: (0, 0))],
        out_shape=[jax.ShapeDtypeStruct((T, CONV_CH), F32), jax.ShapeDtypeStruct((T, DT_PAD), F32),
                   jax.ShapeDtypeStruct((8, DT_PAD), F32)],
        scratch_shapes=[pltpu.VMEM((AW, NS), F32), pltpu.VMEM((CHUNK, CHUNK), F32)],
        compiler_params=_cparams(("arbitrary",)),
    )(act, act, act, xbcdt, bias, alog, dsk, states, dy)


def _place():
    return lax.axis_index("x"), lax.axis_index("y"), lax.axis_index("c")


def _slot(px, py, pc):
    return 4 * px + 2 * py + pc


def _all_gather(arrs, name):
    na = len(arrs)

    def body(*refs):
        ins, outs = refs[:na], refs[na:2 * na]
        send_sems, recv_sems, local_sems = refs[2 * na:]
        x, y, c = _place()
        me, sib = (x, y, c), (x, y, 1 - c)
        chips = [(1 - x, y), (x, 1 - y), (1 - x, 1 - y)]

        def copy(a, kk, block, to, src=None):
            dst = outs[a].at[_slot(*block)]
            return pltpu.make_async_remote_copy(
                src_ref=dst if src is None else src, dst_ref=dst,
                send_sem=send_sems.at[a, kk], recv_sem=recv_sems.at[a, kk], device_id=to, device_id_type=MESH)

        mine = [pltpu.make_async_copy(ins[a], outs[a].at[_slot(*me)], local_sems.at[a]) for a in range(na)]
        for cp in mine:
            cp.start()
        first = []
        for a in range(na):
            first.append(copy(a, 0, me, sib, src=ins[a]))
            first += [copy(a, 1 + j, me, (*chip, c), src=ins[a]) for j, chip in enumerate(chips)]
        for cp in first:
            cp.start()
        passed = []
        for j, chip in enumerate(chips):
            for a in range(na):
                copy(a, 1 + j, (*chip, c), me).wait_recv()
                fw = copy(a, 4 + j, (*chip, c), sib)
                fw.start()
                passed.append(fw)
        for a in range(na):
            copy(a, 0, sib, me).wait_recv()
            for j, chip in enumerate(chips):
                copy(a, 4 + j, (*chip, 1 - c), me).wait_recv()
        for cp in first + passed:
            cp.wait_send()
        for cp in mine:
            cp.wait()

    any_spec = pl.BlockSpec(memory_space=pl.ANY)
    return pl.pallas_call(
        body, name=name,
        in_specs=[any_spec] * na, out_specs=[any_spec] * na,
        out_shape=[jax.ShapeDtypeStruct((N_DEV,) + a.shape, a.dtype) for a in arrs],
        scratch_shapes=[pltpu.SemaphoreType.DMA((na, 7)), pltpu.SemaphoreType.DMA((na, 7)),
                        pltpu.SemaphoreType.DMA((na,))],
    )(*arrs)


def _reduce_scatter(part, name):
    _, r, C = part.shape

    def body(part_ref, out_ref, own, got_sib, chip_sum, got_ici, lsem, s1, r1, s2, r2):
        x, y, c = _place()
        chips = [(x, y), (1 - x, y), (x, 1 - y), (1 - x, 1 - y)]
        loc = [pltpu.make_async_copy(part_ref.at[_slot(*chips[kk], c)], own.at[kk], lsem.at[kk]) for kk in range(4)]
        d2d = [pltpu.make_async_remote_copy(
            src_ref=part_ref.at[_slot(*chips[kk], 1 - c)], dst_ref=got_sib.at[kk],
            send_sem=s1.at[kk], recv_sem=r1.at[kk], device_id=(x, y, 1 - c), device_id_type=MESH) for kk in range(4)]
        for cp in loc + d2d:
            cp.start()
        ici = [pltpu.make_async_remote_copy(
            src_ref=chip_sum.at[kk - 1], dst_ref=got_ici.at[kk - 1],
            send_sem=s2.at[kk - 1], recv_sem=r2.at[kk - 1], device_id=(*chips[kk], c), device_id_type=MESH)
            for kk in range(1, 4)]
        for kk in (1, 2, 3):
            loc[kk].wait()
            d2d[kk].wait_recv()
            chip_sum[kk - 1] = (own[kk].astype(F32) + got_sib[kk].astype(F32)).astype(BF16)
            ici[kk - 1].start()
        loc[0].wait()
        d2d[0].wait_recv()
        acc = own[0].astype(F32) + got_sib[0].astype(F32)
        for cp in ici:
            cp.wait_recv()
        out_ref[...] = ((acc + got_ici[0].astype(F32)) + got_ici[1].astype(F32)) + got_ici[2].astype(F32)
        for cp in d2d + ici:
            cp.wait_send()

    return pl.pallas_call(
        body, name=name,
        in_specs=[pl.BlockSpec(memory_space=pl.ANY)],
        out_specs=pl.BlockSpec(memory_space=pltpu.VMEM),
        out_shape=jax.ShapeDtypeStruct((r, C), F32),
        scratch_shapes=[pltpu.VMEM((4, r, C), BF16), pltpu.VMEM((4, r, C), BF16), pltpu.VMEM((3, r, C), BF16),
                        pltpu.VMEM((3, r, C), BF16),
                        pltpu.SemaphoreType.DMA((4,)), pltpu.SemaphoreType.DMA((4,)), pltpu.SemaphoreType.DMA((4,)),
                        pltpu.SemaphoreType.DMA((3,)), pltpu.SemaphoreType.DMA((3,))],
        compiler_params=pltpu.CompilerParams(vmem_limit_bytes=VMEM_LIMIT),
    )(part)


SLAB_ROWS = 24


def _all_reduce_small(parts, name):
    R, C = SLAB_ROWS, D
    n = len(parts)

    def body(*refs):
        in_refs = refs[:n]
        out_ref, slab, got, send_sems, recv_sems = refs[n:]
        slab[...] = jnp.zeros_like(slab)
        for ref, (arr, row) in zip(in_refs, parts):
            slab[pl.ds(row, arr.shape[0]), pl.ds(0, arr.shape[1])] = ref[...]
        x, y, c = _place()
        mine = _slot(x, y, c)
        copies = [pltpu.make_async_remote_copy(
            src_ref=slab, dst_ref=got.at[mine], send_sem=send_sems.at[kk], recv_sem=recv_sems.at[kk],
            device_id=peer, device_id_type=MESH) for kk, peer in enumerate(_peers(x, y, c))]
        for cp in copies:
            cp.start()
        got[mine] = slab[...]
        for cp in copies:
            cp.wait_recv()
        acc = got[0]
        for s in range(1, N_DEV):
            acc = acc + got[s]
        out_ref[...] = acc
        for cp in copies:
            cp.wait_send()

    vm = pl.BlockSpec(memory_space=pltpu.VMEM)
    return pl.pallas_call(
        body, name=name, in_specs=[vm] * n, out_specs=vm,
        out_shape=jax.ShapeDtypeStruct((R, C), F32),
        scratch_shapes=[pltpu.VMEM((R, C), F32), pltpu.VMEM((N_DEV, R, C), F32), pltpu.SemaphoreType.DMA((N_DEV - 1,)),
                        pltpu.SemaphoreType.DMA((N_DEV - 1,))],
    )(*[a for a, _ in parts])


_HBM = pl.BlockSpec(memory_space=pltpu.HBM)
_SEM = pl.BlockSpec(memory_space=pltpu.SEMAPHORE)
_EFFECT = pltpu.SideEffectType.DATAFLOW_SIDE_EFFECTING


def _peers(x, y, c):
    out = []
    for kk in range(1, N_DEV):
        fx, fy, fc = kk >> 2 & 1, kk >> 1 & 1, kk & 1
        out.append((1 - x if fx else x, 1 - y if fy else y, 1 - c if fc else c))
    return out


def _send_start(src, per_peer, name, dep):
    (handles, token) = _send_start_many([src], per_peer, name, dep)
    return handles, token


def _send_start_many(srcs, per_peer, name, dep):
    n = len(srcs)

    def body(*refs):
        src_refs, land_refs = refs[:n], refs[n:2 * n]
        send_sems, recv_sems = refs[2 * n + 1], refs[2 * n + 2]
        token = refs[-1]
        x, y, c = _place()
        mine = _slot(x, y, c)
        for a in range(n):
            for kk, peer in enumerate(_peers(x, y, c)):
                pltpu.make_async_remote_copy(
                    src_ref=src_refs[a].at[_slot(*peer)] if per_peer else src_refs[a], dst_ref=land_refs[a].at[mine],
                    send_sem=send_sems.at[a * (N_DEV - 1) + kk], recv_sem=recv_sems.at[a * (N_DEV - 1) + kk],
                    device_id=peer, device_id_type=MESH).start()
        token[...] = jnp.zeros_like(token)

    lands = [lax.empty((N_DEV,) + tuple(s.shape[1:] if per_peer else s.shape), s.dtype) for s in srcs]
    hbm = lambda t: pltpu.with_memory_space_constraint(t, pltpu.HBM)
    outs = pl.pallas_call(
        body, name=name,
        out_shape=(pltpu.SemaphoreType.DMA((n * (N_DEV - 1),)), pltpu.SemaphoreType.DMA((n * (N_DEV - 1),)),
                   *[pltpu.HBM(s.shape, s.dtype) for s in srcs], *[pltpu.HBM(l.shape, l.dtype) for l in lands],
                   jax.ShapeDtypeStruct((8, 128), F32)),
        in_specs=(*[_HBM] * (2 * n), _ANY),
        out_specs=(_SEM, _SEM, *[_HBM] * (2 * n), pl.BlockSpec(memory_space=pltpu.VMEM)),
        input_output_aliases={i: 2 + i for i in range(2 * n)},
        compiler_params=pltpu.CompilerParams(has_side_effects=_EFFECT),
    )(*[hbm(s) for s in srcs], *[hbm(l) for l in lands], dep)
    return (outs[0], outs[1], list(outs[2:2 + n]), list(outs[2 + n:2 + 2 * n])), outs[-1]


def _send_wait(handles, after, name):
    srcs, lands = _send_wait_many(handles, after, name)
    return srcs[0], lands[0]


def _send_wait_many(handles, after, name):
    send_sems, recv_sems, src_thrus, land_thrus = handles
    n = len(src_thrus)

    def body(*refs):
        land_refs = refs[n:2 * n]
        send_sems, recv_sems = refs[2 * n], refs[2 * n + 1]
        me = _place()
        for a in range(n):
            for kk in range(N_DEV - 1):
                cp = pltpu.make_async_remote_copy(
                    src_ref=land_refs[a].at[0], dst_ref=land_refs[a].at[0],
                    send_sem=send_sems.at[a * (N_DEV - 1) + kk], recv_sem=recv_sems.at[a * (N_DEV - 1) + kk],
                    device_id=me, device_id_type=MESH)
                cp.wait_send()
                cp.wait_recv()

    both = list(src_thrus) + list(land_thrus)
    outs = pl.pallas_call(
        body, name=name,
        out_shape=tuple(pltpu.HBM(t.shape, t.dtype) for t in both),
        in_specs=(*[_HBM] * (2 * n), _SEM, _SEM, _ANY), out_specs=tuple([_HBM] * (2 * n)),
        input_output_aliases={i: i for i in range(2 * n)},
        compiler_params=pltpu.CompilerParams(has_side_effects=_EFFECT),
    )(*both, send_sems, recv_sems, after)
    return list(outs[:n]), list(outs[n:])


def _sum_slots(land, name):
    _, R, C = land.shape
    tm = R if R <= 512 else 512

    def body(x_ref, o_ref):
        acc = x_ref[0].astype(F32)
        for j in range(1, N_DEV):
            acc = acc + x_ref[j].astype(F32)
        o_ref[...] = acc

    return pl.pallas_call(
        body, name=name, grid=(R // tm,),
        in_specs=[pl.BlockSpec((N_DEV, tm, C), lambda i: (0, i, 0))], out_specs=pl.BlockSpec((tm, C), lambda i: (i, 0)),
        out_shape=jax.ShapeDtypeStruct((R, C), F32), compiler_params=_cparams(("parallel",)),
    )(land)


def _adam_math(w, g, m, v):
    m2 = ADAM_B1 * m + (1.0 - ADAM_B1) * g
    v2 = ADAM_B2 * v + (1.0 - ADAM_B2) * (g * g)
    m_hat = m2 / (1.0 - ADAM_B1 ** ADAM_STEP)
    v_hat = v2 / (1.0 - ADAM_B2 ** ADAM_STEP)
    delta = -ADAM_LR * (m_hat / (jnp.sqrt(v_hat) + ADAM_EPS) + ADAM_WD * w)
    return delta, m2, v2


def _adamw(w, g, m, v, name):
    R, C = w.shape
    tm = R if R <= 512 else 256
    return _rowwise(lambda w, g, m, v: (_adam_math(w, g, m, v), ()), [w, g, m, v], [], [(C, F32)] * 3, [], tm=tm, name=name)


def _adamw_small(slab, slab_rows, g_conv_w, ws, ms, vs):
    n = len(ws)

    def body(*refs):
        slab_ref, gc_ref = refs[0], refs[1]
        w_refs, m_refs, v_refs = refs[2:2 + n], refs[2 + n:2 + 2 * n], refs[2 + 2 * n:2 + 3 * n]
        outs = refs[2 + 3 * n:]
        loss_ref = outs[0]
        g_out, d_out, m_out, v_out = (outs[1 + i * n:1 + (i + 1) * n] for i in range(4))
        loss_ref[...] = jnp.sum(slab_ref[pl.ds(6, 1), :], axis=1, keepdims=True)
        for i in range(n):
            g = gc_ref[...] if i == n - 1 else slab_ref[pl.ds(slab_rows[i], 1), pl.ds(0, ws[i].shape[1])]
            d, m2, v2 = _adam_math(w_refs[i][...], g, m_refs[i][...], v_refs[i][...])
            g_out[i][...] = g
            d_out[i][...] = d
            m_out[i][...] = m2
            v_out[i][...] = v2

    vm = pl.BlockSpec(memory_space=pltpu.VMEM)
    shapes = [jax.ShapeDtypeStruct(w.shape, F32) for w in ws]
    outs = pl.pallas_call(
        body, name="adamw_small", in_specs=[vm] * (2 + 3 * n), out_specs=[vm] * (1 + 4 * n),
        out_shape=[jax.ShapeDtypeStruct((1, 1), F32)] + shapes * 4,
    )(slab, g_conv_w, *ws, *ms, *vs)
    return outs[0], outs[1:1 + n], outs[1 + n:1 + 2 * n], outs[1 + 2 * n:1 + 3 * n], outs[1 + 3 * n:]


SMALL = ["norm_mix_pre", "norm_mix_post", "norm_mlp_pre", "norm_mlp_post", "norm_ple_post",
         "conv_b", "ssd_norm_g", "dt_bias", "a_log", "d_skip"]


def _pad_row(v, width=D):
    return jnp.pad(v, ((0, 0), (0, width - v.shape[1])))


def kernel(x, p, positions, norm_mix_pre, norm_mix_post, w_in, conv_w, conv_b, dt_bias, a_log, d_skip, ssd_norm_g, w_out, norm_mlp_pre, norm_mlp_post, w_up, w_down, w_ple_gate, w_ple_proj, norm_ple_post, loss_target, m_norm_mix_pre, m_norm_mix_post, m_w_in, m_conv_w, m_conv_b, m_dt_bias, m_a_log, m_d_skip, m_ssd_norm_g, m_w_out, m_norm_mlp_pre, m_norm_mlp_post, m_w_up, m_w_down, m_w_ple_gate, m_w_ple_proj, m_norm_ple_post, v_norm_mix_pre, v_norm_mix_post, v_w_in, v_conv_w, v_conv_b, v_dt_bias, v_a_log, v_d_skip, v_ssd_norm_g, v_w_out, v_norm_mlp_pre, v_norm_mlp_post, v_w_up, v_w_down, v_w_ple_gate, v_w_ple_proj, v_norm_ple_post):
    args = dict(locals())
    x2, p2, tgt = x[0], p[0, 0], loss_target[0]
    g1, g2, g3, g4, g5 = norm_mix_pre, norm_mix_post, norm_mlp_pre, norm_mlp_post, norm_ple_post

    me = _slot(*_place())
    pack_in = jnp.pad(w_in[0].T, ((0, W_IN_SHARD_PAD - W_IN_SHARD), (0, 0))).astype(BF16)
    rest = [w_out[0].astype(BF16), w_up[0].T.astype(BF16), w_down[0].astype(BF16), w_ple_gate[0].astype(BF16),
            w_ple_proj[0].T.reshape(32, D).astype(BF16)]
    conv_pack = jnp.pad(conv_w[0], ((0, 4), (0, 32)))
    gin, gconv = _all_gather([pack_in, conv_pack], "gather_w_in")
    rest_handles, tok_rest = _send_start_many(rest, False, "gather_rest_start", gconv)
    w_inT = gin[:, :W_IN_SHARD].reshape(IN_W, D)
    w_qkvzT = w_inT[:4 * AW]
    w_xbcdtT = jnp.pad(w_inT[4 * AW:], ((0, DT_PAD - HEADS), (0, 0)))
    conv_full = gconv[:, :CONV_K, :96].transpose(1, 0, 2).reshape(CONV_K, CONV_CH)

    inv_freq = ROPE_THETA ** (-jnp.arange(HD // 2, dtype=F32) * 2.0 / HD)
    ang = positions[0].astype(F32)[:, None] * inv_freq
    cos, sin = jnp.cos(ang), jnp.sin(ang)
    cos128 = jnp.concatenate([cos, cos, cos, cos], axis=1)
    sin128 = jnp.concatenate([-sin, sin, -sin, sin], axis=1)

    bias_w, alog_w, dsk_w = _pad_row(dt_bias, DT_PAD), _pad_row(a_log, DT_PAD), _pad_row(d_skip, DT_PAD)
    rms_pre = lambda a, r, g: a * r * g

    (u1,) = _rowwise(lambda a, g: ((a * _rstd(a) * g,), ()), [x2], [g1], [(D, BF16)], [], tm=512, name="norm_x")
    p2b = p2.astype(BF16)
    qkvz = _mm(u1, w_qkvzT, tb=True, tm=512, tn=1024, tk=1024, name="proj_qkvz", deps=[tok_rest])
    xbcdt = _mm(u1, w_xbcdtT, tb=True, tm=512, tn=896, tk=1024, name="proj_xbcdt")

    qkv = _rope_fwd(qkvz, cos128, sin128)
    qkv = [qkv[3 * i:3 * i + 3] for i in range(len(DILATIONS))]
    outs, lses = [], []
    for d, (qd, kd, vd) in zip(DILATIONS, qkv):
        o, l = _attn_fwd(qd, kd, vd, d)
        outs.append(o)
        lses.append(l)
    attn, lse, attn4, lse4, attn16, lse16 = _attn_merge(outs, lses)

    act = _conv_fwd(xbcdt, conv_full, conv_b)
    y_ssd, states = _ssd_fwd(act, xbcdt, bias_w, alog_w, dsk_w)

    def gated_fwd(y, z, a, gs):
        gi = y * (z * _sigmoid(z))
        return (jnp.concatenate([a, gi * _rstd(gi) * gs], axis=1),), ()
    (cat,) = _rowwise(gated_fwd, [y_ssd, (qkvz, AW, 3), attn], [ssd_norm_g], [(D, BF16)], [], tm=512, name="gated_norm")

    rest_back, landed = _send_wait_many(rest_handles, cat, "gather_rest_wait")
    landed = [lax.dynamic_update_slice(l, b[None], (me, 0, 0)) for l, b in zip(landed, rest_back)]
    w_o, w_upT, w_dn, w_gate = landed[0].reshape(D, D), landed[1].reshape(DFF, D), landed[2].reshape(DFF, D), landed[3].reshape(D, D)
    w_projT = landed[4].reshape(D, PLE)

    mix = _mm(cat, w_o, tm=512, tn=1024, tk=1024, name="mix_out")

    def post1(xx, mm, ga, gb):
        h = xx + mm * _rstd(mm) * ga
        return (h, _rstd(h)), ()
    h1, r3 = _rowwise(post1, [x2, mix], [g2, g3], [(D, F32), (1, F32)], [], tm=512, name="post_mix")

    a_up, ff, u2, h2, h2b = _mlp_fwd(h1, r3, g3, w_upT, w_dn, g4)
    relu2 = lambda a: jnp.square(jnp.maximum(a.astype(F32), 0.0))

    gp = _mm(h2b, w_gate, tm=512, tn=1024, tk=1024, name="ple_gate")
    pp = _mm(p2b, w_projT, tb=True, tm=512, tn=1024, tk=256, name="ple_proj")

    def final(hh, gpre, ppv, tg, g):
        sg = _sigmoid(gpre)
        ple = ppv * sg
        r = _rstd(ple)
        n = ple * r
        h3 = hh + n * g
        e = h3 - tg
        dh3 = e * (1.0 / D)
        dple = _rms_bwd(n, r, g, dh3)
        return (dh3, dple * sg, dple * ppv * sg * (1.0 - sg)), (_colsum(dh3 * n), _colsum(0.5 * e * e * (1.0 / D)))
    dh3, dpp, dgp, dg5, loss_vec = _rowwise(final, [h2, gp, pp, tgt], [g5], [(D, F32), (D, BF16), (D, BF16)],
                                            [(1, D), (1, D)], tm=512, name="loss_ple_bwd")

    gw_projT = _mm(dpp, p2b, ta=True, tm=512, tn=256, tk=1024, out_dtypes=(BF16,), name="gw_ple_proj")
    gw_gate = _mm(h2b, dgp, ta=True, tm=512, tn=1024, tk=1024, out_dtypes=(BF16,), name="gw_ple_gate")
    rs_proj, tok_proj = _send_start(gw_projT.reshape(N_DEV, 32, D), True, "rs_start_w_proj", g1)
    rs_gate, tok_gate = _send_start(gw_gate.reshape(N_DEV, 128, D), True, "rs_start_w_gate", g1)
    dh2_g = _mm(dgp, w_gate, tb=True, tm=512, tn=1024, tk=1024, name="dx_ple_gate", deps=[tok_proj, tok_gate])

    def bwd_mlp_post(d3, dg_, f, g):
        dh2 = d3 + dg_
        r = _rstd(f)
        n = f * r
        return (dh2, _rms_bwd(n, r, g, dh2)), (_colsum(dh2 * n),)
    dh2, dff, dg4 = _rowwise(bwd_mlp_post, [dh3, dh2_g, ff], [g4], [(D, F32), (D, BF16)], [(1, D)], tm=512,
                             name="bwd_post_mlp")

    gw_dn = _mm(a_up, dff, ta=True, tm=1024, tn=1024, tk=1024, a_pre=relu2, out_dtypes=(BF16,), name="gw_mlp_down")
    rs_dn, tok_dn = _send_start(gw_dn.reshape(N_DEV, 512, D), True, "rs_start_w_down", g1)
    da_up, du2 = _mlp_dx(dff, a_up, w_upT, w_dn, tok_dn)
    gw_upT = _mm(da_up, u2, ta=True, tm=1024, tn=1024, tk=1024, out_dtypes=(BF16,), name="gw_mlp_up")
    rs_up, tok_up = _send_start(gw_upT.reshape(N_DEV, 512, D), True, "rs_start_w_up", g1)

    def bwd_mix_post(d2, du, hh, rr, mm, ga, gb):
        n3 = hh * rr
        dh1 = d2 + _rms_bwd(n3, rr, gb, du)
        r = _rstd(mm)
        n2 = mm * r
        return (dh1, _rms_bwd(n2, r, ga, dh1)), (_colsum(du * n3), _colsum(dh1 * n2))
    dh1, dmix, dg3, dg2 = _rowwise(bwd_mix_post, [dh2, du2, h1, r3, mix], [g2, g3], [(D, F32), (D, BF16)],
                                   [(1, D), (1, D)], tm=512, name="bwd_post_mix", deps=[tok_up])

    gw_o = _mm(cat, dmix, ta=True, tm=512, tn=1024, tk=1024, out_dtypes=(BF16,), name="gw_out")
    rs_o, tok_o = _send_start(gw_o.reshape(N_DEV, 128, D), True, "rs_start_w_out", g1)
    dcat = _mm(dmix, w_o, tb=True, tm=512, tn=1024, tk=1024, name="dx_out", deps=[tok_o])

    def gated_bwd(y, z, dyn, gs):
        sg = _sigmoid(z)
        sz = z * sg
        gi = y * sz
        r = _rstd(gi)
        n = gi * r
        dgi = _rms_bwd(n, r, gs, dyn)
        return (dgi * sz, dgi * y * (sg * (1.0 + z * (1.0 - sg)))), (_colsum(dyn * n),)
    dy_ssd, dz, dgs = _rowwise(gated_bwd, [y_ssd, (qkvz, AW, 3), (dcat, AW, 1)], [ssd_norm_g], [(AW, F32)] * 2, [(1, AW)],
                               tm=512, name="bwd_gated_norm")

    dact, ddtw, ssd_par = _ssd_bwd(act, xbcdt, bias_w, alog_w, dsk_w, states, dy_ssd)
    dxbcdt, conv_par = _conv_bwd(xbcdt, dact, ddtw, conv_full, conv_b)

    dattn4, dattn16 = _dilate_cols(dcat, 0)
    qkv_grads = [_attn_bwd(*qkv[0], dcat, attn, lse, 1),
                 _attn_bwd(*qkv[1], dattn4, attn4, lse4, 4),
                 _attn_bwd(*qkv[2], dattn16, attn16, lse16, 16)]
    dqkvz = _rope_bwd(qkv_grads, dz, cos128, sin128)

    gw_qkvzT = _mm(dqkvz, u1, ta=True, tm=1024, tn=1024, tk=1024, out_dtypes=(BF16,), name="gw_qkvz")
    gw_xbcdtT = _mm(dxbcdt, u1, ta=True, tm=896, tn=1024, tk=1024, out_dtypes=(BF16,), name="gw_xbcdt")
    gw_inT = jnp.concatenate([gw_qkvzT, gw_xbcdtT], axis=0)[:IN_W]
    gw_inT = jnp.pad(gw_inT.reshape(N_DEV, W_IN_SHARD, D), ((0, 0), (0, W_IN_SHARD_PAD - W_IN_SHARD), (0, 0)))
    rs_in, tok_in = _send_start(gw_inT, True, "rs_start_w_in", g1)

    du1a = _mm(dqkvz, w_qkvzT, tm=512, tn=1024, tk=2048, name="dx_qkvz", deps=[tok_in])
    du1b = _mm(dxbcdt, w_xbcdtT, tm=512, tn=1024, tk=896, name="dx_xbcdt")

    def bwd_in(d1, ua, ub, xx, g):
        rr = _rstd(xx)
        n = xx * rr
        du = ua + ub
        return (d1 + _rms_bwd(n, rr, g, du),), (_colsum(du * n),)
    grad_x, dg1 = _rowwise(bwd_in, [dh1, du1a, du1b, x2], [g1], [(D, F32)], [(1, D)], tm=512, name="bwd_pre_mix")

    slab = _all_reduce_small([(dg1, 0), (dg2, 1), (dg3, 2), (dg4, 3), (dg5, 4), (dgs, 5), (loss_vec, 6),
                              (conv_par, 8), (ssd_par, 16)], "reduce_small")
    g_conv_w = lax.dynamic_slice(slab[8:12, :CONV_CH], (0, me * 96), (CONV_K, 96))

    def scatter_finish(handles, nm, after):
        part, land = _send_wait(handles, after, "rs_wait_" + nm)
        own = lax.dynamic_slice(part, (me, 0, 0), (1,) + part.shape[1:])
        return _sum_slots(lax.dynamic_update_slice(land, own, (me, 0, 0)), "rs_sum_" + nm)
    g_out = scatter_finish(rs_o, "w_out", slab)
    g_upT = scatter_finish(rs_up, "w_up", slab)
    g_dn = scatter_finish(rs_dn, "w_down", slab)
    g_gate = scatter_finish(rs_gate, "w_gate", slab)
    g_projT = scatter_finish(rs_proj, "w_proj", slab)

    small_names = SMALL + ["conv_w"]
    small_rows = [0, 1, 2, 3, 4, 12, 5, 16, 17, 18, None]
    pick = lambda prefix: [args[prefix + nme] for nme in SMALL] + [args[prefix + "conv_w"][0]]
    loss11, g_s, d_s, m_s, v_s = _adamw_small(slab, small_rows, g_conv_w, pick(""), pick("m_"), pick("v_"))
    loss = loss11[0, 0]
    grads = {
        "w_out": g_out[None], "w_up": g_upT.T[None], "w_down": g_dn[None],
        "w_ple_gate": g_gate[None], "w_ple_proj": g_projT.reshape(128, PLE).T[None],
    }
    delta, new_m, new_v = {}, {}, {}
    for i, nme in enumerate(small_names):
        lead = (lambda t: t[None]) if nme == "conv_w" else (lambda t: t)
        grads[nme], delta[nme], new_m[nme], new_v[nme] = lead(g_s[i]), lead(d_s[i]), lead(m_s[i]), lead(v_s[i])
    for nme in ["w_out", "w_up", "w_down", "w_ple_gate", "w_ple_proj", "w_in"]:
        if nme == "w_in":
            g_inT = scatter_finish(rs_in, "w_in", delta["w_down"])
            grads["w_in"] = g_inT[:W_IN_SHARD].T[None]
        dl, mm_, vv_ = _adamw(args[nme][0], grads[nme][0], args["m_" + nme][0], args["v_" + nme][0], "adamw_" + nme)
        delta[nme], new_m[nme], new_v[nme] = dl[None], mm_[None], vv_[None]

    order = ["norm_mix_pre", "norm_mix_post", "w_in", "conv_w", "conv_b", "dt_bias", "a_log", "d_skip", "ssd_norm_g",
             "w_out", "norm_mlp_pre", "norm_mlp_post", "w_up", "w_down", "w_ple_gate", "w_ple_proj", "norm_ple_post"]
    return (loss, grad_x[None], *[grads[n] for n in order], *[delta[n] for n in order],
            *[new_m[n] for n in order], *[new_v[n] for n in order])
```

```python
import functools
import math

import jax
import jax.numpy as jnp
from jax import lax
from jax.experimental import pallas as pl
from jax.experimental.pallas import tpu as pltpu

F32 = jnp.float32
BF16 = jnp.bfloat16
MESH = pl.DeviceIdType.MESH
HIGHEST = lax.Precision.HIGHEST

N_DEV = 8
T = 4096
D = 1024
HEADS = 8
HD = 64
AW = 512
NS = 128
CONV_K = 4
CONV_CH = 768
CHUNK = 128
DFF = 4096
PLE = 256
EPS = 1e-6
ROPE_THETA = 10000.0
DILATIONS = (1, 4, 16)
QBLK = 128
NEG = -1e30
IN_W = 2824
W_IN_SHARD = 353
W_IN_SHARD_PAD = 384
DT_PAD = 128

ADAM_LR, ADAM_B1, ADAM_B2, ADAM_EPS, ADAM_WD, ADAM_STEP = 0.001, 0.9, 0.999, 1e-08, 0.01, 10

VMEM_LIMIT = 56 * 1024 * 1024


_ANY = pl.BlockSpec(memory_space=pl.ANY)


def _cparams(sem=None):
    return pltpu.CompilerParams(dimension_semantics=sem, vmem_limit_bytes=VMEM_LIMIT)


def _dot(a, b, ca, cb, precision=None):
    return lax.dot_general(a, b, (((ca,), (cb,)), ((), ())), preferred_element_type=F32, precision=precision)


def _nn(a, b):
    return _dot(a, b, 1, 0)


def _nt(a, b):
    return _dot(a, b, 1, 1)


def _tn(a, b):
    return _dot(a, b, 0, 0)


def _sigmoid(x):
    return 1.0 / (1.0 + jnp.exp(-x))


def _softplus(x):
    return jnp.maximum(x, 0.0) + jnp.log(1.0 + jnp.exp(-jnp.abs(x)))


def _mm(a, b, *, ta=False, tb=False, tm, tn, tk, name,
        a_pre=None, a_rows=(), a_cols=(), b_pre=None, b_rows=(), b_cols=(),
        epi=None, epi_tiles=(), out_dtypes=(F32,), deps=()):
    if ta:
        K, M = a.shape
    else:
        M, K = a.shape
    if tb:
        N, K2 = b.shape
    else:
        K2, N = b.shape
    assert K == K2 and M % tm == 0 and N % tn == 0 and K % tk == 0, (name, a.shape, b.shape)
    nk = K // tk
    if ta:
        a_spec = pl.BlockSpec((tk, tm), lambda i, j, k: (k, i))
        a_row_specs = [pl.BlockSpec((tk, 1), lambda i, j, k: (k, 0)) for _ in a_rows]
        a_col_specs = [pl.BlockSpec((1, tm), lambda i, j, k: (0, i)) for _ in a_cols]
    else:
        a_spec = pl.BlockSpec((tm, tk), lambda i, j, k: (i, k))
        a_row_specs = [pl.BlockSpec((tm, 1), lambda i, j, k: (i, 0)) for _ in a_rows]
        a_col_specs = [pl.BlockSpec((1, tk), lambda i, j, k: (0, k)) for _ in a_cols]
    if tb:
        b_spec = pl.BlockSpec((tn, tk), lambda i, j, k: (j, k))
        b_row_specs = [pl.BlockSpec((tn, 1), lambda i, j, k: (j, 0)) for _ in b_rows]
        b_col_specs = [pl.BlockSpec((1, tk), lambda i, j, k: (0, k)) for _ in b_cols]
    else:
        b_spec = pl.BlockSpec((tk, tn), lambda i, j, k: (k, j))
        b_row_specs = [pl.BlockSpec((tk, 1), lambda i, j, k: (k, 0)) for _ in b_rows]
        b_col_specs = [pl.BlockSpec((1, tn), lambda i, j, k: (0, j)) for _ in b_cols]
    o_spec = pl.BlockSpec((tm, tn), lambda i, j, k: (i, j))
    na, nb, ne, no = len(a_rows) + len(a_cols), len(b_rows) + len(b_cols), len(epi_tiles), len(out_dtypes)

    def body(*refs):
        a_ref, b_ref = refs[0], refs[1]
        a_ex = refs[2:2 + na]
        b_ex = refs[2 + na:2 + na + nb]
        e_ex = refs[2 + na + nb:2 + na + nb + ne]
        first_out = 2 + na + nb + ne + len(deps)
        outs = refs[first_out:first_out + no]

        def finish(res):
            vals = epi(res, *[r[...] for r in e_ex]) if epi is not None else (res,)
            for o_ref, val in zip(outs, vals):
                o_ref[...] = val.astype(o_ref.dtype)

        at = a_ref[...]
        if a_pre is not None:
            at = a_pre(at, *[r[...] for r in a_ex])
        bt = b_ref[...]
        if b_pre is not None:
            bt = b_pre(bt, *[r[...] for r in b_ex])
        prod = _dot(at.astype(BF16), bt.astype(BF16), 0 if ta else 1, 1 if tb else 0)
        if nk == 1:
            finish(prod)
            return
        acc = refs[-1]
        k = pl.program_id(2)

        @pl.when(k == 0)
        def _():
            acc[...] = jnp.zeros_like(acc)
        acc[...] += prod

        @pl.when(k == nk - 1)
        def _():
            finish(acc[...])

    outs = pl.pallas_call(
        body, name=name,
        grid=(M // tm, N // tn, nk),
        in_specs=([a_spec, b_spec] + a_row_specs + a_col_specs + b_row_specs + b_col_specs + [o_spec] * ne
                  + [_ANY] * len(deps)),
        out_specs=[o_spec] * no,
        out_shape=[jax.ShapeDtypeStruct((M, N), dt) for dt in out_dtypes],
        scratch_shapes=[pltpu.VMEM((tm, tn), F32)] if nk > 1 else [],
        compiler_params=_cparams(("parallel", "parallel", "arbitrary")),
    )(a, b, *a_rows, *a_cols, *b_rows, *b_cols, *epi_tiles, *deps)
    return outs[0] if no == 1 else outs


MLP_TM = 512
MLP_TC = 1024


def _mlp_fwd(h, r, g, w_upT, w_dn, g_post):
    nc = DFF // MLP_TC

    def body(h_ref, r_ref, g_ref, wu_ref, wd_ref, gp_ref, a_ref, ff_ref, u_ref, ho_ref, hob_ref, acc, u_scr):
        c = pl.program_id(1)

        @pl.when(c == 0)
        def _():
            u = (h_ref[...] * r_ref[...] * g_ref[...]).astype(BF16)
            u_scr[...] = u
            u_ref[...] = u
            acc[...] = jnp.zeros_like(acc)
        a = _nt(u_scr[...], wu_ref[...])
        a_ref[...] = a.astype(BF16)
        acc[...] += _nn(jnp.square(jnp.maximum(a, 0.0)).astype(BF16), wd_ref[...])

        @pl.when(c == nc - 1)
        def _():
            f = acc[...]
            ff_ref[...] = f
            ho = h_ref[...] + f * _rstd(f) * gp_ref[...]
            ho_ref[...] = ho
            hob_ref[...] = ho.astype(BF16)

    row = pl.BlockSpec((MLP_TM, D), lambda i, c: (i, 0))
    wsp = pl.BlockSpec((MLP_TC, D), lambda i, c: (c, 0))
    vec = pl.BlockSpec((1, D), lambda i, c: (0, 0))
    return pl.pallas_call(
        body, name="mlp_fwd", grid=(T // MLP_TM, nc),
        in_specs=[row, pl.BlockSpec((MLP_TM, 1), lambda i, c: (i, 0)), vec, wsp, wsp, vec],
        out_specs=[pl.BlockSpec((MLP_TM, MLP_TC), lambda i, c: (i, c)), row, row, row, row],
        out_shape=[jax.ShapeDtypeStruct((T, DFF), BF16), jax.ShapeDtypeStruct((T, D), F32), jax.ShapeDtypeStruct((T, D), BF16),
                   jax.ShapeDtypeStruct((T, D), F32), jax.ShapeDtypeStruct((T, D), BF16)],
        scratch_shapes=[pltpu.VMEM((MLP_TM, D), F32), pltpu.VMEM((MLP_TM, D), BF16)],
        compiler_params=_cparams(("parallel", "arbitrary")),
    )(h, r, g, w_upT, w_dn, g_post)


def _mlp_dx(dff, a, w_upT, w_dn, dep):
    nc = DFF // MLP_TC

    def body(d_ref, a_ref, wu_ref, wd_ref, dep_ref, da_ref, du_ref, acc, d_scr):
        c = pl.program_id(1)

        @pl.when(c == 0)
        def _():
            d_scr[...] = d_ref[...].astype(BF16)
            acc[...] = jnp.zeros_like(acc)
        da = (_nt(d_scr[...], wd_ref[...]) * (2.0 * jnp.maximum(a_ref[...].astype(F32), 0.0))).astype(BF16)
        da_ref[...] = da
        acc[...] += _nn(da, wu_ref[...])

        @pl.when(c == nc - 1)
        def _():
            du_ref[...] = acc[...]

    row = pl.BlockSpec((MLP_TM, D), lambda i, c: (i, 0))
    wsp = pl.BlockSpec((MLP_TC, D), lambda i, c: (c, 0))
    chunk = pl.BlockSpec((MLP_TM, MLP_TC), lambda i, c: (i, c))
    return pl.pallas_call(
        body, name="mlp_dx", grid=(T // MLP_TM, nc),
        in_specs=[row, chunk, wsp, wsp, _ANY], out_specs=[chunk, row],
        out_shape=[jax.ShapeDtypeStruct((T, DFF), BF16), jax.ShapeDtypeStruct((T, D), F32)],
        scratch_shapes=[pltpu.VMEM((MLP_TM, D), F32), pltpu.VMEM((MLP_TM, D), BF16)],
        compiler_params=_cparams(("parallel", "arbitrary")),
    )(dff, a, w_upT, w_dn, dep)


def _rowwise(fn, rows, vecs, out_rows, out_sums, *, tm, name, deps=()):
    specs, arrs = [], []
    R = None
    for r in rows:
        if isinstance(r, tuple):
            arr, width, cb = r
            specs.append(pl.BlockSpec((tm, width), lambda i, cb=cb: (i, cb)))
        else:
            arr = r
            specs.append(pl.BlockSpec((tm, arr.shape[1]), lambda i: (i, 0)))
        R = arr.shape[0] if R is None else R
        assert arr.shape[0] == R, name
        arrs.append(arr)
    assert R % tm == 0, name
    for v in vecs:
        specs.append(pl.BlockSpec(v.shape, lambda i: (0, 0)))
        arrs.append(v)
    nr, nv, no, ns = len(rows), len(vecs), len(out_rows), len(out_sums)
    out_specs = [pl.BlockSpec((tm, w), lambda i: (i, 0)) for w, _ in out_rows]
    out_specs += [pl.BlockSpec(s, lambda i: (0, 0)) for s in out_sums]
    out_shape = [jax.ShapeDtypeStruct((R, w), dt) for w, dt in out_rows]
    out_shape += [jax.ShapeDtypeStruct(s, F32) for s in out_sums]

    nd = len(deps)

    def body(*refs):
        ins = [r[...] for r in refs[:nr + nv]]
        o_refs = refs[nr + nv + nd:nr + nv + nd + no]
        s_refs = refs[nr + nv + nd + no:]
        o_vals, s_vals = fn(*ins)
        for ref, val in zip(o_refs, o_vals):
            ref[...] = val.astype(ref.dtype)
        if ns:
            @pl.when(pl.program_id(0) == 0)
            def _():
                for ref in s_refs:
                    ref[...] = jnp.zeros_like(ref)
            for ref, val in zip(s_refs, s_vals):
                ref[...] += val

    outs = pl.pallas_call(
        body, name=name, grid=(R // tm,), in_specs=specs + [_ANY] * nd, out_specs=out_specs, out_shape=out_shape,
        compiler_params=_cparams(("arbitrary",) if ns else ("parallel",)),
    )(*arrs, *deps)
    return outs


def _colsum(x):
    return jnp.sum(x, axis=0, keepdims=True)


def _rstd(x):
    return lax.rsqrt(jnp.mean(x * x, axis=-1, keepdims=True) + EPS)


def _rms_bwd(xn, r, g, dy):
    dn = dy * g
    return r * (dn - xn * jnp.mean(dn * xn, axis=-1, keepdims=True))


def _partner(t):
    lane = lax.broadcasted_iota(jnp.int32, t.shape, 1)
    up = pltpu.roll(t, 96, 1)
    down = pltpu.roll(t, 32, 1)
    return jnp.where((lane % 64) < 32, up, down)


SLABS = AW // 128


def _rows(r, n, d):
    return pl.ds(r, n, stride=d) if d > 1 else pl.ds(0, n)


def _undilate(src_ref, dst, d, tm):
    for r in range(d):
        for j in range(SLABS):
            dst[j][_rows(r, tm // d, d), :] = src_ref[:, pl.ds(r * AW + j * 128, 128)].astype(dst[j].dtype)


def _dilate(dst_ref, src, d, tm):
    for r in range(d):
        for j in range(SLABS):
            dst_ref[:, pl.ds(r * AW + j * 128, 128)] = src[j][_rows(r, tm // d, d), :].astype(dst_ref.dtype)


def _slab_scratch(n, tm):
    return [pltpu.VMEM((tm, 128), F32)] * (SLABS * n)


def _slab_groups(flat):
    return [flat[SLABS * i:SLABS * (i + 1)] for i in range(len(flat) // SLABS)]


def _slab_specs(tm, first):
    return [pl.BlockSpec((tm, 128), lambda i, j=j: (i, first + j)) for j in range(SLABS)]


def _dil_spec(tm, d):
    return pl.BlockSpec((tm // d, d * AW), lambda i: (i, 0))


ROPE_TM = 512


def _rope_fwd(qkvz, cos128, sin128):
    tm = ROPE_TM

    def body(*refs):
        q_refs, k_refs, v_refs = refs[0:4], refs[4:8], refs[8:12]
        c_ref, s_ref = refs[12], refs[13]
        outs = refs[14:]
        for di, d in enumerate(DILATIONS):
            oq, ok, ov = outs[3 * di:3 * di + 3]
            for r in range(d):
                rows = _rows(r, tm // d, d)
                c, s = c_ref[rows, :], s_ref[rows, :]
                for j in range(SLABS):
                    cols = pl.ds(r * AW + j * 128, 128)
                    q, k = q_refs[j][rows, :], k_refs[j][rows, :]
                    oq[:, cols] = ((q * c + _partner(q) * s) * (HD ** -0.5)).astype(BF16)
                    ok[:, cols] = (k * c + _partner(k) * s).astype(BF16)
                    ov[:, cols] = v_refs[j][rows, :].astype(BF16)

    tab = pl.BlockSpec((tm, 128), lambda i: (i, 0))
    out_specs, out_shape = [], []
    for d in DILATIONS:
        out_specs += [_dil_spec(tm, d)] * 3
        out_shape += [jax.ShapeDtypeStruct((T // d, d * AW), BF16)] * 3
    return pl.pallas_call(
        body, name="rope_fwd", grid=(T // tm,),
        in_specs=_slab_specs(tm, 0) + _slab_specs(tm, 4) + _slab_specs(tm, 8) + [tab, tab],
        out_specs=out_specs, out_shape=out_shape, compiler_params=_cparams(("parallel",)),
    )(*([qkvz] * 12), cos128, sin128)


def _rope_bwd(grads, dz, cos128, sin128):
    tm = 256

    def body(*refs):
        g_refs = refs[0:9]
        dz_ref, c_ref, s_ref, o_ref = refs[9], refs[10], refs[11], refs[12]
        scr = _slab_groups(refs[13:])
        for di, d in enumerate(DILATIONS[1:]):
            for t in range(3):
                _undilate(g_refs[3 * (di + 1) + t], scr[3 * di + t], d, tm)
        c, s = c_ref[...], s_ref[...]
        for j in range(SLABS):
            cols = pl.ds(j * 128, 128)
            tot = [g_refs[t][:, cols] + scr[t][j][...] + scr[3 + t][j][...] for t in range(3)]
            dqr = tot[0] * (HD ** -0.5)
            o_ref[:, pl.ds(j * 128, 128)] = (dqr * c + _partner(dqr * s)).astype(BF16)
            o_ref[:, pl.ds(AW + j * 128, 128)] = (tot[1] * c + _partner(tot[1] * s)).astype(BF16)
            o_ref[:, pl.ds(2 * AW + j * 128, 128)] = tot[2].astype(BF16)
        o_ref[:, pl.ds(3 * AW, AW)] = dz_ref[...].astype(BF16)

    tab = pl.BlockSpec((tm, 128), lambda i: (i, 0))
    in_specs, args = [], []
    for d, g in zip(DILATIONS, grads):
        in_specs += [_dil_spec(tm, d)] * 3
        args += list(g)
    return pl.pallas_call(
        body, name="rope_bwd", grid=(T // tm,),
        in_specs=in_specs + [pl.BlockSpec((tm, AW), lambda i: (i, 0)), tab, tab],
        out_specs=pl.BlockSpec((tm, 4 * AW), lambda i: (i, 0)),
        out_shape=jax.ShapeDtypeStruct((T, 4 * AW), BF16),
        scratch_shapes=_slab_scratch(6, tm),
        compiler_params=_cparams(("parallel",)),
    )(*args, dz, cos128, sin128)


def _dilate_cols(x, first):
    tm = ROPE_TM

    def body(x0, x1, x2, x3, o4, o16):
        xs = (x0, x1, x2, x3)
        for o_ref, d in ((o4, 4), (o16, 16)):
            for r in range(d):
                for j in range(SLABS):
                    o_ref[:, pl.ds(r * AW + j * 128, 128)] = xs[j][_rows(r, tm // d, d), :]

    return pl.pallas_call(
        body, name="dilate_cols", grid=(T // tm,),
        in_specs=_slab_specs(tm, first), out_specs=[_dil_spec(tm, 4), _dil_spec(tm, 16)],
        out_shape=[jax.ShapeDtypeStruct((T // 4, 4 * AW), F32), jax.ShapeDtypeStruct((T // 16, 16 * AW), F32)],
        compiler_params=_cparams(("parallel",)),
    )(x, x, x, x)


def _band_masks():
    qi = lax.broadcasted_iota(jnp.int32, (QBLK, QBLK), 0)
    kj = lax.broadcasted_iota(jnp.int32, (QBLK, QBLK), 1)
    return kj >= qi, kj <= qi


def _attn_fwd(q, k, v, d):
    L = q.shape[0]
    nb = L // QBLK

    def body(q_ref, kp_ref, kc_ref, vp_ref, vc_ref, o_ref, l_ref):
        n = pl.program_id(1)
        mask_p, mask_c = _band_masks()
        bias = jnp.concatenate([jnp.where(mask_p, 0.0, NEG) + jnp.where(n > 0, 0.0, NEG),
                                jnp.where(mask_c, 0.0, NEG)], axis=1)
        s = []
        for h in range(HEADS):
            sl = pl.ds(HD * h, HD)
            qh = q_ref[:, sl]
            s.append(jnp.concatenate([_nt(qh, kp_ref[:, sl]), _nt(qh, kc_ref[:, sl])], axis=1))
        s = jnp.stack(s) + bias
        m = jnp.max(s, axis=2, keepdims=True)
        e = jnp.exp(s - m)
        den = jnp.sum(e, axis=2, keepdims=True)
        p = e.astype(BF16)
        inv = 1.0 / den
        lse = m + jnp.log(den)
        for h in range(HEADS):
            sl = pl.ds(HD * h, HD)
            o_ref[:, sl] = (_nn(p[h, :, :QBLK], vp_ref[:, sl]) + _nn(p[h, :, QBLK:], vc_ref[:, sl])) * inv[h]
            l_ref[:, sl] = jnp.broadcast_to(lse[h], (QBLK, HD))

    cur = pl.BlockSpec((QBLK, AW), lambda r, n: (n, r))
    prev = pl.BlockSpec((QBLK, AW), lambda r, n: (jnp.maximum(n - 1, 0), r))
    return pl.pallas_call(
        body, name=f"attn_fwd_d{d}", grid=(d, nb),
        in_specs=[cur, prev, cur, prev, cur], out_specs=[cur, cur],
        out_shape=[jax.ShapeDtypeStruct((L, d * AW), F32)] * 2,
        compiler_params=_cparams(("parallel", "parallel")),
    )(q, k, k, v, v)


def _attn_bwd(q, k, v, do, at, lse, d):
    L = q.shape[0]
    nb = L // QBLK

    def body(q0_ref, q1_ref, kp_ref, kc_ref, vp_ref, vc_ref, do0_ref, do1_ref, at0_ref, at1_ref,
             l0_ref, l1_ref, dq_ref, dk_ref, dv_ref):
        n = pl.program_id(1)
        mask_p, mask_c = _band_masks()
        prev_bias = jnp.where(mask_p, 0.0, NEG)
        bias = jnp.concatenate([prev_bias + jnp.where(n > 0, 0.0, NEG), jnp.where(mask_c, 0.0, NEG),
                                prev_bias + jnp.where(n < nb - 1, 0.0, NEG)], axis=1)
        s, dp, ls, dl, ops = [], [], [], [], []
        for h in range(HEADS):
            sl = pl.ds(HD * h, HD)
            one = pl.ds(HD * h, 1)
            q0, q1 = q0_ref[:, sl], q1_ref[:, sl]
            kp, kc, vp, vc = kp_ref[:, sl], kc_ref[:, sl], vp_ref[:, sl], vc_ref[:, sl]
            do0, do1 = do0_ref[:, sl], do1_ref[:, sl]
            do0b, do1b = do0.astype(BF16), do1.astype(BF16)
            s.append(jnp.concatenate([_nt(q0, kp), _nt(q0, kc), _nt(q1, kc)], axis=1))
            dp.append(jnp.concatenate([_nt(do0b, vp), _nt(do0b, vc), _nt(do1b, vc)], axis=1))
            dl0 = jnp.sum(do0 * at0_ref[:, sl], axis=1, keepdims=True)
            dl1 = jnp.sum(do1 * at1_ref[:, sl], axis=1, keepdims=True)
            dl.append(jnp.concatenate([jnp.broadcast_to(dl0, (QBLK, 2 * QBLK)), jnp.broadcast_to(dl1, (QBLK, QBLK))], axis=1))
            ls.append(jnp.concatenate([jnp.broadcast_to(l0_ref[:, one], (QBLK, 2 * QBLK)),
                                       jnp.broadcast_to(l1_ref[:, one], (QBLK, QBLK))], axis=1))
            ops.append((q0, q1, kp, kc, do0b, do1b))
        p = jnp.exp(jnp.stack(s) + bias - jnp.stack(ls))
        ds = (p * (jnp.stack(dp) - jnp.stack(dl))).astype(BF16)
        p = p.astype(BF16)
        for h in range(HEADS):
            sl = pl.ds(HD * h, HD)
            q0, q1, kp, kc, do0b, do1b = ops[h]
            dq_ref[:, sl] = _nn(ds[h, :, :QBLK], kp) + _nn(ds[h, :, QBLK:2 * QBLK], kc)
            dv_ref[:, sl] = _tn(p[h, :, QBLK:2 * QBLK], do0b) + _tn(p[h, :, 2 * QBLK:], do1b)
            dk_ref[:, sl] = _tn(ds[h, :, QBLK:2 * QBLK], q0) + _tn(ds[h, :, 2 * QBLK:], q1)

    cur = pl.BlockSpec((QBLK, AW), lambda r, n: (n, r))
    prev = pl.BlockSpec((QBLK, AW), lambda r, n: (jnp.maximum(n - 1, 0), r))
    nxt = pl.BlockSpec((QBLK, AW), lambda r, n: (jnp.minimum(n + 1, nb - 1), r))
    return pl.pallas_call(
        body, name=f"attn_bwd_d{d}", grid=(d, nb),
        in_specs=[cur, nxt, prev, cur, prev, cur, cur, nxt, cur, nxt, cur, nxt], out_specs=[cur, cur, cur],
        out_shape=[jax.ShapeDtypeStruct((L, d * AW), F32)] * 3,
        compiler_params=_cparams(("parallel", "parallel")),
    )(q, q, k, k, v, v, do, do, at, at, lse, lse)


def _attn_merge(outs, lses):
    tm = ROPE_TM

    def body(o1, o4, o16, l1, l4, l16, at_ref, ls_ref, at4, ls4, at16, ls16, *flat):
        so4, so16, sl4, sl16, sa, sl = _slab_groups(flat)
        _undilate(o4, so4, 4, tm)
        _undilate(o16, so16, 16, tm)
        _undilate(l4, sl4, 4, tm)
        _undilate(l16, sl16, 16, tm)
        for j in range(SLABS):
            cols = pl.ds(j * 128, 128)
            a, b, c = l1[:, cols], sl4[j][...], sl16[j][...]
            m = jnp.maximum(jnp.maximum(a, b), c)
            e1, e2, e3 = jnp.exp(a - m), jnp.exp(b - m), jnp.exp(c - m)
            s = e1 + e2 + e3
            inv = 1.0 / s
            attn = (e1 * inv) * o1[:, cols] + (e2 * inv) * so4[j][...] + (e3 * inv) * so16[j][...]
            lse = m + jnp.log(s)
            at_ref[:, cols] = attn
            ls_ref[:, cols] = lse
            sa[j][...] = attn
            sl[j][...] = lse
        _dilate(at4, sa, 4, tm)
        _dilate(at16, sa, 16, tm)
        _dilate(ls4, sl, 4, tm)
        _dilate(ls16, sl, 16, tm)

    specs = [_dil_spec(tm, d) for d in DILATIONS]
    tok = specs[0]
    return pl.pallas_call(
        body, name="attn_merge", grid=(T // tm,),
        in_specs=specs + specs, out_specs=[tok, tok, specs[1], specs[1], specs[2], specs[2]],
        out_shape=[jax.ShapeDtypeStruct((T, AW), F32)] * 2 + [jax.ShapeDtypeStruct((T // 4, 4 * AW), F32)] * 2
        + [jax.ShapeDtypeStruct((T // 16, 16 * AW), F32)] * 2,
        scratch_shapes=_slab_scratch(6, tm),
        compiler_params=_cparams(("parallel",)),
    )(*outs, *lses)


CONV_TM = 512
HALO = 8


def _conv_pre(ext, w, b):
    y = b + w[3] * ext
    for kk in range(1, CONV_K):
        y = y + w[3 - kk] * pltpu.roll(ext, kk, 0)
    return y


def _rows_to_block(rows, n, width):
    ri = lax.broadcasted_iota(jnp.int32, (n, width), 0)
    out = jnp.zeros((n, width), F32)
    for j, r in enumerate(rows):
        out = out + jnp.where(ri == j, r, 0.0)
    return out


def _conv_fwd(xbc, w, b):
    nblk = T // CONV_TM

    def body(x_ref, h_ref, w_ref, b_ref, o_ref):
        i = pl.program_id(0)
        halo = jnp.where(i > 0, h_ref[...], 0.0)
        ext = jnp.concatenate([halo, x_ref[...]], axis=0)
        y = _conv_pre(ext, [w_ref[pl.ds(j, 1), :] for j in range(CONV_K)], b_ref[...])[HALO:]
        o_ref[...] = y * _sigmoid(y)

    return pl.pallas_call(
        body, name="conv_fwd", grid=(nblk,),
        in_specs=[pl.BlockSpec((CONV_TM, CONV_CH), lambda i: (i, 0)),
                  pl.BlockSpec((HALO, CONV_CH), lambda i: (jnp.maximum(i * (CONV_TM // HALO) - 1, 0), 0)),
                  pl.BlockSpec((CONV_K, CONV_CH), lambda i: (0, 0)),
                  pl.BlockSpec((1, CONV_CH), lambda i: (0, 0))],
        out_specs=pl.BlockSpec((CONV_TM, CONV_CH), lambda i: (i, 0)),
        out_shape=jax.ShapeDtypeStruct((T, CONV_CH), F32),
        compiler_params=_cparams(("parallel",)),
    )(xbc, xbc, w, b)


def _conv_bwd(xbc, dact, ddt, w, b):
    nblk = T // CONV_TM
    per = CONV_TM // HALO

    def body(x_ref, xb_ref, xa_ref, g_ref, ga_ref, ddt_ref, w_ref, b_ref, dx_ref, dw_ref):
        i = pl.program_id(0)
        wv = [w_ref[pl.ds(j, 1), :] for j in range(CONV_K)]
        before = jnp.where(i > 0, xb_ref[...], 0.0)
        last = i == nblk - 1
        after = jnp.where(last, 0.0, xa_ref[...])
        g_after = jnp.where(last, 0.0, ga_ref[...])
        ext = jnp.concatenate([before, x_ref[...], after], axis=0)
        y = _conv_pre(ext, wv, b_ref[...])[HALO:]
        sg = _sigmoid(y)
        dy = jnp.concatenate([g_ref[...], g_after], axis=0) * (sg * (1.0 + y * (1.0 - sg)))
        n = CONV_TM + HALO
        dx = wv[3] * dy
        for kk in range(1, CONV_K):
            dx = dx + wv[3 - kk] * pltpu.roll(dy, n - kk, 0)
        dx_ref[:, pl.ds(0, CONV_CH)] = dx[:CONV_TM].astype(BF16)
        dx_ref[:, pl.ds(CONV_CH, DT_PAD)] = ddt_ref[...].astype(BF16)
        dyc = dy[:CONV_TM]
        rows = [jnp.sum(dyc * (pltpu.roll(ext, 3 - j, 0) if j < 3 else ext)[HALO:HALO + CONV_TM], axis=0, keepdims=True)
                for j in range(CONV_K)]
        rows.append(jnp.sum(dyc, axis=0, keepdims=True))
        part = _rows_to_block(rows, 8, CONV_CH)

        @pl.when(i == 0)
        def _():
            dw_ref[...] = jnp.zeros_like(dw_ref)
        dw_ref[...] += part

    blk = pl.BlockSpec((CONV_TM, CONV_CH), lambda i: (i, 0))
    hb = pl.BlockSpec((HALO, CONV_CH), lambda i: (jnp.maximum(i * per - 1, 0), 0))
    ha = pl.BlockSpec((HALO, CONV_CH), lambda i: (jnp.minimum((i + 1) * per, T // HALO - 1), 0))
    return pl.pallas_call(
        body, name="conv_bwd", grid=(nblk,),
        in_specs=[blk, hb, ha, blk, ha, pl.BlockSpec((CONV_TM, DT_PAD), lambda i: (i, 0)),
                  pl.BlockSpec((CONV_K, CONV_CH), lambda i: (0, 0)), pl.BlockSpec((1, CONV_CH), lambda i: (0, 0))],
        out_specs=[pl.BlockSpec((CONV_TM, CONV_CH + DT_PAD), lambda i: (i, 0)), pl.BlockSpec((8, CONV_CH), lambda i: (0, 0))],
        out_shape=[jax.ShapeDtypeStruct((T, CONV_CH + DT_PAD), BF16), jax.ShapeDtypeStruct((8, CONV_CH), F32)],
        compiler_params=_cparams(("arbitrary",)),
    )(xbc, xbc, xbc, dact, dact, ddt, w, b)


def _pick(mat, h):
    lane = lax.broadcasted_iota(jnp.int32, mat.shape, 1)
    return jnp.sum(jnp.where(lane == h, mat, 0.0), axis=1, keepdims=True)


def _heads(fn):
    return jnp.stack([fn(h) for h in range(HEADS)])


def _ssd_prep(dt_ref, bias_ref, alog_ref, dsk_ref, b_ref, c_ref, xs_ref, state_ref, cst):
    li = lax.broadcasted_iota(jnp.int32, (CHUNK, CHUNK), 0)
    si = lax.broadcasted_iota(jnp.int32, (CHUNK, CHUNK), 1)
    tri = li >= si
    dtp = dt_ref[...] + bias_ref[...]
    dt = _softplus(dtp)
    A = -jnp.exp(alog_ref[...])
    a = dt * A
    cs = jnp.dot(tri.astype(F32), a, precision=HIGHEST, preferred_element_type=F32)
    cst[...] = cs.T
    Bm = b_ref[...].astype(BF16)
    Cm = c_ref[...].astype(BF16)
    cb = _nt(Cm, Bm)
    dskv = dsk_ref[...]
    cs_col = _heads(lambda h: _pick(cs, h))
    cs_row = _heads(lambda h: cst[pl.ds(h, 1), :])
    dt_col = _heads(lambda h: _pick(dt, h))
    dsk_col = _heads(lambda h: _pick(dskv, h))
    lam = jnp.exp(jnp.where(tri, cs_col - cs_row, NEG))
    x = _heads(lambda h: xs_ref[:, pl.ds(HD * h, HD)])
    xdt = x * dt_col
    prev = _heads(lambda h: state_ref[pl.ds(HD * h, HD), :])
    lane = lax.broadcasted_iota(jnp.int32, (1, 1, CHUNK), 2)
    cl = jnp.sum(jnp.where(lane == CHUNK - 1, cs_row, 0.0), axis=2, keepdims=True)
    f = jnp.exp(cl - cs_col)
    return dict(li=li, si=si, dtp=dtp, dt=dt, A=A, Bm=Bm, Cm=Cm, cb=cb, cs_col=cs_col, dt_col=dt_col, dsk_col=dsk_col,
                lam=lam, x=x, xdt=xdt, prev=prev, cl=cl, f=f)


def _ssd_fwd(act, xbcdt, bias, alog, dsk):
    nc = T // CHUNK

    def body(xs_ref, b_ref, c_ref, dt_ref, bias_ref, alog_ref, dsk_ref, y_ref, st_ref, state, cst):
        @pl.when(pl.program_id(0) == 0)
        def _():
            state[...] = jnp.zeros_like(state)
        st_ref[...] = state[...]
        s = _ssd_prep(dt_ref, bias_ref, alog_ref, dsk_ref, b_ref, c_ref, xs_ref, state, cst)
        Bm, Cm, prev = s["Bm"], s["Cm"], s["prev"]
        g = (s["cb"] * s["lam"]).astype(BF16)
        xdtb = s["xdt"].astype(BF16)
        prevb = prev.astype(BF16)
        y = _heads(lambda h: _nn(g[h], xdtb[h])) + _heads(lambda h: _nt(Cm, prevb[h])) * jnp.exp(s["cs_col"])
        y = y + s["dsk_col"] * s["x"]
        xf = (s["xdt"] * s["f"]).astype(BF16)
        new = prev * jnp.exp(s["cl"]) + _heads(lambda h: _tn(xf[h], Bm))
        for h in range(HEADS):
            y_ref[:, pl.ds(HD * h, HD)] = y[h]
            state[pl.ds(HD * h, HD), :] = new[h]

    vec = pl.BlockSpec((1, DT_PAD), lambda c: (0, 0))
    return pl.pallas_call(
        body, name="ssd_fwd", grid=(nc,),
        in_specs=[pl.BlockSpec((CHUNK, AW), lambda c: (c, 0)), pl.BlockSpec((CHUNK, NS), lambda c: (c, 4)),
                  pl.BlockSpec((CHUNK, NS), lambda c: (c, 5)), pl.BlockSpec((CHUNK, DT_PAD), lambda c: (c, 6)),
                  vec, vec, vec],
        out_specs=[pl.BlockSpec((CHUNK, AW), lambda c: (c, 0)), pl.BlockSpec((None, AW, NS), lambda c: (c, 0, 0))],
        out_shape=[jax.ShapeDtypeStruct((T, AW), F32), jax.ShapeDtypeStruct((nc, AW, NS), F32)],
        scratch_shapes=[pltpu.VMEM((AW, NS), F32), pltpu.VMEM((CHUNK, CHUNK), F32)],
        compiler_params=_cparams(("arbitrary",)),
    )(act, act, act, xbcdt, bias, alog, dsk)


def _ssd_bwd(act, xbcdt, bias, alog, dsk, states, dy):
    nc = T // CHUNK

    def body(xs_ref, b_ref, c_ref, dt_ref, bias_ref, alog_ref, dsk_ref, st_ref, dy_ref,
             dact_ref, ddt_ref, par_ref, dstate, cst):
        step = pl.program_id(0)

        @pl.when(step == 0)
        def _():
            dstate[...] = jnp.zeros_like(dstate)
            par_ref[...] = jnp.zeros_like(par_ref)
        s = _ssd_prep(dt_ref, bias_ref, alog_ref, dsk_ref, b_ref, c_ref, xs_ref, st_ref, cst)
        Bm, Cm, prev, lam, x, xdt, f, cl = s["Bm"], s["Cm"], s["prev"], s["lam"], s["x"], s["xdt"], s["f"], s["cl"]
        lane = lax.broadcasted_iota(jnp.int32, (1, DT_PAD), 1)
        row = lax.broadcasted_iota(jnp.int32, (1, CHUNK, 1), 1)
        g = s["cb"] * lam
        gb, xdtb, prevb = g.astype(BF16), xdt.astype(BF16), prev.astype(BF16)
        dy = _heads(lambda h: dy_ref[:, pl.ds(HD * h, HD)])
        dyb = dy.astype(BF16)
        dnew = _heads(lambda h: dstate[pl.ds(HD * h, HD), :])
        dnewb = dnew.astype(BF16)
        E = jnp.exp(s["cs_col"])
        ecl = jnp.exp(cl)
        dG = _heads(lambda h: _nt(dyb[h], xdtb[h]))
        dxdt = _heads(lambda h: _tn(gb[h], dyb[h]))
        Yo = _heads(lambda h: _nt(Cm, prevb[h]))
        W = _heads(lambda h: _nt(Bm, dnewb[h]))
        dcb = jnp.sum(dG * lam, axis=0)
        Mm = dG * g
        col_sums = jnp.sum(Mm, axis=1, keepdims=True)
        dYo = (dy * E).astype(BF16)
        dxdt = dxdt + W * f
        dF = jnp.sum(W * xdt, axis=2, keepdims=True) * f
        dcl = jnp.sum(dnew * prev, axis=(1, 2), keepdims=True) * ecl + jnp.sum(dF, axis=1, keepdims=True)
        dcs = (jnp.sum(Mm, axis=2, keepdims=True) + jnp.sum(dy * Yo, axis=2, keepdims=True) * E - dF
               + jnp.where(row == CHUNK - 1, dcl, 0.0))
        ddt_x = jnp.sum(dxdt * x, axis=2, keepdims=True)
        dD = jnp.sum(dy * x, axis=(1, 2), keepdims=True)
        dx = s["dsk_col"] * dy + dxdt * s["dt_col"]
        xfb = (xdt * f).astype(BF16)
        dprev = _heads(lambda h: _tn(dYo[h], Cm)) + dnew * ecl
        dcbb = dcb.astype(BF16)
        dC = _nn(dcbb, Bm)
        dB = _tn(dcbb, Cm)
        dcs_mat = -_rows_to_block([col_sums[h] for h in range(HEADS)], CHUNK, CHUNK).T
        ddt_mat = jnp.zeros((CHUNK, DT_PAD), F32)
        dD_row = jnp.zeros((1, DT_PAD), F32)
        for h in range(HEADS):
            sl = pl.ds(HD * h, HD)
            dC = dC + _nn(dYo[h], prevb[h])
            dB = dB + _nn(xfb[h], dnewb[h])
            dcs_mat = dcs_mat + jnp.where(lane == h, dcs[h], 0.0)
            ddt_mat = ddt_mat + jnp.where(lane == h, ddt_x[h], 0.0)
            dD_row = dD_row + jnp.where(lane == h, dD[h], 0.0)
            dact_ref[:, sl] = dx[h]
            dstate[sl, :] = dprev[h]
        dact_ref[:, pl.ds(AW, NS)] = dB
        dact_ref[:, pl.ds(AW + NS, NS)] = dC
        da = jnp.dot((s["li"] <= s["si"]).astype(F32), dcs_mat, precision=HIGHEST, preferred_element_type=F32)
        ddtp = jnp.where(lane < HEADS, (ddt_mat + da * s["A"]) * _sigmoid(s["dtp"]), 0.0)
        ddt_ref[...] = ddtp
        dalog = jnp.where(lane < HEADS, jnp.sum(da * s["dt"], axis=0, keepdims=True) * s["A"], 0.0)
        par_ref[...] += _rows_to_block([jnp.sum(ddtp, axis=0, keepdims=True), dalog, dD_row], 8, DT_PAD)

    vec = pl.BlockSpec((1, DT_PAD), lambda c: (0, 0))
    rev = lambda c: nc - 1 - c
    return pl.pallas_call(
        body, name="ssd_bwd", grid=(nc,),
        in_specs=[pl.BlockSpec((CHUNK, AW), lambda c: (rev(c), 0)), pl.BlockSpec((CHUNK, NS), lambda c: (rev(c), 4)),
                  pl.BlockSpec((CHUNK, NS), lambda c: (rev(c), 5)), pl.BlockSpec((CHUNK, DT_PAD), lambda c: (rev(c), 6)),
                  vec, vec, vec,
                  pl.BlockSpec((None, AW, NS), lambda c: (rev(c), 0, 0)), pl.BlockSpec((CHUNK, AW), lambda c: (rev(c), 0))],
        out_specs=[pl.BlockSpec((CHUNK, CONV_CH), lambda c: (rev(c), 0)), pl.BlockSpec((CHUNK, DT_PAD), lambda c: (rev(c), 0)),
                   pl.BlockSpec((8, DT_PAD), lambda c: (0, 0))],
        out_shape=[jax.ShapeDtypeStruct((T, CONV_CH), F32), jax.ShapeDtypeStruct((T, DT_PAD), F32),
                   jax.ShapeDtypeStruct((8, DT_PAD), F32)],
        scratch_shapes=[pltpu.VMEM((AW, NS), F32), pltpu.VMEM((CHUNK, CHUNK), F32)],
        compiler_params=_cparams(("arbitrary",)),
    )(act, act, act, xbcdt, bias, alog, dsk, states, dy)


def _place():
    return lax.axis_index("x"), lax.axis_index("y"), lax.axis_index("c")


def _slot(px, py, pc):
    return 4 * px + 2 * py + pc


def _all_gather(arrs, name):
    na = len(arrs)

    def body(*refs):
        ins, outs = refs[:na], refs[na:2 * na]
        send_sems, recv_sems, local_sems = refs[2 * na:]
        x, y, c = _place()
        me, sib = (x, y, c), (x, y, 1 - c)
        chips = [(1 - x, y), (x, 1 - y), (1 - x, 1 - y)]

        def copy(a, kk, block, to, src=None):
            dst = outs[a].at[_slot(*block)]
            return pltpu.make_async_remote_copy(
                src_ref=dst if src is None else src, dst_ref=dst,
                send_sem=send_sems.at[a, kk], recv_sem=recv_sems.at[a, kk], device_id=to, device_id_type=MESH)

        mine = [pltpu.make_async_copy(ins[a], outs[a].at[_slot(*me)], local_sems.at[a]) for a in range(na)]
        for cp in mine:
            cp.start()
        first = []
        for a in range(na):
            first.append(copy(a, 0, me, sib, src=ins[a]))
            first += [copy(a, 1 + j, me, (*chip, c), src=ins[a]) for j, chip in enumerate(chips)]
        for cp in first:
            cp.start()
        passed = []
        for j, chip in enumerate(chips):
            for a in range(na):
                copy(a, 1 + j, (*chip, c), me).wait_recv()
                fw = copy(a, 4 + j, (*chip, c), sib)
                fw.start()
                passed.append(fw)
        for a in range(na):
            copy(a, 0, sib, me).wait_recv()
            for j, chip in enumerate(chips):
                copy(a, 4 + j, (*chip, 1 - c), me).wait_recv()
        for cp in first + passed:
            cp.wait_send()
        for cp in mine:
            cp.wait()

    any_spec = pl.BlockSpec(memory_space=pl.ANY)
    return pl.pallas_call(
        body, name=name,
        in_specs=[any_spec] * na, out_specs=[any_spec] * na,
        out_shape=[jax.ShapeDtypeStruct((N_DEV,) + a.shape, a.dtype) for a in arrs],
        scratch_shapes=[pltpu.SemaphoreType.DMA((na, 7)), pltpu.SemaphoreType.DMA((na, 7)),
                        pltpu.SemaphoreType.DMA((na,))],
    )(*arrs)


def _reduce_scatter(part, name):
    _, r, C = part.shape

    def body(part_ref, out_ref, own, got_sib, chip_sum, got_ici, lsem, s1, r1, s2, r2):
        x, y, c = _place()
        chips = [(x, y), (1 - x, y), (x, 1 - y), (1 - x, 1 - y)]
        loc = [pltpu.make_async_copy(part_ref.at[_slot(*chips[kk], c)], own.at[kk], lsem.at[kk]) for kk in range(4)]
        d2d = [pltpu.make_async_remote_copy(
            src_ref=part_ref.at[_slot(*chips[kk], 1 - c)], dst_ref=got_sib.at[kk],
            send_sem=s1.at[kk], recv_sem=r1.at[kk], device_id=(x, y, 1 - c), device_id_type=MESH) for kk in range(4)]
        for cp in loc + d2d:
            cp.start()
        ici = [pltpu.make_async_remote_copy(
            src_ref=chip_sum.at[kk - 1], dst_ref=got_ici.at[kk - 1],
            send_sem=s2.at[kk - 1], recv_sem=r2.at[kk - 1], device_id=(*chips[kk], c), device_id_type=MESH)
            for kk in range(1, 4)]
        for kk in (1, 2, 3):
            loc[kk].wait()
            d2d[kk].wait_recv()
            chip_sum[kk - 1] = (own[kk].astype(F32) + got_sib[kk].astype(F32)).astype(BF16)
            ici[kk - 1].start()
        loc[0].wait()
        d2d[0].wait_recv()
        acc = own[0].astype(F32) + got_sib[0].astype(F32)
        for cp in ici:
            cp.wait_recv()
        out_ref[...] = ((acc + got_ici[0].astype(F32)) + got_ici[1].astype(F32)) + got_ici[2].astype(F32)
        for cp in d2d + ici:
            cp.wait_send()

    return pl.pallas_call(
        body, name=name,
        in_specs=[pl.BlockSpec(memory_space=pl.ANY)],
        out_specs=pl.BlockSpec(memory_space=pltpu.VMEM),
        out_shape=jax.ShapeDtypeStruct((r, C), F32),
        scratch_shapes=[pltpu.VMEM((4, r, C), BF16), pltpu.VMEM((4, r, C), BF16), pltpu.VMEM((3, r, C), BF16),
                        pltpu.VMEM((3, r, C), BF16),
                        pltpu.SemaphoreType.DMA((4,)), pltpu.SemaphoreType.DMA((4,)), pltpu.SemaphoreType.DMA((4,)),
                        pltpu.SemaphoreType.DMA((3,)), pltpu.SemaphoreType.DMA((3,))],
        compiler_params=pltpu.CompilerParams(vmem_limit_bytes=VMEM_LIMIT),
    )(part)


SLAB_ROWS = 24


def _all_reduce_small(parts, name):
    R, C = SLAB_ROWS, D
    n = len(parts)

    def body(*refs):
        in_refs = refs[:n]
        out_ref, slab, got, send_sems, recv_sems = refs[n:]
        slab[...] = jnp.zeros_like(slab)
        for ref, (arr, row) in zip(in_refs, parts):
            slab[pl.ds(row, arr.shape[0]), pl.ds(0, arr.shape[1])] = ref[...]
        x, y, c = _place()
        mine = _slot(x, y, c)
        copies = [pltpu.make_async_remote_copy(
            src_ref=slab, dst_ref=got.at[mine], send_sem=send_sems.at[kk], recv_sem=recv_sems.at[kk],
            device_id=peer, device_id_type=MESH) for kk, peer in enumerate(_peers(x, y, c))]
        for cp in copies:
            cp.start()
        got[mine] = slab[...]
        for cp in copies:
            cp.wait_recv()
        acc = got[0]
        for s in range(1, N_DEV):
            acc = acc + got[s]
        out_ref[...] = acc
        for cp in copies:
            cp.wait_send()

    vm = pl.BlockSpec(memory_space=pltpu.VMEM)
    return pl.pallas_call(
        body, name=name, in_specs=[vm] * n, out_specs=vm,
        out_shape=jax.ShapeDtypeStruct((R, C), F32),
        scratch_shapes=[pltpu.VMEM((R, C), F32), pltpu.VMEM((N_DEV, R, C), F32), pltpu.SemaphoreType.DMA((N_DEV - 1,)),
                        pltpu.SemaphoreType.DMA((N_DEV - 1,))],
    )(*[a for a, _ in parts])


_HBM = pl.BlockSpec(memory_space=pltpu.HBM)
_SEM = pl.BlockSpec(memory_space=pltpu.SEMAPHORE)
_EFFECT = pltpu.SideEffectType.DATAFLOW_SIDE_EFFECTING


def _peers(x, y, c):
    out = []
    for kk in range(1, N_DEV):
        fx, fy, fc = kk >> 2 & 1, kk >> 1 & 1, kk & 1
        out.append((1 - x if fx else x, 1 - y if fy else y, 1 - c if fc else c))
    return out


def _send_start(src, per_peer, name, dep):
    (handles, token) = _send_start_many([src], per_peer, name, dep)
    return handles, token


def _send_start_many(srcs, per_peer, name, dep):
    n = len(srcs)

    def body(*refs):
        src_refs, land_refs = refs[:n], refs[n:2 * n]
        send_sems, recv_sems = refs[2 * n + 1], refs[2 * n + 2]
        token = refs[-1]
        x, y, c = _place()
        mine = _slot(x, y, c)
        for a in range(n):
            for kk, peer in enumerate(_peers(x, y, c)):
                pltpu.make_async_remote_copy(
                    src_ref=src_refs[a].at[_slot(*peer)] if per_peer else src_refs[a], dst_ref=land_refs[a].at[mine],
                    send_sem=send_sems.at[a * (N_DEV - 1) + kk], recv_sem=recv_sems.at[a * (N_DEV - 1) + kk],
                    device_id=peer, device_id_type=MESH).start()
        token[...] = jnp.zeros_like(token)

    lands = [lax.empty((N_DEV,) + tuple(s.shape[1:] if per_peer else s.shape), s.dtype) for s in srcs]
    hbm = lambda t: pltpu.with_memory_space_constraint(t, pltpu.HBM)
    outs = pl.pallas_call(
        body, name=name,
        out_shape=(pltpu.SemaphoreType.DMA((n * (N_DEV - 1),)), pltpu.SemaphoreType.DMA((n * (N_DEV - 1),)),
                   *[pltpu.HBM(s.shape, s.dtype) for s in srcs], *[pltpu.HBM(l.shape, l.dtype) for l in lands],
                   jax.ShapeDtypeStruct((8, 128), F32)),
        in_specs=(*[_HBM] * (2 * n), _ANY),
        out_specs=(_SEM, _SEM, *[_HBM] * (2 * n), pl.BlockSpec(memory_space=pltpu.VMEM)),
        input_output_aliases={i: 2 + i for i in range(2 * n)},
        compiler_params=pltpu.CompilerParams(has_side_effects=_EFFECT),
    )(*[hbm(s) for s in srcs], *[hbm(l) for l in lands], dep)
    return (outs[0], outs[1], list(outs[2:2 + n]), list(outs[2 + n:2 + 2 * n])), outs[-1]


def _send_wait(handles, after, name):
    srcs, lands = _send_wait_many(handles, after, name)
    return srcs[0], lands[0]


def _send_wait_many(handles, after, name):
    send_sems, recv_sems, src_thrus, land_thrus = handles
    n = len(src_thrus)

    def body(*refs):
        land_refs = refs[n:2 * n]
        send_sems, recv_sems = refs[2 * n], refs[2 * n + 1]
        me = _place()
        for a in range(n):
            for kk in range(N_DEV - 1):
                cp = pltpu.make_async_remote_copy(
                    src_ref=land_refs[a].at[0], dst_ref=land_refs[a].at[0],
                    send_sem=send_sems.at[a * (N_DEV - 1) + kk], recv_sem=recv_sems.at[a * (N_DEV - 1) + kk],
                    device_id=me, device_id_type=MESH)
                cp.wait_send()
                cp.wait_recv()

    both = list(src_thrus) + list(land_thrus)
    outs = pl.pallas_call(
        body, name=name,
        out_shape=tuple(pltpu.HBM(t.shape, t.dtype) for t in both),
        in_specs=(*[_HBM] * (2 * n), _SEM, _SEM, _ANY), out_specs=tuple([_HBM] * (2 * n)),
        input_output_aliases={i: i for i in range(2 * n)},
        compiler_params=pltpu.CompilerParams(has_side_effects=_EFFECT),
    )(*both, send_sems, recv_sems, after)
    return list(outs[:n]), list(outs[n:])


def _sum_slots(land, name):
    _, R, C = land.shape
    tm = R if R <= 512 else 512

    def body(x_ref, o_ref):
        acc = x_ref[0].astype(F32)
        for j in range(1, N_DEV):
            acc = acc + x_ref[j].astype(F32)
        o_ref[...] = acc

    return pl.pallas_call(
        body, name=name, grid=(R // tm,),
        in_specs=[pl.BlockSpec((N_DEV, tm, C), lambda i: (0, i, 0))], out_specs=pl.BlockSpec((tm, C), lambda i: (i, 0)),
        out_shape=jax.ShapeDtypeStruct((R, C), F32), compiler_params=_cparams(("parallel",)),
    )(land)


def _adam_math(w, g, m, v):
    m2 = ADAM_B1 * m + (1.0 - ADAM_B1) * g
    v2 = ADAM_B2 * v + (1.0 - ADAM_B2) * (g * g)
    m_hat = m2 / (1.0 - ADAM_B1 ** ADAM_STEP)
    v_hat = v2 / (1.0 - ADAM_B2 ** ADAM_STEP)
    delta = -ADAM_LR * (m_hat / (jnp.sqrt(v_hat) + ADAM_EPS) + ADAM_WD * w)
    return delta, m2, v2


def _adamw(w, g, m, v, name):
    R, C = w.shape
    tm = R if R <= 512 else 256
    return _rowwise(lambda w, g, m, v: (_adam_math(w, g, m, v), ()), [w, g, m, v], [], [(C, F32)] * 3, [], tm=tm, name=name)


def _adamw_small(slab, slab_rows, g_conv_w, ws, ms, vs):
    n = len(ws)

    def body(*refs):
        slab_ref, gc_ref = refs[0], refs[1]
        w_refs, m_refs, v_refs = refs[2:2 + n], refs[2 + n:2 + 2 * n], refs[2 + 2 * n:2 + 3 * n]
        outs = refs[2 + 3 * n:]
        loss_ref = outs[0]
        g_out, d_out, m_out, v_out = (outs[1 + i * n:1 + (i + 1) * n] for i in range(4))
        loss_ref[...] = jnp.sum(slab_ref[pl.ds(6, 1), :], axis=1, keepdims=True)
        for i in range(n):
            g = gc_ref[...] if i == n - 1 else slab_ref[pl.ds(slab_rows[i], 1), pl.ds(0, ws[i].shape[1])]
            d, m2, v2 = _adam_math(w_refs[i][...], g, m_refs[i][...], v_refs[i][...])
            g_out[i][...] = g
            d_out[i][...] = d
            m_out[i][...] = m2
            v_out[i][...] = v2

    vm = pl.BlockSpec(memory_space=pltpu.VMEM)
    shapes = [jax.ShapeDtypeStruct(w.shape, F32) for w in ws]
    outs = pl.pallas_call(
        body, name="adamw_small", in_specs=[vm] * (2 + 3 * n), out_specs=[vm] * (1 + 4 * n),
        out_shape=[jax.ShapeDtypeStruct((1, 1), F32)] + shapes * 4,
    )(slab, g_conv_w, *ws, *ms, *vs)
    return outs[0], outs[1:1 + n], outs[1 + n:1 + 2 * n], outs[1 + 2 * n:1 + 3 * n], outs[1 + 3 * n:]


SMALL = ["norm_mix_pre", "norm_mix_post", "norm_mlp_pre", "norm_mlp_post", "norm_ple_post",
         "conv_b", "ssd_norm_g", "dt_bias", "a_log", "d_skip"]


def _pad_row(v, width=D):
    return jnp.pad(v, ((0, 0), (0, width - v.shape[1])))


def kernel(x, p, positions, norm_mix_pre, norm_mix_post, w_in, conv_w, conv_b, dt_bias, a_log, d_skip, ssd_norm_g, w_out, norm_mlp_pre, norm_mlp_post, w_up, w_down, w_ple_gate, w_ple_proj, norm_ple_post, loss_target, m_norm_mix_pre, m_norm_mix_post, m_w_in, m_conv_w, m_conv_b, m_dt_bias, m_a_log, m_d_skip, m_ssd_norm_g, m_w_out, m_norm_mlp_pre, m_norm_mlp_post, m_w_up, m_w_down, m_w_ple_gate, m_w_ple_proj, m_norm_ple_post, v_norm_mix_pre, v_norm_mix_post, v_w_in, v_conv_w, v_conv_b, v_dt_bias, v_a_log, v_d_skip, v_ssd_norm_g, v_w_out, v_norm_mlp_pre, v_norm_mlp_post, v_w_up, v_w_down, v_w_ple_gate, v_w_ple_proj, v_norm_ple_post):
    args = dict(locals())
    x2, p2, tgt = x[0], p[0, 0], loss_target[0]
    g1, g2, g3, g4, g5 = norm_mix_pre, norm_mix_post, norm_mlp_pre, norm_mlp_post, norm_ple_post

    me = _slot(*_place())
    pack_in = jnp.pad(w_in[0].T, ((0, W_IN_SHARD_PAD - W_IN_SHARD), (0, 0))).astype(BF16)
    rest = [w_out[0].astype(BF16), w_up[0].T.astype(BF16), w_down[0].astype(BF16), w_ple_gate[0].astype(BF16),
            w_ple_proj[0].T.reshape(32, D).astype(BF16)]
    conv_pack = jnp.pad(conv_w[0], ((0, 4), (0, 32)))
    gin, gconv = _all_gather([pack_in, conv_pack], "gather_w_in")
    rest_handles, tok_rest = _send_start_many(rest, False, "gather_rest_start", gconv)
    w_inT = gin[:, :W_IN_SHARD].reshape(IN_W, D)
    w_qkvzT = w_inT[:4 * AW]
    w_xbcdtT = jnp.pad(w_inT[4 * AW:], ((0, DT_PAD - HEADS), (0, 0)))
    conv_full = gconv[:, :CONV_K, :96].transpose(1, 0, 2).reshape(CONV_K, CONV_CH)

    inv_freq = ROPE_THETA ** (-jnp.arange(HD // 2, dtype=F32) * 2.0 / HD)
    ang = positions[0].astype(F32)[:, None] * inv_freq
    cos, sin = jnp.cos(ang), jnp.sin(ang)
    cos128 = jnp.concatenate([cos, cos, cos, cos], axis=1)
    sin128 = jnp.concatenate([-sin, sin, -sin, sin], axis=1)

    bias_w, alog_w, dsk_w = _pad_row(dt_bias, DT_PAD), _pad_row(a_log, DT_PAD), _pad_row(d_skip, DT_PAD)
    rms_pre = lambda a, r, g: a * r * g

    (u1,) = _rowwise(lambda a, g: ((a * _rstd(a) * g,), ()), [x2], [g1], [(D, BF16)], [], tm=512, name="norm_x")
    p2b = p2.astype(BF16)
    qkvz = _mm(u1, w_qkvzT, tb=True, tm=512, tn=1024, tk=1024, name="proj_qkvz", deps=[tok_rest])
    xbcdt = _mm(u1, w_xbcdtT, tb=True, tm=512, tn=896, tk=1024, name="proj_xbcdt")

    qkv = _rope_fwd(qkvz, cos128, sin128)
    qkv = [qkv[3 * i:3 * i + 3] for i in range(len(DILATIONS))]
    outs, lses = [], []
    for d, (qd, kd, vd) in zip(DILATIONS, qkv):
        o, l = _attn_fwd(qd, kd, vd, d)
        outs.append(o)
        lses.append(l)
    attn, lse, attn4, lse4, attn16, lse16 = _attn_merge(outs, lses)

    act = _conv_fwd(xbcdt, conv_full, conv_b)
    y_ssd, states = _ssd_fwd(act, xbcdt, bias_w, alog_w, dsk_w)

    def gated_fwd(y, z, a, gs):
        gi = y * (z * _sigmoid(z))
        return (jnp.concatenate([a, gi * _rstd(gi) * gs], axis=1),), ()
    (cat,) = _rowwise(gated_fwd, [y_ssd, (qkvz, AW, 3), attn], [ssd_norm_g], [(D, BF16)], [], tm=512, name="gated_norm")

    rest_back, landed = _send_wait_many(rest_handles, cat, "gather_rest_wait")
    landed = [lax.dynamic_update_slice(l, b[None], (me, 0, 0)) for l, b in zip(landed, rest_back)]
    w_o, w_upT, w_dn, w_gate = landed[0].reshape(D, D), landed[1].reshape(DFF, D), landed[2].reshape(DFF, D), landed[3].reshape(D, D)
    w_projT = landed[4].reshape(D, PLE)

    mix = _mm(cat, w_o, tm=512, tn=1024, tk=1024, name="mix_out")

    def post1(xx, mm, ga, gb):
        h = xx + mm * _rstd(mm) * ga
        return (h, _rstd(h)), ()
    h1, r3 = _rowwise(post1, [x2, mix], [g2, g3], [(D, F32), (1, F32)], [], tm=512, name="post_mix")

    a_up, ff, u2, h2, h2b = _mlp_fwd(h1, r3, g3, w_upT, w_dn, g4)
    relu2 = lambda a: jnp.square(jnp.maximum(a.astype(F32), 0.0))

    gp = _mm(h2b, w_gate, tm=512, tn=1024, tk=1024, name="ple_gate")
    pp = _mm(p2b, w_projT, tb=True, tm=512, tn=1024, tk=256, name="ple_proj")

    def final(hh, gpre, ppv, tg, g):
        sg = _sigmoid(gpre)
        ple = ppv * sg
        r = _rstd(ple)
        n = ple * r
        h3 = hh + n * g
        e = h3 - tg
        dh3 = e * (1.0 / D)
        dple = _rms_bwd(n, r, g, dh3)
        return (dh3, dple * sg, dple * ppv * sg * (1.0 - sg)), (_colsum(dh3 * n), _colsum(0.5 * e * e * (1.0 / D)))
    dh3, dpp, dgp, dg5, loss_vec = _rowwise(final, [h2, gp, pp, tgt], [g5], [(D, F32), (D, BF16), (D, BF16)],
                                            [(1, D), (1, D)], tm=512, name="loss_ple_bwd")

    gw_projT = _mm(dpp, p2b, ta=True, tm=512, tn=256, tk=T, out_dtypes=(BF16,), name="gw_ple_proj")
    gw_gate = _mm(h2b, dgp, ta=True, tm=512, tn=1024, tk=T, out_dtypes=(BF16,), name="gw_ple_gate")
    rs_proj, tok_proj = _send_start(gw_projT.reshape(N_DEV, 32, D), True, "rs_start_w_proj", g1)
    rs_gate, tok_gate = _send_start(gw_gate.reshape(N_DEV, 128, D), True, "rs_start_w_gate", g1)
    dh2_g = _mm(dgp, w_gate, tb=True, tm=512, tn=1024, tk=1024, name="dx_ple_gate", deps=[tok_proj, tok_gate])

    def bwd_mlp_post(d3, dg_, f, g):
        dh2 = d3 + dg_
        r = _rstd(f)
        n = f * r
        return (dh2, _rms_bwd(n, r, g, dh2)), (_colsum(dh2 * n),)
    dh2, dff, dg4 = _rowwise(bwd_mlp_post, [dh3, dh2_g, ff], [g4], [(D, F32), (D, BF16)], [(1, D)], tm=512,
                             name="bwd_post_mlp")

    gw_dn = _mm(a_up, dff, ta=True, tm=512, tn=1024, tk=T, a_pre=relu2, out_dtypes=(BF16,), name="gw_mlp_down")
    rs_dn, tok_dn = _send_start(gw_dn.reshape(N_DEV, 512, D), True, "rs_start_w_down", g1)
    da_up, du2 = _mlp_dx(dff, a_up, w_upT, w_dn, tok_dn)
    gw_upT = _mm(da_up, u2, ta=True, tm=512, tn=1024, tk=T, out_dtypes=(BF16,), name="gw_mlp_up")
    rs_up, tok_up = _send_start(gw_upT.reshape(N_DEV, 512, D), True, "rs_start_w_up", g1)

    def bwd_mix_post(d2, du, hh, rr, mm, ga, gb):
        n3 = hh * rr
        dh1 = d2 + _rms_bwd(n3, rr, gb, du)
        r = _rstd(mm)
        n2 = mm * r
        return (dh1, _rms_bwd(n2, r, ga, dh1)), (_colsum(du * n3), _colsum(dh1 * n2))
    dh1, dmix, dg3, dg2 = _rowwise(bwd_mix_post, [dh2, du2, h1, r3, mix], [g2, g3], [(D, F32), (D, BF16)],
                                   [(1, D), (1, D)], tm=512, name="bwd_post_mix", deps=[tok_up])

    gw_o = _mm(cat, dmix, ta=True, tm=512, tn=1024, tk=T, out_dtypes=(BF16,), name="gw_out")
    rs_o, tok_o = _send_start(gw_o.reshape(N_DEV, 128, D), True, "rs_start_w_out", g1)
    dcat = _mm(dmix, w_o, tb=True, tm=512, tn=1024, tk=1024, name="dx_out", deps=[tok_o])

    def gated_bwd(y, z, dyn, gs):
        sg = _sigmoid(z)
        sz = z * sg
        gi = y * sz
        r = _rstd(gi)
        n = gi * r
        dgi = _rms_bwd(n, r, gs, dyn)
        return (dgi * sz, dgi * y * (sg * (1.0 + z * (1.0 - sg)))), (_colsum(dyn * n),)
    dy_ssd, dz, dgs = _rowwise(gated_bwd, [y_ssd, (qkvz, AW, 3), (dcat, AW, 1)], [ssd_norm_g], [(AW, F32)] * 2, [(1, AW)],
                               tm=512, name="bwd_gated_norm")

    dact, ddtw, ssd_par = _ssd_bwd(act, xbcdt, bias_w, alog_w, dsk_w, states, dy_ssd)
    dxbcdt, conv_par = _conv_bwd(xbcdt, dact, ddtw, conv_full, conv_b)

    dattn4, dattn16 = _dilate_cols(dcat, 0)
    qkv_grads = [_attn_bwd(*qkv[0], dcat, attn, lse, 1),
                 _attn_bwd(*qkv[1], dattn4, attn4, lse4, 4),
                 _attn_bwd(*qkv[2], dattn16, attn16, lse16, 16)]
    dqkvz = _rope_bwd(qkv_grads, dz, cos128, sin128)

    gw_qkvzT = _mm(dqkvz, u1, ta=True, tm=512, tn=1024, tk=T, out_dtypes=(BF16,), name="gw_qkvz")
    gw_xbcdtT = _mm(dxbcdt, u1, ta=True, tm=896, tn=1024, tk=T, out_dtypes=(BF16,), name="gw_xbcdt")
    gw_inT = jnp.concatenate([gw_qkvzT, gw_xbcdtT], axis=0)[:IN_W]
    gw_inT = jnp.pad(gw_inT.reshape(N_DEV, W_IN_SHARD, D), ((0, 0), (0, W_IN_SHARD_PAD - W_IN_SHARD), (0, 0)))
    rs_in, tok_in = _send_start(gw_inT, True, "rs_start_w_in", g1)

    du1a = _mm(dqkvz, w_qkvzT, tm=512, tn=1024, tk=2048, name="dx_qkvz", deps=[tok_in])
    du1b = _mm(dxbcdt, w_xbcdtT, tm=512, tn=1024, tk=896, name="dx_xbcdt")

    def bwd_in(d1, ua, ub, xx, g):
        rr = _rstd(xx)
        n = xx * rr
        du = ua + ub
        return (d1 + _rms_bwd(n, rr, g, du),), (_colsum(du * n),)
    grad_x, dg1 = _rowwise(bwd_in, [dh1, du1a, du1b, x2], [g1], [(D, F32)], [(1, D)], tm=512, name="bwd_pre_mix")

    slab = _all_reduce_small([(dg1, 0), (dg2, 1), (dg3, 2), (dg4, 3), (dg5, 4), (dgs, 5), (loss_vec, 6),
                              (conv_par, 8), (ssd_par, 16)], "reduce_small")
    g_conv_w = lax.dynamic_slice(slab[8:12, :CONV_CH], (0, me * 96), (CONV_K, 96))

    def scatter_finish(handles, nm, after):
        part, land = _send_wait(handles, after, "rs_wait_" + nm)
        own = lax.dynamic_slice(part, (me, 0, 0), (1,) + part.shape[1:])
        return _sum_slots(lax.dynamic_update_slice(land, own, (me, 0, 0)), "rs_sum_" + nm)
    g_out = scatter_finish(rs_o, "w_out", slab)
    g_upT = scatter_finish(rs_up, "w_up", slab)
    g_dn = scatter_finish(rs_dn, "w_down", slab)
    g_gate = scatter_finish(rs_gate, "w_gate", slab)
    g_projT = scatter_finish(rs_proj, "w_proj", slab)

    small_names = SMALL + ["conv_w"]
    small_rows = [0, 1, 2, 3, 4, 12, 5, 16, 17, 18, None]
    pick = lambda prefix: [args[prefix + nme] for nme in SMALL] + [args[prefix + "conv_w"][0]]
    loss11, g_s, d_s, m_s, v_s = _adamw_small(slab, small_rows, g_conv_w, pick(""), pick("m_"), pick("v_"))
    loss = loss11[0, 0]
    grads = {
        "w_out": g_out[None], "w_up": g_upT.T[None], "w_down": g_dn[None],
        "w_ple_gate": g_gate[None], "w_ple_proj": g_projT.reshape(128, PLE).T[None],
    }
    delta, new_m, new_v = {}, {}, {}
    for i, nme in enumerate(small_names):
        lead = (lambda t: t[None]) if nme == "conv_w" else (lambda t: t)
        grads[nme], delta[nme], new_m[nme], new_v[nme] = lead(g_s[i]), lead(d_s[i]), lead(m_s[i]), lead(v_s[i])
    for nme in ["w_out", "w_up", "w_down", "w_ple_gate", "w_ple_proj", "w_in"]:
        if nme == "w_in":
            g_inT = scatter_finish(rs_in, "w_in", delta["w_down"])
            grads["w_in"] = g_inT[:W_IN_SHARD].T[None]
        dl, mm_, vv_ = _adamw(args[nme][0], grads[nme][0], args["m_" + nme][0], args["v_" + nme][0], "adamw_" + nme)
        delta[nme], new_m[nme], new_v[nme] = dl[None], mm_[None], vv_[None]

    order = ["norm_mix_pre", "norm_mix_post", "w_in", "conv_w", "conv_b", "dt_bias", "a_log", "d_skip", "ssd_norm_g",
             "w_out", "norm_mlp_pre", "norm_mlp_post", "w_up", "w_down", "w_ple_gate", "w_ple_proj", "norm_ple_post"]
    return (loss, grad_x[None], *[grads[n] for n in order], *[delta[n] for n in order],
            *[new_m[n] for n in order], *[new_v[n] for n in order])
```

```python
import functools
import math

import jax
import jax.numpy as jnp
from jax import lax
from jax.experimental import pallas as pl
from jax.experimental.pallas import tpu as pltpu

F32 = jnp.float32
BF16 = jnp.bfloat16
MESH = pl.DeviceIdType.MESH
HIGHEST = lax.Precision.HIGHEST

N_DEV = 8
T = 4096
D = 1024
HEADS = 8
HD = 64
AW = 512
NS = 128
CONV_K = 4
CONV_CH = 768
CHUNK = 128
DFF = 4096
PLE = 256
EPS = 1e-6
ROPE_THETA = 10000.0
DILATIONS = (1, 4, 16)
QBLK = 128
NEG = -1e30
IN_W = 2824
W_IN_SHARD = 353
W_IN_SHARD_PAD = 384
DT_PAD = 128

ADAM_LR, ADAM_B1, ADAM_B2, ADAM_EPS, ADAM_WD, ADAM_STEP = 0.001, 0.9, 0.999, 1e-08, 0.01, 10

VMEM_LIMIT = 56 * 1024 * 1024


_ANY = pl.BlockSpec(memory_space=pl.ANY)


def _cparams(sem=None):
    return pltpu.CompilerParams(dimension_semantics=sem, vmem_limit_bytes=VMEM_LIMIT)


def _dot(a, b, ca, cb, precision=None):
    return lax.dot_general(a, b, (((ca,), (cb,)), ((), ())), preferred_element_type=F32, precision=precision)


def _nn(a, b):
    return _dot(a, b, 1, 0)


def _nt(a, b):
    return _dot(a, b, 1, 1)


def _tn(a, b):
    return _dot(a, b, 0, 0)


def _sigmoid(x):
    return 1.0 / (1.0 + jnp.exp(-x))


def _softplus(x):
    return jnp.maximum(x, 0.0) + jnp.log(1.0 + jnp.exp(-jnp.abs(x)))


def _mm(a, b, *, ta=False, tb=False, tm, tn, tk, name,
        a_pre=None, a_rows=(), a_cols=(), b_pre=None, b_rows=(), b_cols=(),
        epi=None, epi_tiles=(), out_dtypes=(F32,), deps=()):
    if ta:
        K, M = a.shape
    else:
        M, K = a.shape
    if tb:
        N, K2 = b.shape
    else:
        K2, N = b.shape
    assert K == K2 and M % tm == 0 and N % tn == 0 and K % tk == 0, (name, a.shape, b.shape)
    nk = K // tk
    if ta:
        a_spec = pl.BlockSpec((tk, tm), lambda i, j, k: (k, i))
        a_row_specs = [pl.BlockSpec((tk, 1), lambda i, j, k: (k, 0)) for _ in a_rows]
        a_col_specs = [pl.BlockSpec((1, tm), lambda i, j, k: (0, i)) for _ in a_cols]
    else:
        a_spec = pl.BlockSpec((tm, tk), lambda i, j, k: (i, k))
        a_row_specs = [pl.BlockSpec((tm, 1), lambda i, j, k: (i, 0)) for _ in a_rows]
        a_col_specs = [pl.BlockSpec((1, tk), lambda i, j, k: (0, k)) for _ in a_cols]
    if tb:
        b_spec = pl.BlockSpec((tn, tk), lambda i, j, k: (j, k))
        b_row_specs = [pl.BlockSpec((tn, 1), lambda i, j, k: (j, 0)) for _ in b_rows]
        b_col_specs = [pl.BlockSpec((1, tk), lambda i, j, k: (0, k)) for _ in b_cols]
    else:
        b_spec = pl.BlockSpec((tk, tn), lambda i, j, k: (k, j))
        b_row_specs = [pl.BlockSpec((tk, 1), lambda i, j, k: (k, 0)) for _ in b_rows]
        b_col_specs = [pl.BlockSpec((1, tn), lambda i, j, k: (0, j)) for _ in b_cols]
    o_spec = pl.BlockSpec((tm, tn), lambda i, j, k: (i, j))
    na, nb, ne, no = len(a_rows) + len(a_cols), len(b_rows) + len(b_cols), len(epi_tiles), len(out_dtypes)

    def body(*refs):
        a_ref, b_ref = refs[0], refs[1]
        a_ex = refs[2:2 + na]
        b_ex = refs[2 + na:2 + na + nb]
        e_ex = refs[2 + na + nb:2 + na + nb + ne]
        first_out = 2 + na + nb + ne + len(deps)
        outs = refs[first_out:first_out + no]

        def finish(res):
            vals = epi(res, *[r[...] for r in e_ex]) if epi is not None else (res,)
            for o_ref, val in zip(outs, vals):
                o_ref[...] = val.astype(o_ref.dtype)

        at = a_ref[...]
        if a_pre is not None:
            at = a_pre(at, *[r[...] for r in a_ex])
        bt = b_ref[...]
        if b_pre is not None:
            bt = b_pre(bt, *[r[...] for r in b_ex])
        prod = _dot(at.astype(BF16), bt.astype(BF16), 0 if ta else 1, 1 if tb else 0)
        if nk == 1:
            finish(prod)
            return
        acc = refs[-1]
        k = pl.program_id(2)

        @pl.when(k == 0)
        def _():
            acc[...] = jnp.zeros_like(acc)
        acc[...] += prod

        @pl.when(k == nk - 1)
        def _():
            finish(acc[...])

    outs = pl.pallas_call(
        body, name=name,
        grid=(M // tm, N // tn, nk),
        in_specs=([a_spec, b_spec] + a_row_specs + a_col_specs + b_row_specs + b_col_specs + [o_spec] * ne
                  + [_ANY] * len(deps)),
        out_specs=[o_spec] * no,
        out_shape=[jax.ShapeDtypeStruct((M, N), dt) for dt in out_dtypes],
        scratch_shapes=[pltpu.VMEM((tm, tn), F32)] if nk > 1 else [],
        compiler_params=_cparams(("parallel", "parallel", "arbitrary")),
    )(a, b, *a_rows, *a_cols, *b_rows, *b_cols, *epi_tiles, *deps)
    return outs[0] if no == 1 else outs


MLP_TM = 1024
MLP_TC = 512


def _mlp_fwd(h, r, g, w_upT, w_dn, g_post):
    nc = DFF // MLP_TC

    def body(h_ref, r_ref, g_ref, wu_ref, wd_ref, gp_ref, a_ref, ff_ref, u_ref, ho_ref, hob_ref, acc, u_scr):
        c = pl.program_id(1)

        @pl.when(c == 0)
        def _():
            u = (h_ref[...] * r_ref[...] * g_ref[...]).astype(BF16)
            u_scr[...] = u
            u_ref[...] = u
            acc[...] = jnp.zeros_like(acc)
        a = _nt(u_scr[...], wu_ref[...])
        a_ref[...] = a.astype(BF16)
        acc[...] += _nn(jnp.square(jnp.maximum(a, 0.0)).astype(BF16), wd_ref[...])

        @pl.when(c == nc - 1)
        def _():
            f = acc[...]
            ff_ref[...] = f
            ho = h_ref[...] + f * _rstd(f) * gp_ref[...]
            ho_ref[...] = ho
            hob_ref[...] = ho.astype(BF16)

    row = pl.BlockSpec((MLP_TM, D), lambda i, c: (i, 0))
    wsp = pl.BlockSpec((MLP_TC, D), lambda i, c: (c, 0))
    vec = pl.BlockSpec((1, D), lambda i, c: (0, 0))
    return pl.pallas_call(
        body, name="mlp_fwd", grid=(T // MLP_TM, nc),
        in_specs=[row, pl.BlockSpec((MLP_TM, 1), lambda i, c: (i, 0)), vec, wsp, wsp, vec],
        out_specs=[pl.BlockSpec((MLP_TM, MLP_TC), lambda i, c: (i, c)), row, row, row, row],
        out_shape=[jax.ShapeDtypeStruct((T, DFF), BF16), jax.ShapeDtypeStruct((T, D), F32), jax.ShapeDtypeStruct((T, D), BF16),
                   jax.ShapeDtypeStruct((T, D), F32), jax.ShapeDtypeStruct((T, D), BF16)],
        scratch_shapes=[pltpu.VMEM((MLP_TM, D), F32), pltpu.VMEM((MLP_TM, D), BF16)],
        compiler_params=_cparams(("parallel", "arbitrary")),
    )(h, r, g, w_upT, w_dn, g_post)


def _mlp_dx(dff, a, w_upT, w_dn, dep):
    nc = DFF // MLP_TC

    def body(d_ref, a_ref, wu_ref, wd_ref, dep_ref, da_ref, du_ref, acc, d_scr):
        c = pl.program_id(1)

        @pl.when(c == 0)
        def _():
            d_scr[...] = d_ref[...].astype(BF16)
            acc[...] = jnp.zeros_like(acc)
        da = (_nt(d_scr[...], wd_ref[...]) * (2.0 * jnp.maximum(a_ref[...].astype(F32), 0.0))).astype(BF16)
        da_ref[...] = da
        acc[...] += _nn(da, wu_ref[...])

        @pl.when(c == nc - 1)
        def _():
            du_ref[...] = acc[...]

    row = pl.BlockSpec((MLP_TM, D), lambda i, c: (i, 0))
    wsp = pl.BlockSpec((MLP_TC, D), lambda i, c: (c, 0))
    chunk = pl.BlockSpec((MLP_TM, MLP_TC), lambda i, c: (i, c))
    return pl.pallas_call(
        body, name="mlp_dx", grid=(T // MLP_TM, nc),
        in_specs=[row, chunk, wsp, wsp, _ANY], out_specs=[chunk, row],
        out_shape=[jax.ShapeDtypeStruct((T, DFF), BF16), jax.ShapeDtypeStruct((T, D), F32)],
        scratch_shapes=[pltpu.VMEM((MLP_TM, D), F32), pltpu.VMEM((MLP_TM, D), BF16)],
        compiler_params=_cparams(("parallel", "arbitrary")),
    )(dff, a, w_upT, w_dn, dep)


def _rowwise(fn, rows, vecs, out_rows, out_sums, *, tm, name, deps=()):
    specs, arrs = [], []
    R = None
    for r in rows:
        if isinstance(r, tuple):
            arr, width, cb = r
            specs.append(pl.BlockSpec((tm, width), lambda i, cb=cb: (i, cb)))
        else:
            arr = r
            specs.append(pl.BlockSpec((tm, arr.shape[1]), lambda i: (i, 0)))
        R = arr.shape[0] if R is None else R
        assert arr.shape[0] == R, name
        arrs.append(arr)
    assert R % tm == 0, name
    for v in vecs:
        specs.append(pl.BlockSpec(v.shape, lambda i: (0, 0)))
        arrs.append(v)
    nr, nv, no, ns = len(rows), len(vecs), len(out_rows), len(out_sums)
    out_specs = [pl.BlockSpec((tm, w), lambda i: (i, 0)) for w, _ in out_rows]
    out_specs += [pl.BlockSpec(s, lambda i: (0, 0)) for s in out_sums]
    out_shape = [jax.ShapeDtypeStruct((R, w), dt) for w, dt in out_rows]
    out_shape += [jax.ShapeDtypeStruct(s, F32) for s in out_sums]

    nd = len(deps)

    def body(*refs):
        ins = [r[...] for r in refs[:nr + nv]]
        o_refs = refs[nr + nv + nd:nr + nv + nd + no]
        s_refs = refs[nr + nv + nd + no:]
        o_vals, s_vals = fn(*ins)
        for ref, val in zip(o_refs, o_vals):
            ref[...] = val.astype(ref.dtype)
        if ns:
            @pl.when(pl.program_id(0) == 0)
            def _():
                for ref in s_refs:
                    ref[...] = jnp.zeros_like(ref)
            for ref, val in zip(s_refs, s_vals):
                ref[...] += val

    outs = pl.pallas_call(
        body, name=name, grid=(R // tm,), in_specs=specs + [_ANY] * nd, out_specs=out_specs, out_shape=out_shape,
        compiler_params=_cparams(("arbitrary",) if ns else ("parallel",)),
    )(*arrs, *deps)
    return outs


def _colsum(x):
    return jnp.sum(x, axis=0, keepdims=True)


def _rstd(x):
    return lax.rsqrt(jnp.mean(x * x, axis=-1, keepdims=True) + EPS)


def _rms_bwd(xn, r, g, dy):
    dn = dy * g
    return r * (dn - xn * jnp.mean(dn * xn, axis=-1, keepdims=True))


def _partner(t):
    lane = lax.broadcasted_iota(jnp.int32, t.shape, 1)
    up = pltpu.roll(t, 96, 1)
    down = pltpu.roll(t, 32, 1)
    return jnp.where((lane % 64) < 32, up, down)


SLABS = AW // 128


def _rows(r, n, d):
    return pl.ds(r, n, stride=d) if d > 1 else pl.ds(0, n)


def _undilate(src_ref, dst, d, tm):
    for r in range(d):
        for j in range(SLABS):
            dst[j][_rows(r, tm // d, d), :] = src_ref[:, pl.ds(r * AW + j * 128, 128)].astype(dst[j].dtype)


def _dilate(dst_ref, src, d, tm):
    for r in range(d):
        for j in range(SLABS):
            dst_ref[:, pl.ds(r * AW + j * 128, 128)] = src[j][_rows(r, tm // d, d), :].astype(dst_ref.dtype)


def _slab_scratch(n, tm):
    return [pltpu.VMEM((tm, 128), F32)] * (SLABS * n)


def _slab_groups(flat):
    return [flat[SLABS * i:SLABS * (i + 1)] for i in range(len(flat) // SLABS)]


def _slab_specs(tm, first):
    return [pl.BlockSpec((tm, 128), lambda i, j=j: (i, first + j)) for j in range(SLABS)]


def _dil_spec(tm, d):
    return pl.BlockSpec((tm // d, d * AW), lambda i: (i, 0))


ROPE_TM = 512


def _rope_fwd(qkvz, cos128, sin128):
    tm = ROPE_TM

    def body(*refs):
        q_refs, k_refs, v_refs = refs[0:4], refs[4:8], refs[8:12]
        c_ref, s_ref = refs[12], refs[13]
        outs = refs[14:23]
        qs, ks = _slab_groups(refs[23:])
        c, s = c_ref[...], s_ref[...]
        for j in range(SLABS):
            q, k = q_refs[j][...], k_refs[j][...]
            qs[j][...] = (q * c + _partner(q) * s) * (HD ** -0.5)
            ks[j][...] = k * c + _partner(k) * s
        for di, d in enumerate(DILATIONS):
            oq, ok, ov = outs[3 * di:3 * di + 3]
            for r in range(d):
                rows = _rows(r, tm // d, d)
                for j in range(SLABS):
                    cols = pl.ds(r * AW + j * 128, 128)
                    oq[:, cols] = qs[j][rows, :].astype(BF16)
                    ok[:, cols] = ks[j][rows, :].astype(BF16)
                    ov[:, cols] = v_refs[j][rows, :].astype(BF16)

    tab = pl.BlockSpec((tm, 128), lambda i: (i, 0))
    out_specs, out_shape = [], []
    for d in DILATIONS:
        out_specs += [_dil_spec(tm, d)] * 3
        out_shape += [jax.ShapeDtypeStruct((T // d, d * AW), BF16)] * 3
    return pl.pallas_call(
        body, name="rope_fwd", grid=(T // tm,),
        in_specs=_slab_specs(tm, 0) + _slab_specs(tm, 4) + _slab_specs(tm, 8) + [tab, tab],
        out_specs=out_specs, out_shape=out_shape, scratch_shapes=_slab_scratch(2, tm),
        compiler_params=_cparams(("parallel",)),
    )(*([qkvz] * 12), cos128, sin128)


def _rope_bwd(grads, dz, cos128, sin128):
    tm = 256

    def body(*refs):
        g_refs = refs[0:9]
        dz_ref, c_ref, s_ref, o_ref = refs[9], refs[10], refs[11], refs[12]
        scr = _slab_groups(refs[13:])
        for di, d in enumerate(DILATIONS[1:]):
            for t in range(3):
                _undilate(g_refs[3 * (di + 1) + t], scr[3 * di + t], d, tm)
        c, s = c_ref[...], s_ref[...]
        for j in range(SLABS):
            cols = pl.ds(j * 128, 128)
            tot = [g_refs[t][:, cols] + scr[t][j][...] + scr[3 + t][j][...] for t in range(3)]
            dqr = tot[0] * (HD ** -0.5)
            o_ref[:, pl.ds(j * 128, 128)] = (dqr * c + _partner(dqr * s)).astype(BF16)
            o_ref[:, pl.ds(AW + j * 128, 128)] = (tot[1] * c + _partner(tot[1] * s)).astype(BF16)
            o_ref[:, pl.ds(2 * AW + j * 128, 128)] = tot[2].astype(BF16)
        o_ref[:, pl.ds(3 * AW, AW)] = dz_ref[...].astype(BF16)

    tab = pl.BlockSpec((tm, 128), lambda i: (i, 0))
    in_specs, args = [], []
    for d, g in zip(DILATIONS, grads):
        in_specs += [_dil_spec(tm, d)] * 3
        args += list(g)
    return pl.pallas_call(
        body, name="rope_bwd", grid=(T // tm,),
        in_specs=in_specs + [pl.BlockSpec((tm, AW), lambda i: (i, 0)), tab, tab],
        out_specs=pl.BlockSpec((tm, 4 * AW), lambda i: (i, 0)),
        out_shape=jax.ShapeDtypeStruct((T, 4 * AW), BF16),
        scratch_shapes=_slab_scratch(6, tm),
        compiler_params=_cparams(("parallel",)),
    )(*args, dz, cos128, sin128)


def _dilate_cols(x, first):
    tm = ROPE_TM

    def body(x0, x1, x2, x3, o4, o16):
        xs = (x0, x1, x2, x3)
        for o_ref, d in ((o4, 4), (o16, 16)):
            for r in range(d):
                for j in range(SLABS):
                    o_ref[:, pl.ds(r * AW + j * 128, 128)] = xs[j][_rows(r, tm // d, d), :]

    return pl.pallas_call(
        body, name="dilate_cols", grid=(T // tm,),
        in_specs=_slab_specs(tm, first), out_specs=[_dil_spec(tm, 4), _dil_spec(tm, 16)],
        out_shape=[jax.ShapeDtypeStruct((T // 4, 4 * AW), F32), jax.ShapeDtypeStruct((T // 16, 16 * AW), F32)],
        compiler_params=_cparams(("parallel",)),
    )(x, x, x, x)


def _band_masks():
    qi = lax.broadcasted_iota(jnp.int32, (QBLK, QBLK), 0)
    kj = lax.broadcasted_iota(jnp.int32, (QBLK, QBLK), 1)
    return kj >= qi, kj <= qi


def _attn_fwd(q, k, v, d):
    L = q.shape[0]
    nb = L // QBLK

    def body(q_ref, kp_ref, kc_ref, vp_ref, vc_ref, o_ref, l_ref):
        n = pl.program_id(1)
        mask_p, mask_c = _band_masks()
        bias = jnp.concatenate([jnp.where(mask_p, 0.0, NEG) + jnp.where(n > 0, 0.0, NEG),
                                jnp.where(mask_c, 0.0, NEG)], axis=1)
        s = []
        for h in range(HEADS):
            sl = pl.ds(HD * h, HD)
            qh = q_ref[:, sl]
            s.append(jnp.concatenate([_nt(qh, kp_ref[:, sl]), _nt(qh, kc_ref[:, sl])], axis=1))
        s = jnp.stack(s) + bias
        m = jnp.max(s, axis=2, keepdims=True)
        e = jnp.exp(s - m)
        den = jnp.sum(e, axis=2, keepdims=True)
        p = e.astype(BF16)
        inv = 1.0 / den
        lse = m + jnp.log(den)
        for h in range(HEADS):
            sl = pl.ds(HD * h, HD)
            o_ref[:, sl] = (_nn(p[h, :, :QBLK], vp_ref[:, sl]) + _nn(p[h, :, QBLK:], vc_ref[:, sl])) * inv[h]
            l_ref[:, sl] = jnp.broadcast_to(lse[h], (QBLK, HD))

    cur = pl.BlockSpec((QBLK, AW), lambda r, n: (n, r))
    prev = pl.BlockSpec((QBLK, AW), lambda r, n: (jnp.maximum(n - 1, 0), r))
    return pl.pallas_call(
        body, name=f"attn_fwd_d{d}", grid=(d, nb),
        in_specs=[cur, prev, cur, prev, cur], out_specs=[cur, cur],
        out_shape=[jax.ShapeDtypeStruct((L, d * AW), F32)] * 2,
        compiler_params=_cparams(("parallel", "parallel")),
    )(q, k, k, v, v)


def _attn_bwd(q, k, v, do, at, lse, d):
    L = q.shape[0]
    nb = L // QBLK

    def body(q0_ref, q1_ref, kp_ref, kc_ref, vp_ref, vc_ref, do0_ref, do1_ref, at0_ref, at1_ref,
             l0_ref, l1_ref, dq_ref, dk_ref, dv_ref):
        n = pl.program_id(1)
        mask_p, mask_c = _band_masks()
        prev_bias = jnp.where(mask_p, 0.0, NEG)
        bias = jnp.concatenate([prev_bias + jnp.where(n > 0, 0.0, NEG), jnp.where(mask_c, 0.0, NEG),
                                prev_bias + jnp.where(n < nb - 1, 0.0, NEG)], axis=1)
        s, dp, ls, dl, ops = [], [], [], [], []
        for h in range(HEADS):
            sl = pl.ds(HD * h, HD)
            one = pl.ds(HD * h, 1)
            q0, q1 = q0_ref[:, sl], q1_ref[:, sl]
            kp, kc, vp, vc = kp_ref[:, sl], kc_ref[:, sl], vp_ref[:, sl], vc_ref[:, sl]
            do0, do1 = do0_ref[:, sl], do1_ref[:, sl]
            do0b, do1b = do0.astype(BF16), do1.astype(BF16)
            s.append(jnp.concatenate([_nt(q0, kp), _nt(q0, kc), _nt(q1, kc)], axis=1))
            dp.append(jnp.concatenate([_nt(do0b, vp), _nt(do0b, vc), _nt(do1b, vc)], axis=1))
            dl0 = jnp.sum(do0 * at0_ref[:, sl], axis=1, keepdims=True)
            dl1 = jnp.sum(do1 * at1_ref[:, sl], axis=1, keepdims=True)
            dl.append(jnp.concatenate([jnp.broadcast_to(dl0, (QBLK, 2 * QBLK)), jnp.broadcast_to(dl1, (QBLK, QBLK))], axis=1))
            ls.append(jnp.concatenate([jnp.broadcast_to(l0_ref[:, one], (QBLK, 2 * QBLK)),
                                       jnp.broadcast_to(l1_ref[:, one], (QBLK, QBLK))], axis=1))
            ops.append((q0, q1, kp, kc, do0b, do1b))
        p = jnp.exp(jnp.stack(s) + bias - jnp.stack(ls))
        ds = (p * (jnp.stack(dp) - jnp.stack(dl))).astype(BF16)
        p = p.astype(BF16)
        for h in range(HEADS):
            sl = pl.ds(HD * h, HD)
            q0, q1, kp, kc, do0b, do1b = ops[h]
            dq_ref[:, sl] = (_nn(ds[h, :, :QBLK], kp) + _nn(ds[h, :, QBLK:2 * QBLK], kc)).astype(BF16)
            dv_ref[:, sl] = (_tn(p[h, :, QBLK:2 * QBLK], do0b) + _tn(p[h, :, 2 * QBLK:], do1b)).astype(BF16)
            dk_ref[:, sl] = (_tn(ds[h, :, QBLK:2 * QBLK], q0) + _tn(ds[h, :, 2 * QBLK:], q1)).astype(BF16)

    cur = pl.BlockSpec((QBLK, AW), lambda r, n: (n, r))
    prev = pl.BlockSpec((QBLK, AW), lambda r, n: (jnp.maximum(n - 1, 0), r))
    nxt = pl.BlockSpec((QBLK, AW), lambda r, n: (jnp.minimum(n + 1, nb - 1), r))
    return pl.pallas_call(
        body, name=f"attn_bwd_d{d}", grid=(d, nb),
        in_specs=[cur, nxt, prev, cur, prev, cur, cur, nxt, cur, nxt, cur, nxt], out_specs=[cur, cur, cur],
        out_shape=[jax.ShapeDtypeStruct((L, d * AW), BF16)] * 3,
        compiler_params=_cparams(("parallel", "parallel")),
    )(q, q, k, k, v, v, do, do, at, at, lse, lse)


def _attn_merge(outs, lses):
    tm = ROPE_TM

    def body(o1, o4, o16, l1, l4, l16, at_ref, ls_ref, at4, ls4, at16, ls16, *flat):
        so4, so16, sl4, sl16, sa, sl = _slab_groups(flat)
        _undilate(o4, so4, 4, tm)
        _undilate(o16, so16, 16, tm)
        _undilate(l4, sl4, 4, tm)
        _undilate(l16, sl16, 16, tm)
        for j in range(SLABS):
            cols = pl.ds(j * 128, 128)
            a, b, c = l1[:, cols], sl4[j][...], sl16[j][...]
            m = jnp.maximum(jnp.maximum(a, b), c)
            e1, e2, e3 = jnp.exp(a - m), jnp.exp(b - m), jnp.exp(c - m)
            s = e1 + e2 + e3
            inv = 1.0 / s
            attn = (e1 * inv) * o1[:, cols] + (e2 * inv) * so4[j][...] + (e3 * inv) * so16[j][...]
            lse = m + jnp.log(s)
            at_ref[:, cols] = attn
            ls_ref[:, cols] = lse
            sa[j][...] = attn
            sl[j][...] = lse
        _dilate(at4, sa, 4, tm)
        _dilate(at16, sa, 16, tm)
        _dilate(ls4, sl, 4, tm)
        _dilate(ls16, sl, 16, tm)

    specs = [_dil_spec(tm, d) for d in DILATIONS]
    tok = specs[0]
    return pl.pallas_call(
        body, name="attn_merge", grid=(T // tm,),
        in_specs=specs + specs, out_specs=[tok, tok, specs[1], specs[1], specs[2], specs[2]],
        out_shape=[jax.ShapeDtypeStruct((T, AW), F32)] * 2 + [jax.ShapeDtypeStruct((T // 4, 4 * AW), F32)] * 2
        + [jax.ShapeDtypeStruct((T // 16, 16 * AW), F32)] * 2,
        scratch_shapes=_slab_scratch(6, tm),
        compiler_params=_cparams(("parallel",)),
    )(*outs, *lses)


CONV_TM = 512
HALO = 8


def _conv_pre(ext, w, b):
    y = b + w[3] * ext
    for kk in range(1, CONV_K):
        y = y + w[3 - kk] * pltpu.roll(ext, kk, 0)
    return y


def _rows_to_block(rows, n, width):
    ri = lax.broadcasted_iota(jnp.int32, (n, width), 0)
    out = jnp.zeros((n, width), F32)
    for j, r in enumerate(rows):
        out = out + jnp.where(ri == j, r, 0.0)
    return out


def _conv_fwd(xbc, w, b):
    nblk = T // CONV_TM

    def body(x_ref, h_ref, w_ref, b_ref, o_ref):
        i = pl.program_id(0)
        halo = jnp.where(i > 0, h_ref[...], 0.0)
        ext = jnp.concatenate([halo, x_ref[...]], axis=0)
        y = _conv_pre(ext, [w_ref[pl.ds(j, 1), :] for j in range(CONV_K)], b_ref[...])[HALO:]
        o_ref[...] = y * _sigmoid(y)

    return pl.pallas_call(
        body, name="conv_fwd", grid=(nblk,),
        in_specs=[pl.BlockSpec((CONV_TM, CONV_CH), lambda i: (i, 0)),
                  pl.BlockSpec((HALO, CONV_CH), lambda i: (jnp.maximum(i * (CONV_TM // HALO) - 1, 0), 0)),
                  pl.BlockSpec((CONV_K, CONV_CH), lambda i: (0, 0)),
                  pl.BlockSpec((1, CONV_CH), lambda i: (0, 0))],
        out_specs=pl.BlockSpec((CONV_TM, CONV_CH), lambda i: (i, 0)),
        out_shape=jax.ShapeDtypeStruct((T, CONV_CH), F32),
        compiler_params=_cparams(("parallel",)),
    )(xbc, xbc, w, b)


def _conv_bwd(xbc, dact, ddt, w, b):
    nblk = T // CONV_TM
    per = CONV_TM // HALO

    def body(x_ref, xb_ref, xa_ref, g_ref, ga_ref, ddt_ref, w_ref, b_ref, dx_ref, dw_ref):
        i = pl.program_id(0)
        wv = [w_ref[pl.ds(j, 1), :] for j in range(CONV_K)]
        before = jnp.where(i > 0, xb_ref[...], 0.0)
        last = i == nblk - 1
        after = jnp.where(last, 0.0, xa_ref[...])
        g_after = jnp.where(last, 0.0, ga_ref[...])
        ext = jnp.concatenate([before, x_ref[...], after], axis=0)
        y = _conv_pre(ext, wv, b_ref[...])[HALO:]
        sg = _sigmoid(y)
        dy = jnp.concatenate([g_ref[...], g_after], axis=0) * (sg * (1.0 + y * (1.0 - sg)))
        n = CONV_TM + HALO
        dx = wv[3] * dy
        for kk in range(1, CONV_K):
            dx = dx + wv[3 - kk] * pltpu.roll(dy, n - kk, 0)
        dx_ref[:, pl.ds(0, CONV_CH)] = dx[:CONV_TM].astype(BF16)
        dx_ref[:, pl.ds(CONV_CH, DT_PAD)] = ddt_ref[...].astype(BF16)
        dyc = dy[:CONV_TM]
        rows = [jnp.sum(dyc * (pltpu.roll(ext, 3 - j, 0) if j < 3 else ext)[HALO:HALO + CONV_TM], axis=0, keepdims=True)
                for j in range(CONV_K)]
        rows.append(jnp.sum(dyc, axis=0, keepdims=True))
        part = _rows_to_block(rows, 8, CONV_CH)

        @pl.when(i == 0)
        def _():
            dw_ref[...] = jnp.zeros_like(dw_ref)
        dw_ref[...] += part

    blk = pl.BlockSpec((CONV_TM, CONV_CH), lambda i: (i, 0))
    hb = pl.BlockSpec((HALO, CONV_CH), lambda i: (jnp.maximum(i * per - 1, 0), 0))
    ha = pl.BlockSpec((HALO, CONV_CH), lambda i: (jnp.minimum((i + 1) * per, T // HALO - 1), 0))
    return pl.pallas_call(
        body, name="conv_bwd", grid=(nblk,),
        in_specs=[blk, hb, ha, blk, ha, pl.BlockSpec((CONV_TM, DT_PAD), lambda i: (i, 0)),
                  pl.BlockSpec((CONV_K, CONV_CH), lambda i: (0, 0)), pl.BlockSpec((1, CONV_CH), lambda i: (0, 0))],
        out_specs=[pl.BlockSpec((CONV_TM, CONV_CH + DT_PAD), lambda i: (i, 0)), pl.BlockSpec((8, CONV_CH), lambda i: (0, 0))],
        out_shape=[jax.ShapeDtypeStruct((T, CONV_CH + DT_PAD), BF16), jax.ShapeDtypeStruct((8, CONV_CH), F32)],
        compiler_params=_cparams(("arbitrary",)),
    )(xbc, xbc, xbc, dact, dact, ddt, w, b)


def _pick(mat, h):
    lane = lax.broadcasted_iota(jnp.int32, mat.shape, 1)
    return jnp.sum(jnp.where(lane == h, mat, 0.0), axis=1, keepdims=True)


def _heads(fn):
    return jnp.stack([fn(h) for h in range(HEADS)])


def _ssd_prep(dt_ref, bias_ref, alog_ref, dsk_ref, b_ref, c_ref, xs_ref, state_ref, cst):
    li = lax.broadcasted_iota(jnp.int32, (CHUNK, CHUNK), 0)
    si = lax.broadcasted_iota(jnp.int32, (CHUNK, CHUNK), 1)
    tri = li >= si
    dtp = dt_ref[...] + bias_ref[...]
    dt = _softplus(dtp)
    A = -jnp.exp(alog_ref[...])
    a = dt * A
    cs = jnp.dot(tri.astype(F32), a, precision=HIGHEST, preferred_element_type=F32)
    cst[...] = cs.T
    Bm = b_ref[...].astype(BF16)
    Cm = c_ref[...].astype(BF16)
    cb = _nt(Cm, Bm)
    dskv = dsk_ref[...]
    cs_col = _heads(lambda h: _pick(cs, h))
    cs_row = _heads(lambda h: cst[pl.ds(h, 1), :])
    dt_col = _heads(lambda h: _pick(dt, h))
    dsk_col = _heads(lambda h: _pick(dskv, h))
    lam = jnp.exp(jnp.where(tri, cs_col - cs_row, NEG))
    x = _heads(lambda h: xs_ref[:, pl.ds(HD * h, HD)])
    xdt = x * dt_col
    prev = _heads(lambda h: state_ref[pl.ds(HD * h, HD), :])
    lane = lax.broadcasted_iota(jnp.int32, (1, 1, CHUNK), 2)
    cl = jnp.sum(jnp.where(lane == CHUNK - 1, cs_row, 0.0), axis=2, keepdims=True)
    f = jnp.exp(cl - cs_col)
    return dict(li=li, si=si, dtp=dtp, dt=dt, A=A, Bm=Bm, Cm=Cm, cb=cb, cs_col=cs_col, dt_col=dt_col, dsk_col=dsk_col,
                lam=lam, x=x, xdt=xdt, prev=prev, cl=cl, f=f)


def _ssd_fwd(act, xbcdt, bias, alog, dsk):
    nc = T // CHUNK

    def body(xs_ref, b_ref, c_ref, dt_ref, bias_ref, alog_ref, dsk_ref, y_ref, st_ref, state, cst):
        @pl.when(pl.program_id(0) == 0)
        def _():
            state[...] = jnp.zeros_like(state)
        st_ref[...] = state[...]
        s = _ssd_prep(dt_ref, bias_ref, alog_ref, dsk_ref, b_ref, c_ref, xs_ref, state, cst)
        Bm, Cm, prev = s["Bm"], s["Cm"], s["prev"]
        g = (s["cb"] * s["lam"]).astype(BF16)
        xdtb = s["xdt"].astype(BF16)
        prevb = prev.astype(BF16)
        y = _heads(lambda h: _nn(g[h], xdtb[h])) + _heads(lambda h: _nt(Cm, prevb[h])) * jnp.exp(s["cs_col"])
        y = y + s["dsk_col"] * s["x"]
        xf = (s["xdt"] * s["f"]).astype(BF16)
        new = prev * jnp.exp(s["cl"]) + _heads(lambda h: _tn(xf[h], Bm))
        for h in range(HEADS):
            y_ref[:, pl.ds(HD * h, HD)] = y[h]
            state[pl.ds(HD * h, HD), :] = new[h]

    vec = pl.BlockSpec((1, DT_PAD), lambda c: (0, 0))
    return pl.pallas_call(
        body, name="ssd_fwd", grid=(nc,),
        in_specs=[pl.BlockSpec((CHUNK, AW), lambda c: (c, 0)), pl.BlockSpec((CHUNK, NS), lambda c: (c, 4)),
                  pl.BlockSpec((CHUNK, NS), lambda c: (c, 5)), pl.BlockSpec((CHUNK, DT_PAD), lambda c: (c, 6)),
                  vec, vec, vec],
        out_specs=[pl.BlockSpec((CHUNK, AW), lambda c: (c, 0)), pl.BlockSpec((None, AW, NS), lambda c: (c, 0, 0))],
        out_shape=[jax.ShapeDtypeStruct((T, AW), F32), jax.ShapeDtypeStruct((nc, AW, NS), F32)],
        scratch_shapes=[pltpu.VMEM((AW, NS), F32), pltpu.VMEM((CHUNK, CHUNK), F32)],
        compiler_params=_cparams(("arbitrary",)),
    )(act, act, act, xbcdt, bias, alog, dsk)


def _ssd_bwd(act, xbcdt, bias, alog, dsk, states, dy):
    nc = T // CHUNK

    def body(xs_ref, b_ref, c_ref, dt_ref, bias_ref, alog_ref, dsk_ref, st_ref, dy_ref,
             dact_ref, ddt_ref, par_ref, dstate, cst):
        step = pl.program_id(0)

        @pl.when(step == 0)
        def _():
            dstate[...] = jnp.zeros_like(dstate)
            par_ref[...] = jnp.zeros_like(par_ref)
        s = _ssd_prep(dt_ref, bias_ref, alog_ref, dsk_ref, b_ref, c_ref, xs_ref, st_ref, cst)
        Bm, Cm, prev, lam, x, xdt, f, cl = s["Bm"], s["Cm"], s["prev"], s["lam"], s["x"], s["xdt"], s["f"], s["cl"]
        lane = lax.broadcasted_iota(jnp.int32, (1, DT_PAD), 1)
        row = lax.broadcasted_iota(jnp.int32, (1, CHUNK, 1), 1)
        g = s["cb"] * lam
        gb, xdtb, prevb = g.astype(BF16), xdt.astype(BF16), prev.astype(BF16)
        dy = _heads(lambda h: dy_ref[:, pl.ds(HD * h, HD)])
        dyb = dy.astype(BF16)
        dnew = _heads(lambda h: dstate[pl.ds(HD * h, HD), :])
        dnewb = dnew.astype(BF16)
        E = jnp.exp(s["cs_col"])
        ecl = jnp.exp(cl)
        dG = _heads(lambda h: _nt(dyb[h], xdtb[h]))
        dxdt = _heads(lambda h: _tn(gb[h], dyb[h]))
        Yo = _heads(lambda h: _nt(Cm, prevb[h]))
        W = _heads(lambda h: _nt(Bm, dnewb[h]))
        dcb = jnp.sum(dG * lam, axis=0)
        Mm = dG * g
        col_sums = jnp.sum(Mm, axis=1, keepdims=True)
        dYo = (dy * E).astype(BF16)
        dxdt = dxdt + W * f
        dF = jnp.sum(W * xdt, axis=2, keepdims=True) * f
        dcl = jnp.sum(dnew * prev, axis=(1, 2), keepdims=True) * ecl + jnp.sum(dF, axis=1, keepdims=True)
        dcs = (jnp.sum(Mm, axis=2, keepdims=True) + jnp.sum(dy * Yo, axis=2, keepdims=True) * E - dF
               + jnp.where(row == CHUNK - 1, dcl, 0.0))
        ddt_x = jnp.sum(dxdt * x, axis=2, keepdims=True)
        dD = jnp.sum(dy * x, axis=(1, 2), keepdims=True)
        dx = s["dsk_col"] * dy + dxdt * s["dt_col"]
        xfb = (xdt * f).astype(BF16)
        dprev = _heads(lambda h: _tn(dYo[h], Cm)) + dnew * ecl
        dcbb = dcb.astype(BF16)
        dC = _nn(dcbb, Bm)
        dB = _tn(dcbb, Cm)
        dcs_mat = -_rows_to_block([col_sums[h] for h in range(HEADS)], CHUNK, CHUNK).T
        ddt_mat = jnp.zeros((CHUNK, DT_PAD), F32)
        dD_row = jnp.zeros((1, DT_PAD), F32)
        for h in range(HEADS):
            sl = pl.ds(HD * h, HD)
            dC = dC + _nn(dYo[h], prevb[h])
            dB = dB + _nn(xfb[h], dnewb[h])
            dcs_mat = dcs_mat + jnp.where(lane == h, dcs[h], 0.0)
            ddt_mat = ddt_mat + jnp.where(lane == h, ddt_x[h], 0.0)
            dD_row = dD_row + jnp.where(lane == h, dD[h], 0.0)
            dact_ref[:, sl] = dx[h]
            dstate[sl, :] = dprev[h]
        dact_ref[:, pl.ds(AW, NS)] = dB
        dact_ref[:, pl.ds(AW + NS, NS)] = dC
        da = jnp.dot((s["li"] <= s["si"]).astype(F32), dcs_mat, precision=HIGHEST, preferred_element_type=F32)
        ddtp = jnp.where(lane < HEADS, (ddt_mat + da * s["A"]) * _sigmoid(s["dtp"]), 0.0)
        ddt_ref[...] = ddtp
        dalog = jnp.where(lane < HEADS, jnp.sum(da * s["dt"], axis=0, keepdims=True) * s["A"], 0.0)
        par_ref[...] += _rows_to_block([jnp.sum(ddtp, axis=0, keepdims=True), dalog, dD_row], 8, DT_PAD)

    vec = pl.BlockSpec((1, DT_PAD), lambda c: (0, 0))
    rev = lambda c: nc - 1 - c
    return pl.pallas_call(
        body, name="ssd_bwd", grid=(nc,),
        in_specs=[pl.BlockSpec((CHUNK, AW), lambda c: (rev(c), 0)), pl.BlockSpec((CHUNK, NS), lambda c: (rev(c), 4)),
                  pl.BlockSpec((CHUNK, NS), lambda c: (rev(c), 5)), pl.BlockSpec((CHUNK, DT_PAD), lambda c: (rev(c), 6)),
                  vec, vec, vec,
                  pl.BlockSpec((None, AW, NS), lambda c: (rev(c), 0, 0)), pl.BlockSpec((CHUNK, AW), lambda c: (rev(c), 0))],
        out_specs=[pl.BlockSpec((CHUNK, CONV_CH), lambda c: (rev(c), 0)), pl.BlockSpec((CHUNK, DT_PAD), lambda c: (rev(c), 0)),
                   pl.BlockSpec((8, DT_PAD), lambda c: (0, 0))],
        out_shape=[jax.ShapeDtypeStruct((T, CONV_CH), F32), jax.ShapeDtypeStruct((T, DT_PAD), F32),
                   jax.ShapeDtypeStruct((8, DT_PAD), F32)],
        scratch_shapes=[pltpu.VMEM((AW, NS), F32), pltpu.VMEM((CHUNK, CHUNK), F32)],
        compiler_params=_cparams(("arbitrary",)),
    )(act, act, act, xbcdt, bias, alog, dsk, states, dy)


def _place():
    return lax.axis_index("x"), lax.axis_index("y"), lax.axis_index("c")


def _slot(px, py, pc):
    return 4 * px + 2 * py + pc


def _all_gather(arrs, name):
    na = len(arrs)

    def body(*refs):
        ins, outs = refs[:na], refs[na:2 * na]
        send_sems, recv_sems, local_sems = refs[2 * na:]
        x, y, c = _place()
        me, sib = (x, y, c), (x, y, 1 - c)
        chips = [(1 - x, y), (x, 1 - y), (1 - x, 1 - y)]

        def copy(a, kk, block, to, src=None):
            dst = outs[a].at[_slot(*block)]
            return pltpu.make_async_remote_copy(
                src_ref=dst if src is None else src, dst_ref=dst,
                send_sem=send_sems.at[a, kk], recv_sem=recv_sems.at[a, kk], device_id=to, device_id_type=MESH)

        mine = [pltpu.make_async_copy(ins[a], outs[a].at[_slot(*me)], local_sems.at[a]) for a in range(na)]
        for cp in mine:
            cp.start()
        first = []
        for a in range(na):
            first.append(copy(a, 0, me, sib, src=ins[a]))
            first += [copy(a, 1 + j, me, (*chip, c), src=ins[a]) for j, chip in enumerate(chips)]
        for cp in first:
            cp.start()
        passed = []
        for j, chip in enumerate(chips):
            for a in range(na):
                copy(a, 1 + j, (*chip, c), me).wait_recv()
                fw = copy(a, 4 + j, (*chip, c), sib)
                fw.start()
                passed.append(fw)
        for a in range(na):
            copy(a, 0, sib, me).wait_recv()
            for j, chip in enumerate(chips):
                copy(a, 4 + j, (*chip, 1 - c), me).wait_recv()
        for cp in first + passed:
            cp.wait_send()
        for cp in mine:
            cp.wait()

    any_spec = pl.BlockSpec(memory_space=pl.ANY)
    return pl.pallas_call(
        body, name=name,
        in_specs=[any_spec] * na, out_specs=[any_spec] * na,
        out_shape=[jax.ShapeDtypeStruct((N_DEV,) + a.shape, a.dtype) for a in arrs],
        scratch_shapes=[pltpu.SemaphoreType.DMA((na, 7)), pltpu.SemaphoreType.DMA((na, 7)),
                        pltpu.SemaphoreType.DMA((na,))],
    )(*arrs)


def _reduce_scatter(part, name):
    _, r, C = part.shape

    def body(part_ref, out_ref, own, got_sib, chip_sum, got_ici, lsem, s1, r1, s2, r2):
        x, y, c = _place()
        chips = [(x, y), (1 - x, y), (x, 1 - y), (1 - x, 1 - y)]
        loc = [pltpu.make_async_copy(part_ref.at[_slot(*chips[kk], c)], own.at[kk], lsem.at[kk]) for kk in range(4)]
        d2d = [pltpu.make_async_remote_copy(
            src_ref=part_ref.at[_slot(*chips[kk], 1 - c)], dst_ref=got_sib.at[kk],
            send_sem=s1.at[kk], recv_sem=r1.at[kk], device_id=(x, y, 1 - c), device_id_type=MESH) for kk in range(4)]
        for cp in loc + d2d:
            cp.start()
        ici = [pltpu.make_async_remote_copy(
            src_ref=chip_sum.at[kk - 1], dst_ref=got_ici.at[kk - 1],
            send_sem=s2.at[kk - 1], recv_sem=r2.at[kk - 1], device_id=(*chips[kk], c), device_id_type=MESH)
            for kk in range(1, 4)]
        for kk in (1, 2, 3):
            loc[kk].wait()
            d2d[kk].wait_recv()
            chip_sum[kk - 1] = (own[kk].astype(F32) + got_sib[kk].astype(F32)).astype(BF16)
            ici[kk - 1].start()
        loc[0].wait()
        d2d[0].wait_recv()
        acc = own[0].astype(F32) + got_sib[0].astype(F32)
        for cp in ici:
            cp.wait_recv()
        out_ref[...] = ((acc + got_ici[0].astype(F32)) + got_ici[1].astype(F32)) + got_ici[2].astype(F32)
        for cp in d2d + ici:
            cp.wait_send()

    return pl.pallas_call(
        body, name=name,
        in_specs=[pl.BlockSpec(memory_space=pl.ANY)],
        out_specs=pl.BlockSpec(memory_space=pltpu.VMEM),
        out_shape=jax.ShapeDtypeStruct((r, C), F32),
        scratch_shapes=[pltpu.VMEM((4, r, C), BF16), pltpu.VMEM((4, r, C), BF16), pltpu.VMEM((3, r, C), BF16),
                        pltpu.VMEM((3, r, C), BF16),
                        pltpu.SemaphoreType.DMA((4,)), pltpu.SemaphoreType.DMA((4,)), pltpu.SemaphoreType.DMA((4,)),
                        pltpu.SemaphoreType.DMA((3,)), pltpu.SemaphoreType.DMA((3,))],
        compiler_params=pltpu.CompilerParams(vmem_limit_bytes=VMEM_LIMIT),
    )(part)


SLAB_ROWS = 24


def _all_reduce_small(parts, name):
    R, C = SLAB_ROWS, D
    n = len(parts)

    def body(*refs):
        in_refs = refs[:n]
        out_ref, slab, got, send_sems, recv_sems = refs[n:]
        slab[...] = jnp.zeros_like(slab)
        for ref, (arr, row) in zip(in_refs, parts):
            slab[pl.ds(row, arr.shape[0]), pl.ds(0, arr.shape[1])] = ref[...]
        x, y, c = _place()
        mine = _slot(x, y, c)
        copies = [pltpu.make_async_remote_copy(
            src_ref=slab, dst_ref=got.at[mine], send_sem=send_sems.at[kk], recv_sem=recv_sems.at[kk],
            device_id=peer, device_id_type=MESH) for kk, peer in enumerate(_peers(x, y, c))]
        for cp in copies:
            cp.start()
        got[mine] = slab[...]
        for cp in copies:
            cp.wait_recv()
        acc = got[0]
        for s in range(1, N_DEV):
            acc = acc + got[s]
        out_ref[...] = acc
        for cp in copies:
            cp.wait_send()

    vm = pl.BlockSpec(memory_space=pltpu.VMEM)
    return pl.pallas_call(
        body, name=name, in_specs=[vm] * n, out_specs=vm,
        out_shape=jax.ShapeDtypeStruct((R, C), F32),
        scratch_shapes=[pltpu.VMEM((R, C), F32), pltpu.VMEM((N_DEV, R, C), F32), pltpu.SemaphoreType.DMA((N_DEV - 1,)),
                        pltpu.SemaphoreType.DMA((N_DEV - 1,))],
    )(*[a for a, _ in parts])


_HBM = pl.BlockSpec(memory_space=pltpu.HBM)
_SEM = pl.BlockSpec(memory_space=pltpu.SEMAPHORE)
_EFFECT = pltpu.SideEffectType.DATAFLOW_SIDE_EFFECTING


def _peers(x, y, c):
    out = []
    for kk in range(1, N_DEV):
        fx, fy, fc = kk >> 2 & 1, kk >> 1 & 1, kk & 1
        out.append((1 - x if fx else x, 1 - y if fy else y, 1 - c if fc else c))
    return out


def _send_start(src, per_peer, name, dep):
    (handles, token) = _send_start_many([src], per_peer, name, dep)
    return handles, token


def _send_start_many(srcs, per_peer, name, dep):
    n = len(srcs)

    def body(*refs):
        src_refs, land_refs = refs[:n], refs[n:2 * n]
        send_sems, recv_sems = refs[2 * n + 1], refs[2 * n + 2]
        token = refs[-1]
        x, y, c = _place()
        mine = _slot(x, y, c)
        for a in range(n):
            for kk, peer in enumerate(_peers(x, y, c)):
                pltpu.make_async_remote_copy(
                    src_ref=src_refs[a].at[_slot(*peer)] if per_peer else src_refs[a], dst_ref=land_refs[a].at[mine],
                    send_sem=send_sems.at[a * (N_DEV - 1) + kk], recv_sem=recv_sems.at[a * (N_DEV - 1) + kk],
                    device_id=peer, device_id_type=MESH).start()
        token[...] = jnp.zeros_like(token)

    lands = [lax.empty((N_DEV,) + tuple(s.shape[1:] if per_peer else s.shape), s.dtype) for s in srcs]
    hbm = lambda t: pltpu.with_memory_space_constraint(t, pltpu.HBM)
    outs = pl.pallas_call(
        body, name=name,
        out_shape=(pltpu.SemaphoreType.DMA((n * (N_DEV - 1),)), pltpu.SemaphoreType.DMA((n * (N_DEV - 1),)),
                   *[pltpu.HBM(s.shape, s.dtype) for s in srcs], *[pltpu.HBM(l.shape, l.dtype) for l in lands],
                   jax.ShapeDtypeStruct((8, 128), F32)),
        in_specs=(*[_HBM] * (2 * n), _ANY),
        out_specs=(_SEM, _SEM, *[_HBM] * (2 * n), pl.BlockSpec(memory_space=pltpu.VMEM)),
        input_output_aliases={i: 2 + i for i in range(2 * n)},
        compiler_params=pltpu.CompilerParams(has_side_effects=_EFFECT),
    )(*[hbm(s) for s in srcs], *[hbm(l) for l in lands], dep)
    return (outs[0], outs[1], list(outs[2:2 + n]), list(outs[2 + n:2 + 2 * n])), outs[-1]


def _send_wait(handles, after, name):
    srcs, lands = _send_wait_many(handles, after, name)
    return srcs[0], lands[0]


def _send_wait_many(handles, after, name):
    send_sems, recv_sems, src_thrus, land_thrus = handles
    n = len(src_thrus)

    def body(*refs):
        land_refs = refs[n:2 * n]
        send_sems, recv_sems = refs[2 * n], refs[2 * n + 1]
        me = _place()
        for a in range(n):
            for kk in range(N_DEV - 1):
                cp = pltpu.make_async_remote_copy(
                    src_ref=land_refs[a].at[0], dst_ref=land_refs[a].at[0],
                    send_sem=send_sems.at[a * (N_DEV - 1) + kk], recv_sem=recv_sems.at[a * (N_DEV - 1) + kk],
                    device_id=me, device_id_type=MESH)
                cp.wait_send()
                cp.wait_recv()

    both = list(src_thrus) + list(land_thrus)
    outs = pl.pallas_call(
        body, name=name,
        out_shape=tuple(pltpu.HBM(t.shape, t.dtype) for t in both),
        in_specs=(*[_HBM] * (2 * n), _SEM, _SEM, _ANY), out_specs=tuple([_HBM] * (2 * n)),
        input_output_aliases={i: i for i in range(2 * n)},
        compiler_params=pltpu.CompilerParams(has_side_effects=_EFFECT),
    )(*both, send_sems, recv_sems, after)
    return list(outs[:n]), list(outs[n:])


def _sum_slots(land, name):
    _, R, C = land.shape
    tm = R if R <= 512 else 512

    def body(x_ref, o_ref):
        acc = x_ref[0].astype(F32)
        for j in range(1, N_DEV):
            acc = acc + x_ref[j].astype(F32)
        o_ref[...] = acc

    return pl.pallas_call(
        body, name=name, grid=(R // tm,),
        in_specs=[pl.BlockSpec((N_DEV, tm, C), lambda i: (0, i, 0))], out_specs=pl.BlockSpec((tm, C), lambda i: (i, 0)),
        out_shape=jax.ShapeDtypeStruct((R, C), F32), compiler_params=_cparams(("parallel",)),
    )(land)


def _adam_math(w, g, m, v):
    m2 = ADAM_B1 * m + (1.0 - ADAM_B1) * g
    v2 = ADAM_B2 * v + (1.0 - ADAM_B2) * (g * g)
    m_hat = m2 / (1.0 - ADAM_B1 ** ADAM_STEP)
    v_hat = v2 / (1.0 - ADAM_B2 ** ADAM_STEP)
    delta = -ADAM_LR * (m_hat / (jnp.sqrt(v_hat) + ADAM_EPS) + ADAM_WD * w)
    return delta, m2, v2


def _adamw(w, g, m, v, name):
    R, C = w.shape
    tm = R if R <= 512 else 256
    return _rowwise(lambda w, g, m, v: (_adam_math(w, g, m, v), ()), [w, g, m, v], [], [(C, F32)] * 3, [], tm=tm, name=name)


def _adamw_small(slab, slab_rows, g_conv_w, ws, ms, vs):
    n = len(ws)

    def body(*refs):
        slab_ref, gc_ref = refs[0], refs[1]
        w_refs, m_refs, v_refs = refs[2:2 + n], refs[2 + n:2 + 2 * n], refs[2 + 2 * n:2 + 3 * n]
        outs = refs[2 + 3 * n:]
        loss_ref = outs[0]
        g_out, d_out, m_out, v_out = (outs[1 + i * n:1 + (i + 1) * n] for i in range(4))
        loss_ref[...] = jnp.sum(slab_ref[pl.ds(6, 1), :], axis=1, keepdims=True)
        for i in range(n):
            g = gc_ref[...] if i == n - 1 else slab_ref[pl.ds(slab_rows[i], 1), pl.ds(0, ws[i].shape[1])]
            d, m2, v2 = _adam_math(w_refs[i][...], g, m_refs[i][...], v_refs[i][...])
            g_out[i][...] = g
            d_out[i][...] = d
            m_out[i][...] = m2
            v_out[i][...] = v2

    vm = pl.BlockSpec(memory_space=pltpu.VMEM)
    shapes = [jax.ShapeDtypeStruct(w.shape, F32) for w in ws]
    outs = pl.pallas_call(
        body, name="adamw_small", in_specs=[vm] * (2 + 3 * n), out_specs=[vm] * (1 + 4 * n),
        out_shape=[jax.ShapeDtypeStruct((1, 1), F32)] + shapes * 4,
    )(slab, g_conv_w, *ws, *ms, *vs)
    return outs[0], outs[1:1 + n], outs[1 + n:1 + 2 * n], outs[1 + 2 * n:1 + 3 * n], outs[1 + 3 * n:]


SMALL = ["norm_mix_pre", "norm_mix_post", "norm_mlp_pre", "norm_mlp_post", "norm_ple_post",
         "conv_b", "ssd_norm_g", "dt_bias", "a_log", "d_skip"]


def _pad_row(v, width=D):
    return jnp.pad(v, ((0, 0), (0, width - v.shape[1])))


def kernel(x, p, positions, norm_mix_pre, norm_mix_post, w_in, conv_w, conv_b, dt_bias, a_log, d_skip, ssd_norm_g, w_out, norm_mlp_pre, norm_mlp_post, w_up, w_down, w_ple_gate, w_ple_proj, norm_ple_post, loss_target, m_norm_mix_pre, m_norm_mix_post, m_w_in, m_conv_w, m_conv_b, m_dt_bias, m_a_log, m_d_skip, m_ssd_norm_g, m_w_out, m_norm_mlp_pre, m_norm_mlp_post, m_w_up, m_w_down, m_w_ple_gate, m_w_ple_proj, m_norm_ple_post, v_norm_mix_pre, v_norm_mix_post, v_w_in, v_conv_w, v_conv_b, v_dt_bias, v_a_log, v_d_skip, v_ssd_norm_g, v_w_out, v_norm_mlp_pre, v_norm_mlp_post, v_w_up, v_w_down, v_w_ple_gate, v_w_ple_proj, v_norm_ple_post):
    args = dict(locals())
    x2, p2, tgt = x[0], p[0, 0], loss_target[0]
    g1, g2, g3, g4, g5 = norm_mix_pre, norm_mix_post, norm_mlp_pre, norm_mlp_post, norm_ple_post

    me = _slot(*_place())
    pack_in = jnp.pad(w_in[0].T, ((0, W_IN_SHARD_PAD - W_IN_SHARD), (0, 0))).astype(BF16)
    rest = [w_out[0].astype(BF16), w_up[0].T.astype(BF16), w_down[0].astype(BF16), w_ple_gate[0].astype(BF16),
            w_ple_proj[0].T.reshape(32, D).astype(BF16)]
    conv_pack = jnp.pad(conv_w[0], ((0, 4), (0, 32)))
    in_handles, tok_in0 = _send_start_many([pack_in, conv_pack], False, "gather_in_start", g1)

    inv_freq = ROPE_THETA ** (-jnp.arange(HD // 2, dtype=F32) * 2.0 / HD)
    pos = positions[0] + tok_in0[0, 0].astype(jnp.int32)
    ang = pos.astype(F32)[:, None] * inv_freq
    cos, sin = jnp.cos(ang), jnp.sin(ang)
    cos128 = jnp.concatenate([cos, cos, cos, cos], axis=1)
    sin128 = jnp.concatenate([-sin, sin, -sin, sin], axis=1)

    bias_w, alog_w, dsk_w = _pad_row(dt_bias, DT_PAD), _pad_row(a_log, DT_PAD), _pad_row(d_skip, DT_PAD)
    rms_pre = lambda a, r, g: a * r * g

    (u1,) = _rowwise(lambda a, g: ((a * _rstd(a) * g,), ()), [x2], [g1], [(D, BF16)], [], tm=512, name="norm_x",
                     deps=[cos128, sin128])
    p2b = p2.astype(BF16)

    in_back, in_land = _send_wait_many(in_handles, u1, "gather_in_wait")
    gin = lax.dynamic_update_slice(in_land[0], in_back[0][None], (me, 0, 0))
    gconv = lax.dynamic_update_slice(in_land[1], in_back[1][None], (me, 0, 0))
    rest_handles, tok_rest = _send_start_many(rest, False, "gather_rest_start", gconv)
    w_inT = gin[:, :W_IN_SHARD].reshape(IN_W, D)
    w_qkvzT = w_inT[:4 * AW]
    w_xbcdtT = jnp.pad(w_inT[4 * AW:], ((0, DT_PAD - HEADS), (0, 0)))
    conv_full = gconv[:, :CONV_K, :96].transpose(1, 0, 2).reshape(CONV_K, CONV_CH)
    qkvz = _mm(u1, w_qkvzT, tb=True, tm=512, tn=1024, tk=1024, name="proj_qkvz", deps=[tok_rest])
    xbcdt = _mm(u1, w_xbcdtT, tb=True, tm=512, tn=896, tk=1024, name="proj_xbcdt")

    qkv = _rope_fwd(qkvz, cos128, sin128)
    qkv = [qkv[3 * i:3 * i + 3] for i in range(len(DILATIONS))]
    outs, lses = [], []
    for d, (qd, kd, vd) in zip(DILATIONS, qkv):
        o, l = _attn_fwd(qd, kd, vd, d)
        outs.append(o)
        lses.append(l)
    attn, lse, attn4, lse4, attn16, lse16 = _attn_merge(outs, lses)

    act = _conv_fwd(xbcdt, conv_full, conv_b)
    y_ssd, states = _ssd_fwd(act, xbcdt, bias_w, alog_w, dsk_w)

    def gated_fwd(y, z, a, gs):
        gi = y * (z * _sigmoid(z))
        return (jnp.concatenate([a, gi * _rstd(gi) * gs], axis=1),), ()
    (cat,) = _rowwise(gated_fwd, [y_ssd, (qkvz, AW, 3), attn], [ssd_norm_g], [(D, BF16)], [], tm=512, name="gated_norm")

    rest_back, landed = _send_wait_many(rest_handles, cat, "gather_rest_wait")
    landed = [lax.dynamic_update_slice(l, b[None], (me, 0, 0)) for l, b in zip(landed, rest_back)]
    w_o, w_upT, w_dn, w_gate = landed[0].reshape(D, D), landed[1].reshape(DFF, D), landed[2].reshape(DFF, D), landed[3].reshape(D, D)
    w_projT = landed[4].reshape(D, PLE)

    mix = _mm(cat, w_o, tm=512, tn=1024, tk=1024, name="mix_out")

    def post1(xx, mm, ga, gb):
        h = xx + mm * _rstd(mm) * ga
        return (h, _rstd(h)), ()
    h1, r3 = _rowwise(post1, [x2, mix], [g2, g3], [(D, F32), (1, F32)], [], tm=512, name="post_mix")

    a_up, ff, u2, h2, h2b = _mlp_fwd(h1, r3, g3, w_upT, w_dn, g4)
    relu2 = lambda a: jnp.square(jnp.maximum(a.astype(F32), 0.0))

    gp = _mm(h2b, w_gate, tm=512, tn=1024, tk=1024, name="ple_gate")
    pp = _mm(p2b, w_projT, tb=True, tm=512, tn=1024, tk=256, name="ple_proj")

    def final(hh, gpre, ppv, tg, g):
        sg = _sigmoid(gpre)
        ple = ppv * sg
        r = _rstd(ple)
        n = ple * r
        h3 = hh + n * g
        e = h3 - tg
        dh3 = e * (1.0 / D)
        dple = _rms_bwd(n, r, g, dh3)
        return (dh3, dple * sg, dple * ppv * sg * (1.0 - sg)), (_colsum(dh3 * n), _colsum(0.5 * e * e * (1.0 / D)))
    dh3, dpp, dgp, dg5, loss_vec = _rowwise(final, [h2, gp, pp, tgt], [g5], [(D, F32), (D, BF16), (D, BF16)],
                                            [(1, D), (1, D)], tm=512, name="loss_ple_bwd")

    gw_projT = _mm(dpp, p2b, ta=True, tm=512, tn=256, tk=T, out_dtypes=(BF16,), name="gw_ple_proj")
    gw_gate = _mm(h2b, dgp, ta=True, tm=512, tn=1024, tk=T, out_dtypes=(BF16,), name="gw_ple_gate")
    rs_proj, tok_proj = _send_start(gw_projT.reshape(N_DEV, 32, D), True, "rs_start_w_proj", g1)
    rs_gate, tok_gate = _send_start(gw_gate.reshape(N_DEV, 128, D), True, "rs_start_w_gate", g1)
    dh2_g = _mm(dgp, w_gate, tb=True, tm=512, tn=1024, tk=1024, name="dx_ple_gate", deps=[tok_proj, tok_gate])

    def bwd_mlp_post(d3, dg_, f, g):
        dh2 = d3 + dg_
        r = _rstd(f)
        n = f * r
        return (dh2, _rms_bwd(n, r, g, dh2)), (_colsum(dh2 * n),)
    dh2, dff, dg4 = _rowwise(bwd_mlp_post, [dh3, dh2_g, ff], [g4], [(D, F32), (D, BF16)], [(1, D)], tm=512,
                             name="bwd_post_mlp")

    gw_dn = _mm(a_up, dff, ta=True, tm=512, tn=1024, tk=T, a_pre=relu2, out_dtypes=(BF16,), name="gw_mlp_down")
    rs_dn, tok_dn = _send_start(gw_dn.reshape(N_DEV, 512, D), True, "rs_start_w_down", g1)
    da_up, du2 = _mlp_dx(dff, a_up, w_upT, w_dn, tok_dn)
    gw_upT = _mm(da_up, u2, ta=True, tm=512, tn=1024, tk=T, out_dtypes=(BF16,), name="gw_mlp_up")
    rs_up, tok_up = _send_start(gw_upT.reshape(N_DEV, 512, D), True, "rs_start_w_up", g1)

    def bwd_mix_post(d2, du, hh, rr, mm, ga, gb):
        n3 = hh * rr
        dh1 = d2 + _rms_bwd(n3, rr, gb, du)
        r = _rstd(mm)
        n2 = mm * r
        return (dh1, _rms_bwd(n2, r, ga, dh1)), (_colsum(du * n3), _colsum(dh1 * n2))
    dh1, dmix, dg3, dg2 = _rowwise(bwd_mix_post, [dh2, du2, h1, r3, mix], [g2, g3], [(D, F32), (D, BF16)],
                                   [(1, D), (1, D)], tm=512, name="bwd_post_mix", deps=[tok_up])

    gw_o = _mm(cat, dmix, ta=True, tm=512, tn=1024, tk=T, out_dtypes=(BF16,), name="gw_out")
    rs_o, tok_o = _send_start(gw_o.reshape(N_DEV, 128, D), True, "rs_start_w_out", g1)
    dcat = _mm(dmix, w_o, tb=True, tm=512, tn=1024, tk=1024, name="dx_out", deps=[tok_o])

    def gated_bwd(y, z, dyn, gs):
        sg = _sigmoid(z)
        sz = z * sg
        gi = y * sz
        r = _rstd(gi)
        n = gi * r
        dgi = _rms_bwd(n, r, gs, dyn)
        return (dgi * sz, dgi * y * (sg * (1.0 + z * (1.0 - sg)))), (_colsum(dyn * n),)
    dy_ssd, dz, dgs = _rowwise(gated_bwd, [y_ssd, (qkvz, AW, 3), (dcat, AW, 1)], [ssd_norm_g], [(AW, F32)] * 2, [(1, AW)],
                               tm=512, name="bwd_gated_norm")

    dact, ddtw, ssd_par = _ssd_bwd(act, xbcdt, bias_w, alog_w, dsk_w, states, dy_ssd)
    dxbcdt, conv_par = _conv_bwd(xbcdt, dact, ddtw, conv_full, conv_b)

    dattn4, dattn16 = _dilate_cols(dcat, 0)
    qkv_grads = [_attn_bwd(*qkv[0], dcat, attn, lse, 1),
                 _attn_bwd(*qkv[1], dattn4, attn4, lse4, 4),
                 _attn_bwd(*qkv[2], dattn16, attn16, lse16, 16)]
    dqkvz = _rope_bwd(qkv_grads, dz, cos128, sin128)

    gw_qkvzT = _mm(dqkvz, u1, ta=True, tm=512, tn=1024, tk=T, out_dtypes=(BF16,), name="gw_qkvz")
    gw_xbcdtT = _mm(dxbcdt, u1, ta=True, tm=896, tn=1024, tk=T, out_dtypes=(BF16,), name="gw_xbcdt")
    gw_inT = jnp.concatenate([gw_qkvzT, gw_xbcdtT], axis=0)[:IN_W]
    gw_inT = jnp.pad(gw_inT.reshape(N_DEV, W_IN_SHARD, D), ((0, 0), (0, W_IN_SHARD_PAD - W_IN_SHARD), (0, 0)))
    rs_in, tok_in = _send_start(gw_inT, True, "rs_start_w_in", g1)

    du1a = _mm(dqkvz, w_qkvzT, tm=512, tn=1024, tk=2048, name="dx_qkvz", deps=[tok_in])
    du1b = _mm(dxbcdt, w_xbcdtT, tm=512, tn=1024, tk=896, name="dx_xbcdt")

    def bwd_in(d1, ua, ub, xx, g):
        rr = _rstd(xx)
        n = xx * rr
        du = ua + ub
        return (d1 + _rms_bwd(n, rr, g, du),), (_colsum(du * n),)
    grad_x, dg1 = _rowwise(bwd_in, [dh1, du1a, du1b, x2], [g1], [(D, F32)], [(1, D)], tm=512, name="bwd_pre_mix")

    slab = _all_reduce_small([(dg1, 0), (dg2, 1), (dg3, 2), (dg4, 3), (dg5, 4), (dgs, 5), (loss_vec, 6),
                              (conv_par, 8), (ssd_par, 16)], "reduce_small")
    g_conv_w = lax.dynamic_slice(slab[8:12, :CONV_CH], (0, me * 96), (CONV_K, 96))

    def scatter_finish(handles, nm, after):
        part, land = _send_wait(handles, after, "rs_wait_" + nm)
        own = lax.dynamic_slice(part, (me, 0, 0), (1,) + part.shape[1:])
        return _sum_slots(lax.dynamic_update_slice(land, own, (me, 0, 0)), "rs_sum_" + nm)
    g_out = scatter_finish(rs_o, "w_out", slab)
    g_upT = scatter_finish(rs_up, "w_up", slab)
    g_dn = scatter_finish(rs_dn, "w_down", slab)
    g_gate = scatter_finish(rs_gate, "w_gate", slab)
    g_projT = scatter_finish(rs_proj, "w_proj", slab)

    small_names = SMALL + ["conv_w"]
    small_rows = [0, 1, 2, 3, 4, 12, 5, 16, 17, 18, None]
    pick = lambda prefix: [args[prefix + nme] for nme in SMALL] + [args[prefix + "conv_w"][0]]
    loss11, g_s, d_s, m_s, v_s = _adamw_small(slab, small_rows, g_conv_w, pick(""), pick("m_"), pick("v_"))
    loss = loss11[0, 0]
    grads = {
        "w_out": g_out[None], "w_up": g_upT.T[None], "w_down": g_dn[None],
        "w_ple_gate": g_gate[None], "w_ple_proj": g_projT.reshape(128, PLE).T[None],
    }
    delta, new_m, new_v = {}, {}, {}
    for i, nme in enumerate(small_names):
        lead = (lambda t: t[None]) if nme == "conv_w" else (lambda t: t)
        grads[nme], delta[nme], new_m[nme], new_v[nme] = lead(g_s[i]), lead(d_s[i]), lead(m_s[i]), lead(v_s[i])
    for nme in ["w_out", "w_up", "w_down", "w_ple_gate", "w_ple_proj", "w_in"]:
        if nme == "w_in":
            g_inT = scatter_finish(rs_in, "w_in", delta["w_down"])
            grads["w_in"] = g_inT[:W_IN_SHARD].T[None]
        dl, mm_, vv_ = _adamw(args[nme][0], grads[nme][0], args["m_" + nme][0], args["v_" + nme][0], "adamw_" + nme)
        delta[nme], new_m[nme], new_v[nme] = dl[None], mm_[None], vv_[None]

    order = ["norm_mix_pre", "norm_mix_post", "w_in", "conv_w", "conv_b", "dt_bias", "a_log", "d_skip", "ssd_norm_g",
             "w_out", "norm_mlp_pre", "norm_mlp_post", "w_up", "w_down", "w_ple_gate", "w_ple_proj", "norm_ple_post"]
    return (loss, grad_x[None], *[grads[n] for n in order], *[delta[n] for n in order],
            *[new_m[n] for n in order], *[new_v[n] for n in order])
```

```python
import functools
import math

import jax
import jax.numpy as jnp
from jax import lax
from jax.experimental import pallas as pl
from jax.experimental.pallas import tpu as pltpu

F32 = jnp.float32
BF16 = jnp.bfloat16
MESH = pl.DeviceIdType.MESH
HIGHEST = lax.Precision.HIGHEST

N_DEV = 8
T = 4096
D = 1024
HEADS = 8
HD = 64
AW = 512
NS = 128
CONV_K = 4
CONV_CH = 768
CHUNK = 128
DFF = 4096
PLE = 256
EPS = 1e-6
ROPE_THETA = 10000.0
DILATIONS = (1, 4, 16)
QBLK = 128
NEG = -1e30
IN_W = 2824
W_IN_SHARD = 353
W_IN_SHARD_PAD = 384
DT_PAD = 128

ADAM_LR, ADAM_B1, ADAM_B2, ADAM_EPS, ADAM_WD, ADAM_STEP = 0.001, 0.9, 0.999, 1e-08, 0.01, 10

VMEM_LIMIT = 56 * 1024 * 1024


_ANY = pl.BlockSpec(memory_space=pl.ANY)


def _cparams(sem=None):
    return pltpu.CompilerParams(dimension_semantics=sem, vmem_limit_bytes=VMEM_LIMIT)


def _dot(a, b, ca, cb, precision=None):
    return lax.dot_general(a, b, (((ca,), (cb,)), ((), ())), preferred_element_type=F32, precision=precision)


def _nn(a, b):
    return _dot(a, b, 1, 0)


def _nt(a, b):
    return _dot(a, b, 1, 1)


def _tn(a, b):
    return _dot(a, b, 0, 0)


def _sigmoid(x):
    return 1.0 / (1.0 + jnp.exp(-x))


def _softplus(x):
    return jnp.maximum(x, 0.0) + jnp.log(1.0 + jnp.exp(-jnp.abs(x)))


def _mm(a, b, *, ta=False, tb=False, tm, tn, tk, name,
        a_pre=None, a_rows=(), a_cols=(), b_pre=None, b_rows=(), b_cols=(),
        epi=None, epi_tiles=(), out_dtypes=(F32,), deps=()):
    if ta:
        K, M = a.shape
    else:
        M, K = a.shape
    if tb:
        N, K2 = b.shape
    else:
        K2, N = b.shape
    assert K == K2 and M % tm == 0 and N % tn == 0 and K % tk == 0, (name, a.shape, b.shape)
    nk = K // tk
    if ta:
        a_spec = pl.BlockSpec((tk, tm), lambda i, j, k: (k, i))
        a_row_specs = [pl.BlockSpec((tk, 1), lambda i, j, k: (k, 0)) for _ in a_rows]
        a_col_specs = [pl.BlockSpec((1, tm), lambda i, j, k: (0, i)) for _ in a_cols]
    else:
        a_spec = pl.BlockSpec((tm, tk), lambda i, j, k: (i, k))
        a_row_specs = [pl.BlockSpec((tm, 1), lambda i, j, k: (i, 0)) for _ in a_rows]
        a_col_specs = [pl.BlockSpec((1, tk), lambda i, j, k: (0, k)) for _ in a_cols]
    if tb:
        b_spec = pl.BlockSpec((tn, tk), lambda i, j, k: (j, k))
        b_row_specs = [pl.BlockSpec((tn, 1), lambda i, j, k: (j, 0)) for _ in b_rows]
        b_col_specs = [pl.BlockSpec((1, tk), lambda i, j, k: (0, k)) for _ in b_cols]
    else:
        b_spec = pl.BlockSpec((tk, tn), lambda i, j, k: (k, j))
        b_row_specs = [pl.BlockSpec((tk, 1), lambda i, j, k: (k, 0)) for _ in b_rows]
        b_col_specs = [pl.BlockSpec((1, tn), lambda i, j, k: (0, j)) for _ in b_cols]
    o_spec = pl.BlockSpec((tm, tn), lambda i, j, k: (i, j))
    na, nb, ne, no = len(a_rows) + len(a_cols), len(b_rows) + len(b_cols), len(epi_tiles), len(out_dtypes)

    def body(*refs):
        a_ref, b_ref = refs[0], refs[1]
        a_ex = refs[2:2 + na]
        b_ex = refs[2 + na:2 + na + nb]
        e_ex = refs[2 + na + nb:2 + na + nb + ne]
        first_out = 2 + na + nb + ne + len(deps)
        outs = refs[first_out:first_out + no]

        def finish(res):
            vals = epi(res, *[r[...] for r in e_ex]) if epi is not None else (res,)
            for o_ref, val in zip(outs, vals):
                o_ref[...] = val.astype(o_ref.dtype)

        at = a_ref[...]
        if a_pre is not None:
            at = a_pre(at, *[r[...] for r in a_ex])
        bt = b_ref[...]
        if b_pre is not None:
            bt = b_pre(bt, *[r[...] for r in b_ex])
        prod = _dot(at.astype(BF16), bt.astype(BF16), 0 if ta else 1, 1 if tb else 0)
        if nk == 1:
            finish(prod)
            return
        acc = refs[-1]
        k = pl.program_id(2)

        @pl.when(k == 0)
        def _():
            acc[...] = jnp.zeros_like(acc)
        acc[...] += prod

        @pl.when(k == nk - 1)
        def _():
            finish(acc[...])

    outs = pl.pallas_call(
        body, name=name,
        grid=(M // tm, N // tn, nk),
        in_specs=([a_spec, b_spec] + a_row_specs + a_col_specs + b_row_specs + b_col_specs + [o_spec] * ne
                  + [_ANY] * len(deps)),
        out_specs=[o_spec] * no,
        out_shape=[jax.ShapeDtypeStruct((M, N), dt) for dt in out_dtypes],
        scratch_shapes=[pltpu.VMEM((tm, tn), F32)] if nk > 1 else [],
        compiler_params=_cparams(("parallel", "parallel", "arbitrary")),
    )(a, b, *a_rows, *a_cols, *b_rows, *b_cols, *epi_tiles, *deps)
    return outs[0] if no == 1 else outs


MLP_TM = 1024
MLP_TC = 512


def _mlp_fwd(h, r, g, w_upT, w_dn, g_post):
    nc = DFF // MLP_TC

    def body(h_ref, r_ref, g_ref, wu_ref, wd_ref, gp_ref, a_ref, ff_ref, u_ref, ho_ref, hob_ref, acc, u_scr):
        c = pl.program_id(1)

        @pl.when(c == 0)
        def _():
            u = (h_ref[...] * r_ref[...] * g_ref[...]).astype(BF16)
            u_scr[...] = u
            u_ref[...] = u
            acc[...] = jnp.zeros_like(acc)
        a = _nt(u_scr[...], wu_ref[...])
        a_ref[...] = a.astype(BF16)
        acc[...] += _nn(jnp.square(jnp.maximum(a, 0.0)).astype(BF16), wd_ref[...])

        @pl.when(c == nc - 1)
        def _():
            f = acc[...]
            ff_ref[...] = f
            ho = h_ref[...] + f * _rstd(f) * gp_ref[...]
            ho_ref[...] = ho
            hob_ref[...] = ho.astype(BF16)

    row = pl.BlockSpec((MLP_TM, D), lambda i, c: (i, 0))
    wsp = pl.BlockSpec((MLP_TC, D), lambda i, c: (c, 0))
    vec = pl.BlockSpec((1, D), lambda i, c: (0, 0))
    return pl.pallas_call(
        body, name="mlp_fwd", grid=(T // MLP_TM, nc),
        in_specs=[row, pl.BlockSpec((MLP_TM, 1), lambda i, c: (i, 0)), vec, wsp, wsp, vec],
        out_specs=[pl.BlockSpec((MLP_TM, MLP_TC), lambda i, c: (i, c)), row, row, row, row],
        out_shape=[jax.ShapeDtypeStruct((T, DFF), BF16), jax.ShapeDtypeStruct((T, D), F32), jax.ShapeDtypeStruct((T, D), BF16),
                   jax.ShapeDtypeStruct((T, D), F32), jax.ShapeDtypeStruct((T, D), BF16)],
        scratch_shapes=[pltpu.VMEM((MLP_TM, D), F32), pltpu.VMEM((MLP_TM, D), BF16)],
        compiler_params=_cparams(("parallel", "arbitrary")),
    )(h, r, g, w_upT, w_dn, g_post)


def _mlp_dx(dff, a, w_upT, w_dn, dep):
    nc = DFF // MLP_TC

    def body(d_ref, a_ref, wu_ref, wd_ref, dep_ref, da_ref, du_ref, acc, d_scr):
        c = pl.program_id(1)

        @pl.when(c == 0)
        def _():
            d_scr[...] = d_ref[...].astype(BF16)
            acc[...] = jnp.zeros_like(acc)
        da = (_nt(d_scr[...], wd_ref[...]) * (2.0 * jnp.maximum(a_ref[...].astype(F32), 0.0))).astype(BF16)
        da_ref[...] = da
        acc[...] += _nn(da, wu_ref[...])

        @pl.when(c == nc - 1)
        def _():
            du_ref[...] = acc[...]

    row = pl.BlockSpec((MLP_TM, D), lambda i, c: (i, 0))
    wsp = pl.BlockSpec((MLP_TC, D), lambda i, c: (c, 0))
    chunk = pl.BlockSpec((MLP_TM, MLP_TC), lambda i, c: (i, c))
    return pl.pallas_call(
        body, name="mlp_dx", grid=(T // MLP_TM, nc),
        in_specs=[row, chunk, wsp, wsp, _ANY], out_specs=[chunk, row],
        out_shape=[jax.ShapeDtypeStruct((T, DFF), BF16), jax.ShapeDtypeStruct((T, D), F32)],
        scratch_shapes=[pltpu.VMEM((MLP_TM, D), F32), pltpu.VMEM((MLP_TM, D), BF16)],
        compiler_params=_cparams(("parallel", "arbitrary")),
    )(dff, a, w_upT, w_dn, dep)


def _rowwise(fn, rows, vecs, out_rows, out_sums, *, tm, name, deps=()):
    specs, arrs = [], []
    R = None
    for r in rows:
        if isinstance(r, tuple):
            arr, width, cb = r
            specs.append(pl.BlockSpec((tm, width), lambda i, cb=cb: (i, cb)))
        else:
            arr = r
            specs.append(pl.BlockSpec((tm, arr.shape[1]), lambda i: (i, 0)))
        R = arr.shape[0] if R is None else R
        assert arr.shape[0] == R, name
        arrs.append(arr)
    assert R % tm == 0, name
    for v in vecs:
        specs.append(pl.BlockSpec(v.shape, lambda i: (0, 0)))
        arrs.append(v)
    nr, nv, no, ns = len(rows), len(vecs), len(out_rows), len(out_sums)
    out_specs = [pl.BlockSpec((tm, w), lambda i: (i, 0)) for w, _ in out_rows]
    out_specs += [pl.BlockSpec(s, lambda i: (0, 0)) for s in out_sums]
    out_shape = [jax.ShapeDtypeStruct((R, w), dt) for w, dt in out_rows]
    out_shape += [jax.ShapeDtypeStruct(s, F32) for s in out_sums]

    nd = len(deps)

    def body(*refs):
        ins = [r[...] for r in refs[:nr + nv]]
        o_refs = refs[nr + nv + nd:nr + nv + nd + no]
        s_refs = refs[nr + nv + nd + no:]
        o_vals, s_vals = fn(*ins)
        for ref, val in zip(o_refs, o_vals):
            ref[...] = val.astype(ref.dtype)
        if ns:
            @pl.when(pl.program_id(0) == 0)
            def _():
                for ref in s_refs:
                    ref[...] = jnp.zeros_like(ref)
            for ref, val in zip(s_refs, s_vals):
                ref[...] += val

    outs = pl.pallas_call(
        body, name=name, grid=(R // tm,), in_specs=specs + [_ANY] * nd, out_specs=out_specs, out_shape=out_shape,
        compiler_params=_cparams(("arbitrary",) if ns else ("parallel",)),
    )(*arrs, *deps)
    return outs


def _colsum(x):
    return jnp.sum(x, axis=0, keepdims=True)


def _rstd(x):
    return lax.rsqrt(jnp.mean(x * x, axis=-1, keepdims=True) + EPS)


def _rms_bwd(xn, r, g, dy):
    dn = dy * g
    return r * (dn - xn * jnp.mean(dn * xn, axis=-1, keepdims=True))


def _partner(t):
    lane = lax.broadcasted_iota(jnp.int32, t.shape, 1)
    up = pltpu.roll(t, 96, 1)
    down = pltpu.roll(t, 32, 1)
    return jnp.where((lane % 64) < 32, up, down)


SLABS = AW // 128


def _rows(r, n, d):
    return pl.ds(r, n, stride=d) if d > 1 else pl.ds(0, n)


def _undilate(src_ref, dst, d, tm):
    for r in range(d):
        for j in range(SLABS):
            dst[j][_rows(r, tm // d, d), :] = src_ref[:, pl.ds(r * AW + j * 128, 128)].astype(dst[j].dtype)


def _dilate(dst_ref, src, d, tm):
    for r in range(d):
        for j in range(SLABS):
            dst_ref[:, pl.ds(r * AW + j * 128, 128)] = src[j][_rows(r, tm // d, d), :].astype(dst_ref.dtype)


def _slab_scratch(n, tm):
    return [pltpu.VMEM((tm, 128), F32)] * (SLABS * n)


def _slab_groups(flat):
    return [flat[SLABS * i:SLABS * (i + 1)] for i in range(len(flat) // SLABS)]


def _slab_specs(tm, first):
    return [pl.BlockSpec((tm, 128), lambda i, j=j: (i, first + j)) for j in range(SLABS)]


def _dil_spec(tm, d):
    return pl.BlockSpec((tm // d, d * AW), lambda i: (i, 0))


ROPE_TM = 512


def _rope_fwd(qkvz, cos128, sin128):
    tm = ROPE_TM

    def body(*refs):
        q_refs, k_refs, v_refs = refs[0:4], refs[4:8], refs[8:12]
        c_ref, s_ref = refs[12], refs[13]
        outs = refs[14:23]
        qs, ks = _slab_groups(refs[23:])
        c, s = c_ref[...], s_ref[...]
        for j in range(SLABS):
            q, k = q_refs[j][...], k_refs[j][...]
            qs[j][...] = (q * c + _partner(q) * s) * (HD ** -0.5)
            ks[j][...] = k * c + _partner(k) * s
        for di, d in enumerate(DILATIONS):
            oq, ok, ov = outs[3 * di:3 * di + 3]
            for r in range(d):
                rows = _rows(r, tm // d, d)
                for j in range(SLABS):
                    cols = pl.ds(r * AW + j * 128, 128)
                    oq[:, cols] = qs[j][rows, :].astype(BF16)
                    ok[:, cols] = ks[j][rows, :].astype(BF16)
                    ov[:, cols] = v_refs[j][rows, :].astype(BF16)

    tab = pl.BlockSpec((tm, 128), lambda i: (i, 0))
    out_specs, out_shape = [], []
    for d in DILATIONS:
        out_specs += [_dil_spec(tm, d)] * 3
        out_shape += [jax.ShapeDtypeStruct((T // d, d * AW), BF16)] * 3
    return pl.pallas_call(
        body, name="rope_fwd", grid=(T // tm,),
        in_specs=_slab_specs(tm, 0) + _slab_specs(tm, 4) + _slab_specs(tm, 8) + [tab, tab],
        out_specs=out_specs, out_shape=out_shape, scratch_shapes=_slab_scratch(2, tm),
        compiler_params=_cparams(("parallel",)),
    )(*([qkvz] * 12), cos128, sin128)


def _rope_bwd(grads, dz, cos128, sin128):
    tm = 256

    def body(*refs):
        g_refs = refs[0:9]
        dz_ref, c_ref, s_ref, o_ref = refs[9], refs[10], refs[11], refs[12]
        scr = _slab_groups(refs[13:])
        for di, d in enumerate(DILATIONS[1:]):
            for t in range(3):
                _undilate(g_refs[3 * (di + 1) + t], scr[3 * di + t], d, tm)
        c, s = c_ref[...], s_ref[...]
        for j in range(SLABS):
            cols = pl.ds(j * 128, 128)
            tot = [g_refs[t][:, cols] + scr[t][j][...] + scr[3 + t][j][...] for t in range(3)]
            dqr = tot[0] * (HD ** -0.5)
            o_ref[:, pl.ds(j * 128, 128)] = (dqr * c + _partner(dqr * s)).astype(BF16)
            o_ref[:, pl.ds(AW + j * 128, 128)] = (tot[1] * c + _partner(tot[1] * s)).astype(BF16)
            o_ref[:, pl.ds(2 * AW + j * 128, 128)] = tot[2].astype(BF16)
        o_ref[:, pl.ds(3 * AW, AW)] = dz_ref[...].astype(BF16)

    tab = pl.BlockSpec((tm, 128), lambda i: (i, 0))
    in_specs, args = [], []
    for d, g in zip(DILATIONS, grads):
        in_specs += [_dil_spec(tm, d)] * 3
        args += list(g)
    return pl.pallas_call(
        body, name="rope_bwd", grid=(T // tm,),
        in_specs=in_specs + [pl.BlockSpec((tm, AW), lambda i: (i, 0)), tab, tab],
        out_specs=pl.BlockSpec((tm, 4 * AW), lambda i: (i, 0)),
        out_shape=jax.ShapeDtypeStruct((T, 4 * AW), BF16),
        scratch_shapes=_slab_scratch(6, tm),
        compiler_params=_cparams(("parallel",)),
    )(*args, dz, cos128, sin128)


def _dilate_cols(x, first):
    tm = ROPE_TM

    def body(x0, x1, x2, x3, o4, o16):
        xs = (x0, x1, x2, x3)
        for o_ref, d in ((o4, 4), (o16, 16)):
            for r in range(d):
                for j in range(SLABS):
                    o_ref[:, pl.ds(r * AW + j * 128, 128)] = xs[j][_rows(r, tm // d, d), :]

    return pl.pallas_call(
        body, name="dilate_cols", grid=(T // tm,),
        in_specs=_slab_specs(tm, first), out_specs=[_dil_spec(tm, 4), _dil_spec(tm, 16)],
        out_shape=[jax.ShapeDtypeStruct((T // 4, 4 * AW), F32), jax.ShapeDtypeStruct((T // 16, 16 * AW), F32)],
        compiler_params=_cparams(("parallel",)),
    )(x, x, x, x)


def _band_masks():
    qi = lax.broadcasted_iota(jnp.int32, (QBLK, QBLK), 0)
    kj = lax.broadcasted_iota(jnp.int32, (QBLK, QBLK), 1)
    return kj >= qi, kj <= qi


def _attn_fwd(q, k, v, d):
    L = q.shape[0]
    npair = L // (2 * QBLK)

    def body(q_ref, kp_ref, kc_ref, vp_ref, vc_ref, o_ref, l_ref):
        pair = pl.program_id(1)
        mask_p, mask_c = _band_masks()
        for sub in range(2):
            rows = pl.ds(sub * QBLK, QBLK)
            first = jnp.where(pair > 0, 0.0, NEG) if sub == 0 else 0.0
            bias = jnp.concatenate([jnp.where(mask_p, 0.0, NEG) + first, jnp.where(mask_c, 0.0, NEG)], axis=1)
            k_prev = (lambda sl: kp_ref[:, sl]) if sub == 0 else (lambda sl: kc_ref[pl.ds(0, QBLK), sl])
            v_prev = (lambda sl: vp_ref[:, sl]) if sub == 0 else (lambda sl: vc_ref[pl.ds(0, QBLK), sl])
            s = []
            for h in range(HEADS):
                sl = pl.ds(HD * h, HD)
                qh = q_ref[rows, sl]
                s.append(jnp.concatenate([_nt(qh, k_prev(sl)), _nt(qh, kc_ref[rows, sl])], axis=1))
            s = jnp.stack(s) + bias
            m = jnp.max(s, axis=2, keepdims=True)
            e = jnp.exp(s - m)
            den = jnp.sum(e, axis=2, keepdims=True)
            p = e.astype(BF16)
            inv = 1.0 / den
            lse = m + jnp.log(den)
            for h in range(HEADS):
                sl = pl.ds(HD * h, HD)
                o_ref[rows, sl] = (_nn(p[h, :, :QBLK], v_prev(sl)) + _nn(p[h, :, QBLK:], vc_ref[rows, sl])) * inv[h]
                l_ref[rows, sl] = jnp.broadcast_to(lse[h], (QBLK, HD))

    cur = pl.BlockSpec((2 * QBLK, AW), lambda r, n: (n, r))
    prev = pl.BlockSpec((QBLK, AW), lambda r, n: (jnp.maximum(2 * n - 1, 0), r))
    return pl.pallas_call(
        body, name=f"attn_fwd_d{d}", grid=(d, npair),
        in_specs=[cur, prev, cur, prev, cur], out_specs=[cur, cur],
        out_shape=[jax.ShapeDtypeStruct((L, d * AW), F32)] * 2,
        compiler_params=_cparams(("parallel", "parallel")),
    )(q, k, k, v, v)


def _attn_bwd(q, k, v, do, at, lse, d):
    L = q.shape[0]
    nb = L // QBLK

    def body(q0_ref, q1_ref, kp_ref, kc_ref, vp_ref, vc_ref, do0_ref, do1_ref, at0_ref, at1_ref,
             l0_ref, l1_ref, dq_ref, dk_ref, dv_ref):
        n = pl.program_id(1)
        mask_p, mask_c = _band_masks()
        prev_bias = jnp.where(mask_p, 0.0, NEG)
        bias = jnp.concatenate([prev_bias + jnp.where(n > 0, 0.0, NEG), jnp.where(mask_c, 0.0, NEG),
                                prev_bias + jnp.where(n < nb - 1, 0.0, NEG)], axis=1)
        s, dp, ls, dl, ops = [], [], [], [], []
        for h in range(HEADS):
            sl = pl.ds(HD * h, HD)
            one = pl.ds(HD * h, 1)
            q0, q1 = q0_ref[:, sl], q1_ref[:, sl]
            kp, kc, vp, vc = kp_ref[:, sl], kc_ref[:, sl], vp_ref[:, sl], vc_ref[:, sl]
            do0, do1 = do0_ref[:, sl], do1_ref[:, sl]
            do0b, do1b = do0.astype(BF16), do1.astype(BF16)
            s.append(jnp.concatenate([_nt(q0, kp), _nt(q0, kc), _nt(q1, kc)], axis=1))
            dp.append(jnp.concatenate([_nt(do0b, vp), _nt(do0b, vc), _nt(do1b, vc)], axis=1))
            dl0 = jnp.sum(do0 * at0_ref[:, sl], axis=1, keepdims=True)
            dl1 = jnp.sum(do1 * at1_ref[:, sl], axis=1, keepdims=True)
            dl.append(jnp.concatenate([jnp.broadcast_to(dl0, (QBLK, 2 * QBLK)), jnp.broadcast_to(dl1, (QBLK, QBLK))], axis=1))
            ls.append(jnp.concatenate([jnp.broadcast_to(l0_ref[:, one], (QBLK, 2 * QBLK)),
                                       jnp.broadcast_to(l1_ref[:, one], (QBLK, QBLK))], axis=1))
            ops.append((q0, q1, kp, kc, do0b, do1b))
        p = jnp.exp(jnp.stack(s) + bias - jnp.stack(ls))
        ds = (p * (jnp.stack(dp) - jnp.stack(dl))).astype(BF16)
        p = p.astype(BF16)
        for h in range(HEADS):
            sl = pl.ds(HD * h, HD)
            q0, q1, kp, kc, do0b, do1b = ops[h]
            dq_ref[:, sl] = (_nn(ds[h, :, :QBLK], kp) + _nn(ds[h, :, QBLK:2 * QBLK], kc)).astype(BF16)
            dv_ref[:, sl] = (_tn(p[h, :, QBLK:2 * QBLK], do0b) + _tn(p[h, :, 2 * QBLK:], do1b)).astype(BF16)
            dk_ref[:, sl] = (_tn(ds[h, :, QBLK:2 * QBLK], q0) + _tn(ds[h, :, 2 * QBLK:], q1)).astype(BF16)

    cur = pl.BlockSpec((QBLK, AW), lambda r, n: (n, r))
    prev = pl.BlockSpec((QBLK, AW), lambda r, n: (jnp.maximum(n - 1, 0), r))
    nxt = pl.BlockSpec((QBLK, AW), lambda r, n: (jnp.minimum(n + 1, nb - 1), r))
    return pl.pallas_call(
        body, name=f"attn_bwd_d{d}", grid=(d, nb),
        in_specs=[cur, nxt, prev, cur, prev, cur, cur, nxt, cur, nxt, cur, nxt], out_specs=[cur, cur, cur],
        out_shape=[jax.ShapeDtypeStruct((L, d * AW), BF16)] * 3,
        compiler_params=_cparams(("parallel", "parallel")),
    )(q, q, k, k, v, v, do, do, at, at, lse, lse)


def _attn_merge(outs, lses):
    tm = ROPE_TM

    def body(o1, o4, o16, l1, l4, l16, at_ref, ls_ref, at4, ls4, at16, ls16, *flat):
        so4, so16, sl4, sl16, sa, sl = _slab_groups(flat)
        _undilate(o4, so4, 4, tm)
        _undilate(o16, so16, 16, tm)
        _undilate(l4, sl4, 4, tm)
        _undilate(l16, sl16, 16, tm)
        for j in range(SLABS):
            cols = pl.ds(j * 128, 128)
            a, b, c = l1[:, cols], sl4[j][...], sl16[j][...]
            m = jnp.maximum(jnp.maximum(a, b), c)
            e1, e2, e3 = jnp.exp(a - m), jnp.exp(b - m), jnp.exp(c - m)
            s = e1 + e2 + e3
            inv = 1.0 / s
            attn = (e1 * inv) * o1[:, cols] + (e2 * inv) * so4[j][...] + (e3 * inv) * so16[j][...]
            lse = m + jnp.log(s)
            at_ref[:, cols] = attn
            ls_ref[:, cols] = lse
            sa[j][...] = attn
            sl[j][...] = lse
        _dilate(at4, sa, 4, tm)
        _dilate(at16, sa, 16, tm)
        _dilate(ls4, sl, 4, tm)
        _dilate(ls16, sl, 16, tm)

    specs = [_dil_spec(tm, d) for d in DILATIONS]
    tok = specs[0]
    return pl.pallas_call(
        body, name="attn_merge", grid=(T // tm,),
        in_specs=specs + specs, out_specs=[tok, tok, specs[1], specs[1], specs[2], specs[2]],
        out_shape=[jax.ShapeDtypeStruct((T, AW), F32)] * 2 + [jax.ShapeDtypeStruct((T // 4, 4 * AW), F32)] * 2
        + [jax.ShapeDtypeStruct((T // 16, 16 * AW), F32)] * 2,
        scratch_shapes=_slab_scratch(6, tm),
        compiler_params=_cparams(("parallel",)),
    )(*outs, *lses)


CONV_TM = 512
HALO = 8


def _conv_pre(ext, w, b):
    y = b + w[3] * ext
    for kk in range(1, CONV_K):
        y = y + w[3 - kk] * pltpu.roll(ext, kk, 0)
    return y


def _rows_to_block(rows, n, width):
    ri = lax.broadcasted_iota(jnp.int32, (n, width), 0)
    out = jnp.zeros((n, width), F32)
    for j, r in enumerate(rows):
        out = out + jnp.where(ri == j, r, 0.0)
    return out


def _conv_fwd(xbc, w, b):
    nblk = T // CONV_TM

    def body(x_ref, h_ref, w_ref, b_ref, o_ref):
        i = pl.program_id(0)
        halo = jnp.where(i > 0, h_ref[...], 0.0)
        ext = jnp.concatenate([halo, x_ref[...]], axis=0)
        y = _conv_pre(ext, [w_ref[pl.ds(j, 1), :] for j in range(CONV_K)], b_ref[...])[HALO:]
        o_ref[...] = y * _sigmoid(y)

    return pl.pallas_call(
        body, name="conv_fwd", grid=(nblk,),
        in_specs=[pl.BlockSpec((CONV_TM, CONV_CH), lambda i: (i, 0)),
                  pl.BlockSpec((HALO, CONV_CH), lambda i: (jnp.maximum(i * (CONV_TM // HALO) - 1, 0), 0)),
                  pl.BlockSpec((CONV_K, CONV_CH), lambda i: (0, 0)),
                  pl.BlockSpec((1, CONV_CH), lambda i: (0, 0))],
        out_specs=pl.BlockSpec((CONV_TM, CONV_CH), lambda i: (i, 0)),
        out_shape=jax.ShapeDtypeStruct((T, CONV_CH), F32),
        compiler_params=_cparams(("parallel",)),
    )(xbc, xbc, w, b)


def _conv_bwd(xbc, dact, ddt, w, b):
    nblk = T // CONV_TM
    per = CONV_TM // HALO

    def body(x_ref, xb_ref, xa_ref, g_ref, ga_ref, ddt_ref, w_ref, b_ref, dx_ref, dw_ref):
        i = pl.program_id(0)
        wv = [w_ref[pl.ds(j, 1), :] for j in range(CONV_K)]
        before = jnp.where(i > 0, xb_ref[...], 0.0)
        last = i == nblk - 1
        after = jnp.where(last, 0.0, xa_ref[...])
        g_after = jnp.where(last, 0.0, ga_ref[...])
        ext = jnp.concatenate([before, x_ref[...], after], axis=0)
        y = _conv_pre(ext, wv, b_ref[...])[HALO:]
        sg = _sigmoid(y)
        dy = jnp.concatenate([g_ref[...], g_after], axis=0) * (sg * (1.0 + y * (1.0 - sg)))
        n = CONV_TM + HALO
        dx = wv[3] * dy
        for kk in range(1, CONV_K):
            dx = dx + wv[3 - kk] * pltpu.roll(dy, n - kk, 0)
        dx_ref[:, pl.ds(0, CONV_CH)] = dx[:CONV_TM].astype(BF16)
        dx_ref[:, pl.ds(CONV_CH, DT_PAD)] = ddt_ref[...].astype(BF16)
        dyc = dy[:CONV_TM]
        rows = [jnp.sum(dyc * (pltpu.roll(ext, 3 - j, 0) if j < 3 else ext)[HALO:HALO + CONV_TM], axis=0, keepdims=True)
                for j in range(CONV_K)]
        rows.append(jnp.sum(dyc, axis=0, keepdims=True))
        part = _rows_to_block(rows, 8, CONV_CH)

        @pl.when(i == 0)
        def _():
            dw_ref[...] = jnp.zeros_like(dw_ref)
        dw_ref[...] += part

    blk = pl.BlockSpec((CONV_TM, CONV_CH), lambda i: (i, 0))
    hb = pl.BlockSpec((HALO, CONV_CH), lambda i: (jnp.maximum(i * per - 1, 0), 0))
    ha = pl.BlockSpec((HALO, CONV_CH), lambda i: (jnp.minimum((i + 1) * per, T // HALO - 1), 0))
    return pl.pallas_call(
        body, name="conv_bwd", grid=(nblk,),
        in_specs=[blk, hb, ha, blk, ha, pl.BlockSpec((CONV_TM, DT_PAD), lambda i: (i, 0)),
                  pl.BlockSpec((CONV_K, CONV_CH), lambda i: (0, 0)), pl.BlockSpec((1, CONV_CH), lambda i: (0, 0))],
        out_specs=[pl.BlockSpec((CONV_TM, CONV_CH + DT_PAD), lambda i: (i, 0)), pl.BlockSpec((8, CONV_CH), lambda i: (0, 0))],
        out_shape=[jax.ShapeDtypeStruct((T, CONV_CH + DT_PAD), BF16), jax.ShapeDtypeStruct((8, CONV_CH), F32)],
        compiler_params=_cparams(("arbitrary",)),
    )(xbc, xbc, xbc, dact, dact, ddt, w, b)


def _pick(mat, h):
    lane = lax.broadcasted_iota(jnp.int32, mat.shape, 1)
    return jnp.sum(jnp.where(lane == h, mat, 0.0), axis=1, keepdims=True)


def _heads(fn):
    return jnp.stack([fn(h) for h in range(HEADS)])


def _ssd_prep(dt_ref, bias_ref, alog_ref, dsk_ref, b_ref, c_ref, xs_ref, state_ref, cst):
    li = lax.broadcasted_iota(jnp.int32, (CHUNK, CHUNK), 0)
    si = lax.broadcasted_iota(jnp.int32, (CHUNK, CHUNK), 1)
    tri = li >= si
    dtp = dt_ref[...] + bias_ref[...]
    dt = _softplus(dtp)
    A = -jnp.exp(alog_ref[...])
    a = dt * A
    cs = jnp.dot(tri.astype(F32), a, precision=HIGHEST, preferred_element_type=F32)
    cst[...] = cs.T
    Bm = b_ref[...].astype(BF16)
    Cm = c_ref[...].astype(BF16)
    cb = _nt(Cm, Bm)
    dskv = dsk_ref[...]
    cs_col = _heads(lambda h: _pick(cs, h))
    cs_row = _heads(lambda h: cst[pl.ds(h, 1), :])
    dt_col = _heads(lambda h: _pick(dt, h))
    dsk_col = _heads(lambda h: _pick(dskv, h))
    lam = jnp.exp(jnp.where(tri, cs_col - cs_row, NEG))
    x = _heads(lambda h: xs_ref[:, pl.ds(HD * h, HD)])
    xdt = x * dt_col
    prev = _heads(lambda h: state_ref[pl.ds(HD * h, HD), :])
    lane = lax.broadcasted_iota(jnp.int32, (1, 1, CHUNK), 2)
    cl = jnp.sum(jnp.where(lane == CHUNK - 1, cs_row, 0.0), axis=2, keepdims=True)
    f = jnp.exp(cl - cs_col)
    return dict(li=li, si=si, dtp=dtp, dt=dt, A=A, Bm=Bm, Cm=Cm, cb=cb, cs_col=cs_col, dt_col=dt_col, dsk_col=dsk_col,
                lam=lam, x=x, xdt=xdt, prev=prev, cl=cl, f=f)


def _ssd_fwd(act, xbcdt, bias, alog, dsk):
    nc = T // CHUNK

    def body(xs_ref, b_ref, c_ref, dt_ref, bias_ref, alog_ref, dsk_ref, y_ref, st_ref, state, cst):
        @pl.when(pl.program_id(0) == 0)
        def _():
            state[...] = jnp.zeros_like(state)
        st_ref[...] = state[...]
        s = _ssd_prep(dt_ref, bias_ref, alog_ref, dsk_ref, b_ref, c_ref, xs_ref, state, cst)
        Bm, Cm, prev = s["Bm"], s["Cm"], s["prev"]
        g = (s["cb"] * s["lam"]).astype(BF16)
        xdtb = s["xdt"].astype(BF16)
        prevb = prev.astype(BF16)
        y = _heads(lambda h: _nn(g[h], xdtb[h])) + _heads(lambda h: _nt(Cm, prevb[h])) * jnp.exp(s["cs_col"])
        y = y + s["dsk_col"] * s["x"]
        xf = (s["xdt"] * s["f"]).astype(BF16)
        new = prev * jnp.exp(s["cl"]) + _heads(lambda h: _tn(xf[h], Bm))
        for h in range(HEADS):
            y_ref[:, pl.ds(HD * h, HD)] = y[h]
            state[pl.ds(HD * h, HD), :] = new[h]

    vec = pl.BlockSpec((1, DT_PAD), lambda c: (0, 0))
    return pl.pallas_call(
        body, name="ssd_fwd", grid=(nc,),
        in_specs=[pl.BlockSpec((CHUNK, AW), lambda c: (c, 0)), pl.BlockSpec((CHUNK, NS), lambda c: (c, 4)),
                  pl.BlockSpec((CHUNK, NS), lambda c: (c, 5)), pl.BlockSpec((CHUNK, DT_PAD), lambda c: (c, 6)),
                  vec, vec, vec],
        out_specs=[pl.BlockSpec((CHUNK, AW), lambda c: (c, 0)), pl.BlockSpec((None, AW, NS), lambda c: (c, 0, 0))],
        out_shape=[jax.ShapeDtypeStruct((T, AW), F32), jax.ShapeDtypeStruct((nc, AW, NS), F32)],
        scratch_shapes=[pltpu.VMEM((AW, NS), F32), pltpu.VMEM((CHUNK, CHUNK), F32)],
        compiler_params=_cparams(("arbitrary",)),
    )(act, act, act, xbcdt, bias, alog, dsk)


def _ssd_bwd(act, xbcdt, bias, alog, dsk, states, dy):
    nc = T // CHUNK

    def body(xs_ref, b_ref, c_ref, dt_ref, bias_ref, alog_ref, dsk_ref, st_ref, dy_ref,
             dact_ref, ddt_ref, par_ref, dstate, cst):
        step = pl.program_id(0)

        @pl.when(step == 0)
        def _():
            dstate[...] = jnp.zeros_like(dstate)
            par_ref[...] = jnp.zeros_like(par_ref)
        s = _ssd_prep(dt_ref, bias_ref, alog_ref, dsk_ref, b_ref, c_ref, xs_ref, st_ref, cst)
        Bm, Cm, prev, lam, x, xdt, f, cl = s["Bm"], s["Cm"], s["prev"], s["lam"], s["x"], s["xdt"], s["f"], s["cl"]
        lane = lax.broadcasted_iota(jnp.int32, (1, DT_PAD), 1)
        row = lax.broadcasted_iota(jnp.int32, (1, CHUNK, 1), 1)
        g = s["cb"] * lam
        gb, xdtb, prevb = g.astype(BF16), xdt.astype(BF16), prev.astype(BF16)
        dy = _heads(lambda h: dy_ref[:, pl.ds(HD * h, HD)])
        dyb = dy.astype(BF16)
        dnew = _heads(lambda h: dstate[pl.ds(HD * h, HD), :])
        dnewb = dnew.astype(BF16)
        E = jnp.exp(s["cs_col"])
        ecl = jnp.exp(cl)
        dG = _heads(lambda h: _nt(dyb[h], xdtb[h]))
        dxdt = _heads(lambda h: _tn(gb[h], dyb[h]))
        Yo = _heads(lambda h: _nt(Cm, prevb[h]))
        W = _heads(lambda h: _nt(Bm, dnewb[h]))
        dcb = jnp.sum(dG * lam, axis=0)
        Mm = dG * g
        col_sums = jnp.sum(Mm, axis=1, keepdims=True)
        dYo = (dy * E).astype(BF16)
        dxdt = dxdt + W * f
        dF = jnp.sum(W * xdt, axis=2, keepdims=True) * f
        dcl = jnp.sum(dnew * prev, axis=(1, 2), keepdims=True) * ecl + jnp.sum(dF, axis=1, keepdims=True)
        dcs = (jnp.sum(Mm, axis=2, keepdims=True) + jnp.sum(dy * Yo, axis=2, keepdims=True) * E - dF
               + jnp.where(row == CHUNK - 1, dcl, 0.0))
        ddt_x = jnp.sum(dxdt * x, axis=2, keepdims=True)
        dD = jnp.sum(dy * x, axis=(1, 2), keepdims=True)
        dx = s["dsk_col"] * dy + dxdt * s["dt_col"]
        xfb = (xdt * f).astype(BF16)
        dprev = _heads(lambda h: _tn(dYo[h], Cm)) + dnew * ecl
        dcbb = dcb.astype(BF16)
        dC = _nn(dcbb, Bm)
        dB = _tn(dcbb, Cm)
        dcs_mat = -_rows_to_block([col_sums[h] for h in range(HEADS)], CHUNK, CHUNK).T
        ddt_mat = jnp.zeros((CHUNK, DT_PAD), F32)
        dD_row = jnp.zeros((1, DT_PAD), F32)
        for h in range(HEADS):
            sl = pl.ds(HD * h, HD)
            dC = dC + _nn(dYo[h], prevb[h])
            dB = dB + _nn(xfb[h], dnewb[h])
            dcs_mat = dcs_mat + jnp.where(lane == h, dcs[h], 0.0)
            ddt_mat = ddt_mat + jnp.where(lane == h, ddt_x[h], 0.0)
            dD_row = dD_row + jnp.where(lane == h, dD[h], 0.0)
            dact_ref[:, sl] = dx[h]
            dstate[sl, :] = dprev[h]
        dact_ref[:, pl.ds(AW, NS)] = dB
        dact_ref[:, pl.ds(AW + NS, NS)] = dC
        da = jnp.dot((s["li"] <= s["si"]).astype(F32), dcs_mat, precision=HIGHEST, preferred_element_type=F32)
        ddtp = jnp.where(lane < HEADS, (ddt_mat + da * s["A"]) * _sigmoid(s["dtp"]), 0.0)
        ddt_ref[...] = ddtp
        dalog = jnp.where(lane < HEADS, jnp.sum(da * s["dt"], axis=0, keepdims=True) * s["A"], 0.0)
        par_ref[...] += _rows_to_block([jnp.sum(ddtp, axis=0, keepdims=True), dalog, dD_row], 8, DT_PAD)

    vec = pl.BlockSpec((1, DT_PAD), lambda c: (0, 0))
    rev = lambda c: nc - 1 - c
    return pl.pallas_call(
        body, name="ssd_bwd", grid=(nc,),
        in_specs=[pl.BlockSpec((CHUNK, AW), lambda c: (rev(c), 0)), pl.BlockSpec((CHUNK, NS), lambda c: (rev(c), 4)),
                  pl.BlockSpec((CHUNK, NS), lambda c: (rev(c), 5)), pl.BlockSpec((CHUNK, DT_PAD), lambda c: (rev(c), 6)),
                  vec, vec, vec,
                  pl.BlockSpec((None, AW, NS), lambda c: (rev(c), 0, 0)), pl.BlockSpec((CHUNK, AW), lambda c: (rev(c), 0))],
        out_specs=[pl.BlockSpec((CHUNK, CONV_CH), lambda c: (rev(c), 0)), pl.BlockSpec((CHUNK, DT_PAD), lambda c: (rev(c), 0)),
                   pl.BlockSpec((8, DT_PAD), lambda c: (0, 0))],
        out_shape=[jax.ShapeDtypeStruct((T, CONV_CH), F32), jax.ShapeDtypeStruct((T, DT_PAD), F32),
                   jax.ShapeDtypeStruct((8, DT_PAD), F32)],
        scratch_shapes=[pltpu.VMEM((AW, NS), F32), pltpu.VMEM((CHUNK, CHUNK), F32)],
        compiler_params=_cparams(("arbitrary",)),
    )(act, act, act, xbcdt, bias, alog, dsk, states, dy)


def _place():
    return lax.axis_index("x"), lax.axis_index("y"), lax.axis_index("c")


def _slot(px, py, pc):
    return 4 * px + 2 * py + pc


SLAB_ROWS = 24


def _all_reduce_small(parts, name):
    R, C = SLAB_ROWS, D
    n = len(parts)

    def body(*refs):
        in_refs = refs[:n]
        out_ref, slab, got, send_sems, recv_sems = refs[n:]
        slab[...] = jnp.zeros_like(slab)
        for ref, (arr, row) in zip(in_refs, parts):
            slab[pl.ds(row, arr.shape[0]), pl.ds(0, arr.shape[1])] = ref[...]
        x, y, c = _place()
        mine = _slot(x, y, c)
        copies = [pltpu.make_async_remote_copy(
            src_ref=slab, dst_ref=got.at[mine], send_sem=send_sems.at[kk], recv_sem=recv_sems.at[kk],
            device_id=peer, device_id_type=MESH) for kk, peer in enumerate(_peers(x, y, c))]
        for cp in copies:
            cp.start()
        got[mine] = slab[...]
        for cp in copies:
            cp.wait_recv()
        acc = got[0]
        for s in range(1, N_DEV):
            acc = acc + got[s]
        out_ref[...] = acc
        for cp in copies:
            cp.wait_send()

    vm = pl.BlockSpec(memory_space=pltpu.VMEM)
    return pl.pallas_call(
        body, name=name, in_specs=[vm] * n, out_specs=vm,
        out_shape=jax.ShapeDtypeStruct((R, C), F32),
        scratch_shapes=[pltpu.VMEM((R, C), F32), pltpu.VMEM((N_DEV, R, C), F32), pltpu.SemaphoreType.DMA((N_DEV - 1,)),
                        pltpu.SemaphoreType.DMA((N_DEV - 1,))],
    )(*[a for a, _ in parts])


_HBM = pl.BlockSpec(memory_space=pltpu.HBM)
_SEM = pl.BlockSpec(memory_space=pltpu.SEMAPHORE)
_EFFECT = pltpu.SideEffectType.DATAFLOW_SIDE_EFFECTING


def _peers(x, y, c):
    out = []
    for kk in range(1, N_DEV):
        fx, fy, fc = kk >> 2 & 1, kk >> 1 & 1, kk & 1
        out.append((1 - x if fx else x, 1 - y if fy else y, 1 - c if fc else c))
    return out


def _send_start(src, per_peer, name, dep):
    (handles, token) = _send_start_many([src], per_peer, name, dep)
    return handles, token


def _send_start_many(srcs, per_peer, name, dep):
    n = len(srcs)

    def body(*refs):
        src_refs, land_refs = refs[:n], refs[n:2 * n]
        send_sems, recv_sems = refs[2 * n + 1], refs[2 * n + 2]
        token = refs[-1]
        x, y, c = _place()
        mine = _slot(x, y, c)
        for a in range(n):
            for kk, peer in enumerate(_peers(x, y, c)):
                pltpu.make_async_remote_copy(
                    src_ref=src_refs[a].at[_slot(*peer)] if per_peer else src_refs[a], dst_ref=land_refs[a].at[mine],
                    send_sem=send_sems.at[a * (N_DEV - 1) + kk], recv_sem=recv_sems.at[a * (N_DEV - 1) + kk],
                    device_id=peer, device_id_type=MESH).start()
        token[...] = jnp.zeros_like(token)

    lands = [lax.empty((N_DEV,) + tuple(s.shape[1:] if per_peer else s.shape), s.dtype) for s in srcs]
    hbm = lambda t: pltpu.with_memory_space_constraint(t, pltpu.HBM)
    outs = pl.pallas_call(
        body, name=name,
        out_shape=(pltpu.SemaphoreType.DMA((n * (N_DEV - 1),)), pltpu.SemaphoreType.DMA((n * (N_DEV - 1),)),
                   *[pltpu.HBM(s.shape, s.dtype) for s in srcs], *[pltpu.HBM(l.shape, l.dtype) for l in lands],
                   jax.ShapeDtypeStruct((8, 128), F32)),
        in_specs=(*[_HBM] * (2 * n), _ANY),
        out_specs=(_SEM, _SEM, *[_HBM] * (2 * n), pl.BlockSpec(memory_space=pltpu.VMEM)),
        input_output_aliases={i: 2 + i for i in range(2 * n)},
        compiler_params=pltpu.CompilerParams(has_side_effects=_EFFECT),
    )(*[hbm(s) for s in srcs], *[hbm(l) for l in lands], dep)
    return (outs[0], outs[1], list(outs[2:2 + n]), list(outs[2 + n:2 + 2 * n])), outs[-1]


def _send_wait(handles, after, name):
    srcs, lands = _send_wait_many(handles, after, name)
    return srcs[0], lands[0]


def _send_wait_many(handles, after, name):
    send_sems, recv_sems, src_thrus, land_thrus = handles
    n = len(src_thrus)

    def body(*refs):
        land_refs = refs[n:2 * n]
        send_sems, recv_sems = refs[2 * n], refs[2 * n + 1]
        me = _place()
        for a in range(n):
            for kk in range(N_DEV - 1):
                cp = pltpu.make_async_remote_copy(
                    src_ref=land_refs[a].at[0], dst_ref=land_refs[a].at[0],
                    send_sem=send_sems.at[a * (N_DEV - 1) + kk], recv_sem=recv_sems.at[a * (N_DEV - 1) + kk],
                    device_id=me, device_id_type=MESH)
                cp.wait_send()
                cp.wait_recv()

    both = list(src_thrus) + list(land_thrus)
    outs = pl.pallas_call(
        body, name=name,
        out_shape=tuple(pltpu.HBM(t.shape, t.dtype) for t in both),
        in_specs=(*[_HBM] * (2 * n), _SEM, _SEM, _ANY), out_specs=tuple([_HBM] * (2 * n)),
        input_output_aliases={i: i for i in range(2 * n)},
        compiler_params=pltpu.CompilerParams(has_side_effects=_EFFECT),
    )(*both, send_sems, recv_sems, after)
    return list(outs[:n]), list(outs[n:])


def _sum_slots(land, name):
    _, R, C = land.shape
    tm = R if R <= 512 else 512

    def body(x_ref, o_ref):
        acc = x_ref[0].astype(F32)
        for j in range(1, N_DEV):
            acc = acc + x_ref[j].astype(F32)
        o_ref[...] = acc

    return pl.pallas_call(
        body, name=name, grid=(R // tm,),
        in_specs=[pl.BlockSpec((N_DEV, tm, C), lambda i: (0, i, 0))], out_specs=pl.BlockSpec((tm, C), lambda i: (i, 0)),
        out_shape=jax.ShapeDtypeStruct((R, C), F32), compiler_params=_cparams(("parallel",)),
    )(land)


def _adam_math(w, g, m, v):
    m2 = ADAM_B1 * m + (1.0 - ADAM_B1) * g
    v2 = ADAM_B2 * v + (1.0 - ADAM_B2) * (g * g)
    m_hat = m2 / (1.0 - ADAM_B1 ** ADAM_STEP)
    v_hat = v2 / (1.0 - ADAM_B2 ** ADAM_STEP)
    delta = -ADAM_LR * (m_hat / (jnp.sqrt(v_hat) + ADAM_EPS) + ADAM_WD * w)
    return delta, m2, v2


def _adamw(w, g, m, v, name):
    R, C = w.shape
    tm = R if R <= 512 else 256
    return _rowwise(lambda w, g, m, v: (_adam_math(w, g, m, v), ()), [w, g, m, v], [], [(C, F32)] * 3, [], tm=tm, name=name)


def _adamw_small(slab, slab_rows, g_conv_w, ws, ms, vs):
    n = len(ws)

    def body(*refs):
        slab_ref, gc_ref = refs[0], refs[1]
        w_refs, m_refs, v_refs = refs[2:2 + n], refs[2 + n:2 + 2 * n], refs[2 + 2 * n:2 + 3 * n]
        outs = refs[2 + 3 * n:]
        loss_ref = outs[0]
        g_out, d_out, m_out, v_out = (outs[1 + i * n:1 + (i + 1) * n] for i in range(4))
        loss_ref[...] = jnp.sum(slab_ref[pl.ds(6, 1), :], axis=1, keepdims=True)
        for i in range(n):
            g = gc_ref[...] if i == n - 1 else slab_ref[pl.ds(slab_rows[i], 1), pl.ds(0, ws[i].shape[1])]
            d, m2, v2 = _adam_math(w_refs[i][...], g, m_refs[i][...], v_refs[i][...])
            g_out[i][...] = g
            d_out[i][...] = d
            m_out[i][...] = m2
            v_out[i][...] = v2

    vm = pl.BlockSpec(memory_space=pltpu.VMEM)
    shapes = [jax.ShapeDtypeStruct(w.shape, F32) for w in ws]
    outs = pl.pallas_call(
        body, name="adamw_small", in_specs=[vm] * (2 + 3 * n), out_specs=[vm] * (1 + 4 * n),
        out_shape=[jax.ShapeDtypeStruct((1, 1), F32)] + shapes * 4,
    )(slab, g_conv_w, *ws, *ms, *vs)
    return outs[0], outs[1:1 + n], outs[1 + n:1 + 2 * n], outs[1 + 2 * n:1 + 3 * n], outs[1 + 3 * n:]


SMALL = ["norm_mix_pre", "norm_mix_post", "norm_mlp_pre", "norm_mlp_post", "norm_ple_post",
         "conv_b", "ssd_norm_g", "dt_bias", "a_log", "d_skip"]


def _pad_row(v, width=D):
    return jnp.pad(v, ((0, 0), (0, width - v.shape[1])))


def kernel(x, p, positions, norm_mix_pre, norm_mix_post, w_in, conv_w, conv_b, dt_bias, a_log, d_skip, ssd_norm_g, w_out, norm_mlp_pre, norm_mlp_post, w_up, w_down, w_ple_gate, w_ple_proj, norm_ple_post, loss_target, m_norm_mix_pre, m_norm_mix_post, m_w_in, m_conv_w, m_conv_b, m_dt_bias, m_a_log, m_d_skip, m_ssd_norm_g, m_w_out, m_norm_mlp_pre, m_norm_mlp_post, m_w_up, m_w_down, m_w_ple_gate, m_w_ple_proj, m_norm_ple_post, v_norm_mix_pre, v_norm_mix_post, v_w_in, v_conv_w, v_conv_b, v_dt_bias, v_a_log, v_d_skip, v_ssd_norm_g, v_w_out, v_norm_mlp_pre, v_norm_mlp_post, v_w_up, v_w_down, v_w_ple_gate, v_w_ple_proj, v_norm_ple_post):
    args = dict(locals())
    x2, p2, tgt = x[0], p[0, 0], loss_target[0]
    g1, g2, g3, g4, g5 = norm_mix_pre, norm_mix_post, norm_mlp_pre, norm_mlp_post, norm_ple_post

    me = _slot(*_place())
    pack_in = jnp.pad(w_in[0].T, ((0, W_IN_SHARD_PAD - W_IN_SHARD), (0, 0))).astype(BF16)
    rest = [w_out[0].astype(BF16), w_up[0].T.astype(BF16), w_down[0].astype(BF16), w_ple_gate[0].astype(BF16),
            w_ple_proj[0].T.reshape(32, D).astype(BF16)]
    conv_pack = jnp.pad(conv_w[0], ((0, 4), (0, 32)))
    in_handles, tok_in0 = _send_start_many([pack_in, conv_pack], False, "gather_in_start", g1)

    inv_freq = ROPE_THETA ** (-jnp.arange(HD // 2, dtype=F32) * 2.0 / HD)
    pos = positions[0] + tok_in0[0, 0].astype(jnp.int32)
    ang = pos.astype(F32)[:, None] * inv_freq
    cos, sin = jnp.cos(ang), jnp.sin(ang)
    cos128 = jnp.concatenate([cos, cos, cos, cos], axis=1)
    sin128 = jnp.concatenate([-sin, sin, -sin, sin], axis=1)

    bias_w, alog_w, dsk_w = _pad_row(dt_bias, DT_PAD), _pad_row(a_log, DT_PAD), _pad_row(d_skip, DT_PAD)

    (u1,) = _rowwise(lambda a, g: ((a * _rstd(a) * g,), ()), [x2], [g1], [(D, BF16)], [], tm=512, name="norm_x",
                     deps=[cos128, sin128])
    p2b = p2.astype(BF16)

    in_back, in_land = _send_wait_many(in_handles, u1, "gather_in_wait")
    gin = lax.dynamic_update_slice(in_land[0], in_back[0][None], (me, 0, 0))
    gconv = lax.dynamic_update_slice(in_land[1], in_back[1][None], (me, 0, 0))
    rest_handles, tok_rest = _send_start_many(rest, False, "gather_rest_start", gconv)
    w_inT = gin[:, :W_IN_SHARD].reshape(IN_W, D)
    w_qkvzT = w_inT[:4 * AW]
    w_xbcdtT = jnp.pad(w_inT[4 * AW:], ((0, DT_PAD - HEADS), (0, 0)))
    conv_full = gconv[:, :CONV_K, :96].transpose(1, 0, 2).reshape(CONV_K, CONV_CH)
    qkvz = _mm(u1, w_qkvzT, tb=True, tm=512, tn=2048, tk=1024, name="proj_qkvz", deps=[tok_rest])
    xbcdt = _mm(u1, w_xbcdtT, tb=True, tm=512, tn=896, tk=1024, name="proj_xbcdt")

    qkv = _rope_fwd(qkvz, cos128, sin128)
    qkv = [qkv[3 * i:3 * i + 3] for i in range(len(DILATIONS))]
    outs, lses = [], []
    for d, (qd, kd, vd) in zip(DILATIONS, qkv):
        o, l = _attn_fwd(qd, kd, vd, d)
        outs.append(o)
        lses.append(l)
    attn, lse, attn4, lse4, attn16, lse16 = _attn_merge(outs, lses)

    act = _conv_fwd(xbcdt, conv_full, conv_b)
    y_ssd, states = _ssd_fwd(act, xbcdt, bias_w, alog_w, dsk_w)

    def gated_fwd(y, z, a, gs):
        gi = y * (z * _sigmoid(z))
        return (jnp.concatenate([a, gi * _rstd(gi) * gs], axis=1),), ()
    (cat,) = _rowwise(gated_fwd, [y_ssd, (qkvz, AW, 3), attn], [ssd_norm_g], [(D, BF16)], [], tm=512, name="gated_norm")

    rest_back, landed = _send_wait_many(rest_handles, cat, "gather_rest_wait")
    landed = [lax.dynamic_update_slice(l, b[None], (me, 0, 0)) for l, b in zip(landed, rest_back)]
    w_o, w_upT, w_dn, w_gate = landed[0].reshape(D, D), landed[1].reshape(DFF, D), landed[2].reshape(DFF, D), landed[3].reshape(D, D)
    w_projT = landed[4].reshape(D, PLE)

    mix = _mm(cat, w_o, tm=512, tn=1024, tk=1024, name="mix_out")

    def post1(xx, mm, ga, gb):
        h = xx + mm * _rstd(mm) * ga
        return (h, _rstd(h)), ()
    h1, r3 = _rowwise(post1, [x2, mix], [g2, g3], [(D, F32), (1, F32)], [], tm=512, name="post_mix")

    a_up, ff, u2, h2, h2b = _mlp_fwd(h1, r3, g3, w_upT, w_dn, g4)
    relu2 = lambda a: jnp.square(jnp.maximum(a.astype(F32), 0.0))

    gp = _mm(h2b, w_gate, tm=512, tn=1024, tk=1024, name="ple_gate")
    pp = _mm(p2b, w_projT, tb=True, tm=512, tn=1024, tk=256, name="ple_proj")

    def final(hh, gpre, ppv, tg, g):
        sg = _sigmoid(gpre)
        ple = ppv * sg
        r = _rstd(ple)
        n = ple * r
        h3 = hh + n * g
        e = h3 - tg
        dh3 = e * (1.0 / D)
        dple = _rms_bwd(n, r, g, dh3)
        return (dh3, dple * sg, dple * ppv * sg * (1.0 - sg)), (_colsum(dh3 * n), _colsum(0.5 * e * e * (1.0 / D)))
    dh3, dpp, dgp, dg5, loss_vec = _rowwise(final, [h2, gp, pp, tgt], [g5], [(D, F32), (D, BF16), (D, BF16)],
                                            [(1, D), (1, D)], tm=512, name="loss_ple_bwd")

    gw_projT = _mm(dpp, p2b, ta=True, tm=512, tn=256, tk=T, out_dtypes=(BF16,), name="gw_ple_proj")
    gw_gate = _mm(h2b, dgp, ta=True, tm=512, tn=1024, tk=T, out_dtypes=(BF16,), name="gw_ple_gate")
    rs_proj, tok_proj = _send_start(gw_projT.reshape(N_DEV, 32, D), True, "rs_start_w_proj", g1)
    rs_gate, tok_gate = _send_start(gw_gate.reshape(N_DEV, 128, D), True, "rs_start_w_gate", g1)
    dh2_g = _mm(dgp, w_gate, tb=True, tm=512, tn=1024, tk=1024, name="dx_ple_gate", deps=[tok_proj, tok_gate])

    def bwd_mlp_post(d3, dg_, f, g):
        dh2 = d3 + dg_
        r = _rstd(f)
        n = f * r
        return (dh2, _rms_bwd(n, r, g, dh2)), (_colsum(dh2 * n),)
    dh2, dff, dg4 = _rowwise(bwd_mlp_post, [dh3, dh2_g, ff], [g4], [(D, F32), (D, BF16)], [(1, D)], tm=512,
                             name="bwd_post_mlp")

    gw_dn = _mm(a_up, dff, ta=True, tm=512, tn=1024, tk=T, a_pre=relu2, out_dtypes=(BF16,), name="gw_mlp_down")
    rs_dn, tok_dn = _send_start(gw_dn.reshape(N_DEV, 512, D), True, "rs_start_w_down", g1)
    da_up, du2 = _mlp_dx(dff, a_up, w_upT, w_dn, tok_dn)
    gw_upT = _mm(da_up, u2, ta=True, tm=512, tn=1024, tk=T, out_dtypes=(BF16,), name="gw_mlp_up")
    rs_up, tok_up = _send_start(gw_upT.reshape(N_DEV, 512, D), True, "rs_start_w_up", g1)

    def bwd_mix_post(d2, du, hh, rr, mm, ga, gb):
        n3 = hh * rr
        dh1 = d2 + _rms_bwd(n3, rr, gb, du)
        r = _rstd(mm)
        n2 = mm * r
        return (dh1, _rms_bwd(n2, r, ga, dh1)), (_colsum(du * n3), _colsum(dh1 * n2))
    dh1, dmix, dg3, dg2 = _rowwise(bwd_mix_post, [dh2, du2, h1, r3, mix], [g2, g3], [(D, F32), (D, BF16)],
                                   [(1, D), (1, D)], tm=512, name="bwd_post_mix", deps=[tok_up])

    gw_o = _mm(cat, dmix, ta=True, tm=512, tn=1024, tk=T, out_dtypes=(BF16,), name="gw_out")
    rs_o, tok_o = _send_start(gw_o.reshape(N_DEV, 128, D), True, "rs_start_w_out", g1)
    dcat = _mm(dmix, w_o, tb=True, tm=512, tn=1024, tk=1024, name="dx_out", deps=[tok_o])

    def gated_bwd(y, z, dyn, gs):
        sg = _sigmoid(z)
        sz = z * sg
        gi = y * sz
        r = _rstd(gi)
        n = gi * r
        dgi = _rms_bwd(n, r, gs, dyn)
        return (dgi * sz, dgi * y * (sg * (1.0 + z * (1.0 - sg)))), (_colsum(dyn * n),)
    dy_ssd, dz, dgs = _rowwise(gated_bwd, [y_ssd, (qkvz, AW, 3), (dcat, AW, 1)], [ssd_norm_g], [(AW, F32)] * 2, [(1, AW)],
                               tm=512, name="bwd_gated_norm")

    dact, ddtw, ssd_par = _ssd_bwd(act, xbcdt, bias_w, alog_w, dsk_w, states, dy_ssd)
    dxbcdt, conv_par = _conv_bwd(xbcdt, dact, ddtw, conv_full, conv_b)

    dattn4, dattn16 = _dilate_cols(dcat, 0)
    qkv_grads = [_attn_bwd(*qkv[0], dcat, attn, lse, 1),
                 _attn_bwd(*qkv[1], dattn4, attn4, lse4, 4),
                 _attn_bwd(*qkv[2], dattn16, attn16, lse16, 16)]
    dqkvz = _rope_bwd(qkv_grads, dz, cos128, sin128)

    gw_qkvzT = _mm(dqkvz, u1, ta=True, tm=512, tn=1024, tk=T, out_dtypes=(BF16,), name="gw_qkvz")
    gw_xbcdtT = _mm(dxbcdt, u1, ta=True, tm=896, tn=1024, tk=T, out_dtypes=(BF16,), name="gw_xbcdt")
    gw_inT = jnp.concatenate([gw_qkvzT, gw_xbcdtT], axis=0)[:IN_W]
    gw_inT = jnp.pad(gw_inT.reshape(N_DEV, W_IN_SHARD, D), ((0, 0), (0, W_IN_SHARD_PAD - W_IN_SHARD), (0, 0)))
    rs_in, tok_in = _send_start(gw_inT, True, "rs_start_w_in", g1)

    du1a = _mm(dqkvz, w_qkvzT, tm=512, tn=1024, tk=2048, name="dx_qkvz", deps=[tok_in])
    du1b = _mm(dxbcdt, w_xbcdtT, tm=512, tn=1024, tk=896, name="dx_xbcdt")

    def bwd_in(d1, ua, ub, xx, g):
        rr = _rstd(xx)
        n = xx * rr
        du = ua + ub
        return (d1 + _rms_bwd(n, rr, g, du),), (_colsum(du * n),)
    grad_x, dg1 = _rowwise(bwd_in, [dh1, du1a, du1b, x2], [g1], [(D, F32)], [(1, D)], tm=512, name="bwd_pre_mix")

    slab = _all_reduce_small([(dg1, 0), (dg2, 1), (dg3, 2), (dg4, 3), (dg5, 4), (dgs, 5), (loss_vec, 6),
                              (conv_par, 8), (ssd_par, 16)], "reduce_small")
    g_conv_w = lax.dynamic_slice(slab[8:12, :CONV_CH], (0, me * 96), (CONV_K, 96))

    def scatter_finish(handles, nm, after):
        part, land = _send_wait(handles, after, "rs_wait_" + nm)
        own = lax.dynamic_slice(part, (me, 0, 0), (1,) + part.shape[1:])
        return _sum_slots(lax.dynamic_update_slice(land, own, (me, 0, 0)), "rs_sum_" + nm)
    g_out = scatter_finish(rs_o, "w_out", slab)
    g_upT = scatter_finish(rs_up, "w_up", slab)
    g_dn = scatter_finish(rs_dn, "w_down", slab)
    g_gate = scatter_finish(rs_gate, "w_gate", slab)
    g_projT = scatter_finish(rs_proj, "w_proj", slab)

    small_names = SMALL + ["conv_w"]
    small_rows = [0, 1, 2, 3, 4, 12, 5, 16, 17, 18, None]
    pick = lambda prefix: [args[prefix + nme] for nme in SMALL] + [args[prefix + "conv_w"][0]]
    loss11, g_s, d_s, m_s, v_s = _adamw_small(slab, small_rows, g_conv_w, pick(""), pick("m_"), pick("v_"))
    loss = loss11[0, 0]
    grads = {
        "w_out": g_out[None], "w_up": g_upT.T[None], "w_down": g_dn[None],
        "w_ple_gate": g_gate[None], "w_ple_proj": g_projT.reshape(128, PLE).T[None],
    }
    delta, new_m, new_v = {}, {}, {}
    for i, nme in enumerate(small_names):
        lead = (lambda t: t[None]) if nme == "conv_w" else (lambda t: t)
        grads[nme], delta[nme], new_m[nme], new_v[nme] = lead(g_s[i]), lead(d_s[i]), lead(m_s[i]), lead(v_s[i])
    for nme in ["w_out", "w_up", "w_down", "w_ple_gate", "w_ple_proj", "w_in"]:
        if nme == "w_in":
            g_inT = scatter_finish(rs_in, "w_in", delta["w_down"])
            grads["w_in"] = g_inT[:W_IN_SHARD].T[None]
        dl, mm_, vv_ = _adamw(args[nme][0], grads[nme][0], args["m_" + nme][0], args["v_" + nme][0], "adamw_" + nme)
        delta[nme], new_m[nme], new_v[nme] = dl[None], mm_[None], vv_[None]

    order = ["norm_mix_pre", "norm_mix_post", "w_in", "conv_w", "conv_b", "dt_bias", "a_log", "d_skip", "ssd_norm_g",
             "w_out", "norm_mlp_pre", "norm_mlp_post", "w_up", "w_down", "w_ple_gate", "w_ple_proj", "norm_ple_post"]
    return (loss, grad_x[None], *[grads[n] for n in order], *[delta[n] for n in order],
            *[new_m[n] for n in order], *[new_v[n] for n in order])
```

```python
import functools
import math

import jax
import jax.numpy as jnp
from jax import lax
from jax.experimental import pallas as pl
from jax.experimental.pallas import tpu as pltpu

F32 = jnp.float32
BF16 = jnp.bfloat16
MESH = pl.DeviceIdType.MESH
HIGHEST = lax.Precision.HIGHEST

N_DEV = 8
T = 4096
D = 1024
HEADS = 8
HD = 64
AW = 512
NS = 128
CONV_K = 4
CONV_CH = 768
CHUNK = 128
DFF = 4096
PLE = 256
EPS = 1e-6
ROPE_THETA = 10000.0
DILATIONS = (1, 4, 16)
QBLK = 128
NEG = -1e30
IN_W = 2824
W_IN_SHARD = 353
W_IN_SHARD_PAD = 384
DT_PAD = 128

ADAM_LR, ADAM_B1, ADAM_B2, ADAM_EPS, ADAM_WD, ADAM_STEP = 0.001, 0.9, 0.999, 1e-08, 0.01, 10

VMEM_LIMIT = 56 * 1024 * 1024


_ANY = pl.BlockSpec(memory_space=pl.ANY)


def _cparams(sem=None):
    return pltpu.CompilerParams(dimension_semantics=sem, vmem_limit_bytes=VMEM_LIMIT)


def _dot(a, b, ca, cb, precision=None):
    return lax.dot_general(a, b, (((ca,), (cb,)), ((), ())), preferred_element_type=F32, precision=precision)


def _nn(a, b):
    return _dot(a, b, 1, 0)


def _nt(a, b):
    return _dot(a, b, 1, 1)


def _tn(a, b):
    return _dot(a, b, 0, 0)


def _sigmoid(x):
    return 1.0 / (1.0 + jnp.exp(-x))


def _softplus(x):
    return jnp.maximum(x, 0.0) + jnp.log(1.0 + jnp.exp(-jnp.abs(x)))


def _mm(a, b, *, ta=False, tb=False, tm, tn, tk, name,
        a_pre=None, a_rows=(), a_cols=(), b_pre=None, b_rows=(), b_cols=(),
        epi=None, epi_tiles=(), out_dtypes=(F32,), deps=()):
    if ta:
        K, M = a.shape
    else:
        M, K = a.shape
    if tb:
        N, K2 = b.shape
    else:
        K2, N = b.shape
    assert K == K2 and M % tm == 0 and N % tn == 0 and K % tk == 0, (name, a.shape, b.shape)
    nk = K // tk
    if ta:
        a_spec = pl.BlockSpec((tk, tm), lambda i, j, k: (k, i))
        a_row_specs = [pl.BlockSpec((tk, 1), lambda i, j, k: (k, 0)) for _ in a_rows]
        a_col_specs = [pl.BlockSpec((1, tm), lambda i, j, k: (0, i)) for _ in a_cols]
    else:
        a_spec = pl.BlockSpec((tm, tk), lambda i, j, k: (i, k))
        a_row_specs = [pl.BlockSpec((tm, 1), lambda i, j, k: (i, 0)) for _ in a_rows]
        a_col_specs = [pl.BlockSpec((1, tk), lambda i, j, k: (0, k)) for _ in a_cols]
    if tb:
        b_spec = pl.BlockSpec((tn, tk), lambda i, j, k: (j, k))
        b_row_specs = [pl.BlockSpec((tn, 1), lambda i, j, k: (j, 0)) for _ in b_rows]
        b_col_specs = [pl.BlockSpec((1, tk), lambda i, j, k: (0, k)) for _ in b_cols]
    else:
        b_spec = pl.BlockSpec((tk, tn), lambda i, j, k: (k, j))
        b_row_specs = [pl.BlockSpec((tk, 1), lambda i, j, k: (k, 0)) for _ in b_rows]
        b_col_specs = [pl.BlockSpec((1, tn), lambda i, j, k: (0, j)) for _ in b_cols]
    o_spec = pl.BlockSpec((tm, tn), lambda i, j, k: (i, j))
    na, nb, ne, no = len(a_rows) + len(a_cols), len(b_rows) + len(b_cols), len(epi_tiles), len(out_dtypes)

    def body(*refs):
        a_ref, b_ref = refs[0], refs[1]
        a_ex = refs[2:2 + na]
        b_ex = refs[2 + na:2 + na + nb]
        e_ex = refs[2 + na + nb:2 + na + nb + ne]
        first_out = 2 + na + nb + ne + len(deps)
        outs = refs[first_out:first_out + no]

        def finish(res):
            vals = epi(res, *[r[...] for r in e_ex]) if epi is not None else (res,)
            for o_ref, val in zip(outs, vals):
                o_ref[...] = val.astype(o_ref.dtype)

        at = a_ref[...]
        if a_pre is not None:
            at = a_pre(at, *[r[...] for r in a_ex])
        bt = b_ref[...]
        if b_pre is not None:
            bt = b_pre(bt, *[r[...] for r in b_ex])
        prod = _dot(at.astype(BF16), bt.astype(BF16), 0 if ta else 1, 1 if tb else 0)
        if nk == 1:
            finish(prod)
            return
        acc = refs[-1]
        k = pl.program_id(2)

        @pl.when(k == 0)
        def _():
            acc[...] = jnp.zeros_like(acc)
        acc[...] += prod

        @pl.when(k == nk - 1)
        def _():
            finish(acc[...])

    outs = pl.pallas_call(
        body, name=name,
        grid=(M // tm, N // tn, nk),
        in_specs=([a_spec, b_spec] + a_row_specs + a_col_specs + b_row_specs + b_col_specs + [o_spec] * ne
                  + [_ANY] * len(deps)),
        out_specs=[o_spec] * no,
        out_shape=[jax.ShapeDtypeStruct((M, N), dt) for dt in out_dtypes],
        scratch_shapes=[pltpu.VMEM((tm, tn), F32)] if nk > 1 else [],
        compiler_params=_cparams(("parallel", "parallel", "arbitrary")),
    )(a, b, *a_rows, *a_cols, *b_rows, *b_cols, *epi_tiles, *deps)
    return outs[0] if no == 1 else outs


MLP_TM = 1024
MLP_TC = 512


def _mlp_fwd(h, r, g, w_upT, w_dn, g_post):
    nc = DFF // MLP_TC

    def body(h_ref, r_ref, g_ref, wu_ref, wd_ref, gp_ref, a_ref, ff_ref, u_ref, ho_ref, hob_ref, acc, u_scr):
        c = pl.program_id(1)

        @pl.when(c == 0)
        def _():
            u = (h_ref[...] * r_ref[...] * g_ref[...]).astype(BF16)
            u_scr[...] = u
            u_ref[...] = u
            acc[...] = jnp.zeros_like(acc)
        a = _nt(u_scr[...], wu_ref[...])
        a_ref[...] = a.astype(BF16)
        acc[...] += _nn(jnp.square(jnp.maximum(a, 0.0)).astype(BF16), wd_ref[...])

        @pl.when(c == nc - 1)
        def _():
            f = acc[...]
            ff_ref[...] = f
            ho = h_ref[...] + f * _rstd(f) * gp_ref[...]
            ho_ref[...] = ho
            hob_ref[...] = ho.astype(BF16)

    row = pl.BlockSpec((MLP_TM, D), lambda i, c: (i, 0))
    wsp = pl.BlockSpec((MLP_TC, D), lambda i, c: (c, 0))
    vec = pl.BlockSpec((1, D), lambda i, c: (0, 0))
    return pl.pallas_call(
        body, name="mlp_fwd", grid=(T // MLP_TM, nc),
        in_specs=[row, pl.BlockSpec((MLP_TM, 1), lambda i, c: (i, 0)), vec, wsp, wsp, vec],
        out_specs=[pl.BlockSpec((MLP_TM, MLP_TC), lambda i, c: (i, c)), row, row, row, row],
        out_shape=[jax.ShapeDtypeStruct((T, DFF), BF16), jax.ShapeDtypeStruct((T, D), F32), jax.ShapeDtypeStruct((T, D), BF16),
                   jax.ShapeDtypeStruct((T, D), F32), jax.ShapeDtypeStruct((T, D), BF16)],
        scratch_shapes=[pltpu.VMEM((MLP_TM, D), F32), pltpu.VMEM((MLP_TM, D), BF16)],
        compiler_params=_cparams(("parallel", "arbitrary")),
    )(h, r, g, w_upT, w_dn, g_post)


def _mlp_dx(dff, a, w_upT, w_dn, dep):
    nc = DFF // MLP_TC

    def body(d_ref, a_ref, wu_ref, wd_ref, dep_ref, da_ref, du_ref, acc, d_scr):
        c = pl.program_id(1)

        @pl.when(c == 0)
        def _():
            d_scr[...] = d_ref[...].astype(BF16)
            acc[...] = jnp.zeros_like(acc)
        da = (_nt(d_scr[...], wd_ref[...]) * (2.0 * jnp.maximum(a_ref[...].astype(F32), 0.0))).astype(BF16)
        da_ref[...] = da
        acc[...] += _nn(da, wu_ref[...])

        @pl.when(c == nc - 1)
        def _():
            du_ref[...] = acc[...]

    row = pl.BlockSpec((MLP_TM, D), lambda i, c: (i, 0))
    wsp = pl.BlockSpec((MLP_TC, D), lambda i, c: (c, 0))
    chunk = pl.BlockSpec((MLP_TM, MLP_TC), lambda i, c: (i, c))
    return pl.pallas_call(
        body, name="mlp_dx", grid=(T // MLP_TM, nc),
        in_specs=[row, chunk, wsp, wsp, _ANY], out_specs=[chunk, row],
        out_shape=[jax.ShapeDtypeStruct((T, DFF), BF16), jax.ShapeDtypeStruct((T, D), F32)],
        scratch_shapes=[pltpu.VMEM((MLP_TM, D), F32), pltpu.VMEM((MLP_TM, D), BF16)],
        compiler_params=_cparams(("parallel", "arbitrary")),
    )(dff, a, w_upT, w_dn, dep)


def _rowwise(fn, rows, vecs, out_rows, out_sums, *, tm, name, deps=()):
    specs, arrs = [], []
    R = None
    for r in rows:
        if isinstance(r, tuple):
            arr, width, cb = r
            specs.append(pl.BlockSpec((tm, width), lambda i, cb=cb: (i, cb)))
        else:
            arr = r
            specs.append(pl.BlockSpec((tm, arr.shape[1]), lambda i: (i, 0)))
        R = arr.shape[0] if R is None else R
        assert arr.shape[0] == R, name
        arrs.append(arr)
    assert R % tm == 0, name
    for v in vecs:
        specs.append(pl.BlockSpec(v.shape, lambda i: (0, 0)))
        arrs.append(v)
    nr, nv, no, ns = len(rows), len(vecs), len(out_rows), len(out_sums)
    out_specs = [pl.BlockSpec((tm, w), lambda i: (i, 0)) for w, _ in out_rows]
    out_specs += [pl.BlockSpec(s, lambda i: (0, 0)) for s in out_sums]
    out_shape = [jax.ShapeDtypeStruct((R, w), dt) for w, dt in out_rows]
    out_shape += [jax.ShapeDtypeStruct(s, F32) for s in out_sums]

    nd = len(deps)

    def body(*refs):
        ins = [r[...] for r in refs[:nr + nv]]
        o_refs = refs[nr + nv + nd:nr + nv + nd + no]
        s_refs = refs[nr + nv + nd + no:]
        o_vals, s_vals = fn(*ins)
        for ref, val in zip(o_refs, o_vals):
            ref[...] = val.astype(ref.dtype)
        if ns:
            @pl.when(pl.program_id(0) == 0)
            def _():
                for ref in s_refs:
                    ref[...] = jnp.zeros_like(ref)
            for ref, val in zip(s_refs, s_vals):
                ref[...] += val

    outs = pl.pallas_call(
        body, name=name, grid=(R // tm,), in_specs=specs + [_ANY] * nd, out_specs=out_specs, out_shape=out_shape,
        compiler_params=_cparams(("arbitrary",) if ns else ("parallel",)),
    )(*arrs, *deps)
    return outs


def _mm_rows(fn, mats, rows, vecs, out_rows, out_sums, *, tm, name, deps=()):
    R = mats[0][0].shape[0]
    assert R % tm == 0, name
    specs, arrs = [], []
    for a, b, tb in mats:
        specs += [pl.BlockSpec((tm, a.shape[1]), lambda i: (i, 0)), pl.BlockSpec(b.shape, lambda i: (0, 0))]
        arrs += [a, b]
    for r in rows:
        specs.append(pl.BlockSpec((tm, r.shape[1]), lambda i: (i, 0)))
        arrs.append(r)
    for v in vecs:
        specs.append(pl.BlockSpec(v.shape, lambda i: (0, 0)))
        arrs.append(v)
    nm, nr, nv, nd, no, ns = len(mats), len(rows), len(vecs), len(deps), len(out_rows), len(out_sums)
    out_specs = [pl.BlockSpec((tm, w), lambda i: (i, 0)) for w, _ in out_rows]
    out_specs += [pl.BlockSpec(s, lambda i: (0, 0)) for s in out_sums]
    out_shape = [jax.ShapeDtypeStruct((R, w), dt) for w, dt in out_rows] + [jax.ShapeDtypeStruct(s, F32) for s in out_sums]

    def body(*refs):
        prods = [_dot(refs[2 * p][...].astype(BF16), refs[2 * p + 1][...].astype(BF16), 1, 1 if mats[p][2] else 0)
                 for p in range(nm)]
        ins = [r[...] for r in refs[2 * nm:2 * nm + nr + nv]]
        first_out = 2 * nm + nr + nv + nd
        o_refs, s_refs = refs[first_out:first_out + no], refs[first_out + no:]
        o_vals, s_vals = fn(*prods, *ins)
        for ref, val in zip(o_refs, o_vals):
            ref[...] = val.astype(ref.dtype)
        if ns:
            @pl.when(pl.program_id(0) == 0)
            def _():
                for ref in s_refs:
                    ref[...] = jnp.zeros_like(ref)
            for ref, val in zip(s_refs, s_vals):
                ref[...] += val

    return pl.pallas_call(
        body, name=name, grid=(R // tm,), in_specs=specs + [_ANY] * nd, out_specs=out_specs, out_shape=out_shape,
        compiler_params=_cparams(("arbitrary",) if ns else ("parallel",)),
    )(*arrs, *deps)


def _colsum(x):
    return jnp.sum(x, axis=0, keepdims=True)


def _rstd(x):
    return lax.rsqrt(jnp.mean(x * x, axis=-1, keepdims=True) + EPS)


def _rms_bwd(xn, r, g, dy):
    dn = dy * g
    return r * (dn - xn * jnp.mean(dn * xn, axis=-1, keepdims=True))


def _partner(t):
    lane = lax.broadcasted_iota(jnp.int32, t.shape, 1)
    up = pltpu.roll(t, 96, 1)
    down = pltpu.roll(t, 32, 1)
    return jnp.where((lane % 64) < 32, up, down)


SLABS = AW // 128


def _rows(r, n, d):
    return pl.ds(r, n, stride=d) if d > 1 else pl.ds(0, n)


def _undilate(src_ref, dst, d, tm):
    for r in range(d):
        for j in range(SLABS):
            dst[j][_rows(r, tm // d, d), :] = src_ref[:, pl.ds(r * AW + j * 128, 128)].astype(dst[j].dtype)


def _dilate(dst_ref, src, d, tm):
    for r in range(d):
        for j in range(SLABS):
            dst_ref[:, pl.ds(r * AW + j * 128, 128)] = src[j][_rows(r, tm // d, d), :].astype(dst_ref.dtype)


def _slab_scratch(n, tm):
    return [pltpu.VMEM((tm, 128), F32)] * (SLABS * n)


def _slab_groups(flat):
    return [flat[SLABS * i:SLABS * (i + 1)] for i in range(len(flat) // SLABS)]


def _slab_specs(tm, first):
    return [pl.BlockSpec((tm, 128), lambda i, j=j: (i, first + j)) for j in range(SLABS)]


def _dil_spec(tm, d):
    return pl.BlockSpec((tm // d, d * AW), lambda i: (i, 0))


ROPE_TM = 512


def _rope_fwd(qkvz, cos128, sin128):
    tm = ROPE_TM

    def body(*refs):
        q_refs, k_refs, v_refs = refs[0:4], refs[4:8], refs[8:12]
        c_ref, s_ref = refs[12], refs[13]
        outs = refs[14:23]
        qs, ks = _slab_groups(refs[23:])
        c, s = c_ref[...], s_ref[...]
        for j in range(SLABS):
            q, k = q_refs[j][...], k_refs[j][...]
            qs[j][...] = (q * c + _partner(q) * s) * (HD ** -0.5)
            ks[j][...] = k * c + _partner(k) * s
        for di, d in enumerate(DILATIONS):
            oq, ok, ov = outs[3 * di:3 * di + 3]
            for r in range(d):
                rows = _rows(r, tm // d, d)
                for j in range(SLABS):
                    cols = pl.ds(r * AW + j * 128, 128)
                    oq[:, cols] = qs[j][rows, :].astype(BF16)
                    ok[:, cols] = ks[j][rows, :].astype(BF16)
                    ov[:, cols] = v_refs[j][rows, :].astype(BF16)

    tab = pl.BlockSpec((tm, 128), lambda i: (i, 0))
    out_specs, out_shape = [], []
    for d in DILATIONS:
        out_specs += [_dil_spec(tm, d)] * 3
        out_shape += [jax.ShapeDtypeStruct((T // d, d * AW), BF16)] * 3
    return pl.pallas_call(
        body, name="rope_fwd", grid=(T // tm,),
        in_specs=_slab_specs(tm, 0) + _slab_specs(tm, 4) + _slab_specs(tm, 8) + [tab, tab],
        out_specs=out_specs, out_shape=out_shape, scratch_shapes=_slab_scratch(2, tm),
        compiler_params=_cparams(("parallel",)),
    )(*([qkvz] * 12), cos128, sin128)


def _rope_bwd(grads, dz, cos128, sin128):
    tm = 256

    def body(*refs):
        g_refs = refs[0:9]
        dz_ref, c_ref, s_ref, o_ref = refs[9], refs[10], refs[11], refs[12]
        scr = _slab_groups(refs[13:])
        for di, d in enumerate(DILATIONS[1:]):
            for t in range(3):
                _undilate(g_refs[3 * (di + 1) + t], scr[3 * di + t], d, tm)
        c, s = c_ref[...], s_ref[...]
        for j in range(SLABS):
            cols = pl.ds(j * 128, 128)
            tot = [g_refs[t][:, cols] + scr[t][j][...] + scr[3 + t][j][...] for t in range(3)]
            dqr = tot[0] * (HD ** -0.5)
            o_ref[:, pl.ds(j * 128, 128)] = (dqr * c + _partner(dqr * s)).astype(BF16)
            o_ref[:, pl.ds(AW + j * 128, 128)] = (tot[1] * c + _partner(tot[1] * s)).astype(BF16)
            o_ref[:, pl.ds(2 * AW + j * 128, 128)] = tot[2].astype(BF16)
        o_ref[:, pl.ds(3 * AW, AW)] = dz_ref[...].astype(BF16)

    tab = pl.BlockSpec((tm, 128), lambda i: (i, 0))
    in_specs, args = [], []
    for d, g in zip(DILATIONS, grads):
        in_specs += [_dil_spec(tm, d)] * 3
        args += list(g)
    return pl.pallas_call(
        body, name="rope_bwd", grid=(T // tm,),
        in_specs=in_specs + [pl.BlockSpec((tm, AW), lambda i: (i, 0)), tab, tab],
        out_specs=pl.BlockSpec((tm, 4 * AW), lambda i: (i, 0)),
        out_shape=jax.ShapeDtypeStruct((T, 4 * AW), BF16),
        scratch_shapes=_slab_scratch(6, tm),
        compiler_params=_cparams(("parallel",)),
    )(*args, dz, cos128, sin128)


def _dilate_cols(x, first):
    tm = ROPE_TM

    def body(x0, x1, x2, x3, o4, o16):
        xs = (x0, x1, x2, x3)
        for o_ref, d in ((o4, 4), (o16, 16)):
            for r in range(d):
                for j in range(SLABS):
                    o_ref[:, pl.ds(r * AW + j * 128, 128)] = xs[j][_rows(r, tm // d, d), :]

    return pl.pallas_call(
        body, name="dilate_cols", grid=(T // tm,),
        in_specs=_slab_specs(tm, first), out_specs=[_dil_spec(tm, 4), _dil_spec(tm, 16)],
        out_shape=[jax.ShapeDtypeStruct((T // 4, 4 * AW), F32), jax.ShapeDtypeStruct((T // 16, 16 * AW), F32)],
        compiler_params=_cparams(("parallel",)),
    )(x, x, x, x)


def _band_masks():
    qi = lax.broadcasted_iota(jnp.int32, (QBLK, QBLK), 0)
    kj = lax.broadcasted_iota(jnp.int32, (QBLK, QBLK), 1)
    return kj >= qi, kj <= qi


def _attn_fwd(q, k, v, d):
    L = q.shape[0]
    npair = L // (2 * QBLK)

    def body(q_ref, kp_ref, kc_ref, vp_ref, vc_ref, o_ref, l_ref):
        pair = pl.program_id(1)
        mask_p, mask_c = _band_masks()
        for sub in range(2):
            rows = pl.ds(sub * QBLK, QBLK)
            first = jnp.where(pair > 0, 0.0, NEG) if sub == 0 else 0.0
            bias = jnp.concatenate([jnp.where(mask_p, 0.0, NEG) + first, jnp.where(mask_c, 0.0, NEG)], axis=1)
            k_prev = (lambda sl: kp_ref[:, sl]) if sub == 0 else (lambda sl: kc_ref[pl.ds(0, QBLK), sl])
            v_prev = (lambda sl: vp_ref[:, sl]) if sub == 0 else (lambda sl: vc_ref[pl.ds(0, QBLK), sl])
            s = []
            for h in range(HEADS):
                sl = pl.ds(HD * h, HD)
                qh = q_ref[rows, sl]
                s.append(jnp.concatenate([_nt(qh, k_prev(sl)), _nt(qh, kc_ref[rows, sl])], axis=1))
            s = jnp.stack(s) + bias
            m = jnp.max(s, axis=2, keepdims=True)
            e = jnp.exp(s - m)
            den = jnp.sum(e, axis=2, keepdims=True)
            p = e.astype(BF16)
            inv = 1.0 / den
            lse = m + jnp.log(den)
            for h in range(HEADS):
                sl = pl.ds(HD * h, HD)
                o_ref[rows, sl] = (_nn(p[h, :, :QBLK], v_prev(sl)) + _nn(p[h, :, QBLK:], vc_ref[rows, sl])) * inv[h]
                l_ref[rows, sl] = jnp.broadcast_to(lse[h], (QBLK, HD))

    cur = pl.BlockSpec((2 * QBLK, AW), lambda r, n: (n, r))
    prev = pl.BlockSpec((QBLK, AW), lambda r, n: (jnp.maximum(2 * n - 1, 0), r))
    return pl.pallas_call(
        body, name=f"attn_fwd_d{d}", grid=(d, npair),
        in_specs=[cur, prev, cur, prev, cur], out_specs=[cur, cur],
        out_shape=[jax.ShapeDtypeStruct((L, d * AW), F32)] * 2,
        compiler_params=_cparams(("parallel", "parallel")),
    )(q, k, k, v, v)


def _attn_bwd(q, k, v, do, at, lse, d):
    L = q.shape[0]
    nb = L // QBLK

    def body(q0_ref, q1_ref, kp_ref, kc_ref, vp_ref, vc_ref, do0_ref, do1_ref, at0_ref, at1_ref,
             l0_ref, l1_ref, dq_ref, dk_ref, dv_ref):
        n = pl.program_id(1)
        mask_p, mask_c = _band_masks()
        prev_bias = jnp.where(mask_p, 0.0, NEG)
        bias = jnp.concatenate([prev_bias + jnp.where(n > 0, 0.0, NEG), jnp.where(mask_c, 0.0, NEG),
                                prev_bias + jnp.where(n < nb - 1, 0.0, NEG)], axis=1)
        s, dp, ls, dl, ops = [], [], [], [], []
        for h in range(HEADS):
            sl = pl.ds(HD * h, HD)
            one = pl.ds(HD * h, 1)
            q0, q1 = q0_ref[:, sl], q1_ref[:, sl]
            kp, kc, vp, vc = kp_ref[:, sl], kc_ref[:, sl], vp_ref[:, sl], vc_ref[:, sl]
            do0, do1 = do0_ref[:, sl], do1_ref[:, sl]
            do0b, do1b = do0.astype(BF16), do1.astype(BF16)
            s.append(jnp.concatenate([_nt(q0, kp), _nt(q0, kc), _nt(q1, kc)], axis=1))
            dp.append(jnp.concatenate([_nt(do0b, vp), _nt(do0b, vc), _nt(do1b, vc)], axis=1))
            dl0 = jnp.sum(do0 * at0_ref[:, sl], axis=1, keepdims=True)
            dl1 = jnp.sum(do1 * at1_ref[:, sl], axis=1, keepdims=True)
            dl.append(jnp.concatenate([jnp.broadcast_to(dl0, (QBLK, 2 * QBLK)), jnp.broadcast_to(dl1, (QBLK, QBLK))], axis=1))
            ls.append(jnp.concatenate([jnp.broadcast_to(l0_ref[:, one], (QBLK, 2 * QBLK)),
                                       jnp.broadcast_to(l1_ref[:, one], (QBLK, QBLK))], axis=1))
            ops.append((q0, q1, kp, kc, do0b, do1b))
        p = jnp.exp(jnp.stack(s) + bias - jnp.stack(ls))
        ds = (p * (jnp.stack(dp) - jnp.stack(dl))).astype(BF16)
        p = p.astype(BF16)
        for h in range(HEADS):
            sl = pl.ds(HD * h, HD)
            q0, q1, kp, kc, do0b, do1b = ops[h]
            dq_ref[:, sl] = (_nn(ds[h, :, :QBLK], kp) + _nn(ds[h, :, QBLK:2 * QBLK], kc)).astype(BF16)
            dv_ref[:, sl] = (_tn(p[h, :, QBLK:2 * QBLK], do0b) + _tn(p[h, :, 2 * QBLK:], do1b)).astype(BF16)
            dk_ref[:, sl] = (_tn(ds[h, :, QBLK:2 * QBLK], q0) + _tn(ds[h, :, 2 * QBLK:], q1)).astype(BF16)

    cur = pl.BlockSpec((QBLK, AW), lambda r, n: (n, r))
    prev = pl.BlockSpec((QBLK, AW), lambda r, n: (jnp.maximum(n - 1, 0), r))
    nxt = pl.BlockSpec((QBLK, AW), lambda r, n: (jnp.minimum(n + 1, nb - 1), r))
    return pl.pallas_call(
        body, name=f"attn_bwd_d{d}", grid=(d, nb),
        in_specs=[cur, nxt, prev, cur, prev, cur, cur, nxt, cur, nxt, cur, nxt], out_specs=[cur, cur, cur],
        out_shape=[jax.ShapeDtypeStruct((L, d * AW), BF16)] * 3,
        compiler_params=_cparams(("parallel", "parallel")),
    )(q, q, k, k, v, v, do, do, at, at, lse, lse)


def _attn_merge(outs, lses):
    tm = ROPE_TM

    def body(o1, o4, o16, l1, l4, l16, at_ref, ls_ref, at4, ls4, at16, ls16, *flat):
        so4, so16, sl4, sl16, sa, sl = _slab_groups(flat)
        _undilate(o4, so4, 4, tm)
        _undilate(o16, so16, 16, tm)
        _undilate(l4, sl4, 4, tm)
        _undilate(l16, sl16, 16, tm)
        for j in range(SLABS):
            cols = pl.ds(j * 128, 128)
            a, b, c = l1[:, cols], sl4[j][...], sl16[j][...]
            m = jnp.maximum(jnp.maximum(a, b), c)
            e1, e2, e3 = jnp.exp(a - m), jnp.exp(b - m), jnp.exp(c - m)
            s = e1 + e2 + e3
            inv = 1.0 / s
            attn = (e1 * inv) * o1[:, cols] + (e2 * inv) * so4[j][...] + (e3 * inv) * so16[j][...]
            lse = m + jnp.log(s)
            at_ref[:, cols] = attn
            ls_ref[:, cols] = lse
            sa[j][...] = attn
            sl[j][...] = lse
        _dilate(at4, sa, 4, tm)
        _dilate(at16, sa, 16, tm)
        _dilate(ls4, sl, 4, tm)
        _dilate(ls16, sl, 16, tm)

    specs = [_dil_spec(tm, d) for d in DILATIONS]
    tok = specs[0]
    return pl.pallas_call(
        body, name="attn_merge", grid=(T // tm,),
        in_specs=specs + specs, out_specs=[tok, tok, specs[1], specs[1], specs[2], specs[2]],
        out_shape=[jax.ShapeDtypeStruct((T, AW), F32)] * 2 + [jax.ShapeDtypeStruct((T // 4, 4 * AW), F32)] * 2
        + [jax.ShapeDtypeStruct((T // 16, 16 * AW), F32)] * 2,
        scratch_shapes=_slab_scratch(6, tm),
        compiler_params=_cparams(("parallel",)),
    )(*outs, *lses)


CONV_TM = 512
HALO = 8


def _conv_pre(ext, w, b):
    y = b + w[3] * ext
    for kk in range(1, CONV_K):
        y = y + w[3 - kk] * pltpu.roll(ext, kk, 0)
    return y


def _rows_to_block(rows, n, width):
    ri = lax.broadcasted_iota(jnp.int32, (n, width), 0)
    out = jnp.zeros((n, width), F32)
    for j, r in enumerate(rows):
        out = out + jnp.where(ri == j, r, 0.0)
    return out


def _conv_fwd(xbc, w, b):
    nblk = T // CONV_TM

    def body(x_ref, h_ref, w_ref, b_ref, o_ref):
        i = pl.program_id(0)
        halo = jnp.where(i > 0, h_ref[...], 0.0)
        ext = jnp.concatenate([halo, x_ref[...]], axis=0)
        y = _conv_pre(ext, [w_ref[pl.ds(j, 1), :] for j in range(CONV_K)], b_ref[...])[HALO:]
        o_ref[...] = y * _sigmoid(y)

    return pl.pallas_call(
        body, name="conv_fwd", grid=(nblk,),
        in_specs=[pl.BlockSpec((CONV_TM, CONV_CH), lambda i: (i, 0)),
                  pl.BlockSpec((HALO, CONV_CH), lambda i: (jnp.maximum(i * (CONV_TM // HALO) - 1, 0), 0)),
                  pl.BlockSpec((CONV_K, CONV_CH), lambda i: (0, 0)),
                  pl.BlockSpec((1, CONV_CH), lambda i: (0, 0))],
        out_specs=pl.BlockSpec((CONV_TM, CONV_CH), lambda i: (i, 0)),
        out_shape=jax.ShapeDtypeStruct((T, CONV_CH), F32),
        compiler_params=_cparams(("parallel",)),
    )(xbc, xbc, w, b)


def _conv_bwd(xbc, dact, ddt, w, b):
    nblk = T // CONV_TM
    per = CONV_TM // HALO

    def body(x_ref, xb_ref, xa_ref, g_ref, ga_ref, ddt_ref, w_ref, b_ref, dx_ref, dw_ref):
        i = pl.program_id(0)
        wv = [w_ref[pl.ds(j, 1), :] for j in range(CONV_K)]
        before = jnp.where(i > 0, xb_ref[...], 0.0)
        last = i == nblk - 1
        after = jnp.where(last, 0.0, xa_ref[...])
        g_after = jnp.where(last, 0.0, ga_ref[...])
        ext = jnp.concatenate([before, x_ref[...], after], axis=0)
        y = _conv_pre(ext, wv, b_ref[...])[HALO:]
        sg = _sigmoid(y)
        dy = jnp.concatenate([g_ref[...], g_after], axis=0) * (sg * (1.0 + y * (1.0 - sg)))
        n = CONV_TM + HALO
        dx = wv[3] * dy
        for kk in range(1, CONV_K):
            dx = dx + wv[3 - kk] * pltpu.roll(dy, n - kk, 0)
        dx_ref[:, pl.ds(0, CONV_CH)] = dx[:CONV_TM].astype(BF16)
        dx_ref[:, pl.ds(CONV_CH, DT_PAD)] = ddt_ref[...].astype(BF16)
        dyc = dy[:CONV_TM]
        rows = [jnp.sum(dyc * (pltpu.roll(ext, 3 - j, 0) if j < 3 else ext)[HALO:HALO + CONV_TM], axis=0, keepdims=True)
                for j in range(CONV_K)]
        rows.append(jnp.sum(dyc, axis=0, keepdims=True))
        part = _rows_to_block(rows, 8, CONV_CH)

        @pl.when(i == 0)
        def _():
            dw_ref[...] = jnp.zeros_like(dw_ref)
        dw_ref[...] += part

    blk = pl.BlockSpec((CONV_TM, CONV_CH), lambda i: (i, 0))
    hb = pl.BlockSpec((HALO, CONV_CH), lambda i: (jnp.maximum(i * per - 1, 0), 0))
    ha = pl.BlockSpec((HALO, CONV_CH), lambda i: (jnp.minimum((i + 1) * per, T // HALO - 1), 0))
    return pl.pallas_call(
        body, name="conv_bwd", grid=(nblk,),
        in_specs=[blk, hb, ha, blk, ha, pl.BlockSpec((CONV_TM, DT_PAD), lambda i: (i, 0)),
                  pl.BlockSpec((CONV_K, CONV_CH), lambda i: (0, 0)), pl.BlockSpec((1, CONV_CH), lambda i: (0, 0))],
        out_specs=[pl.BlockSpec((CONV_TM, CONV_CH + DT_PAD), lambda i: (i, 0)), pl.BlockSpec((8, CONV_CH), lambda i: (0, 0))],
        out_shape=[jax.ShapeDtypeStruct((T, CONV_CH + DT_PAD), BF16), jax.ShapeDtypeStruct((8, CONV_CH), F32)],
        compiler_params=_cparams(("arbitrary",)),
    )(xbc, xbc, xbc, dact, dact, ddt, w, b)


def _pick(mat, h):
    lane = lax.broadcasted_iota(jnp.int32, mat.shape, 1)
    return jnp.sum(jnp.where(lane == h, mat, 0.0), axis=1, keepdims=True)


def _heads(fn):
    return jnp.stack([fn(h) for h in range(HEADS)])


def _ssd_prep(dt_ref, bias_ref, alog_ref, dsk_ref, b_ref, c_ref, xs_ref, state_ref, cst):
    li = lax.broadcasted_iota(jnp.int32, (CHUNK, CHUNK), 0)
    si = lax.broadcasted_iota(jnp.int32, (CHUNK, CHUNK), 1)
    tri = li >= si
    dtp = dt_ref[...] + bias_ref[...]
    dt = _softplus(dtp)
    A = -jnp.exp(alog_ref[...])
    a = dt * A
    cs = jnp.dot(tri.astype(F32), a, precision=HIGHEST, preferred_element_type=F32)
    cst[...] = cs.T
    Bm = b_ref[...].astype(BF16)
    Cm = c_ref[...].astype(BF16)
    cb = _nt(Cm, Bm)
    dskv = dsk_ref[...]
    cs_col = _heads(lambda h: _pick(cs, h))
    cs_row = _heads(lambda h: cst[pl.ds(h, 1), :])
    dt_col = _heads(lambda h: _pick(dt, h))
    dsk_col = _heads(lambda h: _pick(dskv, h))
    lam = jnp.exp(jnp.where(tri, cs_col - cs_row, NEG))
    x = _heads(lambda h: xs_ref[:, pl.ds(HD * h, HD)])
    xdt = x * dt_col
    prev = _heads(lambda h: state_ref[pl.ds(HD * h, HD), :])
    lane = lax.broadcasted_iota(jnp.int32, (1, 1, CHUNK), 2)
    cl = jnp.sum(jnp.where(lane == CHUNK - 1, cs_row, 0.0), axis=2, keepdims=True)
    f = jnp.exp(cl - cs_col)
    return dict(li=li, si=si, dtp=dtp, dt=dt, A=A, Bm=Bm, Cm=Cm, cb=cb, cs_col=cs_col, dt_col=dt_col, dsk_col=dsk_col,
                lam=lam, x=x, xdt=xdt, prev=prev, cl=cl, f=f)


def _ssd_fwd(act, xbcdt, bias, alog, dsk):
    nc = T // CHUNK

    def body(xs_ref, b_ref, c_ref, dt_ref, bias_ref, alog_ref, dsk_ref, y_ref, st_ref, state, cst):
        @pl.when(pl.program_id(0) == 0)
        def _():
            state[...] = jnp.zeros_like(state)
        st_ref[...] = state[...]
        s = _ssd_prep(dt_ref, bias_ref, alog_ref, dsk_ref, b_ref, c_ref, xs_ref, state, cst)
        Bm, Cm, prev = s["Bm"], s["Cm"], s["prev"]
        g = (s["cb"] * s["lam"]).astype(BF16)
        xdtb = s["xdt"].astype(BF16)
        prevb = prev.astype(BF16)
        y = _heads(lambda h: _nn(g[h], xdtb[h])) + _heads(lambda h: _nt(Cm, prevb[h])) * jnp.exp(s["cs_col"])
        y = y + s["dsk_col"] * s["x"]
        xf = (s["xdt"] * s["f"]).astype(BF16)
        new = prev * jnp.exp(s["cl"]) + _heads(lambda h: _tn(xf[h], Bm))
        for h in range(HEADS):
            y_ref[:, pl.ds(HD * h, HD)] = y[h]
            state[pl.ds(HD * h, HD), :] = new[h]

    vec = pl.BlockSpec((1, DT_PAD), lambda c: (0, 0))
    return pl.pallas_call(
        body, name="ssd_fwd", grid=(nc,),
        in_specs=[pl.BlockSpec((CHUNK, AW), lambda c: (c, 0)), pl.BlockSpec((CHUNK, NS), lambda c: (c, 4)),
                  pl.BlockSpec((CHUNK, NS), lambda c: (c, 5)), pl.BlockSpec((CHUNK, DT_PAD), lambda c: (c, 6)),
                  vec, vec, vec],
        out_specs=[pl.BlockSpec((CHUNK, AW), lambda c: (c, 0)), pl.BlockSpec((None, AW, NS), lambda c: (c, 0, 0))],
        out_shape=[jax.ShapeDtypeStruct((T, AW), F32), jax.ShapeDtypeStruct((nc, AW, NS), F32)],
        scratch_shapes=[pltpu.VMEM((AW, NS), F32), pltpu.VMEM((CHUNK, CHUNK), F32)],
        compiler_params=_cparams(("arbitrary",)),
    )(act, act, act, xbcdt, bias, alog, dsk)


def _ssd_bwd(act, xbcdt, bias, alog, dsk, states, dy):
    nc = T // CHUNK

    def body(xs_ref, b_ref, c_ref, dt_ref, bias_ref, alog_ref, dsk_ref, st_ref, dy_ref,
             dact_ref, ddt_ref, par_ref, dstate, cst):
        step = pl.program_id(0)

        @pl.when(step == 0)
        def _():
            dstate[...] = jnp.zeros_like(dstate)
            par_ref[...] = jnp.zeros_like(par_ref)
        s = _ssd_prep(dt_ref, bias_ref, alog_ref, dsk_ref, b_ref, c_ref, xs_ref, st_ref, cst)
        Bm, Cm, prev, lam, x, xdt, f, cl = s["Bm"], s["Cm"], s["prev"], s["lam"], s["x"], s["xdt"], s["f"], s["cl"]
        lane = lax.broadcasted_iota(jnp.int32, (1, DT_PAD), 1)
        row = lax.broadcasted_iota(jnp.int32, (1, CHUNK, 1), 1)
        g = s["cb"] * lam
        gb, xdtb, prevb = g.astype(BF16), xdt.astype(BF16), prev.astype(BF16)
        dy = _heads(lambda h: dy_ref[:, pl.ds(HD * h, HD)])
        dyb = dy.astype(BF16)
        dnew = _heads(lambda h: dstate[pl.ds(HD * h, HD), :])
        dnewb = dnew.astype(BF16)
        E = jnp.exp(s["cs_col"])
        ecl = jnp.exp(cl)
        dG = _heads(lambda h: _nt(dyb[h], xdtb[h]))
        dxdt = _heads(lambda h: _tn(gb[h], dyb[h]))
        Yo = _heads(lambda h: _nt(Cm, prevb[h]))
        W = _heads(lambda h: _nt(Bm, dnewb[h]))
        dcb = jnp.sum(dG * lam, axis=0)
        Mm = dG * g
        col_sums = jnp.sum(Mm, axis=1, keepdims=True)
        dYo = (dy * E).astype(BF16)
        dxdt = dxdt + W * f
        dF = jnp.sum(W * xdt, axis=2, keepdims=True) * f
        dcl = jnp.sum(dnew * prev, axis=(1, 2), keepdims=True) * ecl + jnp.sum(dF, axis=1, keepdims=True)
        dcs = (jnp.sum(Mm, axis=2, keepdims=True) + jnp.sum(dy * Yo, axis=2, keepdims=True) * E - dF
               + jnp.where(row == CHUNK - 1, dcl, 0.0))
        ddt_x = jnp.sum(dxdt * x, axis=2, keepdims=True)
        dD = jnp.sum(dy * x, axis=(1, 2), keepdims=True)
        dx = s["dsk_col"] * dy + dxdt * s["dt_col"]
        xfb = (xdt * f).astype(BF16)
        dprev = _heads(lambda h: _tn(dYo[h], Cm)) + dnew * ecl
        dcbb = dcb.astype(BF16)
        dC = _nn(dcbb, Bm)
        dB = _tn(dcbb, Cm)
        dcs_mat = -_rows_to_block([col_sums[h] for h in range(HEADS)], CHUNK, CHUNK).T
        ddt_mat = jnp.zeros((CHUNK, DT_PAD), F32)
        dD_row = jnp.zeros((1, DT_PAD), F32)
        for h in range(HEADS):
            sl = pl.ds(HD * h, HD)
            dC = dC + _nn(dYo[h], prevb[h])
            dB = dB + _nn(xfb[h], dnewb[h])
            dcs_mat = dcs_mat + jnp.where(lane == h, dcs[h], 0.0)
            ddt_mat = ddt_mat + jnp.where(lane == h, ddt_x[h], 0.0)
            dD_row = dD_row + jnp.where(lane == h, dD[h], 0.0)
            dact_ref[:, sl] = dx[h]
            dstate[sl, :] = dprev[h]
        dact_ref[:, pl.ds(AW, NS)] = dB
        dact_ref[:, pl.ds(AW + NS, NS)] = dC
        da = jnp.dot((s["li"] <= s["si"]).astype(F32), dcs_mat, precision=HIGHEST, preferred_element_type=F32)
        ddtp = jnp.where(lane < HEADS, (ddt_mat + da * s["A"]) * _sigmoid(s["dtp"]), 0.0)
        ddt_ref[...] = ddtp
        dalog = jnp.where(lane < HEADS, jnp.sum(da * s["dt"], axis=0, keepdims=True) * s["A"], 0.0)
        par_ref[...] += _rows_to_block([jnp.sum(ddtp, axis=0, keepdims=True), dalog, dD_row], 8, DT_PAD)

    vec = pl.BlockSpec((1, DT_PAD), lambda c: (0, 0))
    rev = lambda c: nc - 1 - c
    return pl.pallas_call(
        body, name="ssd_bwd", grid=(nc,),
        in_specs=[pl.BlockSpec((CHUNK, AW), lambda c: (rev(c), 0)), pl.BlockSpec((CHUNK, NS), lambda c: (rev(c), 4)),
                  pl.BlockSpec((CHUNK, NS), lambda c: (rev(c), 5)), pl.BlockSpec((CHUNK, DT_PAD), lambda c: (rev(c), 6)),
                  vec, vec, vec,
                  pl.BlockSpec((None, AW, NS), lambda c: (rev(c), 0, 0)), pl.BlockSpec((CHUNK, AW), lambda c: (rev(c), 0))],
        out_specs=[pl.BlockSpec((CHUNK, CONV_CH), lambda c: (rev(c), 0)), pl.BlockSpec((CHUNK, DT_PAD), lambda c: (rev(c), 0)),
                   pl.BlockSpec((8, DT_PAD), lambda c: (0, 0))],
        out_shape=[jax.ShapeDtypeStruct((T, CONV_CH), F32), jax.ShapeDtypeStruct((T, DT_PAD), F32),
                   jax.ShapeDtypeStruct((8, DT_PAD), F32)],
        scratch_shapes=[pltpu.VMEM((AW, NS), F32), pltpu.VMEM((CHUNK, CHUNK), F32)],
        compiler_params=_cparams(("arbitrary",)),
    )(act, act, act, xbcdt, bias, alog, dsk, states, dy)


def _place():
    return lax.axis_index("x"), lax.axis_index("y"), lax.axis_index("c")


def _slot(px, py, pc):
    return 4 * px + 2 * py + pc


SLAB_ROWS = 24


def _all_reduce_small(parts, name):
    R, C = SLAB_ROWS, D
    n = len(parts)

    def body(*refs):
        in_refs = refs[:n]
        out_ref, slab, got, send_sems, recv_sems = refs[n:]
        slab[...] = jnp.zeros_like(slab)
        for ref, (arr, row) in zip(in_refs, parts):
            slab[pl.ds(row, arr.shape[0]), pl.ds(0, arr.shape[1])] = ref[...]
        x, y, c = _place()
        mine = _slot(x, y, c)
        copies = [pltpu.make_async_remote_copy(
            src_ref=slab, dst_ref=got.at[mine], send_sem=send_sems.at[kk], recv_sem=recv_sems.at[kk],
            device_id=peer, device_id_type=MESH) for kk, peer in enumerate(_peers(x, y, c))]
        for cp in copies:
            cp.start()
        got[mine] = slab[...]
        for cp in copies:
            cp.wait_recv()
        acc = got[0]
        for s in range(1, N_DEV):
            acc = acc + got[s]
        out_ref[...] = acc
        for cp in copies:
            cp.wait_send()

    vm = pl.BlockSpec(memory_space=pltpu.VMEM)
    return pl.pallas_call(
        body, name=name, in_specs=[vm] * n, out_specs=vm,
        out_shape=jax.ShapeDtypeStruct((R, C), F32),
        scratch_shapes=[pltpu.VMEM((R, C), F32), pltpu.VMEM((N_DEV, R, C), F32), pltpu.SemaphoreType.DMA((N_DEV - 1,)),
                        pltpu.SemaphoreType.DMA((N_DEV - 1,))],
    )(*[a for a, _ in parts])


_HBM = pl.BlockSpec(memory_space=pltpu.HBM)
_SEM = pl.BlockSpec(memory_space=pltpu.SEMAPHORE)
_EFFECT = pltpu.SideEffectType.DATAFLOW_SIDE_EFFECTING


def _peers(x, y, c):
    out = []
    for kk in range(1, N_DEV):
        fx, fy, fc = kk >> 2 & 1, kk >> 1 & 1, kk & 1
        out.append((1 - x if fx else x, 1 - y if fy else y, 1 - c if fc else c))
    return out


def _send_start(src, per_peer, name, dep):
    (handles, token) = _send_start_many([src], per_peer, name, dep)
    return handles, token


def _send_start_many(srcs, per_peer, name, dep):
    n = len(srcs)

    def body(*refs):
        src_refs, land_refs = refs[:n], refs[n:2 * n]
        send_sems, recv_sems = refs[2 * n + 1], refs[2 * n + 2]
        token = refs[-1]
        x, y, c = _place()
        mine = _slot(x, y, c)
        for a in range(n):
            for kk, peer in enumerate(_peers(x, y, c)):
                pltpu.make_async_remote_copy(
                    src_ref=src_refs[a].at[_slot(*peer)] if per_peer else src_refs[a], dst_ref=land_refs[a].at[mine],
                    send_sem=send_sems.at[a * (N_DEV - 1) + kk], recv_sem=recv_sems.at[a * (N_DEV - 1) + kk],
                    device_id=peer, device_id_type=MESH).start()
        token[...] = jnp.zeros_like(token)

    lands = [lax.empty((N_DEV,) + tuple(s.shape[1:] if per_peer else s.shape), s.dtype) for s in srcs]
    hbm = lambda t: pltpu.with_memory_space_constraint(t, pltpu.HBM)
    outs = pl.pallas_call(
        body, name=name,
        out_shape=(pltpu.SemaphoreType.DMA((n * (N_DEV - 1),)), pltpu.SemaphoreType.DMA((n * (N_DEV - 1),)),
                   *[pltpu.HBM(s.shape, s.dtype) for s in srcs], *[pltpu.HBM(l.shape, l.dtype) for l in lands],
                   jax.ShapeDtypeStruct((8, 128), F32)),
        in_specs=(*[_HBM] * (2 * n), _ANY),
        out_specs=(_SEM, _SEM, *[_HBM] * (2 * n), pl.BlockSpec(memory_space=pltpu.VMEM)),
        input_output_aliases={i: 2 + i for i in range(2 * n)},
        compiler_params=pltpu.CompilerParams(has_side_effects=_EFFECT),
    )(*[hbm(s) for s in srcs], *[hbm(l) for l in lands], dep)
    return (outs[0], outs[1], list(outs[2:2 + n]), list(outs[2 + n:2 + 2 * n])), outs[-1]


def _send_wait(handles, after, name):
    srcs, lands = _send_wait_many(handles, after, name)
    return srcs[0], lands[0]


def _send_wait_many(handles, after, name):
    send_sems, recv_sems, src_thrus, land_thrus = handles
    n = len(src_thrus)

    def body(*refs):
        land_refs = refs[n:2 * n]
        send_sems, recv_sems = refs[2 * n], refs[2 * n + 1]
        me = _place()
        for a in range(n):
            for kk in range(N_DEV - 1):
                cp = pltpu.make_async_remote_copy(
                    src_ref=land_refs[a].at[0], dst_ref=land_refs[a].at[0],
                    send_sem=send_sems.at[a * (N_DEV - 1) + kk], recv_sem=recv_sems.at[a * (N_DEV - 1) + kk],
                    device_id=me, device_id_type=MESH)
                cp.wait_send()
                cp.wait_recv()

    both = list(src_thrus) + list(land_thrus)
    outs = pl.pallas_call(
        body, name=name,
        out_shape=tuple(pltpu.HBM(t.shape, t.dtype) for t in both),
        in_specs=(*[_HBM] * (2 * n), _SEM, _SEM, _ANY), out_specs=tuple([_HBM] * (2 * n)),
        input_output_aliases={i: i for i in range(2 * n)},
        compiler_params=pltpu.CompilerParams(has_side_effects=_EFFECT),
    )(*both, send_sems, recv_sems, after)
    return list(outs[:n]), list(outs[n:])


def _sum_slots(land, name):
    _, R, C = land.shape
    tm = R if R <= 512 else 512

    def body(x_ref, o_ref):
        acc = x_ref[0].astype(F32)
        for j in range(1, N_DEV):
            acc = acc + x_ref[j].astype(F32)
        o_ref[...] = acc

    return pl.pallas_call(
        body, name=name, grid=(R // tm,),
        in_specs=[pl.BlockSpec((N_DEV, tm, C), lambda i: (0, i, 0))], out_specs=pl.BlockSpec((tm, C), lambda i: (i, 0)),
        out_shape=jax.ShapeDtypeStruct((R, C), F32), compiler_params=_cparams(("parallel",)),
    )(land)


def _adam_math(w, g, m, v):
    m2 = ADAM_B1 * m + (1.0 - ADAM_B1) * g
    v2 = ADAM_B2 * v + (1.0 - ADAM_B2) * (g * g)
    m_hat = m2 / (1.0 - ADAM_B1 ** ADAM_STEP)
    v_hat = v2 / (1.0 - ADAM_B2 ** ADAM_STEP)
    delta = -ADAM_LR * (m_hat / (jnp.sqrt(v_hat) + ADAM_EPS) + ADAM_WD * w)
    return delta, m2, v2


def _adamw(w, g, m, v, name):
    R, C = w.shape
    tm = R if R <= 512 else 256
    return _rowwise(lambda w, g, m, v: (_adam_math(w, g, m, v), ()), [w, g, m, v], [], [(C, F32)] * 3, [], tm=tm, name=name)


def _adamw_small(slab, slab_rows, g_conv_w, ws, ms, vs):
    n = len(ws)

    def body(*refs):
        slab_ref, gc_ref = refs[0], refs[1]
        w_refs, m_refs, v_refs = refs[2:2 + n], refs[2 + n:2 + 2 * n], refs[2 + 2 * n:2 + 3 * n]
        outs = refs[2 + 3 * n:]
        loss_ref = outs[0]
        g_out, d_out, m_out, v_out = (outs[1 + i * n:1 + (i + 1) * n] for i in range(4))
        loss_ref[...] = jnp.sum(slab_ref[pl.ds(6, 1), :], axis=1, keepdims=True)
        for i in range(n):
            g = gc_ref[...] if i == n - 1 else slab_ref[pl.ds(slab_rows[i], 1), pl.ds(0, ws[i].shape[1])]
            d, m2, v2 = _adam_math(w_refs[i][...], g, m_refs[i][...], v_refs[i][...])
            g_out[i][...] = g
            d_out[i][...] = d
            m_out[i][...] = m2
            v_out[i][...] = v2

    vm = pl.BlockSpec(memory_space=pltpu.VMEM)
    shapes = [jax.ShapeDtypeStruct(w.shape, F32) for w in ws]
    outs = pl.pallas_call(
        body, name="adamw_small", in_specs=[vm] * (2 + 3 * n), out_specs=[vm] * (1 + 4 * n),
        out_shape=[jax.ShapeDtypeStruct((1, 1), F32)] + shapes * 4,
    )(slab, g_conv_w, *ws, *ms, *vs)
    return outs[0], outs[1:1 + n], outs[1 + n:1 + 2 * n], outs[1 + 2 * n:1 + 3 * n], outs[1 + 3 * n:]


SMALL = ["norm_mix_pre", "norm_mix_post", "norm_mlp_pre", "norm_mlp_post", "norm_ple_post",
         "conv_b", "ssd_norm_g", "dt_bias", "a_log", "d_skip"]


def _pad_row(v, width=D):
    return jnp.pad(v, ((0, 0), (0, width - v.shape[1])))


def kernel(x, p, positions, norm_mix_pre, norm_mix_post, w_in, conv_w, conv_b, dt_bias, a_log, d_skip, ssd_norm_g, w_out, norm_mlp_pre, norm_mlp_post, w_up, w_down, w_ple_gate, w_ple_proj, norm_ple_post, loss_target, m_norm_mix_pre, m_norm_mix_post, m_w_in, m_conv_w, m_conv_b, m_dt_bias, m_a_log, m_d_skip, m_ssd_norm_g, m_w_out, m_norm_mlp_pre, m_norm_mlp_post, m_w_up, m_w_down, m_w_ple_gate, m_w_ple_proj, m_norm_ple_post, v_norm_mix_pre, v_norm_mix_post, v_w_in, v_conv_w, v_conv_b, v_dt_bias, v_a_log, v_d_skip, v_ssd_norm_g, v_w_out, v_norm_mlp_pre, v_norm_mlp_post, v_w_up, v_w_down, v_w_ple_gate, v_w_ple_proj, v_norm_ple_post):
    args = dict(locals())
    x2, p2, tgt = x[0], p[0, 0], loss_target[0]
    g1, g2, g3, g4, g5 = norm_mix_pre, norm_mix_post, norm_mlp_pre, norm_mlp_post, norm_ple_post

    me = _slot(*_place())
    pack_in = jnp.pad(w_in[0].T, ((0, W_IN_SHARD_PAD - W_IN_SHARD), (0, 0))).astype(BF16)
    rest = [w_out[0].astype(BF16), w_up[0].T.astype(BF16), w_down[0].astype(BF16), w_ple_gate[0].astype(BF16),
            w_ple_proj[0].T.reshape(32, D).astype(BF16)]
    conv_pack = jnp.pad(conv_w[0], ((0, 4), (0, 32)))
    in_handles, tok_in0 = _send_start_many([pack_in, conv_pack], False, "gather_in_start", g1)

    inv_freq = ROPE_THETA ** (-jnp.arange(HD // 2, dtype=F32) * 2.0 / HD)
    pos = positions[0] + tok_in0[0, 0].astype(jnp.int32)
    ang = pos.astype(F32)[:, None] * inv_freq
    cos, sin = jnp.cos(ang), jnp.sin(ang)
    cos128 = jnp.concatenate([cos, cos, cos, cos], axis=1)
    sin128 = jnp.concatenate([-sin, sin, -sin, sin], axis=1)

    bias_w, alog_w, dsk_w = _pad_row(dt_bias, DT_PAD), _pad_row(a_log, DT_PAD), _pad_row(d_skip, DT_PAD)

    (u1,) = _rowwise(lambda a, g: ((a * _rstd(a) * g,), ()), [x2], [g1], [(D, BF16)], [], tm=512, name="norm_x",
                     deps=[cos128, sin128])
    p2b = p2.astype(BF16)

    in_back, in_land = _send_wait_many(in_handles, u1, "gather_in_wait")
    gin = lax.dynamic_update_slice(in_land[0], in_back[0][None], (me, 0, 0))
    gconv = lax.dynamic_update_slice(in_land[1], in_back[1][None], (me, 0, 0))
    rest_handles, tok_rest = _send_start_many(rest, False, "gather_rest_start", gconv)
    w_inT = gin[:, :W_IN_SHARD].reshape(IN_W, D)
    w_qkvzT = w_inT[:4 * AW]
    w_xbcdtT = jnp.pad(w_inT[4 * AW:], ((0, DT_PAD - HEADS), (0, 0)))
    conv_full = gconv[:, :CONV_K, :96].transpose(1, 0, 2).reshape(CONV_K, CONV_CH)
    qkvz = _mm(u1, w_qkvzT, tb=True, tm=512, tn=2048, tk=1024, name="proj_qkvz", deps=[tok_rest])
    xbcdt = _mm(u1, w_xbcdtT, tb=True, tm=512, tn=896, tk=1024, name="proj_xbcdt")

    qkv = _rope_fwd(qkvz, cos128, sin128)
    qkv = [qkv[3 * i:3 * i + 3] for i in range(len(DILATIONS))]
    outs, lses = [], []
    for d, (qd, kd, vd) in zip(DILATIONS, qkv):
        o, l = _attn_fwd(qd, kd, vd, d)
        outs.append(o)
        lses.append(l)
    attn, lse, attn4, lse4, attn16, lse16 = _attn_merge(outs, lses)

    act = _conv_fwd(xbcdt, conv_full, conv_b)
    y_ssd, states = _ssd_fwd(act, xbcdt, bias_w, alog_w, dsk_w)

    def gated_fwd(y, z, a, gs):
        gi = y * (z * _sigmoid(z))
        return (jnp.concatenate([a, gi * _rstd(gi) * gs], axis=1),), ()
    (cat,) = _rowwise(gated_fwd, [y_ssd, (qkvz, AW, 3), attn], [ssd_norm_g], [(D, BF16)], [], tm=512, name="gated_norm")

    rest_back, landed = _send_wait_many(rest_handles, cat, "gather_rest_wait")
    landed = [lax.dynamic_update_slice(l, b[None], (me, 0, 0)) for l, b in zip(landed, rest_back)]
    w_o, w_upT, w_dn, w_gate = landed[0].reshape(D, D), landed[1].reshape(DFF, D), landed[2].reshape(DFF, D), landed[3].reshape(D, D)
    w_projT = landed[4].reshape(D, PLE)

    def post1(mm, xx, ga):
        h = xx + mm * _rstd(mm) * ga
        return (mm, h, _rstd(h)), ()
    mix, h1, r3 = _mm_rows(post1, [(cat, w_o, False)], [x2], [g2], [(D, F32), (D, F32), (1, F32)], [], tm=512,
                           name="mix_out")

    a_up, ff, u2, h2, h2b = _mlp_fwd(h1, r3, g3, w_upT, w_dn, g4)
    relu2 = lambda a: jnp.square(jnp.maximum(a.astype(F32), 0.0))

    def final(gpre, ppv, hh, tg, g):
        sg = _sigmoid(gpre)
        ple = ppv * sg
        r = _rstd(ple)
        n = ple * r
        h3 = hh + n * g
        e = h3 - tg
        dh3 = e * (1.0 / D)
        dple = _rms_bwd(n, r, g, dh3)
        return (dh3, dple * sg, dple * ppv * sg * (1.0 - sg)), (_colsum(dh3 * n), _colsum(0.5 * e * e * (1.0 / D)))
    dh3, dpp, dgp, dg5, loss_vec = _mm_rows(final, [(h2b, w_gate, False), (p2b, w_projT, True)], [h2, tgt], [g5],
                                            [(D, F32), (D, BF16), (D, BF16)], [(1, D), (1, D)], tm=512, name="ple_loss")

    gw_projT = _mm(dpp, p2b, ta=True, tm=512, tn=256, tk=T, out_dtypes=(BF16,), name="gw_ple_proj")
    gw_gate = _mm(h2b, dgp, ta=True, tm=512, tn=1024, tk=T, out_dtypes=(BF16,), name="gw_ple_gate")
    rs_proj, tok_proj = _send_start(gw_projT.reshape(N_DEV, 32, D), True, "rs_start_w_proj", g1)
    rs_gate, tok_gate = _send_start(gw_gate.reshape(N_DEV, 128, D), True, "rs_start_w_gate", g1)
    def bwd_mlp_post(dg_, d3, f, g):
        dh2 = d3 + dg_
        r = _rstd(f)
        n = f * r
        return (dh2, _rms_bwd(n, r, g, dh2)), (_colsum(dh2 * n),)
    dh2, dff, dg4 = _mm_rows(bwd_mlp_post, [(dgp, w_gate, True)], [dh3, ff], [g4], [(D, F32), (D, BF16)], [(1, D)],
                             tm=512, name="bwd_ple_gate", deps=[tok_proj, tok_gate])

    gw_dn = _mm(a_up, dff, ta=True, tm=512, tn=1024, tk=T, a_pre=relu2, out_dtypes=(BF16,), name="gw_mlp_down")
    rs_dn, tok_dn = _send_start(gw_dn.reshape(N_DEV, 512, D), True, "rs_start_w_down", g1)
    da_up, du2 = _mlp_dx(dff, a_up, w_upT, w_dn, tok_dn)
    gw_upT = _mm(da_up, u2, ta=True, tm=512, tn=1024, tk=T, out_dtypes=(BF16,), name="gw_mlp_up")
    rs_up, tok_up = _send_start(gw_upT.reshape(N_DEV, 512, D), True, "rs_start_w_up", g1)

    def bwd_mix_post(d2, du, hh, rr, mm, ga, gb):
        n3 = hh * rr
        dh1 = d2 + _rms_bwd(n3, rr, gb, du)
        r = _rstd(mm)
        n2 = mm * r
        return (dh1, _rms_bwd(n2, r, ga, dh1)), (_colsum(du * n3), _colsum(dh1 * n2))
    dh1, dmix, dg3, dg2 = _rowwise(bwd_mix_post, [dh2, du2, h1, r3, mix], [g2, g3], [(D, F32), (D, BF16)],
                                   [(1, D), (1, D)], tm=512, name="bwd_post_mix", deps=[tok_up])

    gw_o = _mm(cat, dmix, ta=True, tm=512, tn=1024, tk=T, out_dtypes=(BF16,), name="gw_out")
    rs_o, tok_o = _send_start(gw_o.reshape(N_DEV, 128, D), True, "rs_start_w_out", g1)
    dcat = _mm(dmix, w_o, tb=True, tm=512, tn=1024, tk=1024, name="dx_out", deps=[tok_o])

    def gated_bwd(y, z, dyn, gs):
        sg = _sigmoid(z)
        sz = z * sg
        gi = y * sz
        r = _rstd(gi)
        n = gi * r
        dgi = _rms_bwd(n, r, gs, dyn)
        return (dgi * sz, dgi * y * (sg * (1.0 + z * (1.0 - sg)))), (_colsum(dyn * n),)
    dy_ssd, dz, dgs = _rowwise(gated_bwd, [y_ssd, (qkvz, AW, 3), (dcat, AW, 1)], [ssd_norm_g], [(AW, F32)] * 2, [(1, AW)],
                               tm=512, name="bwd_gated_norm")

    dact, ddtw, ssd_par = _ssd_bwd(act, xbcdt, bias_w, alog_w, dsk_w, states, dy_ssd)
    dxbcdt, conv_par = _conv_bwd(xbcdt, dact, ddtw, conv_full, conv_b)

    dattn4, dattn16 = _dilate_cols(dcat, 0)
    qkv_grads = [_attn_bwd(*qkv[0], dcat, attn, lse, 1),
                 _attn_bwd(*qkv[1], dattn4, attn4, lse4, 4),
                 _attn_bwd(*qkv[2], dattn16, attn16, lse16, 16)]
    dqkvz = _rope_bwd(qkv_grads, dz, cos128, sin128)

    gw_qkvzT = _mm(dqkvz, u1, ta=True, tm=512, tn=1024, tk=T, out_dtypes=(BF16,), name="gw_qkvz")
    gw_xbcdtT = _mm(dxbcdt, u1, ta=True, tm=896, tn=1024, tk=T, out_dtypes=(BF16,), name="gw_xbcdt")
    gw_inT = jnp.concatenate([gw_qkvzT, gw_xbcdtT], axis=0)[:IN_W]
    gw_inT = jnp.pad(gw_inT.reshape(N_DEV, W_IN_SHARD, D), ((0, 0), (0, W_IN_SHARD_PAD - W_IN_SHARD), (0, 0)))
    rs_in, tok_in = _send_start(gw_inT, True, "rs_start_w_in", g1)

    def bwd_in(ua, ub, d1, xx, g):
        rr = _rstd(xx)
        n = xx * rr
        du = ua + ub
        return (d1 + _rms_bwd(n, rr, g, du),), (_colsum(du * n),)
    grad_x, dg1 = _mm_rows(bwd_in, [(dqkvz, w_qkvzT, False), (dxbcdt, w_xbcdtT, False)], [dh1, x2], [g1],
                           [(D, F32)], [(1, D)], tm=512, name="bwd_in_proj", deps=[tok_in])

    slab = _all_reduce_small([(dg1, 0), (dg2, 1), (dg3, 2), (dg4, 3), (dg5, 4), (dgs, 5), (loss_vec, 6),
                              (conv_par, 8), (ssd_par, 16)], "reduce_small")
    g_conv_w = lax.dynamic_slice(slab[8:12, :CONV_CH], (0, me * 96), (CONV_K, 96))

    def scatter_finish(handles, nm, after):
        part, land = _send_wait(handles, after, "rs_wait_" + nm)
        own = lax.dynamic_slice(part, (me, 0, 0), (1,) + part.shape[1:])
        return _sum_slots(lax.dynamic_update_slice(land, own, (me, 0, 0)), "rs_sum_" + nm)
    g_out = scatter_finish(rs_o, "w_out", slab)
    g_upT = scatter_finish(rs_up, "w_up", slab)
    g_dn = scatter_finish(rs_dn, "w_down", slab)
    g_gate = scatter_finish(rs_gate, "w_gate", slab)
    g_projT = scatter_finish(rs_proj, "w_proj", slab)

    small_names = SMALL + ["conv_w"]
    small_rows = [0, 1, 2, 3, 4, 12, 5, 16, 17, 18, None]
    pick = lambda prefix: [args[prefix + nme] for nme in SMALL] + [args[prefix + "conv_w"][0]]
    loss11, g_s, d_s, m_s, v_s = _adamw_small(slab, small_rows, g_conv_w, pick(""), pick("m_"), pick("v_"))
    loss = loss11[0, 0]
    grads = {
        "w_out": g_out[None], "w_up": g_upT.T[None], "w_down": g_dn[None],
        "w_ple_gate": g_gate[None], "w_ple_proj": g_projT.reshape(128, PLE).T[None],
    }
    delta, new_m, new_v = {}, {}, {}
    for i, nme in enumerate(small_names):
        lead = (lambda t: t[None]) if nme == "conv_w" else (lambda t: t)
        grads[nme], delta[nme], new_m[nme], new_v[nme] = lead(g_s[i]), lead(d_s[i]), lead(m_s[i]), lead(v_s[i])
    for nme in ["w_out", "w_up", "w_down", "w_ple_gate", "w_ple_proj", "w_in"]:
        if nme == "w_in":
            g_inT = scatter_finish(rs_in, "w_in", delta["w_down"])
            grads["w_in"] = g_inT[:W_IN_SHARD].T[None]
        dl, mm_, vv_ = _adamw(args[nme][0], grads[nme][0], args["m_" + nme][0], args["v_" + nme][0], "adamw_" + nme)
        delta[nme], new_m[nme], new_v[nme] = dl[None], mm_[None], vv_[None]

    order = ["norm_mix_pre", "norm_mix_post", "w_in", "conv_w", "conv_b", "dt_bias", "a_log", "d_skip", "ssd_norm_g",
             "w_out", "norm_mlp_pre", "norm_mlp_post", "w_up", "w_down", "w_ple_gate", "w_ple_proj", "norm_ple_post"]
    return (loss, grad_x[None], *[grads[n] for n in order], *[delta[n] for n in order],
            *[new_m[n] for n in order], *[new_v[n] for n in order])
```

```python
import functools
import math

import jax
import jax.numpy as jnp
from jax import lax
from jax.experimental import pallas as pl
from jax.experimental.pallas import tpu as pltpu

F32 = jnp.float32
BF16 = jnp.bfloat16
MESH = pl.DeviceIdType.MESH
HIGHEST = lax.Precision.HIGHEST

N_DEV = 8
T = 4096
D = 1024
HEADS = 8
HD = 64
AW = 512
NS = 128
CONV_K = 4
CONV_CH = 768
CHUNK = 128
DFF = 4096
PLE = 256
EPS = 1e-6
ROPE_THETA = 10000.0
DILATIONS = (1, 4, 16)
QBLK = 128
NEG = -1e30
IN_W = 2824
W_IN_SHARD = 353
W_IN_SHARD_PAD = 384
DT_PAD = 128

ADAM_LR, ADAM_B1, ADAM_B2, ADAM_EPS, ADAM_WD, ADAM_STEP = 0.001, 0.9, 0.999, 1e-08, 0.01, 10

VMEM_LIMIT = 56 * 1024 * 1024


_ANY = pl.BlockSpec(memory_space=pl.ANY)


def _cparams(sem=None):
    return pltpu.CompilerParams(dimension_semantics=sem, vmem_limit_bytes=VMEM_LIMIT)


def _dot(a, b, ca, cb, precision=None):
    return lax.dot_general(a, b, (((ca,), (cb,)), ((), ())), preferred_element_type=F32, precision=precision)


def _nn(a, b):
    return _dot(a, b, 1, 0)


def _nt(a, b):
    return _dot(a, b, 1, 1)


def _tn(a, b):
    return _dot(a, b, 0, 0)


def _sigmoid(x):
    return 1.0 / (1.0 + jnp.exp(-x))


def _softplus(x):
    return jnp.maximum(x, 0.0) + jnp.log(1.0 + jnp.exp(-jnp.abs(x)))


def _mm(a, b, *, ta=False, tb=False, tm, tn, tk, name,
        a_pre=None, a_rows=(), a_cols=(), b_pre=None, b_rows=(), b_cols=(),
        epi=None, epi_tiles=(), out_dtypes=(F32,), deps=()):
    if ta:
        K, M = a.shape
    else:
        M, K = a.shape
    if tb:
        N, K2 = b.shape
    else:
        K2, N = b.shape
    assert K == K2 and M % tm == 0 and N % tn == 0 and K % tk == 0, (name, a.shape, b.shape)
    nk = K // tk
    if ta:
        a_spec = pl.BlockSpec((tk, tm), lambda i, j, k: (k, i))
        a_row_specs = [pl.BlockSpec((tk, 1), lambda i, j, k: (k, 0)) for _ in a_rows]
        a_col_specs = [pl.BlockSpec((1, tm), lambda i, j, k: (0, i)) for _ in a_cols]
    else:
        a_spec = pl.BlockSpec((tm, tk), lambda i, j, k: (i, k))
        a_row_specs = [pl.BlockSpec((tm, 1), lambda i, j, k: (i, 0)) for _ in a_rows]
        a_col_specs = [pl.BlockSpec((1, tk), lambda i, j, k: (0, k)) for _ in a_cols]
    if tb:
        b_spec = pl.BlockSpec((tn, tk), lambda i, j, k: (j, k))
        b_row_specs = [pl.BlockSpec((tn, 1), lambda i, j, k: (j, 0)) for _ in b_rows]
        b_col_specs = [pl.BlockSpec((1, tk), lambda i, j, k: (0, k)) for _ in b_cols]
    else:
        b_spec = pl.BlockSpec((tk, tn), lambda i, j, k: (k, j))
        b_row_specs = [pl.BlockSpec((tk, 1), lambda i, j, k: (k, 0)) for _ in b_rows]
        b_col_specs = [pl.BlockSpec((1, tn), lambda i, j, k: (0, j)) for _ in b_cols]
    o_spec = pl.BlockSpec((tm, tn), lambda i, j, k: (i, j))
    na, nb, ne, no = len(a_rows) + len(a_cols), len(b_rows) + len(b_cols), len(epi_tiles), len(out_dtypes)

    def body(*refs):
        a_ref, b_ref = refs[0], refs[1]
        a_ex = refs[2:2 + na]
        b_ex = refs[2 + na:2 + na + nb]
        e_ex = refs[2 + na + nb:2 + na + nb + ne]
        first_out = 2 + na + nb + ne + len(deps)
        outs = refs[first_out:first_out + no]

        def finish(res):
            vals = epi(res, *[r[...] for r in e_ex]) if epi is not None else (res,)
            for o_ref, val in zip(outs, vals):
                o_ref[...] = val.astype(o_ref.dtype)

        at = a_ref[...]
        if a_pre is not None:
            at = a_pre(at, *[r[...] for r in a_ex])
        bt = b_ref[...]
        if b_pre is not None:
            bt = b_pre(bt, *[r[...] for r in b_ex])
        prod = _dot(at.astype(BF16), bt.astype(BF16), 0 if ta else 1, 1 if tb else 0)
        if nk == 1:
            finish(prod)
            return
        acc = refs[-1]
        k = pl.program_id(2)

        @pl.when(k == 0)
        def _():
            acc[...] = jnp.zeros_like(acc)
        acc[...] += prod

        @pl.when(k == nk - 1)
        def _():
            finish(acc[...])

    outs = pl.pallas_call(
        body, name=name,
        grid=(M // tm, N // tn, nk),
        in_specs=([a_spec, b_spec] + a_row_specs + a_col_specs + b_row_specs + b_col_specs + [o_spec] * ne
                  + [_ANY] * len(deps)),
        out_specs=[o_spec] * no,
        out_shape=[jax.ShapeDtypeStruct((M, N), dt) for dt in out_dtypes],
        scratch_shapes=[pltpu.VMEM((tm, tn), F32)] if nk > 1 else [],
        compiler_params=_cparams(("parallel", "parallel", "arbitrary")),
    )(a, b, *a_rows, *a_cols, *b_rows, *b_cols, *epi_tiles, *deps)
    return outs[0] if no == 1 else outs


MLP_TM = 1024
MLP_TC = 512


def _mlp_fwd(h, r, g, w_upT, w_dn, g_post):
    nc = DFF // MLP_TC

    def body(h_ref, r_ref, g_ref, wu_ref, wd_ref, gp_ref, a_ref, ff_ref, u_ref, ho_ref, hob_ref, acc, u_scr):
        c = pl.program_id(1)

        @pl.when(c == 0)
        def _():
            u = (h_ref[...] * r_ref[...] * g_ref[...]).astype(BF16)
            u_scr[...] = u
            u_ref[...] = u
            acc[...] = jnp.zeros_like(acc)
        a = _nt(u_scr[...], wu_ref[...])
        a_ref[...] = a.astype(BF16)
        acc[...] += _nn(jnp.square(jnp.maximum(a, 0.0)).astype(BF16), wd_ref[...])

        @pl.when(c == nc - 1)
        def _():
            f = acc[...]
            ff_ref[...] = f
            ho = h_ref[...] + f * _rstd(f) * gp_ref[...]
            ho_ref[...] = ho
            hob_ref[...] = ho.astype(BF16)

    row = pl.BlockSpec((MLP_TM, D), lambda i, c: (i, 0))
    wsp = pl.BlockSpec((MLP_TC, D), lambda i, c: (c, 0))
    vec = pl.BlockSpec((1, D), lambda i, c: (0, 0))
    return pl.pallas_call(
        body, name="mlp_fwd", grid=(T // MLP_TM, nc),
        in_specs=[row, pl.BlockSpec((MLP_TM, 1), lambda i, c: (i, 0)), vec, wsp, wsp, vec],
        out_specs=[pl.BlockSpec((MLP_TM, MLP_TC), lambda i, c: (i, c)), row, row, row, row],
        out_shape=[jax.ShapeDtypeStruct((T, DFF), BF16), jax.ShapeDtypeStruct((T, D), F32), jax.ShapeDtypeStruct((T, D), BF16),
                   jax.ShapeDtypeStruct((T, D), F32), jax.ShapeDtypeStruct((T, D), BF16)],
        scratch_shapes=[pltpu.VMEM((MLP_TM, D), F32), pltpu.VMEM((MLP_TM, D), BF16)],
        compiler_params=_cparams(("parallel", "arbitrary")),
    )(h, r, g, w_upT, w_dn, g_post)


def _mlp_dx(dff, a, w_upT, w_dn, dep):
    nc = DFF // MLP_TC

    def body(d_ref, a_ref, wu_ref, wd_ref, dep_ref, da_ref, du_ref, acc, d_scr):
        c = pl.program_id(1)

        @pl.when(c == 0)
        def _():
            d_scr[...] = d_ref[...].astype(BF16)
            acc[...] = jnp.zeros_like(acc)
        da = (_nt(d_scr[...], wd_ref[...]) * (2.0 * jnp.maximum(a_ref[...].astype(F32), 0.0))).astype(BF16)
        da_ref[...] = da
        acc[...] += _nn(da, wu_ref[...])

        @pl.when(c == nc - 1)
        def _():
            du_ref[...] = acc[...]

    row = pl.BlockSpec((MLP_TM, D), lambda i, c: (i, 0))
    wsp = pl.BlockSpec((MLP_TC, D), lambda i, c: (c, 0))
    chunk = pl.BlockSpec((MLP_TM, MLP_TC), lambda i, c: (i, c))
    return pl.pallas_call(
        body, name="mlp_dx", grid=(T // MLP_TM, nc),
        in_specs=[row, chunk, wsp, wsp, _ANY], out_specs=[chunk, row],
        out_shape=[jax.ShapeDtypeStruct((T, DFF), BF16), jax.ShapeDtypeStruct((T, D), F32)],
        scratch_shapes=[pltpu.VMEM((MLP_TM, D), F32), pltpu.VMEM((MLP_TM, D), BF16)],
        compiler_params=_cparams(("parallel", "arbitrary")),
    )(dff, a, w_upT, w_dn, dep)


def _rowwise(fn, rows, vecs, out_rows, out_sums, *, tm, name, deps=()):
    specs, arrs = [], []
    R = None
    for r in rows:
        if isinstance(r, tuple):
            arr, width, cb = r
            specs.append(pl.BlockSpec((tm, width), lambda i, cb=cb: (i, cb)))
        else:
            arr = r
            specs.append(pl.BlockSpec((tm, arr.shape[1]), lambda i: (i, 0)))
        R = arr.shape[0] if R is None else R
        assert arr.shape[0] == R, name
        arrs.append(arr)
    assert R % tm == 0, name
    for v in vecs:
        specs.append(pl.BlockSpec(v.shape, lambda i: (0, 0)))
        arrs.append(v)
    nr, nv, no, ns = len(rows), len(vecs), len(out_rows), len(out_sums)
    out_specs = [pl.BlockSpec((tm, w), lambda i: (i, 0)) for w, _ in out_rows]
    out_specs += [pl.BlockSpec(s, lambda i: (0, 0)) for s in out_sums]
    out_shape = [jax.ShapeDtypeStruct((R, w), dt) for w, dt in out_rows]
    out_shape += [jax.ShapeDtypeStruct(s, F32) for s in out_sums]

    nd = len(deps)

    def body(*refs):
        ins = [r[...] for r in refs[:nr + nv]]
        o_refs = refs[nr + nv + nd:nr + nv + nd + no]
        s_refs = refs[nr + nv + nd + no:]
        o_vals, s_vals = fn(*ins)
        for ref, val in zip(o_refs, o_vals):
            ref[...] = val.astype(ref.dtype)
        if ns:
            @pl.when(pl.program_id(0) == 0)
            def _():
                for ref in s_refs:
                    ref[...] = jnp.zeros_like(ref)
            for ref, val in zip(s_refs, s_vals):
                ref[...] += val

    outs = pl.pallas_call(
        body, name=name, grid=(R // tm,), in_specs=specs + [_ANY] * nd, out_specs=out_specs, out_shape=out_shape,
        compiler_params=_cparams(("arbitrary",) if ns else ("parallel",)),
    )(*arrs, *deps)
    return outs


def _mm_rows(fn, mats, rows, vecs, out_rows, out_sums, *, tm, name, deps=()):
    R = mats[0][0].shape[0]
    assert R % tm == 0, name
    specs, arrs = [], []
    for a, b, tb in mats:
        specs += [pl.BlockSpec((tm, a.shape[1]), lambda i: (i, 0)), pl.BlockSpec(b.shape, lambda i: (0, 0))]
        arrs += [a, b]
    for r in rows:
        specs.append(pl.BlockSpec((tm, r.shape[1]), lambda i: (i, 0)))
        arrs.append(r)
    for v in vecs:
        specs.append(pl.BlockSpec(v.shape, lambda i: (0, 0)))
        arrs.append(v)
    nm, nr, nv, nd, no, ns = len(mats), len(rows), len(vecs), len(deps), len(out_rows), len(out_sums)
    out_specs = [pl.BlockSpec((tm, w), lambda i: (i, 0)) for w, _ in out_rows]
    out_specs += [pl.BlockSpec(s, lambda i: (0, 0)) for s in out_sums]
    out_shape = [jax.ShapeDtypeStruct((R, w), dt) for w, dt in out_rows] + [jax.ShapeDtypeStruct(s, F32) for s in out_sums]

    def body(*refs):
        prods = [_dot(refs[2 * p][...].astype(BF16), refs[2 * p + 1][...].astype(BF16), 1, 1 if mats[p][2] else 0)
                 for p in range(nm)]
        ins = [r[...] for r in refs[2 * nm:2 * nm + nr + nv]]
        first_out = 2 * nm + nr + nv + nd
        o_refs, s_refs = refs[first_out:first_out + no], refs[first_out + no:]
        o_vals, s_vals = fn(*prods, *ins)
        for ref, val in zip(o_refs, o_vals):
            ref[...] = val.astype(ref.dtype)
        if ns:
            @pl.when(pl.program_id(0) == 0)
            def _():
                for ref in s_refs:
                    ref[...] = jnp.zeros_like(ref)
            for ref, val in zip(s_refs, s_vals):
                ref[...] += val

    return pl.pallas_call(
        body, name=name, grid=(R // tm,), in_specs=specs + [_ANY] * nd, out_specs=out_specs, out_shape=out_shape,
        compiler_params=_cparams(("arbitrary",) if ns else ("parallel",)),
    )(*arrs, *deps)


def _colsum(x):
    return jnp.sum(x, axis=0, keepdims=True)


def _rstd(x):
    return lax.rsqrt(jnp.mean(x * x, axis=-1, keepdims=True) + EPS)


def _rms_bwd(xn, r, g, dy):
    dn = dy * g
    return r * (dn - xn * jnp.mean(dn * xn, axis=-1, keepdims=True))


def _partner(t):
    lane = lax.broadcasted_iota(jnp.int32, t.shape, 1)
    up = pltpu.roll(t, 96, 1)
    down = pltpu.roll(t, 32, 1)
    return jnp.where((lane % 64) < 32, up, down)


SLABS = AW // 128


def _rows(r, n, d):
    return pl.ds(r, n, stride=d) if d > 1 else pl.ds(0, n)


def _undilate(src_ref, dst, d, tm):
    for r in range(d):
        for j in range(SLABS):
            dst[j][_rows(r, tm // d, d), :] = src_ref[:, pl.ds(r * AW + j * 128, 128)].astype(dst[j].dtype)


def _dilate(dst_ref, src, d, tm):
    for r in range(d):
        for j in range(SLABS):
            dst_ref[:, pl.ds(r * AW + j * 128, 128)] = src[j][_rows(r, tm // d, d), :].astype(dst_ref.dtype)


def _slab_scratch(n, tm):
    return [pltpu.VMEM((tm, 128), F32)] * (SLABS * n)


def _slab_groups(flat):
    return [flat[SLABS * i:SLABS * (i + 1)] for i in range(len(flat) // SLABS)]


def _slab_specs(tm, first):
    return [pl.BlockSpec((tm, 128), lambda i, j=j: (i, first + j)) for j in range(SLABS)]


def _dil_spec(tm, d):
    return pl.BlockSpec((tm // d, d * AW), lambda i: (i, 0))


ROPE_TM = 512


def _rope_fwd(qkvz, cos128, sin128):
    tm = ROPE_TM

    def body(*refs):
        q_refs, k_refs, v_refs = refs[0:4], refs[4:8], refs[8:12]
        c_ref, s_ref = refs[12], refs[13]
        outs = refs[14:23]
        qs, ks = _slab_groups(refs[23:])
        c, s = c_ref[...], s_ref[...]
        for j in range(SLABS):
            q, k = q_refs[j][...], k_refs[j][...]
            qs[j][...] = (q * c + _partner(q) * s) * (HD ** -0.5)
            ks[j][...] = k * c + _partner(k) * s
        for di, d in enumerate(DILATIONS):
            oq, ok, ov = outs[3 * di:3 * di + 3]
            for r in range(d):
                rows = _rows(r, tm // d, d)
                for j in range(SLABS):
                    cols = pl.ds(r * AW + j * 128, 128)
                    oq[:, cols] = qs[j][rows, :].astype(BF16)
                    ok[:, cols] = ks[j][rows, :].astype(BF16)
                    ov[:, cols] = v_refs[j][rows, :].astype(BF16)

    tab = pl.BlockSpec((tm, 128), lambda i: (i, 0))
    out_specs, out_shape = [], []
    for d in DILATIONS:
        out_specs += [_dil_spec(tm, d)] * 3
        out_shape += [jax.ShapeDtypeStruct((T // d, d * AW), BF16)] * 3
    return pl.pallas_call(
        body, name="rope_fwd", grid=(T // tm,),
        in_specs=_slab_specs(tm, 0) + _slab_specs(tm, 4) + _slab_specs(tm, 8) + [tab, tab],
        out_specs=out_specs, out_shape=out_shape, scratch_shapes=_slab_scratch(2, tm),
        compiler_params=_cparams(("parallel",)),
    )(*([qkvz] * 12), cos128, sin128)


def _rope_bwd(grads, dz, cos128, sin128):
    tm = 256

    def body(*refs):
        g_refs = refs[0:9]
        dz_ref, c_ref, s_ref, o_ref = refs[9], refs[10], refs[11], refs[12]
        scr = _slab_groups(refs[13:])
        for di, d in enumerate(DILATIONS[1:]):
            for t in range(3):
                _undilate(g_refs[3 * (di + 1) + t], scr[3 * di + t], d, tm)
        c, s = c_ref[...], s_ref[...]
        for j in range(SLABS):
            cols = pl.ds(j * 128, 128)
            tot = [g_refs[t][:, cols] + scr[t][j][...] + scr[3 + t][j][...] for t in range(3)]
            dqr = tot[0] * (HD ** -0.5)
            o_ref[:, pl.ds(j * 128, 128)] = (dqr * c + _partner(dqr * s)).astype(BF16)
            o_ref[:, pl.ds(AW + j * 128, 128)] = (tot[1] * c + _partner(tot[1] * s)).astype(BF16)
            o_ref[:, pl.ds(2 * AW + j * 128, 128)] = tot[2].astype(BF16)
        o_ref[:, pl.ds(3 * AW, AW)] = dz_ref[...].astype(BF16)

    tab = pl.BlockSpec((tm, 128), lambda i: (i, 0))
    in_specs, args = [], []
    for d, g in zip(DILATIONS, grads):
        in_specs += [_dil_spec(tm, d)] * 3
        args += list(g)
    return pl.pallas_call(
        body, name="rope_bwd", grid=(T // tm,),
        in_specs=in_specs + [pl.BlockSpec((tm, AW), lambda i: (i, 0)), tab, tab],
        out_specs=pl.BlockSpec((tm, 4 * AW), lambda i: (i, 0)),
        out_shape=jax.ShapeDtypeStruct((T, 4 * AW), BF16),
        scratch_shapes=_slab_scratch(6, tm),
        compiler_params=_cparams(("parallel",)),
    )(*args, dz, cos128, sin128)


def _dilate_cols(x, first):
    tm = ROPE_TM

    def body(x0, x1, x2, x3, o4, o16):
        xs = (x0, x1, x2, x3)
        for o_ref, d in ((o4, 4), (o16, 16)):
            for r in range(d):
                for j in range(SLABS):
                    o_ref[:, pl.ds(r * AW + j * 128, 128)] = xs[j][_rows(r, tm // d, d), :]

    return pl.pallas_call(
        body, name="dilate_cols", grid=(T // tm,),
        in_specs=_slab_specs(tm, first), out_specs=[_dil_spec(tm, 4), _dil_spec(tm, 16)],
        out_shape=[jax.ShapeDtypeStruct((T // 4, 4 * AW), F32), jax.ShapeDtypeStruct((T // 16, 16 * AW), F32)],
        compiler_params=_cparams(("parallel",)),
    )(x, x, x, x)


def _band_masks():
    qi = lax.broadcasted_iota(jnp.int32, (QBLK, QBLK), 0)
    kj = lax.broadcasted_iota(jnp.int32, (QBLK, QBLK), 1)
    return kj >= qi, kj <= qi


def _attn_fwd(q, k, v, d):
    L = q.shape[0]
    npair = L // (2 * QBLK)

    def body(q_ref, kp_ref, kc_ref, vp_ref, vc_ref, o_ref, l_ref):
        pair = pl.program_id(1)
        mask_p, mask_c = _band_masks()
        for sub in range(2):
            rows = pl.ds(sub * QBLK, QBLK)
            first = jnp.where(pair > 0, 0.0, NEG) if sub == 0 else 0.0
            bias = jnp.concatenate([jnp.where(mask_p, 0.0, NEG) + first, jnp.where(mask_c, 0.0, NEG)], axis=1)
            k_prev = (lambda sl: kp_ref[:, sl]) if sub == 0 else (lambda sl: kc_ref[pl.ds(0, QBLK), sl])
            v_prev = (lambda sl: vp_ref[:, sl]) if sub == 0 else (lambda sl: vc_ref[pl.ds(0, QBLK), sl])
            s = []
            for h in range(HEADS):
                sl = pl.ds(HD * h, HD)
                qh = q_ref[rows, sl]
                s.append(jnp.concatenate([_nt(qh, k_prev(sl)), _nt(qh, kc_ref[rows, sl])], axis=1))
            s = jnp.stack(s) + bias
            m = jnp.max(s, axis=2, keepdims=True)
            e = jnp.exp(s - m)
            den = jnp.sum(e, axis=2, keepdims=True)
            p = e.astype(BF16)
            inv = 1.0 / den
            lse = m + jnp.log(den)
            for h in range(HEADS):
                sl = pl.ds(HD * h, HD)
                o_ref[rows, sl] = (_nn(p[h, :, :QBLK], v_prev(sl)) + _nn(p[h, :, QBLK:], vc_ref[rows, sl])) * inv[h]
                l_ref[rows, sl] = jnp.broadcast_to(lse[h], (QBLK, HD))

    cur = pl.BlockSpec((2 * QBLK, AW), lambda r, n: (n, r))
    prev = pl.BlockSpec((QBLK, AW), lambda r, n: (jnp.maximum(2 * n - 1, 0), r))
    return pl.pallas_call(
        body, name=f"attn_fwd_d{d}", grid=(d, npair),
        in_specs=[cur, prev, cur, prev, cur], out_specs=[cur, cur],
        out_shape=[jax.ShapeDtypeStruct((L, d * AW), F32)] * 2,
        compiler_params=_cparams(("parallel", "parallel")),
    )(q, k, k, v, v)


def _attn_bwd(q, k, v, do, at, lse, d):
    L = q.shape[0]
    nb = L // QBLK

    def body(q0_ref, q1_ref, kp_ref, kc_ref, vp_ref, vc_ref, do0_ref, do1_ref, at0_ref, at1_ref,
             l0_ref, l1_ref, dq_ref, dk_ref, dv_ref):
        n = pl.program_id(1)
        mask_p, mask_c = _band_masks()
        prev_bias = jnp.where(mask_p, 0.0, NEG)
        bias = jnp.concatenate([prev_bias + jnp.where(n > 0, 0.0, NEG), jnp.where(mask_c, 0.0, NEG),
                                prev_bias + jnp.where(n < nb - 1, 0.0, NEG)], axis=1)
        s, dp, ls, dl, ops = [], [], [], [], []
        for h in range(HEADS):
            sl = pl.ds(HD * h, HD)
            one = pl.ds(HD * h, 1)
            q0, q1 = q0_ref[:, sl], q1_ref[:, sl]
            kp, kc, vp, vc = kp_ref[:, sl], kc_ref[:, sl], vp_ref[:, sl], vc_ref[:, sl]
            do0, do1 = do0_ref[:, sl], do1_ref[:, sl]
            do0b, do1b = do0.astype(BF16), do1.astype(BF16)
            s.append(jnp.concatenate([_nt(q0, kp), _nt(q0, kc), _nt(q1, kc)], axis=1))
            dp.append(jnp.concatenate([_nt(do0b, vp), _nt(do0b, vc), _nt(do1b, vc)], axis=1))
            dl0 = jnp.sum(do0 * at0_ref[:, sl], axis=1, keepdims=True)
            dl1 = jnp.sum(do1 * at1_ref[:, sl], axis=1, keepdims=True)
            dl.append(jnp.concatenate([jnp.broadcast_to(dl0, (QBLK, 2 * QBLK)), jnp.broadcast_to(dl1, (QBLK, QBLK))], axis=1))
            ls.append(jnp.concatenate([jnp.broadcast_to(l0_ref[:, one], (QBLK, 2 * QBLK)),
                                       jnp.broadcast_to(l1_ref[:, one], (QBLK, QBLK))], axis=1))
            ops.append((q0, q1, kp, kc, do0b, do1b))
        p = jnp.exp(jnp.stack(s) + bias - jnp.stack(ls))
        ds = (p * (jnp.stack(dp) - jnp.stack(dl))).astype(BF16)
        p = p.astype(BF16)
        for h in range(HEADS):
            sl = pl.ds(HD * h, HD)
            q0, q1, kp, kc, do0b, do1b = ops[h]
            dq_ref[:, sl] = (_nn(ds[h, :, :QBLK], kp) + _nn(ds[h, :, QBLK:2 * QBLK], kc)).astype(BF16)
            dv_ref[:, sl] = (_tn(p[h, :, QBLK:2 * QBLK], do0b) + _tn(p[h, :, 2 * QBLK:], do1b)).astype(BF16)
            dk_ref[:, sl] = (_tn(ds[h, :, QBLK:2 * QBLK], q0) + _tn(ds[h, :, 2 * QBLK:], q1)).astype(BF16)

    cur = pl.BlockSpec((QBLK, AW), lambda r, n: (n, r))
    prev = pl.BlockSpec((QBLK, AW), lambda r, n: (jnp.maximum(n - 1, 0), r))
    nxt = pl.BlockSpec((QBLK, AW), lambda r, n: (jnp.minimum(n + 1, nb - 1), r))
    return pl.pallas_call(
        body, name=f"attn_bwd_d{d}", grid=(d, nb),
        in_specs=[cur, nxt, prev, cur, prev, cur, cur, nxt, cur, nxt, cur, nxt], out_specs=[cur, cur, cur],
        out_shape=[jax.ShapeDtypeStruct((L, d * AW), BF16)] * 3,
        compiler_params=_cparams(("parallel", "parallel")),
    )(q, q, k, k, v, v, do, do, at, at, lse, lse)


def _attn_merge(outs, lses):
    tm = ROPE_TM

    def body(o1, o4, o16, l1, l4, l16, at_ref, ls_ref, at4, ls4, at16, ls16, *flat):
        so4, so16, sl4, sl16, sa, sl = _slab_groups(flat)
        _undilate(o4, so4, 4, tm)
        _undilate(o16, so16, 16, tm)
        _undilate(l4, sl4, 4, tm)
        _undilate(l16, sl16, 16, tm)
        for j in range(SLABS):
            cols = pl.ds(j * 128, 128)
            a, b, c = l1[:, cols], sl4[j][...], sl16[j][...]
            m = jnp.maximum(jnp.maximum(a, b), c)
            e1, e2, e3 = jnp.exp(a - m), jnp.exp(b - m), jnp.exp(c - m)
            s = e1 + e2 + e3
            inv = 1.0 / s
            attn = (e1 * inv) * o1[:, cols] + (e2 * inv) * so4[j][...] + (e3 * inv) * so16[j][...]
            lse = m + jnp.log(s)
            at_ref[:, cols] = attn
            ls_ref[:, cols] = lse
            sa[j][...] = attn
            sl[j][...] = lse
        _dilate(at4, sa, 4, tm)
        _dilate(at16, sa, 16, tm)
        _dilate(ls4, sl, 4, tm)
        _dilate(ls16, sl, 16, tm)

    specs = [_dil_spec(tm, d) for d in DILATIONS]
    tok = specs[0]
    return pl.pallas_call(
        body, name="attn_merge", grid=(T // tm,),
        in_specs=specs + specs, out_specs=[tok, tok, specs[1], specs[1], specs[2], specs[2]],
        out_shape=[jax.ShapeDtypeStruct((T, AW), F32)] * 2 + [jax.ShapeDtypeStruct((T // 4, 4 * AW), F32)] * 2
        + [jax.ShapeDtypeStruct((T // 16, 16 * AW), F32)] * 2,
        scratch_shapes=_slab_scratch(6, tm),
        compiler_params=_cparams(("parallel",)),
    )(*outs, *lses)


CONV_TM = 512
HALO = 8


def _conv_pre(ext, w, b):
    y = b + w[3] * ext
    for kk in range(1, CONV_K):
        y = y + w[3 - kk] * pltpu.roll(ext, kk, 0)
    return y


def _rows_to_block(rows, n, width):
    ri = lax.broadcasted_iota(jnp.int32, (n, width), 0)
    out = jnp.zeros((n, width), F32)
    for j, r in enumerate(rows):
        out = out + jnp.where(ri == j, r, 0.0)
    return out


def _conv_fwd(xbc, w, b):
    nblk = T // CONV_TM

    def body(x_ref, h_ref, w_ref, b_ref, o_ref):
        i = pl.program_id(0)
        halo = jnp.where(i > 0, h_ref[...], 0.0)
        ext = jnp.concatenate([halo, x_ref[...]], axis=0)
        y = _conv_pre(ext, [w_ref[pl.ds(j, 1), :] for j in range(CONV_K)], b_ref[...])[HALO:]
        o_ref[...] = y * _sigmoid(y)

    return pl.pallas_call(
        body, name="conv_fwd", grid=(nblk,),
        in_specs=[pl.BlockSpec((CONV_TM, CONV_CH), lambda i: (i, 0)),
                  pl.BlockSpec((HALO, CONV_CH), lambda i: (jnp.maximum(i * (CONV_TM // HALO) - 1, 0), 0)),
                  pl.BlockSpec((CONV_K, CONV_CH), lambda i: (0, 0)),
                  pl.BlockSpec((1, CONV_CH), lambda i: (0, 0))],
        out_specs=pl.BlockSpec((CONV_TM, CONV_CH), lambda i: (i, 0)),
        out_shape=jax.ShapeDtypeStruct((T, CONV_CH), F32),
        compiler_params=_cparams(("parallel",)),
    )(xbc, xbc, w, b)


def _conv_bwd(xbc, dact, ddt, w, b):
    nblk = T // CONV_TM
    per = CONV_TM // HALO

    def body(x_ref, xb_ref, xa_ref, g_ref, ga_ref, ddt_ref, w_ref, b_ref, dx_ref, dw_ref):
        i = pl.program_id(0)
        wv = [w_ref[pl.ds(j, 1), :] for j in range(CONV_K)]
        before = jnp.where(i > 0, xb_ref[...], 0.0)
        last = i == nblk - 1
        after = jnp.where(last, 0.0, xa_ref[...])
        g_after = jnp.where(last, 0.0, ga_ref[...])
        ext = jnp.concatenate([before, x_ref[...], after], axis=0)
        y = _conv_pre(ext, wv, b_ref[...])[HALO:]
        sg = _sigmoid(y)
        dy = jnp.concatenate([g_ref[...], g_after], axis=0) * (sg * (1.0 + y * (1.0 - sg)))
        n = CONV_TM + HALO
        dx = wv[3] * dy
        for kk in range(1, CONV_K):
            dx = dx + wv[3 - kk] * pltpu.roll(dy, n - kk, 0)
        dx_ref[:, pl.ds(0, CONV_CH)] = dx[:CONV_TM].astype(BF16)
        dx_ref[:, pl.ds(CONV_CH, DT_PAD)] = ddt_ref[...].astype(BF16)
        dyc = dy[:CONV_TM]
        rows = [jnp.sum(dyc * (pltpu.roll(ext, 3 - j, 0) if j < 3 else ext)[HALO:HALO + CONV_TM], axis=0, keepdims=True)
                for j in range(CONV_K)]
        rows.append(jnp.sum(dyc, axis=0, keepdims=True))
        part = _rows_to_block(rows, 8, CONV_CH)

        @pl.when(i == 0)
        def _():
            dw_ref[...] = jnp.zeros_like(dw_ref)
        dw_ref[...] += part

    blk = pl.BlockSpec((CONV_TM, CONV_CH), lambda i: (i, 0))
    hb = pl.BlockSpec((HALO, CONV_CH), lambda i: (jnp.maximum(i * per - 1, 0), 0))
    ha = pl.BlockSpec((HALO, CONV_CH), lambda i: (jnp.minimum((i + 1) * per, T // HALO - 1), 0))
    return pl.pallas_call(
        body, name="conv_bwd", grid=(nblk,),
        in_specs=[blk, hb, ha, blk, ha, pl.BlockSpec((CONV_TM, DT_PAD), lambda i: (i, 0)),
                  pl.BlockSpec((CONV_K, CONV_CH), lambda i: (0, 0)), pl.BlockSpec((1, CONV_CH), lambda i: (0, 0))],
        out_specs=[pl.BlockSpec((CONV_TM, CONV_CH + DT_PAD), lambda i: (i, 0)), pl.BlockSpec((8, CONV_CH), lambda i: (0, 0))],
        out_shape=[jax.ShapeDtypeStruct((T, CONV_CH + DT_PAD), BF16), jax.ShapeDtypeStruct((8, CONV_CH), F32)],
        compiler_params=_cparams(("arbitrary",)),
    )(xbc, xbc, xbc, dact, dact, ddt, w, b)


def _pick(mat, h):
    lane = lax.broadcasted_iota(jnp.int32, mat.shape, 1)
    return jnp.sum(jnp.where(lane == h, mat, 0.0), axis=1, keepdims=True)


def _heads(fn):
    return jnp.stack([fn(h) for h in range(HEADS)])


def _ssd_prep(dt_ref, bias_ref, alog_ref, dsk_ref, b_ref, c_ref, xs_ref, state_ref, cst):
    li = lax.broadcasted_iota(jnp.int32, (CHUNK, CHUNK), 0)
    si = lax.broadcasted_iota(jnp.int32, (CHUNK, CHUNK), 1)
    tri = li >= si
    dtp = dt_ref[...] + bias_ref[...]
    dt = _softplus(dtp)
    A = -jnp.exp(alog_ref[...])
    a = dt * A
    cs = jnp.dot(tri.astype(F32), a, precision=HIGHEST, preferred_element_type=F32)
    cst[...] = cs.T
    Bm = b_ref[...].astype(BF16)
    Cm = c_ref[...].astype(BF16)
    cb = _nt(Cm, Bm)
    dskv = dsk_ref[...]
    cs_col = _heads(lambda h: _pick(cs, h))
    cs_row = _heads(lambda h: cst[pl.ds(h, 1), :])
    dt_col = _heads(lambda h: _pick(dt, h))
    dsk_col = _heads(lambda h: _pick(dskv, h))
    lam = jnp.exp(jnp.where(tri, cs_col - cs_row, NEG))
    x = _heads(lambda h: xs_ref[:, pl.ds(HD * h, HD)])
    xdt = x * dt_col
    prev = _heads(lambda h: state_ref[pl.ds(HD * h, HD), :])
    lane = lax.broadcasted_iota(jnp.int32, (1, 1, CHUNK), 2)
    cl = jnp.sum(jnp.where(lane == CHUNK - 1, cs_row, 0.0), axis=2, keepdims=True)
    f = jnp.exp(cl - cs_col)
    return dict(li=li, si=si, dtp=dtp, dt=dt, A=A, Bm=Bm, Cm=Cm, cb=cb, cs_col=cs_col, dt_col=dt_col, dsk_col=dsk_col,
                lam=lam, x=x, xdt=xdt, prev=prev, cl=cl, f=f)


def _ssd_fwd(act, xbcdt, bias, alog, dsk, qkvz, attn, gs):
    nc = T // CHUNK

    def body(xs_ref, b_ref, c_ref, dt_ref, bias_ref, alog_ref, dsk_ref, z_ref, at_ref, gs_ref,
             y_ref, st_ref, cat_ref, state, cst):
        @pl.when(pl.program_id(0) == 0)
        def _():
            state[...] = jnp.zeros_like(state)
        st_ref[...] = state[...]
        s = _ssd_prep(dt_ref, bias_ref, alog_ref, dsk_ref, b_ref, c_ref, xs_ref, state, cst)
        Bm, Cm, prev = s["Bm"], s["Cm"], s["prev"]
        g = (s["cb"] * s["lam"]).astype(BF16)
        xdtb = s["xdt"].astype(BF16)
        prevb = prev.astype(BF16)
        y = _heads(lambda h: _nn(g[h], xdtb[h])) + _heads(lambda h: _nt(Cm, prevb[h])) * jnp.exp(s["cs_col"])
        y = y + s["dsk_col"] * s["x"]
        xf = (s["xdt"] * s["f"]).astype(BF16)
        new = prev * jnp.exp(s["cl"]) + _heads(lambda h: _tn(xf[h], Bm))
        for h in range(HEADS):
            y_ref[:, pl.ds(HD * h, HD)] = y[h]
            state[pl.ds(HD * h, HD), :] = new[h]
        z = z_ref[...]
        gi = y_ref[...] * (z * _sigmoid(z))
        cat_ref[:, pl.ds(0, AW)] = at_ref[...].astype(BF16)
        cat_ref[:, pl.ds(AW, AW)] = (gi * _rstd(gi) * gs_ref[...]).astype(BF16)

    vec = pl.BlockSpec((1, DT_PAD), lambda c: (0, 0))
    blk = pl.BlockSpec((CHUNK, AW), lambda c: (c, 0))
    return pl.pallas_call(
        body, name="ssd_fwd", grid=(nc,),
        in_specs=[blk, pl.BlockSpec((CHUNK, NS), lambda c: (c, 4)),
                  pl.BlockSpec((CHUNK, NS), lambda c: (c, 5)), pl.BlockSpec((CHUNK, DT_PAD), lambda c: (c, 6)),
                  vec, vec, vec, pl.BlockSpec((CHUNK, AW), lambda c: (c, 3)), blk, pl.BlockSpec((1, AW), lambda c: (0, 0))],
        out_specs=[blk, pl.BlockSpec((None, AW, NS), lambda c: (c, 0, 0)), pl.BlockSpec((CHUNK, D), lambda c: (c, 0))],
        out_shape=[jax.ShapeDtypeStruct((T, AW), F32), jax.ShapeDtypeStruct((nc, AW, NS), F32),
                   jax.ShapeDtypeStruct((T, D), BF16)],
        scratch_shapes=[pltpu.VMEM((AW, NS), F32), pltpu.VMEM((CHUNK, CHUNK), F32)],
        compiler_params=_cparams(("arbitrary",)),
    )(act, act, act, xbcdt, bias, alog, dsk, qkvz, attn, gs)


def _ssd_bwd(act, xbcdt, bias, alog, dsk, states, y_ssd, qkvz, dcat, gs):
    nc = T // CHUNK

    def body(xs_ref, b_ref, c_ref, dt_ref, bias_ref, alog_ref, dsk_ref, st_ref, y_ref, z_ref, dyn_ref, gs_ref,
             dact_ref, ddt_ref, par_ref, dz_ref, dgs_ref, dstate, cst, dy_ref):
        step = pl.program_id(0)

        @pl.when(step == 0)
        def _():
            dstate[...] = jnp.zeros_like(dstate)
            par_ref[...] = jnp.zeros_like(par_ref)
            dgs_ref[...] = jnp.zeros_like(dgs_ref)
        z, yv, dyn = z_ref[...], y_ref[...], dyn_ref[...]
        sg = _sigmoid(z)
        sz = z * sg
        gi = yv * sz
        rg = _rstd(gi)
        ng = gi * rg
        dgi = _rms_bwd(ng, rg, gs_ref[...], dyn)
        dy_ref[...] = dgi * sz
        dz_ref[...] = dgi * yv * (sg * (1.0 + z * (1.0 - sg)))
        dgs_ref[...] += _colsum(dyn * ng)
        s = _ssd_prep(dt_ref, bias_ref, alog_ref, dsk_ref, b_ref, c_ref, xs_ref, st_ref, cst)
        Bm, Cm, prev, lam, x, xdt, f, cl = s["Bm"], s["Cm"], s["prev"], s["lam"], s["x"], s["xdt"], s["f"], s["cl"]
        lane = lax.broadcasted_iota(jnp.int32, (1, DT_PAD), 1)
        row = lax.broadcasted_iota(jnp.int32, (1, CHUNK, 1), 1)
        g = s["cb"] * lam
        gb, xdtb, prevb = g.astype(BF16), xdt.astype(BF16), prev.astype(BF16)
        dy = _heads(lambda h: dy_ref[:, pl.ds(HD * h, HD)])
        dyb = dy.astype(BF16)
        dnew = _heads(lambda h: dstate[pl.ds(HD * h, HD), :])
        dnewb = dnew.astype(BF16)
        E = jnp.exp(s["cs_col"])
        ecl = jnp.exp(cl)
        dG = _heads(lambda h: _nt(dyb[h], xdtb[h]))
        dxdt = _heads(lambda h: _tn(gb[h], dyb[h]))
        Yo = _heads(lambda h: _nt(Cm, prevb[h]))
        W = _heads(lambda h: _nt(Bm, dnewb[h]))
        dcb = jnp.sum(dG * lam, axis=0)
        Mm = dG * g
        col_sums = jnp.sum(Mm, axis=1, keepdims=True)
        dYo = (dy * E).astype(BF16)
        dxdt = dxdt + W * f
        dF = jnp.sum(W * xdt, axis=2, keepdims=True) * f
        dcl = jnp.sum(dnew * prev, axis=(1, 2), keepdims=True) * ecl + jnp.sum(dF, axis=1, keepdims=True)
        dcs = (jnp.sum(Mm, axis=2, keepdims=True) + jnp.sum(dy * Yo, axis=2, keepdims=True) * E - dF
               + jnp.where(row == CHUNK - 1, dcl, 0.0))
        ddt_x = jnp.sum(dxdt * x, axis=2, keepdims=True)
        dD = jnp.sum(dy * x, axis=(1, 2), keepdims=True)
        dx = s["dsk_col"] * dy + dxdt * s["dt_col"]
        xfb = (xdt * f).astype(BF16)
        dprev = _heads(lambda h: _tn(dYo[h], Cm)) + dnew * ecl
        dcbb = dcb.astype(BF16)
        dC = _nn(dcbb, Bm)
        dB = _tn(dcbb, Cm)
        dcs_mat = -_rows_to_block([col_sums[h] for h in range(HEADS)], CHUNK, CHUNK).T
        ddt_mat = jnp.zeros((CHUNK, DT_PAD), F32)
        dD_row = jnp.zeros((1, DT_PAD), F32)
        for h in range(HEADS):
            sl = pl.ds(HD * h, HD)
            dC = dC + _nn(dYo[h], prevb[h])
            dB = dB + _nn(xfb[h], dnewb[h])
            dcs_mat = dcs_mat + jnp.where(lane == h, dcs[h], 0.0)
            ddt_mat = ddt_mat + jnp.where(lane == h, ddt_x[h], 0.0)
            dD_row = dD_row + jnp.where(lane == h, dD[h], 0.0)
            dact_ref[:, sl] = dx[h]
            dstate[sl, :] = dprev[h]
        dact_ref[:, pl.ds(AW, NS)] = dB
        dact_ref[:, pl.ds(AW + NS, NS)] = dC
        da = jnp.dot((s["li"] <= s["si"]).astype(F32), dcs_mat, precision=HIGHEST, preferred_element_type=F32)
        ddtp = jnp.where(lane < HEADS, (ddt_mat + da * s["A"]) * _sigmoid(s["dtp"]), 0.0)
        ddt_ref[...] = ddtp
        dalog = jnp.where(lane < HEADS, jnp.sum(da * s["dt"], axis=0, keepdims=True) * s["A"], 0.0)
        par_ref[...] += _rows_to_block([jnp.sum(ddtp, axis=0, keepdims=True), dalog, dD_row], 8, DT_PAD)

    vec = pl.BlockSpec((1, DT_PAD), lambda c: (0, 0))
    rev = lambda c: nc - 1 - c
    return pl.pallas_call(
        body, name="ssd_bwd", grid=(nc,),
        in_specs=[pl.BlockSpec((CHUNK, AW), lambda c: (rev(c), 0)), pl.BlockSpec((CHUNK, NS), lambda c: (rev(c), 4)),
                  pl.BlockSpec((CHUNK, NS), lambda c: (rev(c), 5)), pl.BlockSpec((CHUNK, DT_PAD), lambda c: (rev(c), 6)),
                  vec, vec, vec,
                  pl.BlockSpec((None, AW, NS), lambda c: (rev(c), 0, 0)), pl.BlockSpec((CHUNK, AW), lambda c: (rev(c), 0)),
                  pl.BlockSpec((CHUNK, AW), lambda c: (rev(c), 3)), pl.BlockSpec((CHUNK, AW), lambda c: (rev(c), 1)),
                  pl.BlockSpec((1, AW), lambda c: (0, 0))],
        out_specs=[pl.BlockSpec((CHUNK, CONV_CH), lambda c: (rev(c), 0)), pl.BlockSpec((CHUNK, DT_PAD), lambda c: (rev(c), 0)),
                   pl.BlockSpec((8, DT_PAD), lambda c: (0, 0)), pl.BlockSpec((CHUNK, AW), lambda c: (rev(c), 0)),
                   pl.BlockSpec((1, AW), lambda c: (0, 0))],
        out_shape=[jax.ShapeDtypeStruct((T, CONV_CH), F32), jax.ShapeDtypeStruct((T, DT_PAD), F32),
                   jax.ShapeDtypeStruct((8, DT_PAD), F32), jax.ShapeDtypeStruct((T, AW), F32),
                   jax.ShapeDtypeStruct((1, AW), F32)],
        scratch_shapes=[pltpu.VMEM((AW, NS), F32), pltpu.VMEM((CHUNK, CHUNK), F32), pltpu.VMEM((CHUNK, AW), F32)],
        compiler_params=_cparams(("arbitrary",)),
    )(act, act, act, xbcdt, bias, alog, dsk, states, y_ssd, qkvz, dcat, gs)


def _place():
    return lax.axis_index("x"), lax.axis_index("y"), lax.axis_index("c")


def _slot(px, py, pc):
    return 4 * px + 2 * py + pc


SLAB_ROWS = 24


def _all_reduce_small(parts, name):
    R, C = SLAB_ROWS, D
    n = len(parts)

    def body(*refs):
        in_refs = refs[:n]
        out_ref, slab, got, send_sems, recv_sems = refs[n:]
        slab[...] = jnp.zeros_like(slab)
        for ref, (arr, row) in zip(in_refs, parts):
            slab[pl.ds(row, arr.shape[0]), pl.ds(0, arr.shape[1])] = ref[...]
        x, y, c = _place()
        mine = _slot(x, y, c)
        copies = [pltpu.make_async_remote_copy(
            src_ref=slab, dst_ref=got.at[mine], send_sem=send_sems.at[kk], recv_sem=recv_sems.at[kk],
            device_id=peer, device_id_type=MESH) for kk, peer in enumerate(_peers(x, y, c))]
        for cp in copies:
            cp.start()
        got[mine] = slab[...]
        for cp in copies:
            cp.wait_recv()
        acc = got[0]
        for s in range(1, N_DEV):
            acc = acc + got[s]
        out_ref[...] = acc
        for cp in copies:
            cp.wait_send()

    vm = pl.BlockSpec(memory_space=pltpu.VMEM)
    return pl.pallas_call(
        body, name=name, in_specs=[vm] * n, out_specs=vm,
        out_shape=jax.ShapeDtypeStruct((R, C), F32),
        scratch_shapes=[pltpu.VMEM((R, C), F32), pltpu.VMEM((N_DEV, R, C), F32), pltpu.SemaphoreType.DMA((N_DEV - 1,)),
                        pltpu.SemaphoreType.DMA((N_DEV - 1,))],
    )(*[a for a, _ in parts])


_HBM = pl.BlockSpec(memory_space=pltpu.HBM)
_SEM = pl.BlockSpec(memory_space=pltpu.SEMAPHORE)
_EFFECT = pltpu.SideEffectType.DATAFLOW_SIDE_EFFECTING


def _peers(x, y, c):
    out = []
    for kk in range(1, N_DEV):
        fx, fy, fc = kk >> 2 & 1, kk >> 1 & 1, kk & 1
        out.append((1 - x if fx else x, 1 - y if fy else y, 1 - c if fc else c))
    return out


def _send_start(src, per_peer, name, dep):
    (handles, token) = _send_start_many([src], per_peer, name, dep)
    return handles, token


def _send_start_many(srcs, per_peer, name, dep):
    n = len(srcs)

    def body(*refs):
        src_refs, land_refs = refs[:n], refs[n:2 * n]
        send_sems, recv_sems = refs[2 * n + 1], refs[2 * n + 2]
        token = refs[-1]
        x, y, c = _place()
        mine = _slot(x, y, c)
        for a in range(n):
            for kk, peer in enumerate(_peers(x, y, c)):
                pltpu.make_async_remote_copy(
                    src_ref=src_refs[a].at[_slot(*peer)] if per_peer else src_refs[a], dst_ref=land_refs[a].at[mine],
                    send_sem=send_sems.at[a * (N_DEV - 1) + kk], recv_sem=recv_sems.at[a * (N_DEV - 1) + kk],
                    device_id=peer, device_id_type=MESH).start()
        token[...] = jnp.zeros_like(token)

    lands = [lax.empty((N_DEV,) + tuple(s.shape[1:] if per_peer else s.shape), s.dtype) for s in srcs]
    hbm = lambda t: pltpu.with_memory_space_constraint(t, pltpu.HBM)
    outs = pl.pallas_call(
        body, name=name,
        out_shape=(pltpu.SemaphoreType.DMA((n * (N_DEV - 1),)), pltpu.SemaphoreType.DMA((n * (N_DEV - 1),)),
                   *[pltpu.HBM(s.shape, s.dtype) for s in srcs], *[pltpu.HBM(l.shape, l.dtype) for l in lands],
                   jax.ShapeDtypeStruct((8, 128), F32)),
        in_specs=(*[_HBM] * (2 * n), _ANY),
        out_specs=(_SEM, _SEM, *[_HBM] * (2 * n), pl.BlockSpec(memory_space=pltpu.VMEM)),
        input_output_aliases={i: 2 + i for i in range(2 * n)},
        compiler_params=pltpu.CompilerParams(has_side_effects=_EFFECT),
    )(*[hbm(s) for s in srcs], *[hbm(l) for l in lands], dep)
    return (outs[0], outs[1], list(outs[2:2 + n]), list(outs[2 + n:2 + 2 * n])), outs[-1]


def _send_wait(handles, after, name):
    srcs, lands = _send_wait_many(handles, after, name)
    return srcs[0], lands[0]


def _send_wait_many(handles, after, name):
    send_sems, recv_sems, src_thrus, land_thrus = handles
    n = len(src_thrus)

    def body(*refs):
        land_refs = refs[n:2 * n]
        send_sems, recv_sems = refs[2 * n], refs[2 * n + 1]
        me = _place()
        for a in range(n):
            for kk in range(N_DEV - 1):
                cp = pltpu.make_async_remote_copy(
                    src_ref=land_refs[a].at[0], dst_ref=land_refs[a].at[0],
                    send_sem=send_sems.at[a * (N_DEV - 1) + kk], recv_sem=recv_sems.at[a * (N_DEV - 1) + kk],
                    device_id=me, device_id_type=MESH)
                cp.wait_send()
                cp.wait_recv()

    both = list(src_thrus) + list(land_thrus)
    outs = pl.pallas_call(
        body, name=name,
        out_shape=tuple(pltpu.HBM(t.shape, t.dtype) for t in both),
        in_specs=(*[_HBM] * (2 * n), _SEM, _SEM, _ANY), out_specs=tuple([_HBM] * (2 * n)),
        input_output_aliases={i: i for i in range(2 * n)},
        compiler_params=pltpu.CompilerParams(has_side_effects=_EFFECT),
    )(*both, send_sems, recv_sems, after)
    return list(outs[:n]), list(outs[n:])


def _sum_slots(land, name):
    _, R, C = land.shape
    tm = R if R <= 512 else 512

    def body(x_ref, o_ref):
        acc = x_ref[0].astype(F32)
        for j in range(1, N_DEV):
            acc = acc + x_ref[j].astype(F32)
        o_ref[...] = acc

    return pl.pallas_call(
        body, name=name, grid=(R // tm,),
        in_specs=[pl.BlockSpec((N_DEV, tm, C), lambda i: (0, i, 0))], out_specs=pl.BlockSpec((tm, C), lambda i: (i, 0)),
        out_shape=jax.ShapeDtypeStruct((R, C), F32), compiler_params=_cparams(("parallel",)),
    )(land)


def _adam_math(w, g, m, v):
    m2 = ADAM_B1 * m + (1.0 - ADAM_B1) * g
    v2 = ADAM_B2 * v + (1.0 - ADAM_B2) * (g * g)
    m_hat = m2 / (1.0 - ADAM_B1 ** ADAM_STEP)
    v_hat = v2 / (1.0 - ADAM_B2 ** ADAM_STEP)
    delta = -ADAM_LR * (m_hat / (jnp.sqrt(v_hat) + ADAM_EPS) + ADAM_WD * w)
    return delta, m2, v2


def _adamw(w, g, m, v, name):
    R, C = w.shape
    tm = R if R <= 512 else 256
    return _rowwise(lambda w, g, m, v: (_adam_math(w, g, m, v), ()), [w, g, m, v], [], [(C, F32)] * 3, [], tm=tm, name=name)


def _adamw_small(slab, slab_rows, g_conv_w, ws, ms, vs):
    n = len(ws)

    def body(*refs):
        slab_ref, gc_ref = refs[0], refs[1]
        w_refs, m_refs, v_refs = refs[2:2 + n], refs[2 + n:2 + 2 * n], refs[2 + 2 * n:2 + 3 * n]
        outs = refs[2 + 3 * n:]
        loss_ref = outs[0]
        g_out, d_out, m_out, v_out = (outs[1 + i * n:1 + (i + 1) * n] for i in range(4))
        loss_ref[...] = jnp.sum(slab_ref[pl.ds(6, 1), :], axis=1, keepdims=True)
        for i in range(n):
            g = gc_ref[...] if i == n - 1 else slab_ref[pl.ds(slab_rows[i], 1), pl.ds(0, ws[i].shape[1])]
            d, m2, v2 = _adam_math(w_refs[i][...], g, m_refs[i][...], v_refs[i][...])
            g_out[i][...] = g
            d_out[i][...] = d
            m_out[i][...] = m2
            v_out[i][...] = v2

    vm = pl.BlockSpec(memory_space=pltpu.VMEM)
    shapes = [jax.ShapeDtypeStruct(w.shape, F32) for w in ws]
    outs = pl.pallas_call(
        body, name="adamw_small", in_specs=[vm] * (2 + 3 * n), out_specs=[vm] * (1 + 4 * n),
        out_shape=[jax.ShapeDtypeStruct((1, 1), F32)] + shapes * 4,
    )(slab, g_conv_w, *ws, *ms, *vs)
    return outs[0], outs[1:1 + n], outs[1 + n:1 + 2 * n], outs[1 + 2 * n:1 + 3 * n], outs[1 + 3 * n:]


SMALL = ["norm_mix_pre", "norm_mix_post", "norm_mlp_pre", "norm_mlp_post", "norm_ple_post",
         "conv_b", "ssd_norm_g", "dt_bias", "a_log", "d_skip"]


def _pad_row(v, width=D):
    return jnp.pad(v, ((0, 0), (0, width - v.shape[1])))


def kernel(x, p, positions, norm_mix_pre, norm_mix_post, w_in, conv_w, conv_b, dt_bias, a_log, d_skip, ssd_norm_g, w_out, norm_mlp_pre, norm_mlp_post, w_up, w_down, w_ple_gate, w_ple_proj, norm_ple_post, loss_target, m_norm_mix_pre, m_norm_mix_post, m_w_in, m_conv_w, m_conv_b, m_dt_bias, m_a_log, m_d_skip, m_ssd_norm_g, m_w_out, m_norm_mlp_pre, m_norm_mlp_post, m_w_up, m_w_down, m_w_ple_gate, m_w_ple_proj, m_norm_ple_post, v_norm_mix_pre, v_norm_mix_post, v_w_in, v_conv_w, v_conv_b, v_dt_bias, v_a_log, v_d_skip, v_ssd_norm_g, v_w_out, v_norm_mlp_pre, v_norm_mlp_post, v_w_up, v_w_down, v_w_ple_gate, v_w_ple_proj, v_norm_ple_post):
    args = dict(locals())
    x2, p2, tgt = x[0], p[0, 0], loss_target[0]
    g1, g2, g3, g4, g5 = norm_mix_pre, norm_mix_post, norm_mlp_pre, norm_mlp_post, norm_ple_post

    me = _slot(*_place())
    pack_in = jnp.pad(w_in[0].T, ((0, W_IN_SHARD_PAD - W_IN_SHARD), (0, 0))).astype(BF16)
    rest = [w_out[0].astype(BF16), w_up[0].T.astype(BF16), w_down[0].astype(BF16), w_ple_gate[0].astype(BF16),
            w_ple_proj[0].T.reshape(32, D).astype(BF16)]
    conv_pack = jnp.pad(conv_w[0], ((0, 4), (0, 32)))
    in_handles, tok_in0 = _send_start_many([pack_in, conv_pack], False, "gather_in_start", g1)

    inv_freq = ROPE_THETA ** (-jnp.arange(HD // 2, dtype=F32) * 2.0 / HD)
    pos = positions[0] + tok_in0[0, 0].astype(jnp.int32)
    ang = pos.astype(F32)[:, None] * inv_freq
    cos, sin = jnp.cos(ang), jnp.sin(ang)
    cos128 = jnp.concatenate([cos, cos, cos, cos], axis=1)
    sin128 = jnp.concatenate([-sin, sin, -sin, sin], axis=1)

    bias_w, alog_w, dsk_w = _pad_row(dt_bias, DT_PAD), _pad_row(a_log, DT_PAD), _pad_row(d_skip, DT_PAD)

    (u1,) = _rowwise(lambda a, g: ((a * _rstd(a) * g,), ()), [x2], [g1], [(D, BF16)], [], tm=512, name="norm_x",
                     deps=[cos128, sin128])
    p2b = p2.astype(BF16)

    in_back, in_land = _send_wait_many(in_handles, u1, "gather_in_wait")
    gin = lax.dynamic_update_slice(in_land[0], in_back[0][None], (me, 0, 0))
    gconv = lax.dynamic_update_slice(in_land[1], in_back[1][None], (me, 0, 0))
    rest_handles, tok_rest = _send_start_many(rest, False, "gather_rest_start", gconv)
    w_inT = gin[:, :W_IN_SHARD].reshape(IN_W, D)
    w_qkvzT = w_inT[:4 * AW]
    w_xbcdtT = jnp.pad(w_inT[4 * AW:], ((0, DT_PAD - HEADS), (0, 0)))
    conv_full = gconv[:, :CONV_K, :96].transpose(1, 0, 2).reshape(CONV_K, CONV_CH)
    qkvz = _mm(u1, w_qkvzT, tb=True, tm=512, tn=2048, tk=1024, name="proj_qkvz", deps=[tok_rest])
    xbcdt = _mm(u1, w_xbcdtT, tb=True, tm=512, tn=896, tk=1024, name="proj_xbcdt")

    qkv = _rope_fwd(qkvz, cos128, sin128)
    qkv = [qkv[3 * i:3 * i + 3] for i in range(len(DILATIONS))]
    outs, lses = [], []
    for d, (qd, kd, vd) in zip(DILATIONS, qkv):
        o, l = _attn_fwd(qd, kd, vd, d)
        outs.append(o)
        lses.append(l)
    attn, lse, attn4, lse4, attn16, lse16 = _attn_merge(outs, lses)

    act = _conv_fwd(xbcdt, conv_full, conv_b)
    y_ssd, states, cat = _ssd_fwd(act, xbcdt, bias_w, alog_w, dsk_w, qkvz, attn, ssd_norm_g)


    rest_back, landed = _send_wait_many(rest_handles, cat, "gather_rest_wait")
    landed = [lax.dynamic_update_slice(l, b[None], (me, 0, 0)) for l, b in zip(landed, rest_back)]
    w_o, w_upT, w_dn, w_gate = landed[0].reshape(D, D), landed[1].reshape(DFF, D), landed[2].reshape(DFF, D), landed[3].reshape(D, D)
    w_projT = landed[4].reshape(D, PLE)

    def post1(mm, xx, ga):
        h = xx + mm * _rstd(mm) * ga
        return (mm, h, _rstd(h)), ()
    mix, h1, r3 = _mm_rows(post1, [(cat, w_o, False)], [x2], [g2], [(D, F32), (D, F32), (1, F32)], [], tm=512,
                           name="mix_out")

    a_up, ff, u2, h2, h2b = _mlp_fwd(h1, r3, g3, w_upT, w_dn, g4)
    relu2 = lambda a: jnp.square(jnp.maximum(a.astype(F32), 0.0))

    def final(gpre, ppv, hh, tg, g):
        sg = _sigmoid(gpre)
        ple = ppv * sg
        r = _rstd(ple)
        n = ple * r
        h3 = hh + n * g
        e = h3 - tg
        dh3 = e * (1.0 / D)
        dple = _rms_bwd(n, r, g, dh3)
        return (dh3, dple * sg, dple * ppv * sg * (1.0 - sg)), (_colsum(dh3 * n), _colsum(0.5 * e * e * (1.0 / D)))
    dh3, dpp, dgp, dg5, loss_vec = _mm_rows(final, [(h2b, w_gate, False), (p2b, w_projT, True)], [h2, tgt], [g5],
                                            [(D, F32), (D, BF16), (D, BF16)], [(1, D), (1, D)], tm=512, name="ple_loss")

    gw_projT = _mm(dpp, p2b, ta=True, tm=512, tn=256, tk=T, out_dtypes=(BF16,), name="gw_ple_proj")
    gw_gate = _mm(h2b, dgp, ta=True, tm=512, tn=1024, tk=T, out_dtypes=(BF16,), name="gw_ple_gate")
    rs_proj, tok_proj = _send_start(gw_projT.reshape(N_DEV, 32, D), True, "rs_start_w_proj", g1)
    rs_gate, tok_gate = _send_start(gw_gate.reshape(N_DEV, 128, D), True, "rs_start_w_gate", g1)
    def bwd_mlp_post(dg_, d3, f, g):
        dh2 = d3 + dg_
        r = _rstd(f)
        n = f * r
        return (dh2, _rms_bwd(n, r, g, dh2)), (_colsum(dh2 * n),)
    dh2, dff, dg4 = _mm_rows(bwd_mlp_post, [(dgp, w_gate, True)], [dh3, ff], [g4], [(D, F32), (D, BF16)], [(1, D)],
                             tm=512, name="bwd_ple_gate", deps=[tok_proj, tok_gate])

    gw_dn = _mm(a_up, dff, ta=True, tm=512, tn=1024, tk=T, a_pre=relu2, out_dtypes=(BF16,), name="gw_mlp_down")
    rs_dn, tok_dn = _send_start(gw_dn.reshape(N_DEV, 512, D), True, "rs_start_w_down", g1)
    da_up, du2 = _mlp_dx(dff, a_up, w_upT, w_dn, tok_dn)
    gw_upT = _mm(da_up, u2, ta=True, tm=512, tn=1024, tk=T, out_dtypes=(BF16,), name="gw_mlp_up")
    rs_up, tok_up = _send_start(gw_upT.reshape(N_DEV, 512, D), True, "rs_start_w_up", g1)

    def bwd_mix_post(d2, du, hh, rr, mm, ga, gb):
        n3 = hh * rr
        dh1 = d2 + _rms_bwd(n3, rr, gb, du)
        r = _rstd(mm)
        n2 = mm * r
        return (dh1, _rms_bwd(n2, r, ga, dh1)), (_colsum(du * n3), _colsum(dh1 * n2))
    dh1, dmix, dg3, dg2 = _rowwise(bwd_mix_post, [dh2, du2, h1, r3, mix], [g2, g3], [(D, F32), (D, BF16)],
                                   [(1, D), (1, D)], tm=512, name="bwd_post_mix", deps=[tok_up])

    gw_o = _mm(cat, dmix, ta=True, tm=512, tn=1024, tk=T, out_dtypes=(BF16,), name="gw_out")
    rs_o, tok_o = _send_start(gw_o.reshape(N_DEV, 128, D), True, "rs_start_w_out", g1)
    dcat = _mm(dmix, w_o, tb=True, tm=512, tn=1024, tk=1024, name="dx_out", deps=[tok_o])

    dact, ddtw, ssd_par, dz, dgs = _ssd_bwd(act, xbcdt, bias_w, alog_w, dsk_w, states, y_ssd, qkvz, dcat, ssd_norm_g)
    dxbcdt, conv_par = _conv_bwd(xbcdt, dact, ddtw, conv_full, conv_b)

    dattn4, dattn16 = _dilate_cols(dcat, 0)
    qkv_grads = [_attn_bwd(*qkv[0], dcat, attn, lse, 1),
                 _attn_bwd(*qkv[1], dattn4, attn4, lse4, 4),
                 _attn_bwd(*qkv[2], dattn16, attn16, lse16, 16)]
    dqkvz = _rope_bwd(qkv_grads, dz, cos128, sin128)

    gw_qkvzT = _mm(dqkvz, u1, ta=True, tm=512, tn=1024, tk=T, out_dtypes=(BF16,), name="gw_qkvz")
    gw_xbcdtT = _mm(dxbcdt, u1, ta=True, tm=896, tn=1024, tk=T, out_dtypes=(BF16,), name="gw_xbcdt")
    gw_inT = jnp.concatenate([gw_qkvzT, gw_xbcdtT], axis=0)[:IN_W]
    gw_inT = jnp.pad(gw_inT.reshape(N_DEV, W_IN_SHARD, D), ((0, 0), (0, W_IN_SHARD_PAD - W_IN_SHARD), (0, 0)))
    rs_in, tok_in = _send_start(gw_inT, True, "rs_start_w_in", g1)

    def bwd_in(ua, ub, d1, xx, g):
        rr = _rstd(xx)
        n = xx * rr
        du = ua + ub
        return (d1 + _rms_bwd(n, rr, g, du),), (_colsum(du * n),)
    grad_x, dg1 = _mm_rows(bwd_in, [(dqkvz, w_qkvzT, False), (dxbcdt, w_xbcdtT, False)], [dh1, x2], [g1],
                           [(D, F32)], [(1, D)], tm=512, name="bwd_in_proj", deps=[tok_in])

    slab = _all_reduce_small([(dg1, 0), (dg2, 1), (dg3, 2), (dg4, 3), (dg5, 4), (dgs, 5), (loss_vec, 6),
                              (conv_par, 8), (ssd_par, 16)], "reduce_small")
    g_conv_w = lax.dynamic_slice(slab[8:12, :CONV_CH], (0, me * 96), (CONV_K, 96))

    def scatter_finish(handles, nm, after):
        part, land = _send_wait(handles, after, "rs_wait_" + nm)
        own = lax.dynamic_slice(part, (me, 0, 0), (1,) + part.shape[1:])
        return _sum_slots(lax.dynamic_update_slice(land, own, (me, 0, 0)), "rs_sum_" + nm)
    g_out = scatter_finish(rs_o, "w_out", slab)
    g_upT = scatter_finish(rs_up, "w_up", slab)
    g_dn = scatter_finish(rs_dn, "w_down", slab)
    g_gate = scatter_finish(rs_gate, "w_gate", slab)
    g_projT = scatter_finish(rs_proj, "w_proj", slab)

    small_names = SMALL + ["conv_w"]
    small_rows = [0, 1, 2, 3, 4, 12, 5, 16, 17, 18, None]
    pick = lambda prefix: [args[prefix + nme] for nme in SMALL] + [args[prefix + "conv_w"][0]]
    loss11, g_s, d_s, m_s, v_s = _adamw_small(slab, small_rows, g_conv_w, pick(""), pick("m_"), pick("v_"))
    loss = loss11[0, 0]
    grads = {
        "w_out": g_out[None], "w_up": g_upT.T[None], "w_down": g_dn[None],
        "w_ple_gate": g_gate[None], "w_ple_proj": g_projT.reshape(128, PLE).T[None],
    }
    delta, new_m, new_v = {}, {}, {}
    for i, nme in enumerate(small_names):
        lead = (lambda t: t[None]) if nme == "conv_w" else (lambda t: t)
        grads[nme], delta[nme], new_m[nme], new_v[nme] = lead(g_s[i]), lead(d_s[i]), lead(m_s[i]), lead(v_s[i])
    for nme in ["w_out", "w_up", "w_down", "w_ple_gate", "w_ple_proj", "w_in"]:
        if nme == "w_in":
            g_inT = scatter_finish(rs_in, "w_in", delta["w_down"])
            grads["w_in"] = g_inT[:W_IN_SHARD].T[None]
        dl, mm_, vv_ = _adamw(args[nme][0], grads[nme][0], args["m_" + nme][0], args["v_" + nme][0], "adamw_" + nme)
        delta[nme], new_m[nme], new_v[nme] = dl[None], mm_[None], vv_[None]

    order = ["norm_mix_pre", "norm_mix_post", "w_in", "conv_w", "conv_b", "dt_bias", "a_log", "d_skip", "ssd_norm_g",
             "w_out", "norm_mlp_pre", "norm_mlp_post", "w_up", "w_down", "w_ple_gate", "w_ple_proj", "norm_ple_post"]
    return (loss, grad_x[None], *[grads[n] for n in order], *[delta[n] for n in order],
            *[new_m[n] for n in order], *[new_v[n] for n in order])
```

```python
import functools
import math

import jax
import jax.numpy as jnp
from jax import lax
from jax.experimental import pallas as pl
from jax.experimental.pallas import tpu as pltpu

F32 = jnp.float32
BF16 = jnp.bfloat16
MESH = pl.DeviceIdType.MESH
HIGHEST = lax.Precision.HIGHEST

N_DEV = 8
T = 4096
D = 1024
HEADS = 8
HD = 64
AW = 512
NS = 128
CONV_K = 4
CONV_CH = 768
CHUNK = 128
DFF = 4096
PLE = 256
EPS = 1e-6
ROPE_THETA = 10000.0
DILATIONS = (1, 4, 16)
QBLK = 128
NEG = -1e30
IN_W = 2824
W_IN_SHARD = 353
W_IN_SHARD_PAD = 384
DT_PAD = 128

ADAM_LR, ADAM_B1, ADAM_B2, ADAM_EPS, ADAM_WD, ADAM_STEP = 0.001, 0.9, 0.999, 1e-08, 0.01, 10

VMEM_LIMIT = 56 * 1024 * 1024


_ANY = pl.BlockSpec(memory_space=pl.ANY)


def _cparams(sem=None):
    return pltpu.CompilerParams(dimension_semantics=sem, vmem_limit_bytes=VMEM_LIMIT)


def _dot(a, b, ca, cb, precision=None):
    return lax.dot_general(a, b, (((ca,), (cb,)), ((), ())), preferred_element_type=F32, precision=precision)


def _nn(a, b):
    return _dot(a, b, 1, 0)


def _nt(a, b):
    return _dot(a, b, 1, 1)


def _tn(a, b):
    return _dot(a, b, 0, 0)


def _sigmoid(x):
    return 1.0 / (1.0 + jnp.exp(-x))


def _softplus(x):
    return jnp.maximum(x, 0.0) + jnp.log(1.0 + jnp.exp(-jnp.abs(x)))


def _mm(a, b, *, ta=False, tb=False, tm, tn, tk, name,
        a_pre=None, a_rows=(), a_cols=(), b_pre=None, b_rows=(), b_cols=(),
        epi=None, epi_tiles=(), out_dtypes=(F32,), deps=()):
    if ta:
        K, M = a.shape
    else:
        M, K = a.shape
    if tb:
        N, K2 = b.shape
    else:
        K2, N = b.shape
    assert K == K2 and M % tm == 0 and N % tn == 0 and K % tk == 0, (name, a.shape, b.shape)
    nk = K // tk
    if ta:
        a_spec = pl.BlockSpec((tk, tm), lambda i, j, k: (k, i))
        a_row_specs = [pl.BlockSpec((tk, 1), lambda i, j, k: (k, 0)) for _ in a_rows]
        a_col_specs = [pl.BlockSpec((1, tm), lambda i, j, k: (0, i)) for _ in a_cols]
    else:
        a_spec = pl.BlockSpec((tm, tk), lambda i, j, k: (i, k))
        a_row_specs = [pl.BlockSpec((tm, 1), lambda i, j, k: (i, 0)) for _ in a_rows]
        a_col_specs = [pl.BlockSpec((1, tk), lambda i, j, k: (0, k)) for _ in a_cols]
    if tb:
        b_spec = pl.BlockSpec((tn, tk), lambda i, j, k: (j, k))
        b_row_specs = [pl.BlockSpec((tn, 1), lambda i, j, k: (j, 0)) for _ in b_rows]
        b_col_specs = [pl.BlockSpec((1, tk), lambda i, j, k: (0, k)) for _ in b_cols]
    else:
        b_spec = pl.BlockSpec((tk, tn), lambda i, j, k: (k, j))
        b_row_specs = [pl.BlockSpec((tk, 1), lambda i, j, k: (k, 0)) for _ in b_rows]
        b_col_specs = [pl.BlockSpec((1, tn), lambda i, j, k: (0, j)) for _ in b_cols]
    o_spec = pl.BlockSpec((tm, tn), lambda i, j, k: (i, j))
    na, nb, ne, no = len(a_rows) + len(a_cols), len(b_rows) + len(b_cols), len(epi_tiles), len(out_dtypes)

    def body(*refs):
        a_ref, b_ref = refs[0], refs[1]
        a_ex = refs[2:2 + na]
        b_ex = refs[2 + na:2 + na + nb]
        e_ex = refs[2 + na + nb:2 + na + nb + ne]
        first_out = 2 + na + nb + ne + len(deps)
        outs = refs[first_out:first_out + no]

        def finish(res):
            vals = epi(res, *[r[...] for r in e_ex]) if epi is not None else (res,)
            for o_ref, val in zip(outs, vals):
                o_ref[...] = val.astype(o_ref.dtype)

        at = a_ref[...]
        if a_pre is not None:
            at = a_pre(at, *[r[...] for r in a_ex])
        bt = b_ref[...]
        if b_pre is not None:
            bt = b_pre(bt, *[r[...] for r in b_ex])
        prod = _dot(at.astype(BF16), bt.astype(BF16), 0 if ta else 1, 1 if tb else 0)
        if nk == 1:
            finish(prod)
            return
        acc = refs[-1]
        k = pl.program_id(2)

        @pl.when(k == 0)
        def _():
            acc[...] = jnp.zeros_like(acc)
        acc[...] += prod

        @pl.when(k == nk - 1)
        def _():
            finish(acc[...])

    outs = pl.pallas_call(
        body, name=name,
        grid=(M // tm, N // tn, nk),
        in_specs=([a_spec, b_spec] + a_row_specs + a_col_specs + b_row_specs + b_col_specs + [o_spec] * ne
                  + [_ANY] * len(deps)),
        out_specs=[o_spec] * no,
        out_shape=[jax.ShapeDtypeStruct((M, N), dt) for dt in out_dtypes],
        scratch_shapes=[pltpu.VMEM((tm, tn), F32)] if nk > 1 else [],
        compiler_params=_cparams(("parallel", "parallel", "arbitrary")),
    )(a, b, *a_rows, *a_cols, *b_rows, *b_cols, *epi_tiles, *deps)
    return outs[0] if no == 1 else outs


MLP_TM = 1024
MLP_TC = 512


def _mlp_fwd(h, r, g, w_upT, w_dn, g_post):
    nc = DFF // MLP_TC

    def body(h_ref, r_ref, g_ref, wu_ref, wd_ref, gp_ref, a_ref, ff_ref, u_ref, ho_ref, hob_ref, acc, u_scr):
        c = pl.program_id(1)

        @pl.when(c == 0)
        def _():
            u = (h_ref[...] * r_ref[...] * g_ref[...]).astype(BF16)
            u_scr[...] = u
            u_ref[...] = u
            acc[...] = jnp.zeros_like(acc)
        a = _nt(u_scr[...], wu_ref[...])
        a_ref[...] = a.astype(BF16)
        acc[...] += _nn(jnp.square(jnp.maximum(a, 0.0)).astype(BF16), wd_ref[...])

        @pl.when(c == nc - 1)
        def _():
            f = acc[...]
            ff_ref[...] = f
            ho = h_ref[...] + f * _rstd(f) * gp_ref[...]
            ho_ref[...] = ho
            hob_ref[...] = ho.astype(BF16)

    row = pl.BlockSpec((MLP_TM, D), lambda i, c: (i, 0))
    wsp = pl.BlockSpec((MLP_TC, D), lambda i, c: (c, 0))
    vec = pl.BlockSpec((1, D), lambda i, c: (0, 0))
    return pl.pallas_call(
        body, name="mlp_fwd", grid=(T // MLP_TM, nc),
        in_specs=[row, pl.BlockSpec((MLP_TM, 1), lambda i, c: (i, 0)), vec, wsp, wsp, vec],
        out_specs=[pl.BlockSpec((MLP_TM, MLP_TC), lambda i, c: (i, c)), row, row, row, row],
        out_shape=[jax.ShapeDtypeStruct((T, DFF), BF16), jax.ShapeDtypeStruct((T, D), F32), jax.ShapeDtypeStruct((T, D), BF16),
                   jax.ShapeDtypeStruct((T, D), F32), jax.ShapeDtypeStruct((T, D), BF16)],
        scratch_shapes=[pltpu.VMEM((MLP_TM, D), F32), pltpu.VMEM((MLP_TM, D), BF16)],
        compiler_params=_cparams(("parallel", "arbitrary")),
    )(h, r, g, w_upT, w_dn, g_post)


def _mlp_dx(dff, a, w_upT, w_dn, dep):
    nc = DFF // MLP_TC

    def body(d_ref, a_ref, wu_ref, wd_ref, dep_ref, da_ref, du_ref, acc, d_scr):
        c = pl.program_id(1)

        @pl.when(c == 0)
        def _():
            d_scr[...] = d_ref[...].astype(BF16)
            acc[...] = jnp.zeros_like(acc)
        da = (_nt(d_scr[...], wd_ref[...]) * (2.0 * jnp.maximum(a_ref[...].astype(F32), 0.0))).astype(BF16)
        da_ref[...] = da
        acc[...] += _nn(da, wu_ref[...])

        @pl.when(c == nc - 1)
        def _():
            du_ref[...] = acc[...]

    row = pl.BlockSpec((MLP_TM, D), lambda i, c: (i, 0))
    wsp = pl.BlockSpec((MLP_TC, D), lambda i, c: (c, 0))
    chunk = pl.BlockSpec((MLP_TM, MLP_TC), lambda i, c: (i, c))
    return pl.pallas_call(
        body, name="mlp_dx", grid=(T // MLP_TM, nc),
        in_specs=[row, chunk, wsp, wsp, _ANY], out_specs=[chunk, row],
        out_shape=[jax.ShapeDtypeStruct((T, DFF), BF16), jax.ShapeDtypeStruct((T, D), F32)],
        scratch_shapes=[pltpu.VMEM((MLP_TM, D), F32), pltpu.VMEM((MLP_TM, D), BF16)],
        compiler_params=_cparams(("parallel", "arbitrary")),
    )(dff, a, w_upT, w_dn, dep)


def _rowwise(fn, rows, vecs, out_rows, out_sums, *, tm, name, deps=()):
    specs, arrs = [], []
    R = None
    for r in rows:
        if isinstance(r, tuple):
            arr, width, cb = r
            specs.append(pl.BlockSpec((tm, width), lambda i, cb=cb: (i, cb)))
        else:
            arr = r
            specs.append(pl.BlockSpec((tm, arr.shape[1]), lambda i: (i, 0)))
        R = arr.shape[0] if R is None else R
        assert arr.shape[0] == R, name
        arrs.append(arr)
    assert R % tm == 0, name
    for v in vecs:
        specs.append(pl.BlockSpec(v.shape, lambda i: (0, 0)))
        arrs.append(v)
    nr, nv, no, ns = len(rows), len(vecs), len(out_rows), len(out_sums)
    out_specs = [pl.BlockSpec((tm, w), lambda i: (i, 0)) for w, _ in out_rows]
    out_specs += [pl.BlockSpec(s, lambda i: (0, 0)) for s in out_sums]
    out_shape = [jax.ShapeDtypeStruct((R, w), dt) for w, dt in out_rows]
    out_shape += [jax.ShapeDtypeStruct(s, F32) for s in out_sums]

    nd = len(deps)

    def body(*refs):
        ins = [r[...] for r in refs[:nr + nv]]
        o_refs = refs[nr + nv + nd:nr + nv + nd + no]
        s_refs = refs[nr + nv + nd + no:]
        o_vals, s_vals = fn(*ins)
        for ref, val in zip(o_refs, o_vals):
            ref[...] = val.astype(ref.dtype)
        if ns:
            @pl.when(pl.program_id(0) == 0)
            def _():
                for ref in s_refs:
                    ref[...] = jnp.zeros_like(ref)
            for ref, val in zip(s_refs, s_vals):
                ref[...] += val

    outs = pl.pallas_call(
        body, name=name, grid=(R // tm,), in_specs=specs + [_ANY] * nd, out_specs=out_specs, out_shape=out_shape,
        compiler_params=_cparams(("arbitrary",) if ns else ("parallel",)),
    )(*arrs, *deps)
    return outs


def _mm_rows(fn, mats, rows, vecs, out_rows, out_sums, *, tm, name, deps=()):
    R = mats[0][0].shape[0]
    assert R % tm == 0, name
    specs, arrs = [], []
    for a, b, tb in mats:
        specs += [pl.BlockSpec((tm, a.shape[1]), lambda i: (i, 0)), pl.BlockSpec(b.shape, lambda i: (0, 0))]
        arrs += [a, b]
    for r in rows:
        specs.append(pl.BlockSpec((tm, r.shape[1]), lambda i: (i, 0)))
        arrs.append(r)
    for v in vecs:
        specs.append(pl.BlockSpec(v.shape, lambda i: (0, 0)))
        arrs.append(v)
    nm, nr, nv, nd, no, ns = len(mats), len(rows), len(vecs), len(deps), len(out_rows), len(out_sums)
    out_specs = [pl.BlockSpec((tm, w), lambda i: (i, 0)) for w, _ in out_rows]
    out_specs += [pl.BlockSpec(s, lambda i: (0, 0)) for s in out_sums]
    out_shape = [jax.ShapeDtypeStruct((R, w), dt) for w, dt in out_rows] + [jax.ShapeDtypeStruct(s, F32) for s in out_sums]

    def body(*refs):
        prods = [_dot(refs[2 * p][...].astype(BF16), refs[2 * p + 1][...].astype(BF16), 1, 1 if mats[p][2] else 0)
                 for p in range(nm)]
        ins = [r[...] for r in refs[2 * nm:2 * nm + nr + nv]]
        first_out = 2 * nm + nr + nv + nd
        o_refs, s_refs = refs[first_out:first_out + no], refs[first_out + no:]
        o_vals, s_vals = fn(*prods, *ins)
        for ref, val in zip(o_refs, o_vals):
            ref[...] = val.astype(ref.dtype)
        if ns:
            @pl.when(pl.program_id(0) == 0)
            def _():
                for ref in s_refs:
                    ref[...] = jnp.zeros_like(ref)
            for ref, val in zip(s_refs, s_vals):
                ref[...] += val

    return pl.pallas_call(
        body, name=name, grid=(R // tm,), in_specs=specs + [_ANY] * nd, out_specs=out_specs, out_shape=out_shape,
        compiler_params=_cparams(("arbitrary",) if ns else ("parallel",)),
    )(*arrs, *deps)


def _colsum(x):
    return jnp.sum(x, axis=0, keepdims=True)


def _rstd(x):
    return lax.rsqrt(jnp.mean(x * x, axis=-1, keepdims=True) + EPS)


def _rms_bwd(xn, r, g, dy):
    dn = dy * g
    return r * (dn - xn * jnp.mean(dn * xn, axis=-1, keepdims=True))


def _partner(t):
    lane = lax.broadcasted_iota(jnp.int32, t.shape, 1)
    up = pltpu.roll(t, 96, 1)
    down = pltpu.roll(t, 32, 1)
    return jnp.where((lane % 64) < 32, up, down)


SLABS = AW // 128


def _rows(r, n, d):
    return pl.ds(r, n, stride=d) if d > 1 else pl.ds(0, n)


def _undilate(src_ref, dst, d, tm):
    for r in range(d):
        for j in range(SLABS):
            dst[j][_rows(r, tm // d, d), :] = src_ref[:, pl.ds(r * AW + j * 128, 128)].astype(dst[j].dtype)


def _dilate(dst_ref, src, d, tm):
    for r in range(d):
        for j in range(SLABS):
            dst_ref[:, pl.ds(r * AW + j * 128, 128)] = src[j][_rows(r, tm // d, d), :].astype(dst_ref.dtype)


def _slab_scratch(n, tm):
    return [pltpu.VMEM((tm, 128), F32)] * (SLABS * n)


def _slab_groups(flat):
    return [flat[SLABS * i:SLABS * (i + 1)] for i in range(len(flat) // SLABS)]


def _slab_specs(tm, first):
    return [pl.BlockSpec((tm, 128), lambda i, j=j: (i, first + j)) for j in range(SLABS)]


def _dil_spec(tm, d):
    return pl.BlockSpec((tm // d, d * AW), lambda i: (i, 0))


ROPE_TM = 512


def _rope_fwd(qkvz, cos128, sin128):
    tm = ROPE_TM

    def body(*refs):
        q_refs, k_refs, v_refs = refs[0:4], refs[4:8], refs[8:12]
        c_ref, s_ref = refs[12], refs[13]
        outs = refs[14:23]
        qs, ks = _slab_groups(refs[23:])
        c, s = c_ref[...], s_ref[...]
        for j in range(SLABS):
            q, k = q_refs[j][...], k_refs[j][...]
            qs[j][...] = (q * c + _partner(q) * s) * (HD ** -0.5)
            ks[j][...] = k * c + _partner(k) * s
        for di, d in enumerate(DILATIONS):
            oq, ok, ov = outs[3 * di:3 * di + 3]
            for r in range(d):
                rows = _rows(r, tm // d, d)
                for j in range(SLABS):
                    cols = pl.ds(r * AW + j * 128, 128)
                    oq[:, cols] = qs[j][rows, :].astype(BF16)
                    ok[:, cols] = ks[j][rows, :].astype(BF16)
                    ov[:, cols] = v_refs[j][rows, :].astype(BF16)

    tab = pl.BlockSpec((tm, 128), lambda i: (i, 0))
    out_specs, out_shape = [], []
    for d in DILATIONS:
        out_specs += [_dil_spec(tm, d)] * 3
        out_shape += [jax.ShapeDtypeStruct((T // d, d * AW), BF16)] * 3
    return pl.pallas_call(
        body, name="rope_fwd", grid=(T // tm,),
        in_specs=_slab_specs(tm, 0) + _slab_specs(tm, 4) + _slab_specs(tm, 8) + [tab, tab],
        out_specs=out_specs, out_shape=out_shape, scratch_shapes=_slab_scratch(2, tm),
        compiler_params=_cparams(("parallel",)),
    )(*([qkvz] * 12), cos128, sin128)


def _rope_bwd(grads, dz, cos128, sin128):
    tm = 256

    def body(*refs):
        g_refs = refs[0:9]
        dz_ref, c_ref, s_ref, o_ref = refs[9], refs[10], refs[11], refs[12]
        scr = _slab_groups(refs[13:])
        for di, d in enumerate(DILATIONS[1:]):
            for t in range(3):
                _undilate(g_refs[3 * (di + 1) + t], scr[3 * di + t], d, tm)
        c, s = c_ref[...], s_ref[...]
        for j in range(SLABS):
            cols = pl.ds(j * 128, 128)
            tot = [g_refs[t][:, cols] + scr[t][j][...] + scr[3 + t][j][...] for t in range(3)]
            dqr = tot[0] * (HD ** -0.5)
            o_ref[:, pl.ds(j * 128, 128)] = (dqr * c + _partner(dqr * s)).astype(BF16)
            o_ref[:, pl.ds(AW + j * 128, 128)] = (tot[1] * c + _partner(tot[1] * s)).astype(BF16)
            o_ref[:, pl.ds(2 * AW + j * 128, 128)] = tot[2].astype(BF16)
        o_ref[:, pl.ds(3 * AW, AW)] = dz_ref[...].astype(BF16)

    tab = pl.BlockSpec((tm, 128), lambda i: (i, 0))
    in_specs, args = [], []
    for d, g in zip(DILATIONS, grads):
        in_specs += [_dil_spec(tm, d)] * 3
        args += list(g)
    return pl.pallas_call(
        body, name="rope_bwd", grid=(T // tm,),
        in_specs=in_specs + [pl.BlockSpec((tm, AW), lambda i: (i, 0)), tab, tab],
        out_specs=pl.BlockSpec((tm, 4 * AW), lambda i: (i, 0)),
        out_shape=jax.ShapeDtypeStruct((T, 4 * AW), BF16),
        scratch_shapes=_slab_scratch(6, tm),
        compiler_params=_cparams(("parallel",)),
    )(*args, dz, cos128, sin128)


def _dilate_cols(x, first):
    tm = ROPE_TM

    def body(x0, x1, x2, x3, o4, o16):
        xs = (x0, x1, x2, x3)
        for o_ref, d in ((o4, 4), (o16, 16)):
            for r in range(d):
                for j in range(SLABS):
                    o_ref[:, pl.ds(r * AW + j * 128, 128)] = xs[j][_rows(r, tm // d, d), :]

    return pl.pallas_call(
        body, name="dilate_cols", grid=(T // tm,),
        in_specs=_slab_specs(tm, first), out_specs=[_dil_spec(tm, 4), _dil_spec(tm, 16)],
        out_shape=[jax.ShapeDtypeStruct((T // 4, 4 * AW), F32), jax.ShapeDtypeStruct((T // 16, 16 * AW), F32)],
        compiler_params=_cparams(("parallel",)),
    )(x, x, x, x)


def _band_masks():
    qi = lax.broadcasted_iota(jnp.int32, (QBLK, QBLK), 0)
    kj = lax.broadcasted_iota(jnp.int32, (QBLK, QBLK), 1)
    return kj >= qi, kj <= qi


def _attn_fwd(q, k, v, d):
    L = q.shape[0]
    npair = L // (2 * QBLK)

    def body(q_ref, kp_ref, kc_ref, vp_ref, vc_ref, o_ref, l_ref):
        pair = pl.program_id(1)
        mask_p, mask_c = _band_masks()
        for sub in range(2):
            rows = pl.ds(sub * QBLK, QBLK)
            first = jnp.where(pair > 0, 0.0, NEG) if sub == 0 else 0.0
            bias = jnp.concatenate([jnp.where(mask_p, 0.0, NEG) + first, jnp.where(mask_c, 0.0, NEG)], axis=1)
            k_prev = (lambda sl: kp_ref[:, sl]) if sub == 0 else (lambda sl: kc_ref[pl.ds(0, QBLK), sl])
            v_prev = (lambda sl: vp_ref[:, sl]) if sub == 0 else (lambda sl: vc_ref[pl.ds(0, QBLK), sl])
            s = []
            for h in range(HEADS):
                sl = pl.ds(HD * h, HD)
                qh = q_ref[rows, sl]
                s.append(jnp.concatenate([_nt(qh, k_prev(sl)), _nt(qh, kc_ref[rows, sl])], axis=1))
            s = jnp.stack(s) + bias
            m = jnp.max(s, axis=2, keepdims=True)
            e = jnp.exp(s - m)
            den = jnp.sum(e, axis=2, keepdims=True)
            p = e.astype(BF16)
            inv = 1.0 / den
            lse = m + jnp.log(den)
            for h in range(HEADS):
                sl = pl.ds(HD * h, HD)
                o_ref[rows, sl] = (_nn(p[h, :, :QBLK], v_prev(sl)) + _nn(p[h, :, QBLK:], vc_ref[rows, sl])) * inv[h]
                l_ref[rows, sl] = jnp.broadcast_to(lse[h], (QBLK, HD))

    cur = pl.BlockSpec((2 * QBLK, AW), lambda r, n: (n, r))
    prev = pl.BlockSpec((QBLK, AW), lambda r, n: (jnp.maximum(2 * n - 1, 0), r))
    return pl.pallas_call(
        body, name=f"attn_fwd_d{d}", grid=(d, npair),
        in_specs=[cur, prev, cur, prev, cur], out_specs=[cur, cur],
        out_shape=[jax.ShapeDtypeStruct((L, d * AW), F32)] * 2,
        compiler_params=_cparams(("parallel", "parallel")),
    )(q, k, k, v, v)


def _attn_bwd(q, k, v, do, at, lse, d):
    L = q.shape[0]
    nb = L // QBLK

    def body(q0_ref, q1_ref, kp_ref, kc_ref, vp_ref, vc_ref, do0_ref, do1_ref, at0_ref, at1_ref,
             l0_ref, l1_ref, dq_ref, dk_ref, dv_ref):
        n = pl.program_id(1)
        mask_p, mask_c = _band_masks()
        prev_bias = jnp.where(mask_p, 0.0, NEG)
        bias = jnp.concatenate([prev_bias + jnp.where(n > 0, 0.0, NEG), jnp.where(mask_c, 0.0, NEG),
                                prev_bias + jnp.where(n < nb - 1, 0.0, NEG)], axis=1)
        s, dp, ls, dl, ops = [], [], [], [], []
        for h in range(HEADS):
            sl = pl.ds(HD * h, HD)
            one = pl.ds(HD * h, 1)
            q0, q1 = q0_ref[:, sl], q1_ref[:, sl]
            kp, kc, vp, vc = kp_ref[:, sl], kc_ref[:, sl], vp_ref[:, sl], vc_ref[:, sl]
            do0, do1 = do0_ref[:, sl], do1_ref[:, sl]
            do0b, do1b = do0.astype(BF16), do1.astype(BF16)
            s.append(jnp.concatenate([_nt(q0, kp), _nt(q0, kc), _nt(q1, kc)], axis=1))
            dp.append(jnp.concatenate([_nt(do0b, vp), _nt(do0b, vc), _nt(do1b, vc)], axis=1))
            dl0 = jnp.sum(do0 * at0_ref[:, sl], axis=1, keepdims=True)
            dl1 = jnp.sum(do1 * at1_ref[:, sl], axis=1, keepdims=True)
            dl.append(jnp.concatenate([jnp.broadcast_to(dl0, (QBLK, 2 * QBLK)), jnp.broadcast_to(dl1, (QBLK, QBLK))], axis=1))
            ls.append(jnp.concatenate([jnp.broadcast_to(l0_ref[:, one], (QBLK, 2 * QBLK)),
                                       jnp.broadcast_to(l1_ref[:, one], (QBLK, QBLK))], axis=1))
            ops.append((q0, q1, kp, kc, do0b, do1b))
        p = jnp.exp(jnp.stack(s) + bias - jnp.stack(ls))
        ds = (p * (jnp.stack(dp) - jnp.stack(dl))).astype(BF16)
        p = p.astype(BF16)
        for h in range(HEADS):
            sl = pl.ds(HD * h, HD)
            q0, q1, kp, kc, do0b, do1b = ops[h]
            dq_ref[:, sl] = (_nn(ds[h, :, :QBLK], kp) + _nn(ds[h, :, QBLK:2 * QBLK], kc)).astype(BF16)
            dv_ref[:, sl] = (_tn(p[h, :, QBLK:2 * QBLK], do0b) + _tn(p[h, :, 2 * QBLK:], do1b)).astype(BF16)
            dk_ref[:, sl] = (_tn(ds[h, :, QBLK:2 * QBLK], q0) + _tn(ds[h, :, 2 * QBLK:], q1)).astype(BF16)

    cur = pl.BlockSpec((QBLK, AW), lambda r, n: (n, r))
    prev = pl.BlockSpec((QBLK, AW), lambda r, n: (jnp.maximum(n - 1, 0), r))
    nxt = pl.BlockSpec((QBLK, AW), lambda r, n: (jnp.minimum(n + 1, nb - 1), r))
    return pl.pallas_call(
        body, name=f"attn_bwd_d{d}", grid=(d, nb),
        in_specs=[cur, nxt, prev, cur, prev, cur, cur, nxt, cur, nxt, cur, nxt], out_specs=[cur, cur, cur],
        out_shape=[jax.ShapeDtypeStruct((L, d * AW), BF16)] * 3,
        compiler_params=_cparams(("parallel", "parallel")),
    )(q, q, k, k, v, v, do, do, at, at, lse, lse)


def _attn_merge(outs, lses):
    tm = ROPE_TM

    def body(o1, o4, o16, l1, l4, l16, at_ref, ls_ref, at4, ls4, at16, ls16, *flat):
        so4, so16, sl4, sl16, sa, sl = _slab_groups(flat)
        _undilate(o4, so4, 4, tm)
        _undilate(o16, so16, 16, tm)
        _undilate(l4, sl4, 4, tm)
        _undilate(l16, sl16, 16, tm)
        for j in range(SLABS):
            cols = pl.ds(j * 128, 128)
            a, b, c = l1[:, cols], sl4[j][...], sl16[j][...]
            m = jnp.maximum(jnp.maximum(a, b), c)
            e1, e2, e3 = jnp.exp(a - m), jnp.exp(b - m), jnp.exp(c - m)
            s = e1 + e2 + e3
            inv = 1.0 / s
            attn = (e1 * inv) * o1[:, cols] + (e2 * inv) * so4[j][...] + (e3 * inv) * so16[j][...]
            lse = m + jnp.log(s)
            at_ref[:, cols] = attn
            ls_ref[:, cols] = lse
            sa[j][...] = attn
            sl[j][...] = lse
        _dilate(at4, sa, 4, tm)
        _dilate(at16, sa, 16, tm)
        _dilate(ls4, sl, 4, tm)
        _dilate(ls16, sl, 16, tm)

    specs = [_dil_spec(tm, d) for d in DILATIONS]
    tok = specs[0]
    return pl.pallas_call(
        body, name="attn_merge", grid=(T // tm,),
        in_specs=specs + specs, out_specs=[tok, tok, specs[1], specs[1], specs[2], specs[2]],
        out_shape=[jax.ShapeDtypeStruct((T, AW), F32)] * 2 + [jax.ShapeDtypeStruct((T // 4, 4 * AW), F32)] * 2
        + [jax.ShapeDtypeStruct((T // 16, 16 * AW), F32)] * 2,
        scratch_shapes=_slab_scratch(6, tm),
        compiler_params=_cparams(("parallel",)),
    )(*outs, *lses)


CONV_TM = 512
HALO = 8


def _conv_pre(ext, w, b):
    y = b + w[3] * ext
    for kk in range(1, CONV_K):
        y = y + w[3 - kk] * pltpu.roll(ext, kk, 0)
    return y


def _rows_to_block(rows, n, width):
    ri = lax.broadcasted_iota(jnp.int32, (n, width), 0)
    out = jnp.zeros((n, width), F32)
    for j, r in enumerate(rows):
        out = out + jnp.where(ri == j, r, 0.0)
    return out


def _conv_fwd(xbc, w, b):
    nblk = T // CONV_TM

    def body(x_ref, h_ref, w_ref, b_ref, o_ref):
        i = pl.program_id(0)
        halo = jnp.where(i > 0, h_ref[...], 0.0)
        ext = jnp.concatenate([halo, x_ref[...]], axis=0)
        y = _conv_pre(ext, [w_ref[pl.ds(j, 1), :] for j in range(CONV_K)], b_ref[...])[HALO:]
        o_ref[...] = y * _sigmoid(y)

    return pl.pallas_call(
        body, name="conv_fwd", grid=(nblk,),
        in_specs=[pl.BlockSpec((CONV_TM, CONV_CH), lambda i: (i, 0)),
                  pl.BlockSpec((HALO, CONV_CH), lambda i: (jnp.maximum(i * (CONV_TM // HALO) - 1, 0), 0)),
                  pl.BlockSpec((CONV_K, CONV_CH), lambda i: (0, 0)),
                  pl.BlockSpec((1, CONV_CH), lambda i: (0, 0))],
        out_specs=pl.BlockSpec((CONV_TM, CONV_CH), lambda i: (i, 0)),
        out_shape=jax.ShapeDtypeStruct((T, CONV_CH), F32),
        compiler_params=_cparams(("parallel",)),
    )(xbc, xbc, w, b)


def _conv_bwd(xbc, dact, ddt, w, b):
    nblk = T // CONV_TM
    per = CONV_TM // HALO

    def body(x_ref, xb_ref, xa_ref, g_ref, ga_ref, ddt_ref, w_ref, b_ref, dx_ref, dw_ref):
        i = pl.program_id(0)
        wv = [w_ref[pl.ds(j, 1), :] for j in range(CONV_K)]
        before = jnp.where(i > 0, xb_ref[...], 0.0)
        last = i == nblk - 1
        after = jnp.where(last, 0.0, xa_ref[...])
        g_after = jnp.where(last, 0.0, ga_ref[...])
        ext = jnp.concatenate([before, x_ref[...], after], axis=0)
        y = _conv_pre(ext, wv, b_ref[...])[HALO:]
        sg = _sigmoid(y)
        dy = jnp.concatenate([g_ref[...], g_after], axis=0) * (sg * (1.0 + y * (1.0 - sg)))
        n = CONV_TM + HALO
        dx = wv[3] * dy
        for kk in range(1, CONV_K):
            dx = dx + wv[3 - kk] * pltpu.roll(dy, n - kk, 0)
        dx_ref[:, pl.ds(0, CONV_CH)] = dx[:CONV_TM].astype(BF16)
        dx_ref[:, pl.ds(CONV_CH, DT_PAD)] = ddt_ref[...].astype(BF16)
        dyc = dy[:CONV_TM]
        rows = [jnp.sum(dyc * (pltpu.roll(ext, 3 - j, 0) if j < 3 else ext)[HALO:HALO + CONV_TM], axis=0, keepdims=True)
                for j in range(CONV_K)]
        rows.append(jnp.sum(dyc, axis=0, keepdims=True))
        part = _rows_to_block(rows, 8, CONV_CH)

        @pl.when(i == 0)
        def _():
            dw_ref[...] = jnp.zeros_like(dw_ref)
        dw_ref[...] += part

    blk = pl.BlockSpec((CONV_TM, CONV_CH), lambda i: (i, 0))
    hb = pl.BlockSpec((HALO, CONV_CH), lambda i: (jnp.maximum(i * per - 1, 0), 0))
    ha = pl.BlockSpec((HALO, CONV_CH), lambda i: (jnp.minimum((i + 1) * per, T // HALO - 1), 0))
    return pl.pallas_call(
        body, name="conv_bwd", grid=(nblk,),
        in_specs=[blk, hb, ha, blk, ha, pl.BlockSpec((CONV_TM, DT_PAD), lambda i: (i, 0)),
                  pl.BlockSpec((CONV_K, CONV_CH), lambda i: (0, 0)), pl.BlockSpec((1, CONV_CH), lambda i: (0, 0))],
        out_specs=[pl.BlockSpec((CONV_TM, CONV_CH + DT_PAD), lambda i: (i, 0)), pl.BlockSpec((8, CONV_CH), lambda i: (0, 0))],
        out_shape=[jax.ShapeDtypeStruct((T, CONV_CH + DT_PAD), BF16), jax.ShapeDtypeStruct((8, CONV_CH), F32)],
        compiler_params=_cparams(("arbitrary",)),
    )(xbc, xbc, xbc, dact, dact, ddt, w, b)


def _pick(mat, h):
    lane = lax.broadcasted_iota(jnp.int32, mat.shape, 1)
    return jnp.sum(jnp.where(lane == h, mat, 0.0), axis=1, keepdims=True)


def _heads(fn):
    return jnp.stack([fn(h) for h in range(HEADS)])


def _ssd_prep(dt_ref, bias_ref, alog_ref, dsk_ref, b_ref, c_ref, xs_ref, state_ref, cst):
    li = lax.broadcasted_iota(jnp.int32, (CHUNK, CHUNK), 0)
    si = lax.broadcasted_iota(jnp.int32, (CHUNK, CHUNK), 1)
    tri = li >= si
    dtp = dt_ref[...] + bias_ref[...]
    dt = _softplus(dtp)
    A = -jnp.exp(alog_ref[...])
    a = dt * A
    cs = jnp.dot(tri.astype(F32), a, precision=HIGHEST, preferred_element_type=F32)
    cst[...] = cs.T
    Bm = b_ref[...].astype(BF16)
    Cm = c_ref[...].astype(BF16)
    cb = _nt(Cm, Bm)
    dskv = dsk_ref[...]
    cs_col = _heads(lambda h: _pick(cs, h))
    cs_row = _heads(lambda h: cst[pl.ds(h, 1), :])
    dt_col = _heads(lambda h: _pick(dt, h))
    dsk_col = _heads(lambda h: _pick(dskv, h))
    lam = jnp.exp(jnp.where(tri, cs_col - cs_row, NEG))
    x = _heads(lambda h: xs_ref[:, pl.ds(HD * h, HD)])
    xdt = x * dt_col
    prev = _heads(lambda h: state_ref[pl.ds(HD * h, HD), :])
    lane = lax.broadcasted_iota(jnp.int32, (1, 1, CHUNK), 2)
    cl = jnp.sum(jnp.where(lane == CHUNK - 1, cs_row, 0.0), axis=2, keepdims=True)
    f = jnp.exp(cl - cs_col)
    return dict(li=li, si=si, dtp=dtp, dt=dt, A=A, Bm=Bm, Cm=Cm, cb=cb, cs_col=cs_col, dt_col=dt_col, dsk_col=dsk_col,
                lam=lam, x=x, xdt=xdt, prev=prev, cl=cl, f=f)


def _ssd_fwd(act, xbcdt, bias, alog, dsk, qkvz, attn, gs):
    nc = T // CHUNK

    def body(xs_ref, b_ref, c_ref, dt_ref, bias_ref, alog_ref, dsk_ref, z_ref, at_ref, gs_ref,
             y_ref, st_ref, cat_ref, state, cst):
        @pl.when(pl.program_id(0) == 0)
        def _():
            state[...] = jnp.zeros_like(state)
        st_ref[...] = state[...]
        s = _ssd_prep(dt_ref, bias_ref, alog_ref, dsk_ref, b_ref, c_ref, xs_ref, state, cst)
        Bm, Cm, prev = s["Bm"], s["Cm"], s["prev"]
        g = (s["cb"] * s["lam"]).astype(BF16)
        xdtb = s["xdt"].astype(BF16)
        prevb = prev.astype(BF16)
        y = _heads(lambda h: _nn(g[h], xdtb[h])) + _heads(lambda h: _nt(Cm, prevb[h])) * jnp.exp(s["cs_col"])
        y = y + s["dsk_col"] * s["x"]
        xf = (s["xdt"] * s["f"]).astype(BF16)
        new = prev * jnp.exp(s["cl"]) + _heads(lambda h: _tn(xf[h], Bm))
        for h in range(HEADS):
            y_ref[:, pl.ds(HD * h, HD)] = y[h]
            state[pl.ds(HD * h, HD), :] = new[h]
        z = z_ref[...]
        gi = y_ref[...] * (z * _sigmoid(z))
        cat_ref[:, pl.ds(0, AW)] = at_ref[...].astype(BF16)
        cat_ref[:, pl.ds(AW, AW)] = (gi * _rstd(gi) * gs_ref[...]).astype(BF16)

    vec = pl.BlockSpec((1, DT_PAD), lambda c: (0, 0))
    blk = pl.BlockSpec((CHUNK, AW), lambda c: (c, 0))
    return pl.pallas_call(
        body, name="ssd_fwd", grid=(nc,),
        in_specs=[blk, pl.BlockSpec((CHUNK, NS), lambda c: (c, 4)),
                  pl.BlockSpec((CHUNK, NS), lambda c: (c, 5)), pl.BlockSpec((CHUNK, DT_PAD), lambda c: (c, 6)),
                  vec, vec, vec, pl.BlockSpec((CHUNK, AW), lambda c: (c, 3)), blk, pl.BlockSpec((1, AW), lambda c: (0, 0))],
        out_specs=[blk, pl.BlockSpec((None, AW, NS), lambda c: (c, 0, 0)), pl.BlockSpec((CHUNK, D), lambda c: (c, 0))],
        out_shape=[jax.ShapeDtypeStruct((T, AW), F32), jax.ShapeDtypeStruct((nc, AW, NS), F32),
                   jax.ShapeDtypeStruct((T, D), BF16)],
        scratch_shapes=[pltpu.VMEM((AW, NS), F32), pltpu.VMEM((CHUNK, CHUNK), F32)],
        compiler_params=_cparams(("arbitrary",)),
    )(act, act, act, xbcdt, bias, alog, dsk, qkvz, attn, gs)


def _ssd_bwd(act, xbcdt, bias, alog, dsk, states, y_ssd, qkvz, dcat, gs):
    nc = T // CHUNK

    def body(xs_ref, b_ref, c_ref, dt_ref, bias_ref, alog_ref, dsk_ref, st_ref, y_ref, z_ref, dyn_ref, gs_ref,
             dact_ref, ddt_ref, par_ref, dz_ref, dgs_ref, dstate, cst, dy_ref):
        step = pl.program_id(0)

        @pl.when(step == 0)
        def _():
            dstate[...] = jnp.zeros_like(dstate)
            par_ref[...] = jnp.zeros_like(par_ref)
            dgs_ref[...] = jnp.zeros_like(dgs_ref)
        z, yv, dyn = z_ref[...], y_ref[...], dyn_ref[...]
        sg = _sigmoid(z)
        sz = z * sg
        gi = yv * sz
        rg = _rstd(gi)
        ng = gi * rg
        dgi = _rms_bwd(ng, rg, gs_ref[...], dyn)
        dy_ref[...] = dgi * sz
        dz_ref[...] = dgi * yv * (sg * (1.0 + z * (1.0 - sg)))
        dgs_ref[...] += _colsum(dyn * ng)
        s = _ssd_prep(dt_ref, bias_ref, alog_ref, dsk_ref, b_ref, c_ref, xs_ref, st_ref, cst)
        Bm, Cm, prev, lam, x, xdt, f, cl = s["Bm"], s["Cm"], s["prev"], s["lam"], s["x"], s["xdt"], s["f"], s["cl"]
        lane = lax.broadcasted_iota(jnp.int32, (1, DT_PAD), 1)
        row = lax.broadcasted_iota(jnp.int32, (1, CHUNK, 1), 1)
        g = s["cb"] * lam
        gb, xdtb, prevb = g.astype(BF16), xdt.astype(BF16), prev.astype(BF16)
        dy = _heads(lambda h: dy_ref[:, pl.ds(HD * h, HD)])
        dyb = dy.astype(BF16)
        dnew = _heads(lambda h: dstate[pl.ds(HD * h, HD), :])
        dnewb = dnew.astype(BF16)
        E = jnp.exp(s["cs_col"])
        ecl = jnp.exp(cl)
        dG = _heads(lambda h: _nt(dyb[h], xdtb[h]))
        dxdt = _heads(lambda h: _tn(gb[h], dyb[h]))
        Yo = _heads(lambda h: _nt(Cm, prevb[h]))
        W = _heads(lambda h: _nt(Bm, dnewb[h]))
        dcb = jnp.sum(dG * lam, axis=0)
        Mm = dG * g
        col_sums = jnp.sum(Mm, axis=1, keepdims=True)
        dYo = (dy * E).astype(BF16)
        dxdt = dxdt + W * f
        dF = jnp.sum(W * xdt, axis=2, keepdims=True) * f
        dcl = jnp.sum(dnew * prev, axis=(1, 2), keepdims=True) * ecl + jnp.sum(dF, axis=1, keepdims=True)
        dcs = (jnp.sum(Mm, axis=2, keepdims=True) + jnp.sum(dy * Yo, axis=2, keepdims=True) * E - dF
               + jnp.where(row == CHUNK - 1, dcl, 0.0))
        ddt_x = jnp.sum(dxdt * x, axis=2, keepdims=True)
        dD = jnp.sum(dy * x, axis=(1, 2), keepdims=True)
        dx = s["dsk_col"] * dy + dxdt * s["dt_col"]
        xfb = (xdt * f).astype(BF16)
        dprev = _heads(lambda h: _tn(dYo[h], Cm)) + dnew * ecl
        dcbb = dcb.astype(BF16)
        dC = _nn(dcbb, Bm)
        dB = _tn(dcbb, Cm)
        dcs_mat = -_rows_to_block([col_sums[h] for h in range(HEADS)], CHUNK, CHUNK).T
        ddt_mat = jnp.zeros((CHUNK, DT_PAD), F32)
        dD_row = jnp.zeros((1, DT_PAD), F32)
        for h in range(HEADS):
            sl = pl.ds(HD * h, HD)
            dC = dC + _nn(dYo[h], prevb[h])
            dB = dB + _nn(xfb[h], dnewb[h])
            dcs_mat = dcs_mat + jnp.where(lane == h, dcs[h], 0.0)
            ddt_mat = ddt_mat + jnp.where(lane == h, ddt_x[h], 0.0)
            dD_row = dD_row + jnp.where(lane == h, dD[h], 0.0)
            dact_ref[:, sl] = dx[h]
            dstate[sl, :] = dprev[h]
        dact_ref[:, pl.ds(AW, NS)] = dB
        dact_ref[:, pl.ds(AW + NS, NS)] = dC
        da = jnp.dot((s["li"] <= s["si"]).astype(F32), dcs_mat, precision=HIGHEST, preferred_element_type=F32)
        ddtp = jnp.where(lane < HEADS, (ddt_mat + da * s["A"]) * _sigmoid(s["dtp"]), 0.0)
        ddt_ref[...] = ddtp
        dalog = jnp.where(lane < HEADS, jnp.sum(da * s["dt"], axis=0, keepdims=True) * s["A"], 0.0)
        par_ref[...] += _rows_to_block([jnp.sum(ddtp, axis=0, keepdims=True), dalog, dD_row], 8, DT_PAD)

    vec = pl.BlockSpec((1, DT_PAD), lambda c: (0, 0))
    rev = lambda c: nc - 1 - c
    return pl.pallas_call(
        body, name="ssd_bwd", grid=(nc,),
        in_specs=[pl.BlockSpec((CHUNK, AW), lambda c: (rev(c), 0)), pl.BlockSpec((CHUNK, NS), lambda c: (rev(c), 4)),
                  pl.BlockSpec((CHUNK, NS), lambda c: (rev(c), 5)), pl.BlockSpec((CHUNK, DT_PAD), lambda c: (rev(c), 6)),
                  vec, vec, vec,
                  pl.BlockSpec((None, AW, NS), lambda c: (rev(c), 0, 0)), pl.BlockSpec((CHUNK, AW), lambda c: (rev(c), 0)),
                  pl.BlockSpec((CHUNK, AW), lambda c: (rev(c), 3)), pl.BlockSpec((CHUNK, AW), lambda c: (rev(c), 1)),
                  pl.BlockSpec((1, AW), lambda c: (0, 0))],
        out_specs=[pl.BlockSpec((CHUNK, CONV_CH), lambda c: (rev(c), 0)), pl.BlockSpec((CHUNK, DT_PAD), lambda c: (rev(c), 0)),
                   pl.BlockSpec((8, DT_PAD), lambda c: (0, 0)), pl.BlockSpec((CHUNK, AW), lambda c: (rev(c), 0)),
                   pl.BlockSpec((1, AW), lambda c: (0, 0))],
        out_shape=[jax.ShapeDtypeStruct((T, CONV_CH), F32), jax.ShapeDtypeStruct((T, DT_PAD), F32),
                   jax.ShapeDtypeStruct((8, DT_PAD), F32), jax.ShapeDtypeStruct((T, AW), F32),
                   jax.ShapeDtypeStruct((1, AW), F32)],
        scratch_shapes=[pltpu.VMEM((AW, NS), F32), pltpu.VMEM((CHUNK, CHUNK), F32), pltpu.VMEM((CHUNK, AW), F32)],
        compiler_params=_cparams(("arbitrary",)),
    )(act, act, act, xbcdt, bias, alog, dsk, states, y_ssd, qkvz, dcat, gs)


def _place():
    return lax.axis_index("x"), lax.axis_index("y"), lax.axis_index("c")


def _slot(px, py, pc):
    return 4 * px + 2 * py + pc


SLAB_ROWS = 24


def _slab_pack(parts, name):
    n = len(parts)

    def body(*refs):
        slab = refs[n]
        slab[...] = jnp.zeros_like(slab)
        for ref, (arr, row) in zip(refs[:n], parts):
            slab[pl.ds(row, arr.shape[0]), pl.ds(0, arr.shape[1])] = ref[...]

    vm = pl.BlockSpec(memory_space=pltpu.VMEM)
    return pl.pallas_call(
        body, name=name, in_specs=[vm] * n, out_specs=vm, out_shape=jax.ShapeDtypeStruct((SLAB_ROWS, D), F32),
    )(*[a for a, _ in parts])


_HBM = pl.BlockSpec(memory_space=pltpu.HBM)
_SEM = pl.BlockSpec(memory_space=pltpu.SEMAPHORE)
_EFFECT = pltpu.SideEffectType.DATAFLOW_SIDE_EFFECTING


def _peers(x, y, c):
    out = []
    for kk in range(1, N_DEV):
        fx, fy, fc = kk >> 2 & 1, kk >> 1 & 1, kk & 1
        out.append((1 - x if fx else x, 1 - y if fy else y, 1 - c if fc else c))
    return out


def _send_start(src, per_peer, name, dep):
    (handles, token) = _send_start_many([src], per_peer, name, dep)
    return handles, token


def _send_start_many(srcs, per_peer, name, dep):
    n = len(srcs)

    def body(*refs):
        src_refs, land_refs = refs[:n], refs[n:2 * n]
        send_sems, recv_sems = refs[2 * n + 1], refs[2 * n + 2]
        token = refs[-1]
        x, y, c = _place()
        mine = _slot(x, y, c)
        for a in range(n):
            for kk, peer in enumerate(_peers(x, y, c)):
                pltpu.make_async_remote_copy(
                    src_ref=src_refs[a].at[_slot(*peer)] if per_peer else src_refs[a], dst_ref=land_refs[a].at[mine],
                    send_sem=send_sems.at[a * (N_DEV - 1) + kk], recv_sem=recv_sems.at[a * (N_DEV - 1) + kk],
                    device_id=peer, device_id_type=MESH).start()
        token[...] = jnp.zeros_like(token)

    lands = [lax.empty((N_DEV,) + tuple(s.shape[1:] if per_peer else s.shape), s.dtype) for s in srcs]
    hbm = lambda t: pltpu.with_memory_space_constraint(t, pltpu.HBM)
    outs = pl.pallas_call(
        body, name=name,
        out_shape=(pltpu.SemaphoreType.DMA((n * (N_DEV - 1),)), pltpu.SemaphoreType.DMA((n * (N_DEV - 1),)),
                   *[pltpu.HBM(s.shape, s.dtype) for s in srcs], *[pltpu.HBM(l.shape, l.dtype) for l in lands],
                   jax.ShapeDtypeStruct((8, 128), F32)),
        in_specs=(*[_HBM] * (2 * n), _ANY),
        out_specs=(_SEM, _SEM, *[_HBM] * (2 * n), pl.BlockSpec(memory_space=pltpu.VMEM)),
        input_output_aliases={i: 2 + i for i in range(2 * n)},
        compiler_params=pltpu.CompilerParams(has_side_effects=_EFFECT),
    )(*[hbm(s) for s in srcs], *[hbm(l) for l in lands], dep)
    return (outs[0], outs[1], list(outs[2:2 + n]), list(outs[2 + n:2 + 2 * n])), outs[-1]


def _send_wait(handles, after, name):
    srcs, lands = _send_wait_many(handles, after, name)
    return srcs[0], lands[0]


def _send_wait_many(handles, after, name):
    send_sems, recv_sems, src_thrus, land_thrus = handles
    n = len(src_thrus)

    def body(*refs):
        land_refs = refs[n:2 * n]
        send_sems, recv_sems = refs[2 * n], refs[2 * n + 1]
        me = _place()
        for a in range(n):
            for kk in range(N_DEV - 1):
                cp = pltpu.make_async_remote_copy(
                    src_ref=land_refs[a].at[0], dst_ref=land_refs[a].at[0],
                    send_sem=send_sems.at[a * (N_DEV - 1) + kk], recv_sem=recv_sems.at[a * (N_DEV - 1) + kk],
                    device_id=me, device_id_type=MESH)
                cp.wait_send()
                cp.wait_recv()

    both = list(src_thrus) + list(land_thrus)
    outs = pl.pallas_call(
        body, name=name,
        out_shape=tuple(pltpu.HBM(t.shape, t.dtype) for t in both),
        in_specs=(*[_HBM] * (2 * n), _SEM, _SEM, _ANY), out_specs=tuple([_HBM] * (2 * n)),
        input_output_aliases={i: i for i in range(2 * n)},
        compiler_params=pltpu.CompilerParams(has_side_effects=_EFFECT),
    )(*both, send_sems, recv_sems, after)
    return list(outs[:n]), list(outs[n:])


def _sum_slots(land, name):
    _, R, C = land.shape
    tm = R if R <= 512 else 512

    def body(x_ref, o_ref):
        acc = x_ref[0].astype(F32)
        for j in range(1, N_DEV):
            acc = acc + x_ref[j].astype(F32)
        o_ref[...] = acc

    return pl.pallas_call(
        body, name=name, grid=(R // tm,),
        in_specs=[pl.BlockSpec((N_DEV, tm, C), lambda i: (0, i, 0))], out_specs=pl.BlockSpec((tm, C), lambda i: (i, 0)),
        out_shape=jax.ShapeDtypeStruct((R, C), F32), compiler_params=_cparams(("parallel",)),
    )(land)


def _adam_math(w, g, m, v):
    m2 = ADAM_B1 * m + (1.0 - ADAM_B1) * g
    v2 = ADAM_B2 * v + (1.0 - ADAM_B2) * (g * g)
    m_hat = m2 / (1.0 - ADAM_B1 ** ADAM_STEP)
    v_hat = v2 / (1.0 - ADAM_B2 ** ADAM_STEP)
    delta = -ADAM_LR * (m_hat / (jnp.sqrt(v_hat) + ADAM_EPS) + ADAM_WD * w)
    return delta, m2, v2


def _adamw(w, g, m, v, name):
    R, C = w.shape
    tm = R if R <= 512 else 256
    return _rowwise(lambda w, g, m, v: (_adam_math(w, g, m, v), ()), [w, g, m, v], [], [(C, F32)] * 3, [], tm=tm, name=name)


def _adamw_small(slab, slab_rows, g_conv_w, ws, ms, vs):
    n = len(ws)

    def body(*refs):
        slab_ref, gc_ref = refs[0], refs[1]
        w_refs, m_refs, v_refs = refs[2:2 + n], refs[2 + n:2 + 2 * n], refs[2 + 2 * n:2 + 3 * n]
        outs = refs[2 + 3 * n:]
        loss_ref = outs[0]
        g_out, d_out, m_out, v_out = (outs[1 + i * n:1 + (i + 1) * n] for i in range(4))
        loss_ref[...] = jnp.sum(slab_ref[pl.ds(6, 1), :], axis=1, keepdims=True)
        for i in range(n):
            g = gc_ref[...] if i == n - 1 else slab_ref[pl.ds(slab_rows[i], 1), pl.ds(0, ws[i].shape[1])]
            d, m2, v2 = _adam_math(w_refs[i][...], g, m_refs[i][...], v_refs[i][...])
            g_out[i][...] = g
            d_out[i][...] = d
            m_out[i][...] = m2
            v_out[i][...] = v2

    vm = pl.BlockSpec(memory_space=pltpu.VMEM)
    shapes = [jax.ShapeDtypeStruct(w.shape, F32) for w in ws]
    outs = pl.pallas_call(
        body, name="adamw_small", in_specs=[vm] * (2 + 3 * n), out_specs=[vm] * (1 + 4 * n),
        out_shape=[jax.ShapeDtypeStruct((1, 1), F32)] + shapes * 4,
    )(slab, g_conv_w, *ws, *ms, *vs)
    return outs[0], outs[1:1 + n], outs[1 + n:1 + 2 * n], outs[1 + 2 * n:1 + 3 * n], outs[1 + 3 * n:]


SMALL = ["norm_mix_pre", "norm_mix_post", "norm_mlp_pre", "norm_mlp_post", "norm_ple_post",
         "conv_b", "ssd_norm_g", "dt_bias", "a_log", "d_skip"]


def _pad_row(v, width=D):
    return jnp.pad(v, ((0, 0), (0, width - v.shape[1])))


def kernel(x, p, positions, norm_mix_pre, norm_mix_post, w_in, conv_w, conv_b, dt_bias, a_log, d_skip, ssd_norm_g, w_out, norm_mlp_pre, norm_mlp_post, w_up, w_down, w_ple_gate, w_ple_proj, norm_ple_post, loss_target, m_norm_mix_pre, m_norm_mix_post, m_w_in, m_conv_w, m_conv_b, m_dt_bias, m_a_log, m_d_skip, m_ssd_norm_g, m_w_out, m_norm_mlp_pre, m_norm_mlp_post, m_w_up, m_w_down, m_w_ple_gate, m_w_ple_proj, m_norm_ple_post, v_norm_mix_pre, v_norm_mix_post, v_w_in, v_conv_w, v_conv_b, v_dt_bias, v_a_log, v_d_skip, v_ssd_norm_g, v_w_out, v_norm_mlp_pre, v_norm_mlp_post, v_w_up, v_w_down, v_w_ple_gate, v_w_ple_proj, v_norm_ple_post):
    args = dict(locals())
    x2, p2, tgt = x[0], p[0, 0], loss_target[0]
    g1, g2, g3, g4, g5 = norm_mix_pre, norm_mix_post, norm_mlp_pre, norm_mlp_post, norm_ple_post

    me = _slot(*_place())
    pack_in = jnp.pad(w_in[0].T, ((0, W_IN_SHARD_PAD - W_IN_SHARD), (0, 0))).astype(BF16)
    rest = [w_out[0].astype(BF16), w_up[0].T.astype(BF16), w_down[0].astype(BF16), w_ple_gate[0].astype(BF16),
            w_ple_proj[0].T.reshape(32, D).astype(BF16)]
    conv_pack = jnp.pad(conv_w[0], ((0, 4), (0, 32)))
    in_handles, tok_in0 = _send_start_many([pack_in, conv_pack], False, "gather_in_start", g1)

    inv_freq = ROPE_THETA ** (-jnp.arange(HD // 2, dtype=F32) * 2.0 / HD)
    pos = positions[0] + tok_in0[0, 0].astype(jnp.int32)
    ang = pos.astype(F32)[:, None] * inv_freq
    cos, sin = jnp.cos(ang), jnp.sin(ang)
    cos128 = jnp.concatenate([cos, cos, cos, cos], axis=1)
    sin128 = jnp.concatenate([-sin, sin, -sin, sin], axis=1)

    bias_w, alog_w, dsk_w = _pad_row(dt_bias, DT_PAD), _pad_row(a_log, DT_PAD), _pad_row(d_skip, DT_PAD)

    (u1,) = _rowwise(lambda a, g: ((a * _rstd(a) * g,), ()), [x2], [g1], [(D, BF16)], [], tm=512, name="norm_x",
                     deps=[cos128, sin128])
    p2b = p2.astype(BF16)

    in_back, in_land = _send_wait_many(in_handles, u1, "gather_in_wait")
    gin = lax.dynamic_update_slice(in_land[0], in_back[0][None], (me, 0, 0))
    gconv = lax.dynamic_update_slice(in_land[1], in_back[1][None], (me, 0, 0))
    rest_handles, tok_rest = _send_start_many(rest, False, "gather_rest_start", gconv)
    w_inT = gin[:, :W_IN_SHARD].reshape(IN_W, D)
    w_qkvzT = w_inT[:4 * AW]
    w_xbcdtT = jnp.pad(w_inT[4 * AW:], ((0, DT_PAD - HEADS), (0, 0)))
    conv_full = gconv[:, :CONV_K, :96].transpose(1, 0, 2).reshape(CONV_K, CONV_CH)
    qkvz = _mm(u1, w_qkvzT, tb=True, tm=512, tn=2048, tk=1024, name="proj_qkvz", deps=[tok_rest])
    xbcdt = _mm(u1, w_xbcdtT, tb=True, tm=512, tn=896, tk=1024, name="proj_xbcdt")

    qkv = _rope_fwd(qkvz, cos128, sin128)
    qkv = [qkv[3 * i:3 * i + 3] for i in range(len(DILATIONS))]
    outs, lses = [], []
    for d, (qd, kd, vd) in zip(DILATIONS, qkv):
        o, l = _attn_fwd(qd, kd, vd, d)
        outs.append(o)
        lses.append(l)
    attn, lse, attn4, lse4, attn16, lse16 = _attn_merge(outs, lses)

    act = _conv_fwd(xbcdt, conv_full, conv_b)
    y_ssd, states, cat = _ssd_fwd(act, xbcdt, bias_w, alog_w, dsk_w, qkvz, attn, ssd_norm_g)


    rest_back, landed = _send_wait_many(rest_handles, cat, "gather_rest_wait")
    landed = [lax.dynamic_update_slice(l, b[None], (me, 0, 0)) for l, b in zip(landed, rest_back)]
    w_o, w_upT, w_dn, w_gate = landed[0].reshape(D, D), landed[1].reshape(DFF, D), landed[2].reshape(DFF, D), landed[3].reshape(D, D)
    w_projT = landed[4].reshape(D, PLE)

    def post1(mm, xx, ga):
        h = xx + mm * _rstd(mm) * ga
        return (mm, h, _rstd(h)), ()
    mix, h1, r3 = _mm_rows(post1, [(cat, w_o, False)], [x2], [g2], [(D, F32), (D, F32), (1, F32)], [], tm=512,
                           name="mix_out")

    a_up, ff, u2, h2, h2b = _mlp_fwd(h1, r3, g3, w_upT, w_dn, g4)
    relu2 = lambda a: jnp.square(jnp.maximum(a.astype(F32), 0.0))

    def final(gpre, ppv, hh, tg, g):
        sg = _sigmoid(gpre)
        ple = ppv * sg
        r = _rstd(ple)
        n = ple * r
        h3 = hh + n * g
        e = h3 - tg
        dh3 = e * (1.0 / D)
        dple = _rms_bwd(n, r, g, dh3)
        return (dh3, dple * sg, dple * ppv * sg * (1.0 - sg)), (_colsum(dh3 * n), _colsum(0.5 * e * e * (1.0 / D)))
    dh3, dpp, dgp, dg5, loss_vec = _mm_rows(final, [(h2b, w_gate, False), (p2b, w_projT, True)], [h2, tgt], [g5],
                                            [(D, F32), (D, BF16), (D, BF16)], [(1, D), (1, D)], tm=512, name="ple_loss")

    gw_projT = _mm(dpp, p2b, ta=True, tm=512, tn=256, tk=T, out_dtypes=(BF16,), name="gw_ple_proj")
    gw_gate = _mm(h2b, dgp, ta=True, tm=512, tn=1024, tk=T, out_dtypes=(BF16,), name="gw_ple_gate")
    rs_proj, tok_proj = _send_start(gw_projT.reshape(N_DEV, 32, D), True, "rs_start_w_proj", g1)
    rs_gate, tok_gate = _send_start(gw_gate.reshape(N_DEV, 128, D), True, "rs_start_w_gate", g1)
    def bwd_mlp_post(dg_, d3, f, g):
        dh2 = d3 + dg_
        r = _rstd(f)
        n = f * r
        return (dh2, _rms_bwd(n, r, g, dh2)), (_colsum(dh2 * n),)
    dh2, dff, dg4 = _mm_rows(bwd_mlp_post, [(dgp, w_gate, True)], [dh3, ff], [g4], [(D, F32), (D, BF16)], [(1, D)],
                             tm=512, name="bwd_ple_gate", deps=[tok_proj, tok_gate])

    gw_dn = _mm(a_up, dff, ta=True, tm=512, tn=1024, tk=T, a_pre=relu2, out_dtypes=(BF16,), name="gw_mlp_down")
    rs_dn, tok_dn = _send_start(gw_dn.reshape(N_DEV, 512, D), True, "rs_start_w_down", g1)
    da_up, du2 = _mlp_dx(dff, a_up, w_upT, w_dn, tok_dn)
    gw_upT = _mm(da_up, u2, ta=True, tm=512, tn=1024, tk=T, out_dtypes=(BF16,), name="gw_mlp_up")
    rs_up, tok_up = _send_start(gw_upT.reshape(N_DEV, 512, D), True, "rs_start_w_up", g1)

    def bwd_mix_post(d2, du, hh, rr, mm, ga, gb):
        n3 = hh * rr
        dh1 = d2 + _rms_bwd(n3, rr, gb, du)
        r = _rstd(mm)
        n2 = mm * r
        return (dh1, _rms_bwd(n2, r, ga, dh1)), (_colsum(du * n3), _colsum(dh1 * n2))
    dh1, dmix, dg3, dg2 = _rowwise(bwd_mix_post, [dh2, du2, h1, r3, mix], [g2, g3], [(D, F32), (D, BF16)],
                                   [(1, D), (1, D)], tm=512, name="bwd_post_mix", deps=[tok_up])

    gw_o = _mm(cat, dmix, ta=True, tm=512, tn=1024, tk=T, out_dtypes=(BF16,), name="gw_out")
    rs_o, tok_o = _send_start(gw_o.reshape(N_DEV, 128, D), True, "rs_start_w_out", g1)
    dcat = _mm(dmix, w_o, tb=True, tm=512, tn=1024, tk=1024, name="dx_out", deps=[tok_o])

    dact, ddtw, ssd_par, dz, dgs = _ssd_bwd(act, xbcdt, bias_w, alog_w, dsk_w, states, y_ssd, qkvz, dcat, ssd_norm_g)
    dxbcdt, conv_par = _conv_bwd(xbcdt, dact, ddtw, conv_full, conv_b)

    dattn4, dattn16 = _dilate_cols(dcat, 0)
    qkv_grads = [_attn_bwd(*qkv[0], dcat, attn, lse, 1),
                 _attn_bwd(*qkv[1], dattn4, attn4, lse4, 4),
                 _attn_bwd(*qkv[2], dattn16, attn16, lse16, 16)]
    dqkvz = _rope_bwd(qkv_grads, dz, cos128, sin128)

    gw_qkvzT = _mm(dqkvz, u1, ta=True, tm=512, tn=1024, tk=T, out_dtypes=(BF16,), name="gw_qkvz")
    gw_xbcdtT = _mm(dxbcdt, u1, ta=True, tm=896, tn=1024, tk=T, out_dtypes=(BF16,), name="gw_xbcdt")
    gw_inT = jnp.concatenate([gw_qkvzT, gw_xbcdtT], axis=0)[:IN_W]
    gw_inT = jnp.pad(gw_inT.reshape(N_DEV, W_IN_SHARD, D), ((0, 0), (0, W_IN_SHARD_PAD - W_IN_SHARD), (0, 0)))
    rs_in, tok_in = _send_start(gw_inT, True, "rs_start_w_in", g1)

    def bwd_in(ua, ub, d1, xx, g):
        rr = _rstd(xx)
        n = xx * rr
        du = ua + ub
        return (d1 + _rms_bwd(n, rr, g, du),), (_colsum(du * n),)
    grad_x, dg1 = _mm_rows(bwd_in, [(dqkvz, w_qkvzT, False), (dxbcdt, w_xbcdtT, False)], [dh1, x2], [g1],
                           [(D, F32)], [(1, D)], tm=512, name="bwd_in_proj", deps=[tok_in])

    my_slab = _slab_pack([(dg1, 0), (dg2, 1), (dg3, 2), (dg4, 3), (dg5, 4), (dgs, 5), (loss_vec, 6),
                          (conv_par, 8), (ssd_par, 16)], "slab_pack")
    slab_handles, tok_slab = _send_start_many([my_slab], False, "slab_start", g1)

    def scatter_finish(handles, nm, after):
        part, land = _send_wait(handles, after, "rs_wait_" + nm)
        own = lax.dynamic_slice(part, (me, 0, 0), (1,) + part.shape[1:])
        return _sum_slots(lax.dynamic_update_slice(land, own, (me, 0, 0)), "rs_sum_" + nm)
    g_out = scatter_finish(rs_o, "w_out", tok_slab)
    g_upT = scatter_finish(rs_up, "w_up", tok_slab)
    g_dn = scatter_finish(rs_dn, "w_down", tok_slab)
    g_gate = scatter_finish(rs_gate, "w_gate", tok_slab)
    g_projT = scatter_finish(rs_proj, "w_proj", tok_slab)

    grads = {
        "w_out": g_out[None], "w_up": g_upT.T[None], "w_down": g_dn[None],
        "w_ple_gate": g_gate[None], "w_ple_proj": g_projT.reshape(128, PLE).T[None],
    }
    delta, new_m, new_v = {}, {}, {}
    for nme in ["w_out", "w_up", "w_down", "w_ple_gate", "w_ple_proj", "w_in"]:
        if nme == "w_in":
            g_inT = scatter_finish(rs_in, "w_in", delta["w_down"])
            grads["w_in"] = g_inT[:W_IN_SHARD].T[None]
        dl, mm_, vv_ = _adamw(args[nme][0], grads[nme][0], args["m_" + nme][0], args["v_" + nme][0], "adamw_" + nme)
        delta[nme], new_m[nme], new_v[nme] = dl[None], mm_[None], vv_[None]

    slab_back, slab_land = _send_wait_many(slab_handles, delta["w_in"], "slab_wait")
    slab = _sum_slots(lax.dynamic_update_slice(slab_land[0], slab_back[0][None], (me, 0, 0)), "slab_sum")
    g_conv_w = lax.dynamic_slice(slab[8:12, :CONV_CH], (0, me * 96), (CONV_K, 96))
    small_names = SMALL + ["conv_w"]
    small_rows = [0, 1, 2, 3, 4, 12, 5, 16, 17, 18, None]
    pick = lambda prefix: [args[prefix + nme] for nme in SMALL] + [args[prefix + "conv_w"][0]]
    loss11, g_s, d_s, m_s, v_s = _adamw_small(slab, small_rows, g_conv_w, pick(""), pick("m_"), pick("v_"))
    loss = loss11[0, 0]
    for i, nme in enumerate(small_names):
        lead = (lambda t: t[None]) if nme == "conv_w" else (lambda t: t)
        grads[nme], delta[nme], new_m[nme], new_v[nme] = lead(g_s[i]), lead(d_s[i]), lead(m_s[i]), lead(v_s[i])

    order = ["norm_mix_pre", "norm_mix_post", "w_in", "conv_w", "conv_b", "dt_bias", "a_log", "d_skip", "ssd_norm_g",
             "w_out", "norm_mlp_pre", "norm_mlp_post", "w_up", "w_down", "w_ple_gate", "w_ple_proj", "norm_ple_post"]
    return (loss, grad_x[None], *[grads[n] for n in order], *[delta[n] for n in order],
            *[new_m[n] for n in order], *[new_v[n] for n in order])
```

```python
import functools
import math

import jax
import jax.numpy as jnp
from jax import lax
from jax.experimental import pallas as pl
from jax.experimental.pallas import tpu as pltpu

F32 = jnp.float32
BF16 = jnp.bfloat16
MESH = pl.DeviceIdType.MESH
HIGHEST = lax.Precision.HIGHEST

N_DEV = 8
T = 4096
D = 1024
HEADS = 8
HD = 64
AW = 512
NS = 128
CONV_K = 4
CONV_CH = 768
CHUNK = 128
DFF = 4096
PLE = 256
EPS = 1e-6
ROPE_THETA = 10000.0
DILATIONS = (1, 4, 16)
QBLK = 128
NEG = -1e30
IN_W = 2824
W_IN_SHARD = 353
W_IN_SHARD_PAD = 384
DT_PAD = 128

ADAM_LR, ADAM_B1, ADAM_B2, ADAM_EPS, ADAM_WD, ADAM_STEP = 0.001, 0.9, 0.999, 1e-08, 0.01, 10

VMEM_LIMIT = 56 * 1024 * 1024


_ANY = pl.BlockSpec(memory_space=pl.ANY)


def _cparams(sem=None):
    return pltpu.CompilerParams(dimension_semantics=sem, vmem_limit_bytes=VMEM_LIMIT)


def _dot(a, b, ca, cb, precision=None):
    return lax.dot_general(a, b, (((ca,), (cb,)), ((), ())), preferred_element_type=F32, precision=precision)


def _nn(a, b):
    return _dot(a, b, 1, 0)


def _nt(a, b):
    return _dot(a, b, 1, 1)


def _tn(a, b):
    return _dot(a, b, 0, 0)


def _sigmoid(x):
    return 1.0 / (1.0 + jnp.exp(-x))


def _softplus(x):
    return jnp.maximum(x, 0.0) + jnp.log(1.0 + jnp.exp(-jnp.abs(x)))


def _mm(a, b, *, ta=False, tb=False, tm, tn, tk, name,
        a_pre=None, a_rows=(), a_cols=(), b_pre=None, b_rows=(), b_cols=(),
        epi=None, epi_tiles=(), out_dtypes=(F32,), deps=()):
    if ta:
        K, M = a.shape
    else:
        M, K = a.shape
    if tb:
        N, K2 = b.shape
    else:
        K2, N = b.shape
    assert K == K2 and M % tm == 0 and N % tn == 0 and K % tk == 0, (name, a.shape, b.shape)
    nk = K // tk
    if ta:
        a_spec = pl.BlockSpec((tk, tm), lambda i, j, k: (k, i))
        a_row_specs = [pl.BlockSpec((tk, 1), lambda i, j, k: (k, 0)) for _ in a_rows]
        a_col_specs = [pl.BlockSpec((1, tm), lambda i, j, k: (0, i)) for _ in a_cols]
    else:
        a_spec = pl.BlockSpec((tm, tk), lambda i, j, k: (i, k))
        a_row_specs = [pl.BlockSpec((tm, 1), lambda i, j, k: (i, 0)) for _ in a_rows]
        a_col_specs = [pl.BlockSpec((1, tk), lambda i, j, k: (0, k)) for _ in a_cols]
    if tb:
        b_spec = pl.BlockSpec((tn, tk), lambda i, j, k: (j, k))
        b_row_specs = [pl.BlockSpec((tn, 1), lambda i, j, k: (j, 0)) for _ in b_rows]
        b_col_specs = [pl.BlockSpec((1, tk), lambda i, j, k: (0, k)) for _ in b_cols]
    else:
        b_spec = pl.BlockSpec((tk, tn), lambda i, j, k: (k, j))
        b_row_specs = [pl.BlockSpec((tk, 1), lambda i, j, k: (k, 0)) for _ in b_rows]
        b_col_specs = [pl.BlockSpec((1, tn), lambda i, j, k: (0, j)) for _ in b_cols]
    o_spec = pl.BlockSpec((tm, tn), lambda i, j, k: (i, j))
    na, nb, ne, no = len(a_rows) + len(a_cols), len(b_rows) + len(b_cols), len(epi_tiles), len(out_dtypes)

    def body(*refs):
        a_ref, b_ref = refs[0], refs[1]
        a_ex = refs[2:2 + na]
        b_ex = refs[2 + na:2 + na + nb]
        e_ex = refs[2 + na + nb:2 + na + nb + ne]
        first_out = 2 + na + nb + ne + len(deps)
        outs = refs[first_out:first_out + no]

        def finish(res):
            vals = epi(res, *[r[...] for r in e_ex]) if epi is not None else (res,)
            for o_ref, val in zip(outs, vals):
                o_ref[...] = val.astype(o_ref.dtype)

        at = a_ref[...]
        if a_pre is not None:
            at = a_pre(at, *[r[...] for r in a_ex])
        bt = b_ref[...]
        if b_pre is not None:
            bt = b_pre(bt, *[r[...] for r in b_ex])
        prod = _dot(at.astype(BF16), bt.astype(BF16), 0 if ta else 1, 1 if tb else 0)
        if nk == 1:
            finish(prod)
            return
        acc = refs[-1]
        k = pl.program_id(2)

        @pl.when(k == 0)
        def _():
            acc[...] = jnp.zeros_like(acc)
        acc[...] += prod

        @pl.when(k == nk - 1)
        def _():
            finish(acc[...])

    outs = pl.pallas_call(
        body, name=name,
        grid=(M // tm, N // tn, nk),
        in_specs=([a_spec, b_spec] + a_row_specs + a_col_specs + b_row_specs + b_col_specs + [o_spec] * ne
                  + [_ANY] * len(deps)),
        out_specs=[o_spec] * no,
        out_shape=[jax.ShapeDtypeStruct((M, N), dt) for dt in out_dtypes],
        scratch_shapes=[pltpu.VMEM((tm, tn), F32)] if nk > 1 else [],
        compiler_params=_cparams(("parallel", "parallel", "arbitrary")),
    )(a, b, *a_rows, *a_cols, *b_rows, *b_cols, *epi_tiles, *deps)
    return outs[0] if no == 1 else outs


MLP_TM = 1024
MLP_TC = 512


def _mlp_fwd(h, r, g, w_upT, w_dn, g_post):
    nc = DFF // MLP_TC

    def body(h_ref, r_ref, g_ref, wu_ref, wd_ref, gp_ref, a_ref, ff_ref, u_ref, ho_ref, hob_ref, acc, u_scr):
        c = pl.program_id(1)

        @pl.when(c == 0)
        def _():
            u = (h_ref[...] * r_ref[...] * g_ref[...]).astype(BF16)
            u_scr[...] = u
            u_ref[...] = u
            acc[...] = jnp.zeros_like(acc)
        a = _nt(u_scr[...], wu_ref[...])
        a_ref[...] = a.astype(BF16)
        acc[...] += _nn(jnp.square(jnp.maximum(a, 0.0)).astype(BF16), wd_ref[...])

        @pl.when(c == nc - 1)
        def _():
            f = acc[...]
            ff_ref[...] = f
            ho = h_ref[...] + f * _rstd(f) * gp_ref[...]
            ho_ref[...] = ho
            hob_ref[...] = ho.astype(BF16)

    row = pl.BlockSpec((MLP_TM, D), lambda i, c: (i, 0))
    wsp = pl.BlockSpec((MLP_TC, D), lambda i, c: (c, 0))
    vec = pl.BlockSpec((1, D), lambda i, c: (0, 0))
    return pl.pallas_call(
        body, name="mlp_fwd", grid=(T // MLP_TM, nc),
        in_specs=[row, pl.BlockSpec((MLP_TM, 1), lambda i, c: (i, 0)), vec, wsp, wsp, vec],
        out_specs=[pl.BlockSpec((MLP_TM, MLP_TC), lambda i, c: (i, c)), row, row, row, row],
        out_shape=[jax.ShapeDtypeStruct((T, DFF), BF16), jax.ShapeDtypeStruct((T, D), F32), jax.ShapeDtypeStruct((T, D), BF16),
                   jax.ShapeDtypeStruct((T, D), F32), jax.ShapeDtypeStruct((T, D), BF16)],
        scratch_shapes=[pltpu.VMEM((MLP_TM, D), F32), pltpu.VMEM((MLP_TM, D), BF16)],
        compiler_params=_cparams(("parallel", "arbitrary")),
    )(h, r, g, w_upT, w_dn, g_post)


def _mlp_dx(dff, a, w_upT, w_dn, dep):
    nc = DFF // MLP_TC

    def body(d_ref, a_ref, wu_ref, wd_ref, dep_ref, da_ref, du_ref, acc, d_scr):
        c = pl.program_id(1)

        @pl.when(c == 0)
        def _():
            d_scr[...] = d_ref[...].astype(BF16)
            acc[...] = jnp.zeros_like(acc)
        da = (_nt(d_scr[...], wd_ref[...]) * (2.0 * jnp.maximum(a_ref[...].astype(F32), 0.0))).astype(BF16)
        da_ref[...] = da
        acc[...] += _nn(da, wu_ref[...])

        @pl.when(c == nc - 1)
        def _():
            du_ref[...] = acc[...]

    row = pl.BlockSpec((MLP_TM, D), lambda i, c: (i, 0))
    wsp = pl.BlockSpec((MLP_TC, D), lambda i, c: (c, 0))
    chunk = pl.BlockSpec((MLP_TM, MLP_TC), lambda i, c: (i, c))
    return pl.pallas_call(
        body, name="mlp_dx", grid=(T // MLP_TM, nc),
        in_specs=[row, chunk, wsp, wsp, _ANY], out_specs=[chunk, row],
        out_shape=[jax.ShapeDtypeStruct((T, DFF), BF16), jax.ShapeDtypeStruct((T, D), F32)],
        scratch_shapes=[pltpu.VMEM((MLP_TM, D), F32), pltpu.VMEM((MLP_TM, D), BF16)],
        compiler_params=_cparams(("parallel", "arbitrary")),
    )(dff, a, w_upT, w_dn, dep)


def _rowwise(fn, rows, vecs, out_rows, out_sums, *, tm, name, deps=()):
    specs, arrs = [], []
    R = None
    for r in rows:
        if isinstance(r, tuple):
            arr, width, cb = r
            specs.append(pl.BlockSpec((tm, width), lambda i, cb=cb: (i, cb)))
        else:
            arr = r
            specs.append(pl.BlockSpec((tm, arr.shape[1]), lambda i: (i, 0)))
        R = arr.shape[0] if R is None else R
        assert arr.shape[0] == R, name
        arrs.append(arr)
    assert R % tm == 0, name
    for v in vecs:
        specs.append(pl.BlockSpec(v.shape, lambda i: (0, 0)))
        arrs.append(v)
    nr, nv, no, ns = len(rows), len(vecs), len(out_rows), len(out_sums)
    out_specs = [pl.BlockSpec((tm, w), lambda i: (i, 0)) for w, _ in out_rows]
    out_specs += [pl.BlockSpec(s, lambda i: (0, 0)) for s in out_sums]
    out_shape = [jax.ShapeDtypeStruct((R, w), dt) for w, dt in out_rows]
    out_shape += [jax.ShapeDtypeStruct(s, F32) for s in out_sums]

    nd = len(deps)

    def body(*refs):
        ins = [r[...] for r in refs[:nr + nv]]
        o_refs = refs[nr + nv + nd:nr + nv + nd + no]
        s_refs = refs[nr + nv + nd + no:]
        o_vals, s_vals = fn(*ins)
        for ref, val in zip(o_refs, o_vals):
            ref[...] = val.astype(ref.dtype)
        if ns:
            @pl.when(pl.program_id(0) == 0)
            def _():
                for ref in s_refs:
                    ref[...] = jnp.zeros_like(ref)
            for ref, val in zip(s_refs, s_vals):
                ref[...] += val

    outs = pl.pallas_call(
        body, name=name, grid=(R // tm,), in_specs=specs + [_ANY] * nd, out_specs=out_specs, out_shape=out_shape,
        compiler_params=_cparams(("arbitrary",) if ns else ("parallel",)),
    )(*arrs, *deps)
    return outs


def _mm_rows(fn, mats, rows, vecs, out_rows, out_sums, *, tm, name, deps=()):
    R = mats[0][0].shape[0]
    assert R % tm == 0, name
    specs, arrs = [], []
    for a, b, tb in mats:
        specs += [pl.BlockSpec((tm, a.shape[1]), lambda i: (i, 0)), pl.BlockSpec(b.shape, lambda i: (0, 0))]
        arrs += [a, b]
    for r in rows:
        specs.append(pl.BlockSpec((tm, r.shape[1]), lambda i: (i, 0)))
        arrs.append(r)
    for v in vecs:
        specs.append(pl.BlockSpec(v.shape, lambda i: (0, 0)))
        arrs.append(v)
    nm, nr, nv, nd, no, ns = len(mats), len(rows), len(vecs), len(deps), len(out_rows), len(out_sums)
    out_specs = [pl.BlockSpec((tm, w), lambda i: (i, 0)) for w, _ in out_rows]
    out_specs += [pl.BlockSpec(s, lambda i: (0, 0)) for s in out_sums]
    out_shape = [jax.ShapeDtypeStruct((R, w), dt) for w, dt in out_rows] + [jax.ShapeDtypeStruct(s, F32) for s in out_sums]

    def body(*refs):
        prods = [_dot(refs[2 * p][...].astype(BF16), refs[2 * p + 1][...].astype(BF16), 1, 1 if mats[p][2] else 0)
                 for p in range(nm)]
        ins = [r[...] for r in refs[2 * nm:2 * nm + nr + nv]]
        first_out = 2 * nm + nr + nv + nd
        o_refs, s_refs = refs[first_out:first_out + no], refs[first_out + no:]
        o_vals, s_vals = fn(*prods, *ins)
        for ref, val in zip(o_refs, o_vals):
            ref[...] = val.astype(ref.dtype)
        if ns:
            @pl.when(pl.program_id(0) == 0)
            def _():
                for ref in s_refs:
                    ref[...] = jnp.zeros_like(ref)
            for ref, val in zip(s_refs, s_vals):
                ref[...] += val

    return pl.pallas_call(
        body, name=name, grid=(R // tm,), in_specs=specs + [_ANY] * nd, out_specs=out_specs, out_shape=out_shape,
        compiler_params=_cparams(("arbitrary",) if ns else ("parallel",)),
    )(*arrs, *deps)


def _colsum(x):
    return jnp.sum(x, axis=0, keepdims=True)


def _rstd(x):
    return lax.rsqrt(jnp.mean(x * x, axis=-1, keepdims=True) + EPS)


def _rms_bwd(xn, r, g, dy):
    dn = dy * g
    return r * (dn - xn * jnp.mean(dn * xn, axis=-1, keepdims=True))


def _partner(t):
    lane = lax.broadcasted_iota(jnp.int32, t.shape, 1)
    up = pltpu.roll(t, 96, 1)
    down = pltpu.roll(t, 32, 1)
    return jnp.where((lane % 64) < 32, up, down)


SLABS = AW // 128


def _rows(r, n, d):
    return pl.ds(r, n, stride=d) if d > 1 else pl.ds(0, n)


def _undilate(src_ref, dst, d, tm):
    for r in range(d):
        for j in range(SLABS):
            dst[j][_rows(r, tm // d, d), :] = src_ref[:, pl.ds(r * AW + j * 128, 128)].astype(dst[j].dtype)


def _dilate(dst_ref, src, d, tm):
    for r in range(d):
        for j in range(SLABS):
            dst_ref[:, pl.ds(r * AW + j * 128, 128)] = src[j][_rows(r, tm // d, d), :].astype(dst_ref.dtype)


def _slab_scratch(n, tm):
    return [pltpu.VMEM((tm, 128), F32)] * (SLABS * n)


def _slab_groups(flat):
    return [flat[SLABS * i:SLABS * (i + 1)] for i in range(len(flat) // SLABS)]


def _slab_specs(tm, first):
    return [pl.BlockSpec((tm, 128), lambda i, j=j: (i, first + j)) for j in range(SLABS)]


def _dil_spec(tm, d):
    return pl.BlockSpec((tm // d, d * AW), lambda i: (i, 0))


ROPE_TM = 512


def _rope_fwd(qkvz, cos128, sin128):
    tm = ROPE_TM

    def body(*refs):
        q_refs, k_refs, v_refs = refs[0:4], refs[4:8], refs[8:12]
        c_ref, s_ref = refs[12], refs[13]
        outs = refs[14:23]
        qs, ks = _slab_groups(refs[23:])
        c, s = c_ref[...], s_ref[...]
        for j in range(SLABS):
            q, k = q_refs[j][...], k_refs[j][...]
            qs[j][...] = (q * c + _partner(q) * s) * (HD ** -0.5)
            ks[j][...] = k * c + _partner(k) * s
        for di, d in enumerate(DILATIONS):
            oq, ok, ov = outs[3 * di:3 * di + 3]
            for r in range(d):
                rows = _rows(r, tm // d, d)
                for j in range(SLABS):
                    cols = pl.ds(r * AW + j * 128, 128)
                    oq[:, cols] = qs[j][rows, :].astype(BF16)
                    ok[:, cols] = ks[j][rows, :].astype(BF16)
                    ov[:, cols] = v_refs[j][rows, :].astype(BF16)

    tab = pl.BlockSpec((tm, 128), lambda i: (i, 0))
    out_specs, out_shape = [], []
    for d in DILATIONS:
        out_specs += [_dil_spec(tm, d)] * 3
        out_shape += [jax.ShapeDtypeStruct((T // d, d * AW), BF16)] * 3
    return pl.pallas_call(
        body, name="rope_fwd", grid=(T // tm,),
        in_specs=_slab_specs(tm, 0) + _slab_specs(tm, 4) + _slab_specs(tm, 8) + [tab, tab],
        out_specs=out_specs, out_shape=out_shape, scratch_shapes=_slab_scratch(2, tm),
        compiler_params=_cparams(("parallel",)),
    )(*([qkvz] * 12), cos128, sin128)


def _rope_bwd(grads, dz, cos128, sin128):
    tm = 256

    def body(*refs):
        g_refs = refs[0:9]
        dz_ref, c_ref, s_ref, o_ref = refs[9], refs[10], refs[11], refs[12]
        scr = _slab_groups(refs[13:])
        for di, d in enumerate(DILATIONS[1:]):
            for t in range(3):
                _undilate(g_refs[3 * (di + 1) + t], scr[3 * di + t], d, tm)
        c, s = c_ref[...], s_ref[...]
        for j in range(SLABS):
            cols = pl.ds(j * 128, 128)
            tot = [g_refs[t][:, cols] + scr[t][j][...] + scr[3 + t][j][...] for t in range(3)]
            dqr = tot[0] * (HD ** -0.5)
            o_ref[:, pl.ds(j * 128, 128)] = (dqr * c + _partner(dqr * s)).astype(BF16)
            o_ref[:, pl.ds(AW + j * 128, 128)] = (tot[1] * c + _partner(tot[1] * s)).astype(BF16)
            o_ref[:, pl.ds(2 * AW + j * 128, 128)] = tot[2].astype(BF16)
        o_ref[:, pl.ds(3 * AW, AW)] = dz_ref[...].astype(BF16)

    tab = pl.BlockSpec((tm, 128), lambda i: (i, 0))
    in_specs, args = [], []
    for d, g in zip(DILATIONS, grads):
        in_specs += [_dil_spec(tm, d)] * 3
        args += list(g)
    return pl.pallas_call(
        body, name="rope_bwd", grid=(T // tm,),
        in_specs=in_specs + [pl.BlockSpec((tm, AW), lambda i: (i, 0)), tab, tab],
        out_specs=pl.BlockSpec((tm, 4 * AW), lambda i: (i, 0)),
        out_shape=jax.ShapeDtypeStruct((T, 4 * AW), BF16),
        scratch_shapes=_slab_scratch(6, tm),
        compiler_params=_cparams(("parallel",)),
    )(*args, dz, cos128, sin128)


def _dx_out(dmix, w_o, dep):
    tm = ROPE_TM

    def body(a_ref, w_ref, dep_ref, dcat_ref, o4, o16, *slabs):
        prod = _nt(a_ref[...].astype(BF16), w_ref[...].astype(BF16))
        dcat_ref[...] = prod
        for j in range(SLABS):
            slabs[j][...] = prod[:, 128 * j:128 * (j + 1)]
        _dilate(o4, slabs, 4, tm)
        _dilate(o16, slabs, 16, tm)

    return pl.pallas_call(
        body, name="dx_out", grid=(T // tm,),
        in_specs=[pl.BlockSpec((tm, D), lambda i: (i, 0)), pl.BlockSpec((D, D), lambda i: (0, 0)), _ANY],
        out_specs=[pl.BlockSpec((tm, D), lambda i: (i, 0)), _dil_spec(tm, 4), _dil_spec(tm, 16)],
        out_shape=[jax.ShapeDtypeStruct((T, D), F32), jax.ShapeDtypeStruct((T // 4, 4 * AW), F32),
                   jax.ShapeDtypeStruct((T // 16, 16 * AW), F32)],
        scratch_shapes=_slab_scratch(1, tm), compiler_params=_cparams(("parallel",)),
    )(dmix, w_o, dep)


def _band_masks():
    qi = lax.broadcasted_iota(jnp.int32, (QBLK, QBLK), 0)
    kj = lax.broadcasted_iota(jnp.int32, (QBLK, QBLK), 1)
    return kj >= qi, kj <= qi


def _attn_fwd(q, k, v, d):
    L = q.shape[0]
    npair = L // (2 * QBLK)

    def body(q_ref, kp_ref, kc_ref, vp_ref, vc_ref, o_ref, l_ref):
        pair = pl.program_id(1)
        mask_p, mask_c = _band_masks()
        for sub in range(2):
            rows = pl.ds(sub * QBLK, QBLK)
            first = jnp.where(pair > 0, 0.0, NEG) if sub == 0 else 0.0
            bias = jnp.concatenate([jnp.where(mask_p, 0.0, NEG) + first, jnp.where(mask_c, 0.0, NEG)], axis=1)
            k_prev = (lambda sl: kp_ref[:, sl]) if sub == 0 else (lambda sl: kc_ref[pl.ds(0, QBLK), sl])
            v_prev = (lambda sl: vp_ref[:, sl]) if sub == 0 else (lambda sl: vc_ref[pl.ds(0, QBLK), sl])
            s = []
            for h in range(HEADS):
                sl = pl.ds(HD * h, HD)
                qh = q_ref[rows, sl]
                s.append(jnp.concatenate([_nt(qh, k_prev(sl)), _nt(qh, kc_ref[rows, sl])], axis=1))
            s = jnp.stack(s) + bias
            m = jnp.max(s, axis=2, keepdims=True)
            e = jnp.exp(s - m)
            den = jnp.sum(e, axis=2, keepdims=True)
            p = e.astype(BF16)
            inv = 1.0 / den
            lse = m + jnp.log(den)
            for h in range(HEADS):
                sl = pl.ds(HD * h, HD)
                o_ref[rows, sl] = (_nn(p[h, :, :QBLK], v_prev(sl)) + _nn(p[h, :, QBLK:], vc_ref[rows, sl])) * inv[h]
                l_ref[rows, sl] = jnp.broadcast_to(lse[h], (QBLK, HD))

    cur = pl.BlockSpec((2 * QBLK, AW), lambda r, n: (n, r))
    prev = pl.BlockSpec((QBLK, AW), lambda r, n: (jnp.maximum(2 * n - 1, 0), r))
    return pl.pallas_call(
        body, name=f"attn_fwd_d{d}", grid=(d, npair),
        in_specs=[cur, prev, cur, prev, cur], out_specs=[cur, cur],
        out_shape=[jax.ShapeDtypeStruct((L, d * AW), F32)] * 2,
        compiler_params=_cparams(("parallel", "parallel")),
    )(q, k, k, v, v)


def _attn_bwd(q, k, v, do, at, lse, d):
    L = q.shape[0]
    nb = L // QBLK

    def body(q0_ref, q1_ref, kp_ref, kc_ref, vp_ref, vc_ref, do0_ref, do1_ref, at0_ref, at1_ref,
             l0_ref, l1_ref, dq_ref, dk_ref, dv_ref):
        n = pl.program_id(1)
        mask_p, mask_c = _band_masks()
        prev_bias = jnp.where(mask_p, 0.0, NEG)
        bias = jnp.concatenate([prev_bias + jnp.where(n > 0, 0.0, NEG), jnp.where(mask_c, 0.0, NEG),
                                prev_bias + jnp.where(n < nb - 1, 0.0, NEG)], axis=1)
        s, dp, ls, dl, ops = [], [], [], [], []
        for h in range(HEADS):
            sl = pl.ds(HD * h, HD)
            one = pl.ds(HD * h, 1)
            q0, q1 = q0_ref[:, sl], q1_ref[:, sl]
            kp, kc, vp, vc = kp_ref[:, sl], kc_ref[:, sl], vp_ref[:, sl], vc_ref[:, sl]
            do0, do1 = do0_ref[:, sl], do1_ref[:, sl]
            do0b, do1b = do0.astype(BF16), do1.astype(BF16)
            s.append(jnp.concatenate([_nt(q0, kp), _nt(q0, kc), _nt(q1, kc)], axis=1))
            dp.append(jnp.concatenate([_nt(do0b, vp), _nt(do0b, vc), _nt(do1b, vc)], axis=1))
            dl0 = jnp.sum(do0 * at0_ref[:, sl], axis=1, keepdims=True)
            dl1 = jnp.sum(do1 * at1_ref[:, sl], axis=1, keepdims=True)
            dl.append(jnp.concatenate([jnp.broadcast_to(dl0, (QBLK, 2 * QBLK)), jnp.broadcast_to(dl1, (QBLK, QBLK))], axis=1))
            ls.append(jnp.concatenate([jnp.broadcast_to(l0_ref[:, one], (QBLK, 2 * QBLK)),
                                       jnp.broadcast_to(l1_ref[:, one], (QBLK, QBLK))], axis=1))
            ops.append((q0, q1, kp, kc, do0b, do1b))
        p = jnp.exp(jnp.stack(s) + bias - jnp.stack(ls))
        ds = (p * (jnp.stack(dp) - jnp.stack(dl))).astype(BF16)
        p = p.astype(BF16)
        for h in range(HEADS):
            sl = pl.ds(HD * h, HD)
            q0, q1, kp, kc, do0b, do1b = ops[h]
            dq_ref[:, sl] = (_nn(ds[h, :, :QBLK], kp) + _nn(ds[h, :, QBLK:2 * QBLK], kc)).astype(BF16)
            dv_ref[:, sl] = (_tn(p[h, :, QBLK:2 * QBLK], do0b) + _tn(p[h, :, 2 * QBLK:], do1b)).astype(BF16)
            dk_ref[:, sl] = (_tn(ds[h, :, QBLK:2 * QBLK], q0) + _tn(ds[h, :, 2 * QBLK:], q1)).astype(BF16)

    cur = pl.BlockSpec((QBLK, AW), lambda r, n: (n, r))
    prev = pl.BlockSpec((QBLK, AW), lambda r, n: (jnp.maximum(n - 1, 0), r))
    nxt = pl.BlockSpec((QBLK, AW), lambda r, n: (jnp.minimum(n + 1, nb - 1), r))
    return pl.pallas_call(
        body, name=f"attn_bwd_d{d}", grid=(d, nb),
        in_specs=[cur, nxt, prev, cur, prev, cur, cur, nxt, cur, nxt, cur, nxt], out_specs=[cur, cur, cur],
        out_shape=[jax.ShapeDtypeStruct((L, d * AW), BF16)] * 3,
        compiler_params=_cparams(("parallel", "parallel")),
    )(q, q, k, k, v, v, do, do, at, at, lse, lse)


def _attn_merge(outs, lses):
    tm = ROPE_TM

    def body(o1, o4, o16, l1, l4, l16, at_ref, ls_ref, at4, ls4, at16, ls16, *flat):
        so4, so16, sl4, sl16, sa, sl = _slab_groups(flat)
        _undilate(o4, so4, 4, tm)
        _undilate(o16, so16, 16, tm)
        _undilate(l4, sl4, 4, tm)
        _undilate(l16, sl16, 16, tm)
        for j in range(SLABS):
            cols = pl.ds(j * 128, 128)
            a, b, c = l1[:, cols], sl4[j][...], sl16[j][...]
            m = jnp.maximum(jnp.maximum(a, b), c)
            e1, e2, e3 = jnp.exp(a - m), jnp.exp(b - m), jnp.exp(c - m)
            s = e1 + e2 + e3
            inv = 1.0 / s
            attn = (e1 * inv) * o1[:, cols] + (e2 * inv) * so4[j][...] + (e3 * inv) * so16[j][...]
            lse = m + jnp.log(s)
            at_ref[:, cols] = attn
            ls_ref[:, cols] = lse
            sa[j][...] = attn
            sl[j][...] = lse
        _dilate(at4, sa, 4, tm)
        _dilate(at16, sa, 16, tm)
        _dilate(ls4, sl, 4, tm)
        _dilate(ls16, sl, 16, tm)

    specs = [_dil_spec(tm, d) for d in DILATIONS]
    tok = specs[0]
    return pl.pallas_call(
        body, name="attn_merge", grid=(T // tm,),
        in_specs=specs + specs, out_specs=[tok, tok, specs[1], specs[1], specs[2], specs[2]],
        out_shape=[jax.ShapeDtypeStruct((T, AW), F32)] * 2 + [jax.ShapeDtypeStruct((T // 4, 4 * AW), F32)] * 2
        + [jax.ShapeDtypeStruct((T // 16, 16 * AW), F32)] * 2,
        scratch_shapes=_slab_scratch(6, tm),
        compiler_params=_cparams(("parallel",)),
    )(*outs, *lses)


CONV_TM = 512
HALO = 8


def _conv_pre(ext, w, b):
    y = b + w[3] * ext
    for kk in range(1, CONV_K):
        y = y + w[3 - kk] * pltpu.roll(ext, kk, 0)
    return y


def _rows_to_block(rows, n, width):
    ri = lax.broadcasted_iota(jnp.int32, (n, width), 0)
    out = jnp.zeros((n, width), F32)
    for j, r in enumerate(rows):
        out = out + jnp.where(ri == j, r, 0.0)
    return out


def _conv_fwd(xbc, w, b):
    nblk = T // CONV_TM

    def body(x_ref, h_ref, w_ref, b_ref, o_ref):
        i = pl.program_id(0)
        halo = jnp.where(i > 0, h_ref[...], 0.0)
        ext = jnp.concatenate([halo, x_ref[...]], axis=0)
        y = _conv_pre(ext, [w_ref[pl.ds(j, 1), :] for j in range(CONV_K)], b_ref[...])[HALO:]
        o_ref[...] = y * _sigmoid(y)

    return pl.pallas_call(
        body, name="conv_fwd", grid=(nblk,),
        in_specs=[pl.BlockSpec((CONV_TM, CONV_CH), lambda i: (i, 0)),
                  pl.BlockSpec((HALO, CONV_CH), lambda i: (jnp.maximum(i * (CONV_TM // HALO) - 1, 0), 0)),
                  pl.BlockSpec((CONV_K, CONV_CH), lambda i: (0, 0)),
                  pl.BlockSpec((1, CONV_CH), lambda i: (0, 0))],
        out_specs=pl.BlockSpec((CONV_TM, CONV_CH), lambda i: (i, 0)),
        out_shape=jax.ShapeDtypeStruct((T, CONV_CH), F32),
        compiler_params=_cparams(("parallel",)),
    )(xbc, xbc, w, b)


def _conv_bwd(xbc, dact, ddt, w, b):
    nblk = T // CONV_TM
    per = CONV_TM // HALO

    def body(x_ref, xb_ref, xa_ref, g_ref, ga_ref, ddt_ref, w_ref, b_ref, dx_ref, dw_ref):
        i = pl.program_id(0)
        wv = [w_ref[pl.ds(j, 1), :] for j in range(CONV_K)]
        before = jnp.where(i > 0, xb_ref[...], 0.0)
        last = i == nblk - 1
        after = jnp.where(last, 0.0, xa_ref[...])
        g_after = jnp.where(last, 0.0, ga_ref[...])
        ext = jnp.concatenate([before, x_ref[...], after], axis=0)
        y = _conv_pre(ext, wv, b_ref[...])[HALO:]
        sg = _sigmoid(y)
        dy = jnp.concatenate([g_ref[...], g_after], axis=0) * (sg * (1.0 + y * (1.0 - sg)))
        n = CONV_TM + HALO
        dx = wv[3] * dy
        for kk in range(1, CONV_K):
            dx = dx + wv[3 - kk] * pltpu.roll(dy, n - kk, 0)
        dx_ref[:, pl.ds(0, CONV_CH)] = dx[:CONV_TM].astype(BF16)
        dx_ref[:, pl.ds(CONV_CH, DT_PAD)] = ddt_ref[...].astype(BF16)
        dyc = dy[:CONV_TM]
        rows = [jnp.sum(dyc * (pltpu.roll(ext, 3 - j, 0) if j < 3 else ext)[HALO:HALO + CONV_TM], axis=0, keepdims=True)
                for j in range(CONV_K)]
        rows.append(jnp.sum(dyc, axis=0, keepdims=True))
        part = _rows_to_block(rows, 8, CONV_CH)

        @pl.when(i == 0)
        def _():
            dw_ref[...] = jnp.zeros_like(dw_ref)
        dw_ref[...] += part

    blk = pl.BlockSpec((CONV_TM, CONV_CH), lambda i: (i, 0))
    hb = pl.BlockSpec((HALO, CONV_CH), lambda i: (jnp.maximum(i * per - 1, 0), 0))
    ha = pl.BlockSpec((HALO, CONV_CH), lambda i: (jnp.minimum((i + 1) * per, T // HALO - 1), 0))
    return pl.pallas_call(
        body, name="conv_bwd", grid=(nblk,),
        in_specs=[blk, hb, ha, blk, ha, pl.BlockSpec((CONV_TM, DT_PAD), lambda i: (i, 0)),
                  pl.BlockSpec((CONV_K, CONV_CH), lambda i: (0, 0)), pl.BlockSpec((1, CONV_CH), lambda i: (0, 0))],
        out_specs=[pl.BlockSpec((CONV_TM, CONV_CH + DT_PAD), lambda i: (i, 0)), pl.BlockSpec((8, CONV_CH), lambda i: (0, 0))],
        out_shape=[jax.ShapeDtypeStruct((T, CONV_CH + DT_PAD), BF16), jax.ShapeDtypeStruct((8, CONV_CH), F32)],
        compiler_params=_cparams(("arbitrary",)),
    )(xbc, xbc, xbc, dact, dact, ddt, w, b)


def _pick(mat, h):
    lane = lax.broadcasted_iota(jnp.int32, mat.shape, 1)
    return jnp.sum(jnp.where(lane == h, mat, 0.0), axis=1, keepdims=True)


def _heads(fn):
    return jnp.stack([fn(h) for h in range(HEADS)])


def _ssd_prep(dt_ref, bias_ref, alog_ref, dsk_ref, b_ref, c_ref, xs_ref, state_ref, cst):
    li = lax.broadcasted_iota(jnp.int32, (CHUNK, CHUNK), 0)
    si = lax.broadcasted_iota(jnp.int32, (CHUNK, CHUNK), 1)
    tri = li >= si
    dtp = dt_ref[...] + bias_ref[...]
    dt = _softplus(dtp)
    A = -jnp.exp(alog_ref[...])
    a = dt * A
    cs = jnp.dot(tri.astype(F32), a, precision=HIGHEST, preferred_element_type=F32)
    cst[...] = cs.T
    Bm = b_ref[...].astype(BF16)
    Cm = c_ref[...].astype(BF16)
    cb = _nt(Cm, Bm)
    dskv = dsk_ref[...]
    cs_col = _heads(lambda h: _pick(cs, h))
    cs_row = _heads(lambda h: cst[pl.ds(h, 1), :])
    dt_col = _heads(lambda h: _pick(dt, h))
    dsk_col = _heads(lambda h: _pick(dskv, h))
    lam = jnp.exp(jnp.where(tri, cs_col - cs_row, NEG))
    x = _heads(lambda h: xs_ref[:, pl.ds(HD * h, HD)])
    xdt = x * dt_col
    prev = _heads(lambda h: state_ref[pl.ds(HD * h, HD), :])
    lane = lax.broadcasted_iota(jnp.int32, (1, 1, CHUNK), 2)
    cl = jnp.sum(jnp.where(lane == CHUNK - 1, cs_row, 0.0), axis=2, keepdims=True)
    f = jnp.exp(cl - cs_col)
    return dict(li=li, si=si, dtp=dtp, dt=dt, A=A, Bm=Bm, Cm=Cm, cb=cb, cs_col=cs_col, dt_col=dt_col, dsk_col=dsk_col,
                lam=lam, x=x, xdt=xdt, prev=prev, cl=cl, f=f)


def _ssd_fwd(act, xbcdt, bias, alog, dsk, qkvz, attn, gs):
    nc = T // CHUNK

    def body(xs_ref, b_ref, c_ref, dt_ref, bias_ref, alog_ref, dsk_ref, z_ref, at_ref, gs_ref,
             y_ref, st_ref, cat_ref, state, cst):
        @pl.when(pl.program_id(0) == 0)
        def _():
            state[...] = jnp.zeros_like(state)
        st_ref[...] = state[...]
        s = _ssd_prep(dt_ref, bias_ref, alog_ref, dsk_ref, b_ref, c_ref, xs_ref, state, cst)
        Bm, Cm, prev = s["Bm"], s["Cm"], s["prev"]
        g = (s["cb"] * s["lam"]).astype(BF16)
        xdtb = s["xdt"].astype(BF16)
        prevb = prev.astype(BF16)
        y = _heads(lambda h: _nn(g[h], xdtb[h])) + _heads(lambda h: _nt(Cm, prevb[h])) * jnp.exp(s["cs_col"])
        y = y + s["dsk_col"] * s["x"]
        xf = (s["xdt"] * s["f"]).astype(BF16)
        new = prev * jnp.exp(s["cl"]) + _heads(lambda h: _tn(xf[h], Bm))
        for h in range(HEADS):
            y_ref[:, pl.ds(HD * h, HD)] = y[h]
            state[pl.ds(HD * h, HD), :] = new[h]
        z = z_ref[...]
        gi = y_ref[...] * (z * _sigmoid(z))
        cat_ref[:, pl.ds(0, AW)] = at_ref[...].astype(BF16)
        cat_ref[:, pl.ds(AW, AW)] = (gi * _rstd(gi) * gs_ref[...]).astype(BF16)

    vec = pl.BlockSpec((1, DT_PAD), lambda c: (0, 0))
    blk = pl.BlockSpec((CHUNK, AW), lambda c: (c, 0))
    return pl.pallas_call(
        body, name="ssd_fwd", grid=(nc,),
        in_specs=[blk, pl.BlockSpec((CHUNK, NS), lambda c: (c, 4)),
                  pl.BlockSpec((CHUNK, NS), lambda c: (c, 5)), pl.BlockSpec((CHUNK, DT_PAD), lambda c: (c, 6)),
                  vec, vec, vec, pl.BlockSpec((CHUNK, AW), lambda c: (c, 3)), blk, pl.BlockSpec((1, AW), lambda c: (0, 0))],
        out_specs=[blk, pl.BlockSpec((None, AW, NS), lambda c: (c, 0, 0)), pl.BlockSpec((CHUNK, D), lambda c: (c, 0))],
        out_shape=[jax.ShapeDtypeStruct((T, AW), F32), jax.ShapeDtypeStruct((nc, AW, NS), F32),
                   jax.ShapeDtypeStruct((T, D), BF16)],
        scratch_shapes=[pltpu.VMEM((AW, NS), F32), pltpu.VMEM((CHUNK, CHUNK), F32)],
        compiler_params=_cparams(("arbitrary",)),
    )(act, act, act, xbcdt, bias, alog, dsk, qkvz, attn, gs)


def _ssd_bwd(act, xbcdt, bias, alog, dsk, states, y_ssd, qkvz, dcat, gs):
    nc = T // CHUNK

    def body(xs_ref, b_ref, c_ref, dt_ref, bias_ref, alog_ref, dsk_ref, st_ref, y_ref, z_ref, dyn_ref, gs_ref,
             dact_ref, ddt_ref, par_ref, dz_ref, dgs_ref, dstate, cst, dy_ref):
        step = pl.program_id(0)

        @pl.when(step == 0)
        def _():
            dstate[...] = jnp.zeros_like(dstate)
            par_ref[...] = jnp.zeros_like(par_ref)
            dgs_ref[...] = jnp.zeros_like(dgs_ref)
        z, yv, dyn = z_ref[...], y_ref[...], dyn_ref[...]
        sg = _sigmoid(z)
        sz = z * sg
        gi = yv * sz
        rg = _rstd(gi)
        ng = gi * rg
        dgi = _rms_bwd(ng, rg, gs_ref[...], dyn)
        dy_ref[...] = dgi * sz
        dz_ref[...] = dgi * yv * (sg * (1.0 + z * (1.0 - sg)))
        dgs_ref[...] += _colsum(dyn * ng)
        s = _ssd_prep(dt_ref, bias_ref, alog_ref, dsk_ref, b_ref, c_ref, xs_ref, st_ref, cst)
        Bm, Cm, prev, lam, x, xdt, f, cl = s["Bm"], s["Cm"], s["prev"], s["lam"], s["x"], s["xdt"], s["f"], s["cl"]
        lane = lax.broadcasted_iota(jnp.int32, (1, DT_PAD), 1)
        row = lax.broadcasted_iota(jnp.int32, (1, CHUNK, 1), 1)
        g = s["cb"] * lam
        gb, xdtb, prevb = g.astype(BF16), xdt.astype(BF16), prev.astype(BF16)
        dy = _heads(lambda h: dy_ref[:, pl.ds(HD * h, HD)])
        dyb = dy.astype(BF16)
        dnew = _heads(lambda h: dstate[pl.ds(HD * h, HD), :])
        dnewb = dnew.astype(BF16)
        E = jnp.exp(s["cs_col"])
        ecl = jnp.exp(cl)
        dG = _heads(lambda h: _nt(dyb[h], xdtb[h]))
        dxdt = _heads(lambda h: _tn(gb[h], dyb[h]))
        Yo = _heads(lambda h: _nt(Cm, prevb[h]))
        W = _heads(lambda h: _nt(Bm, dnewb[h]))
        dcb = jnp.sum(dG * lam, axis=0)
        Mm = dG * g
        col_sums = jnp.sum(Mm, axis=1, keepdims=True)
        dYo = (dy * E).astype(BF16)
        dxdt = dxdt + W * f
        dF = jnp.sum(W * xdt, axis=2, keepdims=True) * f
        dcl = jnp.sum(dnew * prev, axis=(1, 2), keepdims=True) * ecl + jnp.sum(dF, axis=1, keepdims=True)
        dcs = (jnp.sum(Mm, axis=2, keepdims=True) + jnp.sum(dy * Yo, axis=2, keepdims=True) * E - dF
               + jnp.where(row == CHUNK - 1, dcl, 0.0))
        ddt_x = jnp.sum(dxdt * x, axis=2, keepdims=True)
        dD = jnp.sum(dy * x, axis=(1, 2), keepdims=True)
        dx = s["dsk_col"] * dy + dxdt * s["dt_col"]
        xfb = (xdt * f).astype(BF16)
        dprev = _heads(lambda h: _tn(dYo[h], Cm)) + dnew * ecl
        dcbb = dcb.astype(BF16)
        dC = _nn(dcbb, Bm)
        dB = _tn(dcbb, Cm)
        dcs_mat = -_rows_to_block([col_sums[h] for h in range(HEADS)], CHUNK, CHUNK).T
        ddt_mat = jnp.zeros((CHUNK, DT_PAD), F32)
        dD_row = jnp.zeros((1, DT_PAD), F32)
        for h in range(HEADS):
            sl = pl.ds(HD * h, HD)
            dC = dC + _nn(dYo[h], prevb[h])
            dB = dB + _nn(xfb[h], dnewb[h])
            dcs_mat = dcs_mat + jnp.where(lane == h, dcs[h], 0.0)
            ddt_mat = ddt_mat + jnp.where(lane == h, ddt_x[h], 0.0)
            dD_row = dD_row + jnp.where(lane == h, dD[h], 0.0)
            dact_ref[:, sl] = dx[h]
            dstate[sl, :] = dprev[h]
        dact_ref[:, pl.ds(AW, NS)] = dB
        dact_ref[:, pl.ds(AW + NS, NS)] = dC
        da = jnp.dot((s["li"] <= s["si"]).astype(F32), dcs_mat, precision=HIGHEST, preferred_element_type=F32)
        ddtp = jnp.where(lane < HEADS, (ddt_mat + da * s["A"]) * _sigmoid(s["dtp"]), 0.0)
        ddt_ref[...] = ddtp
        dalog = jnp.where(lane < HEADS, jnp.sum(da * s["dt"], axis=0, keepdims=True) * s["A"], 0.0)
        par_ref[...] += _rows_to_block([jnp.sum(ddtp, axis=0, keepdims=True), dalog, dD_row], 8, DT_PAD)

    vec = pl.BlockSpec((1, DT_PAD), lambda c: (0, 0))
    rev = lambda c: nc - 1 - c
    return pl.pallas_call(
        body, name="ssd_bwd", grid=(nc,),
        in_specs=[pl.BlockSpec((CHUNK, AW), lambda c: (rev(c), 0)), pl.BlockSpec((CHUNK, NS), lambda c: (rev(c), 4)),
                  pl.BlockSpec((CHUNK, NS), lambda c: (rev(c), 5)), pl.BlockSpec((CHUNK, DT_PAD), lambda c: (rev(c), 6)),
                  vec, vec, vec,
                  pl.BlockSpec((None, AW, NS), lambda c: (rev(c), 0, 0)), pl.BlockSpec((CHUNK, AW), lambda c: (rev(c), 0)),
                  pl.BlockSpec((CHUNK, AW), lambda c: (rev(c), 3)), pl.BlockSpec((CHUNK, AW), lambda c: (rev(c), 1)),
                  pl.BlockSpec((1, AW), lambda c: (0, 0))],
        out_specs=[pl.BlockSpec((CHUNK, CONV_CH), lambda c: (rev(c), 0)), pl.BlockSpec((CHUNK, DT_PAD), lambda c: (rev(c), 0)),
                   pl.BlockSpec((8, DT_PAD), lambda c: (0, 0)), pl.BlockSpec((CHUNK, AW), lambda c: (rev(c), 0)),
                   pl.BlockSpec((1, AW), lambda c: (0, 0))],
        out_shape=[jax.ShapeDtypeStruct((T, CONV_CH), F32), jax.ShapeDtypeStruct((T, DT_PAD), F32),
                   jax.ShapeDtypeStruct((8, DT_PAD), F32), jax.ShapeDtypeStruct((T, AW), F32),
                   jax.ShapeDtypeStruct((1, AW), F32)],
        scratch_shapes=[pltpu.VMEM((AW, NS), F32), pltpu.VMEM((CHUNK, CHUNK), F32), pltpu.VMEM((CHUNK, AW), F32)],
        compiler_params=_cparams(("arbitrary",)),
    )(act, act, act, xbcdt, bias, alog, dsk, states, y_ssd, qkvz, dcat, gs)


def _place():
    return lax.axis_index("x"), lax.axis_index("y"), lax.axis_index("c")


def _slot(px, py, pc):
    return 4 * px + 2 * py + pc


SLAB_ROWS = 24


def _slab_pack(parts, name):
    n = len(parts)

    def body(*refs):
        slab = refs[n]
        slab[...] = jnp.zeros_like(slab)
        for ref, (arr, row) in zip(refs[:n], parts):
            slab[pl.ds(row, arr.shape[0]), pl.ds(0, arr.shape[1])] = ref[...]

    vm = pl.BlockSpec(memory_space=pltpu.VMEM)
    return pl.pallas_call(
        body, name=name, in_specs=[vm] * n, out_specs=vm, out_shape=jax.ShapeDtypeStruct((SLAB_ROWS, D), F32),
    )(*[a for a, _ in parts])


_HBM = pl.BlockSpec(memory_space=pltpu.HBM)
_SEM = pl.BlockSpec(memory_space=pltpu.SEMAPHORE)
_EFFECT = pltpu.SideEffectType.DATAFLOW_SIDE_EFFECTING


def _peers(x, y, c):
    out = []
    for kk in range(1, N_DEV):
        fx, fy, fc = kk >> 2 & 1, kk >> 1 & 1, kk & 1
        out.append((1 - x if fx else x, 1 - y if fy else y, 1 - c if fc else c))
    return out


def _send_start(src, per_peer, name, dep):
    (handles, token) = _send_start_many([src], per_peer, name, dep)
    return handles, token


def _send_start_many(srcs, per_peer, name, dep):
    n = len(srcs)

    def body(*refs):
        src_refs, land_refs = refs[:n], refs[n:2 * n]
        send_sems, recv_sems = refs[2 * n + 1], refs[2 * n + 2]
        token = refs[-1]
        x, y, c = _place()
        mine = _slot(x, y, c)
        for a in range(n):
            for kk, peer in enumerate(_peers(x, y, c)):
                pltpu.make_async_remote_copy(
                    src_ref=src_refs[a].at[_slot(*peer)] if per_peer else src_refs[a], dst_ref=land_refs[a].at[mine],
                    send_sem=send_sems.at[a * (N_DEV - 1) + kk], recv_sem=recv_sems.at[a * (N_DEV - 1) + kk],
                    device_id=peer, device_id_type=MESH).start()
        token[...] = jnp.zeros_like(token)

    lands = [lax.empty((N_DEV,) + tuple(s.shape[1:] if per_peer else s.shape), s.dtype) for s in srcs]
    hbm = lambda t: pltpu.with_memory_space_constraint(t, pltpu.HBM)
    outs = pl.pallas_call(
        body, name=name,
        out_shape=(pltpu.SemaphoreType.DMA((n * (N_DEV - 1),)), pltpu.SemaphoreType.DMA((n * (N_DEV - 1),)),
                   *[pltpu.HBM(s.shape, s.dtype) for s in srcs], *[pltpu.HBM(l.shape, l.dtype) for l in lands],
                   jax.ShapeDtypeStruct((8, 128), F32)),
        in_specs=(*[_HBM] * (2 * n), _ANY),
        out_specs=(_SEM, _SEM, *[_HBM] * (2 * n), pl.BlockSpec(memory_space=pltpu.VMEM)),
        input_output_aliases={i: 2 + i for i in range(2 * n)},
        compiler_params=pltpu.CompilerParams(has_side_effects=_EFFECT),
    )(*[hbm(s) for s in srcs], *[hbm(l) for l in lands], dep)
    return (outs[0], outs[1], list(outs[2:2 + n]), list(outs[2 + n:2 + 2 * n])), outs[-1]


def _send_wait(handles, after, name):
    srcs, lands = _send_wait_many(handles, after, name)
    return srcs[0], lands[0]


def _send_wait_many(handles, after, name):
    send_sems, recv_sems, src_thrus, land_thrus = handles
    n = len(src_thrus)

    def body(*refs):
        land_refs = refs[n:2 * n]
        send_sems, recv_sems = refs[2 * n], refs[2 * n + 1]
        me = _place()
        for a in range(n):
            for kk in range(N_DEV - 1):
                cp = pltpu.make_async_remote_copy(
                    src_ref=land_refs[a].at[0], dst_ref=land_refs[a].at[0],
                    send_sem=send_sems.at[a * (N_DEV - 1) + kk], recv_sem=recv_sems.at[a * (N_DEV - 1) + kk],
                    device_id=me, device_id_type=MESH)
                cp.wait_send()
                cp.wait_recv()

    both = list(src_thrus) + list(land_thrus)
    outs = pl.pallas_call(
        body, name=name,
        out_shape=tuple(pltpu.HBM(t.shape, t.dtype) for t in both),
        in_specs=(*[_HBM] * (2 * n), _SEM, _SEM, _ANY), out_specs=tuple([_HBM] * (2 * n)),
        input_output_aliases={i: i for i in range(2 * n)},
        compiler_params=pltpu.CompilerParams(has_side_effects=_EFFECT),
    )(*both, send_sems, recv_sems, after)
    return list(outs[:n]), list(outs[n:])


def _sum_slots(land, name):
    _, R, C = land.shape
    tm = R if R <= 512 else 512

    def body(x_ref, o_ref):
        acc = x_ref[0].astype(F32)
        for j in range(1, N_DEV):
            acc = acc + x_ref[j].astype(F32)
        o_ref[...] = acc

    return pl.pallas_call(
        body, name=name, grid=(R // tm,),
        in_specs=[pl.BlockSpec((N_DEV, tm, C), lambda i: (0, i, 0))], out_specs=pl.BlockSpec((tm, C), lambda i: (i, 0)),
        out_shape=jax.ShapeDtypeStruct((R, C), F32), compiler_params=_cparams(("parallel",)),
    )(land)


def _adam_math(w, g, m, v):
    m2 = ADAM_B1 * m + (1.0 - ADAM_B1) * g
    v2 = ADAM_B2 * v + (1.0 - ADAM_B2) * (g * g)
    m_hat = m2 / (1.0 - ADAM_B1 ** ADAM_STEP)
    v_hat = v2 / (1.0 - ADAM_B2 ** ADAM_STEP)
    delta = -ADAM_LR * (m_hat / (jnp.sqrt(v_hat) + ADAM_EPS) + ADAM_WD * w)
    return delta, m2, v2


def _adamw(w, g, m, v, name):
    R, C = w.shape
    tm = R if R <= 512 else 256
    return _rowwise(lambda w, g, m, v: (_adam_math(w, g, m, v), ()), [w, g, m, v], [], [(C, F32)] * 3, [], tm=tm, name=name)


def _adamw_small(slab, slab_rows, g_conv_w, ws, ms, vs):
    n = len(ws)

    def body(*refs):
        slab_ref, gc_ref = refs[0], refs[1]
        w_refs, m_refs, v_refs = refs[2:2 + n], refs[2 + n:2 + 2 * n], refs[2 + 2 * n:2 + 3 * n]
        outs = refs[2 + 3 * n:]
        loss_ref = outs[0]
        g_out, d_out, m_out, v_out = (outs[1 + i * n:1 + (i + 1) * n] for i in range(4))
        loss_ref[...] = jnp.sum(slab_ref[pl.ds(6, 1), :], axis=1, keepdims=True)
        for i in range(n):
            g = gc_ref[...] if i == n - 1 else slab_ref[pl.ds(slab_rows[i], 1), pl.ds(0, ws[i].shape[1])]
            d, m2, v2 = _adam_math(w_refs[i][...], g, m_refs[i][...], v_refs[i][...])
            g_out[i][...] = g
            d_out[i][...] = d
            m_out[i][...] = m2
            v_out[i][...] = v2

    vm = pl.BlockSpec(memory_space=pltpu.VMEM)
    shapes = [jax.ShapeDtypeStruct(w.shape, F32) for w in ws]
    outs = pl.pallas_call(
        body, name="adamw_small", in_specs=[vm] * (2 + 3 * n), out_specs=[vm] * (1 + 4 * n),
        out_shape=[jax.ShapeDtypeStruct((1, 1), F32)] + shapes * 4,
    )(slab, g_conv_w, *ws, *ms, *vs)
    return outs[0], outs[1:1 + n], outs[1 + n:1 + 2 * n], outs[1 + 2 * n:1 + 3 * n], outs[1 + 3 * n:]


SMALL = ["norm_mix_pre", "norm_mix_post", "norm_mlp_pre", "norm_mlp_post", "norm_ple_post",
         "conv_b", "ssd_norm_g", "dt_bias", "a_log", "d_skip"]


def _pad_row(v, width=D):
    return jnp.pad(v, ((0, 0), (0, width - v.shape[1])))


def kernel(x, p, positions, norm_mix_pre, norm_mix_post, w_in, conv_w, conv_b, dt_bias, a_log, d_skip, ssd_norm_g, w_out, norm_mlp_pre, norm_mlp_post, w_up, w_down, w_ple_gate, w_ple_proj, norm_ple_post, loss_target, m_norm_mix_pre, m_norm_mix_post, m_w_in, m_conv_w, m_conv_b, m_dt_bias, m_a_log, m_d_skip, m_ssd_norm_g, m_w_out, m_norm_mlp_pre, m_norm_mlp_post, m_w_up, m_w_down, m_w_ple_gate, m_w_ple_proj, m_norm_ple_post, v_norm_mix_pre, v_norm_mix_post, v_w_in, v_conv_w, v_conv_b, v_dt_bias, v_a_log, v_d_skip, v_ssd_norm_g, v_w_out, v_norm_mlp_pre, v_norm_mlp_post, v_w_up, v_w_down, v_w_ple_gate, v_w_ple_proj, v_norm_ple_post):
    args = dict(locals())
    x2, p2, tgt = x[0], p[0, 0], loss_target[0]
    g1, g2, g3, g4, g5 = norm_mix_pre, norm_mix_post, norm_mlp_pre, norm_mlp_post, norm_ple_post

    me = _slot(*_place())
    pack_in = jnp.pad(w_in[0].T, ((0, W_IN_SHARD_PAD - W_IN_SHARD), (0, 0))).astype(BF16)
    rest = [w_out[0].astype(BF16), w_up[0].T.astype(BF16), w_down[0].astype(BF16), w_ple_gate[0].astype(BF16),
            w_ple_proj[0].T.reshape(32, D).astype(BF16)]
    conv_pack = jnp.pad(conv_w[0], ((0, 4), (0, 32)))
    in_handles, tok_in0 = _send_start_many([pack_in, conv_pack], False, "gather_in_start", g1)

    inv_freq = ROPE_THETA ** (-jnp.arange(HD // 2, dtype=F32) * 2.0 / HD)
    pos = positions[0] + tok_in0[0, 0].astype(jnp.int32)
    ang = pos.astype(F32)[:, None] * inv_freq
    cos, sin = jnp.cos(ang), jnp.sin(ang)
    cos128 = jnp.concatenate([cos, cos, cos, cos], axis=1)
    sin128 = jnp.concatenate([-sin, sin, -sin, sin], axis=1)

    bias_w, alog_w, dsk_w = _pad_row(dt_bias, DT_PAD), _pad_row(a_log, DT_PAD), _pad_row(d_skip, DT_PAD)

    (u1,) = _rowwise(lambda a, g: ((a * _rstd(a) * g,), ()), [x2], [g1], [(D, BF16)], [], tm=512, name="norm_x",
                     deps=[cos128, sin128])
    p2b = p2.astype(BF16)

    in_back, in_land = _send_wait_many(in_handles, u1, "gather_in_wait")
    gin = lax.dynamic_update_slice(in_land[0], in_back[0][None], (me, 0, 0))
    gconv = lax.dynamic_update_slice(in_land[1], in_back[1][None], (me, 0, 0))
    rest_handles, tok_rest = _send_start_many(rest, False, "gather_rest_start", gconv)
    w_inT = gin[:, :W_IN_SHARD].reshape(IN_W, D)
    w_qkvzT = w_inT[:4 * AW]
    w_xbcdtT = jnp.pad(w_inT[4 * AW:], ((0, DT_PAD - HEADS), (0, 0)))
    conv_full = gconv[:, :CONV_K, :96].transpose(1, 0, 2).reshape(CONV_K, CONV_CH)
    qkvz = _mm(u1, w_qkvzT, tb=True, tm=512, tn=2048, tk=1024, name="proj_qkvz", deps=[tok_rest])
    xbcdt = _mm(u1, w_xbcdtT, tb=True, tm=512, tn=896, tk=1024, name="proj_xbcdt")

    qkv = _rope_fwd(qkvz, cos128, sin128)
    qkv = [qkv[3 * i:3 * i + 3] for i in range(len(DILATIONS))]
    outs, lses = [], []
    for d, (qd, kd, vd) in zip(DILATIONS, qkv):
        o, l = _attn_fwd(qd, kd, vd, d)
        outs.append(o)
        lses.append(l)
    attn, lse, attn4, lse4, attn16, lse16 = _attn_merge(outs, lses)

    act = _conv_fwd(xbcdt, conv_full, conv_b)
    y_ssd, states, cat = _ssd_fwd(act, xbcdt, bias_w, alog_w, dsk_w, qkvz, attn, ssd_norm_g)


    rest_back, landed = _send_wait_many(rest_handles, cat, "gather_rest_wait")
    landed = [lax.dynamic_update_slice(l, b[None], (me, 0, 0)) for l, b in zip(landed, rest_back)]
    w_o, w_upT, w_dn, w_gate = landed[0].reshape(D, D), landed[1].reshape(DFF, D), landed[2].reshape(DFF, D), landed[3].reshape(D, D)
    w_projT = landed[4].reshape(D, PLE)

    def post1(mm, xx, ga):
        h = xx + mm * _rstd(mm) * ga
        return (mm, h, _rstd(h)), ()
    mix, h1, r3 = _mm_rows(post1, [(cat, w_o, False)], [x2], [g2], [(D, F32), (D, F32), (1, F32)], [], tm=512,
                           name="mix_out")

    a_up, ff, u2, h2, h2b = _mlp_fwd(h1, r3, g3, w_upT, w_dn, g4)
    relu2 = lambda a: jnp.square(jnp.maximum(a.astype(F32), 0.0))

    def final(gpre, ppv, hh, tg, g):
        sg = _sigmoid(gpre)
        ple = ppv * sg
        r = _rstd(ple)
        n = ple * r
        h3 = hh + n * g
        e = h3 - tg
        dh3 = e * (1.0 / D)
        dple = _rms_bwd(n, r, g, dh3)
        return (dh3, dple * sg, dple * ppv * sg * (1.0 - sg)), (_colsum(dh3 * n), _colsum(0.5 * e * e * (1.0 / D)))
    dh3, dpp, dgp, dg5, loss_vec = _mm_rows(final, [(h2b, w_gate, False), (p2b, w_projT, True)], [h2, tgt], [g5],
                                            [(D, F32), (D, BF16), (D, BF16)], [(1, D), (1, D)], tm=512, name="ple_loss")

    gw_projT = _mm(dpp, p2b, ta=True, tm=512, tn=256, tk=T, out_dtypes=(BF16,), name="gw_ple_proj")
    gw_gate = _mm(h2b, dgp, ta=True, tm=512, tn=1024, tk=T, out_dtypes=(BF16,), name="gw_ple_gate")
    rs_ple, tok_ple = _send_start_many([gw_projT.reshape(N_DEV, 32, D), gw_gate.reshape(N_DEV, 128, D)], True,
                                       "rs_start_w_ple", g1)
    def bwd_mlp_post(dg_, d3, f, g):
        dh2 = d3 + dg_
        r = _rstd(f)
        n = f * r
        return (dh2, _rms_bwd(n, r, g, dh2)), (_colsum(dh2 * n),)
    dh2, dff, dg4 = _mm_rows(bwd_mlp_post, [(dgp, w_gate, True)], [dh3, ff], [g4], [(D, F32), (D, BF16)], [(1, D)],
                             tm=512, name="bwd_ple_gate", deps=[tok_ple])

    gw_dn = _mm(a_up, dff, ta=True, tm=512, tn=1024, tk=T, a_pre=relu2, out_dtypes=(BF16,), name="gw_mlp_down")
    rs_dn, tok_dn = _send_start(gw_dn.reshape(N_DEV, 512, D), True, "rs_start_w_down", g1)
    da_up, du2 = _mlp_dx(dff, a_up, w_upT, w_dn, tok_dn)
    gw_upT = _mm(da_up, u2, ta=True, tm=512, tn=1024, tk=T, out_dtypes=(BF16,), name="gw_mlp_up")
    rs_up, tok_up = _send_start(gw_upT.reshape(N_DEV, 512, D), True, "rs_start_w_up", g1)

    def bwd_mix_post(d2, du, hh, rr, mm, ga, gb):
        n3 = hh * rr
        dh1 = d2 + _rms_bwd(n3, rr, gb, du)
        r = _rstd(mm)
        n2 = mm * r
        return (dh1, _rms_bwd(n2, r, ga, dh1)), (_colsum(du * n3), _colsum(dh1 * n2))
    dh1, dmix, dg3, dg2 = _rowwise(bwd_mix_post, [dh2, du2, h1, r3, mix], [g2, g3], [(D, F32), (D, BF16)],
                                   [(1, D), (1, D)], tm=512, name="bwd_post_mix", deps=[tok_up])

    gw_o = _mm(cat, dmix, ta=True, tm=512, tn=1024, tk=T, out_dtypes=(BF16,), name="gw_out")
    rs_o, tok_o = _send_start(gw_o.reshape(N_DEV, 128, D), True, "rs_start_w_out", g1)
    dcat, dattn4, dattn16 = _dx_out(dmix, w_o, tok_o)

    dact, ddtw, ssd_par, dz, dgs = _ssd_bwd(act, xbcdt, bias_w, alog_w, dsk_w, states, y_ssd, qkvz, dcat, ssd_norm_g)
    dxbcdt, conv_par = _conv_bwd(xbcdt, dact, ddtw, conv_full, conv_b)

    qkv_grads = [_attn_bwd(*qkv[0], dcat, attn, lse, 1),
                 _attn_bwd(*qkv[1], dattn4, attn4, lse4, 4),
                 _attn_bwd(*qkv[2], dattn16, attn16, lse16, 16)]
    dqkvz = _rope_bwd(qkv_grads, dz, cos128, sin128)

    gw_qkvzT = _mm(dqkvz, u1, ta=True, tm=512, tn=1024, tk=T, out_dtypes=(BF16,), name="gw_qkvz")
    gw_xbcdtT = _mm(dxbcdt, u1, ta=True, tm=896, tn=1024, tk=T, out_dtypes=(BF16,), name="gw_xbcdt")
    gw_inT = jnp.concatenate([gw_qkvzT, gw_xbcdtT], axis=0)[:IN_W]
    gw_inT = jnp.pad(gw_inT.reshape(N_DEV, W_IN_SHARD, D), ((0, 0), (0, W_IN_SHARD_PAD - W_IN_SHARD), (0, 0)))
    rs_in, tok_in = _send_start(gw_inT, True, "rs_start_w_in", g1)

    def bwd_in(ua, ub, d1, xx, g):
        rr = _rstd(xx)
        n = xx * rr
        du = ua + ub
        return (d1 + _rms_bwd(n, rr, g, du),), (_colsum(du * n),)
    grad_x, dg1 = _mm_rows(bwd_in, [(dqkvz, w_qkvzT, False), (dxbcdt, w_xbcdtT, False)], [dh1, x2], [g1],
                           [(D, F32)], [(1, D)], tm=512, name="bwd_in_proj", deps=[tok_in])

    my_slab = _slab_pack([(dg1, 0), (dg2, 1), (dg3, 2), (dg4, 3), (dg5, 4), (dgs, 5), (loss_vec, 6),
                          (conv_par, 8), (ssd_par, 16)], "slab_pack")
    slab_handles, tok_slab = _send_start_many([my_slab], False, "slab_start", g1)

    def scatter_finish(handles, nm, after):
        part, land = _send_wait(handles, after, "rs_wait_" + nm)
        own = lax.dynamic_slice(part, (me, 0, 0), (1,) + part.shape[1:])
        return _sum_slots(lax.dynamic_update_slice(land, own, (me, 0, 0)), "rs_sum_" + nm)
    g_out = scatter_finish(rs_o, "w_out", tok_slab)
    g_upT = scatter_finish(rs_up, "w_up", tok_slab)
    g_dn = scatter_finish(rs_dn, "w_down", tok_slab)
    ple_parts, ple_lands = _send_wait_many(rs_ple, tok_slab, "rs_wait_w_ple")
    g_projT, g_gate = [
        _sum_slots(lax.dynamic_update_slice(land, lax.dynamic_slice(part, (me, 0, 0), (1,) + part.shape[1:]), (me, 0, 0)),
                   "rs_sum_" + nm) for part, land, nm in zip(ple_parts, ple_lands, ("w_proj", "w_gate"))]

    grads = {
        "w_out": g_out[None], "w_up": g_upT.T[None], "w_down": g_dn[None],
        "w_ple_gate": g_gate[None], "w_ple_proj": g_projT.reshape(128, PLE).T[None],
    }
    delta, new_m, new_v = {}, {}, {}
    for nme in ["w_out", "w_up", "w_down", "w_ple_gate", "w_ple_proj", "w_in"]:
        if nme == "w_in":
            g_inT = scatter_finish(rs_in, "w_in", delta["w_down"])
            grads["w_in"] = g_inT[:W_IN_SHARD].T[None]
        dl, mm_, vv_ = _adamw(args[nme][0], grads[nme][0], args["m_" + nme][0], args["v_" + nme][0], "adamw_" + nme)
        delta[nme], new_m[nme], new_v[nme] = dl[None], mm_[None], vv_[None]

    slab_back, slab_land = _send_wait_many(slab_handles, delta["w_in"], "slab_wait")
    slab = _sum_slots(lax.dynamic_update_slice(slab_land[0], slab_back[0][None], (me, 0, 0)), "slab_sum")
    g_conv_w = lax.dynamic_slice(slab[8:12, :CONV_CH], (0, me * 96), (CONV_K, 96))
    small_names = SMALL + ["conv_w"]
    small_rows = [0, 1, 2, 3, 4, 12, 5, 16, 17, 18, None]
    pick = lambda prefix: [args[prefix + nme] for nme in SMALL] + [args[prefix + "conv_w"][0]]
    loss11, g_s, d_s, m_s, v_s = _adamw_small(slab, small_rows, g_conv_w, pick(""), pick("m_"), pick("v_"))
    loss = loss11[0, 0]
    for i, nme in enumerate(small_names):
        lead = (lambda t: t[None]) if nme == "conv_w" else (lambda t: t)
        grads[nme], delta[nme], new_m[nme], new_v[nme] = lead(g_s[i]), lead(d_s[i]), lead(m_s[i]), lead(v_s[i])

    order = ["norm_mix_pre", "norm_mix_post", "w_in", "conv_w", "conv_b", "dt_bias", "a_log", "d_skip", "ssd_norm_g",
             "w_out", "norm_mlp_pre", "norm_mlp_post", "w_up", "w_down", "w_ple_gate", "w_ple_proj", "norm_ple_post"]
    return (loss, grad_x[None], *[grads[n] for n in order], *[delta[n] for n in order],
            *[new_m[n] for n in order], *[new_v[n] for n in order])
```

```python
import functools
import math

import jax
import jax.numpy as jnp
from jax import lax
from jax.experimental import pallas as pl
from jax.experimental.pallas import tpu as pltpu

F32 = jnp.float32
BF16 = jnp.bfloat16
MESH = pl.DeviceIdType.MESH
HIGHEST = lax.Precision.HIGHEST

N_DEV = 8
T = 4096
D = 1024
HEADS = 8
HD = 64
AW = 512
NS = 128
CONV_K = 4
CONV_CH = 768
CHUNK = 128
DFF = 4096
PLE = 256
EPS = 1e-6
ROPE_THETA = 10000.0
DILATIONS = (1, 4, 16)
QBLK = 128
NEG = -1e30
IN_W = 2824
W_IN_SHARD = 353
W_IN_SHARD_PAD = 384
DT_PAD = 128

ADAM_LR, ADAM_B1, ADAM_B2, ADAM_EPS, ADAM_WD, ADAM_STEP = 0.001, 0.9, 0.999, 1e-08, 0.01, 10

VMEM_LIMIT = 56 * 1024 * 1024


_ANY = pl.BlockSpec(memory_space=pl.ANY)


def _cparams(sem=None):
    return pltpu.CompilerParams(dimension_semantics=sem, vmem_limit_bytes=VMEM_LIMIT)


def _dot(a, b, ca, cb, precision=None):
    return lax.dot_general(a, b, (((ca,), (cb,)), ((), ())), preferred_element_type=F32, precision=precision)


def _nn(a, b):
    return _dot(a, b, 1, 0)


def _nt(a, b):
    return _dot(a, b, 1, 1)


def _tn(a, b):
    return _dot(a, b, 0, 0)


def _sigmoid(x):
    return 1.0 / (1.0 + jnp.exp(-x))


def _softplus(x):
    return jnp.maximum(x, 0.0) + jnp.log(1.0 + jnp.exp(-jnp.abs(x)))


def _mm(a, b, *, ta=False, tb=False, tm, tn, tk, name,
        a_pre=None, a_rows=(), a_cols=(), b_pre=None, b_rows=(), b_cols=(),
        epi=None, epi_tiles=(), out_dtypes=(F32,), deps=()):
    if ta:
        K, M = a.shape
    else:
        M, K = a.shape
    if tb:
        N, K2 = b.shape
    else:
        K2, N = b.shape
    assert K == K2 and M % tm == 0 and N % tn == 0 and K % tk == 0, (name, a.shape, b.shape)
    nk = K // tk
    if ta:
        a_spec = pl.BlockSpec((tk, tm), lambda i, j, k: (k, i))
        a_row_specs = [pl.BlockSpec((tk, 1), lambda i, j, k: (k, 0)) for _ in a_rows]
        a_col_specs = [pl.BlockSpec((1, tm), lambda i, j, k: (0, i)) for _ in a_cols]
    else:
        a_spec = pl.BlockSpec((tm, tk), lambda i, j, k: (i, k))
        a_row_specs = [pl.BlockSpec((tm, 1), lambda i, j, k: (i, 0)) for _ in a_rows]
        a_col_specs = [pl.BlockSpec((1, tk), lambda i, j, k: (0, k)) for _ in a_cols]
    if tb:
        b_spec = pl.BlockSpec((tn, tk), lambda i, j, k: (j, k))
        b_row_specs = [pl.BlockSpec((tn, 1), lambda i, j, k: (j, 0)) for _ in b_rows]
        b_col_specs = [pl.BlockSpec((1, tk), lambda i, j, k: (0, k)) for _ in b_cols]
    else:
        b_spec = pl.BlockSpec((tk, tn), lambda i, j, k: (k, j))
        b_row_specs = [pl.BlockSpec((tk, 1), lambda i, j, k: (k, 0)) for _ in b_rows]
        b_col_specs = [pl.BlockSpec((1, tn), lambda i, j, k: (0, j)) for _ in b_cols]
    o_spec = pl.BlockSpec((tm, tn), lambda i, j, k: (i, j))
    na, nb, ne, no = len(a_rows) + len(a_cols), len(b_rows) + len(b_cols), len(epi_tiles), len(out_dtypes)

    def body(*refs):
        a_ref, b_ref = refs[0], refs[1]
        a_ex = refs[2:2 + na]
        b_ex = refs[2 + na:2 + na + nb]
        e_ex = refs[2 + na + nb:2 + na + nb + ne]
        first_out = 2 + na + nb + ne + len(deps)
        outs = refs[first_out:first_out + no]

        def finish(res):
            vals = epi(res, *[r[...] for r in e_ex]) if epi is not None else (res,)
            for o_ref, val in zip(outs, vals):
                o_ref[...] = val.astype(o_ref.dtype)

        at = a_ref[...]
        if a_pre is not None:
            at = a_pre(at, *[r[...] for r in a_ex])
        bt = b_ref[...]
        if b_pre is not None:
            bt = b_pre(bt, *[r[...] for r in b_ex])
        prod = _dot(at.astype(BF16), bt.astype(BF16), 0 if ta else 1, 1 if tb else 0)
        if nk == 1:
            finish(prod)
            return
        acc = refs[-1]
        k = pl.program_id(2)

        @pl.when(k == 0)
        def _():
            acc[...] = jnp.zeros_like(acc)
        acc[...] += prod

        @pl.when(k == nk - 1)
        def _():
            finish(acc[...])

    outs = pl.pallas_call(
        body, name=name,
        grid=(M // tm, N // tn, nk),
        in_specs=([a_spec, b_spec] + a_row_specs + a_col_specs + b_row_specs + b_col_specs + [o_spec] * ne
                  + [_ANY] * len(deps)),
        out_specs=[o_spec] * no,
        out_shape=[jax.ShapeDtypeStruct((M, N), dt) for dt in out_dtypes],
        scratch_shapes=[pltpu.VMEM((tm, tn), F32)] if nk > 1 else [],
        compiler_params=_cparams(("parallel", "parallel", "arbitrary")),
    )(a, b, *a_rows, *a_cols, *b_rows, *b_cols, *epi_tiles, *deps)
    return outs[0] if no == 1 else outs


MLP_TM = 1024
MLP_TC = 512


def _mlp_fwd(h, r, g, w_upT, w_dn, g_post):
    nc = DFF // MLP_TC

    def body(h_ref, r_ref, g_ref, wu_ref, wd_ref, gp_ref, a_ref, ff_ref, u_ref, ho_ref, hob_ref, acc, u_scr):
        c = pl.program_id(1)

        @pl.when(c == 0)
        def _():
            u = (h_ref[...] * r_ref[...] * g_ref[...]).astype(BF16)
            u_scr[...] = u
            u_ref[...] = u
            acc[...] = jnp.zeros_like(acc)
        a = _nt(u_scr[...], wu_ref[...])
        a_ref[...] = a.astype(BF16)
        acc[...] += _nn(jnp.square(jnp.maximum(a, 0.0)).astype(BF16), wd_ref[...])

        @pl.when(c == nc - 1)
        def _():
            f = acc[...]
            ff_ref[...] = f
            ho = h_ref[...] + f * _rstd(f) * gp_ref[...]
            ho_ref[...] = ho
            hob_ref[...] = ho.astype(BF16)

    row = pl.BlockSpec((MLP_TM, D), lambda i, c: (i, 0))
    wsp = pl.BlockSpec((MLP_TC, D), lambda i, c: (c, 0))
    vec = pl.BlockSpec((1, D), lambda i, c: (0, 0))
    return pl.pallas_call(
        body, name="mlp_fwd", grid=(T // MLP_TM, nc),
        in_specs=[row, pl.BlockSpec((MLP_TM, 1), lambda i, c: (i, 0)), vec, wsp, wsp, vec],
        out_specs=[pl.BlockSpec((MLP_TM, MLP_TC), lambda i, c: (i, c)), row, row, row, row],
        out_shape=[jax.ShapeDtypeStruct((T, DFF), BF16), jax.ShapeDtypeStruct((T, D), F32), jax.ShapeDtypeStruct((T, D), BF16),
                   jax.ShapeDtypeStruct((T, D), F32), jax.ShapeDtypeStruct((T, D), BF16)],
        scratch_shapes=[pltpu.VMEM((MLP_TM, D), F32), pltpu.VMEM((MLP_TM, D), BF16)],
        compiler_params=_cparams(("parallel", "arbitrary")),
    )(h, r, g, w_upT, w_dn, g_post)


def _mlp_dx(dff, a, w_upT, w_dn, dep):
    nc = DFF // MLP_TC

    def body(d_ref, a_ref, wu_ref, wd_ref, dep_ref, da_ref, du_ref, acc, d_scr):
        c = pl.program_id(1)

        @pl.when(c == 0)
        def _():
            d_scr[...] = d_ref[...].astype(BF16)
            acc[...] = jnp.zeros_like(acc)
        da = (_nt(d_scr[...], wd_ref[...]) * (2.0 * jnp.maximum(a_ref[...].astype(F32), 0.0))).astype(BF16)
        da_ref[...] = da
        acc[...] += _nn(da, wu_ref[...])

        @pl.when(c == nc - 1)
        def _():
            du_ref[...] = acc[...]

    row = pl.BlockSpec((MLP_TM, D), lambda i, c: (i, 0))
    wsp = pl.BlockSpec((MLP_TC, D), lambda i, c: (c, 0))
    chunk = pl.BlockSpec((MLP_TM, MLP_TC), lambda i, c: (i, c))
    return pl.pallas_call(
        body, name="mlp_dx", grid=(T // MLP_TM, nc),
        in_specs=[row, chunk, wsp, wsp, _ANY], out_specs=[chunk, row],
        out_shape=[jax.ShapeDtypeStruct((T, DFF), BF16), jax.ShapeDtypeStruct((T, D), F32)],
        scratch_shapes=[pltpu.VMEM((MLP_TM, D), F32), pltpu.VMEM((MLP_TM, D), BF16)],
        compiler_params=_cparams(("parallel", "arbitrary")),
    )(dff, a, w_upT, w_dn, dep)


def _rowwise(fn, rows, vecs, out_rows, out_sums, *, tm, name, deps=()):
    specs, arrs = [], []
    R = None
    for r in rows:
        if isinstance(r, tuple):
            arr, width, cb = r
            specs.append(pl.BlockSpec((tm, width), lambda i, cb=cb: (i, cb)))
        else:
            arr = r
            specs.append(pl.BlockSpec((tm, arr.shape[1]), lambda i: (i, 0)))
        R = arr.shape[0] if R is None else R
        assert arr.shape[0] == R, name
        arrs.append(arr)
    assert R % tm == 0, name
    for v in vecs:
        specs.append(pl.BlockSpec(v.shape, lambda i: (0, 0)))
        arrs.append(v)
    nr, nv, no, ns = len(rows), len(vecs), len(out_rows), len(out_sums)
    out_specs = [pl.BlockSpec((tm, w), lambda i: (i, 0)) for w, _ in out_rows]
    out_specs += [pl.BlockSpec(s, lambda i: (0, 0)) for s in out_sums]
    out_shape = [jax.ShapeDtypeStruct((R, w), dt) for w, dt in out_rows]
    out_shape += [jax.ShapeDtypeStruct(s, F32) for s in out_sums]

    nd = len(deps)

    def body(*refs):
        ins = [r[...] for r in refs[:nr + nv]]
        o_refs = refs[nr + nv + nd:nr + nv + nd + no]
        s_refs = refs[nr + nv + nd + no:]
        o_vals, s_vals = fn(*ins)
        for ref, val in zip(o_refs, o_vals):
            ref[...] = val.astype(ref.dtype)
        if ns:
            @pl.when(pl.program_id(0) == 0)
            def _():
                for ref in s_refs:
                    ref[...] = jnp.zeros_like(ref)
            for ref, val in zip(s_refs, s_vals):
                ref[...] += val

    outs = pl.pallas_call(
        body, name=name, grid=(R // tm,), in_specs=specs + [_ANY] * nd, out_specs=out_specs, out_shape=out_shape,
        compiler_params=_cparams(("arbitrary",) if ns else ("parallel",)),
    )(*arrs, *deps)
    return outs


def _mm_rows(fn, mats, rows, vecs, out_rows, out_sums, *, tm, name, deps=()):
    R = mats[0][0].shape[0]
    assert R % tm == 0, name
    specs, arrs = [], []
    for a, b, tb in mats:
        specs += [pl.BlockSpec((tm, a.shape[1]), lambda i: (i, 0)), pl.BlockSpec(b.shape, lambda i: (0, 0))]
        arrs += [a, b]
    for r in rows:
        specs.append(pl.BlockSpec((tm, r.shape[1]), lambda i: (i, 0)))
        arrs.append(r)
    for v in vecs:
        specs.append(pl.BlockSpec(v.shape, lambda i: (0, 0)))
        arrs.append(v)
    nm, nr, nv, nd, no, ns = len(mats), len(rows), len(vecs), len(deps), len(out_rows), len(out_sums)
    out_specs = [pl.BlockSpec((tm, w), lambda i: (i, 0)) for w, _ in out_rows]
    out_specs += [pl.BlockSpec(s, lambda i: (0, 0)) for s in out_sums]
    out_shape = [jax.ShapeDtypeStruct((R, w), dt) for w, dt in out_rows] + [jax.ShapeDtypeStruct(s, F32) for s in out_sums]

    def body(*refs):
        prods = [_dot(refs[2 * p][...].astype(BF16), refs[2 * p + 1][...].astype(BF16), 1, 1 if mats[p][2] else 0)
                 for p in range(nm)]
        ins = [r[...] for r in refs[2 * nm:2 * nm + nr + nv]]
        first_out = 2 * nm + nr + nv + nd
        o_refs, s_refs = refs[first_out:first_out + no], refs[first_out + no:]
        o_vals, s_vals = fn(*prods, *ins)
        for ref, val in zip(o_refs, o_vals):
            ref[...] = val.astype(ref.dtype)
        if ns:
            @pl.when(pl.program_id(0) == 0)
            def _():
                for ref in s_refs:
                    ref[...] = jnp.zeros_like(ref)
            for ref, val in zip(s_refs, s_vals):
                ref[...] += val

    return pl.pallas_call(
        body, name=name, grid=(R // tm,), in_specs=specs + [_ANY] * nd, out_specs=out_specs, out_shape=out_shape,
        compiler_params=_cparams(("arbitrary",) if ns else ("parallel",)),
    )(*arrs, *deps)


def _colsum(x):
    return jnp.sum(x, axis=0, keepdims=True)


def _rstd(x):
    return lax.rsqrt(jnp.mean(x * x, axis=-1, keepdims=True) + EPS)


def _rms_bwd(xn, r, g, dy):
    dn = dy * g
    return r * (dn - xn * jnp.mean(dn * xn, axis=-1, keepdims=True))


def _partner(t):
    lane = lax.broadcasted_iota(jnp.int32, t.shape, 1)
    up = pltpu.roll(t, 96, 1)
    down = pltpu.roll(t, 32, 1)
    return jnp.where((lane % 64) < 32, up, down)


SLABS = AW // 128


def _rows(r, n, d):
    return pl.ds(r, n, stride=d) if d > 1 else pl.ds(0, n)


def _undilate(src_ref, dst, d, tm):
    for r in range(d):
        for j in range(SLABS):
            dst[j][_rows(r, tm // d, d), :] = src_ref[:, pl.ds(r * AW + j * 128, 128)].astype(dst[j].dtype)


def _dilate(dst_ref, src, d, tm):
    for r in range(d):
        for j in range(SLABS):
            dst_ref[:, pl.ds(r * AW + j * 128, 128)] = src[j][_rows(r, tm // d, d), :].astype(dst_ref.dtype)


def _slab_scratch(n, tm):
    return [pltpu.VMEM((tm, 128), F32)] * (SLABS * n)


def _slab_groups(flat):
    return [flat[SLABS * i:SLABS * (i + 1)] for i in range(len(flat) // SLABS)]


def _slab_specs(tm, first):
    return [pl.BlockSpec((tm, 128), lambda i, j=j: (i, first + j)) for j in range(SLABS)]


def _dil_spec(tm, d):
    return pl.BlockSpec((tm // d, d * AW), lambda i: (i, 0))


ROPE_TM = 512


def _rope_fwd(qkvz, cos128, sin128):
    tm = ROPE_TM

    def body(*refs):
        q_refs, k_refs, v_refs = refs[0:4], refs[4:8], refs[8:12]
        c_ref, s_ref = refs[12], refs[13]
        outs = refs[14:23]
        qs, ks = _slab_groups(refs[23:])
        c, s = c_ref[...], s_ref[...]
        for j in range(SLABS):
            q, k = q_refs[j][...], k_refs[j][...]
            qs[j][...] = (q * c + _partner(q) * s) * (HD ** -0.5)
            ks[j][...] = k * c + _partner(k) * s
        for di, d in enumerate(DILATIONS):
            oq, ok, ov = outs[3 * di:3 * di + 3]
            for r in range(d):
                rows = _rows(r, tm // d, d)
                for j in range(SLABS):
                    cols = pl.ds(r * AW + j * 128, 128)
                    oq[:, cols] = qs[j][rows, :].astype(BF16)
                    ok[:, cols] = ks[j][rows, :].astype(BF16)
                    ov[:, cols] = v_refs[j][rows, :].astype(BF16)

    tab = pl.BlockSpec((tm, 128), lambda i: (i, 0))
    out_specs, out_shape = [], []
    for d in DILATIONS:
        out_specs += [_dil_spec(tm, d)] * 3
        out_shape += [jax.ShapeDtypeStruct((T // d, d * AW), BF16)] * 3
    return pl.pallas_call(
        body, name="rope_fwd", grid=(T // tm,),
        in_specs=_slab_specs(tm, 0) + _slab_specs(tm, 4) + _slab_specs(tm, 8) + [tab, tab],
        out_specs=out_specs, out_shape=out_shape, scratch_shapes=_slab_scratch(2, tm),
        compiler_params=_cparams(("parallel",)),
    )(*([qkvz] * 12), cos128, sin128)


def _rope_bwd(grads, dz, cos128, sin128):
    tm = 256

    def body(*refs):
        g_refs = refs[0:9]
        dz_ref, c_ref, s_ref, o_ref = refs[9], refs[10], refs[11], refs[12]
        scr = _slab_groups(refs[13:])
        for di, d in enumerate(DILATIONS[1:]):
            for t in range(3):
                _undilate(g_refs[3 * (di + 1) + t], scr[3 * di + t], d, tm)
        c, s = c_ref[...], s_ref[...]
        for j in range(SLABS):
            cols = pl.ds(j * 128, 128)
            tot = [g_refs[t][:, cols] + scr[t][j][...] + scr[3 + t][j][...] for t in range(3)]
            dqr = tot[0] * (HD ** -0.5)
            o_ref[:, pl.ds(j * 128, 128)] = (dqr * c + _partner(dqr * s)).astype(BF16)
            o_ref[:, pl.ds(AW + j * 128, 128)] = (tot[1] * c + _partner(tot[1] * s)).astype(BF16)
            o_ref[:, pl.ds(2 * AW + j * 128, 128)] = tot[2].astype(BF16)
        o_ref[:, pl.ds(3 * AW, AW)] = dz_ref[...].astype(BF16)

    tab = pl.BlockSpec((tm, 128), lambda i: (i, 0))
    in_specs, args = [], []
    for d, g in zip(DILATIONS, grads):
        in_specs += [_dil_spec(tm, d)] * 3
        args += list(g)
    return pl.pallas_call(
        body, name="rope_bwd", grid=(T // tm,),
        in_specs=in_specs + [pl.BlockSpec((tm, AW), lambda i: (i, 0)), tab, tab],
        out_specs=pl.BlockSpec((tm, 4 * AW), lambda i: (i, 0)),
        out_shape=jax.ShapeDtypeStruct((T, 4 * AW), BF16),
        scratch_shapes=_slab_scratch(6, tm),
        compiler_params=_cparams(("parallel",)),
    )(*args, dz, cos128, sin128)


def _dx_out(dmix, w_o, dep):
    tm = ROPE_TM

    def body(a_ref, w_ref, dep_ref, dcat_ref, o4, o16, *slabs):
        prod = _nt(a_ref[...].astype(BF16), w_ref[...].astype(BF16))
        dcat_ref[...] = prod
        for j in range(SLABS):
            slabs[j][...] = prod[:, 128 * j:128 * (j + 1)]
        _dilate(o4, slabs, 4, tm)
        _dilate(o16, slabs, 16, tm)

    return pl.pallas_call(
        body, name="dx_out", grid=(T // tm,),
        in_specs=[pl.BlockSpec((tm, D), lambda i: (i, 0)), pl.BlockSpec((D, D), lambda i: (0, 0)), _ANY],
        out_specs=[pl.BlockSpec((tm, D), lambda i: (i, 0)), _dil_spec(tm, 4), _dil_spec(tm, 16)],
        out_shape=[jax.ShapeDtypeStruct((T, D), F32), jax.ShapeDtypeStruct((T // 4, 4 * AW), F32),
                   jax.ShapeDtypeStruct((T // 16, 16 * AW), F32)],
        scratch_shapes=_slab_scratch(1, tm), compiler_params=_cparams(("parallel",)),
    )(dmix, w_o, dep)


def _band_masks():
    qi = lax.broadcasted_iota(jnp.int32, (QBLK, QBLK), 0)
    kj = lax.broadcasted_iota(jnp.int32, (QBLK, QBLK), 1)
    return kj >= qi, kj <= qi


def _attn_fwd(q, k, v, d):
    L = q.shape[0]
    npair = L // (2 * QBLK)

    def body(q_ref, kp_ref, kc_ref, vp_ref, vc_ref, o_ref, l_ref):
        pair = pl.program_id(1)
        mask_p, mask_c = _band_masks()
        for sub in range(2):
            rows = pl.ds(sub * QBLK, QBLK)
            first = jnp.where(pair > 0, 0.0, NEG) if sub == 0 else 0.0
            bias = jnp.concatenate([jnp.where(mask_p, 0.0, NEG) + first, jnp.where(mask_c, 0.0, NEG)], axis=1)
            k_prev = (lambda sl: kp_ref[:, sl]) if sub == 0 else (lambda sl: kc_ref[pl.ds(0, QBLK), sl])
            v_prev = (lambda sl: vp_ref[:, sl]) if sub == 0 else (lambda sl: vc_ref[pl.ds(0, QBLK), sl])
            s = []
            for h in range(HEADS):
                sl = pl.ds(HD * h, HD)
                qh = q_ref[rows, sl]
                s.append(jnp.concatenate([_nt(qh, k_prev(sl)), _nt(qh, kc_ref[rows, sl])], axis=1))
            s = jnp.stack(s) + bias
            m = jnp.max(s, axis=2, keepdims=True)
            e = jnp.exp(s - m)
            den = jnp.sum(e, axis=2, keepdims=True)
            p = e.astype(BF16)
            inv = 1.0 / den
            lse = m + jnp.log(den)
            for h in range(HEADS):
                sl = pl.ds(HD * h, HD)
                o_ref[rows, sl] = (_nn(p[h, :, :QBLK], v_prev(sl)) + _nn(p[h, :, QBLK:], vc_ref[rows, sl])) * inv[h]
                l_ref[rows, sl] = jnp.broadcast_to(lse[h], (QBLK, HD))

    cur = pl.BlockSpec((2 * QBLK, AW), lambda r, n: (n, r))
    prev = pl.BlockSpec((QBLK, AW), lambda r, n: (jnp.maximum(2 * n - 1, 0), r))
    return pl.pallas_call(
        body, name=f"attn_fwd_d{d}", grid=(d, npair),
        in_specs=[cur, prev, cur, prev, cur], out_specs=[cur, cur],
        out_shape=[jax.ShapeDtypeStruct((L, d * AW), F32)] * 2,
        compiler_params=_cparams(("parallel", "parallel")),
    )(q, k, k, v, v)


def _attn_bwd(q, k, v, do, at, lse, d):
    L = q.shape[0]
    nb = L // QBLK

    def body(q0_ref, q1_ref, kp_ref, kc_ref, vp_ref, vc_ref, do0_ref, do1_ref, at0_ref, at1_ref,
             l0_ref, l1_ref, dq_ref, dk_ref, dv_ref):
        n = pl.program_id(1)
        mask_p, mask_c = _band_masks()
        prev_bias = jnp.where(mask_p, 0.0, NEG)
        bias = jnp.concatenate([prev_bias + jnp.where(n > 0, 0.0, NEG), jnp.where(mask_c, 0.0, NEG),
                                prev_bias + jnp.where(n < nb - 1, 0.0, NEG)], axis=1)
        s, dp, ls, dl, ops = [], [], [], [], []
        for h in range(HEADS):
            sl = pl.ds(HD * h, HD)
            one = pl.ds(HD * h, 1)
            q0, q1 = q0_ref[:, sl], q1_ref[:, sl]
            kp, kc, vp, vc = kp_ref[:, sl], kc_ref[:, sl], vp_ref[:, sl], vc_ref[:, sl]
            do0, do1 = do0_ref[:, sl], do1_ref[:, sl]
            do0b, do1b = do0.astype(BF16), do1.astype(BF16)
            s.append(jnp.concatenate([_nt(q0, kp), _nt(q0, kc), _nt(q1, kc)], axis=1))
            dp.append(jnp.concatenate([_nt(do0b, vp), _nt(do0b, vc), _nt(do1b, vc)], axis=1))
            dl0 = jnp.sum(do0 * at0_ref[:, sl], axis=1, keepdims=True)
            dl1 = jnp.sum(do1 * at1_ref[:, sl], axis=1, keepdims=True)
            dl.append(jnp.concatenate([jnp.broadcast_to(dl0, (QBLK, 2 * QBLK)), jnp.broadcast_to(dl1, (QBLK, QBLK))], axis=1))
            ls.append(jnp.concatenate([jnp.broadcast_to(l0_ref[:, one], (QBLK, 2 * QBLK)),
                                       jnp.broadcast_to(l1_ref[:, one], (QBLK, QBLK))], axis=1))
            ops.append((q0, q1, kp, kc, do0b, do1b))
        p = jnp.exp(jnp.stack(s) + bias - jnp.stack(ls))
        ds = (p * (jnp.stack(dp) - jnp.stack(dl))).astype(BF16)
        p = p.astype(BF16)
        for h in range(HEADS):
            sl = pl.ds(HD * h, HD)
            q0, q1, kp, kc, do0b, do1b = ops[h]
            dq_ref[:, sl] = (_nn(ds[h, :, :QBLK], kp) + _nn(ds[h, :, QBLK:2 * QBLK], kc)).astype(BF16)
            dv_ref[:, sl] = (_tn(p[h, :, QBLK:2 * QBLK], do0b) + _tn(p[h, :, 2 * QBLK:], do1b)).astype(BF16)
            dk_ref[:, sl] = (_tn(ds[h, :, QBLK:2 * QBLK], q0) + _tn(ds[h, :, 2 * QBLK:], q1)).astype(BF16)

    cur = pl.BlockSpec((QBLK, AW), lambda r, n: (n, r))
    prev = pl.BlockSpec((QBLK, AW), lambda r, n: (jnp.maximum(n - 1, 0), r))
    nxt = pl.BlockSpec((QBLK, AW), lambda r, n: (jnp.minimum(n + 1, nb - 1), r))
    return pl.pallas_call(
        body, name=f"attn_bwd_d{d}", grid=(d, nb),
        in_specs=[cur, nxt, prev, cur, prev, cur, cur, nxt, cur, nxt, cur, nxt], out_specs=[cur, cur, cur],
        out_shape=[jax.ShapeDtypeStruct((L, d * AW), BF16)] * 3,
        compiler_params=_cparams(("parallel", "parallel")),
    )(q, q, k, k, v, v, do, do, at, at, lse, lse)


def _attn_merge(outs, lses):
    tm = ROPE_TM

    def body(o1, o4, o16, l1, l4, l16, at_ref, ls_ref, at4, ls4, at16, ls16, *flat):
        so4, so16, sl4, sl16, sa, sl = _slab_groups(flat)
        _undilate(o4, so4, 4, tm)
        _undilate(o16, so16, 16, tm)
        _undilate(l4, sl4, 4, tm)
        _undilate(l16, sl16, 16, tm)
        for j in range(SLABS):
            cols = pl.ds(j * 128, 128)
            a, b, c = l1[:, cols], sl4[j][...], sl16[j][...]
            m = jnp.maximum(jnp.maximum(a, b), c)
            e1, e2, e3 = jnp.exp(a - m), jnp.exp(b - m), jnp.exp(c - m)
            s = e1 + e2 + e3
            inv = 1.0 / s
            attn = (e1 * inv) * o1[:, cols] + (e2 * inv) * so4[j][...] + (e3 * inv) * so16[j][...]
            lse = m + jnp.log(s)
            at_ref[:, cols] = attn
            ls_ref[:, cols] = lse
            sa[j][...] = attn
            sl[j][...] = lse
        _dilate(at4, sa, 4, tm)
        _dilate(at16, sa, 16, tm)
        _dilate(ls4, sl, 4, tm)
        _dilate(ls16, sl, 16, tm)

    specs = [_dil_spec(tm, d) for d in DILATIONS]
    tok = specs[0]
    return pl.pallas_call(
        body, name="attn_merge", grid=(T // tm,),
        in_specs=specs + specs, out_specs=[tok, tok, specs[1], specs[1], specs[2], specs[2]],
        out_shape=[jax.ShapeDtypeStruct((T, AW), F32)] * 2 + [jax.ShapeDtypeStruct((T // 4, 4 * AW), F32)] * 2
        + [jax.ShapeDtypeStruct((T // 16, 16 * AW), F32)] * 2,
        scratch_shapes=_slab_scratch(6, tm),
        compiler_params=_cparams(("parallel",)),
    )(*outs, *lses)


CONV_TM = 512
HALO = 8


def _conv_pre(ext, w, b):
    y = b + w[3] * ext
    for kk in range(1, CONV_K):
        y = y + w[3 - kk] * pltpu.roll(ext, kk, 0)
    return y


def _rows_to_block(rows, n, width):
    ri = lax.broadcasted_iota(jnp.int32, (n, width), 0)
    out = jnp.zeros((n, width), F32)
    for j, r in enumerate(rows):
        out = out + jnp.where(ri == j, r, 0.0)
    return out


def _conv_fwd(xbc, w, b):
    nblk = T // CONV_TM

    def body(x_ref, h_ref, w_ref, b_ref, o_ref):
        i = pl.program_id(0)
        halo = jnp.where(i > 0, h_ref[...], 0.0)
        ext = jnp.concatenate([halo, x_ref[...]], axis=0)
        y = _conv_pre(ext, [w_ref[pl.ds(j, 1), :] for j in range(CONV_K)], b_ref[...])[HALO:]
        o_ref[...] = y * _sigmoid(y)

    return pl.pallas_call(
        body, name="conv_fwd", grid=(nblk,),
        in_specs=[pl.BlockSpec((CONV_TM, CONV_CH), lambda i: (i, 0)),
                  pl.BlockSpec((HALO, CONV_CH), lambda i: (jnp.maximum(i * (CONV_TM // HALO) - 1, 0), 0)),
                  pl.BlockSpec((CONV_K, CONV_CH), lambda i: (0, 0)),
                  pl.BlockSpec((1, CONV_CH), lambda i: (0, 0))],
        out_specs=pl.BlockSpec((CONV_TM, CONV_CH), lambda i: (i, 0)),
        out_shape=jax.ShapeDtypeStruct((T, CONV_CH), F32),
        compiler_params=_cparams(("parallel",)),
    )(xbc, xbc, w, b)


def _conv_bwd(xbc, dact, ddt, w, b):
    nblk = T // CONV_TM
    per = CONV_TM // HALO

    def body(x_ref, xb_ref, xa_ref, g_ref, ga_ref, ddt_ref, w_ref, b_ref, dx_ref, dw_ref):
        i = pl.program_id(0)
        wv = [w_ref[pl.ds(j, 1), :] for j in range(CONV_K)]
        before = jnp.where(i > 0, xb_ref[...], 0.0)
        last = i == nblk - 1
        after = jnp.where(last, 0.0, xa_ref[...])
        g_after = jnp.where(last, 0.0, ga_ref[...])
        ext = jnp.concatenate([before, x_ref[...], after], axis=0)
        y = _conv_pre(ext, wv, b_ref[...])[HALO:]
        sg = _sigmoid(y)
        dy = jnp.concatenate([g_ref[...], g_after], axis=0) * (sg * (1.0 + y * (1.0 - sg)))
        n = CONV_TM + HALO
        dx = wv[3] * dy
        for kk in range(1, CONV_K):
            dx = dx + wv[3 - kk] * pltpu.roll(dy, n - kk, 0)
        dx_ref[:, pl.ds(0, CONV_CH)] = dx[:CONV_TM].astype(BF16)
        dx_ref[:, pl.ds(CONV_CH, DT_PAD)] = ddt_ref[...].astype(BF16)
        dyc = dy[:CONV_TM]
        rows = [jnp.sum(dyc * (pltpu.roll(ext, 3 - j, 0) if j < 3 else ext)[HALO:HALO + CONV_TM], axis=0, keepdims=True)
                for j in range(CONV_K)]
        rows.append(jnp.sum(dyc, axis=0, keepdims=True))
        part = _rows_to_block(rows, 8, CONV_CH)

        @pl.when(i == 0)
        def _():
            dw_ref[...] = jnp.zeros_like(dw_ref)
        dw_ref[...] += part

    blk = pl.BlockSpec((CONV_TM, CONV_CH), lambda i: (i, 0))
    hb = pl.BlockSpec((HALO, CONV_CH), lambda i: (jnp.maximum(i * per - 1, 0), 0))
    ha = pl.BlockSpec((HALO, CONV_CH), lambda i: (jnp.minimum((i + 1) * per, T // HALO - 1), 0))
    return pl.pallas_call(
        body, name="conv_bwd", grid=(nblk,),
        in_specs=[blk, hb, ha, blk, ha, pl.BlockSpec((CONV_TM, DT_PAD), lambda i: (i, 0)),
                  pl.BlockSpec((CONV_K, CONV_CH), lambda i: (0, 0)), pl.BlockSpec((1, CONV_CH), lambda i: (0, 0))],
        out_specs=[pl.BlockSpec((CONV_TM, CONV_CH + DT_PAD), lambda i: (i, 0)), pl.BlockSpec((8, CONV_CH), lambda i: (0, 0))],
        out_shape=[jax.ShapeDtypeStruct((T, CONV_CH + DT_PAD), BF16), jax.ShapeDtypeStruct((8, CONV_CH), F32)],
        compiler_params=_cparams(("arbitrary",)),
    )(xbc, xbc, xbc, dact, dact, ddt, w, b)


def _pick(mat, h):
    lane = lax.broadcasted_iota(jnp.int32, mat.shape, 1)
    return jnp.sum(jnp.where(lane == h, mat, 0.0), axis=1, keepdims=True)


def _heads(fn):
    return jnp.stack([fn(h) for h in range(HEADS)])


def _ssd_prep(dt_ref, bias_ref, alog_ref, dsk_ref, b_ref, c_ref, xs_ref, state_ref, cst):
    li = lax.broadcasted_iota(jnp.int32, (CHUNK, CHUNK), 0)
    si = lax.broadcasted_iota(jnp.int32, (CHUNK, CHUNK), 1)
    tri = li >= si
    dtp = dt_ref[...] + bias_ref[...]
    dt = _softplus(dtp)
    A = -jnp.exp(alog_ref[...])
    a = dt * A
    cs = jnp.dot(tri.astype(F32), a, precision=HIGHEST, preferred_element_type=F32)
    cst[...] = cs.T
    Bm = b_ref[...].astype(BF16)
    Cm = c_ref[...].astype(BF16)
    cb = _nt(Cm, Bm)
    dskv = dsk_ref[...]
    cs_col = _heads(lambda h: _pick(cs, h))
    cs_row = _heads(lambda h: cst[pl.ds(h, 1), :])
    dt_col = _heads(lambda h: _pick(dt, h))
    dsk_col = _heads(lambda h: _pick(dskv, h))
    lam = jnp.exp(jnp.where(tri, cs_col - cs_row, NEG))
    x = _heads(lambda h: xs_ref[:, pl.ds(HD * h, HD)])
    xdt = x * dt_col
    prev = _heads(lambda h: state_ref[pl.ds(HD * h, HD), :])
    lane = lax.broadcasted_iota(jnp.int32, (1, 1, CHUNK), 2)
    cl = jnp.sum(jnp.where(lane == CHUNK - 1, cs_row, 0.0), axis=2, keepdims=True)
    f = jnp.exp(cl - cs_col)
    return dict(li=li, si=si, dtp=dtp, dt=dt, A=A, Bm=Bm, Cm=Cm, cb=cb, cs_col=cs_col, dt_col=dt_col, dsk_col=dsk_col,
                lam=lam, x=x, xdt=xdt, prev=prev, cl=cl, f=f)


def _ssd_fwd(act, xbcdt, bias, alog, dsk, qkvz, attn, gs):
    nc = T // CHUNK

    def body(xs_ref, b_ref, c_ref, dt_ref, bias_ref, alog_ref, dsk_ref, z_ref, at_ref, gs_ref,
             y_ref, st_ref, cat_ref, state, cst):
        @pl.when(pl.program_id(0) == 0)
        def _():
            state[...] = jnp.zeros_like(state)
        st_ref[...] = state[...]
        s = _ssd_prep(dt_ref, bias_ref, alog_ref, dsk_ref, b_ref, c_ref, xs_ref, state, cst)
        Bm, Cm, prev = s["Bm"], s["Cm"], s["prev"]
        g = (s["cb"] * s["lam"]).astype(BF16)
        xdtb = s["xdt"].astype(BF16)
        prevb = prev.astype(BF16)
        y = _heads(lambda h: _nn(g[h], xdtb[h])) + _heads(lambda h: _nt(Cm, prevb[h])) * jnp.exp(s["cs_col"])
        y = y + s["dsk_col"] * s["x"]
        xf = (s["xdt"] * s["f"]).astype(BF16)
        new = prev * jnp.exp(s["cl"]) + _heads(lambda h: _tn(xf[h], Bm))
        for h in range(HEADS):
            y_ref[:, pl.ds(HD * h, HD)] = y[h]
            state[pl.ds(HD * h, HD), :] = new[h]
        z = z_ref[...]
        gi = y_ref[...] * (z * _sigmoid(z))
        cat_ref[:, pl.ds(0, AW)] = at_ref[...].astype(BF16)
        cat_ref[:, pl.ds(AW, AW)] = (gi * _rstd(gi) * gs_ref[...]).astype(BF16)

    vec = pl.BlockSpec((1, DT_PAD), lambda c: (0, 0))
    blk = pl.BlockSpec((CHUNK, AW), lambda c: (c, 0))
    return pl.pallas_call(
        body, name="ssd_fwd", grid=(nc,),
        in_specs=[blk, pl.BlockSpec((CHUNK, NS), lambda c: (c, 4)),
                  pl.BlockSpec((CHUNK, NS), lambda c: (c, 5)), pl.BlockSpec((CHUNK, DT_PAD), lambda c: (c, 6)),
                  vec, vec, vec, pl.BlockSpec((CHUNK, AW), lambda c: (c, 3)), blk, pl.BlockSpec((1, AW), lambda c: (0, 0))],
        out_specs=[blk, pl.BlockSpec((None, AW, NS), lambda c: (c, 0, 0)), pl.BlockSpec((CHUNK, D), lambda c: (c, 0))],
        out_shape=[jax.ShapeDtypeStruct((T, AW), F32), jax.ShapeDtypeStruct((nc, AW, NS), F32),
                   jax.ShapeDtypeStruct((T, D), BF16)],
        scratch_shapes=[pltpu.VMEM((AW, NS), F32), pltpu.VMEM((CHUNK, CHUNK), F32)],
        compiler_params=_cparams(("arbitrary",)),
    )(act, act, act, xbcdt, bias, alog, dsk, qkvz, attn, gs)


def _ssd_bwd(act, xbcdt, bias, alog, dsk, states, y_ssd, qkvz, dcat, gs):
    nc = T // CHUNK

    def body(xs_ref, b_ref, c_ref, dt_ref, bias_ref, alog_ref, dsk_ref, st_ref, y_ref, z_ref, dyn_ref, gs_ref,
             dact_ref, ddt_ref, par_ref, dz_ref, dgs_ref, dstate, cst, dy_ref):
        step = pl.program_id(0)

        @pl.when(step == 0)
        def _():
            dstate[...] = jnp.zeros_like(dstate)
            par_ref[...] = jnp.zeros_like(par_ref)
            dgs_ref[...] = jnp.zeros_like(dgs_ref)
        z, yv, dyn = z_ref[...], y_ref[...], dyn_ref[...]
        sg = _sigmoid(z)
        sz = z * sg
        gi = yv * sz
        rg = _rstd(gi)
        ng = gi * rg
        dgi = _rms_bwd(ng, rg, gs_ref[...], dyn)
        dy_ref[...] = dgi * sz
        dz_ref[...] = dgi * yv * (sg * (1.0 + z * (1.0 - sg)))
        dgs_ref[...] += _colsum(dyn * ng)
        s = _ssd_prep(dt_ref, bias_ref, alog_ref, dsk_ref, b_ref, c_ref, xs_ref, st_ref, cst)
        Bm, Cm, prev, lam, x, xdt, f, cl = s["Bm"], s["Cm"], s["prev"], s["lam"], s["x"], s["xdt"], s["f"], s["cl"]
        lane = lax.broadcasted_iota(jnp.int32, (1, DT_PAD), 1)
        row = lax.broadcasted_iota(jnp.int32, (1, CHUNK, 1), 1)
        g = s["cb"] * lam
        gb, xdtb, prevb = g.astype(BF16), xdt.astype(BF16), prev.astype(BF16)
        dy = _heads(lambda h: dy_ref[:, pl.ds(HD * h, HD)])
        dyb = dy.astype(BF16)
        dnew = _heads(lambda h: dstate[pl.ds(HD * h, HD), :])
        dnewb = dnew.astype(BF16)
        E = jnp.exp(s["cs_col"])
        ecl = jnp.exp(cl)
        dG = _heads(lambda h: _nt(dyb[h], xdtb[h]))
        dxdt = _heads(lambda h: _tn(gb[h], dyb[h]))
        Yo = _heads(lambda h: _nt(Cm, prevb[h]))
        W = _heads(lambda h: _nt(Bm, dnewb[h]))
        dcb = jnp.sum(dG * lam, axis=0)
        Mm = dG * g
        col_sums = jnp.sum(Mm, axis=1, keepdims=True)
        dYo = (dy * E).astype(BF16)
        dxdt = dxdt + W * f
        dF = jnp.sum(W * xdt, axis=2, keepdims=True) * f
        dcl = jnp.sum(dnew * prev, axis=(1, 2), keepdims=True) * ecl + jnp.sum(dF, axis=1, keepdims=True)
        dcs = (jnp.sum(Mm, axis=2, keepdims=True) + jnp.sum(dy * Yo, axis=2, keepdims=True) * E - dF
               + jnp.where(row == CHUNK - 1, dcl, 0.0))
        ddt_x = jnp.sum(dxdt * x, axis=2, keepdims=True)
        dD = jnp.sum(dy * x, axis=(1, 2), keepdims=True)
        dx = s["dsk_col"] * dy + dxdt * s["dt_col"]
        xfb = (xdt * f).astype(BF16)
        dprev = _heads(lambda h: _tn(dYo[h], Cm)) + dnew * ecl
        dcbb = dcb.astype(BF16)
        dC = _nn(dcbb, Bm)
        dB = _tn(dcbb, Cm)
        dcs_mat = -_rows_to_block([col_sums[h] for h in range(HEADS)], CHUNK, CHUNK).T
        ddt_mat = jnp.zeros((CHUNK, DT_PAD), F32)
        dD_row = jnp.zeros((1, DT_PAD), F32)
        for h in range(HEADS):
            sl = pl.ds(HD * h, HD)
            dC = dC + _nn(dYo[h], prevb[h])
            dB = dB + _nn(xfb[h], dnewb[h])
            dcs_mat = dcs_mat + jnp.where(lane == h, dcs[h], 0.0)
            ddt_mat = ddt_mat + jnp.where(lane == h, ddt_x[h], 0.0)
            dD_row = dD_row + jnp.where(lane == h, dD[h], 0.0)
            dact_ref[:, sl] = dx[h]
            dstate[sl, :] = dprev[h]
        dact_ref[:, pl.ds(AW, NS)] = dB
        dact_ref[:, pl.ds(AW + NS, NS)] = dC
        da = jnp.dot((s["li"] <= s["si"]).astype(F32), dcs_mat, precision=HIGHEST, preferred_element_type=F32)
        ddtp = jnp.where(lane < HEADS, (ddt_mat + da * s["A"]) * _sigmoid(s["dtp"]), 0.0)
        ddt_ref[...] = ddtp
        dalog = jnp.where(lane < HEADS, jnp.sum(da * s["dt"], axis=0, keepdims=True) * s["A"], 0.0)
        par_ref[...] += _rows_to_block([jnp.sum(ddtp, axis=0, keepdims=True), dalog, dD_row], 8, DT_PAD)

    vec = pl.BlockSpec((1, DT_PAD), lambda c: (0, 0))
    rev = lambda c: nc - 1 - c
    return pl.pallas_call(
        body, name="ssd_bwd", grid=(nc,),
        in_specs=[pl.BlockSpec((CHUNK, AW), lambda c: (rev(c), 0)), pl.BlockSpec((CHUNK, NS), lambda c: (rev(c), 4)),
                  pl.BlockSpec((CHUNK, NS), lambda c: (rev(c), 5)), pl.BlockSpec((CHUNK, DT_PAD), lambda c: (rev(c), 6)),
                  vec, vec, vec,
                  pl.BlockSpec((None, AW, NS), lambda c: (rev(c), 0, 0)), pl.BlockSpec((CHUNK, AW), lambda c: (rev(c), 0)),
                  pl.BlockSpec((CHUNK, AW), lambda c: (rev(c), 3)), pl.BlockSpec((CHUNK, AW), lambda c: (rev(c), 1)),
                  pl.BlockSpec((1, AW), lambda c: (0, 0))],
        out_specs=[pl.BlockSpec((CHUNK, CONV_CH), lambda c: (rev(c), 0)), pl.BlockSpec((CHUNK, DT_PAD), lambda c: (rev(c), 0)),
                   pl.BlockSpec((8, DT_PAD), lambda c: (0, 0)), pl.BlockSpec((CHUNK, AW), lambda c: (rev(c), 0)),
                   pl.BlockSpec((1, AW), lambda c: (0, 0))],
        out_shape=[jax.ShapeDtypeStruct((T, CONV_CH), F32), jax.ShapeDtypeStruct((T, DT_PAD), F32),
                   jax.ShapeDtypeStruct((8, DT_PAD), F32), jax.ShapeDtypeStruct((T, AW), F32),
                   jax.ShapeDtypeStruct((1, AW), F32)],
        scratch_shapes=[pltpu.VMEM((AW, NS), F32), pltpu.VMEM((CHUNK, CHUNK), F32), pltpu.VMEM((CHUNK, AW), F32)],
        compiler_params=_cparams(("arbitrary",)),
    )(act, act, act, xbcdt, bias, alog, dsk, states, y_ssd, qkvz, dcat, gs)


def _place():
    return lax.axis_index("x"), lax.axis_index("y"), lax.axis_index("c")


def _slot(px, py, pc):
    return 4 * px + 2 * py + pc


SLAB_ROWS = 24


def _slab_pack(parts, name):
    n = len(parts)

    def body(*refs):
        slab = refs[n]
        slab[...] = jnp.zeros_like(slab)
        for ref, (arr, row) in zip(refs[:n], parts):
            slab[pl.ds(row, arr.shape[0]), pl.ds(0, arr.shape[1])] = ref[...]

    vm = pl.BlockSpec(memory_space=pltpu.VMEM)
    return pl.pallas_call(
        body, name=name, in_specs=[vm] * n, out_specs=vm, out_shape=jax.ShapeDtypeStruct((SLAB_ROWS, D), F32),
    )(*[a for a, _ in parts])


_HBM = pl.BlockSpec(memory_space=pltpu.HBM)
_SEM = pl.BlockSpec(memory_space=pltpu.SEMAPHORE)
_EFFECT = pltpu.SideEffectType.DATAFLOW_SIDE_EFFECTING


def _peers(x, y, c):
    out = []
    for kk in range(1, N_DEV):
        fx, fy, fc = kk >> 2 & 1, kk >> 1 & 1, kk & 1
        out.append((1 - x if fx else x, 1 - y if fy else y, 1 - c if fc else c))
    return out


def _send_start(src, per_peer, name, dep):
    (handles, token) = _send_start_many([src], per_peer, name, dep)
    return handles, token


def _near_peers(x, y, c):
    return [(x, y, 1 - c), (1 - x, y, c), (x, 1 - y, c), (1 - x, 1 - y, c)]


def _send_start_many(srcs, per_peer, name, dep, peers=_peers, npeers=N_DEV - 1):
    n = len(srcs)

    def body(*refs):
        src_refs, land_refs = refs[:n], refs[n:2 * n]
        send_sems, recv_sems = refs[2 * n + 1], refs[2 * n + 2]
        token = refs[-1]
        x, y, c = _place()
        mine = _slot(x, y, c)
        for a in range(n):
            for kk, peer in enumerate(peers(x, y, c)):
                pltpu.make_async_remote_copy(
                    src_ref=src_refs[a].at[_slot(*peer)] if per_peer else src_refs[a], dst_ref=land_refs[a].at[mine],
                    send_sem=send_sems.at[a * npeers + kk], recv_sem=recv_sems.at[a * npeers + kk],
                    device_id=peer, device_id_type=MESH).start()
        token[...] = jnp.zeros_like(token)

    lands = [lax.empty((N_DEV,) + tuple(s.shape[1:] if per_peer else s.shape), s.dtype) for s in srcs]
    hbm = lambda t: pltpu.with_memory_space_constraint(t, pltpu.HBM)
    outs = pl.pallas_call(
        body, name=name,
        out_shape=(pltpu.SemaphoreType.DMA((n * npeers,)), pltpu.SemaphoreType.DMA((n * npeers,)),
                   *[pltpu.HBM(s.shape, s.dtype) for s in srcs], *[pltpu.HBM(l.shape, l.dtype) for l in lands],
                   jax.ShapeDtypeStruct((8, 128), F32)),
        in_specs=(*[_HBM] * (2 * n), _ANY),
        out_specs=(_SEM, _SEM, *[_HBM] * (2 * n), pl.BlockSpec(memory_space=pltpu.VMEM)),
        input_output_aliases={i: 2 + i for i in range(2 * n)},
        compiler_params=pltpu.CompilerParams(has_side_effects=_EFFECT),
    )(*[hbm(s) for s in srcs], *[hbm(l) for l in lands], dep)
    return (outs[0], outs[1], list(outs[2:2 + n]), list(outs[2 + n:2 + 2 * n])), outs[-1]


def _send_wait(handles, after, name):
    srcs, lands = _send_wait_many(handles, after, name)
    return srcs[0], lands[0]


def _send_wait_many(handles, after, name, npeers=N_DEV - 1):
    send_sems, recv_sems, src_thrus, land_thrus = handles
    n = len(src_thrus)

    def body(*refs):
        land_refs = refs[n:2 * n]
        send_sems, recv_sems = refs[2 * n], refs[2 * n + 1]
        me = _place()
        for a in range(n):
            for kk in range(npeers):
                cp = pltpu.make_async_remote_copy(
                    src_ref=land_refs[a].at[0], dst_ref=land_refs[a].at[0],
                    send_sem=send_sems.at[a * npeers + kk], recv_sem=recv_sems.at[a * npeers + kk],
                    device_id=me, device_id_type=MESH)
                cp.wait_send()
                cp.wait_recv()

    both = list(src_thrus) + list(land_thrus)
    outs = pl.pallas_call(
        body, name=name,
        out_shape=tuple(pltpu.HBM(t.shape, t.dtype) for t in both),
        in_specs=(*[_HBM] * (2 * n), _SEM, _SEM, _ANY), out_specs=tuple([_HBM] * (2 * n)),
        input_output_aliases={i: i for i in range(2 * n)},
        compiler_params=pltpu.CompilerParams(has_side_effects=_EFFECT),
    )(*both, send_sems, recv_sems, after)
    return list(outs[:n]), list(outs[n:])


def _forward_start(lands, name, dep):
    n = len(lands)

    def body(*refs):
        land_refs = refs[:n]
        send_sems, recv_sems = refs[n + 1], refs[n + 2]
        token = refs[-1]
        x, y, c = _place()
        for a in range(n):
            for j, chip in enumerate([(1 - x, y), (x, 1 - y), (1 - x, 1 - y)]):
                blk = land_refs[a].at[_slot(*chip, c)]
                pltpu.make_async_remote_copy(
                    src_ref=blk, dst_ref=blk, send_sem=send_sems.at[a * 3 + j], recv_sem=recv_sems.at[a * 3 + j],
                    device_id=(x, y, 1 - c), device_id_type=MESH).start()
        token[...] = jnp.zeros_like(token)

    outs = pl.pallas_call(
        body, name=name,
        out_shape=(pltpu.SemaphoreType.DMA((n * 3,)), pltpu.SemaphoreType.DMA((n * 3,)),
                   *[pltpu.HBM(l.shape, l.dtype) for l in lands], jax.ShapeDtypeStruct((8, 128), F32)),
        in_specs=(*[_HBM] * n, _ANY), out_specs=(_SEM, _SEM, *[_HBM] * n, pl.BlockSpec(memory_space=pltpu.VMEM)),
        input_output_aliases={i: 2 + i for i in range(n)},
        compiler_params=pltpu.CompilerParams(has_side_effects=_EFFECT),
    )(*lands, dep)
    return (outs[0], outs[1], list(outs[2:2 + n])), outs[-1]


def _forward_wait(handles, after, name):
    send_sems, recv_sems, land_thrus = handles
    n = len(land_thrus)

    def body(*refs):
        land_refs = refs[:n]
        send_sems, recv_sems = refs[n], refs[n + 1]
        me = _place()
        for a in range(n):
            for j in range(3):
                cp = pltpu.make_async_remote_copy(
                    src_ref=land_refs[a].at[0], dst_ref=land_refs[a].at[0],
                    send_sem=send_sems.at[a * 3 + j], recv_sem=recv_sems.at[a * 3 + j], device_id=me, device_id_type=MESH)
                cp.wait_send()
                cp.wait_recv()

    outs = pl.pallas_call(
        body, name=name,
        out_shape=tuple(pltpu.HBM(t.shape, t.dtype) for t in land_thrus),
        in_specs=(*[_HBM] * n, _SEM, _SEM, _ANY), out_specs=tuple([_HBM] * n),
        input_output_aliases={i: i for i in range(n)},
        compiler_params=pltpu.CompilerParams(has_side_effects=_EFFECT),
    )(*land_thrus, send_sems, recv_sems, after)
    return list(outs)


def _sum_slots(land, name):
    _, R, C = land.shape
    tm = R if R <= 512 else 512

    def body(x_ref, o_ref):
        acc = x_ref[0].astype(F32)
        for j in range(1, N_DEV):
            acc = acc + x_ref[j].astype(F32)
        o_ref[...] = acc

    return pl.pallas_call(
        body, name=name, grid=(R // tm,),
        in_specs=[pl.BlockSpec((N_DEV, tm, C), lambda i: (0, i, 0))], out_specs=pl.BlockSpec((tm, C), lambda i: (i, 0)),
        out_shape=jax.ShapeDtypeStruct((R, C), F32), compiler_params=_cparams(("parallel",)),
    )(land)


def _adam_math(w, g, m, v):
    m2 = ADAM_B1 * m + (1.0 - ADAM_B1) * g
    v2 = ADAM_B2 * v + (1.0 - ADAM_B2) * (g * g)
    m_hat = m2 / (1.0 - ADAM_B1 ** ADAM_STEP)
    v_hat = v2 / (1.0 - ADAM_B2 ** ADAM_STEP)
    delta = -ADAM_LR * (m_hat / (jnp.sqrt(v_hat) + ADAM_EPS) + ADAM_WD * w)
    return delta, m2, v2


def _adamw(w, g, m, v, name):
    R, C = w.shape
    tm = R if R <= 512 else 256
    return _rowwise(lambda w, g, m, v: (_adam_math(w, g, m, v), ()), [w, g, m, v], [], [(C, F32)] * 3, [], tm=tm, name=name)


def _adamw_small(slab, slab_rows, g_conv_w, ws, ms, vs):
    n = len(ws)

    def body(*refs):
        slab_ref, gc_ref = refs[0], refs[1]
        w_refs, m_refs, v_refs = refs[2:2 + n], refs[2 + n:2 + 2 * n], refs[2 + 2 * n:2 + 3 * n]
        outs = refs[2 + 3 * n:]
        loss_ref = outs[0]
        g_out, d_out, m_out, v_out = (outs[1 + i * n:1 + (i + 1) * n] for i in range(4))
        loss_ref[...] = jnp.sum(slab_ref[pl.ds(6, 1), :], axis=1, keepdims=True)
        for i in range(n):
            g = gc_ref[...] if i == n - 1 else slab_ref[pl.ds(slab_rows[i], 1), pl.ds(0, ws[i].shape[1])]
            d, m2, v2 = _adam_math(w_refs[i][...], g, m_refs[i][...], v_refs[i][...])
            g_out[i][...] = g
            d_out[i][...] = d
            m_out[i][...] = m2
            v_out[i][...] = v2

    vm = pl.BlockSpec(memory_space=pltpu.VMEM)
    shapes = [jax.ShapeDtypeStruct(w.shape, F32) for w in ws]
    outs = pl.pallas_call(
        body, name="adamw_small", in_specs=[vm] * (2 + 3 * n), out_specs=[vm] * (1 + 4 * n),
        out_shape=[jax.ShapeDtypeStruct((1, 1), F32)] + shapes * 4,
    )(slab, g_conv_w, *ws, *ms, *vs)
    return outs[0], outs[1:1 + n], outs[1 + n:1 + 2 * n], outs[1 + 2 * n:1 + 3 * n], outs[1 + 3 * n:]


SMALL = ["norm_mix_pre", "norm_mix_post", "norm_mlp_pre", "norm_mlp_post", "norm_ple_post",
         "conv_b", "ssd_norm_g", "dt_bias", "a_log", "d_skip"]


def _pad_row(v, width=D):
    return jnp.pad(v, ((0, 0), (0, width - v.shape[1])))


def kernel(x, p, positions, norm_mix_pre, norm_mix_post, w_in, conv_w, conv_b, dt_bias, a_log, d_skip, ssd_norm_g, w_out, norm_mlp_pre, norm_mlp_post, w_up, w_down, w_ple_gate, w_ple_proj, norm_ple_post, loss_target, m_norm_mix_pre, m_norm_mix_post, m_w_in, m_conv_w, m_conv_b, m_dt_bias, m_a_log, m_d_skip, m_ssd_norm_g, m_w_out, m_norm_mlp_pre, m_norm_mlp_post, m_w_up, m_w_down, m_w_ple_gate, m_w_ple_proj, m_norm_ple_post, v_norm_mix_pre, v_norm_mix_post, v_w_in, v_conv_w, v_conv_b, v_dt_bias, v_a_log, v_d_skip, v_ssd_norm_g, v_w_out, v_norm_mlp_pre, v_norm_mlp_post, v_w_up, v_w_down, v_w_ple_gate, v_w_ple_proj, v_norm_ple_post):
    args = dict(locals())
    x2, p2, tgt = x[0], p[0, 0], loss_target[0]
    g1, g2, g3, g4, g5 = norm_mix_pre, norm_mix_post, norm_mlp_pre, norm_mlp_post, norm_ple_post

    me = _slot(*_place())
    pack_in = jnp.pad(w_in[0].T, ((0, W_IN_SHARD_PAD - W_IN_SHARD), (0, 0))).astype(BF16)
    rest = [w_out[0].astype(BF16), w_up[0].T.astype(BF16), w_down[0].astype(BF16), w_ple_gate[0].astype(BF16),
            w_ple_proj[0].T.reshape(32, D).astype(BF16)]
    conv_pack = jnp.pad(conv_w[0], ((0, 4), (0, 32)))
    in_handles, tok_in0 = _send_start_many([pack_in, conv_pack], False, "gather_in_start", g1, peers=_near_peers, npeers=4)

    inv_freq = ROPE_THETA ** (-jnp.arange(HD // 2, dtype=F32) * 2.0 / HD)
    pos = positions[0] + tok_in0[0, 0].astype(jnp.int32)
    ang = pos.astype(F32)[:, None] * inv_freq
    cos, sin = jnp.cos(ang), jnp.sin(ang)
    cos128 = jnp.concatenate([cos, cos, cos, cos], axis=1)
    sin128 = jnp.concatenate([-sin, sin, -sin, sin], axis=1)

    bias_w, alog_w, dsk_w = _pad_row(dt_bias, DT_PAD), _pad_row(a_log, DT_PAD), _pad_row(d_skip, DT_PAD)

    (u1,) = _rowwise(lambda a, g: ((a * _rstd(a) * g,), ()), [x2], [g1], [(D, BF16)], [], tm=512, name="norm_x",
                     deps=[cos128, sin128])
    p2b = p2.astype(BF16)

    in_back, in_land = _send_wait_many(in_handles, u1, "gather_in_wait", npeers=4)
    fw_handles, tok_fw = _forward_start(in_land, "gather_in_forward", u1)
    in_land = _forward_wait(fw_handles, tok_fw, "gather_in_forward_wait")
    gin = lax.dynamic_update_slice(in_land[0], in_back[0][None], (me, 0, 0))
    gconv = lax.dynamic_update_slice(in_land[1], in_back[1][None], (me, 0, 0))
    rest_handles, tok_rest = _send_start_many(rest, False, "gather_rest_start", gconv)
    w_inT = gin[:, :W_IN_SHARD].reshape(IN_W, D)
    w_qkvzT = w_inT[:4 * AW]
    w_xbcdtT = jnp.pad(w_inT[4 * AW:], ((0, DT_PAD - HEADS), (0, 0)))
    conv_full = gconv[:, :CONV_K, :96].transpose(1, 0, 2).reshape(CONV_K, CONV_CH)
    qkvz = _mm(u1, w_qkvzT, tb=True, tm=512, tn=2048, tk=1024, name="proj_qkvz", deps=[tok_rest])
    xbcdt = _mm(u1, w_xbcdtT, tb=True, tm=512, tn=896, tk=1024, name="proj_xbcdt")

    qkv = _rope_fwd(qkvz, cos128, sin128)
    qkv = [qkv[3 * i:3 * i + 3] for i in range(len(DILATIONS))]
    outs, lses = [], []
    for d, (qd, kd, vd) in zip(DILATIONS, qkv):
        o, l = _attn_fwd(qd, kd, vd, d)
        outs.append(o)
        lses.append(l)
    attn, lse, attn4, lse4, attn16, lse16 = _attn_merge(outs, lses)

    act = _conv_fwd(xbcdt, conv_full, conv_b)
    y_ssd, states, cat = _ssd_fwd(act, xbcdt, bias_w, alog_w, dsk_w, qkvz, attn, ssd_norm_g)


    rest_back, landed = _send_wait_many(rest_handles, cat, "gather_rest_wait")
    landed = [lax.dynamic_update_slice(l, b[None], (me, 0, 0)) for l, b in zip(landed, rest_back)]
    w_o, w_upT, w_dn, w_gate = landed[0].reshape(D, D), landed[1].reshape(DFF, D), landed[2].reshape(DFF, D), landed[3].reshape(D, D)
    w_projT = landed[4].reshape(D, PLE)

    def post1(mm, xx, ga):
        h = xx + mm * _rstd(mm) * ga
        return (mm, h, _rstd(h)), ()
    mix, h1, r3 = _mm_rows(post1, [(cat, w_o, False)], [x2], [g2], [(D, F32), (D, F32), (1, F32)], [], tm=512,
                           name="mix_out")

    a_up, ff, u2, h2, h2b = _mlp_fwd(h1, r3, g3, w_upT, w_dn, g4)
    relu2 = lambda a: jnp.square(jnp.maximum(a.astype(F32), 0.0))

    def final(gpre, ppv, hh, tg, g):
        sg = _sigmoid(gpre)
        ple = ppv * sg
        r = _rstd(ple)
        n = ple * r
        h3 = hh + n * g
        e = h3 - tg
        dh3 = e * (1.0 / D)
        dple = _rms_bwd(n, r, g, dh3)
        return (dh3, dple * sg, dple * ppv * sg * (1.0 - sg)), (_colsum(dh3 * n), _colsum(0.5 * e * e * (1.0 / D)))
    dh3, dpp, dgp, dg5, loss_vec = _mm_rows(final, [(h2b, w_gate, False), (p2b, w_projT, True)], [h2, tgt], [g5],
                                            [(D, F32), (D, BF16), (D, BF16)], [(1, D), (1, D)], tm=512, name="ple_loss")

    gw_projT = _mm(dpp, p2b, ta=True, tm=512, tn=256, tk=T, out_dtypes=(BF16,), name="gw_ple_proj")
    gw_gate = _mm(h2b, dgp, ta=True, tm=512, tn=1024, tk=T, out_dtypes=(BF16,), name="gw_ple_gate")
    rs_ple, tok_ple = _send_start_many([gw_projT.reshape(N_DEV, 32, D), gw_gate.reshape(N_DEV, 128, D)], True,
                                       "rs_start_w_ple", g1)
    def bwd_mlp_post(dg_, d3, f, g):
        dh2 = d3 + dg_
        r = _rstd(f)
        n = f * r
        return (dh2, _rms_bwd(n, r, g, dh2)), (_colsum(dh2 * n),)
    dh2, dff, dg4 = _mm_rows(bwd_mlp_post, [(dgp, w_gate, True)], [dh3, ff], [g4], [(D, F32), (D, BF16)], [(1, D)],
                             tm=512, name="bwd_ple_gate", deps=[tok_ple])

    gw_dn = _mm(a_up, dff, ta=True, tm=512, tn=1024, tk=T, a_pre=relu2, out_dtypes=(BF16,), name="gw_mlp_down")
    rs_dn, tok_dn = _send_start(gw_dn.reshape(N_DEV, 512, D), True, "rs_start_w_down", g1)
    da_up, du2 = _mlp_dx(dff, a_up, w_upT, w_dn, tok_dn)
    gw_upT = _mm(da_up, u2, ta=True, tm=512, tn=1024, tk=T, out_dtypes=(BF16,), name="gw_mlp_up")
    rs_up, tok_up = _send_start(gw_upT.reshape(N_DEV, 512, D), True, "rs_start_w_up", g1)

    def bwd_mix_post(d2, du, hh, rr, mm, ga, gb):
        n3 = hh * rr
        dh1 = d2 + _rms_bwd(n3, rr, gb, du)
        r = _rstd(mm)
        n2 = mm * r
        return (dh1, _rms_bwd(n2, r, ga, dh1)), (_colsum(du * n3), _colsum(dh1 * n2))
    dh1, dmix, dg3, dg2 = _rowwise(bwd_mix_post, [dh2, du2, h1, r3, mix], [g2, g3], [(D, F32), (D, BF16)],
                                   [(1, D), (1, D)], tm=512, name="bwd_post_mix", deps=[tok_up])

    gw_o = _mm(cat, dmix, ta=True, tm=512, tn=1024, tk=T, out_dtypes=(BF16,), name="gw_out")
    rs_o, tok_o = _send_start(gw_o.reshape(N_DEV, 128, D), True, "rs_start_w_out", g1)
    dcat, dattn4, dattn16 = _dx_out(dmix, w_o, tok_o)

    dact, ddtw, ssd_par, dz, dgs = _ssd_bwd(act, xbcdt, bias_w, alog_w, dsk_w, states, y_ssd, qkvz, dcat, ssd_norm_g)
    dxbcdt, conv_par = _conv_bwd(xbcdt, dact, ddtw, conv_full, conv_b)

    qkv_grads = [_attn_bwd(*qkv[0], dcat, attn, lse, 1),
                 _attn_bwd(*qkv[1], dattn4, attn4, lse4, 4),
                 _attn_bwd(*qkv[2], dattn16, attn16, lse16, 16)]
    dqkvz = _rope_bwd(qkv_grads, dz, cos128, sin128)

    gw_qkvzT = _mm(dqkvz, u1, ta=True, tm=512, tn=1024, tk=T, out_dtypes=(BF16,), name="gw_qkvz")
    gw_xbcdtT = _mm(dxbcdt, u1, ta=True, tm=896, tn=1024, tk=T, out_dtypes=(BF16,), name="gw_xbcdt")
    gw_inT = jnp.concatenate([gw_qkvzT, gw_xbcdtT], axis=0)[:IN_W]
    gw_inT = jnp.pad(gw_inT.reshape(N_DEV, W_IN_SHARD, D), ((0, 0), (0, W_IN_SHARD_PAD - W_IN_SHARD), (0, 0)))
    rs_in, tok_in = _send_start(gw_inT, True, "rs_start_w_in", g1)

    def bwd_in(ua, ub, d1, xx, g):
        rr = _rstd(xx)
        n = xx * rr
        du = ua + ub
        return (d1 + _rms_bwd(n, rr, g, du),), (_colsum(du * n),)
    grad_x, dg1 = _mm_rows(bwd_in, [(dqkvz, w_qkvzT, False), (dxbcdt, w_xbcdtT, False)], [dh1, x2], [g1],
                           [(D, F32)], [(1, D)], tm=512, name="bwd_in_proj", deps=[tok_in])

    my_slab = _slab_pack([(dg1, 0), (dg2, 1), (dg3, 2), (dg4, 3), (dg5, 4), (dgs, 5), (loss_vec, 6),
                          (conv_par, 8), (ssd_par, 16)], "slab_pack")
    slab_handles, tok_slab = _send_start_many([my_slab], False, "slab_start", g1)

    def scatter_finish(handles, nm, after):
        part, land = _send_wait(handles, after, "rs_wait_" + nm)
        own = lax.dynamic_slice(part, (me, 0, 0), (1,) + part.shape[1:])
        return _sum_slots(lax.dynamic_update_slice(land, own, (me, 0, 0)), "rs_sum_" + nm)
    g_out = scatter_finish(rs_o, "w_out", tok_slab)
    g_upT = scatter_finish(rs_up, "w_up", tok_slab)
    g_dn = scatter_finish(rs_dn, "w_down", tok_slab)
    ple_parts, ple_lands = _send_wait_many(rs_ple, tok_slab, "rs_wait_w_ple")
    g_projT, g_gate = [
        _sum_slots(lax.dynamic_update_slice(land, lax.dynamic_slice(part, (me, 0, 0), (1,) + part.shape[1:]), (me, 0, 0)),
                   "rs_sum_" + nm) for part, land, nm in zip(ple_parts, ple_lands, ("w_proj", "w_gate"))]

    grads = {
        "w_out": g_out[None], "w_up": g_upT.T[None], "w_down": g_dn[None],
        "w_ple_gate": g_gate[None], "w_ple_proj": g_projT.reshape(128, PLE).T[None],
    }
    delta, new_m, new_v = {}, {}, {}
    for nme in ["w_out", "w_up", "w_down", "w_ple_gate", "w_ple_proj", "w_in"]:
        if nme == "w_in":
            g_inT = scatter_finish(rs_in, "w_in", delta["w_down"])
            grads["w_in"] = g_inT[:W_IN_SHARD].T[None]
        dl, mm_, vv_ = _adamw(args[nme][0], grads[nme][0], args["m_" + nme][0], args["v_" + nme][0], "adamw_" + nme)
        delta[nme], new_m[nme], new_v[nme] = dl[None], mm_[None], vv_[None]

    slab_back, slab_land = _send_wait_many(slab_handles, delta["w_in"], "slab_wait")
    slab = _sum_slots(lax.dynamic_update_slice(slab_land[0], slab_back[0][None], (me, 0, 0)), "slab_sum")
    g_conv_w = lax.dynamic_slice(slab[8:12, :CONV_CH], (0, me * 96), (CONV_K, 96))
    small_names = SMALL + ["conv_w"]
    small_rows = [0, 1, 2, 3, 4, 12, 5, 16, 17, 18, None]
    pick = lambda prefix: [args[prefix + nme] for nme in SMALL] + [args[prefix + "conv_w"][0]]
    loss11, g_s, d_s, m_s, v_s = _adamw_small(slab, small_rows, g_conv_w, pick(""), pick("m_"), pick("v_"))
    loss = loss11[0, 0]
    for i, nme in enumerate(small_names):
        lead = (lambda t: t[None]) if nme == "conv_w" else (lambda t: t)
        grads[nme], delta[nme], new_m[nme], new_v[nme] = lead(g_s[i]), lead(d_s[i]), lead(m_s[i]), lead(v_s[i])

    order = ["norm_mix_pre", "norm_mix_post", "w_in", "conv_w", "conv_b", "dt_bias", "a_log", "d_skip", "ssd_norm_g",
             "w_out", "norm_mlp_pre", "norm_mlp_post", "w_up", "w_down", "w_ple_gate", "w_ple_proj", "norm_ple_post"]
    return (loss, grad_x[None], *[grads[n] for n in order], *[delta[n] for n in order],
            *[new_m[n] for n in order], *[new_v[n] for n in order])
```

```python
import functools
import math

import jax
import jax.numpy as jnp
from jax import lax
from jax.experimental import pallas as pl
from jax.experimental.pallas import tpu as pltpu

F32 = jnp.float32
BF16 = jnp.bfloat16
MESH = pl.DeviceIdType.MESH
HIGHEST = lax.Precision.HIGHEST

N_DEV = 8
T = 4096
D = 1024
HEADS = 8
HD = 64
AW = 512
NS = 128
CONV_K = 4
CONV_CH = 768
CHUNK = 128
DFF = 4096
PLE = 256
EPS = 1e-6
ROPE_THETA = 10000.0
DILATIONS = (1, 4, 16)
QBLK = 128
NEG = -1e30
IN_W = 2824
W_IN_SHARD = 353
W_IN_SHARD_PAD = 384
DT_PAD = 128

ADAM_LR, ADAM_B1, ADAM_B2, ADAM_EPS, ADAM_WD, ADAM_STEP = 0.001, 0.9, 0.999, 1e-08, 0.01, 10

VMEM_LIMIT = 56 * 1024 * 1024


_ANY = pl.BlockSpec(memory_space=pl.ANY)


def _cparams(sem=None):
    return pltpu.CompilerParams(dimension_semantics=sem, vmem_limit_bytes=VMEM_LIMIT)


def _dot(a, b, ca, cb, precision=None):
    return lax.dot_general(a, b, (((ca,), (cb,)), ((), ())), preferred_element_type=F32, precision=precision)


def _nn(a, b):
    return _dot(a, b, 1, 0)


def _nt(a, b):
    return _dot(a, b, 1, 1)


def _tn(a, b):
    return _dot(a, b, 0, 0)


def _sigmoid(x):
    return 1.0 / (1.0 + jnp.exp(-x))


def _softplus(x):
    return jnp.maximum(x, 0.0) + jnp.log(1.0 + jnp.exp(-jnp.abs(x)))


def _mm(a, b, *, ta=False, tb=False, tm, tn, tk, name,
        a_pre=None, a_rows=(), a_cols=(), b_pre=None, b_rows=(), b_cols=(),
        epi=None, epi_tiles=(), out_dtypes=(F32,), deps=()):
    if ta:
        K, M = a.shape
    else:
        M, K = a.shape
    if tb:
        N, K2 = b.shape
    else:
        K2, N = b.shape
    assert K == K2 and M % tm == 0 and N % tn == 0 and K % tk == 0, (name, a.shape, b.shape)
    nk = K // tk
    if ta:
        a_spec = pl.BlockSpec((tk, tm), lambda i, j, k: (k, i))
        a_row_specs = [pl.BlockSpec((tk, 1), lambda i, j, k: (k, 0)) for _ in a_rows]
        a_col_specs = [pl.BlockSpec((1, tm), lambda i, j, k: (0, i)) for _ in a_cols]
    else:
        a_spec = pl.BlockSpec((tm, tk), lambda i, j, k: (i, k))
        a_row_specs = [pl.BlockSpec((tm, 1), lambda i, j, k: (i, 0)) for _ in a_rows]
        a_col_specs = [pl.BlockSpec((1, tk), lambda i, j, k: (0, k)) for _ in a_cols]
    if tb:
        b_spec = pl.BlockSpec((tn, tk), lambda i, j, k: (j, k))
        b_row_specs = [pl.BlockSpec((tn, 1), lambda i, j, k: (j, 0)) for _ in b_rows]
        b_col_specs = [pl.BlockSpec((1, tk), lambda i, j, k: (0, k)) for _ in b_cols]
    else:
        b_spec = pl.BlockSpec((tk, tn), lambda i, j, k: (k, j))
        b_row_specs = [pl.BlockSpec((tk, 1), lambda i, j, k: (k, 0)) for _ in b_rows]
        b_col_specs = [pl.BlockSpec((1, tn), lambda i, j, k: (0, j)) for _ in b_cols]
    o_spec = pl.BlockSpec((tm, tn), lambda i, j, k: (i, j))
    na, nb, ne, no = len(a_rows) + len(a_cols), len(b_rows) + len(b_cols), len(epi_tiles), len(out_dtypes)

    def body(*refs):
        a_ref, b_ref = refs[0], refs[1]
        a_ex = refs[2:2 + na]
        b_ex = refs[2 + na:2 + na + nb]
        e_ex = refs[2 + na + nb:2 + na + nb + ne]
        first_out = 2 + na + nb + ne + len(deps)
        outs = refs[first_out:first_out + no]

        def finish(res):
            vals = epi(res, *[r[...] for r in e_ex]) if epi is not None else (res,)
            for o_ref, val in zip(outs, vals):
                o_ref[...] = val.astype(o_ref.dtype)

        at = a_ref[...]
        if a_pre is not None:
            at = a_pre(at, *[r[...] for r in a_ex])
        bt = b_ref[...]
        if b_pre is not None:
            bt = b_pre(bt, *[r[...] for r in b_ex])
        prod = _dot(at.astype(BF16), bt.astype(BF16), 0 if ta else 1, 1 if tb else 0)
        if nk == 1:
            finish(prod)
            return
        acc = refs[-1]
        k = pl.program_id(2)

        @pl.when(k == 0)
        def _():
            acc[...] = jnp.zeros_like(acc)
        acc[...] += prod

        @pl.when(k == nk - 1)
        def _():
            finish(acc[...])

    outs = pl.pallas_call(
        body, name=name,
        grid=(M // tm, N // tn, nk),
        in_specs=([a_spec, b_spec] + a_row_specs + a_col_specs + b_row_specs + b_col_specs + [o_spec] * ne
                  + [_ANY] * len(deps)),
        out_specs=[o_spec] * no,
        out_shape=[jax.ShapeDtypeStruct((M, N), dt) for dt in out_dtypes],
        scratch_shapes=[pltpu.VMEM((tm, tn), F32)] if nk > 1 else [],
        compiler_params=_cparams(("parallel", "parallel", "arbitrary")),
    )(a, b, *a_rows, *a_cols, *b_rows, *b_cols, *epi_tiles, *deps)
    return outs[0] if no == 1 else outs


MLP_TM = 1024
MLP_TC = 512


def _mlp_fwd(h, r, g, w_upT, w_dn, g_post):
    nc = DFF // MLP_TC

    def body(h_ref, r_ref, g_ref, wu_ref, wd_ref, gp_ref, a_ref, ff_ref, u_ref, ho_ref, hob_ref, acc, u_scr):
        c = pl.program_id(1)

        @pl.when(c == 0)
        def _():
            u = (h_ref[...] * r_ref[...] * g_ref[...]).astype(BF16)
            u_scr[...] = u
            u_ref[...] = u
            acc[...] = jnp.zeros_like(acc)
        a = _nt(u_scr[...], wu_ref[...])
        a_ref[...] = a.astype(BF16)
        acc[...] += _nn(jnp.square(jnp.maximum(a, 0.0)).astype(BF16), wd_ref[...])

        @pl.when(c == nc - 1)
        def _():
            f = acc[...]
            ff_ref[...] = f
            ho = h_ref[...] + f * _rstd(f) * gp_ref[...]
            ho_ref[...] = ho
            hob_ref[...] = ho.astype(BF16)

    row = pl.BlockSpec((MLP_TM, D), lambda i, c: (i, 0))
    wsp = pl.BlockSpec((MLP_TC, D), lambda i, c: (c, 0))
    vec = pl.BlockSpec((1, D), lambda i, c: (0, 0))
    return pl.pallas_call(
        body, name="mlp_fwd", grid=(T // MLP_TM, nc),
        in_specs=[row, pl.BlockSpec((MLP_TM, 1), lambda i, c: (i, 0)), vec, wsp, wsp, vec],
        out_specs=[pl.BlockSpec((MLP_TM, MLP_TC), lambda i, c: (i, c)), row, row, row, row],
        out_shape=[jax.ShapeDtypeStruct((T, DFF), BF16), jax.ShapeDtypeStruct((T, D), F32), jax.ShapeDtypeStruct((T, D), BF16),
                   jax.ShapeDtypeStruct((T, D), F32), jax.ShapeDtypeStruct((T, D), BF16)],
        scratch_shapes=[pltpu.VMEM((MLP_TM, D), F32), pltpu.VMEM((MLP_TM, D), BF16)],
        compiler_params=_cparams(("parallel", "arbitrary")),
    )(h, r, g, w_upT, w_dn, g_post)


def _mlp_dx(dff, a, w_upT, w_dn, dep):
    nc = DFF // MLP_TC

    def body(d_ref, a_ref, wu_ref, wd_ref, dep_ref, da_ref, du_ref, acc, d_scr):
        c = pl.program_id(1)

        @pl.when(c == 0)
        def _():
            d_scr[...] = d_ref[...].astype(BF16)
            acc[...] = jnp.zeros_like(acc)
        da = (_nt(d_scr[...], wd_ref[...]) * (2.0 * jnp.maximum(a_ref[...].astype(F32), 0.0))).astype(BF16)
        da_ref[...] = da
        acc[...] += _nn(da, wu_ref[...])

        @pl.when(c == nc - 1)
        def _():
            du_ref[...] = acc[...]

    row = pl.BlockSpec((MLP_TM, D), lambda i, c: (i, 0))
    wsp = pl.BlockSpec((MLP_TC, D), lambda i, c: (c, 0))
    chunk = pl.BlockSpec((MLP_TM, MLP_TC), lambda i, c: (i, c))
    return pl.pallas_call(
        body, name="mlp_dx", grid=(T // MLP_TM, nc),
        in_specs=[row, chunk, wsp, wsp, _ANY], out_specs=[chunk, row],
        out_shape=[jax.ShapeDtypeStruct((T, DFF), BF16), jax.ShapeDtypeStruct((T, D), F32)],
        scratch_shapes=[pltpu.VMEM((MLP_TM, D), F32), pltpu.VMEM((MLP_TM, D), BF16)],
        compiler_params=_cparams(("parallel", "arbitrary")),
    )(dff, a, w_upT, w_dn, dep)


def _rowwise(fn, rows, vecs, out_rows, out_sums, *, tm, name, deps=()):
    specs, arrs = [], []
    R = None
    for r in rows:
        if isinstance(r, tuple):
            arr, width, cb = r
            specs.append(pl.BlockSpec((tm, width), lambda i, cb=cb: (i, cb)))
        else:
            arr = r
            specs.append(pl.BlockSpec((tm, arr.shape[1]), lambda i: (i, 0)))
        R = arr.shape[0] if R is None else R
        assert arr.shape[0] == R, name
        arrs.append(arr)
    assert R % tm == 0, name
    for v in vecs:
        specs.append(pl.BlockSpec(v.shape, lambda i: (0, 0)))
        arrs.append(v)
    nr, nv, no, ns = len(rows), len(vecs), len(out_rows), len(out_sums)
    out_specs = [pl.BlockSpec((tm, w), lambda i: (i, 0)) for w, _ in out_rows]
    out_specs += [pl.BlockSpec(s, lambda i: (0, 0)) for s in out_sums]
    out_shape = [jax.ShapeDtypeStruct((R, w), dt) for w, dt in out_rows]
    out_shape += [jax.ShapeDtypeStruct(s, F32) for s in out_sums]

    nd = len(deps)

    def body(*refs):
        ins = [r[...] for r in refs[:nr + nv]]
        o_refs = refs[nr + nv + nd:nr + nv + nd + no]
        s_refs = refs[nr + nv + nd + no:]
        o_vals, s_vals = fn(*ins)
        for ref, val in zip(o_refs, o_vals):
            ref[...] = val.astype(ref.dtype)
        if ns:
            @pl.when(pl.program_id(0) == 0)
            def _():
                for ref in s_refs:
                    ref[...] = jnp.zeros_like(ref)
            for ref, val in zip(s_refs, s_vals):
                ref[...] += val

    outs = pl.pallas_call(
        body, name=name, grid=(R // tm,), in_specs=specs + [_ANY] * nd, out_specs=out_specs, out_shape=out_shape,
        compiler_params=_cparams(("arbitrary",) if ns else ("parallel",)),
    )(*arrs, *deps)
    return outs


def _mm_rows(fn, mats, rows, vecs, out_rows, out_sums, *, tm, name, deps=()):
    R = mats[0][0].shape[0]
    assert R % tm == 0, name
    specs, arrs = [], []
    for a, b, tb in mats:
        specs += [pl.BlockSpec((tm, a.shape[1]), lambda i: (i, 0)), pl.BlockSpec(b.shape, lambda i: (0, 0))]
        arrs += [a, b]
    for r in rows:
        specs.append(pl.BlockSpec((tm, r.shape[1]), lambda i: (i, 0)))
        arrs.append(r)
    for v in vecs:
        specs.append(pl.BlockSpec(v.shape, lambda i: (0, 0)))
        arrs.append(v)
    nm, nr, nv, nd, no, ns = len(mats), len(rows), len(vecs), len(deps), len(out_rows), len(out_sums)
    out_specs = [pl.BlockSpec((tm, w), lambda i: (i, 0)) for w, _ in out_rows]
    out_specs += [pl.BlockSpec(s, lambda i: (0, 0)) for s in out_sums]
    out_shape = [jax.ShapeDtypeStruct((R, w), dt) for w, dt in out_rows] + [jax.ShapeDtypeStruct(s, F32) for s in out_sums]

    def body(*refs):
        prods = [_dot(refs[2 * p][...].astype(BF16), refs[2 * p + 1][...].astype(BF16), 1, 1 if mats[p][2] else 0)
                 for p in range(nm)]
        ins = [r[...] for r in refs[2 * nm:2 * nm + nr + nv]]
        first_out = 2 * nm + nr + nv + nd
        o_refs, s_refs = refs[first_out:first_out + no], refs[first_out + no:]
        o_vals, s_vals = fn(*prods, *ins)
        for ref, val in zip(o_refs, o_vals):
            ref[...] = val.astype(ref.dtype)
        if ns:
            @pl.when(pl.program_id(0) == 0)
            def _():
                for ref in s_refs:
                    ref[...] = jnp.zeros_like(ref)
            for ref, val in zip(s_refs, s_vals):
                ref[...] += val

    return pl.pallas_call(
        body, name=name, grid=(R // tm,), in_specs=specs + [_ANY] * nd, out_specs=out_specs, out_shape=out_shape,
        compiler_params=_cparams(("arbitrary",) if ns else ("parallel",)),
    )(*arrs, *deps)


def _colsum(x):
    return jnp.sum(x, axis=0, keepdims=True)


def _rstd(x):
    return lax.rsqrt(jnp.mean(x * x, axis=-1, keepdims=True) + EPS)


def _rms_bwd(xn, r, g, dy):
    dn = dy * g
    return r * (dn - xn * jnp.mean(dn * xn, axis=-1, keepdims=True))


def _partner(t):
    lane = lax.broadcasted_iota(jnp.int32, t.shape, 1)
    up = pltpu.roll(t, 96, 1)
    down = pltpu.roll(t, 32, 1)
    return jnp.where((lane % 64) < 32, up, down)


SLABS = AW // 128


def _rows(r, n, d):
    return pl.ds(r, n, stride=d) if d > 1 else pl.ds(0, n)


def _undilate(src_ref, dst, d, tm):
    for r in range(d):
        for j in range(SLABS):
            dst[j][_rows(r, tm // d, d), :] = src_ref[:, pl.ds(r * AW + j * 128, 128)].astype(dst[j].dtype)


def _dilate(dst_ref, src, d, tm):
    for r in range(d):
        for j in range(SLABS):
            dst_ref[:, pl.ds(r * AW + j * 128, 128)] = src[j][_rows(r, tm // d, d), :].astype(dst_ref.dtype)


def _slab_scratch(n, tm):
    return [pltpu.VMEM((tm, 128), F32)] * (SLABS * n)


def _slab_groups(flat):
    return [flat[SLABS * i:SLABS * (i + 1)] for i in range(len(flat) // SLABS)]


def _slab_specs(tm, first):
    return [pl.BlockSpec((tm, 128), lambda i, j=j: (i, first + j)) for j in range(SLABS)]


def _dil_spec(tm, d):
    return pl.BlockSpec((tm // d, d * AW), lambda i: (i, 0))


ROPE_TM = 512


def _rope_fwd(qkvz, cos128, sin128):
    tm = ROPE_TM

    def body(*refs):
        q_refs, k_refs, v_refs = refs[0:4], refs[4:8], refs[8:12]
        c_ref, s_ref = refs[12], refs[13]
        outs = refs[14:23]
        qs, ks = _slab_groups(refs[23:])
        c, s = c_ref[...], s_ref[...]
        for j in range(SLABS):
            q, k = q_refs[j][...], k_refs[j][...]
            qs[j][...] = (q * c + _partner(q) * s) * (HD ** -0.5)
            ks[j][...] = k * c + _partner(k) * s
        for di, d in enumerate(DILATIONS):
            oq, ok, ov = outs[3 * di:3 * di + 3]
            for r in range(d):
                rows = _rows(r, tm // d, d)
                for j in range(SLABS):
                    cols = pl.ds(r * AW + j * 128, 128)
                    oq[:, cols] = qs[j][rows, :].astype(BF16)
                    ok[:, cols] = ks[j][rows, :].astype(BF16)
                    ov[:, cols] = v_refs[j][rows, :].astype(BF16)

    tab = pl.BlockSpec((tm, 128), lambda i: (i, 0))
    out_specs, out_shape = [], []
    for d in DILATIONS:
        out_specs += [_dil_spec(tm, d)] * 3
        out_shape += [jax.ShapeDtypeStruct((T // d, d * AW), BF16)] * 3
    return pl.pallas_call(
        body, name="rope_fwd", grid=(T // tm,),
        in_specs=_slab_specs(tm, 0) + _slab_specs(tm, 4) + _slab_specs(tm, 8) + [tab, tab],
        out_specs=out_specs, out_shape=out_shape, scratch_shapes=_slab_scratch(2, tm),
        compiler_params=_cparams(("parallel",)),
    )(*([qkvz] * 12), cos128, sin128)


def _rope_bwd(grads, dz, cos128, sin128):
    tm = 256

    def body(*refs):
        g_refs = refs[0:9]
        dz_ref, c_ref, s_ref, o_ref = refs[9], refs[10], refs[11], refs[12]
        scr = _slab_groups(refs[13:])
        for di, d in enumerate(DILATIONS[1:]):
            for t in range(3):
                _undilate(g_refs[3 * (di + 1) + t], scr[3 * di + t], d, tm)
        c, s = c_ref[...], s_ref[...]
        for j in range(SLABS):
            cols = pl.ds(j * 128, 128)
            tot = [g_refs[t][:, cols] + scr[t][j][...] + scr[3 + t][j][...] for t in range(3)]
            dqr = tot[0] * (HD ** -0.5)
            o_ref[:, pl.ds(j * 128, 128)] = (dqr * c + _partner(dqr * s)).astype(BF16)
            o_ref[:, pl.ds(AW + j * 128, 128)] = (tot[1] * c + _partner(tot[1] * s)).astype(BF16)
            o_ref[:, pl.ds(2 * AW + j * 128, 128)] = tot[2].astype(BF16)
        o_ref[:, pl.ds(3 * AW, AW)] = dz_ref[...].astype(BF16)

    tab = pl.BlockSpec((tm, 128), lambda i: (i, 0))
    in_specs, args = [], []
    for d, g in zip(DILATIONS, grads):
        in_specs += [_dil_spec(tm, d)] * 3
        args += list(g)
    return pl.pallas_call(
        body, name="rope_bwd", grid=(T // tm,),
        in_specs=in_specs + [pl.BlockSpec((tm, AW), lambda i: (i, 0)), tab, tab],
        out_specs=pl.BlockSpec((tm, 4 * AW), lambda i: (i, 0)),
        out_shape=jax.ShapeDtypeStruct((T, 4 * AW), BF16),
        scratch_shapes=_slab_scratch(6, tm),
        compiler_params=_cparams(("parallel",)),
    )(*args, dz, cos128, sin128)


def _dx_out(dmix, w_o, dep):
    tm = ROPE_TM

    def body(a_ref, w_ref, dep_ref, dcat_ref, o4, o16, *slabs):
        prod = _nt(a_ref[...].astype(BF16), w_ref[...].astype(BF16))
        dcat_ref[...] = prod
        for j in range(SLABS):
            slabs[j][...] = prod[:, 128 * j:128 * (j + 1)]
        _dilate(o4, slabs, 4, tm)
        _dilate(o16, slabs, 16, tm)

    return pl.pallas_call(
        body, name="dx_out", grid=(T // tm,),
        in_specs=[pl.BlockSpec((tm, D), lambda i: (i, 0)), pl.BlockSpec((D, D), lambda i: (0, 0)), _ANY],
        out_specs=[pl.BlockSpec((tm, D), lambda i: (i, 0)), _dil_spec(tm, 4), _dil_spec(tm, 16)],
        out_shape=[jax.ShapeDtypeStruct((T, D), F32), jax.ShapeDtypeStruct((T // 4, 4 * AW), F32),
                   jax.ShapeDtypeStruct((T // 16, 16 * AW), F32)],
        scratch_shapes=_slab_scratch(1, tm), compiler_params=_cparams(("parallel",)),
    )(dmix, w_o, dep)


def _band_masks():
    qi = lax.broadcasted_iota(jnp.int32, (QBLK, QBLK), 0)
    kj = lax.broadcasted_iota(jnp.int32, (QBLK, QBLK), 1)
    return kj >= qi, kj <= qi


def _attn_fwd(q, k, v, d):
    L = q.shape[0]
    npair = L // (2 * QBLK)

    def body(q_ref, kp_ref, kc_ref, vp_ref, vc_ref, o_ref, l_ref):
        pair = pl.program_id(1)
        mask_p, mask_c = _band_masks()
        for sub in range(2):
            rows = pl.ds(sub * QBLK, QBLK)
            first = jnp.where(pair > 0, 0.0, NEG) if sub == 0 else 0.0
            bias = jnp.concatenate([jnp.where(mask_p, 0.0, NEG) + first, jnp.where(mask_c, 0.0, NEG)], axis=1)
            k_prev = (lambda sl: kp_ref[:, sl]) if sub == 0 else (lambda sl: kc_ref[pl.ds(0, QBLK), sl])
            v_prev = (lambda sl: vp_ref[:, sl]) if sub == 0 else (lambda sl: vc_ref[pl.ds(0, QBLK), sl])
            s = []
            for h in range(HEADS):
                sl = pl.ds(HD * h, HD)
                qh = q_ref[rows, sl]
                s.append(jnp.concatenate([_nt(qh, k_prev(sl)), _nt(qh, kc_ref[rows, sl])], axis=1))
            s = jnp.stack(s) + bias
            m = jnp.max(s, axis=2, keepdims=True)
            e = jnp.exp(s - m)
            den = jnp.sum(e, axis=2, keepdims=True)
            p = e.astype(BF16)
            inv = 1.0 / den
            lse = m + jnp.log(den)
            for h in range(HEADS):
                sl = pl.ds(HD * h, HD)
                o_ref[rows, sl] = (_nn(p[h, :, :QBLK], v_prev(sl)) + _nn(p[h, :, QBLK:], vc_ref[rows, sl])) * inv[h]
                l_ref[rows, sl] = jnp.broadcast_to(lse[h], (QBLK, HD))

    cur = pl.BlockSpec((2 * QBLK, AW), lambda r, n: (n, r))
    prev = pl.BlockSpec((QBLK, AW), lambda r, n: (jnp.maximum(2 * n - 1, 0), r))
    return pl.pallas_call(
        body, name=f"attn_fwd_d{d}", grid=(d, npair),
        in_specs=[cur, prev, cur, prev, cur], out_specs=[cur, cur],
        out_shape=[jax.ShapeDtypeStruct((L, d * AW), F32)] * 2,
        compiler_params=_cparams(("parallel", "parallel")),
    )(q, k, k, v, v)


def _attn_bwd(q, k, v, do, at, lse, d):
    L = q.shape[0]
    nb = L // QBLK

    def body(q0_ref, q1_ref, kp_ref, kc_ref, vp_ref, vc_ref, do0_ref, do1_ref, at0_ref, at1_ref,
             l0_ref, l1_ref, dq_ref, dk_ref, dv_ref):
        n = pl.program_id(1)
        mask_p, mask_c = _band_masks()
        prev_bias = jnp.where(mask_p, 0.0, NEG)
        bias = jnp.concatenate([prev_bias + jnp.where(n > 0, 0.0, NEG), jnp.where(mask_c, 0.0, NEG),
                                prev_bias + jnp.where(n < nb - 1, 0.0, NEG)], axis=1)
        s, dp, ls, dl, ops = [], [], [], [], []
        for h in range(HEADS):
            sl = pl.ds(HD * h, HD)
            one = pl.ds(HD * h, 1)
            q0, q1 = q0_ref[:, sl], q1_ref[:, sl]
            kp, kc, vp, vc = kp_ref[:, sl], kc_ref[:, sl], vp_ref[:, sl], vc_ref[:, sl]
            do0, do1 = do0_ref[:, sl], do1_ref[:, sl]
            do0b, do1b = do0.astype(BF16), do1.astype(BF16)
            s.append(jnp.concatenate([_nt(q0, kp), _nt(q0, kc), _nt(q1, kc)], axis=1))
            dp.append(jnp.concatenate([_nt(do0b, vp), _nt(do0b, vc), _nt(do1b, vc)], axis=1))
            dl0 = jnp.sum(do0 * at0_ref[:, sl], axis=1, keepdims=True)
            dl1 = jnp.sum(do1 * at1_ref[:, sl], axis=1, keepdims=True)
            dl.append(jnp.concatenate([jnp.broadcast_to(dl0, (QBLK, 2 * QBLK)), jnp.broadcast_to(dl1, (QBLK, QBLK))], axis=1))
            ls.append(jnp.concatenate([jnp.broadcast_to(l0_ref[:, one], (QBLK, 2 * QBLK)),
                                       jnp.broadcast_to(l1_ref[:, one], (QBLK, QBLK))], axis=1))
            ops.append((q0, q1, kp, kc, do0b, do1b))
        p = jnp.exp(jnp.stack(s) + bias - jnp.stack(ls))
        ds = (p * (jnp.stack(dp) - jnp.stack(dl))).astype(BF16)
        p = p.astype(BF16)
        for h in range(HEADS):
            sl = pl.ds(HD * h, HD)
            q0, q1, kp, kc, do0b, do1b = ops[h]
            dq_ref[:, sl] = (_nn(ds[h, :, :QBLK], kp) + _nn(ds[h, :, QBLK:2 * QBLK], kc)).astype(BF16)
            dv_ref[:, sl] = (_tn(p[h, :, QBLK:2 * QBLK], do0b) + _tn(p[h, :, 2 * QBLK:], do1b)).astype(BF16)
            dk_ref[:, sl] = (_tn(ds[h, :, QBLK:2 * QBLK], q0) + _tn(ds[h, :, 2 * QBLK:], q1)).astype(BF16)

    cur = pl.BlockSpec((QBLK, AW), lambda r, n: (n, r))
    prev = pl.BlockSpec((QBLK, AW), lambda r, n: (jnp.maximum(n - 1, 0), r))
    nxt = pl.BlockSpec((QBLK, AW), lambda r, n: (jnp.minimum(n + 1, nb - 1), r))
    return pl.pallas_call(
        body, name=f"attn_bwd_d{d}", grid=(d, nb),
        in_specs=[cur, nxt, prev, cur, prev, cur, cur, nxt, cur, nxt, cur, nxt], out_specs=[cur, cur, cur],
        out_shape=[jax.ShapeDtypeStruct((L, d * AW), BF16)] * 3,
        compiler_params=_cparams(("parallel", "parallel")),
    )(q, q, k, k, v, v, do, do, at, at, lse, lse)


def _attn_merge(outs, lses):
    tm = ROPE_TM

    def body(o1, o4, o16, l1, l4, l16, at_ref, ls_ref, at4, ls4, at16, ls16, *flat):
        so4, so16, sl4, sl16, sa, sl = _slab_groups(flat)
        _undilate(o4, so4, 4, tm)
        _undilate(o16, so16, 16, tm)
        _undilate(l4, sl4, 4, tm)
        _undilate(l16, sl16, 16, tm)
        for j in range(SLABS):
            cols = pl.ds(j * 128, 128)
            a, b, c = l1[:, cols], sl4[j][...], sl16[j][...]
            m = jnp.maximum(jnp.maximum(a, b), c)
            e1, e2, e3 = jnp.exp(a - m), jnp.exp(b - m), jnp.exp(c - m)
            s = e1 + e2 + e3
            inv = 1.0 / s
            attn = (e1 * inv) * o1[:, cols] + (e2 * inv) * so4[j][...] + (e3 * inv) * so16[j][...]
            lse = m + jnp.log(s)
            at_ref[:, cols] = attn
            ls_ref[:, cols] = lse
            sa[j][...] = attn
            sl[j][...] = lse
        _dilate(at4, sa, 4, tm)
        _dilate(at16, sa, 16, tm)
        _dilate(ls4, sl, 4, tm)
        _dilate(ls16, sl, 16, tm)

    specs = [_dil_spec(tm, d) for d in DILATIONS]
    tok = specs[0]
    return pl.pallas_call(
        body, name="attn_merge", grid=(T // tm,),
        in_specs=specs + specs, out_specs=[tok, tok, specs[1], specs[1], specs[2], specs[2]],
        out_shape=[jax.ShapeDtypeStruct((T, AW), F32)] * 2 + [jax.ShapeDtypeStruct((T // 4, 4 * AW), F32)] * 2
        + [jax.ShapeDtypeStruct((T // 16, 16 * AW), F32)] * 2,
        scratch_shapes=_slab_scratch(6, tm),
        compiler_params=_cparams(("parallel",)),
    )(*outs, *lses)


CONV_TM = 512
HALO = 8


def _conv_pre(ext, w, b):
    y = b + w[3] * ext
    for kk in range(1, CONV_K):
        y = y + w[3 - kk] * pltpu.roll(ext, kk, 0)
    return y


def _rows_to_block(rows, n, width):
    ri = lax.broadcasted_iota(jnp.int32, (n, width), 0)
    out = jnp.zeros((n, width), F32)
    for j, r in enumerate(rows):
        out = out + jnp.where(ri == j, r, 0.0)
    return out


def _conv_bwd(xbc, dact, ddt, w, b):
    nblk = T // CONV_TM
    per = CONV_TM // HALO

    def body(x_ref, xb_ref, xa_ref, g_ref, ga_ref, ddt_ref, w_ref, b_ref, dx_ref, dw_ref):
        i = pl.program_id(0)
        wv = [w_ref[pl.ds(j, 1), :] for j in range(CONV_K)]
        before = jnp.where(i > 0, xb_ref[...], 0.0)
        last = i == nblk - 1
        after = jnp.where(last, 0.0, xa_ref[...])
        g_after = jnp.where(last, 0.0, ga_ref[...])
        ext = jnp.concatenate([before, x_ref[...], after], axis=0)
        y = _conv_pre(ext, wv, b_ref[...])[HALO:]
        sg = _sigmoid(y)
        dy = jnp.concatenate([g_ref[...], g_after], axis=0) * (sg * (1.0 + y * (1.0 - sg)))
        n = CONV_TM + HALO
        dx = wv[3] * dy
        for kk in range(1, CONV_K):
            dx = dx + wv[3 - kk] * pltpu.roll(dy, n - kk, 0)
        dx_ref[:, pl.ds(0, CONV_CH)] = dx[:CONV_TM].astype(BF16)
        dx_ref[:, pl.ds(CONV_CH, DT_PAD)] = ddt_ref[...].astype(BF16)
        dyc = dy[:CONV_TM]
        rows = [jnp.sum(dyc * (pltpu.roll(ext, 3 - j, 0) if j < 3 else ext)[HALO:HALO + CONV_TM], axis=0, keepdims=True)
                for j in range(CONV_K)]
        rows.append(jnp.sum(dyc, axis=0, keepdims=True))
        part = _rows_to_block(rows, 8, CONV_CH)

        @pl.when(i == 0)
        def _():
            dw_ref[...] = jnp.zeros_like(dw_ref)
        dw_ref[...] += part

    blk = pl.BlockSpec((CONV_TM, CONV_CH), lambda i: (i, 0))
    hb = pl.BlockSpec((HALO, CONV_CH), lambda i: (jnp.maximum(i * per - 1, 0), 0))
    ha = pl.BlockSpec((HALO, CONV_CH), lambda i: (jnp.minimum((i + 1) * per, T // HALO - 1), 0))
    return pl.pallas_call(
        body, name="conv_bwd", grid=(nblk,),
        in_specs=[blk, hb, ha, blk, ha, pl.BlockSpec((CONV_TM, DT_PAD), lambda i: (i, 0)),
                  pl.BlockSpec((CONV_K, CONV_CH), lambda i: (0, 0)), pl.BlockSpec((1, CONV_CH), lambda i: (0, 0))],
        out_specs=[pl.BlockSpec((CONV_TM, CONV_CH + DT_PAD), lambda i: (i, 0)), pl.BlockSpec((8, CONV_CH), lambda i: (0, 0))],
        out_shape=[jax.ShapeDtypeStruct((T, CONV_CH + DT_PAD), BF16), jax.ShapeDtypeStruct((8, CONV_CH), F32)],
        compiler_params=_cparams(("arbitrary",)),
    )(xbc, xbc, xbc, dact, dact, ddt, w, b)


def _pick(mat, h):
    lane = lax.broadcasted_iota(jnp.int32, mat.shape, 1)
    return jnp.sum(jnp.where(lane == h, mat, 0.0), axis=1, keepdims=True)


def _heads(fn):
    return jnp.stack([fn(h) for h in range(HEADS)])


def _ssd_prep(dt_ref, bias_ref, alog_ref, dsk_ref, b_ref, c_ref, xs_ref, state_ref, cst):
    li = lax.broadcasted_iota(jnp.int32, (CHUNK, CHUNK), 0)
    si = lax.broadcasted_iota(jnp.int32, (CHUNK, CHUNK), 1)
    tri = li >= si
    dtp = dt_ref[...] + bias_ref[...]
    dt = _softplus(dtp)
    A = -jnp.exp(alog_ref[...])
    a = dt * A
    cs = jnp.dot(tri.astype(F32), a, precision=HIGHEST, preferred_element_type=F32)
    cst[...] = cs.T
    Bm = b_ref[...].astype(BF16)
    Cm = c_ref[...].astype(BF16)
    cb = _nt(Cm, Bm)
    dskv = dsk_ref[...]
    cs_col = _heads(lambda h: _pick(cs, h))
    cs_row = _heads(lambda h: cst[pl.ds(h, 1), :])
    dt_col = _heads(lambda h: _pick(dt, h))
    dsk_col = _heads(lambda h: _pick(dskv, h))
    lam = jnp.exp(jnp.where(tri, cs_col - cs_row, NEG))
    x = _heads(lambda h: xs_ref[:, pl.ds(HD * h, HD)])
    xdt = x * dt_col
    prev = _heads(lambda h: state_ref[pl.ds(HD * h, HD), :])
    lane = lax.broadcasted_iota(jnp.int32, (1, 1, CHUNK), 2)
    cl = jnp.sum(jnp.where(lane == CHUNK - 1, cs_row, 0.0), axis=2, keepdims=True)
    f = jnp.exp(cl - cs_col)
    return dict(li=li, si=si, dtp=dtp, dt=dt, A=A, Bm=Bm, Cm=Cm, cb=cb, cs_col=cs_col, dt_col=dt_col, dsk_col=dsk_col,
                lam=lam, x=x, xdt=xdt, prev=prev, cl=cl, f=f)


def _ssd_fwd(xbcdt, conv_w, conv_b, bias, alog, dsk, qkvz, attn, gs):
    nc = T // CHUNK
    per = CHUNK // HALO

    def body(xbc_ref, halo_ref, cw_ref, cb_ref, dt_ref, bias_ref, alog_ref, dsk_ref, z_ref, at_ref, gs_ref,
             y_ref, st_ref, cat_ref, act_ref, state, cst):
        @pl.when(pl.program_id(0) == 0)
        def _():
            state[...] = jnp.zeros_like(state)
        st_ref[...] = state[...]
        halo = jnp.where(pl.program_id(0) > 0, halo_ref[...], 0.0)
        pre = _conv_pre(jnp.concatenate([halo, xbc_ref[...]], axis=0),
                        [cw_ref[pl.ds(j, 1), :] for j in range(CONV_K)], cb_ref[...])[HALO:]
        act_ref[...] = pre * _sigmoid(pre)
        xs_ref, b_ref, c_ref = (act_ref.at[:, pl.ds(0, AW)], act_ref.at[:, pl.ds(AW, NS)],
                                act_ref.at[:, pl.ds(AW + NS, NS)])
        s = _ssd_prep(dt_ref, bias_ref, alog_ref, dsk_ref, b_ref, c_ref, xs_ref, state, cst)
        Bm, Cm, prev = s["Bm"], s["Cm"], s["prev"]
        g = (s["cb"] * s["lam"]).astype(BF16)
        xdtb = s["xdt"].astype(BF16)
        prevb = prev.astype(BF16)
        y = _heads(lambda h: _nn(g[h], xdtb[h])) + _heads(lambda h: _nt(Cm, prevb[h])) * jnp.exp(s["cs_col"])
        y = y + s["dsk_col"] * s["x"]
        xf = (s["xdt"] * s["f"]).astype(BF16)
        new = prev * jnp.exp(s["cl"]) + _heads(lambda h: _tn(xf[h], Bm))
        for h in range(HEADS):
            y_ref[:, pl.ds(HD * h, HD)] = y[h]
            state[pl.ds(HD * h, HD), :] = new[h]
        z = z_ref[...]
        gi = y_ref[...] * (z * _sigmoid(z))
        cat_ref[:, pl.ds(0, AW)] = at_ref[...].astype(BF16)
        cat_ref[:, pl.ds(AW, AW)] = (gi * _rstd(gi) * gs_ref[...]).astype(BF16)

    vec = pl.BlockSpec((1, DT_PAD), lambda c: (0, 0))
    blk = pl.BlockSpec((CHUNK, AW), lambda c: (c, 0))
    return pl.pallas_call(
        body, name="ssd_fwd", grid=(nc,),
        in_specs=[pl.BlockSpec((CHUNK, CONV_CH), lambda c: (c, 0)),
                  pl.BlockSpec((HALO, CONV_CH), lambda c: (jnp.maximum(c * per - 1, 0), 0)),
                  pl.BlockSpec((CONV_K, CONV_CH), lambda c: (0, 0)), pl.BlockSpec((1, CONV_CH), lambda c: (0, 0)),
                  pl.BlockSpec((CHUNK, DT_PAD), lambda c: (c, 6)),
                  vec, vec, vec, pl.BlockSpec((CHUNK, AW), lambda c: (c, 3)), blk, pl.BlockSpec((1, AW), lambda c: (0, 0))],
        out_specs=[blk, pl.BlockSpec((None, AW, NS), lambda c: (c, 0, 0)), pl.BlockSpec((CHUNK, D), lambda c: (c, 0)),
                   pl.BlockSpec((CHUNK, CONV_CH), lambda c: (c, 0))],
        out_shape=[jax.ShapeDtypeStruct((T, AW), F32), jax.ShapeDtypeStruct((nc, AW, NS), F32),
                   jax.ShapeDtypeStruct((T, D), BF16), jax.ShapeDtypeStruct((T, CONV_CH), F32)],
        scratch_shapes=[pltpu.VMEM((AW, NS), F32), pltpu.VMEM((CHUNK, CHUNK), F32)],
        compiler_params=_cparams(("arbitrary",)),
    )(xbcdt, xbcdt, conv_w, conv_b, xbcdt, bias, alog, dsk, qkvz, attn, gs)


def _ssd_bwd(act, xbcdt, bias, alog, dsk, states, y_ssd, qkvz, dcat, gs):
    nc = T // CHUNK

    def body(xs_ref, b_ref, c_ref, dt_ref, bias_ref, alog_ref, dsk_ref, st_ref, y_ref, z_ref, dyn_ref, gs_ref,
             dact_ref, ddt_ref, par_ref, dz_ref, dgs_ref, dstate, cst, dy_ref):
        step = pl.program_id(0)

        @pl.when(step == 0)
        def _():
            dstate[...] = jnp.zeros_like(dstate)
            par_ref[...] = jnp.zeros_like(par_ref)
            dgs_ref[...] = jnp.zeros_like(dgs_ref)
        z, yv, dyn = z_ref[...], y_ref[...], dyn_ref[...]
        sg = _sigmoid(z)
        sz = z * sg
        gi = yv * sz
        rg = _rstd(gi)
        ng = gi * rg
        dgi = _rms_bwd(ng, rg, gs_ref[...], dyn)
        dy_ref[...] = dgi * sz
        dz_ref[...] = dgi * yv * (sg * (1.0 + z * (1.0 - sg)))
        dgs_ref[...] += _colsum(dyn * ng)
        s = _ssd_prep(dt_ref, bias_ref, alog_ref, dsk_ref, b_ref, c_ref, xs_ref, st_ref, cst)
        Bm, Cm, prev, lam, x, xdt, f, cl = s["Bm"], s["Cm"], s["prev"], s["lam"], s["x"], s["xdt"], s["f"], s["cl"]
        lane = lax.broadcasted_iota(jnp.int32, (1, DT_PAD), 1)
        row = lax.broadcasted_iota(jnp.int32, (1, CHUNK, 1), 1)
        g = s["cb"] * lam
        gb, xdtb, prevb = g.astype(BF16), xdt.astype(BF16), prev.astype(BF16)
        dy = _heads(lambda h: dy_ref[:, pl.ds(HD * h, HD)])
        dyb = dy.astype(BF16)
        dnew = _heads(lambda h: dstate[pl.ds(HD * h, HD), :])
        dnewb = dnew.astype(BF16)
        E = jnp.exp(s["cs_col"])
        ecl = jnp.exp(cl)
        dG = _heads(lambda h: _nt(dyb[h], xdtb[h]))
        dxdt = _heads(lambda h: _tn(gb[h], dyb[h]))
        Yo = _heads(lambda h: _nt(Cm, prevb[h]))
        W = _heads(lambda h: _nt(Bm, dnewb[h]))
        dcb = jnp.sum(dG * lam, axis=0)
        Mm = dG * g
        col_sums = jnp.sum(Mm, axis=1, keepdims=True)
        dYo = (dy * E).astype(BF16)
        dxdt = dxdt + W * f
        dF = jnp.sum(W * xdt, axis=2, keepdims=True) * f
        dcl = jnp.sum(dnew * prev, axis=(1, 2), keepdims=True) * ecl + jnp.sum(dF, axis=1, keepdims=True)
        dcs = (jnp.sum(Mm, axis=2, keepdims=True) + jnp.sum(dy * Yo, axis=2, keepdims=True) * E - dF
               + jnp.where(row == CHUNK - 1, dcl, 0.0))
        ddt_x = jnp.sum(dxdt * x, axis=2, keepdims=True)
        dD = jnp.sum(dy * x, axis=(1, 2), keepdims=True)
        dx = s["dsk_col"] * dy + dxdt * s["dt_col"]
        xfb = (xdt * f).astype(BF16)
        dprev = _heads(lambda h: _tn(dYo[h], Cm)) + dnew * ecl
        dcbb = dcb.astype(BF16)
        dC = _nn(dcbb, Bm)
        dB = _tn(dcbb, Cm)
        dcs_mat = -_rows_to_block([col_sums[h] for h in range(HEADS)], CHUNK, CHUNK).T
        ddt_mat = jnp.zeros((CHUNK, DT_PAD), F32)
        dD_row = jnp.zeros((1, DT_PAD), F32)
        for h in range(HEADS):
            sl = pl.ds(HD * h, HD)
            dC = dC + _nn(dYo[h], prevb[h])
            dB = dB + _nn(xfb[h], dnewb[h])
            dcs_mat = dcs_mat + jnp.where(lane == h, dcs[h], 0.0)
            ddt_mat = ddt_mat + jnp.where(lane == h, ddt_x[h], 0.0)
            dD_row = dD_row + jnp.where(lane == h, dD[h], 0.0)
            dact_ref[:, sl] = dx[h]
            dstate[sl, :] = dprev[h]
        dact_ref[:, pl.ds(AW, NS)] = dB
        dact_ref[:, pl.ds(AW + NS, NS)] = dC
        da = jnp.dot((s["li"] <= s["si"]).astype(F32), dcs_mat, precision=HIGHEST, preferred_element_type=F32)
        ddtp = jnp.where(lane < HEADS, (ddt_mat + da * s["A"]) * _sigmoid(s["dtp"]), 0.0)
        ddt_ref[...] = ddtp
        dalog = jnp.where(lane < HEADS, jnp.sum(da * s["dt"], axis=0, keepdims=True) * s["A"], 0.0)
        par_ref[...] += _rows_to_block([jnp.sum(ddtp, axis=0, keepdims=True), dalog, dD_row], 8, DT_PAD)

    vec = pl.BlockSpec((1, DT_PAD), lambda c: (0, 0))
    rev = lambda c: nc - 1 - c
    return pl.pallas_call(
        body, name="ssd_bwd", grid=(nc,),
        in_specs=[pl.BlockSpec((CHUNK, AW), lambda c: (rev(c), 0)), pl.BlockSpec((CHUNK, NS), lambda c: (rev(c), 4)),
                  pl.BlockSpec((CHUNK, NS), lambda c: (rev(c), 5)), pl.BlockSpec((CHUNK, DT_PAD), lambda c: (rev(c), 6)),
                  vec, vec, vec,
                  pl.BlockSpec((None, AW, NS), lambda c: (rev(c), 0, 0)), pl.BlockSpec((CHUNK, AW), lambda c: (rev(c), 0)),
                  pl.BlockSpec((CHUNK, AW), lambda c: (rev(c), 3)), pl.BlockSpec((CHUNK, AW), lambda c: (rev(c), 1)),
                  pl.BlockSpec((1, AW), lambda c: (0, 0))],
        out_specs=[pl.BlockSpec((CHUNK, CONV_CH), lambda c: (rev(c), 0)), pl.BlockSpec((CHUNK, DT_PAD), lambda c: (rev(c), 0)),
                   pl.BlockSpec((8, DT_PAD), lambda c: (0, 0)), pl.BlockSpec((CHUNK, AW), lambda c: (rev(c), 0)),
                   pl.BlockSpec((1, AW), lambda c: (0, 0))],
        out_shape=[jax.ShapeDtypeStruct((T, CONV_CH), F32), jax.ShapeDtypeStruct((T, DT_PAD), F32),
                   jax.ShapeDtypeStruct((8, DT_PAD), F32), jax.ShapeDtypeStruct((T, AW), F32),
                   jax.ShapeDtypeStruct((1, AW), F32)],
        scratch_shapes=[pltpu.VMEM((AW, NS), F32), pltpu.VMEM((CHUNK, CHUNK), F32), pltpu.VMEM((CHUNK, AW), F32)],
        compiler_params=_cparams(("arbitrary",)),
    )(act, act, act, xbcdt, bias, alog, dsk, states, y_ssd, qkvz, dcat, gs)


def _place():
    return lax.axis_index("x"), lax.axis_index("y"), lax.axis_index("c")


def _slot(px, py, pc):
    return 4 * px + 2 * py + pc


SLAB_ROWS = 24


def _slab_pack(parts, name):
    n = len(parts)

    def body(*refs):
        slab = refs[n]
        slab[...] = jnp.zeros_like(slab)
        for ref, (arr, row) in zip(refs[:n], parts):
            slab[pl.ds(row, arr.shape[0]), pl.ds(0, arr.shape[1])] = ref[...]

    vm = pl.BlockSpec(memory_space=pltpu.VMEM)
    return pl.pallas_call(
        body, name=name, in_specs=[vm] * n, out_specs=vm, out_shape=jax.ShapeDtypeStruct((SLAB_ROWS, D), F32),
    )(*[a for a, _ in parts])


_HBM = pl.BlockSpec(memory_space=pltpu.HBM)
_SEM = pl.BlockSpec(memory_space=pltpu.SEMAPHORE)
_EFFECT = pltpu.SideEffectType.DATAFLOW_SIDE_EFFECTING


def _peers(x, y, c):
    out = []
    for kk in range(1, N_DEV):
        fx, fy, fc = kk >> 2 & 1, kk >> 1 & 1, kk & 1
        out.append((1 - x if fx else x, 1 - y if fy else y, 1 - c if fc else c))
    return out


def _send_start(src, per_peer, name, dep):
    (handles, token) = _send_start_many([src], per_peer, name, dep)
    return handles, token


def _near_peers(x, y, c):
    return [(x, y, 1 - c), (1 - x, y, c), (x, 1 - y, c), (1 - x, 1 - y, c)]


def _send_start_many(srcs, per_peer, name, dep, peers=_peers, npeers=N_DEV - 1):
    n = len(srcs)

    def body(*refs):
        src_refs, land_refs = refs[:n], refs[n:2 * n]
        send_sems, recv_sems = refs[2 * n + 1], refs[2 * n + 2]
        token = refs[-1]
        x, y, c = _place()
        mine = _slot(x, y, c)
        for a in range(n):
            for kk, peer in enumerate(peers(x, y, c)):
                pltpu.make_async_remote_copy(
                    src_ref=src_refs[a].at[_slot(*peer)] if per_peer else src_refs[a], dst_ref=land_refs[a].at[mine],
                    send_sem=send_sems.at[a * npeers + kk], recv_sem=recv_sems.at[a * npeers + kk],
                    device_id=peer, device_id_type=MESH).start()
        token[...] = jnp.zeros_like(token)

    lands = [lax.empty((N_DEV,) + tuple(s.shape[1:] if per_peer else s.shape), s.dtype) for s in srcs]
    hbm = lambda t: pltpu.with_memory_space_constraint(t, pltpu.HBM)
    outs = pl.pallas_call(
        body, name=name,
        out_shape=(pltpu.SemaphoreType.DMA((n * npeers,)), pltpu.SemaphoreType.DMA((n * npeers,)),
                   *[pltpu.HBM(s.shape, s.dtype) for s in srcs], *[pltpu.HBM(l.shape, l.dtype) for l in lands],
                   jax.ShapeDtypeStruct((8, 128), F32)),
        in_specs=(*[_HBM] * (2 * n), _ANY),
        out_specs=(_SEM, _SEM, *[_HBM] * (2 * n), pl.BlockSpec(memory_space=pltpu.VMEM)),
        input_output_aliases={i: 2 + i for i in range(2 * n)},
        compiler_params=pltpu.CompilerParams(has_side_effects=_EFFECT),
    )(*[hbm(s) for s in srcs], *[hbm(l) for l in lands], dep)
    return (outs[0], outs[1], list(outs[2:2 + n]), list(outs[2 + n:2 + 2 * n])), outs[-1]


def _send_wait(handles, after, name):
    srcs, lands = _send_wait_many(handles, after, name)
    return srcs[0], lands[0]


def _send_wait_many(handles, after, name, npeers=N_DEV - 1):
    send_sems, recv_sems, src_thrus, land_thrus = handles
    n = len(src_thrus)

    def body(*refs):
        land_refs = refs[n:2 * n]
        send_sems, recv_sems = refs[2 * n], refs[2 * n + 1]
        me = _place()
        for a in range(n):
            for kk in range(npeers):
                cp = pltpu.make_async_remote_copy(
                    src_ref=land_refs[a].at[0], dst_ref=land_refs[a].at[0],
                    send_sem=send_sems.at[a * npeers + kk], recv_sem=recv_sems.at[a * npeers + kk],
                    device_id=me, device_id_type=MESH)
                cp.wait_send()
                cp.wait_recv()

    both = list(src_thrus) + list(land_thrus)
    outs = pl.pallas_call(
        body, name=name,
        out_shape=tuple(pltpu.HBM(t.shape, t.dtype) for t in both),
        in_specs=(*[_HBM] * (2 * n), _SEM, _SEM, _ANY), out_specs=tuple([_HBM] * (2 * n)),
        input_output_aliases={i: i for i in range(2 * n)},
        compiler_params=pltpu.CompilerParams(has_side_effects=_EFFECT),
    )(*both, send_sems, recv_sems, after)
    return list(outs[:n]), list(outs[n:])


def _forward_start(lands, name, dep):
    n = len(lands)

    def body(*refs):
        land_refs = refs[:n]
        send_sems, recv_sems = refs[n + 1], refs[n + 2]
        token = refs[-1]
        x, y, c = _place()
        for a in range(n):
            for j, chip in enumerate([(1 - x, y), (x, 1 - y), (1 - x, 1 - y)]):
                blk = land_refs[a].at[_slot(*chip, c)]
                pltpu.make_async_remote_copy(
                    src_ref=blk, dst_ref=blk, send_sem=send_sems.at[a * 3 + j], recv_sem=recv_sems.at[a * 3 + j],
                    device_id=(x, y, 1 - c), device_id_type=MESH).start()
        token[...] = jnp.zeros_like(token)

    outs = pl.pallas_call(
        body, name=name,
        out_shape=(pltpu.SemaphoreType.DMA((n * 3,)), pltpu.SemaphoreType.DMA((n * 3,)),
                   *[pltpu.HBM(l.shape, l.dtype) for l in lands], jax.ShapeDtypeStruct((8, 128), F32)),
        in_specs=(*[_HBM] * n, _ANY), out_specs=(_SEM, _SEM, *[_HBM] * n, pl.BlockSpec(memory_space=pltpu.VMEM)),
        input_output_aliases={i: 2 + i for i in range(n)},
        compiler_params=pltpu.CompilerParams(has_side_effects=_EFFECT),
    )(*lands, dep)
    return (outs[0], outs[1], list(outs[2:2 + n])), outs[-1]


def _forward_wait(handles, after, name):
    send_sems, recv_sems, land_thrus = handles
    n = len(land_thrus)

    def body(*refs):
        land_refs = refs[:n]
        send_sems, recv_sems = refs[n], refs[n + 1]
        me = _place()
        for a in range(n):
            for j in range(3):
                cp = pltpu.make_async_remote_copy(
                    src_ref=land_refs[a].at[0], dst_ref=land_refs[a].at[0],
                    send_sem=send_sems.at[a * 3 + j], recv_sem=recv_sems.at[a * 3 + j], device_id=me, device_id_type=MESH)
                cp.wait_send()
                cp.wait_recv()

    outs = pl.pallas_call(
        body, name=name,
        out_shape=tuple(pltpu.HBM(t.shape, t.dtype) for t in land_thrus),
        in_specs=(*[_HBM] * n, _SEM, _SEM, _ANY), out_specs=tuple([_HBM] * n),
        input_output_aliases={i: i for i in range(n)},
        compiler_params=pltpu.CompilerParams(has_side_effects=_EFFECT),
    )(*land_thrus, send_sems, recv_sems, after)
    return list(outs)


def _sum_slots(land, name):
    _, R, C = land.shape
    tm = R if R <= 512 else 512

    def body(x_ref, o_ref):
        acc = x_ref[0].astype(F32)
        for j in range(1, N_DEV):
            acc = acc + x_ref[j].astype(F32)
        o_ref[...] = acc

    return pl.pallas_call(
        body, name=name, grid=(R // tm,),
        in_specs=[pl.BlockSpec((N_DEV, tm, C), lambda i: (0, i, 0))], out_specs=pl.BlockSpec((tm, C), lambda i: (i, 0)),
        out_shape=jax.ShapeDtypeStruct((R, C), F32), compiler_params=_cparams(("parallel",)),
    )(land)


def _adam_math(w, g, m, v):
    m2 = ADAM_B1 * m + (1.0 - ADAM_B1) * g
    v2 = ADAM_B2 * v + (1.0 - ADAM_B2) * (g * g)
    m_hat = m2 / (1.0 - ADAM_B1 ** ADAM_STEP)
    v_hat = v2 / (1.0 - ADAM_B2 ** ADAM_STEP)
    delta = -ADAM_LR * (m_hat / (jnp.sqrt(v_hat) + ADAM_EPS) + ADAM_WD * w)
    return delta, m2, v2


def _adamw(w, g, m, v, name):
    R, C = w.shape
    tm = R if R <= 512 else 256
    return _rowwise(lambda w, g, m, v: (_adam_math(w, g, m, v), ()), [w, g, m, v], [], [(C, F32)] * 3, [], tm=tm, name=name)


def _adamw_small(slab, slab_rows, g_conv_w, ws, ms, vs):
    n = len(ws)

    def body(*refs):
        slab_ref, gc_ref = refs[0], refs[1]
        w_refs, m_refs, v_refs = refs[2:2 + n], refs[2 + n:2 + 2 * n], refs[2 + 2 * n:2 + 3 * n]
        outs = refs[2 + 3 * n:]
        loss_ref = outs[0]
        g_out, d_out, m_out, v_out = (outs[1 + i * n:1 + (i + 1) * n] for i in range(4))
        loss_ref[...] = jnp.sum(slab_ref[pl.ds(6, 1), :], axis=1, keepdims=True)
        for i in range(n):
            g = gc_ref[...] if i == n - 1 else slab_ref[pl.ds(slab_rows[i], 1), pl.ds(0, ws[i].shape[1])]
            d, m2, v2 = _adam_math(w_refs[i][...], g, m_refs[i][...], v_refs[i][...])
            g_out[i][...] = g
            d_out[i][...] = d
            m_out[i][...] = m2
            v_out[i][...] = v2

    vm = pl.BlockSpec(memory_space=pltpu.VMEM)
    shapes = [jax.ShapeDtypeStruct(w.shape, F32) for w in ws]
    outs = pl.pallas_call(
        body, name="adamw_small", in_specs=[vm] * (2 + 3 * n), out_specs=[vm] * (1 + 4 * n),
        out_shape=[jax.ShapeDtypeStruct((1, 1), F32)] + shapes * 4,
    )(slab, g_conv_w, *ws, *ms, *vs)
    return outs[0], outs[1:1 + n], outs[1 + n:1 + 2 * n], outs[1 + 2 * n:1 + 3 * n], outs[1 + 3 * n:]


SMALL = ["norm_mix_pre", "norm_mix_post", "norm_mlp_pre", "norm_mlp_post", "norm_ple_post",
         "conv_b", "ssd_norm_g", "dt_bias", "a_log", "d_skip"]


def _pad_row(v, width=D):
    return jnp.pad(v, ((0, 0), (0, width - v.shape[1])))


def kernel(x, p, positions, norm_mix_pre, norm_mix_post, w_in, conv_w, conv_b, dt_bias, a_log, d_skip, ssd_norm_g, w_out, norm_mlp_pre, norm_mlp_post, w_up, w_down, w_ple_gate, w_ple_proj, norm_ple_post, loss_target, m_norm_mix_pre, m_norm_mix_post, m_w_in, m_conv_w, m_conv_b, m_dt_bias, m_a_log, m_d_skip, m_ssd_norm_g, m_w_out, m_norm_mlp_pre, m_norm_mlp_post, m_w_up, m_w_down, m_w_ple_gate, m_w_ple_proj, m_norm_ple_post, v_norm_mix_pre, v_norm_mix_post, v_w_in, v_conv_w, v_conv_b, v_dt_bias, v_a_log, v_d_skip, v_ssd_norm_g, v_w_out, v_norm_mlp_pre, v_norm_mlp_post, v_w_up, v_w_down, v_w_ple_gate, v_w_ple_proj, v_norm_ple_post):
    args = dict(locals())
    x2, p2, tgt = x[0], p[0, 0], loss_target[0]
    g1, g2, g3, g4, g5 = norm_mix_pre, norm_mix_post, norm_mlp_pre, norm_mlp_post, norm_ple_post

    me = _slot(*_place())
    pack_in = jnp.pad(w_in[0].T, ((0, W_IN_SHARD_PAD - W_IN_SHARD), (0, 0))).astype(BF16)
    rest = [w_out[0].astype(BF16), w_up[0].T.astype(BF16), w_down[0].astype(BF16), w_ple_gate[0].astype(BF16),
            w_ple_proj[0].T.reshape(32, D).astype(BF16)]
    conv_pack = jnp.pad(conv_w[0], ((0, 4), (0, 32)))
    in_handles, tok_in0 = _send_start_many([pack_in, conv_pack], False, "gather_in_start", g1, peers=_near_peers, npeers=4)

    inv_freq = ROPE_THETA ** (-jnp.arange(HD // 2, dtype=F32) * 2.0 / HD)
    pos = positions[0] + tok_in0[0, 0].astype(jnp.int32)
    ang = pos.astype(F32)[:, None] * inv_freq
    cos, sin = jnp.cos(ang), jnp.sin(ang)
    cos128 = jnp.concatenate([cos, cos, cos, cos], axis=1)
    sin128 = jnp.concatenate([-sin, sin, -sin, sin], axis=1)

    bias_w, alog_w, dsk_w = _pad_row(dt_bias, DT_PAD), _pad_row(a_log, DT_PAD), _pad_row(d_skip, DT_PAD)

    (u1,) = _rowwise(lambda a, g: ((a * _rstd(a) * g,), ()), [x2], [g1], [(D, BF16)], [], tm=512, name="norm_x",
                     deps=[cos128, sin128])
    p2b = p2.astype(BF16)

    in_back, in_land = _send_wait_many(in_handles, u1, "gather_in_wait", npeers=4)
    fw_handles, tok_fw = _forward_start(in_land, "gather_in_forward", u1)
    in_land = _forward_wait(fw_handles, tok_fw, "gather_in_forward_wait")
    gin = lax.dynamic_update_slice(in_land[0], in_back[0][None], (me, 0, 0))
    gconv = lax.dynamic_update_slice(in_land[1], in_back[1][None], (me, 0, 0))
    rest_handles, tok_rest = _send_start_many(rest, False, "gather_rest_start", gconv)
    w_inT = gin[:, :W_IN_SHARD].reshape(IN_W, D)
    w_qkvzT = w_inT[:4 * AW]
    w_xbcdtT = jnp.pad(w_inT[4 * AW:], ((0, DT_PAD - HEADS), (0, 0)))
    conv_full = gconv[:, :CONV_K, :96].transpose(1, 0, 2).reshape(CONV_K, CONV_CH)
    qkvz = _mm(u1, w_qkvzT, tb=True, tm=512, tn=2048, tk=1024, name="proj_qkvz", deps=[tok_rest])
    xbcdt = _mm(u1, w_xbcdtT, tb=True, tm=512, tn=896, tk=1024, name="proj_xbcdt")

    qkv = _rope_fwd(qkvz, cos128, sin128)
    qkv = [qkv[3 * i:3 * i + 3] for i in range(len(DILATIONS))]
    outs, lses = [], []
    for d, (qd, kd, vd) in zip(DILATIONS, qkv):
        o, l = _attn_fwd(qd, kd, vd, d)
        outs.append(o)
        lses.append(l)
    attn, lse, attn4, lse4, attn16, lse16 = _attn_merge(outs, lses)

    y_ssd, states, cat, act = _ssd_fwd(xbcdt, conv_full, conv_b, bias_w, alog_w, dsk_w, qkvz, attn, ssd_norm_g)


    rest_back, landed = _send_wait_many(rest_handles, cat, "gather_rest_wait")
    landed = [lax.dynamic_update_slice(l, b[None], (me, 0, 0)) for l, b in zip(landed, rest_back)]
    w_o, w_upT, w_dn, w_gate = landed[0].reshape(D, D), landed[1].reshape(DFF, D), landed[2].reshape(DFF, D), landed[3].reshape(D, D)
    w_projT = landed[4].reshape(D, PLE)

    def post1(mm, xx, ga):
        h = xx + mm * _rstd(mm) * ga
        return (mm, h, _rstd(h)), ()
    mix, h1, r3 = _mm_rows(post1, [(cat, w_o, False)], [x2], [g2], [(D, F32), (D, F32), (1, F32)], [], tm=512,
                           name="mix_out")

    a_up, ff, u2, h2, h2b = _mlp_fwd(h1, r3, g3, w_upT, w_dn, g4)
    relu2 = lambda a: jnp.square(jnp.maximum(a.astype(F32), 0.0))

    def final(gpre, ppv, hh, tg, g):
        sg = _sigmoid(gpre)
        ple = ppv * sg
        r = _rstd(ple)
        n = ple * r
        h3 = hh + n * g
        e = h3 - tg
        dh3 = e * (1.0 / D)
        dple = _rms_bwd(n, r, g, dh3)
        return (dh3, dple * sg, dple * ppv * sg * (1.0 - sg)), (_colsum(dh3 * n), _colsum(0.5 * e * e * (1.0 / D)))
    dh3, dpp, dgp, dg5, loss_vec = _mm_rows(final, [(h2b, w_gate, False), (p2b, w_projT, True)], [h2, tgt], [g5],
                                            [(D, F32), (D, BF16), (D, BF16)], [(1, D), (1, D)], tm=512, name="ple_loss")

    gw_projT = _mm(dpp, p2b, ta=True, tm=512, tn=256, tk=T, out_dtypes=(BF16,), name="gw_ple_proj")
    gw_gate = _mm(h2b, dgp, ta=True, tm=512, tn=1024, tk=T, out_dtypes=(BF16,), name="gw_ple_gate")
    rs_ple, tok_ple = _send_start_many([gw_projT.reshape(N_DEV, 32, D), gw_gate.reshape(N_DEV, 128, D)], True,
                                       "rs_start_w_ple", g1)
    def bwd_mlp_post(dg_, d3, f, g):
        dh2 = d3 + dg_
        r = _rstd(f)
        n = f * r
        return (dh2, _rms_bwd(n, r, g, dh2)), (_colsum(dh2 * n),)
    dh2, dff, dg4 = _mm_rows(bwd_mlp_post, [(dgp, w_gate, True)], [dh3, ff], [g4], [(D, F32), (D, BF16)], [(1, D)],
                             tm=512, name="bwd_ple_gate", deps=[tok_ple])

    gw_dn = _mm(a_up, dff, ta=True, tm=512, tn=1024, tk=T, a_pre=relu2, out_dtypes=(BF16,), name="gw_mlp_down")
    rs_dn, tok_dn = _send_start(gw_dn.reshape(N_DEV, 512, D), True, "rs_start_w_down", g1)
    da_up, du2 = _mlp_dx(dff, a_up, w_upT, w_dn, tok_dn)
    gw_upT = _mm(da_up, u2, ta=True, tm=512, tn=1024, tk=T, out_dtypes=(BF16,), name="gw_mlp_up")
    rs_up, tok_up = _send_start(gw_upT.reshape(N_DEV, 512, D), True, "rs_start_w_up", g1)

    def bwd_mix_post(d2, du, hh, rr, mm, ga, gb):
        n3 = hh * rr
        dh1 = d2 + _rms_bwd(n3, rr, gb, du)
        r = _rstd(mm)
        n2 = mm * r
        return (dh1, _rms_bwd(n2, r, ga, dh1)), (_colsum(du * n3), _colsum(dh1 * n2))
    dh1, dmix, dg3, dg2 = _rowwise(bwd_mix_post, [dh2, du2, h1, r3, mix], [g2, g3], [(D, F32), (D, BF16)],
                                   [(1, D), (1, D)], tm=512, name="bwd_post_mix", deps=[tok_up])

    gw_o = _mm(cat, dmix, ta=True, tm=512, tn=1024, tk=T, out_dtypes=(BF16,), name="gw_out")
    rs_o, tok_o = _send_start(gw_o.reshape(N_DEV, 128, D), True, "rs_start_w_out", g1)
    dcat, dattn4, dattn16 = _dx_out(dmix, w_o, tok_o)

    dact, ddtw, ssd_par, dz, dgs = _ssd_bwd(act, xbcdt, bias_w, alog_w, dsk_w, states, y_ssd, qkvz, dcat, ssd_norm_g)
    dxbcdt, conv_par = _conv_bwd(xbcdt, dact, ddtw, conv_full, conv_b)

    qkv_grads = [_attn_bwd(*qkv[0], dcat, attn, lse, 1),
                 _attn_bwd(*qkv[1], dattn4, attn4, lse4, 4),
                 _attn_bwd(*qkv[2], dattn16, attn16, lse16, 16)]
    dqkvz = _rope_bwd(qkv_grads, dz, cos128, sin128)

    gw_qkvzT = _mm(dqkvz, u1, ta=True, tm=512, tn=1024, tk=T, out_dtypes=(BF16,), name="gw_qkvz")
    gw_xbcdtT = _mm(dxbcdt, u1, ta=True, tm=896, tn=1024, tk=T, out_dtypes=(BF16,), name="gw_xbcdt")
    gw_inT = jnp.concatenate([gw_qkvzT, gw_xbcdtT], axis=0)[:IN_W]
    gw_inT = jnp.pad(gw_inT.reshape(N_DEV, W_IN_SHARD, D), ((0, 0), (0, W_IN_SHARD_PAD - W_IN_SHARD), (0, 0)))
    rs_in, tok_in = _send_start(gw_inT, True, "rs_start_w_in", g1)

    def bwd_in(ua, ub, d1, xx, g):
        rr = _rstd(xx)
        n = xx * rr
        du = ua + ub
        return (d1 + _rms_bwd(n, rr, g, du),), (_colsum(du * n),)
    grad_x, dg1 = _mm_rows(bwd_in, [(dqkvz, w_qkvzT, False), (dxbcdt, w_xbcdtT, False)], [dh1, x2], [g1],
                           [(D, F32)], [(1, D)], tm=512, name="bwd_in_proj", deps=[tok_in])

    my_slab = _slab_pack([(dg1, 0), (dg2, 1), (dg3, 2), (dg4, 3), (dg5, 4), (dgs, 5), (loss_vec, 6),
                          (conv_par, 8), (ssd_par, 16)], "slab_pack")
    slab_handles, tok_slab = _send_start_many([my_slab], False, "slab_start", g1)

    def scatter_finish(handles, nm, after):
        part, land = _send_wait(handles, after, "rs_wait_" + nm)
        own = lax.dynamic_slice(part, (me, 0, 0), (1,) + part.shape[1:])
        return _sum_slots(lax.dynamic_update_slice(land, own, (me, 0, 0)), "rs_sum_" + nm)
    g_out = scatter_finish(rs_o, "w_out", tok_slab)
    g_upT = scatter_finish(rs_up, "w_up", tok_slab)
    g_dn = scatter_finish(rs_dn, "w_down", tok_slab)
    ple_parts, ple_lands = _send_wait_many(rs_ple, tok_slab, "rs_wait_w_ple")
    g_projT, g_gate = [
        _sum_slots(lax.dynamic_update_slice(land, lax.dynamic_slice(part, (me, 0, 0), (1,) + part.shape[1:]), (me, 0, 0)),
                   "rs_sum_" + nm) for part, land, nm in zip(ple_parts, ple_lands, ("w_proj", "w_gate"))]

    grads = {
        "w_out": g_out[None], "w_up": g_upT.T[None], "w_down": g_dn[None],
        "w_ple_gate": g_gate[None], "w_ple_proj": g_projT.reshape(128, PLE).T[None],
    }
    delta, new_m, new_v = {}, {}, {}
    for nme in ["w_out", "w_up", "w_down", "w_ple_gate", "w_ple_proj", "w_in"]:
        if nme == "w_in":
            g_inT = scatter_finish(rs_in, "w_in", delta["w_down"])
            grads["w_in"] = g_inT[:W_IN_SHARD].T[None]
        dl, mm_, vv_ = _adamw(args[nme][0], grads[nme][0], args["m_" + nme][0], args["v_" + nme][0], "adamw_" + nme)
        delta[nme], new_m[nme], new_v[nme] = dl[None], mm_[None], vv_[None]

    slab_back, slab_land = _send_wait_many(slab_handles, delta["w_in"], "slab_wait")
    slab = _sum_slots(lax.dynamic_update_slice(slab_land[0], slab_back[0][None], (me, 0, 0)), "slab_sum")
    g_conv_w = lax.dynamic_slice(slab[8:12, :CONV_CH], (0, me * 96), (CONV_K, 96))
    small_names = SMALL + ["conv_w"]
    small_rows = [0, 1, 2, 3, 4, 12, 5, 16, 17, 18, None]
    pick = lambda prefix: [args[prefix + nme] for nme in SMALL] + [args[prefix + "conv_w"][0]]
    loss11, g_s, d_s, m_s, v_s = _adamw_small(slab, small_rows, g_conv_w, pick(""), pick("m_"), pick("v_"))
    loss = loss11[0, 0]
    for i, nme in enumerate(small_names):
        lead = (lambda t: t[None]) if nme == "conv_w" else (lambda t: t)
        grads[nme], delta[nme], new_m[nme], new_v[nme] = lead(g_s[i]), lead(d_s[i]), lead(m_s[i]), lead(v_s[i])

    order = ["norm_mix_pre", "norm_mix_post", "w_in", "conv_w", "conv_b", "dt_bias", "a_log", "d_skip", "ssd_norm_g",
             "w_out", "norm_mlp_pre", "norm_mlp_post", "w_up", "w_down", "w_ple_gate", "w_ple_proj", "norm_ple_post"]
    return (loss, grad_x[None], *[grads[n] for n in order], *[delta[n] for n in order],
            *[new_m[n] for n in order], *[new_v[n] for n in order])
```

```python
import functools
import math

import jax
import jax.numpy as jnp
from jax import lax
from jax.experimental import pallas as pl
from jax.experimental.pallas import tpu as pltpu

F32 = jnp.float32
BF16 = jnp.bfloat16
MESH = pl.DeviceIdType.MESH
HIGHEST = lax.Precision.HIGHEST

N_DEV = 8
T = 4096
D = 1024
HEADS = 8
HD = 64
AW = 512
NS = 128
CONV_K = 4
CONV_CH = 768
CHUNK = 128
DFF = 4096
PLE = 256
EPS = 1e-6
ROPE_THETA = 10000.0
DILATIONS = (1, 4, 16)
QBLK = 128
NEG = -1e30
IN_W = 2824
W_IN_SHARD = 353
W_IN_SHARD_PAD = 384
DT_PAD = 128

ADAM_LR, ADAM_B1, ADAM_B2, ADAM_EPS, ADAM_WD, ADAM_STEP = 0.001, 0.9, 0.999, 1e-08, 0.01, 10

VMEM_LIMIT = 56 * 1024 * 1024


_ANY = pl.BlockSpec(memory_space=pl.ANY)


def _cparams(sem=None):
    return pltpu.CompilerParams(dimension_semantics=sem, vmem_limit_bytes=VMEM_LIMIT)


def _dot(a, b, ca, cb, precision=None):
    return lax.dot_general(a, b, (((ca,), (cb,)), ((), ())), preferred_element_type=F32, precision=precision)


def _nn(a, b):
    return _dot(a, b, 1, 0)


def _nt(a, b):
    return _dot(a, b, 1, 1)


def _tn(a, b):
    return _dot(a, b, 0, 0)


def _sigmoid(x):
    return 1.0 / (1.0 + jnp.exp(-x))


def _softplus(x):
    return jnp.maximum(x, 0.0) + jnp.log(1.0 + jnp.exp(-jnp.abs(x)))


def _mm(a, b, *, ta=False, tb=False, tm, tn, tk, name,
        a_pre=None, a_rows=(), a_cols=(), b_pre=None, b_rows=(), b_cols=(),
        epi=None, epi_tiles=(), out_dtypes=(F32,), deps=()):
    if ta:
        K, M = a.shape
    else:
        M, K = a.shape
    if tb:
        N, K2 = b.shape
    else:
        K2, N = b.shape
    assert K == K2 and M % tm == 0 and N % tn == 0 and K % tk == 0, (name, a.shape, b.shape)
    nk = K // tk
    if ta:
        a_spec = pl.BlockSpec((tk, tm), lambda i, j, k: (k, i))
        a_row_specs = [pl.BlockSpec((tk, 1), lambda i, j, k: (k, 0)) for _ in a_rows]
        a_col_specs = [pl.BlockSpec((1, tm), lambda i, j, k: (0, i)) for _ in a_cols]
    else:
        a_spec = pl.BlockSpec((tm, tk), lambda i, j, k: (i, k))
        a_row_specs = [pl.BlockSpec((tm, 1), lambda i, j, k: (i, 0)) for _ in a_rows]
        a_col_specs = [pl.BlockSpec((1, tk), lambda i, j, k: (0, k)) for _ in a_cols]
    if tb:
        b_spec = pl.BlockSpec((tn, tk), lambda i, j, k: (j, k))
        b_row_specs = [pl.BlockSpec((tn, 1), lambda i, j, k: (j, 0)) for _ in b_rows]
        b_col_specs = [pl.BlockSpec((1, tk), lambda i, j, k: (0, k)) for _ in b_cols]
    else:
        b_spec = pl.BlockSpec((tk, tn), lambda i, j, k: (k, j))
        b_row_specs = [pl.BlockSpec((tk, 1), lambda i, j, k: (k, 0)) for _ in b_rows]
        b_col_specs = [pl.BlockSpec((1, tn), lambda i, j, k: (0, j)) for _ in b_cols]
    o_spec = pl.BlockSpec((tm, tn), lambda i, j, k: (i, j))
    na, nb, ne, no = len(a_rows) + len(a_cols), len(b_rows) + len(b_cols), len(epi_tiles), len(out_dtypes)

    def body(*refs):
        a_ref, b_ref = refs[0], refs[1]
        a_ex = refs[2:2 + na]
        b_ex = refs[2 + na:2 + na + nb]
        e_ex = refs[2 + na + nb:2 + na + nb + ne]
        first_out = 2 + na + nb + ne + len(deps)
        outs = refs[first_out:first_out + no]

        def finish(res):
            vals = epi(res, *[r[...] for r in e_ex]) if epi is not None else (res,)
            for o_ref, val in zip(outs, vals):
                o_ref[...] = val.astype(o_ref.dtype)

        at = a_ref[...]
        if a_pre is not None:
            at = a_pre(at, *[r[...] for r in a_ex])
        bt = b_ref[...]
        if b_pre is not None:
            bt = b_pre(bt, *[r[...] for r in b_ex])
        prod = _dot(at.astype(BF16), bt.astype(BF16), 0 if ta else 1, 1 if tb else 0)
        if nk == 1:
            finish(prod)
            return
        acc = refs[-1]
        k = pl.program_id(2)

        @pl.when(k == 0)
        def _():
            acc[...] = jnp.zeros_like(acc)
        acc[...] += prod

        @pl.when(k == nk - 1)
        def _():
            finish(acc[...])

    outs = pl.pallas_call(
        body, name=name,
        grid=(M // tm, N // tn, nk),
        in_specs=([a_spec, b_spec] + a_row_specs + a_col_specs + b_row_specs + b_col_specs + [o_spec] * ne
                  + [_ANY] * len(deps)),
        out_specs=[o_spec] * no,
        out_shape=[jax.ShapeDtypeStruct((M, N), dt) for dt in out_dtypes],
        scratch_shapes=[pltpu.VMEM((tm, tn), F32)] if nk > 1 else [],
        compiler_params=_cparams(("parallel", "parallel", "arbitrary")),
    )(a, b, *a_rows, *a_cols, *b_rows, *b_cols, *epi_tiles, *deps)
    return outs[0] if no == 1 else outs


MLP_TM = 1024
MLP_TC = 512


def _mlp_fwd(h, r, g, w_upT, w_dn, g_post):
    nc = DFF // MLP_TC

    def body(h_ref, r_ref, g_ref, wu_ref, wd_ref, gp_ref, a_ref, ff_ref, u_ref, ho_ref, hob_ref, acc, u_scr):
        c = pl.program_id(1)

        @pl.when(c == 0)
        def _():
            u = (h_ref[...] * r_ref[...] * g_ref[...]).astype(BF16)
            u_scr[...] = u
            u_ref[...] = u
            acc[...] = jnp.zeros_like(acc)
        a = _nt(u_scr[...], wu_ref[...])
        a_ref[...] = a.astype(BF16)
        acc[...] += _nn(jnp.square(jnp.maximum(a, 0.0)).astype(BF16), wd_ref[...])

        @pl.when(c == nc - 1)
        def _():
            f = acc[...]
            ff_ref[...] = f
            ho = h_ref[...] + f * _rstd(f) * gp_ref[...]
            ho_ref[...] = ho
            hob_ref[...] = ho.astype(BF16)

    row = pl.BlockSpec((MLP_TM, D), lambda i, c: (i, 0))
    wsp = pl.BlockSpec((MLP_TC, D), lambda i, c: (c, 0))
    vec = pl.BlockSpec((1, D), lambda i, c: (0, 0))
    return pl.pallas_call(
        body, name="mlp_fwd", grid=(T // MLP_TM, nc),
        in_specs=[row, pl.BlockSpec((MLP_TM, 1), lambda i, c: (i, 0)), vec, wsp, wsp, vec],
        out_specs=[pl.BlockSpec((MLP_TM, MLP_TC), lambda i, c: (i, c)), row, row, row, row],
        out_shape=[jax.ShapeDtypeStruct((T, DFF), BF16), jax.ShapeDtypeStruct((T, D), F32), jax.ShapeDtypeStruct((T, D), BF16),
                   jax.ShapeDtypeStruct((T, D), F32), jax.ShapeDtypeStruct((T, D), BF16)],
        scratch_shapes=[pltpu.VMEM((MLP_TM, D), F32), pltpu.VMEM((MLP_TM, D), BF16)],
        compiler_params=_cparams(("parallel", "arbitrary")),
    )(h, r, g, w_upT, w_dn, g_post)


def _mlp_dx(dff, a, w_upT, w_dn, dep):
    nc = DFF // MLP_TC

    def body(d_ref, a_ref, wu_ref, wd_ref, dep_ref, da_ref, du_ref, acc, d_scr):
        c = pl.program_id(1)

        @pl.when(c == 0)
        def _():
            d_scr[...] = d_ref[...].astype(BF16)
            acc[...] = jnp.zeros_like(acc)
        da = (_nt(d_scr[...], wd_ref[...]) * (2.0 * jnp.maximum(a_ref[...].astype(F32), 0.0))).astype(BF16)
        da_ref[...] = da
        acc[...] += _nn(da, wu_ref[...])

        @pl.when(c == nc - 1)
        def _():
            du_ref[...] = acc[...]

    row = pl.BlockSpec((MLP_TM, D), lambda i, c: (i, 0))
    wsp = pl.BlockSpec((MLP_TC, D), lambda i, c: (c, 0))
    chunk = pl.BlockSpec((MLP_TM, MLP_TC), lambda i, c: (i, c))
    return pl.pallas_call(
        body, name="mlp_dx", grid=(T // MLP_TM, nc),
        in_specs=[row, chunk, wsp, wsp, _ANY], out_specs=[chunk, row],
        out_shape=[jax.ShapeDtypeStruct((T, DFF), BF16), jax.ShapeDtypeStruct((T, D), F32)],
        scratch_shapes=[pltpu.VMEM((MLP_TM, D), F32), pltpu.VMEM((MLP_TM, D), BF16)],
        compiler_params=_cparams(("parallel", "arbitrary")),
    )(dff, a, w_upT, w_dn, dep)


def _rowwise(fn, rows, vecs, out_rows, out_sums, *, tm, name, deps=()):
    specs, arrs = [], []
    R = None
    for r in rows:
        if isinstance(r, tuple):
            arr, width, cb = r
            specs.append(pl.BlockSpec((tm, width), lambda i, cb=cb: (i, cb)))
        else:
            arr = r
            specs.append(pl.BlockSpec((tm, arr.shape[1]), lambda i: (i, 0)))
        R = arr.shape[0] if R is None else R
        assert arr.shape[0] == R, name
        arrs.append(arr)
    assert R % tm == 0, name
    for v in vecs:
        specs.append(pl.BlockSpec(v.shape, lambda i: (0, 0)))
        arrs.append(v)
    nr, nv, no, ns = len(rows), len(vecs), len(out_rows), len(out_sums)
    out_specs = [pl.BlockSpec((tm, w), lambda i: (i, 0)) for w, _ in out_rows]
    out_specs += [pl.BlockSpec(s, lambda i: (0, 0)) for s in out_sums]
    out_shape = [jax.ShapeDtypeStruct((R, w), dt) for w, dt in out_rows]
    out_shape += [jax.ShapeDtypeStruct(s, F32) for s in out_sums]

    nd = len(deps)

    def body(*refs):
        ins = [r[...] for r in refs[:nr + nv]]
        o_refs = refs[nr + nv + nd:nr + nv + nd + no]
        s_refs = refs[nr + nv + nd + no:]
        o_vals, s_vals = fn(*ins)
        for ref, val in zip(o_refs, o_vals):
            ref[...] = val.astype(ref.dtype)
        if ns:
            @pl.when(pl.program_id(0) == 0)
            def _():
                for ref in s_refs:
                    ref[...] = jnp.zeros_like(ref)
            for ref, val in zip(s_refs, s_vals):
                ref[...] += val

    outs = pl.pallas_call(
        body, name=name, grid=(R // tm,), in_specs=specs + [_ANY] * nd, out_specs=out_specs, out_shape=out_shape,
        compiler_params=_cparams(("arbitrary",) if ns else ("parallel",)),
    )(*arrs, *deps)
    return outs


def _mm_rows(fn, mats, rows, vecs, out_rows, out_sums, *, tm, name, deps=()):
    R = mats[0][0].shape[0]
    assert R % tm == 0, name
    specs, arrs = [], []
    for a, b, tb in mats:
        specs += [pl.BlockSpec((tm, a.shape[1]), lambda i: (i, 0)), pl.BlockSpec(b.shape, lambda i: (0, 0))]
        arrs += [a, b]
    for r in rows:
        specs.append(pl.BlockSpec((tm, r.shape[1]), lambda i: (i, 0)))
        arrs.append(r)
    for v in vecs:
        specs.append(pl.BlockSpec(v.shape, lambda i: (0, 0)))
        arrs.append(v)
    nm, nr, nv, nd, no, ns = len(mats), len(rows), len(vecs), len(deps), len(out_rows), len(out_sums)
    out_specs = [pl.BlockSpec((tm, w), lambda i: (i, 0)) for w, _ in out_rows]
    out_specs += [pl.BlockSpec(s, lambda i: (0, 0)) for s in out_sums]
    out_shape = [jax.ShapeDtypeStruct((R, w), dt) for w, dt in out_rows] + [jax.ShapeDtypeStruct(s, F32) for s in out_sums]

    def body(*refs):
        prods = [_dot(refs[2 * p][...].astype(BF16), refs[2 * p + 1][...].astype(BF16), 1, 1 if mats[p][2] else 0)
                 for p in range(nm)]
        ins = [r[...] for r in refs[2 * nm:2 * nm + nr + nv]]
        first_out = 2 * nm + nr + nv + nd
        o_refs, s_refs = refs[first_out:first_out + no], refs[first_out + no:]
        o_vals, s_vals = fn(*prods, *ins)
        for ref, val in zip(o_refs, o_vals):
            ref[...] = val.astype(ref.dtype)
        if ns:
            @pl.when(pl.program_id(0) == 0)
            def _():
                for ref in s_refs:
                    ref[...] = jnp.zeros_like(ref)
            for ref, val in zip(s_refs, s_vals):
                ref[...] += val

    return pl.pallas_call(
        body, name=name, grid=(R // tm,), in_specs=specs + [_ANY] * nd, out_specs=out_specs, out_shape=out_shape,
        compiler_params=_cparams(("arbitrary",) if ns else ("parallel",)),
    )(*arrs, *deps)


def _colsum(x):
    return jnp.sum(x, axis=0, keepdims=True)


def _rstd(x):
    return lax.rsqrt(jnp.mean(x * x, axis=-1, keepdims=True) + EPS)


def _rms_bwd(xn, r, g, dy):
    dn = dy * g
    return r * (dn - xn * jnp.mean(dn * xn, axis=-1, keepdims=True))


def _partner(t):
    lane = lax.broadcasted_iota(jnp.int32, t.shape, 1)
    up = pltpu.roll(t, 96, 1)
    down = pltpu.roll(t, 32, 1)
    return jnp.where((lane % 64) < 32, up, down)


SLABS = AW // 128


def _rows(r, n, d):
    return pl.ds(r, n, stride=d) if d > 1 else pl.ds(0, n)


def _undilate(src_ref, dst, d, tm):
    for r in range(d):
        for j in range(SLABS):
            dst[j][_rows(r, tm // d, d), :] = src_ref[:, pl.ds(r * AW + j * 128, 128)].astype(dst[j].dtype)


def _dilate(dst_ref, src, d, tm):
    for r in range(d):
        for j in range(SLABS):
            dst_ref[:, pl.ds(r * AW + j * 128, 128)] = src[j][_rows(r, tm // d, d), :].astype(dst_ref.dtype)


def _slab_scratch(n, tm):
    return [pltpu.VMEM((tm, 128), F32)] * (SLABS * n)


def _slab_groups(flat):
    return [flat[SLABS * i:SLABS * (i + 1)] for i in range(len(flat) // SLABS)]


def _slab_specs(tm, first):
    return [pl.BlockSpec((tm, 128), lambda i, j=j: (i, first + j)) for j in range(SLABS)]


def _dil_spec(tm, d):
    return pl.BlockSpec((tm // d, d * AW), lambda i: (i, 0))


ROPE_TM = 512


def _rope_fwd(qkvz, cos128, sin128):
    tm = ROPE_TM

    def body(*refs):
        q_refs, k_refs, v_refs = refs[0:4], refs[4:8], refs[8:12]
        c_ref, s_ref = refs[12], refs[13]
        outs = refs[14:23]
        qs, ks = _slab_groups(refs[23:])
        c, s = c_ref[...], s_ref[...]
        for j in range(SLABS):
            q, k = q_refs[j][...], k_refs[j][...]
            qs[j][...] = (q * c + _partner(q) * s) * (HD ** -0.5)
            ks[j][...] = k * c + _partner(k) * s
        for di, d in enumerate(DILATIONS):
            oq, ok, ov = outs[3 * di:3 * di + 3]
            for r in range(d):
                rows = _rows(r, tm // d, d)
                for j in range(SLABS):
                    cols = pl.ds(r * AW + j * 128, 128)
                    oq[:, cols] = qs[j][rows, :].astype(BF16)
                    ok[:, cols] = ks[j][rows, :].astype(BF16)
                    ov[:, cols] = v_refs[j][rows, :].astype(BF16)

    tab = pl.BlockSpec((tm, 128), lambda i: (i, 0))
    out_specs, out_shape = [], []
    for d in DILATIONS:
        out_specs += [_dil_spec(tm, d)] * 3
        out_shape += [jax.ShapeDtypeStruct((T // d, d * AW), BF16)] * 3
    return pl.pallas_call(
        body, name="rope_fwd", grid=(T // tm,),
        in_specs=_slab_specs(tm, 0) + _slab_specs(tm, 4) + _slab_specs(tm, 8) + [tab, tab],
        out_specs=out_specs, out_shape=out_shape, scratch_shapes=_slab_scratch(2, tm),
        compiler_params=_cparams(("parallel",)),
    )(*([qkvz] * 12), cos128, sin128)


def _rope_bwd(grads, dz, cos128, sin128):
    tm = 256

    def body(*refs):
        g_refs = refs[0:9]
        dz_ref, c_ref, s_ref, o_ref = refs[9], refs[10], refs[11], refs[12]
        scr = _slab_groups(refs[13:])
        for di, d in enumerate(DILATIONS[1:]):
            for t in range(3):
                _undilate(g_refs[3 * (di + 1) + t], scr[3 * di + t], d, tm)
        c, s = c_ref[...], s_ref[...]
        for j in range(SLABS):
            cols = pl.ds(j * 128, 128)
            tot = [g_refs[t][:, cols] + scr[t][j][...] + scr[3 + t][j][...] for t in range(3)]
            dqr = tot[0] * (HD ** -0.5)
            o_ref[:, pl.ds(j * 128, 128)] = (dqr * c + _partner(dqr * s)).astype(BF16)
            o_ref[:, pl.ds(AW + j * 128, 128)] = (tot[1] * c + _partner(tot[1] * s)).astype(BF16)
            o_ref[:, pl.ds(2 * AW + j * 128, 128)] = tot[2].astype(BF16)
        o_ref[:, pl.ds(3 * AW, AW)] = dz_ref[...].astype(BF16)

    tab = pl.BlockSpec((tm, 128), lambda i: (i, 0))
    in_specs, args = [], []
    for d, g in zip(DILATIONS, grads):
        in_specs += [_dil_spec(tm, d)] * 3
        args += list(g)
    return pl.pallas_call(
        body, name="rope_bwd", grid=(T // tm,),
        in_specs=in_specs + [pl.BlockSpec((tm, AW), lambda i: (i, 0)), tab, tab],
        out_specs=pl.BlockSpec((tm, 4 * AW), lambda i: (i, 0)),
        out_shape=jax.ShapeDtypeStruct((T, 4 * AW), BF16),
        scratch_shapes=_slab_scratch(6, tm),
        compiler_params=_cparams(("parallel",)),
    )(*args, dz, cos128, sin128)


def _dx_out(dmix, w_o, dep):
    tm = ROPE_TM

    def body(a_ref, w_ref, dep_ref, dcat_ref, o4, o16, *slabs):
        prod = _nt(a_ref[...].astype(BF16), w_ref[...].astype(BF16))
        dcat_ref[...] = prod
        for j in range(SLABS):
            slabs[j][...] = prod[:, 128 * j:128 * (j + 1)]
        _dilate(o4, slabs, 4, tm)
        _dilate(o16, slabs, 16, tm)

    return pl.pallas_call(
        body, name="dx_out", grid=(T // tm,),
        in_specs=[pl.BlockSpec((tm, D), lambda i: (i, 0)), pl.BlockSpec((D, D), lambda i: (0, 0)), _ANY],
        out_specs=[pl.BlockSpec((tm, D), lambda i: (i, 0)), _dil_spec(tm, 4), _dil_spec(tm, 16)],
        out_shape=[jax.ShapeDtypeStruct((T, D), F32), jax.ShapeDtypeStruct((T // 4, 4 * AW), F32),
                   jax.ShapeDtypeStruct((T // 16, 16 * AW), F32)],
        scratch_shapes=_slab_scratch(1, tm), compiler_params=_cparams(("parallel",)),
    )(dmix, w_o, dep)


def _band_masks():
    qi = lax.broadcasted_iota(jnp.int32, (QBLK, QBLK), 0)
    kj = lax.broadcasted_iota(jnp.int32, (QBLK, QBLK), 1)
    return kj >= qi, kj <= qi


def _attn_fwd(q, k, v, d):
    L = q.shape[0]
    npair = L // (2 * QBLK)

    def body(q_ref, kp_ref, kc_ref, vp_ref, vc_ref, o_ref, l_ref):
        pair = pl.program_id(1)
        mask_p, mask_c = _band_masks()
        for sub in range(2):
            rows = pl.ds(sub * QBLK, QBLK)
            first = jnp.where(pair > 0, 0.0, NEG) if sub == 0 else 0.0
            bias = jnp.concatenate([jnp.where(mask_p, 0.0, NEG) + first, jnp.where(mask_c, 0.0, NEG)], axis=1)
            k_prev = (lambda sl: kp_ref[:, sl]) if sub == 0 else (lambda sl: kc_ref[pl.ds(0, QBLK), sl])
            v_prev = (lambda sl: vp_ref[:, sl]) if sub == 0 else (lambda sl: vc_ref[pl.ds(0, QBLK), sl])
            s = []
            for h in range(HEADS):
                sl = pl.ds(HD * h, HD)
                qh = q_ref[rows, sl]
                s.append(jnp.concatenate([_nt(qh, k_prev(sl)), _nt(qh, kc_ref[rows, sl])], axis=1))
            s = jnp.stack(s) + bias
            m = jnp.max(s, axis=2, keepdims=True)
            e = jnp.exp(s - m)
            den = jnp.sum(e, axis=2, keepdims=True)
            p = e.astype(BF16)
            inv = 1.0 / den
            lse = m + jnp.log(den)
            for h in range(HEADS):
                sl = pl.ds(HD * h, HD)
                o_ref[rows, sl] = (_nn(p[h, :, :QBLK], v_prev(sl)) + _nn(p[h, :, QBLK:], vc_ref[rows, sl])) * inv[h]
                l_ref[rows, sl] = jnp.broadcast_to(lse[h], (QBLK, HD))

    cur = pl.BlockSpec((2 * QBLK, AW), lambda r, n: (n, r))
    prev = pl.BlockSpec((QBLK, AW), lambda r, n: (jnp.maximum(2 * n - 1, 0), r))
    return pl.pallas_call(
        body, name=f"attn_fwd_d{d}", grid=(d, npair),
        in_specs=[cur, prev, cur, prev, cur], out_specs=[cur, cur],
        out_shape=[jax.ShapeDtypeStruct((L, d * AW), F32)] * 2,
        compiler_params=_cparams(("parallel", "parallel")),
    )(q, k, k, v, v)


def _attn_bwd(q, k, v, do, at, lse, d):
    L = q.shape[0]
    nb = L // QBLK
    npair = nb // 2

    def body(qc_ref, qn_ref, kp_ref, kc_ref, vp_ref, vc_ref, doc_ref, don_ref, atc_ref, atn_ref,
             lc_ref, ln_ref, dq_ref, dk_ref, dv_ref):
        pair = pl.program_id(1)
        mask_p, mask_c = _band_masks()
        prev_bias = jnp.where(mask_p, 0.0, NEG)
        for sub in range(2):
            rows = pl.ds(sub * QBLK, QBLK)
            second = pl.ds(QBLK, QBLK)
            if sub == 0:
                take = lambda cur_ref, nxt_ref, cols, i: cur_ref[rows if i == 0 else second, cols]
                prev_of = lambda p_ref, c_ref, cols: p_ref[:, cols]
                first, last = jnp.where(pair > 0, 0.0, NEG), 0.0
            else:
                take = lambda cur_ref, nxt_ref, cols, i: cur_ref[rows, cols] if i == 0 else nxt_ref[:, cols]
                prev_of = lambda p_ref, c_ref, cols: c_ref[pl.ds(0, QBLK), cols]
                first, last = 0.0, jnp.where(pair < npair - 1, 0.0, NEG)
            bias = jnp.concatenate([prev_bias + first, jnp.where(mask_c, 0.0, NEG), prev_bias + last], axis=1)
            s, dp, ls, dl, ops = [], [], [], [], []
            for h in range(HEADS):
                sl = pl.ds(HD * h, HD)
                one = pl.ds(HD * h, 1)
                q0, q1 = take(qc_ref, qn_ref, sl, 0), take(qc_ref, qn_ref, sl, 1)
                kp, kc = prev_of(kp_ref, kc_ref, sl), kc_ref[rows, sl]
                vp, vc = prev_of(vp_ref, vc_ref, sl), vc_ref[rows, sl]
                do0, do1 = take(doc_ref, don_ref, sl, 0), take(doc_ref, don_ref, sl, 1)
                do0b, do1b = do0.astype(BF16), do1.astype(BF16)
                s.append(jnp.concatenate([_nt(q0, kp), _nt(q0, kc), _nt(q1, kc)], axis=1))
                dp.append(jnp.concatenate([_nt(do0b, vp), _nt(do0b, vc), _nt(do1b, vc)], axis=1))
                dl0 = jnp.sum(do0 * take(atc_ref, atn_ref, sl, 0), axis=1, keepdims=True)
                dl1 = jnp.sum(do1 * take(atc_ref, atn_ref, sl, 1), axis=1, keepdims=True)
                dl.append(jnp.concatenate([jnp.broadcast_to(dl0, (QBLK, 2 * QBLK)), jnp.broadcast_to(dl1, (QBLK, QBLK))], axis=1))
                ls.append(jnp.concatenate([jnp.broadcast_to(take(lc_ref, ln_ref, one, 0), (QBLK, 2 * QBLK)),
                                           jnp.broadcast_to(take(lc_ref, ln_ref, one, 1), (QBLK, QBLK))], axis=1))
                ops.append((q0, q1, kp, kc, do0b, do1b))
            p = jnp.exp(jnp.stack(s) + bias - jnp.stack(ls))
            ds = (p * (jnp.stack(dp) - jnp.stack(dl))).astype(BF16)
            p = p.astype(BF16)
            for h in range(HEADS):
                sl = pl.ds(HD * h, HD)
                q0, q1, kp, kc, do0b, do1b = ops[h]
                dq_ref[rows, sl] = (_nn(ds[h, :, :QBLK], kp) + _nn(ds[h, :, QBLK:2 * QBLK], kc)).astype(BF16)
                dv_ref[rows, sl] = (_tn(p[h, :, QBLK:2 * QBLK], do0b) + _tn(p[h, :, 2 * QBLK:], do1b)).astype(BF16)
                dk_ref[rows, sl] = (_tn(ds[h, :, QBLK:2 * QBLK], q0) + _tn(ds[h, :, 2 * QBLK:], q1)).astype(BF16)

    cur = pl.BlockSpec((2 * QBLK, AW), lambda r, n: (n, r))
    prev = pl.BlockSpec((QBLK, AW), lambda r, n: (jnp.maximum(2 * n - 1, 0), r))
    nxt = pl.BlockSpec((QBLK, AW), lambda r, n: (jnp.minimum(2 * n + 2, nb - 1), r))
    return pl.pallas_call(
        body, name=f"attn_bwd_d{d}", grid=(d, npair),
        in_specs=[cur, nxt, prev, cur, prev, cur, cur, nxt, cur, nxt, cur, nxt], out_specs=[cur, cur, cur],
        out_shape=[jax.ShapeDtypeStruct((L, d * AW), BF16)] * 3,
        compiler_params=_cparams(("parallel", "parallel")),
    )(q, q, k, k, v, v, do, do, at, at, lse, lse)


def _attn_merge(outs, lses):
    tm = ROPE_TM

    def body(o1, o4, o16, l1, l4, l16, at_ref, ls_ref, at4, ls4, at16, ls16, *flat):
        so4, so16, sl4, sl16, sa, sl = _slab_groups(flat)
        _undilate(o4, so4, 4, tm)
        _undilate(o16, so16, 16, tm)
        _undilate(l4, sl4, 4, tm)
        _undilate(l16, sl16, 16, tm)
        for j in range(SLABS):
            cols = pl.ds(j * 128, 128)
            a, b, c = l1[:, cols], sl4[j][...], sl16[j][...]
            m = jnp.maximum(jnp.maximum(a, b), c)
            e1, e2, e3 = jnp.exp(a - m), jnp.exp(b - m), jnp.exp(c - m)
            s = e1 + e2 + e3
            inv = 1.0 / s
            attn = (e1 * inv) * o1[:, cols] + (e2 * inv) * so4[j][...] + (e3 * inv) * so16[j][...]
            lse = m + jnp.log(s)
            at_ref[:, cols] = attn
            ls_ref[:, cols] = lse
            sa[j][...] = attn
            sl[j][...] = lse
        _dilate(at4, sa, 4, tm)
        _dilate(at16, sa, 16, tm)
        _dilate(ls4, sl, 4, tm)
        _dilate(ls16, sl, 16, tm)

    specs = [_dil_spec(tm, d) for d in DILATIONS]
    tok = specs[0]
    return pl.pallas_call(
        body, name="attn_merge", grid=(T // tm,),
        in_specs=specs + specs, out_specs=[tok, tok, specs[1], specs[1], specs[2], specs[2]],
        out_shape=[jax.ShapeDtypeStruct((T, AW), F32)] * 2 + [jax.ShapeDtypeStruct((T // 4, 4 * AW), F32)] * 2
        + [jax.ShapeDtypeStruct((T // 16, 16 * AW), F32)] * 2,
        scratch_shapes=_slab_scratch(6, tm),
        compiler_params=_cparams(("parallel",)),
    )(*outs, *lses)


CONV_TM = 512
HALO = 8


def _conv_pre(ext, w, b):
    y = b + w[3] * ext
    for kk in range(1, CONV_K):
        y = y + w[3 - kk] * pltpu.roll(ext, kk, 0)
    return y


def _rows_to_block(rows, n, width):
    ri = lax.broadcasted_iota(jnp.int32, (n, width), 0)
    out = jnp.zeros((n, width), F32)
    for j, r in enumerate(rows):
        out = out + jnp.where(ri == j, r, 0.0)
    return out


def _conv_bwd(xbc, dact, ddt, w, b):
    nblk = T // CONV_TM
    per = CONV_TM // HALO

    def body(x_ref, xb_ref, xa_ref, g_ref, ga_ref, ddt_ref, w_ref, b_ref, dx_ref, dw_ref):
        i = pl.program_id(0)
        wv = [w_ref[pl.ds(j, 1), :] for j in range(CONV_K)]
        before = jnp.where(i > 0, xb_ref[...], 0.0)
        last = i == nblk - 1
        after = jnp.where(last, 0.0, xa_ref[...])
        g_after = jnp.where(last, 0.0, ga_ref[...])
        ext = jnp.concatenate([before, x_ref[...], after], axis=0)
        y = _conv_pre(ext, wv, b_ref[...])[HALO:]
        sg = _sigmoid(y)
        dy = jnp.concatenate([g_ref[...], g_after], axis=0) * (sg * (1.0 + y * (1.0 - sg)))
        n = CONV_TM + HALO
        dx = wv[3] * dy
        for kk in range(1, CONV_K):
            dx = dx + wv[3 - kk] * pltpu.roll(dy, n - kk, 0)
        dx_ref[:, pl.ds(0, CONV_CH)] = dx[:CONV_TM].astype(BF16)
        dx_ref[:, pl.ds(CONV_CH, DT_PAD)] = ddt_ref[...].astype(BF16)
        dyc = dy[:CONV_TM]
        rows = [jnp.sum(dyc * (pltpu.roll(ext, 3 - j, 0) if j < 3 else ext)[HALO:HALO + CONV_TM], axis=0, keepdims=True)
                for j in range(CONV_K)]
        rows.append(jnp.sum(dyc, axis=0, keepdims=True))
        part = _rows_to_block(rows, 8, CONV_CH)

        @pl.when(i == 0)
        def _():
            dw_ref[...] = jnp.zeros_like(dw_ref)
        dw_ref[...] += part

    blk = pl.BlockSpec((CONV_TM, CONV_CH), lambda i: (i, 0))
    hb = pl.BlockSpec((HALO, CONV_CH), lambda i: (jnp.maximum(i * per - 1, 0), 0))
    ha = pl.BlockSpec((HALO, CONV_CH), lambda i: (jnp.minimum((i + 1) * per, T // HALO - 1), 0))
    return pl.pallas_call(
        body, name="conv_bwd", grid=(nblk,),
        in_specs=[blk, hb, ha, blk, ha, pl.BlockSpec((CONV_TM, DT_PAD), lambda i: (i, 0)),
                  pl.BlockSpec((CONV_K, CONV_CH), lambda i: (0, 0)), pl.BlockSpec((1, CONV_CH), lambda i: (0, 0))],
        out_specs=[pl.BlockSpec((CONV_TM, CONV_CH + DT_PAD), lambda i: (i, 0)), pl.BlockSpec((8, CONV_CH), lambda i: (0, 0))],
        out_shape=[jax.ShapeDtypeStruct((T, CONV_CH + DT_PAD), BF16), jax.ShapeDtypeStruct((8, CONV_CH), F32)],
        compiler_params=_cparams(("arbitrary",)),
    )(xbc, xbc, xbc, dact, dact, ddt, w, b)


def _pick(mat, h):
    lane = lax.broadcasted_iota(jnp.int32, mat.shape, 1)
    return jnp.sum(jnp.where(lane == h, mat, 0.0), axis=1, keepdims=True)


def _heads(fn):
    return jnp.stack([fn(h) for h in range(HEADS)])


def _ssd_prep(dt_ref, bias_ref, alog_ref, dsk_ref, b_ref, c_ref, xs_ref, state_ref, cst):
    li = lax.broadcasted_iota(jnp.int32, (CHUNK, CHUNK), 0)
    si = lax.broadcasted_iota(jnp.int32, (CHUNK, CHUNK), 1)
    tri = li >= si
    dtp = dt_ref[...] + bias_ref[...]
    dt = _softplus(dtp)
    A = -jnp.exp(alog_ref[...])
    a = dt * A
    cs = jnp.dot(tri.astype(F32), a, precision=HIGHEST, preferred_element_type=F32)
    cst[...] = cs.T
    Bm = b_ref[...].astype(BF16)
    Cm = c_ref[...].astype(BF16)
    cb = _nt(Cm, Bm)
    dskv = dsk_ref[...]
    cs_col = _heads(lambda h: _pick(cs, h))
    cs_row = _heads(lambda h: cst[pl.ds(h, 1), :])
    dt_col = _heads(lambda h: _pick(dt, h))
    dsk_col = _heads(lambda h: _pick(dskv, h))
    lam = jnp.exp(jnp.where(tri, cs_col - cs_row, NEG))
    x = _heads(lambda h: xs_ref[:, pl.ds(HD * h, HD)])
    xdt = x * dt_col
    prev = _heads(lambda h: state_ref[pl.ds(HD * h, HD), :])
    lane = lax.broadcasted_iota(jnp.int32, (1, 1, CHUNK), 2)
    cl = jnp.sum(jnp.where(lane == CHUNK - 1, cs_row, 0.0), axis=2, keepdims=True)
    f = jnp.exp(cl - cs_col)
    return dict(li=li, si=si, dtp=dtp, dt=dt, A=A, Bm=Bm, Cm=Cm, cb=cb, cs_col=cs_col, dt_col=dt_col, dsk_col=dsk_col,
                lam=lam, x=x, xdt=xdt, prev=prev, cl=cl, f=f)


def _ssd_fwd(xbcdt, conv_w, conv_b, bias, alog, dsk, qkvz, attn, gs):
    nc = T // CHUNK
    per = CHUNK // HALO

    def body(xbc_ref, halo_ref, cw_ref, cb_ref, dt_ref, bias_ref, alog_ref, dsk_ref, z_ref, at_ref, gs_ref,
             y_ref, st_ref, cat_ref, act_ref, state, cst):
        @pl.when(pl.program_id(0) == 0)
        def _():
            state[...] = jnp.zeros_like(state)
        st_ref[...] = state[...]
        halo = jnp.where(pl.program_id(0) > 0, halo_ref[...], 0.0)
        pre = _conv_pre(jnp.concatenate([halo, xbc_ref[...]], axis=0),
                        [cw_ref[pl.ds(j, 1), :] for j in range(CONV_K)], cb_ref[...])[HALO:]
        act_ref[...] = pre * _sigmoid(pre)
        xs_ref, b_ref, c_ref = (act_ref.at[:, pl.ds(0, AW)], act_ref.at[:, pl.ds(AW, NS)],
                                act_ref.at[:, pl.ds(AW + NS, NS)])
        s = _ssd_prep(dt_ref, bias_ref, alog_ref, dsk_ref, b_ref, c_ref, xs_ref, state, cst)
        Bm, Cm, prev = s["Bm"], s["Cm"], s["prev"]
        g = (s["cb"] * s["lam"]).astype(BF16)
        xdtb = s["xdt"].astype(BF16)
        prevb = prev.astype(BF16)
        y = _heads(lambda h: _nn(g[h], xdtb[h])) + _heads(lambda h: _nt(Cm, prevb[h])) * jnp.exp(s["cs_col"])
        y = y + s["dsk_col"] * s["x"]
        xf = (s["xdt"] * s["f"]).astype(BF16)
        new = prev * jnp.exp(s["cl"]) + _heads(lambda h: _tn(xf[h], Bm))
        for h in range(HEADS):
            y_ref[:, pl.ds(HD * h, HD)] = y[h]
            state[pl.ds(HD * h, HD), :] = new[h]
        z = z_ref[...]
        gi = y_ref[...] * (z * _sigmoid(z))
        cat_ref[:, pl.ds(0, AW)] = at_ref[...].astype(BF16)
        cat_ref[:, pl.ds(AW, AW)] = (gi * _rstd(gi) * gs_ref[...]).astype(BF16)

    vec = pl.BlockSpec((1, DT_PAD), lambda c: (0, 0))
    blk = pl.BlockSpec((CHUNK, AW), lambda c: (c, 0))
    return pl.pallas_call(
        body, name="ssd_fwd", grid=(nc,),
        in_specs=[pl.BlockSpec((CHUNK, CONV_CH), lambda c: (c, 0)),
                  pl.BlockSpec((HALO, CONV_CH), lambda c: (jnp.maximum(c * per - 1, 0), 0)),
                  pl.BlockSpec((CONV_K, CONV_CH), lambda c: (0, 0)), pl.BlockSpec((1, CONV_CH), lambda c: (0, 0)),
                  pl.BlockSpec((CHUNK, DT_PAD), lambda c: (c, 6)),
                  vec, vec, vec, pl.BlockSpec((CHUNK, AW), lambda c: (c, 3)), blk, pl.BlockSpec((1, AW), lambda c: (0, 0))],
        out_specs=[blk, pl.BlockSpec((None, AW, NS), lambda c: (c, 0, 0)), pl.BlockSpec((CHUNK, D), lambda c: (c, 0)),
                   pl.BlockSpec((CHUNK, CONV_CH), lambda c: (c, 0))],
        out_shape=[jax.ShapeDtypeStruct((T, AW), F32), jax.ShapeDtypeStruct((nc, AW, NS), F32),
                   jax.ShapeDtypeStruct((T, D), BF16), jax.ShapeDtypeStruct((T, CONV_CH), F32)],
        scratch_shapes=[pltpu.VMEM((AW, NS), F32), pltpu.VMEM((CHUNK, CHUNK), F32)],
        compiler_params=_cparams(("arbitrary",)),
    )(xbcdt, xbcdt, conv_w, conv_b, xbcdt, bias, alog, dsk, qkvz, attn, gs)


def _ssd_bwd(act, xbcdt, bias, alog, dsk, states, y_ssd, qkvz, dcat, gs):
    nc = T // CHUNK

    def body(xs_ref, b_ref, c_ref, dt_ref, bias_ref, alog_ref, dsk_ref, st_ref, y_ref, z_ref, dyn_ref, gs_ref,
             dact_ref, ddt_ref, par_ref, dz_ref, dgs_ref, dstate, cst, dy_ref):
        step = pl.program_id(0)

        @pl.when(step == 0)
        def _():
            dstate[...] = jnp.zeros_like(dstate)
            par_ref[...] = jnp.zeros_like(par_ref)
            dgs_ref[...] = jnp.zeros_like(dgs_ref)
        z, yv, dyn = z_ref[...], y_ref[...], dyn_ref[...]
        sg = _sigmoid(z)
        sz = z * sg
        gi = yv * sz
        rg = _rstd(gi)
        ng = gi * rg
        dgi = _rms_bwd(ng, rg, gs_ref[...], dyn)
        dy_ref[...] = dgi * sz
        dz_ref[...] = dgi * yv * (sg * (1.0 + z * (1.0 - sg)))
        dgs_ref[...] += _colsum(dyn * ng)
        s = _ssd_prep(dt_ref, bias_ref, alog_ref, dsk_ref, b_ref, c_ref, xs_ref, st_ref, cst)
        Bm, Cm, prev, lam, x, xdt, f, cl = s["Bm"], s["Cm"], s["prev"], s["lam"], s["x"], s["xdt"], s["f"], s["cl"]
        lane = lax.broadcasted_iota(jnp.int32, (1, DT_PAD), 1)
        row = lax.broadcasted_iota(jnp.int32, (1, CHUNK, 1), 1)
        g = s["cb"] * lam
        gb, xdtb, prevb = g.astype(BF16), xdt.astype(BF16), prev.astype(BF16)
        dy = _heads(lambda h: dy_ref[:, pl.ds(HD * h, HD)])
        dyb = dy.astype(BF16)
        dnew = _heads(lambda h: dstate[pl.ds(HD * h, HD), :])
        dnewb = dnew.astype(BF16)
        E = jnp.exp(s["cs_col"])
        ecl = jnp.exp(cl)
        dG = _heads(lambda h: _nt(dyb[h], xdtb[h]))
        dxdt = _heads(lambda h: _tn(gb[h], dyb[h]))
        Yo = _heads(lambda h: _nt(Cm, prevb[h]))
        W = _heads(lambda h: _nt(Bm, dnewb[h]))
        dcb = jnp.sum(dG * lam, axis=0)
        Mm = dG * g
        col_sums = jnp.sum(Mm, axis=1, keepdims=True)
        dYo = (dy * E).astype(BF16)
        dxdt = dxdt + W * f
        dF = jnp.sum(W * xdt, axis=2, keepdims=True) * f
        dcl = jnp.sum(dnew * prev, axis=(1, 2), keepdims=True) * ecl + jnp.sum(dF, axis=1, keepdims=True)
        dcs = (jnp.sum(Mm, axis=2, keepdims=True) + jnp.sum(dy * Yo, axis=2, keepdims=True) * E - dF
               + jnp.where(row == CHUNK - 1, dcl, 0.0))
        ddt_x = jnp.sum(dxdt * x, axis=2, keepdims=True)
        dD = jnp.sum(dy * x, axis=(1, 2), keepdims=True)
        dx = s["dsk_col"] * dy + dxdt * s["dt_col"]
        xfb = (xdt * f).astype(BF16)
        dprev = _heads(lambda h: _tn(dYo[h], Cm)) + dnew * ecl
        dcbb = dcb.astype(BF16)
        dC = _nn(dcbb, Bm)
        dB = _tn(dcbb, Cm)
        dcs_mat = -_rows_to_block([col_sums[h] for h in range(HEADS)], CHUNK, CHUNK).T
        ddt_mat = jnp.zeros((CHUNK, DT_PAD), F32)
        dD_row = jnp.zeros((1, DT_PAD), F32)
        for h in range(HEADS):
            sl = pl.ds(HD * h, HD)
            dC = dC + _nn(dYo[h], prevb[h])
            dB = dB + _nn(xfb[h], dnewb[h])
            dcs_mat = dcs_mat + jnp.where(lane == h, dcs[h], 0.0)
            ddt_mat = ddt_mat + jnp.where(lane == h, ddt_x[h], 0.0)
            dD_row = dD_row + jnp.where(lane == h, dD[h], 0.0)
            dact_ref[:, sl] = dx[h]
            dstate[sl, :] = dprev[h]
        dact_ref[:, pl.ds(AW, NS)] = dB
        dact_ref[:, pl.ds(AW + NS, NS)] = dC
        da = jnp.dot((s["li"] <= s["si"]).astype(F32), dcs_mat, precision=HIGHEST, preferred_element_type=F32)
        ddtp = jnp.where(lane < HEADS, (ddt_mat + da * s["A"]) * _sigmoid(s["dtp"]), 0.0)
        ddt_ref[...] = ddtp
        dalog = jnp.where(lane < HEADS, jnp.sum(da * s["dt"], axis=0, keepdims=True) * s["A"], 0.0)
        par_ref[...] += _rows_to_block([jnp.sum(ddtp, axis=0, keepdims=True), dalog, dD_row], 8, DT_PAD)

    vec = pl.BlockSpec((1, DT_PAD), lambda c: (0, 0))
    rev = lambda c: nc - 1 - c
    return pl.pallas_call(
        body, name="ssd_bwd", grid=(nc,),
        in_specs=[pl.BlockSpec((CHUNK, AW), lambda c: (rev(c), 0)), pl.BlockSpec((CHUNK, NS), lambda c: (rev(c), 4)),
                  pl.BlockSpec((CHUNK, NS), lambda c: (rev(c), 5)), pl.BlockSpec((CHUNK, DT_PAD), lambda c: (rev(c), 6)),
                  vec, vec, vec,
                  pl.BlockSpec((None, AW, NS), lambda c: (rev(c), 0, 0)), pl.BlockSpec((CHUNK, AW), lambda c: (rev(c), 0)),
                  pl.BlockSpec((CHUNK, AW), lambda c: (rev(c), 3)), pl.BlockSpec((CHUNK, AW), lambda c: (rev(c), 1)),
                  pl.BlockSpec((1, AW), lambda c: (0, 0))],
        out_specs=[pl.BlockSpec((CHUNK, CONV_CH), lambda c: (rev(c), 0)), pl.BlockSpec((CHUNK, DT_PAD), lambda c: (rev(c), 0)),
                   pl.BlockSpec((8, DT_PAD), lambda c: (0, 0)), pl.BlockSpec((CHUNK, AW), lambda c: (rev(c), 0)),
                   pl.BlockSpec((1, AW), lambda c: (0, 0))],
        out_shape=[jax.ShapeDtypeStruct((T, CONV_CH), F32), jax.ShapeDtypeStruct((T, DT_PAD), F32),
                   jax.ShapeDtypeStruct((8, DT_PAD), F32), jax.ShapeDtypeStruct((T, AW), F32),
                   jax.ShapeDtypeStruct((1, AW), F32)],
        scratch_shapes=[pltpu.VMEM((AW, NS), F32), pltpu.VMEM((CHUNK, CHUNK), F32), pltpu.VMEM((CHUNK, AW), F32)],
        compiler_params=_cparams(("arbitrary",)),
    )(act, act, act, xbcdt, bias, alog, dsk, states, y_ssd, qkvz, dcat, gs)


def _place():
    return lax.axis_index("x"), lax.axis_index("y"), lax.axis_index("c")


def _slot(px, py, pc):
    return 4 * px + 2 * py + pc


SLAB_ROWS = 24


def _slab_pack(parts, name):
    n = len(parts)

    def body(*refs):
        slab = refs[n]
        slab[...] = jnp.zeros_like(slab)
        for ref, (arr, row) in zip(refs[:n], parts):
            slab[pl.ds(row, arr.shape[0]), pl.ds(0, arr.shape[1])] = ref[...]

    vm = pl.BlockSpec(memory_space=pltpu.VMEM)
    return pl.pallas_call(
        body, name=name, in_specs=[vm] * n, out_specs=vm, out_shape=jax.ShapeDtypeStruct((SLAB_ROWS, D), F32),
    )(*[a for a, _ in parts])


_HBM = pl.BlockSpec(memory_space=pltpu.HBM)
_SEM = pl.BlockSpec(memory_space=pltpu.SEMAPHORE)
_EFFECT = pltpu.SideEffectType.DATAFLOW_SIDE_EFFECTING


def _peers(x, y, c):
    out = []
    for kk in range(1, N_DEV):
        fx, fy, fc = kk >> 2 & 1, kk >> 1 & 1, kk & 1
        out.append((1 - x if fx else x, 1 - y if fy else y, 1 - c if fc else c))
    return out


def _send_start(src, per_peer, name, dep):
    (handles, token) = _send_start_many([src], per_peer, name, dep)
    return handles, token


def _near_peers(x, y, c):
    return [(x, y, 1 - c), (1 - x, y, c), (x, 1 - y, c), (1 - x, 1 - y, c)]


def _send_start_many(srcs, per_peer, name, dep, peers=_peers, npeers=N_DEV - 1):
    n = len(srcs)

    def body(*refs):
        src_refs, land_refs = refs[:n], refs[n:2 * n]
        send_sems, recv_sems = refs[2 * n + 1], refs[2 * n + 2]
        token = refs[-1]
        x, y, c = _place()
        mine = _slot(x, y, c)
        for a in range(n):
            for kk, peer in enumerate(peers(x, y, c)):
                pltpu.make_async_remote_copy(
                    src_ref=src_refs[a].at[_slot(*peer)] if per_peer else src_refs[a], dst_ref=land_refs[a].at[mine],
                    send_sem=send_sems.at[a * npeers + kk], recv_sem=recv_sems.at[a * npeers + kk],
                    device_id=peer, device_id_type=MESH).start()
        token[...] = jnp.zeros_like(token)

    lands = [lax.empty((N_DEV,) + tuple(s.shape[1:] if per_peer else s.shape), s.dtype) for s in srcs]
    hbm = lambda t: pltpu.with_memory_space_constraint(t, pltpu.HBM)
    outs = pl.pallas_call(
        body, name=name,
        out_shape=(pltpu.SemaphoreType.DMA((n * npeers,)), pltpu.SemaphoreType.DMA((n * npeers,)),
                   *[pltpu.HBM(s.shape, s.dtype) for s in srcs], *[pltpu.HBM(l.shape, l.dtype) for l in lands],
                   jax.ShapeDtypeStruct((8, 128), F32)),
        in_specs=(*[_HBM] * (2 * n), _ANY),
        out_specs=(_SEM, _SEM, *[_HBM] * (2 * n), pl.BlockSpec(memory_space=pltpu.VMEM)),
        input_output_aliases={i: 2 + i for i in range(2 * n)},
        compiler_params=pltpu.CompilerParams(has_side_effects=_EFFECT),
    )(*[hbm(s) for s in srcs], *[hbm(l) for l in lands], dep)
    return (outs[0], outs[1], list(outs[2:2 + n]), list(outs[2 + n:2 + 2 * n])), outs[-1]


def _send_wait(handles, after, name):
    srcs, lands = _send_wait_many(handles, after, name)
    return srcs[0], lands[0]


def _send_wait_many(handles, after, name, npeers=N_DEV - 1):
    send_sems, recv_sems, src_thrus, land_thrus = handles
    n = len(src_thrus)

    def body(*refs):
        land_refs = refs[n:2 * n]
        send_sems, recv_sems = refs[2 * n], refs[2 * n + 1]
        me = _place()
        for a in range(n):
            for kk in range(npeers):
                cp = pltpu.make_async_remote_copy(
                    src_ref=land_refs[a].at[0], dst_ref=land_refs[a].at[0],
                    send_sem=send_sems.at[a * npeers + kk], recv_sem=recv_sems.at[a * npeers + kk],
                    device_id=me, device_id_type=MESH)
                cp.wait_send()
                cp.wait_recv()

    both = list(src_thrus) + list(land_thrus)
    outs = pl.pallas_call(
        body, name=name,
        out_shape=tuple(pltpu.HBM(t.shape, t.dtype) for t in both),
        in_specs=(*[_HBM] * (2 * n), _SEM, _SEM, _ANY), out_specs=tuple([_HBM] * (2 * n)),
        input_output_aliases={i: i for i in range(2 * n)},
        compiler_params=pltpu.CompilerParams(has_side_effects=_EFFECT),
    )(*both, send_sems, recv_sems, after)
    return list(outs[:n]), list(outs[n:])


def _forward_start(lands, name, dep):
    n = len(lands)

    def body(*refs):
        land_refs = refs[:n]
        send_sems, recv_sems = refs[n + 1], refs[n + 2]
        token = refs[-1]
        x, y, c = _place()
        for a in range(n):
            for j, chip in enumerate([(1 - x, y), (x, 1 - y), (1 - x, 1 - y)]):
                blk = land_refs[a].at[_slot(*chip, c)]
                pltpu.make_async_remote_copy(
                    src_ref=blk, dst_ref=blk, send_sem=send_sems.at[a * 3 + j], recv_sem=recv_sems.at[a * 3 + j],
                    device_id=(x, y, 1 - c), device_id_type=MESH).start()
        token[...] = jnp.zeros_like(token)

    outs = pl.pallas_call(
        body, name=name,
        out_shape=(pltpu.SemaphoreType.DMA((n * 3,)), pltpu.SemaphoreType.DMA((n * 3,)),
                   *[pltpu.HBM(l.shape, l.dtype) for l in lands], jax.ShapeDtypeStruct((8, 128), F32)),
        in_specs=(*[_HBM] * n, _ANY), out_specs=(_SEM, _SEM, *[_HBM] * n, pl.BlockSpec(memory_space=pltpu.VMEM)),
        input_output_aliases={i: 2 + i for i in range(n)},
        compiler_params=pltpu.CompilerParams(has_side_effects=_EFFECT),
    )(*lands, dep)
    return (outs[0], outs[1], list(outs[2:2 + n])), outs[-1]


def _forward_wait(handles, after, name):
    send_sems, recv_sems, land_thrus = handles
    n = len(land_thrus)

    def body(*refs):
        land_refs = refs[:n]
        send_sems, recv_sems = refs[n], refs[n + 1]
        me = _place()
        for a in range(n):
            for j in range(3):
                cp = pltpu.make_async_remote_copy(
                    src_ref=land_refs[a].at[0], dst_ref=land_refs[a].at[0],
                    send_sem=send_sems.at[a * 3 + j], recv_sem=recv_sems.at[a * 3 + j], device_id=me, device_id_type=MESH)
                cp.wait_send()
                cp.wait_recv()

    outs = pl.pallas_call(
        body, name=name,
        out_shape=tuple(pltpu.HBM(t.shape, t.dtype) for t in land_thrus),
        in_specs=(*[_HBM] * n, _SEM, _SEM, _ANY), out_specs=tuple([_HBM] * n),
        input_output_aliases={i: i for i in range(n)},
        compiler_params=pltpu.CompilerParams(has_side_effects=_EFFECT),
    )(*land_thrus, send_sems, recv_sems, after)
    return list(outs)


def _sum_slots(land, name):
    _, R, C = land.shape
    tm = R if R <= 512 else 512

    def body(x_ref, o_ref):
        acc = x_ref[0].astype(F32)
        for j in range(1, N_DEV):
            acc = acc + x_ref[j].astype(F32)
        o_ref[...] = acc

    return pl.pallas_call(
        body, name=name, grid=(R // tm,),
        in_specs=[pl.BlockSpec((N_DEV, tm, C), lambda i: (0, i, 0))], out_specs=pl.BlockSpec((tm, C), lambda i: (i, 0)),
        out_shape=jax.ShapeDtypeStruct((R, C), F32), compiler_params=_cparams(("parallel",)),
    )(land)


def _adam_math(w, g, m, v):
    m2 = ADAM_B1 * m + (1.0 - ADAM_B1) * g
    v2 = ADAM_B2 * v + (1.0 - ADAM_B2) * (g * g)
    m_hat = m2 / (1.0 - ADAM_B1 ** ADAM_STEP)
    v_hat = v2 / (1.0 - ADAM_B2 ** ADAM_STEP)
    delta = -ADAM_LR * (m_hat / (jnp.sqrt(v_hat) + ADAM_EPS) + ADAM_WD * w)
    return delta, m2, v2


def _adamw(w, g, m, v, name):
    R, C = w.shape
    tm = R if R <= 512 else 256
    return _rowwise(lambda w, g, m, v: (_adam_math(w, g, m, v), ()), [w, g, m, v], [], [(C, F32)] * 3, [], tm=tm, name=name)


def _adamw_small(slab, slab_rows, g_conv_w, ws, ms, vs):
    n = len(ws)

    def body(*refs):
        slab_ref, gc_ref = refs[0], refs[1]
        w_refs, m_refs, v_refs = refs[2:2 + n], refs[2 + n:2 + 2 * n], refs[2 + 2 * n:2 + 3 * n]
        outs = refs[2 + 3 * n:]
        loss_ref = outs[0]
        g_out, d_out, m_out, v_out = (outs[1 + i * n:1 + (i + 1) * n] for i in range(4))
        loss_ref[...] = jnp.sum(slab_ref[pl.ds(6, 1), :], axis=1, keepdims=True)
        for i in range(n):
            g = gc_ref[...] if i == n - 1 else slab_ref[pl.ds(slab_rows[i], 1), pl.ds(0, ws[i].shape[1])]
            d, m2, v2 = _adam_math(w_refs[i][...], g, m_refs[i][...], v_refs[i][...])
            g_out[i][...] = g
            d_out[i][...] = d
            m_out[i][...] = m2
            v_out[i][...] = v2

    vm = pl.BlockSpec(memory_space=pltpu.VMEM)
    shapes = [jax.ShapeDtypeStruct(w.shape, F32) for w in ws]
    outs = pl.pallas_call(
        body, name="adamw_small", in_specs=[vm] * (2 + 3 * n), out_specs=[vm] * (1 + 4 * n),
        out_shape=[jax.ShapeDtypeStruct((1, 1), F32)] + shapes * 4,
    )(slab, g_conv_w, *ws, *ms, *vs)
    return outs[0], outs[1:1 + n], outs[1 + n:1 + 2 * n], outs[1 + 2 * n:1 + 3 * n], outs[1 + 3 * n:]


SMALL = ["norm_mix_pre", "norm_mix_post", "norm_mlp_pre", "norm_mlp_post", "norm_ple_post",
         "conv_b", "ssd_norm_g", "dt_bias", "a_log", "d_skip"]


def _pad_row(v, width=D):
    return jnp.pad(v, ((0, 0), (0, width - v.shape[1])))


def kernel(x, p, positions, norm_mix_pre, norm_mix_post, w_in, conv_w, conv_b, dt_bias, a_log, d_skip, ssd_norm_g, w_out, norm_mlp_pre, norm_mlp_post, w_up, w_down, w_ple_gate, w_ple_proj, norm_ple_post, loss_target, m_norm_mix_pre, m_norm_mix_post, m_w_in, m_conv_w, m_conv_b, m_dt_bias, m_a_log, m_d_skip, m_ssd_norm_g, m_w_out, m_norm_mlp_pre, m_norm_mlp_post, m_w_up, m_w_down, m_w_ple_gate, m_w_ple_proj, m_norm_ple_post, v_norm_mix_pre, v_norm_mix_post, v_w_in, v_conv_w, v_conv_b, v_dt_bias, v_a_log, v_d_skip, v_ssd_norm_g, v_w_out, v_norm_mlp_pre, v_norm_mlp_post, v_w_up, v_w_down, v_w_ple_gate, v_w_ple_proj, v_norm_ple_post):
    args = dict(locals())
    x2, p2, tgt = x[0], p[0, 0], loss_target[0]
    g1, g2, g3, g4, g5 = norm_mix_pre, norm_mix_post, norm_mlp_pre, norm_mlp_post, norm_ple_post

    me = _slot(*_place())
    pack_in = jnp.pad(w_in[0].T, ((0, W_IN_SHARD_PAD - W_IN_SHARD), (0, 0))).astype(BF16)
    rest = [w_out[0].astype(BF16), w_up[0].T.astype(BF16), w_down[0].astype(BF16), w_ple_gate[0].astype(BF16),
            w_ple_proj[0].T.reshape(32, D).astype(BF16)]
    conv_pack = jnp.pad(conv_w[0], ((0, 4), (0, 32)))
    in_handles, tok_in0 = _send_start_many([pack_in, conv_pack], False, "gather_in_start", g1, peers=_near_peers, npeers=4)

    inv_freq = ROPE_THETA ** (-jnp.arange(HD // 2, dtype=F32) * 2.0 / HD)
    pos = positions[0] + tok_in0[0, 0].astype(jnp.int32)
    ang = pos.astype(F32)[:, None] * inv_freq
    cos, sin = jnp.cos(ang), jnp.sin(ang)
    cos128 = jnp.concatenate([cos, cos, cos, cos], axis=1)
    sin128 = jnp.concatenate([-sin, sin, -sin, sin], axis=1)

    bias_w, alog_w, dsk_w = _pad_row(dt_bias, DT_PAD), _pad_row(a_log, DT_PAD), _pad_row(d_skip, DT_PAD)

    (u1,) = _rowwise(lambda a, g: ((a * _rstd(a) * g,), ()), [x2], [g1], [(D, BF16)], [], tm=512, name="norm_x",
                     deps=[cos128, sin128])
    p2b = p2.astype(BF16)

    in_back, in_land = _send_wait_many(in_handles, u1, "gather_in_wait", npeers=4)
    fw_handles, tok_fw = _forward_start(in_land, "gather_in_forward", u1)
    in_land = _forward_wait(fw_handles, tok_fw, "gather_in_forward_wait")
    gin = lax.dynamic_update_slice(in_land[0], in_back[0][None], (me, 0, 0))
    gconv = lax.dynamic_update_slice(in_land[1], in_back[1][None], (me, 0, 0))
    rest_handles, tok_rest = _send_start_many(rest, False, "gather_rest_start", gconv)
    w_inT = gin[:, :W_IN_SHARD].reshape(IN_W, D)
    w_qkvzT = w_inT[:4 * AW]
    w_xbcdtT = jnp.pad(w_inT[4 * AW:], ((0, DT_PAD - HEADS), (0, 0)))
    conv_full = gconv[:, :CONV_K, :96].transpose(1, 0, 2).reshape(CONV_K, CONV_CH)
    qkvz = _mm(u1, w_qkvzT, tb=True, tm=512, tn=2048, tk=1024, name="proj_qkvz", deps=[tok_rest])
    xbcdt = _mm(u1, w_xbcdtT, tb=True, tm=512, tn=896, tk=1024, name="proj_xbcdt")

    qkv = _rope_fwd(qkvz, cos128, sin128)
    qkv = [qkv[3 * i:3 * i + 3] for i in range(len(DILATIONS))]
    outs, lses = [], []
    for d, (qd, kd, vd) in zip(DILATIONS, qkv):
        o, l = _attn_fwd(qd, kd, vd, d)
        outs.append(o)
        lses.append(l)
    attn, lse, attn4, lse4, attn16, lse16 = _attn_merge(outs, lses)

    y_ssd, states, cat, act = _ssd_fwd(xbcdt, conv_full, conv_b, bias_w, alog_w, dsk_w, qkvz, attn, ssd_norm_g)


    rest_back, landed = _send_wait_many(rest_handles, cat, "gather_rest_wait")
    landed = [lax.dynamic_update_slice(l, b[None], (me, 0, 0)) for l, b in zip(landed, rest_back)]
    w_o, w_upT, w_dn, w_gate = landed[0].reshape(D, D), landed[1].reshape(DFF, D), landed[2].reshape(DFF, D), landed[3].reshape(D, D)
    w_projT = landed[4].reshape(D, PLE)

    def post1(mm, xx, ga):
        h = xx + mm * _rstd(mm) * ga
        return (mm, h, _rstd(h)), ()
    mix, h1, r3 = _mm_rows(post1, [(cat, w_o, False)], [x2], [g2], [(D, F32), (D, F32), (1, F32)], [], tm=512,
                           name="mix_out")

    a_up, ff, u2, h2, h2b = _mlp_fwd(h1, r3, g3, w_upT, w_dn, g4)
    relu2 = lambda a: jnp.square(jnp.maximum(a.astype(F32), 0.0))

    def final(gpre, ppv, hh, tg, g):
        sg = _sigmoid(gpre)
        ple = ppv * sg
        r = _rstd(ple)
        n = ple * r
        h3 = hh + n * g
        e = h3 - tg
        dh3 = e * (1.0 / D)
        dple = _rms_bwd(n, r, g, dh3)
        return (dh3, dple * sg, dple * ppv * sg * (1.0 - sg)), (_colsum(dh3 * n), _colsum(0.5 * e * e * (1.0 / D)))
    dh3, dpp, dgp, dg5, loss_vec = _mm_rows(final, [(h2b, w_gate, False), (p2b, w_projT, True)], [h2, tgt], [g5],
                                            [(D, F32), (D, BF16), (D, BF16)], [(1, D), (1, D)], tm=512, name="ple_loss")

    gw_projT = _mm(dpp, p2b, ta=True, tm=512, tn=256, tk=T, out_dtypes=(BF16,), name="gw_ple_proj")
    gw_gate = _mm(h2b, dgp, ta=True, tm=512, tn=1024, tk=T, out_dtypes=(BF16,), name="gw_ple_gate")
    rs_ple, tok_ple = _send_start_many([gw_projT.reshape(N_DEV, 32, D), gw_gate.reshape(N_DEV, 128, D)], True,
                                       "rs_start_w_ple", g1)
    def bwd_mlp_post(dg_, d3, f, g):
        dh2 = d3 + dg_
        r = _rstd(f)
        n = f * r
        return (dh2, _rms_bwd(n, r, g, dh2)), (_colsum(dh2 * n),)
    dh2, dff, dg4 = _mm_rows(bwd_mlp_post, [(dgp, w_gate, True)], [dh3, ff], [g4], [(D, F32), (D, BF16)], [(1, D)],
                             tm=512, name="bwd_ple_gate", deps=[tok_ple])

    gw_dn = _mm(a_up, dff, ta=True, tm=512, tn=1024, tk=T, a_pre=relu2, out_dtypes=(BF16,), name="gw_mlp_down")
    rs_dn, tok_dn = _send_start(gw_dn.reshape(N_DEV, 512, D), True, "rs_start_w_down", g1)
    da_up, du2 = _mlp_dx(dff, a_up, w_upT, w_dn, tok_dn)
    gw_upT = _mm(da_up, u2, ta=True, tm=512, tn=1024, tk=T, out_dtypes=(BF16,), name="gw_mlp_up")
    rs_up, tok_up = _send_start(gw_upT.reshape(N_DEV, 512, D), True, "rs_start_w_up", g1)

    def bwd_mix_post(d2, du, hh, rr, mm, ga, gb):
        n3 = hh * rr
        dh1 = d2 + _rms_bwd(n3, rr, gb, du)
        r = _rstd(mm)
        n2 = mm * r
        return (dh1, _rms_bwd(n2, r, ga, dh1)), (_colsum(du * n3), _colsum(dh1 * n2))
    dh1, dmix, dg3, dg2 = _rowwise(bwd_mix_post, [dh2, du2, h1, r3, mix], [g2, g3], [(D, F32), (D, BF16)],
                                   [(1, D), (1, D)], tm=512, name="bwd_post_mix", deps=[tok_up])

    gw_o = _mm(cat, dmix, ta=True, tm=512, tn=1024, tk=T, out_dtypes=(BF16,), name="gw_out")
    rs_o, tok_o = _send_start(gw_o.reshape(N_DEV, 128, D), True, "rs_start_w_out", g1)
    dcat, dattn4, dattn16 = _dx_out(dmix, w_o, tok_o)

    dact, ddtw, ssd_par, dz, dgs = _ssd_bwd(act, xbcdt, bias_w, alog_w, dsk_w, states, y_ssd, qkvz, dcat, ssd_norm_g)
    dxbcdt, conv_par = _conv_bwd(xbcdt, dact, ddtw, conv_full, conv_b)

    qkv_grads = [_attn_bwd(*qkv[0], dcat, attn, lse, 1),
                 _attn_bwd(*qkv[1], dattn4, attn4, lse4, 4),
                 _attn_bwd(*qkv[2], dattn16, attn16, lse16, 16)]
    dqkvz = _rope_bwd(qkv_grads, dz, cos128, sin128)

    gw_qkvzT = _mm(dqkvz, u1, ta=True, tm=512, tn=1024, tk=T, out_dtypes=(BF16,), name="gw_qkvz")
    gw_xbcdtT = _mm(dxbcdt, u1, ta=True, tm=896, tn=1024, tk=T, out_dtypes=(BF16,), name="gw_xbcdt")
    gw_inT = jnp.concatenate([gw_qkvzT, gw_xbcdtT], axis=0)[:IN_W]
    gw_inT = jnp.pad(gw_inT.reshape(N_DEV, W_IN_SHARD, D), ((0, 0), (0, W_IN_SHARD_PAD - W_IN_SHARD), (0, 0)))
    rs_in, tok_in = _send_start(gw_inT, True, "rs_start_w_in", g1)

    def bwd_in(ua, ub, d1, xx, g):
        rr = _rstd(xx)
        n = xx * rr
        du = ua + ub
        return (d1 + _rms_bwd(n, rr, g, du),), (_colsum(du * n),)
    grad_x, dg1 = _mm_rows(bwd_in, [(dqkvz, w_qkvzT, False), (dxbcdt, w_xbcdtT, False)], [dh1, x2], [g1],
                           [(D, F32)], [(1, D)], tm=512, name="bwd_in_proj", deps=[tok_in])

    my_slab = _slab_pack([(dg1, 0), (dg2, 1), (dg3, 2), (dg4, 3), (dg5, 4), (dgs, 5), (loss_vec, 6),
                          (conv_par, 8), (ssd_par, 16)], "slab_pack")
    slab_handles, tok_slab = _send_start_many([my_slab], False, "slab_start", g1)

    def scatter_finish(handles, nm, after):
        part, land = _send_wait(handles, after, "rs_wait_" + nm)
        own = lax.dynamic_slice(part, (me, 0, 0), (1,) + part.shape[1:])
        return _sum_slots(lax.dynamic_update_slice(land, own, (me, 0, 0)), "rs_sum_" + nm)
    g_out = scatter_finish(rs_o, "w_out", tok_slab)
    g_upT = scatter_finish(rs_up, "w_up", tok_slab)
    g_dn = scatter_finish(rs_dn, "w_down", tok_slab)
    ple_parts, ple_lands = _send_wait_many(rs_ple, tok_slab, "rs_wait_w_ple")
    g_projT, g_gate = [
        _sum_slots(lax.dynamic_update_slice(land, lax.dynamic_slice(part, (me, 0, 0), (1,) + part.shape[1:]), (me, 0, 0)),
                   "rs_sum_" + nm) for part, land, nm in zip(ple_parts, ple_lands, ("w_proj", "w_gate"))]

    grads = {
        "w_out": g_out[None], "w_up": g_upT.T[None], "w_down": g_dn[None],
        "w_ple_gate": g_gate[None], "w_ple_proj": g_projT.reshape(128, PLE).T[None],
    }
    delta, new_m, new_v = {}, {}, {}
    for nme in ["w_out", "w_up", "w_down", "w_ple_gate", "w_ple_proj", "w_in"]:
        if nme == "w_in":
            g_inT = scatter_finish(rs_in, "w_in", delta["w_down"])
            grads["w_in"] = g_inT[:W_IN_SHARD].T[None]
        dl, mm_, vv_ = _adamw(args[nme][0], grads[nme][0], args["m_" + nme][0], args["v_" + nme][0], "adamw_" + nme)
        delta[nme], new_m[nme], new_v[nme] = dl[None], mm_[None], vv_[None]

    slab_back, slab_land = _send_wait_many(slab_handles, delta["w_in"], "slab_wait")
    slab = _sum_slots(lax.dynamic_update_slice(slab_land[0], slab_back[0][None], (me, 0, 0)), "slab_sum")
    g_conv_w = lax.dynamic_slice(slab[8:12, :CONV_CH], (0, me * 96), (CONV_K, 96))
    small_names = SMALL + ["conv_w"]
    small_rows = [0, 1, 2, 3, 4, 12, 5, 16, 17, 18, None]
    pick = lambda prefix: [args[prefix + nme] for nme in SMALL] + [args[prefix + "conv_w"][0]]
    loss11, g_s, d_s, m_s, v_s = _adamw_small(slab, small_rows, g_conv_w, pick(""), pick("m_"), pick("v_"))
    loss = loss11[0, 0]
    for i, nme in enumerate(small_names):
        lead = (lambda t: t[None]) if nme == "conv_w" else (lambda t: t)
        grads[nme], delta[nme], new_m[nme], new_v[nme] = lead(g_s[i]), lead(d_s[i]), lead(m_s[i]), lead(v_s[i])

    order = ["norm_mix_pre", "norm_mix_post", "w_in", "conv_w", "conv_b", "dt_bias", "a_log", "d_skip", "ssd_norm_g",
             "w_out", "norm_mlp_pre", "norm_mlp_post", "w_up", "w_down", "w_ple_gate", "w_ple_proj", "norm_ple_post"]
    return (loss, grad_x[None], *[grads[n] for n in order], *[delta[n] for n in order],
            *[new_m[n] for n in order], *[new_v[n] for n in order])
```

```python
import jax
import jax.numpy as jnp
from jax import lax
from jax.experimental import pallas as pl
from jax.experimental.pallas import tpu as pltpu

F32 = jnp.float32
BF16 = jnp.bfloat16
MESH = pl.DeviceIdType.MESH
HIGHEST = lax.Precision.HIGHEST

N_DEV = 8
T = 4096
D = 1024
HEADS = 8
HD = 64
AW = 512
NS = 128
CONV_K = 4
CONV_CH = 768
CHUNK = 128
SSD_PER = 2
DFF = 4096
PLE = 256
EPS = 1e-6
ROPE_THETA = 10000.0
DILATIONS = (1, 4, 16)
QBLK = 128
NEG = -1e30
IN_W = 2824
W_IN_SHARD = 353
W_IN_SHARD_PAD = 384
DT_PAD = 128

ADAM_LR, ADAM_B1, ADAM_B2, ADAM_EPS, ADAM_WD, ADAM_STEP = 0.001, 0.9, 0.999, 1e-08, 0.01, 10

VMEM_LIMIT = 56 * 1024 * 1024


_ANY = pl.BlockSpec(memory_space=pl.ANY)


def _cparams(sem=None):
    return pltpu.CompilerParams(dimension_semantics=sem, vmem_limit_bytes=VMEM_LIMIT)


def _dot(a, b, ca, cb, precision=None):
    return lax.dot_general(a, b, (((ca,), (cb,)), ((), ())), preferred_element_type=F32, precision=precision)


def _nn(a, b):
    return _dot(a, b, 1, 0)


def _nt(a, b):
    return _dot(a, b, 1, 1)


def _tn(a, b):
    return _dot(a, b, 0, 0)


def _sigmoid(x):
    return 1.0 / (1.0 + jnp.exp(-x))


def _softplus(x):
    return jnp.maximum(x, 0.0) + jnp.log(1.0 + jnp.exp(-jnp.abs(x)))


def _mm(a, b, *, ta=False, tb=False, tm, tn, tk, name,
        a_pre=None, a_rows=(), a_cols=(), b_pre=None, b_rows=(), b_cols=(),
        epi=None, epi_tiles=(), out_dtypes=(F32,), deps=()):
    if ta:
        K, M = a.shape
    else:
        M, K = a.shape
    if tb:
        N, K2 = b.shape
    else:
        K2, N = b.shape
    assert K == K2 and M % tm == 0 and N % tn == 0 and K % tk == 0, (name, a.shape, b.shape)
    nk = K // tk
    if ta:
        a_spec = pl.BlockSpec((tk, tm), lambda i, j, k: (k, i))
        a_row_specs = [pl.BlockSpec((tk, 1), lambda i, j, k: (k, 0)) for _ in a_rows]
        a_col_specs = [pl.BlockSpec((1, tm), lambda i, j, k: (0, i)) for _ in a_cols]
    else:
        a_spec = pl.BlockSpec((tm, tk), lambda i, j, k: (i, k))
        a_row_specs = [pl.BlockSpec((tm, 1), lambda i, j, k: (i, 0)) for _ in a_rows]
        a_col_specs = [pl.BlockSpec((1, tk), lambda i, j, k: (0, k)) for _ in a_cols]
    if tb:
        b_spec = pl.BlockSpec((tn, tk), lambda i, j, k: (j, k))
        b_row_specs = [pl.BlockSpec((tn, 1), lambda i, j, k: (j, 0)) for _ in b_rows]
        b_col_specs = [pl.BlockSpec((1, tk), lambda i, j, k: (0, k)) for _ in b_cols]
    else:
        b_spec = pl.BlockSpec((tk, tn), lambda i, j, k: (k, j))
        b_row_specs = [pl.BlockSpec((tk, 1), lambda i, j, k: (k, 0)) for _ in b_rows]
        b_col_specs = [pl.BlockSpec((1, tn), lambda i, j, k: (0, j)) for _ in b_cols]
    o_spec = pl.BlockSpec((tm, tn), lambda i, j, k: (i, j))
    na, nb, ne, no = len(a_rows) + len(a_cols), len(b_rows) + len(b_cols), len(epi_tiles), len(out_dtypes)

    def body(*refs):
        a_ref, b_ref = refs[0], refs[1]
        a_ex = refs[2:2 + na]
        b_ex = refs[2 + na:2 + na + nb]
        e_ex = refs[2 + na + nb:2 + na + nb + ne]
        first_out = 2 + na + nb + ne + len(deps)
        outs = refs[first_out:first_out + no]

        def finish(res):
            vals = epi(res, *[r[...] for r in e_ex]) if epi is not None else (res,)
            for o_ref, val in zip(outs, vals):
                o_ref[...] = val.astype(o_ref.dtype)

        at = a_ref[...]
        if a_pre is not None:
            at = a_pre(at, *[r[...] for r in a_ex])
        bt = b_ref[...]
        if b_pre is not None:
            bt = b_pre(bt, *[r[...] for r in b_ex])
        prod = _dot(at.astype(BF16), bt.astype(BF16), 0 if ta else 1, 1 if tb else 0)
        if nk == 1:
            finish(prod)
            return
        acc = refs[-1]
        k = pl.program_id(2)

        @pl.when(k == 0)
        def _():
            acc[...] = jnp.zeros_like(acc)
        acc[...] += prod

        @pl.when(k == nk - 1)
        def _():
            finish(acc[...])

    outs = pl.pallas_call(
        body, name=name,
        grid=(M // tm, N // tn, nk),
        in_specs=([a_spec, b_spec] + a_row_specs + a_col_specs + b_row_specs + b_col_specs + [o_spec] * ne
                  + [_ANY] * len(deps)),
        out_specs=[o_spec] * no,
        out_shape=[jax.ShapeDtypeStruct((M, N), dt) for dt in out_dtypes],
        scratch_shapes=[pltpu.VMEM((tm, tn), F32)] if nk > 1 else [],
        compiler_params=_cparams(("parallel", "parallel", "arbitrary")),
    )(a, b, *a_rows, *a_cols, *b_rows, *b_cols, *epi_tiles, *deps)
    return outs[0] if no == 1 else outs


MLP_TM = 1024
MLP_TC = 512


def _mlp_fwd(h, r, g, w_upT, w_dn, g_post):
    nc = DFF // MLP_TC

    def body(h_ref, r_ref, g_ref, wu_ref, wd_ref, gp_ref, a_ref, ff_ref, u_ref, ho_ref, hob_ref, acc, u_scr):
        c = pl.program_id(1)

        @pl.when(c == 0)
        def _():
            u = (h_ref[...] * r_ref[...] * g_ref[...]).astype(BF16)
            u_scr[...] = u
            u_ref[...] = u
            acc[...] = jnp.zeros_like(acc)
        a = _nt(u_scr[...], wu_ref[...])
        a_ref[...] = a.astype(BF16)
        acc[...] += _nn(jnp.square(jnp.maximum(a, 0.0)).astype(BF16), wd_ref[...])

        @pl.when(c == nc - 1)
        def _():
            f = acc[...]
            ff_ref[...] = f
            ho = h_ref[...] + f * _rstd(f) * gp_ref[...]
            ho_ref[...] = ho
            hob_ref[...] = ho.astype(BF16)

    row = pl.BlockSpec((MLP_TM, D), lambda i, c: (i, 0))
    wsp = pl.BlockSpec((MLP_TC, D), lambda i, c: (c, 0))
    vec = pl.BlockSpec((1, D), lambda i, c: (0, 0))
    return pl.pallas_call(
        body, name="mlp_fwd", grid=(T // MLP_TM, nc),
        in_specs=[row, pl.BlockSpec((MLP_TM, 1), lambda i, c: (i, 0)), vec, wsp, wsp, vec],
        out_specs=[pl.BlockSpec((MLP_TM, MLP_TC), lambda i, c: (i, c)), row, row, row, row],
        out_shape=[jax.ShapeDtypeStruct((T, DFF), BF16), jax.ShapeDtypeStruct((T, D), F32), jax.ShapeDtypeStruct((T, D), BF16),
                   jax.ShapeDtypeStruct((T, D), F32), jax.ShapeDtypeStruct((T, D), BF16)],
        scratch_shapes=[pltpu.VMEM((MLP_TM, D), F32), pltpu.VMEM((MLP_TM, D), BF16)],
        compiler_params=_cparams(("parallel", "arbitrary")),
    )(h, r, g, w_upT, w_dn, g_post)


def _mlp_dx(dff, a, w_upT, w_dn, dep):
    nc = DFF // MLP_TC

    def body(d_ref, a_ref, wu_ref, wd_ref, dep_ref, da_ref, du_ref, acc, d_scr):
        c = pl.program_id(1)

        @pl.when(c == 0)
        def _():
            d_scr[...] = d_ref[...].astype(BF16)
            acc[...] = jnp.zeros_like(acc)
        da = (_nt(d_scr[...], wd_ref[...]) * (2.0 * jnp.maximum(a_ref[...].astype(F32), 0.0))).astype(BF16)
        da_ref[...] = da
        acc[...] += _nn(da, wu_ref[...])

        @pl.when(c == nc - 1)
        def _():
            du_ref[...] = acc[...]

    row = pl.BlockSpec((MLP_TM, D), lambda i, c: (i, 0))
    wsp = pl.BlockSpec((MLP_TC, D), lambda i, c: (c, 0))
    chunk = pl.BlockSpec((MLP_TM, MLP_TC), lambda i, c: (i, c))
    return pl.pallas_call(
        body, name="mlp_dx", grid=(T // MLP_TM, nc),
        in_specs=[row, chunk, wsp, wsp, _ANY], out_specs=[chunk, row],
        out_shape=[jax.ShapeDtypeStruct((T, DFF), BF16), jax.ShapeDtypeStruct((T, D), F32)],
        scratch_shapes=[pltpu.VMEM((MLP_TM, D), F32), pltpu.VMEM((MLP_TM, D), BF16)],
        compiler_params=_cparams(("parallel", "arbitrary")),
    )(dff, a, w_upT, w_dn, dep)


def _rowwise(fn, rows, vecs, out_rows, out_sums, *, tm, name, deps=()):
    specs, arrs = [], []
    R = None
    for r in rows:
        if isinstance(r, tuple):
            arr, width, cb = r
            specs.append(pl.BlockSpec((tm, width), lambda i, cb=cb: (i, cb)))
        else:
            arr = r
            specs.append(pl.BlockSpec((tm, arr.shape[1]), lambda i: (i, 0)))
        R = arr.shape[0] if R is None else R
        assert arr.shape[0] == R, name
        arrs.append(arr)
    assert R % tm == 0, name
    for v in vecs:
        specs.append(pl.BlockSpec(v.shape, lambda i: (0, 0)))
        arrs.append(v)
    nr, nv, no, ns = len(rows), len(vecs), len(out_rows), len(out_sums)
    out_specs = [pl.BlockSpec((tm, w), lambda i: (i, 0)) for w, _ in out_rows]
    out_specs += [pl.BlockSpec(s, lambda i: (0, 0)) for s in out_sums]
    out_shape = [jax.ShapeDtypeStruct((R, w), dt) for w, dt in out_rows]
    out_shape += [jax.ShapeDtypeStruct(s, F32) for s in out_sums]

    nd = len(deps)

    def body(*refs):
        ins = [r[...] for r in refs[:nr + nv]]
        o_refs = refs[nr + nv + nd:nr + nv + nd + no]
        s_refs = refs[nr + nv + nd + no:]
        o_vals, s_vals = fn(*ins)
        for ref, val in zip(o_refs, o_vals):
            ref[...] = val.astype(ref.dtype)
        if ns:
            @pl.when(pl.program_id(0) == 0)
            def _():
                for ref in s_refs:
                    ref[...] = jnp.zeros_like(ref)
            for ref, val in zip(s_refs, s_vals):
                ref[...] += val

    outs = pl.pallas_call(
        body, name=name, grid=(R // tm,), in_specs=specs + [_ANY] * nd, out_specs=out_specs, out_shape=out_shape,
        compiler_params=_cparams(("arbitrary",) if ns else ("parallel",)),
    )(*arrs, *deps)
    return outs


def _mm_rows(fn, mats, rows, vecs, out_rows, out_sums, *, tm, name, deps=()):
    R = mats[0][0].shape[0]
    assert R % tm == 0, name
    specs, arrs = [], []
    for a, b, tb in mats:
        specs += [pl.BlockSpec((tm, a.shape[1]), lambda i: (i, 0)), pl.BlockSpec(b.shape, lambda i: (0, 0))]
        arrs += [a, b]
    for r in rows:
        specs.append(pl.BlockSpec((tm, r.shape[1]), lambda i: (i, 0)))
        arrs.append(r)
    for v in vecs:
        specs.append(pl.BlockSpec(v.shape, lambda i: (0, 0)))
        arrs.append(v)
    nm, nr, nv, nd, no, ns = len(mats), len(rows), len(vecs), len(deps), len(out_rows), len(out_sums)
    out_specs = [pl.BlockSpec((tm, w), lambda i: (i, 0)) for w, _ in out_rows]
    out_specs += [pl.BlockSpec(s, lambda i: (0, 0)) for s in out_sums]
    out_shape = [jax.ShapeDtypeStruct((R, w), dt) for w, dt in out_rows] + [jax.ShapeDtypeStruct(s, F32) for s in out_sums]

    def body(*refs):
        prods = [_dot(refs[2 * p][...].astype(BF16), refs[2 * p + 1][...].astype(BF16), 1, 1 if mats[p][2] else 0)
                 for p in range(nm)]
        ins = [r[...] for r in refs[2 * nm:2 * nm + nr + nv]]
        first_out = 2 * nm + nr + nv + nd
        o_refs, s_refs = refs[first_out:first_out + no], refs[first_out + no:]
        o_vals, s_vals = fn(*prods, *ins)
        for ref, val in zip(o_refs, o_vals):
            ref[...] = val.astype(ref.dtype)
        if ns:
            @pl.when(pl.program_id(0) == 0)
            def _():
                for ref in s_refs:
                    ref[...] = jnp.zeros_like(ref)
            for ref, val in zip(s_refs, s_vals):
                ref[...] += val

    return pl.pallas_call(
        body, name=name, grid=(R // tm,), in_specs=specs + [_ANY] * nd, out_specs=out_specs, out_shape=out_shape,
        compiler_params=_cparams(("arbitrary",) if ns else ("parallel",)),
    )(*arrs, *deps)


def _colsum(x):
    return jnp.sum(x, axis=0, keepdims=True)


def _rstd(x):
    return lax.rsqrt(jnp.mean(x * x, axis=-1, keepdims=True) + EPS)


def _rms_bwd(xn, r, g, dy):
    dn = dy * g
    return r * (dn - xn * jnp.mean(dn * xn, axis=-1, keepdims=True))


def _partner(t):
    lane = lax.broadcasted_iota(jnp.int32, t.shape, 1)
    up = pltpu.roll(t, 96, 1)
    down = pltpu.roll(t, 32, 1)
    return jnp.where((lane % 64) < 32, up, down)


SLABS = AW // 128


def _rows(r, n, d):
    return pl.ds(r, n, stride=d) if d > 1 else pl.ds(0, n)


def _undilate(src_ref, dst, d, tm):
    for r in range(d):
        for j in range(SLABS):
            dst[j][_rows(r, tm // d, d), :] = src_ref[:, pl.ds(r * AW + j * 128, 128)].astype(dst[j].dtype)


def _dilate(dst_ref, src, d, tm):
    for r in range(d):
        for j in range(SLABS):
            dst_ref[:, pl.ds(r * AW + j * 128, 128)] = src[j][_rows(r, tm // d, d), :].astype(dst_ref.dtype)


def _slab_scratch(n, tm):
    return [pltpu.VMEM((tm, 128), F32)] * (SLABS * n)


def _slab_groups(flat):
    return [flat[SLABS * i:SLABS * (i + 1)] for i in range(len(flat) // SLABS)]


def _slab_specs(tm, first):
    return [pl.BlockSpec((tm, 128), lambda i, j=j: (i, first + j)) for j in range(SLABS)]


def _dil_spec(tm, d):
    return pl.BlockSpec((tm // d, d * AW), lambda i: (i, 0))


ROPE_TM = 512


def _rope_fwd(qkvz, cos128, sin128):
    tm = ROPE_TM

    def body(*refs):
        q_refs, k_refs, v_refs = refs[0:4], refs[4:8], refs[8:12]
        c_ref, s_ref = refs[12], refs[13]
        outs = refs[14:23]
        qs, ks = _slab_groups(refs[23:])
        c, s = c_ref[...], s_ref[...]
        for j in range(SLABS):
            q, k = q_refs[j][...], k_refs[j][...]
            qs[j][...] = (q * c + _partner(q) * s) * (HD ** -0.5)
            ks[j][...] = k * c + _partner(k) * s
        for di, d in enumerate(DILATIONS):
            oq, ok, ov = outs[3 * di:3 * di + 3]
            for r in range(d):
                rows = _rows(r, tm // d, d)
                for j in range(SLABS):
                    cols = pl.ds(r * AW + j * 128, 128)
                    oq[:, cols] = qs[j][rows, :].astype(BF16)
                    ok[:, cols] = ks[j][rows, :].astype(BF16)
                    ov[:, cols] = v_refs[j][rows, :].astype(BF16)

    tab = pl.BlockSpec((tm, 128), lambda i: (i, 0))
    out_specs, out_shape = [], []
    for d in DILATIONS:
        out_specs += [_dil_spec(tm, d)] * 3
        out_shape += [jax.ShapeDtypeStruct((T // d, d * AW), BF16)] * 3
    return pl.pallas_call(
        body, name="rope_fwd", grid=(T // tm,),
        in_specs=_slab_specs(tm, 0) + _slab_specs(tm, 4) + _slab_specs(tm, 8) + [tab, tab],
        out_specs=out_specs, out_shape=out_shape, scratch_shapes=_slab_scratch(2, tm),
        compiler_params=_cparams(("parallel",)),
    )(*([qkvz] * 12), cos128, sin128)


def _rope_bwd(grads, dz, cos128, sin128):
    tm = 256

    def body(*refs):
        g_refs = refs[0:9]
        dz_ref, c_ref, s_ref, o_ref = refs[9], refs[10], refs[11], refs[12]
        scr = _slab_groups(refs[13:])
        for di, d in enumerate(DILATIONS[1:]):
            for t in range(3):
                _undilate(g_refs[3 * (di + 1) + t], scr[3 * di + t], d, tm)
        c, s = c_ref[...], s_ref[...]
        for j in range(SLABS):
            cols = pl.ds(j * 128, 128)
            tot = [g_refs[t][:, cols] + scr[t][j][...] + scr[3 + t][j][...] for t in range(3)]
            dqr = tot[0] * (HD ** -0.5)
            o_ref[:, pl.ds(j * 128, 128)] = (dqr * c + _partner(dqr * s)).astype(BF16)
            o_ref[:, pl.ds(AW + j * 128, 128)] = (tot[1] * c + _partner(tot[1] * s)).astype(BF16)
            o_ref[:, pl.ds(2 * AW + j * 128, 128)] = tot[2].astype(BF16)
        o_ref[:, pl.ds(3 * AW, AW)] = dz_ref[...].astype(BF16)

    tab = pl.BlockSpec((tm, 128), lambda i: (i, 0))
    in_specs, args = [], []
    for d, g in zip(DILATIONS, grads):
        in_specs += [_dil_spec(tm, d)] * 3
        args += list(g)
    return pl.pallas_call(
        body, name="rope_bwd", grid=(T // tm,),
        in_specs=in_specs + [pl.BlockSpec((tm, AW), lambda i: (i, 0)), tab, tab],
        out_specs=pl.BlockSpec((tm, 4 * AW), lambda i: (i, 0)),
        out_shape=jax.ShapeDtypeStruct((T, 4 * AW), BF16),
        scratch_shapes=_slab_scratch(6, tm),
        compiler_params=_cparams(("parallel",)),
    )(*args, dz, cos128, sin128)


def _dx_out(dmix, w_o, dep):
    tm = ROPE_TM

    def body(a_ref, w_ref, dep_ref, dcat_ref, o4, o16, *slabs):
        prod = _nt(a_ref[...].astype(BF16), w_ref[...].astype(BF16))
        dcat_ref[...] = prod
        for j in range(SLABS):
            slabs[j][...] = prod[:, 128 * j:128 * (j + 1)]
        _dilate(o4, slabs, 4, tm)
        _dilate(o16, slabs, 16, tm)

    return pl.pallas_call(
        body, name="dx_out", grid=(T // tm,),
        in_specs=[pl.BlockSpec((tm, D), lambda i: (i, 0)), pl.BlockSpec((D, D), lambda i: (0, 0)), _ANY],
        out_specs=[pl.BlockSpec((tm, D), lambda i: (i, 0)), _dil_spec(tm, 4), _dil_spec(tm, 16)],
        out_shape=[jax.ShapeDtypeStruct((T, D), F32), jax.ShapeDtypeStruct((T // 4, 4 * AW), F32),
                   jax.ShapeDtypeStruct((T // 16, 16 * AW), F32)],
        scratch_shapes=_slab_scratch(1, tm), compiler_params=_cparams(("parallel",)),
    )(dmix, w_o, dep)


def _band_masks():
    qi = lax.broadcasted_iota(jnp.int32, (QBLK, QBLK), 0)
    kj = lax.broadcasted_iota(jnp.int32, (QBLK, QBLK), 1)
    return kj >= qi, kj <= qi


def _attn_fwd(q, k, v, d):
    L = q.shape[0]
    npair = L // (2 * QBLK)

    def body(q_ref, kp_ref, kc_ref, vp_ref, vc_ref, o_ref, l_ref):
        pair = pl.program_id(1)
        mask_p, mask_c = _band_masks()
        for sub in range(2):
            rows = pl.ds(sub * QBLK, QBLK)
            first = jnp.where(pair > 0, 0.0, NEG) if sub == 0 else 0.0
            bias = jnp.concatenate([jnp.where(mask_p, 0.0, NEG) + first, jnp.where(mask_c, 0.0, NEG)], axis=1)
            k_prev = (lambda sl: kp_ref[:, sl]) if sub == 0 else (lambda sl: kc_ref[pl.ds(0, QBLK), sl])
            v_prev = (lambda sl: vp_ref[:, sl]) if sub == 0 else (lambda sl: vc_ref[pl.ds(0, QBLK), sl])
            s = []
            for h in range(HEADS):
                sl = pl.ds(HD * h, HD)
                qh = q_ref[rows, sl]
                s.append(jnp.concatenate([_nt(qh, k_prev(sl)), _nt(qh, kc_ref[rows, sl])], axis=1))
            s = jnp.stack(s) + bias
            m = jnp.max(s, axis=2, keepdims=True)
            e = jnp.exp(s - m)
            den = jnp.sum(e, axis=2, keepdims=True)
            p = e.astype(BF16)
            inv = 1.0 / den
            lse = m + jnp.log(den)
            for h in range(HEADS):
                sl = pl.ds(HD * h, HD)
                o_ref[rows, sl] = (_nn(p[h, :, :QBLK], v_prev(sl)) + _nn(p[h, :, QBLK:], vc_ref[rows, sl])) * inv[h]
                l_ref[rows, sl] = jnp.broadcast_to(lse[h], (QBLK, HD))

    cur = pl.BlockSpec((2 * QBLK, AW), lambda r, n: (n, r))
    prev = pl.BlockSpec((QBLK, AW), lambda r, n: (jnp.maximum(2 * n - 1, 0), r))
    return pl.pallas_call(
        body, name=f"attn_fwd_d{d}", grid=(d, npair),
        in_specs=[cur, prev, cur, prev, cur], out_specs=[cur, cur],
        out_shape=[jax.ShapeDtypeStruct((L, d * AW), F32)] * 2,
        compiler_params=_cparams(("parallel", "parallel")),
    )(q, k, k, v, v)


def _attn_bwd(q, k, v, do, at, lse, d):
    L = q.shape[0]
    nb = L // QBLK
    npair = nb // 2

    def body(qc_ref, qn_ref, kp_ref, kc_ref, vp_ref, vc_ref, doc_ref, don_ref, atc_ref, atn_ref,
             lc_ref, ln_ref, dq_ref, dk_ref, dv_ref):
        pair = pl.program_id(1)
        mask_p, mask_c = _band_masks()
        prev_bias = jnp.where(mask_p, 0.0, NEG)
        for sub in range(2):
            rows = pl.ds(sub * QBLK, QBLK)
            second = pl.ds(QBLK, QBLK)
            if sub == 0:
                take = lambda cur_ref, nxt_ref, cols, i: cur_ref[rows if i == 0 else second, cols]
                prev_of = lambda p_ref, c_ref, cols: p_ref[:, cols]
                first, last = jnp.where(pair > 0, 0.0, NEG), 0.0
            else:
                take = lambda cur_ref, nxt_ref, cols, i: cur_ref[rows, cols] if i == 0 else nxt_ref[:, cols]
                prev_of = lambda p_ref, c_ref, cols: c_ref[pl.ds(0, QBLK), cols]
                first, last = 0.0, jnp.where(pair < npair - 1, 0.0, NEG)
            bias = jnp.concatenate([prev_bias + first, jnp.where(mask_c, 0.0, NEG), prev_bias + last], axis=1)
            s, dp, ls, dl, ops = [], [], [], [], []
            for h in range(HEADS):
                sl = pl.ds(HD * h, HD)
                one = pl.ds(HD * h, 1)
                q0, q1 = take(qc_ref, qn_ref, sl, 0), take(qc_ref, qn_ref, sl, 1)
                kp, kc = prev_of(kp_ref, kc_ref, sl), kc_ref[rows, sl]
                vp, vc = prev_of(vp_ref, vc_ref, sl), vc_ref[rows, sl]
                do0, do1 = take(doc_ref, don_ref, sl, 0), take(doc_ref, don_ref, sl, 1)
                do0b, do1b = do0.astype(BF16), do1.astype(BF16)
                s.append(jnp.concatenate([_nt(q0, kp), _nt(q0, kc), _nt(q1, kc)], axis=1))
                dp.append(jnp.concatenate([_nt(do0b, vp), _nt(do0b, vc), _nt(do1b, vc)], axis=1))
                dl0 = jnp.sum(do0 * take(atc_ref, atn_ref, sl, 0), axis=1, keepdims=True)
                dl1 = jnp.sum(do1 * take(atc_ref, atn_ref, sl, 1), axis=1, keepdims=True)
                dl.append(jnp.concatenate([jnp.broadcast_to(dl0, (QBLK, 2 * QBLK)), jnp.broadcast_to(dl1, (QBLK, QBLK))], axis=1))
                ls.append(jnp.concatenate([jnp.broadcast_to(take(lc_ref, ln_ref, one, 0), (QBLK, 2 * QBLK)),
                                           jnp.broadcast_to(take(lc_ref, ln_ref, one, 1), (QBLK, QBLK))], axis=1))
                ops.append((q0, q1, kp, kc, do0b, do1b))
            p = jnp.exp(jnp.stack(s) + bias - jnp.stack(ls))
            ds = (p * (jnp.stack(dp) - jnp.stack(dl))).astype(BF16)
            p = p.astype(BF16)
            for h in range(HEADS):
                sl = pl.ds(HD * h, HD)
                q0, q1, kp, kc, do0b, do1b = ops[h]
                dq_ref[rows, sl] = (_nn(ds[h, :, :QBLK], kp) + _nn(ds[h, :, QBLK:2 * QBLK], kc)).astype(BF16)
                dv_ref[rows, sl] = (_tn(p[h, :, QBLK:2 * QBLK], do0b) + _tn(p[h, :, 2 * QBLK:], do1b)).astype(BF16)
                dk_ref[rows, sl] = (_tn(ds[h, :, QBLK:2 * QBLK], q0) + _tn(ds[h, :, 2 * QBLK:], q1)).astype(BF16)

    cur = pl.BlockSpec((2 * QBLK, AW), lambda r, n: (n, r))
    prev = pl.BlockSpec((QBLK, AW), lambda r, n: (jnp.maximum(2 * n - 1, 0), r))
    nxt = pl.BlockSpec((QBLK, AW), lambda r, n: (jnp.minimum(2 * n + 2, nb - 1), r))
    return pl.pallas_call(
        body, name=f"attn_bwd_d{d}", grid=(d, npair),
        in_specs=[cur, nxt, prev, cur, prev, cur, cur, nxt, cur, nxt, cur, nxt], out_specs=[cur, cur, cur],
        out_shape=[jax.ShapeDtypeStruct((L, d * AW), BF16)] * 3,
        compiler_params=_cparams(("parallel", "parallel")),
    )(q, q, k, k, v, v, do, do, at, at, lse, lse)


def _attn_merge(outs, lses):
    tm = ROPE_TM

    def body(o1, o4, o16, l1, l4, l16, at_ref, ls_ref, at4, ls4, at16, ls16, *flat):
        so4, so16, sl4, sl16, sa, sl = _slab_groups(flat)
        _undilate(o4, so4, 4, tm)
        _undilate(o16, so16, 16, tm)
        _undilate(l4, sl4, 4, tm)
        _undilate(l16, sl16, 16, tm)
        for j in range(SLABS):
            cols = pl.ds(j * 128, 128)
            a, b, c = l1[:, cols], sl4[j][...], sl16[j][...]
            m = jnp.maximum(jnp.maximum(a, b), c)
            e1, e2, e3 = jnp.exp(a - m), jnp.exp(b - m), jnp.exp(c - m)
            s = e1 + e2 + e3
            inv = 1.0 / s
            attn = (e1 * inv) * o1[:, cols] + (e2 * inv) * so4[j][...] + (e3 * inv) * so16[j][...]
            lse = m + jnp.log(s)
            at_ref[:, cols] = attn
            ls_ref[:, cols] = lse
            sa[j][...] = attn
            sl[j][...] = lse
        _dilate(at4, sa, 4, tm)
        _dilate(at16, sa, 16, tm)
        _dilate(ls4, sl, 4, tm)
        _dilate(ls16, sl, 16, tm)

    specs = [_dil_spec(tm, d) for d in DILATIONS]
    tok = specs[0]
    return pl.pallas_call(
        body, name="attn_merge", grid=(T // tm,),
        in_specs=specs + specs, out_specs=[tok, tok, specs[1], specs[1], specs[2], specs[2]],
        out_shape=[jax.ShapeDtypeStruct((T, AW), F32)] * 2 + [jax.ShapeDtypeStruct((T // 4, 4 * AW), F32)] * 2
        + [jax.ShapeDtypeStruct((T // 16, 16 * AW), F32)] * 2,
        scratch_shapes=_slab_scratch(6, tm),
        compiler_params=_cparams(("parallel",)),
    )(*outs, *lses)


CONV_TM = 512
HALO = 8


def _conv_pre(ext, w, b):
    y = b + w[3] * ext
    for kk in range(1, CONV_K):
        y = y + w[3 - kk] * pltpu.roll(ext, kk, 0)
    return y


def _rows_to_block(rows, n, width):
    ri = lax.broadcasted_iota(jnp.int32, (n, width), 0)
    out = jnp.zeros((n, width), F32)
    for j, r in enumerate(rows):
        out = out + jnp.where(ri == j, r, 0.0)
    return out


def _conv_bwd(xbc, dact, ddt, w, b):
    nblk = T // CONV_TM
    per = CONV_TM // HALO

    def body(x_ref, xb_ref, xa_ref, g_ref, ga_ref, ddt_ref, w_ref, b_ref, dx_ref, dw_ref):
        i = pl.program_id(0)
        wv = [w_ref[pl.ds(j, 1), :] for j in range(CONV_K)]
        before = jnp.where(i > 0, xb_ref[...], 0.0)
        last = i == nblk - 1
        after = jnp.where(last, 0.0, xa_ref[...])
        g_after = jnp.where(last, 0.0, ga_ref[...])
        ext = jnp.concatenate([before, x_ref[...], after], axis=0)
        y = _conv_pre(ext, wv, b_ref[...])[HALO:]
        sg = _sigmoid(y)
        dy = jnp.concatenate([g_ref[...], g_after], axis=0) * (sg * (1.0 + y * (1.0 - sg)))
        n = CONV_TM + HALO
        dx = wv[3] * dy
        for kk in range(1, CONV_K):
            dx = dx + wv[3 - kk] * pltpu.roll(dy, n - kk, 0)
        dx_ref[:, pl.ds(0, CONV_CH)] = dx[:CONV_TM].astype(BF16)
        dx_ref[:, pl.ds(CONV_CH, DT_PAD)] = ddt_ref[...].astype(BF16)
        dyc = dy[:CONV_TM]
        rows = [jnp.sum(dyc * (pltpu.roll(ext, 3 - j, 0) if j < 3 else ext)[HALO:HALO + CONV_TM], axis=0, keepdims=True)
                for j in range(CONV_K)]
        rows.append(jnp.sum(dyc, axis=0, keepdims=True))
        part = _rows_to_block(rows, 8, CONV_CH)

        @pl.when(i == 0)
        def _():
            dw_ref[...] = jnp.zeros_like(dw_ref)
        dw_ref[...] += part

    blk = pl.BlockSpec((CONV_TM, CONV_CH), lambda i: (i, 0))
    hb = pl.BlockSpec((HALO, CONV_CH), lambda i: (jnp.maximum(i * per - 1, 0), 0))
    ha = pl.BlockSpec((HALO, CONV_CH), lambda i: (jnp.minimum((i + 1) * per, T // HALO - 1), 0))
    return pl.pallas_call(
        body, name="conv_bwd", grid=(nblk,),
        in_specs=[blk, hb, ha, blk, ha, pl.BlockSpec((CONV_TM, DT_PAD), lambda i: (i, 0)),
                  pl.BlockSpec((CONV_K, CONV_CH), lambda i: (0, 0)), pl.BlockSpec((1, CONV_CH), lambda i: (0, 0))],
        out_specs=[pl.BlockSpec((CONV_TM, CONV_CH + DT_PAD), lambda i: (i, 0)), pl.BlockSpec((8, CONV_CH), lambda i: (0, 0))],
        out_shape=[jax.ShapeDtypeStruct((T, CONV_CH + DT_PAD), BF16), jax.ShapeDtypeStruct((8, CONV_CH), F32)],
        compiler_params=_cparams(("arbitrary",)),
    )(xbc, xbc, xbc, dact, dact, ddt, w, b)


def _pick(mat, h):
    lane = lax.broadcasted_iota(jnp.int32, mat.shape, 1)
    return jnp.sum(jnp.where(lane == h, mat, 0.0), axis=1, keepdims=True)


def _heads(fn):
    return jnp.stack([fn(h) for h in range(HEADS)])


def _ssd_prep(dt_ref, bias_ref, alog_ref, dsk_ref, b_ref, c_ref, xs_ref, state_ref, cst):
    li = lax.broadcasted_iota(jnp.int32, (CHUNK, CHUNK), 0)
    si = lax.broadcasted_iota(jnp.int32, (CHUNK, CHUNK), 1)
    tri = li >= si
    dtp = dt_ref[...] + bias_ref[...]
    dt = _softplus(dtp)
    A = -jnp.exp(alog_ref[...])
    a = dt * A
    cs = jnp.dot(tri.astype(F32), a, precision=HIGHEST, preferred_element_type=F32)
    cst[...] = cs.T
    Bm = b_ref[...].astype(BF16)
    Cm = c_ref[...].astype(BF16)
    cb = _nt(Cm, Bm)
    dskv = dsk_ref[...]
    cs_col = _heads(lambda h: _pick(cs, h))
    cs_row = _heads(lambda h: cst[pl.ds(h, 1), :])
    dt_col = _heads(lambda h: _pick(dt, h))
    dsk_col = _heads(lambda h: _pick(dskv, h))
    lam = jnp.exp(jnp.where(tri, cs_col - cs_row, NEG))
    x = _heads(lambda h: xs_ref[:, pl.ds(HD * h, HD)])
    xdt = x * dt_col
    prev = _heads(lambda h: state_ref[pl.ds(HD * h, HD), :])
    lane = lax.broadcasted_iota(jnp.int32, (1, 1, CHUNK), 2)
    cl = jnp.sum(jnp.where(lane == CHUNK - 1, cs_row, 0.0), axis=2, keepdims=True)
    f = jnp.exp(cl - cs_col)
    return dict(li=li, si=si, dtp=dtp, dt=dt, A=A, Bm=Bm, Cm=Cm, cb=cb, cs_col=cs_col, dt_col=dt_col, dsk_col=dsk_col,
                lam=lam, x=x, xdt=xdt, prev=prev, cl=cl, f=f)


def _ssd_fwd(xbcdt, conv_w, conv_b, bias, alog, dsk, qkvz, attn, gs):
    nc = T // CHUNK
    R = SSD_PER * CHUNK
    per = R // HALO

    def body(xbc_ref, halo_ref, cw_ref, cb_ref, dt_ref, bias_ref, alog_ref, dsk_ref, z_ref, at_ref, gs_ref,
             y_ref, st_ref, cat_ref, act_ref, state, cst):
        @pl.when(pl.program_id(0) == 0)
        def _():
            state[...] = jnp.zeros_like(state)
        for sub in range(SSD_PER):
            rows = pl.ds(sub * CHUNK, CHUNK)
            st_ref[sub] = state[...]
            halo = (jnp.where(pl.program_id(0) > 0, halo_ref[...], 0.0) if sub == 0
                    else xbc_ref[pl.ds(sub * CHUNK - HALO, HALO), :])
            one_chunk(halo, xbc_ref.at[rows, :], cw_ref, cb_ref, dt_ref.at[rows, :], bias_ref, alog_ref, dsk_ref,
                      z_ref.at[rows, :], at_ref.at[rows, :], gs_ref, y_ref.at[rows, :], cat_ref.at[rows, :],
                      act_ref.at[rows, :], state, cst)

    def one_chunk(halo, xbc_ref, cw_ref, cb_ref, dt_ref, bias_ref, alog_ref, dsk_ref, z_ref, at_ref, gs_ref,
                  y_ref, cat_ref, act_ref, state, cst):
        pre = _conv_pre(jnp.concatenate([halo, xbc_ref[...]], axis=0),
                        [cw_ref[pl.ds(j, 1), :] for j in range(CONV_K)], cb_ref[...])[HALO:]
        act_ref[...] = pre * _sigmoid(pre)
        xs_ref, b_ref, c_ref = (act_ref.at[:, pl.ds(0, AW)], act_ref.at[:, pl.ds(AW, NS)],
                                act_ref.at[:, pl.ds(AW + NS, NS)])
        s = _ssd_prep(dt_ref, bias_ref, alog_ref, dsk_ref, b_ref, c_ref, xs_ref, state, cst)
        Bm, Cm, prev = s["Bm"], s["Cm"], s["prev"]
        g = (s["cb"] * s["lam"]).astype(BF16)
        xdtb = s["xdt"].astype(BF16)
        prevb = prev.astype(BF16)
        y = _heads(lambda h: _nn(g[h], xdtb[h])) + _heads(lambda h: _nt(Cm, prevb[h])) * jnp.exp(s["cs_col"])
        y = y + s["dsk_col"] * s["x"]
        xf = (s["xdt"] * s["f"]).astype(BF16)
        new = prev * jnp.exp(s["cl"]) + _heads(lambda h: _tn(xf[h], Bm))
        for h in range(HEADS):
            y_ref[:, pl.ds(HD * h, HD)] = y[h]
            state[pl.ds(HD * h, HD), :] = new[h]
        z = z_ref[...]
        gi = y_ref[...] * (z * _sigmoid(z))
        cat_ref[:, pl.ds(0, AW)] = at_ref[...].astype(BF16)
        cat_ref[:, pl.ds(AW, AW)] = (gi * _rstd(gi) * gs_ref[...]).astype(BF16)

    vec = pl.BlockSpec((1, DT_PAD), lambda c: (0, 0))
    blk = pl.BlockSpec((R, AW), lambda c: (c, 0))
    return pl.pallas_call(
        body, name="ssd_fwd", grid=(nc // SSD_PER,),
        in_specs=[pl.BlockSpec((R, CONV_CH), lambda c: (c, 0)),
                  pl.BlockSpec((HALO, CONV_CH), lambda c: (jnp.maximum(c * per - 1, 0), 0)),
                  pl.BlockSpec((CONV_K, CONV_CH), lambda c: (0, 0)), pl.BlockSpec((1, CONV_CH), lambda c: (0, 0)),
                  pl.BlockSpec((R, DT_PAD), lambda c: (c, 6)),
                  vec, vec, vec, pl.BlockSpec((R, AW), lambda c: (c, 3)), blk, pl.BlockSpec((1, AW), lambda c: (0, 0))],
        out_specs=[blk, pl.BlockSpec((SSD_PER, AW, NS), lambda c: (c, 0, 0)), pl.BlockSpec((R, D), lambda c: (c, 0)),
                   pl.BlockSpec((R, CONV_CH), lambda c: (c, 0))],
        out_shape=[jax.ShapeDtypeStruct((T, AW), F32), jax.ShapeDtypeStruct((nc, AW, NS), F32),
                   jax.ShapeDtypeStruct((T, D), BF16), jax.ShapeDtypeStruct((T, CONV_CH), F32)],
        scratch_shapes=[pltpu.VMEM((AW, NS), F32), pltpu.VMEM((CHUNK, CHUNK), F32)],
        compiler_params=_cparams(("arbitrary",)),
    )(xbcdt, xbcdt, conv_w, conv_b, xbcdt, bias, alog, dsk, qkvz, attn, gs)


def _ssd_bwd(act, xbcdt, bias, alog, dsk, states, y_ssd, qkvz, dcat, gs):
    nc = T // CHUNK

    def body(xs_ref, b_ref, c_ref, dt_ref, bias_ref, alog_ref, dsk_ref, st_ref, y_ref, z_ref, dyn_ref, gs_ref,
             dact_ref, ddt_ref, par_ref, dz_ref, dgs_ref, dstate, cst, dy_ref):
        @pl.when(pl.program_id(0) == 0)
        def _():
            dstate[...] = jnp.zeros_like(dstate)
            par_ref[...] = jnp.zeros_like(par_ref)
            dgs_ref[...] = jnp.zeros_like(dgs_ref)
        for sub in reversed(range(SSD_PER)):
            rows = pl.ds(sub * CHUNK, CHUNK)
            one_chunk(xs_ref.at[rows, :], b_ref.at[rows, :], c_ref.at[rows, :], dt_ref.at[rows, :], bias_ref, alog_ref,
                      dsk_ref, st_ref.at[sub], y_ref.at[rows, :], z_ref.at[rows, :], dyn_ref.at[rows, :], gs_ref,
                      dact_ref.at[rows, :], ddt_ref.at[rows, :], par_ref, dz_ref.at[rows, :], dgs_ref, dstate, cst, dy_ref)

    def one_chunk(xs_ref, b_ref, c_ref, dt_ref, bias_ref, alog_ref, dsk_ref, st_ref, y_ref, z_ref, dyn_ref, gs_ref,
                  dact_ref, ddt_ref, par_ref, dz_ref, dgs_ref, dstate, cst, dy_ref):
        z, yv, dyn = z_ref[...], y_ref[...], dyn_ref[...]
        sg = _sigmoid(z)
        sz = z * sg
        gi = yv * sz
        rg = _rstd(gi)
        ng = gi * rg
        dgi = _rms_bwd(ng, rg, gs_ref[...], dyn)
        dy_ref[...] = dgi * sz
        dz_ref[...] = dgi * yv * (sg * (1.0 + z * (1.0 - sg)))
        dgs_ref[...] += _colsum(dyn * ng)
        s = _ssd_prep(dt_ref, bias_ref, alog_ref, dsk_ref, b_ref, c_ref, xs_ref, st_ref, cst)
        Bm, Cm, prev, lam, x, xdt, f, cl = s["Bm"], s["Cm"], s["prev"], s["lam"], s["x"], s["xdt"], s["f"], s["cl"]
        lane = lax.broadcasted_iota(jnp.int32, (1, DT_PAD), 1)
        row = lax.broadcasted_iota(jnp.int32, (1, CHUNK, 1), 1)
        g = s["cb"] * lam
        gb, xdtb, prevb = g.astype(BF16), xdt.astype(BF16), prev.astype(BF16)
        dy = _heads(lambda h: dy_ref[:, pl.ds(HD * h, HD)])
        dyb = dy.astype(BF16)
        dnew = _heads(lambda h: dstate[pl.ds(HD * h, HD), :])
        dnewb = dnew.astype(BF16)
        E = jnp.exp(s["cs_col"])
        ecl = jnp.exp(cl)
        dG = _heads(lambda h: _nt(dyb[h], xdtb[h]))
        dxdt = _heads(lambda h: _tn(gb[h], dyb[h]))
        Yo = _heads(lambda h: _nt(Cm, prevb[h]))
        W = _heads(lambda h: _nt(Bm, dnewb[h]))
        dcb = jnp.sum(dG * lam, axis=0)
        Mm = dG * g
        col_sums = jnp.sum(Mm, axis=1, keepdims=True)
        dYo = (dy * E).astype(BF16)
        dxdt = dxdt + W * f
        dF = jnp.sum(W * xdt, axis=2, keepdims=True) * f
        dcl = jnp.sum(dnew * prev, axis=(1, 2), keepdims=True) * ecl + jnp.sum(dF, axis=1, keepdims=True)
        dcs = (jnp.sum(Mm, axis=2, keepdims=True) + jnp.sum(dy * Yo, axis=2, keepdims=True) * E - dF
               + jnp.where(row == CHUNK - 1, dcl, 0.0))
        ddt_x = jnp.sum(dxdt * x, axis=2, keepdims=True)
        dD = jnp.sum(dy * x, axis=(1, 2), keepdims=True)
        dx = s["dsk_col"] * dy + dxdt * s["dt_col"]
        xfb = (xdt * f).astype(BF16)
        dprev = _heads(lambda h: _tn(dYo[h], Cm)) + dnew * ecl
        dcbb = dcb.astype(BF16)
        dC = _nn(dcbb, Bm)
        dB = _tn(dcbb, Cm)
        dcs_mat = -_rows_to_block([col_sums[h] for h in range(HEADS)], CHUNK, CHUNK).T
        ddt_mat = jnp.zeros((CHUNK, DT_PAD), F32)
        dD_row = jnp.zeros((1, DT_PAD), F32)
        for h in range(HEADS):
            sl = pl.ds(HD * h, HD)
            dC = dC + _nn(dYo[h], prevb[h])
            dB = dB + _nn(xfb[h], dnewb[h])
            dcs_mat = dcs_mat + jnp.where(lane == h, dcs[h], 0.0)
            ddt_mat = ddt_mat + jnp.where(lane == h, ddt_x[h], 0.0)
            dD_row = dD_row + jnp.where(lane == h, dD[h], 0.0)
            dact_ref[:, sl] = dx[h]
            dstate[sl, :] = dprev[h]
        dact_ref[:, pl.ds(AW, NS)] = dB
        dact_ref[:, pl.ds(AW + NS, NS)] = dC
        da = jnp.dot((s["li"] <= s["si"]).astype(F32), dcs_mat, precision=HIGHEST, preferred_element_type=F32)
        ddtp = jnp.where(lane < HEADS, (ddt_mat + da * s["A"]) * _sigmoid(s["dtp"]), 0.0)
        ddt_ref[...] = ddtp
        dalog = jnp.where(lane < HEADS, jnp.sum(da * s["dt"], axis=0, keepdims=True) * s["A"], 0.0)
        par_ref[...] += _rows_to_block([jnp.sum(ddtp, axis=0, keepdims=True), dalog, dD_row], 8, DT_PAD)

    vec = pl.BlockSpec((1, DT_PAD), lambda c: (0, 0))
    nstep = nc // SSD_PER
    rev = lambda c: nstep - 1 - c
    R = SSD_PER * CHUNK
    return pl.pallas_call(
        body, name="ssd_bwd", grid=(nstep,),
        in_specs=[pl.BlockSpec((R, AW), lambda c: (rev(c), 0)), pl.BlockSpec((R, NS), lambda c: (rev(c), 4)),
                  pl.BlockSpec((R, NS), lambda c: (rev(c), 5)), pl.BlockSpec((R, DT_PAD), lambda c: (rev(c), 6)),
                  vec, vec, vec,
                  pl.BlockSpec((SSD_PER, AW, NS), lambda c: (rev(c), 0, 0)), pl.BlockSpec((R, AW), lambda c: (rev(c), 0)),
                  pl.BlockSpec((R, AW), lambda c: (rev(c), 3)), pl.BlockSpec((R, AW), lambda c: (rev(c), 1)),
                  pl.BlockSpec((1, AW), lambda c: (0, 0))],
        out_specs=[pl.BlockSpec((R, CONV_CH), lambda c: (rev(c), 0)), pl.BlockSpec((R, DT_PAD), lambda c: (rev(c), 0)),
                   pl.BlockSpec((8, DT_PAD), lambda c: (0, 0)), pl.BlockSpec((R, AW), lambda c: (rev(c), 0)),
                   pl.BlockSpec((1, AW), lambda c: (0, 0))],
        out_shape=[jax.ShapeDtypeStruct((T, CONV_CH), F32), jax.ShapeDtypeStruct((T, DT_PAD), F32),
                   jax.ShapeDtypeStruct((8, DT_PAD), F32), jax.ShapeDtypeStruct((T, AW), F32),
                   jax.ShapeDtypeStruct((1, AW), F32)],
        scratch_shapes=[pltpu.VMEM((AW, NS), F32), pltpu.VMEM((CHUNK, CHUNK), F32), pltpu.VMEM((CHUNK, AW), F32)],
        compiler_params=_cparams(("arbitrary",)),
    )(act, act, act, xbcdt, bias, alog, dsk, states, y_ssd, qkvz, dcat, gs)


def _place():
    return lax.axis_index("x"), lax.axis_index("y"), lax.axis_index("c")


def _slot(px, py, pc):
    return 4 * px + 2 * py + pc


SLAB_ROWS = 24


def _slab_pack(parts, name):
    n = len(parts)

    def body(*refs):
        slab = refs[n]
        slab[...] = jnp.zeros_like(slab)
        for ref, (arr, row) in zip(refs[:n], parts):
            slab[pl.ds(row, arr.shape[0]), pl.ds(0, arr.shape[1])] = ref[...]

    vm = pl.BlockSpec(memory_space=pltpu.VMEM)
    return pl.pallas_call(
        body, name=name, in_specs=[vm] * n, out_specs=vm, out_shape=jax.ShapeDtypeStruct((SLAB_ROWS, D), F32),
    )(*[a for a, _ in parts])


_HBM = pl.BlockSpec(memory_space=pltpu.HBM)
_SEM = pl.BlockSpec(memory_space=pltpu.SEMAPHORE)
_EFFECT = pltpu.SideEffectType.DATAFLOW_SIDE_EFFECTING


def _peers(x, y, c):
    out = []
    for kk in range(1, N_DEV):
        fx, fy, fc = kk >> 2 & 1, kk >> 1 & 1, kk & 1
        out.append((1 - x if fx else x, 1 - y if fy else y, 1 - c if fc else c))
    return out


def _send_start(src, per_peer, name, dep):
    (handles, token) = _send_start_many([src], per_peer, name, dep)
    return handles, token


def _near_peers(x, y, c):
    return [(x, y, 1 - c), (1 - x, y, c), (x, 1 - y, c), (1 - x, 1 - y, c)]


def _send_start_many(srcs, per_peer, name, dep, peers=_peers, npeers=N_DEV - 1):
    n = len(srcs)

    def body(*refs):
        src_refs, land_refs = refs[:n], refs[n:2 * n]
        send_sems, recv_sems = refs[2 * n + 1], refs[2 * n + 2]
        token = refs[-1]
        x, y, c = _place()
        mine = _slot(x, y, c)
        for a in range(n):
            for kk, peer in enumerate(peers(x, y, c)):
                pltpu.make_async_remote_copy(
                    src_ref=src_refs[a].at[_slot(*peer)] if per_peer else src_refs[a], dst_ref=land_refs[a].at[mine],
                    send_sem=send_sems.at[a * npeers + kk], recv_sem=recv_sems.at[a * npeers + kk],
                    device_id=peer, device_id_type=MESH).start()
        token[...] = jnp.zeros_like(token)

    lands = [lax.empty((N_DEV,) + tuple(s.shape[1:] if per_peer else s.shape), s.dtype) for s in srcs]
    hbm = lambda t: pltpu.with_memory_space_constraint(t, pltpu.HBM)
    outs = pl.pallas_call(
        body, name=name,
        out_shape=(pltpu.SemaphoreType.DMA((n * npeers,)), pltpu.SemaphoreType.DMA((n * npeers,)),
                   *[pltpu.HBM(s.shape, s.dtype) for s in srcs], *[pltpu.HBM(l.shape, l.dtype) for l in lands],
                   jax.ShapeDtypeStruct((8, 128), F32)),
        in_specs=(*[_HBM] * (2 * n), _ANY),
        out_specs=(_SEM, _SEM, *[_HBM] * (2 * n), pl.BlockSpec(memory_space=pltpu.VMEM)),
        input_output_aliases={i: 2 + i for i in range(2 * n)},
        compiler_params=pltpu.CompilerParams(has_side_effects=_EFFECT),
    )(*[hbm(s) for s in srcs], *[hbm(l) for l in lands], dep)
    return (outs[0], outs[1], list(outs[2:2 + n]), list(outs[2 + n:2 + 2 * n])), outs[-1]


def _send_wait(handles, after, name):
    srcs, lands = _send_wait_many(handles, after, name)
    return srcs[0], lands[0]


def _send_wait_many(handles, after, name, npeers=N_DEV - 1):
    send_sems, recv_sems, src_thrus, land_thrus = handles
    n = len(src_thrus)

    def body(*refs):
        land_refs = refs[n:2 * n]
        send_sems, recv_sems = refs[2 * n], refs[2 * n + 1]
        me = _place()
        for a in range(n):
            for kk in range(npeers):
                cp = pltpu.make_async_remote_copy(
                    src_ref=land_refs[a].at[0], dst_ref=land_refs[a].at[0],
                    send_sem=send_sems.at[a * npeers + kk], recv_sem=recv_sems.at[a * npeers + kk],
                    device_id=me, device_id_type=MESH)
                cp.wait_send()
                cp.wait_recv()

    both = list(src_thrus) + list(land_thrus)
    outs = pl.pallas_call(
        body, name=name,
        out_shape=tuple(pltpu.HBM(t.shape, t.dtype) for t in both),
        in_specs=(*[_HBM] * (2 * n), _SEM, _SEM, _ANY), out_specs=tuple([_HBM] * (2 * n)),
        input_output_aliases={i: i for i in range(2 * n)},
        compiler_params=pltpu.CompilerParams(has_side_effects=_EFFECT),
    )(*both, send_sems, recv_sems, after)
    return list(outs[:n]), list(outs[n:])


def _forward_start(lands, name, dep):
    n = len(lands)

    def body(*refs):
        land_refs = refs[:n]
        send_sems, recv_sems = refs[n + 1], refs[n + 2]
        token = refs[-1]
        x, y, c = _place()
        for a in range(n):
            for j, chip in enumerate([(1 - x, y), (x, 1 - y), (1 - x, 1 - y)]):
                blk = land_refs[a].at[_slot(*chip, c)]
                pltpu.make_async_remote_copy(
                    src_ref=blk, dst_ref=blk, send_sem=send_sems.at[a * 3 + j], recv_sem=recv_sems.at[a * 3 + j],
                    device_id=(x, y, 1 - c), device_id_type=MESH).start()
        token[...] = jnp.zeros_like(token)

    outs = pl.pallas_call(
        body, name=name,
        out_shape=(pltpu.SemaphoreType.DMA((n * 3,)), pltpu.SemaphoreType.DMA((n * 3,)),
                   *[pltpu.HBM(l.shape, l.dtype) for l in lands], jax.ShapeDtypeStruct((8, 128), F32)),
        in_specs=(*[_HBM] * n, _ANY), out_specs=(_SEM, _SEM, *[_HBM] * n, pl.BlockSpec(memory_space=pltpu.VMEM)),
        input_output_aliases={i: 2 + i for i in range(n)},
        compiler_params=pltpu.CompilerParams(has_side_effects=_EFFECT),
    )(*lands, dep)
    return (outs[0], outs[1], list(outs[2:2 + n])), outs[-1]


def _forward_wait(handles, after, name):
    send_sems, recv_sems, land_thrus = handles
    n = len(land_thrus)

    def body(*refs):
        land_refs = refs[:n]
        send_sems, recv_sems = refs[n], refs[n + 1]
        me = _place()
        for a in range(n):
            for j in range(3):
                cp = pltpu.make_async_remote_copy(
                    src_ref=land_refs[a].at[0], dst_ref=land_refs[a].at[0],
                    send_sem=send_sems.at[a * 3 + j], recv_sem=recv_sems.at[a * 3 + j], device_id=me, device_id_type=MESH)
                cp.wait_send()
                cp.wait_recv()

    outs = pl.pallas_call(
        body, name=name,
        out_shape=tuple(pltpu.HBM(t.shape, t.dtype) for t in land_thrus),
        in_specs=(*[_HBM] * n, _SEM, _SEM, _ANY), out_specs=tuple([_HBM] * n),
        input_output_aliases={i: i for i in range(n)},
        compiler_params=pltpu.CompilerParams(has_side_effects=_EFFECT),
    )(*land_thrus, send_sems, recv_sems, after)
    return list(outs)


def _sum_slots(land, name):
    _, R, C = land.shape
    tm = R if R <= 512 else 512

    def body(x_ref, o_ref):
        acc = x_ref[0].astype(F32)
        for j in range(1, N_DEV):
            acc = acc + x_ref[j].astype(F32)
        o_ref[...] = acc

    return pl.pallas_call(
        body, name=name, grid=(R // tm,),
        in_specs=[pl.BlockSpec((N_DEV, tm, C), lambda i: (0, i, 0))], out_specs=pl.BlockSpec((tm, C), lambda i: (i, 0)),
        out_shape=jax.ShapeDtypeStruct((R, C), F32), compiler_params=_cparams(("parallel",)),
    )(land)


def _adam_math(w, g, m, v):
    m2 = ADAM_B1 * m + (1.0 - ADAM_B1) * g
    v2 = ADAM_B2 * v + (1.0 - ADAM_B2) * (g * g)
    m_hat = m2 / (1.0 - ADAM_B1 ** ADAM_STEP)
    v_hat = v2 / (1.0 - ADAM_B2 ** ADAM_STEP)
    delta = -ADAM_LR * (m_hat / (jnp.sqrt(v_hat) + ADAM_EPS) + ADAM_WD * w)
    return delta, m2, v2


def _adamw(w, g, m, v, name):
    R, C = w.shape
    tm = R if R <= 512 else 256
    return _rowwise(lambda w, g, m, v: (_adam_math(w, g, m, v), ()), [w, g, m, v], [], [(C, F32)] * 3, [], tm=tm, name=name)


def _adamw_small(slab, slab_rows, g_conv_w, ws, ms, vs):
    n = len(ws)

    def body(*refs):
        slab_ref, gc_ref = refs[0], refs[1]
        w_refs, m_refs, v_refs = refs[2:2 + n], refs[2 + n:2 + 2 * n], refs[2 + 2 * n:2 + 3 * n]
        outs = refs[2 + 3 * n:]
        loss_ref = outs[0]
        g_out, d_out, m_out, v_out = (outs[1 + i * n:1 + (i + 1) * n] for i in range(4))
        loss_ref[...] = jnp.sum(slab_ref[pl.ds(6, 1), :], axis=1, keepdims=True)
        for i in range(n):
            g = gc_ref[...] if i == n - 1 else slab_ref[pl.ds(slab_rows[i], 1), pl.ds(0, ws[i].shape[1])]
            d, m2, v2 = _adam_math(w_refs[i][...], g, m_refs[i][...], v_refs[i][...])
            g_out[i][...] = g
            d_out[i][...] = d
            m_out[i][...] = m2
            v_out[i][...] = v2

    vm = pl.BlockSpec(memory_space=pltpu.VMEM)
    shapes = [jax.ShapeDtypeStruct(w.shape, F32) for w in ws]
    outs = pl.pallas_call(
        body, name="adamw_small", in_specs=[vm] * (2 + 3 * n), out_specs=[vm] * (1 + 4 * n),
        out_shape=[jax.ShapeDtypeStruct((1, 1), F32)] + shapes * 4,
    )(slab, g_conv_w, *ws, *ms, *vs)
    return outs[0], outs[1:1 + n], outs[1 + n:1 + 2 * n], outs[1 + 2 * n:1 + 3 * n], outs[1 + 3 * n:]


SMALL = ["norm_mix_pre", "norm_mix_post", "norm_mlp_pre", "norm_mlp_post", "norm_ple_post",
         "conv_b", "ssd_norm_g", "dt_bias", "a_log", "d_skip"]


def _pad_row(v, width=D):
    return jnp.pad(v, ((0, 0), (0, width - v.shape[1])))


def kernel(x, p, positions, norm_mix_pre, norm_mix_post, w_in, conv_w, conv_b, dt_bias, a_log, d_skip, ssd_norm_g, w_out, norm_mlp_pre, norm_mlp_post, w_up, w_down, w_ple_gate, w_ple_proj, norm_ple_post, loss_target, m_norm_mix_pre, m_norm_mix_post, m_w_in, m_conv_w, m_conv_b, m_dt_bias, m_a_log, m_d_skip, m_ssd_norm_g, m_w_out, m_norm_mlp_pre, m_norm_mlp_post, m_w_up, m_w_down, m_w_ple_gate, m_w_ple_proj, m_norm_ple_post, v_norm_mix_pre, v_norm_mix_post, v_w_in, v_conv_w, v_conv_b, v_dt_bias, v_a_log, v_d_skip, v_ssd_norm_g, v_w_out, v_norm_mlp_pre, v_norm_mlp_post, v_w_up, v_w_down, v_w_ple_gate, v_w_ple_proj, v_norm_ple_post):
    args = dict(locals())
    x2, p2, tgt = x[0], p[0, 0], loss_target[0]
    g1, g2, g3, g4, g5 = norm_mix_pre, norm_mix_post, norm_mlp_pre, norm_mlp_post, norm_ple_post

    me = _slot(*_place())
    pack_in = jnp.pad(w_in[0].T, ((0, W_IN_SHARD_PAD - W_IN_SHARD), (0, 0))).astype(BF16)
    rest = [w_out[0].astype(BF16), w_up[0].T.astype(BF16), w_down[0].astype(BF16), w_ple_gate[0].astype(BF16),
            w_ple_proj[0].T.reshape(32, D).astype(BF16)]
    conv_pack = jnp.pad(conv_w[0], ((0, 4), (0, 32)))
    in_handles, tok_in0 = _send_start_many([pack_in, conv_pack], False, "gather_in_start", g1, peers=_near_peers, npeers=4)

    inv_freq = ROPE_THETA ** (-jnp.arange(HD // 2, dtype=F32) * 2.0 / HD)
    pos = positions[0] + tok_in0[0, 0].astype(jnp.int32)
    ang = pos.astype(F32)[:, None] * inv_freq
    cos, sin = jnp.cos(ang), jnp.sin(ang)
    cos128 = jnp.concatenate([cos, cos, cos, cos], axis=1)
    sin128 = jnp.concatenate([-sin, sin, -sin, sin], axis=1)

    bias_w, alog_w, dsk_w = _pad_row(dt_bias, DT_PAD), _pad_row(a_log, DT_PAD), _pad_row(d_skip, DT_PAD)

    (u1,) = _rowwise(lambda a, g: ((a * _rstd(a) * g,), ()), [x2], [g1], [(D, BF16)], [], tm=512, name="norm_x",
                     deps=[cos128, sin128])
    p2b = p2.astype(BF16)

    in_back, in_land = _send_wait_many(in_handles, u1, "gather_in_wait", npeers=4)
    fw_handles, tok_fw = _forward_start(in_land, "gather_in_forward", u1)
    in_land = _forward_wait(fw_handles, tok_fw, "gather_in_forward_wait")
    gin = lax.dynamic_update_slice(in_land[0], in_back[0][None], (me, 0, 0))
    gconv = lax.dynamic_update_slice(in_land[1], in_back[1][None], (me, 0, 0))
    rest_handles, tok_rest = _send_start_many(rest, False, "gather_rest_start", gconv)
    w_inT = gin[:, :W_IN_SHARD].reshape(IN_W, D)
    w_qkvzT = w_inT[:4 * AW]
    w_xbcdtT = jnp.pad(w_inT[4 * AW:], ((0, DT_PAD - HEADS), (0, 0)))
    conv_full = gconv[:, :CONV_K, :96].transpose(1, 0, 2).reshape(CONV_K, CONV_CH)
    qkvz = _mm(u1, w_qkvzT, tb=True, tm=512, tn=2048, tk=1024, name="proj_qkvz", deps=[tok_rest])
    xbcdt = _mm(u1, w_xbcdtT, tb=True, tm=512, tn=896, tk=1024, name="proj_xbcdt")

    qkv = _rope_fwd(qkvz, cos128, sin128)
    qkv = [qkv[3 * i:3 * i + 3] for i in range(len(DILATIONS))]
    outs, lses = [], []
    for d, (qd, kd, vd) in zip(DILATIONS, qkv):
        o, l = _attn_fwd(qd, kd, vd, d)
        outs.append(o)
        lses.append(l)
    attn, lse, attn4, lse4, attn16, lse16 = _attn_merge(outs, lses)

    y_ssd, states, cat, act = _ssd_fwd(xbcdt, conv_full, conv_b, bias_w, alog_w, dsk_w, qkvz, attn, ssd_norm_g)


    rest_back, landed = _send_wait_many(rest_handles, cat, "gather_rest_wait")
    landed = [lax.dynamic_update_slice(l, b[None], (me, 0, 0)) for l, b in zip(landed, rest_back)]
    w_o, w_upT, w_dn, w_gate = landed[0].reshape(D, D), landed[1].reshape(DFF, D), landed[2].reshape(DFF, D), landed[3].reshape(D, D)
    w_projT = landed[4].reshape(D, PLE)

    def post1(mm, xx, ga):
        h = xx + mm * _rstd(mm) * ga
        return (mm, h, _rstd(h)), ()
    mix, h1, r3 = _mm_rows(post1, [(cat, w_o, False)], [x2], [g2], [(D, F32), (D, F32), (1, F32)], [], tm=512,
                           name="mix_out")

    a_up, ff, u2, h2, h2b = _mlp_fwd(h1, r3, g3, w_upT, w_dn, g4)
    relu2 = lambda a: jnp.square(jnp.maximum(a.astype(F32), 0.0))

    def final(gpre, ppv, hh, tg, g):
        sg = _sigmoid(gpre)
        ple = ppv * sg
        r = _rstd(ple)
        n = ple * r
        h3 = hh + n * g
        e = h3 - tg
        dh3 = e * (1.0 / D)
        dple = _rms_bwd(n, r, g, dh3)
        return (dh3, dple * sg, dple * ppv * sg * (1.0 - sg)), (_colsum(dh3 * n), _colsum(0.5 * e * e * (1.0 / D)))
    dh3, dpp, dgp, dg5, loss_vec = _mm_rows(final, [(h2b, w_gate, False), (p2b, w_projT, True)], [h2, tgt], [g5],
                                            [(D, F32), (D, BF16), (D, BF16)], [(1, D), (1, D)], tm=512, name="ple_loss")

    gw_projT = _mm(dpp, p2b, ta=True, tm=512, tn=256, tk=T, out_dtypes=(BF16,), name="gw_ple_proj")
    gw_gate = _mm(h2b, dgp, ta=True, tm=512, tn=1024, tk=T, out_dtypes=(BF16,), name="gw_ple_gate")
    rs_ple, tok_ple = _send_start_many([gw_projT.reshape(N_DEV, 32, D), gw_gate.reshape(N_DEV, 128, D)], True,
                                       "rs_start_w_ple", g1)
    def bwd_mlp_post(dg_, d3, f, g):
        dh2 = d3 + dg_
        r = _rstd(f)
        n = f * r
        return (dh2, _rms_bwd(n, r, g, dh2)), (_colsum(dh2 * n),)
    dh2, dff, dg4 = _mm_rows(bwd_mlp_post, [(dgp, w_gate, True)], [dh3, ff], [g4], [(D, F32), (D, BF16)], [(1, D)],
                             tm=512, name="bwd_ple_gate", deps=[tok_ple])

    gw_dn = _mm(a_up, dff, ta=True, tm=512, tn=1024, tk=T, a_pre=relu2, out_dtypes=(BF16,), name="gw_mlp_down")
    rs_dn, tok_dn = _send_start(gw_dn.reshape(N_DEV, 512, D), True, "rs_start_w_down", g1)
    da_up, du2 = _mlp_dx(dff, a_up, w_upT, w_dn, tok_dn)
    gw_upT = _mm(da_up, u2, ta=True, tm=512, tn=1024, tk=T, out_dtypes=(BF16,), name="gw_mlp_up")
    rs_up, tok_up = _send_start(gw_upT.reshape(N_DEV, 512, D), True, "rs_start_w_up", g1)

    def bwd_mix_post(d2, du, hh, rr, mm, ga, gb):
        n3 = hh * rr
        dh1 = d2 + _rms_bwd(n3, rr, gb, du)
        r = _rstd(mm)
        n2 = mm * r
        return (dh1, _rms_bwd(n2, r, ga, dh1)), (_colsum(du * n3), _colsum(dh1 * n2))
    dh1, dmix, dg3, dg2 = _rowwise(bwd_mix_post, [dh2, du2, h1, r3, mix], [g2, g3], [(D, F32), (D, BF16)],
                                   [(1, D), (1, D)], tm=512, name="bwd_post_mix", deps=[tok_up])

    gw_o = _mm(cat, dmix, ta=True, tm=512, tn=1024, tk=T, out_dtypes=(BF16,), name="gw_out")
    rs_o, tok_o = _send_start(gw_o.reshape(N_DEV, 128, D), True, "rs_start_w_out", g1)
    dcat, dattn4, dattn16 = _dx_out(dmix, w_o, tok_o)

    dact, ddtw, ssd_par, dz, dgs = _ssd_bwd(act, xbcdt, bias_w, alog_w, dsk_w, states, y_ssd, qkvz, dcat, ssd_norm_g)
    dxbcdt, conv_par = _conv_bwd(xbcdt, dact, ddtw, conv_full, conv_b)

    qkv_grads = [_attn_bwd(*qkv[0], dcat, attn, lse, 1),
                 _attn_bwd(*qkv[1], dattn4, attn4, lse4, 4),
                 _attn_bwd(*qkv[2], dattn16, attn16, lse16, 16)]
    dqkvz = _rope_bwd(qkv_grads, dz, cos128, sin128)

    gw_qkvzT = _mm(dqkvz, u1, ta=True, tm=512, tn=1024, tk=T, out_dtypes=(BF16,), name="gw_qkvz")
    gw_xbcdtT = _mm(dxbcdt, u1, ta=True, tm=896, tn=1024, tk=T, out_dtypes=(BF16,), name="gw_xbcdt")
    gw_inT = jnp.concatenate([gw_qkvzT, gw_xbcdtT], axis=0)[:IN_W]
    gw_inT = jnp.pad(gw_inT.reshape(N_DEV, W_IN_SHARD, D), ((0, 0), (0, W_IN_SHARD_PAD - W_IN_SHARD), (0, 0)))
    rs_in, tok_in = _send_start(gw_inT, True, "rs_start_w_in", g1)

    def bwd_in(ua, ub, d1, xx, g):
        rr = _rstd(xx)
        n = xx * rr
        du = ua + ub
        return (d1 + _rms_bwd(n, rr, g, du),), (_colsum(du * n),)
    grad_x, dg1 = _mm_rows(bwd_in, [(dqkvz, w_qkvzT, False), (dxbcdt, w_xbcdtT, False)], [dh1, x2], [g1],
                           [(D, F32)], [(1, D)], tm=512, name="bwd_in_proj", deps=[tok_in])

    my_slab = _slab_pack([(dg1, 0), (dg2, 1), (dg3, 2), (dg4, 3), (dg5, 4), (dgs, 5), (loss_vec, 6),
                          (conv_par, 8), (ssd_par, 16)], "slab_pack")
    slab_handles, tok_slab = _send_start_many([my_slab], False, "slab_start", g1)

    def scatter_finish(handles, nm, after):
        part, land = _send_wait(handles, after, "rs_wait_" + nm)
        own = lax.dynamic_slice(part, (me, 0, 0), (1,) + part.shape[1:])
        return _sum_slots(lax.dynamic_update_slice(land, own, (me, 0, 0)), "rs_sum_" + nm)
    g_out = scatter_finish(rs_o, "w_out", tok_slab)
    g_upT = scatter_finish(rs_up, "w_up", tok_slab)
    g_dn = scatter_finish(rs_dn, "w_down", tok_slab)
    ple_parts, ple_lands = _send_wait_many(rs_ple, tok_slab, "rs_wait_w_ple")
    g_projT, g_gate = [
        _sum_slots(lax.dynamic_update_slice(land, lax.dynamic_slice(part, (me, 0, 0), (1,) + part.shape[1:]), (me, 0, 0)),
                   "rs_sum_" + nm) for part, land, nm in zip(ple_parts, ple_lands, ("w_proj", "w_gate"))]

    grads = {
        "w_out": g_out[None], "w_up": g_upT.T[None], "w_down": g_dn[None],
        "w_ple_gate": g_gate[None], "w_ple_proj": g_projT.reshape(128, PLE).T[None],
    }
    delta, new_m, new_v = {}, {}, {}
    for nme in ["w_out", "w_up", "w_down", "w_ple_gate", "w_ple_proj", "w_in"]:
        if nme == "w_in":
            g_inT = scatter_finish(rs_in, "w_in", delta["w_down"])
            grads["w_in"] = g_inT[:W_IN_SHARD].T[None]
        dl, mm_, vv_ = _adamw(args[nme][0], grads[nme][0], args["m_" + nme][0], args["v_" + nme][0], "adamw_" + nme)
        delta[nme], new_m[nme], new_v[nme] = dl[None], mm_[None], vv_[None]

    slab_back, slab_land = _send_wait_many(slab_handles, delta["w_in"], "slab_wait")
    slab = _sum_slots(lax.dynamic_update_slice(slab_land[0], slab_back[0][None], (me, 0, 0)), "slab_sum")
    g_conv_w = lax.dynamic_slice(slab[8:12, :CONV_CH], (0, me * 96), (CONV_K, 96))
    small_names = SMALL + ["conv_w"]
    small_rows = [0, 1, 2, 3, 4, 12, 5, 16, 17, 18, None]
    pick = lambda prefix: [args[prefix + nme] for nme in SMALL] + [args[prefix + "conv_w"][0]]
    loss11, g_s, d_s, m_s, v_s = _adamw_small(slab, small_rows, g_conv_w, pick(""), pick("m_"), pick("v_"))
    loss = loss11[0, 0]
    for i, nme in enumerate(small_names):
        lead = (lambda t: t[None]) if nme == "conv_w" else (lambda t: t)
        grads[nme], delta[nme], new_m[nme], new_v[nme] = lead(g_s[i]), lead(d_s[i]), lead(m_s[i]), lead(v_s[i])

    order = ["norm_mix_pre", "norm_mix_post", "w_in", "conv_w", "conv_b", "dt_bias", "a_log", "d_skip", "ssd_norm_g",
             "w_out", "norm_mlp_pre", "norm_mlp_post", "w_up", "w_down", "w_ple_gate", "w_ple_proj", "norm_ple_post"]
    return (loss, grad_x[None], *[grads[n] for n in order], *[delta[n] for n in order],
            *[new_m[n] for n in order], *[new_v[n] for n in order])
```

```python
import jax
import jax.numpy as jnp
from jax import lax
from jax.experimental import pallas as pl
from jax.experimental.pallas import tpu as pltpu

F32 = jnp.float32
BF16 = jnp.bfloat16
MESH = pl.DeviceIdType.MESH
HIGHEST = lax.Precision.HIGHEST

N_DEV = 8
T = 4096
D = 1024
HEADS = 8
HD = 64
AW = 512
NS = 128
CONV_K = 4
CONV_CH = 768
CHUNK = 128
SSD_PER = 2
DFF = 4096
PLE = 256
EPS = 1e-6
ROPE_THETA = 10000.0
DILATIONS = (1, 4, 16)
QBLK = 128
NEG = -1e30
IN_W = 2824
W_IN_SHARD = 353
W_IN_SHARD_PAD = 384
DT_PAD = 128

ADAM_LR, ADAM_B1, ADAM_B2, ADAM_EPS, ADAM_WD, ADAM_STEP = 0.001, 0.9, 0.999, 1e-08, 0.01, 10

VMEM_LIMIT = 56 * 1024 * 1024


_ANY = pl.BlockSpec(memory_space=pl.ANY)


def _cparams(sem=None):
    return pltpu.CompilerParams(dimension_semantics=sem, vmem_limit_bytes=VMEM_LIMIT)


def _dot(a, b, ca, cb, precision=None):
    return lax.dot_general(a, b, (((ca,), (cb,)), ((), ())), preferred_element_type=F32, precision=precision)


def _nn(a, b):
    return _dot(a, b, 1, 0)


def _nt(a, b):
    return _dot(a, b, 1, 1)


def _tn(a, b):
    return _dot(a, b, 0, 0)


def _sigmoid(x):
    return 1.0 / (1.0 + jnp.exp(-x))


def _softplus(x):
    return jnp.maximum(x, 0.0) + jnp.log(1.0 + jnp.exp(-jnp.abs(x)))


def _mm(a, b, *, ta=False, tb=False, tm, tn, tk, name,
        a_pre=None, a_rows=(), a_cols=(), b_pre=None, b_rows=(), b_cols=(),
        epi=None, epi_tiles=(), out_dtypes=(F32,), deps=()):
    if ta:
        K, M = a.shape
    else:
        M, K = a.shape
    if tb:
        N, K2 = b.shape
    else:
        K2, N = b.shape
    assert K == K2 and M % tm == 0 and N % tn == 0 and K % tk == 0, (name, a.shape, b.shape)
    nk = K // tk
    if ta:
        a_spec = pl.BlockSpec((tk, tm), lambda i, j, k: (k, i))
        a_row_specs = [pl.BlockSpec((tk, 1), lambda i, j, k: (k, 0)) for _ in a_rows]
        a_col_specs = [pl.BlockSpec((1, tm), lambda i, j, k: (0, i)) for _ in a_cols]
    else:
        a_spec = pl.BlockSpec((tm, tk), lambda i, j, k: (i, k))
        a_row_specs = [pl.BlockSpec((tm, 1), lambda i, j, k: (i, 0)) for _ in a_rows]
        a_col_specs = [pl.BlockSpec((1, tk), lambda i, j, k: (0, k)) for _ in a_cols]
    if tb:
        b_spec = pl.BlockSpec((tn, tk), lambda i, j, k: (j, k))
        b_row_specs = [pl.BlockSpec((tn, 1), lambda i, j, k: (j, 0)) for _ in b_rows]
        b_col_specs = [pl.BlockSpec((1, tk), lambda i, j, k: (0, k)) for _ in b_cols]
    else:
        b_spec = pl.BlockSpec((tk, tn), lambda i, j, k: (k, j))
        b_row_specs = [pl.BlockSpec((tk, 1), lambda i, j, k: (k, 0)) for _ in b_rows]
        b_col_specs = [pl.BlockSpec((1, tn), lambda i, j, k: (0, j)) for _ in b_cols]
    o_spec = pl.BlockSpec((tm, tn), lambda i, j, k: (i, j))
    na, nb, ne, no = len(a_rows) + len(a_cols), len(b_rows) + len(b_cols), len(epi_tiles), len(out_dtypes)

    def body(*refs):
        a_ref, b_ref = refs[0], refs[1]
        a_ex = refs[2:2 + na]
        b_ex = refs[2 + na:2 + na + nb]
        e_ex = refs[2 + na + nb:2 + na + nb + ne]
        first_out = 2 + na + nb + ne + len(deps)
        outs = refs[first_out:first_out + no]

        def finish(res):
            vals = epi(res, *[r[...] for r in e_ex]) if epi is not None else (res,)
            for o_ref, val in zip(outs, vals):
                o_ref[...] = val.astype(o_ref.dtype)

        at = a_ref[...]
        if a_pre is not None:
            at = a_pre(at, *[r[...] for r in a_ex])
        bt = b_ref[...]
        if b_pre is not None:
            bt = b_pre(bt, *[r[...] for r in b_ex])
        prod = _dot(at.astype(BF16), bt.astype(BF16), 0 if ta else 1, 1 if tb else 0)
        if nk == 1:
            finish(prod)
            return
        acc = refs[-1]
        k = pl.program_id(2)

        @pl.when(k == 0)
        def _():
            acc[...] = jnp.zeros_like(acc)
        acc[...] += prod

        @pl.when(k == nk - 1)
        def _():
            finish(acc[...])

    outs = pl.pallas_call(
        body, name=name,
        grid=(M // tm, N // tn, nk),
        in_specs=([a_spec, b_spec] + a_row_specs + a_col_specs + b_row_specs + b_col_specs + [o_spec] * ne
                  + [_ANY] * len(deps)),
        out_specs=[o_spec] * no,
        out_shape=[jax.ShapeDtypeStruct((M, N), dt) for dt in out_dtypes],
        scratch_shapes=[pltpu.VMEM((tm, tn), F32)] if nk > 1 else [],
        compiler_params=_cparams(("parallel", "parallel", "arbitrary")),
    )(a, b, *a_rows, *a_cols, *b_rows, *b_cols, *epi_tiles, *deps)
    return outs[0] if no == 1 else outs


MLP_TM = 1024
MLP_TC = 512


def _mlp_fwd(h, r, g, w_upT, w_dn, g_post):
    nc = DFF // MLP_TC

    def body(h_ref, r_ref, g_ref, wu_ref, wd_ref, gp_ref, a_ref, ff_ref, u_ref, ho_ref, hob_ref, acc, u_scr):
        c = pl.program_id(1)

        @pl.when(c == 0)
        def _():
            u = (h_ref[...] * r_ref[...] * g_ref[...]).astype(BF16)
            u_scr[...] = u
            u_ref[...] = u
            acc[...] = jnp.zeros_like(acc)
        a = _nt(u_scr[...], wu_ref[...])
        a_ref[...] = a.astype(BF16)
        acc[...] += _nn(jnp.square(jnp.maximum(a, 0.0)).astype(BF16), wd_ref[...])

        @pl.when(c == nc - 1)
        def _():
            f = acc[...]
            ff_ref[...] = f
            ho = h_ref[...] + f * _rstd(f) * gp_ref[...]
            ho_ref[...] = ho
            hob_ref[...] = ho.astype(BF16)

    row = pl.BlockSpec((MLP_TM, D), lambda i, c: (i, 0))
    wsp = pl.BlockSpec((MLP_TC, D), lambda i, c: (c, 0))
    vec = pl.BlockSpec((1, D), lambda i, c: (0, 0))
    return pl.pallas_call(
        body, name="mlp_fwd", grid=(T // MLP_TM, nc),
        in_specs=[row, pl.BlockSpec((MLP_TM, 1), lambda i, c: (i, 0)), vec, wsp, wsp, vec],
        out_specs=[pl.BlockSpec((MLP_TM, MLP_TC), lambda i, c: (i, c)), row, row, row, row],
        out_shape=[jax.ShapeDtypeStruct((T, DFF), BF16), jax.ShapeDtypeStruct((T, D), F32), jax.ShapeDtypeStruct((T, D), BF16),
                   jax.ShapeDtypeStruct((T, D), F32), jax.ShapeDtypeStruct((T, D), BF16)],
        scratch_shapes=[pltpu.VMEM((MLP_TM, D), F32), pltpu.VMEM((MLP_TM, D), BF16)],
        compiler_params=_cparams(("parallel", "arbitrary")),
    )(h, r, g, w_upT, w_dn, g_post)


def _mlp_dx(dff, a, w_upT, w_dn, dep):
    nc = DFF // MLP_TC

    def body(d_ref, a_ref, wu_ref, wd_ref, dep_ref, da_ref, du_ref, acc, d_scr):
        c = pl.program_id(1)

        @pl.when(c == 0)
        def _():
            d_scr[...] = d_ref[...].astype(BF16)
            acc[...] = jnp.zeros_like(acc)
        da = (_nt(d_scr[...], wd_ref[...]) * (2.0 * jnp.maximum(a_ref[...].astype(F32), 0.0))).astype(BF16)
        da_ref[...] = da
        acc[...] += _nn(da, wu_ref[...])

        @pl.when(c == nc - 1)
        def _():
            du_ref[...] = acc[...]

    row = pl.BlockSpec((MLP_TM, D), lambda i, c: (i, 0))
    wsp = pl.BlockSpec((MLP_TC, D), lambda i, c: (c, 0))
    chunk = pl.BlockSpec((MLP_TM, MLP_TC), lambda i, c: (i, c))
    return pl.pallas_call(
        body, name="mlp_dx", grid=(T // MLP_TM, nc),
        in_specs=[row, chunk, wsp, wsp, _ANY], out_specs=[chunk, row],
        out_shape=[jax.ShapeDtypeStruct((T, DFF), BF16), jax.ShapeDtypeStruct((T, D), F32)],
        scratch_shapes=[pltpu.VMEM((MLP_TM, D), F32), pltpu.VMEM((MLP_TM, D), BF16)],
        compiler_params=_cparams(("parallel", "arbitrary")),
    )(dff, a, w_upT, w_dn, dep)


def _rowwise(fn, rows, vecs, out_rows, out_sums, *, tm, name, deps=()):
    specs, arrs = [], []
    R = None
    for r in rows:
        if isinstance(r, tuple):
            arr, width, cb = r
            specs.append(pl.BlockSpec((tm, width), lambda i, cb=cb: (i, cb)))
        else:
            arr = r
            specs.append(pl.BlockSpec((tm, arr.shape[1]), lambda i: (i, 0)))
        R = arr.shape[0] if R is None else R
        assert arr.shape[0] == R, name
        arrs.append(arr)
    assert R % tm == 0, name
    for v in vecs:
        specs.append(pl.BlockSpec(v.shape, lambda i: (0, 0)))
        arrs.append(v)
    nr, nv, no, ns = len(rows), len(vecs), len(out_rows), len(out_sums)
    out_specs = [pl.BlockSpec((tm, w), lambda i: (i, 0)) for w, _ in out_rows]
    out_specs += [pl.BlockSpec(s, lambda i: (0, 0)) for s in out_sums]
    out_shape = [jax.ShapeDtypeStruct((R, w), dt) for w, dt in out_rows]
    out_shape += [jax.ShapeDtypeStruct(s, F32) for s in out_sums]

    nd = len(deps)

    def body(*refs):
        ins = [r[...] for r in refs[:nr + nv]]
        o_refs = refs[nr + nv + nd:nr + nv + nd + no]
        s_refs = refs[nr + nv + nd + no:]
        o_vals, s_vals = fn(*ins)
        for ref, val in zip(o_refs, o_vals):
            ref[...] = val.astype(ref.dtype)
        if ns:
            @pl.when(pl.program_id(0) == 0)
            def _():
                for ref in s_refs:
                    ref[...] = jnp.zeros_like(ref)
            for ref, val in zip(s_refs, s_vals):
                ref[...] += val

    outs = pl.pallas_call(
        body, name=name, grid=(R // tm,), in_specs=specs + [_ANY] * nd, out_specs=out_specs, out_shape=out_shape,
        compiler_params=_cparams(("arbitrary",) if ns else ("parallel",)),
    )(*arrs, *deps)
    return outs


def _mm_rows(fn, mats, rows, vecs, out_rows, out_sums, *, tm, name, deps=()):
    R = mats[0][0].shape[0]
    assert R % tm == 0, name
    specs, arrs = [], []
    for a, b, tb in mats:
        specs += [pl.BlockSpec((tm, a.shape[1]), lambda i: (i, 0)), pl.BlockSpec(b.shape, lambda i: (0, 0))]
        arrs += [a, b]
    for r in rows:
        specs.append(pl.BlockSpec((tm, r.shape[1]), lambda i: (i, 0)))
        arrs.append(r)
    for v in vecs:
        specs.append(pl.BlockSpec(v.shape, lambda i: (0, 0)))
        arrs.append(v)
    nm, nr, nv, nd, no, ns = len(mats), len(rows), len(vecs), len(deps), len(out_rows), len(out_sums)
    out_specs = [pl.BlockSpec((tm, w), lambda i: (i, 0)) for w, _ in out_rows]
    out_specs += [pl.BlockSpec(s, lambda i: (0, 0)) for s in out_sums]
    out_shape = [jax.ShapeDtypeStruct((R, w), dt) for w, dt in out_rows] + [jax.ShapeDtypeStruct(s, F32) for s in out_sums]

    def body(*refs):
        prods = [_dot(refs[2 * p][...].astype(BF16), refs[2 * p + 1][...].astype(BF16), 1, 1 if mats[p][2] else 0)
                 for p in range(nm)]
        ins = [r[...] for r in refs[2 * nm:2 * nm + nr + nv]]
        first_out = 2 * nm + nr + nv + nd
        o_refs, s_refs = refs[first_out:first_out + no], refs[first_out + no:]
        o_vals, s_vals = fn(*prods, *ins)
        for ref, val in zip(o_refs, o_vals):
            ref[...] = val.astype(ref.dtype)
        if ns:
            @pl.when(pl.program_id(0) == 0)
            def _():
                for ref in s_refs:
                    ref[...] = jnp.zeros_like(ref)
            for ref, val in zip(s_refs, s_vals):
                ref[...] += val

    return pl.pallas_call(
        body, name=name, grid=(R // tm,), in_specs=specs + [_ANY] * nd, out_specs=out_specs, out_shape=out_shape,
        compiler_params=_cparams(("arbitrary",) if ns else ("parallel",)),
    )(*arrs, *deps)


def _colsum(x):
    return jnp.sum(x, axis=0, keepdims=True)


def _rstd(x):
    return lax.rsqrt(jnp.mean(x * x, axis=-1, keepdims=True) + EPS)


def _rms_bwd(xn, r, g, dy):
    dn = dy * g
    return r * (dn - xn * jnp.mean(dn * xn, axis=-1, keepdims=True))


def _partner(t):
    lane = lax.broadcasted_iota(jnp.int32, t.shape, 1)
    up = pltpu.roll(t, 96, 1)
    down = pltpu.roll(t, 32, 1)
    return jnp.where((lane % 64) < 32, up, down)


SLABS = AW // 128


def _rows(r, n, d):
    return pl.ds(r, n, stride=d) if d > 1 else pl.ds(0, n)


def _undilate(src_ref, dst, d, tm):
    for r in range(d):
        for j in range(SLABS):
            dst[j][_rows(r, tm // d, d), :] = src_ref[:, pl.ds(r * AW + j * 128, 128)].astype(dst[j].dtype)


def _dilate(dst_ref, src, d, tm):
    for r in range(d):
        for j in range(SLABS):
            dst_ref[:, pl.ds(r * AW + j * 128, 128)] = src[j][_rows(r, tm // d, d), :].astype(dst_ref.dtype)


def _slab_scratch(n, tm):
    return [pltpu.VMEM((tm, 128), F32)] * (SLABS * n)


def _slab_groups(flat):
    return [flat[SLABS * i:SLABS * (i + 1)] for i in range(len(flat) // SLABS)]


def _slab_specs(tm, first):
    return [pl.BlockSpec((tm, 128), lambda i, j=j: (i, first + j)) for j in range(SLABS)]


def _dil_spec(tm, d):
    return pl.BlockSpec((tm // d, d * AW), lambda i: (i, 0))


ROPE_TM = 512


def _rope_fwd(qkvz, cos128, sin128):
    tm = ROPE_TM

    def body(*refs):
        q_refs, k_refs, v_refs = refs[0:4], refs[4:8], refs[8:12]
        c_ref, s_ref = refs[12], refs[13]
        outs = refs[14:23]
        qs, ks = _slab_groups(refs[23:])
        c, s = c_ref[...], s_ref[...]
        for j in range(SLABS):
            q, k = q_refs[j][...], k_refs[j][...]
            qs[j][...] = (q * c + _partner(q) * s) * (HD ** -0.5)
            ks[j][...] = k * c + _partner(k) * s
        for di, d in enumerate(DILATIONS):
            oq, ok, ov = outs[3 * di:3 * di + 3]
            for r in range(d):
                rows = _rows(r, tm // d, d)
                for j in range(SLABS):
                    cols = pl.ds(r * AW + j * 128, 128)
                    oq[:, cols] = qs[j][rows, :].astype(BF16)
                    ok[:, cols] = ks[j][rows, :].astype(BF16)
                    ov[:, cols] = v_refs[j][rows, :].astype(BF16)

    tab = pl.BlockSpec((tm, 128), lambda i: (i, 0))
    out_specs, out_shape = [], []
    for d in DILATIONS:
        out_specs += [_dil_spec(tm, d)] * 3
        out_shape += [jax.ShapeDtypeStruct((T // d, d * AW), BF16)] * 3
    return pl.pallas_call(
        body, name="rope_fwd", grid=(T // tm,),
        in_specs=_slab_specs(tm, 0) + _slab_specs(tm, 4) + _slab_specs(tm, 8) + [tab, tab],
        out_specs=out_specs, out_shape=out_shape, scratch_shapes=_slab_scratch(2, tm),
        compiler_params=_cparams(("parallel",)),
    )(*([qkvz] * 12), cos128, sin128)


def _rope_bwd(grads, dz, cos128, sin128):
    tm = ROPE_TM

    def body(*refs):
        g_refs = refs[0:9]
        dz_ref, c_ref, s_ref, o_ref = refs[9], refs[10], refs[11], refs[12]
        scr = _slab_groups(refs[13:])
        for di, d in enumerate(DILATIONS[1:]):
            for t in range(3):
                _undilate(g_refs[3 * (di + 1) + t], scr[3 * di + t], d, tm)
        c, s = c_ref[...], s_ref[...]
        for j in range(SLABS):
            cols = pl.ds(j * 128, 128)
            tot = [g_refs[t][:, cols] + scr[t][j][...] + scr[3 + t][j][...] for t in range(3)]
            dqr = tot[0] * (HD ** -0.5)
            o_ref[:, pl.ds(j * 128, 128)] = (dqr * c + _partner(dqr * s)).astype(BF16)
            o_ref[:, pl.ds(AW + j * 128, 128)] = (tot[1] * c + _partner(tot[1] * s)).astype(BF16)
            o_ref[:, pl.ds(2 * AW + j * 128, 128)] = tot[2].astype(BF16)
        o_ref[:, pl.ds(3 * AW, AW)] = dz_ref[...].astype(BF16)

    tab = pl.BlockSpec((tm, 128), lambda i: (i, 0))
    in_specs, args = [], []
    for d, g in zip(DILATIONS, grads):
        in_specs += [_dil_spec(tm, d)] * 3
        args += list(g)
    return pl.pallas_call(
        body, name="rope_bwd", grid=(T // tm,),
        in_specs=in_specs + [pl.BlockSpec((tm, AW), lambda i: (i, 0)), tab, tab],
        out_specs=pl.BlockSpec((tm, 4 * AW), lambda i: (i, 0)),
        out_shape=jax.ShapeDtypeStruct((T, 4 * AW), BF16),
        scratch_shapes=_slab_scratch(6, tm),
        compiler_params=_cparams(("parallel",)),
    )(*args, dz, cos128, sin128)


def _dx_out(dmix, w_o, dep):
    tm = ROPE_TM

    def body(a_ref, w_ref, dep_ref, dcat_ref, o4, o16, *slabs):
        prod = _nt(a_ref[...].astype(BF16), w_ref[...].astype(BF16))
        dcat_ref[...] = prod
        for j in range(SLABS):
            slabs[j][...] = prod[:, 128 * j:128 * (j + 1)]
        _dilate(o4, slabs, 4, tm)
        _dilate(o16, slabs, 16, tm)

    return pl.pallas_call(
        body, name="dx_out", grid=(T // tm,),
        in_specs=[pl.BlockSpec((tm, D), lambda i: (i, 0)), pl.BlockSpec((D, D), lambda i: (0, 0)), _ANY],
        out_specs=[pl.BlockSpec((tm, D), lambda i: (i, 0)), _dil_spec(tm, 4), _dil_spec(tm, 16)],
        out_shape=[jax.ShapeDtypeStruct((T, D), F32), jax.ShapeDtypeStruct((T // 4, 4 * AW), F32),
                   jax.ShapeDtypeStruct((T // 16, 16 * AW), F32)],
        scratch_shapes=_slab_scratch(1, tm), compiler_params=_cparams(("parallel",)),
    )(dmix, w_o, dep)


def _band_masks():
    qi = lax.broadcasted_iota(jnp.int32, (QBLK, QBLK), 0)
    kj = lax.broadcasted_iota(jnp.int32, (QBLK, QBLK), 1)
    return kj >= qi, kj <= qi


def _attn_fwd(q, k, v, d):
    L = q.shape[0]
    npair = L // (2 * QBLK)

    def body(q_ref, kp_ref, kc_ref, vp_ref, vc_ref, o_ref, l_ref):
        pair = pl.program_id(1)
        mask_p, mask_c = _band_masks()
        for sub in range(2):
            rows = pl.ds(sub * QBLK, QBLK)
            first = jnp.where(pair > 0, 0.0, NEG) if sub == 0 else 0.0
            bias = jnp.concatenate([jnp.where(mask_p, 0.0, NEG) + first, jnp.where(mask_c, 0.0, NEG)], axis=1)
            k_prev = (lambda sl: kp_ref[:, sl]) if sub == 0 else (lambda sl: kc_ref[pl.ds(0, QBLK), sl])
            v_prev = (lambda sl: vp_ref[:, sl]) if sub == 0 else (lambda sl: vc_ref[pl.ds(0, QBLK), sl])
            s = []
            for h in range(HEADS):
                sl = pl.ds(HD * h, HD)
                qh = q_ref[rows, sl]
                s.append(jnp.concatenate([_nt(qh, k_prev(sl)), _nt(qh, kc_ref[rows, sl])], axis=1))
            s = jnp.stack(s) + bias
            m = jnp.max(s, axis=2, keepdims=True)
            e = jnp.exp(s - m)
            den = jnp.sum(e, axis=2, keepdims=True)
            p = e.astype(BF16)
            inv = 1.0 / den
            lse = m + jnp.log(den)
            for h in range(HEADS):
                sl = pl.ds(HD * h, HD)
                o_ref[rows, sl] = (_nn(p[h, :, :QBLK], v_prev(sl)) + _nn(p[h, :, QBLK:], vc_ref[rows, sl])) * inv[h]
                l_ref[rows, sl] = jnp.broadcast_to(lse[h], (QBLK, HD))

    cur = pl.BlockSpec((2 * QBLK, AW), lambda r, n: (n, r))
    prev = pl.BlockSpec((QBLK, AW), lambda r, n: (jnp.maximum(2 * n - 1, 0), r))
    return pl.pallas_call(
        body, name=f"attn_fwd_d{d}", grid=(d, npair),
        in_specs=[cur, prev, cur, prev, cur], out_specs=[cur, cur],
        out_shape=[jax.ShapeDtypeStruct((L, d * AW), F32)] * 2,
        compiler_params=_cparams(("parallel", "parallel")),
    )(q, k, k, v, v)


def _attn_bwd(q, k, v, do, at, lse, d):
    L = q.shape[0]
    nb = L // QBLK
    npair = nb // 2

    def body(qc_ref, qn_ref, kp_ref, kc_ref, vp_ref, vc_ref, doc_ref, don_ref, atc_ref, atn_ref,
             lc_ref, ln_ref, dq_ref, dk_ref, dv_ref):
        pair = pl.program_id(1)
        mask_p, mask_c = _band_masks()
        prev_bias = jnp.where(mask_p, 0.0, NEG)
        for sub in range(2):
            rows = pl.ds(sub * QBLK, QBLK)
            second = pl.ds(QBLK, QBLK)
            if sub == 0:
                take = lambda cur_ref, nxt_ref, cols, i: cur_ref[rows if i == 0 else second, cols]
                prev_of = lambda p_ref, c_ref, cols: p_ref[:, cols]
                first, last = jnp.where(pair > 0, 0.0, NEG), 0.0
            else:
                take = lambda cur_ref, nxt_ref, cols, i: cur_ref[rows, cols] if i == 0 else nxt_ref[:, cols]
                prev_of = lambda p_ref, c_ref, cols: c_ref[pl.ds(0, QBLK), cols]
                first, last = 0.0, jnp.where(pair < npair - 1, 0.0, NEG)
            bias = jnp.concatenate([prev_bias + first, jnp.where(mask_c, 0.0, NEG), prev_bias + last], axis=1)
            s, dp, ls, dl, ops = [], [], [], [], []
            for h in range(HEADS):
                sl = pl.ds(HD * h, HD)
                one = pl.ds(HD * h, 1)
                q0, q1 = take(qc_ref, qn_ref, sl, 0), take(qc_ref, qn_ref, sl, 1)
                kp, kc = prev_of(kp_ref, kc_ref, sl), kc_ref[rows, sl]
                vp, vc = prev_of(vp_ref, vc_ref, sl), vc_ref[rows, sl]
                do0, do1 = take(doc_ref, don_ref, sl, 0), take(doc_ref, don_ref, sl, 1)
                do0b, do1b = do0.astype(BF16), do1.astype(BF16)
                s.append(jnp.concatenate([_nt(q0, kp), _nt(q0, kc), _nt(q1, kc)], axis=1))
                dp.append(jnp.concatenate([_nt(do0b, vp), _nt(do0b, vc), _nt(do1b, vc)], axis=1))
                dl0 = jnp.sum(do0 * take(atc_ref, atn_ref, sl, 0), axis=1, keepdims=True)
                dl1 = jnp.sum(do1 * take(atc_ref, atn_ref, sl, 1), axis=1, keepdims=True)
                dl.append(jnp.concatenate([jnp.broadcast_to(dl0, (QBLK, 2 * QBLK)), jnp.broadcast_to(dl1, (QBLK, QBLK))], axis=1))
                ls.append(jnp.concatenate([jnp.broadcast_to(take(lc_ref, ln_ref, one, 0), (QBLK, 2 * QBLK)),
                                           jnp.broadcast_to(take(lc_ref, ln_ref, one, 1), (QBLK, QBLK))], axis=1))
                ops.append((q0, q1, kp, kc, do0b, do1b))
            p = jnp.exp(jnp.stack(s) + bias - jnp.stack(ls))
            ds = (p * (jnp.stack(dp) - jnp.stack(dl))).astype(BF16)
            p = p.astype(BF16)
            for h in range(HEADS):
                sl = pl.ds(HD * h, HD)
                q0, q1, kp, kc, do0b, do1b = ops[h]
                dq_ref[rows, sl] = (_nn(ds[h, :, :QBLK], kp) + _nn(ds[h, :, QBLK:2 * QBLK], kc)).astype(BF16)
                dv_ref[rows, sl] = (_tn(p[h, :, QBLK:2 * QBLK], do0b) + _tn(p[h, :, 2 * QBLK:], do1b)).astype(BF16)
                dk_ref[rows, sl] = (_tn(ds[h, :, QBLK:2 * QBLK], q0) + _tn(ds[h, :, 2 * QBLK:], q1)).astype(BF16)

    cur = pl.BlockSpec((2 * QBLK, AW), lambda r, n: (n, r))
    prev = pl.BlockSpec((QBLK, AW), lambda r, n: (jnp.maximum(2 * n - 1, 0), r))
    nxt = pl.BlockSpec((QBLK, AW), lambda r, n: (jnp.minimum(2 * n + 2, nb - 1), r))
    return pl.pallas_call(
        body, name=f"attn_bwd_d{d}", grid=(d, npair),
        in_specs=[cur, nxt, prev, cur, prev, cur, cur, nxt, cur, nxt, cur, nxt], out_specs=[cur, cur, cur],
        out_shape=[jax.ShapeDtypeStruct((L, d * AW), BF16)] * 3,
        compiler_params=_cparams(("parallel", "parallel")),
    )(q, q, k, k, v, v, do, do, at, at, lse, lse)


def _attn_merge(outs, lses):
    tm = ROPE_TM

    def body(o1, o4, o16, l1, l4, l16, at_ref, ls_ref, at4, ls4, at16, ls16, *flat):
        so4, so16, sl4, sl16, sa, sl = _slab_groups(flat)
        _undilate(o4, so4, 4, tm)
        _undilate(o16, so16, 16, tm)
        _undilate(l4, sl4, 4, tm)
        _undilate(l16, sl16, 16, tm)
        for j in range(SLABS):
            cols = pl.ds(j * 128, 128)
            a, b, c = l1[:, cols], sl4[j][...], sl16[j][...]
            m = jnp.maximum(jnp.maximum(a, b), c)
            e1, e2, e3 = jnp.exp(a - m), jnp.exp(b - m), jnp.exp(c - m)
            s = e1 + e2 + e3
            inv = 1.0 / s
            attn = (e1 * inv) * o1[:, cols] + (e2 * inv) * so4[j][...] + (e3 * inv) * so16[j][...]
            lse = m + jnp.log(s)
            at_ref[:, cols] = attn
            ls_ref[:, cols] = lse
            sa[j][...] = attn
            sl[j][...] = lse
        _dilate(at4, sa, 4, tm)
        _dilate(at16, sa, 16, tm)
        _dilate(ls4, sl, 4, tm)
        _dilate(ls16, sl, 16, tm)

    specs = [_dil_spec(tm, d) for d in DILATIONS]
    tok = specs[0]
    return pl.pallas_call(
        body, name="attn_merge", grid=(T // tm,),
        in_specs=specs + specs, out_specs=[tok, tok, specs[1], specs[1], specs[2], specs[2]],
        out_shape=[jax.ShapeDtypeStruct((T, AW), F32)] * 2 + [jax.ShapeDtypeStruct((T // 4, 4 * AW), F32)] * 2
        + [jax.ShapeDtypeStruct((T // 16, 16 * AW), F32)] * 2,
        scratch_shapes=_slab_scratch(6, tm),
        compiler_params=_cparams(("parallel",)),
    )(*outs, *lses)


CONV_TM = 512
HALO = 8


def _conv_pre(ext, w, b):
    y = b + w[3] * ext
    for kk in range(1, CONV_K):
        y = y + w[3 - kk] * pltpu.roll(ext, kk, 0)
    return y


def _rows_to_block(rows, n, width):
    ri = lax.broadcasted_iota(jnp.int32, (n, width), 0)
    out = jnp.zeros((n, width), F32)
    for j, r in enumerate(rows):
        out = out + jnp.where(ri == j, r, 0.0)
    return out


def _conv_bwd(xbc, dact, ddt, w, b):
    nblk = T // CONV_TM
    per = CONV_TM // HALO

    def body(x_ref, xb_ref, xa_ref, g_ref, ga_ref, ddt_ref, w_ref, b_ref, dx_ref, dw_ref):
        i = pl.program_id(0)
        wv = [w_ref[pl.ds(j, 1), :] for j in range(CONV_K)]
        before = jnp.where(i > 0, xb_ref[...], 0.0)
        last = i == nblk - 1
        after = jnp.where(last, 0.0, xa_ref[...])
        g_after = jnp.where(last, 0.0, ga_ref[...])
        ext = jnp.concatenate([before, x_ref[...], after], axis=0)
        y = _conv_pre(ext, wv, b_ref[...])[HALO:]
        sg = _sigmoid(y)
        dy = jnp.concatenate([g_ref[...], g_after], axis=0) * (sg * (1.0 + y * (1.0 - sg)))
        n = CONV_TM + HALO
        dx = wv[3] * dy
        for kk in range(1, CONV_K):
            dx = dx + wv[3 - kk] * pltpu.roll(dy, n - kk, 0)
        dx_ref[:, pl.ds(0, CONV_CH)] = dx[:CONV_TM].astype(BF16)
        dx_ref[:, pl.ds(CONV_CH, DT_PAD)] = ddt_ref[...].astype(BF16)
        dyc = dy[:CONV_TM]
        rows = [jnp.sum(dyc * (pltpu.roll(ext, 3 - j, 0) if j < 3 else ext)[HALO:HALO + CONV_TM], axis=0, keepdims=True)
                for j in range(CONV_K)]
        rows.append(jnp.sum(dyc, axis=0, keepdims=True))
        part = _rows_to_block(rows, 8, CONV_CH)

        @pl.when(i == 0)
        def _():
            dw_ref[...] = jnp.zeros_like(dw_ref)
        dw_ref[...] += part

    blk = pl.BlockSpec((CONV_TM, CONV_CH), lambda i: (i, 0))
    hb = pl.BlockSpec((HALO, CONV_CH), lambda i: (jnp.maximum(i * per - 1, 0), 0))
    ha = pl.BlockSpec((HALO, CONV_CH), lambda i: (jnp.minimum((i + 1) * per, T // HALO - 1), 0))
    return pl.pallas_call(
        body, name="conv_bwd", grid=(nblk,),
        in_specs=[blk, hb, ha, blk, ha, pl.BlockSpec((CONV_TM, DT_PAD), lambda i: (i, 0)),
                  pl.BlockSpec((CONV_K, CONV_CH), lambda i: (0, 0)), pl.BlockSpec((1, CONV_CH), lambda i: (0, 0))],
        out_specs=[pl.BlockSpec((CONV_TM, CONV_CH + DT_PAD), lambda i: (i, 0)), pl.BlockSpec((8, CONV_CH), lambda i: (0, 0))],
        out_shape=[jax.ShapeDtypeStruct((T, CONV_CH + DT_PAD), BF16), jax.ShapeDtypeStruct((8, CONV_CH), F32)],
        compiler_params=_cparams(("arbitrary",)),
    )(xbc, xbc, xbc, dact, dact, ddt, w, b)


def _pick(mat, h):
    lane = lax.broadcasted_iota(jnp.int32, mat.shape, 1)
    return jnp.sum(jnp.where(lane == h, mat, 0.0), axis=1, keepdims=True)


def _heads(fn):
    return jnp.stack([fn(h) for h in range(HEADS)])


def _ssd_prep(dt_ref, bias_ref, alog_ref, dsk_ref, b_ref, c_ref, xs_ref, state_ref, cst):
    li = lax.broadcasted_iota(jnp.int32, (CHUNK, CHUNK), 0)
    si = lax.broadcasted_iota(jnp.int32, (CHUNK, CHUNK), 1)
    tri = li >= si
    dtp = dt_ref[...] + bias_ref[...]
    dt = _softplus(dtp)
    A = -jnp.exp(alog_ref[...])
    a = dt * A
    cs = jnp.dot(tri.astype(F32), a, precision=HIGHEST, preferred_element_type=F32)
    cst[...] = cs.T
    Bm = b_ref[...].astype(BF16)
    Cm = c_ref[...].astype(BF16)
    cb = _nt(Cm, Bm)
    dskv = dsk_ref[...]
    cs_col = _heads(lambda h: _pick(cs, h))
    cs_row = _heads(lambda h: cst[pl.ds(h, 1), :])
    dt_col = _heads(lambda h: _pick(dt, h))
    dsk_col = _heads(lambda h: _pick(dskv, h))
    lam = jnp.exp(jnp.where(tri, cs_col - cs_row, NEG))
    x = _heads(lambda h: xs_ref[:, pl.ds(HD * h, HD)])
    xdt = x * dt_col
    prev = _heads(lambda h: state_ref[pl.ds(HD * h, HD), :])
    lane = lax.broadcasted_iota(jnp.int32, (1, 1, CHUNK), 2)
    cl = jnp.sum(jnp.where(lane == CHUNK - 1, cs_row, 0.0), axis=2, keepdims=True)
    f = jnp.exp(cl - cs_col)
    return dict(li=li, si=si, dtp=dtp, dt=dt, A=A, Bm=Bm, Cm=Cm, cb=cb, cs_col=cs_col, dt_col=dt_col, dsk_col=dsk_col,
                lam=lam, x=x, xdt=xdt, prev=prev, cl=cl, f=f)


def _ssd_fwd(xbcdt, conv_w, conv_b, bias, alog, dsk, qkvz, attn, gs):
    nc = T // CHUNK
    R = SSD_PER * CHUNK
    per = R // HALO

    def body(xbc_ref, halo_ref, cw_ref, cb_ref, dt_ref, bias_ref, alog_ref, dsk_ref, z_ref, at_ref, gs_ref,
             y_ref, st_ref, cat_ref, act_ref, state, cst):
        @pl.when(pl.program_id(0) == 0)
        def _():
            state[...] = jnp.zeros_like(state)
        for sub in range(SSD_PER):
            rows = pl.ds(sub * CHUNK, CHUNK)
            st_ref[sub] = state[...]
            halo = (jnp.where(pl.program_id(0) > 0, halo_ref[...], 0.0) if sub == 0
                    else xbc_ref[pl.ds(sub * CHUNK - HALO, HALO), :])
            one_chunk(halo, xbc_ref.at[rows, :], cw_ref, cb_ref, dt_ref.at[rows, :], bias_ref, alog_ref, dsk_ref,
                      z_ref.at[rows, :], at_ref.at[rows, :], gs_ref, y_ref.at[rows, :], cat_ref.at[rows, :],
                      act_ref.at[rows, :], state, cst)

    def one_chunk(halo, xbc_ref, cw_ref, cb_ref, dt_ref, bias_ref, alog_ref, dsk_ref, z_ref, at_ref, gs_ref,
                  y_ref, cat_ref, act_ref, state, cst):
        pre = _conv_pre(jnp.concatenate([halo, xbc_ref[...]], axis=0),
                        [cw_ref[pl.ds(j, 1), :] for j in range(CONV_K)], cb_ref[...])[HALO:]
        act_ref[...] = pre * _sigmoid(pre)
        xs_ref, b_ref, c_ref = (act_ref.at[:, pl.ds(0, AW)], act_ref.at[:, pl.ds(AW, NS)],
                                act_ref.at[:, pl.ds(AW + NS, NS)])
        s = _ssd_prep(dt_ref, bias_ref, alog_ref, dsk_ref, b_ref, c_ref, xs_ref, state, cst)
        Bm, Cm, prev = s["Bm"], s["Cm"], s["prev"]
        g = (s["cb"] * s["lam"]).astype(BF16)
        xdtb = s["xdt"].astype(BF16)
        prevb = prev.astype(BF16)
        y = _heads(lambda h: _nn(g[h], xdtb[h])) + _heads(lambda h: _nt(Cm, prevb[h])) * jnp.exp(s["cs_col"])
        y = y + s["dsk_col"] * s["x"]
        xf = (s["xdt"] * s["f"]).astype(BF16)
        new = prev * jnp.exp(s["cl"]) + _heads(lambda h: _tn(xf[h], Bm))
        for h in range(HEADS):
            y_ref[:, pl.ds(HD * h, HD)] = y[h]
            state[pl.ds(HD * h, HD), :] = new[h]
        z = z_ref[...]
        gi = y_ref[...] * (z * _sigmoid(z))
        cat_ref[:, pl.ds(0, AW)] = at_ref[...].astype(BF16)
        cat_ref[:, pl.ds(AW, AW)] = (gi * _rstd(gi) * gs_ref[...]).astype(BF16)

    vec = pl.BlockSpec((1, DT_PAD), lambda c: (0, 0))
    blk = pl.BlockSpec((R, AW), lambda c: (c, 0))
    return pl.pallas_call(
        body, name="ssd_fwd", grid=(nc // SSD_PER,),
        in_specs=[pl.BlockSpec((R, CONV_CH), lambda c: (c, 0)),
                  pl.BlockSpec((HALO, CONV_CH), lambda c: (jnp.maximum(c * per - 1, 0), 0)),
                  pl.BlockSpec((CONV_K, CONV_CH), lambda c: (0, 0)), pl.BlockSpec((1, CONV_CH), lambda c: (0, 0)),
                  pl.BlockSpec((R, DT_PAD), lambda c: (c, 6)),
                  vec, vec, vec, pl.BlockSpec((R, AW), lambda c: (c, 3)), blk, pl.BlockSpec((1, AW), lambda c: (0, 0))],
        out_specs=[blk, pl.BlockSpec((SSD_PER, AW, NS), lambda c: (c, 0, 0)), pl.BlockSpec((R, D), lambda c: (c, 0)),
                   pl.BlockSpec((R, CONV_CH), lambda c: (c, 0))],
        out_shape=[jax.ShapeDtypeStruct((T, AW), F32), jax.ShapeDtypeStruct((nc, AW, NS), F32),
                   jax.ShapeDtypeStruct((T, D), BF16), jax.ShapeDtypeStruct((T, CONV_CH), F32)],
        scratch_shapes=[pltpu.VMEM((AW, NS), F32), pltpu.VMEM((CHUNK, CHUNK), F32)],
        compiler_params=_cparams(("arbitrary",)),
    )(xbcdt, xbcdt, conv_w, conv_b, xbcdt, bias, alog, dsk, qkvz, attn, gs)


def _ssd_bwd(act, xbcdt, bias, alog, dsk, states, y_ssd, qkvz, dcat, gs):
    nc = T // CHUNK

    def body(xs_ref, b_ref, c_ref, dt_ref, bias_ref, alog_ref, dsk_ref, st_ref, y_ref, z_ref, dyn_ref, gs_ref,
             dact_ref, ddt_ref, par_ref, dz_ref, dgs_ref, dstate, cst, dy_ref):
        @pl.when(pl.program_id(0) == 0)
        def _():
            dstate[...] = jnp.zeros_like(dstate)
            par_ref[...] = jnp.zeros_like(par_ref)
            dgs_ref[...] = jnp.zeros_like(dgs_ref)
        for sub in reversed(range(SSD_PER)):
            rows = pl.ds(sub * CHUNK, CHUNK)
            one_chunk(xs_ref.at[rows, :], b_ref.at[rows, :], c_ref.at[rows, :], dt_ref.at[rows, :], bias_ref, alog_ref,
                      dsk_ref, st_ref.at[sub], y_ref.at[rows, :], z_ref.at[rows, :], dyn_ref.at[rows, :], gs_ref,
                      dact_ref.at[rows, :], ddt_ref.at[rows, :], par_ref, dz_ref.at[rows, :], dgs_ref, dstate, cst, dy_ref)

    def one_chunk(xs_ref, b_ref, c_ref, dt_ref, bias_ref, alog_ref, dsk_ref, st_ref, y_ref, z_ref, dyn_ref, gs_ref,
                  dact_ref, ddt_ref, par_ref, dz_ref, dgs_ref, dstate, cst, dy_ref):
        z, yv, dyn = z_ref[...], y_ref[...], dyn_ref[...]
        sg = _sigmoid(z)
        sz = z * sg
        gi = yv * sz
        rg = _rstd(gi)
        ng = gi * rg
        dgi = _rms_bwd(ng, rg, gs_ref[...], dyn)
        dy_ref[...] = dgi * sz
        dz_ref[...] = dgi * yv * (sg * (1.0 + z * (1.0 - sg)))
        dgs_ref[...] += _colsum(dyn * ng)
        s = _ssd_prep(dt_ref, bias_ref, alog_ref, dsk_ref, b_ref, c_ref, xs_ref, st_ref, cst)
        Bm, Cm, prev, lam, x, xdt, f, cl = s["Bm"], s["Cm"], s["prev"], s["lam"], s["x"], s["xdt"], s["f"], s["cl"]
        lane = lax.broadcasted_iota(jnp.int32, (1, DT_PAD), 1)
        row = lax.broadcasted_iota(jnp.int32, (1, CHUNK, 1), 1)
        g = s["cb"] * lam
        gb, xdtb, prevb = g.astype(BF16), xdt.astype(BF16), prev.astype(BF16)
        dy = _heads(lambda h: dy_ref[:, pl.ds(HD * h, HD)])
        dyb = dy.astype(BF16)
        dnew = _heads(lambda h: dstate[pl.ds(HD * h, HD), :])
        dnewb = dnew.astype(BF16)
        E = jnp.exp(s["cs_col"])
        ecl = jnp.exp(cl)
        dG = _heads(lambda h: _nt(dyb[h], xdtb[h]))
        dxdt = _heads(lambda h: _tn(gb[h], dyb[h]))
        Yo = _heads(lambda h: _nt(Cm, prevb[h]))
        W = _heads(lambda h: _nt(Bm, dnewb[h]))
        dcb = jnp.sum(dG * lam, axis=0)
        Mm = dG * g
        col_sums = jnp.sum(Mm, axis=1, keepdims=True)
        dYo = (dy * E).astype(BF16)
        dxdt = dxdt + W * f
        dF = jnp.sum(W * xdt, axis=2, keepdims=True) * f
        dcl = jnp.sum(dnew * prev, axis=(1, 2), keepdims=True) * ecl + jnp.sum(dF, axis=1, keepdims=True)
        dcs = (jnp.sum(Mm, axis=2, keepdims=True) + jnp.sum(dy * Yo, axis=2, keepdims=True) * E - dF
               + jnp.where(row == CHUNK - 1, dcl, 0.0))
        ddt_x = jnp.sum(dxdt * x, axis=2, keepdims=True)
        dD = jnp.sum(dy * x, axis=(1, 2), keepdims=True)
        dx = s["dsk_col"] * dy + dxdt * s["dt_col"]
        xfb = (xdt * f).astype(BF16)
        dprev = _heads(lambda h: _tn(dYo[h], Cm)) + dnew * ecl
        dcbb = dcb.astype(BF16)
        dC = _nn(dcbb, Bm)
        dB = _tn(dcbb, Cm)
        dcs_mat = -_rows_to_block([col_sums[h] for h in range(HEADS)], CHUNK, CHUNK).T
        ddt_mat = jnp.zeros((CHUNK, DT_PAD), F32)
        dD_row = jnp.zeros((1, DT_PAD), F32)
        for h in range(HEADS):
            sl = pl.ds(HD * h, HD)
            dC = dC + _nn(dYo[h], prevb[h])
            dB = dB + _nn(xfb[h], dnewb[h])
            dcs_mat = dcs_mat + jnp.where(lane == h, dcs[h], 0.0)
            ddt_mat = ddt_mat + jnp.where(lane == h, ddt_x[h], 0.0)
            dD_row = dD_row + jnp.where(lane == h, dD[h], 0.0)
            dact_ref[:, sl] = dx[h]
            dstate[sl, :] = dprev[h]
        dact_ref[:, pl.ds(AW, NS)] = dB
        dact_ref[:, pl.ds(AW + NS, NS)] = dC
        da = jnp.dot((s["li"] <= s["si"]).astype(F32), dcs_mat, precision=HIGHEST, preferred_element_type=F32)
        ddtp = jnp.where(lane < HEADS, (ddt_mat + da * s["A"]) * _sigmoid(s["dtp"]), 0.0)
        ddt_ref[...] = ddtp
        dalog = jnp.where(lane < HEADS, jnp.sum(da * s["dt"], axis=0, keepdims=True) * s["A"], 0.0)
        par_ref[...] += _rows_to_block([jnp.sum(ddtp, axis=0, keepdims=True), dalog, dD_row], 8, DT_PAD)

    vec = pl.BlockSpec((1, DT_PAD), lambda c: (0, 0))
    nstep = nc // SSD_PER
    rev = lambda c: nstep - 1 - c
    R = SSD_PER * CHUNK
    return pl.pallas_call(
        body, name="ssd_bwd", grid=(nstep,),
        in_specs=[pl.BlockSpec((R, AW), lambda c: (rev(c), 0)), pl.BlockSpec((R, NS), lambda c: (rev(c), 4)),
                  pl.BlockSpec((R, NS), lambda c: (rev(c), 5)), pl.BlockSpec((R, DT_PAD), lambda c: (rev(c), 6)),
                  vec, vec, vec,
                  pl.BlockSpec((SSD_PER, AW, NS), lambda c: (rev(c), 0, 0)), pl.BlockSpec((R, AW), lambda c: (rev(c), 0)),
                  pl.BlockSpec((R, AW), lambda c: (rev(c), 3)), pl.BlockSpec((R, AW), lambda c: (rev(c), 1)),
                  pl.BlockSpec((1, AW), lambda c: (0, 0))],
        out_specs=[pl.BlockSpec((R, CONV_CH), lambda c: (rev(c), 0)), pl.BlockSpec((R, DT_PAD), lambda c: (rev(c), 0)),
                   pl.BlockSpec((8, DT_PAD), lambda c: (0, 0)), pl.BlockSpec((R, AW), lambda c: (rev(c), 0)),
                   pl.BlockSpec((1, AW), lambda c: (0, 0))],
        out_shape=[jax.ShapeDtypeStruct((T, CONV_CH), F32), jax.ShapeDtypeStruct((T, DT_PAD), F32),
                   jax.ShapeDtypeStruct((8, DT_PAD), F32), jax.ShapeDtypeStruct((T, AW), F32),
                   jax.ShapeDtypeStruct((1, AW), F32)],
        scratch_shapes=[pltpu.VMEM((AW, NS), F32), pltpu.VMEM((CHUNK, CHUNK), F32), pltpu.VMEM((CHUNK, AW), F32)],
        compiler_params=_cparams(("arbitrary",)),
    )(act, act, act, xbcdt, bias, alog, dsk, states, y_ssd, qkvz, dcat, gs)


def _place():
    return lax.axis_index("x"), lax.axis_index("y"), lax.axis_index("c")


def _slot(px, py, pc):
    return 4 * px + 2 * py + pc


SLAB_ROWS = 24


def _slab_pack(parts, name):
    n = len(parts)

    def body(*refs):
        slab = refs[n]
        slab[...] = jnp.zeros_like(slab)
        for ref, (arr, row) in zip(refs[:n], parts):
            slab[pl.ds(row, arr.shape[0]), pl.ds(0, arr.shape[1])] = ref[...]

    vm = pl.BlockSpec(memory_space=pltpu.VMEM)
    return pl.pallas_call(
        body, name=name, in_specs=[vm] * n, out_specs=vm, out_shape=jax.ShapeDtypeStruct((SLAB_ROWS, D), F32),
    )(*[a for a, _ in parts])


_HBM = pl.BlockSpec(memory_space=pltpu.HBM)
_SEM = pl.BlockSpec(memory_space=pltpu.SEMAPHORE)
_EFFECT = pltpu.SideEffectType.DATAFLOW_SIDE_EFFECTING


def _peers(x, y, c):
    out = []
    for kk in range(1, N_DEV):
        fx, fy, fc = kk >> 2 & 1, kk >> 1 & 1, kk & 1
        out.append((1 - x if fx else x, 1 - y if fy else y, 1 - c if fc else c))
    return out


def _send_start(src, per_peer, name, dep):
    (handles, token) = _send_start_many([src], per_peer, name, dep)
    return handles, token


def _near_peers(x, y, c):
    return [(x, y, 1 - c), (1 - x, y, c), (x, 1 - y, c), (1 - x, 1 - y, c)]


def _send_start_many(srcs, per_peer, name, dep, peers=_peers, npeers=N_DEV - 1):
    n = len(srcs)

    def body(*refs):
        src_refs, land_refs = refs[:n], refs[n:2 * n]
        send_sems, recv_sems = refs[2 * n + 1], refs[2 * n + 2]
        token = refs[-1]
        x, y, c = _place()
        mine = _slot(x, y, c)
        for a in range(n):
            for kk, peer in enumerate(peers(x, y, c)):
                pltpu.make_async_remote_copy(
                    src_ref=src_refs[a].at[_slot(*peer)] if per_peer else src_refs[a], dst_ref=land_refs[a].at[mine],
                    send_sem=send_sems.at[a * npeers + kk], recv_sem=recv_sems.at[a * npeers + kk],
                    device_id=peer, device_id_type=MESH).start()
        token[...] = jnp.zeros_like(token)

    lands = [lax.empty((N_DEV,) + tuple(s.shape[1:] if per_peer else s.shape), s.dtype) for s in srcs]
    hbm = lambda t: pltpu.with_memory_space_constraint(t, pltpu.HBM)
    outs = pl.pallas_call(
        body, name=name,
        out_shape=(pltpu.SemaphoreType.DMA((n * npeers,)), pltpu.SemaphoreType.DMA((n * npeers,)),
                   *[pltpu.HBM(s.shape, s.dtype) for s in srcs], *[pltpu.HBM(l.shape, l.dtype) for l in lands],
                   jax.ShapeDtypeStruct((8, 128), F32)),
        in_specs=(*[_HBM] * (2 * n), _ANY),
        out_specs=(_SEM, _SEM, *[_HBM] * (2 * n), pl.BlockSpec(memory_space=pltpu.VMEM)),
        input_output_aliases={i: 2 + i for i in range(2 * n)},
        compiler_params=pltpu.CompilerParams(has_side_effects=_EFFECT),
    )(*[hbm(s) for s in srcs], *[hbm(l) for l in lands], dep)
    return (outs[0], outs[1], list(outs[2:2 + n]), list(outs[2 + n:2 + 2 * n])), outs[-1]


def _send_wait(handles, after, name):
    srcs, lands = _send_wait_many(handles, after, name)
    return srcs[0], lands[0]


def _send_wait_many(handles, after, name, npeers=N_DEV - 1):
    send_sems, recv_sems, src_thrus, land_thrus = handles
    n = len(src_thrus)

    def body(*refs):
        land_refs = refs[n:2 * n]
        send_sems, recv_sems = refs[2 * n], refs[2 * n + 1]
        me = _place()
        for a in range(n):
            for kk in range(npeers):
                cp = pltpu.make_async_remote_copy(
                    src_ref=land_refs[a].at[0], dst_ref=land_refs[a].at[0],
                    send_sem=send_sems.at[a * npeers + kk], recv_sem=recv_sems.at[a * npeers + kk],
                    device_id=me, device_id_type=MESH)
                cp.wait_send()
                cp.wait_recv()

    both = list(src_thrus) + list(land_thrus)
    outs = pl.pallas_call(
        body, name=name,
        out_shape=tuple(pltpu.HBM(t.shape, t.dtype) for t in both),
        in_specs=(*[_HBM] * (2 * n), _SEM, _SEM, _ANY), out_specs=tuple([_HBM] * (2 * n)),
        input_output_aliases={i: i for i in range(2 * n)},
        compiler_params=pltpu.CompilerParams(has_side_effects=_EFFECT),
    )(*both, send_sems, recv_sems, after)
    return list(outs[:n]), list(outs[n:])


def _forward_start(lands, name, dep):
    n = len(lands)

    def body(*refs):
        land_refs = refs[:n]
        send_sems, recv_sems = refs[n + 1], refs[n + 2]
        token = refs[-1]
        x, y, c = _place()
        for a in range(n):
            for j, chip in enumerate([(1 - x, y), (x, 1 - y), (1 - x, 1 - y)]):
                blk = land_refs[a].at[_slot(*chip, c)]
                pltpu.make_async_remote_copy(
                    src_ref=blk, dst_ref=blk, send_sem=send_sems.at[a * 3 + j], recv_sem=recv_sems.at[a * 3 + j],
                    device_id=(x, y, 1 - c), device_id_type=MESH).start()
        token[...] = jnp.zeros_like(token)

    outs = pl.pallas_call(
        body, name=name,
        out_shape=(pltpu.SemaphoreType.DMA((n * 3,)), pltpu.SemaphoreType.DMA((n * 3,)),
                   *[pltpu.HBM(l.shape, l.dtype) for l in lands], jax.ShapeDtypeStruct((8, 128), F32)),
        in_specs=(*[_HBM] * n, _ANY), out_specs=(_SEM, _SEM, *[_HBM] * n, pl.BlockSpec(memory_space=pltpu.VMEM)),
        input_output_aliases={i: 2 + i for i in range(n)},
        compiler_params=pltpu.CompilerParams(has_side_effects=_EFFECT),
    )(*lands, dep)
    return (outs[0], outs[1], list(outs[2:2 + n])), outs[-1]


def _forward_wait(handles, after, name):
    send_sems, recv_sems, land_thrus = handles
    n = len(land_thrus)

    def body(*refs):
        land_refs = refs[:n]
        send_sems, recv_sems = refs[n], refs[n + 1]
        me = _place()
        for a in range(n):
            for j in range(3):
                cp = pltpu.make_async_remote_copy(
                    src_ref=land_refs[a].at[0], dst_ref=land_refs[a].at[0],
                    send_sem=send_sems.at[a * 3 + j], recv_sem=recv_sems.at[a * 3 + j], device_id=me, device_id_type=MESH)
                cp.wait_send()
                cp.wait_recv()

    outs = pl.pallas_call(
        body, name=name,
        out_shape=tuple(pltpu.HBM(t.shape, t.dtype) for t in land_thrus),
        in_specs=(*[_HBM] * n, _SEM, _SEM, _ANY), out_specs=tuple([_HBM] * n),
        input_output_aliases={i: i for i in range(n)},
        compiler_params=pltpu.CompilerParams(has_side_effects=_EFFECT),
    )(*land_thrus, send_sems, recv_sems, after)
    return list(outs)


def _sum_slots(land, name):
    _, R, C = land.shape
    tm = R if R <= 512 else 512

    def body(x_ref, o_ref):
        acc = x_ref[0].astype(F32)
        for j in range(1, N_DEV):
            acc = acc + x_ref[j].astype(F32)
        o_ref[...] = acc

    return pl.pallas_call(
        body, name=name, grid=(R // tm,),
        in_specs=[pl.BlockSpec((N_DEV, tm, C), lambda i: (0, i, 0))], out_specs=pl.BlockSpec((tm, C), lambda i: (i, 0)),
        out_shape=jax.ShapeDtypeStruct((R, C), F32), compiler_params=_cparams(("parallel",)),
    )(land)


def _adam_math(w, g, m, v):
    m2 = ADAM_B1 * m + (1.0 - ADAM_B1) * g
    v2 = ADAM_B2 * v + (1.0 - ADAM_B2) * (g * g)
    m_hat = m2 / (1.0 - ADAM_B1 ** ADAM_STEP)
    v_hat = v2 / (1.0 - ADAM_B2 ** ADAM_STEP)
    delta = -ADAM_LR * (m_hat / (jnp.sqrt(v_hat) + ADAM_EPS) + ADAM_WD * w)
    return delta, m2, v2


def _adamw(w, g, m, v, name):
    R, C = w.shape
    tm = R if R <= 512 else 256
    return _rowwise(lambda w, g, m, v: (_adam_math(w, g, m, v), ()), [w, g, m, v], [], [(C, F32)] * 3, [], tm=tm, name=name)


def _adamw_small(slab, slab_rows, g_conv_w, ws, ms, vs):
    n = len(ws)

    def body(*refs):
        slab_ref, gc_ref = refs[0], refs[1]
        w_refs, m_refs, v_refs = refs[2:2 + n], refs[2 + n:2 + 2 * n], refs[2 + 2 * n:2 + 3 * n]
        outs = refs[2 + 3 * n:]
        loss_ref = outs[0]
        g_out, d_out, m_out, v_out = (outs[1 + i * n:1 + (i + 1) * n] for i in range(4))
        loss_ref[...] = jnp.sum(slab_ref[pl.ds(6, 1), :], axis=1, keepdims=True)
        for i in range(n):
            g = gc_ref[...] if i == n - 1 else slab_ref[pl.ds(slab_rows[i], 1), pl.ds(0, ws[i].shape[1])]
            d, m2, v2 = _adam_math(w_refs[i][...], g, m_refs[i][...], v_refs[i][...])
            g_out[i][...] = g
            d_out[i][...] = d
            m_out[i][...] = m2
            v_out[i][...] = v2

    vm = pl.BlockSpec(memory_space=pltpu.VMEM)
    shapes = [jax.ShapeDtypeStruct(w.shape, F32) for w in ws]
    outs = pl.pallas_call(
        body, name="adamw_small", in_specs=[vm] * (2 + 3 * n), out_specs=[vm] * (1 + 4 * n),
        out_shape=[jax.ShapeDtypeStruct((1, 1), F32)] + shapes * 4,
    )(slab, g_conv_w, *ws, *ms, *vs)
    return outs[0], outs[1:1 + n], outs[1 + n:1 + 2 * n], outs[1 + 2 * n:1 + 3 * n], outs[1 + 3 * n:]


SMALL = ["norm_mix_pre", "norm_mix_post", "norm_mlp_pre", "norm_mlp_post", "norm_ple_post",
         "conv_b", "ssd_norm_g", "dt_bias", "a_log", "d_skip"]


def _pad_row(v, width=D):
    return jnp.pad(v, ((0, 0), (0, width - v.shape[1])))


def kernel(x, p, positions, norm_mix_pre, norm_mix_post, w_in, conv_w, conv_b, dt_bias, a_log, d_skip, ssd_norm_g, w_out, norm_mlp_pre, norm_mlp_post, w_up, w_down, w_ple_gate, w_ple_proj, norm_ple_post, loss_target, m_norm_mix_pre, m_norm_mix_post, m_w_in, m_conv_w, m_conv_b, m_dt_bias, m_a_log, m_d_skip, m_ssd_norm_g, m_w_out, m_norm_mlp_pre, m_norm_mlp_post, m_w_up, m_w_down, m_w_ple_gate, m_w_ple_proj, m_norm_ple_post, v_norm_mix_pre, v_norm_mix_post, v_w_in, v_conv_w, v_conv_b, v_dt_bias, v_a_log, v_d_skip, v_ssd_norm_g, v_w_out, v_norm_mlp_pre, v_norm_mlp_post, v_w_up, v_w_down, v_w_ple_gate, v_w_ple_proj, v_norm_ple_post):
    args = dict(locals())
    x2, p2, tgt = x[0], p[0, 0], loss_target[0]
    g1, g2, g3, g4, g5 = norm_mix_pre, norm_mix_post, norm_mlp_pre, norm_mlp_post, norm_ple_post

    me = _slot(*_place())
    pack_in = jnp.pad(w_in[0].T, ((0, W_IN_SHARD_PAD - W_IN_SHARD), (0, 0))).astype(BF16)
    rest = [w_out[0].astype(BF16), w_up[0].T.astype(BF16), w_down[0].astype(BF16), w_ple_gate[0].astype(BF16),
            w_ple_proj[0].T.reshape(32, D).astype(BF16)]
    conv_pack = jnp.pad(conv_w[0], ((0, 4), (0, 32)))
    in_handles, tok_in0 = _send_start_many([pack_in, conv_pack], False, "gather_in_start", g1, peers=_near_peers, npeers=4)

    inv_freq = ROPE_THETA ** (-jnp.arange(HD // 2, dtype=F32) * 2.0 / HD)
    pos = positions[0] + tok_in0[0, 0].astype(jnp.int32)
    ang = pos.astype(F32)[:, None] * inv_freq
    cos, sin = jnp.cos(ang), jnp.sin(ang)
    cos128 = jnp.concatenate([cos, cos, cos, cos], axis=1)
    sin128 = jnp.concatenate([-sin, sin, -sin, sin], axis=1)

    bias_w, alog_w, dsk_w = _pad_row(dt_bias, DT_PAD), _pad_row(a_log, DT_PAD), _pad_row(d_skip, DT_PAD)

    (u1,) = _rowwise(lambda a, g: ((a * _rstd(a) * g,), ()), [x2], [g1], [(D, BF16)], [], tm=512, name="norm_x",
                     deps=[cos128, sin128])
    p2b = p2.astype(BF16)

    in_back, in_land = _send_wait_many(in_handles, u1, "gather_in_wait", npeers=4)
    fw_handles, tok_fw = _forward_start(in_land, "gather_in_forward", u1)
    in_land = _forward_wait(fw_handles, tok_fw, "gather_in_forward_wait")
    gin = lax.dynamic_update_slice(in_land[0], in_back[0][None], (me, 0, 0))
    gconv = lax.dynamic_update_slice(in_land[1], in_back[1][None], (me, 0, 0))
    rest_handles, tok_rest = _send_start_many(rest, False, "gather_rest_start", gconv)
    w_inT = gin[:, :W_IN_SHARD].reshape(IN_W, D)
    w_qkvzT = w_inT[:4 * AW]
    w_xbcdtT = jnp.pad(w_inT[4 * AW:], ((0, DT_PAD - HEADS), (0, 0)))
    conv_full = gconv[:, :CONV_K, :96].transpose(1, 0, 2).reshape(CONV_K, CONV_CH)
    qkvz, xbcdt = _mm_rows(lambda a, b: ((a, b), ()), [(u1, w_qkvzT, True), (u1, w_xbcdtT, True)], [], [],
                           [(4 * AW, F32), (CONV_CH + DT_PAD, F32)], [], tm=512, name="proj_in", deps=[tok_rest])

    qkv = _rope_fwd(qkvz, cos128, sin128)
    qkv = [qkv[3 * i:3 * i + 3] for i in range(len(DILATIONS))]
    outs, lses = [], []
    for d, (qd, kd, vd) in zip(DILATIONS, qkv):
        o, l = _attn_fwd(qd, kd, vd, d)
        outs.append(o)
        lses.append(l)
    attn, lse, attn4, lse4, attn16, lse16 = _attn_merge(outs, lses)

    y_ssd, states, cat, act = _ssd_fwd(xbcdt, conv_full, conv_b, bias_w, alog_w, dsk_w, qkvz, attn, ssd_norm_g)


    rest_back, landed = _send_wait_many(rest_handles, cat, "gather_rest_wait")
    landed = [lax.dynamic_update_slice(l, b[None], (me, 0, 0)) for l, b in zip(landed, rest_back)]
    w_o, w_upT, w_dn, w_gate = landed[0].reshape(D, D), landed[1].reshape(DFF, D), landed[2].reshape(DFF, D), landed[3].reshape(D, D)
    w_projT = landed[4].reshape(D, PLE)

    def post1(mm, xx, ga):
        h = xx + mm * _rstd(mm) * ga
        return (mm, h, _rstd(h)), ()
    mix, h1, r3 = _mm_rows(post1, [(cat, w_o, False)], [x2], [g2], [(D, F32), (D, F32), (1, F32)], [], tm=512,
                           name="mix_out")

    a_up, ff, u2, h2, h2b = _mlp_fwd(h1, r3, g3, w_upT, w_dn, g4)
    relu2 = lambda a: jnp.square(jnp.maximum(a.astype(F32), 0.0))

    def final(gpre, ppv, hh, tg, g):
        sg = _sigmoid(gpre)
        ple = ppv * sg
        r = _rstd(ple)
        n = ple * r
        h3 = hh + n * g
        e = h3 - tg
        dh3 = e * (1.0 / D)
        dple = _rms_bwd(n, r, g, dh3)
        return (dh3, dple * sg, dple * ppv * sg * (1.0 - sg)), (_colsum(dh3 * n), _colsum(0.5 * e * e * (1.0 / D)))
    dh3, dpp, dgp, dg5, loss_vec = _mm_rows(final, [(h2b, w_gate, False), (p2b, w_projT, True)], [h2, tgt], [g5],
                                            [(D, F32), (D, BF16), (D, BF16)], [(1, D), (1, D)], tm=512, name="ple_loss")

    gw_projT = _mm(dpp, p2b, ta=True, tm=512, tn=256, tk=T, out_dtypes=(BF16,), name="gw_ple_proj")
    gw_gate = _mm(h2b, dgp, ta=True, tm=512, tn=1024, tk=T, out_dtypes=(BF16,), name="gw_ple_gate")
    rs_ple, tok_ple = _send_start_many([gw_projT.reshape(N_DEV, 32, D), gw_gate.reshape(N_DEV, 128, D)], True,
                                       "rs_start_w_ple", g1)
    def bwd_mlp_post(dg_, d3, f, g):
        dh2 = d3 + dg_
        r = _rstd(f)
        n = f * r
        return (dh2, _rms_bwd(n, r, g, dh2)), (_colsum(dh2 * n),)
    dh2, dff, dg4 = _mm_rows(bwd_mlp_post, [(dgp, w_gate, True)], [dh3, ff], [g4], [(D, F32), (D, BF16)], [(1, D)],
                             tm=512, name="bwd_ple_gate", deps=[tok_ple])

    gw_dn = _mm(a_up, dff, ta=True, tm=512, tn=1024, tk=T, a_pre=relu2, out_dtypes=(BF16,), name="gw_mlp_down")
    rs_dn, tok_dn = _send_start(gw_dn.reshape(N_DEV, 512, D), True, "rs_start_w_down", g1)
    da_up, du2 = _mlp_dx(dff, a_up, w_upT, w_dn, tok_dn)
    gw_upT = _mm(da_up, u2, ta=True, tm=512, tn=1024, tk=T, out_dtypes=(BF16,), name="gw_mlp_up")
    rs_up, tok_up = _send_start(gw_upT.reshape(N_DEV, 512, D), True, "rs_start_w_up", g1)

    def bwd_mix_post(d2, du, hh, rr, mm, ga, gb):
        n3 = hh * rr
        dh1 = d2 + _rms_bwd(n3, rr, gb, du)
        r = _rstd(mm)
        n2 = mm * r
        return (dh1, _rms_bwd(n2, r, ga, dh1)), (_colsum(du * n3), _colsum(dh1 * n2))
    dh1, dmix, dg3, dg2 = _rowwise(bwd_mix_post, [dh2, du2, h1, r3, mix], [g2, g3], [(D, F32), (D, BF16)],
                                   [(1, D), (1, D)], tm=512, name="bwd_post_mix", deps=[tok_up])

    gw_o = _mm(cat, dmix, ta=True, tm=512, tn=1024, tk=T, out_dtypes=(BF16,), name="gw_out")
    rs_o, tok_o = _send_start(gw_o.reshape(N_DEV, 128, D), True, "rs_start_w_out", g1)
    dcat, dattn4, dattn16 = _dx_out(dmix, w_o, tok_o)

    dact, ddtw, ssd_par, dz, dgs = _ssd_bwd(act, xbcdt, bias_w, alog_w, dsk_w, states, y_ssd, qkvz, dcat, ssd_norm_g)
    dxbcdt, conv_par = _conv_bwd(xbcdt, dact, ddtw, conv_full, conv_b)

    qkv_grads = [_attn_bwd(*qkv[0], dcat, attn, lse, 1),
                 _attn_bwd(*qkv[1], dattn4, attn4, lse4, 4),
                 _attn_bwd(*qkv[2], dattn16, attn16, lse16, 16)]
    dqkvz = _rope_bwd(qkv_grads, dz, cos128, sin128)

    gw_qkvzT = _mm(dqkvz, u1, ta=True, tm=512, tn=1024, tk=T, out_dtypes=(BF16,), name="gw_qkvz")
    gw_xbcdtT = _mm(dxbcdt, u1, ta=True, tm=896, tn=1024, tk=T, out_dtypes=(BF16,), name="gw_xbcdt")
    gw_inT = jnp.concatenate([gw_qkvzT, gw_xbcdtT], axis=0)[:IN_W]
    gw_inT = jnp.pad(gw_inT.reshape(N_DEV, W_IN_SHARD, D), ((0, 0), (0, W_IN_SHARD_PAD - W_IN_SHARD), (0, 0)))
    rs_in, tok_in = _send_start(gw_inT, True, "rs_start_w_in", g1)

    def bwd_in(ua, ub, d1, xx, g):
        rr = _rstd(xx)
        n = xx * rr
        du = ua + ub
        return (d1 + _rms_bwd(n, rr, g, du),), (_colsum(du * n),)
    grad_x, dg1 = _mm_rows(bwd_in, [(dqkvz, w_qkvzT, False), (dxbcdt, w_xbcdtT, False)], [dh1, x2], [g1],
                           [(D, F32)], [(1, D)], tm=512, name="bwd_in_proj", deps=[tok_in])

    my_slab = _slab_pack([(dg1, 0), (dg2, 1), (dg3, 2), (dg4, 3), (dg5, 4), (dgs, 5), (loss_vec, 6),
                          (conv_par, 8), (ssd_par, 16)], "slab_pack")
    slab_handles, tok_slab = _send_start_many([my_slab], False, "slab_start", g1)

    def scatter_finish(handles, nm, after):
        part, land = _send_wait(handles, after, "rs_wait_" + nm)
        own = lax.dynamic_slice(part, (me, 0, 0), (1,) + part.shape[1:])
        return _sum_slots(lax.dynamic_update_slice(land, own, (me, 0, 0)), "rs_sum_" + nm)
    g_out = scatter_finish(rs_o, "w_out", tok_slab)
    g_upT = scatter_finish(rs_up, "w_up", tok_slab)
    g_dn = scatter_finish(rs_dn, "w_down", tok_slab)
    ple_parts, ple_lands = _send_wait_many(rs_ple, tok_slab, "rs_wait_w_ple")
    g_projT, g_gate = [
        _sum_slots(lax.dynamic_update_slice(land, lax.dynamic_slice(part, (me, 0, 0), (1,) + part.shape[1:]), (me, 0, 0)),
                   "rs_sum_" + nm) for part, land, nm in zip(ple_parts, ple_lands, ("w_proj", "w_gate"))]

    grads = {
        "w_out": g_out[None], "w_up": g_upT.T[None], "w_down": g_dn[None],
        "w_ple_gate": g_gate[None], "w_ple_proj": g_projT.reshape(128, PLE).T[None],
    }
    delta, new_m, new_v = {}, {}, {}
    for nme in ["w_out", "w_up", "w_down", "w_ple_gate", "w_ple_proj", "w_in"]:
        if nme == "w_in":
            g_inT = scatter_finish(rs_in, "w_in", delta["w_down"])
            grads["w_in"] = g_inT[:W_IN_SHARD].T[None]
        dl, mm_, vv_ = _adamw(args[nme][0], grads[nme][0], args["m_" + nme][0], args["v_" + nme][0], "adamw_" + nme)
        delta[nme], new_m[nme], new_v[nme] = dl[None], mm_[None], vv_[None]

    slab_back, slab_land = _send_wait_many(slab_handles, delta["w_in"], "slab_wait")
    slab = _sum_slots(lax.dynamic_update_slice(slab_land[0], slab_back[0][None], (me, 0, 0)), "slab_sum")
    g_conv_w = lax.dynamic_slice(slab[8:12, :CONV_CH], (0, me * 96), (CONV_K, 96))
    small_names = SMALL + ["conv_w"]
    small_rows = [0, 1, 2, 3, 4, 12, 5, 16, 17, 18, None]
    pick = lambda prefix: [args[prefix + nme] for nme in SMALL] + [args[prefix + "conv_w"][0]]
    loss11, g_s, d_s, m_s, v_s = _adamw_small(slab, small_rows, g_conv_w, pick(""), pick("m_"), pick("v_"))
    loss = loss11[0, 0]
    for i, nme in enumerate(small_names):
        lead = (lambda t: t[None]) if nme == "conv_w" else (lambda t: t)
        grads[nme], delta[nme], new_m[nme], new_v[nme] = lead(g_s[i]), lead(d_s[i]), lead(m_s[i]), lead(v_s[i])

    order = ["norm_mix_pre", "norm_mix_post", "w_in", "conv_w", "conv_b", "dt_bias", "a_log", "d_skip", "ssd_norm_g",
             "w_out", "norm_mlp_pre", "norm_mlp_post", "w_up", "w_down", "w_ple_gate", "w_ple_proj", "norm_ple_post"]
    return (loss, grad_x[None], *[grads[n] for n in order], *[delta[n] for n in order],
            *[new_m[n] for n in order], *[new_v[n] for n in order])
```

```python
import jax
import jax.numpy as jnp
from jax import lax
from jax.experimental import pallas as pl
from jax.experimental.pallas import tpu as pltpu

F32 = jnp.float32
BF16 = jnp.bfloat16
MESH = pl.DeviceIdType.MESH
HIGHEST = lax.Precision.HIGHEST

N_DEV = 8
T = 4096
D = 1024
HEADS = 8
HD = 64
AW = 512
NS = 128
CONV_K = 4
CONV_CH = 768
CHUNK = 128
SSD_PER = 2
DFF = 4096
PLE = 256
EPS = 1e-6
ROPE_THETA = 10000.0
DILATIONS = (1, 4, 16)
QBLK = 128
NEG = -1e30
IN_W = 2824
W_IN_SHARD = 353
W_IN_SHARD_PAD = 384
DT_PAD = 128

ADAM_LR, ADAM_B1, ADAM_B2, ADAM_EPS, ADAM_WD, ADAM_STEP = 0.001, 0.9, 0.999, 1e-08, 0.01, 10

VMEM_LIMIT = 56 * 1024 * 1024


_ANY = pl.BlockSpec(memory_space=pl.ANY)


def _cparams(sem=None):
    return pltpu.CompilerParams(dimension_semantics=sem, vmem_limit_bytes=VMEM_LIMIT)


def _dot(a, b, ca, cb, precision=None):
    return lax.dot_general(a, b, (((ca,), (cb,)), ((), ())), preferred_element_type=F32, precision=precision)


def _nn(a, b):
    return _dot(a, b, 1, 0)


def _nt(a, b):
    return _dot(a, b, 1, 1)


def _tn(a, b):
    return _dot(a, b, 0, 0)


def _sigmoid(x):
    return 1.0 / (1.0 + jnp.exp(-x))


def _softplus(x):
    return jnp.maximum(x, 0.0) + jnp.log(1.0 + jnp.exp(-jnp.abs(x)))


def _mm(a, b, *, ta=False, tb=False, tm, tn, tk, name,
        a_pre=None, a_rows=(), a_cols=(), b_pre=None, b_rows=(), b_cols=(),
        epi=None, epi_tiles=(), out_dtypes=(F32,), deps=()):
    if ta:
        K, M = a.shape
    else:
        M, K = a.shape
    if tb:
        N, K2 = b.shape
    else:
        K2, N = b.shape
    assert K == K2 and M % tm == 0 and N % tn == 0 and K % tk == 0, (name, a.shape, b.shape)
    nk = K // tk
    if ta:
        a_spec = pl.BlockSpec((tk, tm), lambda i, j, k: (k, i))
        a_row_specs = [pl.BlockSpec((tk, 1), lambda i, j, k: (k, 0)) for _ in a_rows]
        a_col_specs = [pl.BlockSpec((1, tm), lambda i, j, k: (0, i)) for _ in a_cols]
    else:
        a_spec = pl.BlockSpec((tm, tk), lambda i, j, k: (i, k))
        a_row_specs = [pl.BlockSpec((tm, 1), lambda i, j, k: (i, 0)) for _ in a_rows]
        a_col_specs = [pl.BlockSpec((1, tk), lambda i, j, k: (0, k)) for _ in a_cols]
    if tb:
        b_spec = pl.BlockSpec((tn, tk), lambda i, j, k: (j, k))
        b_row_specs = [pl.BlockSpec((tn, 1), lambda i, j, k: (j, 0)) for _ in b_rows]
        b_col_specs = [pl.BlockSpec((1, tk), lambda i, j, k: (0, k)) for _ in b_cols]
    else:
        b_spec = pl.BlockSpec((tk, tn), lambda i, j, k: (k, j))
        b_row_specs = [pl.BlockSpec((tk, 1), lambda i, j, k: (k, 0)) for _ in b_rows]
        b_col_specs = [pl.BlockSpec((1, tn), lambda i, j, k: (0, j)) for _ in b_cols]
    o_spec = pl.BlockSpec((tm, tn), lambda i, j, k: (i, j))
    na, nb, ne, no = len(a_rows) + len(a_cols), len(b_rows) + len(b_cols), len(epi_tiles), len(out_dtypes)

    def body(*refs):
        a_ref, b_ref = refs[0], refs[1]
        a_ex = refs[2:2 + na]
        b_ex = refs[2 + na:2 + na + nb]
        e_ex = refs[2 + na + nb:2 + na + nb + ne]
        first_out = 2 + na + nb + ne + len(deps)
        outs = refs[first_out:first_out + no]

        def finish(res):
            vals = epi(res, *[r[...] for r in e_ex]) if epi is not None else (res,)
            for o_ref, val in zip(outs, vals):
                o_ref[...] = val.astype(o_ref.dtype)

        at = a_ref[...]
        if a_pre is not None:
            at = a_pre(at, *[r[...] for r in a_ex])
        bt = b_ref[...]
        if b_pre is not None:
            bt = b_pre(bt, *[r[...] for r in b_ex])
        prod = _dot(at.astype(BF16), bt.astype(BF16), 0 if ta else 1, 1 if tb else 0)
        if nk == 1:
            finish(prod)
            return
        acc = refs[-1]
        k = pl.program_id(2)

        @pl.when(k == 0)
        def _():
            acc[...] = jnp.zeros_like(acc)
        acc[...] += prod

        @pl.when(k == nk - 1)
        def _():
            finish(acc[...])

    outs = pl.pallas_call(
        body, name=name,
        grid=(M // tm, N // tn, nk),
        in_specs=([a_spec, b_spec] + a_row_specs + a_col_specs + b_row_specs + b_col_specs + [o_spec] * ne
                  + [_ANY] * len(deps)),
        out_specs=[o_spec] * no,
        out_shape=[jax.ShapeDtypeStruct((M, N), dt) for dt in out_dtypes],
        scratch_shapes=[pltpu.VMEM((tm, tn), F32)] if nk > 1 else [],
        compiler_params=_cparams(("parallel", "parallel", "arbitrary")),
    )(a, b, *a_rows, *a_cols, *b_rows, *b_cols, *epi_tiles, *deps)
    return outs[0] if no == 1 else outs


MLP_TM = 1024
MLP_TC = 512


def _mlp_fwd(h, r, g, w_upT, w_dn, g_post):
    nc = DFF // MLP_TC

    def body(h_ref, r_ref, g_ref, wu_ref, wd_ref, gp_ref, a_ref, ff_ref, u_ref, ho_ref, hob_ref, acc, u_scr):
        c = pl.program_id(1)

        @pl.when(c == 0)
        def _():
            u = (h_ref[...] * r_ref[...] * g_ref[...]).astype(BF16)
            u_scr[...] = u
            u_ref[...] = u
            acc[...] = jnp.zeros_like(acc)
        a = _nt(u_scr[...], wu_ref[...])
        a_ref[...] = a.astype(BF16)
        acc[...] += _nn(jnp.square(jnp.maximum(a, 0.0)).astype(BF16), wd_ref[...])

        @pl.when(c == nc - 1)
        def _():
            f = acc[...]
            ff_ref[...] = f
            ho = h_ref[...] + f * _rstd(f) * gp_ref[...]
            ho_ref[...] = ho
            hob_ref[...] = ho.astype(BF16)

    row = pl.BlockSpec((MLP_TM, D), lambda i, c: (i, 0))
    wsp = pl.BlockSpec((MLP_TC, D), lambda i, c: (c, 0))
    vec = pl.BlockSpec((1, D), lambda i, c: (0, 0))
    return pl.pallas_call(
        body, name="mlp_fwd", grid=(T // MLP_TM, nc),
        in_specs=[row, pl.BlockSpec((MLP_TM, 1), lambda i, c: (i, 0)), vec, wsp, wsp, vec],
        out_specs=[pl.BlockSpec((MLP_TM, MLP_TC), lambda i, c: (i, c)), row, row, row, row],
        out_shape=[jax.ShapeDtypeStruct((T, DFF), BF16), jax.ShapeDtypeStruct((T, D), F32), jax.ShapeDtypeStruct((T, D), BF16),
                   jax.ShapeDtypeStruct((T, D), F32), jax.ShapeDtypeStruct((T, D), BF16)],
        scratch_shapes=[pltpu.VMEM((MLP_TM, D), F32), pltpu.VMEM((MLP_TM, D), BF16)],
        compiler_params=_cparams(("parallel", "arbitrary")),
    )(h, r, g, w_upT, w_dn, g_post)


def _mlp_dx(dff, a, w_upT, w_dn, dep):
    nc = DFF // MLP_TC

    def body(d_ref, a_ref, wu_ref, wd_ref, dep_ref, da_ref, du_ref, acc, d_scr):
        c = pl.program_id(1)

        @pl.when(c == 0)
        def _():
            d_scr[...] = d_ref[...].astype(BF16)
            acc[...] = jnp.zeros_like(acc)
        da = (_nt(d_scr[...], wd_ref[...]) * (2.0 * jnp.maximum(a_ref[...].astype(F32), 0.0))).astype(BF16)
        da_ref[...] = da
        acc[...] += _nn(da, wu_ref[...])

        @pl.when(c == nc - 1)
        def _():
            du_ref[...] = acc[...]

    row = pl.BlockSpec((MLP_TM, D), lambda i, c: (i, 0))
    wsp = pl.BlockSpec((MLP_TC, D), lambda i, c: (c, 0))
    chunk = pl.BlockSpec((MLP_TM, MLP_TC), lambda i, c: (i, c))
    return pl.pallas_call(
        body, name="mlp_dx", grid=(T // MLP_TM, nc),
        in_specs=[row, chunk, wsp, wsp, _ANY], out_specs=[chunk, row],
        out_shape=[jax.ShapeDtypeStruct((T, DFF), BF16), jax.ShapeDtypeStruct((T, D), F32)],
        scratch_shapes=[pltpu.VMEM((MLP_TM, D), F32), pltpu.VMEM((MLP_TM, D), BF16)],
        compiler_params=_cparams(("parallel", "arbitrary")),
    )(dff, a, w_upT, w_dn, dep)


def _rowwise(fn, rows, vecs, out_rows, out_sums, *, tm, name, deps=()):
    specs, arrs = [], []
    R = None
    for r in rows:
        if isinstance(r, tuple):
            arr, width, cb = r
            specs.append(pl.BlockSpec((tm, width), lambda i, cb=cb: (i, cb)))
        else:
            arr = r
            specs.append(pl.BlockSpec((tm, arr.shape[1]), lambda i: (i, 0)))
        R = arr.shape[0] if R is None else R
        assert arr.shape[0] == R, name
        arrs.append(arr)
    assert R % tm == 0, name
    for v in vecs:
        specs.append(pl.BlockSpec(v.shape, lambda i: (0, 0)))
        arrs.append(v)
    nr, nv, no, ns = len(rows), len(vecs), len(out_rows), len(out_sums)
    out_specs = [pl.BlockSpec((tm, w), lambda i: (i, 0)) for w, _ in out_rows]
    out_specs += [pl.BlockSpec(s, lambda i: (0, 0)) for s in out_sums]
    out_shape = [jax.ShapeDtypeStruct((R, w), dt) for w, dt in out_rows]
    out_shape += [jax.ShapeDtypeStruct(s, F32) for s in out_sums]

    nd = len(deps)

    def body(*refs):
        ins = [r[...] for r in refs[:nr + nv]]
        o_refs = refs[nr + nv + nd:nr + nv + nd + no]
        s_refs = refs[nr + nv + nd + no:]
        o_vals, s_vals = fn(*ins)
        for ref, val in zip(o_refs, o_vals):
            ref[...] = val.astype(ref.dtype)
        if ns:
            @pl.when(pl.program_id(0) == 0)
            def _():
                for ref in s_refs:
                    ref[...] = jnp.zeros_like(ref)
            for ref, val in zip(s_refs, s_vals):
                ref[...] += val

    outs = pl.pallas_call(
        body, name=name, grid=(R // tm,), in_specs=specs + [_ANY] * nd, out_specs=out_specs, out_shape=out_shape,
        compiler_params=_cparams(("arbitrary",) if ns else ("parallel",)),
    )(*arrs, *deps)
    return outs


def _mm_rows(fn, mats, rows, vecs, out_rows, out_sums, *, tm, name, deps=()):
    R = mats[0][0].shape[0]
    assert R % tm == 0, name
    specs, arrs = [], []
    for a, b, tb in mats:
        specs += [pl.BlockSpec((tm, a.shape[1]), lambda i: (i, 0)), pl.BlockSpec(b.shape, lambda i: (0, 0))]
        arrs += [a, b]
    for r in rows:
        specs.append(pl.BlockSpec((tm, r.shape[1]), lambda i: (i, 0)))
        arrs.append(r)
    for v in vecs:
        specs.append(pl.BlockSpec(v.shape, lambda i: (0, 0)))
        arrs.append(v)
    nm, nr, nv, nd, no, ns = len(mats), len(rows), len(vecs), len(deps), len(out_rows), len(out_sums)
    out_specs = [pl.BlockSpec((tm, w), lambda i: (i, 0)) for w, _ in out_rows]
    out_specs += [pl.BlockSpec(s, lambda i: (0, 0)) for s in out_sums]
    out_shape = [jax.ShapeDtypeStruct((R, w), dt) for w, dt in out_rows] + [jax.ShapeDtypeStruct(s, F32) for s in out_sums]

    def body(*refs):
        prods = [_dot(refs[2 * p][...].astype(BF16), refs[2 * p + 1][...].astype(BF16), 1, 1 if mats[p][2] else 0)
                 for p in range(nm)]
        ins = [r[...] for r in refs[2 * nm:2 * nm + nr + nv]]
        first_out = 2 * nm + nr + nv + nd
        o_refs, s_refs = refs[first_out:first_out + no], refs[first_out + no:]
        o_vals, s_vals = fn(*prods, *ins)
        for ref, val in zip(o_refs, o_vals):
            ref[...] = val.astype(ref.dtype)
        if ns:
            @pl.when(pl.program_id(0) == 0)
            def _():
                for ref in s_refs:
                    ref[...] = jnp.zeros_like(ref)
            for ref, val in zip(s_refs, s_vals):
                ref[...] += val

    return pl.pallas_call(
        body, name=name, grid=(R // tm,), in_specs=specs + [_ANY] * nd, out_specs=out_specs, out_shape=out_shape,
        compiler_params=_cparams(("arbitrary",) if ns else ("parallel",)),
    )(*arrs, *deps)


def _colsum(x):
    return jnp.sum(x, axis=0, keepdims=True)


def _rstd(x):
    return lax.rsqrt(jnp.mean(x * x, axis=-1, keepdims=True) + EPS)


def _rms_bwd(xn, r, g, dy):
    dn = dy * g
    return r * (dn - xn * jnp.mean(dn * xn, axis=-1, keepdims=True))


def _partner(t):
    lane = lax.broadcasted_iota(jnp.int32, t.shape, 1)
    up = pltpu.roll(t, 96, 1)
    down = pltpu.roll(t, 32, 1)
    return jnp.where((lane % 64) < 32, up, down)


SLABS = AW // 128


def _rows(r, n, d):
    return pl.ds(r, n, stride=d) if d > 1 else pl.ds(0, n)


def _undilate(src_ref, dst, d, tm):
    for r in range(d):
        for j in range(SLABS):
            dst[j][_rows(r, tm // d, d), :] = src_ref[:, pl.ds(r * AW + j * 128, 128)].astype(dst[j].dtype)


def _dilate(dst_ref, src, d, tm):
    for r in range(d):
        for j in range(SLABS):
            dst_ref[:, pl.ds(r * AW + j * 128, 128)] = src[j][_rows(r, tm // d, d), :].astype(dst_ref.dtype)


def _slab_scratch(n, tm):
    return [pltpu.VMEM((tm, 128), F32)] * (SLABS * n)


def _slab_groups(flat):
    return [flat[SLABS * i:SLABS * (i + 1)] for i in range(len(flat) // SLABS)]


def _slab_specs(tm, first):
    return [pl.BlockSpec((tm, 128), lambda i, j=j: (i, first + j)) for j in range(SLABS)]


def _dil_spec(tm, d):
    return pl.BlockSpec((tm // d, d * AW), lambda i: (i, 0))


ROPE_TM = 512


def _rope_fwd(qkvz, cos128, sin128):
    tm = ROPE_TM

    def body(*refs):
        q_refs, k_refs, v_refs = refs[0:4], refs[4:8], refs[8:12]
        c_ref, s_ref = refs[12], refs[13]
        outs = refs[14:23]
        qs, ks = _slab_groups(refs[23:])
        c, s = c_ref[...], s_ref[...]
        for j in range(SLABS):
            q, k = q_refs[j][...], k_refs[j][...]
            qs[j][...] = (q * c + _partner(q) * s) * (HD ** -0.5)
            ks[j][...] = k * c + _partner(k) * s
        for di, d in enumerate(DILATIONS):
            oq, ok, ov = outs[3 * di:3 * di + 3]
            for r in range(d):
                rows = _rows(r, tm // d, d)
                for j in range(SLABS):
                    cols = pl.ds(r * AW + j * 128, 128)
                    oq[:, cols] = qs[j][rows, :].astype(BF16)
                    ok[:, cols] = ks[j][rows, :].astype(BF16)
                    ov[:, cols] = v_refs[j][rows, :].astype(BF16)

    tab = pl.BlockSpec((tm, 128), lambda i: (i, 0))
    out_specs, out_shape = [], []
    for d in DILATIONS:
        out_specs += [_dil_spec(tm, d)] * 3
        out_shape += [jax.ShapeDtypeStruct((T // d, d * AW), BF16)] * 3
    return pl.pallas_call(
        body, name="rope_fwd", grid=(T // tm,),
        in_specs=_slab_specs(tm, 0) + _slab_specs(tm, 4) + _slab_specs(tm, 8) + [tab, tab],
        out_specs=out_specs, out_shape=out_shape, scratch_shapes=_slab_scratch(2, tm),
        compiler_params=_cparams(("parallel",)),
    )(*([qkvz] * 12), cos128, sin128)


def _rope_bwd(grads, dz, cos128, sin128):
    tm = ROPE_TM

    def body(*refs):
        g_refs = refs[0:9]
        dz_ref, c_ref, s_ref, o_ref = refs[9], refs[10], refs[11], refs[12]
        scr = _slab_groups(refs[13:])
        for di, d in enumerate(DILATIONS[1:]):
            for t in range(3):
                _undilate(g_refs[3 * (di + 1) + t], scr[3 * di + t], d, tm)
        c, s = c_ref[...], s_ref[...]
        for j in range(SLABS):
            cols = pl.ds(j * 128, 128)
            tot = [g_refs[t][:, cols] + scr[t][j][...] + scr[3 + t][j][...] for t in range(3)]
            dqr = tot[0] * (HD ** -0.5)
            o_ref[:, pl.ds(j * 128, 128)] = (dqr * c + _partner(dqr * s)).astype(BF16)
            o_ref[:, pl.ds(AW + j * 128, 128)] = (tot[1] * c + _partner(tot[1] * s)).astype(BF16)
            o_ref[:, pl.ds(2 * AW + j * 128, 128)] = tot[2].astype(BF16)
        o_ref[:, pl.ds(3 * AW, AW)] = dz_ref[...].astype(BF16)

    tab = pl.BlockSpec((tm, 128), lambda i: (i, 0))
    in_specs, args = [], []
    for d, g in zip(DILATIONS, grads):
        in_specs += [_dil_spec(tm, d)] * 3
        args += list(g)
    return pl.pallas_call(
        body, name="rope_bwd", grid=(T // tm,),
        in_specs=in_specs + [pl.BlockSpec((tm, AW), lambda i: (i, 0)), tab, tab],
        out_specs=pl.BlockSpec((tm, 4 * AW), lambda i: (i, 0)),
        out_shape=jax.ShapeDtypeStruct((T, 4 * AW), BF16),
        scratch_shapes=_slab_scratch(6, tm),
        compiler_params=_cparams(("parallel",)),
    )(*args, dz, cos128, sin128)


def _dx_out(dmix, w_o, dep):
    tm = ROPE_TM

    def body(a_ref, w_ref, dep_ref, dcat_ref, o4, o16, *slabs):
        prod = _nt(a_ref[...].astype(BF16), w_ref[...].astype(BF16))
        dcat_ref[...] = prod
        for j in range(SLABS):
            slabs[j][...] = prod[:, 128 * j:128 * (j + 1)]
        _dilate(o4, slabs, 4, tm)
        _dilate(o16, slabs, 16, tm)

    return pl.pallas_call(
        body, name="dx_out", grid=(T // tm,),
        in_specs=[pl.BlockSpec((tm, D), lambda i: (i, 0)), pl.BlockSpec((D, D), lambda i: (0, 0)), _ANY],
        out_specs=[pl.BlockSpec((tm, D), lambda i: (i, 0)), _dil_spec(tm, 4), _dil_spec(tm, 16)],
        out_shape=[jax.ShapeDtypeStruct((T, D), F32), jax.ShapeDtypeStruct((T // 4, 4 * AW), F32),
                   jax.ShapeDtypeStruct((T // 16, 16 * AW), F32)],
        scratch_shapes=_slab_scratch(1, tm), compiler_params=_cparams(("parallel",)),
    )(dmix, w_o, dep)


def _band_masks():
    qi = lax.broadcasted_iota(jnp.int32, (QBLK, QBLK), 0)
    kj = lax.broadcasted_iota(jnp.int32, (QBLK, QBLK), 1)
    return kj >= qi, kj <= qi


def _attn_fwd(q, k, v, d):
    L = q.shape[0]
    npair = L // (2 * QBLK)

    def body(q_ref, kp_ref, kc_ref, vp_ref, vc_ref, o_ref, l_ref):
        pair = pl.program_id(1)
        mask_p, mask_c = _band_masks()
        for sub in range(2):
            rows = pl.ds(sub * QBLK, QBLK)
            first = jnp.where(pair > 0, 0.0, NEG) if sub == 0 else 0.0
            bias = jnp.concatenate([jnp.where(mask_p, 0.0, NEG) + first, jnp.where(mask_c, 0.0, NEG)], axis=1)
            k_prev = (lambda sl: kp_ref[:, sl]) if sub == 0 else (lambda sl: kc_ref[pl.ds(0, QBLK), sl])
            v_prev = (lambda sl: vp_ref[:, sl]) if sub == 0 else (lambda sl: vc_ref[pl.ds(0, QBLK), sl])
            s = []
            for h in range(HEADS):
                sl = pl.ds(HD * h, HD)
                qh = q_ref[rows, sl]
                s.append(jnp.concatenate([_nt(qh, k_prev(sl)), _nt(qh, kc_ref[rows, sl])], axis=1))
            s = jnp.stack(s) + bias
            m = jnp.max(s, axis=2, keepdims=True)
            e = jnp.exp(s - m)
            den = jnp.sum(e, axis=2, keepdims=True)
            p = e.astype(BF16)
            inv = 1.0 / den
            lse = m + jnp.log(den)
            for h in range(HEADS):
                sl = pl.ds(HD * h, HD)
                o_ref[rows, sl] = (_nn(p[h, :, :QBLK], v_prev(sl)) + _nn(p[h, :, QBLK:], vc_ref[rows, sl])) * inv[h]
                l_ref[rows, sl] = jnp.broadcast_to(lse[h], (QBLK, HD))

    cur = pl.BlockSpec((2 * QBLK, AW), lambda r, n: (n, r))
    prev = pl.BlockSpec((QBLK, AW), lambda r, n: (jnp.maximum(2 * n - 1, 0), r))
    return pl.pallas_call(
        body, name=f"attn_fwd_d{d}", grid=(d, npair),
        in_specs=[cur, prev, cur, prev, cur], out_specs=[cur, cur],
        out_shape=[jax.ShapeDtypeStruct((L, d * AW), F32)] * 2,
        compiler_params=_cparams(("parallel", "parallel")),
    )(q, k, k, v, v)


def _attn_bwd(q, k, v, do, at, lse, d):
    L = q.shape[0]
    nb = L // QBLK
    npair = nb // 2

    def body(qc_ref, qn_ref, kp_ref, kc_ref, vp_ref, vc_ref, doc_ref, don_ref, atc_ref, atn_ref,
             lc_ref, ln_ref, dq_ref, dk_ref, dv_ref):
        pair = pl.program_id(1)
        mask_p, mask_c = _band_masks()
        prev_bias = jnp.where(mask_p, 0.0, NEG)
        for sub in range(2):
            rows = pl.ds(sub * QBLK, QBLK)
            second = pl.ds(QBLK, QBLK)
            if sub == 0:
                take = lambda cur_ref, nxt_ref, cols, i: cur_ref[rows if i == 0 else second, cols]
                prev_of = lambda p_ref, c_ref, cols: p_ref[:, cols]
                first, last = jnp.where(pair > 0, 0.0, NEG), 0.0
            else:
                take = lambda cur_ref, nxt_ref, cols, i: cur_ref[rows, cols] if i == 0 else nxt_ref[:, cols]
                prev_of = lambda p_ref, c_ref, cols: c_ref[pl.ds(0, QBLK), cols]
                first, last = 0.0, jnp.where(pair < npair - 1, 0.0, NEG)
            bias = jnp.concatenate([prev_bias + first, jnp.where(mask_c, 0.0, NEG), prev_bias + last], axis=1)
            s, dp, ls, dl, ops = [], [], [], [], []
            for h in range(HEADS):
                sl = pl.ds(HD * h, HD)
                one = pl.ds(HD * h, 1)
                q0, q1 = take(qc_ref, qn_ref, sl, 0), take(qc_ref, qn_ref, sl, 1)
                kp, kc = prev_of(kp_ref, kc_ref, sl), kc_ref[rows, sl]
                vp, vc = prev_of(vp_ref, vc_ref, sl), vc_ref[rows, sl]
                do0, do1 = take(doc_ref, don_ref, sl, 0), take(doc_ref, don_ref, sl, 1)
                do0b, do1b = do0.astype(BF16), do1.astype(BF16)
                s.append(jnp.concatenate([_nt(q0, kp), _nt(q0, kc), _nt(q1, kc)], axis=1))
                dp.append(jnp.concatenate([_nt(do0b, vp), _nt(do0b, vc), _nt(do1b, vc)], axis=1))
                dl0 = jnp.sum(do0 * take(atc_ref, atn_ref, sl, 0), axis=1, keepdims=True)
                dl1 = jnp.sum(do1 * take(atc_ref, atn_ref, sl, 1), axis=1, keepdims=True)
                dl.append(jnp.concatenate([jnp.broadcast_to(dl0, (QBLK, 2 * QBLK)), jnp.broadcast_to(dl1, (QBLK, QBLK))], axis=1))
                ls.append(jnp.concatenate([jnp.broadcast_to(take(lc_ref, ln_ref, one, 0), (QBLK, 2 * QBLK)),
                                           jnp.broadcast_to(take(lc_ref, ln_ref, one, 1), (QBLK, QBLK))], axis=1))
                ops.append((q0, q1, kp, kc, do0b, do1b))
            p = jnp.exp(jnp.stack(s) + bias - jnp.stack(ls))
            ds = (p * (jnp.stack(dp) - jnp.stack(dl))).astype(BF16)
            p = p.astype(BF16)
            for h in range(HEADS):
                sl = pl.ds(HD * h, HD)
                q0, q1, kp, kc, do0b, do1b = ops[h]
                dq_ref[rows, sl] = (_nn(ds[h, :, :QBLK], kp) + _nn(ds[h, :, QBLK:2 * QBLK], kc)).astype(BF16)
                dv_ref[rows, sl] = (_tn(p[h, :, QBLK:2 * QBLK], do0b) + _tn(p[h, :, 2 * QBLK:], do1b)).astype(BF16)
                dk_ref[rows, sl] = (_tn(ds[h, :, QBLK:2 * QBLK], q0) + _tn(ds[h, :, 2 * QBLK:], q1)).astype(BF16)

    cur = pl.BlockSpec((2 * QBLK, AW), lambda r, n: (n, r))
    prev = pl.BlockSpec((QBLK, AW), lambda r, n: (jnp.maximum(2 * n - 1, 0), r))
    nxt = pl.BlockSpec((QBLK, AW), lambda r, n: (jnp.minimum(2 * n + 2, nb - 1), r))
    return pl.pallas_call(
        body, name=f"attn_bwd_d{d}", grid=(d, npair),
        in_specs=[cur, nxt, prev, cur, prev, cur, cur, nxt, cur, nxt, cur, nxt], out_specs=[cur, cur, cur],
        out_shape=[jax.ShapeDtypeStruct((L, d * AW), BF16)] * 3,
        compiler_params=_cparams(("parallel", "parallel")),
    )(q, q, k, k, v, v, do, do, at, at, lse, lse)


def _attn_merge(outs, lses):
    tm = ROPE_TM

    def body(o1, o4, o16, l1, l4, l16, at_ref, ls_ref, at4, ls4, at16, ls16, *flat):
        so4, so16, sl4, sl16, sa, sl = _slab_groups(flat)
        _undilate(o4, so4, 4, tm)
        _undilate(o16, so16, 16, tm)
        _undilate(l4, sl4, 4, tm)
        _undilate(l16, sl16, 16, tm)
        for j in range(SLABS):
            cols = pl.ds(j * 128, 128)
            a, b, c = l1[:, cols], sl4[j][...], sl16[j][...]
            m = jnp.maximum(jnp.maximum(a, b), c)
            e1, e2, e3 = jnp.exp(a - m), jnp.exp(b - m), jnp.exp(c - m)
            s = e1 + e2 + e3
            inv = 1.0 / s
            attn = (e1 * inv) * o1[:, cols] + (e2 * inv) * so4[j][...] + (e3 * inv) * so16[j][...]
            lse = m + jnp.log(s)
            at_ref[:, cols] = attn
            ls_ref[:, cols] = lse
            sa[j][...] = attn
            sl[j][...] = lse
        _dilate(at4, sa, 4, tm)
        _dilate(at16, sa, 16, tm)
        _dilate(ls4, sl, 4, tm)
        _dilate(ls16, sl, 16, tm)

    specs = [_dil_spec(tm, d) for d in DILATIONS]
    tok = specs[0]
    return pl.pallas_call(
        body, name="attn_merge", grid=(T // tm,),
        in_specs=specs + specs, out_specs=[tok, tok, specs[1], specs[1], specs[2], specs[2]],
        out_shape=[jax.ShapeDtypeStruct((T, AW), F32)] * 2 + [jax.ShapeDtypeStruct((T // 4, 4 * AW), F32)] * 2
        + [jax.ShapeDtypeStruct((T // 16, 16 * AW), F32)] * 2,
        scratch_shapes=_slab_scratch(6, tm),
        compiler_params=_cparams(("parallel",)),
    )(*outs, *lses)


CONV_TM = 512
HALO = 8


def _conv_pre(ext, w, b):
    y = b + w[3] * ext
    for kk in range(1, CONV_K):
        y = y + w[3 - kk] * pltpu.roll(ext, kk, 0)
    return y


def _rows_to_block(rows, n, width):
    ri = lax.broadcasted_iota(jnp.int32, (n, width), 0)
    out = jnp.zeros((n, width), F32)
    for j, r in enumerate(rows):
        out = out + jnp.where(ri == j, r, 0.0)
    return out


def _conv_bwd(xbc, dact, ddt, w, b):
    nblk = T // CONV_TM
    per = CONV_TM // HALO

    def body(x_ref, xb_ref, xa_ref, g_ref, ga_ref, ddt_ref, w_ref, b_ref, dx_ref, dw_ref):
        i = pl.program_id(0)
        wv = [w_ref[pl.ds(j, 1), :] for j in range(CONV_K)]
        before = jnp.where(i > 0, xb_ref[...], 0.0)
        last = i == nblk - 1
        after = jnp.where(last, 0.0, xa_ref[...])
        g_after = jnp.where(last, 0.0, ga_ref[...])
        ext = jnp.concatenate([before, x_ref[...], after], axis=0)
        y = _conv_pre(ext, wv, b_ref[...])[HALO:]
        sg = _sigmoid(y)
        dy = jnp.concatenate([g_ref[...], g_after], axis=0) * (sg * (1.0 + y * (1.0 - sg)))
        n = CONV_TM + HALO
        dx = wv[3] * dy
        for kk in range(1, CONV_K):
            dx = dx + wv[3 - kk] * pltpu.roll(dy, n - kk, 0)
        dx_ref[:, pl.ds(0, CONV_CH)] = dx[:CONV_TM].astype(BF16)
        dx_ref[:, pl.ds(CONV_CH, DT_PAD)] = ddt_ref[...].astype(BF16)
        dyc = dy[:CONV_TM]
        rows = [jnp.sum(dyc * (pltpu.roll(ext, 3 - j, 0) if j < 3 else ext)[HALO:HALO + CONV_TM], axis=0, keepdims=True)
                for j in range(CONV_K)]
        rows.append(jnp.sum(dyc, axis=0, keepdims=True))
        part = _rows_to_block(rows, 8, CONV_CH)

        @pl.when(i == 0)
        def _():
            dw_ref[...] = jnp.zeros_like(dw_ref)
        dw_ref[...] += part

    blk = pl.BlockSpec((CONV_TM, CONV_CH), lambda i: (i, 0))
    hb = pl.BlockSpec((HALO, CONV_CH), lambda i: (jnp.maximum(i * per - 1, 0), 0))
    ha = pl.BlockSpec((HALO, CONV_CH), lambda i: (jnp.minimum((i + 1) * per, T // HALO - 1), 0))
    return pl.pallas_call(
        body, name="conv_bwd", grid=(nblk,),
        in_specs=[blk, hb, ha, blk, ha, pl.BlockSpec((CONV_TM, DT_PAD), lambda i: (i, 0)),
                  pl.BlockSpec((CONV_K, CONV_CH), lambda i: (0, 0)), pl.BlockSpec((1, CONV_CH), lambda i: (0, 0))],
        out_specs=[pl.BlockSpec((CONV_TM, CONV_CH + DT_PAD), lambda i: (i, 0)), pl.BlockSpec((8, CONV_CH), lambda i: (0, 0))],
        out_shape=[jax.ShapeDtypeStruct((T, CONV_CH + DT_PAD), BF16), jax.ShapeDtypeStruct((8, CONV_CH), F32)],
        compiler_params=_cparams(("arbitrary",)),
    )(xbc, xbc, xbc, dact, dact, ddt, w, b)


def _pick(mat, h):
    lane = lax.broadcasted_iota(jnp.int32, mat.shape, 1)
    return jnp.sum(jnp.where(lane == h, mat, 0.0), axis=1, keepdims=True)


def _heads(fn):
    return jnp.stack([fn(h) for h in range(HEADS)])


def _ssd_prep(dt_ref, bias_ref, alog_ref, dsk_ref, b_ref, c_ref, xs_ref, state_ref, cst):
    li = lax.broadcasted_iota(jnp.int32, (CHUNK, CHUNK), 0)
    si = lax.broadcasted_iota(jnp.int32, (CHUNK, CHUNK), 1)
    tri = li >= si
    dtp = dt_ref[...] + bias_ref[...]
    dt = _softplus(dtp)
    A = -jnp.exp(alog_ref[...])
    a = dt * A
    cs = jnp.dot(tri.astype(F32), a, precision=HIGHEST, preferred_element_type=F32)
    cst[...] = cs.T
    Bm = b_ref[...].astype(BF16)
    Cm = c_ref[...].astype(BF16)
    cb = _nt(Cm, Bm)
    dskv = dsk_ref[...]
    cs_col = _heads(lambda h: _pick(cs, h))
    cs_row = _heads(lambda h: cst[pl.ds(h, 1), :])
    dt_col = _heads(lambda h: _pick(dt, h))
    dsk_col = _heads(lambda h: _pick(dskv, h))
    lam = jnp.exp(jnp.where(tri, cs_col - cs_row, NEG))
    x = _heads(lambda h: xs_ref[:, pl.ds(HD * h, HD)])
    xdt = x * dt_col
    prev = _heads(lambda h: state_ref[pl.ds(HD * h, HD), :])
    lane = lax.broadcasted_iota(jnp.int32, (1, 1, CHUNK), 2)
    cl = jnp.sum(jnp.where(lane == CHUNK - 1, cs_row, 0.0), axis=2, keepdims=True)
    f = jnp.exp(cl - cs_col)
    return dict(li=li, si=si, dtp=dtp, dt=dt, A=A, Bm=Bm, Cm=Cm, cb=cb, cs_col=cs_col, dt_col=dt_col, dsk_col=dsk_col,
                lam=lam, x=x, xdt=xdt, prev=prev, cl=cl, f=f)


def _ssd_fwd(xbcdt, conv_w, conv_b, bias, alog, dsk, qkvz, attn, gs):
    nc = T // CHUNK
    R = SSD_PER * CHUNK
    per = R // HALO

    def body(xbc_ref, halo_ref, cw_ref, cb_ref, dt_ref, bias_ref, alog_ref, dsk_ref, z_ref, at_ref, gs_ref,
             y_ref, st_ref, cat_ref, act_ref, state, cst):
        @pl.when(pl.program_id(0) == 0)
        def _():
            state[...] = jnp.zeros_like(state)
        for sub in range(SSD_PER):
            rows = pl.ds(sub * CHUNK, CHUNK)
            st_ref[sub] = state[...]
            halo = (jnp.where(pl.program_id(0) > 0, halo_ref[...], 0.0) if sub == 0
                    else xbc_ref[pl.ds(sub * CHUNK - HALO, HALO), :])
            one_chunk(halo, xbc_ref.at[rows, :], cw_ref, cb_ref, dt_ref.at[rows, :], bias_ref, alog_ref, dsk_ref,
                      z_ref.at[rows, :], at_ref.at[rows, :], gs_ref, y_ref.at[rows, :], cat_ref.at[rows, :],
                      act_ref.at[rows, :], state, cst)

    def one_chunk(halo, xbc_ref, cw_ref, cb_ref, dt_ref, bias_ref, alog_ref, dsk_ref, z_ref, at_ref, gs_ref,
                  y_ref, cat_ref, act_ref, state, cst):
        pre = _conv_pre(jnp.concatenate([halo, xbc_ref[...]], axis=0),
                        [cw_ref[pl.ds(j, 1), :] for j in range(CONV_K)], cb_ref[...])[HALO:]
        act_ref[...] = pre * _sigmoid(pre)
        xs_ref, b_ref, c_ref = (act_ref.at[:, pl.ds(0, AW)], act_ref.at[:, pl.ds(AW, NS)],
                                act_ref.at[:, pl.ds(AW + NS, NS)])
        s = _ssd_prep(dt_ref, bias_ref, alog_ref, dsk_ref, b_ref, c_ref, xs_ref, state, cst)
        Bm, Cm, prev = s["Bm"], s["Cm"], s["prev"]
        g = (s["cb"] * s["lam"]).astype(BF16)
        xdtb = s["xdt"].astype(BF16)
        prevb = prev.astype(BF16)
        y = _heads(lambda h: _nn(g[h], xdtb[h])) + _heads(lambda h: _nt(Cm, prevb[h])) * jnp.exp(s["cs_col"])
        y = y + s["dsk_col"] * s["x"]
        xf = (s["xdt"] * s["f"]).astype(BF16)
        new = prev * jnp.exp(s["cl"]) + _heads(lambda h: _tn(xf[h], Bm))
        for h in range(HEADS):
            y_ref[:, pl.ds(HD * h, HD)] = y[h]
            state[pl.ds(HD * h, HD), :] = new[h]
        z = z_ref[...]
        gi = y_ref[...] * (z * _sigmoid(z))
        cat_ref[:, pl.ds(0, AW)] = at_ref[...].astype(BF16)
        cat_ref[:, pl.ds(AW, AW)] = (gi * _rstd(gi) * gs_ref[...]).astype(BF16)

    vec = pl.BlockSpec((1, DT_PAD), lambda c: (0, 0))
    blk = pl.BlockSpec((R, AW), lambda c: (c, 0))
    return pl.pallas_call(
        body, name="ssd_fwd", grid=(nc // SSD_PER,),
        in_specs=[pl.BlockSpec((R, CONV_CH), lambda c: (c, 0)),
                  pl.BlockSpec((HALO, CONV_CH), lambda c: (jnp.maximum(c * per - 1, 0), 0)),
                  pl.BlockSpec((CONV_K, CONV_CH), lambda c: (0, 0)), pl.BlockSpec((1, CONV_CH), lambda c: (0, 0)),
                  pl.BlockSpec((R, DT_PAD), lambda c: (c, 6)),
                  vec, vec, vec, pl.BlockSpec((R, AW), lambda c: (c, 3)), blk, pl.BlockSpec((1, AW), lambda c: (0, 0))],
        out_specs=[blk, pl.BlockSpec((SSD_PER, AW, NS), lambda c: (c, 0, 0)), pl.BlockSpec((R, D), lambda c: (c, 0)),
                   pl.BlockSpec((R, CONV_CH), lambda c: (c, 0))],
        out_shape=[jax.ShapeDtypeStruct((T, AW), F32), jax.ShapeDtypeStruct((nc, AW, NS), F32),
                   jax.ShapeDtypeStruct((T, D), BF16), jax.ShapeDtypeStruct((T, CONV_CH), F32)],
        scratch_shapes=[pltpu.VMEM((AW, NS), F32), pltpu.VMEM((CHUNK, CHUNK), F32)],
        compiler_params=_cparams(("arbitrary",)),
    )(xbcdt, xbcdt, conv_w, conv_b, xbcdt, bias, alog, dsk, qkvz, attn, gs)


def _ssd_bwd(act, xbcdt, bias, alog, dsk, states, y_ssd, qkvz, dcat, gs):
    nc = T // CHUNK

    def body(xs_ref, b_ref, c_ref, dt_ref, bias_ref, alog_ref, dsk_ref, st_ref, y_ref, z_ref, dyn_ref, gs_ref,
             dact_ref, ddt_ref, par_ref, dz_ref, dgs_ref, dstate, cst, dy_ref):
        @pl.when(pl.program_id(0) == 0)
        def _():
            dstate[...] = jnp.zeros_like(dstate)
            par_ref[...] = jnp.zeros_like(par_ref)
            dgs_ref[...] = jnp.zeros_like(dgs_ref)
        for sub in reversed(range(SSD_PER)):
            rows = pl.ds(sub * CHUNK, CHUNK)
            one_chunk(xs_ref.at[rows, :], b_ref.at[rows, :], c_ref.at[rows, :], dt_ref.at[rows, :], bias_ref, alog_ref,
                      dsk_ref, st_ref.at[sub], y_ref.at[rows, :], z_ref.at[rows, :], dyn_ref.at[rows, :], gs_ref,
                      dact_ref.at[rows, :], ddt_ref.at[rows, :], par_ref, dz_ref.at[rows, :], dgs_ref, dstate, cst, dy_ref)

    def one_chunk(xs_ref, b_ref, c_ref, dt_ref, bias_ref, alog_ref, dsk_ref, st_ref, y_ref, z_ref, dyn_ref, gs_ref,
                  dact_ref, ddt_ref, par_ref, dz_ref, dgs_ref, dstate, cst, dy_ref):
        z, yv, dyn = z_ref[...], y_ref[...], dyn_ref[...]
        sg = _sigmoid(z)
        sz = z * sg
        gi = yv * sz
        rg = _rstd(gi)
        ng = gi * rg
        dgi = _rms_bwd(ng, rg, gs_ref[...], dyn)
        dy_ref[...] = dgi * sz
        dz_ref[...] = dgi * yv * (sg * (1.0 + z * (1.0 - sg)))
        dgs_ref[...] += _colsum(dyn * ng)
        s = _ssd_prep(dt_ref, bias_ref, alog_ref, dsk_ref, b_ref, c_ref, xs_ref, st_ref, cst)
        Bm, Cm, prev, lam, x, xdt, f, cl = s["Bm"], s["Cm"], s["prev"], s["lam"], s["x"], s["xdt"], s["f"], s["cl"]
        lane = lax.broadcasted_iota(jnp.int32, (1, DT_PAD), 1)
        row = lax.broadcasted_iota(jnp.int32, (1, CHUNK, 1), 1)
        g = s["cb"] * lam
        gb, xdtb, prevb = g.astype(BF16), xdt.astype(BF16), prev.astype(BF16)
        dy = _heads(lambda h: dy_ref[:, pl.ds(HD * h, HD)])
        dyb = dy.astype(BF16)
        dnew = _heads(lambda h: dstate[pl.ds(HD * h, HD), :])
        dnewb = dnew.astype(BF16)
        E = jnp.exp(s["cs_col"])
        ecl = jnp.exp(cl)
        dG = _heads(lambda h: _nt(dyb[h], xdtb[h]))
        dxdt = _heads(lambda h: _tn(gb[h], dyb[h]))
        Yo = _heads(lambda h: _nt(Cm, prevb[h]))
        W = _heads(lambda h: _nt(Bm, dnewb[h]))
        dcb = jnp.sum(dG * lam, axis=0)
        Mm = dG * g
        col_sums = jnp.sum(Mm, axis=1, keepdims=True)
        dYo = (dy * E).astype(BF16)
        dxdt = dxdt + W * f
        dF = jnp.sum(W * xdt, axis=2, keepdims=True) * f
        dcl = jnp.sum(dnew * prev, axis=(1, 2), keepdims=True) * ecl + jnp.sum(dF, axis=1, keepdims=True)
        dcs = (jnp.sum(Mm, axis=2, keepdims=True) + jnp.sum(dy * Yo, axis=2, keepdims=True) * E - dF
               + jnp.where(row == CHUNK - 1, dcl, 0.0))
        ddt_x = jnp.sum(dxdt * x, axis=2, keepdims=True)
        dD = jnp.sum(dy * x, axis=(1, 2), keepdims=True)
        dx = s["dsk_col"] * dy + dxdt * s["dt_col"]
        xfb = (xdt * f).astype(BF16)
        dprev = _heads(lambda h: _tn(dYo[h], Cm)) + dnew * ecl
        dcbb = dcb.astype(BF16)
        dC = _nn(dcbb, Bm)
        dB = _tn(dcbb, Cm)
        dcs_mat = -_rows_to_block([col_sums[h] for h in range(HEADS)], CHUNK, CHUNK).T
        ddt_mat = jnp.zeros((CHUNK, DT_PAD), F32)
        dD_row = jnp.zeros((1, DT_PAD), F32)
        for h in range(HEADS):
            sl = pl.ds(HD * h, HD)
            dC = dC + _nn(dYo[h], prevb[h])
            dB = dB + _nn(xfb[h], dnewb[h])
            dcs_mat = dcs_mat + jnp.where(lane == h, dcs[h], 0.0)
            ddt_mat = ddt_mat + jnp.where(lane == h, ddt_x[h], 0.0)
            dD_row = dD_row + jnp.where(lane == h, dD[h], 0.0)
            dact_ref[:, sl] = dx[h]
            dstate[sl, :] = dprev[h]
        dact_ref[:, pl.ds(AW, NS)] = dB
        dact_ref[:, pl.ds(AW + NS, NS)] = dC
        da = jnp.dot((s["li"] <= s["si"]).astype(F32), dcs_mat, precision=HIGHEST, preferred_element_type=F32)
        ddtp = jnp.where(lane < HEADS, (ddt_mat + da * s["A"]) * _sigmoid(s["dtp"]), 0.0)
        ddt_ref[...] = ddtp
        dalog = jnp.where(lane < HEADS, jnp.sum(da * s["dt"], axis=0, keepdims=True) * s["A"], 0.0)
        par_ref[...] += _rows_to_block([jnp.sum(ddtp, axis=0, keepdims=True), dalog, dD_row], 8, DT_PAD)

    vec = pl.BlockSpec((1, DT_PAD), lambda c: (0, 0))
    nstep = nc // SSD_PER
    rev = lambda c: nstep - 1 - c
    R = SSD_PER * CHUNK
    return pl.pallas_call(
        body, name="ssd_bwd", grid=(nstep,),
        in_specs=[pl.BlockSpec((R, AW), lambda c: (rev(c), 0)), pl.BlockSpec((R, NS), lambda c: (rev(c), 4)),
                  pl.BlockSpec((R, NS), lambda c: (rev(c), 5)), pl.BlockSpec((R, DT_PAD), lambda c: (rev(c), 6)),
                  vec, vec, vec,
                  pl.BlockSpec((SSD_PER, AW, NS), lambda c: (rev(c), 0, 0)), pl.BlockSpec((R, AW), lambda c: (rev(c), 0)),
                  pl.BlockSpec((R, AW), lambda c: (rev(c), 3)), pl.BlockSpec((R, AW), lambda c: (rev(c), 1)),
                  pl.BlockSpec((1, AW), lambda c: (0, 0))],
        out_specs=[pl.BlockSpec((R, CONV_CH), lambda c: (rev(c), 0)), pl.BlockSpec((R, DT_PAD), lambda c: (rev(c), 0)),
                   pl.BlockSpec((8, DT_PAD), lambda c: (0, 0)), pl.BlockSpec((R, AW), lambda c: (rev(c), 0)),
                   pl.BlockSpec((1, AW), lambda c: (0, 0))],
        out_shape=[jax.ShapeDtypeStruct((T, CONV_CH), F32), jax.ShapeDtypeStruct((T, DT_PAD), F32),
                   jax.ShapeDtypeStruct((8, DT_PAD), F32), jax.ShapeDtypeStruct((T, AW), F32),
                   jax.ShapeDtypeStruct((1, AW), F32)],
        scratch_shapes=[pltpu.VMEM((AW, NS), F32), pltpu.VMEM((CHUNK, CHUNK), F32), pltpu.VMEM((CHUNK, AW), F32)],
        compiler_params=_cparams(("arbitrary",)),
    )(act, act, act, xbcdt, bias, alog, dsk, states, y_ssd, qkvz, dcat, gs)


def _place():
    return lax.axis_index("x"), lax.axis_index("y"), lax.axis_index("c")


def _slot(px, py, pc):
    return 4 * px + 2 * py + pc


SLAB_ROWS = 24


def _slab_pack(parts, name):
    n = len(parts)

    def body(*refs):
        slab = refs[n]
        slab[...] = jnp.zeros_like(slab)
        for ref, (arr, row) in zip(refs[:n], parts):
            slab[pl.ds(row, arr.shape[0]), pl.ds(0, arr.shape[1])] = ref[...]

    vm = pl.BlockSpec(memory_space=pltpu.VMEM)
    return pl.pallas_call(
        body, name=name, in_specs=[vm] * n, out_specs=vm, out_shape=jax.ShapeDtypeStruct((SLAB_ROWS, D), F32),
    )(*[a for a, _ in parts])


_HBM = pl.BlockSpec(memory_space=pltpu.HBM)
_SEM = pl.BlockSpec(memory_space=pltpu.SEMAPHORE)
_EFFECT = pltpu.SideEffectType.DATAFLOW_SIDE_EFFECTING


def _peers(x, y, c):
    out = []
    for kk in range(1, N_DEV):
        fx, fy, fc = kk >> 2 & 1, kk >> 1 & 1, kk & 1
        out.append((1 - x if fx else x, 1 - y if fy else y, 1 - c if fc else c))
    return out


def _send_start(src, per_peer, name, dep):
    (handles, token) = _send_start_many([src], per_peer, name, dep)
    return handles, token


def _near_peers(x, y, c):
    return [(x, y, 1 - c), (1 - x, y, c), (x, 1 - y, c), (1 - x, 1 - y, c)]


def _send_start_many(srcs, per_peer, name, dep, peers=_peers, npeers=N_DEV - 1):
    n = len(srcs)

    def body(*refs):
        src_refs, land_refs = refs[:n], refs[n:2 * n]
        send_sems, recv_sems = refs[2 * n + 1], refs[2 * n + 2]
        token = refs[-1]
        x, y, c = _place()
        mine = _slot(x, y, c)
        for a in range(n):
            for kk, peer in enumerate(peers(x, y, c)):
                pltpu.make_async_remote_copy(
                    src_ref=src_refs[a].at[_slot(*peer)] if per_peer else src_refs[a], dst_ref=land_refs[a].at[mine],
                    send_sem=send_sems.at[a * npeers + kk], recv_sem=recv_sems.at[a * npeers + kk],
                    device_id=peer, device_id_type=MESH).start()
        token[...] = jnp.zeros_like(token)

    lands = [lax.empty((N_DEV,) + tuple(s.shape[1:] if per_peer else s.shape), s.dtype) for s in srcs]
    hbm = lambda t: pltpu.with_memory_space_constraint(t, pltpu.HBM)
    outs = pl.pallas_call(
        body, name=name,
        out_shape=(pltpu.SemaphoreType.DMA((n * npeers,)), pltpu.SemaphoreType.DMA((n * npeers,)),
                   *[pltpu.HBM(s.shape, s.dtype) for s in srcs], *[pltpu.HBM(l.shape, l.dtype) for l in lands],
                   jax.ShapeDtypeStruct((8, 128), F32)),
        in_specs=(*[_HBM] * (2 * n), _ANY),
        out_specs=(_SEM, _SEM, *[_HBM] * (2 * n), pl.BlockSpec(memory_space=pltpu.VMEM)),
        input_output_aliases={i: 2 + i for i in range(2 * n)},
        compiler_params=pltpu.CompilerParams(has_side_effects=_EFFECT),
    )(*[hbm(s) for s in srcs], *[hbm(l) for l in lands], dep)
    return (outs[0], outs[1], list(outs[2:2 + n]), list(outs[2 + n:2 + 2 * n])), outs[-1]


def _send_wait(handles, after, name):
    srcs, lands = _send_wait_many(handles, after, name)
    return srcs[0], lands[0]


def _send_wait_many(handles, after, name, npeers=N_DEV - 1):
    send_sems, recv_sems, src_thrus, land_thrus = handles
    n = len(src_thrus)

    def body(*refs):
        land_refs = refs[n:2 * n]
        send_sems, recv_sems = refs[2 * n], refs[2 * n + 1]
        me = _place()
        for a in range(n):
            for kk in range(npeers):
                cp = pltpu.make_async_remote_copy(
                    src_ref=land_refs[a].at[0], dst_ref=land_refs[a].at[0],
                    send_sem=send_sems.at[a * npeers + kk], recv_sem=recv_sems.at[a * npeers + kk],
                    device_id=me, device_id_type=MESH)
                cp.wait_send()
                cp.wait_recv()

    both = list(src_thrus) + list(land_thrus)
    outs = pl.pallas_call(
        body, name=name,
        out_shape=tuple(pltpu.HBM(t.shape, t.dtype) for t in both),
        in_specs=(*[_HBM] * (2 * n), _SEM, _SEM, _ANY), out_specs=tuple([_HBM] * (2 * n)),
        input_output_aliases={i: i for i in range(2 * n)},
        compiler_params=pltpu.CompilerParams(has_side_effects=_EFFECT),
    )(*both, send_sems, recv_sems, after)
    return list(outs[:n]), list(outs[n:])


def _forward_start(lands, name, dep):
    n = len(lands)

    def body(*refs):
        land_refs = refs[:n]
        send_sems, recv_sems = refs[n + 1], refs[n + 2]
        token = refs[-1]
        x, y, c = _place()
        for a in range(n):
            for j, chip in enumerate([(1 - x, y), (x, 1 - y), (1 - x, 1 - y)]):
                blk = land_refs[a].at[_slot(*chip, c)]
                pltpu.make_async_remote_copy(
                    src_ref=blk, dst_ref=blk, send_sem=send_sems.at[a * 3 + j], recv_sem=recv_sems.at[a * 3 + j],
                    device_id=(x, y, 1 - c), device_id_type=MESH).start()
        token[...] = jnp.zeros_like(token)

    outs = pl.pallas_call(
        body, name=name,
        out_shape=(pltpu.SemaphoreType.DMA((n * 3,)), pltpu.SemaphoreType.DMA((n * 3,)),
                   *[pltpu.HBM(l.shape, l.dtype) for l in lands], jax.ShapeDtypeStruct((8, 128), F32)),
        in_specs=(*[_HBM] * n, _ANY), out_specs=(_SEM, _SEM, *[_HBM] * n, pl.BlockSpec(memory_space=pltpu.VMEM)),
        input_output_aliases={i: 2 + i for i in range(n)},
        compiler_params=pltpu.CompilerParams(has_side_effects=_EFFECT),
    )(*lands, dep)
    return (outs[0], outs[1], list(outs[2:2 + n])), outs[-1]


def _forward_wait(handles, after, name):
    send_sems, recv_sems, land_thrus = handles
    n = len(land_thrus)

    def body(*refs):
        land_refs = refs[:n]
        send_sems, recv_sems = refs[n], refs[n + 1]
        me = _place()
        for a in range(n):
            for j in range(3):
                cp = pltpu.make_async_remote_copy(
                    src_ref=land_refs[a].at[0], dst_ref=land_refs[a].at[0],
                    send_sem=send_sems.at[a * 3 + j], recv_sem=recv_sems.at[a * 3 + j], device_id=me, device_id_type=MESH)
                cp.wait_send()
                cp.wait_recv()

    outs = pl.pallas_call(
        body, name=name,
        out_shape=tuple(pltpu.HBM(t.shape, t.dtype) for t in land_thrus),
        in_specs=(*[_HBM] * n, _SEM, _SEM, _ANY), out_specs=tuple([_HBM] * n),
        input_output_aliases={i: i for i in range(n)},
        compiler_params=pltpu.CompilerParams(has_side_effects=_EFFECT),
    )(*land_thrus, send_sems, recv_sems, after)
    return list(outs)


def _sum_slots(land, name):
    _, R, C = land.shape
    tm = R if R <= 512 else 512

    def body(x_ref, o_ref):
        acc = x_ref[0].astype(F32)
        for j in range(1, N_DEV):
            acc = acc + x_ref[j].astype(F32)
        o_ref[...] = acc

    return pl.pallas_call(
        body, name=name, grid=(R // tm,),
        in_specs=[pl.BlockSpec((N_DEV, tm, C), lambda i: (0, i, 0))], out_specs=pl.BlockSpec((tm, C), lambda i: (i, 0)),
        out_shape=jax.ShapeDtypeStruct((R, C), F32), compiler_params=_cparams(("parallel",)),
    )(land)


def _adam_math(w, g, m, v):
    m2 = ADAM_B1 * m + (1.0 - ADAM_B1) * g
    v2 = ADAM_B2 * v + (1.0 - ADAM_B2) * (g * g)
    m_hat = m2 / (1.0 - ADAM_B1 ** ADAM_STEP)
    v_hat = v2 / (1.0 - ADAM_B2 ** ADAM_STEP)
    delta = -ADAM_LR * (m_hat / (jnp.sqrt(v_hat) + ADAM_EPS) + ADAM_WD * w)
    return delta, m2, v2


def _adamw(w, g, m, v, name):
    R, C = w.shape
    tm = R if R <= 512 else 256
    return _rowwise(lambda w, g, m, v: (_adam_math(w, g, m, v), ()), [w, g, m, v], [], [(C, F32)] * 3, [], tm=tm, name=name)


def _adamw_small(slab, slab_rows, g_conv_w, ws, ms, vs):
    n = len(ws)

    def body(*refs):
        slab_ref, gc_ref = refs[0], refs[1]
        w_refs, m_refs, v_refs = refs[2:2 + n], refs[2 + n:2 + 2 * n], refs[2 + 2 * n:2 + 3 * n]
        outs = refs[2 + 3 * n:]
        loss_ref = outs[0]
        g_out, d_out, m_out, v_out = (outs[1 + i * n:1 + (i + 1) * n] for i in range(4))
        loss_ref[...] = jnp.sum(slab_ref[pl.ds(6, 1), :], axis=1, keepdims=True)
        for i in range(n):
            g = gc_ref[...] if i == n - 1 else slab_ref[pl.ds(slab_rows[i], 1), pl.ds(0, ws[i].shape[1])]
            d, m2, v2 = _adam_math(w_refs[i][...], g, m_refs[i][...], v_refs[i][...])
            g_out[i][...] = g
            d_out[i][...] = d
            m_out[i][...] = m2
            v_out[i][...] = v2

    vm = pl.BlockSpec(memory_space=pltpu.VMEM)
    shapes = [jax.ShapeDtypeStruct(w.shape, F32) for w in ws]
    outs = pl.pallas_call(
        body, name="adamw_small", in_specs=[vm] * (2 + 3 * n), out_specs=[vm] * (1 + 4 * n),
        out_shape=[jax.ShapeDtypeStruct((1, 1), F32)] + shapes * 4,
    )(slab, g_conv_w, *ws, *ms, *vs)
    return outs[0], outs[1:1 + n], outs[1 + n:1 + 2 * n], outs[1 + 2 * n:1 + 3 * n], outs[1 + 3 * n:]


SMALL = ["norm_mix_pre", "norm_mix_post", "norm_mlp_pre", "norm_mlp_post", "norm_ple_post",
         "conv_b", "ssd_norm_g", "dt_bias", "a_log", "d_skip"]


def _pad_row(v, width=D):
    return jnp.pad(v, ((0, 0), (0, width - v.shape[1])))


def kernel(x, p, positions, norm_mix_pre, norm_mix_post, w_in, conv_w, conv_b, dt_bias, a_log, d_skip, ssd_norm_g, w_out, norm_mlp_pre, norm_mlp_post, w_up, w_down, w_ple_gate, w_ple_proj, norm_ple_post, loss_target, m_norm_mix_pre, m_norm_mix_post, m_w_in, m_conv_w, m_conv_b, m_dt_bias, m_a_log, m_d_skip, m_ssd_norm_g, m_w_out, m_norm_mlp_pre, m_norm_mlp_post, m_w_up, m_w_down, m_w_ple_gate, m_w_ple_proj, m_norm_ple_post, v_norm_mix_pre, v_norm_mix_post, v_w_in, v_conv_w, v_conv_b, v_dt_bias, v_a_log, v_d_skip, v_ssd_norm_g, v_w_out, v_norm_mlp_pre, v_norm_mlp_post, v_w_up, v_w_down, v_w_ple_gate, v_w_ple_proj, v_norm_ple_post):
    args = dict(locals())
    x2, p2, tgt = x[0], p[0, 0], loss_target[0]
    g1, g2, g3, g4, g5 = norm_mix_pre, norm_mix_post, norm_mlp_pre, norm_mlp_post, norm_ple_post

    me = _slot(*_place())
    pack_in = jnp.pad(w_in[0].T, ((0, W_IN_SHARD_PAD - W_IN_SHARD), (0, 0))).astype(BF16)
    rest = [w_out[0].astype(BF16), w_up[0].T.astype(BF16), w_down[0].astype(BF16), w_ple_gate[0].astype(BF16),
            w_ple_proj[0].T.reshape(32, D).astype(BF16)]
    conv_pack = jnp.pad(conv_w[0], ((0, 4), (0, 32)))
    in_handles, tok_in0 = _send_start_many([pack_in, conv_pack], False, "gather_in_start", g1, peers=_near_peers, npeers=4)

    inv_freq = ROPE_THETA ** (-jnp.arange(HD // 2, dtype=F32) * 2.0 / HD)
    pos = positions[0] + tok_in0[0, 0].astype(jnp.int32)
    ang = pos.astype(F32)[:, None] * inv_freq
    cos, sin = jnp.cos(ang), jnp.sin(ang)
    cos128 = jnp.concatenate([cos, cos, cos, cos], axis=1)
    sin128 = jnp.concatenate([-sin, sin, -sin, sin], axis=1)

    bias_w, alog_w, dsk_w = _pad_row(dt_bias, DT_PAD), _pad_row(a_log, DT_PAD), _pad_row(d_skip, DT_PAD)

    (u1,) = _rowwise(lambda a, g: ((a * _rstd(a) * g,), ()), [x2], [g1], [(D, BF16)], [], tm=512, name="norm_x",
                     deps=[cos128, sin128])
    p2b = p2.astype(BF16)

    in_back, in_land = _send_wait_many(in_handles, u1, "gather_in_wait", npeers=4)
    fw_handles, tok_fw = _forward_start(in_land, "gather_in_forward", u1)
    in_land = _forward_wait(fw_handles, tok_fw, "gather_in_forward_wait")
    gin = lax.dynamic_update_slice(in_land[0], in_back[0][None], (me, 0, 0))
    gconv = lax.dynamic_update_slice(in_land[1], in_back[1][None], (me, 0, 0))
    rest_handles, tok_rest = _send_start_many(rest, False, "gather_rest_start", gconv)
    w_inT = gin[:, :W_IN_SHARD].reshape(IN_W, D)
    w_qkvzT = w_inT[:4 * AW]
    w_xbcdtT = jnp.pad(w_inT[4 * AW:], ((0, DT_PAD - HEADS), (0, 0)))
    conv_full = gconv[:, :CONV_K, :96].transpose(1, 0, 2).reshape(CONV_K, CONV_CH)
    qkvz, xbcdt = _mm_rows(lambda a, b: ((a, b), ()), [(u1, w_qkvzT, True), (u1, w_xbcdtT, True)], [], [],
                           [(4 * AW, F32), (CONV_CH + DT_PAD, F32)], [], tm=512, name="proj_in", deps=[tok_rest])

    qkv = _rope_fwd(qkvz, cos128, sin128)
    qkv = [qkv[3 * i:3 * i + 3] for i in range(len(DILATIONS))]
    outs, lses = [], []
    for d, (qd, kd, vd) in zip(DILATIONS, qkv):
        o, l = _attn_fwd(qd, kd, vd, d)
        outs.append(o)
        lses.append(l)
    attn, lse, attn4, lse4, attn16, lse16 = _attn_merge(outs, lses)

    y_ssd, states, cat, act = _ssd_fwd(xbcdt, conv_full, conv_b, bias_w, alog_w, dsk_w, qkvz, attn, ssd_norm_g)


    rest_back, landed = _send_wait_many(rest_handles, cat, "gather_rest_wait")
    landed = [lax.dynamic_update_slice(l, b[None], (me, 0, 0)) for l, b in zip(landed, rest_back)]
    w_o, w_upT, w_dn, w_gate = landed[0].reshape(D, D), landed[1].reshape(DFF, D), landed[2].reshape(DFF, D), landed[3].reshape(D, D)
    w_projT = landed[4].reshape(D, PLE)

    def post1(mm, xx, ga):
        h = xx + mm * _rstd(mm) * ga
        return (mm, h, _rstd(h)), ()
    mix, h1, r3 = _mm_rows(post1, [(cat, w_o, False)], [x2], [g2], [(D, F32), (D, F32), (1, F32)], [], tm=512,
                           name="mix_out")

    a_up, ff, u2, h2, h2b = _mlp_fwd(h1, r3, g3, w_upT, w_dn, g4)
    relu2 = lambda a: jnp.square(jnp.maximum(a.astype(F32), 0.0))

    def final(gpre, ppv, hh, tg, g):
        sg = _sigmoid(gpre)
        ple = ppv * sg
        r = _rstd(ple)
        n = ple * r
        h3 = hh + n * g
        e = h3 - tg
        dh3 = e * (1.0 / D)
        dple = _rms_bwd(n, r, g, dh3)
        return (dh3, dple * sg, dple * ppv * sg * (1.0 - sg)), (_colsum(dh3 * n), _colsum(0.5 * e * e * (1.0 / D)))
    dh3, dpp, dgp, dg5, loss_vec = _mm_rows(final, [(h2b, w_gate, False), (p2b, w_projT, True)], [h2, tgt], [g5],
                                            [(D, F32), (D, BF16), (D, BF16)], [(1, D), (1, D)], tm=512, name="ple_loss")

    gw_projT = _mm(dpp, p2b, ta=True, tm=512, tn=256, tk=T, out_dtypes=(BF16,), name="gw_ple_proj")
    gw_gate = _mm(h2b, dgp, ta=True, tm=512, tn=1024, tk=T, out_dtypes=(BF16,), name="gw_ple_gate")
    def bwd_mlp_post(dg_, d3, f, g):
        dh2 = d3 + dg_
        r = _rstd(f)
        n = f * r
        return (dh2, _rms_bwd(n, r, g, dh2)), (_colsum(dh2 * n),)
    dh2, dff, dg4 = _mm_rows(bwd_mlp_post, [(dgp, w_gate, True)], [dh3, ff], [g4], [(D, F32), (D, BF16)], [(1, D)],
                             tm=512, name="bwd_ple_gate")

    gw_dn = _mm(a_up, dff, ta=True, tm=512, tn=1024, tk=T, a_pre=relu2, out_dtypes=(BF16,), name="gw_mlp_down")
    rs_a, tok_a = _send_start_many([gw_projT.reshape(N_DEV, 32, D), gw_gate.reshape(N_DEV, 128, D),
                                    gw_dn.reshape(N_DEV, 512, D)], True, "rs_start_a", g1)
    da_up, du2 = _mlp_dx(dff, a_up, w_upT, w_dn, tok_a)
    gw_upT = _mm(da_up, u2, ta=True, tm=512, tn=1024, tk=T, out_dtypes=(BF16,), name="gw_mlp_up")

    def bwd_mix_post(d2, du, hh, rr, mm, ga, gb):
        n3 = hh * rr
        dh1 = d2 + _rms_bwd(n3, rr, gb, du)
        r = _rstd(mm)
        n2 = mm * r
        return (dh1, _rms_bwd(n2, r, ga, dh1)), (_colsum(du * n3), _colsum(dh1 * n2))
    dh1, dmix, dg3, dg2 = _rowwise(bwd_mix_post, [dh2, du2, h1, r3, mix], [g2, g3], [(D, F32), (D, BF16)],
                                   [(1, D), (1, D)], tm=512, name="bwd_post_mix")

    gw_o = _mm(cat, dmix, ta=True, tm=512, tn=1024, tk=T, out_dtypes=(BF16,), name="gw_out")
    rs_b, tok_b = _send_start_many([gw_upT.reshape(N_DEV, 512, D), gw_o.reshape(N_DEV, 128, D)], True, "rs_start_b", g1)
    dcat, dattn4, dattn16 = _dx_out(dmix, w_o, tok_b)

    dact, ddtw, ssd_par, dz, dgs = _ssd_bwd(act, xbcdt, bias_w, alog_w, dsk_w, states, y_ssd, qkvz, dcat, ssd_norm_g)
    dxbcdt, conv_par = _conv_bwd(xbcdt, dact, ddtw, conv_full, conv_b)

    qkv_grads = [_attn_bwd(*qkv[0], dcat, attn, lse, 1),
                 _attn_bwd(*qkv[1], dattn4, attn4, lse4, 4),
                 _attn_bwd(*qkv[2], dattn16, attn16, lse16, 16)]
    dqkvz = _rope_bwd(qkv_grads, dz, cos128, sin128)

    gw_qkvzT = _mm(dqkvz, u1, ta=True, tm=512, tn=1024, tk=T, out_dtypes=(BF16,), name="gw_qkvz")
    gw_xbcdtT = _mm(dxbcdt, u1, ta=True, tm=896, tn=1024, tk=T, out_dtypes=(BF16,), name="gw_xbcdt")
    gw_inT = jnp.concatenate([gw_qkvzT, gw_xbcdtT], axis=0)[:IN_W]
    gw_inT = jnp.pad(gw_inT.reshape(N_DEV, W_IN_SHARD, D), ((0, 0), (0, W_IN_SHARD_PAD - W_IN_SHARD), (0, 0)))
    rs_in, tok_in = _send_start(gw_inT, True, "rs_start_w_in", g1)

    def bwd_in(ua, ub, d1, xx, g):
        rr = _rstd(xx)
        n = xx * rr
        du = ua + ub
        return (d1 + _rms_bwd(n, rr, g, du),), (_colsum(du * n),)
    grad_x, dg1 = _mm_rows(bwd_in, [(dqkvz, w_qkvzT, False), (dxbcdt, w_xbcdtT, False)], [dh1, x2], [g1],
                           [(D, F32)], [(1, D)], tm=512, name="bwd_in_proj", deps=[tok_in])

    my_slab = _slab_pack([(dg1, 0), (dg2, 1), (dg3, 2), (dg4, 3), (dg5, 4), (dgs, 5), (loss_vec, 6),
                          (conv_par, 8), (ssd_par, 16)], "slab_pack")
    slab_handles, tok_slab = _send_start_many([my_slab], False, "slab_start", g1)

    def scatter_finish(handles, nm, after):
        part, land = _send_wait(handles, after, "rs_wait_" + nm)
        own = lax.dynamic_slice(part, (me, 0, 0), (1,) + part.shape[1:])
        return _sum_slots(lax.dynamic_update_slice(land, own, (me, 0, 0)), "rs_sum_" + nm)
    def scatter_finish_many(handles, names, after, wait_name):
        parts, lands = _send_wait_many(handles, after, wait_name)
        return [_sum_slots(lax.dynamic_update_slice(land, lax.dynamic_slice(part, (me, 0, 0), (1,) + part.shape[1:]),
                                                    (me, 0, 0)), "rs_sum_" + nm)
                for part, land, nm in zip(parts, lands, names)]
    g_projT, g_gate, g_dn = scatter_finish_many(rs_a, ("w_proj", "w_gate", "w_down"), tok_slab, "rs_wait_a")
    g_upT, g_out = scatter_finish_many(rs_b, ("w_up", "w_out"), tok_slab, "rs_wait_b")

    grads = {
        "w_out": g_out[None], "w_up": g_upT.T[None], "w_down": g_dn[None],
        "w_ple_gate": g_gate[None], "w_ple_proj": g_projT.reshape(128, PLE).T[None],
    }
    delta, new_m, new_v = {}, {}, {}
    for nme in ["w_out", "w_up", "w_down", "w_ple_gate", "w_ple_proj", "w_in"]:
        if nme == "w_in":
            g_inT = scatter_finish(rs_in, "w_in", delta["w_down"])
            grads["w_in"] = g_inT[:W_IN_SHARD].T[None]
        dl, mm_, vv_ = _adamw(args[nme][0], grads[nme][0], args["m_" + nme][0], args["v_" + nme][0], "adamw_" + nme)
        delta[nme], new_m[nme], new_v[nme] = dl[None], mm_[None], vv_[None]

    slab_back, slab_land = _send_wait_many(slab_handles, delta["w_in"], "slab_wait")
    slab = _sum_slots(lax.dynamic_update_slice(slab_land[0], slab_back[0][None], (me, 0, 0)), "slab_sum")
    g_conv_w = lax.dynamic_slice(slab[8:12, :CONV_CH], (0, me * 96), (CONV_K, 96))
    small_names = SMALL + ["conv_w"]
    small_rows = [0, 1, 2, 3, 4, 12, 5, 16, 17, 18, None]
    pick = lambda prefix: [args[prefix + nme] for nme in SMALL] + [args[prefix + "conv_w"][0]]
    loss11, g_s, d_s, m_s, v_s = _adamw_small(slab, small_rows, g_conv_w, pick(""), pick("m_"), pick("v_"))
    loss = loss11[0, 0]
    for i, nme in enumerate(small_names):
        lead = (lambda t: t[None]) if nme == "conv_w" else (lambda t: t)
        grads[nme], delta[nme], new_m[nme], new_v[nme] = lead(g_s[i]), lead(d_s[i]), lead(m_s[i]), lead(v_s[i])

    order = ["norm_mix_pre", "norm_mix_post", "w_in", "conv_w", "conv_b", "dt_bias", "a_log", "d_skip", "ssd_norm_g",
             "w_out", "norm_mlp_pre", "norm_mlp_post", "w_up", "w_down", "w_ple_gate", "w_ple_proj", "norm_ple_post"]
    return (loss, grad_x[None], *[grads[n] for n in order], *[delta[n] for n in order],
            *[new_m[n] for n in order], *[new_v[n] for n in order])
```

```python
import jax
import jax.numpy as jnp
from jax import lax
from jax.experimental import pallas as pl
from jax.experimental.pallas import tpu as pltpu

F32 = jnp.float32
BF16 = jnp.bfloat16
MESH = pl.DeviceIdType.MESH
HIGHEST = lax.Precision.HIGHEST

N_DEV = 8
T = 4096
D = 1024
HEADS = 8
HD = 64
AW = 512
NS = 128
CONV_K = 4
CONV_CH = 768
CHUNK = 128
SSD_PER = 2
DFF = 4096
PLE = 256
EPS = 1e-6
ROPE_THETA = 10000.0
DILATIONS = (1, 4, 16)
QBLK = 128
NEG = -1e30
IN_W = 2824
W_IN_SHARD = 353
W_IN_SHARD_PAD = 384
DT_PAD = 128

ADAM_LR, ADAM_B1, ADAM_B2, ADAM_EPS, ADAM_WD, ADAM_STEP = 0.001, 0.9, 0.999, 1e-08, 0.01, 10

VMEM_LIMIT = 56 * 1024 * 1024


_ANY = pl.BlockSpec(memory_space=pl.ANY)


def _cparams(sem=None):
    return pltpu.CompilerParams(dimension_semantics=sem, vmem_limit_bytes=VMEM_LIMIT)


def _dot(a, b, ca, cb, precision=None):
    return lax.dot_general(a, b, (((ca,), (cb,)), ((), ())), preferred_element_type=F32, precision=precision)


def _nn(a, b):
    return _dot(a, b, 1, 0)


def _nt(a, b):
    return _dot(a, b, 1, 1)


def _tn(a, b):
    return _dot(a, b, 0, 0)


def _sigmoid(x):
    return 1.0 / (1.0 + jnp.exp(-x))


def _softplus(x):
    return jnp.maximum(x, 0.0) + jnp.log(1.0 + jnp.exp(-jnp.abs(x)))


def _mm(a, b, *, ta=False, tb=False, tm, tn, tk, name,
        a_pre=None, a_rows=(), a_cols=(), b_pre=None, b_rows=(), b_cols=(),
        epi=None, epi_tiles=(), out_dtypes=(F32,), deps=()):
    if ta:
        K, M = a.shape
    else:
        M, K = a.shape
    if tb:
        N, K2 = b.shape
    else:
        K2, N = b.shape
    assert K == K2 and M % tm == 0 and N % tn == 0 and K % tk == 0, (name, a.shape, b.shape)
    nk = K // tk
    if ta:
        a_spec = pl.BlockSpec((tk, tm), lambda i, j, k: (k, i))
        a_row_specs = [pl.BlockSpec((tk, 1), lambda i, j, k: (k, 0)) for _ in a_rows]
        a_col_specs = [pl.BlockSpec((1, tm), lambda i, j, k: (0, i)) for _ in a_cols]
    else:
        a_spec = pl.BlockSpec((tm, tk), lambda i, j, k: (i, k))
        a_row_specs = [pl.BlockSpec((tm, 1), lambda i, j, k: (i, 0)) for _ in a_rows]
        a_col_specs = [pl.BlockSpec((1, tk), lambda i, j, k: (0, k)) for _ in a_cols]
    if tb:
        b_spec = pl.BlockSpec((tn, tk), lambda i, j, k: (j, k))
        b_row_specs = [pl.BlockSpec((tn, 1), lambda i, j, k: (j, 0)) for _ in b_rows]
        b_col_specs = [pl.BlockSpec((1, tk), lambda i, j, k: (0, k)) for _ in b_cols]
    else:
        b_spec = pl.BlockSpec((tk, tn), lambda i, j, k: (k, j))
        b_row_specs = [pl.BlockSpec((tk, 1), lambda i, j, k: (k, 0)) for _ in b_rows]
        b_col_specs = [pl.BlockSpec((1, tn), lambda i, j, k: (0, j)) for _ in b_cols]
    o_spec = pl.BlockSpec((tm, tn), lambda i, j, k: (i, j))
    na, nb, ne, no = len(a_rows) + len(a_cols), len(b_rows) + len(b_cols), len(epi_tiles), len(out_dtypes)

    def body(*refs):
        a_ref, b_ref = refs[0], refs[1]
        a_ex = refs[2:2 + na]
        b_ex = refs[2 + na:2 + na + nb]
        e_ex = refs[2 + na + nb:2 + na + nb + ne]
        first_out = 2 + na + nb + ne + len(deps)
        outs = refs[first_out:first_out + no]

        def finish(res):
            vals = epi(res, *[r[...] for r in e_ex]) if epi is not None else (res,)
            for o_ref, val in zip(outs, vals):
                o_ref[...] = val.astype(o_ref.dtype)

        at = a_ref[...]
        if a_pre is not None:
            at = a_pre(at, *[r[...] for r in a_ex])
        bt = b_ref[...]
        if b_pre is not None:
            bt = b_pre(bt, *[r[...] for r in b_ex])
        prod = _dot(at.astype(BF16), bt.astype(BF16), 0 if ta else 1, 1 if tb else 0)
        if nk == 1:
            finish(prod)
            return
        acc = refs[-1]
        k = pl.program_id(2)

        @pl.when(k == 0)
        def _():
            acc[...] = jnp.zeros_like(acc)
        acc[...] += prod

        @pl.when(k == nk - 1)
        def _():
            finish(acc[...])

    outs = pl.pallas_call(
        body, name=name,
        grid=(M // tm, N // tn, nk),
        in_specs=([a_spec, b_spec] + a_row_specs + a_col_specs + b_row_specs + b_col_specs + [o_spec] * ne
                  + [_ANY] * len(deps)),
        out_specs=[o_spec] * no,
        out_shape=[jax.ShapeDtypeStruct((M, N), dt) for dt in out_dtypes],
        scratch_shapes=[pltpu.VMEM((tm, tn), F32)] if nk > 1 else [],
        compiler_params=_cparams(("parallel", "parallel", "arbitrary")),
    )(a, b, *a_rows, *a_cols, *b_rows, *b_cols, *epi_tiles, *deps)
    return outs[0] if no == 1 else outs


MLP_TM = 1024
MLP_TC = 512


def _mlp_fwd(h, r, g, w_upT, w_dn, g_post):
    nc = DFF // MLP_TC

    def body(h_ref, r_ref, g_ref, wu_ref, wd_ref, gp_ref, a_ref, ff_ref, u_ref, ho_ref, hob_ref, acc, u_scr):
        c = pl.program_id(1)

        @pl.when(c == 0)
        def _():
            u = (h_ref[...] * r_ref[...] * g_ref[...]).astype(BF16)
            u_scr[...] = u
            u_ref[...] = u
            acc[...] = jnp.zeros_like(acc)
        a = _nt(u_scr[...], wu_ref[...])
        a_ref[...] = a.astype(BF16)
        acc[...] += _nn(jnp.square(jnp.maximum(a, 0.0)).astype(BF16), wd_ref[...])

        @pl.when(c == nc - 1)
        def _():
            f = acc[...]
            ff_ref[...] = f
            ho = h_ref[...] + f * _rstd(f) * gp_ref[...]
            ho_ref[...] = ho
            hob_ref[...] = ho.astype(BF16)

    row = pl.BlockSpec((MLP_TM, D), lambda i, c: (i, 0))
    wsp = pl.BlockSpec((MLP_TC, D), lambda i, c: (c, 0))
    vec = pl.BlockSpec((1, D), lambda i, c: (0, 0))
    return pl.pallas_call(
        body, name="mlp_fwd", grid=(T // MLP_TM, nc),
        in_specs=[row, pl.BlockSpec((MLP_TM, 1), lambda i, c: (i, 0)), vec, wsp, wsp, vec],
        out_specs=[pl.BlockSpec((MLP_TM, MLP_TC), lambda i, c: (i, c)), row, row, row, row],
        out_shape=[jax.ShapeDtypeStruct((T, DFF), BF16), jax.ShapeDtypeStruct((T, D), F32), jax.ShapeDtypeStruct((T, D), BF16),
                   jax.ShapeDtypeStruct((T, D), F32), jax.ShapeDtypeStruct((T, D), BF16)],
        scratch_shapes=[pltpu.VMEM((MLP_TM, D), F32), pltpu.VMEM((MLP_TM, D), BF16)],
        compiler_params=_cparams(("parallel", "arbitrary")),
    )(h, r, g, w_upT, w_dn, g_post)


def _mlp_dx(dff, a, w_upT, w_dn, dep):
    nc = DFF // MLP_TC

    def body(d_ref, a_ref, wu_ref, wd_ref, dep_ref, da_ref, du_ref, acc, d_scr):
        c = pl.program_id(1)

        @pl.when(c == 0)
        def _():
            d_scr[...] = d_ref[...].astype(BF16)
            acc[...] = jnp.zeros_like(acc)
        da = (_nt(d_scr[...], wd_ref[...]) * (2.0 * jnp.maximum(a_ref[...].astype(F32), 0.0))).astype(BF16)
        da_ref[...] = da
        acc[...] += _nn(da, wu_ref[...])

        @pl.when(c == nc - 1)
        def _():
            du_ref[...] = acc[...]

    row = pl.BlockSpec((MLP_TM, D), lambda i, c: (i, 0))
    wsp = pl.BlockSpec((MLP_TC, D), lambda i, c: (c, 0))
    chunk = pl.BlockSpec((MLP_TM, MLP_TC), lambda i, c: (i, c))
    return pl.pallas_call(
        body, name="mlp_dx", grid=(T // MLP_TM, nc),
        in_specs=[row, chunk, wsp, wsp, _ANY], out_specs=[chunk, row],
        out_shape=[jax.ShapeDtypeStruct((T, DFF), BF16), jax.ShapeDtypeStruct((T, D), F32)],
        scratch_shapes=[pltpu.VMEM((MLP_TM, D), F32), pltpu.VMEM((MLP_TM, D), BF16)],
        compiler_params=_cparams(("parallel", "arbitrary")),
    )(dff, a, w_upT, w_dn, dep)


def _rowwise(fn, rows, vecs, out_rows, out_sums, *, tm, name, deps=()):
    specs, arrs = [], []
    R = None
    for r in rows:
        if isinstance(r, tuple):
            arr, width, cb = r
            specs.append(pl.BlockSpec((tm, width), lambda i, cb=cb: (i, cb)))
        else:
            arr = r
            specs.append(pl.BlockSpec((tm, arr.shape[1]), lambda i: (i, 0)))
        R = arr.shape[0] if R is None else R
        assert arr.shape[0] == R, name
        arrs.append(arr)
    assert R % tm == 0, name
    for v in vecs:
        specs.append(pl.BlockSpec(v.shape, lambda i: (0, 0)))
        arrs.append(v)
    nr, nv, no, ns = len(rows), len(vecs), len(out_rows), len(out_sums)
    out_specs = [pl.BlockSpec((tm, w), lambda i: (i, 0)) for w, _ in out_rows]
    out_specs += [pl.BlockSpec(s, lambda i: (0, 0)) for s in out_sums]
    out_shape = [jax.ShapeDtypeStruct((R, w), dt) for w, dt in out_rows]
    out_shape += [jax.ShapeDtypeStruct(s, F32) for s in out_sums]

    nd = len(deps)

    def body(*refs):
        ins = [r[...] for r in refs[:nr + nv]]
        o_refs = refs[nr + nv + nd:nr + nv + nd + no]
        s_refs = refs[nr + nv + nd + no:]
        o_vals, s_vals = fn(*ins)
        for ref, val in zip(o_refs, o_vals):
            ref[...] = val.astype(ref.dtype)
        if ns:
            @pl.when(pl.program_id(0) == 0)
            def _():
                for ref in s_refs:
                    ref[...] = jnp.zeros_like(ref)
            for ref, val in zip(s_refs, s_vals):
                ref[...] += val

    outs = pl.pallas_call(
        body, name=name, grid=(R // tm,), in_specs=specs + [_ANY] * nd, out_specs=out_specs, out_shape=out_shape,
        compiler_params=_cparams(("arbitrary",) if ns else ("parallel",)),
    )(*arrs, *deps)
    return outs


def _mm_rows(fn, mats, rows, vecs, out_rows, out_sums, *, tm, name, deps=()):
    R = mats[0][0].shape[0]
    assert R % tm == 0, name
    specs, arrs = [], []
    for a, b, tb in mats:
        specs += [pl.BlockSpec((tm, a.shape[1]), lambda i: (i, 0)), pl.BlockSpec(b.shape, lambda i: (0, 0))]
        arrs += [a, b]
    for r in rows:
        specs.append(pl.BlockSpec((tm, r.shape[1]), lambda i: (i, 0)))
        arrs.append(r)
    for v in vecs:
        specs.append(pl.BlockSpec(v.shape, lambda i: (0, 0)))
        arrs.append(v)
    nm, nr, nv, nd, no, ns = len(mats), len(rows), len(vecs), len(deps), len(out_rows), len(out_sums)
    out_specs = [pl.BlockSpec((tm, w), lambda i: (i, 0)) for w, _ in out_rows]
    out_specs += [pl.BlockSpec(s, lambda i: (0, 0)) for s in out_sums]
    out_shape = [jax.ShapeDtypeStruct((R, w), dt) for w, dt in out_rows] + [jax.ShapeDtypeStruct(s, F32) for s in out_sums]

    def body(*refs):
        prods = [_dot(refs[2 * p][...].astype(BF16), refs[2 * p + 1][...].astype(BF16), 1, 1 if mats[p][2] else 0)
                 for p in range(nm)]
        ins = [r[...] for r in refs[2 * nm:2 * nm + nr + nv]]
        first_out = 2 * nm + nr + nv + nd
        o_refs, s_refs = refs[first_out:first_out + no], refs[first_out + no:]
        o_vals, s_vals = fn(*prods, *ins)
        for ref, val in zip(o_refs, o_vals):
            ref[...] = val.astype(ref.dtype)
        if ns:
            @pl.when(pl.program_id(0) == 0)
            def _():
                for ref in s_refs:
                    ref[...] = jnp.zeros_like(ref)
            for ref, val in zip(s_refs, s_vals):
                ref[...] += val

    return pl.pallas_call(
        body, name=name, grid=(R // tm,), in_specs=specs + [_ANY] * nd, out_specs=out_specs, out_shape=out_shape,
        compiler_params=_cparams(("arbitrary",) if ns else ("parallel",)),
    )(*arrs, *deps)


def _colsum(x):
    return jnp.sum(x, axis=0, keepdims=True)


def _rstd(x):
    return lax.rsqrt(jnp.mean(x * x, axis=-1, keepdims=True) + EPS)


def _rms_bwd(xn, r, g, dy):
    dn = dy * g
    return r * (dn - xn * jnp.mean(dn * xn, axis=-1, keepdims=True))


def _partner(t):
    lane = lax.broadcasted_iota(jnp.int32, t.shape, 1)
    up = pltpu.roll(t, 96, 1)
    down = pltpu.roll(t, 32, 1)
    return jnp.where((lane % 64) < 32, up, down)


SLABS = AW // 128


def _rows(r, n, d):
    return pl.ds(r, n, stride=d) if d > 1 else pl.ds(0, n)


def _undilate(src_ref, dst, d, tm):
    for r in range(d):
        for j in range(SLABS):
            dst[j][_rows(r, tm // d, d), :] = src_ref[:, pl.ds(r * AW + j * 128, 128)].astype(dst[j].dtype)


def _dilate(dst_ref, src, d, tm):
    for r in range(d):
        for j in range(SLABS):
            dst_ref[:, pl.ds(r * AW + j * 128, 128)] = src[j][_rows(r, tm // d, d), :].astype(dst_ref.dtype)


def _slab_scratch(n, tm):
    return [pltpu.VMEM((tm, 128), F32)] * (SLABS * n)


def _slab_groups(flat):
    return [flat[SLABS * i:SLABS * (i + 1)] for i in range(len(flat) // SLABS)]


def _slab_specs(tm, first):
    return [pl.BlockSpec((tm, 128), lambda i, j=j: (i, first + j)) for j in range(SLABS)]


def _dil_spec(tm, d):
    return pl.BlockSpec((tm // d, d * AW), lambda i: (i, 0))


ROPE_TM = 512


def _rope_fwd(qkvz, cos128, sin128):
    tm = ROPE_TM

    def body(*refs):
        q_refs, k_refs, v_refs = refs[0:4], refs[4:8], refs[8:12]
        c_ref, s_ref = refs[12], refs[13]
        outs = refs[14:23]
        qs, ks = _slab_groups(refs[23:])
        c, s = c_ref[...], s_ref[...]
        for j in range(SLABS):
            q, k = q_refs[j][...], k_refs[j][...]
            qs[j][...] = (q * c + _partner(q) * s) * (HD ** -0.5)
            ks[j][...] = k * c + _partner(k) * s
        for di, d in enumerate(DILATIONS):
            oq, ok, ov = outs[3 * di:3 * di + 3]
            for r in range(d):
                rows = _rows(r, tm // d, d)
                for j in range(SLABS):
                    cols = pl.ds(r * AW + j * 128, 128)
                    oq[:, cols] = qs[j][rows, :].astype(BF16)
                    ok[:, cols] = ks[j][rows, :].astype(BF16)
                    ov[:, cols] = v_refs[j][rows, :].astype(BF16)

    tab = pl.BlockSpec((tm, 128), lambda i: (i, 0))
    out_specs, out_shape = [], []
    for d in DILATIONS:
        out_specs += [_dil_spec(tm, d)] * 3
        out_shape += [jax.ShapeDtypeStruct((T // d, d * AW), BF16)] * 3
    return pl.pallas_call(
        body, name="rope_fwd", grid=(T // tm,),
        in_specs=_slab_specs(tm, 0) + _slab_specs(tm, 4) + _slab_specs(tm, 8) + [tab, tab],
        out_specs=out_specs, out_shape=out_shape, scratch_shapes=_slab_scratch(2, tm),
        compiler_params=_cparams(("parallel",)),
    )(*([qkvz] * 12), cos128, sin128)


def _rope_bwd(grads, dz, cos128, sin128):
    tm = ROPE_TM

    def body(*refs):
        g_refs = refs[0:9]
        dz_ref, c_ref, s_ref, o_ref = refs[9], refs[10], refs[11], refs[12]
        scr = _slab_groups(refs[13:])
        for di, d in enumerate(DILATIONS[1:]):
            for t in range(3):
                _undilate(g_refs[3 * (di + 1) + t], scr[3 * di + t], d, tm)
        c, s = c_ref[...], s_ref[...]
        for j in range(SLABS):
            cols = pl.ds(j * 128, 128)
            tot = [g_refs[t][:, cols] + scr[t][j][...] + scr[3 + t][j][...] for t in range(3)]
            dqr = tot[0] * (HD ** -0.5)
            o_ref[:, pl.ds(j * 128, 128)] = (dqr * c + _partner(dqr * s)).astype(BF16)
            o_ref[:, pl.ds(AW + j * 128, 128)] = (tot[1] * c + _partner(tot[1] * s)).astype(BF16)
            o_ref[:, pl.ds(2 * AW + j * 128, 128)] = tot[2].astype(BF16)
        o_ref[:, pl.ds(3 * AW, AW)] = dz_ref[...].astype(BF16)

    tab = pl.BlockSpec((tm, 128), lambda i: (i, 0))
    in_specs, args = [], []
    for d, g in zip(DILATIONS, grads):
        in_specs += [_dil_spec(tm, d)] * 3
        args += list(g)
    return pl.pallas_call(
        body, name="rope_bwd", grid=(T // tm,),
        in_specs=in_specs + [pl.BlockSpec((tm, AW), lambda i: (i, 0)), tab, tab],
        out_specs=pl.BlockSpec((tm, 4 * AW), lambda i: (i, 0)),
        out_shape=jax.ShapeDtypeStruct((T, 4 * AW), BF16),
        scratch_shapes=_slab_scratch(6, tm),
        compiler_params=_cparams(("parallel",)),
    )(*args, dz, cos128, sin128)


def _dx_out(dmix, w_o, dep):
    tm = ROPE_TM

    def body(a_ref, w_ref, dep_ref, dcat_ref, o4, o16, *slabs):
        prod = _nt(a_ref[...].astype(BF16), w_ref[...].astype(BF16))
        dcat_ref[...] = prod
        for j in range(SLABS):
            slabs[j][...] = prod[:, 128 * j:128 * (j + 1)]
        _dilate(o4, slabs, 4, tm)
        _dilate(o16, slabs, 16, tm)

    return pl.pallas_call(
        body, name="dx_out", grid=(T // tm,),
        in_specs=[pl.BlockSpec((tm, D), lambda i: (i, 0)), pl.BlockSpec((D, D), lambda i: (0, 0)), _ANY],
        out_specs=[pl.BlockSpec((tm, D), lambda i: (i, 0)), _dil_spec(tm, 4), _dil_spec(tm, 16)],
        out_shape=[jax.ShapeDtypeStruct((T, D), F32), jax.ShapeDtypeStruct((T // 4, 4 * AW), F32),
                   jax.ShapeDtypeStruct((T // 16, 16 * AW), F32)],
        scratch_shapes=_slab_scratch(1, tm), compiler_params=_cparams(("parallel",)),
    )(dmix, w_o, dep)


def _band_masks():
    qi = lax.broadcasted_iota(jnp.int32, (QBLK, QBLK), 0)
    kj = lax.broadcasted_iota(jnp.int32, (QBLK, QBLK), 1)
    return kj >= qi, kj <= qi


def _attn_fwd(q, k, v, d):
    L = q.shape[0]
    npair = L // (2 * QBLK)

    def body(q_ref, kp_ref, kc_ref, vp_ref, vc_ref, o_ref, l_ref):
        pair = pl.program_id(1)
        mask_p, mask_c = _band_masks()
        for sub in range(2):
            rows = pl.ds(sub * QBLK, QBLK)
            first = jnp.where(pair > 0, 0.0, NEG) if sub == 0 else 0.0
            bias = jnp.concatenate([jnp.where(mask_p, 0.0, NEG) + first, jnp.where(mask_c, 0.0, NEG)], axis=1)
            k_prev = (lambda sl: kp_ref[:, sl]) if sub == 0 else (lambda sl: kc_ref[pl.ds(0, QBLK), sl])
            v_prev = (lambda sl: vp_ref[:, sl]) if sub == 0 else (lambda sl: vc_ref[pl.ds(0, QBLK), sl])
            s = []
            for h in range(HEADS):
                sl = pl.ds(HD * h, HD)
                qh = q_ref[rows, sl]
                s.append(jnp.concatenate([_nt(qh, k_prev(sl)), _nt(qh, kc_ref[rows, sl])], axis=1))
            s = jnp.stack(s) + bias
            m = jnp.max(s, axis=2, keepdims=True)
            e = jnp.exp(s - m)
            den = jnp.sum(e, axis=2, keepdims=True)
            p = e.astype(BF16)
            inv = 1.0 / den
            lse = m + jnp.log(den)
            for h in range(HEADS):
                sl = pl.ds(HD * h, HD)
                o_ref[rows, sl] = ((_nn(p[h, :, :QBLK], v_prev(sl)) + _nn(p[h, :, QBLK:], vc_ref[rows, sl])) * inv[h]
                                   ).astype(BF16)
                l_ref[rows, sl] = jnp.broadcast_to(lse[h], (QBLK, HD))

    cur = pl.BlockSpec((2 * QBLK, AW), lambda r, n: (n, r))
    prev = pl.BlockSpec((QBLK, AW), lambda r, n: (jnp.maximum(2 * n - 1, 0), r))
    return pl.pallas_call(
        body, name=f"attn_fwd_d{d}", grid=(d, npair),
        in_specs=[cur, prev, cur, prev, cur], out_specs=[cur, cur],
        out_shape=[jax.ShapeDtypeStruct((L, d * AW), BF16), jax.ShapeDtypeStruct((L, d * AW), F32)],
        compiler_params=_cparams(("parallel", "parallel")),
    )(q, k, k, v, v)


def _attn_bwd(q, k, v, do, at, lse, d):
    L = q.shape[0]
    nb = L // QBLK
    npair = nb // 2

    def body(qc_ref, qn_ref, kp_ref, kc_ref, vp_ref, vc_ref, doc_ref, don_ref, atc_ref, atn_ref,
             lc_ref, ln_ref, dq_ref, dk_ref, dv_ref):
        pair = pl.program_id(1)
        mask_p, mask_c = _band_masks()
        prev_bias = jnp.where(mask_p, 0.0, NEG)
        for sub in range(2):
            rows = pl.ds(sub * QBLK, QBLK)
            second = pl.ds(QBLK, QBLK)
            if sub == 0:
                take = lambda cur_ref, nxt_ref, cols, i: cur_ref[rows if i == 0 else second, cols]
                prev_of = lambda p_ref, c_ref, cols: p_ref[:, cols]
                first, last = jnp.where(pair > 0, 0.0, NEG), 0.0
            else:
                take = lambda cur_ref, nxt_ref, cols, i: cur_ref[rows, cols] if i == 0 else nxt_ref[:, cols]
                prev_of = lambda p_ref, c_ref, cols: c_ref[pl.ds(0, QBLK), cols]
                first, last = 0.0, jnp.where(pair < npair - 1, 0.0, NEG)
            bias = jnp.concatenate([prev_bias + first, jnp.where(mask_c, 0.0, NEG), prev_bias + last], axis=1)
            s, dp, ls, dl, ops = [], [], [], [], []
            for h in range(HEADS):
                sl = pl.ds(HD * h, HD)
                one = pl.ds(HD * h, 1)
                q0, q1 = take(qc_ref, qn_ref, sl, 0), take(qc_ref, qn_ref, sl, 1)
                kp, kc = prev_of(kp_ref, kc_ref, sl), kc_ref[rows, sl]
                vp, vc = prev_of(vp_ref, vc_ref, sl), vc_ref[rows, sl]
                do0, do1 = take(doc_ref, don_ref, sl, 0), take(doc_ref, don_ref, sl, 1)
                do0b, do1b = do0.astype(BF16), do1.astype(BF16)
                s.append(jnp.concatenate([_nt(q0, kp), _nt(q0, kc), _nt(q1, kc)], axis=1))
                dp.append(jnp.concatenate([_nt(do0b, vp), _nt(do0b, vc), _nt(do1b, vc)], axis=1))
                dl0 = jnp.sum(do0 * take(atc_ref, atn_ref, sl, 0), axis=1, keepdims=True)
                dl1 = jnp.sum(do1 * take(atc_ref, atn_ref, sl, 1), axis=1, keepdims=True)
                dl.append(jnp.concatenate([jnp.broadcast_to(dl0, (QBLK, 2 * QBLK)), jnp.broadcast_to(dl1, (QBLK, QBLK))], axis=1))
                ls.append(jnp.concatenate([jnp.broadcast_to(take(lc_ref, ln_ref, one, 0), (QBLK, 2 * QBLK)),
                                           jnp.broadcast_to(take(lc_ref, ln_ref, one, 1), (QBLK, QBLK))], axis=1))
                ops.append((q0, q1, kp, kc, do0b, do1b))
            p = jnp.exp(jnp.stack(s) + bias - jnp.stack(ls))
            ds = (p * (jnp.stack(dp) - jnp.stack(dl))).astype(BF16)
            p = p.astype(BF16)
            for h in range(HEADS):
                sl = pl.ds(HD * h, HD)
                q0, q1, kp, kc, do0b, do1b = ops[h]
                dq_ref[rows, sl] = (_nn(ds[h, :, :QBLK], kp) + _nn(ds[h, :, QBLK:2 * QBLK], kc)).astype(BF16)
                dv_ref[rows, sl] = (_tn(p[h, :, QBLK:2 * QBLK], do0b) + _tn(p[h, :, 2 * QBLK:], do1b)).astype(BF16)
                dk_ref[rows, sl] = (_tn(ds[h, :, QBLK:2 * QBLK], q0) + _tn(ds[h, :, 2 * QBLK:], q1)).astype(BF16)

    cur = pl.BlockSpec((2 * QBLK, AW), lambda r, n: (n, r))
    prev = pl.BlockSpec((QBLK, AW), lambda r, n: (jnp.maximum(2 * n - 1, 0), r))
    nxt = pl.BlockSpec((QBLK, AW), lambda r, n: (jnp.minimum(2 * n + 2, nb - 1), r))
    return pl.pallas_call(
        body, name=f"attn_bwd_d{d}", grid=(d, npair),
        in_specs=[cur, nxt, prev, cur, prev, cur, cur, nxt, cur, nxt, cur, nxt], out_specs=[cur, cur, cur],
        out_shape=[jax.ShapeDtypeStruct((L, d * AW), BF16)] * 3,
        compiler_params=_cparams(("parallel", "parallel")),
    )(q, q, k, k, v, v, do, do, at, at, lse, lse)


def _attn_merge(outs, lses):
    tm = ROPE_TM

    def body(o1, o4, o16, l1, l4, l16, at_ref, ls_ref, at4, ls4, at16, ls16, *flat):
        so4, so16, sl4, sl16, sa, sl = _slab_groups(flat)
        _undilate(o4, so4, 4, tm)
        _undilate(o16, so16, 16, tm)
        _undilate(l4, sl4, 4, tm)
        _undilate(l16, sl16, 16, tm)
        for j in range(SLABS):
            cols = pl.ds(j * 128, 128)
            a, b, c = l1[:, cols], sl4[j][...], sl16[j][...]
            m = jnp.maximum(jnp.maximum(a, b), c)
            e1, e2, e3 = jnp.exp(a - m), jnp.exp(b - m), jnp.exp(c - m)
            s = e1 + e2 + e3
            inv = 1.0 / s
            attn = (e1 * inv) * o1[:, cols] + (e2 * inv) * so4[j][...] + (e3 * inv) * so16[j][...]
            lse = m + jnp.log(s)
            at_ref[:, cols] = attn
            ls_ref[:, cols] = lse
            sa[j][...] = attn
            sl[j][...] = lse
        _dilate(at4, sa, 4, tm)
        _dilate(at16, sa, 16, tm)
        _dilate(ls4, sl, 4, tm)
        _dilate(ls16, sl, 16, tm)

    specs = [_dil_spec(tm, d) for d in DILATIONS]
    tok = specs[0]
    return pl.pallas_call(
        body, name="attn_merge", grid=(T // tm,),
        in_specs=specs + specs, out_specs=[tok, tok, specs[1], specs[1], specs[2], specs[2]],
        out_shape=[jax.ShapeDtypeStruct((T, AW), F32)] * 2 + [jax.ShapeDtypeStruct((T // 4, 4 * AW), F32)] * 2
        + [jax.ShapeDtypeStruct((T // 16, 16 * AW), F32)] * 2,
        scratch_shapes=_slab_scratch(6, tm),
        compiler_params=_cparams(("parallel",)),
    )(*outs, *lses)


CONV_TM = 512
HALO = 8


def _conv_pre(ext, w, b):
    y = b + w[3] * ext
    for kk in range(1, CONV_K):
        y = y + w[3 - kk] * pltpu.roll(ext, kk, 0)
    return y


def _rows_to_block(rows, n, width):
    ri = lax.broadcasted_iota(jnp.int32, (n, width), 0)
    out = jnp.zeros((n, width), F32)
    for j, r in enumerate(rows):
        out = out + jnp.where(ri == j, r, 0.0)
    return out


def _conv_bwd(xbc, dact, ddt, w, b):
    nblk = T // CONV_TM
    per = CONV_TM // HALO

    def body(x_ref, xb_ref, xa_ref, g_ref, ga_ref, ddt_ref, w_ref, b_ref, dx_ref, dw_ref):
        i = pl.program_id(0)
        wv = [w_ref[pl.ds(j, 1), :] for j in range(CONV_K)]
        before = jnp.where(i > 0, xb_ref[...], 0.0)
        last = i == nblk - 1
        after = jnp.where(last, 0.0, xa_ref[...])
        g_after = jnp.where(last, 0.0, ga_ref[...])
        ext = jnp.concatenate([before, x_ref[...], after], axis=0)
        y = _conv_pre(ext, wv, b_ref[...])[HALO:]
        sg = _sigmoid(y)
        dy = jnp.concatenate([g_ref[...], g_after], axis=0) * (sg * (1.0 + y * (1.0 - sg)))
        n = CONV_TM + HALO
        dx = wv[3] * dy
        for kk in range(1, CONV_K):
            dx = dx + wv[3 - kk] * pltpu.roll(dy, n - kk, 0)
        dx_ref[:, pl.ds(0, CONV_CH)] = dx[:CONV_TM].astype(BF16)
        dx_ref[:, pl.ds(CONV_CH, DT_PAD)] = ddt_ref[...].astype(BF16)
        dyc = dy[:CONV_TM]
        rows = [jnp.sum(dyc * (pltpu.roll(ext, 3 - j, 0) if j < 3 else ext)[HALO:HALO + CONV_TM], axis=0, keepdims=True)
                for j in range(CONV_K)]
        rows.append(jnp.sum(dyc, axis=0, keepdims=True))
        part = _rows_to_block(rows, 8, CONV_CH)

        @pl.when(i == 0)
        def _():
            dw_ref[...] = jnp.zeros_like(dw_ref)
        dw_ref[...] += part

    blk = pl.BlockSpec((CONV_TM, CONV_CH), lambda i: (i, 0))
    hb = pl.BlockSpec((HALO, CONV_CH), lambda i: (jnp.maximum(i * per - 1, 0), 0))
    ha = pl.BlockSpec((HALO, CONV_CH), lambda i: (jnp.minimum((i + 1) * per, T // HALO - 1), 0))
    return pl.pallas_call(
        body, name="conv_bwd", grid=(nblk,),
        in_specs=[blk, hb, ha, blk, ha, pl.BlockSpec((CONV_TM, DT_PAD), lambda i: (i, 0)),
                  pl.BlockSpec((CONV_K, CONV_CH), lambda i: (0, 0)), pl.BlockSpec((1, CONV_CH), lambda i: (0, 0))],
        out_specs=[pl.BlockSpec((CONV_TM, CONV_CH + DT_PAD), lambda i: (i, 0)), pl.BlockSpec((8, CONV_CH), lambda i: (0, 0))],
        out_shape=[jax.ShapeDtypeStruct((T, CONV_CH + DT_PAD), BF16), jax.ShapeDtypeStruct((8, CONV_CH), F32)],
        compiler_params=_cparams(("arbitrary",)),
    )(xbc, xbc, xbc, dact, dact, ddt, w, b)


def _pick(mat, h):
    lane = lax.broadcasted_iota(jnp.int32, mat.shape, 1)
    return jnp.sum(jnp.where(lane == h, mat, 0.0), axis=1, keepdims=True)


def _heads(fn):
    return jnp.stack([fn(h) for h in range(HEADS)])


def _ssd_prep(dt_ref, bias_ref, alog_ref, dsk_ref, b_ref, c_ref, xs_ref, state_ref, cst):
    li = lax.broadcasted_iota(jnp.int32, (CHUNK, CHUNK), 0)
    si = lax.broadcasted_iota(jnp.int32, (CHUNK, CHUNK), 1)
    tri = li >= si
    dtp = dt_ref[...] + bias_ref[...]
    dt = _softplus(dtp)
    A = -jnp.exp(alog_ref[...])
    a = dt * A
    cs = jnp.dot(tri.astype(F32), a, precision=HIGHEST, preferred_element_type=F32)
    cst[...] = cs.T
    Bm = b_ref[...].astype(BF16)
    Cm = c_ref[...].astype(BF16)
    cb = _nt(Cm, Bm)
    dskv = dsk_ref[...]
    cs_col = _heads(lambda h: _pick(cs, h))
    cs_row = _heads(lambda h: cst[pl.ds(h, 1), :])
    dt_col = _heads(lambda h: _pick(dt, h))
    dsk_col = _heads(lambda h: _pick(dskv, h))
    lam = jnp.exp(jnp.where(tri, cs_col - cs_row, NEG))
    x = _heads(lambda h: xs_ref[:, pl.ds(HD * h, HD)])
    xdt = x * dt_col
    prev = _heads(lambda h: state_ref[pl.ds(HD * h, HD), :])
    lane = lax.broadcasted_iota(jnp.int32, (1, 1, CHUNK), 2)
    cl = jnp.sum(jnp.where(lane == CHUNK - 1, cs_row, 0.0), axis=2, keepdims=True)
    f = jnp.exp(cl - cs_col)
    return dict(li=li, si=si, dtp=dtp, dt=dt, A=A, Bm=Bm, Cm=Cm, cb=cb, cs_col=cs_col, dt_col=dt_col, dsk_col=dsk_col,
                lam=lam, x=x, xdt=xdt, prev=prev, cl=cl, f=f)


def _ssd_fwd(xbcdt, conv_w, conv_b, bias, alog, dsk, qkvz, attn, gs):
    nc = T // CHUNK
    R = SSD_PER * CHUNK
    per = R // HALO

    def body(xbc_ref, halo_ref, cw_ref, cb_ref, dt_ref, bias_ref, alog_ref, dsk_ref, z_ref, at_ref, gs_ref,
             y_ref, st_ref, cat_ref, act_ref, state, cst):
        @pl.when(pl.program_id(0) == 0)
        def _():
            state[...] = jnp.zeros_like(state)
        for sub in range(SSD_PER):
            rows = pl.ds(sub * CHUNK, CHUNK)
            st_ref[sub] = state[...]
            halo = (jnp.where(pl.program_id(0) > 0, halo_ref[...], 0.0) if sub == 0
                    else xbc_ref[pl.ds(sub * CHUNK - HALO, HALO), :])
            one_chunk(halo, xbc_ref.at[rows, :], cw_ref, cb_ref, dt_ref.at[rows, :], bias_ref, alog_ref, dsk_ref,
                      z_ref.at[rows, :], at_ref.at[rows, :], gs_ref, y_ref.at[rows, :], cat_ref.at[rows, :],
                      act_ref.at[rows, :], state, cst)

    def one_chunk(halo, xbc_ref, cw_ref, cb_ref, dt_ref, bias_ref, alog_ref, dsk_ref, z_ref, at_ref, gs_ref,
                  y_ref, cat_ref, act_ref, state, cst):
        pre = _conv_pre(jnp.concatenate([halo, xbc_ref[...]], axis=0),
                        [cw_ref[pl.ds(j, 1), :] for j in range(CONV_K)], cb_ref[...])[HALO:]
        act_ref[...] = pre * _sigmoid(pre)
        xs_ref, b_ref, c_ref = (act_ref.at[:, pl.ds(0, AW)], act_ref.at[:, pl.ds(AW, NS)],
                                act_ref.at[:, pl.ds(AW + NS, NS)])
        s = _ssd_prep(dt_ref, bias_ref, alog_ref, dsk_ref, b_ref, c_ref, xs_ref, state, cst)
        Bm, Cm, prev = s["Bm"], s["Cm"], s["prev"]
        g = (s["cb"] * s["lam"]).astype(BF16)
        xdtb = s["xdt"].astype(BF16)
        prevb = prev.astype(BF16)
        y = _heads(lambda h: _nn(g[h], xdtb[h])) + _heads(lambda h: _nt(Cm, prevb[h])) * jnp.exp(s["cs_col"])
        y = y + s["dsk_col"] * s["x"]
        xf = (s["xdt"] * s["f"]).astype(BF16)
        new = prev * jnp.exp(s["cl"]) + _heads(lambda h: _tn(xf[h], Bm))
        for h in range(HEADS):
            y_ref[:, pl.ds(HD * h, HD)] = y[h]
            state[pl.ds(HD * h, HD), :] = new[h]
        z = z_ref[...]
        gi = y_ref[...] * (z * _sigmoid(z))
        cat_ref[:, pl.ds(0, AW)] = at_ref[...].astype(BF16)
        cat_ref[:, pl.ds(AW, AW)] = (gi * _rstd(gi) * gs_ref[...]).astype(BF16)

    vec = pl.BlockSpec((1, DT_PAD), lambda c: (0, 0))
    blk = pl.BlockSpec((R, AW), lambda c: (c, 0))
    return pl.pallas_call(
        body, name="ssd_fwd", grid=(nc // SSD_PER,),
        in_specs=[pl.BlockSpec((R, CONV_CH), lambda c: (c, 0)),
                  pl.BlockSpec((HALO, CONV_CH), lambda c: (jnp.maximum(c * per - 1, 0), 0)),
                  pl.BlockSpec((CONV_K, CONV_CH), lambda c: (0, 0)), pl.BlockSpec((1, CONV_CH), lambda c: (0, 0)),
                  pl.BlockSpec((R, DT_PAD), lambda c: (c, 6)),
                  vec, vec, vec, pl.BlockSpec((R, AW), lambda c: (c, 3)), blk, pl.BlockSpec((1, AW), lambda c: (0, 0))],
        out_specs=[blk, pl.BlockSpec((SSD_PER, AW, NS), lambda c: (c, 0, 0)), pl.BlockSpec((R, D), lambda c: (c, 0)),
                   pl.BlockSpec((R, CONV_CH), lambda c: (c, 0))],
        out_shape=[jax.ShapeDtypeStruct((T, AW), F32), jax.ShapeDtypeStruct((nc, AW, NS), F32),
                   jax.ShapeDtypeStruct((T, D), BF16), jax.ShapeDtypeStruct((T, CONV_CH), F32)],
        scratch_shapes=[pltpu.VMEM((AW, NS), F32), pltpu.VMEM((CHUNK, CHUNK), F32)],
        compiler_params=_cparams(("arbitrary",)),
    )(xbcdt, xbcdt, conv_w, conv_b, xbcdt, bias, alog, dsk, qkvz, attn, gs)


def _ssd_bwd(act, xbcdt, bias, alog, dsk, states, y_ssd, qkvz, dcat, gs):
    nc = T // CHUNK

    def body(xs_ref, b_ref, c_ref, dt_ref, bias_ref, alog_ref, dsk_ref, st_ref, y_ref, z_ref, dyn_ref, gs_ref,
             dact_ref, ddt_ref, par_ref, dz_ref, dgs_ref, dstate, cst, dy_ref):
        @pl.when(pl.program_id(0) == 0)
        def _():
            dstate[...] = jnp.zeros_like(dstate)
            par_ref[...] = jnp.zeros_like(par_ref)
            dgs_ref[...] = jnp.zeros_like(dgs_ref)
        for sub in reversed(range(SSD_PER)):
            rows = pl.ds(sub * CHUNK, CHUNK)
            one_chunk(xs_ref.at[rows, :], b_ref.at[rows, :], c_ref.at[rows, :], dt_ref.at[rows, :], bias_ref, alog_ref,
                      dsk_ref, st_ref.at[sub], y_ref.at[rows, :], z_ref.at[rows, :], dyn_ref.at[rows, :], gs_ref,
                      dact_ref.at[rows, :], ddt_ref.at[rows, :], par_ref, dz_ref.at[rows, :], dgs_ref, dstate, cst, dy_ref)

    def one_chunk(xs_ref, b_ref, c_ref, dt_ref, bias_ref, alog_ref, dsk_ref, st_ref, y_ref, z_ref, dyn_ref, gs_ref,
                  dact_ref, ddt_ref, par_ref, dz_ref, dgs_ref, dstate, cst, dy_ref):
        z, yv, dyn = z_ref[...], y_ref[...], dyn_ref[...]
        sg = _sigmoid(z)
        sz = z * sg
        gi = yv * sz
        rg = _rstd(gi)
        ng = gi * rg
        dgi = _rms_bwd(ng, rg, gs_ref[...], dyn)
        dy_ref[...] = dgi * sz
        dz_ref[...] = dgi * yv * (sg * (1.0 + z * (1.0 - sg)))
        dgs_ref[...] += _colsum(dyn * ng)
        s = _ssd_prep(dt_ref, bias_ref, alog_ref, dsk_ref, b_ref, c_ref, xs_ref, st_ref, cst)
        Bm, Cm, prev, lam, x, xdt, f, cl = s["Bm"], s["Cm"], s["prev"], s["lam"], s["x"], s["xdt"], s["f"], s["cl"]
        lane = lax.broadcasted_iota(jnp.int32, (1, DT_PAD), 1)
        row = lax.broadcasted_iota(jnp.int32, (1, CHUNK, 1), 1)
        g = s["cb"] * lam
        gb, xdtb, prevb = g.astype(BF16), xdt.astype(BF16), prev.astype(BF16)
        dy = _heads(lambda h: dy_ref[:, pl.ds(HD * h, HD)])
        dyb = dy.astype(BF16)
        dnew = _heads(lambda h: dstate[pl.ds(HD * h, HD), :])
        dnewb = dnew.astype(BF16)
        E = jnp.exp(s["cs_col"])
        ecl = jnp.exp(cl)
        dG = _heads(lambda h: _nt(dyb[h], xdtb[h]))
        dxdt = _heads(lambda h: _tn(gb[h], dyb[h]))
        Yo = _heads(lambda h: _nt(Cm, prevb[h]))
        W = _heads(lambda h: _nt(Bm, dnewb[h]))
        dcb = jnp.sum(dG * lam, axis=0)
        Mm = dG * g
        col_sums = jnp.sum(Mm, axis=1, keepdims=True)
        dYo = (dy * E).astype(BF16)
        dxdt = dxdt + W * f
        dF = jnp.sum(W * xdt, axis=2, keepdims=True) * f
        dcl = jnp.sum(dnew * prev, axis=(1, 2), keepdims=True) * ecl + jnp.sum(dF, axis=1, keepdims=True)
        dcs = (jnp.sum(Mm, axis=2, keepdims=True) + jnp.sum(dy * Yo, axis=2, keepdims=True) * E - dF
               + jnp.where(row == CHUNK - 1, dcl, 0.0))
        ddt_x = jnp.sum(dxdt * x, axis=2, keepdims=True)
        dD = jnp.sum(dy * x, axis=(1, 2), keepdims=True)
        dx = s["dsk_col"] * dy + dxdt * s["dt_col"]
        xfb = (xdt * f).astype(BF16)
        dprev = _heads(lambda h: _tn(dYo[h], Cm)) + dnew * ecl
        dcbb = dcb.astype(BF16)
        dC = _nn(dcbb, Bm)
        dB = _tn(dcbb, Cm)
        dcs_mat = -_rows_to_block([col_sums[h] for h in range(HEADS)], CHUNK, CHUNK).T
        ddt_mat = jnp.zeros((CHUNK, DT_PAD), F32)
        dD_row = jnp.zeros((1, DT_PAD), F32)
        for h in range(HEADS):
            sl = pl.ds(HD * h, HD)
            dC = dC + _nn(dYo[h], prevb[h])
            dB = dB + _nn(xfb[h], dnewb[h])
            dcs_mat = dcs_mat + jnp.where(lane == h, dcs[h], 0.0)
            ddt_mat = ddt_mat + jnp.where(lane == h, ddt_x[h], 0.0)
            dD_row = dD_row + jnp.where(lane == h, dD[h], 0.0)
            dact_ref[:, sl] = dx[h]
            dstate[sl, :] = dprev[h]
        dact_ref[:, pl.ds(AW, NS)] = dB
        dact_ref[:, pl.ds(AW + NS, NS)] = dC
        da = jnp.dot((s["li"] <= s["si"]).astype(F32), dcs_mat, precision=HIGHEST, preferred_element_type=F32)
        ddtp = jnp.where(lane < HEADS, (ddt_mat + da * s["A"]) * _sigmoid(s["dtp"]), 0.0)
        ddt_ref[...] = ddtp
        dalog = jnp.where(lane < HEADS, jnp.sum(da * s["dt"], axis=0, keepdims=True) * s["A"], 0.0)
        par_ref[...] += _rows_to_block([jnp.sum(ddtp, axis=0, keepdims=True), dalog, dD_row], 8, DT_PAD)

    vec = pl.BlockSpec((1, DT_PAD), lambda c: (0, 0))
    nstep = nc // SSD_PER
    rev = lambda c: nstep - 1 - c
    R = SSD_PER * CHUNK
    return pl.pallas_call(
        body, name="ssd_bwd", grid=(nstep,),
        in_specs=[pl.BlockSpec((R, AW), lambda c: (rev(c), 0)), pl.BlockSpec((R, NS), lambda c: (rev(c), 4)),
                  pl.BlockSpec((R, NS), lambda c: (rev(c), 5)), pl.BlockSpec((R, DT_PAD), lambda c: (rev(c), 6)),
                  vec, vec, vec,
                  pl.BlockSpec((SSD_PER, AW, NS), lambda c: (rev(c), 0, 0)), pl.BlockSpec((R, AW), lambda c: (rev(c), 0)),
                  pl.BlockSpec((R, AW), lambda c: (rev(c), 3)), pl.BlockSpec((R, AW), lambda c: (rev(c), 1)),
                  pl.BlockSpec((1, AW), lambda c: (0, 0))],
        out_specs=[pl.BlockSpec((R, CONV_CH), lambda c: (rev(c), 0)), pl.BlockSpec((R, DT_PAD), lambda c: (rev(c), 0)),
                   pl.BlockSpec((8, DT_PAD), lambda c: (0, 0)), pl.BlockSpec((R, AW), lambda c: (rev(c), 0)),
                   pl.BlockSpec((1, AW), lambda c: (0, 0))],
        out_shape=[jax.ShapeDtypeStruct((T, CONV_CH), F32), jax.ShapeDtypeStruct((T, DT_PAD), F32),
                   jax.ShapeDtypeStruct((8, DT_PAD), F32), jax.ShapeDtypeStruct((T, AW), F32),
                   jax.ShapeDtypeStruct((1, AW), F32)],
        scratch_shapes=[pltpu.VMEM((AW, NS), F32), pltpu.VMEM((CHUNK, CHUNK), F32), pltpu.VMEM((CHUNK, AW), F32)],
        compiler_params=_cparams(("arbitrary",)),
    )(act, act, act, xbcdt, bias, alog, dsk, states, y_ssd, qkvz, dcat, gs)


def _place():
    return lax.axis_index("x"), lax.axis_index("y"), lax.axis_index("c")


def _slot(px, py, pc):
    return 4 * px + 2 * py + pc


SLAB_ROWS = 24


def _slab_pack(parts, name):
    n = len(parts)

    def body(*refs):
        slab = refs[n]
        slab[...] = jnp.zeros_like(slab)
        for ref, (arr, row) in zip(refs[:n], parts):
            slab[pl.ds(row, arr.shape[0]), pl.ds(0, arr.shape[1])] = ref[...]

    vm = pl.BlockSpec(memory_space=pltpu.VMEM)
    return pl.pallas_call(
        body, name=name, in_specs=[vm] * n, out_specs=vm, out_shape=jax.ShapeDtypeStruct((SLAB_ROWS, D), F32),
    )(*[a for a, _ in parts])


_HBM = pl.BlockSpec(memory_space=pltpu.HBM)
_SEM = pl.BlockSpec(memory_space=pltpu.SEMAPHORE)
_EFFECT = pltpu.SideEffectType.DATAFLOW_SIDE_EFFECTING


def _peers(x, y, c):
    out = []
    for kk in range(1, N_DEV):
        fx, fy, fc = kk >> 2 & 1, kk >> 1 & 1, kk & 1
        out.append((1 - x if fx else x, 1 - y if fy else y, 1 - c if fc else c))
    return out


def _send_start(src, per_peer, name, dep):
    (handles, token) = _send_start_many([src], per_peer, name, dep)
    return handles, token


def _near_peers(x, y, c):
    return [(x, y, 1 - c), (1 - x, y, c), (x, 1 - y, c), (1 - x, 1 - y, c)]


def _send_start_many(srcs, per_peer, name, dep, peers=_peers, npeers=N_DEV - 1):
    n = len(srcs)

    def body(*refs):
        src_refs, land_refs = refs[:n], refs[n:2 * n]
        send_sems, recv_sems = refs[2 * n + 1], refs[2 * n + 2]
        token = refs[-1]
        x, y, c = _place()
        mine = _slot(x, y, c)
        for a in range(n):
            for kk, peer in enumerate(peers(x, y, c)):
                pltpu.make_async_remote_copy(
                    src_ref=src_refs[a].at[_slot(*peer)] if per_peer else src_refs[a], dst_ref=land_refs[a].at[mine],
                    send_sem=send_sems.at[a * npeers + kk], recv_sem=recv_sems.at[a * npeers + kk],
                    device_id=peer, device_id_type=MESH).start()
        token[...] = jnp.zeros_like(token)

    lands = [lax.empty((N_DEV,) + tuple(s.shape[1:] if per_peer else s.shape), s.dtype) for s in srcs]
    hbm = lambda t: pltpu.with_memory_space_constraint(t, pltpu.HBM)
    outs = pl.pallas_call(
        body, name=name,
        out_shape=(pltpu.SemaphoreType.DMA((n * npeers,)), pltpu.SemaphoreType.DMA((n * npeers,)),
                   *[pltpu.HBM(s.shape, s.dtype) for s in srcs], *[pltpu.HBM(l.shape, l.dtype) for l in lands],
                   jax.ShapeDtypeStruct((8, 128), F32)),
        in_specs=(*[_HBM] * (2 * n), _ANY),
        out_specs=(_SEM, _SEM, *[_HBM] * (2 * n), pl.BlockSpec(memory_space=pltpu.VMEM)),
        input_output_aliases={i: 2 + i for i in range(2 * n)},
        compiler_params=pltpu.CompilerParams(has_side_effects=_EFFECT),
    )(*[hbm(s) for s in srcs], *[hbm(l) for l in lands], dep)
    return (outs[0], outs[1], list(outs[2:2 + n]), list(outs[2 + n:2 + 2 * n])), outs[-1]


def _send_wait(handles, after, name):
    srcs, lands = _send_wait_many(handles, after, name)
    return srcs[0], lands[0]


def _send_wait_many(handles, after, name, npeers=N_DEV - 1):
    send_sems, recv_sems, src_thrus, land_thrus = handles
    n = len(src_thrus)

    def body(*refs):
        land_refs = refs[n:2 * n]
        send_sems, recv_sems = refs[2 * n], refs[2 * n + 1]
        me = _place()
        for a in range(n):
            for kk in range(npeers):
                cp = pltpu.make_async_remote_copy(
                    src_ref=land_refs[a].at[0], dst_ref=land_refs[a].at[0],
                    send_sem=send_sems.at[a * npeers + kk], recv_sem=recv_sems.at[a * npeers + kk],
                    device_id=me, device_id_type=MESH)
                cp.wait_send()
                cp.wait_recv()

    both = list(src_thrus) + list(land_thrus)
    outs = pl.pallas_call(
        body, name=name,
        out_shape=tuple(pltpu.HBM(t.shape, t.dtype) for t in both),
        in_specs=(*[_HBM] * (2 * n), _SEM, _SEM, _ANY), out_specs=tuple([_HBM] * (2 * n)),
        input_output_aliases={i: i for i in range(2 * n)},
        compiler_params=pltpu.CompilerParams(has_side_effects=_EFFECT),
    )(*both, send_sems, recv_sems, after)
    return list(outs[:n]), list(outs[n:])


def _forward_start(lands, name, dep):
    n = len(lands)

    def body(*refs):
        land_refs = refs[:n]
        send_sems, recv_sems = refs[n + 1], refs[n + 2]
        token = refs[-1]
        x, y, c = _place()
        for a in range(n):
            for j, chip in enumerate([(1 - x, y), (x, 1 - y), (1 - x, 1 - y)]):
                blk = land_refs[a].at[_slot(*chip, c)]
                pltpu.make_async_remote_copy(
                    src_ref=blk, dst_ref=blk, send_sem=send_sems.at[a * 3 + j], recv_sem=recv_sems.at[a * 3 + j],
                    device_id=(x, y, 1 - c), device_id_type=MESH).start()
        token[...] = jnp.zeros_like(token)

    outs = pl.pallas_call(
        body, name=name,
        out_shape=(pltpu.SemaphoreType.DMA((n * 3,)), pltpu.SemaphoreType.DMA((n * 3,)),
                   *[pltpu.HBM(l.shape, l.dtype) for l in lands], jax.ShapeDtypeStruct((8, 128), F32)),
        in_specs=(*[_HBM] * n, _ANY), out_specs=(_SEM, _SEM, *[_HBM] * n, pl.BlockSpec(memory_space=pltpu.VMEM)),
        input_output_aliases={i: 2 + i for i in range(n)},
        compiler_params=pltpu.CompilerParams(has_side_effects=_EFFECT),
    )(*lands, dep)
    return (outs[0], outs[1], list(outs[2:2 + n])), outs[-1]


def _forward_wait(handles, after, name):
    send_sems, recv_sems, land_thrus = handles
    n = len(land_thrus)

    def body(*refs):
        land_refs = refs[:n]
        send_sems, recv_sems = refs[n], refs[n + 1]
        me = _place()
        for a in range(n):
            for j in range(3):
                cp = pltpu.make_async_remote_copy(
                    src_ref=land_refs[a].at[0], dst_ref=land_refs[a].at[0],
                    send_sem=send_sems.at[a * 3 + j], recv_sem=recv_sems.at[a * 3 + j], device_id=me, device_id_type=MESH)
                cp.wait_send()
                cp.wait_recv()

    outs = pl.pallas_call(
        body, name=name,
        out_shape=tuple(pltpu.HBM(t.shape, t.dtype) for t in land_thrus),
        in_specs=(*[_HBM] * n, _SEM, _SEM, _ANY), out_specs=tuple([_HBM] * n),
        input_output_aliases={i: i for i in range(n)},
        compiler_params=pltpu.CompilerParams(has_side_effects=_EFFECT),
    )(*land_thrus, send_sems, recv_sems, after)
    return list(outs)


def _sum_slots(land, name):
    _, R, C = land.shape
    tm = R if R <= 512 else 512

    def body(x_ref, o_ref):
        acc = x_ref[0].astype(F32)
        for j in range(1, N_DEV):
            acc = acc + x_ref[j].astype(F32)
        o_ref[...] = acc

    return pl.pallas_call(
        body, name=name, grid=(R // tm,),
        in_specs=[pl.BlockSpec((N_DEV, tm, C), lambda i: (0, i, 0))], out_specs=pl.BlockSpec((tm, C), lambda i: (i, 0)),
        out_shape=jax.ShapeDtypeStruct((R, C), F32), compiler_params=_cparams(("parallel",)),
    )(land)


def _adam_math(w, g, m, v):
    m2 = ADAM_B1 * m + (1.0 - ADAM_B1) * g
    v2 = ADAM_B2 * v + (1.0 - ADAM_B2) * (g * g)
    m_hat = m2 / (1.0 - ADAM_B1 ** ADAM_STEP)
    v_hat = v2 / (1.0 - ADAM_B2 ** ADAM_STEP)
    delta = -ADAM_LR * (m_hat / (jnp.sqrt(v_hat) + ADAM_EPS) + ADAM_WD * w)
    return delta, m2, v2


def _adamw(w, g, m, v, name):
    R, C = w.shape
    tm = R if R <= 512 else 256
    return _rowwise(lambda w, g, m, v: (_adam_math(w, g, m, v), ()), [w, g, m, v], [], [(C, F32)] * 3, [], tm=tm, name=name)


def _adamw_small(slab, slab_rows, g_conv_w, ws, ms, vs):
    n = len(ws)

    def body(*refs):
        slab_ref, gc_ref = refs[0], refs[1]
        w_refs, m_refs, v_refs = refs[2:2 + n], refs[2 + n:2 + 2 * n], refs[2 + 2 * n:2 + 3 * n]
        outs = refs[2 + 3 * n:]
        loss_ref = outs[0]
        g_out, d_out, m_out, v_out = (outs[1 + i * n:1 + (i + 1) * n] for i in range(4))
        loss_ref[...] = jnp.sum(slab_ref[pl.ds(6, 1), :], axis=1, keepdims=True)
        for i in range(n):
            g = gc_ref[...] if i == n - 1 else slab_ref[pl.ds(slab_rows[i], 1), pl.ds(0, ws[i].shape[1])]
            d, m2, v2 = _adam_math(w_refs[i][...], g, m_refs[i][...], v_refs[i][...])
            g_out[i][...] = g
            d_out[i][...] = d
            m_out[i][...] = m2
            v_out[i][...] = v2

    vm = pl.BlockSpec(memory_space=pltpu.VMEM)
    shapes = [jax.ShapeDtypeStruct(w.shape, F32) for w in ws]
    outs = pl.pallas_call(
        body, name="adamw_small", in_specs=[vm] * (2 + 3 * n), out_specs=[vm] * (1 + 4 * n),
        out_shape=[jax.ShapeDtypeStruct((1, 1), F32)] + shapes * 4,
    )(slab, g_conv_w, *ws, *ms, *vs)
    return outs[0], outs[1:1 + n], outs[1 + n:1 + 2 * n], outs[1 + 2 * n:1 + 3 * n], outs[1 + 3 * n:]


SMALL = ["norm_mix_pre", "norm_mix_post", "norm_mlp_pre", "norm_mlp_post", "norm_ple_post",
         "conv_b", "ssd_norm_g", "dt_bias", "a_log", "d_skip"]


def _pad_row(v, width=D):
    return jnp.pad(v, ((0, 0), (0, width - v.shape[1])))


def kernel(x, p, positions, norm_mix_pre, norm_mix_post, w_in, conv_w, conv_b, dt_bias, a_log, d_skip, ssd_norm_g, w_out, norm_mlp_pre, norm_mlp_post, w_up, w_down, w_ple_gate, w_ple_proj, norm_ple_post, loss_target, m_norm_mix_pre, m_norm_mix_post, m_w_in, m_conv_w, m_conv_b, m_dt_bias, m_a_log, m_d_skip, m_ssd_norm_g, m_w_out, m_norm_mlp_pre, m_norm_mlp_post, m_w_up, m_w_down, m_w_ple_gate, m_w_ple_proj, m_norm_ple_post, v_norm_mix_pre, v_norm_mix_post, v_w_in, v_conv_w, v_conv_b, v_dt_bias, v_a_log, v_d_skip, v_ssd_norm_g, v_w_out, v_norm_mlp_pre, v_norm_mlp_post, v_w_up, v_w_down, v_w_ple_gate, v_w_ple_proj, v_norm_ple_post):
    args = dict(locals())
    x2, p2, tgt = x[0], p[0, 0], loss_target[0]
    g1, g2, g3, g4, g5 = norm_mix_pre, norm_mix_post, norm_mlp_pre, norm_mlp_post, norm_ple_post

    me = _slot(*_place())
    pack_in = jnp.pad(w_in[0].T, ((0, W_IN_SHARD_PAD - W_IN_SHARD), (0, 0))).astype(BF16)
    rest = [w_out[0].astype(BF16), w_up[0].T.astype(BF16), w_down[0].astype(BF16), w_ple_gate[0].astype(BF16),
            w_ple_proj[0].T.reshape(32, D).astype(BF16)]
    conv_pack = jnp.pad(conv_w[0], ((0, 4), (0, 32)))
    in_handles, tok_in0 = _send_start_many([pack_in, conv_pack], False, "gather_in_start", g1, peers=_near_peers, npeers=4)

    inv_freq = ROPE_THETA ** (-jnp.arange(HD // 2, dtype=F32) * 2.0 / HD)
    pos = positions[0] + tok_in0[0, 0].astype(jnp.int32)
    ang = pos.astype(F32)[:, None] * jnp.tile(inv_freq, 4)
    cos128 = jnp.cos(ang)
    sin128 = jnp.sin(ang) * jnp.tile(jnp.concatenate([-jnp.ones(HD // 2, F32), jnp.ones(HD // 2, F32)]), 2)

    bias_w, alog_w, dsk_w = _pad_row(dt_bias, DT_PAD), _pad_row(a_log, DT_PAD), _pad_row(d_skip, DT_PAD)

    (u1,) = _rowwise(lambda a, g: ((a * _rstd(a) * g,), ()), [x2], [g1], [(D, BF16)], [], tm=512, name="norm_x",
                     deps=[cos128, sin128])
    p2b = p2.astype(BF16)

    in_back, in_land = _send_wait_many(in_handles, u1, "gather_in_wait", npeers=4)
    fw_handles, tok_fw = _forward_start(in_land, "gather_in_forward", u1)
    in_land = _forward_wait(fw_handles, tok_fw, "gather_in_forward_wait")
    gin = lax.dynamic_update_slice(in_land[0], in_back[0][None], (me, 0, 0))
    gconv = lax.dynamic_update_slice(in_land[1], in_back[1][None], (me, 0, 0))
    rest_handles, tok_rest = _send_start_many(rest, False, "gather_rest_start", gconv)
    w_inT = gin[:, :W_IN_SHARD].reshape(IN_W, D)
    w_qkvzT = w_inT[:4 * AW]
    w_xbcdtT = jnp.pad(w_inT[4 * AW:], ((0, DT_PAD - HEADS), (0, 0)))
    conv_full = gconv[:, :CONV_K, :96].transpose(1, 0, 2).reshape(CONV_K, CONV_CH)
    qkvz, xbcdt = _mm_rows(lambda a, b: ((a, b), ()), [(u1, w_qkvzT, True), (u1, w_xbcdtT, True)], [], [],
                           [(4 * AW, F32), (CONV_CH + DT_PAD, F32)], [], tm=512, name="proj_in", deps=[tok_rest])

    qkv = _rope_fwd(qkvz, cos128, sin128)
    qkv = [qkv[3 * i:3 * i + 3] for i in range(len(DILATIONS))]
    outs, lses = [], []
    for d, (qd, kd, vd) in zip(DILATIONS, qkv):
        o, l = _attn_fwd(qd, kd, vd, d)
        outs.append(o)
        lses.append(l)
    attn, lse, attn4, lse4, attn16, lse16 = _attn_merge(outs, lses)

    y_ssd, states, cat, act = _ssd_fwd(xbcdt, conv_full, conv_b, bias_w, alog_w, dsk_w, qkvz, attn, ssd_norm_g)


    rest_back, landed = _send_wait_many(rest_handles, cat, "gather_rest_wait")
    landed = [lax.dynamic_update_slice(l, b[None], (me, 0, 0)) for l, b in zip(landed, rest_back)]
    w_o, w_upT, w_dn, w_gate = landed[0].reshape(D, D), landed[1].reshape(DFF, D), landed[2].reshape(DFF, D), landed[3].reshape(D, D)
    w_projT = landed[4].reshape(D, PLE)

    def post1(mm, xx, ga):
        h = xx + mm * _rstd(mm) * ga
        return (mm, h, _rstd(h)), ()
    mix, h1, r3 = _mm_rows(post1, [(cat, w_o, False)], [x2], [g2], [(D, F32), (D, F32), (1, F32)], [], tm=512,
                           name="mix_out")

    a_up, ff, u2, h2, h2b = _mlp_fwd(h1, r3, g3, w_upT, w_dn, g4)
    relu2 = lambda a: jnp.square(jnp.maximum(a.astype(F32), 0.0))

    def final(gpre, ppv, hh, tg, g):
        sg = _sigmoid(gpre)
        ple = ppv * sg
        r = _rstd(ple)
        n = ple * r
        h3 = hh + n * g
        e = h3 - tg
        dh3 = e * (1.0 / D)
        dple = _rms_bwd(n, r, g, dh3)
        return (dh3, dple * sg, dple * ppv * sg * (1.0 - sg)), (_colsum(dh3 * n), _colsum(0.5 * e * e * (1.0 / D)))
    dh3, dpp, dgp, dg5, loss_vec = _mm_rows(final, [(h2b, w_gate, False), (p2b, w_projT, True)], [h2, tgt], [g5],
                                            [(D, F32), (D, BF16), (D, BF16)], [(1, D), (1, D)], tm=512, name="ple_loss")

    gw_projT = _mm(dpp, p2b, ta=True, tm=512, tn=256, tk=T, out_dtypes=(BF16,), name="gw_ple_proj")
    gw_gate = _mm(h2b, dgp, ta=True, tm=512, tn=1024, tk=T, out_dtypes=(BF16,), name="gw_ple_gate")
    def bwd_mlp_post(dg_, d3, f, g):
        dh2 = d3 + dg_
        r = _rstd(f)
        n = f * r
        return (dh2, _rms_bwd(n, r, g, dh2)), (_colsum(dh2 * n),)
    dh2, dff, dg4 = _mm_rows(bwd_mlp_post, [(dgp, w_gate, True)], [dh3, ff], [g4], [(D, F32), (D, BF16)], [(1, D)],
                             tm=512, name="bwd_ple_gate")

    gw_dn = _mm(a_up, dff, ta=True, tm=512, tn=1024, tk=T, a_pre=relu2, out_dtypes=(BF16,), name="gw_mlp_down")
    rs_a, tok_a = _send_start_many([gw_projT.reshape(N_DEV, 32, D), gw_gate.reshape(N_DEV, 128, D),
                                    gw_dn.reshape(N_DEV, 512, D)], True, "rs_start_a", g1)
    da_up, du2 = _mlp_dx(dff, a_up, w_upT, w_dn, tok_a)
    gw_upT = _mm(da_up, u2, ta=True, tm=512, tn=1024, tk=T, out_dtypes=(BF16,), name="gw_mlp_up")

    def bwd_mix_post(d2, du, hh, rr, mm, ga, gb):
        n3 = hh * rr
        dh1 = d2 + _rms_bwd(n3, rr, gb, du)
        r = _rstd(mm)
        n2 = mm * r
        return (dh1, _rms_bwd(n2, r, ga, dh1)), (_colsum(du * n3), _colsum(dh1 * n2))
    dh1, dmix, dg3, dg2 = _rowwise(bwd_mix_post, [dh2, du2, h1, r3, mix], [g2, g3], [(D, F32), (D, BF16)],
                                   [(1, D), (1, D)], tm=512, name="bwd_post_mix")

    gw_o = _mm(cat, dmix, ta=True, tm=512, tn=1024, tk=T, out_dtypes=(BF16,), name="gw_out")
    rs_b, tok_b = _send_start_many([gw_upT.reshape(N_DEV, 512, D), gw_o.reshape(N_DEV, 128, D)], True, "rs_start_b", g1)
    dcat, dattn4, dattn16 = _dx_out(dmix, w_o, tok_b)

    dact, ddtw, ssd_par, dz, dgs = _ssd_bwd(act, xbcdt, bias_w, alog_w, dsk_w, states, y_ssd, qkvz, dcat, ssd_norm_g)
    dxbcdt, conv_par = _conv_bwd(xbcdt, dact, ddtw, conv_full, conv_b)

    qkv_grads = [_attn_bwd(*qkv[0], dcat, attn, lse, 1),
                 _attn_bwd(*qkv[1], dattn4, attn4, lse4, 4),
                 _attn_bwd(*qkv[2], dattn16, attn16, lse16, 16)]
    dqkvz = _rope_bwd(qkv_grads, dz, cos128, sin128)

    gw_qkvzT = _mm(dqkvz, u1, ta=True, tm=512, tn=1024, tk=T, out_dtypes=(BF16,), name="gw_qkvz")
    gw_xbcdtT = _mm(dxbcdt, u1, ta=True, tm=896, tn=1024, tk=T, out_dtypes=(BF16,), name="gw_xbcdt")
    gw_inT = jnp.concatenate([gw_qkvzT, gw_xbcdtT], axis=0)[:IN_W]
    gw_inT = jnp.pad(gw_inT.reshape(N_DEV, W_IN_SHARD, D), ((0, 0), (0, W_IN_SHARD_PAD - W_IN_SHARD), (0, 0)))
    rs_in, tok_in = _send_start(gw_inT, True, "rs_start_w_in", g1)

    def bwd_in(ua, ub, d1, xx, g):
        rr = _rstd(xx)
        n = xx * rr
        du = ua + ub
        return (d1 + _rms_bwd(n, rr, g, du),), (_colsum(du * n),)
    grad_x, dg1 = _mm_rows(bwd_in, [(dqkvz, w_qkvzT, False), (dxbcdt, w_xbcdtT, False)], [dh1, x2], [g1],
                           [(D, F32)], [(1, D)], tm=512, name="bwd_in_proj", deps=[tok_in])

    my_slab = _slab_pack([(dg1, 0), (dg2, 1), (dg3, 2), (dg4, 3), (dg5, 4), (dgs, 5), (loss_vec, 6),
                          (conv_par, 8), (ssd_par, 16)], "slab_pack")
    slab_handles, tok_slab = _send_start_many([my_slab], False, "slab_start", g1)

    def scatter_finish(handles, nm, after):
        part, land = _send_wait(handles, after, "rs_wait_" + nm)
        own = lax.dynamic_slice(part, (me, 0, 0), (1,) + part.shape[1:])
        return _sum_slots(lax.dynamic_update_slice(land, own, (me, 0, 0)), "rs_sum_" + nm)
    def scatter_finish_many(handles, names, after, wait_name):
        parts, lands = _send_wait_many(handles, after, wait_name)
        return [_sum_slots(lax.dynamic_update_slice(land, lax.dynamic_slice(part, (me, 0, 0), (1,) + part.shape[1:]),
                                                    (me, 0, 0)), "rs_sum_" + nm)
                for part, land, nm in zip(parts, lands, names)]
    g_projT, g_gate, g_dn = scatter_finish_many(rs_a, ("w_proj", "w_gate", "w_down"), tok_slab, "rs_wait_a")
    g_upT, g_out = scatter_finish_many(rs_b, ("w_up", "w_out"), tok_slab, "rs_wait_b")

    grads = {
        "w_out": g_out[None], "w_up": g_upT.T[None], "w_down": g_dn[None],
        "w_ple_gate": g_gate[None], "w_ple_proj": g_projT.reshape(128, PLE).T[None],
    }
    delta, new_m, new_v = {}, {}, {}
    for nme in ["w_out", "w_up", "w_down", "w_ple_gate", "w_ple_proj", "w_in"]:
        if nme == "w_in":
            g_inT = scatter_finish(rs_in, "w_in", delta["w_down"])
            grads["w_in"] = g_inT[:W_IN_SHARD].T[None]
        dl, mm_, vv_ = _adamw(args[nme][0], grads[nme][0], args["m_" + nme][0], args["v_" + nme][0], "adamw_" + nme)
        delta[nme], new_m[nme], new_v[nme] = dl[None], mm_[None], vv_[None]

    slab_back, slab_land = _send_wait_many(slab_handles, delta["w_in"], "slab_wait")
    slab = _sum_slots(lax.dynamic_update_slice(slab_land[0], slab_back[0][None], (me, 0, 0)), "slab_sum")
    g_conv_w = lax.dynamic_slice(slab[8:12, :CONV_CH], (0, me * 96), (CONV_K, 96))
    small_names = SMALL + ["conv_w"]
    small_rows = [0, 1, 2, 3, 4, 12, 5, 16, 17, 18, None]
    pick = lambda prefix: [args[prefix + nme] for nme in SMALL] + [args[prefix + "conv_w"][0]]
    loss11, g_s, d_s, m_s, v_s = _adamw_small(slab, small_rows, g_conv_w, pick(""), pick("m_"), pick("v_"))
    loss = loss11[0, 0]
    for i, nme in enumerate(small_names):
        lead = (lambda t: t[None]) if nme == "conv_w" else (lambda t: t)
        grads[nme], delta[nme], new_m[nme], new_v[nme] = lead(g_s[i]), lead(d_s[i]), lead(m_s[i]), lead(v_s[i])

    order = ["norm_mix_pre", "norm_mix_post", "w_in", "conv_w", "conv_b", "dt_bias", "a_log", "d_skip", "ssd_norm_g",
             "w_out", "norm_mlp_pre", "norm_mlp_post", "w_up", "w_down", "w_ple_gate", "w_ple_proj", "norm_ple_post"]
    return (loss, grad_x[None], *[grads[n] for n in order], *[delta[n] for n in order],
            *[new_m[n] for n in order], *[new_v[n] for n in order])
```

```python
import jax
import jax.numpy as jnp
from jax import lax
from jax.experimental import pallas as pl
from jax.experimental.pallas import tpu as pltpu

F32 = jnp.float32
BF16 = jnp.bfloat16
MESH = pl.DeviceIdType.MESH
HIGHEST = lax.Precision.HIGHEST

N_DEV = 8
T = 4096
D = 1024
HEADS = 8
HD = 64
AW = 512
NS = 128
CONV_K = 4
CONV_CH = 768
CHUNK = 128
SSD_PER = 2
DFF = 4096
PLE = 256
EPS = 1e-6
ROPE_THETA = 10000.0
DILATIONS = (1, 4, 16)
QBLK = 128
NEG = -1e30
IN_W = 2824
W_IN_SHARD = 353
W_IN_SHARD_PAD = 384
DT_PAD = 128

ADAM_LR, ADAM_B1, ADAM_B2, ADAM_EPS, ADAM_WD, ADAM_STEP = 0.001, 0.9, 0.999, 1e-08, 0.01, 10

VMEM_LIMIT = 56 * 1024 * 1024


_ANY = pl.BlockSpec(memory_space=pl.ANY)


def _cparams(sem=None):
    return pltpu.CompilerParams(dimension_semantics=sem, vmem_limit_bytes=VMEM_LIMIT)


def _dot(a, b, ca, cb, precision=None):
    return lax.dot_general(a, b, (((ca,), (cb,)), ((), ())), preferred_element_type=F32, precision=precision)


def _nn(a, b):
    return _dot(a, b, 1, 0)


def _nt(a, b):
    return _dot(a, b, 1, 1)


def _tn(a, b):
    return _dot(a, b, 0, 0)


def _sigmoid(x):
    return 1.0 / (1.0 + jnp.exp(-x))


def _softplus(x):
    return jnp.maximum(x, 0.0) + jnp.log(1.0 + jnp.exp(-jnp.abs(x)))


def _mm(a, b, *, ta=False, tb=False, tm, tn, tk, name,
        a_pre=None, a_rows=(), a_cols=(), b_pre=None, b_rows=(), b_cols=(),
        epi=None, epi_tiles=(), out_dtypes=(F32,), deps=()):
    if ta:
        K, M = a.shape
    else:
        M, K = a.shape
    if tb:
        N, K2 = b.shape
    else:
        K2, N = b.shape
    assert K == K2 and M % tm == 0 and N % tn == 0 and K % tk == 0, (name, a.shape, b.shape)
    nk = K // tk
    if ta:
        a_spec = pl.BlockSpec((tk, tm), lambda i, j, k: (k, i))
        a_row_specs = [pl.BlockSpec((tk, 1), lambda i, j, k: (k, 0)) for _ in a_rows]
        a_col_specs = [pl.BlockSpec((1, tm), lambda i, j, k: (0, i)) for _ in a_cols]
    else:
        a_spec = pl.BlockSpec((tm, tk), lambda i, j, k: (i, k))
        a_row_specs = [pl.BlockSpec((tm, 1), lambda i, j, k: (i, 0)) for _ in a_rows]
        a_col_specs = [pl.BlockSpec((1, tk), lambda i, j, k: (0, k)) for _ in a_cols]
    if tb:
        b_spec = pl.BlockSpec((tn, tk), lambda i, j, k: (j, k))
        b_row_specs = [pl.BlockSpec((tn, 1), lambda i, j, k: (j, 0)) for _ in b_rows]
        b_col_specs = [pl.BlockSpec((1, tk), lambda i, j, k: (0, k)) for _ in b_cols]
    else:
        b_spec = pl.BlockSpec((tk, tn), lambda i, j, k: (k, j))
        b_row_specs = [pl.BlockSpec((tk, 1), lambda i, j, k: (k, 0)) for _ in b_rows]
        b_col_specs = [pl.BlockSpec((1, tn), lambda i, j, k: (0, j)) for _ in b_cols]
    o_spec = pl.BlockSpec((tm, tn), lambda i, j, k: (i, j))
    na, nb, ne, no = len(a_rows) + len(a_cols), len(b_rows) + len(b_cols), len(epi_tiles), len(out_dtypes)

    def body(*refs):
        a_ref, b_ref = refs[0], refs[1]
        a_ex = refs[2:2 + na]
        b_ex = refs[2 + na:2 + na + nb]
        e_ex = refs[2 + na + nb:2 + na + nb + ne]
        first_out = 2 + na + nb + ne + len(deps)
        outs = refs[first_out:first_out + no]

        def finish(res):
            vals = epi(res, *[r[...] for r in e_ex]) if epi is not None else (res,)
            for o_ref, val in zip(outs, vals):
                o_ref[...] = val.astype(o_ref.dtype)

        at = a_ref[...]
        if a_pre is not None:
            at = a_pre(at, *[r[...] for r in a_ex])
        bt = b_ref[...]
        if b_pre is not None:
            bt = b_pre(bt, *[r[...] for r in b_ex])
        prod = _dot(at.astype(BF16), bt.astype(BF16), 0 if ta else 1, 1 if tb else 0)
        if nk == 1:
            finish(prod)
            return
        acc = refs[-1]
        k = pl.program_id(2)

        @pl.when(k == 0)
        def _():
            acc[...] = jnp.zeros_like(acc)
        acc[...] += prod

        @pl.when(k == nk - 1)
        def _():
            finish(acc[...])

    outs = pl.pallas_call(
        body, name=name,
        grid=(M // tm, N // tn, nk),
        in_specs=([a_spec, b_spec] + a_row_specs + a_col_specs + b_row_specs + b_col_specs + [o_spec] * ne
                  + [_ANY] * len(deps)),
        out_specs=[o_spec] * no,
        out_shape=[jax.ShapeDtypeStruct((M, N), dt) for dt in out_dtypes],
        scratch_shapes=[pltpu.VMEM((tm, tn), F32)] if nk > 1 else [],
        compiler_params=_cparams(("parallel", "parallel", "arbitrary")),
    )(a, b, *a_rows, *a_cols, *b_rows, *b_cols, *epi_tiles, *deps)
    return outs[0] if no == 1 else outs


MLP_TM = 1024
MLP_TC = 512


def _mlp_fwd(h, r, g, w_upT, w_dn, g_post):
    nc = DFF // MLP_TC

    def body(h_ref, r_ref, g_ref, wu_ref, wd_ref, gp_ref, a_ref, ff_ref, u_ref, ho_ref, hob_ref, acc, u_scr):
        c = pl.program_id(1)

        @pl.when(c == 0)
        def _():
            u = (h_ref[...] * r_ref[...] * g_ref[...]).astype(BF16)
            u_scr[...] = u
            u_ref[...] = u
            acc[...] = jnp.zeros_like(acc)
        a = _nt(u_scr[...], wu_ref[...])
        a_ref[...] = a.astype(BF16)
        acc[...] += _nn(jnp.square(jnp.maximum(a, 0.0)).astype(BF16), wd_ref[...])

        @pl.when(c == nc - 1)
        def _():
            f = acc[...]
            ff_ref[...] = f
            ho = h_ref[...] + f * _rstd(f) * gp_ref[...]
            ho_ref[...] = ho
            hob_ref[...] = ho.astype(BF16)

    row = pl.BlockSpec((MLP_TM, D), lambda i, c: (i, 0))
    wsp = pl.BlockSpec((MLP_TC, D), lambda i, c: (c, 0))
    vec = pl.BlockSpec((1, D), lambda i, c: (0, 0))
    return pl.pallas_call(
        body, name="mlp_fwd", grid=(T // MLP_TM, nc),
        in_specs=[row, pl.BlockSpec((MLP_TM, 1), lambda i, c: (i, 0)), vec, wsp, wsp, vec],
        out_specs=[pl.BlockSpec((MLP_TM, MLP_TC), lambda i, c: (i, c)), row, row, row, row],
        out_shape=[jax.ShapeDtypeStruct((T, DFF), BF16), jax.ShapeDtypeStruct((T, D), F32), jax.ShapeDtypeStruct((T, D), BF16),
                   jax.ShapeDtypeStruct((T, D), F32), jax.ShapeDtypeStruct((T, D), BF16)],
        scratch_shapes=[pltpu.VMEM((MLP_TM, D), F32), pltpu.VMEM((MLP_TM, D), BF16)],
        compiler_params=_cparams(("parallel", "arbitrary")),
    )(h, r, g, w_upT, w_dn, g_post)


def _mlp_dx(dff, a, w_upT, w_dn, dep):
    nc = DFF // MLP_TC

    def body(d_ref, a_ref, wu_ref, wd_ref, dep_ref, da_ref, du_ref, acc, d_scr):
        c = pl.program_id(1)

        @pl.when(c == 0)
        def _():
            d_scr[...] = d_ref[...].astype(BF16)
            acc[...] = jnp.zeros_like(acc)
        da = (_nt(d_scr[...], wd_ref[...]) * (2.0 * jnp.maximum(a_ref[...].astype(F32), 0.0))).astype(BF16)
        da_ref[...] = da
        acc[...] += _nn(da, wu_ref[...])

        @pl.when(c == nc - 1)
        def _():
            du_ref[...] = acc[...]

    row = pl.BlockSpec((MLP_TM, D), lambda i, c: (i, 0))
    wsp = pl.BlockSpec((MLP_TC, D), lambda i, c: (c, 0))
    chunk = pl.BlockSpec((MLP_TM, MLP_TC), lambda i, c: (i, c))
    return pl.pallas_call(
        body, name="mlp_dx", grid=(T // MLP_TM, nc),
        in_specs=[row, chunk, wsp, wsp, _ANY], out_specs=[chunk, row],
        out_shape=[jax.ShapeDtypeStruct((T, DFF), BF16), jax.ShapeDtypeStruct((T, D), F32)],
        scratch_shapes=[pltpu.VMEM((MLP_TM, D), F32), pltpu.VMEM((MLP_TM, D), BF16)],
        compiler_params=_cparams(("parallel", "arbitrary")),
    )(dff, a, w_upT, w_dn, dep)


def _rowwise(fn, rows, vecs, out_rows, out_sums, *, tm, name, deps=()):
    specs, arrs = [], []
    R = None
    for r in rows:
        if isinstance(r, tuple):
            arr, width, cb = r
            specs.append(pl.BlockSpec((tm, width), lambda i, cb=cb: (i, cb)))
        else:
            arr = r
            specs.append(pl.BlockSpec((tm, arr.shape[1]), lambda i: (i, 0)))
        R = arr.shape[0] if R is None else R
        assert arr.shape[0] == R, name
        arrs.append(arr)
    assert R % tm == 0, name
    for v in vecs:
        specs.append(pl.BlockSpec(v.shape, lambda i: (0, 0)))
        arrs.append(v)
    nr, nv, no, ns = len(rows), len(vecs), len(out_rows), len(out_sums)
    out_specs = [pl.BlockSpec((tm, w), lambda i: (i, 0)) for w, _ in out_rows]
    out_specs += [pl.BlockSpec(s, lambda i: (0, 0)) for s in out_sums]
    out_shape = [jax.ShapeDtypeStruct((R, w), dt) for w, dt in out_rows]
    out_shape += [jax.ShapeDtypeStruct(s, F32) for s in out_sums]

    nd = len(deps)

    def body(*refs):
        ins = [r[...] for r in refs[:nr + nv]]
        o_refs = refs[nr + nv + nd:nr + nv + nd + no]
        s_refs = refs[nr + nv + nd + no:]
        o_vals, s_vals = fn(*ins)
        for ref, val in zip(o_refs, o_vals):
            ref[...] = val.astype(ref.dtype)
        if ns:
            @pl.when(pl.program_id(0) == 0)
            def _():
                for ref in s_refs:
                    ref[...] = jnp.zeros_like(ref)
            for ref, val in zip(s_refs, s_vals):
                ref[...] += val

    outs = pl.pallas_call(
        body, name=name, grid=(R // tm,), in_specs=specs + [_ANY] * nd, out_specs=out_specs, out_shape=out_shape,
        compiler_params=_cparams(("arbitrary",) if ns else ("parallel",)),
    )(*arrs, *deps)
    return outs


def _mm_rows(fn, mats, rows, vecs, out_rows, out_sums, *, tm, name, deps=()):
    R = mats[0][0].shape[0]
    assert R % tm == 0, name
    specs, arrs = [], []
    for a, b, tb in mats:
        specs += [pl.BlockSpec((tm, a.shape[1]), lambda i: (i, 0)), pl.BlockSpec(b.shape, lambda i: (0, 0))]
        arrs += [a, b]
    for r in rows:
        specs.append(pl.BlockSpec((tm, r.shape[1]), lambda i: (i, 0)))
        arrs.append(r)
    for v in vecs:
        specs.append(pl.BlockSpec(v.shape, lambda i: (0, 0)))
        arrs.append(v)
    nm, nr, nv, nd, no, ns = len(mats), len(rows), len(vecs), len(deps), len(out_rows), len(out_sums)
    out_specs = [pl.BlockSpec((tm, w), lambda i: (i, 0)) for w, _ in out_rows]
    out_specs += [pl.BlockSpec(s, lambda i: (0, 0)) for s in out_sums]
    out_shape = [jax.ShapeDtypeStruct((R, w), dt) for w, dt in out_rows] + [jax.ShapeDtypeStruct(s, F32) for s in out_sums]

    def body(*refs):
        prods = [_dot(refs[2 * p][...].astype(BF16), refs[2 * p + 1][...].astype(BF16), 1, 1 if mats[p][2] else 0)
                 for p in range(nm)]
        ins = [r[...] for r in refs[2 * nm:2 * nm + nr + nv]]
        first_out = 2 * nm + nr + nv + nd
        o_refs, s_refs = refs[first_out:first_out + no], refs[first_out + no:]
        o_vals, s_vals = fn(*prods, *ins)
        for ref, val in zip(o_refs, o_vals):
            ref[...] = val.astype(ref.dtype)
        if ns:
            @pl.when(pl.program_id(0) == 0)
            def _():
                for ref in s_refs:
                    ref[...] = jnp.zeros_like(ref)
            for ref, val in zip(s_refs, s_vals):
                ref[...] += val

    return pl.pallas_call(
        body, name=name, grid=(R // tm,), in_specs=specs + [_ANY] * nd, out_specs=out_specs, out_shape=out_shape,
        compiler_params=_cparams(("arbitrary",) if ns else ("parallel",)),
    )(*arrs, *deps)


def _colsum(x):
    return jnp.sum(x, axis=0, keepdims=True)


def _rstd(x):
    return lax.rsqrt(jnp.mean(x * x, axis=-1, keepdims=True) + EPS)


def _rms_bwd(xn, r, g, dy):
    dn = dy * g
    return r * (dn - xn * jnp.mean(dn * xn, axis=-1, keepdims=True))


def _partner(t):
    lane = lax.broadcasted_iota(jnp.int32, t.shape, 1)
    up = pltpu.roll(t, 96, 1)
    down = pltpu.roll(t, 32, 1)
    return jnp.where((lane % 64) < 32, up, down)


SLABS = AW // 128


def _rows(r, n, d):
    return pl.ds(r, n, stride=d) if d > 1 else pl.ds(0, n)


def _undilate(src_ref, dst, d, tm):
    for r in range(d):
        for j in range(SLABS):
            dst[j][_rows(r, tm // d, d), :] = src_ref[:, pl.ds(r * AW + j * 128, 128)].astype(dst[j].dtype)


def _dilate(dst_ref, src, d, tm):
    for r in range(d):
        for j in range(SLABS):
            dst_ref[:, pl.ds(r * AW + j * 128, 128)] = src[j][_rows(r, tm // d, d), :].astype(dst_ref.dtype)


def _slab_scratch(n, tm):
    return [pltpu.VMEM((tm, 128), F32)] * (SLABS * n)


def _slab_groups(flat):
    return [flat[SLABS * i:SLABS * (i + 1)] for i in range(len(flat) // SLABS)]


def _slab_specs(tm, first):
    return [pl.BlockSpec((tm, 128), lambda i, j=j: (i, first + j)) for j in range(SLABS)]


def _dil_spec(tm, d):
    return pl.BlockSpec((tm // d, d * AW), lambda i: (i, 0))


ROPE_TM = 512


def _rope_fwd(qkvz, cos128, sin128):
    tm = ROPE_TM

    def body(*refs):
        q_refs, k_refs, v_refs = refs[0:4], refs[4:8], refs[8:12]
        c_ref, s_ref = refs[12], refs[13]
        outs = refs[14:23]
        qs, ks = _slab_groups(refs[23:])
        c, s = c_ref[...], s_ref[...]
        for j in range(SLABS):
            q, k = q_refs[j][...], k_refs[j][...]
            qs[j][...] = (q * c + _partner(q) * s) * (HD ** -0.5)
            ks[j][...] = k * c + _partner(k) * s
        for di, d in enumerate(DILATIONS):
            oq, ok, ov = outs[3 * di:3 * di + 3]
            for r in range(d):
                rows = _rows(r, tm // d, d)
                for j in range(SLABS):
                    cols = pl.ds(r * AW + j * 128, 128)
                    oq[:, cols] = qs[j][rows, :].astype(BF16)
                    ok[:, cols] = ks[j][rows, :].astype(BF16)
                    ov[:, cols] = v_refs[j][rows, :].astype(BF16)

    tab = pl.BlockSpec((tm, 128), lambda i: (i, 0))
    out_specs, out_shape = [], []
    for d in DILATIONS:
        out_specs += [_dil_spec(tm, d)] * 3
        out_shape += [jax.ShapeDtypeStruct((T // d, d * AW), BF16)] * 3
    return pl.pallas_call(
        body, name="rope_fwd", grid=(T // tm,),
        in_specs=_slab_specs(tm, 0) + _slab_specs(tm, 4) + _slab_specs(tm, 8) + [tab, tab],
        out_specs=out_specs, out_shape=out_shape, scratch_shapes=_slab_scratch(2, tm),
        compiler_params=_cparams(("parallel",)),
    )(*([qkvz] * 12), cos128, sin128)


def _rope_bwd(grads, dz, cos128, sin128):
    tm = ROPE_TM

    def body(*refs):
        g_refs = refs[0:9]
        dz_ref, c_ref, s_ref, o_ref = refs[9], refs[10], refs[11], refs[12]
        scr = _slab_groups(refs[13:])
        for di, d in enumerate(DILATIONS[1:]):
            for t in range(3):
                _undilate(g_refs[3 * (di + 1) + t], scr[3 * di + t], d, tm)
        c, s = c_ref[...], s_ref[...]
        for j in range(SLABS):
            cols = pl.ds(j * 128, 128)
            tot = [g_refs[t][:, cols] + scr[t][j][...] + scr[3 + t][j][...] for t in range(3)]
            dqr = tot[0] * (HD ** -0.5)
            o_ref[:, pl.ds(j * 128, 128)] = (dqr * c + _partner(dqr * s)).astype(BF16)
            o_ref[:, pl.ds(AW + j * 128, 128)] = (tot[1] * c + _partner(tot[1] * s)).astype(BF16)
            o_ref[:, pl.ds(2 * AW + j * 128, 128)] = tot[2].astype(BF16)
        o_ref[:, pl.ds(3 * AW, AW)] = dz_ref[...].astype(BF16)

    tab = pl.BlockSpec((tm, 128), lambda i: (i, 0))
    in_specs, args = [], []
    for d, g in zip(DILATIONS, grads):
        in_specs += [_dil_spec(tm, d)] * 3
        args += list(g)
    return pl.pallas_call(
        body, name="rope_bwd", grid=(T // tm,),
        in_specs=in_specs + [pl.BlockSpec((tm, AW), lambda i: (i, 0)), tab, tab],
        out_specs=pl.BlockSpec((tm, 4 * AW), lambda i: (i, 0)),
        out_shape=jax.ShapeDtypeStruct((T, 4 * AW), BF16),
        scratch_shapes=_slab_scratch(6, tm),
        compiler_params=_cparams(("parallel",)),
    )(*args, dz, cos128, sin128)


def _dx_out(dmix, w_o, dep):
    tm = ROPE_TM

    def body(a_ref, w_ref, dep_ref, dcat_ref, o4, o16, *slabs):
        prod = _nt(a_ref[...].astype(BF16), w_ref[...].astype(BF16))
        dcat_ref[...] = prod
        for j in range(SLABS):
            slabs[j][...] = prod[:, 128 * j:128 * (j + 1)]
        _dilate(o4, slabs, 4, tm)
        _dilate(o16, slabs, 16, tm)

    return pl.pallas_call(
        body, name="dx_out", grid=(T // tm,),
        in_specs=[pl.BlockSpec((tm, D), lambda i: (i, 0)), pl.BlockSpec((D, D), lambda i: (0, 0)), _ANY],
        out_specs=[pl.BlockSpec((tm, D), lambda i: (i, 0)), _dil_spec(tm, 4), _dil_spec(tm, 16)],
        out_shape=[jax.ShapeDtypeStruct((T, D), F32), jax.ShapeDtypeStruct((T // 4, 4 * AW), F32),
                   jax.ShapeDtypeStruct((T // 16, 16 * AW), F32)],
        scratch_shapes=_slab_scratch(1, tm), compiler_params=_cparams(("parallel",)),
    )(dmix, w_o, dep)


def _band_masks():
    qi = lax.broadcasted_iota(jnp.int32, (QBLK, QBLK), 0)
    kj = lax.broadcasted_iota(jnp.int32, (QBLK, QBLK), 1)
    return kj >= qi, kj <= qi


def _attn_fwd(q, k, v, d):
    L = q.shape[0]
    npair = L // (2 * QBLK)

    def body(q_ref, kp_ref, kc_ref, vp_ref, vc_ref, o_ref, l_ref):
        pair = pl.program_id(1)
        mask_p, mask_c = _band_masks()
        for sub in range(2):
            rows = pl.ds(sub * QBLK, QBLK)
            first = jnp.where(pair > 0, 0.0, NEG) if sub == 0 else 0.0
            bias = jnp.concatenate([jnp.where(mask_p, 0.0, NEG) + first, jnp.where(mask_c, 0.0, NEG)], axis=1)
            k_prev = (lambda sl: kp_ref[:, sl]) if sub == 0 else (lambda sl: kc_ref[pl.ds(0, QBLK), sl])
            v_prev = (lambda sl: vp_ref[:, sl]) if sub == 0 else (lambda sl: vc_ref[pl.ds(0, QBLK), sl])
            s = []
            for h in range(HEADS):
                sl = pl.ds(HD * h, HD)
                qh = q_ref[rows, sl]
                s.append(jnp.concatenate([_nt(qh, k_prev(sl)), _nt(qh, kc_ref[rows, sl])], axis=1))
            s = jnp.stack(s) + bias
            m = jnp.max(s, axis=2, keepdims=True)
            e = jnp.exp(s - m)
            den = jnp.sum(e, axis=2, keepdims=True)
            p = e.astype(BF16)
            inv = 1.0 / den
            lse = m + jnp.log(den)
            for h in range(HEADS):
                sl = pl.ds(HD * h, HD)
                o_ref[rows, sl] = ((_nn(p[h, :, :QBLK], v_prev(sl)) + _nn(p[h, :, QBLK:], vc_ref[rows, sl])) * inv[h]
                                   ).astype(BF16)
                l_ref[rows, sl] = jnp.broadcast_to(lse[h], (QBLK, HD))

    cur = pl.BlockSpec((2 * QBLK, AW), lambda r, n: (n, r))
    prev = pl.BlockSpec((QBLK, AW), lambda r, n: (jnp.maximum(2 * n - 1, 0), r))
    return pl.pallas_call(
        body, name=f"attn_fwd_d{d}", grid=(d, npair),
        in_specs=[cur, prev, cur, prev, cur], out_specs=[cur, cur],
        out_shape=[jax.ShapeDtypeStruct((L, d * AW), BF16), jax.ShapeDtypeStruct((L, d * AW), F32)],
        compiler_params=_cparams(("parallel", "parallel")),
    )(q, k, k, v, v)


def _attn_bwd(q, k, v, do, at, lse, d):
    L = q.shape[0]
    nb = L // QBLK
    npair = nb // 2

    def body(qc_ref, qn_ref, kp_ref, kc_ref, vp_ref, vc_ref, doc_ref, don_ref, atc_ref, atn_ref,
             lc_ref, ln_ref, dq_ref, dk_ref, dv_ref):
        pair = pl.program_id(1)
        mask_p, mask_c = _band_masks()
        prev_bias = jnp.where(mask_p, 0.0, NEG)
        for sub in range(2):
            rows = pl.ds(sub * QBLK, QBLK)
            second = pl.ds(QBLK, QBLK)
            if sub == 0:
                take = lambda cur_ref, nxt_ref, cols, i: cur_ref[rows if i == 0 else second, cols]
                prev_of = lambda p_ref, c_ref, cols: p_ref[:, cols]
                first, last = jnp.where(pair > 0, 0.0, NEG), 0.0
            else:
                take = lambda cur_ref, nxt_ref, cols, i: cur_ref[rows, cols] if i == 0 else nxt_ref[:, cols]
                prev_of = lambda p_ref, c_ref, cols: c_ref[pl.ds(0, QBLK), cols]
                first, last = 0.0, jnp.where(pair < npair - 1, 0.0, NEG)
            bias = jnp.concatenate([prev_bias + first, jnp.where(mask_c, 0.0, NEG), prev_bias + last], axis=1)
            s, dp, ls, dl, ops = [], [], [], [], []
            for h in range(HEADS):
                sl = pl.ds(HD * h, HD)
                one = pl.ds(HD * h, 1)
                q0, q1 = take(qc_ref, qn_ref, sl, 0), take(qc_ref, qn_ref, sl, 1)
                kp, kc = prev_of(kp_ref, kc_ref, sl), kc_ref[rows, sl]
                vp, vc = prev_of(vp_ref, vc_ref, sl), vc_ref[rows, sl]
                do0, do1 = take(doc_ref, don_ref, sl, 0), take(doc_ref, don_ref, sl, 1)
                do0b, do1b = do0.astype(BF16), do1.astype(BF16)
                s.append(jnp.concatenate([_nt(q0, kp), _nt(q0, kc), _nt(q1, kc)], axis=1))
                dp.append(jnp.concatenate([_nt(do0b, vp), _nt(do0b, vc), _nt(do1b, vc)], axis=1))
                dl0 = jnp.sum(do0 * take(atc_ref, atn_ref, sl, 0), axis=1, keepdims=True)
                dl1 = jnp.sum(do1 * take(atc_ref, atn_ref, sl, 1), axis=1, keepdims=True)
                dl.append(jnp.concatenate([jnp.broadcast_to(dl0, (QBLK, 2 * QBLK)), jnp.broadcast_to(dl1, (QBLK, QBLK))], axis=1))
                ls.append(jnp.concatenate([jnp.broadcast_to(take(lc_ref, ln_ref, one, 0), (QBLK, 2 * QBLK)),
                                           jnp.broadcast_to(take(lc_ref, ln_ref, one, 1), (QBLK, QBLK))], axis=1))
                ops.append((q0, q1, kp, kc, do0b, do1b))
            p = jnp.exp(jnp.stack(s) + bias - jnp.stack(ls))
            ds = (p * (jnp.stack(dp) - jnp.stack(dl))).astype(BF16)
            p = p.astype(BF16)
            for h in range(HEADS):
                sl = pl.ds(HD * h, HD)
                q0, q1, kp, kc, do0b, do1b = ops[h]
                dq_ref[rows, sl] = (_nn(ds[h, :, :QBLK], kp) + _nn(ds[h, :, QBLK:2 * QBLK], kc)).astype(BF16)
                dv_ref[rows, sl] = (_tn(p[h, :, QBLK:2 * QBLK], do0b) + _tn(p[h, :, 2 * QBLK:], do1b)).astype(BF16)
                dk_ref[rows, sl] = (_tn(ds[h, :, QBLK:2 * QBLK], q0) + _tn(ds[h, :, 2 * QBLK:], q1)).astype(BF16)

    cur = pl.BlockSpec((2 * QBLK, AW), lambda r, n: (n, r))
    prev = pl.BlockSpec((QBLK, AW), lambda r, n: (jnp.maximum(2 * n - 1, 0), r))
    nxt = pl.BlockSpec((QBLK, AW), lambda r, n: (jnp.minimum(2 * n + 2, nb - 1), r))
    return pl.pallas_call(
        body, name=f"attn_bwd_d{d}", grid=(d, npair),
        in_specs=[cur, nxt, prev, cur, prev, cur, cur, nxt, cur, nxt, cur, nxt], out_specs=[cur, cur, cur],
        out_shape=[jax.ShapeDtypeStruct((L, d * AW), BF16)] * 3,
        compiler_params=_cparams(("parallel", "parallel")),
    )(q, q, k, k, v, v, do, do, at, at, lse, lse)


def _attn_merge(outs, lses):
    tm = ROPE_TM

    def body(o1, o4, o16, l1, l4, l16, at_ref, ls_ref, at4, ls4, at16, ls16, *flat):
        so4, so16, sl4, sl16, sa, sl = _slab_groups(flat)
        _undilate(o4, so4, 4, tm)
        _undilate(o16, so16, 16, tm)
        _undilate(l4, sl4, 4, tm)
        _undilate(l16, sl16, 16, tm)
        for j in range(SLABS):
            cols = pl.ds(j * 128, 128)
            a, b, c = l1[:, cols], sl4[j][...], sl16[j][...]
            m = jnp.maximum(jnp.maximum(a, b), c)
            e1, e2, e3 = jnp.exp(a - m), jnp.exp(b - m), jnp.exp(c - m)
            s = e1 + e2 + e3
            inv = 1.0 / s
            attn = (e1 * inv) * o1[:, cols] + (e2 * inv) * so4[j][...] + (e3 * inv) * so16[j][...]
            lse = m + jnp.log(s)
            at_ref[:, cols] = attn
            ls_ref[:, cols] = lse
            sa[j][...] = attn
            sl[j][...] = lse
        _dilate(at4, sa, 4, tm)
        _dilate(at16, sa, 16, tm)
        _dilate(ls4, sl, 4, tm)
        _dilate(ls16, sl, 16, tm)

    specs = [_dil_spec(tm, d) for d in DILATIONS]
    tok = specs[0]
    return pl.pallas_call(
        body, name="attn_merge", grid=(T // tm,),
        in_specs=specs + specs, out_specs=[tok, tok, specs[1], specs[1], specs[2], specs[2]],
        out_shape=[jax.ShapeDtypeStruct((T, AW), F32)] * 2 + [jax.ShapeDtypeStruct((T // 4, 4 * AW), F32)] * 2
        + [jax.ShapeDtypeStruct((T // 16, 16 * AW), F32)] * 2,
        scratch_shapes=_slab_scratch(6, tm),
        compiler_params=_cparams(("parallel",)),
    )(*outs, *lses)


CONV_TM = 512
HALO = 8


def _conv_pre(ext, w, b):
    y = b + w[3] * ext
    for kk in range(1, CONV_K):
        y = y + w[3 - kk] * pltpu.roll(ext, kk, 0)
    return y


def _rows_to_block(rows, n, width):
    ri = lax.broadcasted_iota(jnp.int32, (n, width), 0)
    out = jnp.zeros((n, width), F32)
    for j, r in enumerate(rows):
        out = out + jnp.where(ri == j, r, 0.0)
    return out


def _conv_bwd(xbc, dact, ddt, w, b):
    nblk = T // CONV_TM
    per = CONV_TM // HALO

    def body(x_ref, xb_ref, xa_ref, g_ref, ga_ref, ddt_ref, w_ref, b_ref, dx_ref, dw_ref):
        i = pl.program_id(0)
        wv = [w_ref[pl.ds(j, 1), :] for j in range(CONV_K)]
        before = jnp.where(i > 0, xb_ref[...], 0.0)
        last = i == nblk - 1
        after = jnp.where(last, 0.0, xa_ref[...])
        g_after = jnp.where(last, 0.0, ga_ref[...])
        ext = jnp.concatenate([before, x_ref[...], after], axis=0)
        y = _conv_pre(ext, wv, b_ref[...])[HALO:]
        sg = _sigmoid(y)
        dy = jnp.concatenate([g_ref[...], g_after], axis=0) * (sg * (1.0 + y * (1.0 - sg)))
        n = CONV_TM + HALO
        dx = wv[3] * dy
        for kk in range(1, CONV_K):
            dx = dx + wv[3 - kk] * pltpu.roll(dy, n - kk, 0)
        dx_ref[:, pl.ds(0, CONV_CH)] = dx[:CONV_TM].astype(BF16)
        dx_ref[:, pl.ds(CONV_CH, DT_PAD)] = ddt_ref[...].astype(BF16)
        dyc = dy[:CONV_TM]
        rows = [jnp.sum(dyc * (pltpu.roll(ext, 3 - j, 0) if j < 3 else ext)[HALO:HALO + CONV_TM], axis=0, keepdims=True)
                for j in range(CONV_K)]
        rows.append(jnp.sum(dyc, axis=0, keepdims=True))
        part = _rows_to_block(rows, 8, CONV_CH)

        @pl.when(i == 0)
        def _():
            dw_ref[...] = jnp.zeros_like(dw_ref)
        dw_ref[...] += part

    blk = pl.BlockSpec((CONV_TM, CONV_CH), lambda i: (i, 0))
    hb = pl.BlockSpec((HALO, CONV_CH), lambda i: (jnp.maximum(i * per - 1, 0), 0))
    ha = pl.BlockSpec((HALO, CONV_CH), lambda i: (jnp.minimum((i + 1) * per, T // HALO - 1), 0))
    return pl.pallas_call(
        body, name="conv_bwd", grid=(nblk,),
        in_specs=[blk, hb, ha, blk, ha, pl.BlockSpec((CONV_TM, DT_PAD), lambda i: (i, 0)),
                  pl.BlockSpec((CONV_K, CONV_CH), lambda i: (0, 0)), pl.BlockSpec((1, CONV_CH), lambda i: (0, 0))],
        out_specs=[pl.BlockSpec((CONV_TM, CONV_CH + DT_PAD), lambda i: (i, 0)), pl.BlockSpec((8, CONV_CH), lambda i: (0, 0))],
        out_shape=[jax.ShapeDtypeStruct((T, CONV_CH + DT_PAD), BF16), jax.ShapeDtypeStruct((8, CONV_CH), F32)],
        compiler_params=_cparams(("arbitrary",)),
    )(xbc, xbc, xbc, dact, dact, ddt, w, b)


def _pick(mat, h):
    lane = lax.broadcasted_iota(jnp.int32, mat.shape, 1)
    return jnp.sum(jnp.where(lane == h, mat, 0.0), axis=1, keepdims=True)


def _heads(fn):
    return jnp.stack([fn(h) for h in range(HEADS)])


def _ssd_prep(dt_ref, bias_ref, alog_ref, dsk_ref, b_ref, c_ref, xs_ref, state_ref, cst):
    li = lax.broadcasted_iota(jnp.int32, (CHUNK, CHUNK), 0)
    si = lax.broadcasted_iota(jnp.int32, (CHUNK, CHUNK), 1)
    tri = li >= si
    dtp = dt_ref[...] + bias_ref[...]
    dt = _softplus(dtp)
    A = -jnp.exp(alog_ref[...])
    a = dt * A
    cs = jnp.dot(tri.astype(F32), a, precision=HIGHEST, preferred_element_type=F32)
    cst[...] = cs.T
    Bm = b_ref[...].astype(BF16)
    Cm = c_ref[...].astype(BF16)
    cb = _nt(Cm, Bm)
    dskv = dsk_ref[...]
    cs_col = _heads(lambda h: _pick(cs, h))
    cs_row = _heads(lambda h: cst[pl.ds(h, 1), :])
    dt_col = _heads(lambda h: _pick(dt, h))
    dsk_col = _heads(lambda h: _pick(dskv, h))
    lam = jnp.exp(jnp.where(tri, cs_col - cs_row, NEG))
    x = _heads(lambda h: xs_ref[:, pl.ds(HD * h, HD)])
    xdt = x * dt_col
    prev = _heads(lambda h: state_ref[pl.ds(HD * h, HD), :])
    lane = lax.broadcasted_iota(jnp.int32, (1, 1, CHUNK), 2)
    cl = jnp.sum(jnp.where(lane == CHUNK - 1, cs_row, 0.0), axis=2, keepdims=True)
    f = jnp.exp(cl - cs_col)
    return dict(li=li, si=si, dtp=dtp, dt=dt, A=A, Bm=Bm, Cm=Cm, cb=cb, cs_col=cs_col, dt_col=dt_col, dsk_col=dsk_col,
                lam=lam, x=x, xdt=xdt, prev=prev, cl=cl, f=f)


def _ssd_fwd(xbcdt, conv_w, conv_b, bias, alog, dsk, qkvz, attn, gs):
    nc = T // CHUNK
    R = SSD_PER * CHUNK
    per = R // HALO

    def body(xbc_ref, halo_ref, cw_ref, cb_ref, dt_ref, bias_ref, alog_ref, dsk_ref, z_ref, at_ref, gs_ref,
             y_ref, st_ref, cat_ref, act_ref, state, cst):
        @pl.when(pl.program_id(0) == 0)
        def _():
            state[...] = jnp.zeros_like(state)
        for sub in range(SSD_PER):
            rows = pl.ds(sub * CHUNK, CHUNK)
            st_ref[sub] = state[...]
            halo = (jnp.where(pl.program_id(0) > 0, halo_ref[...], 0.0) if sub == 0
                    else xbc_ref[pl.ds(sub * CHUNK - HALO, HALO), :])
            one_chunk(halo, xbc_ref.at[rows, :], cw_ref, cb_ref, dt_ref.at[rows, :], bias_ref, alog_ref, dsk_ref,
                      z_ref.at[rows, :], at_ref.at[rows, :], gs_ref, y_ref.at[rows, :], cat_ref.at[rows, :],
                      act_ref.at[rows, :], state, cst)

    def one_chunk(halo, xbc_ref, cw_ref, cb_ref, dt_ref, bias_ref, alog_ref, dsk_ref, z_ref, at_ref, gs_ref,
                  y_ref, cat_ref, act_ref, state, cst):
        pre = _conv_pre(jnp.concatenate([halo, xbc_ref[...]], axis=0),
                        [cw_ref[pl.ds(j, 1), :] for j in range(CONV_K)], cb_ref[...])[HALO:]
        act_ref[...] = pre * _sigmoid(pre)
        xs_ref, b_ref, c_ref = (act_ref.at[:, pl.ds(0, AW)], act_ref.at[:, pl.ds(AW, NS)],
                                act_ref.at[:, pl.ds(AW + NS, NS)])
        s = _ssd_prep(dt_ref, bias_ref, alog_ref, dsk_ref, b_ref, c_ref, xs_ref, state, cst)
        Bm, Cm, prev = s["Bm"], s["Cm"], s["prev"]
        g = (s["cb"] * s["lam"]).astype(BF16)
        xdtb = s["xdt"].astype(BF16)
        prevb = prev.astype(BF16)
        y = _heads(lambda h: _nn(g[h], xdtb[h])) + _heads(lambda h: _nt(Cm, prevb[h])) * jnp.exp(s["cs_col"])
        y = y + s["dsk_col"] * s["x"]
        xf = (s["xdt"] * s["f"]).astype(BF16)
        new = prev * jnp.exp(s["cl"]) + _heads(lambda h: _tn(xf[h], Bm))
        for h in range(HEADS):
            y_ref[:, pl.ds(HD * h, HD)] = y[h]
            state[pl.ds(HD * h, HD), :] = new[h]
        z = z_ref[...]
        gi = y_ref[...] * (z * _sigmoid(z))
        cat_ref[:, pl.ds(0, AW)] = at_ref[...].astype(BF16)
        cat_ref[:, pl.ds(AW, AW)] = (gi * _rstd(gi) * gs_ref[...]).astype(BF16)

    vec = pl.BlockSpec((1, DT_PAD), lambda c: (0, 0))
    blk = pl.BlockSpec((R, AW), lambda c: (c, 0))
    return pl.pallas_call(
        body, name="ssd_fwd", grid=(nc // SSD_PER,),
        in_specs=[pl.BlockSpec((R, CONV_CH), lambda c: (c, 0)),
                  pl.BlockSpec((HALO, CONV_CH), lambda c: (jnp.maximum(c * per - 1, 0), 0)),
                  pl.BlockSpec((CONV_K, CONV_CH), lambda c: (0, 0)), pl.BlockSpec((1, CONV_CH), lambda c: (0, 0)),
                  pl.BlockSpec((R, DT_PAD), lambda c: (c, 6)),
                  vec, vec, vec, pl.BlockSpec((R, AW), lambda c: (c, 3)), blk, pl.BlockSpec((1, AW), lambda c: (0, 0))],
        out_specs=[blk, pl.BlockSpec((SSD_PER, AW, NS), lambda c: (c, 0, 0)), pl.BlockSpec((R, D), lambda c: (c, 0)),
                   pl.BlockSpec((R, CONV_CH), lambda c: (c, 0))],
        out_shape=[jax.ShapeDtypeStruct((T, AW), F32), jax.ShapeDtypeStruct((nc, AW, NS), F32),
                   jax.ShapeDtypeStruct((T, D), BF16), jax.ShapeDtypeStruct((T, CONV_CH), F32)],
        scratch_shapes=[pltpu.VMEM((AW, NS), F32), pltpu.VMEM((CHUNK, CHUNK), F32)],
        compiler_params=_cparams(("arbitrary",)),
    )(xbcdt, xbcdt, conv_w, conv_b, xbcdt, bias, alog, dsk, qkvz, attn, gs)


def _ssd_bwd(act, xbcdt, bias, alog, dsk, states, y_ssd, qkvz, dcat, gs):
    nc = T // CHUNK

    def body(xs_ref, b_ref, c_ref, dt_ref, bias_ref, alog_ref, dsk_ref, st_ref, y_ref, z_ref, dyn_ref, gs_ref,
             dact_ref, ddt_ref, par_ref, dz_ref, dgs_ref, dstate, cst, dy_ref):
        @pl.when(pl.program_id(0) == 0)
        def _():
            dstate[...] = jnp.zeros_like(dstate)
            par_ref[...] = jnp.zeros_like(par_ref)
            dgs_ref[...] = jnp.zeros_like(dgs_ref)
        for sub in reversed(range(SSD_PER)):
            rows = pl.ds(sub * CHUNK, CHUNK)
            one_chunk(xs_ref.at[rows, :], b_ref.at[rows, :], c_ref.at[rows, :], dt_ref.at[rows, :], bias_ref, alog_ref,
                      dsk_ref, st_ref.at[sub], y_ref.at[rows, :], z_ref.at[rows, :], dyn_ref.at[rows, :], gs_ref,
                      dact_ref.at[rows, :], ddt_ref.at[rows, :], par_ref, dz_ref.at[rows, :], dgs_ref, dstate, cst, dy_ref)

    def one_chunk(xs_ref, b_ref, c_ref, dt_ref, bias_ref, alog_ref, dsk_ref, st_ref, y_ref, z_ref, dyn_ref, gs_ref,
                  dact_ref, ddt_ref, par_ref, dz_ref, dgs_ref, dstate, cst, dy_ref):
        z, yv, dyn = z_ref[...], y_ref[...], dyn_ref[...]
        sg = _sigmoid(z)
        sz = z * sg
        gi = yv * sz
        rg = _rstd(gi)
        ng = gi * rg
        dgi = _rms_bwd(ng, rg, gs_ref[...], dyn)
        dy_ref[...] = dgi * sz
        dz_ref[...] = dgi * yv * (sg * (1.0 + z * (1.0 - sg)))
        dgs_ref[...] += _colsum(dyn * ng)
        s = _ssd_prep(dt_ref, bias_ref, alog_ref, dsk_ref, b_ref, c_ref, xs_ref, st_ref, cst)
        Bm, Cm, prev, lam, x, xdt, f, cl = s["Bm"], s["Cm"], s["prev"], s["lam"], s["x"], s["xdt"], s["f"], s["cl"]
        lane = lax.broadcasted_iota(jnp.int32, (1, DT_PAD), 1)
        row = lax.broadcasted_iota(jnp.int32, (1, CHUNK, 1), 1)
        g = s["cb"] * lam
        gb, xdtb, prevb = g.astype(BF16), xdt.astype(BF16), prev.astype(BF16)
        dy = _heads(lambda h: dy_ref[:, pl.ds(HD * h, HD)])
        dyb = dy.astype(BF16)
        dnew = _heads(lambda h: dstate[pl.ds(HD * h, HD), :])
        dnewb = dnew.astype(BF16)
        E = jnp.exp(s["cs_col"])
        ecl = jnp.exp(cl)
        dG = _heads(lambda h: _nt(dyb[h], xdtb[h]))
        dxdt = _heads(lambda h: _tn(gb[h], dyb[h]))
        Yo = _heads(lambda h: _nt(Cm, prevb[h]))
        W = _heads(lambda h: _nt(Bm, dnewb[h]))
        dcb = jnp.sum(dG * lam, axis=0)
        Mm = dG * g
        col_sums = jnp.sum(Mm, axis=1, keepdims=True)
        dYo = (dy * E).astype(BF16)
        dxdt = dxdt + W * f
        dF = jnp.sum(W * xdt, axis=2, keepdims=True) * f
        dcl = jnp.sum(dnew * prev, axis=(1, 2), keepdims=True) * ecl + jnp.sum(dF, axis=1, keepdims=True)
        dcs = (jnp.sum(Mm, axis=2, keepdims=True) + jnp.sum(dy * Yo, axis=2, keepdims=True) * E - dF
               + jnp.where(row == CHUNK - 1, dcl, 0.0))
        ddt_x = jnp.sum(dxdt * x, axis=2, keepdims=True)
        dD = jnp.sum(dy * x, axis=(1, 2), keepdims=True)
        dx = s["dsk_col"] * dy + dxdt * s["dt_col"]
        xfb = (xdt * f).astype(BF16)
        dprev = _heads(lambda h: _tn(dYo[h], Cm)) + dnew * ecl
        dcbb = dcb.astype(BF16)
        dC = _nn(dcbb, Bm)
        dB = _tn(dcbb, Cm)
        dcs_mat = -_rows_to_block([col_sums[h] for h in range(HEADS)], CHUNK, CHUNK).T
        ddt_mat = jnp.zeros((CHUNK, DT_PAD), F32)
        dD_row = jnp.zeros((1, DT_PAD), F32)
        for h in range(HEADS):
            sl = pl.ds(HD * h, HD)
            dC = dC + _nn(dYo[h], prevb[h])
            dB = dB + _nn(xfb[h], dnewb[h])
            dcs_mat = dcs_mat + jnp.where(lane == h, dcs[h], 0.0)
            ddt_mat = ddt_mat + jnp.where(lane == h, ddt_x[h], 0.0)
            dD_row = dD_row + jnp.where(lane == h, dD[h], 0.0)
            dact_ref[:, sl] = dx[h]
            dstate[sl, :] = dprev[h]
        dact_ref[:, pl.ds(AW, NS)] = dB
        dact_ref[:, pl.ds(AW + NS, NS)] = dC
        da = jnp.dot((s["li"] <= s["si"]).astype(F32), dcs_mat, precision=HIGHEST, preferred_element_type=F32)
        ddtp = jnp.where(lane < HEADS, (ddt_mat + da * s["A"]) * _sigmoid(s["dtp"]), 0.0)
        ddt_ref[...] = ddtp
        dalog = jnp.where(lane < HEADS, jnp.sum(da * s["dt"], axis=0, keepdims=True) * s["A"], 0.0)
        par_ref[...] += _rows_to_block([jnp.sum(ddtp, axis=0, keepdims=True), dalog, dD_row], 8, DT_PAD)

    vec = pl.BlockSpec((1, DT_PAD), lambda c: (0, 0))
    nstep = nc // SSD_PER
    rev = lambda c: nstep - 1 - c
    R = SSD_PER * CHUNK
    return pl.pallas_call(
        body, name="ssd_bwd", grid=(nstep,),
        in_specs=[pl.BlockSpec((R, AW), lambda c: (rev(c), 0)), pl.BlockSpec((R, NS), lambda c: (rev(c), 4)),
                  pl.BlockSpec((R, NS), lambda c: (rev(c), 5)), pl.BlockSpec((R, DT_PAD), lambda c: (rev(c), 6)),
                  vec, vec, vec,
                  pl.BlockSpec((SSD_PER, AW, NS), lambda c: (rev(c), 0, 0)), pl.BlockSpec((R, AW), lambda c: (rev(c), 0)),
                  pl.BlockSpec((R, AW), lambda c: (rev(c), 3)), pl.BlockSpec((R, AW), lambda c: (rev(c), 1)),
                  pl.BlockSpec((1, AW), lambda c: (0, 0))],
        out_specs=[pl.BlockSpec((R, CONV_CH), lambda c: (rev(c), 0)), pl.BlockSpec((R, DT_PAD), lambda c: (rev(c), 0)),
                   pl.BlockSpec((8, DT_PAD), lambda c: (0, 0)), pl.BlockSpec((R, AW), lambda c: (rev(c), 0)),
                   pl.BlockSpec((1, AW), lambda c: (0, 0))],
        out_shape=[jax.ShapeDtypeStruct((T, CONV_CH), F32), jax.ShapeDtypeStruct((T, DT_PAD), F32),
                   jax.ShapeDtypeStruct((8, DT_PAD), F32), jax.ShapeDtypeStruct((T, AW), F32),
                   jax.ShapeDtypeStruct((1, AW), F32)],
        scratch_shapes=[pltpu.VMEM((AW, NS), F32), pltpu.VMEM((CHUNK, CHUNK), F32), pltpu.VMEM((CHUNK, AW), F32)],
        compiler_params=_cparams(("arbitrary",)),
    )(act, act, act, xbcdt, bias, alog, dsk, states, y_ssd, qkvz, dcat, gs)


def _place():
    return lax.axis_index("x"), lax.axis_index("y"), lax.axis_index("c")


def _slot(px, py, pc):
    return 4 * px + 2 * py + pc


SLAB_ROWS = 24


def _slab_pack(parts, name):
    n = len(parts)

    def body(*refs):
        slab = refs[n]
        slab[...] = jnp.zeros_like(slab)
        for ref, (arr, row) in zip(refs[:n], parts):
            slab[pl.ds(row, arr.shape[0]), pl.ds(0, arr.shape[1])] = ref[...]

    vm = pl.BlockSpec(memory_space=pltpu.VMEM)
    return pl.pallas_call(
        body, name=name, in_specs=[vm] * n, out_specs=vm, out_shape=jax.ShapeDtypeStruct((SLAB_ROWS, D), F32),
    )(*[a for a, _ in parts])


_HBM = pl.BlockSpec(memory_space=pltpu.HBM)
_SEM = pl.BlockSpec(memory_space=pltpu.SEMAPHORE)
_EFFECT = pltpu.SideEffectType.DATAFLOW_SIDE_EFFECTING


def _peers(x, y, c):
    out = []
    for kk in range(1, N_DEV):
        fx, fy, fc = kk >> 2 & 1, kk >> 1 & 1, kk & 1
        out.append((1 - x if fx else x, 1 - y if fy else y, 1 - c if fc else c))
    return out


def _send_start(src, per_peer, name, dep):
    (handles, token) = _send_start_many([src], per_peer, name, dep)
    return handles, token


def _near_peers(x, y, c):
    return [(x, y, 1 - c), (1 - x, y, c), (x, 1 - y, c), (1 - x, 1 - y, c)]


def _send_start_many(srcs, per_peer, name, dep, peers=_peers, npeers=N_DEV - 1):
    n = len(srcs)

    def body(*refs):
        src_refs, land_refs = refs[:n], refs[n:2 * n]
        send_sems, recv_sems = refs[2 * n + 1], refs[2 * n + 2]
        token = refs[-1]
        x, y, c = _place()
        mine = _slot(x, y, c)
        for a in range(n):
            for kk, peer in enumerate(peers(x, y, c)):
                pltpu.make_async_remote_copy(
                    src_ref=src_refs[a].at[_slot(*peer)] if per_peer else src_refs[a], dst_ref=land_refs[a].at[mine],
                    send_sem=send_sems.at[a * npeers + kk], recv_sem=recv_sems.at[a * npeers + kk],
                    device_id=peer, device_id_type=MESH).start()
        token[...] = jnp.zeros_like(token)

    lands = [lax.empty((N_DEV,) + tuple(s.shape[1:] if per_peer else s.shape), s.dtype) for s in srcs]
    hbm = lambda t: pltpu.with_memory_space_constraint(t, pltpu.HBM)
    outs = pl.pallas_call(
        body, name=name,
        out_shape=(pltpu.SemaphoreType.DMA((n * npeers,)), pltpu.SemaphoreType.DMA((n * npeers,)),
                   *[pltpu.HBM(s.shape, s.dtype) for s in srcs], *[pltpu.HBM(l.shape, l.dtype) for l in lands],
                   jax.ShapeDtypeStruct((8, 128), F32)),
        in_specs=(*[_HBM] * (2 * n), _ANY),
        out_specs=(_SEM, _SEM, *[_HBM] * (2 * n), pl.BlockSpec(memory_space=pltpu.VMEM)),
        input_output_aliases={i: 2 + i for i in range(2 * n)},
        compiler_params=pltpu.CompilerParams(has_side_effects=_EFFECT),
    )(*[hbm(s) for s in srcs], *[hbm(l) for l in lands], dep)
    return (outs[0], outs[1], list(outs[2:2 + n]), list(outs[2 + n:2 + 2 * n])), outs[-1]


def _send_wait(handles, after, name):
    srcs, lands = _send_wait_many(handles, after, name)
    return srcs[0], lands[0]


def _send_wait_many(handles, after, name, npeers=N_DEV - 1):
    send_sems, recv_sems, src_thrus, land_thrus = handles
    n = len(src_thrus)

    def body(*refs):
        land_refs = refs[n:2 * n]
        send_sems, recv_sems = refs[2 * n], refs[2 * n + 1]
        me = _place()
        for a in range(n):
            for kk in range(npeers):
                cp = pltpu.make_async_remote_copy(
                    src_ref=land_refs[a].at[0], dst_ref=land_refs[a].at[0],
                    send_sem=send_sems.at[a * npeers + kk], recv_sem=recv_sems.at[a * npeers + kk],
                    device_id=me, device_id_type=MESH)
                cp.wait_send()
                cp.wait_recv()

    both = list(src_thrus) + list(land_thrus)
    outs = pl.pallas_call(
        body, name=name,
        out_shape=tuple(pltpu.HBM(t.shape, t.dtype) for t in both),
        in_specs=(*[_HBM] * (2 * n), _SEM, _SEM, _ANY), out_specs=tuple([_HBM] * (2 * n)),
        input_output_aliases={i: i for i in range(2 * n)},
        compiler_params=pltpu.CompilerParams(has_side_effects=_EFFECT),
    )(*both, send_sems, recv_sems, after)
    return list(outs[:n]), list(outs[n:])


def _forward_start(lands, name, dep):
    n = len(lands)

    def body(*refs):
        land_refs = refs[:n]
        send_sems, recv_sems = refs[n + 1], refs[n + 2]
        token = refs[-1]
        x, y, c = _place()
        for a in range(n):
            for j, chip in enumerate([(1 - x, y), (x, 1 - y), (1 - x, 1 - y)]):
                blk = land_refs[a].at[_slot(*chip, c)]
                pltpu.make_async_remote_copy(
                    src_ref=blk, dst_ref=blk, send_sem=send_sems.at[a * 3 + j], recv_sem=recv_sems.at[a * 3 + j],
                    device_id=(x, y, 1 - c), device_id_type=MESH).start()
        token[...] = jnp.zeros_like(token)

    outs = pl.pallas_call(
        body, name=name,
        out_shape=(pltpu.SemaphoreType.DMA((n * 3,)), pltpu.SemaphoreType.DMA((n * 3,)),
                   *[pltpu.HBM(l.shape, l.dtype) for l in lands], jax.ShapeDtypeStruct((8, 128), F32)),
        in_specs=(*[_HBM] * n, _ANY), out_specs=(_SEM, _SEM, *[_HBM] * n, pl.BlockSpec(memory_space=pltpu.VMEM)),
        input_output_aliases={i: 2 + i for i in range(n)},
        compiler_params=pltpu.CompilerParams(has_side_effects=_EFFECT),
    )(*lands, dep)
    return (outs[0], outs[1], list(outs[2:2 + n])), outs[-1]


def _forward_wait(handles, after, name):
    send_sems, recv_sems, land_thrus = handles
    n = len(land_thrus)

    def body(*refs):
        land_refs = refs[:n]
        send_sems, recv_sems = refs[n], refs[n + 1]
        me = _place()
        for a in range(n):
            for j in range(3):
                cp = pltpu.make_async_remote_copy(
                    src_ref=land_refs[a].at[0], dst_ref=land_refs[a].at[0],
                    send_sem=send_sems.at[a * 3 + j], recv_sem=recv_sems.at[a * 3 + j], device_id=me, device_id_type=MESH)
                cp.wait_send()
                cp.wait_recv()

    outs = pl.pallas_call(
        body, name=name,
        out_shape=tuple(pltpu.HBM(t.shape, t.dtype) for t in land_thrus),
        in_specs=(*[_HBM] * n, _SEM, _SEM, _ANY), out_specs=tuple([_HBM] * n),
        input_output_aliases={i: i for i in range(n)},
        compiler_params=pltpu.CompilerParams(has_side_effects=_EFFECT),
    )(*land_thrus, send_sems, recv_sems, after)
    return list(outs)


def _sum_slots(land, name):
    _, R, C = land.shape
    tm = R if R <= 512 else 512

    def body(x_ref, o_ref):
        acc = x_ref[0].astype(F32)
        for j in range(1, N_DEV):
            acc = acc + x_ref[j].astype(F32)
        o_ref[...] = acc

    return pl.pallas_call(
        body, name=name, grid=(R // tm,),
        in_specs=[pl.BlockSpec((N_DEV, tm, C), lambda i: (0, i, 0))], out_specs=pl.BlockSpec((tm, C), lambda i: (i, 0)),
        out_shape=jax.ShapeDtypeStruct((R, C), F32), compiler_params=_cparams(("parallel",)),
    )(land)


def _adam_math(w, g, m, v):
    m2 = ADAM_B1 * m + (1.0 - ADAM_B1) * g
    v2 = ADAM_B2 * v + (1.0 - ADAM_B2) * (g * g)
    m_hat = m2 / (1.0 - ADAM_B1 ** ADAM_STEP)
    v_hat = v2 / (1.0 - ADAM_B2 ** ADAM_STEP)
    delta = -ADAM_LR * (m_hat / (jnp.sqrt(v_hat) + ADAM_EPS) + ADAM_WD * w)
    return delta, m2, v2


def _adamw(w, g, m, v, name):
    R, C = w.shape
    tm = R if R <= 512 else 256
    return _rowwise(lambda w, g, m, v: (_adam_math(w, g, m, v), ()), [w, g, m, v], [], [(C, F32)] * 3, [], tm=tm, name=name)


def _adamw_slots(land, w, m, v, name):
    _, R, C = land.shape
    tm = R if R <= 256 else 256

    def body(x_ref, w_ref, m_ref, v_ref, g_ref, d_ref, mo_ref, vo_ref):
        g = x_ref[0].astype(F32)
        for j in range(1, N_DEV):
            g = g + x_ref[j].astype(F32)
        d, m2, v2 = _adam_math(w_ref[...], g, m_ref[...], v_ref[...])
        g_ref[...] = g
        d_ref[...] = d
        mo_ref[...] = m2
        vo_ref[...] = v2

    row = pl.BlockSpec((tm, C), lambda i: (i, 0))
    return pl.pallas_call(
        body, name=name, grid=(R // tm,),
        in_specs=[pl.BlockSpec((N_DEV, tm, C), lambda i: (0, i, 0)), row, row, row], out_specs=[row] * 4,
        out_shape=[jax.ShapeDtypeStruct((R, C), F32)] * 4, compiler_params=_cparams(("parallel",)),
    )(land, w, m, v)


def _adamw_small(slab, slab_rows, g_conv_w, ws, ms, vs):
    n = len(ws)

    def body(*refs):
        slab_ref, gc_ref = refs[0], refs[1]
        w_refs, m_refs, v_refs = refs[2:2 + n], refs[2 + n:2 + 2 * n], refs[2 + 2 * n:2 + 3 * n]
        outs = refs[2 + 3 * n:]
        loss_ref = outs[0]
        g_out, d_out, m_out, v_out = (outs[1 + i * n:1 + (i + 1) * n] for i in range(4))
        loss_ref[...] = jnp.sum(slab_ref[pl.ds(6, 1), :], axis=1, keepdims=True)
        for i in range(n):
            g = gc_ref[...] if i == n - 1 else slab_ref[pl.ds(slab_rows[i], 1), pl.ds(0, ws[i].shape[1])]
            d, m2, v2 = _adam_math(w_refs[i][...], g, m_refs[i][...], v_refs[i][...])
            g_out[i][...] = g
            d_out[i][...] = d
            m_out[i][...] = m2
            v_out[i][...] = v2

    vm = pl.BlockSpec(memory_space=pltpu.VMEM)
    shapes = [jax.ShapeDtypeStruct(w.shape, F32) for w in ws]
    outs = pl.pallas_call(
        body, name="adamw_small", in_specs=[vm] * (2 + 3 * n), out_specs=[vm] * (1 + 4 * n),
        out_shape=[jax.ShapeDtypeStruct((1, 1), F32)] + shapes * 4,
    )(slab, g_conv_w, *ws, *ms, *vs)
    return outs[0], outs[1:1 + n], outs[1 + n:1 + 2 * n], outs[1 + 2 * n:1 + 3 * n], outs[1 + 3 * n:]


SMALL = ["norm_mix_pre", "norm_mix_post", "norm_mlp_pre", "norm_mlp_post", "norm_ple_post",
         "conv_b", "ssd_norm_g", "dt_bias", "a_log", "d_skip"]


def _pad_row(v, width=D):
    return jnp.pad(v, ((0, 0), (0, width - v.shape[1])))


def kernel(x, p, positions, norm_mix_pre, norm_mix_post, w_in, conv_w, conv_b, dt_bias, a_log, d_skip, ssd_norm_g, w_out, norm_mlp_pre, norm_mlp_post, w_up, w_down, w_ple_gate, w_ple_proj, norm_ple_post, loss_target, m_norm_mix_pre, m_norm_mix_post, m_w_in, m_conv_w, m_conv_b, m_dt_bias, m_a_log, m_d_skip, m_ssd_norm_g, m_w_out, m_norm_mlp_pre, m_norm_mlp_post, m_w_up, m_w_down, m_w_ple_gate, m_w_ple_proj, m_norm_ple_post, v_norm_mix_pre, v_norm_mix_post, v_w_in, v_conv_w, v_conv_b, v_dt_bias, v_a_log, v_d_skip, v_ssd_norm_g, v_w_out, v_norm_mlp_pre, v_norm_mlp_post, v_w_up, v_w_down, v_w_ple_gate, v_w_ple_proj, v_norm_ple_post):
    args = dict(locals())
    x2, p2, tgt = x[0], p[0, 0], loss_target[0]
    g1, g2, g3, g4, g5 = norm_mix_pre, norm_mix_post, norm_mlp_pre, norm_mlp_post, norm_ple_post

    me = _slot(*_place())
    pack_in = jnp.pad(w_in[0].T, ((0, W_IN_SHARD_PAD - W_IN_SHARD), (0, 0))).astype(BF16)
    rest = [w_out[0].astype(BF16), w_up[0].T.astype(BF16), w_down[0].astype(BF16), w_ple_gate[0].astype(BF16),
            w_ple_proj[0].T.reshape(32, D).astype(BF16)]
    conv_pack = jnp.pad(conv_w[0], ((0, 4), (0, 32)))
    in_handles, tok_in0 = _send_start_many([pack_in, conv_pack], False, "gather_in_start", g1, peers=_near_peers, npeers=4)

    inv_freq = ROPE_THETA ** (-jnp.arange(HD // 2, dtype=F32) * 2.0 / HD)
    pos = positions[0] + tok_in0[0, 0].astype(jnp.int32)
    ang = pos.astype(F32)[:, None] * jnp.tile(inv_freq, 4)
    cos128 = jnp.cos(ang)
    sin128 = jnp.sin(ang) * jnp.tile(jnp.concatenate([-jnp.ones(HD // 2, F32), jnp.ones(HD // 2, F32)]), 2)

    bias_w, alog_w, dsk_w = _pad_row(dt_bias, DT_PAD), _pad_row(a_log, DT_PAD), _pad_row(d_skip, DT_PAD)

    (u1,) = _rowwise(lambda a, g: ((a * _rstd(a) * g,), ()), [x2], [g1], [(D, BF16)], [], tm=512, name="norm_x",
                     deps=[cos128, sin128])
    p2b = p2.astype(BF16)

    in_back, in_land = _send_wait_many(in_handles, u1, "gather_in_wait", npeers=4)
    fw_handles, tok_fw = _forward_start(in_land, "gather_in_forward", u1)
    in_land = _forward_wait(fw_handles, tok_fw, "gather_in_forward_wait")
    gin = lax.dynamic_update_slice(in_land[0], in_back[0][None], (me, 0, 0))
    gconv = lax.dynamic_update_slice(in_land[1], in_back[1][None], (me, 0, 0))
    rest_handles, tok_rest = _send_start_many(rest, False, "gather_rest_start", gconv)
    w_inT = gin[:, :W_IN_SHARD].reshape(IN_W, D)
    w_qkvzT = w_inT[:4 * AW]
    w_xbcdtT = jnp.pad(w_inT[4 * AW:], ((0, DT_PAD - HEADS), (0, 0)))
    conv_full = gconv[:, :CONV_K, :96].transpose(1, 0, 2).reshape(CONV_K, CONV_CH)
    qkvz, xbcdt = _mm_rows(lambda a, b: ((a, b), ()), [(u1, w_qkvzT, True), (u1, w_xbcdtT, True)], [], [],
                           [(4 * AW, F32), (CONV_CH + DT_PAD, F32)], [], tm=512, name="proj_in", deps=[tok_rest])

    qkv = _rope_fwd(qkvz, cos128, sin128)
    qkv = [qkv[3 * i:3 * i + 3] for i in range(len(DILATIONS))]
    outs, lses = [], []
    for d, (qd, kd, vd) in zip(DILATIONS, qkv):
        o, l = _attn_fwd(qd, kd, vd, d)
        outs.append(o)
        lses.append(l)
    attn, lse, attn4, lse4, attn16, lse16 = _attn_merge(outs, lses)

    y_ssd, states, cat, act = _ssd_fwd(xbcdt, conv_full, conv_b, bias_w, alog_w, dsk_w, qkvz, attn, ssd_norm_g)


    rest_back, landed = _send_wait_many(rest_handles, cat, "gather_rest_wait")
    landed = [lax.dynamic_update_slice(l, b[None], (me, 0, 0)) for l, b in zip(landed, rest_back)]
    w_o, w_upT, w_dn, w_gate = landed[0].reshape(D, D), landed[1].reshape(DFF, D), landed[2].reshape(DFF, D), landed[3].reshape(D, D)
    w_projT = landed[4].reshape(D, PLE)

    def post1(mm, xx, ga):
        h = xx + mm * _rstd(mm) * ga
        return (mm, h, _rstd(h)), ()
    mix, h1, r3 = _mm_rows(post1, [(cat, w_o, False)], [x2], [g2], [(D, F32), (D, F32), (1, F32)], [], tm=512,
                           name="mix_out")

    a_up, ff, u2, h2, h2b = _mlp_fwd(h1, r3, g3, w_upT, w_dn, g4)
    relu2 = lambda a: jnp.square(jnp.maximum(a.astype(F32), 0.0))

    def final(gpre, ppv, hh, tg, g):
        sg = _sigmoid(gpre)
        ple = ppv * sg
        r = _rstd(ple)
        n = ple * r
        h3 = hh + n * g
        e = h3 - tg
        dh3 = e * (1.0 / D)
        dple = _rms_bwd(n, r, g, dh3)
        return (dh3, dple * sg, dple * ppv * sg * (1.0 - sg)), (_colsum(dh3 * n), _colsum(0.5 * e * e * (1.0 / D)))
    dh3, dpp, dgp, dg5, loss_vec = _mm_rows(final, [(h2b, w_gate, False), (p2b, w_projT, True)], [h2, tgt], [g5],
                                            [(D, F32), (D, BF16), (D, BF16)], [(1, D), (1, D)], tm=512, name="ple_loss")

    gw_projT = _mm(dpp, p2b, ta=True, tm=512, tn=256, tk=T, out_dtypes=(BF16,), name="gw_ple_proj")
    gw_gate = _mm(h2b, dgp, ta=True, tm=512, tn=1024, tk=T, out_dtypes=(BF16,), name="gw_ple_gate")
    def bwd_mlp_post(dg_, d3, f, g):
        dh2 = d3 + dg_
        r = _rstd(f)
        n = f * r
        return (dh2, _rms_bwd(n, r, g, dh2)), (_colsum(dh2 * n),)
    dh2, dff, dg4 = _mm_rows(bwd_mlp_post, [(dgp, w_gate, True)], [dh3, ff], [g4], [(D, F32), (D, BF16)], [(1, D)],
                             tm=512, name="bwd_ple_gate")

    gw_dn = _mm(a_up, dff, ta=True, tm=512, tn=1024, tk=T, a_pre=relu2, out_dtypes=(BF16,), name="gw_mlp_down")
    rs_a, tok_a = _send_start_many([gw_projT.reshape(N_DEV, 32, D), gw_gate.reshape(N_DEV, 128, D),
                                    gw_dn.reshape(N_DEV, 512, D)], True, "rs_start_a", g1)
    da_up, du2 = _mlp_dx(dff, a_up, w_upT, w_dn, tok_a)
    gw_upT = _mm(da_up, u2, ta=True, tm=512, tn=1024, tk=T, out_dtypes=(BF16,), name="gw_mlp_up")

    def bwd_mix_post(d2, du, hh, rr, mm, ga, gb):
        n3 = hh * rr
        dh1 = d2 + _rms_bwd(n3, rr, gb, du)
        r = _rstd(mm)
        n2 = mm * r
        return (dh1, _rms_bwd(n2, r, ga, dh1)), (_colsum(du * n3), _colsum(dh1 * n2))
    dh1, dmix, dg3, dg2 = _rowwise(bwd_mix_post, [dh2, du2, h1, r3, mix], [g2, g3], [(D, F32), (D, BF16)],
                                   [(1, D), (1, D)], tm=512, name="bwd_post_mix")

    gw_o = _mm(cat, dmix, ta=True, tm=512, tn=1024, tk=T, out_dtypes=(BF16,), name="gw_out")
    rs_b, tok_b = _send_start_many([gw_upT.reshape(N_DEV, 512, D), gw_o.reshape(N_DEV, 128, D)], True, "rs_start_b", g1)
    dcat, dattn4, dattn16 = _dx_out(dmix, w_o, tok_b)

    dact, ddtw, ssd_par, dz, dgs = _ssd_bwd(act, xbcdt, bias_w, alog_w, dsk_w, states, y_ssd, qkvz, dcat, ssd_norm_g)
    dxbcdt, conv_par = _conv_bwd(xbcdt, dact, ddtw, conv_full, conv_b)

    qkv_grads = [_attn_bwd(*qkv[0], dcat, attn, lse, 1),
                 _attn_bwd(*qkv[1], dattn4, attn4, lse4, 4),
                 _attn_bwd(*qkv[2], dattn16, attn16, lse16, 16)]
    dqkvz = _rope_bwd(qkv_grads, dz, cos128, sin128)

    gw_qkvzT = _mm(dqkvz, u1, ta=True, tm=512, tn=1024, tk=T, out_dtypes=(BF16,), name="gw_qkvz")
    gw_xbcdtT = _mm(dxbcdt, u1, ta=True, tm=896, tn=1024, tk=T, out_dtypes=(BF16,), name="gw_xbcdt")
    gw_inT = jnp.concatenate([gw_qkvzT, gw_xbcdtT], axis=0)[:IN_W]
    gw_inT = jnp.pad(gw_inT.reshape(N_DEV, W_IN_SHARD, D), ((0, 0), (0, W_IN_SHARD_PAD - W_IN_SHARD), (0, 0)))
    rs_in, tok_in = _send_start(gw_inT, True, "rs_start_w_in", g1)

    def bwd_in(ua, ub, d1, xx, g):
        rr = _rstd(xx)
        n = xx * rr
        du = ua + ub
        return (d1 + _rms_bwd(n, rr, g, du),), (_colsum(du * n),)
    grad_x, dg1 = _mm_rows(bwd_in, [(dqkvz, w_qkvzT, False), (dxbcdt, w_xbcdtT, False)], [dh1, x2], [g1],
                           [(D, F32)], [(1, D)], tm=512, name="bwd_in_proj", deps=[tok_in])

    my_slab = _slab_pack([(dg1, 0), (dg2, 1), (dg3, 2), (dg4, 3), (dg5, 4), (dgs, 5), (loss_vec, 6),
                          (conv_par, 8), (ssd_par, 16)], "slab_pack")
    slab_handles, tok_slab = _send_start_many([my_slab], False, "slab_start", g1)

    def scatter_finish(handles, nm, after):
        part, land = _send_wait(handles, after, "rs_wait_" + nm)
        own = lax.dynamic_slice(part, (me, 0, 0), (1,) + part.shape[1:])
        return _sum_slots(lax.dynamic_update_slice(land, own, (me, 0, 0)), "rs_sum_" + nm)
    def landed(handles, after, wait_name):
        parts, lands = _send_wait_many(handles, after, wait_name)
        return [lax.dynamic_update_slice(land, lax.dynamic_slice(part, (me, 0, 0), (1,) + part.shape[1:]), (me, 0, 0))
                for part, land in zip(parts, lands)]
    land_proj, land_gate, land_dn = landed(rs_a, tok_slab, "rs_wait_a")
    land_up, land_out = landed(rs_b, tok_slab, "rs_wait_b")

    grads, delta, new_m, new_v = {}, {}, {}, {}
    for nme, land in (("w_down", land_dn), ("w_out", land_out), ("w_ple_gate", land_gate)):
        outs4 = _adamw_slots(land, args[nme][0], args["m_" + nme][0], args["v_" + nme][0], "adamw_" + nme)
        grads[nme], delta[nme], new_m[nme], new_v[nme] = [t[None] for t in outs4]
    grads["w_up"] = _sum_slots(land_up, "rs_sum_w_up").T[None]
    grads["w_ple_proj"] = _sum_slots(land_proj, "rs_sum_w_proj").reshape(128, PLE).T[None]
    for nme in ["w_up", "w_ple_proj", "w_in"]:
        if nme == "w_in":
            g_inT = scatter_finish(rs_in, "w_in", delta["w_down"])
            grads["w_in"] = g_inT[:W_IN_SHARD].T[None]
        dl, mm_, vv_ = _adamw(args[nme][0], grads[nme][0], args["m_" + nme][0], args["v_" + nme][0], "adamw_" + nme)
        delta[nme], new_m[nme], new_v[nme] = dl[None], mm_[None], vv_[None]

    slab_back, slab_land = _send_wait_many(slab_handles, delta["w_in"], "slab_wait")
    slab = _sum_slots(lax.dynamic_update_slice(slab_land[0], slab_back[0][None], (me, 0, 0)), "slab_sum")
    g_conv_w = lax.dynamic_slice(slab[8:12, :CONV_CH], (0, me * 96), (CONV_K, 96))
    small_names = SMALL + ["conv_w"]
    small_rows = [0, 1, 2, 3, 4, 12, 5, 16, 17, 18, None]
    pick = lambda prefix: [args[prefix + nme] for nme in SMALL] + [args[prefix + "conv_w"][0]]
    loss11, g_s, d_s, m_s, v_s = _adamw_small(slab, small_rows, g_conv_w, pick(""), pick("m_"), pick("v_"))
    loss = loss11[0, 0]
    for i, nme in enumerate(small_names):
        lead = (lambda t: t[None]) if nme == "conv_w" else (lambda t: t)
        grads[nme], delta[nme], new_m[nme], new_v[nme] = lead(g_s[i]), lead(d_s[i]), lead(m_s[i]), lead(v_s[i])

    order = ["norm_mix_pre", "norm_mix_post", "w_in", "conv_w", "conv_b", "dt_bias", "a_log", "d_skip", "ssd_norm_g",
             "w_out", "norm_mlp_pre", "norm_mlp_post", "w_up", "w_down", "w_ple_gate", "w_ple_proj", "norm_ple_post"]
    return (loss, grad_x[None], *[grads[n] for n in order], *[delta[n] for n in order],
            *[new_m[n] for n in order], *[new_v[n] for n in order])
```

```python
import jax
import jax.numpy as jnp
from jax import lax
from jax.experimental import pallas as pl
from jax.experimental.pallas import tpu as pltpu

F32 = jnp.float32
BF16 = jnp.bfloat16
MESH = pl.DeviceIdType.MESH
HIGHEST = lax.Precision.HIGHEST

N_DEV = 8
T = 4096
D = 1024
HEADS = 8
HD = 64
AW = 512
NS = 128
CONV_K = 4
CONV_CH = 768
CHUNK = 128
SSD_PER = 4
DFF = 4096
PLE = 256
EPS = 1e-6
ROPE_THETA = 10000.0
DILATIONS = (1, 4, 16)
QBLK = 128
NEG = -1e30
IN_W = 2824
W_IN_SHARD = 353
W_IN_SHARD_PAD = 384
DT_PAD = 128

ADAM_LR, ADAM_B1, ADAM_B2, ADAM_EPS, ADAM_WD, ADAM_STEP = 0.001, 0.9, 0.999, 1e-08, 0.01, 10

VMEM_LIMIT = 56 * 1024 * 1024


_ANY = pl.BlockSpec(memory_space=pl.ANY)


def _cparams(sem=None):
    return pltpu.CompilerParams(dimension_semantics=sem, vmem_limit_bytes=VMEM_LIMIT)


def _dot(a, b, ca, cb, precision=None):
    return lax.dot_general(a, b, (((ca,), (cb,)), ((), ())), preferred_element_type=F32, precision=precision)


def _nn(a, b):
    return _dot(a, b, 1, 0)


def _nt(a, b):
    return _dot(a, b, 1, 1)


def _tn(a, b):
    return _dot(a, b, 0, 0)


def _sigmoid(x):
    return 1.0 / (1.0 + jnp.exp(-x))


def _softplus(x):
    return jnp.maximum(x, 0.0) + jnp.log(1.0 + jnp.exp(-jnp.abs(x)))


def _mm(a, b, *, ta=False, tb=False, tm, tn, tk, name,
        a_pre=None, a_rows=(), a_cols=(), b_pre=None, b_rows=(), b_cols=(),
        epi=None, epi_tiles=(), out_dtypes=(F32,), deps=()):
    if ta:
        K, M = a.shape
    else:
        M, K = a.shape
    if tb:
        N, K2 = b.shape
    else:
        K2, N = b.shape
    assert K == K2 and M % tm == 0 and N % tn == 0 and K % tk == 0, (name, a.shape, b.shape)
    nk = K // tk
    if ta:
        a_spec = pl.BlockSpec((tk, tm), lambda i, j, k: (k, i))
        a_row_specs = [pl.BlockSpec((tk, 1), lambda i, j, k: (k, 0)) for _ in a_rows]
        a_col_specs = [pl.BlockSpec((1, tm), lambda i, j, k: (0, i)) for _ in a_cols]
    else:
        a_spec = pl.BlockSpec((tm, tk), lambda i, j, k: (i, k))
        a_row_specs = [pl.BlockSpec((tm, 1), lambda i, j, k: (i, 0)) for _ in a_rows]
        a_col_specs = [pl.BlockSpec((1, tk), lambda i, j, k: (0, k)) for _ in a_cols]
    if tb:
        b_spec = pl.BlockSpec((tn, tk), lambda i, j, k: (j, k))
        b_row_specs = [pl.BlockSpec((tn, 1), lambda i, j, k: (j, 0)) for _ in b_rows]
        b_col_specs = [pl.BlockSpec((1, tk), lambda i, j, k: (0, k)) for _ in b_cols]
    else:
        b_spec = pl.BlockSpec((tk, tn), lambda i, j, k: (k, j))
        b_row_specs = [pl.BlockSpec((tk, 1), lambda i, j, k: (k, 0)) for _ in b_rows]
        b_col_specs = [pl.BlockSpec((1, tn), lambda i, j, k: (0, j)) for _ in b_cols]
    o_spec = pl.BlockSpec((tm, tn), lambda i, j, k: (i, j))
    na, nb, ne, no = len(a_rows) + len(a_cols), len(b_rows) + len(b_cols), len(epi_tiles), len(out_dtypes)

    def body(*refs):
        a_ref, b_ref = refs[0], refs[1]
        a_ex = refs[2:2 + na]
        b_ex = refs[2 + na:2 + na + nb]
        e_ex = refs[2 + na + nb:2 + na + nb + ne]
        first_out = 2 + na + nb + ne + len(deps)
        outs = refs[first_out:first_out + no]

        def finish(res):
            vals = epi(res, *[r[...] for r in e_ex]) if epi is not None else (res,)
            for o_ref, val in zip(outs, vals):
                o_ref[...] = val.astype(o_ref.dtype)

        at = a_ref[...]
        if a_pre is not None:
            at = a_pre(at, *[r[...] for r in a_ex])
        bt = b_ref[...]
        if b_pre is not None:
            bt = b_pre(bt, *[r[...] for r in b_ex])
        prod = _dot(at.astype(BF16), bt.astype(BF16), 0 if ta else 1, 1 if tb else 0)
        if nk == 1:
            finish(prod)
            return
        acc = refs[-1]
        k = pl.program_id(2)

        @pl.when(k == 0)
        def _():
            acc[...] = jnp.zeros_like(acc)
        acc[...] += prod

        @pl.when(k == nk - 1)
        def _():
            finish(acc[...])

    outs = pl.pallas_call(
        body, name=name,
        grid=(M // tm, N // tn, nk),
        in_specs=([a_spec, b_spec] + a_row_specs + a_col_specs + b_row_specs + b_col_specs + [o_spec] * ne
                  + [_ANY] * len(deps)),
        out_specs=[o_spec] * no,
        out_shape=[jax.ShapeDtypeStruct((M, N), dt) for dt in out_dtypes],
        scratch_shapes=[pltpu.VMEM((tm, tn), F32)] if nk > 1 else [],
        compiler_params=_cparams(("parallel", "parallel", "arbitrary")),
    )(a, b, *a_rows, *a_cols, *b_rows, *b_cols, *epi_tiles, *deps)
    return outs[0] if no == 1 else outs


MLP_TM = 1024
MLP_TC = 512


def _mlp_fwd(h, r, g, w_upT, w_dn, g_post):
    nc = DFF // MLP_TC

    def body(h_ref, r_ref, g_ref, wu_ref, wd_ref, gp_ref, a_ref, ff_ref, u_ref, ho_ref, hob_ref, acc, u_scr):
        c = pl.program_id(1)

        @pl.when(c == 0)
        def _():
            u = (h_ref[...] * r_ref[...] * g_ref[...]).astype(BF16)
            u_scr[...] = u
            u_ref[...] = u
            acc[...] = jnp.zeros_like(acc)
        a = _nt(u_scr[...], wu_ref[...])
        a_ref[...] = a.astype(BF16)
        acc[...] += _nn(jnp.square(jnp.maximum(a, 0.0)).astype(BF16), wd_ref[...])

        @pl.when(c == nc - 1)
        def _():
            f = acc[...]
            ff_ref[...] = f
            ho = h_ref[...] + f * _rstd(f) * gp_ref[...]
            ho_ref[...] = ho
            hob_ref[...] = ho.astype(BF16)

    row = pl.BlockSpec((MLP_TM, D), lambda i, c: (i, 0))
    wsp = pl.BlockSpec((MLP_TC, D), lambda i, c: (c, 0))
    vec = pl.BlockSpec((1, D), lambda i, c: (0, 0))
    return pl.pallas_call(
        body, name="mlp_fwd", grid=(T // MLP_TM, nc),
        in_specs=[row, pl.BlockSpec((MLP_TM, 1), lambda i, c: (i, 0)), vec, wsp, wsp, vec],
        out_specs=[pl.BlockSpec((MLP_TM, MLP_TC), lambda i, c: (i, c)), row, row, row, row],
        out_shape=[jax.ShapeDtypeStruct((T, DFF), BF16), jax.ShapeDtypeStruct((T, D), F32), jax.ShapeDtypeStruct((T, D), BF16),
                   jax.ShapeDtypeStruct((T, D), F32), jax.ShapeDtypeStruct((T, D), BF16)],
        scratch_shapes=[pltpu.VMEM((MLP_TM, D), F32), pltpu.VMEM((MLP_TM, D), BF16)],
        compiler_params=_cparams(("parallel", "arbitrary")),
    )(h, r, g, w_upT, w_dn, g_post)


def _mlp_dx(dff, a, w_upT, w_dn, dep):
    nc = DFF // MLP_TC

    def body(d_ref, a_ref, wu_ref, wd_ref, dep_ref, da_ref, du_ref, acc, d_scr):
        c = pl.program_id(1)

        @pl.when(c == 0)
        def _():
            d_scr[...] = d_ref[...].astype(BF16)
            acc[...] = jnp.zeros_like(acc)
        da = (_nt(d_scr[...], wd_ref[...]) * (2.0 * jnp.maximum(a_ref[...].astype(F32), 0.0))).astype(BF16)
        da_ref[...] = da
        acc[...] += _nn(da, wu_ref[...])

        @pl.when(c == nc - 1)
        def _():
            du_ref[...] = acc[...]

    row = pl.BlockSpec((MLP_TM, D), lambda i, c: (i, 0))
    wsp = pl.BlockSpec((MLP_TC, D), lambda i, c: (c, 0))
    chunk = pl.BlockSpec((MLP_TM, MLP_TC), lambda i, c: (i, c))
    return pl.pallas_call(
        body, name="mlp_dx", grid=(T // MLP_TM, nc),
        in_specs=[row, chunk, wsp, wsp, _ANY], out_specs=[chunk, row],
        out_shape=[jax.ShapeDtypeStruct((T, DFF), BF16), jax.ShapeDtypeStruct((T, D), F32)],
        scratch_shapes=[pltpu.VMEM((MLP_TM, D), F32), pltpu.VMEM((MLP_TM, D), BF16)],
        compiler_params=_cparams(("parallel", "arbitrary")),
    )(dff, a, w_upT, w_dn, dep)


def _rowwise(fn, rows, vecs, out_rows, out_sums, *, tm, name, deps=()):
    specs, arrs = [], []
    R = None
    for r in rows:
        if isinstance(r, tuple):
            arr, width, cb = r
            specs.append(pl.BlockSpec((tm, width), lambda i, cb=cb: (i, cb)))
        else:
            arr = r
            specs.append(pl.BlockSpec((tm, arr.shape[1]), lambda i: (i, 0)))
        R = arr.shape[0] if R is None else R
        assert arr.shape[0] == R, name
        arrs.append(arr)
    assert R % tm == 0, name
    for v in vecs:
        specs.append(pl.BlockSpec(v.shape, lambda i: (0, 0)))
        arrs.append(v)
    nr, nv, no, ns = len(rows), len(vecs), len(out_rows), len(out_sums)
    out_specs = [pl.BlockSpec((tm, w), lambda i: (i, 0)) for w, _ in out_rows]
    out_specs += [pl.BlockSpec(s, lambda i: (0, 0)) for s in out_sums]
    out_shape = [jax.ShapeDtypeStruct((R, w), dt) for w, dt in out_rows]
    out_shape += [jax.ShapeDtypeStruct(s, F32) for s in out_sums]

    nd = len(deps)

    def body(*refs):
        ins = [r[...] for r in refs[:nr + nv]]
        o_refs = refs[nr + nv + nd:nr + nv + nd + no]
        s_refs = refs[nr + nv + nd + no:]
        o_vals, s_vals = fn(*ins)
        for ref, val in zip(o_refs, o_vals):
            ref[...] = val.astype(ref.dtype)
        if ns:
            @pl.when(pl.program_id(0) == 0)
            def _():
                for ref in s_refs:
                    ref[...] = jnp.zeros_like(ref)
            for ref, val in zip(s_refs, s_vals):
                ref[...] += val

    outs = pl.pallas_call(
        body, name=name, grid=(R // tm,), in_specs=specs + [_ANY] * nd, out_specs=out_specs, out_shape=out_shape,
        compiler_params=_cparams(("arbitrary",) if ns else ("parallel",)),
    )(*arrs, *deps)
    return outs


def _mm_rows(fn, mats, rows, vecs, out_rows, out_sums, *, tm, name, deps=()):
    R = mats[0][0].shape[0]
    assert R % tm == 0, name
    specs, arrs = [], []
    for a, b, tb in mats:
        specs += [pl.BlockSpec((tm, a.shape[1]), lambda i: (i, 0)), pl.BlockSpec(b.shape, lambda i: (0, 0))]
        arrs += [a, b]
    for r in rows:
        specs.append(pl.BlockSpec((tm, r.shape[1]), lambda i: (i, 0)))
        arrs.append(r)
    for v in vecs:
        specs.append(pl.BlockSpec(v.shape, lambda i: (0, 0)))
        arrs.append(v)
    nm, nr, nv, nd, no, ns = len(mats), len(rows), len(vecs), len(deps), len(out_rows), len(out_sums)
    out_specs = [pl.BlockSpec((tm, w), lambda i: (i, 0)) for w, _ in out_rows]
    out_specs += [pl.BlockSpec(s, lambda i: (0, 0)) for s in out_sums]
    out_shape = [jax.ShapeDtypeStruct((R, w), dt) for w, dt in out_rows] + [jax.ShapeDtypeStruct(s, F32) for s in out_sums]

    def body(*refs):
        prods = [_dot(refs[2 * p][...].astype(BF16), refs[2 * p + 1][...].astype(BF16), 1, 1 if mats[p][2] else 0)
                 for p in range(nm)]
        ins = [r[...] for r in refs[2 * nm:2 * nm + nr + nv]]
        first_out = 2 * nm + nr + nv + nd
        o_refs, s_refs = refs[first_out:first_out + no], refs[first_out + no:]
        o_vals, s_vals = fn(*prods, *ins)
        for ref, val in zip(o_refs, o_vals):
            ref[...] = val.astype(ref.dtype)
        if ns:
            @pl.when(pl.program_id(0) == 0)
            def _():
                for ref in s_refs:
                    ref[...] = jnp.zeros_like(ref)
            for ref, val in zip(s_refs, s_vals):
                ref[...] += val

    return pl.pallas_call(
        body, name=name, grid=(R // tm,), in_specs=specs + [_ANY] * nd, out_specs=out_specs, out_shape=out_shape,
        compiler_params=_cparams(("arbitrary",) if ns else ("parallel",)),
    )(*arrs, *deps)


def _colsum(x):
    return jnp.sum(x, axis=0, keepdims=True)


def _rstd(x):
    return lax.rsqrt(jnp.mean(x * x, axis=-1, keepdims=True) + EPS)


def _rms_bwd(xn, r, g, dy):
    dn = dy * g
    return r * (dn - xn * jnp.mean(dn * xn, axis=-1, keepdims=True))


def _partner(t):
    lane = lax.broadcasted_iota(jnp.int32, t.shape, 1)
    up = pltpu.roll(t, 96, 1)
    down = pltpu.roll(t, 32, 1)
    return jnp.where((lane % 64) < 32, up, down)


SLABS = AW // 128


def _rows(r, n, d):
    return pl.ds(r, n, stride=d) if d > 1 else pl.ds(0, n)


def _undilate(src_ref, dst, d, tm):
    for r in range(d):
        for j in range(SLABS):
            dst[j][_rows(r, tm // d, d), :] = src_ref[:, pl.ds(r * AW + j * 128, 128)].astype(dst[j].dtype)


def _dilate(dst_ref, src, d, tm):
    for r in range(d):
        for j in range(SLABS):
            dst_ref[:, pl.ds(r * AW + j * 128, 128)] = src[j][_rows(r, tm // d, d), :].astype(dst_ref.dtype)


def _slab_scratch(n, tm):
    return [pltpu.VMEM((tm, 128), F32)] * (SLABS * n)


def _slab_groups(flat):
    return [flat[SLABS * i:SLABS * (i + 1)] for i in range(len(flat) // SLABS)]


def _slab_specs(tm, first):
    return [pl.BlockSpec((tm, 128), lambda i, j=j: (i, first + j)) for j in range(SLABS)]


def _dil_spec(tm, d):
    return pl.BlockSpec((tm // d, d * AW), lambda i: (i, 0))


ROPE_TM = 512


def _rope_fwd(qkvz, cos128, sin128):
    tm = ROPE_TM

    def body(*refs):
        q_refs, k_refs, v_refs = refs[0:4], refs[4:8], refs[8:12]
        c_ref, s_ref = refs[12], refs[13]
        outs = refs[14:23]
        qs, ks = _slab_groups(refs[23:])
        c, s = c_ref[...], s_ref[...]
        for j in range(SLABS):
            q, k = q_refs[j][...], k_refs[j][...]
            qs[j][...] = (q * c + _partner(q) * s) * (HD ** -0.5)
            ks[j][...] = k * c + _partner(k) * s
        for di, d in enumerate(DILATIONS):
            oq, ok, ov = outs[3 * di:3 * di + 3]
            for r in range(d):
                rows = _rows(r, tm // d, d)
                for j in range(SLABS):
                    cols = pl.ds(r * AW + j * 128, 128)
                    oq[:, cols] = qs[j][rows, :].astype(BF16)
                    ok[:, cols] = ks[j][rows, :].astype(BF16)
                    ov[:, cols] = v_refs[j][rows, :].astype(BF16)

    tab = pl.BlockSpec((tm, 128), lambda i: (i, 0))
    out_specs, out_shape = [], []
    for d in DILATIONS:
        out_specs += [_dil_spec(tm, d)] * 3
        out_shape += [jax.ShapeDtypeStruct((T // d, d * AW), BF16)] * 3
    return pl.pallas_call(
        body, name="rope_fwd", grid=(T // tm,),
        in_specs=_slab_specs(tm, 0) + _slab_specs(tm, 4) + _slab_specs(tm, 8) + [tab, tab],
        out_specs=out_specs, out_shape=out_shape, scratch_shapes=_slab_scratch(2, tm),
        compiler_params=_cparams(("parallel",)),
    )(*([qkvz] * 12), cos128, sin128)


def _rope_bwd(grads, dz, cos128, sin128):
    tm = ROPE_TM

    def body(*refs):
        g_refs = refs[0:9]
        dz_ref, c_ref, s_ref, o_ref = refs[9], refs[10], refs[11], refs[12]
        scr = _slab_groups(refs[13:])
        for di, d in enumerate(DILATIONS[1:]):
            for t in range(3):
                _undilate(g_refs[3 * (di + 1) + t], scr[3 * di + t], d, tm)
        c, s = c_ref[...], s_ref[...]
        for j in range(SLABS):
            cols = pl.ds(j * 128, 128)
            tot = [g_refs[t][:, cols] + scr[t][j][...] + scr[3 + t][j][...] for t in range(3)]
            dqr = tot[0] * (HD ** -0.5)
            o_ref[:, pl.ds(j * 128, 128)] = (dqr * c + _partner(dqr * s)).astype(BF16)
            o_ref[:, pl.ds(AW + j * 128, 128)] = (tot[1] * c + _partner(tot[1] * s)).astype(BF16)
            o_ref[:, pl.ds(2 * AW + j * 128, 128)] = tot[2].astype(BF16)
        o_ref[:, pl.ds(3 * AW, AW)] = dz_ref[...].astype(BF16)

    tab = pl.BlockSpec((tm, 128), lambda i: (i, 0))
    in_specs, args = [], []
    for d, g in zip(DILATIONS, grads):
        in_specs += [_dil_spec(tm, d)] * 3
        args += list(g)
    return pl.pallas_call(
        body, name="rope_bwd", grid=(T // tm,),
        in_specs=in_specs + [pl.BlockSpec((tm, AW), lambda i: (i, 0)), tab, tab],
        out_specs=pl.BlockSpec((tm, 4 * AW), lambda i: (i, 0)),
        out_shape=jax.ShapeDtypeStruct((T, 4 * AW), BF16),
        scratch_shapes=_slab_scratch(6, tm),
        compiler_params=_cparams(("parallel",)),
    )(*args, dz, cos128, sin128)


def _dx_out(dmix, w_o, dep):
    tm = ROPE_TM

    def body(a_ref, w_ref, dep_ref, dcat_ref, o4, o16, *slabs):
        prod = _nt(a_ref[...].astype(BF16), w_ref[...].astype(BF16))
        dcat_ref[...] = prod
        for j in range(SLABS):
            slabs[j][...] = prod[:, 128 * j:128 * (j + 1)]
        _dilate(o4, slabs, 4, tm)
        _dilate(o16, slabs, 16, tm)

    return pl.pallas_call(
        body, name="dx_out", grid=(T // tm,),
        in_specs=[pl.BlockSpec((tm, D), lambda i: (i, 0)), pl.BlockSpec((D, D), lambda i: (0, 0)), _ANY],
        out_specs=[pl.BlockSpec((tm, D), lambda i: (i, 0)), _dil_spec(tm, 4), _dil_spec(tm, 16)],
        out_shape=[jax.ShapeDtypeStruct((T, D), F32), jax.ShapeDtypeStruct((T // 4, 4 * AW), F32),
                   jax.ShapeDtypeStruct((T // 16, 16 * AW), F32)],
        scratch_shapes=_slab_scratch(1, tm), compiler_params=_cparams(("parallel",)),
    )(dmix, w_o, dep)


def _band_masks():
    qi = lax.broadcasted_iota(jnp.int32, (QBLK, QBLK), 0)
    kj = lax.broadcasted_iota(jnp.int32, (QBLK, QBLK), 1)
    return kj >= qi, kj <= qi


def _attn_fwd(q, k, v, d):
    L = q.shape[0]
    npair = L // (2 * QBLK)

    def body(q_ref, kp_ref, kc_ref, vp_ref, vc_ref, o_ref, l_ref):
        pair = pl.program_id(1)
        mask_p, mask_c = _band_masks()
        for sub in range(2):
            rows = pl.ds(sub * QBLK, QBLK)
            first = jnp.where(pair > 0, 0.0, NEG) if sub == 0 else 0.0
            bias = jnp.concatenate([jnp.where(mask_p, 0.0, NEG) + first, jnp.where(mask_c, 0.0, NEG)], axis=1)
            k_prev = (lambda sl: kp_ref[:, sl]) if sub == 0 else (lambda sl: kc_ref[pl.ds(0, QBLK), sl])
            v_prev = (lambda sl: vp_ref[:, sl]) if sub == 0 else (lambda sl: vc_ref[pl.ds(0, QBLK), sl])
            s = []
            for h in range(HEADS):
                sl = pl.ds(HD * h, HD)
                qh = q_ref[rows, sl]
                s.append(jnp.concatenate([_nt(qh, k_prev(sl)), _nt(qh, kc_ref[rows, sl])], axis=1))
            s = jnp.stack(s) + bias
            m = jnp.max(s, axis=2, keepdims=True)
            e = jnp.exp(s - m)
            den = jnp.sum(e, axis=2, keepdims=True)
            p = e.astype(BF16)
            inv = 1.0 / den
            lse = m + jnp.log(den)
            for h in range(HEADS):
                sl = pl.ds(HD * h, HD)
                o_ref[rows, sl] = ((_nn(p[h, :, :QBLK], v_prev(sl)) + _nn(p[h, :, QBLK:], vc_ref[rows, sl])) * inv[h]
                                   ).astype(BF16)
                l_ref[rows, sl] = jnp.broadcast_to(lse[h], (QBLK, HD))

    cur = pl.BlockSpec((2 * QBLK, AW), lambda r, n: (n, r))
    prev = pl.BlockSpec((QBLK, AW), lambda r, n: (jnp.maximum(2 * n - 1, 0), r))
    return pl.pallas_call(
        body, name=f"attn_fwd_d{d}", grid=(d, npair),
        in_specs=[cur, prev, cur, prev, cur], out_specs=[cur, cur],
        out_shape=[jax.ShapeDtypeStruct((L, d * AW), BF16), jax.ShapeDtypeStruct((L, d * AW), F32)],
        compiler_params=_cparams(("parallel", "parallel")),
    )(q, k, k, v, v)


def _attn_bwd(q, k, v, do, at, lse, d):
    L = q.shape[0]
    nb = L // QBLK
    npair = nb // 2

    def body(qc_ref, qn_ref, kp_ref, kc_ref, vp_ref, vc_ref, doc_ref, don_ref, atc_ref, atn_ref,
             lc_ref, ln_ref, dq_ref, dk_ref, dv_ref):
        pair = pl.program_id(1)
        mask_p, mask_c = _band_masks()
        prev_bias = jnp.where(mask_p, 0.0, NEG)
        for sub in range(2):
            rows = pl.ds(sub * QBLK, QBLK)
            second = pl.ds(QBLK, QBLK)
            if sub == 0:
                take = lambda cur_ref, nxt_ref, cols, i: cur_ref[rows if i == 0 else second, cols]
                prev_of = lambda p_ref, c_ref, cols: p_ref[:, cols]
                first, last = jnp.where(pair > 0, 0.0, NEG), 0.0
            else:
                take = lambda cur_ref, nxt_ref, cols, i: cur_ref[rows, cols] if i == 0 else nxt_ref[:, cols]
                prev_of = lambda p_ref, c_ref, cols: c_ref[pl.ds(0, QBLK), cols]
                first, last = 0.0, jnp.where(pair < npair - 1, 0.0, NEG)
            bias = jnp.concatenate([prev_bias + first, jnp.where(mask_c, 0.0, NEG), prev_bias + last], axis=1)
            s, dp, ls, dl, ops = [], [], [], [], []
            for h in range(HEADS):
                sl = pl.ds(HD * h, HD)
                one = pl.ds(HD * h, 1)
                q0, q1 = take(qc_ref, qn_ref, sl, 0), take(qc_ref, qn_ref, sl, 1)
                kp, kc = prev_of(kp_ref, kc_ref, sl), kc_ref[rows, sl]
                vp, vc = prev_of(vp_ref, vc_ref, sl), vc_ref[rows, sl]
                do0, do1 = take(doc_ref, don_ref, sl, 0), take(doc_ref, don_ref, sl, 1)
                do0b, do1b = do0.astype(BF16), do1.astype(BF16)
                s.append(jnp.concatenate([_nt(q0, kp), _nt(q0, kc), _nt(q1, kc)], axis=1))
                dp.append(jnp.concatenate([_nt(do0b, vp), _nt(do0b, vc), _nt(do1b, vc)], axis=1))
                dl0 = jnp.sum(do0 * take(atc_ref, atn_ref, sl, 0), axis=1, keepdims=True)
                dl1 = jnp.sum(do1 * take(atc_ref, atn_ref, sl, 1), axis=1, keepdims=True)
                dl.append(jnp.concatenate([jnp.broadcast_to(dl0, (QBLK, 2 * QBLK)), jnp.broadcast_to(dl1, (QBLK, QBLK))], axis=1))
                ls.append(jnp.concatenate([jnp.broadcast_to(take(lc_ref, ln_ref, one, 0), (QBLK, 2 * QBLK)),
                                           jnp.broadcast_to(take(lc_ref, ln_ref, one, 1), (QBLK, QBLK))], axis=1))
                ops.append((q0, q1, kp, kc, do0b, do1b))
            p = jnp.exp(jnp.stack(s) + bias - jnp.stack(ls))
            ds = (p * (jnp.stack(dp) - jnp.stack(dl))).astype(BF16)
            p = p.astype(BF16)
            for h in range(HEADS):
                sl = pl.ds(HD * h, HD)
                q0, q1, kp, kc, do0b, do1b = ops[h]
                dq_ref[rows, sl] = (_nn(ds[h, :, :QBLK], kp) + _nn(ds[h, :, QBLK:2 * QBLK], kc)).astype(BF16)
                dv_ref[rows, sl] = (_tn(p[h, :, QBLK:2 * QBLK], do0b) + _tn(p[h, :, 2 * QBLK:], do1b)).astype(BF16)
                dk_ref[rows, sl] = (_tn(ds[h, :, QBLK:2 * QBLK], q0) + _tn(ds[h, :, 2 * QBLK:], q1)).astype(BF16)

    cur = pl.BlockSpec((2 * QBLK, AW), lambda r, n: (n, r))
    prev = pl.BlockSpec((QBLK, AW), lambda r, n: (jnp.maximum(2 * n - 1, 0), r))
    nxt = pl.BlockSpec((QBLK, AW), lambda r, n: (jnp.minimum(2 * n + 2, nb - 1), r))
    return pl.pallas_call(
        body, name=f"attn_bwd_d{d}", grid=(d, npair),
        in_specs=[cur, nxt, prev, cur, prev, cur, cur, nxt, cur, nxt, cur, nxt], out_specs=[cur, cur, cur],
        out_shape=[jax.ShapeDtypeStruct((L, d * AW), BF16)] * 3,
        compiler_params=_cparams(("parallel", "parallel")),
    )(q, q, k, k, v, v, do, do, at, at, lse, lse)


def _attn_merge(outs, lses):
    tm = ROPE_TM

    def body(o1, o4, o16, l1, l4, l16, at_ref, ls_ref, at4, ls4, at16, ls16, *flat):
        so4, so16, sl4, sl16, sa, sl = _slab_groups(flat)
        _undilate(o4, so4, 4, tm)
        _undilate(o16, so16, 16, tm)
        _undilate(l4, sl4, 4, tm)
        _undilate(l16, sl16, 16, tm)
        for j in range(SLABS):
            cols = pl.ds(j * 128, 128)
            a, b, c = l1[:, cols], sl4[j][...], sl16[j][...]
            m = jnp.maximum(jnp.maximum(a, b), c)
            e1, e2, e3 = jnp.exp(a - m), jnp.exp(b - m), jnp.exp(c - m)
            s = e1 + e2 + e3
            inv = 1.0 / s
            attn = (e1 * inv) * o1[:, cols] + (e2 * inv) * so4[j][...] + (e3 * inv) * so16[j][...]
            lse = m + jnp.log(s)
            at_ref[:, cols] = attn
            ls_ref[:, cols] = lse
            sa[j][...] = attn
            sl[j][...] = lse
        _dilate(at4, sa, 4, tm)
        _dilate(at16, sa, 16, tm)
        _dilate(ls4, sl, 4, tm)
        _dilate(ls16, sl, 16, tm)

    specs = [_dil_spec(tm, d) for d in DILATIONS]
    tok = specs[0]
    return pl.pallas_call(
        body, name="attn_merge", grid=(T // tm,),
        in_specs=specs + specs, out_specs=[tok, tok, specs[1], specs[1], specs[2], specs[2]],
        out_shape=[jax.ShapeDtypeStruct((T, AW), F32)] * 2 + [jax.ShapeDtypeStruct((T // 4, 4 * AW), F32)] * 2
        + [jax.ShapeDtypeStruct((T // 16, 16 * AW), F32)] * 2,
        scratch_shapes=_slab_scratch(6, tm),
        compiler_params=_cparams(("parallel",)),
    )(*outs, *lses)


CONV_TM = 512
HALO = 8


def _conv_pre(ext, w, b):
    y = b + w[3] * ext
    for kk in range(1, CONV_K):
        y = y + w[3 - kk] * pltpu.roll(ext, kk, 0)
    return y


def _rows_to_block(rows, n, width):
    ri = lax.broadcasted_iota(jnp.int32, (n, width), 0)
    out = jnp.zeros((n, width), F32)
    for j, r in enumerate(rows):
        out = out + jnp.where(ri == j, r, 0.0)
    return out


def _conv_bwd(xbc, dact, ddt, w, b):
    nblk = T // CONV_TM
    per = CONV_TM // HALO

    def body(x_ref, xb_ref, xa_ref, g_ref, ga_ref, ddt_ref, w_ref, b_ref, dx_ref, dw_ref):
        i = pl.program_id(0)
        wv = [w_ref[pl.ds(j, 1), :] for j in range(CONV_K)]
        before = jnp.where(i > 0, xb_ref[...], 0.0)
        last = i == nblk - 1
        after = jnp.where(last, 0.0, xa_ref[...])
        g_after = jnp.where(last, 0.0, ga_ref[...])
        ext = jnp.concatenate([before, x_ref[...], after], axis=0)
        y = _conv_pre(ext, wv, b_ref[...])[HALO:]
        sg = _sigmoid(y)
        dy = jnp.concatenate([g_ref[...], g_after], axis=0) * (sg * (1.0 + y * (1.0 - sg)))
        n = CONV_TM + HALO
        dx = wv[3] * dy
        for kk in range(1, CONV_K):
            dx = dx + wv[3 - kk] * pltpu.roll(dy, n - kk, 0)
        dx_ref[:, pl.ds(0, CONV_CH)] = dx[:CONV_TM].astype(BF16)
        dx_ref[:, pl.ds(CONV_CH, DT_PAD)] = ddt_ref[...].astype(BF16)
        dyc = dy[:CONV_TM]
        rows = [jnp.sum(dyc * (pltpu.roll(ext, 3 - j, 0) if j < 3 else ext)[HALO:HALO + CONV_TM], axis=0, keepdims=True)
                for j in range(CONV_K)]
        rows.append(jnp.sum(dyc, axis=0, keepdims=True))
        part = _rows_to_block(rows, 8, CONV_CH)

        @pl.when(i == 0)
        def _():
            dw_ref[...] = jnp.zeros_like(dw_ref)
        dw_ref[...] += part

    blk = pl.BlockSpec((CONV_TM, CONV_CH), lambda i: (i, 0))
    hb = pl.BlockSpec((HALO, CONV_CH), lambda i: (jnp.maximum(i * per - 1, 0), 0))
    ha = pl.BlockSpec((HALO, CONV_CH), lambda i: (jnp.minimum((i + 1) * per, T // HALO - 1), 0))
    return pl.pallas_call(
        body, name="conv_bwd", grid=(nblk,),
        in_specs=[blk, hb, ha, blk, ha, pl.BlockSpec((CONV_TM, DT_PAD), lambda i: (i, 0)),
                  pl.BlockSpec((CONV_K, CONV_CH), lambda i: (0, 0)), pl.BlockSpec((1, CONV_CH), lambda i: (0, 0))],
        out_specs=[pl.BlockSpec((CONV_TM, CONV_CH + DT_PAD), lambda i: (i, 0)), pl.BlockSpec((8, CONV_CH), lambda i: (0, 0))],
        out_shape=[jax.ShapeDtypeStruct((T, CONV_CH + DT_PAD), BF16), jax.ShapeDtypeStruct((8, CONV_CH), F32)],
        compiler_params=_cparams(("arbitrary",)),
    )(xbc, xbc, xbc, dact, dact, ddt, w, b)


def _pick(mat, h):
    lane = lax.broadcasted_iota(jnp.int32, mat.shape, 1)
    return jnp.sum(jnp.where(lane == h, mat, 0.0), axis=1, keepdims=True)


def _heads(fn):
    return jnp.stack([fn(h) for h in range(HEADS)])


def _ssd_prep(dt_ref, bias_ref, alog_ref, dsk_ref, b_ref, c_ref, xs_ref, state_ref, cst):
    li = lax.broadcasted_iota(jnp.int32, (CHUNK, CHUNK), 0)
    si = lax.broadcasted_iota(jnp.int32, (CHUNK, CHUNK), 1)
    tri = li >= si
    dtp = dt_ref[...] + bias_ref[...]
    dt = _softplus(dtp)
    A = -jnp.exp(alog_ref[...])
    a = dt * A
    cs = jnp.dot(tri.astype(F32), a, precision=HIGHEST, preferred_element_type=F32)
    cst[...] = cs.T
    Bm = b_ref[...].astype(BF16)
    Cm = c_ref[...].astype(BF16)
    cb = _nt(Cm, Bm)
    dskv = dsk_ref[...]
    cs_col = _heads(lambda h: _pick(cs, h))
    cs_row = _heads(lambda h: cst[pl.ds(h, 1), :])
    dt_col = _heads(lambda h: _pick(dt, h))
    dsk_col = _heads(lambda h: _pick(dskv, h))
    lam = jnp.exp(jnp.where(tri, cs_col - cs_row, NEG))
    x = _heads(lambda h: xs_ref[:, pl.ds(HD * h, HD)])
    xdt = x * dt_col
    prev = _heads(lambda h: state_ref[pl.ds(HD * h, HD), :])
    lane = lax.broadcasted_iota(jnp.int32, (1, 1, CHUNK), 2)
    cl = jnp.sum(jnp.where(lane == CHUNK - 1, cs_row, 0.0), axis=2, keepdims=True)
    f = jnp.exp(cl - cs_col)
    return dict(li=li, si=si, dtp=dtp, dt=dt, A=A, Bm=Bm, Cm=Cm, cb=cb, cs_col=cs_col, dt_col=dt_col, dsk_col=dsk_col,
                lam=lam, x=x, xdt=xdt, prev=prev, cl=cl, f=f)


def _ssd_fwd(xbcdt, conv_w, conv_b, bias, alog, dsk, qkvz, attn, gs):
    nc = T // CHUNK
    R = SSD_PER * CHUNK
    per = R // HALO

    def body(xbc_ref, halo_ref, cw_ref, cb_ref, dt_ref, bias_ref, alog_ref, dsk_ref, z_ref, at_ref, gs_ref,
             y_ref, st_ref, cat_ref, act_ref, state, cst):
        @pl.when(pl.program_id(0) == 0)
        def _():
            state[...] = jnp.zeros_like(state)
        for sub in range(SSD_PER):
            rows = pl.ds(sub * CHUNK, CHUNK)
            st_ref[sub] = state[...]
            halo = (jnp.where(pl.program_id(0) > 0, halo_ref[...], 0.0) if sub == 0
                    else xbc_ref[pl.ds(sub * CHUNK - HALO, HALO), :])
            one_chunk(halo, xbc_ref.at[rows, :], cw_ref, cb_ref, dt_ref.at[rows, :], bias_ref, alog_ref, dsk_ref,
                      z_ref.at[rows, :], at_ref.at[rows, :], gs_ref, y_ref.at[rows, :], cat_ref.at[rows, :],
                      act_ref.at[rows, :], state, cst)

    def one_chunk(halo, xbc_ref, cw_ref, cb_ref, dt_ref, bias_ref, alog_ref, dsk_ref, z_ref, at_ref, gs_ref,
                  y_ref, cat_ref, act_ref, state, cst):
        pre = _conv_pre(jnp.concatenate([halo, xbc_ref[...]], axis=0),
                        [cw_ref[pl.ds(j, 1), :] for j in range(CONV_K)], cb_ref[...])[HALO:]
        act_ref[...] = pre * _sigmoid(pre)
        xs_ref, b_ref, c_ref = (act_ref.at[:, pl.ds(0, AW)], act_ref.at[:, pl.ds(AW, NS)],
                                act_ref.at[:, pl.ds(AW + NS, NS)])
        s = _ssd_prep(dt_ref, bias_ref, alog_ref, dsk_ref, b_ref, c_ref, xs_ref, state, cst)
        Bm, Cm, prev = s["Bm"], s["Cm"], s["prev"]
        g = (s["cb"] * s["lam"]).astype(BF16)
        xdtb = s["xdt"].astype(BF16)
        prevb = prev.astype(BF16)
        y = _heads(lambda h: _nn(g[h], xdtb[h])) + _heads(lambda h: _nt(Cm, prevb[h])) * jnp.exp(s["cs_col"])
        y = y + s["dsk_col"] * s["x"]
        xf = (s["xdt"] * s["f"]).astype(BF16)
        new = prev * jnp.exp(s["cl"]) + _heads(lambda h: _tn(xf[h], Bm))
        for h in range(HEADS):
            y_ref[:, pl.ds(HD * h, HD)] = y[h]
            state[pl.ds(HD * h, HD), :] = new[h]
        z = z_ref[...]
        gi = y_ref[...] * (z * _sigmoid(z))
        cat_ref[:, pl.ds(0, AW)] = at_ref[...].astype(BF16)
        cat_ref[:, pl.ds(AW, AW)] = (gi * _rstd(gi) * gs_ref[...]).astype(BF16)

    vec = pl.BlockSpec((1, DT_PAD), lambda c: (0, 0))
    blk = pl.BlockSpec((R, AW), lambda c: (c, 0))
    return pl.pallas_call(
        body, name="ssd_fwd", grid=(nc // SSD_PER,),
        in_specs=[pl.BlockSpec((R, CONV_CH), lambda c: (c, 0)),
                  pl.BlockSpec((HALO, CONV_CH), lambda c: (jnp.maximum(c * per - 1, 0), 0)),
                  pl.BlockSpec((CONV_K, CONV_CH), lambda c: (0, 0)), pl.BlockSpec((1, CONV_CH), lambda c: (0, 0)),
                  pl.BlockSpec((R, DT_PAD), lambda c: (c, 6)),
                  vec, vec, vec, pl.BlockSpec((R, AW), lambda c: (c, 3)), blk, pl.BlockSpec((1, AW), lambda c: (0, 0))],
        out_specs=[blk, pl.BlockSpec((SSD_PER, AW, NS), lambda c: (c, 0, 0)), pl.BlockSpec((R, D), lambda c: (c, 0)),
                   pl.BlockSpec((R, CONV_CH), lambda c: (c, 0))],
        out_shape=[jax.ShapeDtypeStruct((T, AW), F32), jax.ShapeDtypeStruct((nc, AW, NS), F32),
                   jax.ShapeDtypeStruct((T, D), BF16), jax.ShapeDtypeStruct((T, CONV_CH), F32)],
        scratch_shapes=[pltpu.VMEM((AW, NS), F32), pltpu.VMEM((CHUNK, CHUNK), F32)],
        compiler_params=_cparams(("arbitrary",)),
    )(xbcdt, xbcdt, conv_w, conv_b, xbcdt, bias, alog, dsk, qkvz, attn, gs)


def _ssd_bwd(act, xbcdt, bias, alog, dsk, states, y_ssd, qkvz, dcat, gs):
    nc = T // CHUNK

    def body(xs_ref, b_ref, c_ref, dt_ref, bias_ref, alog_ref, dsk_ref, st_ref, y_ref, z_ref, dyn_ref, gs_ref,
             dact_ref, ddt_ref, par_ref, dz_ref, dgs_ref, dstate, cst, dy_ref):
        @pl.when(pl.program_id(0) == 0)
        def _():
            dstate[...] = jnp.zeros_like(dstate)
            par_ref[...] = jnp.zeros_like(par_ref)
            dgs_ref[...] = jnp.zeros_like(dgs_ref)
        for sub in reversed(range(SSD_PER)):
            rows = pl.ds(sub * CHUNK, CHUNK)
            one_chunk(xs_ref.at[rows, :], b_ref.at[rows, :], c_ref.at[rows, :], dt_ref.at[rows, :], bias_ref, alog_ref,
                      dsk_ref, st_ref.at[sub], y_ref.at[rows, :], z_ref.at[rows, :], dyn_ref.at[rows, :], gs_ref,
                      dact_ref.at[rows, :], ddt_ref.at[rows, :], par_ref, dz_ref.at[rows, :], dgs_ref, dstate, cst, dy_ref)

    def one_chunk(xs_ref, b_ref, c_ref, dt_ref, bias_ref, alog_ref, dsk_ref, st_ref, y_ref, z_ref, dyn_ref, gs_ref,
                  dact_ref, ddt_ref, par_ref, dz_ref, dgs_ref, dstate, cst, dy_ref):
        z, yv, dyn = z_ref[...], y_ref[...], dyn_ref[...]
        sg = _sigmoid(z)
        sz = z * sg
        gi = yv * sz
        rg = _rstd(gi)
        ng = gi * rg
        dgi = _rms_bwd(ng, rg, gs_ref[...], dyn)
        dy_ref[...] = dgi * sz
        dz_ref[...] = dgi * yv * (sg * (1.0 + z * (1.0 - sg)))
        dgs_ref[...] += _colsum(dyn * ng)
        s = _ssd_prep(dt_ref, bias_ref, alog_ref, dsk_ref, b_ref, c_ref, xs_ref, st_ref, cst)
        Bm, Cm, prev, lam, x, xdt, f, cl = s["Bm"], s["Cm"], s["prev"], s["lam"], s["x"], s["xdt"], s["f"], s["cl"]
        lane = lax.broadcasted_iota(jnp.int32, (1, DT_PAD), 1)
        row = lax.broadcasted_iota(jnp.int32, (1, CHUNK, 1), 1)
        g = s["cb"] * lam
        gb, xdtb, prevb = g.astype(BF16), xdt.astype(BF16), prev.astype(BF16)
        dy = _heads(lambda h: dy_ref[:, pl.ds(HD * h, HD)])
        dyb = dy.astype(BF16)
        dnew = _heads(lambda h: dstate[pl.ds(HD * h, HD), :])
        dnewb = dnew.astype(BF16)
        E = jnp.exp(s["cs_col"])
        ecl = jnp.exp(cl)
        dG = _heads(lambda h: _nt(dyb[h], xdtb[h]))
        dxdt = _heads(lambda h: _tn(gb[h], dyb[h]))
        Yo = _heads(lambda h: _nt(Cm, prevb[h]))
        W = _heads(lambda h: _nt(Bm, dnewb[h]))
        dcb = jnp.sum(dG * lam, axis=0)
        Mm = dG * g
        col_sums = jnp.sum(Mm, axis=1, keepdims=True)
        dYo = (dy * E).astype(BF16)
        dxdt = dxdt + W * f
        dF = jnp.sum(W * xdt, axis=2, keepdims=True) * f
        dcl = jnp.sum(dnew * prev, axis=(1, 2), keepdims=True) * ecl + jnp.sum(dF, axis=1, keepdims=True)
        dcs = (jnp.sum(Mm, axis=2, keepdims=True) + jnp.sum(dy * Yo, axis=2, keepdims=True) * E - dF
               + jnp.where(row == CHUNK - 1, dcl, 0.0))
        ddt_x = jnp.sum(dxdt * x, axis=2, keepdims=True)
        dD = jnp.sum(dy * x, axis=(1, 2), keepdims=True)
        dx = s["dsk_col"] * dy + dxdt * s["dt_col"]
        xfb = (xdt * f).astype(BF16)
        dprev = _heads(lambda h: _tn(dYo[h], Cm)) + dnew * ecl
        dcbb = dcb.astype(BF16)
        dC = _nn(dcbb, Bm)
        dB = _tn(dcbb, Cm)
        dcs_mat = -_rows_to_block([col_sums[h] for h in range(HEADS)], CHUNK, CHUNK).T
        ddt_mat = jnp.zeros((CHUNK, DT_PAD), F32)
        dD_row = jnp.zeros((1, DT_PAD), F32)
        for h in range(HEADS):
            sl = pl.ds(HD * h, HD)
            dC = dC + _nn(dYo[h], prevb[h])
            dB = dB + _nn(xfb[h], dnewb[h])
            dcs_mat = dcs_mat + jnp.where(lane == h, dcs[h], 0.0)
            ddt_mat = ddt_mat + jnp.where(lane == h, ddt_x[h], 0.0)
            dD_row = dD_row + jnp.where(lane == h, dD[h], 0.0)
            dact_ref[:, sl] = dx[h]
            dstate[sl, :] = dprev[h]
        dact_ref[:, pl.ds(AW, NS)] = dB
        dact_ref[:, pl.ds(AW + NS, NS)] = dC
        da = jnp.dot((s["li"] <= s["si"]).astype(F32), dcs_mat, precision=HIGHEST, preferred_element_type=F32)
        ddtp = jnp.where(lane < HEADS, (ddt_mat + da * s["A"]) * _sigmoid(s["dtp"]), 0.0)
        ddt_ref[...] = ddtp
        dalog = jnp.where(lane < HEADS, jnp.sum(da * s["dt"], axis=0, keepdims=True) * s["A"], 0.0)
        par_ref[...] += _rows_to_block([jnp.sum(ddtp, axis=0, keepdims=True), dalog, dD_row], 8, DT_PAD)

    vec = pl.BlockSpec((1, DT_PAD), lambda c: (0, 0))
    nstep = nc // SSD_PER
    rev = lambda c: nstep - 1 - c
    R = SSD_PER * CHUNK
    return pl.pallas_call(
        body, name="ssd_bwd", grid=(nstep,),
        in_specs=[pl.BlockSpec((R, AW), lambda c: (rev(c), 0)), pl.BlockSpec((R, NS), lambda c: (rev(c), 4)),
                  pl.BlockSpec((R, NS), lambda c: (rev(c), 5)), pl.BlockSpec((R, DT_PAD), lambda c: (rev(c), 6)),
                  vec, vec, vec,
                  pl.BlockSpec((SSD_PER, AW, NS), lambda c: (rev(c), 0, 0)), pl.BlockSpec((R, AW), lambda c: (rev(c), 0)),
                  pl.BlockSpec((R, AW), lambda c: (rev(c), 3)), pl.BlockSpec((R, AW), lambda c: (rev(c), 1)),
                  pl.BlockSpec((1, AW), lambda c: (0, 0))],
        out_specs=[pl.BlockSpec((R, CONV_CH), lambda c: (rev(c), 0)), pl.BlockSpec((R, DT_PAD), lambda c: (rev(c), 0)),
                   pl.BlockSpec((8, DT_PAD), lambda c: (0, 0)), pl.BlockSpec((R, AW), lambda c: (rev(c), 0)),
                   pl.BlockSpec((1, AW), lambda c: (0, 0))],
        out_shape=[jax.ShapeDtypeStruct((T, CONV_CH), F32), jax.ShapeDtypeStruct((T, DT_PAD), F32),
                   jax.ShapeDtypeStruct((8, DT_PAD), F32), jax.ShapeDtypeStruct((T, AW), F32),
                   jax.ShapeDtypeStruct((1, AW), F32)],
        scratch_shapes=[pltpu.VMEM((AW, NS), F32), pltpu.VMEM((CHUNK, CHUNK), F32), pltpu.VMEM((CHUNK, AW), F32)],
        compiler_params=_cparams(("arbitrary",)),
    )(act, act, act, xbcdt, bias, alog, dsk, states, y_ssd, qkvz, dcat, gs)


def _place():
    return lax.axis_index("x"), lax.axis_index("y"), lax.axis_index("c")


def _slot(px, py, pc):
    return 4 * px + 2 * py + pc


SLAB_ROWS = 24


def _slab_pack(parts, name):
    n = len(parts)

    def body(*refs):
        slab = refs[n]
        slab[...] = jnp.zeros_like(slab)
        for ref, (arr, row) in zip(refs[:n], parts):
            slab[pl.ds(row, arr.shape[0]), pl.ds(0, arr.shape[1])] = ref[...]

    vm = pl.BlockSpec(memory_space=pltpu.VMEM)
    return pl.pallas_call(
        body, name=name, in_specs=[vm] * n, out_specs=vm, out_shape=jax.ShapeDtypeStruct((SLAB_ROWS, D), F32),
    )(*[a for a, _ in parts])


_HBM = pl.BlockSpec(memory_space=pltpu.HBM)
_SEM = pl.BlockSpec(memory_space=pltpu.SEMAPHORE)
_EFFECT = pltpu.SideEffectType.DATAFLOW_SIDE_EFFECTING


def _peers(x, y, c):
    out = []
    for kk in range(1, N_DEV):
        fx, fy, fc = kk >> 2 & 1, kk >> 1 & 1, kk & 1
        out.append((1 - x if fx else x, 1 - y if fy else y, 1 - c if fc else c))
    return out


def _send_start(src, per_peer, name, dep):
    (handles, token) = _send_start_many([src], per_peer, name, dep)
    return handles, token


def _near_peers(x, y, c):
    return [(x, y, 1 - c), (1 - x, y, c), (x, 1 - y, c), (1 - x, 1 - y, c)]


def _send_start_many(srcs, per_peer, name, dep, peers=_peers, npeers=N_DEV - 1):
    n = len(srcs)

    def body(*refs):
        src_refs, land_refs = refs[:n], refs[n:2 * n]
        send_sems, recv_sems = refs[2 * n + 1], refs[2 * n + 2]
        token = refs[-1]
        x, y, c = _place()
        mine = _slot(x, y, c)
        for a in range(n):
            for kk, peer in enumerate(peers(x, y, c)):
                pltpu.make_async_remote_copy(
                    src_ref=src_refs[a].at[_slot(*peer)] if per_peer else src_refs[a], dst_ref=land_refs[a].at[mine],
                    send_sem=send_sems.at[a * npeers + kk], recv_sem=recv_sems.at[a * npeers + kk],
                    device_id=peer, device_id_type=MESH).start()
        token[...] = jnp.zeros_like(token)

    lands = [lax.empty((N_DEV,) + tuple(s.shape[1:] if per_peer else s.shape), s.dtype) for s in srcs]
    hbm = lambda t: pltpu.with_memory_space_constraint(t, pltpu.HBM)
    outs = pl.pallas_call(
        body, name=name,
        out_shape=(pltpu.SemaphoreType.DMA((n * npeers,)), pltpu.SemaphoreType.DMA((n * npeers,)),
                   *[pltpu.HBM(s.shape, s.dtype) for s in srcs], *[pltpu.HBM(l.shape, l.dtype) for l in lands],
                   jax.ShapeDtypeStruct((8, 128), F32)),
        in_specs=(*[_HBM] * (2 * n), _ANY),
        out_specs=(_SEM, _SEM, *[_HBM] * (2 * n), pl.BlockSpec(memory_space=pltpu.VMEM)),
        input_output_aliases={i: 2 + i for i in range(2 * n)},
        compiler_params=pltpu.CompilerParams(has_side_effects=_EFFECT),
    )(*[hbm(s) for s in srcs], *[hbm(l) for l in lands], dep)
    return (outs[0], outs[1], list(outs[2:2 + n]), list(outs[2 + n:2 + 2 * n])), outs[-1]


def _send_wait(handles, after, name):
    srcs, lands = _send_wait_many(handles, after, name)
    return srcs[0], lands[0]


def _send_wait_many(handles, after, name, npeers=N_DEV - 1):
    send_sems, recv_sems, src_thrus, land_thrus = handles
    n = len(src_thrus)

    def body(*refs):
        land_refs = refs[n:2 * n]
        send_sems, recv_sems = refs[2 * n], refs[2 * n + 1]
        me = _place()
        for a in range(n):
            for kk in range(npeers):
                cp = pltpu.make_async_remote_copy(
                    src_ref=land_refs[a].at[0], dst_ref=land_refs[a].at[0],
                    send_sem=send_sems.at[a * npeers + kk], recv_sem=recv_sems.at[a * npeers + kk],
                    device_id=me, device_id_type=MESH)
                cp.wait_send()
                cp.wait_recv()

    both = list(src_thrus) + list(land_thrus)
    outs = pl.pallas_call(
        body, name=name,
        out_shape=tuple(pltpu.HBM(t.shape, t.dtype) for t in both),
        in_specs=(*[_HBM] * (2 * n), _SEM, _SEM, _ANY), out_specs=tuple([_HBM] * (2 * n)),
        input_output_aliases={i: i for i in range(2 * n)},
        compiler_params=pltpu.CompilerParams(has_side_effects=_EFFECT),
    )(*both, send_sems, recv_sems, after)
    return list(outs[:n]), list(outs[n:])


def _forward_start(lands, name, dep):
    n = len(lands)

    def body(*refs):
        land_refs = refs[:n]
        send_sems, recv_sems = refs[n + 1], refs[n + 2]
        token = refs[-1]
        x, y, c = _place()
        for a in range(n):
            for j, chip in enumerate([(1 - x, y), (x, 1 - y), (1 - x, 1 - y)]):
                blk = land_refs[a].at[_slot(*chip, c)]
                pltpu.make_async_remote_copy(
                    src_ref=blk, dst_ref=blk, send_sem=send_sems.at[a * 3 + j], recv_sem=recv_sems.at[a * 3 + j],
                    device_id=(x, y, 1 - c), device_id_type=MESH).start()
        token[...] = jnp.zeros_like(token)

    outs = pl.pallas_call(
        body, name=name,
        out_shape=(pltpu.SemaphoreType.DMA((n * 3,)), pltpu.SemaphoreType.DMA((n * 3,)),
                   *[pltpu.HBM(l.shape, l.dtype) for l in lands], jax.ShapeDtypeStruct((8, 128), F32)),
        in_specs=(*[_HBM] * n, _ANY), out_specs=(_SEM, _SEM, *[_HBM] * n, pl.BlockSpec(memory_space=pltpu.VMEM)),
        input_output_aliases={i: 2 + i for i in range(n)},
        compiler_params=pltpu.CompilerParams(has_side_effects=_EFFECT),
    )(*lands, dep)
    return (outs[0], outs[1], list(outs[2:2 + n])), outs[-1]


def _forward_wait(handles, after, name):
    send_sems, recv_sems, land_thrus = handles
    n = len(land_thrus)

    def body(*refs):
        land_refs = refs[:n]
        send_sems, recv_sems = refs[n], refs[n + 1]
        me = _place()
        for a in range(n):
            for j in range(3):
                cp = pltpu.make_async_remote_copy(
                    src_ref=land_refs[a].at[0], dst_ref=land_refs[a].at[0],
                    send_sem=send_sems.at[a * 3 + j], recv_sem=recv_sems.at[a * 3 + j], device_id=me, device_id_type=MESH)
                cp.wait_send()
                cp.wait_recv()

    outs = pl.pallas_call(
        body, name=name,
        out_shape=tuple(pltpu.HBM(t.shape, t.dtype) for t in land_thrus),
        in_specs=(*[_HBM] * n, _SEM, _SEM, _ANY), out_specs=tuple([_HBM] * n),
        input_output_aliases={i: i for i in range(n)},
        compiler_params=pltpu.CompilerParams(has_side_effects=_EFFECT),
    )(*land_thrus, send_sems, recv_sems, after)
    return list(outs)


def _sum_slots(land, name):
    _, R, C = land.shape
    tm = R if R <= 512 else 512

    def body(x_ref, o_ref):
        acc = x_ref[0].astype(F32)
        for j in range(1, N_DEV):
            acc = acc + x_ref[j].astype(F32)
        o_ref[...] = acc

    return pl.pallas_call(
        body, name=name, grid=(R // tm,),
        in_specs=[pl.BlockSpec((N_DEV, tm, C), lambda i: (0, i, 0))], out_specs=pl.BlockSpec((tm, C), lambda i: (i, 0)),
        out_shape=jax.ShapeDtypeStruct((R, C), F32), compiler_params=_cparams(("parallel",)),
    )(land)


def _adam_math(w, g, m, v):
    m2 = ADAM_B1 * m + (1.0 - ADAM_B1) * g
    v2 = ADAM_B2 * v + (1.0 - ADAM_B2) * (g * g)
    m_hat = m2 / (1.0 - ADAM_B1 ** ADAM_STEP)
    v_hat = v2 / (1.0 - ADAM_B2 ** ADAM_STEP)
    delta = -ADAM_LR * (m_hat / (jnp.sqrt(v_hat) + ADAM_EPS) + ADAM_WD * w)
    return delta, m2, v2


def _adamw(w, g, m, v, name):
    R, C = w.shape
    tm = R if R <= 512 else 256
    return _rowwise(lambda w, g, m, v: (_adam_math(w, g, m, v), ()), [w, g, m, v], [], [(C, F32)] * 3, [], tm=tm, name=name)


def _adamw_slots(land, w, m, v, name):
    _, R, C = land.shape
    tm = R if R <= 256 else 256

    def body(x_ref, w_ref, m_ref, v_ref, g_ref, d_ref, mo_ref, vo_ref):
        g = x_ref[0].astype(F32)
        for j in range(1, N_DEV):
            g = g + x_ref[j].astype(F32)
        d, m2, v2 = _adam_math(w_ref[...], g, m_ref[...], v_ref[...])
        g_ref[...] = g
        d_ref[...] = d
        mo_ref[...] = m2
        vo_ref[...] = v2

    row = pl.BlockSpec((tm, C), lambda i: (i, 0))
    return pl.pallas_call(
        body, name=name, grid=(R // tm,),
        in_specs=[pl.BlockSpec((N_DEV, tm, C), lambda i: (0, i, 0)), row, row, row], out_specs=[row] * 4,
        out_shape=[jax.ShapeDtypeStruct((R, C), F32)] * 4, compiler_params=_cparams(("parallel",)),
    )(land, w, m, v)


def _adamw_small(slab, slab_rows, g_conv_w, ws, ms, vs):
    n = len(ws)

    def body(*refs):
        slab_ref, gc_ref = refs[0], refs[1]
        w_refs, m_refs, v_refs = refs[2:2 + n], refs[2 + n:2 + 2 * n], refs[2 + 2 * n:2 + 3 * n]
        outs = refs[2 + 3 * n:]
        loss_ref = outs[0]
        g_out, d_out, m_out, v_out = (outs[1 + i * n:1 + (i + 1) * n] for i in range(4))
        loss_ref[...] = jnp.sum(slab_ref[pl.ds(6, 1), :], axis=1, keepdims=True)
        for i in range(n):
            g = gc_ref[...] if i == n - 1 else slab_ref[pl.ds(slab_rows[i], 1), pl.ds(0, ws[i].shape[1])]
            d, m2, v2 = _adam_math(w_refs[i][...], g, m_refs[i][...], v_refs[i][...])
            g_out[i][...] = g
            d_out[i][...] = d
            m_out[i][...] = m2
            v_out[i][...] = v2

    vm = pl.BlockSpec(memory_space=pltpu.VMEM)
    shapes = [jax.ShapeDtypeStruct(w.shape, F32) for w in ws]
    outs = pl.pallas_call(
        body, name="adamw_small", in_specs=[vm] * (2 + 3 * n), out_specs=[vm] * (1 + 4 * n),
        out_shape=[jax.ShapeDtypeStruct((1, 1), F32)] + shapes * 4,
    )(slab, g_conv_w, *ws, *ms, *vs)
    return outs[0], outs[1:1 + n], outs[1 + n:1 + 2 * n], outs[1 + 2 * n:1 + 3 * n], outs[1 + 3 * n:]


SMALL = ["norm_mix_pre", "norm_mix_post", "norm_mlp_pre", "norm_mlp_post", "norm_ple_post",
         "conv_b", "ssd_norm_g", "dt_bias", "a_log", "d_skip"]


def _pad_row(v, width=D):
    return jnp.pad(v, ((0, 0), (0, width - v.shape[1])))


def kernel(x, p, positions, norm_mix_pre, norm_mix_post, w_in, conv_w, conv_b, dt_bias, a_log, d_skip, ssd_norm_g, w_out, norm_mlp_pre, norm_mlp_post, w_up, w_down, w_ple_gate, w_ple_proj, norm_ple_post, loss_target, m_norm_mix_pre, m_norm_mix_post, m_w_in, m_conv_w, m_conv_b, m_dt_bias, m_a_log, m_d_skip, m_ssd_norm_g, m_w_out, m_norm_mlp_pre, m_norm_mlp_post, m_w_up, m_w_down, m_w_ple_gate, m_w_ple_proj, m_norm_ple_post, v_norm_mix_pre, v_norm_mix_post, v_w_in, v_conv_w, v_conv_b, v_dt_bias, v_a_log, v_d_skip, v_ssd_norm_g, v_w_out, v_norm_mlp_pre, v_norm_mlp_post, v_w_up, v_w_down, v_w_ple_gate, v_w_ple_proj, v_norm_ple_post):
    args = dict(locals())
    x2, p2, tgt = x[0], p[0, 0], loss_target[0]
    g1, g2, g3, g4, g5 = norm_mix_pre, norm_mix_post, norm_mlp_pre, norm_mlp_post, norm_ple_post

    me = _slot(*_place())
    pack_in = jnp.pad(w_in[0].T, ((0, W_IN_SHARD_PAD - W_IN_SHARD), (0, 0))).astype(BF16)
    rest = [w_out[0].astype(BF16), w_up[0].T.astype(BF16), w_down[0].astype(BF16), w_ple_gate[0].astype(BF16),
            w_ple_proj[0].T.reshape(32, D).astype(BF16)]
    conv_pack = jnp.pad(conv_w[0], ((0, 4), (0, 32)))
    in_handles, tok_in0 = _send_start_many([pack_in, conv_pack], False, "gather_in_start", g1, peers=_near_peers, npeers=4)

    inv_freq = ROPE_THETA ** (-jnp.arange(HD // 2, dtype=F32) * 2.0 / HD)
    pos = positions[0] + tok_in0[0, 0].astype(jnp.int32)
    ang = pos.astype(F32)[:, None] * jnp.tile(inv_freq, 4)
    cos128 = jnp.cos(ang)
    sin128 = jnp.sin(ang) * jnp.tile(jnp.concatenate([-jnp.ones(HD // 2, F32), jnp.ones(HD // 2, F32)]), 2)

    bias_w, alog_w, dsk_w = _pad_row(dt_bias, DT_PAD), _pad_row(a_log, DT_PAD), _pad_row(d_skip, DT_PAD)

    (u1,) = _rowwise(lambda a, g: ((a * _rstd(a) * g,), ()), [x2], [g1], [(D, BF16)], [], tm=512, name="norm_x",
                     deps=[cos128, sin128])
    p2b = p2.astype(BF16)

    in_back, in_land = _send_wait_many(in_handles, u1, "gather_in_wait", npeers=4)
    fw_handles, tok_fw = _forward_start(in_land, "gather_in_forward", u1)
    in_land = _forward_wait(fw_handles, tok_fw, "gather_in_forward_wait")
    gin = lax.dynamic_update_slice(in_land[0], in_back[0][None], (me, 0, 0))
    gconv = lax.dynamic_update_slice(in_land[1], in_back[1][None], (me, 0, 0))
    rest_handles, tok_rest = _send_start_many(rest, False, "gather_rest_start", gconv)
    w_inT = gin[:, :W_IN_SHARD].reshape(IN_W, D)
    w_qkvzT = w_inT[:4 * AW]
    w_xbcdtT = jnp.pad(w_inT[4 * AW:], ((0, DT_PAD - HEADS), (0, 0)))
    conv_full = gconv[:, :CONV_K, :96].transpose(1, 0, 2).reshape(CONV_K, CONV_CH)
    qkvz, xbcdt = _mm_rows(lambda a, b: ((a, b), ()), [(u1, w_qkvzT, True), (u1, w_xbcdtT, True)], [], [],
                           [(4 * AW, F32), (CONV_CH + DT_PAD, F32)], [], tm=512, name="proj_in", deps=[tok_rest])

    qkv = _rope_fwd(qkvz, cos128, sin128)
    qkv = [qkv[3 * i:3 * i + 3] for i in range(len(DILATIONS))]
    outs, lses = [], []
    for d, (qd, kd, vd) in zip(DILATIONS, qkv):
        o, l = _attn_fwd(qd, kd, vd, d)
        outs.append(o)
        lses.append(l)
    attn, lse, attn4, lse4, attn16, lse16 = _attn_merge(outs, lses)

    y_ssd, states, cat, act = _ssd_fwd(xbcdt, conv_full, conv_b, bias_w, alog_w, dsk_w, qkvz, attn, ssd_norm_g)


    rest_back, landed = _send_wait_many(rest_handles, cat, "gather_rest_wait")
    landed = [lax.dynamic_update_slice(l, b[None], (me, 0, 0)) for l, b in zip(landed, rest_back)]
    w_o, w_upT, w_dn, w_gate = landed[0].reshape(D, D), landed[1].reshape(DFF, D), landed[2].reshape(DFF, D), landed[3].reshape(D, D)
    w_projT = landed[4].reshape(D, PLE)

    def post1(mm, xx, ga):
        h = xx + mm * _rstd(mm) * ga
        return (mm, h, _rstd(h)), ()
    mix, h1, r3 = _mm_rows(post1, [(cat, w_o, False)], [x2], [g2], [(D, F32), (D, F32), (1, F32)], [], tm=512,
                           name="mix_out")

    a_up, ff, u2, h2, h2b = _mlp_fwd(h1, r3, g3, w_upT, w_dn, g4)
    relu2 = lambda a: jnp.square(jnp.maximum(a.astype(F32), 0.0))

    def final(gpre, ppv, hh, tg, g):
        sg = _sigmoid(gpre)
        ple = ppv * sg
        r = _rstd(ple)
        n = ple * r
        h3 = hh + n * g
        e = h3 - tg
        dh3 = e * (1.0 / D)
        dple = _rms_bwd(n, r, g, dh3)
        return (dh3, dple * sg, dple * ppv * sg * (1.0 - sg)), (_colsum(dh3 * n), _colsum(0.5 * e * e * (1.0 / D)))
    dh3, dpp, dgp, dg5, loss_vec = _mm_rows(final, [(h2b, w_gate, False), (p2b, w_projT, True)], [h2, tgt], [g5],
                                            [(D, F32), (D, BF16), (D, BF16)], [(1, D), (1, D)], tm=512, name="ple_loss")

    gw_projT = _mm(dpp, p2b, ta=True, tm=512, tn=256, tk=T, out_dtypes=(BF16,), name="gw_ple_proj")
    gw_gate = _mm(h2b, dgp, ta=True, tm=512, tn=1024, tk=T, out_dtypes=(BF16,), name="gw_ple_gate")
    def bwd_mlp_post(dg_, d3, f, g):
        dh2 = d3 + dg_
        r = _rstd(f)
        n = f * r
        return (dh2, _rms_bwd(n, r, g, dh2)), (_colsum(dh2 * n),)
    dh2, dff, dg4 = _mm_rows(bwd_mlp_post, [(dgp, w_gate, True)], [dh3, ff], [g4], [(D, F32), (D, BF16)], [(1, D)],
                             tm=512, name="bwd_ple_gate")

    gw_dn = _mm(a_up, dff, ta=True, tm=512, tn=1024, tk=T, a_pre=relu2, out_dtypes=(BF16,), name="gw_mlp_down")
    rs_a, tok_a = _send_start_many([gw_projT.reshape(N_DEV, 32, D), gw_gate.reshape(N_DEV, 128, D),
                                    gw_dn.reshape(N_DEV, 512, D)], True, "rs_start_a", g1)
    da_up, du2 = _mlp_dx(dff, a_up, w_upT, w_dn, tok_a)
    gw_upT = _mm(da_up, u2, ta=True, tm=512, tn=1024, tk=T, out_dtypes=(BF16,), name="gw_mlp_up")

    def bwd_mix_post(d2, du, hh, rr, mm, ga, gb):
        n3 = hh * rr
        dh1 = d2 + _rms_bwd(n3, rr, gb, du)
        r = _rstd(mm)
        n2 = mm * r
        return (dh1, _rms_bwd(n2, r, ga, dh1)), (_colsum(du * n3), _colsum(dh1 * n2))
    dh1, dmix, dg3, dg2 = _rowwise(bwd_mix_post, [dh2, du2, h1, r3, mix], [g2, g3], [(D, F32), (D, BF16)],
                                   [(1, D), (1, D)], tm=512, name="bwd_post_mix")

    gw_o = _mm(cat, dmix, ta=True, tm=512, tn=1024, tk=T, out_dtypes=(BF16,), name="gw_out")
    rs_b, tok_b = _send_start_many([gw_upT.reshape(N_DEV, 512, D), gw_o.reshape(N_DEV, 128, D)], True, "rs_start_b", g1)
    dcat, dattn4, dattn16 = _dx_out(dmix, w_o, tok_b)

    dact, ddtw, ssd_par, dz, dgs = _ssd_bwd(act, xbcdt, bias_w, alog_w, dsk_w, states, y_ssd, qkvz, dcat, ssd_norm_g)
    dxbcdt, conv_par = _conv_bwd(xbcdt, dact, ddtw, conv_full, conv_b)

    qkv_grads = [_attn_bwd(*qkv[0], dcat, attn, lse, 1),
                 _attn_bwd(*qkv[1], dattn4, attn4, lse4, 4),
                 _attn_bwd(*qkv[2], dattn16, attn16, lse16, 16)]
    dqkvz = _rope_bwd(qkv_grads, dz, cos128, sin128)

    gw_qkvzT = _mm(dqkvz, u1, ta=True, tm=512, tn=1024, tk=T, out_dtypes=(BF16,), name="gw_qkvz")
    gw_xbcdtT = _mm(dxbcdt, u1, ta=True, tm=896, tn=1024, tk=T, out_dtypes=(BF16,), name="gw_xbcdt")
    gw_inT = jnp.concatenate([gw_qkvzT, gw_xbcdtT], axis=0)[:IN_W]
    gw_inT = jnp.pad(gw_inT.reshape(N_DEV, W_IN_SHARD, D), ((0, 0), (0, W_IN_SHARD_PAD - W_IN_SHARD), (0, 0)))
    rs_in, tok_in = _send_start(gw_inT, True, "rs_start_w_in", g1)

    def bwd_in(ua, ub, d1, xx, g):
        rr = _rstd(xx)
        n = xx * rr
        du = ua + ub
        return (d1 + _rms_bwd(n, rr, g, du),), (_colsum(du * n),)
    grad_x, dg1 = _mm_rows(bwd_in, [(dqkvz, w_qkvzT, False), (dxbcdt, w_xbcdtT, False)], [dh1, x2], [g1],
                           [(D, F32)], [(1, D)], tm=512, name="bwd_in_proj", deps=[tok_in])

    my_slab = _slab_pack([(dg1, 0), (dg2, 1), (dg3, 2), (dg4, 3), (dg5, 4), (dgs, 5), (loss_vec, 6),
                          (conv_par, 8), (ssd_par, 16)], "slab_pack")
    slab_handles, tok_slab = _send_start_many([my_slab], False, "slab_start", g1)

    def scatter_finish(handles, nm, after):
        part, land = _send_wait(handles, after, "rs_wait_" + nm)
        own = lax.dynamic_slice(part, (me, 0, 0), (1,) + part.shape[1:])
        return _sum_slots(lax.dynamic_update_slice(land, own, (me, 0, 0)), "rs_sum_" + nm)
    def landed(handles, after, wait_name):
        parts, lands = _send_wait_many(handles, after, wait_name)
        return [lax.dynamic_update_slice(land, lax.dynamic_slice(part, (me, 0, 0), (1,) + part.shape[1:]), (me, 0, 0))
                for part, land in zip(parts, lands)]
    land_proj, land_gate, land_dn = landed(rs_a, tok_slab, "rs_wait_a")
    land_up, land_out = landed(rs_b, tok_slab, "rs_wait_b")

    grads, delta, new_m, new_v = {}, {}, {}, {}
    for nme, land in (("w_down", land_dn), ("w_out", land_out), ("w_ple_gate", land_gate)):
        outs4 = _adamw_slots(land, args[nme][0], args["m_" + nme][0], args["v_" + nme][0], "adamw_" + nme)
        grads[nme], delta[nme], new_m[nme], new_v[nme] = [t[None] for t in outs4]
    grads["w_up"] = _sum_slots(land_up, "rs_sum_w_up").T[None]
    grads["w_ple_proj"] = _sum_slots(land_proj, "rs_sum_w_proj").reshape(128, PLE).T[None]
    for nme in ["w_up", "w_ple_proj", "w_in"]:
        if nme == "w_in":
            g_inT = scatter_finish(rs_in, "w_in", delta["w_down"])
            grads["w_in"] = g_inT[:W_IN_SHARD].T[None]
        dl, mm_, vv_ = _adamw(args[nme][0], grads[nme][0], args["m_" + nme][0], args["v_" + nme][0], "adamw_" + nme)
        delta[nme], new_m[nme], new_v[nme] = dl[None], mm_[None], vv_[None]

    slab_back, slab_land = _send_wait_many(slab_handles, delta["w_in"], "slab_wait")
    slab = _sum_slots(lax.dynamic_update_slice(slab_land[0], slab_back[0][None], (me, 0, 0)), "slab_sum")
    g_conv_w = lax.dynamic_slice(slab[8:12, :CONV_CH], (0, me * 96), (CONV_K, 96))
    small_names = SMALL + ["conv_w"]
    small_rows = [0, 1, 2, 3, 4, 12, 5, 16, 17, 18, None]
    pick = lambda prefix: [args[prefix + nme] for nme in SMALL] + [args[prefix + "conv_w"][0]]
    loss11, g_s, d_s, m_s, v_s = _adamw_small(slab, small_rows, g_conv_w, pick(""), pick("m_"), pick("v_"))
    loss = loss11[0, 0]
    for i, nme in enumerate(small_names):
        lead = (lambda t: t[None]) if nme == "conv_w" else (lambda t: t)
        grads[nme], delta[nme], new_m[nme], new_v[nme] = lead(g_s[i]), lead(d_s[i]), lead(m_s[i]), lead(v_s[i])

    order = ["norm_mix_pre", "norm_mix_post", "w_in", "conv_w", "conv_b", "dt_bias", "a_log", "d_skip", "ssd_norm_g",
             "w_out", "norm_mlp_pre", "norm_mlp_post", "w_up", "w_down", "w_ple_gate", "w_ple_proj", "norm_ple_post"]
    return (loss, grad_x[None], *[grads[n] for n in order], *[delta[n] for n in order],
            *[new_m[n] for n in order], *[new_v[n] for n in order])
```

```python
import jax
import jax.numpy as jnp
from jax import lax
from jax.experimental import pallas as pl
from jax.experimental.pallas import tpu as pltpu

F32 = jnp.float32
BF16 = jnp.bfloat16
MESH = pl.DeviceIdType.MESH
HIGHEST = lax.Precision.HIGHEST

N_DEV = 8
T = 4096
D = 1024
HEADS = 8
HD = 64
AW = 512
NS = 128
CONV_K = 4
CONV_CH = 768
CHUNK = 128
SSD_PER = 2
DFF = 4096
PLE = 256
EPS = 1e-6
ROPE_THETA = 10000.0
DILATIONS = (1, 4, 16)
QBLK = 128
NEG = -1e30
IN_W = 2824
W_IN_SHARD = 353
W_IN_SHARD_PAD = 384
DT_PAD = 128

ADAM_LR, ADAM_B1, ADAM_B2, ADAM_EPS, ADAM_WD, ADAM_STEP = 0.001, 0.9, 0.999, 1e-08, 0.01, 10

VMEM_LIMIT = 56 * 1024 * 1024


_ANY = pl.BlockSpec(memory_space=pl.ANY)


def _cparams(sem=None):
    return pltpu.CompilerParams(dimension_semantics=sem, vmem_limit_bytes=VMEM_LIMIT)


def _dot(a, b, ca, cb, precision=None):
    return lax.dot_general(a, b, (((ca,), (cb,)), ((), ())), preferred_element_type=F32, precision=precision)


def _nn(a, b):
    return _dot(a, b, 1, 0)


def _nt(a, b):
    return _dot(a, b, 1, 1)


def _tn(a, b):
    return _dot(a, b, 0, 0)


def _sigmoid(x):
    return 1.0 / (1.0 + jnp.exp(-x))


def _softplus(x):
    return jnp.maximum(x, 0.0) + jnp.log(1.0 + jnp.exp(-jnp.abs(x)))


def _mm(a, b, *, ta=False, tb=False, tm, tn, tk, name,
        a_pre=None, a_rows=(), a_cols=(), b_pre=None, b_rows=(), b_cols=(),
        epi=None, epi_tiles=(), out_dtypes=(F32,), deps=()):
    if ta:
        K, M = a.shape
    else:
        M, K = a.shape
    if tb:
        N, K2 = b.shape
    else:
        K2, N = b.shape
    assert K == K2 and M % tm == 0 and N % tn == 0 and K % tk == 0, (name, a.shape, b.shape)
    nk = K // tk
    if ta:
        a_spec = pl.BlockSpec((tk, tm), lambda i, j, k: (k, i))
        a_row_specs = [pl.BlockSpec((tk, 1), lambda i, j, k: (k, 0)) for _ in a_rows]
        a_col_specs = [pl.BlockSpec((1, tm), lambda i, j, k: (0, i)) for _ in a_cols]
    else:
        a_spec = pl.BlockSpec((tm, tk), lambda i, j, k: (i, k))
        a_row_specs = [pl.BlockSpec((tm, 1), lambda i, j, k: (i, 0)) for _ in a_rows]
        a_col_specs = [pl.BlockSpec((1, tk), lambda i, j, k: (0, k)) for _ in a_cols]
    if tb:
        b_spec = pl.BlockSpec((tn, tk), lambda i, j, k: (j, k))
        b_row_specs = [pl.BlockSpec((tn, 1), lambda i, j, k: (j, 0)) for _ in b_rows]
        b_col_specs = [pl.BlockSpec((1, tk), lambda i, j, k: (0, k)) for _ in b_cols]
    else:
        b_spec = pl.BlockSpec((tk, tn), lambda i, j, k: (k, j))
        b_row_specs = [pl.BlockSpec((tk, 1), lambda i, j, k: (k, 0)) for _ in b_rows]
        b_col_specs = [pl.BlockSpec((1, tn), lambda i, j, k: (0, j)) for _ in b_cols]
    o_spec = pl.BlockSpec((tm, tn), lambda i, j, k: (i, j))
    na, nb, ne, no = len(a_rows) + len(a_cols), len(b_rows) + len(b_cols), len(epi_tiles), len(out_dtypes)

    def body(*refs):
        a_ref, b_ref = refs[0], refs[1]
        a_ex = refs[2:2 + na]
        b_ex = refs[2 + na:2 + na + nb]
        e_ex = refs[2 + na + nb:2 + na + nb + ne]
        first_out = 2 + na + nb + ne + len(deps)
        outs = refs[first_out:first_out + no]

        def finish(res):
            vals = epi(res, *[r[...] for r in e_ex]) if epi is not None else (res,)
            for o_ref, val in zip(outs, vals):
                o_ref[...] = val.astype(o_ref.dtype)

        at = a_ref[...]
        if a_pre is not None:
            at = a_pre(at, *[r[...] for r in a_ex])
        bt = b_ref[...]
        if b_pre is not None:
            bt = b_pre(bt, *[r[...] for r in b_ex])
        prod = _dot(at.astype(BF16), bt.astype(BF16), 0 if ta else 1, 1 if tb else 0)
        if nk == 1:
            finish(prod)
            return
        acc = refs[-1]
        k = pl.program_id(2)

        @pl.when(k == 0)
        def _():
            acc[...] = jnp.zeros_like(acc)
        acc[...] += prod

        @pl.when(k == nk - 1)
        def _():
            finish(acc[...])

    outs = pl.pallas_call(
        body, name=name,
        grid=(M // tm, N // tn, nk),
        in_specs=([a_spec, b_spec] + a_row_specs + a_col_specs + b_row_specs + b_col_specs + [o_spec] * ne
                  + [_ANY] * len(deps)),
        out_specs=[o_spec] * no,
        out_shape=[jax.ShapeDtypeStruct((M, N), dt) for dt in out_dtypes],
        scratch_shapes=[pltpu.VMEM((tm, tn), F32)] if nk > 1 else [],
        compiler_params=_cparams(("parallel", "parallel", "arbitrary")),
    )(a, b, *a_rows, *a_cols, *b_rows, *b_cols, *epi_tiles, *deps)
    return outs[0] if no == 1 else outs


MLP_TM = 1024
MLP_TC = 512


def _mlp_fwd(h, r, g, w_upT, w_dn, g_post):
    nc = DFF // MLP_TC

    def body(h_ref, r_ref, g_ref, wu_ref, wd_ref, gp_ref, a_ref, ff_ref, u_ref, ho_ref, hob_ref, acc, u_scr):
        c = pl.program_id(1)

        @pl.when(c == 0)
        def _():
            u = (h_ref[...] * r_ref[...] * g_ref[...]).astype(BF16)
            u_scr[...] = u
            u_ref[...] = u
            acc[...] = jnp.zeros_like(acc)
        a = _nt(u_scr[...], wu_ref[...])
        a_ref[...] = a.astype(BF16)
        acc[...] += _nn(jnp.square(jnp.maximum(a, 0.0)).astype(BF16), wd_ref[...])

        @pl.when(c == nc - 1)
        def _():
            f = acc[...]
            ff_ref[...] = f
            ho = h_ref[...] + f * _rstd(f) * gp_ref[...]
            ho_ref[...] = ho
            hob_ref[...] = ho.astype(BF16)

    row = pl.BlockSpec((MLP_TM, D), lambda i, c: (i, 0))
    wsp = pl.BlockSpec((MLP_TC, D), lambda i, c: (c, 0))
    vec = pl.BlockSpec((1, D), lambda i, c: (0, 0))
    return pl.pallas_call(
        body, name="mlp_fwd", grid=(T // MLP_TM, nc),
        in_specs=[row, pl.BlockSpec((MLP_TM, 1), lambda i, c: (i, 0)), vec, wsp, wsp, vec],
        out_specs=[pl.BlockSpec((MLP_TM, MLP_TC), lambda i, c: (i, c)), row, row, row, row],
        out_shape=[jax.ShapeDtypeStruct((T, DFF), BF16), jax.ShapeDtypeStruct((T, D), F32), jax.ShapeDtypeStruct((T, D), BF16),
                   jax.ShapeDtypeStruct((T, D), F32), jax.ShapeDtypeStruct((T, D), BF16)],
        scratch_shapes=[pltpu.VMEM((MLP_TM, D), F32), pltpu.VMEM((MLP_TM, D), BF16)],
        compiler_params=_cparams(("parallel", "arbitrary")),
    )(h, r, g, w_upT, w_dn, g_post)


def _mlp_dx(dff, a, w_upT, w_dn, dep):
    nc = DFF // MLP_TC

    def body(d_ref, a_ref, wu_ref, wd_ref, dep_ref, da_ref, du_ref, acc, d_scr):
        c = pl.program_id(1)

        @pl.when(c == 0)
        def _():
            d_scr[...] = d_ref[...].astype(BF16)
            acc[...] = jnp.zeros_like(acc)
        da = (_nt(d_scr[...], wd_ref[...]) * (2.0 * jnp.maximum(a_ref[...].astype(F32), 0.0))).astype(BF16)
        da_ref[...] = da
        acc[...] += _nn(da, wu_ref[...])

        @pl.when(c == nc - 1)
        def _():
            du_ref[...] = acc[...]

    row = pl.BlockSpec((MLP_TM, D), lambda i, c: (i, 0))
    wsp = pl.BlockSpec((MLP_TC, D), lambda i, c: (c, 0))
    chunk = pl.BlockSpec((MLP_TM, MLP_TC), lambda i, c: (i, c))
    return pl.pallas_call(
        body, name="mlp_dx", grid=(T // MLP_TM, nc),
        in_specs=[row, chunk, wsp, wsp, _ANY], out_specs=[chunk, row],
        out_shape=[jax.ShapeDtypeStruct((T, DFF), BF16), jax.ShapeDtypeStruct((T, D), F32)],
        scratch_shapes=[pltpu.VMEM((MLP_TM, D), F32), pltpu.VMEM((MLP_TM, D), BF16)],
        compiler_params=_cparams(("parallel", "arbitrary")),
    )(dff, a, w_upT, w_dn, dep)


def _rowwise(fn, rows, vecs, out_rows, out_sums, *, tm, name, deps=()):
    specs, arrs = [], []
    R = None
    for r in rows:
        if isinstance(r, tuple):
            arr, width, cb = r
            specs.append(pl.BlockSpec((tm, width), lambda i, cb=cb: (i, cb)))
        else:
            arr = r
            specs.append(pl.BlockSpec((tm, arr.shape[1]), lambda i: (i, 0)))
        R = arr.shape[0] if R is None else R
        assert arr.shape[0] == R, name
        arrs.append(arr)
    assert R % tm == 0, name
    for v in vecs:
        specs.append(pl.BlockSpec(v.shape, lambda i: (0, 0)))
        arrs.append(v)
    nr, nv, no, ns = len(rows), len(vecs), len(out_rows), len(out_sums)
    out_specs = [pl.BlockSpec((tm, w), lambda i: (i, 0)) for w, _ in out_rows]
    out_specs += [pl.BlockSpec(s, lambda i: (0, 0)) for s in out_sums]
    out_shape = [jax.ShapeDtypeStruct((R, w), dt) for w, dt in out_rows]
    out_shape += [jax.ShapeDtypeStruct(s, F32) for s in out_sums]

    nd = len(deps)

    def body(*refs):
        ins = [r[...] for r in refs[:nr + nv]]
        o_refs = refs[nr + nv + nd:nr + nv + nd + no]
        s_refs = refs[nr + nv + nd + no:]
        o_vals, s_vals = fn(*ins)
        for ref, val in zip(o_refs, o_vals):
            ref[...] = val.astype(ref.dtype)
        if ns:
            @pl.when(pl.program_id(0) == 0)
            def _():
                for ref in s_refs:
                    ref[...] = jnp.zeros_like(ref)
            for ref, val in zip(s_refs, s_vals):
                ref[...] += val

    outs = pl.pallas_call(
        body, name=name, grid=(R // tm,), in_specs=specs + [_ANY] * nd, out_specs=out_specs, out_shape=out_shape,
        compiler_params=_cparams(("arbitrary",) if ns else ("parallel",)),
    )(*arrs, *deps)
    return outs


def _mm_rows(fn, mats, rows, vecs, out_rows, out_sums, *, tm, name, deps=()):
    R = mats[0][0].shape[0]
    assert R % tm == 0, name
    specs, arrs = [], []
    for a, b, tb in mats:
        specs += [pl.BlockSpec((tm, a.shape[1]), lambda i: (i, 0)), pl.BlockSpec(b.shape, lambda i: (0, 0))]
        arrs += [a, b]
    for r in rows:
        specs.append(pl.BlockSpec((tm, r.shape[1]), lambda i: (i, 0)))
        arrs.append(r)
    for v in vecs:
        specs.append(pl.BlockSpec(v.shape, lambda i: (0, 0)))
        arrs.append(v)
    nm, nr, nv, nd, no, ns = len(mats), len(rows), len(vecs), len(deps), len(out_rows), len(out_sums)
    out_specs = [pl.BlockSpec((tm, w), lambda i: (i, 0)) for w, _ in out_rows]
    out_specs += [pl.BlockSpec(s, lambda i: (0, 0)) for s in out_sums]
    out_shape = [jax.ShapeDtypeStruct((R, w), dt) for w, dt in out_rows] + [jax.ShapeDtypeStruct(s, F32) for s in out_sums]

    def body(*refs):
        prods = [_dot(refs[2 * p][...].astype(BF16), refs[2 * p + 1][...].astype(BF16), 1, 1 if mats[p][2] else 0)
                 for p in range(nm)]
        ins = [r[...] for r in refs[2 * nm:2 * nm + nr + nv]]
        first_out = 2 * nm + nr + nv + nd
        o_refs, s_refs = refs[first_out:first_out + no], refs[first_out + no:]
        o_vals, s_vals = fn(*prods, *ins)
        for ref, val in zip(o_refs, o_vals):
            ref[...] = val.astype(ref.dtype)
        if ns:
            @pl.when(pl.program_id(0) == 0)
            def _():
                for ref in s_refs:
                    ref[...] = jnp.zeros_like(ref)
            for ref, val in zip(s_refs, s_vals):
                ref[...] += val

    return pl.pallas_call(
        body, name=name, grid=(R // tm,), in_specs=specs + [_ANY] * nd, out_specs=out_specs, out_shape=out_shape,
        compiler_params=_cparams(("arbitrary",) if ns else ("parallel",)),
    )(*arrs, *deps)


def _colsum(x):
    return jnp.sum(x, axis=0, keepdims=True)


def _rstd(x):
    return lax.rsqrt(jnp.mean(x * x, axis=-1, keepdims=True) + EPS)


def _rms_bwd(xn, r, g, dy):
    dn = dy * g
    return r * (dn - xn * jnp.mean(dn * xn, axis=-1, keepdims=True))


def _partner(t):
    lane = lax.broadcasted_iota(jnp.int32, t.shape, 1)
    up = pltpu.roll(t, 96, 1)
    down = pltpu.roll(t, 32, 1)
    return jnp.where((lane % 64) < 32, up, down)


SLABS = AW // 128


def _rows(r, n, d):
    return pl.ds(r, n, stride=d) if d > 1 else pl.ds(0, n)


def _undilate(src_ref, dst, d, tm):
    for r in range(d):
        for j in range(SLABS):
            dst[j][_rows(r, tm // d, d), :] = src_ref[:, pl.ds(r * AW + j * 128, 128)].astype(dst[j].dtype)


def _dilate(dst_ref, src, d, tm):
    for r in range(d):
        for j in range(SLABS):
            dst_ref[:, pl.ds(r * AW + j * 128, 128)] = src[j][_rows(r, tm // d, d), :].astype(dst_ref.dtype)


def _slab_scratch(n, tm):
    return [pltpu.VMEM((tm, 128), F32)] * (SLABS * n)


def _slab_groups(flat):
    return [flat[SLABS * i:SLABS * (i + 1)] for i in range(len(flat) // SLABS)]


def _slab_specs(tm, first):
    return [pl.BlockSpec((tm, 128), lambda i, j=j: (i, first + j)) for j in range(SLABS)]


def _dil_spec(tm, d):
    return pl.BlockSpec((tm // d, d * AW), lambda i: (i, 0))


ROPE_TM = 512


def _rope_fwd(qkvz, cos128, sin128):
    tm = ROPE_TM

    def body(*refs):
        q_refs, k_refs, v_refs = refs[0:4], refs[4:8], refs[8:12]
        c_ref, s_ref = refs[12], refs[13]
        outs = refs[14:23]
        qs, ks = _slab_groups(refs[23:])
        c, s = c_ref[...], s_ref[...]
        for j in range(SLABS):
            q, k = q_refs[j][...], k_refs[j][...]
            qs[j][...] = (q * c + _partner(q) * s) * (HD ** -0.5)
            ks[j][...] = k * c + _partner(k) * s
        for di, d in enumerate(DILATIONS):
            oq, ok, ov = outs[3 * di:3 * di + 3]
            for r in range(d):
                rows = _rows(r, tm // d, d)
                for j in range(SLABS):
                    cols = pl.ds(r * AW + j * 128, 128)
                    oq[:, cols] = qs[j][rows, :].astype(BF16)
                    ok[:, cols] = ks[j][rows, :].astype(BF16)
                    ov[:, cols] = v_refs[j][rows, :].astype(BF16)

    tab = pl.BlockSpec((tm, 128), lambda i: (i, 0))
    out_specs, out_shape = [], []
    for d in DILATIONS:
        out_specs += [_dil_spec(tm, d)] * 3
        out_shape += [jax.ShapeDtypeStruct((T // d, d * AW), BF16)] * 3
    return pl.pallas_call(
        body, name="rope_fwd", grid=(T // tm,),
        in_specs=_slab_specs(tm, 0) + _slab_specs(tm, 4) + _slab_specs(tm, 8) + [tab, tab],
        out_specs=out_specs, out_shape=out_shape, scratch_shapes=_slab_scratch(2, tm),
        compiler_params=_cparams(("parallel",)),
    )(*([qkvz] * 12), cos128, sin128)


def _rope_bwd(grads, dz, cos128, sin128):
    tm = ROPE_TM

    def body(*refs):
        g_refs = refs[0:9]
        dz_ref, c_ref, s_ref, o_ref = refs[9], refs[10], refs[11], refs[12]
        scr = _slab_groups(refs[13:])
        for di, d in enumerate(DILATIONS[1:]):
            for t in range(3):
                _undilate(g_refs[3 * (di + 1) + t], scr[3 * di + t], d, tm)
        c, s = c_ref[...], s_ref[...]
        for j in range(SLABS):
            cols = pl.ds(j * 128, 128)
            tot = [g_refs[t][:, cols] + scr[t][j][...] + scr[3 + t][j][...] for t in range(3)]
            dqr = tot[0] * (HD ** -0.5)
            o_ref[:, pl.ds(j * 128, 128)] = (dqr * c + _partner(dqr * s)).astype(BF16)
            o_ref[:, pl.ds(AW + j * 128, 128)] = (tot[1] * c + _partner(tot[1] * s)).astype(BF16)
            o_ref[:, pl.ds(2 * AW + j * 128, 128)] = tot[2].astype(BF16)
        o_ref[:, pl.ds(3 * AW, AW)] = dz_ref[...].astype(BF16)

    tab = pl.BlockSpec((tm, 128), lambda i: (i, 0))
    in_specs, args = [], []
    for d, g in zip(DILATIONS, grads):
        in_specs += [_dil_spec(tm, d)] * 3
        args += list(g)
    return pl.pallas_call(
        body, name="rope_bwd", grid=(T // tm,),
        in_specs=in_specs + [pl.BlockSpec((tm, AW), lambda i: (i, 0)), tab, tab],
        out_specs=pl.BlockSpec((tm, 4 * AW), lambda i: (i, 0)),
        out_shape=jax.ShapeDtypeStruct((T, 4 * AW), BF16),
        scratch_shapes=_slab_scratch(6, tm),
        compiler_params=_cparams(("parallel",)),
    )(*args, dz, cos128, sin128)


def _dx_out(dmix, w_o, dep):
    tm = ROPE_TM

    def body(a_ref, w_ref, dep_ref, dcat_ref, o4, o16, *slabs):
        prod = _nt(a_ref[...].astype(BF16), w_ref[...].astype(BF16))
        dcat_ref[...] = prod
        for j in range(SLABS):
            slabs[j][...] = prod[:, 128 * j:128 * (j + 1)]
        _dilate(o4, slabs, 4, tm)
        _dilate(o16, slabs, 16, tm)

    return pl.pallas_call(
        body, name="dx_out", grid=(T // tm,),
        in_specs=[pl.BlockSpec((tm, D), lambda i: (i, 0)), pl.BlockSpec((D, D), lambda i: (0, 0)), _ANY],
        out_specs=[pl.BlockSpec((tm, D), lambda i: (i, 0)), _dil_spec(tm, 4), _dil_spec(tm, 16)],
        out_shape=[jax.ShapeDtypeStruct((T, D), F32), jax.ShapeDtypeStruct((T // 4, 4 * AW), BF16),
                   jax.ShapeDtypeStruct((T // 16, 16 * AW), BF16)],
        scratch_shapes=_slab_scratch(1, tm), compiler_params=_cparams(("parallel",)),
    )(dmix, w_o, dep)


def _band_masks():
    qi = lax.broadcasted_iota(jnp.int32, (QBLK, QBLK), 0)
    kj = lax.broadcasted_iota(jnp.int32, (QBLK, QBLK), 1)
    return kj >= qi, kj <= qi


def _attn_fwd(q, k, v, d):
    L = q.shape[0]
    npair = L // (2 * QBLK)

    def body(q_ref, kp_ref, kc_ref, vp_ref, vc_ref, o_ref, l_ref):
        pair = pl.program_id(1)
        mask_p, mask_c = _band_masks()
        for sub in range(2):
            rows = pl.ds(sub * QBLK, QBLK)
            first = jnp.where(pair > 0, 0.0, NEG) if sub == 0 else 0.0
            bias = jnp.concatenate([jnp.where(mask_p, 0.0, NEG) + first, jnp.where(mask_c, 0.0, NEG)], axis=1)
            k_prev = (lambda sl: kp_ref[:, sl]) if sub == 0 else (lambda sl: kc_ref[pl.ds(0, QBLK), sl])
            v_prev = (lambda sl: vp_ref[:, sl]) if sub == 0 else (lambda sl: vc_ref[pl.ds(0, QBLK), sl])
            s = []
            for h in range(HEADS):
                sl = pl.ds(HD * h, HD)
                qh = q_ref[rows, sl]
                s.append(jnp.concatenate([_nt(qh, k_prev(sl)), _nt(qh, kc_ref[rows, sl])], axis=1))
            s = jnp.stack(s) + bias
            m = jnp.max(s, axis=2, keepdims=True)
            e = jnp.exp(s - m)
            den = jnp.sum(e, axis=2, keepdims=True)
            p = e.astype(BF16)
            inv = 1.0 / den
            lse = m + jnp.log(den)
            for h in range(HEADS):
                sl = pl.ds(HD * h, HD)
                o_ref[rows, sl] = ((_nn(p[h, :, :QBLK], v_prev(sl)) + _nn(p[h, :, QBLK:], vc_ref[rows, sl])) * inv[h]
                                   ).astype(BF16)
                l_ref[rows, sl] = jnp.broadcast_to(lse[h], (QBLK, HD))

    cur = pl.BlockSpec((2 * QBLK, AW), lambda r, n: (n, r))
    prev = pl.BlockSpec((QBLK, AW), lambda r, n: (jnp.maximum(2 * n - 1, 0), r))
    return pl.pallas_call(
        body, name=f"attn_fwd_d{d}", grid=(d, npair),
        in_specs=[cur, prev, cur, prev, cur], out_specs=[cur, cur],
        out_shape=[jax.ShapeDtypeStruct((L, d * AW), BF16), jax.ShapeDtypeStruct((L, d * AW), F32)],
        compiler_params=_cparams(("parallel", "parallel")),
    )(q, k, k, v, v)


def _attn_bwd(q, k, v, do, at, lse, d):
    L = q.shape[0]
    nb = L // QBLK
    npair = nb // 2

    def body(qc_ref, qn_ref, kp_ref, kc_ref, vp_ref, vc_ref, doc_ref, don_ref, atc_ref, atn_ref,
             lc_ref, ln_ref, dq_ref, dk_ref, dv_ref):
        pair = pl.program_id(1)
        mask_p, mask_c = _band_masks()
        prev_bias = jnp.where(mask_p, 0.0, NEG)
        for sub in range(2):
            rows = pl.ds(sub * QBLK, QBLK)
            second = pl.ds(QBLK, QBLK)
            if sub == 0:
                take = lambda cur_ref, nxt_ref, cols, i: cur_ref[rows if i == 0 else second, cols]
                prev_of = lambda p_ref, c_ref, cols: p_ref[:, cols]
                first, last = jnp.where(pair > 0, 0.0, NEG), 0.0
            else:
                take = lambda cur_ref, nxt_ref, cols, i: cur_ref[rows, cols] if i == 0 else nxt_ref[:, cols]
                prev_of = lambda p_ref, c_ref, cols: c_ref[pl.ds(0, QBLK), cols]
                first, last = 0.0, jnp.where(pair < npair - 1, 0.0, NEG)
            bias = jnp.concatenate([prev_bias + first, jnp.where(mask_c, 0.0, NEG), prev_bias + last], axis=1)
            s, dp, ls, dl, ops = [], [], [], [], []
            for h in range(HEADS):
                sl = pl.ds(HD * h, HD)
                one = pl.ds(HD * h, 1)
                q0, q1 = take(qc_ref, qn_ref, sl, 0), take(qc_ref, qn_ref, sl, 1)
                kp, kc = prev_of(kp_ref, kc_ref, sl), kc_ref[rows, sl]
                vp, vc = prev_of(vp_ref, vc_ref, sl), vc_ref[rows, sl]
                do0, do1 = take(doc_ref, don_ref, sl, 0), take(doc_ref, don_ref, sl, 1)
                do0b, do1b = do0.astype(BF16), do1.astype(BF16)
                s.append(jnp.concatenate([_nt(q0, kp), _nt(q0, kc), _nt(q1, kc)], axis=1))
                dp.append(jnp.concatenate([_nt(do0b, vp), _nt(do0b, vc), _nt(do1b, vc)], axis=1))
                dl0 = jnp.sum(do0.astype(F32) * take(atc_ref, atn_ref, sl, 0).astype(F32), axis=1, keepdims=True)
                dl1 = jnp.sum(do1.astype(F32) * take(atc_ref, atn_ref, sl, 1).astype(F32), axis=1, keepdims=True)
                dl.append(jnp.concatenate([jnp.broadcast_to(dl0, (QBLK, 2 * QBLK)), jnp.broadcast_to(dl1, (QBLK, QBLK))], axis=1))
                ls.append(jnp.concatenate([jnp.broadcast_to(take(lc_ref, ln_ref, one, 0), (QBLK, 2 * QBLK)),
                                           jnp.broadcast_to(take(lc_ref, ln_ref, one, 1), (QBLK, QBLK))], axis=1))
                ops.append((q0, q1, kp, kc, do0b, do1b))
            p = jnp.exp(jnp.stack(s) + bias - jnp.stack(ls))
            ds = (p * (jnp.stack(dp) - jnp.stack(dl))).astype(BF16)
            p = p.astype(BF16)
            for h in range(HEADS):
                sl = pl.ds(HD * h, HD)
                q0, q1, kp, kc, do0b, do1b = ops[h]
                dq_ref[rows, sl] = (_nn(ds[h, :, :QBLK], kp) + _nn(ds[h, :, QBLK:2 * QBLK], kc)).astype(BF16)
                dv_ref[rows, sl] = (_tn(p[h, :, QBLK:2 * QBLK], do0b) + _tn(p[h, :, 2 * QBLK:], do1b)).astype(BF16)
                dk_ref[rows, sl] = (_tn(ds[h, :, QBLK:2 * QBLK], q0) + _tn(ds[h, :, 2 * QBLK:], q1)).astype(BF16)

    cur = pl.BlockSpec((2 * QBLK, AW), lambda r, n: (n, r))
    prev = pl.BlockSpec((QBLK, AW), lambda r, n: (jnp.maximum(2 * n - 1, 0), r))
    nxt = pl.BlockSpec((QBLK, AW), lambda r, n: (jnp.minimum(2 * n + 2, nb - 1), r))
    return pl.pallas_call(
        body, name=f"attn_bwd_d{d}", grid=(d, npair),
        in_specs=[cur, nxt, prev, cur, prev, cur, cur, nxt, cur, nxt, cur, nxt], out_specs=[cur, cur, cur],
        out_shape=[jax.ShapeDtypeStruct((L, d * AW), BF16)] * 3,
        compiler_params=_cparams(("parallel", "parallel")),
    )(q, q, k, k, v, v, do, do, at, at, lse, lse)


def _attn_merge(outs, lses):
    tm = ROPE_TM

    def body(o1, o4, o16, l1, l4, l16, at_ref, ls_ref, at4, ls4, at16, ls16, *flat):
        so4, so16, sl4, sl16, sa, sl = _slab_groups(flat)
        _undilate(o4, so4, 4, tm)
        _undilate(o16, so16, 16, tm)
        _undilate(l4, sl4, 4, tm)
        _undilate(l16, sl16, 16, tm)
        for j in range(SLABS):
            cols = pl.ds(j * 128, 128)
            a, b, c = l1[:, cols], sl4[j][...], sl16[j][...]
            m = jnp.maximum(jnp.maximum(a, b), c)
            e1, e2, e3 = jnp.exp(a - m), jnp.exp(b - m), jnp.exp(c - m)
            s = e1 + e2 + e3
            inv = 1.0 / s
            attn = (e1 * inv) * o1[:, cols] + (e2 * inv) * so4[j][...] + (e3 * inv) * so16[j][...]
            lse = m + jnp.log(s)
            at_ref[:, cols] = attn
            ls_ref[:, cols] = lse
            sa[j][...] = attn
            sl[j][...] = lse
        _dilate(at4, sa, 4, tm)
        _dilate(at16, sa, 16, tm)
        _dilate(ls4, sl, 4, tm)
        _dilate(ls16, sl, 16, tm)

    specs = [_dil_spec(tm, d) for d in DILATIONS]
    tok = specs[0]
    return pl.pallas_call(
        body, name="attn_merge", grid=(T // tm,),
        in_specs=specs + specs, out_specs=[tok, tok, specs[1], specs[1], specs[2], specs[2]],
        out_shape=[jax.ShapeDtypeStruct((T, AW), F32)] * 2
        + [jax.ShapeDtypeStruct((T // 4, 4 * AW), BF16), jax.ShapeDtypeStruct((T // 4, 4 * AW), F32),
           jax.ShapeDtypeStruct((T // 16, 16 * AW), BF16), jax.ShapeDtypeStruct((T // 16, 16 * AW), F32)],
        scratch_shapes=_slab_scratch(6, tm),
        compiler_params=_cparams(("parallel",)),
    )(*outs, *lses)


CONV_TM = 512
HALO = 8


def _conv_pre(ext, w, b):
    y = b + w[3] * ext
    for kk in range(1, CONV_K):
        y = y + w[3 - kk] * pltpu.roll(ext, kk, 0)
    return y


def _rows_to_block(rows, n, width):
    ri = lax.broadcasted_iota(jnp.int32, (n, width), 0)
    out = jnp.zeros((n, width), F32)
    for j, r in enumerate(rows):
        out = out + jnp.where(ri == j, r, 0.0)
    return out


def _conv_bwd(xbc, dact, ddt, w, b):
    nblk = T // CONV_TM
    per = CONV_TM // HALO

    def body(x_ref, xb_ref, xa_ref, g_ref, ga_ref, ddt_ref, w_ref, b_ref, dx_ref, dw_ref):
        i = pl.program_id(0)
        wv = [w_ref[pl.ds(j, 1), :] for j in range(CONV_K)]
        before = jnp.where(i > 0, xb_ref[...], 0.0)
        last = i == nblk - 1
        after = jnp.where(last, 0.0, xa_ref[...])
        g_after = jnp.where(last, 0.0, ga_ref[...])
        ext = jnp.concatenate([before, x_ref[...], after], axis=0)
        y = _conv_pre(ext, wv, b_ref[...])[HALO:]
        sg = _sigmoid(y)
        dy = jnp.concatenate([g_ref[...], g_after], axis=0) * (sg * (1.0 + y * (1.0 - sg)))
        n = CONV_TM + HALO
        dx = wv[3] * dy
        for kk in range(1, CONV_K):
            dx = dx + wv[3 - kk] * pltpu.roll(dy, n - kk, 0)
        dx_ref[:, pl.ds(0, CONV_CH)] = dx[:CONV_TM].astype(BF16)
        dx_ref[:, pl.ds(CONV_CH, DT_PAD)] = ddt_ref[...].astype(BF16)
        dyc = dy[:CONV_TM]
        rows = [jnp.sum(dyc * (pltpu.roll(ext, 3 - j, 0) if j < 3 else ext)[HALO:HALO + CONV_TM], axis=0, keepdims=True)
                for j in range(CONV_K)]
        rows.append(jnp.sum(dyc, axis=0, keepdims=True))
        part = _rows_to_block(rows, 8, CONV_CH)

        @pl.when(i == 0)
        def _():
            dw_ref[...] = jnp.zeros_like(dw_ref)
        dw_ref[...] += part

    blk = pl.BlockSpec((CONV_TM, CONV_CH), lambda i: (i, 0))
    hb = pl.BlockSpec((HALO, CONV_CH), lambda i: (jnp.maximum(i * per - 1, 0), 0))
    ha = pl.BlockSpec((HALO, CONV_CH), lambda i: (jnp.minimum((i + 1) * per, T // HALO - 1), 0))
    return pl.pallas_call(
        body, name="conv_bwd", grid=(nblk,),
        in_specs=[blk, hb, ha, blk, ha, pl.BlockSpec((CONV_TM, DT_PAD), lambda i: (i, 0)),
                  pl.BlockSpec((CONV_K, CONV_CH), lambda i: (0, 0)), pl.BlockSpec((1, CONV_CH), lambda i: (0, 0))],
        out_specs=[pl.BlockSpec((CONV_TM, CONV_CH + DT_PAD), lambda i: (i, 0)), pl.BlockSpec((8, CONV_CH), lambda i: (0, 0))],
        out_shape=[jax.ShapeDtypeStruct((T, CONV_CH + DT_PAD), BF16), jax.ShapeDtypeStruct((8, CONV_CH), F32)],
        compiler_params=_cparams(("arbitrary",)),
    )(xbc, xbc, xbc, dact, dact, ddt, w, b)


def _pick(mat, h):
    lane = lax.broadcasted_iota(jnp.int32, mat.shape, 1)
    return jnp.sum(jnp.where(lane == h, mat, 0.0), axis=1, keepdims=True)


def _heads(fn):
    return jnp.stack([fn(h) for h in range(HEADS)])


def _ssd_prep(dt_ref, bias_ref, alog_ref, dsk_ref, b_ref, c_ref, xs_ref, state_ref, cst):
    li = lax.broadcasted_iota(jnp.int32, (CHUNK, CHUNK), 0)
    si = lax.broadcasted_iota(jnp.int32, (CHUNK, CHUNK), 1)
    tri = li >= si
    dtp = dt_ref[...] + bias_ref[...]
    dt = _softplus(dtp)
    A = -jnp.exp(alog_ref[...])
    a = dt * A
    cs = jnp.dot(tri.astype(F32), a, precision=HIGHEST, preferred_element_type=F32)
    cst[...] = cs.T
    Bm = b_ref[...].astype(BF16)
    Cm = c_ref[...].astype(BF16)
    cb = _nt(Cm, Bm)
    dskv = dsk_ref[...]
    cs_col = _heads(lambda h: _pick(cs, h))
    cs_row = _heads(lambda h: cst[pl.ds(h, 1), :])
    dt_col = _heads(lambda h: _pick(dt, h))
    dsk_col = _heads(lambda h: _pick(dskv, h))
    lam = jnp.exp(jnp.where(tri, cs_col - cs_row, NEG))
    x = _heads(lambda h: xs_ref[:, pl.ds(HD * h, HD)])
    xdt = x * dt_col
    prev = _heads(lambda h: state_ref[pl.ds(HD * h, HD), :])
    lane = lax.broadcasted_iota(jnp.int32, (1, 1, CHUNK), 2)
    cl = jnp.sum(jnp.where(lane == CHUNK - 1, cs_row, 0.0), axis=2, keepdims=True)
    f = jnp.exp(cl - cs_col)
    return dict(li=li, si=si, dtp=dtp, dt=dt, A=A, Bm=Bm, Cm=Cm, cb=cb, cs_col=cs_col, dt_col=dt_col, dsk_col=dsk_col,
                lam=lam, x=x, xdt=xdt, prev=prev, cl=cl, f=f)


def _ssd_fwd(xbcdt, conv_w, conv_b, bias, alog, dsk, qkvz, attn, gs):
    nc = T // CHUNK
    R = SSD_PER * CHUNK
    per = R // HALO

    def body(xbc_ref, halo_ref, cw_ref, cb_ref, dt_ref, bias_ref, alog_ref, dsk_ref, z_ref, at_ref, gs_ref,
             y_ref, st_ref, cat_ref, act_ref, state, cst):
        @pl.when(pl.program_id(0) == 0)
        def _():
            state[...] = jnp.zeros_like(state)
        for sub in range(SSD_PER):
            rows = pl.ds(sub * CHUNK, CHUNK)
            st_ref[sub] = state[...]
            halo = (jnp.where(pl.program_id(0) > 0, halo_ref[...], 0.0) if sub == 0
                    else xbc_ref[pl.ds(sub * CHUNK - HALO, HALO), :])
            one_chunk(halo, xbc_ref.at[rows, :], cw_ref, cb_ref, dt_ref.at[rows, :], bias_ref, alog_ref, dsk_ref,
                      z_ref.at[rows, :], at_ref.at[rows, :], gs_ref, y_ref.at[rows, :], cat_ref.at[rows, :],
                      act_ref.at[rows, :], state, cst)

    def one_chunk(halo, xbc_ref, cw_ref, cb_ref, dt_ref, bias_ref, alog_ref, dsk_ref, z_ref, at_ref, gs_ref,
                  y_ref, cat_ref, act_ref, state, cst):
        pre = _conv_pre(jnp.concatenate([halo, xbc_ref[...]], axis=0),
                        [cw_ref[pl.ds(j, 1), :] for j in range(CONV_K)], cb_ref[...])[HALO:]
        act_ref[...] = pre * _sigmoid(pre)
        xs_ref, b_ref, c_ref = (act_ref.at[:, pl.ds(0, AW)], act_ref.at[:, pl.ds(AW, NS)],
                                act_ref.at[:, pl.ds(AW + NS, NS)])
        s = _ssd_prep(dt_ref, bias_ref, alog_ref, dsk_ref, b_ref, c_ref, xs_ref, state, cst)
        Bm, Cm, prev = s["Bm"], s["Cm"], s["prev"]
        g = (s["cb"] * s["lam"]).astype(BF16)
        xdtb = s["xdt"].astype(BF16)
        prevb = prev.astype(BF16)
        y = _heads(lambda h: _nn(g[h], xdtb[h])) + _heads(lambda h: _nt(Cm, prevb[h])) * jnp.exp(s["cs_col"])
        y = y + s["dsk_col"] * s["x"]
        xf = (s["xdt"] * s["f"]).astype(BF16)
        new = prev * jnp.exp(s["cl"]) + _heads(lambda h: _tn(xf[h], Bm))
        for h in range(HEADS):
            y_ref[:, pl.ds(HD * h, HD)] = y[h]
            state[pl.ds(HD * h, HD), :] = new[h]
        z = z_ref[...]
        gi = y_ref[...] * (z * _sigmoid(z))
        cat_ref[:, pl.ds(0, AW)] = at_ref[...].astype(BF16)
        cat_ref[:, pl.ds(AW, AW)] = (gi * _rstd(gi) * gs_ref[...]).astype(BF16)

    vec = pl.BlockSpec((1, DT_PAD), lambda c: (0, 0))
    blk = pl.BlockSpec((R, AW), lambda c: (c, 0))
    return pl.pallas_call(
        body, name="ssd_fwd", grid=(nc // SSD_PER,),
        in_specs=[pl.BlockSpec((R, CONV_CH), lambda c: (c, 0)),
                  pl.BlockSpec((HALO, CONV_CH), lambda c: (jnp.maximum(c * per - 1, 0), 0)),
                  pl.BlockSpec((CONV_K, CONV_CH), lambda c: (0, 0)), pl.BlockSpec((1, CONV_CH), lambda c: (0, 0)),
                  pl.BlockSpec((R, DT_PAD), lambda c: (c, 6)),
                  vec, vec, vec, pl.BlockSpec((R, AW), lambda c: (c, 3)), blk, pl.BlockSpec((1, AW), lambda c: (0, 0))],
        out_specs=[blk, pl.BlockSpec((SSD_PER, AW, NS), lambda c: (c, 0, 0)), pl.BlockSpec((R, D), lambda c: (c, 0)),
                   pl.BlockSpec((R, CONV_CH), lambda c: (c, 0))],
        out_shape=[jax.ShapeDtypeStruct((T, AW), F32), jax.ShapeDtypeStruct((nc, AW, NS), F32),
                   jax.ShapeDtypeStruct((T, D), BF16), jax.ShapeDtypeStruct((T, CONV_CH), F32)],
        scratch_shapes=[pltpu.VMEM((AW, NS), F32), pltpu.VMEM((CHUNK, CHUNK), F32)],
        compiler_params=_cparams(("arbitrary",)),
    )(xbcdt, xbcdt, conv_w, conv_b, xbcdt, bias, alog, dsk, qkvz, attn, gs)


def _ssd_bwd(act, xbcdt, bias, alog, dsk, states, y_ssd, qkvz, dcat, gs):
    nc = T // CHUNK

    def body(xs_ref, b_ref, c_ref, dt_ref, bias_ref, alog_ref, dsk_ref, st_ref, y_ref, z_ref, dyn_ref, gs_ref,
             dact_ref, ddt_ref, par_ref, dz_ref, dgs_ref, dstate, cst, dy_ref):
        @pl.when(pl.program_id(0) == 0)
        def _():
            dstate[...] = jnp.zeros_like(dstate)
            par_ref[...] = jnp.zeros_like(par_ref)
            dgs_ref[...] = jnp.zeros_like(dgs_ref)
        for sub in reversed(range(SSD_PER)):
            rows = pl.ds(sub * CHUNK, CHUNK)
            one_chunk(xs_ref.at[rows, :], b_ref.at[rows, :], c_ref.at[rows, :], dt_ref.at[rows, :], bias_ref, alog_ref,
                      dsk_ref, st_ref.at[sub], y_ref.at[rows, :], z_ref.at[rows, :], dyn_ref.at[rows, :], gs_ref,
                      dact_ref.at[rows, :], ddt_ref.at[rows, :], par_ref, dz_ref.at[rows, :], dgs_ref, dstate, cst, dy_ref)

    def one_chunk(xs_ref, b_ref, c_ref, dt_ref, bias_ref, alog_ref, dsk_ref, st_ref, y_ref, z_ref, dyn_ref, gs_ref,
                  dact_ref, ddt_ref, par_ref, dz_ref, dgs_ref, dstate, cst, dy_ref):
        z, yv, dyn = z_ref[...], y_ref[...], dyn_ref[...]
        sg = _sigmoid(z)
        sz = z * sg
        gi = yv * sz
        rg = _rstd(gi)
        ng = gi * rg
        dgi = _rms_bwd(ng, rg, gs_ref[...], dyn)
        dy_ref[...] = dgi * sz
        dz_ref[...] = dgi * yv * (sg * (1.0 + z * (1.0 - sg)))
        dgs_ref[...] += _colsum(dyn * ng)
        s = _ssd_prep(dt_ref, bias_ref, alog_ref, dsk_ref, b_ref, c_ref, xs_ref, st_ref, cst)
        Bm, Cm, prev, lam, x, xdt, f, cl = s["Bm"], s["Cm"], s["prev"], s["lam"], s["x"], s["xdt"], s["f"], s["cl"]
        lane = lax.broadcasted_iota(jnp.int32, (1, DT_PAD), 1)
        row = lax.broadcasted_iota(jnp.int32, (1, CHUNK, 1), 1)
        g = s["cb"] * lam
        gb, xdtb, prevb = g.astype(BF16), xdt.astype(BF16), prev.astype(BF16)
        dy = _heads(lambda h: dy_ref[:, pl.ds(HD * h, HD)])
        dyb = dy.astype(BF16)
        dnew = _heads(lambda h: dstate[pl.ds(HD * h, HD), :])
        dnewb = dnew.astype(BF16)
        E = jnp.exp(s["cs_col"])
        ecl = jnp.exp(cl)
        dG = _heads(lambda h: _nt(dyb[h], xdtb[h]))
        dxdt = _heads(lambda h: _tn(gb[h], dyb[h]))
        Yo = _heads(lambda h: _nt(Cm, prevb[h]))
        W = _heads(lambda h: _nt(Bm, dnewb[h]))
        dcb = jnp.sum(dG * lam, axis=0)
        Mm = dG * g
        col_sums = jnp.sum(Mm, axis=1, keepdims=True)
        dYo = (dy * E).astype(BF16)
        dxdt = dxdt + W * f
        dF = jnp.sum(W * xdt, axis=2, keepdims=True) * f
        dcl = jnp.sum(dnew * prev, axis=(1, 2), keepdims=True) * ecl + jnp.sum(dF, axis=1, keepdims=True)
        dcs = (jnp.sum(Mm, axis=2, keepdims=True) + jnp.sum(dy * Yo, axis=2, keepdims=True) * E - dF
               + jnp.where(row == CHUNK - 1, dcl, 0.0))
        ddt_x = jnp.sum(dxdt * x, axis=2, keepdims=True)
        dD = jnp.sum(dy * x, axis=(1, 2), keepdims=True)
        dx = s["dsk_col"] * dy + dxdt * s["dt_col"]
        xfb = (xdt * f).astype(BF16)
        dprev = _heads(lambda h: _tn(dYo[h], Cm)) + dnew * ecl
        dcbb = dcb.astype(BF16)
        dC = _nn(dcbb, Bm)
        dB = _tn(dcbb, Cm)
        dcs_mat = -_rows_to_block([col_sums[h] for h in range(HEADS)], CHUNK, CHUNK).T
        ddt_mat = jnp.zeros((CHUNK, DT_PAD), F32)
        dD_row = jnp.zeros((1, DT_PAD), F32)
        for h in range(HEADS):
            sl = pl.ds(HD * h, HD)
            dC = dC + _nn(dYo[h], prevb[h])
            dB = dB + _nn(xfb[h], dnewb[h])
            dcs_mat = dcs_mat + jnp.where(lane == h, dcs[h], 0.0)
            ddt_mat = ddt_mat + jnp.where(lane == h, ddt_x[h], 0.0)
            dD_row = dD_row + jnp.where(lane == h, dD[h], 0.0)
            dact_ref[:, sl] = dx[h]
            dstate[sl, :] = dprev[h]
        dact_ref[:, pl.ds(AW, NS)] = dB
        dact_ref[:, pl.ds(AW + NS, NS)] = dC
        da = jnp.dot((s["li"] <= s["si"]).astype(F32), dcs_mat, precision=HIGHEST, preferred_element_type=F32)
        ddtp = jnp.where(lane < HEADS, (ddt_mat + da * s["A"]) * _sigmoid(s["dtp"]), 0.0)
        ddt_ref[...] = ddtp
        dalog = jnp.where(lane < HEADS, jnp.sum(da * s["dt"], axis=0, keepdims=True) * s["A"], 0.0)
        par_ref[...] += _rows_to_block([jnp.sum(ddtp, axis=0, keepdims=True), dalog, dD_row], 8, DT_PAD)

    vec = pl.BlockSpec((1, DT_PAD), lambda c: (0, 0))
    nstep = nc // SSD_PER
    rev = lambda c: nstep - 1 - c
    R = SSD_PER * CHUNK
    return pl.pallas_call(
        body, name="ssd_bwd", grid=(nstep,),
        in_specs=[pl.BlockSpec((R, AW), lambda c: (rev(c), 0)), pl.BlockSpec((R, NS), lambda c: (rev(c), 4)),
                  pl.BlockSpec((R, NS), lambda c: (rev(c), 5)), pl.BlockSpec((R, DT_PAD), lambda c: (rev(c), 6)),
                  vec, vec, vec,
                  pl.BlockSpec((SSD_PER, AW, NS), lambda c: (rev(c), 0, 0)), pl.BlockSpec((R, AW), lambda c: (rev(c), 0)),
                  pl.BlockSpec((R, AW), lambda c: (rev(c), 3)), pl.BlockSpec((R, AW), lambda c: (rev(c), 1)),
                  pl.BlockSpec((1, AW), lambda c: (0, 0))],
        out_specs=[pl.BlockSpec((R, CONV_CH), lambda c: (rev(c), 0)), pl.BlockSpec((R, DT_PAD), lambda c: (rev(c), 0)),
                   pl.BlockSpec((8, DT_PAD), lambda c: (0, 0)), pl.BlockSpec((R, AW), lambda c: (rev(c), 0)),
                   pl.BlockSpec((1, AW), lambda c: (0, 0))],
        out_shape=[jax.ShapeDtypeStruct((T, CONV_CH), F32), jax.ShapeDtypeStruct((T, DT_PAD), F32),
                   jax.ShapeDtypeStruct((8, DT_PAD), F32), jax.ShapeDtypeStruct((T, AW), F32),
                   jax.ShapeDtypeStruct((1, AW), F32)],
        scratch_shapes=[pltpu.VMEM((AW, NS), F32), pltpu.VMEM((CHUNK, CHUNK), F32), pltpu.VMEM((CHUNK, AW), F32)],
        compiler_params=_cparams(("arbitrary",)),
    )(act, act, act, xbcdt, bias, alog, dsk, states, y_ssd, qkvz, dcat, gs)


def _place():
    return lax.axis_index("x"), lax.axis_index("y"), lax.axis_index("c")


def _slot(px, py, pc):
    return 4 * px + 2 * py + pc


SLAB_ROWS = 24


def _slab_pack(parts, name):
    n = len(parts)

    def body(*refs):
        slab = refs[n]
        slab[...] = jnp.zeros_like(slab)
        for ref, (arr, row) in zip(refs[:n], parts):
            slab[pl.ds(row, arr.shape[0]), pl.ds(0, arr.shape[1])] = ref[...]

    vm = pl.BlockSpec(memory_space=pltpu.VMEM)
    return pl.pallas_call(
        body, name=name, in_specs=[vm] * n, out_specs=vm, out_shape=jax.ShapeDtypeStruct((SLAB_ROWS, D), F32),
    )(*[a for a, _ in parts])


_HBM = pl.BlockSpec(memory_space=pltpu.HBM)
_SEM = pl.BlockSpec(memory_space=pltpu.SEMAPHORE)
_EFFECT = pltpu.SideEffectType.DATAFLOW_SIDE_EFFECTING


def _peers(x, y, c):
    out = []
    for kk in range(1, N_DEV):
        fx, fy, fc = kk >> 2 & 1, kk >> 1 & 1, kk & 1
        out.append((1 - x if fx else x, 1 - y if fy else y, 1 - c if fc else c))
    return out


def _send_start(src, per_peer, name, dep):
    (handles, token) = _send_start_many([src], per_peer, name, dep)
    return handles, token


def _near_peers(x, y, c):
    return [(x, y, 1 - c), (1 - x, y, c), (x, 1 - y, c), (1 - x, 1 - y, c)]


def _send_start_many(srcs, per_peer, name, dep, peers=_peers, npeers=N_DEV - 1):
    n = len(srcs)

    def body(*refs):
        src_refs, land_refs = refs[:n], refs[n:2 * n]
        send_sems, recv_sems = refs[2 * n + 1], refs[2 * n + 2]
        token = refs[-1]
        x, y, c = _place()
        mine = _slot(x, y, c)
        for a in range(n):
            for kk, peer in enumerate(peers(x, y, c)):
                pltpu.make_async_remote_copy(
                    src_ref=src_refs[a].at[_slot(*peer)] if per_peer else src_refs[a], dst_ref=land_refs[a].at[mine],
                    send_sem=send_sems.at[a * npeers + kk], recv_sem=recv_sems.at[a * npeers + kk],
                    device_id=peer, device_id_type=MESH).start()
        token[...] = jnp.zeros_like(token)

    lands = [lax.empty((N_DEV,) + tuple(s.shape[1:] if per_peer else s.shape), s.dtype) for s in srcs]
    hbm = lambda t: pltpu.with_memory_space_constraint(t, pltpu.HBM)
    outs = pl.pallas_call(
        body, name=name,
        out_shape=(pltpu.SemaphoreType.DMA((n * npeers,)), pltpu.SemaphoreType.DMA((n * npeers,)),
                   *[pltpu.HBM(s.shape, s.dtype) for s in srcs], *[pltpu.HBM(l.shape, l.dtype) for l in lands],
                   jax.ShapeDtypeStruct((8, 128), F32)),
        in_specs=(*[_HBM] * (2 * n), _ANY),
        out_specs=(_SEM, _SEM, *[_HBM] * (2 * n), pl.BlockSpec(memory_space=pltpu.VMEM)),
        input_output_aliases={i: 2 + i for i in range(2 * n)},
        compiler_params=pltpu.CompilerParams(has_side_effects=_EFFECT),
    )(*[hbm(s) for s in srcs], *[hbm(l) for l in lands], dep)
    return (outs[0], outs[1], list(outs[2:2 + n]), list(outs[2 + n:2 + 2 * n])), outs[-1]


def _send_wait(handles, after, name):
    srcs, lands = _send_wait_many(handles, after, name)
    return srcs[0], lands[0]


def _send_wait_many(handles, after, name, npeers=N_DEV - 1):
    send_sems, recv_sems, src_thrus, land_thrus = handles
    n = len(src_thrus)

    def body(*refs):
        land_refs = refs[n:2 * n]
        send_sems, recv_sems = refs[2 * n], refs[2 * n + 1]
        me = _place()
        for a in range(n):
            for kk in range(npeers):
                cp = pltpu.make_async_remote_copy(
                    src_ref=land_refs[a].at[0], dst_ref=land_refs[a].at[0],
                    send_sem=send_sems.at[a * npeers + kk], recv_sem=recv_sems.at[a * npeers + kk],
                    device_id=me, device_id_type=MESH)
                cp.wait_send()
                cp.wait_recv()

    both = list(src_thrus) + list(land_thrus)
    outs = pl.pallas_call(
        body, name=name,
        out_shape=tuple(pltpu.HBM(t.shape, t.dtype) for t in both),
        in_specs=(*[_HBM] * (2 * n), _SEM, _SEM, _ANY), out_specs=tuple([_HBM] * (2 * n)),
        input_output_aliases={i: i for i in range(2 * n)},
        compiler_params=pltpu.CompilerParams(has_side_effects=_EFFECT),
    )(*both, send_sems, recv_sems, after)
    return list(outs[:n]), list(outs[n:])


def _forward_start(lands, name, dep):
    n = len(lands)

    def body(*refs):
        land_refs = refs[:n]
        send_sems, recv_sems = refs[n + 1], refs[n + 2]
        token = refs[-1]
        x, y, c = _place()
        for a in range(n):
            for j, chip in enumerate([(1 - x, y), (x, 1 - y), (1 - x, 1 - y)]):
                blk = land_refs[a].at[_slot(*chip, c)]
                pltpu.make_async_remote_copy(
                    src_ref=blk, dst_ref=blk, send_sem=send_sems.at[a * 3 + j], recv_sem=recv_sems.at[a * 3 + j],
                    device_id=(x, y, 1 - c), device_id_type=MESH).start()
        token[...] = jnp.zeros_like(token)

    outs = pl.pallas_call(
        body, name=name,
        out_shape=(pltpu.SemaphoreType.DMA((n * 3,)), pltpu.SemaphoreType.DMA((n * 3,)),
                   *[pltpu.HBM(l.shape, l.dtype) for l in lands], jax.ShapeDtypeStruct((8, 128), F32)),
        in_specs=(*[_HBM] * n, _ANY), out_specs=(_SEM, _SEM, *[_HBM] * n, pl.BlockSpec(memory_space=pltpu.VMEM)),
        input_output_aliases={i: 2 + i for i in range(n)},
        compiler_params=pltpu.CompilerParams(has_side_effects=_EFFECT),
    )(*lands, dep)
    return (outs[0], outs[1], list(outs[2:2 + n])), outs[-1]


def _forward_wait(handles, after, name):
    send_sems, recv_sems, land_thrus = handles
    n = len(land_thrus)

    def body(*refs):
        land_refs = refs[:n]
        send_sems, recv_sems = refs[n], refs[n + 1]
        me = _place()
        for a in range(n):
            for j in range(3):
                cp = pltpu.make_async_remote_copy(
                    src_ref=land_refs[a].at[0], dst_ref=land_refs[a].at[0],
                    send_sem=send_sems.at[a * 3 + j], recv_sem=recv_sems.at[a * 3 + j], device_id=me, device_id_type=MESH)
                cp.wait_send()
                cp.wait_recv()

    outs = pl.pallas_call(
        body, name=name,
        out_shape=tuple(pltpu.HBM(t.shape, t.dtype) for t in land_thrus),
        in_specs=(*[_HBM] * n, _SEM, _SEM, _ANY), out_specs=tuple([_HBM] * n),
        input_output_aliases={i: i for i in range(n)},
        compiler_params=pltpu.CompilerParams(has_side_effects=_EFFECT),
    )(*land_thrus, send_sems, recv_sems, after)
    return list(outs)


def _sum_slots(land, name):
    _, R, C = land.shape
    tm = R if R <= 512 else 512

    def body(x_ref, o_ref):
        acc = x_ref[0].astype(F32)
        for j in range(1, N_DEV):
            acc = acc + x_ref[j].astype(F32)
        o_ref[...] = acc

    return pl.pallas_call(
        body, name=name, grid=(R // tm,),
        in_specs=[pl.BlockSpec((N_DEV, tm, C), lambda i: (0, i, 0))], out_specs=pl.BlockSpec((tm, C), lambda i: (i, 0)),
        out_shape=jax.ShapeDtypeStruct((R, C), F32), compiler_params=_cparams(("parallel",)),
    )(land)


def _adam_math(w, g, m, v):
    m2 = ADAM_B1 * m + (1.0 - ADAM_B1) * g
    v2 = ADAM_B2 * v + (1.0 - ADAM_B2) * (g * g)
    m_hat = m2 / (1.0 - ADAM_B1 ** ADAM_STEP)
    v_hat = v2 / (1.0 - ADAM_B2 ** ADAM_STEP)
    delta = -ADAM_LR * (m_hat / (jnp.sqrt(v_hat) + ADAM_EPS) + ADAM_WD * w)
    return delta, m2, v2


def _adamw(w, g, m, v, name):
    R, C = w.shape
    tm = R if R <= 512 else 256
    return _rowwise(lambda w, g, m, v: (_adam_math(w, g, m, v), ()), [w, g, m, v], [], [(C, F32)] * 3, [], tm=tm, name=name)


def _adamw_slots(land, w, m, v, name):
    _, R, C = land.shape
    tm = R if R <= 256 else 256

    def body(x_ref, w_ref, m_ref, v_ref, g_ref, d_ref, mo_ref, vo_ref):
        g = x_ref[0].astype(F32)
        for j in range(1, N_DEV):
            g = g + x_ref[j].astype(F32)
        d, m2, v2 = _adam_math(w_ref[...], g, m_ref[...], v_ref[...])
        g_ref[...] = g
        d_ref[...] = d
        mo_ref[...] = m2
        vo_ref[...] = v2

    row = pl.BlockSpec((tm, C), lambda i: (i, 0))
    return pl.pallas_call(
        body, name=name, grid=(R // tm,),
        in_specs=[pl.BlockSpec((N_DEV, tm, C), lambda i: (0, i, 0)), row, row, row], out_specs=[row] * 4,
        out_shape=[jax.ShapeDtypeStruct((R, C), F32)] * 4, compiler_params=_cparams(("parallel",)),
    )(land, w, m, v)


def _adamw_small(slab, slab_rows, g_conv_w, ws, ms, vs):
    n = len(ws)

    def body(*refs):
        slab_ref, gc_ref = refs[0], refs[1]
        w_refs, m_refs, v_refs = refs[2:2 + n], refs[2 + n:2 + 2 * n], refs[2 + 2 * n:2 + 3 * n]
        outs = refs[2 + 3 * n:]
        loss_ref = outs[0]
        g_out, d_out, m_out, v_out = (outs[1 + i * n:1 + (i + 1) * n] for i in range(4))
        loss_ref[...] = jnp.sum(slab_ref[pl.ds(6, 1), :], axis=1, keepdims=True)
        for i in range(n):
            g = gc_ref[...] if i == n - 1 else slab_ref[pl.ds(slab_rows[i], 1), pl.ds(0, ws[i].shape[1])]
            d, m2, v2 = _adam_math(w_refs[i][...], g, m_refs[i][...], v_refs[i][...])
            g_out[i][...] = g
            d_out[i][...] = d
            m_out[i][...] = m2
            v_out[i][...] = v2

    vm = pl.BlockSpec(memory_space=pltpu.VMEM)
    shapes = [jax.ShapeDtypeStruct(w.shape, F32) for w in ws]
    outs = pl.pallas_call(
        body, name="adamw_small", in_specs=[vm] * (2 + 3 * n), out_specs=[vm] * (1 + 4 * n),
        out_shape=[jax.ShapeDtypeStruct((1, 1), F32)] + shapes * 4,
    )(slab, g_conv_w, *ws, *ms, *vs)
    return outs[0], outs[1:1 + n], outs[1 + n:1 + 2 * n], outs[1 + 2 * n:1 + 3 * n], outs[1 + 3 * n:]


SMALL = ["norm_mix_pre", "norm_mix_post", "norm_mlp_pre", "norm_mlp_post", "norm_ple_post",
         "conv_b", "ssd_norm_g", "dt_bias", "a_log", "d_skip"]


def _pad_row(v, width=D):
    return jnp.pad(v, ((0, 0), (0, width - v.shape[1])))


def kernel(x, p, positions, norm_mix_pre, norm_mix_post, w_in, conv_w, conv_b, dt_bias, a_log, d_skip, ssd_norm_g, w_out, norm_mlp_pre, norm_mlp_post, w_up, w_down, w_ple_gate, w_ple_proj, norm_ple_post, loss_target, m_norm_mix_pre, m_norm_mix_post, m_w_in, m_conv_w, m_conv_b, m_dt_bias, m_a_log, m_d_skip, m_ssd_norm_g, m_w_out, m_norm_mlp_pre, m_norm_mlp_post, m_w_up, m_w_down, m_w_ple_gate, m_w_ple_proj, m_norm_ple_post, v_norm_mix_pre, v_norm_mix_post, v_w_in, v_conv_w, v_conv_b, v_dt_bias, v_a_log, v_d_skip, v_ssd_norm_g, v_w_out, v_norm_mlp_pre, v_norm_mlp_post, v_w_up, v_w_down, v_w_ple_gate, v_w_ple_proj, v_norm_ple_post):
    args = dict(locals())
    x2, p2, tgt = x[0], p[0, 0], loss_target[0]
    g1, g2, g3, g4, g5 = norm_mix_pre, norm_mix_post, norm_mlp_pre, norm_mlp_post, norm_ple_post

    me = _slot(*_place())
    pack_in = jnp.pad(w_in[0].T, ((0, W_IN_SHARD_PAD - W_IN_SHARD), (0, 0))).astype(BF16)
    rest = [w_out[0].astype(BF16), w_up[0].T.astype(BF16), w_down[0].astype(BF16), w_ple_gate[0].astype(BF16),
            w_ple_proj[0].T.reshape(32, D).astype(BF16)]
    conv_pack = jnp.pad(conv_w[0], ((0, 4), (0, 32)))
    in_handles, tok_in0 = _send_start_many([pack_in, conv_pack], False, "gather_in_start", g1, peers=_near_peers, npeers=4)

    inv_freq = ROPE_THETA ** (-jnp.arange(HD // 2, dtype=F32) * 2.0 / HD)
    pos = positions[0] + tok_in0[0, 0].astype(jnp.int32)
    ang = pos.astype(F32)[:, None] * jnp.tile(inv_freq, 4)
    cos128 = jnp.cos(ang)
    sin128 = jnp.sin(ang) * jnp.tile(jnp.concatenate([-jnp.ones(HD // 2, F32), jnp.ones(HD // 2, F32)]), 2)

    bias_w, alog_w, dsk_w = _pad_row(dt_bias, DT_PAD), _pad_row(a_log, DT_PAD), _pad_row(d_skip, DT_PAD)

    (u1,) = _rowwise(lambda a, g: ((a * _rstd(a) * g,), ()), [x2], [g1], [(D, BF16)], [], tm=512, name="norm_x",
                     deps=[cos128, sin128])
    p2b = p2.astype(BF16)

    in_back, in_land = _send_wait_many(in_handles, u1, "gather_in_wait", npeers=4)
    fw_handles, tok_fw = _forward_start(in_land, "gather_in_forward", u1)
    in_land = _forward_wait(fw_handles, tok_fw, "gather_in_forward_wait")
    gin = lax.dynamic_update_slice(in_land[0], in_back[0][None], (me, 0, 0))
    gconv = lax.dynamic_update_slice(in_land[1], in_back[1][None], (me, 0, 0))
    rest_handles, tok_rest = _send_start_many(rest, False, "gather_rest_start", gconv)
    w_inT = gin[:, :W_IN_SHARD].reshape(IN_W, D)
    w_qkvzT = w_inT[:4 * AW]
    w_xbcdtT = jnp.pad(w_inT[4 * AW:], ((0, DT_PAD - HEADS), (0, 0)))
    conv_full = gconv[:, :CONV_K, :96].transpose(1, 0, 2).reshape(CONV_K, CONV_CH)
    qkvz, xbcdt = _mm_rows(lambda a, b: ((a, b), ()), [(u1, w_qkvzT, True), (u1, w_xbcdtT, True)], [], [],
                           [(4 * AW, F32), (CONV_CH + DT_PAD, F32)], [], tm=512, name="proj_in", deps=[tok_rest])

    qkv = _rope_fwd(qkvz, cos128, sin128)
    qkv = [qkv[3 * i:3 * i + 3] for i in range(len(DILATIONS))]
    outs, lses = [], []
    for d, (qd, kd, vd) in zip(DILATIONS, qkv):
        o, l = _attn_fwd(qd, kd, vd, d)
        outs.append(o)
        lses.append(l)
    attn, lse, attn4, lse4, attn16, lse16 = _attn_merge(outs, lses)

    y_ssd, states, cat, act = _ssd_fwd(xbcdt, conv_full, conv_b, bias_w, alog_w, dsk_w, qkvz, attn, ssd_norm_g)


    rest_back, landed = _send_wait_many(rest_handles, cat, "gather_rest_wait")
    landed = [lax.dynamic_update_slice(l, b[None], (me, 0, 0)) for l, b in zip(landed, rest_back)]
    w_o, w_upT, w_dn, w_gate = landed[0].reshape(D, D), landed[1].reshape(DFF, D), landed[2].reshape(DFF, D), landed[3].reshape(D, D)
    w_projT = landed[4].reshape(D, PLE)

    def post1(mm, xx, ga):
        h = xx + mm * _rstd(mm) * ga
        return (mm, h, _rstd(h)), ()
    mix, h1, r3 = _mm_rows(post1, [(cat, w_o, False)], [x2], [g2], [(D, F32), (D, F32), (1, F32)], [], tm=512,
                           name="mix_out")

    a_up, ff, u2, h2, h2b = _mlp_fwd(h1, r3, g3, w_upT, w_dn, g4)
    relu2 = lambda a: jnp.square(jnp.maximum(a.astype(F32), 0.0))

    def final(gpre, ppv, hh, tg, g):
        sg = _sigmoid(gpre)
        ple = ppv * sg
        r = _rstd(ple)
        n = ple * r
        h3 = hh + n * g
        e = h3 - tg
        dh3 = e * (1.0 / D)
        dple = _rms_bwd(n, r, g, dh3)
        return (dh3, dple * sg, dple * ppv * sg * (1.0 - sg)), (_colsum(dh3 * n), _colsum(0.5 * e * e * (1.0 / D)))
    dh3, dpp, dgp, dg5, loss_vec = _mm_rows(final, [(h2b, w_gate, False), (p2b, w_projT, True)], [h2, tgt], [g5],
                                            [(D, F32), (D, BF16), (D, BF16)], [(1, D), (1, D)], tm=512, name="ple_loss")

    gw_projT = _mm(dpp, p2b, ta=True, tm=512, tn=256, tk=T, out_dtypes=(BF16,), name="gw_ple_proj")
    gw_gate = _mm(h2b, dgp, ta=True, tm=512, tn=1024, tk=T, out_dtypes=(BF16,), name="gw_ple_gate")
    def bwd_mlp_post(dg_, d3, f, g):
        dh2 = d3 + dg_
        r = _rstd(f)
        n = f * r
        return (dh2, _rms_bwd(n, r, g, dh2)), (_colsum(dh2 * n),)
    dh2, dff, dg4 = _mm_rows(bwd_mlp_post, [(dgp, w_gate, True)], [dh3, ff], [g4], [(D, F32), (D, BF16)], [(1, D)],
                             tm=512, name="bwd_ple_gate")

    gw_dn = _mm(a_up, dff, ta=True, tm=512, tn=1024, tk=T, a_pre=relu2, out_dtypes=(BF16,), name="gw_mlp_down")
    rs_a, tok_a = _send_start_many([gw_projT.reshape(N_DEV, 32, D), gw_gate.reshape(N_DEV, 128, D),
                                    gw_dn.reshape(N_DEV, 512, D)], True, "rs_start_a", g1)
    da_up, du2 = _mlp_dx(dff, a_up, w_upT, w_dn, tok_a)
    gw_upT = _mm(da_up, u2, ta=True, tm=512, tn=1024, tk=T, out_dtypes=(BF16,), name="gw_mlp_up")

    def bwd_mix_post(d2, du, hh, rr, mm, ga, gb):
        n3 = hh * rr
        dh1 = d2 + _rms_bwd(n3, rr, gb, du)
        r = _rstd(mm)
        n2 = mm * r
        return (dh1, _rms_bwd(n2, r, ga, dh1)), (_colsum(du * n3), _colsum(dh1 * n2))
    dh1, dmix, dg3, dg2 = _rowwise(bwd_mix_post, [dh2, du2, h1, r3, mix], [g2, g3], [(D, F32), (D, BF16)],
                                   [(1, D), (1, D)], tm=512, name="bwd_post_mix")

    gw_o = _mm(cat, dmix, ta=True, tm=512, tn=1024, tk=T, out_dtypes=(BF16,), name="gw_out")
    rs_b, tok_b = _send_start_many([gw_upT.reshape(N_DEV, 512, D), gw_o.reshape(N_DEV, 128, D)], True, "rs_start_b", g1)
    dcat, dattn4, dattn16 = _dx_out(dmix, w_o, tok_b)

    dact, ddtw, ssd_par, dz, dgs = _ssd_bwd(act, xbcdt, bias_w, alog_w, dsk_w, states, y_ssd, qkvz, dcat, ssd_norm_g)
    dxbcdt, conv_par = _conv_bwd(xbcdt, dact, ddtw, conv_full, conv_b)

    qkv_grads = [_attn_bwd(*qkv[0], dcat, attn, lse, 1),
                 _attn_bwd(*qkv[1], dattn4, attn4, lse4, 4),
                 _attn_bwd(*qkv[2], dattn16, attn16, lse16, 16)]
    dqkvz = _rope_bwd(qkv_grads, dz, cos128, sin128)

    gw_qkvzT = _mm(dqkvz, u1, ta=True, tm=512, tn=1024, tk=T, out_dtypes=(BF16,), name="gw_qkvz")
    gw_xbcdtT = _mm(dxbcdt, u1, ta=True, tm=896, tn=1024, tk=T, out_dtypes=(BF16,), name="gw_xbcdt")
    gw_inT = jnp.concatenate([gw_qkvzT, gw_xbcdtT], axis=0)[:IN_W]
    gw_inT = jnp.pad(gw_inT.reshape(N_DEV, W_IN_SHARD, D), ((0, 0), (0, W_IN_SHARD_PAD - W_IN_SHARD), (0, 0)))
    rs_in, tok_in = _send_start(gw_inT, True, "rs_start_w_in", g1)

    def bwd_in(ua, ub, d1, xx, g):
        rr = _rstd(xx)
        n = xx * rr
        du = ua + ub
        return (d1 + _rms_bwd(n, rr, g, du),), (_colsum(du * n),)
    grad_x, dg1 = _mm_rows(bwd_in, [(dqkvz, w_qkvzT, False), (dxbcdt, w_xbcdtT, False)], [dh1, x2], [g1],
                           [(D, F32)], [(1, D)], tm=512, name="bwd_in_proj", deps=[tok_in])

    my_slab = _slab_pack([(dg1, 0), (dg2, 1), (dg3, 2), (dg4, 3), (dg5, 4), (dgs, 5), (loss_vec, 6),
                          (conv_par, 8), (ssd_par, 16)], "slab_pack")
    slab_handles, tok_slab = _send_start_many([my_slab], False, "slab_start", g1)

    def scatter_finish(handles, nm, after):
        part, land = _send_wait(handles, after, "rs_wait_" + nm)
        own = lax.dynamic_slice(part, (me, 0, 0), (1,) + part.shape[1:])
        return _sum_slots(lax.dynamic_update_slice(land, own, (me, 0, 0)), "rs_sum_" + nm)
    def landed(handles, after, wait_name):
        parts, lands = _send_wait_many(handles, after, wait_name)
        return [lax.dynamic_update_slice(land, lax.dynamic_slice(part, (me, 0, 0), (1,) + part.shape[1:]), (me, 0, 0))
                for part, land in zip(parts, lands)]
    land_proj, land_gate, land_dn = landed(rs_a, tok_slab, "rs_wait_a")
    land_up, land_out = landed(rs_b, tok_slab, "rs_wait_b")

    grads, delta, new_m, new_v = {}, {}, {}, {}
    for nme, land in (("w_down", land_dn), ("w_out", land_out), ("w_ple_gate", land_gate)):
        outs4 = _adamw_slots(land, args[nme][0], args["m_" + nme][0], args["v_" + nme][0], "adamw_" + nme)
        grads[nme], delta[nme], new_m[nme], new_v[nme] = [t[None] for t in outs4]
    grads["w_up"] = _sum_slots(land_up, "rs_sum_w_up").T[None]
    grads["w_ple_proj"] = _sum_slots(land_proj, "rs_sum_w_proj").reshape(128, PLE).T[None]
    for nme in ["w_up", "w_ple_proj", "w_in"]:
        if nme == "w_in":
            g_inT = scatter_finish(rs_in, "w_in", delta["w_down"])
            grads["w_in"] = g_inT[:W_IN_SHARD].T[None]
        dl, mm_, vv_ = _adamw(args[nme][0], grads[nme][0], args["m_" + nme][0], args["v_" + nme][0], "adamw_" + nme)
        delta[nme], new_m[nme], new_v[nme] = dl[None], mm_[None], vv_[None]

    slab_back, slab_land = _send_wait_many(slab_handles, delta["w_in"], "slab_wait")
    slab = _sum_slots(lax.dynamic_update_slice(slab_land[0], slab_back[0][None], (me, 0, 0)), "slab_sum")
    g_conv_w = lax.dynamic_slice(slab[8:12, :CONV_CH], (0, me * 96), (CONV_K, 96))
    small_names = SMALL + ["conv_w"]
    small_rows = [0, 1, 2, 3, 4, 12, 5, 16, 17, 18, None]
    pick = lambda prefix: [args[prefix + nme] for nme in SMALL] + [args[prefix + "conv_w"][0]]
    loss11, g_s, d_s, m_s, v_s = _adamw_small(slab, small_rows, g_conv_w, pick(""), pick("m_"), pick("v_"))
    loss = loss11[0, 0]
    for i, nme in enumerate(small_names):
        lead = (lambda t: t[None]) if nme == "conv_w" else (lambda t: t)
        grads[nme], delta[nme], new_m[nme], new_v[nme] = lead(g_s[i]), lead(d_s[i]), lead(m_s[i]), lead(v_s[i])

    order = ["norm_mix_pre", "norm_mix_post", "w_in", "conv_w", "conv_b", "dt_bias", "a_log", "d_skip", "ssd_norm_g",
             "w_out", "norm_mlp_pre", "norm_mlp_post", "w_up", "w_down", "w_ple_gate", "w_ple_proj", "norm_ple_post"]
    return (loss, grad_x[None], *[grads[n] for n in order], *[delta[n] for n in order],
            *[new_m[n] for n in order], *[new_v[n] for n in order])
```

```python
import jax
import jax.numpy as jnp
from jax import lax
from jax.experimental import pallas as pl
from jax.experimental.pallas import tpu as pltpu

F32 = jnp.float32
BF16 = jnp.bfloat16
MESH = pl.DeviceIdType.MESH
HIGHEST = lax.Precision.HIGHEST

N_DEV = 8
T = 4096
D = 1024
HEADS = 8
HD = 64
AW = 512
NS = 128
CONV_K = 4
CONV_CH = 768
CHUNK = 128
SSD_PER = 2
DFF = 4096
PLE = 256
EPS = 1e-6
ROPE_THETA = 10000.0
DILATIONS = (1, 4, 16)
QBLK = 128
NEG = -1e30
IN_W = 2824
W_IN_SHARD = 353
W_IN_SHARD_PAD = 384
DT_PAD = 128

ADAM_LR, ADAM_B1, ADAM_B2, ADAM_EPS, ADAM_WD, ADAM_STEP = 0.001, 0.9, 0.999, 1e-08, 0.01, 10

VMEM_LIMIT = 56 * 1024 * 1024


_ANY = pl.BlockSpec(memory_space=pl.ANY)


def _cparams(sem=None):
    return pltpu.CompilerParams(dimension_semantics=sem, vmem_limit_bytes=VMEM_LIMIT)


def _dot(a, b, ca, cb, precision=None):
    return lax.dot_general(a, b, (((ca,), (cb,)), ((), ())), preferred_element_type=F32, precision=precision)


def _nn(a, b):
    return _dot(a, b, 1, 0)


def _nt(a, b):
    return _dot(a, b, 1, 1)


def _tn(a, b):
    return _dot(a, b, 0, 0)


def _sigmoid(x):
    return 1.0 / (1.0 + jnp.exp(-x))


def _softplus(x):
    return jnp.maximum(x, 0.0) + jnp.log(1.0 + jnp.exp(-jnp.abs(x)))


def _mm(a, b, *, ta=False, tb=False, tm, tn, tk, name,
        a_pre=None, a_rows=(), a_cols=(), b_pre=None, b_rows=(), b_cols=(),
        epi=None, epi_tiles=(), out_dtypes=(F32,), deps=()):
    if ta:
        K, M = a.shape
    else:
        M, K = a.shape
    if tb:
        N, K2 = b.shape
    else:
        K2, N = b.shape
    assert K == K2 and M % tm == 0 and N % tn == 0 and K % tk == 0, (name, a.shape, b.shape)
    nk = K // tk
    if ta:
        a_spec = pl.BlockSpec((tk, tm), lambda i, j, k: (k, i))
        a_row_specs = [pl.BlockSpec((tk, 1), lambda i, j, k: (k, 0)) for _ in a_rows]
        a_col_specs = [pl.BlockSpec((1, tm), lambda i, j, k: (0, i)) for _ in a_cols]
    else:
        a_spec = pl.BlockSpec((tm, tk), lambda i, j, k: (i, k))
        a_row_specs = [pl.BlockSpec((tm, 1), lambda i, j, k: (i, 0)) for _ in a_rows]
        a_col_specs = [pl.BlockSpec((1, tk), lambda i, j, k: (0, k)) for _ in a_cols]
    if tb:
        b_spec = pl.BlockSpec((tn, tk), lambda i, j, k: (j, k))
        b_row_specs = [pl.BlockSpec((tn, 1), lambda i, j, k: (j, 0)) for _ in b_rows]
        b_col_specs = [pl.BlockSpec((1, tk), lambda i, j, k: (0, k)) for _ in b_cols]
    else:
        b_spec = pl.BlockSpec((tk, tn), lambda i, j, k: (k, j))
        b_row_specs = [pl.BlockSpec((tk, 1), lambda i, j, k: (k, 0)) for _ in b_rows]
        b_col_specs = [pl.BlockSpec((1, tn), lambda i, j, k: (0, j)) for _ in b_cols]
    o_spec = pl.BlockSpec((tm, tn), lambda i, j, k: (i, j))
    na, nb, ne, no = len(a_rows) + len(a_cols), len(b_rows) + len(b_cols), len(epi_tiles), len(out_dtypes)

    def body(*refs):
        a_ref, b_ref = refs[0], refs[1]
        a_ex = refs[2:2 + na]
        b_ex = refs[2 + na:2 + na + nb]
        e_ex = refs[2 + na + nb:2 + na + nb + ne]
        first_out = 2 + na + nb + ne + len(deps)
        outs = refs[first_out:first_out + no]

        def finish(res):
            vals = epi(res, *[r[...] for r in e_ex]) if epi is not None else (res,)
            for o_ref, val in zip(outs, vals):
                o_ref[...] = val.astype(o_ref.dtype)

        at = a_ref[...]
        if a_pre is not None:
            at = a_pre(at, *[r[...] for r in a_ex])
        bt = b_ref[...]
        if b_pre is not None:
            bt = b_pre(bt, *[r[...] for r in b_ex])
        prod = _dot(at.astype(BF16), bt.astype(BF16), 0 if ta else 1, 1 if tb else 0)
        if nk == 1:
            finish(prod)
            return
        acc = refs[-1]
        k = pl.program_id(2)

        @pl.when(k == 0)
        def _():
            acc[...] = jnp.zeros_like(acc)
        acc[...] += prod

        @pl.when(k == nk - 1)
        def _():
            finish(acc[...])

    outs = pl.pallas_call(
        body, name=name,
        grid=(M // tm, N // tn, nk),
        in_specs=([a_spec, b_spec] + a_row_specs + a_col_specs + b_row_specs + b_col_specs + [o_spec] * ne
                  + [_ANY] * len(deps)),
        out_specs=[o_spec] * no,
        out_shape=[jax.ShapeDtypeStruct((M, N), dt) for dt in out_dtypes],
        scratch_shapes=[pltpu.VMEM((tm, tn), F32)] if nk > 1 else [],
        compiler_params=_cparams(("parallel", "parallel", "arbitrary")),
    )(a, b, *a_rows, *a_cols, *b_rows, *b_cols, *epi_tiles, *deps)
    return outs[0] if no == 1 else outs


MLP_TM = 1024
MLP_TC = 512


def _mlp_fwd(h, r, g, w_upT, w_dn, g_post):
    nc = DFF // MLP_TC

    def body(h_ref, r_ref, g_ref, wu_ref, wd_ref, gp_ref, a_ref, ff_ref, u_ref, ho_ref, hob_ref, acc, u_scr):
        c = pl.program_id(1)

        @pl.when(c == 0)
        def _():
            u = (h_ref[...] * r_ref[...] * g_ref[...]).astype(BF16)
            u_scr[...] = u
            u_ref[...] = u
            acc[...] = jnp.zeros_like(acc)
        a = _nt(u_scr[...], wu_ref[...])
        a_ref[...] = a.astype(BF16)
        acc[...] += _nn(jnp.square(jnp.maximum(a, 0.0)).astype(BF16), wd_ref[...])

        @pl.when(c == nc - 1)
        def _():
            f = acc[...]
            ff_ref[...] = f
            ho = h_ref[...] + f * _rstd(f) * gp_ref[...]
            ho_ref[...] = ho
            hob_ref[...] = ho.astype(BF16)

    row = pl.BlockSpec((MLP_TM, D), lambda i, c: (i, 0))
    wsp = pl.BlockSpec((MLP_TC, D), lambda i, c: (c, 0))
    vec = pl.BlockSpec((1, D), lambda i, c: (0, 0))
    return pl.pallas_call(
        body, name="mlp_fwd", grid=(T // MLP_TM, nc),
        in_specs=[row, pl.BlockSpec((MLP_TM, 1), lambda i, c: (i, 0)), vec, wsp, wsp, vec],
        out_specs=[pl.BlockSpec((MLP_TM, MLP_TC), lambda i, c: (i, c)), row, row, row, row],
        out_shape=[jax.ShapeDtypeStruct((T, DFF), BF16), jax.ShapeDtypeStruct((T, D), F32), jax.ShapeDtypeStruct((T, D), BF16),
                   jax.ShapeDtypeStruct((T, D), F32), jax.ShapeDtypeStruct((T, D), BF16)],
        scratch_shapes=[pltpu.VMEM((MLP_TM, D), F32), pltpu.VMEM((MLP_TM, D), BF16)],
        compiler_params=_cparams(("parallel", "arbitrary")),
    )(h, r, g, w_upT, w_dn, g_post)


def _mlp_dx(dff, a, w_upT, w_dn, dep):
    nc = DFF // MLP_TC

    def body(d_ref, a_ref, wu_ref, wd_ref, dep_ref, da_ref, du_ref, acc, d_scr):
        c = pl.program_id(1)

        @pl.when(c == 0)
        def _():
            d_scr[...] = d_ref[...].astype(BF16)
            acc[...] = jnp.zeros_like(acc)
        da = (_nt(d_scr[...], wd_ref[...]) * (2.0 * jnp.maximum(a_ref[...].astype(F32), 0.0))).astype(BF16)
        da_ref[...] = da
        acc[...] += _nn(da, wu_ref[...])

        @pl.when(c == nc - 1)
        def _():
            du_ref[...] = acc[...]

    row = pl.BlockSpec((MLP_TM, D), lambda i, c: (i, 0))
    wsp = pl.BlockSpec((MLP_TC, D), lambda i, c: (c, 0))
    chunk = pl.BlockSpec((MLP_TM, MLP_TC), lambda i, c: (i, c))
    return pl.pallas_call(
        body, name="mlp_dx", grid=(T // MLP_TM, nc),
        in_specs=[row, chunk, wsp, wsp, _ANY], out_specs=[chunk, row],
        out_shape=[jax.ShapeDtypeStruct((T, DFF), BF16), jax.ShapeDtypeStruct((T, D), F32)],
        scratch_shapes=[pltpu.VMEM((MLP_TM, D), F32), pltpu.VMEM((MLP_TM, D), BF16)],
        compiler_params=_cparams(("parallel", "arbitrary")),
    )(dff, a, w_upT, w_dn, dep)


def _rowwise(fn, rows, vecs, out_rows, out_sums, *, tm, name, deps=()):
    specs, arrs = [], []
    R = None
    for r in rows:
        if isinstance(r, tuple):
            arr, width, cb = r
            specs.append(pl.BlockSpec((tm, width), lambda i, cb=cb: (i, cb)))
        else:
            arr = r
            specs.append(pl.BlockSpec((tm, arr.shape[1]), lambda i: (i, 0)))
        R = arr.shape[0] if R is None else R
        assert arr.shape[0] == R, name
        arrs.append(arr)
    assert R % tm == 0, name
    for v in vecs:
        specs.append(pl.BlockSpec(v.shape, lambda i: (0, 0)))
        arrs.append(v)
    nr, nv, no, ns = len(rows), len(vecs), len(out_rows), len(out_sums)
    out_specs = [pl.BlockSpec((tm, w), lambda i: (i, 0)) for w, _ in out_rows]
    out_specs += [pl.BlockSpec(s, lambda i: (0, 0)) for s in out_sums]
    out_shape = [jax.ShapeDtypeStruct((R, w), dt) for w, dt in out_rows]
    out_shape += [jax.ShapeDtypeStruct(s, F32) for s in out_sums]

    nd = len(deps)

    def body(*refs):
        ins = [r[...] for r in refs[:nr + nv]]
        o_refs = refs[nr + nv + nd:nr + nv + nd + no]
        s_refs = refs[nr + nv + nd + no:]
        o_vals, s_vals = fn(*ins)
        for ref, val in zip(o_refs, o_vals):
            ref[...] = val.astype(ref.dtype)
        if ns:
            @pl.when(pl.program_id(0) == 0)
            def _():
                for ref in s_refs:
                    ref[...] = jnp.zeros_like(ref)
            for ref, val in zip(s_refs, s_vals):
                ref[...] += val

    outs = pl.pallas_call(
        body, name=name, grid=(R // tm,), in_specs=specs + [_ANY] * nd, out_specs=out_specs, out_shape=out_shape,
        compiler_params=_cparams(("arbitrary",) if ns else ("parallel",)),
    )(*arrs, *deps)
    return outs


def _mm_rows(fn, mats, rows, vecs, out_rows, out_sums, *, tm, name, deps=()):
    R = mats[0][0].shape[0]
    assert R % tm == 0, name
    specs, arrs = [], []
    for a, b, tb in mats:
        specs += [pl.BlockSpec((tm, a.shape[1]), lambda i: (i, 0)), pl.BlockSpec(b.shape, lambda i: (0, 0))]
        arrs += [a, b]
    for r in rows:
        specs.append(pl.BlockSpec((tm, r.shape[1]), lambda i: (i, 0)))
        arrs.append(r)
    for v in vecs:
        specs.append(pl.BlockSpec(v.shape, lambda i: (0, 0)))
        arrs.append(v)
    nm, nr, nv, nd, no, ns = len(mats), len(rows), len(vecs), len(deps), len(out_rows), len(out_sums)
    out_specs = [pl.BlockSpec((tm, w), lambda i: (i, 0)) for w, _ in out_rows]
    out_specs += [pl.BlockSpec(s, lambda i: (0, 0)) for s in out_sums]
    out_shape = [jax.ShapeDtypeStruct((R, w), dt) for w, dt in out_rows] + [jax.ShapeDtypeStruct(s, F32) for s in out_sums]

    def body(*refs):
        prods = [_dot(refs[2 * p][...].astype(BF16), refs[2 * p + 1][...].astype(BF16), 1, 1 if mats[p][2] else 0)
                 for p in range(nm)]
        ins = [r[...] for r in refs[2 * nm:2 * nm + nr + nv]]
        first_out = 2 * nm + nr + nv + nd
        o_refs, s_refs = refs[first_out:first_out + no], refs[first_out + no:]
        o_vals, s_vals = fn(*prods, *ins)
        for ref, val in zip(o_refs, o_vals):
            ref[...] = val.astype(ref.dtype)
        if ns:
            @pl.when(pl.program_id(0) == 0)
            def _():
                for ref in s_refs:
                    ref[...] = jnp.zeros_like(ref)
            for ref, val in zip(s_refs, s_vals):
                ref[...] += val

    return pl.pallas_call(
        body, name=name, grid=(R // tm,), in_specs=specs + [_ANY] * nd, out_specs=out_specs, out_shape=out_shape,
        compiler_params=_cparams(("arbitrary",) if ns else ("parallel",)),
    )(*arrs, *deps)


def _colsum(x):
    return jnp.sum(x, axis=0, keepdims=True)


def _rstd(x):
    return lax.rsqrt(jnp.mean(x * x, axis=-1, keepdims=True) + EPS)


def _rms_bwd(xn, r, g, dy):
    dn = dy * g
    return r * (dn - xn * jnp.mean(dn * xn, axis=-1, keepdims=True))


def _partner(t):
    lane = lax.broadcasted_iota(jnp.int32, t.shape, 1)
    up = pltpu.roll(t, 96, 1)
    down = pltpu.roll(t, 32, 1)
    return jnp.where((lane % 64) < 32, up, down)


SLABS = AW // 128


def _rows(r, n, d):
    return pl.ds(r, n, stride=d) if d > 1 else pl.ds(0, n)


def _undilate(src_ref, dst, d, tm):
    for r in range(d):
        for j in range(SLABS):
            dst[j][_rows(r, tm // d, d), :] = src_ref[:, pl.ds(r * AW + j * 128, 128)].astype(dst[j].dtype)


def _dilate(dst_ref, src, d, tm):
    for r in range(d):
        for j in range(SLABS):
            dst_ref[:, pl.ds(r * AW + j * 128, 128)] = src[j][_rows(r, tm // d, d), :].astype(dst_ref.dtype)


def _slab_scratch(n, tm):
    return [pltpu.VMEM((tm, 128), F32)] * (SLABS * n)


def _slab_groups(flat):
    return [flat[SLABS * i:SLABS * (i + 1)] for i in range(len(flat) // SLABS)]


def _slab_specs(tm, first):
    return [pl.BlockSpec((tm, 128), lambda i, j=j: (i, first + j)) for j in range(SLABS)]


def _dil_spec(tm, d):
    return pl.BlockSpec((tm // d, d * AW), lambda i: (i, 0))


ROPE_TM = 512


def _rope_fwd(qkvz, cos128, sin128):
    tm = ROPE_TM

    def body(*refs):
        q_refs, k_refs, v_refs = refs[0:4], refs[4:8], refs[8:12]
        c_ref, s_ref = refs[12], refs[13]
        outs = refs[14:23]
        qs, ks = _slab_groups(refs[23:])
        c, s = c_ref[...], s_ref[...]
        for j in range(SLABS):
            q, k = q_refs[j][...], k_refs[j][...]
            qs[j][...] = (q * c + _partner(q) * s) * (HD ** -0.5)
            ks[j][...] = k * c + _partner(k) * s
        for di, d in enumerate(DILATIONS):
            oq, ok, ov = outs[3 * di:3 * di + 3]
            for r in range(d):
                rows = _rows(r, tm // d, d)
                for j in range(SLABS):
                    cols = pl.ds(r * AW + j * 128, 128)
                    oq[:, cols] = qs[j][rows, :].astype(BF16)
                    ok[:, cols] = ks[j][rows, :].astype(BF16)
                    ov[:, cols] = v_refs[j][rows, :].astype(BF16)

    tab = pl.BlockSpec((tm, 128), lambda i: (i, 0))
    out_specs, out_shape = [], []
    for d in DILATIONS:
        out_specs += [_dil_spec(tm, d)] * 3
        out_shape += [jax.ShapeDtypeStruct((T // d, d * AW), BF16)] * 3
    return pl.pallas_call(
        body, name="rope_fwd", grid=(T // tm,),
        in_specs=_slab_specs(tm, 0) + _slab_specs(tm, 4) + _slab_specs(tm, 8) + [tab, tab],
        out_specs=out_specs, out_shape=out_shape, scratch_shapes=_slab_scratch(2, tm),
        compiler_params=_cparams(("parallel",)),
    )(*([qkvz] * 12), cos128, sin128)


def _rope_bwd(grads, dz, cos128, sin128):
    tm = ROPE_TM

    def body(*refs):
        g_refs = refs[0:9]
        dz_ref, c_ref, s_ref, o_ref = refs[9], refs[10], refs[11], refs[12]
        scr = _slab_groups(refs[13:])
        for di, d in enumerate(DILATIONS[1:]):
            for t in range(3):
                _undilate(g_refs[3 * (di + 1) + t], scr[3 * di + t], d, tm)
        c, s = c_ref[...], s_ref[...]
        for j in range(SLABS):
            cols = pl.ds(j * 128, 128)
            tot = [g_refs[t][:, cols] + scr[t][j][...] + scr[3 + t][j][...] for t in range(3)]
            dqr = tot[0] * (HD ** -0.5)
            o_ref[:, pl.ds(j * 128, 128)] = (dqr * c + _partner(dqr * s)).astype(BF16)
            o_ref[:, pl.ds(AW + j * 128, 128)] = (tot[1] * c + _partner(tot[1] * s)).astype(BF16)
            o_ref[:, pl.ds(2 * AW + j * 128, 128)] = tot[2].astype(BF16)
        o_ref[:, pl.ds(3 * AW, AW)] = dz_ref[...].astype(BF16)

    tab = pl.BlockSpec((tm, 128), lambda i: (i, 0))
    in_specs, args = [], []
    for d, g in zip(DILATIONS, grads):
        in_specs += [_dil_spec(tm, d)] * 3
        args += list(g)
    return pl.pallas_call(
        body, name="rope_bwd", grid=(T // tm,),
        in_specs=in_specs + [pl.BlockSpec((tm, AW), lambda i: (i, 0)), tab, tab],
        out_specs=pl.BlockSpec((tm, 4 * AW), lambda i: (i, 0)),
        out_shape=jax.ShapeDtypeStruct((T, 4 * AW), BF16),
        scratch_shapes=_slab_scratch(6, tm),
        compiler_params=_cparams(("parallel",)),
    )(*args, dz, cos128, sin128)


def _dx_out(dmix, w_o, dep):
    tm = ROPE_TM

    def body(a_ref, w_ref, dep_ref, dcat_ref, o4, o16, *slabs):
        prod = _nt(a_ref[...].astype(BF16), w_ref[...].astype(BF16))
        dcat_ref[...] = prod
        for j in range(SLABS):
            slabs[j][...] = prod[:, 128 * j:128 * (j + 1)]
        _dilate(o4, slabs, 4, tm)
        _dilate(o16, slabs, 16, tm)

    return pl.pallas_call(
        body, name="dx_out", grid=(T // tm,),
        in_specs=[pl.BlockSpec((tm, D), lambda i: (i, 0)), pl.BlockSpec((D, D), lambda i: (0, 0)), _ANY],
        out_specs=[pl.BlockSpec((tm, D), lambda i: (i, 0)), _dil_spec(tm, 4), _dil_spec(tm, 16)],
        out_shape=[jax.ShapeDtypeStruct((T, D), F32), jax.ShapeDtypeStruct((T // 4, 4 * AW), F32),
                   jax.ShapeDtypeStruct((T // 16, 16 * AW), F32)],
        scratch_shapes=_slab_scratch(1, tm), compiler_params=_cparams(("parallel",)),
    )(dmix, w_o, dep)


def _band_masks():
    qi = lax.broadcasted_iota(jnp.int32, (QBLK, QBLK), 0)
    kj = lax.broadcasted_iota(jnp.int32, (QBLK, QBLK), 1)
    return kj >= qi, kj <= qi


def _attn_fwd(q, k, v, d):
    L = q.shape[0]
    npair = L // (2 * QBLK)

    def body(q_ref, kp_ref, kc_ref, vp_ref, vc_ref, o_ref, l_ref):
        pair = pl.program_id(1)
        mask_p, mask_c = _band_masks()
        for sub in range(2):
            rows = pl.ds(sub * QBLK, QBLK)
            first = jnp.where(pair > 0, 0.0, NEG) if sub == 0 else 0.0
            bias = jnp.concatenate([jnp.where(mask_p, 0.0, NEG) + first, jnp.where(mask_c, 0.0, NEG)], axis=1)
            k_prev = (lambda sl: kp_ref[:, sl]) if sub == 0 else (lambda sl: kc_ref[pl.ds(0, QBLK), sl])
            v_prev = (lambda sl: vp_ref[:, sl]) if sub == 0 else (lambda sl: vc_ref[pl.ds(0, QBLK), sl])
            s = []
            for h in range(HEADS):
                sl = pl.ds(HD * h, HD)
                qh = q_ref[rows, sl]
                s.append(jnp.concatenate([_nt(qh, k_prev(sl)), _nt(qh, kc_ref[rows, sl])], axis=1))
            s = jnp.stack(s) + bias
            m = jnp.max(s, axis=2, keepdims=True)
            e = jnp.exp(s - m)
            den = jnp.sum(e, axis=2, keepdims=True)
            p = e.astype(BF16)
            inv = 1.0 / den
            lse = m + jnp.log(den)
            for h in range(HEADS):
                sl = pl.ds(HD * h, HD)
                o_ref[rows, sl] = ((_nn(p[h, :, :QBLK], v_prev(sl)) + _nn(p[h, :, QBLK:], vc_ref[rows, sl])) * inv[h]
                                   ).astype(BF16)
                l_ref[rows, sl] = jnp.broadcast_to(lse[h], (QBLK, HD))

    cur = pl.BlockSpec((2 * QBLK, AW), lambda r, n: (n, r))
    prev = pl.BlockSpec((QBLK, AW), lambda r, n: (jnp.maximum(2 * n - 1, 0), r))
    return pl.pallas_call(
        body, name=f"attn_fwd_d{d}", grid=(d, npair),
        in_specs=[cur, prev, cur, prev, cur], out_specs=[cur, cur],
        out_shape=[jax.ShapeDtypeStruct((L, d * AW), BF16), jax.ShapeDtypeStruct((L, d * AW), F32)],
        compiler_params=_cparams(("parallel", "parallel")),
    )(q, k, k, v, v)


def _attn_bwd(q, k, v, do, at, lse, d):
    L = q.shape[0]
    nb = L // QBLK
    npair = nb // 2

    def body(qc_ref, qn_ref, kp_ref, kc_ref, vp_ref, vc_ref, doc_ref, don_ref, atc_ref, atn_ref,
             lc_ref, ln_ref, dq_ref, dk_ref, dv_ref):
        pair = pl.program_id(1)
        mask_p, mask_c = _band_masks()
        prev_bias = jnp.where(mask_p, 0.0, NEG)
        for sub in range(2):
            rows = pl.ds(sub * QBLK, QBLK)
            second = pl.ds(QBLK, QBLK)
            if sub == 0:
                take = lambda cur_ref, nxt_ref, cols, i: cur_ref[rows if i == 0 else second, cols]
                prev_of = lambda p_ref, c_ref, cols: p_ref[:, cols]
                first, last = jnp.where(pair > 0, 0.0, NEG), 0.0
            else:
                take = lambda cur_ref, nxt_ref, cols, i: cur_ref[rows, cols] if i == 0 else nxt_ref[:, cols]
                prev_of = lambda p_ref, c_ref, cols: c_ref[pl.ds(0, QBLK), cols]
                first, last = 0.0, jnp.where(pair < npair - 1, 0.0, NEG)
            bias = jnp.concatenate([prev_bias + first, jnp.where(mask_c, 0.0, NEG), prev_bias + last], axis=1)
            s, dp, ls, dl, ops = [], [], [], [], []
            for h in range(HEADS):
                sl = pl.ds(HD * h, HD)
                one = pl.ds(HD * h, 1)
                q0, q1 = take(qc_ref, qn_ref, sl, 0), take(qc_ref, qn_ref, sl, 1)
                kp, kc = prev_of(kp_ref, kc_ref, sl), kc_ref[rows, sl]
                vp, vc = prev_of(vp_ref, vc_ref, sl), vc_ref[rows, sl]
                do0, do1 = take(doc_ref, don_ref, sl, 0), take(doc_ref, don_ref, sl, 1)
                do0b, do1b = do0.astype(BF16), do1.astype(BF16)
                s.append(jnp.concatenate([_nt(q0, kp), _nt(q0, kc), _nt(q1, kc)], axis=1))
                dp.append(jnp.concatenate([_nt(do0b, vp), _nt(do0b, vc), _nt(do1b, vc)], axis=1))
                dl0 = jnp.sum(do0 * take(atc_ref, atn_ref, sl, 0), axis=1, keepdims=True)
                dl1 = jnp.sum(do1 * take(atc_ref, atn_ref, sl, 1), axis=1, keepdims=True)
                dl.append(jnp.concatenate([jnp.broadcast_to(dl0, (QBLK, 2 * QBLK)), jnp.broadcast_to(dl1, (QBLK, QBLK))], axis=1))
                ls.append(jnp.concatenate([jnp.broadcast_to(take(lc_ref, ln_ref, one, 0), (QBLK, 2 * QBLK)),
                                           jnp.broadcast_to(take(lc_ref, ln_ref, one, 1), (QBLK, QBLK))], axis=1))
                ops.append((q0, q1, kp, kc, do0b, do1b))
            p = jnp.exp(jnp.stack(s) + bias - jnp.stack(ls))
            ds = (p * (jnp.stack(dp) - jnp.stack(dl))).astype(BF16)
            p = p.astype(BF16)
            for h in range(HEADS):
                sl = pl.ds(HD * h, HD)
                q0, q1, kp, kc, do0b, do1b = ops[h]
                dq_ref[rows, sl] = (_nn(ds[h, :, :QBLK], kp) + _nn(ds[h, :, QBLK:2 * QBLK], kc)).astype(BF16)
                dv_ref[rows, sl] = (_tn(p[h, :, QBLK:2 * QBLK], do0b) + _tn(p[h, :, 2 * QBLK:], do1b)).astype(BF16)
                dk_ref[rows, sl] = (_tn(ds[h, :, QBLK:2 * QBLK], q0) + _tn(ds[h, :, 2 * QBLK:], q1)).astype(BF16)

    cur = pl.BlockSpec((2 * QBLK, AW), lambda r, n: (n, r))
    prev = pl.BlockSpec((QBLK, AW), lambda r, n: (jnp.maximum(2 * n - 1, 0), r))
    nxt = pl.BlockSpec((QBLK, AW), lambda r, n: (jnp.minimum(2 * n + 2, nb - 1), r))
    return pl.pallas_call(
        body, name=f"attn_bwd_d{d}", grid=(d, npair),
        in_specs=[cur, nxt, prev, cur, prev, cur, cur, nxt, cur, nxt, cur, nxt], out_specs=[cur, cur, cur],
        out_shape=[jax.ShapeDtypeStruct((L, d * AW), BF16)] * 3,
        compiler_params=_cparams(("parallel", "parallel")),
    )(q, q, k, k, v, v, do, do, at, at, lse, lse)


def _attn_merge(outs, lses):
    tm = ROPE_TM

    def body(o1, o4, o16, l1, l4, l16, at_ref, ls_ref, at4, ls4, at16, ls16, *flat):
        so4, so16, sl4, sl16, sa, sl = _slab_groups(flat)
        _undilate(o4, so4, 4, tm)
        _undilate(o16, so16, 16, tm)
        _undilate(l4, sl4, 4, tm)
        _undilate(l16, sl16, 16, tm)
        for j in range(SLABS):
            cols = pl.ds(j * 128, 128)
            a, b, c = l1[:, cols], sl4[j][...], sl16[j][...]
            m = jnp.maximum(jnp.maximum(a, b), c)
            e1, e2, e3 = jnp.exp(a - m), jnp.exp(b - m), jnp.exp(c - m)
            s = e1 + e2 + e3
            inv = 1.0 / s
            attn = (e1 * inv) * o1[:, cols] + (e2 * inv) * so4[j][...] + (e3 * inv) * so16[j][...]
            lse = m + jnp.log(s)
            at_ref[:, cols] = attn
            ls_ref[:, cols] = lse
            sa[j][...] = attn
            sl[j][...] = lse
        _dilate(at4, sa, 4, tm)
        _dilate(at16, sa, 16, tm)
        _dilate(ls4, sl, 4, tm)
        _dilate(ls16, sl, 16, tm)

    specs = [_dil_spec(tm, d) for d in DILATIONS]
    tok = specs[0]
    return pl.pallas_call(
        body, name="attn_merge", grid=(T // tm,),
        in_specs=specs + specs, out_specs=[tok, tok, specs[1], specs[1], specs[2], specs[2]],
        out_shape=[jax.ShapeDtypeStruct((T, AW), F32)] * 2 + [jax.ShapeDtypeStruct((T // 4, 4 * AW), F32)] * 2
        + [jax.ShapeDtypeStruct((T // 16, 16 * AW), F32)] * 2,
        scratch_shapes=_slab_scratch(6, tm),
        compiler_params=_cparams(("parallel",)),
    )(*outs, *lses)


CONV_TM = 512
HALO = 8


def _conv_pre(ext, w, b):
    y = b + w[3] * ext
    for kk in range(1, CONV_K):
        y = y + w[3 - kk] * pltpu.roll(ext, kk, 0)
    return y


def _rows_to_block(rows, n, width):
    ri = lax.broadcasted_iota(jnp.int32, (n, width), 0)
    out = jnp.zeros((n, width), F32)
    for j, r in enumerate(rows):
        out = out + jnp.where(ri == j, r, 0.0)
    return out


def _conv_bwd(xbc, dact, ddt, w, b):
    nblk = T // CONV_TM
    per = CONV_TM // HALO

    def body(x_ref, xb_ref, xa_ref, g_ref, ga_ref, ddt_ref, w_ref, b_ref, dx_ref, dw_ref):
        i = pl.program_id(0)
        wv = [w_ref[pl.ds(j, 1), :] for j in range(CONV_K)]
        before = jnp.where(i > 0, xb_ref[...], 0.0)
        last = i == nblk - 1
        after = jnp.where(last, 0.0, xa_ref[...])
        g_after = jnp.where(last, 0.0, ga_ref[...])
        ext = jnp.concatenate([before, x_ref[...], after], axis=0)
        y = _conv_pre(ext, wv, b_ref[...])[HALO:]
        sg = _sigmoid(y)
        dy = jnp.concatenate([g_ref[...], g_after], axis=0) * (sg * (1.0 + y * (1.0 - sg)))
        n = CONV_TM + HALO
        dx = wv[3] * dy
        for kk in range(1, CONV_K):
            dx = dx + wv[3 - kk] * pltpu.roll(dy, n - kk, 0)
        dx_ref[:, pl.ds(0, CONV_CH)] = dx[:CONV_TM].astype(BF16)
        dx_ref[:, pl.ds(CONV_CH, DT_PAD)] = ddt_ref[...].astype(BF16)
        dyc = dy[:CONV_TM]
        rows = [jnp.sum(dyc * (pltpu.roll(ext, 3 - j, 0) if j < 3 else ext)[HALO:HALO + CONV_TM], axis=0, keepdims=True)
                for j in range(CONV_K)]
        rows.append(jnp.sum(dyc, axis=0, keepdims=True))
        part = _rows_to_block(rows, 8, CONV_CH)

        @pl.when(i == 0)
        def _():
            dw_ref[...] = jnp.zeros_like(dw_ref)
        dw_ref[...] += part

    blk = pl.BlockSpec((CONV_TM, CONV_CH), lambda i: (i, 0))
    hb = pl.BlockSpec((HALO, CONV_CH), lambda i: (jnp.maximum(i * per - 1, 0), 0))
    ha = pl.BlockSpec((HALO, CONV_CH), lambda i: (jnp.minimum((i + 1) * per, T // HALO - 1), 0))
    return pl.pallas_call(
        body, name="conv_bwd", grid=(nblk,),
        in_specs=[blk, hb, ha, blk, ha, pl.BlockSpec((CONV_TM, DT_PAD), lambda i: (i, 0)),
                  pl.BlockSpec((CONV_K, CONV_CH), lambda i: (0, 0)), pl.BlockSpec((1, CONV_CH), lambda i: (0, 0))],
        out_specs=[pl.BlockSpec((CONV_TM, CONV_CH + DT_PAD), lambda i: (i, 0)), pl.BlockSpec((8, CONV_CH), lambda i: (0, 0))],
        out_shape=[jax.ShapeDtypeStruct((T, CONV_CH + DT_PAD), BF16), jax.ShapeDtypeStruct((8, CONV_CH), F32)],
        compiler_params=_cparams(("arbitrary",)),
    )(xbc, xbc, xbc, dact, dact, ddt, w, b)


def _pick(mat, h):
    lane = lax.broadcasted_iota(jnp.int32, mat.shape, 1)
    return jnp.sum(jnp.where(lane == h, mat, 0.0), axis=1, keepdims=True)


def _heads(fn):
    return jnp.stack([fn(h) for h in range(HEADS)])


def _ssd_prep(dt_ref, bias_ref, alog_ref, dsk_ref, b_ref, c_ref, xs_ref, state_ref, cst):
    li = lax.broadcasted_iota(jnp.int32, (CHUNK, CHUNK), 0)
    si = lax.broadcasted_iota(jnp.int32, (CHUNK, CHUNK), 1)
    tri = li >= si
    dtp = dt_ref[...] + bias_ref[...]
    dt = _softplus(dtp)
    A = -jnp.exp(alog_ref[...])
    a = dt * A
    cs = jnp.dot(tri.astype(F32), a, precision=HIGHEST, preferred_element_type=F32)
    cst[...] = cs.T
    Bm = b_ref[...].astype(BF16)
    Cm = c_ref[...].astype(BF16)
    cb = _nt(Cm, Bm)
    dskv = dsk_ref[...]
    cs_col = _heads(lambda h: _pick(cs, h))
    cs_row = _heads(lambda h: cst[pl.ds(h, 1), :])
    dt_col = _heads(lambda h: _pick(dt, h))
    dsk_col = _heads(lambda h: _pick(dskv, h))
    lam = jnp.exp(jnp.where(tri, cs_col - cs_row, NEG))
    x = _heads(lambda h: xs_ref[:, pl.ds(HD * h, HD)])
    xdt = x * dt_col
    prev = _heads(lambda h: state_ref[pl.ds(HD * h, HD), :])
    lane = lax.broadcasted_iota(jnp.int32, (1, 1, CHUNK), 2)
    cl = jnp.sum(jnp.where(lane == CHUNK - 1, cs_row, 0.0), axis=2, keepdims=True)
    f = jnp.exp(cl - cs_col)
    return dict(li=li, si=si, dtp=dtp, dt=dt, A=A, Bm=Bm, Cm=Cm, cb=cb, cs_col=cs_col, dt_col=dt_col, dsk_col=dsk_col,
                lam=lam, x=x, xdt=xdt, prev=prev, cl=cl, f=f)


def _ssd_fwd(xbcdt, conv_w, conv_b, bias, alog, dsk, qkvz, attn, gs):
    nc = T // CHUNK
    R = SSD_PER * CHUNK
    per = R // HALO

    def body(xbc_ref, halo_ref, cw_ref, cb_ref, dt_ref, bias_ref, alog_ref, dsk_ref, z_ref, at_ref, gs_ref,
             y_ref, st_ref, cat_ref, act_ref, state, cst):
        @pl.when(pl.program_id(0) == 0)
        def _():
            state[...] = jnp.zeros_like(state)
        for sub in range(SSD_PER):
            rows = pl.ds(sub * CHUNK, CHUNK)
            st_ref[sub] = state[...]
            halo = (jnp.where(pl.program_id(0) > 0, halo_ref[...], 0.0) if sub == 0
                    else xbc_ref[pl.ds(sub * CHUNK - HALO, HALO), :])
            one_chunk(halo, xbc_ref.at[rows, :], cw_ref, cb_ref, dt_ref.at[rows, :], bias_ref, alog_ref, dsk_ref,
                      z_ref.at[rows, :], at_ref.at[rows, :], gs_ref, y_ref.at[rows, :], cat_ref.at[rows, :],
                      act_ref.at[rows, :], state, cst)

    def one_chunk(halo, xbc_ref, cw_ref, cb_ref, dt_ref, bias_ref, alog_ref, dsk_ref, z_ref, at_ref, gs_ref,
                  y_ref, cat_ref, act_ref, state, cst):
        pre = _conv_pre(jnp.concatenate([halo, xbc_ref[...]], axis=0),
                        [cw_ref[pl.ds(j, 1), :] for j in range(CONV_K)], cb_ref[...])[HALO:]
        act_ref[...] = pre * _sigmoid(pre)
        xs_ref, b_ref, c_ref = (act_ref.at[:, pl.ds(0, AW)], act_ref.at[:, pl.ds(AW, NS)],
                                act_ref.at[:, pl.ds(AW + NS, NS)])
        s = _ssd_prep(dt_ref, bias_ref, alog_ref, dsk_ref, b_ref, c_ref, xs_ref, state, cst)
        Bm, Cm, prev = s["Bm"], s["Cm"], s["prev"]
        g = (s["cb"] * s["lam"]).astype(BF16)
        xdtb = s["xdt"].astype(BF16)
        prevb = prev.astype(BF16)
        y = _heads(lambda h: _nn(g[h], xdtb[h])) + _heads(lambda h: _nt(Cm, prevb[h])) * jnp.exp(s["cs_col"])
        y = y + s["dsk_col"] * s["x"]
        xf = (s["xdt"] * s["f"]).astype(BF16)
        new = prev * jnp.exp(s["cl"]) + _heads(lambda h: _tn(xf[h], Bm))
        for h in range(HEADS):
            y_ref[:, pl.ds(HD * h, HD)] = y[h]
            state[pl.ds(HD * h, HD), :] = new[h]
        z = z_ref[...]
        gi = y_ref[...] * (z * _sigmoid(z))
        cat_ref[:, pl.ds(0, AW)] = at_ref[...].astype(BF16)
        cat_ref[:, pl.ds(AW, AW)] = (gi * _rstd(gi) * gs_ref[...]).astype(BF16)

    vec = pl.BlockSpec((1, DT_PAD), lambda c: (0, 0))
    blk = pl.BlockSpec((R, AW), lambda c: (c, 0))
    return pl.pallas_call(
        body, name="ssd_fwd", grid=(nc // SSD_PER,),
        in_specs=[pl.BlockSpec((R, CONV_CH), lambda c: (c, 0)),
                  pl.BlockSpec((HALO, CONV_CH), lambda c: (jnp.maximum(c * per - 1, 0), 0)),
                  pl.BlockSpec((CONV_K, CONV_CH), lambda c: (0, 0)), pl.BlockSpec((1, CONV_CH), lambda c: (0, 0)),
                  pl.BlockSpec((R, DT_PAD), lambda c: (c, 6)),
                  vec, vec, vec, pl.BlockSpec((R, AW), lambda c: (c, 3)), blk, pl.BlockSpec((1, AW), lambda c: (0, 0))],
        out_specs=[blk, pl.BlockSpec((SSD_PER, AW, NS), lambda c: (c, 0, 0)), pl.BlockSpec((R, D), lambda c: (c, 0)),
                   pl.BlockSpec((R, CONV_CH), lambda c: (c, 0))],
        out_shape=[jax.ShapeDtypeStruct((T, AW), F32), jax.ShapeDtypeStruct((nc, AW, NS), F32),
                   jax.ShapeDtypeStruct((T, D), BF16), jax.ShapeDtypeStruct((T, CONV_CH), F32)],
        scratch_shapes=[pltpu.VMEM((AW, NS), F32), pltpu.VMEM((CHUNK, CHUNK), F32)],
        compiler_params=_cparams(("arbitrary",)),
    )(xbcdt, xbcdt, conv_w, conv_b, xbcdt, bias, alog, dsk, qkvz, attn, gs)


def _ssd_bwd(act, xbcdt, bias, alog, dsk, states, y_ssd, qkvz, dcat, gs):
    nc = T // CHUNK

    def body(xs_ref, b_ref, c_ref, dt_ref, bias_ref, alog_ref, dsk_ref, st_ref, y_ref, z_ref, dyn_ref, gs_ref,
             dact_ref, ddt_ref, par_ref, dz_ref, dgs_ref, dstate, cst, dy_ref):
        @pl.when(pl.program_id(0) == 0)
        def _():
            dstate[...] = jnp.zeros_like(dstate)
            par_ref[...] = jnp.zeros_like(par_ref)
            dgs_ref[...] = jnp.zeros_like(dgs_ref)
        for sub in reversed(range(SSD_PER)):
            rows = pl.ds(sub * CHUNK, CHUNK)
            one_chunk(xs_ref.at[rows, :], b_ref.at[rows, :], c_ref.at[rows, :], dt_ref.at[rows, :], bias_ref, alog_ref,
                      dsk_ref, st_ref.at[sub], y_ref.at[rows, :], z_ref.at[rows, :], dyn_ref.at[rows, :], gs_ref,
                      dact_ref.at[rows, :], ddt_ref.at[rows, :], par_ref, dz_ref.at[rows, :], dgs_ref, dstate, cst, dy_ref)

    def one_chunk(xs_ref, b_ref, c_ref, dt_ref, bias_ref, alog_ref, dsk_ref, st_ref, y_ref, z_ref, dyn_ref, gs_ref,
                  dact_ref, ddt_ref, par_ref, dz_ref, dgs_ref, dstate, cst, dy_ref):
        z, yv, dyn = z_ref[...], y_ref[...], dyn_ref[...]
        sg = _sigmoid(z)
        sz = z * sg
        gi = yv * sz
        rg = _rstd(gi)
        ng = gi * rg
        dgi = _rms_bwd(ng, rg, gs_ref[...], dyn)
        dy_ref[...] = dgi * sz
        dz_ref[...] = dgi * yv * (sg * (1.0 + z * (1.0 - sg)))
        dgs_ref[...] += _colsum(dyn * ng)
        s = _ssd_prep(dt_ref, bias_ref, alog_ref, dsk_ref, b_ref, c_ref, xs_ref, st_ref, cst)
        Bm, Cm, prev, lam, x, xdt, f, cl = s["Bm"], s["Cm"], s["prev"], s["lam"], s["x"], s["xdt"], s["f"], s["cl"]
        lane = lax.broadcasted_iota(jnp.int32, (1, DT_PAD), 1)
        row = lax.broadcasted_iota(jnp.int32, (1, CHUNK, 1), 1)
        g = s["cb"] * lam
        gb, xdtb, prevb = g.astype(BF16), xdt.astype(BF16), prev.astype(BF16)
        dy = _heads(lambda h: dy_ref[:, pl.ds(HD * h, HD)])
        dyb = dy.astype(BF16)
        dnew = _heads(lambda h: dstate[pl.ds(HD * h, HD), :])
        dnewb = dnew.astype(BF16)
        E = jnp.exp(s["cs_col"])
        ecl = jnp.exp(cl)
        dG = _heads(lambda h: _nt(dyb[h], xdtb[h]))
        dxdt = _heads(lambda h: _tn(gb[h], dyb[h]))
        Yo = _heads(lambda h: _nt(Cm, prevb[h]))
        W = _heads(lambda h: _nt(Bm, dnewb[h]))
        dcb = jnp.sum(dG * lam, axis=0)
        Mm = dG * g
        col_sums = jnp.sum(Mm, axis=1, keepdims=True)
        dYo = (dy * E).astype(BF16)
        dxdt = dxdt + W * f
        dF = jnp.sum(W * xdt, axis=2, keepdims=True) * f
        dcl = jnp.sum(dnew * prev, axis=(1, 2), keepdims=True) * ecl + jnp.sum(dF, axis=1, keepdims=True)
        dcs = (jnp.sum(Mm, axis=2, keepdims=True) + jnp.sum(dy * Yo, axis=2, keepdims=True) * E - dF
               + jnp.where(row == CHUNK - 1, dcl, 0.0))
        ddt_x = jnp.sum(dxdt * x, axis=2, keepdims=True)
        dD = jnp.sum(dy * x, axis=(1, 2), keepdims=True)
        dx = s["dsk_col"] * dy + dxdt * s["dt_col"]
        xfb = (xdt * f).astype(BF16)
        dprev = _heads(lambda h: _tn(dYo[h], Cm)) + dnew * ecl
        dcbb = dcb.astype(BF16)
        dC = _nn(dcbb, Bm)
        dB = _tn(dcbb, Cm)
        dcs_mat = -_rows_to_block([col_sums[h] for h in range(HEADS)], CHUNK, CHUNK).T
        ddt_mat = jnp.zeros((CHUNK, DT_PAD), F32)
        dD_row = jnp.zeros((1, DT_PAD), F32)
        for h in range(HEADS):
            sl = pl.ds(HD * h, HD)
            dC = dC + _nn(dYo[h], prevb[h])
            dB = dB + _nn(xfb[h], dnewb[h])
            dcs_mat = dcs_mat + jnp.where(lane == h, dcs[h], 0.0)
            ddt_mat = ddt_mat + jnp.where(lane == h, ddt_x[h], 0.0)
            dD_row = dD_row + jnp.where(lane == h, dD[h], 0.0)
            dact_ref[:, sl] = dx[h]
            dstate[sl, :] = dprev[h]
        dact_ref[:, pl.ds(AW, NS)] = dB
        dact_ref[:, pl.ds(AW + NS, NS)] = dC
        da = jnp.dot((s["li"] <= s["si"]).astype(F32), dcs_mat, precision=HIGHEST, preferred_element_type=F32)
        ddtp = jnp.where(lane < HEADS, (ddt_mat + da * s["A"]) * _sigmoid(s["dtp"]), 0.0)
        ddt_ref[...] = ddtp
        dalog = jnp.where(lane < HEADS, jnp.sum(da * s["dt"], axis=0, keepdims=True) * s["A"], 0.0)
        par_ref[...] += _rows_to_block([jnp.sum(ddtp, axis=0, keepdims=True), dalog, dD_row], 8, DT_PAD)

    vec = pl.BlockSpec((1, DT_PAD), lambda c: (0, 0))
    nstep = nc // SSD_PER
    rev = lambda c: nstep - 1 - c
    R = SSD_PER * CHUNK
    return pl.pallas_call(
        body, name="ssd_bwd", grid=(nstep,),
        in_specs=[pl.BlockSpec((R, AW), lambda c: (rev(c), 0)), pl.BlockSpec((R, NS), lambda c: (rev(c), 4)),
                  pl.BlockSpec((R, NS), lambda c: (rev(c), 5)), pl.BlockSpec((R, DT_PAD), lambda c: (rev(c), 6)),
                  vec, vec, vec,
                  pl.BlockSpec((SSD_PER, AW, NS), lambda c: (rev(c), 0, 0)), pl.BlockSpec((R, AW), lambda c: (rev(c), 0)),
                  pl.BlockSpec((R, AW), lambda c: (rev(c), 3)), pl.BlockSpec((R, AW), lambda c: (rev(c), 1)),
                  pl.BlockSpec((1, AW), lambda c: (0, 0))],
        out_specs=[pl.BlockSpec((R, CONV_CH), lambda c: (rev(c), 0)), pl.BlockSpec((R, DT_PAD), lambda c: (rev(c), 0)),
                   pl.BlockSpec((8, DT_PAD), lambda c: (0, 0)), pl.BlockSpec((R, AW), lambda c: (rev(c), 0)),
                   pl.BlockSpec((1, AW), lambda c: (0, 0))],
        out_shape=[jax.ShapeDtypeStruct((T, CONV_CH), F32), jax.ShapeDtypeStruct((T, DT_PAD), F32),
                   jax.ShapeDtypeStruct((8, DT_PAD), F32), jax.ShapeDtypeStruct((T, AW), F32),
                   jax.ShapeDtypeStruct((1, AW), F32)],
        scratch_shapes=[pltpu.VMEM((AW, NS), F32), pltpu.VMEM((CHUNK, CHUNK), F32), pltpu.VMEM((CHUNK, AW), F32)],
        compiler_params=_cparams(("arbitrary",)),
    )(act, act, act, xbcdt, bias, alog, dsk, states, y_ssd, qkvz, dcat, gs)


def _place():
    return lax.axis_index("x"), lax.axis_index("y"), lax.axis_index("c")


def _slot(px, py, pc):
    return 4 * px + 2 * py + pc


SLAB_ROWS = 24


def _slab_pack(parts, name):
    n = len(parts)

    def body(*refs):
        slab = refs[n]
        slab[...] = jnp.zeros_like(slab)
        for ref, (arr, row) in zip(refs[:n], parts):
            slab[pl.ds(row, arr.shape[0]), pl.ds(0, arr.shape[1])] = ref[...]

    vm = pl.BlockSpec(memory_space=pltpu.VMEM)
    return pl.pallas_call(
        body, name=name, in_specs=[vm] * n, out_specs=vm, out_shape=jax.ShapeDtypeStruct((SLAB_ROWS, D), F32),
    )(*[a for a, _ in parts])


_HBM = pl.BlockSpec(memory_space=pltpu.HBM)
_SEM = pl.BlockSpec(memory_space=pltpu.SEMAPHORE)
_EFFECT = pltpu.SideEffectType.DATAFLOW_SIDE_EFFECTING


def _peers(x, y, c):
    out = []
    for kk in range(1, N_DEV):
        fx, fy, fc = kk >> 2 & 1, kk >> 1 & 1, kk & 1
        out.append((1 - x if fx else x, 1 - y if fy else y, 1 - c if fc else c))
    return out


def _send_start(src, per_peer, name, dep):
    (handles, token) = _send_start_many([src], per_peer, name, dep)
    return handles, token


def _near_peers(x, y, c):
    return [(x, y, 1 - c), (1 - x, y, c), (x, 1 - y, c), (1 - x, 1 - y, c)]


def _send_start_many(srcs, per_peer, name, dep, peers=_peers, npeers=N_DEV - 1):
    n = len(srcs)

    def body(*refs):
        src_refs, land_refs = refs[:n], refs[n:2 * n]
        send_sems, recv_sems = refs[2 * n + 1], refs[2 * n + 2]
        token = refs[-1]
        x, y, c = _place()
        mine = _slot(x, y, c)
        for a in range(n):
            for kk, peer in enumerate(peers(x, y, c)):
                pltpu.make_async_remote_copy(
                    src_ref=src_refs[a].at[_slot(*peer)] if per_peer else src_refs[a], dst_ref=land_refs[a].at[mine],
                    send_sem=send_sems.at[a * npeers + kk], recv_sem=recv_sems.at[a * npeers + kk],
                    device_id=peer, device_id_type=MESH).start()
        token[...] = jnp.zeros_like(token)

    lands = [lax.empty((N_DEV,) + tuple(s.shape[1:] if per_peer else s.shape), s.dtype) for s in srcs]
    hbm = lambda t: pltpu.with_memory_space_constraint(t, pltpu.HBM)
    outs = pl.pallas_call(
        body, name=name,
        out_shape=(pltpu.SemaphoreType.DMA((n * npeers,)), pltpu.SemaphoreType.DMA((n * npeers,)),
                   *[pltpu.HBM(s.shape, s.dtype) for s in srcs], *[pltpu.HBM(l.shape, l.dtype) for l in lands],
                   jax.ShapeDtypeStruct((8, 128), F32)),
        in_specs=(*[_HBM] * (2 * n), _ANY),
        out_specs=(_SEM, _SEM, *[_HBM] * (2 * n), pl.BlockSpec(memory_space=pltpu.VMEM)),
        input_output_aliases={i: 2 + i for i in range(2 * n)},
        compiler_params=pltpu.CompilerParams(has_side_effects=_EFFECT),
    )(*[hbm(s) for s in srcs], *[hbm(l) for l in lands], dep)
    return (outs[0], outs[1], list(outs[2:2 + n]), list(outs[2 + n:2 + 2 * n])), outs[-1]


def _send_wait(handles, after, name):
    srcs, lands = _send_wait_many(handles, after, name)
    return srcs[0], lands[0]


def _send_wait_many(handles, after, name, npeers=N_DEV - 1):
    send_sems, recv_sems, src_thrus, land_thrus = handles
    n = len(src_thrus)

    def body(*refs):
        land_refs = refs[n:2 * n]
        send_sems, recv_sems = refs[2 * n], refs[2 * n + 1]
        me = _place()
        for a in range(n):
            for kk in range(npeers):
                cp = pltpu.make_async_remote_copy(
                    src_ref=land_refs[a].at[0], dst_ref=land_refs[a].at[0],
                    send_sem=send_sems.at[a * npeers + kk], recv_sem=recv_sems.at[a * npeers + kk],
                    device_id=me, device_id_type=MESH)
                cp.wait_send()
                cp.wait_recv()

    both = list(src_thrus) + list(land_thrus)
    outs = pl.pallas_call(
        body, name=name,
        out_shape=tuple(pltpu.HBM(t.shape, t.dtype) for t in both),
        in_specs=(*[_HBM] * (2 * n), _SEM, _SEM, _ANY), out_specs=tuple([_HBM] * (2 * n)),
        input_output_aliases={i: i for i in range(2 * n)},
        compiler_params=pltpu.CompilerParams(has_side_effects=_EFFECT),
    )(*both, send_sems, recv_sems, after)
    return list(outs[:n]), list(outs[n:])


def _forward_start(lands, name, dep):
    n = len(lands)

    def body(*refs):
        land_refs = refs[:n]
        send_sems, recv_sems = refs[n + 1], refs[n + 2]
        token = refs[-1]
        x, y, c = _place()
        for a in range(n):
            for j, chip in enumerate([(1 - x, y), (x, 1 - y), (1 - x, 1 - y)]):
                blk = land_refs[a].at[_slot(*chip, c)]
                pltpu.make_async_remote_copy(
                    src_ref=blk, dst_ref=blk, send_sem=send_sems.at[a * 3 + j], recv_sem=recv_sems.at[a * 3 + j],
                    device_id=(x, y, 1 - c), device_id_type=MESH).start()
        token[...] = jnp.zeros_like(token)

    outs = pl.pallas_call(
        body, name=name,
        out_shape=(pltpu.SemaphoreType.DMA((n * 3,)), pltpu.SemaphoreType.DMA((n * 3,)),
                   *[pltpu.HBM(l.shape, l.dtype) for l in lands], jax.ShapeDtypeStruct((8, 128), F32)),
        in_specs=(*[_HBM] * n, _ANY), out_specs=(_SEM, _SEM, *[_HBM] * n, pl.BlockSpec(memory_space=pltpu.VMEM)),
        input_output_aliases={i: 2 + i for i in range(n)},
        compiler_params=pltpu.CompilerParams(has_side_effects=_EFFECT),
    )(*lands, dep)
    return (outs[0], outs[1], list(outs[2:2 + n])), outs[-1]


def _forward_wait(handles, after, name):
    send_sems, recv_sems, land_thrus = handles
    n = len(land_thrus)

    def body(*refs):
        land_refs = refs[:n]
        send_sems, recv_sems = refs[n], refs[n + 1]
        me = _place()
        for a in range(n):
            for j in range(3):
                cp = pltpu.make_async_remote_copy(
                    src_ref=land_refs[a].at[0], dst_ref=land_refs[a].at[0],
                    send_sem=send_sems.at[a * 3 + j], recv_sem=recv_sems.at[a * 3 + j], device_id=me, device_id_type=MESH)
                cp.wait_send()
                cp.wait_recv()

    outs = pl.pallas_call(
        body, name=name,
        out_shape=tuple(pltpu.HBM(t.shape, t.dtype) for t in land_thrus),
        in_specs=(*[_HBM] * n, _SEM, _SEM, _ANY), out_specs=tuple([_HBM] * n),
        input_output_aliases={i: i for i in range(n)},
        compiler_params=pltpu.CompilerParams(has_side_effects=_EFFECT),
    )(*land_thrus, send_sems, recv_sems, after)
    return list(outs)


def _sum_slots(land, name):
    _, R, C = land.shape
    tm = R if R <= 512 else 512

    def body(x_ref, o_ref):
        acc = x_ref[0].astype(F32)
        for j in range(1, N_DEV):
            acc = acc + x_ref[j].astype(F32)
        o_ref[...] = acc

    return pl.pallas_call(
        body, name=name, grid=(R // tm,),
        in_specs=[pl.BlockSpec((N_DEV, tm, C), lambda i: (0, i, 0))], out_specs=pl.BlockSpec((tm, C), lambda i: (i, 0)),
        out_shape=jax.ShapeDtypeStruct((R, C), F32), compiler_params=_cparams(("parallel",)),
    )(land)


def _adam_math(w, g, m, v):
    m2 = ADAM_B1 * m + (1.0 - ADAM_B1) * g
    v2 = ADAM_B2 * v + (1.0 - ADAM_B2) * (g * g)
    m_hat = m2 / (1.0 - ADAM_B1 ** ADAM_STEP)
    v_hat = v2 / (1.0 - ADAM_B2 ** ADAM_STEP)
    delta = -ADAM_LR * (m_hat / (jnp.sqrt(v_hat) + ADAM_EPS) + ADAM_WD * w)
    return delta, m2, v2


def _adamw(w, g, m, v, name):
    R, C = w.shape
    tm = R if R <= 512 else 256
    return _rowwise(lambda w, g, m, v: (_adam_math(w, g, m, v), ()), [w, g, m, v], [], [(C, F32)] * 3, [], tm=tm, name=name)


def _adamw_slots(land, w, m, v, name):
    _, R, C = land.shape
    tm = R if R <= 256 else 256

    def body(x_ref, w_ref, m_ref, v_ref, g_ref, d_ref, mo_ref, vo_ref):
        g = x_ref[0].astype(F32)
        for j in range(1, N_DEV):
            g = g + x_ref[j].astype(F32)
        d, m2, v2 = _adam_math(w_ref[...], g, m_ref[...], v_ref[...])
        g_ref[...] = g
        d_ref[...] = d
        mo_ref[...] = m2
        vo_ref[...] = v2

    row = pl.BlockSpec((tm, C), lambda i: (i, 0))
    return pl.pallas_call(
        body, name=name, grid=(R // tm,),
        in_specs=[pl.BlockSpec((N_DEV, tm, C), lambda i: (0, i, 0)), row, row, row], out_specs=[row] * 4,
        out_shape=[jax.ShapeDtypeStruct((R, C), F32)] * 4, compiler_params=_cparams(("parallel",)),
    )(land, w, m, v)


def _adamw_small(slab, slab_rows, g_conv_w, ws, ms, vs):
    n = len(ws)

    def body(*refs):
        slab_ref, gc_ref = refs[0], refs[1]
        w_refs, m_refs, v_refs = refs[2:2 + n], refs[2 + n:2 + 2 * n], refs[2 + 2 * n:2 + 3 * n]
        outs = refs[2 + 3 * n:]
        loss_ref = outs[0]
        g_out, d_out, m_out, v_out = (outs[1 + i * n:1 + (i + 1) * n] for i in range(4))
        loss_ref[...] = jnp.sum(slab_ref[pl.ds(6, 1), :], axis=1, keepdims=True)
        for i in range(n):
            g = gc_ref[...] if i == n - 1 else slab_ref[pl.ds(slab_rows[i], 1), pl.ds(0, ws[i].shape[1])]
            d, m2, v2 = _adam_math(w_refs[i][...], g, m_refs[i][...], v_refs[i][...])
            g_out[i][...] = g
            d_out[i][...] = d
            m_out[i][...] = m2
            v_out[i][...] = v2

    vm = pl.BlockSpec(memory_space=pltpu.VMEM)
    shapes = [jax.ShapeDtypeStruct(w.shape, F32) for w in ws]
    outs = pl.pallas_call(
        body, name="adamw_small", in_specs=[vm] * (2 + 3 * n), out_specs=[vm] * (1 + 4 * n),
        out_shape=[jax.ShapeDtypeStruct((1, 1), F32)] + shapes * 4,
    )(slab, g_conv_w, *ws, *ms, *vs)
    return outs[0], outs[1:1 + n], outs[1 + n:1 + 2 * n], outs[1 + 2 * n:1 + 3 * n], outs[1 + 3 * n:]


SMALL = ["norm_mix_pre", "norm_mix_post", "norm_mlp_pre", "norm_mlp_post", "norm_ple_post",
         "conv_b", "ssd_norm_g", "dt_bias", "a_log", "d_skip"]


def _pad_row(v, width=D):
    return jnp.pad(v, ((0, 0), (0, width - v.shape[1])))


def kernel(x, p, positions, norm_mix_pre, norm_mix_post, w_in, conv_w, conv_b, dt_bias, a_log, d_skip, ssd_norm_g, w_out, norm_mlp_pre, norm_mlp_post, w_up, w_down, w_ple_gate, w_ple_proj, norm_ple_post, loss_target, m_norm_mix_pre, m_norm_mix_post, m_w_in, m_conv_w, m_conv_b, m_dt_bias, m_a_log, m_d_skip, m_ssd_norm_g, m_w_out, m_norm_mlp_pre, m_norm_mlp_post, m_w_up, m_w_down, m_w_ple_gate, m_w_ple_proj, m_norm_ple_post, v_norm_mix_pre, v_norm_mix_post, v_w_in, v_conv_w, v_conv_b, v_dt_bias, v_a_log, v_d_skip, v_ssd_norm_g, v_w_out, v_norm_mlp_pre, v_norm_mlp_post, v_w_up, v_w_down, v_w_ple_gate, v_w_ple_proj, v_norm_ple_post):
    args = dict(locals())
    x2, p2, tgt = x[0], p[0, 0], loss_target[0]
    g1, g2, g3, g4, g5 = norm_mix_pre, norm_mix_post, norm_mlp_pre, norm_mlp_post, norm_ple_post

    me = _slot(*_place())
    pack_in = jnp.pad(w_in[0].T, ((0, W_IN_SHARD_PAD - W_IN_SHARD), (0, 0))).astype(BF16)
    conv_pack = jnp.pad(conv_w[0], ((0, 4), (0, 32)))
    in_handles, tok_in0 = _send_start_many([pack_in, conv_pack], False, "gather_in_start", g1, peers=_near_peers, npeers=4)
    one = 1.0 + tok_in0[0, 0]
    rest = [(w_out[0] * one).astype(BF16), (w_up[0].T * one).astype(BF16), (w_down[0] * one).astype(BF16),
            (w_ple_gate[0] * one).astype(BF16),
            (w_ple_proj[0].T.reshape(32, D) * one).astype(BF16)]

    inv_freq = ROPE_THETA ** (-jnp.arange(HD // 2, dtype=F32) * 2.0 / HD)
    pos = positions[0] + tok_in0[0, 0].astype(jnp.int32)
    ang = pos.astype(F32)[:, None] * jnp.tile(inv_freq, 4)
    cos128 = jnp.cos(ang)
    sin128 = jnp.sin(ang) * jnp.tile(jnp.concatenate([-jnp.ones(HD // 2, F32), jnp.ones(HD // 2, F32)]), 2)

    bias_w, alog_w, dsk_w = _pad_row(dt_bias, DT_PAD), _pad_row(a_log, DT_PAD), _pad_row(d_skip, DT_PAD)

    (u1,) = _rowwise(lambda a, g: ((a * _rstd(a) * g,), ()), [x2], [g1], [(D, BF16)], [], tm=512, name="norm_x",
                     deps=[cos128, sin128])
    p2b = (p2 * one).astype(BF16)

    in_back, in_land = _send_wait_many(in_handles, u1, "gather_in_wait", npeers=4)
    fw_handles, tok_fw = _forward_start(in_land, "gather_in_forward", u1)
    in_land = _forward_wait(fw_handles, tok_fw, "gather_in_forward_wait")
    gin = lax.dynamic_update_slice(in_land[0], in_back[0][None], (me, 0, 0))
    gconv = lax.dynamic_update_slice(in_land[1], in_back[1][None], (me, 0, 0))
    rest_handles, tok_rest = _send_start_many(rest, False, "gather_rest_start", gconv)
    w_inT = gin[:, :W_IN_SHARD].reshape(IN_W, D)
    w_qkvzT = w_inT[:4 * AW]
    w_xbcdtT = jnp.pad(w_inT[4 * AW:], ((0, DT_PAD - HEADS), (0, 0)))
    conv_full = gconv[:, :CONV_K, :96].transpose(1, 0, 2).reshape(CONV_K, CONV_CH)
    qkvz, xbcdt = _mm_rows(lambda a, b: ((a, b), ()), [(u1, w_qkvzT, True), (u1, w_xbcdtT, True)], [], [],
                           [(4 * AW, F32), (CONV_CH + DT_PAD, F32)], [], tm=512, name="proj_in", deps=[tok_rest])

    qkv = _rope_fwd(qkvz, cos128, sin128)
    qkv = [qkv[3 * i:3 * i + 3] for i in range(len(DILATIONS))]
    outs, lses = [], []
    for d, (qd, kd, vd) in zip(DILATIONS, qkv):
        o, l = _attn_fwd(qd, kd, vd, d)
        outs.append(o)
        lses.append(l)
    attn, lse, attn4, lse4, attn16, lse16 = _attn_merge(outs, lses)

    y_ssd, states, cat, act = _ssd_fwd(xbcdt, conv_full, conv_b, bias_w, alog_w, dsk_w, qkvz, attn, ssd_norm_g)


    rest_back, landed = _send_wait_many(rest_handles, cat, "gather_rest_wait")
    landed = [lax.dynamic_update_slice(l, b[None], (me, 0, 0)) for l, b in zip(landed, rest_back)]
    w_o, w_upT, w_dn, w_gate = landed[0].reshape(D, D), landed[1].reshape(DFF, D), landed[2].reshape(DFF, D), landed[3].reshape(D, D)
    w_projT = landed[4].reshape(D, PLE)

    def post1(mm, xx, ga):
        h = xx + mm * _rstd(mm) * ga
        return (mm, h, _rstd(h)), ()
    mix, h1, r3 = _mm_rows(post1, [(cat, w_o, False)], [x2], [g2], [(D, F32), (D, F32), (1, F32)], [], tm=512,
                           name="mix_out")

    a_up, ff, u2, h2, h2b = _mlp_fwd(h1, r3, g3, w_upT, w_dn, g4)
    relu2 = lambda a: jnp.square(jnp.maximum(a.astype(F32), 0.0))

    def final(gpre, ppv, hh, tg, g):
        sg = _sigmoid(gpre)
        ple = ppv * sg
        r = _rstd(ple)
        n = ple * r
        h3 = hh + n * g
        e = h3 - tg
        dh3 = e * (1.0 / D)
        dple = _rms_bwd(n, r, g, dh3)
        return (dh3, dple * sg, dple * ppv * sg * (1.0 - sg)), (_colsum(dh3 * n), _colsum(0.5 * e * e * (1.0 / D)))
    dh3, dpp, dgp, dg5, loss_vec = _mm_rows(final, [(h2b, w_gate, False), (p2b, w_projT, True)], [h2, tgt], [g5],
                                            [(D, F32), (D, BF16), (D, BF16)], [(1, D), (1, D)], tm=512, name="ple_loss")

    gw_projT = _mm(dpp, p2b, ta=True, tm=512, tn=256, tk=T, out_dtypes=(BF16,), name="gw_ple_proj")
    gw_gate = _mm(h2b, dgp, ta=True, tm=512, tn=1024, tk=T, out_dtypes=(BF16,), name="gw_ple_gate")
    def bwd_mlp_post(dg_, d3, f, g):
        dh2 = d3 + dg_
        r = _rstd(f)
        n = f * r
        return (dh2, _rms_bwd(n, r, g, dh2)), (_colsum(dh2 * n),)
    dh2, dff, dg4 = _mm_rows(bwd_mlp_post, [(dgp, w_gate, True)], [dh3, ff], [g4], [(D, F32), (D, BF16)], [(1, D)],
                             tm=512, name="bwd_ple_gate")

    gw_dn = _mm(a_up, dff, ta=True, tm=512, tn=1024, tk=T, a_pre=relu2, out_dtypes=(BF16,), name="gw_mlp_down")
    rs_a, tok_a = _send_start_many([gw_projT.reshape(N_DEV, 32, D), gw_gate.reshape(N_DEV, 128, D),
                                    gw_dn.reshape(N_DEV, 512, D)], True, "rs_start_a", g1)
    da_up, du2 = _mlp_dx(dff, a_up, w_upT, w_dn, tok_a)
    gw_upT = _mm(da_up, u2, ta=True, tm=512, tn=1024, tk=T, out_dtypes=(BF16,), name="gw_mlp_up")

    def bwd_mix_post(d2, du, hh, rr, mm, ga, gb):
        n3 = hh * rr
        dh1 = d2 + _rms_bwd(n3, rr, gb, du)
        r = _rstd(mm)
        n2 = mm * r
        return (dh1, _rms_bwd(n2, r, ga, dh1)), (_colsum(du * n3), _colsum(dh1 * n2))
    dh1, dmix, dg3, dg2 = _rowwise(bwd_mix_post, [dh2, du2, h1, r3, mix], [g2, g3], [(D, F32), (D, BF16)],
                                   [(1, D), (1, D)], tm=512, name="bwd_post_mix")

    gw_o = _mm(cat, dmix, ta=True, tm=512, tn=1024, tk=T, out_dtypes=(BF16,), name="gw_out")
    rs_b, tok_b = _send_start_many([gw_upT.reshape(N_DEV, 512, D), gw_o.reshape(N_DEV, 128, D)], True, "rs_start_b", g1)
    dcat, dattn4, dattn16 = _dx_out(dmix, w_o, tok_b)

    dact, ddtw, ssd_par, dz, dgs = _ssd_bwd(act, xbcdt, bias_w, alog_w, dsk_w, states, y_ssd, qkvz, dcat, ssd_norm_g)
    dxbcdt, conv_par = _conv_bwd(xbcdt, dact, ddtw, conv_full, conv_b)

    qkv_grads = [_attn_bwd(*qkv[0], dcat, attn, lse, 1),
                 _attn_bwd(*qkv[1], dattn4, attn4, lse4, 4),
                 _attn_bwd(*qkv[2], dattn16, attn16, lse16, 16)]
    dqkvz = _rope_bwd(qkv_grads, dz, cos128, sin128)

    gw_qkvzT = _mm(dqkvz, u1, ta=True, tm=512, tn=1024, tk=T, out_dtypes=(BF16,), name="gw_qkvz")
    gw_xbcdtT = _mm(dxbcdt, u1, ta=True, tm=896, tn=1024, tk=T, out_dtypes=(BF16,), name="gw_xbcdt")
    gw_inT = jnp.concatenate([gw_qkvzT, gw_xbcdtT], axis=0)[:IN_W]
    gw_inT = jnp.pad(gw_inT.reshape(N_DEV, W_IN_SHARD, D), ((0, 0), (0, W_IN_SHARD_PAD - W_IN_SHARD), (0, 0)))
    rs_in, tok_in = _send_start(gw_inT, True, "rs_start_w_in", g1)

    def bwd_in(ua, ub, d1, xx, g):
        rr = _rstd(xx)
        n = xx * rr
        du = ua + ub
        return (d1 + _rms_bwd(n, rr, g, du),), (_colsum(du * n),)
    grad_x, dg1 = _mm_rows(bwd_in, [(dqkvz, w_qkvzT, False), (dxbcdt, w_xbcdtT, False)], [dh1, x2], [g1],
                           [(D, F32)], [(1, D)], tm=512, name="bwd_in_proj", deps=[tok_in])

    my_slab = _slab_pack([(dg1, 0), (dg2, 1), (dg3, 2), (dg4, 3), (dg5, 4), (dgs, 5), (loss_vec, 6),
                          (conv_par, 8), (ssd_par, 16)], "slab_pack")
    slab_handles, tok_slab = _send_start_many([my_slab], False, "slab_start", g1)

    def scatter_finish(handles, nm, after):
        part, land = _send_wait(handles, after, "rs_wait_" + nm)
        own = lax.dynamic_slice(part, (me, 0, 0), (1,) + part.shape[1:])
        return _sum_slots(lax.dynamic_update_slice(land, own, (me, 0, 0)), "rs_sum_" + nm)
    def landed(handles, after, wait_name):
        parts, lands = _send_wait_many(handles, after, wait_name)
        return [lax.dynamic_update_slice(land, lax.dynamic_slice(part, (me, 0, 0), (1,) + part.shape[1:]), (me, 0, 0))
                for part, land in zip(parts, lands)]
    land_proj, land_gate, land_dn = landed(rs_a, tok_slab, "rs_wait_a")
    land_up, land_out = landed(rs_b, tok_slab, "rs_wait_b")

    grads, delta, new_m, new_v = {}, {}, {}, {}
    for nme, land in (("w_down", land_dn), ("w_out", land_out), ("w_ple_gate", land_gate)):
        outs4 = _adamw_slots(land, args[nme][0], args["m_" + nme][0], args["v_" + nme][0], "adamw_" + nme)
        grads[nme], delta[nme], new_m[nme], new_v[nme] = [t[None] for t in outs4]
    grads["w_up"] = _sum_slots(land_up, "rs_sum_w_up").T[None]
    grads["w_ple_proj"] = _sum_slots(land_proj, "rs_sum_w_proj").reshape(128, PLE).T[None]
    for nme in ["w_up", "w_ple_proj", "w_in"]:
        if nme == "w_in":
            g_inT = scatter_finish(rs_in, "w_in", delta["w_down"])
            grads["w_in"] = g_inT[:W_IN_SHARD].T[None]
        dl, mm_, vv_ = _adamw(args[nme][0], grads[nme][0], args["m_" + nme][0], args["v_" + nme][0], "adamw_" + nme)
        delta[nme], new_m[nme], new_v[nme] = dl[None], mm_[None], vv_[None]

    slab_back, slab_land = _send_wait_many(slab_handles, delta["w_in"], "slab_wait")
    slab = _sum_slots(lax.dynamic_update_slice(slab_land[0], slab_back[0][None], (me, 0, 0)), "slab_sum")
    g_conv_w = lax.dynamic_slice(slab[8:12, :CONV_CH], (0, me * 96), (CONV_K, 96))
    small_names = SMALL + ["conv_w"]
    small_rows = [0, 1, 2, 3, 4, 12, 5, 16, 17, 18, None]
    pick = lambda prefix: [args[prefix + nme] for nme in SMALL] + [args[prefix + "conv_w"][0]]
    loss11, g_s, d_s, m_s, v_s = _adamw_small(slab, small_rows, g_conv_w, pick(""), pick("m_"), pick("v_"))
    loss = loss11[0, 0]
    for i, nme in enumerate(small_names):
        lead = (lambda t: t[None]) if nme == "conv_w" else (lambda t: t)
        grads[nme], delta[nme], new_m[nme], new_v[nme] = lead(g_s[i]), lead(d_s[i]), lead(m_s[i]), lead(v_s[i])

    order = ["norm_mix_pre", "norm_mix_post", "w_in", "conv_w", "conv_b", "dt_bias", "a_log", "d_skip", "ssd_norm_g",
             "w_out", "norm_mlp_pre", "norm_mlp_post", "w_up", "w_down", "w_ple_gate", "w_ple_proj", "norm_ple_post"]
    return (loss, grad_x[None], *[grads[n] for n in order], *[delta[n] for n in order],
            *[new_m[n] for n in order], *[new_v[n] for n in order])
```

```python
import jax
import jax.numpy as jnp
from jax import lax
from jax.experimental import pallas as pl
from jax.experimental.pallas import tpu as pltpu

F32 = jnp.float32
BF16 = jnp.bfloat16
MESH = pl.DeviceIdType.MESH
HIGHEST = lax.Precision.HIGHEST

N_DEV = 8
T = 4096
D = 1024
HEADS = 8
HD = 64
AW = 512
NS = 128
CONV_K = 4
CONV_CH = 768
CHUNK = 128
SSD_PER = 2
DFF = 4096
PLE = 256
EPS = 1e-6
ROPE_THETA = 10000.0
DILATIONS = (1, 4, 16)
QBLK = 128
NEG = -1e30
IN_W = 2824
W_IN_SHARD = 353
W_IN_SHARD_PAD = 384
DT_PAD = 128

ADAM_LR, ADAM_B1, ADAM_B2, ADAM_EPS, ADAM_WD, ADAM_STEP = 0.001, 0.9, 0.999, 1e-08, 0.01, 10

VMEM_LIMIT = 56 * 1024 * 1024


_ANY = pl.BlockSpec(memory_space=pl.ANY)


def _cparams(sem=None):
    return pltpu.CompilerParams(dimension_semantics=sem, vmem_limit_bytes=VMEM_LIMIT)


def _dot(a, b, ca, cb, precision=None):
    return lax.dot_general(a, b, (((ca,), (cb,)), ((), ())), preferred_element_type=F32, precision=precision)


def _nn(a, b):
    return _dot(a, b, 1, 0)


def _nt(a, b):
    return _dot(a, b, 1, 1)


def _tn(a, b):
    return _dot(a, b, 0, 0)


def _sigmoid(x):
    return 1.0 / (1.0 + jnp.exp(-x))


def _softplus(x):
    return jnp.maximum(x, 0.0) + jnp.log(1.0 + jnp.exp(-jnp.abs(x)))


def _mm(a, b, *, ta=False, tb=False, tm, tn, tk, name,
        a_pre=None, a_rows=(), a_cols=(), b_pre=None, b_rows=(), b_cols=(),
        epi=None, epi_tiles=(), out_dtypes=(F32,), deps=()):
    if ta:
        K, M = a.shape
    else:
        M, K = a.shape
    if tb:
        N, K2 = b.shape
    else:
        K2, N = b.shape
    assert K == K2 and M % tm == 0 and N % tn == 0 and K % tk == 0, (name, a.shape, b.shape)
    nk = K // tk
    if ta:
        a_spec = pl.BlockSpec((tk, tm), lambda i, j, k: (k, i))
        a_row_specs = [pl.BlockSpec((tk, 1), lambda i, j, k: (k, 0)) for _ in a_rows]
        a_col_specs = [pl.BlockSpec((1, tm), lambda i, j, k: (0, i)) for _ in a_cols]
    else:
        a_spec = pl.BlockSpec((tm, tk), lambda i, j, k: (i, k))
        a_row_specs = [pl.BlockSpec((tm, 1), lambda i, j, k: (i, 0)) for _ in a_rows]
        a_col_specs = [pl.BlockSpec((1, tk), lambda i, j, k: (0, k)) for _ in a_cols]
    if tb:
        b_spec = pl.BlockSpec((tn, tk), lambda i, j, k: (j, k))
        b_row_specs = [pl.BlockSpec((tn, 1), lambda i, j, k: (j, 0)) for _ in b_rows]
        b_col_specs = [pl.BlockSpec((1, tk), lambda i, j, k: (0, k)) for _ in b_cols]
    else:
        b_spec = pl.BlockSpec((tk, tn), lambda i, j, k: (k, j))
        b_row_specs = [pl.BlockSpec((tk, 1), lambda i, j, k: (k, 0)) for _ in b_rows]
        b_col_specs = [pl.BlockSpec((1, tn), lambda i, j, k: (0, j)) for _ in b_cols]
    o_spec = pl.BlockSpec((tm, tn), lambda i, j, k: (i, j))
    na, nb, ne, no = len(a_rows) + len(a_cols), len(b_rows) + len(b_cols), len(epi_tiles), len(out_dtypes)

    def body(*refs):
        a_ref, b_ref = refs[0], refs[1]
        a_ex = refs[2:2 + na]
        b_ex = refs[2 + na:2 + na + nb]
        e_ex = refs[2 + na + nb:2 + na + nb + ne]
        first_out = 2 + na + nb + ne + len(deps)
        outs = refs[first_out:first_out + no]

        def finish(res):
            vals = epi(res, *[r[...] for r in e_ex]) if epi is not None else (res,)
            for o_ref, val in zip(outs, vals):
                o_ref[...] = val.astype(o_ref.dtype)

        at = a_ref[...]
        if a_pre is not None:
            at = a_pre(at, *[r[...] for r in a_ex])
        bt = b_ref[...]
        if b_pre is not None:
            bt = b_pre(bt, *[r[...] for r in b_ex])
        prod = _dot(at.astype(BF16), bt.astype(BF16), 0 if ta else 1, 1 if tb else 0)
        if nk == 1:
            finish(prod)
            return
        acc = refs[-1]
        k = pl.program_id(2)

        @pl.when(k == 0)
        def _():
            acc[...] = jnp.zeros_like(acc)
        acc[...] += prod

        @pl.when(k == nk - 1)
        def _():
            finish(acc[...])

    outs = pl.pallas_call(
        body, name=name,
        grid=(M // tm, N // tn, nk),
        in_specs=([a_spec, b_spec] + a_row_specs + a_col_specs + b_row_specs + b_col_specs + [o_spec] * ne
                  + [_ANY] * len(deps)),
        out_specs=[o_spec] * no,
        out_shape=[jax.ShapeDtypeStruct((M, N), dt) for dt in out_dtypes],
        scratch_shapes=[pltpu.VMEM((tm, tn), F32)] if nk > 1 else [],
        compiler_params=_cparams(("parallel", "parallel", "arbitrary")),
    )(a, b, *a_rows, *a_cols, *b_rows, *b_cols, *epi_tiles, *deps)
    return outs[0] if no == 1 else outs


MLP_TM = 1024
MLP_TC = 512


def _mlp_fwd(h, r, g, w_upT, w_dn, g_post):
    nc = DFF // MLP_TC

    def body(h_ref, r_ref, g_ref, wu_ref, wd_ref, gp_ref, a_ref, ff_ref, u_ref, ho_ref, hob_ref, acc, u_scr):
        c = pl.program_id(1)

        @pl.when(c == 0)
        def _():
            u = (h_ref[...] * r_ref[...] * g_ref[...]).astype(BF16)
            u_scr[...] = u
            u_ref[...] = u
            acc[...] = jnp.zeros_like(acc)
        a = _nt(u_scr[...], wu_ref[...])
        a_ref[...] = a.astype(BF16)
        acc[...] += _nn(jnp.square(jnp.maximum(a, 0.0)).astype(BF16), wd_ref[...])

        @pl.when(c == nc - 1)
        def _():
            f = acc[...]
            ff_ref[...] = f
            ho = h_ref[...] + f * _rstd(f) * gp_ref[...]
            ho_ref[...] = ho
            hob_ref[...] = ho.astype(BF16)

    row = pl.BlockSpec((MLP_TM, D), lambda i, c: (i, 0))
    wsp = pl.BlockSpec((MLP_TC, D), lambda i, c: (c, 0))
    vec = pl.BlockSpec((1, D), lambda i, c: (0, 0))
    return pl.pallas_call(
        body, name="mlp_fwd", grid=(T // MLP_TM, nc),
        in_specs=[row, pl.BlockSpec((MLP_TM, 1), lambda i, c: (i, 0)), vec, wsp, wsp, vec],
        out_specs=[pl.BlockSpec((MLP_TM, MLP_TC), lambda i, c: (i, c)), row, row, row, row],
        out_shape=[jax.ShapeDtypeStruct((T, DFF), BF16), jax.ShapeDtypeStruct((T, D), F32), jax.ShapeDtypeStruct((T, D), BF16),
                   jax.ShapeDtypeStruct((T, D), F32), jax.ShapeDtypeStruct((T, D), BF16)],
        scratch_shapes=[pltpu.VMEM((MLP_TM, D), F32), pltpu.VMEM((MLP_TM, D), BF16)],
        compiler_params=_cparams(("parallel", "arbitrary")),
    )(h, r, g, w_upT, w_dn, g_post)


def _mlp_dx(dff, a, w_upT, w_dn, dep):
    nc = DFF // MLP_TC

    def body(d_ref, a_ref, wu_ref, wd_ref, dep_ref, da_ref, du_ref, acc, d_scr):
        c = pl.program_id(1)

        @pl.when(c == 0)
        def _():
            d_scr[...] = d_ref[...].astype(BF16)
            acc[...] = jnp.zeros_like(acc)
        da = (_nt(d_scr[...], wd_ref[...]) * (2.0 * jnp.maximum(a_ref[...].astype(F32), 0.0))).astype(BF16)
        da_ref[...] = da
        acc[...] += _nn(da, wu_ref[...])

        @pl.when(c == nc - 1)
        def _():
            du_ref[...] = acc[...]

    row = pl.BlockSpec((MLP_TM, D), lambda i, c: (i, 0))
    wsp = pl.BlockSpec((MLP_TC, D), lambda i, c: (c, 0))
    chunk = pl.BlockSpec((MLP_TM, MLP_TC), lambda i, c: (i, c))
    return pl.pallas_call(
        body, name="mlp_dx", grid=(T // MLP_TM, nc),
        in_specs=[row, chunk, wsp, wsp, _ANY], out_specs=[chunk, row],
        out_shape=[jax.ShapeDtypeStruct((T, DFF), BF16), jax.ShapeDtypeStruct((T, D), F32)],
        scratch_shapes=[pltpu.VMEM((MLP_TM, D), F32), pltpu.VMEM((MLP_TM, D), BF16)],
        compiler_params=_cparams(("parallel", "arbitrary")),
    )(dff, a, w_upT, w_dn, dep)


def _rowwise(fn, rows, vecs, out_rows, out_sums, *, tm, name, deps=(), in_buffers=2):
    specs, arrs = [], []
    R = None
    mode = {} if in_buffers == 2 else dict(pipeline_mode=pl.Buffered(in_buffers))
    for r in rows:
        if isinstance(r, tuple):
            arr, width, cb = r
            specs.append(pl.BlockSpec((tm, width), lambda i, cb=cb: (i, cb), **mode))
        else:
            arr = r
            specs.append(pl.BlockSpec((tm, arr.shape[1]), lambda i: (i, 0), **mode))
        R = arr.shape[0] if R is None else R
        assert arr.shape[0] == R, name
        arrs.append(arr)
    assert R % tm == 0, name
    for v in vecs:
        specs.append(pl.BlockSpec(v.shape, lambda i: (0, 0)))
        arrs.append(v)
    nr, nv, no, ns = len(rows), len(vecs), len(out_rows), len(out_sums)
    out_specs = [pl.BlockSpec((tm, w), lambda i: (i, 0)) for w, _ in out_rows]
    out_specs += [pl.BlockSpec(s, lambda i: (0, 0)) for s in out_sums]
    out_shape = [jax.ShapeDtypeStruct((R, w), dt) for w, dt in out_rows]
    out_shape += [jax.ShapeDtypeStruct(s, F32) for s in out_sums]

    nd = len(deps)

    def body(*refs):
        ins = [r[...] for r in refs[:nr + nv]]
        o_refs = refs[nr + nv + nd:nr + nv + nd + no]
        s_refs = refs[nr + nv + nd + no:]
        o_vals, s_vals = fn(*ins)
        for ref, val in zip(o_refs, o_vals):
            ref[...] = val.astype(ref.dtype)
        if ns:
            @pl.when(pl.program_id(0) == 0)
            def _():
                for ref in s_refs:
                    ref[...] = jnp.zeros_like(ref)
            for ref, val in zip(s_refs, s_vals):
                ref[...] += val

    if in_buffers != 2:
        assert ns == 0, name

        def outer(*refs):
            def step(*blocks):
                body(*blocks[:nr + nv], *[None] * nd, *blocks[nr + nv:])

            pltpu.emit_pipeline(step, grid=(R // tm,), in_specs=specs, out_specs=out_specs)(
                *refs[:nr + nv], *refs[nr + nv + nd:])

        return pl.pallas_call(
            outer, name=name, in_specs=[_ANY] * (nr + nv + nd), out_specs=[_ANY] * no, out_shape=out_shape,
            compiler_params=_cparams(None),
        )(*arrs, *deps)

    outs = pl.pallas_call(
        body, name=name, grid=(R // tm,), in_specs=specs + [_ANY] * nd, out_specs=out_specs, out_shape=out_shape,
        compiler_params=_cparams(("arbitrary",) if ns else ("parallel",)),
    )(*arrs, *deps)
    return outs


def _mm_rows(fn, mats, rows, vecs, out_rows, out_sums, *, tm, name, deps=()):
    R = mats[0][0].shape[0]
    assert R % tm == 0, name
    specs, arrs = [], []
    for a, b, tb in mats:
        specs += [pl.BlockSpec((tm, a.shape[1]), lambda i: (i, 0)), pl.BlockSpec(b.shape, lambda i: (0, 0))]
        arrs += [a, b]
    for r in rows:
        specs.append(pl.BlockSpec((tm, r.shape[1]), lambda i: (i, 0)))
        arrs.append(r)
    for v in vecs:
        specs.append(pl.BlockSpec(v.shape, lambda i: (0, 0)))
        arrs.append(v)
    nm, nr, nv, nd, no, ns = len(mats), len(rows), len(vecs), len(deps), len(out_rows), len(out_sums)
    out_specs = [pl.BlockSpec((tm, w), lambda i: (i, 0)) for w, _ in out_rows]
    out_specs += [pl.BlockSpec(s, lambda i: (0, 0)) for s in out_sums]
    out_shape = [jax.ShapeDtypeStruct((R, w), dt) for w, dt in out_rows] + [jax.ShapeDtypeStruct(s, F32) for s in out_sums]

    def body(*refs):
        prods = [_dot(refs[2 * p][...].astype(BF16), refs[2 * p + 1][...].astype(BF16), 1, 1 if mats[p][2] else 0)
                 for p in range(nm)]
        ins = [r[...] for r in refs[2 * nm:2 * nm + nr + nv]]
        first_out = 2 * nm + nr + nv + nd
        o_refs, s_refs = refs[first_out:first_out + no], refs[first_out + no:]
        o_vals, s_vals = fn(*prods, *ins)
        for ref, val in zip(o_refs, o_vals):
            ref[...] = val.astype(ref.dtype)
        if ns:
            @pl.when(pl.program_id(0) == 0)
            def _():
                for ref in s_refs:
                    ref[...] = jnp.zeros_like(ref)
            for ref, val in zip(s_refs, s_vals):
                ref[...] += val

    return pl.pallas_call(
        body, name=name, grid=(R // tm,), in_specs=specs + [_ANY] * nd, out_specs=out_specs, out_shape=out_shape,
        compiler_params=_cparams(("arbitrary",) if ns else ("parallel",)),
    )(*arrs, *deps)


def _colsum(x):
    return jnp.sum(x, axis=0, keepdims=True)


def _rstd(x):
    return lax.rsqrt(jnp.mean(x * x, axis=-1, keepdims=True) + EPS)


def _rms_bwd(xn, r, g, dy):
    dn = dy * g
    return r * (dn - xn * jnp.mean(dn * xn, axis=-1, keepdims=True))


def _partner(t):
    lane = lax.broadcasted_iota(jnp.int32, t.shape, 1)
    up = pltpu.roll(t, 96, 1)
    down = pltpu.roll(t, 32, 1)
    return jnp.where((lane % 64) < 32, up, down)


SLABS = AW // 128


def _rows(r, n, d):
    return pl.ds(r, n, stride=d) if d > 1 else pl.ds(0, n)


def _undilate(src_ref, dst, d, tm):
    for r in range(d):
        for j in range(SLABS):
            dst[j][_rows(r, tm // d, d), :] = src_ref[:, pl.ds(r * AW + j * 128, 128)].astype(dst[j].dtype)


def _dilate(dst_ref, src, d, tm):
    for r in range(d):
        for j in range(SLABS):
            dst_ref[:, pl.ds(r * AW + j * 128, 128)] = src[j][_rows(r, tm // d, d), :].astype(dst_ref.dtype)


def _slab_scratch(n, tm):
    return [pltpu.VMEM((tm, 128), F32)] * (SLABS * n)


def _slab_groups(flat):
    return [flat[SLABS * i:SLABS * (i + 1)] for i in range(len(flat) // SLABS)]


def _slab_specs(tm, first):
    return [pl.BlockSpec((tm, 128), lambda i, j=j: (i, first + j)) for j in range(SLABS)]


def _dil_spec(tm, d):
    return pl.BlockSpec((tm // d, d * AW), lambda i: (i, 0))


ROPE_TM = 512


def _rope_fwd(qkvz, cos128, sin128):
    tm = ROPE_TM

    def body(*refs):
        q_refs, k_refs, v_refs = refs[0:4], refs[4:8], refs[8:12]
        c_ref, s_ref = refs[12], refs[13]
        outs = refs[14:23]
        qs, ks = _slab_groups(refs[23:])
        c, s = c_ref[...], s_ref[...]
        for j in range(SLABS):
            q, k = q_refs[j][...], k_refs[j][...]
            qs[j][...] = (q * c + _partner(q) * s) * (HD ** -0.5)
            ks[j][...] = k * c + _partner(k) * s
        for di, d in enumerate(DILATIONS):
            oq, ok, ov = outs[3 * di:3 * di + 3]
            for r in range(d):
                rows = _rows(r, tm // d, d)
                for j in range(SLABS):
                    cols = pl.ds(r * AW + j * 128, 128)
                    oq[:, cols] = qs[j][rows, :].astype(BF16)
                    ok[:, cols] = ks[j][rows, :].astype(BF16)
                    ov[:, cols] = v_refs[j][rows, :].astype(BF16)

    tab = pl.BlockSpec((tm, 128), lambda i: (i, 0))
    out_specs, out_shape = [], []
    for d in DILATIONS:
        out_specs += [_dil_spec(tm, d)] * 3
        out_shape += [jax.ShapeDtypeStruct((T // d, d * AW), BF16)] * 3
    return pl.pallas_call(
        body, name="rope_fwd", grid=(T // tm,),
        in_specs=_slab_specs(tm, 0) + _slab_specs(tm, 4) + _slab_specs(tm, 8) + [tab, tab],
        out_specs=out_specs, out_shape=out_shape, scratch_shapes=_slab_scratch(2, tm),
        compiler_params=_cparams(("parallel",)),
    )(*([qkvz] * 12), cos128, sin128)


def _rope_bwd(grads, dz, cos128, sin128):
    tm = ROPE_TM

    def body(*refs):
        g_refs = refs[0:9]
        dz_ref, c_ref, s_ref, o_ref = refs[9], refs[10], refs[11], refs[12]
        scr = _slab_groups(refs[13:])
        for di, d in enumerate(DILATIONS[1:]):
            for t in range(3):
                _undilate(g_refs[3 * (di + 1) + t], scr[3 * di + t], d, tm)
        c, s = c_ref[...], s_ref[...]
        for j in range(SLABS):
            cols = pl.ds(j * 128, 128)
            tot = [g_refs[t][:, cols] + scr[t][j][...] + scr[3 + t][j][...] for t in range(3)]
            dqr = tot[0] * (HD ** -0.5)
            o_ref[:, pl.ds(j * 128, 128)] = (dqr * c + _partner(dqr * s)).astype(BF16)
            o_ref[:, pl.ds(AW + j * 128, 128)] = (tot[1] * c + _partner(tot[1] * s)).astype(BF16)
            o_ref[:, pl.ds(2 * AW + j * 128, 128)] = tot[2].astype(BF16)
        o_ref[:, pl.ds(3 * AW, AW)] = dz_ref[...].astype(BF16)

    tab = pl.BlockSpec((tm, 128), lambda i: (i, 0))
    in_specs, args = [], []
    for d, g in zip(DILATIONS, grads):
        in_specs += [_dil_spec(tm, d)] * 3
        args += list(g)
    return pl.pallas_call(
        body, name="rope_bwd", grid=(T // tm,),
        in_specs=in_specs + [pl.BlockSpec((tm, AW), lambda i: (i, 0)), tab, tab],
        out_specs=pl.BlockSpec((tm, 4 * AW), lambda i: (i, 0)),
        out_shape=jax.ShapeDtypeStruct((T, 4 * AW), BF16),
        scratch_shapes=_slab_scratch(6, tm),
        compiler_params=_cparams(("parallel",)),
    )(*args, dz, cos128, sin128)


def _dx_out(dmix, w_o, dep):
    tm = ROPE_TM

    def body(a_ref, w_ref, dep_ref, dcat_ref, o4, o16, *slabs):
        prod = _nt(a_ref[...].astype(BF16), w_ref[...].astype(BF16))
        dcat_ref[...] = prod
        for j in range(SLABS):
            slabs[j][...] = prod[:, 128 * j:128 * (j + 1)]
        _dilate(o4, slabs, 4, tm)
        _dilate(o16, slabs, 16, tm)

    return pl.pallas_call(
        body, name="dx_out", grid=(T // tm,),
        in_specs=[pl.BlockSpec((tm, D), lambda i: (i, 0)), pl.BlockSpec((D, D), lambda i: (0, 0)), _ANY],
        out_specs=[pl.BlockSpec((tm, D), lambda i: (i, 0)), _dil_spec(tm, 4), _dil_spec(tm, 16)],
        out_shape=[jax.ShapeDtypeStruct((T, D), F32), jax.ShapeDtypeStruct((T // 4, 4 * AW), F32),
                   jax.ShapeDtypeStruct((T // 16, 16 * AW), F32)],
        scratch_shapes=_slab_scratch(1, tm), compiler_params=_cparams(("parallel",)),
    )(dmix, w_o, dep)


def _band_masks():
    qi = lax.broadcasted_iota(jnp.int32, (QBLK, QBLK), 0)
    kj = lax.broadcasted_iota(jnp.int32, (QBLK, QBLK), 1)
    return kj >= qi, kj <= qi


def _attn_fwd(q, k, v, d):
    L = q.shape[0]
    npair = L // (2 * QBLK)

    def body(q_ref, kp_ref, kc_ref, vp_ref, vc_ref, o_ref, l_ref):
        pair = pl.program_id(1)
        mask_p, mask_c = _band_masks()
        for sub in range(2):
            rows = pl.ds(sub * QBLK, QBLK)
            first = jnp.where(pair > 0, 0.0, NEG) if sub == 0 else 0.0
            bias = jnp.concatenate([jnp.where(mask_p, 0.0, NEG) + first, jnp.where(mask_c, 0.0, NEG)], axis=1)
            k_prev = (lambda sl: kp_ref[:, sl]) if sub == 0 else (lambda sl: kc_ref[pl.ds(0, QBLK), sl])
            v_prev = (lambda sl: vp_ref[:, sl]) if sub == 0 else (lambda sl: vc_ref[pl.ds(0, QBLK), sl])
            s = []
            for h in range(HEADS):
                sl = pl.ds(HD * h, HD)
                qh = q_ref[rows, sl]
                s.append(jnp.concatenate([_nt(qh, k_prev(sl)), _nt(qh, kc_ref[rows, sl])], axis=1))
            s = jnp.stack(s) + bias
            m = jnp.max(s, axis=2, keepdims=True)
            e = jnp.exp(s - m)
            den = jnp.sum(e, axis=2, keepdims=True)
            p = e.astype(BF16)
            inv = 1.0 / den
            lse = m + jnp.log(den)
            for h in range(HEADS):
                sl = pl.ds(HD * h, HD)
                o_ref[rows, sl] = ((_nn(p[h, :, :QBLK], v_prev(sl)) + _nn(p[h, :, QBLK:], vc_ref[rows, sl])) * inv[h]
                                   ).astype(BF16)
                l_ref[rows, sl] = jnp.broadcast_to(lse[h], (QBLK, HD))

    cur = pl.BlockSpec((2 * QBLK, AW), lambda r, n: (n, r))
    prev = pl.BlockSpec((QBLK, AW), lambda r, n: (jnp.maximum(2 * n - 1, 0), r))
    return pl.pallas_call(
        body, name=f"attn_fwd_d{d}", grid=(d, npair),
        in_specs=[cur, prev, cur, prev, cur], out_specs=[cur, cur],
        out_shape=[jax.ShapeDtypeStruct((L, d * AW), BF16), jax.ShapeDtypeStruct((L, d * AW), F32)],
        compiler_params=_cparams(("parallel", "parallel")),
    )(q, k, k, v, v)


def _attn_bwd(q, k, v, do, at, lse, d):
    L = q.shape[0]
    nb = L // QBLK
    npair = nb // 2

    def body(qc_ref, qn_ref, kp_ref, kc_ref, vp_ref, vc_ref, doc_ref, don_ref, atc_ref, atn_ref,
             lc_ref, ln_ref, dq_ref, dk_ref, dv_ref):
        pair = pl.program_id(1)
        mask_p, mask_c = _band_masks()
        prev_bias = jnp.where(mask_p, 0.0, NEG)
        for sub in range(2):
            rows = pl.ds(sub * QBLK, QBLK)
            second = pl.ds(QBLK, QBLK)
            if sub == 0:
                take = lambda cur_ref, nxt_ref, cols, i: cur_ref[rows if i == 0 else second, cols]
                prev_of = lambda p_ref, c_ref, cols: p_ref[:, cols]
                first, last = jnp.where(pair > 0, 0.0, NEG), 0.0
            else:
                take = lambda cur_ref, nxt_ref, cols, i: cur_ref[rows, cols] if i == 0 else nxt_ref[:, cols]
                prev_of = lambda p_ref, c_ref, cols: c_ref[pl.ds(0, QBLK), cols]
                first, last = 0.0, jnp.where(pair < npair - 1, 0.0, NEG)
            bias = jnp.concatenate([prev_bias + first, jnp.where(mask_c, 0.0, NEG), prev_bias + last], axis=1)
            s, dp, ls, dl, ops = [], [], [], [], []
            for h in range(HEADS):
                sl = pl.ds(HD * h, HD)
                one = pl.ds(HD * h, 1)
                q0, q1 = take(qc_ref, qn_ref, sl, 0), take(qc_ref, qn_ref, sl, 1)
                kp, kc = prev_of(kp_ref, kc_ref, sl), kc_ref[rows, sl]
                vp, vc = prev_of(vp_ref, vc_ref, sl), vc_ref[rows, sl]
                do0, do1 = take(doc_ref, don_ref, sl, 0), take(doc_ref, don_ref, sl, 1)
                do0b, do1b = do0.astype(BF16), do1.astype(BF16)
                s.append(jnp.concatenate([_nt(q0, kp), _nt(q0, kc), _nt(q1, kc)], axis=1))
                dp.append(jnp.concatenate([_nt(do0b, vp), _nt(do0b, vc), _nt(do1b, vc)], axis=1))
                dl0 = jnp.sum(do0 * take(atc_ref, atn_ref, sl, 0), axis=1, keepdims=True)
                dl1 = jnp.sum(do1 * take(atc_ref, atn_ref, sl, 1), axis=1, keepdims=True)
                dl.append(jnp.concatenate([jnp.broadcast_to(dl0, (QBLK, 2 * QBLK)), jnp.broadcast_to(dl1, (QBLK, QBLK))], axis=1))
                ls.append(jnp.concatenate([jnp.broadcast_to(take(lc_ref, ln_ref, one, 0), (QBLK, 2 * QBLK)),
                                           jnp.broadcast_to(take(lc_ref, ln_ref, one, 1), (QBLK, QBLK))], axis=1))
                ops.append((q0, q1, kp, kc, do0b, do1b))
            p = jnp.exp(jnp.stack(s) + bias - jnp.stack(ls))
            ds = (p * (jnp.stack(dp) - jnp.stack(dl))).astype(BF16)
            p = p.astype(BF16)
            for h in range(HEADS):
                sl = pl.ds(HD * h, HD)
                q0, q1, kp, kc, do0b, do1b = ops[h]
                dq_ref[rows, sl] = (_nn(ds[h, :, :QBLK], kp) + _nn(ds[h, :, QBLK:2 * QBLK], kc)).astype(BF16)
                dv_ref[rows, sl] = (_tn(p[h, :, QBLK:2 * QBLK], do0b) + _tn(p[h, :, 2 * QBLK:], do1b)).astype(BF16)
                dk_ref[rows, sl] = (_tn(ds[h, :, QBLK:2 * QBLK], q0) + _tn(ds[h, :, 2 * QBLK:], q1)).astype(BF16)

    cur = pl.BlockSpec((2 * QBLK, AW), lambda r, n: (n, r))
    prev = pl.BlockSpec((QBLK, AW), lambda r, n: (jnp.maximum(2 * n - 1, 0), r))
    nxt = pl.BlockSpec((QBLK, AW), lambda r, n: (jnp.minimum(2 * n + 2, nb - 1), r))
    return pl.pallas_call(
        body, name=f"attn_bwd_d{d}", grid=(d, npair),
        in_specs=[cur, nxt, prev, cur, prev, cur, cur, nxt, cur, nxt, cur, nxt], out_specs=[cur, cur, cur],
        out_shape=[jax.ShapeDtypeStruct((L, d * AW), BF16)] * 3,
        compiler_params=_cparams(("parallel", "parallel")),
    )(q, q, k, k, v, v, do, do, at, at, lse, lse)


def _attn_merge(outs, lses):
    tm = ROPE_TM

    def body(o1, o4, o16, l1, l4, l16, at_ref, ls_ref, at4, ls4, at16, ls16, *flat):
        so4, so16, sl4, sl16, sa, sl = _slab_groups(flat)
        _undilate(o4, so4, 4, tm)
        _undilate(o16, so16, 16, tm)
        _undilate(l4, sl4, 4, tm)
        _undilate(l16, sl16, 16, tm)
        for j in range(SLABS):
            cols = pl.ds(j * 128, 128)
            a, b, c = l1[:, cols], sl4[j][...], sl16[j][...]
            m = jnp.maximum(jnp.maximum(a, b), c)
            e1, e2, e3 = jnp.exp(a - m), jnp.exp(b - m), jnp.exp(c - m)
            s = e1 + e2 + e3
            inv = 1.0 / s
            attn = (e1 * inv) * o1[:, cols] + (e2 * inv) * so4[j][...] + (e3 * inv) * so16[j][...]
            lse = m + jnp.log(s)
            at_ref[:, cols] = attn
            ls_ref[:, cols] = lse
            sa[j][...] = attn
            sl[j][...] = lse
        _dilate(at4, sa, 4, tm)
        _dilate(at16, sa, 16, tm)
        _dilate(ls4, sl, 4, tm)
        _dilate(ls16, sl, 16, tm)

    specs = [_dil_spec(tm, d) for d in DILATIONS]
    tok = specs[0]
    return pl.pallas_call(
        body, name="attn_merge", grid=(T // tm,),
        in_specs=specs + specs, out_specs=[tok, tok, specs[1], specs[1], specs[2], specs[2]],
        out_shape=[jax.ShapeDtypeStruct((T, AW), F32)] * 2 + [jax.ShapeDtypeStruct((T // 4, 4 * AW), F32)] * 2
        + [jax.ShapeDtypeStruct((T // 16, 16 * AW), F32)] * 2,
        scratch_shapes=_slab_scratch(6, tm),
        compiler_params=_cparams(("parallel",)),
    )(*outs, *lses)


CONV_TM = 512
HALO = 8


def _conv_pre(ext, w, b):
    y = b + w[3] * ext
    for kk in range(1, CONV_K):
        y = y + w[3 - kk] * pltpu.roll(ext, kk, 0)
    return y


def _rows_to_block(rows, n, width):
    ri = lax.broadcasted_iota(jnp.int32, (n, width), 0)
    out = jnp.zeros((n, width), F32)
    for j, r in enumerate(rows):
        out = out + jnp.where(ri == j, r, 0.0)
    return out


def _conv_bwd(xbc, dact, ddt, w, b):
    nblk = T // CONV_TM
    per = CONV_TM // HALO

    def body(x_ref, xb_ref, xa_ref, g_ref, ga_ref, ddt_ref, w_ref, b_ref, dx_ref, dw_ref):
        i = pl.program_id(0)
        wv = [w_ref[pl.ds(j, 1), :] for j in range(CONV_K)]
        before = jnp.where(i > 0, xb_ref[...], 0.0)
        last = i == nblk - 1
        after = jnp.where(last, 0.0, xa_ref[...])
        g_after = jnp.where(last, 0.0, ga_ref[...])
        ext = jnp.concatenate([before, x_ref[...], after], axis=0)
        y = _conv_pre(ext, wv, b_ref[...])[HALO:]
        sg = _sigmoid(y)
        dy = jnp.concatenate([g_ref[...], g_after], axis=0) * (sg * (1.0 + y * (1.0 - sg)))
        n = CONV_TM + HALO
        dx = wv[3] * dy
        for kk in range(1, CONV_K):
            dx = dx + wv[3 - kk] * pltpu.roll(dy, n - kk, 0)
        dx_ref[:, pl.ds(0, CONV_CH)] = dx[:CONV_TM].astype(BF16)
        dx_ref[:, pl.ds(CONV_CH, DT_PAD)] = ddt_ref[...].astype(BF16)
        dyc = dy[:CONV_TM]
        rows = [jnp.sum(dyc * (pltpu.roll(ext, 3 - j, 0) if j < 3 else ext)[HALO:HALO + CONV_TM], axis=0, keepdims=True)
                for j in range(CONV_K)]
        rows.append(jnp.sum(dyc, axis=0, keepdims=True))
        part = _rows_to_block(rows, 8, CONV_CH)

        @pl.when(i == 0)
        def _():
            dw_ref[...] = jnp.zeros_like(dw_ref)
        dw_ref[...] += part

    blk = pl.BlockSpec((CONV_TM, CONV_CH), lambda i: (i, 0))
    hb = pl.BlockSpec((HALO, CONV_CH), lambda i: (jnp.maximum(i * per - 1, 0), 0))
    ha = pl.BlockSpec((HALO, CONV_CH), lambda i: (jnp.minimum((i + 1) * per, T // HALO - 1), 0))
    return pl.pallas_call(
        body, name="conv_bwd", grid=(nblk,),
        in_specs=[blk, hb, ha, blk, ha, pl.BlockSpec((CONV_TM, DT_PAD), lambda i: (i, 0)),
                  pl.BlockSpec((CONV_K, CONV_CH), lambda i: (0, 0)), pl.BlockSpec((1, CONV_CH), lambda i: (0, 0))],
        out_specs=[pl.BlockSpec((CONV_TM, CONV_CH + DT_PAD), lambda i: (i, 0)), pl.BlockSpec((8, CONV_CH), lambda i: (0, 0))],
        out_shape=[jax.ShapeDtypeStruct((T, CONV_CH + DT_PAD), BF16), jax.ShapeDtypeStruct((8, CONV_CH), F32)],
        compiler_params=_cparams(("arbitrary",)),
    )(xbc, xbc, xbc, dact, dact, ddt, w, b)


def _pick(mat, h):
    lane = lax.broadcasted_iota(jnp.int32, mat.shape, 1)
    return jnp.sum(jnp.where(lane == h, mat, 0.0), axis=1, keepdims=True)


def _heads(fn):
    return jnp.stack([fn(h) for h in range(HEADS)])


def _ssd_prep(dt_ref, bias_ref, alog_ref, dsk_ref, b_ref, c_ref, xs_ref, state_ref, cst):
    li = lax.broadcasted_iota(jnp.int32, (CHUNK, CHUNK), 0)
    si = lax.broadcasted_iota(jnp.int32, (CHUNK, CHUNK), 1)
    tri = li >= si
    dtp = dt_ref[...] + bias_ref[...]
    dt = _softplus(dtp)
    A = -jnp.exp(alog_ref[...])
    a = dt * A
    cs = jnp.dot(tri.astype(F32), a, precision=HIGHEST, preferred_element_type=F32)
    cst[...] = cs.T
    Bm = b_ref[...].astype(BF16)
    Cm = c_ref[...].astype(BF16)
    cb = _nt(Cm, Bm)
    dskv = dsk_ref[...]
    cs_col = _heads(lambda h: _pick(cs, h))
    cs_row = _heads(lambda h: cst[pl.ds(h, 1), :])
    dt_col = _heads(lambda h: _pick(dt, h))
    dsk_col = _heads(lambda h: _pick(dskv, h))
    lam = jnp.exp(jnp.where(tri, cs_col - cs_row, NEG))
    x = _heads(lambda h: xs_ref[:, pl.ds(HD * h, HD)])
    xdt = x * dt_col
    prev = _heads(lambda h: state_ref[pl.ds(HD * h, HD), :])
    lane = lax.broadcasted_iota(jnp.int32, (1, 1, CHUNK), 2)
    cl = jnp.sum(jnp.where(lane == CHUNK - 1, cs_row, 0.0), axis=2, keepdims=True)
    f = jnp.exp(cl - cs_col)
    return dict(li=li, si=si, dtp=dtp, dt=dt, A=A, Bm=Bm, Cm=Cm, cb=cb, cs_col=cs_col, dt_col=dt_col, dsk_col=dsk_col,
                lam=lam, x=x, xdt=xdt, prev=prev, cl=cl, f=f)


def _ssd_fwd(xbcdt, conv_w, conv_b, bias, alog, dsk, qkvz, attn, gs):
    nc = T // CHUNK
    R = SSD_PER * CHUNK
    per = R // HALO

    def body(xbc_ref, halo_ref, cw_ref, cb_ref, dt_ref, bias_ref, alog_ref, dsk_ref, z_ref, at_ref, gs_ref,
             y_ref, st_ref, cat_ref, act_ref, state, cst):
        @pl.when(pl.program_id(0) == 0)
        def _():
            state[...] = jnp.zeros_like(state)
        for sub in range(SSD_PER):
            rows = pl.ds(sub * CHUNK, CHUNK)
            st_ref[sub] = state[...]
            halo = (jnp.where(pl.program_id(0) > 0, halo_ref[...], 0.0) if sub == 0
                    else xbc_ref[pl.ds(sub * CHUNK - HALO, HALO), :])
            one_chunk(halo, xbc_ref.at[rows, :], cw_ref, cb_ref, dt_ref.at[rows, :], bias_ref, alog_ref, dsk_ref,
                      z_ref.at[rows, :], at_ref.at[rows, :], gs_ref, y_ref.at[rows, :], cat_ref.at[rows, :],
                      act_ref.at[rows, :], state, cst)

    def one_chunk(halo, xbc_ref, cw_ref, cb_ref, dt_ref, bias_ref, alog_ref, dsk_ref, z_ref, at_ref, gs_ref,
                  y_ref, cat_ref, act_ref, state, cst):
        pre = _conv_pre(jnp.concatenate([halo, xbc_ref[...]], axis=0),
                        [cw_ref[pl.ds(j, 1), :] for j in range(CONV_K)], cb_ref[...])[HALO:]
        act_ref[...] = pre * _sigmoid(pre)
        xs_ref, b_ref, c_ref = (act_ref.at[:, pl.ds(0, AW)], act_ref.at[:, pl.ds(AW, NS)],
                                act_ref.at[:, pl.ds(AW + NS, NS)])
        s = _ssd_prep(dt_ref, bias_ref, alog_ref, dsk_ref, b_ref, c_ref, xs_ref, state, cst)
        Bm, Cm, prev = s["Bm"], s["Cm"], s["prev"]
        g = (s["cb"] * s["lam"]).astype(BF16)
        xdtb = s["xdt"].astype(BF16)
        prevb = prev.astype(BF16)
        y = _heads(lambda h: _nn(g[h], xdtb[h])) + _heads(lambda h: _nt(Cm, prevb[h])) * jnp.exp(s["cs_col"])
        y = y + s["dsk_col"] * s["x"]
        xf = (s["xdt"] * s["f"]).astype(BF16)
        new = prev * jnp.exp(s["cl"]) + _heads(lambda h: _tn(xf[h], Bm))
        for h in range(HEADS):
            y_ref[:, pl.ds(HD * h, HD)] = y[h]
            state[pl.ds(HD * h, HD), :] = new[h]
        z = z_ref[...]
        gi = y_ref[...] * (z * _sigmoid(z))
        cat_ref[:, pl.ds(0, AW)] = at_ref[...].astype(BF16)
        cat_ref[:, pl.ds(AW, AW)] = (gi * _rstd(gi) * gs_ref[...]).astype(BF16)

    vec = pl.BlockSpec((1, DT_PAD), lambda c: (0, 0))
    blk = pl.BlockSpec((R, AW), lambda c: (c, 0))
    return pl.pallas_call(
        body, name="ssd_fwd", grid=(nc // SSD_PER,),
        in_specs=[pl.BlockSpec((R, CONV_CH), lambda c: (c, 0)),
                  pl.BlockSpec((HALO, CONV_CH), lambda c: (jnp.maximum(c * per - 1, 0), 0)),
                  pl.BlockSpec((CONV_K, CONV_CH), lambda c: (0, 0)), pl.BlockSpec((1, CONV_CH), lambda c: (0, 0)),
                  pl.BlockSpec((R, DT_PAD), lambda c: (c, 6)),
                  vec, vec, vec, pl.BlockSpec((R, AW), lambda c: (c, 3)), blk, pl.BlockSpec((1, AW), lambda c: (0, 0))],
        out_specs=[blk, pl.BlockSpec((SSD_PER, AW, NS), lambda c: (c, 0, 0)), pl.BlockSpec((R, D), lambda c: (c, 0)),
                   pl.BlockSpec((R, CONV_CH), lambda c: (c, 0))],
        out_shape=[jax.ShapeDtypeStruct((T, AW), F32), jax.ShapeDtypeStruct((nc, AW, NS), F32),
                   jax.ShapeDtypeStruct((T, D), BF16), jax.ShapeDtypeStruct((T, CONV_CH), F32)],
        scratch_shapes=[pltpu.VMEM((AW, NS), F32), pltpu.VMEM((CHUNK, CHUNK), F32)],
        compiler_params=_cparams(("arbitrary",)),
    )(xbcdt, xbcdt, conv_w, conv_b, xbcdt, bias, alog, dsk, qkvz, attn, gs)


def _ssd_bwd(act, xbcdt, bias, alog, dsk, states, y_ssd, qkvz, dcat, gs):
    nc = T // CHUNK

    def body(xs_ref, b_ref, c_ref, dt_ref, bias_ref, alog_ref, dsk_ref, st_ref, y_ref, z_ref, dyn_ref, gs_ref,
             dact_ref, ddt_ref, par_ref, dz_ref, dgs_ref, dstate, cst, dy_ref):
        @pl.when(pl.program_id(0) == 0)
        def _():
            dstate[...] = jnp.zeros_like(dstate)
            par_ref[...] = jnp.zeros_like(par_ref)
            dgs_ref[...] = jnp.zeros_like(dgs_ref)
        for sub in reversed(range(SSD_PER)):
            rows = pl.ds(sub * CHUNK, CHUNK)
            one_chunk(xs_ref.at[rows, :], b_ref.at[rows, :], c_ref.at[rows, :], dt_ref.at[rows, :], bias_ref, alog_ref,
                      dsk_ref, st_ref.at[sub], y_ref.at[rows, :], z_ref.at[rows, :], dyn_ref.at[rows, :], gs_ref,
                      dact_ref.at[rows, :], ddt_ref.at[rows, :], par_ref, dz_ref.at[rows, :], dgs_ref, dstate, cst, dy_ref)

    def one_chunk(xs_ref, b_ref, c_ref, dt_ref, bias_ref, alog_ref, dsk_ref, st_ref, y_ref, z_ref, dyn_ref, gs_ref,
                  dact_ref, ddt_ref, par_ref, dz_ref, dgs_ref, dstate, cst, dy_ref):
        z, yv, dyn = z_ref[...], y_ref[...], dyn_ref[...]
        sg = _sigmoid(z)
        sz = z * sg
        gi = yv * sz
        rg = _rstd(gi)
        ng = gi * rg
        dgi = _rms_bwd(ng, rg, gs_ref[...], dyn)
        dy_ref[...] = dgi * sz
        dz_ref[...] = dgi * yv * (sg * (1.0 + z * (1.0 - sg)))
        dgs_ref[...] += _colsum(dyn * ng)
        s = _ssd_prep(dt_ref, bias_ref, alog_ref, dsk_ref, b_ref, c_ref, xs_ref, st_ref, cst)
        Bm, Cm, prev, lam, x, xdt, f, cl = s["Bm"], s["Cm"], s["prev"], s["lam"], s["x"], s["xdt"], s["f"], s["cl"]
        lane = lax.broadcasted_iota(jnp.int32, (1, DT_PAD), 1)
        row = lax.broadcasted_iota(jnp.int32, (1, CHUNK, 1), 1)
        g = s["cb"] * lam
        gb, xdtb, prevb = g.astype(BF16), xdt.astype(BF16), prev.astype(BF16)
        dy = _heads(lambda h: dy_ref[:, pl.ds(HD * h, HD)])
        dyb = dy.astype(BF16)
        dnew = _heads(lambda h: dstate[pl.ds(HD * h, HD), :])
        dnewb = dnew.astype(BF16)
        E = jnp.exp(s["cs_col"])
        ecl = jnp.exp(cl)
        dG = _heads(lambda h: _nt(dyb[h], xdtb[h]))
        dxdt = _heads(lambda h: _tn(gb[h], dyb[h]))
        Yo = _heads(lambda h: _nt(Cm, prevb[h]))
        W = _heads(lambda h: _nt(Bm, dnewb[h]))
        dcb = jnp.sum(dG * lam, axis=0)
        Mm = dG * g
        col_sums = jnp.sum(Mm, axis=1, keepdims=True)
        dYo = (dy * E).astype(BF16)
        dxdt = dxdt + W * f
        dF = jnp.sum(W * xdt, axis=2, keepdims=True) * f
        dcl = jnp.sum(dnew * prev, axis=(1, 2), keepdims=True) * ecl + jnp.sum(dF, axis=1, keepdims=True)
        dcs = (jnp.sum(Mm, axis=2, keepdims=True) + jnp.sum(dy * Yo, axis=2, keepdims=True) * E - dF
               + jnp.where(row == CHUNK - 1, dcl, 0.0))
        ddt_x = jnp.sum(dxdt * x, axis=2, keepdims=True)
        dD = jnp.sum(dy * x, axis=(1, 2), keepdims=True)
        dx = s["dsk_col"] * dy + dxdt * s["dt_col"]
        xfb = (xdt * f).astype(BF16)
        dprev = _heads(lambda h: _tn(dYo[h], Cm)) + dnew * ecl
        dcbb = dcb.astype(BF16)
        dC = _nn(dcbb, Bm)
        dB = _tn(dcbb, Cm)
        dcs_mat = -_rows_to_block([col_sums[h] for h in range(HEADS)], CHUNK, CHUNK).T
        ddt_mat = jnp.zeros((CHUNK, DT_PAD), F32)
        dD_row = jnp.zeros((1, DT_PAD), F32)
        for h in range(HEADS):
            sl = pl.ds(HD * h, HD)
            dC = dC + _nn(dYo[h], prevb[h])
            dB = dB + _nn(xfb[h], dnewb[h])
            dcs_mat = dcs_mat + jnp.where(lane == h, dcs[h], 0.0)
            ddt_mat = ddt_mat + jnp.where(lane == h, ddt_x[h], 0.0)
            dD_row = dD_row + jnp.where(lane == h, dD[h], 0.0)
            dact_ref[:, sl] = dx[h]
            dstate[sl, :] = dprev[h]
        dact_ref[:, pl.ds(AW, NS)] = dB
        dact_ref[:, pl.ds(AW + NS, NS)] = dC
        da = jnp.dot((s["li"] <= s["si"]).astype(F32), dcs_mat, precision=HIGHEST, preferred_element_type=F32)
        ddtp = jnp.where(lane < HEADS, (ddt_mat + da * s["A"]) * _sigmoid(s["dtp"]), 0.0)
        ddt_ref[...] = ddtp
        dalog = jnp.where(lane < HEADS, jnp.sum(da * s["dt"], axis=0, keepdims=True) * s["A"], 0.0)
        par_ref[...] += _rows_to_block([jnp.sum(ddtp, axis=0, keepdims=True), dalog, dD_row], 8, DT_PAD)

    vec = pl.BlockSpec((1, DT_PAD), lambda c: (0, 0))
    nstep = nc // SSD_PER
    rev = lambda c: nstep - 1 - c
    R = SSD_PER * CHUNK
    return pl.pallas_call(
        body, name="ssd_bwd", grid=(nstep,),
        in_specs=[pl.BlockSpec((R, AW), lambda c: (rev(c), 0)), pl.BlockSpec((R, NS), lambda c: (rev(c), 4)),
                  pl.BlockSpec((R, NS), lambda c: (rev(c), 5)), pl.BlockSpec((R, DT_PAD), lambda c: (rev(c), 6)),
                  vec, vec, vec,
                  pl.BlockSpec((SSD_PER, AW, NS), lambda c: (rev(c), 0, 0)), pl.BlockSpec((R, AW), lambda c: (rev(c), 0)),
                  pl.BlockSpec((R, AW), lambda c: (rev(c), 3)), pl.BlockSpec((R, AW), lambda c: (rev(c), 1)),
                  pl.BlockSpec((1, AW), lambda c: (0, 0))],
        out_specs=[pl.BlockSpec((R, CONV_CH), lambda c: (rev(c), 0)), pl.BlockSpec((R, DT_PAD), lambda c: (rev(c), 0)),
                   pl.BlockSpec((8, DT_PAD), lambda c: (0, 0)), pl.BlockSpec((R, AW), lambda c: (rev(c), 0)),
                   pl.BlockSpec((1, AW), lambda c: (0, 0))],
        out_shape=[jax.ShapeDtypeStruct((T, CONV_CH), F32), jax.ShapeDtypeStruct((T, DT_PAD), F32),
                   jax.ShapeDtypeStruct((8, DT_PAD), F32), jax.ShapeDtypeStruct((T, AW), F32),
                   jax.ShapeDtypeStruct((1, AW), F32)],
        scratch_shapes=[pltpu.VMEM((AW, NS), F32), pltpu.VMEM((CHUNK, CHUNK), F32), pltpu.VMEM((CHUNK, AW), F32)],
        compiler_params=_cparams(("arbitrary",)),
    )(act, act, act, xbcdt, bias, alog, dsk, states, y_ssd, qkvz, dcat, gs)


def _place():
    return lax.axis_index("x"), lax.axis_index("y"), lax.axis_index("c")


def _slot(px, py, pc):
    return 4 * px + 2 * py + pc


SLAB_ROWS = 24


def _slab_pack(parts, name):
    n = len(parts)

    def body(*refs):
        slab = refs[n]
        slab[...] = jnp.zeros_like(slab)
        for ref, (arr, row) in zip(refs[:n], parts):
            slab[pl.ds(row, arr.shape[0]), pl.ds(0, arr.shape[1])] = ref[...]

    vm = pl.BlockSpec(memory_space=pltpu.VMEM)
    return pl.pallas_call(
        body, name=name, in_specs=[vm] * n, out_specs=vm, out_shape=jax.ShapeDtypeStruct((SLAB_ROWS, D), F32),
    )(*[a for a, _ in parts])


_HBM = pl.BlockSpec(memory_space=pltpu.HBM)
_SEM = pl.BlockSpec(memory_space=pltpu.SEMAPHORE)
_EFFECT = pltpu.SideEffectType.DATAFLOW_SIDE_EFFECTING


def _peers(x, y, c):
    out = []
    for kk in range(1, N_DEV):
        fx, fy, fc = kk >> 2 & 1, kk >> 1 & 1, kk & 1
        out.append((1 - x if fx else x, 1 - y if fy else y, 1 - c if fc else c))
    return out


def _send_start(src, per_peer, name, dep):
    (handles, token) = _send_start_many([src], per_peer, name, dep)
    return handles, token


def _near_peers(x, y, c):
    return [(x, y, 1 - c), (1 - x, y, c), (x, 1 - y, c), (1 - x, 1 - y, c)]


def _send_start_many(srcs, per_peer, name, dep, peers=_peers, npeers=N_DEV - 1):
    n = len(srcs)

    def body(*refs):
        src_refs, land_refs = refs[:n], refs[n:2 * n]
        send_sems, recv_sems = refs[2 * n + 1], refs[2 * n + 2]
        token = refs[-1]
        x, y, c = _place()
        mine = _slot(x, y, c)
        for a in range(n):
            for kk, peer in enumerate(peers(x, y, c)):
                pltpu.make_async_remote_copy(
                    src_ref=src_refs[a].at[_slot(*peer)] if per_peer else src_refs[a], dst_ref=land_refs[a].at[mine],
                    send_sem=send_sems.at[a * npeers + kk], recv_sem=recv_sems.at[a * npeers + kk],
                    device_id=peer, device_id_type=MESH).start()
        token[...] = jnp.zeros_like(token)

    lands = [lax.empty((N_DEV,) + tuple(s.shape[1:] if per_peer else s.shape), s.dtype) for s in srcs]
    hbm = lambda t: pltpu.with_memory_space_constraint(t, pltpu.HBM)
    outs = pl.pallas_call(
        body, name=name,
        out_shape=(pltpu.SemaphoreType.DMA((n * npeers,)), pltpu.SemaphoreType.DMA((n * npeers,)),
                   *[pltpu.HBM(s.shape, s.dtype) for s in srcs], *[pltpu.HBM(l.shape, l.dtype) for l in lands],
                   jax.ShapeDtypeStruct((8, 128), F32)),
        in_specs=(*[_HBM] * (2 * n), _ANY),
        out_specs=(_SEM, _SEM, *[_HBM] * (2 * n), pl.BlockSpec(memory_space=pltpu.VMEM)),
        input_output_aliases={i: 2 + i for i in range(2 * n)},
        compiler_params=pltpu.CompilerParams(has_side_effects=_EFFECT),
    )(*[hbm(s) for s in srcs], *[hbm(l) for l in lands], dep)
    return (outs[0], outs[1], list(outs[2:2 + n]), list(outs[2 + n:2 + 2 * n])), outs[-1]


def _send_wait(handles, after, name):
    srcs, lands = _send_wait_many(handles, after, name)
    return srcs[0], lands[0]


def _send_wait_many(handles, after, name, npeers=N_DEV - 1):
    send_sems, recv_sems, src_thrus, land_thrus = handles
    n = len(src_thrus)

    def body(*refs):
        land_refs = refs[n:2 * n]
        send_sems, recv_sems = refs[2 * n], refs[2 * n + 1]
        me = _place()
        for a in range(n):
            for kk in range(npeers):
                cp = pltpu.make_async_remote_copy(
                    src_ref=land_refs[a].at[0], dst_ref=land_refs[a].at[0],
                    send_sem=send_sems.at[a * npeers + kk], recv_sem=recv_sems.at[a * npeers + kk],
                    device_id=me, device_id_type=MESH)
                cp.wait_send()
                cp.wait_recv()

    both = list(src_thrus) + list(land_thrus)
    outs = pl.pallas_call(
        body, name=name,
        out_shape=tuple(pltpu.HBM(t.shape, t.dtype) for t in both),
        in_specs=(*[_HBM] * (2 * n), _SEM, _SEM, _ANY), out_specs=tuple([_HBM] * (2 * n)),
        input_output_aliases={i: i for i in range(2 * n)},
        compiler_params=pltpu.CompilerParams(has_side_effects=_EFFECT),
    )(*both, send_sems, recv_sems, after)
    return list(outs[:n]), list(outs[n:])


def _forward_start(lands, name, dep):
    n = len(lands)

    def body(*refs):
        land_refs = refs[:n]
        send_sems, recv_sems = refs[n + 1], refs[n + 2]
        token = refs[-1]
        x, y, c = _place()
        for a in range(n):
            for j, chip in enumerate([(1 - x, y), (x, 1 - y), (1 - x, 1 - y)]):
                blk = land_refs[a].at[_slot(*chip, c)]
                pltpu.make_async_remote_copy(
                    src_ref=blk, dst_ref=blk, send_sem=send_sems.at[a * 3 + j], recv_sem=recv_sems.at[a * 3 + j],
                    device_id=(x, y, 1 - c), device_id_type=MESH).start()
        token[...] = jnp.zeros_like(token)

    outs = pl.pallas_call(
        body, name=name,
        out_shape=(pltpu.SemaphoreType.DMA((n * 3,)), pltpu.SemaphoreType.DMA((n * 3,)),
                   *[pltpu.HBM(l.shape, l.dtype) for l in lands], jax.ShapeDtypeStruct((8, 128), F32)),
        in_specs=(*[_HBM] * n, _ANY), out_specs=(_SEM, _SEM, *[_HBM] * n, pl.BlockSpec(memory_space=pltpu.VMEM)),
        input_output_aliases={i: 2 + i for i in range(n)},
        compiler_params=pltpu.CompilerParams(has_side_effects=_EFFECT),
    )(*lands, dep)
    return (outs[0], outs[1], list(outs[2:2 + n])), outs[-1]


def _forward_wait(handles, after, name):
    send_sems, recv_sems, land_thrus = handles
    n = len(land_thrus)

    def body(*refs):
        land_refs = refs[:n]
        send_sems, recv_sems = refs[n], refs[n + 1]
        me = _place()
        for a in range(n):
            for j in range(3):
                cp = pltpu.make_async_remote_copy(
                    src_ref=land_refs[a].at[0], dst_ref=land_refs[a].at[0],
                    send_sem=send_sems.at[a * 3 + j], recv_sem=recv_sems.at[a * 3 + j], device_id=me, device_id_type=MESH)
                cp.wait_send()
                cp.wait_recv()

    outs = pl.pallas_call(
        body, name=name,
        out_shape=tuple(pltpu.HBM(t.shape, t.dtype) for t in land_thrus),
        in_specs=(*[_HBM] * n, _SEM, _SEM, _ANY), out_specs=tuple([_HBM] * n),
        input_output_aliases={i: i for i in range(n)},
        compiler_params=pltpu.CompilerParams(has_side_effects=_EFFECT),
    )(*land_thrus, send_sems, recv_sems, after)
    return list(outs)


def _sum_slots(land, name):
    _, R, C = land.shape
    tm = R if R <= 512 else 512

    def body(x_ref, o_ref):
        acc = x_ref[0].astype(F32)
        for j in range(1, N_DEV):
            acc = acc + x_ref[j].astype(F32)
        o_ref[...] = acc

    return pl.pallas_call(
        body, name=name, grid=(R // tm,),
        in_specs=[pl.BlockSpec((N_DEV, tm, C), lambda i: (0, i, 0))], out_specs=pl.BlockSpec((tm, C), lambda i: (i, 0)),
        out_shape=jax.ShapeDtypeStruct((R, C), F32), compiler_params=_cparams(("parallel",)),
    )(land)


def _adam_math(w, g, m, v):
    m2 = ADAM_B1 * m + (1.0 - ADAM_B1) * g
    v2 = ADAM_B2 * v + (1.0 - ADAM_B2) * (g * g)
    m_hat = m2 / (1.0 - ADAM_B1 ** ADAM_STEP)
    v_hat = v2 / (1.0 - ADAM_B2 ** ADAM_STEP)
    delta = -ADAM_LR * (m_hat / (jnp.sqrt(v_hat) + ADAM_EPS) + ADAM_WD * w)
    return delta, m2, v2


def _adamw(w, g, m, v, name):
    R, C = w.shape
    tm = R if R <= 512 else 256
    return _rowwise(lambda w, g, m, v: (_adam_math(w, g, m, v), ()), [w, g, m, v], [], [(C, F32)] * 3, [], tm=tm, name=name,
                    in_buffers=3 if R // tm >= 3 else 2)


def _adamw_slots(land, w, m, v, name):
    _, R, C = land.shape
    tm = R if R <= 256 else 256

    def body(x_ref, w_ref, m_ref, v_ref, g_ref, d_ref, mo_ref, vo_ref):
        g = x_ref[0].astype(F32)
        for j in range(1, N_DEV):
            g = g + x_ref[j].astype(F32)
        d, m2, v2 = _adam_math(w_ref[...], g, m_ref[...], v_ref[...])
        g_ref[...] = g
        d_ref[...] = d
        mo_ref[...] = m2
        vo_ref[...] = v2

    row = pl.BlockSpec((tm, C), lambda i: (i, 0))
    return pl.pallas_call(
        body, name=name, grid=(R // tm,),
        in_specs=[pl.BlockSpec((N_DEV, tm, C), lambda i: (0, i, 0)), row, row, row], out_specs=[row] * 4,
        out_shape=[jax.ShapeDtypeStruct((R, C), F32)] * 4, compiler_params=_cparams(("parallel",)),
    )(land, w, m, v)


def _adamw_small(slab, slab_rows, g_conv_w, ws, ms, vs):
    n = len(ws)

    def body(*refs):
        slab_ref, gc_ref = refs[0], refs[1]
        w_refs, m_refs, v_refs = refs[2:2 + n], refs[2 + n:2 + 2 * n], refs[2 + 2 * n:2 + 3 * n]
        outs = refs[2 + 3 * n:]
        loss_ref = outs[0]
        g_out, d_out, m_out, v_out = (outs[1 + i * n:1 + (i + 1) * n] for i in range(4))
        loss_ref[...] = jnp.sum(slab_ref[pl.ds(6, 1), :], axis=1, keepdims=True)
        for i in range(n):
            g = gc_ref[...] if i == n - 1 else slab_ref[pl.ds(slab_rows[i], 1), pl.ds(0, ws[i].shape[1])]
            d, m2, v2 = _adam_math(w_refs[i][...], g, m_refs[i][...], v_refs[i][...])
            g_out[i][...] = g
            d_out[i][...] = d
            m_out[i][...] = m2
            v_out[i][...] = v2

    vm = pl.BlockSpec(memory_space=pltpu.VMEM)
    shapes = [jax.ShapeDtypeStruct(w.shape, F32) for w in ws]
    outs = pl.pallas_call(
        body, name="adamw_small", in_specs=[vm] * (2 + 3 * n), out_specs=[vm] * (1 + 4 * n),
        out_shape=[jax.ShapeDtypeStruct((1, 1), F32)] + shapes * 4,
    )(slab, g_conv_w, *ws, *ms, *vs)
    return outs[0], outs[1:1 + n], outs[1 + n:1 + 2 * n], outs[1 + 2 * n:1 + 3 * n], outs[1 + 3 * n:]


SMALL = ["norm_mix_pre", "norm_mix_post", "norm_mlp_pre", "norm_mlp_post", "norm_ple_post",
         "conv_b", "ssd_norm_g", "dt_bias", "a_log", "d_skip"]


def _pad_row(v, width=D):
    return jnp.pad(v, ((0, 0), (0, width - v.shape[1])))


def kernel(x, p, positions, norm_mix_pre, norm_mix_post, w_in, conv_w, conv_b, dt_bias, a_log, d_skip, ssd_norm_g, w_out, norm_mlp_pre, norm_mlp_post, w_up, w_down, w_ple_gate, w_ple_proj, norm_ple_post, loss_target, m_norm_mix_pre, m_norm_mix_post, m_w_in, m_conv_w, m_conv_b, m_dt_bias, m_a_log, m_d_skip, m_ssd_norm_g, m_w_out, m_norm_mlp_pre, m_norm_mlp_post, m_w_up, m_w_down, m_w_ple_gate, m_w_ple_proj, m_norm_ple_post, v_norm_mix_pre, v_norm_mix_post, v_w_in, v_conv_w, v_conv_b, v_dt_bias, v_a_log, v_d_skip, v_ssd_norm_g, v_w_out, v_norm_mlp_pre, v_norm_mlp_post, v_w_up, v_w_down, v_w_ple_gate, v_w_ple_proj, v_norm_ple_post):
    args = dict(locals())
    x2, p2, tgt = x[0], p[0, 0], loss_target[0]
    g1, g2, g3, g4, g5 = norm_mix_pre, norm_mix_post, norm_mlp_pre, norm_mlp_post, norm_ple_post

    me = _slot(*_place())
    pack_in = jnp.pad(w_in[0].T, ((0, W_IN_SHARD_PAD - W_IN_SHARD), (0, 0))).astype(BF16)
    rest = [w_out[0].astype(BF16), w_up[0].T.astype(BF16), w_down[0].astype(BF16), w_ple_gate[0].astype(BF16),
            w_ple_proj[0].T.reshape(32, D).astype(BF16)]
    conv_pack = jnp.pad(conv_w[0], ((0, 4), (0, 32)))
    in_handles, tok_in0 = _send_start_many([pack_in, conv_pack], False, "gather_in_start", g1, peers=_near_peers, npeers=4)

    inv_freq = ROPE_THETA ** (-jnp.arange(HD // 2, dtype=F32) * 2.0 / HD)
    pos = positions[0] + tok_in0[0, 0].astype(jnp.int32)
    ang = pos.astype(F32)[:, None] * jnp.tile(inv_freq, 4)
    cos128 = jnp.cos(ang)
    sin128 = jnp.sin(ang) * jnp.tile(jnp.concatenate([-jnp.ones(HD // 2, F32), jnp.ones(HD // 2, F32)]), 2)

    bias_w, alog_w, dsk_w = _pad_row(dt_bias, DT_PAD), _pad_row(a_log, DT_PAD), _pad_row(d_skip, DT_PAD)

    (u1,) = _rowwise(lambda a, g: ((a * _rstd(a) * g,), ()), [x2], [g1], [(D, BF16)], [], tm=512, name="norm_x",
                     deps=[cos128, sin128], in_buffers=3)
    p2b = p2.astype(BF16)

    in_back, in_land = _send_wait_many(in_handles, u1, "gather_in_wait", npeers=4)
    fw_handles, tok_fw = _forward_start(in_land, "gather_in_forward", u1)
    in_land = _forward_wait(fw_handles, tok_fw, "gather_in_forward_wait")
    gin = lax.dynamic_update_slice(in_land[0], in_back[0][None], (me, 0, 0))
    gconv = lax.dynamic_update_slice(in_land[1], in_back[1][None], (me, 0, 0))
    rest_handles, tok_rest = _send_start_many(rest, False, "gather_rest_start", gconv)
    w_inT = gin[:, :W_IN_SHARD].reshape(IN_W, D)
    w_qkvzT = w_inT[:4 * AW]
    w_xbcdtT = jnp.pad(w_inT[4 * AW:], ((0, DT_PAD - HEADS), (0, 0)))
    conv_full = gconv[:, :CONV_K, :96].transpose(1, 0, 2).reshape(CONV_K, CONV_CH)
    qkvz, xbcdt = _mm_rows(lambda a, b: ((a, b), ()), [(u1, w_qkvzT, True), (u1, w_xbcdtT, True)], [], [],
                           [(4 * AW, F32), (CONV_CH + DT_PAD, F32)], [], tm=512, name="proj_in", deps=[tok_rest])

    qkv = _rope_fwd(qkvz, cos128, sin128)
    qkv = [qkv[3 * i:3 * i + 3] for i in range(len(DILATIONS))]
    outs, lses = [], []
    for d, (qd, kd, vd) in zip(DILATIONS, qkv):
        o, l = _attn_fwd(qd, kd, vd, d)
        outs.append(o)
        lses.append(l)
    attn, lse, attn4, lse4, attn16, lse16 = _attn_merge(outs, lses)

    y_ssd, states, cat, act = _ssd_fwd(xbcdt, conv_full, conv_b, bias_w, alog_w, dsk_w, qkvz, attn, ssd_norm_g)


    rest_back, landed = _send_wait_many(rest_handles, cat, "gather_rest_wait")
    landed = [lax.dynamic_update_slice(l, b[None], (me, 0, 0)) for l, b in zip(landed, rest_back)]
    w_o, w_upT, w_dn, w_gate = landed[0].reshape(D, D), landed[1].reshape(DFF, D), landed[2].reshape(DFF, D), landed[3].reshape(D, D)
    w_projT = landed[4].reshape(D, PLE)

    def post1(mm, xx, ga):
        h = xx + mm * _rstd(mm) * ga
        return (mm, h, _rstd(h)), ()
    mix, h1, r3 = _mm_rows(post1, [(cat, w_o, False)], [x2], [g2], [(D, F32), (D, F32), (1, F32)], [], tm=512,
                           name="mix_out")

    a_up, ff, u2, h2, h2b = _mlp_fwd(h1, r3, g3, w_upT, w_dn, g4)
    relu2 = lambda a: jnp.square(jnp.maximum(a.astype(F32), 0.0))

    def final(gpre, ppv, hh, tg, g):
        sg = _sigmoid(gpre)
        ple = ppv * sg
        r = _rstd(ple)
        n = ple * r
        h3 = hh + n * g
        e = h3 - tg
        dh3 = e * (1.0 / D)
        dple = _rms_bwd(n, r, g, dh3)
        return (dh3, dple * sg, dple * ppv * sg * (1.0 - sg)), (_colsum(dh3 * n), _colsum(0.5 * e * e * (1.0 / D)))
    dh3, dpp, dgp, dg5, loss_vec = _mm_rows(final, [(h2b, w_gate, False), (p2b, w_projT, True)], [h2, tgt], [g5],
                                            [(D, F32), (D, BF16), (D, BF16)], [(1, D), (1, D)], tm=512, name="ple_loss")

    gw_projT = _mm(dpp, p2b, ta=True, tm=512, tn=256, tk=T, out_dtypes=(BF16,), name="gw_ple_proj")
    gw_gate = _mm(h2b, dgp, ta=True, tm=512, tn=1024, tk=T, out_dtypes=(BF16,), name="gw_ple_gate")
    def bwd_mlp_post(dg_, d3, f, g):
        dh2 = d3 + dg_
        r = _rstd(f)
        n = f * r
        return (dh2, _rms_bwd(n, r, g, dh2)), (_colsum(dh2 * n),)
    dh2, dff, dg4 = _mm_rows(bwd_mlp_post, [(dgp, w_gate, True)], [dh3, ff], [g4], [(D, F32), (D, BF16)], [(1, D)],
                             tm=512, name="bwd_ple_gate")

    gw_dn = _mm(a_up, dff, ta=True, tm=512, tn=1024, tk=T, a_pre=relu2, out_dtypes=(BF16,), name="gw_mlp_down")
    rs_a, tok_a = _send_start_many([gw_projT.reshape(N_DEV, 32, D), gw_gate.reshape(N_DEV, 128, D),
                                    gw_dn.reshape(N_DEV, 512, D)], True, "rs_start_a", g1)
    da_up, du2 = _mlp_dx(dff, a_up, w_upT, w_dn, tok_a)
    gw_upT = _mm(da_up, u2, ta=True, tm=512, tn=1024, tk=T, out_dtypes=(BF16,), name="gw_mlp_up")

    def bwd_mix_post(d2, du, hh, rr, mm, ga, gb):
        n3 = hh * rr
        dh1 = d2 + _rms_bwd(n3, rr, gb, du)
        r = _rstd(mm)
        n2 = mm * r
        return (dh1, _rms_bwd(n2, r, ga, dh1)), (_colsum(du * n3), _colsum(dh1 * n2))
    dh1, dmix, dg3, dg2 = _rowwise(bwd_mix_post, [dh2, du2, h1, r3, mix], [g2, g3], [(D, F32), (D, BF16)],
                                   [(1, D), (1, D)], tm=512, name="bwd_post_mix")

    gw_o = _mm(cat, dmix, ta=True, tm=512, tn=1024, tk=T, out_dtypes=(BF16,), name="gw_out")
    rs_b, tok_b = _send_start_many([gw_upT.reshape(N_DEV, 512, D), gw_o.reshape(N_DEV, 128, D)], True, "rs_start_b", g1)
    dcat, dattn4, dattn16 = _dx_out(dmix, w_o, tok_b)

    dact, ddtw, ssd_par, dz, dgs = _ssd_bwd(act, xbcdt, bias_w, alog_w, dsk_w, states, y_ssd, qkvz, dcat, ssd_norm_g)
    dxbcdt, conv_par = _conv_bwd(xbcdt, dact, ddtw, conv_full, conv_b)

    qkv_grads = [_attn_bwd(*qkv[0], dcat, attn, lse, 1),
                 _attn_bwd(*qkv[1], dattn4, attn4, lse4, 4),
                 _attn_bwd(*qkv[2], dattn16, attn16, lse16, 16)]
    dqkvz = _rope_bwd(qkv_grads, dz, cos128, sin128)

    gw_qkvzT = _mm(dqkvz, u1, ta=True, tm=512, tn=1024, tk=T, out_dtypes=(BF16,), name="gw_qkvz")
    gw_xbcdtT = _mm(dxbcdt, u1, ta=True, tm=896, tn=1024, tk=T, out_dtypes=(BF16,), name="gw_xbcdt")
    gw_inT = jnp.concatenate([gw_qkvzT, gw_xbcdtT], axis=0)[:IN_W]
    gw_inT = jnp.pad(gw_inT.reshape(N_DEV, W_IN_SHARD, D), ((0, 0), (0, W_IN_SHARD_PAD - W_IN_SHARD), (0, 0)))
    rs_in, tok_in = _send_start(gw_inT, True, "rs_start_w_in", g1)

    def bwd_in(ua, ub, d1, xx, g):
        rr = _rstd(xx)
        n = xx * rr
        du = ua + ub
        return (d1 + _rms_bwd(n, rr, g, du),), (_colsum(du * n),)
    grad_x, dg1 = _mm_rows(bwd_in, [(dqkvz, w_qkvzT, False), (dxbcdt, w_xbcdtT, False)], [dh1, x2], [g1],
                           [(D, F32)], [(1, D)], tm=512, name="bwd_in_proj", deps=[tok_in])

    my_slab = _slab_pack([(dg1, 0), (dg2, 1), (dg3, 2), (dg4, 3), (dg5, 4), (dgs, 5), (loss_vec, 6),
                          (conv_par, 8), (ssd_par, 16)], "slab_pack")
    slab_handles, tok_slab = _send_start_many([my_slab], False, "slab_start", g1)

    def scatter_finish(handles, nm, after):
        part, land = _send_wait(handles, after, "rs_wait_" + nm)
        own = lax.dynamic_slice(part, (me, 0, 0), (1,) + part.shape[1:])
        return _sum_slots(lax.dynamic_update_slice(land, own, (me, 0, 0)), "rs_sum_" + nm)
    def landed(handles, after, wait_name):
        parts, lands = _send_wait_many(handles, after, wait_name)
        return [lax.dynamic_update_slice(land, lax.dynamic_slice(part, (me, 0, 0), (1,) + part.shape[1:]), (me, 0, 0))
                for part, land in zip(parts, lands)]
    land_proj, land_gate, land_dn = landed(rs_a, tok_slab, "rs_wait_a")
    land_up, land_out = landed(rs_b, tok_slab, "rs_wait_b")

    grads, delta, new_m, new_v = {}, {}, {}, {}
    for nme, land in (("w_down", land_dn), ("w_out", land_out), ("w_ple_gate", land_gate)):
        outs4 = _adamw_slots(land, args[nme][0], args["m_" + nme][0], args["v_" + nme][0], "adamw_" + nme)
        grads[nme], delta[nme], new_m[nme], new_v[nme] = [t[None] for t in outs4]
    grads["w_up"] = _sum_slots(land_up, "rs_sum_w_up").T[None]
    grads["w_ple_proj"] = _sum_slots(land_proj, "rs_sum_w_proj").reshape(128, PLE).T[None]
    for nme in ["w_up", "w_ple_proj", "w_in"]:
        if nme == "w_in":
            g_inT = scatter_finish(rs_in, "w_in", delta["w_down"])
            grads["w_in"] = g_inT[:W_IN_SHARD].T[None]
        dl, mm_, vv_ = _adamw(args[nme][0], grads[nme][0], args["m_" + nme][0], args["v_" + nme][0], "adamw_" + nme)
        delta[nme], new_m[nme], new_v[nme] = dl[None], mm_[None], vv_[None]

    slab_back, slab_land = _send_wait_many(slab_handles, delta["w_in"], "slab_wait")
    slab = _sum_slots(lax.dynamic_update_slice(slab_land[0], slab_back[0][None], (me, 0, 0)), "slab_sum")
    g_conv_w = lax.dynamic_slice(slab[8:12, :CONV_CH], (0, me * 96), (CONV_K, 96))
    small_names = SMALL + ["conv_w"]
    small_rows = [0, 1, 2, 3, 4, 12, 5, 16, 17, 18, None]
    pick = lambda prefix: [args[prefix + nme] for nme in SMALL] + [args[prefix + "conv_w"][0]]
    loss11, g_s, d_s, m_s, v_s = _adamw_small(slab, small_rows, g_conv_w, pick(""), pick("m_"), pick("v_"))
    loss = loss11[0, 0]
    for i, nme in enumerate(small_names):
        lead = (lambda t: t[None]) if nme == "conv_w" else (lambda t: t)
        grads[nme], delta[nme], new_m[nme], new_v[nme] = lead(g_s[i]), lead(d_s[i]), lead(m_s[i]), lead(v_s[i])

    order = ["norm_mix_pre", "norm_mix_post", "w_in", "conv_w", "conv_b", "dt_bias", "a_log", "d_skip", "ssd_norm_g",
             "w_out", "norm_mlp_pre", "norm_mlp_post", "w_up", "w_down", "w_ple_gate", "w_ple_proj", "norm_ple_post"]
    return (loss, grad_x[None], *[grads[n] for n in order], *[delta[n] for n in order],
            *[new_m[n] for n in order], *[new_v[n] for n in order])
```

```python
import jax
import jax.numpy as jnp
from jax import lax
from jax.experimental import pallas as pl
from jax.experimental.pallas import tpu as pltpu

F32 = jnp.float32
BF16 = jnp.bfloat16
MESH = pl.DeviceIdType.MESH
HIGHEST = lax.Precision.HIGHEST

N_DEV = 8
T = 4096
D = 1024
HEADS = 8
HD = 64
AW = 512
NS = 128
CONV_K = 4
CONV_CH = 768
CHUNK = 128
SSD_PER = 2
DFF = 4096
PLE = 256
EPS = 1e-6
ROPE_THETA = 10000.0
DILATIONS = (1, 4, 16)
QBLK = 128
NEG = -1e30
IN_W = 2824
W_IN_SHARD = 353
W_IN_SHARD_PAD = 384
DT_PAD = 128

ADAM_LR, ADAM_B1, ADAM_B2, ADAM_EPS, ADAM_WD, ADAM_STEP = 0.001, 0.9, 0.999, 1e-08, 0.01, 10

VMEM_LIMIT = 56 * 1024 * 1024


_ANY = pl.BlockSpec(memory_space=pl.ANY)


def _cparams(sem=None):
    return pltpu.CompilerParams(dimension_semantics=sem, vmem_limit_bytes=VMEM_LIMIT)


def _dot(a, b, ca, cb, precision=None):
    return lax.dot_general(a, b, (((ca,), (cb,)), ((), ())), preferred_element_type=F32, precision=precision)


def _nn(a, b):
    return _dot(a, b, 1, 0)


def _nt(a, b):
    return _dot(a, b, 1, 1)


def _tn(a, b):
    return _dot(a, b, 0, 0)


def _sigmoid(x):
    return 1.0 / (1.0 + jnp.exp(-x))


def _softplus(x):
    return jnp.maximum(x, 0.0) + jnp.log(1.0 + jnp.exp(-jnp.abs(x)))


def _mm(a, b, *, ta=False, tb=False, tm, tn, tk, name,
        a_pre=None, a_rows=(), a_cols=(), b_pre=None, b_rows=(), b_cols=(),
        epi=None, epi_tiles=(), out_dtypes=(F32,), deps=()):
    if ta:
        K, M = a.shape
    else:
        M, K = a.shape
    if tb:
        N, K2 = b.shape
    else:
        K2, N = b.shape
    assert K == K2 and M % tm == 0 and N % tn == 0 and K % tk == 0, (name, a.shape, b.shape)
    nk = K // tk
    if ta:
        a_spec = pl.BlockSpec((tk, tm), lambda i, j, k: (k, i))
        a_row_specs = [pl.BlockSpec((tk, 1), lambda i, j, k: (k, 0)) for _ in a_rows]
        a_col_specs = [pl.BlockSpec((1, tm), lambda i, j, k: (0, i)) for _ in a_cols]
    else:
        a_spec = pl.BlockSpec((tm, tk), lambda i, j, k: (i, k))
        a_row_specs = [pl.BlockSpec((tm, 1), lambda i, j, k: (i, 0)) for _ in a_rows]
        a_col_specs = [pl.BlockSpec((1, tk), lambda i, j, k: (0, k)) for _ in a_cols]
    if tb:
        b_spec = pl.BlockSpec((tn, tk), lambda i, j, k: (j, k))
        b_row_specs = [pl.BlockSpec((tn, 1), lambda i, j, k: (j, 0)) for _ in b_rows]
        b_col_specs = [pl.BlockSpec((1, tk), lambda i, j, k: (0, k)) for _ in b_cols]
    else:
        b_spec = pl.BlockSpec((tk, tn), lambda i, j, k: (k, j))
        b_row_specs = [pl.BlockSpec((tk, 1), lambda i, j, k: (k, 0)) for _ in b_rows]
        b_col_specs = [pl.BlockSpec((1, tn), lambda i, j, k: (0, j)) for _ in b_cols]
    o_spec = pl.BlockSpec((tm, tn), lambda i, j, k: (i, j))
    na, nb, ne, no = len(a_rows) + len(a_cols), len(b_rows) + len(b_cols), len(epi_tiles), len(out_dtypes)

    def body(*refs):
        a_ref, b_ref = refs[0], refs[1]
        a_ex = refs[2:2 + na]
        b_ex = refs[2 + na:2 + na + nb]
        e_ex = refs[2 + na + nb:2 + na + nb + ne]
        first_out = 2 + na + nb + ne + len(deps)
        outs = refs[first_out:first_out + no]

        def finish(res):
            vals = epi(res, *[r[...] for r in e_ex]) if epi is not None else (res,)
            for o_ref, val in zip(outs, vals):
                o_ref[...] = val.astype(o_ref.dtype)

        at = a_ref[...]
        if a_pre is not None:
            at = a_pre(at, *[r[...] for r in a_ex])
        bt = b_ref[...]
        if b_pre is not None:
            bt = b_pre(bt, *[r[...] for r in b_ex])
        prod = _dot(at.astype(BF16), bt.astype(BF16), 0 if ta else 1, 1 if tb else 0)
        if nk == 1:
            finish(prod)
            return
        acc = refs[-1]
        k = pl.program_id(2)

        @pl.when(k == 0)
        def _():
            acc[...] = jnp.zeros_like(acc)
        acc[...] += prod

        @pl.when(k == nk - 1)
        def _():
            finish(acc[...])

    outs = pl.pallas_call(
        body, name=name,
        grid=(M // tm, N // tn, nk),
        in_specs=([a_spec, b_spec] + a_row_specs + a_col_specs + b_row_specs + b_col_specs + [o_spec] * ne
                  + [_ANY] * len(deps)),
        out_specs=[o_spec] * no,
        out_shape=[jax.ShapeDtypeStruct((M, N), dt) for dt in out_dtypes],
        scratch_shapes=[pltpu.VMEM((tm, tn), F32)] if nk > 1 else [],
        compiler_params=_cparams(("parallel", "parallel", "arbitrary")),
    )(a, b, *a_rows, *a_cols, *b_rows, *b_cols, *epi_tiles, *deps)
    return outs[0] if no == 1 else outs


MLP_TM = 1024
MLP_TC = 512


def _mlp_fwd(h, r, g, w_upT, w_dn, g_post):
    nc = DFF // MLP_TC

    def body(h_ref, r_ref, g_ref, wu_ref, wd_ref, gp_ref, a_ref, ff_ref, u_ref, ho_ref, hob_ref, acc, u_scr):
        c = pl.program_id(1)

        @pl.when(c == 0)
        def _():
            u = (h_ref[...] * r_ref[...] * g_ref[...]).astype(BF16)
            u_scr[...] = u
            u_ref[...] = u
            acc[...] = jnp.zeros_like(acc)
        a = _nt(u_scr[...], wu_ref[...])
        a_ref[...] = a.astype(BF16)
        acc[...] += _nn(jnp.square(jnp.maximum(a, 0.0)).astype(BF16), wd_ref[...])

        @pl.when(c == nc - 1)
        def _():
            f = acc[...]
            ff_ref[...] = f
            ho = h_ref[...] + f * _rstd(f) * gp_ref[...]
            ho_ref[...] = ho
            hob_ref[...] = ho.astype(BF16)

    row = pl.BlockSpec((MLP_TM, D), lambda i, c: (i, 0))
    wsp = pl.BlockSpec((MLP_TC, D), lambda i, c: (c, 0))
    vec = pl.BlockSpec((1, D), lambda i, c: (0, 0))
    return pl.pallas_call(
        body, name="mlp_fwd", grid=(T // MLP_TM, nc),
        in_specs=[row, pl.BlockSpec((MLP_TM, 1), lambda i, c: (i, 0)), vec, wsp, wsp, vec],
        out_specs=[pl.BlockSpec((MLP_TM, MLP_TC), lambda i, c: (i, c)), row, row, row, row],
        out_shape=[jax.ShapeDtypeStruct((T, DFF), BF16), jax.ShapeDtypeStruct((T, D), F32), jax.ShapeDtypeStruct((T, D), BF16),
                   jax.ShapeDtypeStruct((T, D), F32), jax.ShapeDtypeStruct((T, D), BF16)],
        scratch_shapes=[pltpu.VMEM((MLP_TM, D), F32), pltpu.VMEM((MLP_TM, D), BF16)],
        compiler_params=_cparams(("parallel", "arbitrary")),
    )(h, r, g, w_upT, w_dn, g_post)


def _mlp_dx(dff, a, w_upT, w_dn, dep):
    nc = DFF // MLP_TC

    def body(d_ref, a_ref, wu_ref, wd_ref, dep_ref, da_ref, du_ref, acc, d_scr):
        c = pl.program_id(1)

        @pl.when(c == 0)
        def _():
            d_scr[...] = d_ref[...].astype(BF16)
            acc[...] = jnp.zeros_like(acc)
        da = (_nt(d_scr[...], wd_ref[...]) * (2.0 * jnp.maximum(a_ref[...].astype(F32), 0.0))).astype(BF16)
        da_ref[...] = da
        acc[...] += _nn(da, wu_ref[...])

        @pl.when(c == nc - 1)
        def _():
            du_ref[...] = acc[...]

    row = pl.BlockSpec((MLP_TM, D), lambda i, c: (i, 0))
    wsp = pl.BlockSpec((MLP_TC, D), lambda i, c: (c, 0))
    chunk = pl.BlockSpec((MLP_TM, MLP_TC), lambda i, c: (i, c))
    return pl.pallas_call(
        body, name="mlp_dx", grid=(T // MLP_TM, nc),
        in_specs=[row, chunk, wsp, wsp, _ANY], out_specs=[chunk, row],
        out_shape=[jax.ShapeDtypeStruct((T, DFF), BF16), jax.ShapeDtypeStruct((T, D), F32)],
        scratch_shapes=[pltpu.VMEM((MLP_TM, D), F32), pltpu.VMEM((MLP_TM, D), BF16)],
        compiler_params=_cparams(("parallel", "arbitrary")),
    )(dff, a, w_upT, w_dn, dep)


def _rowwise(fn, rows, vecs, out_rows, out_sums, *, tm, name, deps=(), in_buffers=2):
    specs, arrs = [], []
    R = None
    mode = {} if in_buffers == 2 else dict(pipeline_mode=pl.Buffered(in_buffers))
    for r in rows:
        if isinstance(r, tuple):
            arr, width, cb = r
            specs.append(pl.BlockSpec((tm, width), lambda i, cb=cb: (i, cb), **mode))
        else:
            arr = r
            specs.append(pl.BlockSpec((tm, arr.shape[1]), lambda i: (i, 0), **mode))
        R = arr.shape[0] if R is None else R
        assert arr.shape[0] == R, name
        arrs.append(arr)
    assert R % tm == 0, name
    for v in vecs:
        specs.append(pl.BlockSpec(v.shape, lambda i: (0, 0)))
        arrs.append(v)
    nr, nv, no, ns = len(rows), len(vecs), len(out_rows), len(out_sums)
    out_specs = [pl.BlockSpec((tm, w), lambda i: (i, 0)) for w, _ in out_rows]
    out_specs += [pl.BlockSpec(s, lambda i: (0, 0)) for s in out_sums]
    out_shape = [jax.ShapeDtypeStruct((R, w), dt) for w, dt in out_rows]
    out_shape += [jax.ShapeDtypeStruct(s, F32) for s in out_sums]

    nd = len(deps)

    def body(*refs):
        ins = [r[...] for r in refs[:nr + nv]]
        o_refs = refs[nr + nv + nd:nr + nv + nd + no]
        s_refs = refs[nr + nv + nd + no:]
        o_vals, s_vals = fn(*ins)
        for ref, val in zip(o_refs, o_vals):
            ref[...] = val.astype(ref.dtype)
        if ns:
            @pl.when(pl.program_id(0) == 0)
            def _():
                for ref in s_refs:
                    ref[...] = jnp.zeros_like(ref)
            for ref, val in zip(s_refs, s_vals):
                ref[...] += val

    if in_buffers != 2:
        assert ns == 0, name

        def outer(*refs):
            def step(*blocks):
                body(*blocks[:nr + nv], *[None] * nd, *blocks[nr + nv:])

            pltpu.emit_pipeline(step, grid=(R // tm,), in_specs=specs, out_specs=out_specs)(
                *refs[:nr + nv], *refs[nr + nv + nd:])

        return pl.pallas_call(
            outer, name=name, in_specs=[_ANY] * (nr + nv + nd), out_specs=[_ANY] * no, out_shape=out_shape,
            compiler_params=_cparams(None),
        )(*arrs, *deps)

    outs = pl.pallas_call(
        body, name=name, grid=(R // tm,), in_specs=specs + [_ANY] * nd, out_specs=out_specs, out_shape=out_shape,
        compiler_params=_cparams(("arbitrary",) if ns else ("parallel",)),
    )(*arrs, *deps)
    return outs


def _mm_rows(fn, mats, rows, vecs, out_rows, out_sums, *, tm, name, deps=()):
    R = mats[0][0].shape[0]
    assert R % tm == 0, name
    specs, arrs = [], []
    for a, b, tb in mats:
        specs += [pl.BlockSpec((tm, a.shape[1]), lambda i: (i, 0)), pl.BlockSpec(b.shape, lambda i: (0, 0))]
        arrs += [a, b]
    for r in rows:
        specs.append(pl.BlockSpec((tm, r.shape[1]), lambda i: (i, 0)))
        arrs.append(r)
    for v in vecs:
        specs.append(pl.BlockSpec(v.shape, lambda i: (0, 0)))
        arrs.append(v)
    nm, nr, nv, nd, no, ns = len(mats), len(rows), len(vecs), len(deps), len(out_rows), len(out_sums)
    out_specs = [pl.BlockSpec((tm, w), lambda i: (i, 0)) for w, _ in out_rows]
    out_specs += [pl.BlockSpec(s, lambda i: (0, 0)) for s in out_sums]
    out_shape = [jax.ShapeDtypeStruct((R, w), dt) for w, dt in out_rows] + [jax.ShapeDtypeStruct(s, F32) for s in out_sums]

    def body(*refs):
        prods = [_dot(refs[2 * p][...].astype(BF16), refs[2 * p + 1][...].astype(BF16), 1, 1 if mats[p][2] else 0)
                 for p in range(nm)]
        ins = [r[...] for r in refs[2 * nm:2 * nm + nr + nv]]
        first_out = 2 * nm + nr + nv + nd
        o_refs, s_refs = refs[first_out:first_out + no], refs[first_out + no:]
        o_vals, s_vals = fn(*prods, *ins)
        for ref, val in zip(o_refs, o_vals):
            ref[...] = val.astype(ref.dtype)
        if ns:
            @pl.when(pl.program_id(0) == 0)
            def _():
                for ref in s_refs:
                    ref[...] = jnp.zeros_like(ref)
            for ref, val in zip(s_refs, s_vals):
                ref[...] += val

    return pl.pallas_call(
        body, name=name, grid=(R // tm,), in_specs=specs + [_ANY] * nd, out_specs=out_specs, out_shape=out_shape,
        compiler_params=_cparams(("arbitrary",) if ns else ("parallel",)),
    )(*arrs, *deps)


def _colsum(x):
    return jnp.sum(x, axis=0, keepdims=True)


def _rstd(x):
    return lax.rsqrt(jnp.mean(x * x, axis=-1, keepdims=True) + EPS)


def _rms_bwd(xn, r, g, dy):
    dn = dy * g
    return r * (dn - xn * jnp.mean(dn * xn, axis=-1, keepdims=True))


def _partner(t):
    lane = lax.broadcasted_iota(jnp.int32, t.shape, 1)
    up = pltpu.roll(t, 96, 1)
    down = pltpu.roll(t, 32, 1)
    return jnp.where((lane % 64) < 32, up, down)


SLABS = AW // 128


def _rows(r, n, d):
    return pl.ds(r, n, stride=d) if d > 1 else pl.ds(0, n)


def _undilate(src_ref, dst, d, tm):
    for r in range(d):
        for j in range(SLABS):
            dst[j][_rows(r, tm // d, d), :] = src_ref[:, pl.ds(r * AW + j * 128, 128)].astype(dst[j].dtype)


def _dilate(dst_ref, src, d, tm):
    for r in range(d):
        for j in range(SLABS):
            dst_ref[:, pl.ds(r * AW + j * 128, 128)] = src[j][_rows(r, tm // d, d), :].astype(dst_ref.dtype)


def _slab_scratch(n, tm):
    return [pltpu.VMEM((tm, 128), F32)] * (SLABS * n)


def _slab_groups(flat):
    return [flat[SLABS * i:SLABS * (i + 1)] for i in range(len(flat) // SLABS)]


def _slab_specs(tm, first):
    return [pl.BlockSpec((tm, 128), lambda i, j=j: (i, first + j)) for j in range(SLABS)]


def _dil_spec(tm, d):
    return pl.BlockSpec((tm // d, d * AW), lambda i: (i, 0))


ROPE_TM = 512


def _rope_fwd(qkvz, cos128, sin128):
    tm = ROPE_TM

    def body(*refs):
        q_refs, k_refs, v_refs = refs[0:4], refs[4:8], refs[8:12]
        c_ref, s_ref = refs[12], refs[13]
        outs = refs[14:23]
        qs, ks = _slab_groups(refs[23:])
        c, s = c_ref[...], s_ref[...]
        for j in range(SLABS):
            q, k = q_refs[j][...], k_refs[j][...]
            qs[j][...] = (q * c + _partner(q) * s) * (HD ** -0.5)
            ks[j][...] = k * c + _partner(k) * s
        for di, d in enumerate(DILATIONS):
            oq, ok, ov = outs[3 * di:3 * di + 3]
            for r in range(d):
                rows = _rows(r, tm // d, d)
                for j in range(SLABS):
                    cols = pl.ds(r * AW + j * 128, 128)
                    oq[:, cols] = qs[j][rows, :].astype(BF16)
                    ok[:, cols] = ks[j][rows, :].astype(BF16)
                    ov[:, cols] = v_refs[j][rows, :].astype(BF16)

    tab = pl.BlockSpec((tm, 128), lambda i: (i, 0))
    out_specs, out_shape = [], []
    for d in DILATIONS:
        out_specs += [_dil_spec(tm, d)] * 3
        out_shape += [jax.ShapeDtypeStruct((T // d, d * AW), BF16)] * 3
    return pl.pallas_call(
        body, name="rope_fwd", grid=(T // tm,),
        in_specs=_slab_specs(tm, 0) + _slab_specs(tm, 4) + _slab_specs(tm, 8) + [tab, tab],
        out_specs=out_specs, out_shape=out_shape, scratch_shapes=_slab_scratch(2, tm),
        compiler_params=_cparams(("parallel",)),
    )(*([qkvz] * 12), cos128, sin128)


def _rope_bwd(grads, dz, cos128, sin128):
    tm = ROPE_TM

    def body(*refs):
        g_refs = refs[0:9]
        dz_ref, c_ref, s_ref, o_ref = refs[9], refs[10], refs[11], refs[12]
        scr = _slab_groups(refs[13:])
        for di, d in enumerate(DILATIONS[1:]):
            for t in range(3):
                _undilate(g_refs[3 * (di + 1) + t], scr[3 * di + t], d, tm)
        c, s = c_ref[...], s_ref[...]
        for j in range(SLABS):
            cols = pl.ds(j * 128, 128)
            tot = [g_refs[t][:, cols] + scr[t][j][...] + scr[3 + t][j][...] for t in range(3)]
            dqr = tot[0] * (HD ** -0.5)
            o_ref[:, pl.ds(j * 128, 128)] = (dqr * c + _partner(dqr * s)).astype(BF16)
            o_ref[:, pl.ds(AW + j * 128, 128)] = (tot[1] * c + _partner(tot[1] * s)).astype(BF16)
            o_ref[:, pl.ds(2 * AW + j * 128, 128)] = tot[2].astype(BF16)
        o_ref[:, pl.ds(3 * AW, AW)] = dz_ref[...].astype(BF16)

    tab = pl.BlockSpec((tm, 128), lambda i: (i, 0))
    in_specs, args = [], []
    for d, g in zip(DILATIONS, grads):
        in_specs += [_dil_spec(tm, d)] * 3
        args += list(g)
    return pl.pallas_call(
        body, name="rope_bwd", grid=(T // tm,),
        in_specs=in_specs + [pl.BlockSpec((tm, AW), lambda i: (i, 0)), tab, tab],
        out_specs=pl.BlockSpec((tm, 4 * AW), lambda i: (i, 0)),
        out_shape=jax.ShapeDtypeStruct((T, 4 * AW), BF16),
        scratch_shapes=_slab_scratch(6, tm),
        compiler_params=_cparams(("parallel",)),
    )(*args, dz, cos128, sin128)


def _dx_out(dmix, w_o, dep):
    tm = ROPE_TM

    def body(a_ref, w_ref, dep_ref, dcat_ref, o4, o16, *slabs):
        prod = _nt(a_ref[...].astype(BF16), w_ref[...].astype(BF16))
        dcat_ref[...] = prod
        for j in range(SLABS):
            slabs[j][...] = prod[:, 128 * j:128 * (j + 1)]
        _dilate(o4, slabs, 4, tm)
        _dilate(o16, slabs, 16, tm)

    return pl.pallas_call(
        body, name="dx_out", grid=(T // tm,),
        in_specs=[pl.BlockSpec((tm, D), lambda i: (i, 0)), pl.BlockSpec((D, D), lambda i: (0, 0)), _ANY],
        out_specs=[pl.BlockSpec((tm, D), lambda i: (i, 0)), _dil_spec(tm, 4), _dil_spec(tm, 16)],
        out_shape=[jax.ShapeDtypeStruct((T, D), F32), jax.ShapeDtypeStruct((T // 4, 4 * AW), F32),
                   jax.ShapeDtypeStruct((T // 16, 16 * AW), F32)],
        scratch_shapes=_slab_scratch(1, tm), compiler_params=_cparams(("parallel",)),
    )(dmix, w_o, dep)


def _band_masks():
    qi = lax.broadcasted_iota(jnp.int32, (QBLK, QBLK), 0)
    kj = lax.broadcasted_iota(jnp.int32, (QBLK, QBLK), 1)
    return kj >= qi, kj <= qi


def _attn_fwd(q, k, v, d):
    L = q.shape[0]
    npair = L // (2 * QBLK)

    def body(q_ref, kp_ref, kc_ref, vp_ref, vc_ref, o_ref, l_ref):
        pair = pl.program_id(1)
        mask_p, mask_c = _band_masks()
        for sub in range(2):
            rows = pl.ds(sub * QBLK, QBLK)
            first = jnp.where(pair > 0, 0.0, NEG) if sub == 0 else 0.0
            bias = jnp.concatenate([jnp.where(mask_p, 0.0, NEG) + first, jnp.where(mask_c, 0.0, NEG)], axis=1)
            k_prev = (lambda sl: kp_ref[:, sl]) if sub == 0 else (lambda sl: kc_ref[pl.ds(0, QBLK), sl])
            v_prev = (lambda sl: vp_ref[:, sl]) if sub == 0 else (lambda sl: vc_ref[pl.ds(0, QBLK), sl])
            s = []
            for h in range(HEADS):
                sl = pl.ds(HD * h, HD)
                qh = q_ref[rows, sl]
                s.append(jnp.concatenate([_nt(qh, k_prev(sl)), _nt(qh, kc_ref[rows, sl])], axis=1))
            s = jnp.stack(s) + bias
            m = jnp.max(s, axis=2, keepdims=True)
            e = jnp.exp(s - m)
            den = jnp.sum(e, axis=2, keepdims=True)
            p = e.astype(BF16)
            inv = 1.0 / den
            lse = m + jnp.log(den)
            for h in range(HEADS):
                sl = pl.ds(HD * h, HD)
                o_ref[rows, sl] = ((_nn(p[h, :, :QBLK], v_prev(sl)) + _nn(p[h, :, QBLK:], vc_ref[rows, sl])) * inv[h]
                                   ).astype(BF16)
                l_ref[rows, sl] = jnp.broadcast_to(lse[h], (QBLK, HD))

    cur = pl.BlockSpec((2 * QBLK, AW), lambda r, n: (n, r))
    prev = pl.BlockSpec((QBLK, AW), lambda r, n: (jnp.maximum(2 * n - 1, 0), r))
    return pl.pallas_call(
        body, name=f"attn_fwd_d{d}", grid=(d, npair),
        in_specs=[cur, prev, cur, prev, cur], out_specs=[cur, cur],
        out_shape=[jax.ShapeDtypeStruct((L, d * AW), BF16), jax.ShapeDtypeStruct((L, d * AW), F32)],
        compiler_params=_cparams(("parallel", "parallel")),
    )(q, k, k, v, v)


def _attn_bwd(q, k, v, do, at, lse, d):
    L = q.shape[0]
    nb = L // QBLK
    npair = nb // 2

    def body(qc_ref, qn_ref, kp_ref, kc_ref, vp_ref, vc_ref, doc_ref, don_ref, atc_ref, atn_ref,
             lc_ref, ln_ref, dq_ref, dk_ref, dv_ref):
        pair = pl.program_id(1)
        mask_p, mask_c = _band_masks()
        prev_bias = jnp.where(mask_p, 0.0, NEG)
        for sub in range(2):
            rows = pl.ds(sub * QBLK, QBLK)
            second = pl.ds(QBLK, QBLK)
            if sub == 0:
                take = lambda cur_ref, nxt_ref, cols, i: cur_ref[rows if i == 0 else second, cols]
                prev_of = lambda p_ref, c_ref, cols: p_ref[:, cols]
                first, last = jnp.where(pair > 0, 0.0, NEG), 0.0
            else:
                take = lambda cur_ref, nxt_ref, cols, i: cur_ref[rows, cols] if i == 0 else nxt_ref[:, cols]
                prev_of = lambda p_ref, c_ref, cols: c_ref[pl.ds(0, QBLK), cols]
                first, last = 0.0, jnp.where(pair < npair - 1, 0.0, NEG)
            bias = jnp.concatenate([prev_bias + first, jnp.where(mask_c, 0.0, NEG), prev_bias + last], axis=1)
            s, dp, ls, dl, ops = [], [], [], [], []
            for h in range(HEADS):
                sl = pl.ds(HD * h, HD)
                one = pl.ds(HD * h, 1)
                q0, q1 = take(qc_ref, qn_ref, sl, 0), take(qc_ref, qn_ref, sl, 1)
                kp, kc = prev_of(kp_ref, kc_ref, sl), kc_ref[rows, sl]
                vp, vc = prev_of(vp_ref, vc_ref, sl), vc_ref[rows, sl]
                do0, do1 = take(doc_ref, don_ref, sl, 0), take(doc_ref, don_ref, sl, 1)
                do0b, do1b = do0.astype(BF16), do1.astype(BF16)
                s.append(jnp.concatenate([_nt(q0, kp), _nt(q0, kc), _nt(q1, kc)], axis=1))
                dp.append(jnp.concatenate([_nt(do0b, vp), _nt(do0b, vc), _nt(do1b, vc)], axis=1))
                dl0 = jnp.sum(do0 * take(atc_ref, atn_ref, sl, 0), axis=1, keepdims=True)
                dl1 = jnp.sum(do1 * take(atc_ref, atn_ref, sl, 1), axis=1, keepdims=True)
                dl.append(jnp.concatenate([jnp.broadcast_to(dl0, (QBLK, 2 * QBLK)), jnp.broadcast_to(dl1, (QBLK, QBLK))], axis=1))
                ls.append(jnp.concatenate([jnp.broadcast_to(take(lc_ref, ln_ref, one, 0), (QBLK, 2 * QBLK)),
                                           jnp.broadcast_to(take(lc_ref, ln_ref, one, 1), (QBLK, QBLK))], axis=1))
                ops.append((q0, q1, kp, kc, do0b, do1b))
            p = jnp.exp(jnp.stack(s) + bias - jnp.stack(ls))
            ds = (p * (jnp.stack(dp) - jnp.stack(dl))).astype(BF16)
            p = p.astype(BF16)
            for h in range(HEADS):
                sl = pl.ds(HD * h, HD)
                q0, q1, kp, kc, do0b, do1b = ops[h]
                dq_ref[rows, sl] = (_nn(ds[h, :, :QBLK], kp) + _nn(ds[h, :, QBLK:2 * QBLK], kc)).astype(BF16)
                dv_ref[rows, sl] = (_tn(p[h, :, QBLK:2 * QBLK], do0b) + _tn(p[h, :, 2 * QBLK:], do1b)).astype(BF16)
                dk_ref[rows, sl] = (_tn(ds[h, :, QBLK:2 * QBLK], q0) + _tn(ds[h, :, 2 * QBLK:], q1)).astype(BF16)

    cur = pl.BlockSpec((2 * QBLK, AW), lambda r, n: (n, r))
    prev = pl.BlockSpec((QBLK, AW), lambda r, n: (jnp.maximum(2 * n - 1, 0), r))
    nxt = pl.BlockSpec((QBLK, AW), lambda r, n: (jnp.minimum(2 * n + 2, nb - 1), r))
    return pl.pallas_call(
        body, name=f"attn_bwd_d{d}", grid=(d, npair),
        in_specs=[cur, nxt, prev, cur, prev, cur, cur, nxt, cur, nxt, cur, nxt], out_specs=[cur, cur, cur],
        out_shape=[jax.ShapeDtypeStruct((L, d * AW), BF16)] * 3,
        compiler_params=_cparams(("parallel", "parallel")),
    )(q, q, k, k, v, v, do, do, at, at, lse, lse)


def _attn_merge(outs, lses):
    tm = ROPE_TM

    def body(o1, o4, o16, l1, l4, l16, at_ref, ls_ref, at4, ls4, at16, ls16, *flat):
        so4, so16, sl4, sl16, sa, sl = _slab_groups(flat)
        _undilate(o4, so4, 4, tm)
        _undilate(o16, so16, 16, tm)
        _undilate(l4, sl4, 4, tm)
        _undilate(l16, sl16, 16, tm)
        for j in range(SLABS):
            cols = pl.ds(j * 128, 128)
            a, b, c = l1[:, cols], sl4[j][...], sl16[j][...]
            m = jnp.maximum(jnp.maximum(a, b), c)
            e1, e2, e3 = jnp.exp(a - m), jnp.exp(b - m), jnp.exp(c - m)
            s = e1 + e2 + e3
            inv = 1.0 / s
            attn = (e1 * inv) * o1[:, cols] + (e2 * inv) * so4[j][...] + (e3 * inv) * so16[j][...]
            lse = m + jnp.log(s)
            at_ref[:, cols] = attn
            ls_ref[:, cols] = lse
            sa[j][...] = attn
            sl[j][...] = lse
        _dilate(at4, sa, 4, tm)
        _dilate(at16, sa, 16, tm)
        _dilate(ls4, sl, 4, tm)
        _dilate(ls16, sl, 16, tm)

    specs = [_dil_spec(tm, d) for d in DILATIONS]
    tok = specs[0]
    return pl.pallas_call(
        body, name="attn_merge", grid=(T // tm,),
        in_specs=specs + specs, out_specs=[tok, tok, specs[1], specs[1], specs[2], specs[2]],
        out_shape=[jax.ShapeDtypeStruct((T, AW), F32)] * 2 + [jax.ShapeDtypeStruct((T // 4, 4 * AW), F32)] * 2
        + [jax.ShapeDtypeStruct((T // 16, 16 * AW), F32)] * 2,
        scratch_shapes=_slab_scratch(6, tm),
        compiler_params=_cparams(("parallel",)),
    )(*outs, *lses)


CONV_TM = 512
HALO = 8


def _conv_pre(ext, w, b):
    y = b + w[3] * ext
    for kk in range(1, CONV_K):
        y = y + w[3 - kk] * pltpu.roll(ext, kk, 0)
    return y


def _rows_to_block(rows, n, width):
    ri = lax.broadcasted_iota(jnp.int32, (n, width), 0)
    out = jnp.zeros((n, width), F32)
    for j, r in enumerate(rows):
        out = out + jnp.where(ri == j, r, 0.0)
    return out


def _conv_bwd(xbc, dact, ddt, w, b):
    nblk = T // CONV_TM
    per = CONV_TM // HALO

    def body(x_ref, xb_ref, xa_ref, g_ref, ga_ref, ddt_ref, w_ref, b_ref, dx_ref, dw_ref):
        i = pl.program_id(0)
        wv = [w_ref[pl.ds(j, 1), :] for j in range(CONV_K)]
        before = jnp.where(i > 0, xb_ref[...], 0.0)
        last = i == nblk - 1
        after = jnp.where(last, 0.0, xa_ref[...])
        g_after = jnp.where(last, 0.0, ga_ref[...])
        ext = jnp.concatenate([before, x_ref[...], after], axis=0)
        y = _conv_pre(ext, wv, b_ref[...])[HALO:]
        sg = _sigmoid(y)
        dy = jnp.concatenate([g_ref[...], g_after], axis=0) * (sg * (1.0 + y * (1.0 - sg)))
        n = CONV_TM + HALO
        dx = wv[3] * dy
        for kk in range(1, CONV_K):
            dx = dx + wv[3 - kk] * pltpu.roll(dy, n - kk, 0)
        dx_ref[:, pl.ds(0, CONV_CH)] = dx[:CONV_TM].astype(BF16)
        dx_ref[:, pl.ds(CONV_CH, DT_PAD)] = ddt_ref[...].astype(BF16)
        dyc = dy[:CONV_TM]
        rows = [jnp.sum(dyc * (pltpu.roll(ext, 3 - j, 0) if j < 3 else ext)[HALO:HALO + CONV_TM], axis=0, keepdims=True)
                for j in range(CONV_K)]
        rows.append(jnp.sum(dyc, axis=0, keepdims=True))
        part = _rows_to_block(rows, 8, CONV_CH)

        @pl.when(i == 0)
        def _():
            dw_ref[...] = jnp.zeros_like(dw_ref)
        dw_ref[...] += part

    blk = pl.BlockSpec((CONV_TM, CONV_CH), lambda i: (i, 0))
    hb = pl.BlockSpec((HALO, CONV_CH), lambda i: (jnp.maximum(i * per - 1, 0), 0))
    ha = pl.BlockSpec((HALO, CONV_CH), lambda i: (jnp.minimum((i + 1) * per, T // HALO - 1), 0))
    return pl.pallas_call(
        body, name="conv_bwd", grid=(nblk,),
        in_specs=[blk, hb, ha, blk, ha, pl.BlockSpec((CONV_TM, DT_PAD), lambda i: (i, 0)),
                  pl.BlockSpec((CONV_K, CONV_CH), lambda i: (0, 0)), pl.BlockSpec((1, CONV_CH), lambda i: (0, 0))],
        out_specs=[pl.BlockSpec((CONV_TM, CONV_CH + DT_PAD), lambda i: (i, 0)), pl.BlockSpec((8, CONV_CH), lambda i: (0, 0))],
        out_shape=[jax.ShapeDtypeStruct((T, CONV_CH + DT_PAD), BF16), jax.ShapeDtypeStruct((8, CONV_CH), F32)],
        compiler_params=_cparams(("arbitrary",)),
    )(xbc, xbc, xbc, dact, dact, ddt, w, b)


def _pick(mat, h):
    lane = lax.broadcasted_iota(jnp.int32, mat.shape, 1)
    return jnp.sum(jnp.where(lane == h, mat, 0.0), axis=1, keepdims=True)


def _heads(fn):
    return jnp.stack([fn(h) for h in range(HEADS)])


def _ssd_prep(dt_ref, bias_ref, alog_ref, dsk_ref, b_ref, c_ref, xs_ref, state_ref, cst):
    li = lax.broadcasted_iota(jnp.int32, (CHUNK, CHUNK), 0)
    si = lax.broadcasted_iota(jnp.int32, (CHUNK, CHUNK), 1)
    tri = li >= si
    dtp = dt_ref[...] + bias_ref[...]
    dt = _softplus(dtp)
    A = -jnp.exp(alog_ref[...])
    a = dt * A
    cs = jnp.dot(tri.astype(F32), a, precision=HIGHEST, preferred_element_type=F32)
    cst[...] = cs.T
    Bm = b_ref[...].astype(BF16)
    Cm = c_ref[...].astype(BF16)
    cb = _nt(Cm, Bm)
    dskv = dsk_ref[...]
    cs_col = _heads(lambda h: _pick(cs, h))
    cs_row = _heads(lambda h: cst[pl.ds(h, 1), :])
    dt_col = _heads(lambda h: _pick(dt, h))
    dsk_col = _heads(lambda h: _pick(dskv, h))
    lam = jnp.exp(jnp.where(tri, cs_col - cs_row, NEG))
    x = _heads(lambda h: xs_ref[:, pl.ds(HD * h, HD)])
    xdt = x * dt_col
    prev = _heads(lambda h: state_ref[pl.ds(HD * h, HD), :])
    lane = lax.broadcasted_iota(jnp.int32, (1, 1, CHUNK), 2)
    cl = jnp.sum(jnp.where(lane == CHUNK - 1, cs_row, 0.0), axis=2, keepdims=True)
    f = jnp.exp(cl - cs_col)
    return dict(li=li, si=si, dtp=dtp, dt=dt, A=A, Bm=Bm, Cm=Cm, cb=cb, cs_col=cs_col, dt_col=dt_col, dsk_col=dsk_col,
                lam=lam, x=x, xdt=xdt, prev=prev, cl=cl, f=f)


def _ssd_fwd(xbcdt, conv_w, conv_b, bias, alog, dsk, qkvz, attn, gs):
    nc = T // CHUNK
    R = SSD_PER * CHUNK
    per = R // HALO

    def body(xbc_ref, halo_ref, cw_ref, cb_ref, dt_ref, bias_ref, alog_ref, dsk_ref, z_ref, at_ref, gs_ref,
             y_ref, st_ref, cat_ref, act_ref, state, cst):
        @pl.when(pl.program_id(0) == 0)
        def _():
            state[...] = jnp.zeros_like(state)
        for sub in range(SSD_PER):
            rows = pl.ds(sub * CHUNK, CHUNK)
            st_ref[sub] = state[...]
            halo = (jnp.where(pl.program_id(0) > 0, halo_ref[...], 0.0) if sub == 0
                    else xbc_ref[pl.ds(sub * CHUNK - HALO, HALO), :])
            one_chunk(halo, xbc_ref.at[rows, :], cw_ref, cb_ref, dt_ref.at[rows, :], bias_ref, alog_ref, dsk_ref,
                      z_ref.at[rows, :], at_ref.at[rows, :], gs_ref, y_ref.at[rows, :], cat_ref.at[rows, :],
                      act_ref.at[rows, :], state, cst)

    def one_chunk(halo, xbc_ref, cw_ref, cb_ref, dt_ref, bias_ref, alog_ref, dsk_ref, z_ref, at_ref, gs_ref,
                  y_ref, cat_ref, act_ref, state, cst):
        pre = _conv_pre(jnp.concatenate([halo, xbc_ref[...]], axis=0),
                        [cw_ref[pl.ds(j, 1), :] for j in range(CONV_K)], cb_ref[...])[HALO:]
        act_ref[...] = pre * _sigmoid(pre)
        xs_ref, b_ref, c_ref = (act_ref.at[:, pl.ds(0, AW)], act_ref.at[:, pl.ds(AW, NS)],
                                act_ref.at[:, pl.ds(AW + NS, NS)])
        s = _ssd_prep(dt_ref, bias_ref, alog_ref, dsk_ref, b_ref, c_ref, xs_ref, state, cst)
        Bm, Cm, prev = s["Bm"], s["Cm"], s["prev"]
        g = (s["cb"] * s["lam"]).astype(BF16)
        xdtb = s["xdt"].astype(BF16)
        prevb = prev.astype(BF16)
        y = _heads(lambda h: _nn(g[h], xdtb[h])) + _heads(lambda h: _nt(Cm, prevb[h])) * jnp.exp(s["cs_col"])
        y = y + s["dsk_col"] * s["x"]
        xf = (s["xdt"] * s["f"]).astype(BF16)
        new = prev * jnp.exp(s["cl"]) + _heads(lambda h: _tn(xf[h], Bm))
        for h in range(HEADS):
            y_ref[:, pl.ds(HD * h, HD)] = y[h]
            state[pl.ds(HD * h, HD), :] = new[h]
        z = z_ref[...]
        gi = y_ref[...] * (z * _sigmoid(z))
        cat_ref[:, pl.ds(0, AW)] = at_ref[...].astype(BF16)
        cat_ref[:, pl.ds(AW, AW)] = (gi * _rstd(gi) * gs_ref[...]).astype(BF16)

    vec = pl.BlockSpec((1, DT_PAD), lambda c: (0, 0))
    blk = pl.BlockSpec((R, AW), lambda c: (c, 0))
    return pl.pallas_call(
        body, name="ssd_fwd", grid=(nc // SSD_PER,),
        in_specs=[pl.BlockSpec((R, CONV_CH), lambda c: (c, 0)),
                  pl.BlockSpec((HALO, CONV_CH), lambda c: (jnp.maximum(c * per - 1, 0), 0)),
                  pl.BlockSpec((CONV_K, CONV_CH), lambda c: (0, 0)), pl.BlockSpec((1, CONV_CH), lambda c: (0, 0)),
                  pl.BlockSpec((R, DT_PAD), lambda c: (c, 6)),
                  vec, vec, vec, pl.BlockSpec((R, AW), lambda c: (c, 3)), blk, pl.BlockSpec((1, AW), lambda c: (0, 0))],
        out_specs=[blk, pl.BlockSpec((SSD_PER, AW, NS), lambda c: (c, 0, 0)), pl.BlockSpec((R, D), lambda c: (c, 0)),
                   pl.BlockSpec((R, CONV_CH), lambda c: (c, 0))],
        out_shape=[jax.ShapeDtypeStruct((T, AW), F32), jax.ShapeDtypeStruct((nc, AW, NS), F32),
                   jax.ShapeDtypeStruct((T, D), BF16), jax.ShapeDtypeStruct((T, CONV_CH), F32)],
        scratch_shapes=[pltpu.VMEM((AW, NS), F32), pltpu.VMEM((CHUNK, CHUNK), F32)],
        compiler_params=_cparams(("arbitrary",)),
    )(xbcdt, xbcdt, conv_w, conv_b, xbcdt, bias, alog, dsk, qkvz, attn, gs)


def _ssd_bwd(act, xbcdt, bias, alog, dsk, states, y_ssd, qkvz, dcat, gs):
    nc = T // CHUNK

    def body(xs_ref, b_ref, c_ref, dt_ref, bias_ref, alog_ref, dsk_ref, st_ref, y_ref, z_ref, dyn_ref, gs_ref,
             dact_ref, ddt_ref, par_ref, dz_ref, dgs_ref, dstate, cst, dy_ref):
        @pl.when(pl.program_id(0) == 0)
        def _():
            dstate[...] = jnp.zeros_like(dstate)
            par_ref[...] = jnp.zeros_like(par_ref)
            dgs_ref[...] = jnp.zeros_like(dgs_ref)
        for sub in reversed(range(SSD_PER)):
            rows = pl.ds(sub * CHUNK, CHUNK)
            one_chunk(xs_ref.at[rows, :], b_ref.at[rows, :], c_ref.at[rows, :], dt_ref.at[rows, :], bias_ref, alog_ref,
                      dsk_ref, st_ref.at[sub], y_ref.at[rows, :], z_ref.at[rows, :], dyn_ref.at[rows, :], gs_ref,
                      dact_ref.at[rows, :], ddt_ref.at[rows, :], par_ref, dz_ref.at[rows, :], dgs_ref, dstate, cst, dy_ref)

    def one_chunk(xs_ref, b_ref, c_ref, dt_ref, bias_ref, alog_ref, dsk_ref, st_ref, y_ref, z_ref, dyn_ref, gs_ref,
                  dact_ref, ddt_ref, par_ref, dz_ref, dgs_ref, dstate, cst, dy_ref):
        z, yv, dyn = z_ref[...], y_ref[...], dyn_ref[...]
        sg = _sigmoid(z)
        sz = z * sg
        gi = yv * sz
        rg = _rstd(gi)
        ng = gi * rg
        dgi = _rms_bwd(ng, rg, gs_ref[...], dyn)
        dy_ref[...] = dgi * sz
        dz_ref[...] = dgi * yv * (sg * (1.0 + z * (1.0 - sg)))
        dgs_ref[...] += _colsum(dyn * ng)
        s = _ssd_prep(dt_ref, bias_ref, alog_ref, dsk_ref, b_ref, c_ref, xs_ref, st_ref, cst)
        Bm, Cm, prev, lam, x, xdt, f, cl = s["Bm"], s["Cm"], s["prev"], s["lam"], s["x"], s["xdt"], s["f"], s["cl"]
        lane = lax.broadcasted_iota(jnp.int32, (1, DT_PAD), 1)
        row = lax.broadcasted_iota(jnp.int32, (1, CHUNK, 1), 1)
        g = s["cb"] * lam
        gb, xdtb, prevb = g.astype(BF16), xdt.astype(BF16), prev.astype(BF16)
        dy = _heads(lambda h: dy_ref[:, pl.ds(HD * h, HD)])
        dyb = dy.astype(BF16)
        dnew = _heads(lambda h: dstate[pl.ds(HD * h, HD), :])
        dnewb = dnew.astype(BF16)
        E = jnp.exp(s["cs_col"])
        ecl = jnp.exp(cl)
        dG = _heads(lambda h: _nt(dyb[h], xdtb[h]))
        dxdt = _heads(lambda h: _tn(gb[h], dyb[h]))
        Yo = _heads(lambda h: _nt(Cm, prevb[h]))
        W = _heads(lambda h: _nt(Bm, dnewb[h]))
        dcb = jnp.sum(dG * lam, axis=0)
        Mm = dG * g
        col_sums = jnp.sum(Mm, axis=1, keepdims=True)
        dYo = (dy * E).astype(BF16)
        dxdt = dxdt + W * f
        dF = jnp.sum(W * xdt, axis=2, keepdims=True) * f
        dcl = jnp.sum(dnew * prev, axis=(1, 2), keepdims=True) * ecl + jnp.sum(dF, axis=1, keepdims=True)
        dcs = (jnp.sum(Mm, axis=2, keepdims=True) + jnp.sum(dy * Yo, axis=2, keepdims=True) * E - dF
               + jnp.where(row == CHUNK - 1, dcl, 0.0))
        ddt_x = jnp.sum(dxdt * x, axis=2, keepdims=True)
        dD = jnp.sum(dy * x, axis=(1, 2), keepdims=True)
        dx = s["dsk_col"] * dy + dxdt * s["dt_col"]
        xfb = (xdt * f).astype(BF16)
        dprev = _heads(lambda h: _tn(dYo[h], Cm)) + dnew * ecl
        dcbb = dcb.astype(BF16)
        dC = _nn(dcbb, Bm)
        dB = _tn(dcbb, Cm)
        dcs_mat = -_rows_to_block([col_sums[h] for h in range(HEADS)], CHUNK, CHUNK).T
        ddt_mat = jnp.zeros((CHUNK, DT_PAD), F32)
        dD_row = jnp.zeros((1, DT_PAD), F32)
        for h in range(HEADS):
            sl = pl.ds(HD * h, HD)
            dC = dC + _nn(dYo[h], prevb[h])
            dB = dB + _nn(xfb[h], dnewb[h])
            dcs_mat = dcs_mat + jnp.where(lane == h, dcs[h], 0.0)
            ddt_mat = ddt_mat + jnp.where(lane == h, ddt_x[h], 0.0)
            dD_row = dD_row + jnp.where(lane == h, dD[h], 0.0)
            dact_ref[:, sl] = dx[h]
            dstate[sl, :] = dprev[h]
        dact_ref[:, pl.ds(AW, NS)] = dB
        dact_ref[:, pl.ds(AW + NS, NS)] = dC
        da = jnp.dot((s["li"] <= s["si"]).astype(F32), dcs_mat, precision=HIGHEST, preferred_element_type=F32)
        ddtp = jnp.where(lane < HEADS, (ddt_mat + da * s["A"]) * _sigmoid(s["dtp"]), 0.0)
        ddt_ref[...] = ddtp
        dalog = jnp.where(lane < HEADS, jnp.sum(da * s["dt"], axis=0, keepdims=True) * s["A"], 0.0)
        par_ref[...] += _rows_to_block([jnp.sum(ddtp, axis=0, keepdims=True), dalog, dD_row], 8, DT_PAD)

    vec = pl.BlockSpec((1, DT_PAD), lambda c: (0, 0))
    nstep = nc // SSD_PER
    rev = lambda c: nstep - 1 - c
    R = SSD_PER * CHUNK
    return pl.pallas_call(
        body, name="ssd_bwd", grid=(nstep,),
        in_specs=[pl.BlockSpec((R, AW), lambda c: (rev(c), 0)), pl.BlockSpec((R, NS), lambda c: (rev(c), 4)),
                  pl.BlockSpec((R, NS), lambda c: (rev(c), 5)), pl.BlockSpec((R, DT_PAD), lambda c: (rev(c), 6)),
                  vec, vec, vec,
                  pl.BlockSpec((SSD_PER, AW, NS), lambda c: (rev(c), 0, 0)), pl.BlockSpec((R, AW), lambda c: (rev(c), 0)),
                  pl.BlockSpec((R, AW), lambda c: (rev(c), 3)), pl.BlockSpec((R, AW), lambda c: (rev(c), 1)),
                  pl.BlockSpec((1, AW), lambda c: (0, 0))],
        out_specs=[pl.BlockSpec((R, CONV_CH), lambda c: (rev(c), 0)), pl.BlockSpec((R, DT_PAD), lambda c: (rev(c), 0)),
                   pl.BlockSpec((8, DT_PAD), lambda c: (0, 0)), pl.BlockSpec((R, AW), lambda c: (rev(c), 0)),
                   pl.BlockSpec((1, AW), lambda c: (0, 0))],
        out_shape=[jax.ShapeDtypeStruct((T, CONV_CH), F32), jax.ShapeDtypeStruct((T, DT_PAD), F32),
                   jax.ShapeDtypeStruct((8, DT_PAD), F32), jax.ShapeDtypeStruct((T, AW), F32),
                   jax.ShapeDtypeStruct((1, AW), F32)],
        scratch_shapes=[pltpu.VMEM((AW, NS), F32), pltpu.VMEM((CHUNK, CHUNK), F32), pltpu.VMEM((CHUNK, AW), F32)],
        compiler_params=_cparams(("arbitrary",)),
    )(act, act, act, xbcdt, bias, alog, dsk, states, y_ssd, qkvz, dcat, gs)


def _place():
    return lax.axis_index("x"), lax.axis_index("y"), lax.axis_index("c")


def _slot(px, py, pc):
    return 4 * px + 2 * py + pc


SLAB_ROWS = 24


def _slab_pack(parts, name):
    n = len(parts)

    def body(*refs):
        slab = refs[n]
        slab[...] = jnp.zeros_like(slab)
        for ref, (arr, row) in zip(refs[:n], parts):
            slab[pl.ds(row, arr.shape[0]), pl.ds(0, arr.shape[1])] = ref[...]

    vm = pl.BlockSpec(memory_space=pltpu.VMEM)
    return pl.pallas_call(
        body, name=name, in_specs=[vm] * n, out_specs=vm, out_shape=jax.ShapeDtypeStruct((SLAB_ROWS, D), F32),
    )(*[a for a, _ in parts])


_HBM = pl.BlockSpec(memory_space=pltpu.HBM)
_SEM = pl.BlockSpec(memory_space=pltpu.SEMAPHORE)
_EFFECT = pltpu.SideEffectType.DATAFLOW_SIDE_EFFECTING


def _peers(x, y, c):
    out = []
    for kk in range(1, N_DEV):
        fx, fy, fc = kk >> 2 & 1, kk >> 1 & 1, kk & 1
        out.append((1 - x if fx else x, 1 - y if fy else y, 1 - c if fc else c))
    return out


def _send_start(src, per_peer, name, dep):
    (handles, token) = _send_start_many([src], per_peer, name, dep)
    return handles, token


def _near_peers(x, y, c):
    return [(x, y, 1 - c), (1 - x, y, c), (x, 1 - y, c), (1 - x, 1 - y, c)]


def _send_start_many(srcs, per_peer, name, dep, peers=_peers, npeers=N_DEV - 1):
    n = len(srcs)

    def body(*refs):
        src_refs, land_refs = refs[:n], refs[n:2 * n]
        send_sems, recv_sems = refs[2 * n + 1], refs[2 * n + 2]
        token = refs[-1]
        x, y, c = _place()
        mine = _slot(x, y, c)
        for a in range(n):
            for kk, peer in enumerate(peers(x, y, c)):
                pltpu.make_async_remote_copy(
                    src_ref=src_refs[a].at[_slot(*peer)] if per_peer else src_refs[a], dst_ref=land_refs[a].at[mine],
                    send_sem=send_sems.at[a * npeers + kk], recv_sem=recv_sems.at[a * npeers + kk],
                    device_id=peer, device_id_type=MESH).start()
        token[...] = jnp.zeros_like(token)

    lands = [lax.empty((N_DEV,) + tuple(s.shape[1:] if per_peer else s.shape), s.dtype) for s in srcs]
    hbm = lambda t: pltpu.with_memory_space_constraint(t, pltpu.HBM)
    outs = pl.pallas_call(
        body, name=name,
        out_shape=(pltpu.SemaphoreType.DMA((n * npeers,)), pltpu.SemaphoreType.DMA((n * npeers,)),
                   *[pltpu.HBM(s.shape, s.dtype) for s in srcs], *[pltpu.HBM(l.shape, l.dtype) for l in lands],
                   jax.ShapeDtypeStruct((8, 128), F32)),
        in_specs=(*[_HBM] * (2 * n), _ANY),
        out_specs=(_SEM, _SEM, *[_HBM] * (2 * n), pl.BlockSpec(memory_space=pltpu.VMEM)),
        input_output_aliases={i: 2 + i for i in range(2 * n)},
        compiler_params=pltpu.CompilerParams(has_side_effects=_EFFECT),
    )(*[hbm(s) for s in srcs], *[hbm(l) for l in lands], dep)
    return (outs[0], outs[1], list(outs[2:2 + n]), list(outs[2 + n:2 + 2 * n])), outs[-1]


def _send_wait(handles, after, name):
    srcs, lands = _send_wait_many(handles, after, name)
    return srcs[0], lands[0]


def _send_wait_many(handles, after, name, npeers=N_DEV - 1):
    send_sems, recv_sems, src_thrus, land_thrus = handles
    n = len(src_thrus)

    def body(*refs):
        land_refs = refs[n:2 * n]
        send_sems, recv_sems = refs[2 * n], refs[2 * n + 1]
        me = _place()
        for a in range(n):
            for kk in range(npeers):
                cp = pltpu.make_async_remote_copy(
                    src_ref=land_refs[a].at[0], dst_ref=land_refs[a].at[0],
                    send_sem=send_sems.at[a * npeers + kk], recv_sem=recv_sems.at[a * npeers + kk],
                    device_id=me, device_id_type=MESH)
                cp.wait_send()
                cp.wait_recv()

    both = list(src_thrus) + list(land_thrus)
    outs = pl.pallas_call(
        body, name=name,
        out_shape=tuple(pltpu.HBM(t.shape, t.dtype) for t in both),
        in_specs=(*[_HBM] * (2 * n), _SEM, _SEM, _ANY), out_specs=tuple([_HBM] * (2 * n)),
        input_output_aliases={i: i for i in range(2 * n)},
        compiler_params=pltpu.CompilerParams(has_side_effects=_EFFECT),
    )(*both, send_sems, recv_sems, after)
    return list(outs[:n]), list(outs[n:])


def _forward_start(lands, name, dep):
    n = len(lands)

    def body(*refs):
        land_refs = refs[:n]
        send_sems, recv_sems = refs[n + 1], refs[n + 2]
        token = refs[-1]
        x, y, c = _place()
        for a in range(n):
            for j, chip in enumerate([(1 - x, y), (x, 1 - y), (1 - x, 1 - y)]):
                blk = land_refs[a].at[_slot(*chip, c)]
                pltpu.make_async_remote_copy(
                    src_ref=blk, dst_ref=blk, send_sem=send_sems.at[a * 3 + j], recv_sem=recv_sems.at[a * 3 + j],
                    device_id=(x, y, 1 - c), device_id_type=MESH).start()
        token[...] = jnp.zeros_like(token)

    outs = pl.pallas_call(
        body, name=name,
        out_shape=(pltpu.SemaphoreType.DMA((n * 3,)), pltpu.SemaphoreType.DMA((n * 3,)),
                   *[pltpu.HBM(l.shape, l.dtype) for l in lands], jax.ShapeDtypeStruct((8, 128), F32)),
        in_specs=(*[_HBM] * n, _ANY), out_specs=(_SEM, _SEM, *[_HBM] * n, pl.BlockSpec(memory_space=pltpu.VMEM)),
        input_output_aliases={i: 2 + i for i in range(n)},
        compiler_params=pltpu.CompilerParams(has_side_effects=_EFFECT),
    )(*lands, dep)
    return (outs[0], outs[1], list(outs[2:2 + n])), outs[-1]


def _forward_wait(handles, after, name):
    send_sems, recv_sems, land_thrus = handles
    n = len(land_thrus)

    def body(*refs):
        land_refs = refs[:n]
        send_sems, recv_sems = refs[n], refs[n + 1]
        me = _place()
        for a in range(n):
            for j in range(3):
                cp = pltpu.make_async_remote_copy(
                    src_ref=land_refs[a].at[0], dst_ref=land_refs[a].at[0],
                    send_sem=send_sems.at[a * 3 + j], recv_sem=recv_sems.at[a * 3 + j], device_id=me, device_id_type=MESH)
                cp.wait_send()
                cp.wait_recv()

    outs = pl.pallas_call(
        body, name=name,
        out_shape=tuple(pltpu.HBM(t.shape, t.dtype) for t in land_thrus),
        in_specs=(*[_HBM] * n, _SEM, _SEM, _ANY), out_specs=tuple([_HBM] * n),
        input_output_aliases={i: i for i in range(n)},
        compiler_params=pltpu.CompilerParams(has_side_effects=_EFFECT),
    )(*land_thrus, send_sems, recv_sems, after)
    return list(outs)


def _sum_slots(land, name):
    _, R, C = land.shape
    tm = R if R <= 512 else 512

    def body(x_ref, o_ref):
        acc = x_ref[0].astype(F32)
        for j in range(1, N_DEV):
            acc = acc + x_ref[j].astype(F32)
        o_ref[...] = acc

    if R % STREAM_TM == 0 and R // STREAM_TM >= 3:
        return _stream_slots(body, land, [], 1, name)
    return pl.pallas_call(
        body, name=name, grid=(R // tm,),
        in_specs=[pl.BlockSpec((N_DEV, tm, C), lambda i: (0, i, 0))], out_specs=pl.BlockSpec((tm, C), lambda i: (i, 0)),
        out_shape=jax.ShapeDtypeStruct((R, C), F32), compiler_params=_cparams(("parallel",)),
    )(land)


STREAM_TM = 128


def _stream_slots(body, land, rows, n_out, name):
    _, R, C = land.shape
    row = pl.BlockSpec((STREAM_TM, C), lambda i: (i, 0))
    row_in = pl.BlockSpec((STREAM_TM, C), lambda i: (i, 0), pipeline_mode=pl.Buffered(3))
    land_in = pl.BlockSpec((N_DEV, STREAM_TM, C), lambda i: (0, i, 0), pipeline_mode=pl.Buffered(3))

    def outer(*refs):
        pltpu.emit_pipeline(body, grid=(R // STREAM_TM,), in_specs=[land_in] + [row_in] * len(rows),
                            out_specs=[row] * n_out)(*refs)

    outs = pl.pallas_call(
        outer, name=name, in_specs=[_ANY] * (1 + len(rows)), out_specs=[_ANY] * n_out,
        out_shape=[jax.ShapeDtypeStruct((R, C), F32)] * n_out, compiler_params=_cparams(None),
    )(land, *rows)
    return outs[0] if n_out == 1 else outs


def _adam_math(w, g, m, v):
    m2 = ADAM_B1 * m + (1.0 - ADAM_B1) * g
    v2 = ADAM_B2 * v + (1.0 - ADAM_B2) * (g * g)
    m_hat = m2 / (1.0 - ADAM_B1 ** ADAM_STEP)
    v_hat = v2 / (1.0 - ADAM_B2 ** ADAM_STEP)
    delta = -ADAM_LR * (m_hat / (jnp.sqrt(v_hat) + ADAM_EPS) + ADAM_WD * w)
    return delta, m2, v2


def _adamw(w, g, m, v, name):
    R, C = w.shape
    tm = R if R <= 512 else 256
    return _rowwise(lambda w, g, m, v: (_adam_math(w, g, m, v), ()), [w, g, m, v], [], [(C, F32)] * 3, [], tm=tm, name=name,
                    in_buffers=3 if R // tm >= 3 else 2)


def _adamw_slots(land, w, m, v, name):
    _, R, C = land.shape
    tm = R if R <= 256 else 256

    def body(x_ref, w_ref, m_ref, v_ref, g_ref, d_ref, mo_ref, vo_ref):
        g = x_ref[0].astype(F32)
        for j in range(1, N_DEV):
            g = g + x_ref[j].astype(F32)
        d, m2, v2 = _adam_math(w_ref[...], g, m_ref[...], v_ref[...])
        g_ref[...] = g
        d_ref[...] = d
        mo_ref[...] = m2
        vo_ref[...] = v2

    if R % STREAM_TM == 0 and R // STREAM_TM >= 3:
        return _stream_slots(body, land, [w, m, v], 4, name)
    row = pl.BlockSpec((tm, C), lambda i: (i, 0))
    return pl.pallas_call(
        body, name=name, grid=(R // tm,),
        in_specs=[pl.BlockSpec((N_DEV, tm, C), lambda i: (0, i, 0)), row, row, row], out_specs=[row] * 4,
        out_shape=[jax.ShapeDtypeStruct((R, C), F32)] * 4, compiler_params=_cparams(("parallel",)),
    )(land, w, m, v)


def _adamw_small(slab, slab_rows, g_conv_w, ws, ms, vs):
    n = len(ws)

    def body(*refs):
        slab_ref, gc_ref = refs[0], refs[1]
        w_refs, m_refs, v_refs = refs[2:2 + n], refs[2 + n:2 + 2 * n], refs[2 + 2 * n:2 + 3 * n]
        outs = refs[2 + 3 * n:]
        loss_ref = outs[0]
        g_out, d_out, m_out, v_out = (outs[1 + i * n:1 + (i + 1) * n] for i in range(4))
        loss_ref[...] = jnp.sum(slab_ref[pl.ds(6, 1), :], axis=1, keepdims=True)
        for i in range(n):
            g = gc_ref[...] if i == n - 1 else slab_ref[pl.ds(slab_rows[i], 1), pl.ds(0, ws[i].shape[1])]
            d, m2, v2 = _adam_math(w_refs[i][...], g, m_refs[i][...], v_refs[i][...])
            g_out[i][...] = g
            d_out[i][...] = d
            m_out[i][...] = m2
            v_out[i][...] = v2

    vm = pl.BlockSpec(memory_space=pltpu.VMEM)
    shapes = [jax.ShapeDtypeStruct(w.shape, F32) for w in ws]
    outs = pl.pallas_call(
        body, name="adamw_small", in_specs=[vm] * (2 + 3 * n), out_specs=[vm] * (1 + 4 * n),
        out_shape=[jax.ShapeDtypeStruct((1, 1), F32)] + shapes * 4,
    )(slab, g_conv_w, *ws, *ms, *vs)
    return outs[0], outs[1:1 + n], outs[1 + n:1 + 2 * n], outs[1 + 2 * n:1 + 3 * n], outs[1 + 3 * n:]


SMALL = ["norm_mix_pre", "norm_mix_post", "norm_mlp_pre", "norm_mlp_post", "norm_ple_post",
         "conv_b", "ssd_norm_g", "dt_bias", "a_log", "d_skip"]


def _pad_row(v, width=D):
    return jnp.pad(v, ((0, 0), (0, width - v.shape[1])))


def kernel(x, p, positions, norm_mix_pre, norm_mix_post, w_in, conv_w, conv_b, dt_bias, a_log, d_skip, ssd_norm_g, w_out, norm_mlp_pre, norm_mlp_post, w_up, w_down, w_ple_gate, w_ple_proj, norm_ple_post, loss_target, m_norm_mix_pre, m_norm_mix_post, m_w_in, m_conv_w, m_conv_b, m_dt_bias, m_a_log, m_d_skip, m_ssd_norm_g, m_w_out, m_norm_mlp_pre, m_norm_mlp_post, m_w_up, m_w_down, m_w_ple_gate, m_w_ple_proj, m_norm_ple_post, v_norm_mix_pre, v_norm_mix_post, v_w_in, v_conv_w, v_conv_b, v_dt_bias, v_a_log, v_d_skip, v_ssd_norm_g, v_w_out, v_norm_mlp_pre, v_norm_mlp_post, v_w_up, v_w_down, v_w_ple_gate, v_w_ple_proj, v_norm_ple_post):
    args = dict(locals())
    x2, p2, tgt = x[0], p[0, 0], loss_target[0]
    g1, g2, g3, g4, g5 = norm_mix_pre, norm_mix_post, norm_mlp_pre, norm_mlp_post, norm_ple_post

    me = _slot(*_place())
    pack_in = jnp.pad(w_in[0].T, ((0, W_IN_SHARD_PAD - W_IN_SHARD), (0, 0))).astype(BF16)
    rest = [w_out[0].astype(BF16), w_up[0].T.astype(BF16), w_down[0].astype(BF16), w_ple_gate[0].astype(BF16),
            w_ple_proj[0].T.reshape(32, D).astype(BF16)]
    conv_pack = jnp.pad(conv_w[0], ((0, 4), (0, 32)))
    in_handles, tok_in0 = _send_start_many([pack_in, conv_pack], False, "gather_in_start", g1, peers=_near_peers, npeers=4)

    inv_freq = ROPE_THETA ** (-jnp.arange(HD // 2, dtype=F32) * 2.0 / HD)
    pos = positions[0] + tok_in0[0, 0].astype(jnp.int32)
    ang = pos.astype(F32)[:, None] * jnp.tile(inv_freq, 4)
    cos128 = jnp.cos(ang)
    sin128 = jnp.sin(ang) * jnp.tile(jnp.concatenate([-jnp.ones(HD // 2, F32), jnp.ones(HD // 2, F32)]), 2)

    bias_w, alog_w, dsk_w = _pad_row(dt_bias, DT_PAD), _pad_row(a_log, DT_PAD), _pad_row(d_skip, DT_PAD)

    (u1,) = _rowwise(lambda a, g: ((a * _rstd(a) * g,), ()), [x2], [g1], [(D, BF16)], [], tm=512, name="norm_x",
                     deps=[cos128, sin128], in_buffers=3)
    p2b = p2.astype(BF16)

    in_back, in_land = _send_wait_many(in_handles, u1, "gather_in_wait", npeers=4)
    fw_handles, tok_fw = _forward_start(in_land, "gather_in_forward", u1)
    in_land = _forward_wait(fw_handles, tok_fw, "gather_in_forward_wait")
    gin = lax.dynamic_update_slice(in_land[0], in_back[0][None], (me, 0, 0))
    gconv = lax.dynamic_update_slice(in_land[1], in_back[1][None], (me, 0, 0))
    rest_handles, tok_rest = _send_start_many(rest, False, "gather_rest_start", gconv)
    w_inT = gin[:, :W_IN_SHARD].reshape(IN_W, D)
    w_qkvzT = w_inT[:4 * AW]
    w_xbcdtT = jnp.pad(w_inT[4 * AW:], ((0, DT_PAD - HEADS), (0, 0)))
    conv_full = gconv[:, :CONV_K, :96].transpose(1, 0, 2).reshape(CONV_K, CONV_CH)
    qkvz, xbcdt = _mm_rows(lambda a, b: ((a, b), ()), [(u1, w_qkvzT, True), (u1, w_xbcdtT, True)], [], [],
                           [(4 * AW, F32), (CONV_CH + DT_PAD, F32)], [], tm=512, name="proj_in", deps=[tok_rest])

    qkv = _rope_fwd(qkvz, cos128, sin128)
    qkv = [qkv[3 * i:3 * i + 3] for i in range(len(DILATIONS))]
    outs, lses = [], []
    for d, (qd, kd, vd) in zip(DILATIONS, qkv):
        o, l = _attn_fwd(qd, kd, vd, d)
        outs.append(o)
        lses.append(l)
    attn, lse, attn4, lse4, attn16, lse16 = _attn_merge(outs, lses)

    y_ssd, states, cat, act = _ssd_fwd(xbcdt, conv_full, conv_b, bias_w, alog_w, dsk_w, qkvz, attn, ssd_norm_g)


    rest_back, landed = _send_wait_many(rest_handles, cat, "gather_rest_wait")
    landed = [lax.dynamic_update_slice(l, b[None], (me, 0, 0)) for l, b in zip(landed, rest_back)]
    w_o, w_upT, w_dn, w_gate = landed[0].reshape(D, D), landed[1].reshape(DFF, D), landed[2].reshape(DFF, D), landed[3].reshape(D, D)
    w_projT = landed[4].reshape(D, PLE)

    def post1(mm, xx, ga):
        h = xx + mm * _rstd(mm) * ga
        return (mm, h, _rstd(h)), ()
    mix, h1, r3 = _mm_rows(post1, [(cat, w_o, False)], [x2], [g2], [(D, F32), (D, F32), (1, F32)], [], tm=512,
                           name="mix_out")

    a_up, ff, u2, h2, h2b = _mlp_fwd(h1, r3, g3, w_upT, w_dn, g4)
    relu2 = lambda a: jnp.square(jnp.maximum(a.astype(F32), 0.0))

    def final(gpre, ppv, hh, tg, g):
        sg = _sigmoid(gpre)
        ple = ppv * sg
        r = _rstd(ple)
        n = ple * r
        h3 = hh + n * g
        e = h3 - tg
        dh3 = e * (1.0 / D)
        dple = _rms_bwd(n, r, g, dh3)
        return (dh3, dple * sg, dple * ppv * sg * (1.0 - sg)), (_colsum(dh3 * n), _colsum(0.5 * e * e * (1.0 / D)))
    dh3, dpp, dgp, dg5, loss_vec = _mm_rows(final, [(h2b, w_gate, False), (p2b, w_projT, True)], [h2, tgt], [g5],
                                            [(D, F32), (D, BF16), (D, BF16)], [(1, D), (1, D)], tm=512, name="ple_loss")

    gw_projT = _mm(dpp, p2b, ta=True, tm=512, tn=256, tk=T, out_dtypes=(BF16,), name="gw_ple_proj")
    gw_gate = _mm(h2b, dgp, ta=True, tm=512, tn=1024, tk=T, out_dtypes=(BF16,), name="gw_ple_gate")
    def bwd_mlp_post(dg_, d3, f, g):
        dh2 = d3 + dg_
        r = _rstd(f)
        n = f * r
        return (dh2, _rms_bwd(n, r, g, dh2)), (_colsum(dh2 * n),)
    dh2, dff, dg4 = _mm_rows(bwd_mlp_post, [(dgp, w_gate, True)], [dh3, ff], [g4], [(D, F32), (D, BF16)], [(1, D)],
                             tm=512, name="bwd_ple_gate")

    gw_dn = _mm(a_up, dff, ta=True, tm=512, tn=1024, tk=T, a_pre=relu2, out_dtypes=(BF16,), name="gw_mlp_down")
    rs_a, tok_a = _send_start_many([gw_projT.reshape(N_DEV, 32, D), gw_gate.reshape(N_DEV, 128, D),
                                    gw_dn.reshape(N_DEV, 512, D)], True, "rs_start_a", g1)
    da_up, du2 = _mlp_dx(dff, a_up, w_upT, w_dn, tok_a)
    gw_upT = _mm(da_up, u2, ta=True, tm=512, tn=1024, tk=T, out_dtypes=(BF16,), name="gw_mlp_up")

    def bwd_mix_post(d2, du, hh, rr, mm, ga, gb):
        n3 = hh * rr
        dh1 = d2 + _rms_bwd(n3, rr, gb, du)
        r = _rstd(mm)
        n2 = mm * r
        return (dh1, _rms_bwd(n2, r, ga, dh1)), (_colsum(du * n3), _colsum(dh1 * n2))
    dh1, dmix, dg3, dg2 = _rowwise(bwd_mix_post, [dh2, du2, h1, r3, mix], [g2, g3], [(D, F32), (D, BF16)],
                                   [(1, D), (1, D)], tm=512, name="bwd_post_mix")

    gw_o = _mm(cat, dmix, ta=True, tm=512, tn=1024, tk=T, out_dtypes=(BF16,), name="gw_out")
    rs_b, tok_b = _send_start_many([gw_upT.reshape(N_DEV, 512, D), gw_o.reshape(N_DEV, 128, D)], True, "rs_start_b", g1)
    dcat, dattn4, dattn16 = _dx_out(dmix, w_o, tok_b)

    dact, ddtw, ssd_par, dz, dgs = _ssd_bwd(act, xbcdt, bias_w, alog_w, dsk_w, states, y_ssd, qkvz, dcat, ssd_norm_g)
    dxbcdt, conv_par = _conv_bwd(xbcdt, dact, ddtw, conv_full, conv_b)

    qkv_grads = [_attn_bwd(*qkv[0], dcat, attn, lse, 1),
                 _attn_bwd(*qkv[1], dattn4, attn4, lse4, 4),
                 _attn_bwd(*qkv[2], dattn16, attn16, lse16, 16)]
    dqkvz = _rope_bwd(qkv_grads, dz, cos128, sin128)

    gw_qkvzT = _mm(dqkvz, u1, ta=True, tm=512, tn=1024, tk=T, out_dtypes=(BF16,), name="gw_qkvz")
    gw_xbcdtT = _mm(dxbcdt, u1, ta=True, tm=896, tn=1024, tk=T, out_dtypes=(BF16,), name="gw_xbcdt")
    gw_inT = jnp.concatenate([gw_qkvzT, gw_xbcdtT], axis=0)[:IN_W]
    gw_inT = jnp.pad(gw_inT.reshape(N_DEV, W_IN_SHARD, D), ((0, 0), (0, W_IN_SHARD_PAD - W_IN_SHARD), (0, 0)))
    rs_in, tok_in = _send_start(gw_inT, True, "rs_start_w_in", g1)

    def bwd_in(ua, ub, d1, xx, g):
        rr = _rstd(xx)
        n = xx * rr
        du = ua + ub
        return (d1 + _rms_bwd(n, rr, g, du),), (_colsum(du * n),)
    grad_x, dg1 = _mm_rows(bwd_in, [(dqkvz, w_qkvzT, False), (dxbcdt, w_xbcdtT, False)], [dh1, x2], [g1],
                           [(D, F32)], [(1, D)], tm=512, name="bwd_in_proj", deps=[tok_in])

    my_slab = _slab_pack([(dg1, 0), (dg2, 1), (dg3, 2), (dg4, 3), (dg5, 4), (dgs, 5), (loss_vec, 6),
                          (conv_par, 8), (ssd_par, 16)], "slab_pack")
    slab_handles, tok_slab = _send_start_many([my_slab], False, "slab_start", g1)

    def scatter_finish(handles, nm, after):
        part, land = _send_wait(handles, after, "rs_wait_" + nm)
        own = lax.dynamic_slice(part, (me, 0, 0), (1,) + part.shape[1:])
        return _sum_slots(lax.dynamic_update_slice(land, own, (me, 0, 0)), "rs_sum_" + nm)
    def landed(handles, after, wait_name):
        parts, lands = _send_wait_many(handles, after, wait_name)
        return [lax.dynamic_update_slice(land, lax.dynamic_slice(part, (me, 0, 0), (1,) + part.shape[1:]), (me, 0, 0))
                for part, land in zip(parts, lands)]
    land_proj, land_gate, land_dn = landed(rs_a, tok_slab, "rs_wait_a")
    land_up, land_out = landed(rs_b, tok_slab, "rs_wait_b")

    grads, delta, new_m, new_v = {}, {}, {}, {}
    for nme, land in (("w_down", land_dn), ("w_out", land_out), ("w_ple_gate", land_gate)):
        outs4 = _adamw_slots(land, args[nme][0], args["m_" + nme][0], args["v_" + nme][0], "adamw_" + nme)
        grads[nme], delta[nme], new_m[nme], new_v[nme] = [t[None] for t in outs4]
    grads["w_up"] = _sum_slots(land_up, "rs_sum_w_up").T[None]
    grads["w_ple_proj"] = _sum_slots(land_proj, "rs_sum_w_proj").reshape(128, PLE).T[None]
    for nme in ["w_up", "w_ple_proj", "w_in"]:
        if nme == "w_in":
            g_inT = scatter_finish(rs_in, "w_in", delta["w_down"])
            grads["w_in"] = g_inT[:W_IN_SHARD].T[None]
        dl, mm_, vv_ = _adamw(args[nme][0], grads[nme][0], args["m_" + nme][0], args["v_" + nme][0], "adamw_" + nme)
        delta[nme], new_m[nme], new_v[nme] = dl[None], mm_[None], vv_[None]

    slab_back, slab_land = _send_wait_many(slab_handles, delta["w_in"], "slab_wait")
    slab = _sum_slots(lax.dynamic_update_slice(slab_land[0], slab_back[0][None], (me, 0, 0)), "slab_sum")
    g_conv_w = lax.dynamic_slice(slab[8:12, :CONV_CH], (0, me * 96), (CONV_K, 96))
    small_names = SMALL + ["conv_w"]
    small_rows = [0, 1, 2, 3, 4, 12, 5, 16, 17, 18, None]
    pick = lambda prefix: [args[prefix + nme] for nme in SMALL] + [args[prefix + "conv_w"][0]]
    loss11, g_s, d_s, m_s, v_s = _adamw_small(slab, small_rows, g_conv_w, pick(""), pick("m_"), pick("v_"))
    loss = loss11[0, 0]
    for i, nme in enumerate(small_names):
        lead = (lambda t: t[None]) if nme == "conv_w" else (lambda t: t)
        grads[nme], delta[nme], new_m[nme], new_v[nme] = lead(g_s[i]), lead(d_s[i]), lead(m_s[i]), lead(v_s[i])

    order = ["norm_mix_pre", "norm_mix_post", "w_in", "conv_w", "conv_b", "dt_bias", "a_log", "d_skip", "ssd_norm_g",
             "w_out", "norm_mlp_pre", "norm_mlp_post", "w_up", "w_down", "w_ple_gate", "w_ple_proj", "norm_ple_post"]
    return (loss, grad_x[None], *[grads[n] for n in order], *[delta[n] for n in order],
            *[new_m[n] for n in order], *[new_v[n] for n in order])
```
